```python
import jax, jax.numpy as jnp
from jax import lax
import numpy as np

D_MODEL = 1024
BATCH = 8
SEQ = 8192
DEPTH = 2

BLOCK = 128
NORM_EPS = 1e-6
SWA_WINDOW = 128
SWA_HEADS = 4
SWA_KV_HEADS = 2
SWA_HEAD_DIM = 64
CONV_WIDTH = 256
CONV_K = 3
MLA_HEADS = 4
MLA_Q_RANK = 256
MLA_KV_RANK = 128
MLA_NOPE_DIM = 64
MLA_ROPE_DIM = 32
MLA_V_DIM = 64
ROPE_THETA = 10000.0
SB_HEADS = 4
SB_HEAD_DIM = 64
GROUP_WIDTH = 256
N_GROUPS = 4
D_MIX = GROUP_WIDTH * N_GROUPS

A_Q = SWA_HEADS * SWA_HEAD_DIM
A_KV = SWA_KV_HEADS * SWA_HEAD_DIM
SB_W = SB_HEADS * SB_HEAD_DIM
IN_SIZES = (A_Q, A_KV, A_KV,
            CONV_WIDTH, CONV_WIDTH, CONV_WIDTH,
            MLA_Q_RANK, MLA_KV_RANK, MLA_ROPE_DIM,
            SB_W, SB_W, SB_W,
            D_MIX)
D_IN = int(sum(IN_SIZES))
SPLIT_IDX = [int(v) for v in np.cumsum(IN_SIZES)[:-1]]

kernel_name = "hymba_style_four_group_hybrid"


def rmsnorm(x, g):
    x32 = x.astype(jnp.float32)
    y = x32 * lax.rsqrt(jnp.mean(x32 * x32, axis=-1, keepdims=True) + NORM_EPS)
    return (y * g.astype(jnp.float32)).astype(x.dtype)


def to_blocks(t):
    b, s = t.shape[:2]
    t = t.reshape((b, s // BLOCK, BLOCK) + t.shape[2:])
    return jnp.moveaxis(t, 1, 0)


def from_blocks(t):
    t = jnp.moveaxis(t, 0, 1)
    return t.reshape((t.shape[0], t.shape[1] * t.shape[2]) + t.shape[3:])


def rope(x, pos):
    half = x.shape[-1] // 2
    freqs = ROPE_THETA ** (-jnp.arange(half, dtype=jnp.float32) / half)
    ang = pos.astype(jnp.float32)[..., None] * freqs
    ang = ang.reshape(ang.shape[:2] + (1,) * (x.ndim - 3) + (half,))
    cos, sin = jnp.cos(ang), jnp.sin(ang)
    x32 = x.astype(jnp.float32)
    x1, x2 = x32[..., :half], x32[..., half:]
    return jnp.concatenate([x1 * cos - x2 * sin, x1 * sin + x2 * cos], axis=-1).astype(x.dtype)


def swa_sink_attention(q, k, v, sinks):
    b, s, h, d = q.shape
    kvh = k.shape[2]
    g = h // kvh
    nb = s // BLOCK
    qb = q.reshape(b, nb, BLOCK, kvh, g, d).astype(jnp.float32)
    kb = k.reshape(b, nb, BLOCK, kvh, d).astype(jnp.float32)
    vb = v.reshape(b, nb, BLOCK, kvh, d).astype(jnp.float32)
    prev = lambda t: jnp.concatenate([jnp.zeros_like(t[:, :1]), t[:, :-1]], axis=1)
    kk = jnp.concatenate([prev(kb), kb], axis=2)
    vv = jnp.concatenate([prev(vb), vb], axis=2)
    scores = jnp.einsum('bnqhgd,bnkhd->bnhgqk', qb, kk) * (d ** -0.5)
    qi = jnp.arange(BLOCK)[:, None]
    kj = jnp.arange(2 * BLOCK)[None, :]
    diff = qi + BLOCK - kj
    blk = jnp.arange(nb)[:, None, None]
    valid = (diff >= 0) & (diff < SWA_WINDOW) & (blk * BLOCK + kj - BLOCK >= 0)
    scores = jnp.where(valid[None, :, None, None], scores, -jnp.inf)
    sink = jnp.broadcast_to(sinks.astype(jnp.float32).reshape(1, 1, kvh, g, 1, 1), scores.shape[:-1] + (1,))
    probs = jax.nn.softmax(jnp.concatenate([scores, sink], axis=-1), axis=-1)[..., :-1]
    out = jnp.einsum('bnhgqk,bnkhd->bnqhgd', probs, vv)
    return out.reshape(b, s, h * d).astype(q.dtype)


def short_gated_conv(bg, cg, xin, conv_w, conv_b):
    u = cg * xin
    y = lax.conv_general_dilated(u, conv_w[:, None, :], window_strides=(1,),
                                 padding=[(CONV_K - 1, 0)],
                                 dimension_numbers=('NWC', 'WIO', 'NWC'),
                                 feature_group_count=u.shape[-1])
    return bg * (y + conv_b)


def causal_softmax_attention(q, k, v):
    b, s, h, dk = q.shape
    scale = dk ** -0.5
    kf = k.astype(jnp.float32)
    vf = v.astype(jnp.float32)
    kpos = jnp.arange(s)

    def step(args):
        qblk, i = args
        sc = jnp.einsum('bqhd,bkhd->bhqk', qblk.astype(jnp.float32), kf) * scale
        qpos = i * BLOCK + jnp.arange(BLOCK)
        sc = jnp.where(kpos[None, :] <= qpos[:, None], sc, -jnp.inf)
        p = jax.nn.softmax(sc, axis=-1)
        return jnp.einsum('bhqk,bkhd->bqhd', p, vf)

    out = lax.map(step, (to_blocks(q), jnp.arange(s // BLOCK)))
    return from_blocks(out).reshape(b, s, -1).astype(q.dtype)


def mla(cq, ckv, kr, pos, g_q, w_uq, g_kv, w_ukv):
    b, s, _ = cq.shape
    q = (rmsnorm(cq, g_q) @ w_uq).reshape(b, s, MLA_HEADS, MLA_NOPE_DIM + MLA_ROPE_DIM)
    q = jnp.concatenate([q[..., :MLA_NOPE_DIM], rope(q[..., MLA_NOPE_DIM:], pos)], axis=-1)
    kv = (rmsnorm(ckv, g_kv) @ w_ukv).reshape(b, s, MLA_HEADS, MLA_NOPE_DIM + MLA_V_DIM)
    k_nope, v = kv[..., :MLA_NOPE_DIM], kv[..., MLA_NOPE_DIM:]
    k_rope = jnp.broadcast_to(rope(kr, pos)[:, :, None, :], (b, s, MLA_HEADS, MLA_ROPE_DIM))
    k = jnp.concatenate([k_nope, k_rope], axis=-1)
    return causal_softmax_attention(q, k, v)


def stick_breaking_attention(q, k, v):
    b, s, h, d = q.shape
    scale = d ** -0.5
    kf = k.astype(jnp.float32)
    vf = v.astype(jnp.float32)
    kpos = jnp.arange(s)

    def step(args):
        qblk, i = args
        z = jnp.einsum('bqhd,bkhd->bhqk', qblk.astype(jnp.float32), kf) * scale
        qpos = i * BLOCK + jnp.arange(BLOCK)
        mask = kpos[None, :] < qpos[:, None]
        log_keep = jnp.where(mask, jax.nn.log_sigmoid(-z), 0.0)
        after = lax.cumsum(log_keep, axis=3, reverse=True) - log_keep
        a = jnp.where(mask, jnp.exp(jax.nn.log_sigmoid(z) + after), 0.0)
        return jnp.einsum('bhqk,bkhd->bqhd', a, vf)

    out = lax.map(step, (to_blocks(q), jnp.arange(s // BLOCK)))
    return from_blocks(out).reshape(b, s, -1).astype(q.dtype)


def hybrid_layer(x, pos, g_pre, w_in, sinks, conv_w, conv_b, g_cq, w_uq, g_ckv, w_ukv, g_grp, w_out, g_post):
    b, s, _ = x.shape
    h = rmsnorm(x, g_pre) @ w_in
    (a_q, a_k, a_v, b_b, b_c, b_x, c_q, c_kv, c_kr, d_q, d_k, d_v, gate) = jnp.split(h, SPLIT_IDX, axis=-1)
    ya = swa_sink_attention(a_q.reshape(b, s, SWA_HEADS, SWA_HEAD_DIM),
                            a_k.reshape(b, s, SWA_KV_HEADS, SWA_HEAD_DIM),
                            a_v.reshape(b, s, SWA_KV_HEADS, SWA_HEAD_DIM), sinks)
    yb = short_gated_conv(b_b, b_c, b_x, conv_w, conv_b)
    yc = mla(c_q, c_kv, c_kr, pos, g_cq, w_uq, g_ckv, w_ukv)
    yd = stick_breaking_attention(d_q.reshape(b, s, SB_HEADS, SB_HEAD_DIM),
                                  d_k.reshape(b, s, SB_HEADS, SB_HEAD_DIM),
                                  d_v.reshape(b, s, SB_HEADS, SB_HEAD_DIM))
    y = jnp.stack([ya, yb, yc, yd], axis=2)
    y = rmsnorm(y, g_grp.reshape(N_GROUPS, GROUP_WIDTH)).reshape(b, s, D_MIX)
    y = y * jax.nn.silu(gate)
    return x + rmsnorm(y @ w_out, g_post)


def _fwd_setup_inputs(seed: int = 0) -> dict:
    key = jax.random.key(seed)
    ks = jax.random.split(key, 16)
    f32 = jnp.float32
    nrm = lambda k, shape, scale: jax.random.normal(k, shape, f32) * scale
    gain = lambda k, shape: 1.0 + 0.02 * jax.random.normal(k, shape, f32)
    x = jax.random.normal(ks[0], (BATCH, SEQ, D_MODEL), f32)
    positions = jnp.broadcast_to(jnp.arange(SEQ, dtype=jnp.int32)[None, :], (BATCH, SEQ))
    return {
        "x": x,
        "positions": positions,
        "norm_pre": gain(ks[1], (DEPTH, D_MODEL)),
        "w_in": nrm(ks[2], (DEPTH, D_MODEL, D_IN), D_MODEL ** -0.5),
        "attn_sinks": nrm(ks[3], (DEPTH, SWA_HEADS), 0.5),
        "conv_w": nrm(ks[4], (DEPTH, CONV_K, CONV_WIDTH), CONV_K ** -0.5),
        "conv_b": nrm(ks[5], (DEPTH, CONV_WIDTH), 0.01),
        "mla_q_norm": gain(ks[6], (DEPTH, MLA_Q_RANK)),
        "mla_w_uq": nrm(ks[7], (DEPTH, MLA_Q_RANK, MLA_HEADS * (MLA_NOPE_DIM + MLA_ROPE_DIM)), MLA_Q_RANK ** -0.5),
        "mla_kv_norm": gain(ks[8], (DEPTH, MLA_KV_RANK)),
        "mla_w_ukv": nrm(ks[9], (DEPTH, MLA_KV_RANK, MLA_HEADS * (MLA_NOPE_DIM + MLA_V_DIM)), MLA_KV_RANK ** -0.5),
        "group_norm": gain(ks[10], (DEPTH, D_MIX)),
        "w_out": nrm(ks[11], (DEPTH, D_MIX, D_MODEL), D_MIX ** -0.5),
        "norm_post": gain(ks[12], (DEPTH, D_MODEL)),
    }


def _fwd_reference(x, positions, norm_pre, w_in, attn_sinks, conv_w, conv_b, mla_q_norm, mla_w_uq,
              mla_kv_norm, mla_w_ukv, group_norm, w_out, norm_post):
    for l in range(DEPTH):
        x = hybrid_layer(x, positions, norm_pre[l], w_in[l], attn_sinks[l], conv_w[l], conv_b[l],
                         mla_q_norm[l], mla_w_uq[l], mla_kv_norm[l], mla_w_ukv[l],
                         group_norm[l], w_out[l], norm_post[l])
    return x


import jax as _jax
import jax.numpy as _jnp

TWIN_FORMAT = 'train_step'
FWD_PARAMS = ['x', 'positions', 'norm_pre', 'w_in', 'attn_sinks', 'conv_w', 'conv_b', 'mla_q_norm', 'mla_w_uq', 'mla_kv_norm', 'mla_w_ukv', 'group_norm', 'w_out', 'norm_post']
TWIN_WEIGHTS = ['norm_pre', 'w_in', 'attn_sinks', 'conv_w', 'conv_b', 'mla_q_norm', 'mla_w_uq', 'mla_kv_norm', 'mla_w_ukv', 'group_norm', 'w_out', 'norm_post']
TWIN_DIFF_INPUT = 'x'
TWIN_INPUTS = ['x', 'positions', 'norm_pre', 'w_in', 'attn_sinks', 'conv_w', 'conv_b', 'mla_q_norm', 'mla_w_uq', 'mla_kv_norm', 'mla_w_ukv', 'group_norm', 'w_out', 'norm_post', 'loss_target', 'm_norm_pre', 'm_w_in', 'm_attn_sinks', 'm_conv_w', 'm_conv_b', 'm_mla_q_norm', 'm_mla_w_uq', 'm_mla_kv_norm', 'm_mla_w_ukv', 'm_group_norm', 'm_w_out', 'm_norm_post', 'v_norm_pre', 'v_w_in', 'v_attn_sinks', 'v_conv_w', 'v_conv_b', 'v_mla_q_norm', 'v_mla_w_uq', 'v_mla_kv_norm', 'v_mla_w_ukv', 'v_group_norm', 'v_w_out', 'v_norm_post']
TWIN_OUTPUTS = ['loss', 'grad_x', 'grad_norm_pre', 'grad_w_in', 'grad_attn_sinks', 'grad_conv_w', 'grad_conv_b', 'grad_mla_q_norm', 'grad_mla_w_uq', 'grad_mla_kv_norm', 'grad_mla_w_ukv', 'grad_group_norm', 'grad_w_out', 'grad_norm_post', 'delta_norm_pre', 'delta_w_in', 'delta_attn_sinks', 'delta_conv_w', 'delta_conv_b', 'delta_mla_q_norm', 'delta_mla_w_uq', 'delta_mla_kv_norm', 'delta_mla_w_ukv', 'delta_group_norm', 'delta_w_out', 'delta_norm_post', 'new_m_norm_pre', 'new_m_w_in', 'new_m_attn_sinks', 'new_m_conv_w', 'new_m_conv_b', 'new_m_mla_q_norm', 'new_m_mla_w_uq', 'new_m_mla_kv_norm', 'new_m_mla_w_ukv', 'new_m_group_norm', 'new_m_w_out', 'new_m_norm_post', 'new_v_norm_pre', 'new_v_w_in', 'new_v_attn_sinks', 'new_v_conv_w', 'new_v_conv_b', 'new_v_mla_q_norm', 'new_v_mla_w_uq', 'new_v_mla_kv_norm', 'new_v_mla_w_ukv', 'new_v_group_norm', 'new_v_w_out', 'new_v_norm_post']
TWIN_LEAF_KINDS = {'loss': 'loss', 'grad_x': 'grad_x', 'grad_norm_pre': 'grad_w', 'grad_w_in': 'grad_w', 'grad_attn_sinks': 'grad_w', 'grad_conv_w': 'grad_w', 'grad_conv_b': 'grad_w', 'grad_mla_q_norm': 'grad_w', 'grad_mla_w_uq': 'grad_w', 'grad_mla_kv_norm': 'grad_w', 'grad_mla_w_ukv': 'grad_w', 'grad_group_norm': 'grad_w', 'grad_w_out': 'grad_w', 'grad_norm_post': 'grad_w', 'delta_norm_pre': 'delta_w', 'delta_w_in': 'delta_w', 'delta_attn_sinks': 'delta_w', 'delta_conv_w': 'delta_w', 'delta_conv_b': 'delta_w', 'delta_mla_q_norm': 'delta_w', 'delta_mla_w_uq': 'delta_w', 'delta_mla_kv_norm': 'delta_w', 'delta_mla_w_ukv': 'delta_w', 'delta_group_norm': 'delta_w', 'delta_w_out': 'delta_w', 'delta_norm_post': 'delta_w', 'new_m_norm_pre': 'new_m', 'new_m_w_in': 'new_m', 'new_m_attn_sinks': 'new_m', 'new_m_conv_w': 'new_m', 'new_m_conv_b': 'new_m', 'new_m_mla_q_norm': 'new_m', 'new_m_mla_w_uq': 'new_m', 'new_m_mla_kv_norm': 'new_m', 'new_m_mla_w_ukv': 'new_m', 'new_m_group_norm': 'new_m', 'new_m_w_out': 'new_m', 'new_m_norm_post': 'new_m', 'new_v_norm_pre': 'new_v', 'new_v_w_in': 'new_v', 'new_v_attn_sinks': 'new_v', 'new_v_conv_w': 'new_v', 'new_v_conv_b': 'new_v', 'new_v_mla_q_norm': 'new_v', 'new_v_mla_w_uq': 'new_v', 'new_v_mla_kv_norm': 'new_v', 'new_v_mla_w_ukv': 'new_v', 'new_v_group_norm': 'new_v', 'new_v_w_out': 'new_v', 'new_v_norm_post': 'new_v'}


def _forward(args):
    return _fwd_reference(*[args[k] for k in FWD_PARAMS])


def _output_shape():
    def fwd():
        inp = _fwd_setup_inputs(0)
        return _fwd_reference(*[inp[k] for k in FWD_PARAMS])
    out = _jax.eval_shape(fwd)
    return out.shape, out.dtype

N_MICROBATCH = 1
ADAM_LR = 0.001
ADAM_B1 = 0.9
ADAM_B2 = 0.999
ADAM_EPS = 1e-08
ADAM_WD = 0.01
ADAM_STEP = 10
PER_EXAMPLE_BATCH_AXIS = {'x': 0, 'positions': 0, 'loss_target': 0}
SHARED_INPUTS = []
_WEIGHT_DTYPES = {'norm_pre': _jnp.float32, 'w_in': _jnp.float32, 'attn_sinks': _jnp.float32, 'conv_w': _jnp.float32, 'conv_b': _jnp.float32, 'mla_q_norm': _jnp.float32, 'mla_w_uq': _jnp.float32, 'mla_kv_norm': _jnp.float32, 'mla_w_ukv': _jnp.float32, 'group_norm': _jnp.float32, 'w_out': _jnp.float32, 'norm_post': _jnp.float32}
MOMENT_SCALE = {'norm_pre': 1.144623e+00, 'w_in': 5.979124e-01, 'attn_sinks': 1.472326e-01, 'conv_w': 5.764385e-01, 'conv_b': 8.048789e-01, 'mla_q_norm': 6.873911e-01, 'mla_w_uq': 5.497678e-01, 'mla_kv_norm': 1.943678e+00, 'mla_w_ukv': 7.222383e-01, 'group_norm': 6.575381e-01, 'w_out': 6.598440e-01, 'norm_post': 6.408082e+01}


def _to_microbatches(a, axis):
    t = _jnp.moveaxis(a, axis, 0)
    t = t.reshape((N_MICROBATCH, t.shape[0] // N_MICROBATCH) + t.shape[1:])
    return _jnp.moveaxis(t, 1, axis + 1)


def setup_inputs(seed: int = 0) -> dict:
    inp = _fwd_setup_inputs(seed)
    key = _jax.random.fold_in(_jax.random.key(seed), 7919)
    shape, _ = _output_shape()
    out = dict(inp)
    out["loss_target"] = _jax.random.normal(_jax.random.fold_in(key, 0), shape, _jnp.float32)
    for i, name in enumerate(TWIN_WEIGHTS):
        w = inp[name].astype(_jnp.float32)
        if MOMENT_SCALE is None:
            s = _jnp.sqrt(_jnp.mean(_jnp.square(w)) + 1e-30)
        else:
            s = MOMENT_SCALE[name]
        km, kv = _jax.random.split(_jax.random.fold_in(key, i + 1))
        out[name] = w
        out["m_" + name] = s * _jax.random.normal(km, w.shape, _jnp.float32)
        out["v_" + name] = (s * s) * _jax.random.uniform(kv, w.shape, _jnp.float32, 0.5, 1.5)
    if N_MICROBATCH > 1:
        for name, axis in PER_EXAMPLE_BATCH_AXIS.items():
            out[name] = _to_microbatches(out[name], axis)
    return {'x': out['x'], 'positions': out['positions'], 'norm_pre': out['norm_pre'], 'w_in': out['w_in'], 'attn_sinks': out['attn_sinks'], 'conv_w': out['conv_w'], 'conv_b': out['conv_b'], 'mla_q_norm': out['mla_q_norm'], 'mla_w_uq': out['mla_w_uq'], 'mla_kv_norm': out['mla_kv_norm'], 'mla_w_ukv': out['mla_w_ukv'], 'group_norm': out['group_norm'], 'w_out': out['w_out'], 'norm_post': out['norm_post'], 'loss_target': out['loss_target'], 'm_norm_pre': out['m_norm_pre'], 'm_w_in': out['m_w_in'], 'm_attn_sinks': out['m_attn_sinks'], 'm_conv_w': out['m_conv_w'], 'm_conv_b': out['m_conv_b'], 'm_mla_q_norm': out['m_mla_q_norm'], 'm_mla_w_uq': out['m_mla_w_uq'], 'm_mla_kv_norm': out['m_mla_kv_norm'], 'm_mla_w_ukv': out['m_mla_w_ukv'], 'm_group_norm': out['m_group_norm'], 'm_w_out': out['m_w_out'], 'm_norm_post': out['m_norm_post'], 'v_norm_pre': out['v_norm_pre'], 'v_w_in': out['v_w_in'], 'v_attn_sinks': out['v_attn_sinks'], 'v_conv_w': out['v_conv_w'], 'v_conv_b': out['v_conv_b'], 'v_mla_q_norm': out['v_mla_q_norm'], 'v_mla_w_uq': out['v_mla_w_uq'], 'v_mla_kv_norm': out['v_mla_kv_norm'], 'v_mla_w_ukv': out['v_mla_w_ukv'], 'v_group_norm': out['v_group_norm'], 'v_w_out': out['v_w_out'], 'v_norm_post': out['v_norm_post']}


def _loss(weights, diff, rest, loss_target):
    with _jax.named_scope("forward"):
        args = {**rest, TWIN_DIFF_INPUT: diff, **{k: w.astype(_WEIGHT_DTYPES[k]) for k, w in weights.items()}}
        y = _forward(args)
    with _jax.named_scope("loss_head"):
        err = _jnp.square(y.astype(_jnp.float32) - loss_target)
        return 0.5 * _jnp.sum(_jnp.mean(err, axis=-1)) if err.ndim else 0.5 * err


def _adamw(w, g, m, v):
    m = ADAM_B1 * m + (1.0 - ADAM_B1) * g
    v = ADAM_B2 * v + (1.0 - ADAM_B2) * _jnp.square(g)
    m_hat = m / (1.0 - ADAM_B1 ** ADAM_STEP)
    v_hat = v / (1.0 - ADAM_B2 ** ADAM_STEP)
    delta = -ADAM_LR * (m_hat / (_jnp.sqrt(v_hat) + ADAM_EPS) + ADAM_WD * w)
    return delta, m, v


def reference(x, positions, norm_pre, w_in, attn_sinks, conv_w, conv_b, mla_q_norm, mla_w_uq, mla_kv_norm, mla_w_ukv, group_norm, w_out, norm_post, loss_target, m_norm_pre, m_w_in, m_attn_sinks, m_conv_w, m_conv_b, m_mla_q_norm, m_mla_w_uq, m_mla_kv_norm, m_mla_w_ukv, m_group_norm, m_w_out, m_norm_post, v_norm_pre, v_w_in, v_attn_sinks, v_conv_w, v_conv_b, v_mla_q_norm, v_mla_w_uq, v_mla_kv_norm, v_mla_w_ukv, v_group_norm, v_w_out, v_norm_post):
    given = dict(x=x, positions=positions, norm_pre=norm_pre, w_in=w_in, attn_sinks=attn_sinks, conv_w=conv_w, conv_b=conv_b, mla_q_norm=mla_q_norm, mla_w_uq=mla_w_uq, mla_kv_norm=mla_kv_norm, mla_w_ukv=mla_w_ukv, group_norm=group_norm, w_out=w_out, norm_post=norm_post, loss_target=loss_target, m_norm_pre=m_norm_pre, m_w_in=m_w_in, m_attn_sinks=m_attn_sinks, m_conv_w=m_conv_w, m_conv_b=m_conv_b, m_mla_q_norm=m_mla_q_norm, m_mla_w_uq=m_mla_w_uq, m_mla_kv_norm=m_mla_kv_norm, m_mla_w_ukv=m_mla_w_ukv, m_group_norm=m_group_norm, m_w_out=m_w_out, m_norm_post=m_norm_post, v_norm_pre=v_norm_pre, v_w_in=v_w_in, v_attn_sinks=v_attn_sinks, v_conv_w=v_conv_w, v_conv_b=v_conv_b, v_mla_q_norm=v_mla_q_norm, v_mla_w_uq=v_mla_w_uq, v_mla_kv_norm=v_mla_kv_norm, v_mla_w_ukv=v_mla_w_ukv, v_group_norm=v_group_norm, v_w_out=v_w_out, v_norm_post=v_norm_post)
    weights = {n: given[n] for n in TWIN_WEIGHTS}
    shared = {n: given[n] for n in SHARED_INPUTS}
    per_example = {n: given[n] for n in ['x', 'positions']}
    grad_fn = _jax.value_and_grad(_loss, argnums=(0, 1))

    def one_microbatch(ex, loss_target):
        ex = dict(ex)
        diff = ex.pop(TWIN_DIFF_INPUT)
        return grad_fn(weights, diff, {**shared, **ex}, loss_target)

    if N_MICROBATCH == 1:
        loss, (grad_w, grad_x) = one_microbatch(per_example, given["loss_target"])
    else:
        def body(carry, xs):
            loss_sum, grad_sum = carry
            l_k, (gw_k, gx_k) = one_microbatch(xs[0], xs[1])
            with _jax.named_scope("update"):
                return (loss_sum + l_k, _jax.tree.map(_jnp.add, grad_sum, gw_k)), gx_k

        init = (_jnp.zeros((), _jnp.float32), _jax.tree.map(_jnp.zeros_like, weights))
        (loss, grad_w), grad_x = _jax.lax.scan(body, init, (per_example, given["loss_target"]))
    with _jax.named_scope("update"):
        delta_w, new_m, new_v = {}, {}, {}
        for n in TWIN_WEIGHTS:
            delta_w[n], new_m[n], new_v[n] = _adamw(weights[n], grad_w[n], given["m_" + n], given["v_" + n])
    return (loss, grad_x, *[grad_w[n] for n in TWIN_WEIGHTS], *[delta_w[n] for n in TWIN_WEIGHTS],
            *[new_m[n] for n in TWIN_WEIGHTS], *[new_v[n] for n in TWIN_WEIGHTS])
```

```python
import functools
import math

import jax
import jax.numpy as jnp
from jax import lax
from jax.experimental import pallas as pl
from jax.experimental.pallas import tpu as pltpu

F32 = jnp.float32
BF16 = jnp.bfloat16
MESH = pl.DeviceIdType.MESH

D_MODEL = 1024
DEPTH = 2
EPS = 1e-6
BLOCK = 128
HEAD = 64
LANES = 128
GROUP = 256
MLA_SCALE = 96 ** -0.5
ROPE_HALF = 16
ROPE_THETA = 10000.0
ATT_BLK = 256
NEG = -1e30

ADAM_LR, ADAM_B1, ADAM_B2, ADAM_EPS, ADAM_WD, ADAM_STEP = 0.001, 0.9, 0.999, 1e-08, 0.01, 10

_REAL = [("a_q", 256), ("a_k", 128), ("a_v", 128), ("b_b", 256), ("b_c", 256), ("b_x", 256),
         ("c_q", 256), ("c_kv", 128), ("c_kr", 32), ("d_q", 256), ("d_k", 256), ("d_v", 256),
         ("gate", 1024)]
_REAL_OFF = {}
_o = 0
for _n, _w in _REAL:
    _REAL_OFF[_n] = (_o, _w)
    _o += _w
D_IN = _o
_INT_ORDER = ["a_q", "a_k", "a_v", "d_q", "d_k", "d_v", "gate", "b_b", "b_c", "b_x", "c_q", "c_kv", "c_kr"]
_INT_W = dict(_REAL)
_INT_W["c_kr"] = 128
_INT_OFF = {}
_o = 0
for _n in _INT_ORDER:
    _INT_OFF[_n] = _o
    _o += _INT_W[_n]
N_INT = _o
N_HB = _INT_OFF["gate"]
N_HF = N_INT - N_HB

VMEM_LIMIT = 56 * 1024 * 1024


def _cparams(sem):
    return pltpu.CompilerParams(dimension_semantics=sem, vmem_limit_bytes=VMEM_LIMIT)


def _dot(a, b):
    return jnp.dot(a, b, preferred_element_type=F32)


def _dot_nt(a, b):
    return lax.dot_general(a, b, (((1,), (1,)), ((), ())), preferred_element_type=F32)


def _dot_tn(a, b):
    return lax.dot_general(a, b, (((0,), (0,)), ((), ())), preferred_element_type=F32)


def _split(x):
    hi = x.astype(BF16)
    lo = (x - hi.astype(F32)).astype(BF16)
    return hi, lo


def _rms(x):
    return lax.rsqrt(jnp.mean(x * x, axis=-1, keepdims=True) + EPS)


def _rms_bwd(dy, xhat, r, g):
    dxhat = dy * g
    return r * (dxhat - xhat * jnp.mean(dxhat * xhat, axis=-1, keepdims=True)), dy * xhat


def _colsum(x):
    return jnp.sum(x, axis=0, keepdims=True)


def _inproj_fwd(x, g, w):
    T = x.shape[0]
    tm = 256

    def body(x_ref, g_ref, w_ref, xn_ref, hb_ref, hf_ref):
        xv = x_ref[...]
        xn = (xv * _rms(xv) * g_ref[...]).astype(BF16)
        xn_ref[...] = xn
        h = _dot(xn, w_ref[...])
        hb_ref[...] = h[:, :N_HB].astype(BF16)
        hf_ref[...] = h[:, N_HB:]

    return pl.pallas_call(
        body, name="inproj_fwd", grid=(T // tm,),
        in_specs=[pl.BlockSpec((tm, D_MODEL), lambda i: (i, 0)),
                  pl.BlockSpec((1, D_MODEL), lambda i: (0, 0)),
                  pl.BlockSpec((D_MODEL, N_INT), lambda i: (0, 0))],
        out_specs=[pl.BlockSpec((tm, D_MODEL), lambda i: (i, 0)),
                   pl.BlockSpec((tm, N_HB), lambda i: (i, 0)),
                   pl.BlockSpec((tm, N_HF), lambda i: (i, 0))],
        out_shape=[jax.ShapeDtypeStruct((T, D_MODEL), BF16),
                   jax.ShapeDtypeStruct((T, N_HB), BF16),
                   jax.ShapeDtypeStruct((T, N_HF), F32)],
        compiler_params=_cparams(("parallel",)),
    )(x, g, w)


def _inproj_bwd_dx(x, g, w, dx_next, pieces):
    T = x.shape[0]
    tm = 256
    widths = [p.shape[1] for p in pieces]
    assert sum(widths) == N_INT

    def body(x_ref, g_ref, w_ref, dxn_ref, *rest):
        p_refs = rest[:len(pieces)]
        dx_ref, dh_ref, dg_ref = rest[len(pieces):]
        dh = jnp.concatenate([p[...].astype(BF16) for p in p_refs], axis=1)
        dh_ref[...] = dh
        dxn = _dot_nt(dh, w_ref[...])
        xv = x_ref[...]
        r = _rms(xv)
        dx, dgrow = _rms_bwd(dxn, xv * r, r, g_ref[...])
        dx_ref[...] = dx + dxn_ref[...]

        @pl.when(pl.program_id(0) == 0)
        def _():
            dg_ref[...] = jnp.zeros_like(dg_ref)

        dg_ref[...] += _colsum(dgrow)

    return pl.pallas_call(
        body, name="inproj_bwd_dx", grid=(T // tm,),
        in_specs=[pl.BlockSpec((tm, D_MODEL), lambda i: (i, 0)),
                  pl.BlockSpec((1, D_MODEL), lambda i: (0, 0)),
                  pl.BlockSpec((D_MODEL, N_INT), lambda i: (0, 0)),
                  pl.BlockSpec((tm, D_MODEL), lambda i: (i, 0))]
                 + [pl.BlockSpec((tm, wd), lambda i: (i, 0)) for wd in widths],
        out_specs=[pl.BlockSpec((tm, D_MODEL), lambda i: (i, 0)),
                   pl.BlockSpec((tm, N_INT), lambda i: (i, 0)),
                   pl.BlockSpec((1, D_MODEL), lambda i: (0, 0))],
        out_shape=[jax.ShapeDtypeStruct((T, D_MODEL), F32),
                   jax.ShapeDtypeStruct((T, N_INT), BF16),
                   jax.ShapeDtypeStruct((1, D_MODEL), F32)],
        compiler_params=_cparams(("arbitrary",)),
    )(x, g, w, dx_next, *pieces)


def _matmul_tn(a, b, name):
    T, M = a.shape
    N = b.shape[1]
    tm, tn = 512, 512

    def body(a_ref, b_ref, o_ref):
        @pl.when(pl.program_id(1) == 0)
        def _():
            o_ref[...] = jnp.zeros_like(o_ref)

        o_ref[...] += _dot_tn(a_ref[...], b_ref[...])

    return pl.pallas_call(
        body, name=name, grid=(N // tn, T // tm),
        in_specs=[pl.BlockSpec((tm, M), lambda j, t: (t, 0)),
                  pl.BlockSpec((tm, tn), lambda j, t: (t, j))],
        out_specs=pl.BlockSpec((M, tn), lambda j, t: (0, j)),
        out_shape=jax.ShapeDtypeStruct((M, N), F32),
        compiler_params=_cparams(("parallel", "arbitrary")),
    )(a, b)


def _roll_f32(x, shift):
    return pltpu.roll(x.astype(F32), shift, 1)


def _swa_head(h, q_ref, kp_ref, kc_ref, vp_ref, vc_ref, sink, first):
    p, e = h // 2, h % 2
    lane = lax.broadcasted_iota(jnp.int32, (1, LANES), 1) // HEAD
    q = q_ref[:, p * LANES:(p + 1) * LANES]
    k_prev, k_cur, v_prev, v_cur = kp_ref[...], kc_ref[...], vp_ref[...], vc_ref[...]
    if e != p:
        q = _roll_f32(q, HEAD).astype(BF16)
        v_prev = _roll_f32(v_prev, HEAD).astype(BF16)
        v_cur = _roll_f32(v_cur, HEAD).astype(BF16)
    qs = jnp.where(lane == p, q, 0) * 0.125
    v_prev = jnp.where(lane == e, v_prev, 0)
    v_cur = jnp.where(lane == e, v_cur, 0)
    row = lax.broadcasted_iota(jnp.int32, (BLOCK, BLOCK), 0)
    col = lax.broadcasted_iota(jnp.int32, (BLOCK, BLOCK), 1)
    ok_prev = jnp.logical_and(col > row, jnp.logical_not(first))
    ok_cur = col <= row
    s_prev = jnp.where(ok_prev, _dot_nt(qs, k_prev), NEG)
    s_cur = jnp.where(ok_cur, _dot_nt(qs, k_cur), NEG)
    m = jnp.maximum(jnp.maximum(jnp.max(s_prev, axis=1, keepdims=True),
                                jnp.max(s_cur, axis=1, keepdims=True)), sink)
    p_prev = jnp.exp(s_prev - m)
    p_cur = jnp.exp(s_cur - m)
    p_sink = jnp.exp(sink - m)
    inv = 1.0 / (jnp.sum(p_prev, axis=1, keepdims=True) + jnp.sum(p_cur, axis=1, keepdims=True) + p_sink)
    return (p, e, lane, qs, k_prev, k_cur, v_prev, v_cur, p_prev * inv, p_cur * inv, p_sink * inv)


def _swa_specs(T):
    nb = T // BLOCK
    qo, ko, vo = (_INT_OFF[n] // LANES for n in ("a_q", "a_k", "a_v"))
    prev = lambda i: jnp.maximum(i - 1, 0)
    return [pl.BlockSpec((BLOCK, 256), lambda i: (i, qo // 2)),
            pl.BlockSpec((BLOCK, LANES), lambda i: (prev(i), ko)),
            pl.BlockSpec((BLOCK, LANES), lambda i: (i, ko)),
            pl.BlockSpec((BLOCK, LANES), lambda i: (prev(i), vo)),
            pl.BlockSpec((BLOCK, LANES), lambda i: (i, vo)),
            pl.BlockSpec(memory_space=pltpu.SMEM)], nb


def _swa_fwd(hb, sinks):
    T = hb.shape[0]
    specs, nb = _swa_specs(T)

    def body(q_ref, kp_ref, kc_ref, vp_ref, vc_ref, s_ref, o_ref):
        first = pl.program_id(0) == 0
        for p in range(2):
            out = jnp.zeros((BLOCK, LANES), F32)
            for e in range(2):
                h = 2 * p + e
                (_, _, _, _, _, _, v_prev, v_cur, p_prev, p_cur, _) = _swa_head(
                    h, q_ref, kp_ref, kc_ref, vp_ref, vc_ref, s_ref[h], first)
                out += _dot(p_prev.astype(BF16), v_prev) + _dot(p_cur.astype(BF16), v_cur)
            o_ref[:, p * LANES:(p + 1) * LANES] = out

    return pl.pallas_call(
        body, name="swa_fwd", grid=(nb,), in_specs=specs,
        out_specs=pl.BlockSpec((BLOCK, 256), lambda i: (i, 0)),
        out_shape=jax.ShapeDtypeStruct((T, 256), F32),
        compiler_params=_cparams(("parallel",)),
    )(hb, hb, hb, hb, hb, sinks)


def _swa_bwd(hb, sinks, dy):
    T = hb.shape[0]
    specs, nb = _swa_specs(T)

    def body(q_ref, kp_ref, kc_ref, vp_ref, vc_ref, s_ref, dy_ref, dq_ref, dk_ref, dv_ref, ds_ref):
        i = pl.program_id(0)
        first = i == 0
        cur = pl.ds(pl.multiple_of(i * BLOCK, BLOCK), BLOCK)
        prv = pl.ds(pl.multiple_of(jnp.maximum(i - 1, 0) * BLOCK, BLOCK), BLOCK)

        @pl.when(first)
        def _():
            ds_ref[...] = jnp.zeros_like(ds_ref)

        dk_ref[cur, :] = jnp.zeros((BLOCK, LANES), F32)
        dv_ref[cur, :] = jnp.zeros((BLOCK, LANES), F32)
        lane_id = lax.broadcasted_iota(jnp.int32, (8, LANES), 1)
        for p in range(2):
            dq_pair = jnp.zeros((BLOCK, LANES), F32)
            for e in range(2):
                h = 2 * p + e
                (_, _, lane, qs, k_prev, k_cur, v_prev, v_cur, p_prev, p_cur, p_sink) = _swa_head(
                    h, q_ref, kp_ref, kc_ref, vp_ref, vc_ref, s_ref[h], first)
                do = jnp.where(lane == e, dy_ref[:, p * LANES:(p + 1) * LANES], 0.0)
                dob = do.astype(BF16)
                pb_prev, pb_cur = p_prev.astype(BF16), p_cur.astype(BF16)
                o = _dot(pb_prev, v_prev) + _dot(pb_cur, v_cur)
                delta = jnp.sum(do * o, axis=1, keepdims=True)
                ds_prev = (p_prev * (_dot_nt(dob, v_prev) - delta)).astype(BF16)
                ds_cur = (p_cur * (_dot_nt(dob, v_cur) - delta)).astype(BF16)
                dsink = -jnp.sum(p_sink * delta, axis=0, keepdims=True)
                ds_ref[...] += jnp.where(lane_id == h, dsink, 0.0)
                dq = (_dot(ds_prev, k_prev) + _dot(ds_cur, k_cur)) * 0.125
                dq = jnp.where(lane == p, dq, 0.0)
                dob_v = dob
                if e != p:
                    dq = pltpu.roll(dq, HEAD, 1)
                    dob_v = pltpu.roll(do, HEAD, 1).astype(BF16)
                dq_pair += dq
                dk_ref[prv, :] += _dot_tn(ds_prev, qs)
                dk_ref[cur, :] += _dot_tn(ds_cur, qs)
                dv_ref[prv, :] += _dot_tn(pb_prev, dob_v)
                dv_ref[cur, :] += _dot_tn(pb_cur, dob_v)
            dq_ref[:, p * LANES:(p + 1) * LANES] = dq_pair

    return pl.pallas_call(
        body, name="swa_bwd", grid=(nb,),
        in_specs=specs + [pl.BlockSpec((BLOCK, 256), lambda i: (i, 0))],
        out_specs=[pl.BlockSpec((BLOCK, 256), lambda i: (i, 0)),
                   pl.BlockSpec((T, LANES), lambda i: (0, 0)),
                   pl.BlockSpec((T, LANES), lambda i: (0, 0)),
                   pl.BlockSpec((8, LANES), lambda i: (0, 0))],
        out_shape=[jax.ShapeDtypeStruct((T, 256), F32),
                   jax.ShapeDtypeStruct((T, LANES), F32),
                   jax.ShapeDtypeStruct((T, LANES), F32),
                   jax.ShapeDtypeStruct((8, LANES), F32)],
        compiler_params=_cparams(("arbitrary",)),
    )(hb, hb, hb, hb, hb, sinks, dy)


def _rope_tables(pos_ref):
    lane = lax.broadcasted_iota(jnp.int32, (1, LANES), 1)
    active = jnp.logical_and(lane >= HEAD, lane < HEAD + 2 * ROPE_HALF)
    idx = ((lane - HEAD) % ROPE_HALF).astype(F32)
    freq = jnp.exp(idx * (-math.log(ROPE_THETA) / ROPE_HALF))
    ang = pos_ref[...].astype(F32) * freq
    cos, sin = jnp.cos(ang), jnp.sin(ang)
    c = jnp.where(active, cos, 1.0)
    s_up = jnp.where(jnp.logical_and(active, lane >= HEAD + ROPE_HALF), sin, 0.0)
    s_dn = jnp.where(jnp.logical_and(active, lane < HEAD + ROPE_HALF), -sin, 0.0)
    return c, s_up, s_dn


def _rope(x, tabs):
    c, s_up, s_dn = tabs
    return x * c + pltpu.roll(x, ROPE_HALF, 1) * s_up + pltpu.roll(x, LANES - ROPE_HALF, 1) * s_dn


def _rope_t(dy, tabs):
    c, s_up, s_dn = tabs
    return dy * c + pltpu.roll(dy * s_up, LANES - ROPE_HALF, 1) + pltpu.roll(dy * s_dn, ROPE_HALF, 1)


def _mla_lat_specs(tm):
    cq, ckv, ckr = ((_INT_OFF[n] - N_HB) for n in ("c_q", "c_kv", "c_kr"))
    return [pl.BlockSpec((tm, 256), lambda i: (i, cq // 256)),
            pl.BlockSpec((tm, LANES), lambda i: (i, ckv // LANES)),
            pl.BlockSpec((tm, LANES), lambda i: (i, ckr // LANES)),
            pl.BlockSpec((tm, 1), lambda i: (i, 0)),
            pl.BlockSpec((1, 256), lambda i: (0, 0)),
            pl.BlockSpec((1, LANES), lambda i: (0, 0)),
            pl.BlockSpec((256, 512), lambda i: (0, 0)),
            pl.BlockSpec((LANES, 768), lambda i: (0, 0))]


def _mla_prep_fwd(hf, pos, g_q, g_kv, w_uq, w_ukv):
    T = hf.shape[0]
    tm = 512

    def body(cq_ref, ckv_ref, ckr_ref, pos_ref, gq_ref, gkv_ref, wq_ref, wkv_ref, qm_ref, km_ref, vm_ref):
        tabs = _rope_tables(pos_ref)
        cq = cq_ref[...]
        q = _dot((cq * _rms(cq) * gq_ref[...]).astype(BF16), wq_ref[...])
        ckv = ckv_ref[...]
        kv = _dot((ckv * _rms(ckv) * gkv_ref[...]).astype(BF16), wkv_ref[...])
        kr = _rope(pltpu.roll(ckr_ref[...], HEAD, 1), tabs)
        for h in range(4):
            sl = slice(h * LANES, (h + 1) * LANES)
            qm_ref[:, sl] = (_rope(q[:, sl], tabs) * MLA_SCALE).astype(BF16)
            km_ref[:, sl] = (kv[:, sl] + kr).astype(BF16)
        vm_ref[...] = kv[:, 512:].astype(BF16)

    return pl.pallas_call(
        body, name="mla_prep_fwd", grid=(T // tm,), in_specs=_mla_lat_specs(tm),
        out_specs=[pl.BlockSpec((tm, 512), lambda i: (i, 0)),
                   pl.BlockSpec((tm, 512), lambda i: (i, 0)),
                   pl.BlockSpec((tm, 256), lambda i: (i, 0))],
        out_shape=[jax.ShapeDtypeStruct((T, 512), BF16),
                   jax.ShapeDtypeStruct((T, 512), BF16),
                   jax.ShapeDtypeStruct((T, 256), BF16)],
        compiler_params=_cparams(("parallel",)),
    )(hf, hf, hf, pos, g_q, g_kv, w_uq, w_ukv)


def _mla_prep_bwd(hf, pos, g_q, g_kv, w_uq, w_ukv, dqm, dkm, dvm):
    T = hf.shape[0]
    tm = 512

    def body(cq_ref, ckv_ref, ckr_ref, pos_ref, gq_ref, gkv_ref, wq_ref, wkv_ref, dq_ref, dk_ref, dv_ref,
             dc_ref, dwq_ref, dwkv_ref, dgq_ref, dgkv_ref):
        @pl.when(pl.program_id(0) == 0)
        def _():
            dwq_ref[...] = jnp.zeros_like(dwq_ref)
            dwkv_ref[...] = jnp.zeros_like(dwkv_ref)
            dgq_ref[...] = jnp.zeros_like(dgq_ref)
            dgkv_ref[...] = jnp.zeros_like(dgkv_ref)

        tabs = _rope_tables(pos_ref)
        lane = lax.broadcasted_iota(jnp.int32, (1, LANES), 1)
        dq = jnp.concatenate([_rope_t(dq_ref[:, h * LANES:(h + 1) * LANES] * MLA_SCALE, tabs)
                              for h in range(4)], axis=1).astype(BF16)
        cq = cq_ref[...]
        rq = _rms(cq)
        cqn = (cq * rq * gq_ref[...]).astype(BF16)
        dwq_ref[...] += _dot_tn(cqn, dq)
        dcq, dgrow = _rms_bwd(_dot_nt(dq, wq_ref[...]), cq * rq, rq, gq_ref[...])
        dgq_ref[...] += _colsum(dgrow)
        dc_ref[:, 0:256] = dcq

        dk = dk_ref[...]
        dkr = dk[:, 0:LANES] + dk[:, LANES:2 * LANES] + dk[:, 2 * LANES:3 * LANES] + dk[:, 3 * LANES:]
        dkr = pltpu.roll(_rope_t(dkr, tabs), HEAD, 1)
        dc_ref[:, 384:512] = jnp.where(lane < 2 * ROPE_HALF, dkr, 0.0)
        dkv = jnp.concatenate([dk.astype(BF16), dv_ref[...].astype(BF16)], axis=1)
        ckv = ckv_ref[...]
        rkv = _rms(ckv)
        ckvn = (ckv * rkv * gkv_ref[...]).astype(BF16)
        dwkv_ref[...] += _dot_tn(ckvn, dkv)
        dckv, dgrow = _rms_bwd(_dot_nt(dkv, wkv_ref[...]), ckv * rkv, rkv, gkv_ref[...])
        dgkv_ref[...] += _colsum(dgrow)
        dc_ref[:, 256:384] = dckv

    return pl.pallas_call(
        body, name="mla_prep_bwd", grid=(T // tm,),
        in_specs=_mla_lat_specs(tm) + [pl.BlockSpec((tm, 512), lambda i: (i, 0)),
                                       pl.BlockSpec((tm, 512), lambda i: (i, 0)),
                                       pl.BlockSpec((tm, 256), lambda i: (i, 0))],
        out_specs=[pl.BlockSpec((tm, 512), lambda i: (i, 0)),
                   pl.BlockSpec((256, 512), lambda i: (0, 0)),
                   pl.BlockSpec((LANES, 768), lambda i: (0, 0)),
                   pl.BlockSpec((1, 256), lambda i: (0, 0)),
                   pl.BlockSpec((1, LANES), lambda i: (0, 0))],
        out_shape=[jax.ShapeDtypeStruct((T, 512), F32),
                   jax.ShapeDtypeStruct((256, 512), F32),
                   jax.ShapeDtypeStruct((LANES, 768), F32),
                   jax.ShapeDtypeStruct((1, 256), F32),
                   jax.ShapeDtypeStruct((1, LANES), F32)],
        compiler_params=_cparams(("arbitrary",)),
    )(hf, hf, hf, pos, g_q, g_kv, w_uq, w_ukv, dqm, dkm, dvm)


def _causal_masks(bq, bk):
    row = lax.broadcasted_iota(jnp.int32, (bq, bk), 0)
    col = lax.broadcasted_iota(jnp.int32, (bq, bk), 1)
    return row, col


def _mla_fwd(qm, km, vm):
    T = qm.shape[0]
    bq = bk = ATT_BLK
    nq = T // bq

    def body(q_ref, k_ref, v_ref, o_ref, lse_ref, acc_ref, m_ref, l_ref):
        qi = pl.program_id(1)
        lane = lax.broadcasted_iota(jnp.int32, (1, LANES), 1) // HEAD
        row, col = _causal_masks(bq, bk)
        causal = col <= row
        out = jnp.zeros((bq, LANES), F32)
        lse = jnp.zeros((bq, LANES), F32)
        for e in range(2):
            hs = slice(e * LANES, (e + 1) * LANES)
            q = q_ref[:, hs]
            acc_ref[...] = jnp.zeros_like(acc_ref)
            m_ref[...] = jnp.full_like(m_ref, NEG)
            l_ref[...] = jnp.zeros_like(l_ref)

            def step(kb, masked):
                rows = pl.ds(pl.multiple_of(kb * bk, bk), bk)
                k = k_ref[rows, hs]
                v = jnp.where(lane == e, v_ref[rows, :], 0)
                s = _dot_nt(q, k)
                if masked:
                    s = jnp.where(causal, s, NEG)
                m_prev = m_ref[...]
                m_new = jnp.maximum(m_prev, jnp.max(s, axis=1, keepdims=True))
                alpha = jnp.exp(m_prev - m_new)
                p = jnp.exp(s - m_new)
                l_ref[...] = alpha * l_ref[...] + jnp.sum(p, axis=1, keepdims=True)
                acc_ref[...] = alpha * acc_ref[...] + _dot(p.astype(BF16), v)
                m_ref[...] = m_new

            step(qi, True)

            def loop(kb, c):
                step(kb, False)
                return c

            lax.fori_loop(0, qi, loop, 0)
            out += acc_ref[...] * (1.0 / l_ref[...])
            lse = jnp.where(lane == e, m_ref[...] + jnp.log(l_ref[...]), lse)
        o_ref[...] = out
        lse_ref[...] = lse

    return pl.pallas_call(
        body, name="mla_fwd", grid=(2, nq),
        in_specs=[pl.BlockSpec((bq, 256), lambda j, i: (i, j)),
                  pl.BlockSpec((T, 256), lambda j, i: (0, j)),
                  pl.BlockSpec((T, LANES), lambda j, i: (0, j))],
        out_specs=[pl.BlockSpec((bq, LANES), lambda j, i: (i, j)),
                   pl.BlockSpec((bq, LANES), lambda j, i: (i, j))],
        out_shape=[jax.ShapeDtypeStruct((T, 256), F32), jax.ShapeDtypeStruct((T, 256), F32)],
        scratch_shapes=[pltpu.VMEM((bq, LANES), F32), pltpu.VMEM((bq, 1), F32), pltpu.VMEM((bq, 1), F32)],
        compiler_params=_cparams(("parallel", "arbitrary")),
    )(qm, km, vm)


def _mla_bwd(qm, km, vm, y, lse, dy):
    T = qm.shape[0]
    bq = bk = ATT_BLK
    nq = T // bq

    def body(q_ref, k_ref, v_ref, y_ref, lse_ref, dy_ref, dq_ref, dk_ref, dv_ref, dqa_ref):
        qi = pl.program_id(1)

        @pl.when(qi == 0)
        def _():
            dk_ref[...] = jnp.zeros_like(dk_ref)
            dv_ref[...] = jnp.zeros_like(dv_ref)

        lane = lax.broadcasted_iota(jnp.int32, (1, LANES), 1) // HEAD
        row, col = _causal_masks(bq, bk)
        causal = col <= row
        for e in range(2):
            hs = slice(e * LANES, (e + 1) * LANES)
            q = q_ref[:, hs]
            do = jnp.where(lane == e, dy_ref[...], 0.0)
            dob = do.astype(BF16)
            delta = jnp.sum(do * y_ref[...], axis=1, keepdims=True)
            lse_h = lse_ref[:, e * HEAD:e * HEAD + 1]
            dqa_ref[...] = jnp.zeros_like(dqa_ref)

            def step(kb, masked):
                rows = pl.ds(pl.multiple_of(kb * bk, bk), bk)
                k = k_ref[rows, hs]
                v = jnp.where(lane == e, v_ref[rows, :], 0)
                s = _dot_nt(q, k)
                if masked:
                    s = jnp.where(causal, s, NEG)
                p = jnp.exp(s - lse_h)
                ds = (p * (_dot_nt(dob, v) - delta)).astype(BF16)
                dv_ref[rows, :] += _dot_tn(p.astype(BF16), dob)
                dk_ref[rows, hs] += _dot_tn(ds, q)
                dqa_ref[...] += _dot(ds, k)

            step(qi, True)

            def loop(kb, c):
                step(kb, False)
                return c

            lax.fori_loop(0, qi, loop, 0)
            dq_ref[:, hs] = dqa_ref[...]

    return pl.pallas_call(
        body, name="mla_bwd", grid=(2, nq),
        in_specs=[pl.BlockSpec((bq, 256), lambda j, i: (i, j)),
                  pl.BlockSpec((T, 256), lambda j, i: (0, j)),
                  pl.BlockSpec((T, LANES), lambda j, i: (0, j)),
                  pl.BlockSpec((bq, LANES), lambda j, i: (i, j)),
                  pl.BlockSpec((bq, LANES), lambda j, i: (i, j)),
                  pl.BlockSpec((bq, LANES), lambda j, i: (i, j))],
        out_specs=[pl.BlockSpec((bq, 256), lambda j, i: (i, j)),
                   pl.BlockSpec((T, 256), lambda j, i: (0, j)),
                   pl.BlockSpec((T, LANES), lambda j, i: (0, j))],
        out_shape=[jax.ShapeDtypeStruct((T, 512), F32),
                   jax.ShapeDtypeStruct((T, 512), F32),
                   jax.ShapeDtypeStruct((T, 256), F32)],
        scratch_shapes=[pltpu.VMEM((bq, LANES), F32)],
        compiler_params=_cparams(("parallel", "arbitrary")),
    )(qm, km, vm, y, lse, dy)


def _suffix_ones(n):
    r = lax.broadcasted_iota(jnp.int32, (n, n), 0)
    c = lax.broadcasted_iota(jnp.int32, (n, n), 1)
    return (r >= c).astype(BF16)


def _prefix_ones(n):
    r = lax.broadcasted_iota(jnp.int32, (n, n), 0)
    c = lax.broadcasted_iota(jnp.int32, (n, n), 1)
    return (r <= c).astype(BF16)


def _tri_sum(x, u):
    hi, lo = _split(x)
    return _dot(hi, u) + _dot(lo, u)


def _sb_specs(T, bq):
    qo, ko, vo = (_INT_OFF[n] // LANES for n in ("d_q", "d_k", "d_v"))
    return [pl.BlockSpec((bq, LANES), lambda j, i: (i, qo + j)),
            pl.BlockSpec((T, LANES), lambda j, i: (0, ko + j)),
            pl.BlockSpec((T, LANES), lambda j, i: (0, vo + j))]


def _sb_fwd(hb):
    T = hb.shape[0]
    bq = bk = ATT_BLK
    nq = T // bq

    def body(q_ref, k_ref, v_ref, o_ref, tot_ref, acc_ref, car_ref):
        qi = pl.program_id(1)
        lane = lax.broadcasted_iota(jnp.int32, (1, LANES), 1) // HEAD
        row, col = _causal_masks(bq, bk)
        strict = col < row
        u = _suffix_ones(bk)
        acc_ref[...] = jnp.zeros_like(acc_ref)
        tot = jnp.zeros((bq, LANES), F32)
        for e in range(2):
            q = jnp.where(lane == e, q_ref[...], 0) * 0.125
            car_ref[...] = jnp.zeros_like(car_ref)

            def step(kb, masked):
                rows = pl.ds(pl.multiple_of(kb * bk, bk), bk)
                v = jnp.where(lane == e, v_ref[rows, :], 0)
                z = _dot_nt(q, k_ref[rows, :])
                lk = jnp.minimum(-z, 0.0) - jnp.log(1.0 + jnp.exp(-jnp.abs(z)))
                if masked:
                    lk = jnp.where(strict, lk, 0.0)
                suf = _tri_sum(lk, u)
                a = jnp.exp(z + suf + car_ref[...])
                if masked:
                    a = jnp.where(strict, a, 0.0)
                acc_ref[...] += _dot(a.astype(BF16), v)
                car_ref[...] += suf[:, 0:1]

            step(qi, True)

            def loop(it, c):
                step(qi - 1 - it, False)
                return c

            lax.fori_loop(0, qi, loop, 0)
            tot = jnp.where(lane == e, car_ref[...], tot)
        o_ref[...] = acc_ref[...]
        tot_ref[...] = tot

    return pl.pallas_call(
        body, name="sb_fwd", grid=(2, nq), in_specs=_sb_specs(T, bq),
        out_specs=[pl.BlockSpec((bq, LANES), lambda j, i: (i, j)), pl.BlockSpec((bq, LANES), lambda j, i: (i, j))],
        out_shape=[jax.ShapeDtypeStruct((T, 256), F32), jax.ShapeDtypeStruct((T, 256), F32)],
        scratch_shapes=[pltpu.VMEM((bq, LANES), F32), pltpu.VMEM((bq, 1), F32)],
        compiler_params=_cparams(("parallel", "arbitrary")),
    )(hb, hb, hb)


def _sb_bwd(hb, tot, dy):
    T = hb.shape[0]
    bq = bk = ATT_BLK
    nq = T // bq

    def body(q_ref, k_ref, v_ref, tot_ref, dy_ref, dq_ref, dk_ref, dv_ref, dqa_ref, rem_ref, cg_ref):
        qi = pl.program_id(1)

        @pl.when(qi == 0)
        def _():
            dk_ref[...] = jnp.zeros_like(dk_ref)
            dv_ref[...] = jnp.zeros_like(dv_ref)

        lane = lax.broadcasted_iota(jnp.int32, (1, LANES), 1) // HEAD
        row, col = _causal_masks(bq, bk)
        strict = col < row
        u = _prefix_ones(bk)
        dq_pair = jnp.zeros((bq, LANES), F32)
        for e in range(2):
            q = jnp.where(lane == e, q_ref[...], 0) * 0.125
            dob = jnp.where(lane == e, dy_ref[...], 0.0).astype(BF16)
            dqa_ref[...] = jnp.zeros_like(dqa_ref)
            rem_ref[...] = tot_ref[:, e * HEAD:e * HEAD + 1]
            cg_ref[...] = jnp.zeros_like(cg_ref)

            def step(kb, masked):
                rows = pl.ds(pl.multiple_of(kb * bk, bk), bk)
                k = k_ref[rows, :]
                v = jnp.where(lane == e, v_ref[rows, :], 0)
                z = _dot_nt(q, k)
                lk = jnp.minimum(-z, 0.0) - jnp.log(1.0 + jnp.exp(-jnp.abs(z)))
                if masked:
                    lk = jnp.where(strict, lk, 0.0)
                pre = _tri_sum(lk, u)
                a = jnp.exp(z + lk + (rem_ref[...] - pre))
                if masked:
                    a = jnp.where(strict, a, 0.0)
                g = a * _dot_nt(dob, v)
                gpre = _tri_sum(g, u)
                dz = g - jnp.exp(z + lk) * (cg_ref[...] + gpre)
                if masked:
                    dz = jnp.where(strict, dz, 0.0)
                dzb = dz.astype(BF16)
                dv_ref[rows, :] += _dot_tn(a.astype(BF16), dob)
                dk_ref[rows, :] += _dot_tn(dzb, q)
                dqa_ref[...] += _dot(dzb, k)
                rem_ref[...] -= pre[:, bk - 1:bk]
                cg_ref[...] += gpre[:, bk - 1:bk]

            def loop(kb, c):
                step(kb, False)
                return c

            lax.fori_loop(0, qi, loop, 0)
            step(qi, True)
            dq_pair += jnp.where(lane == e, dqa_ref[...], 0.0) * 0.125
        dq_ref[...] = dq_pair

    return pl.pallas_call(
        body, name="sb_bwd", grid=(2, nq),
        in_specs=_sb_specs(T, bq) + [pl.BlockSpec((bq, LANES), lambda j, i: (i, j)),
                                     pl.BlockSpec((bq, LANES), lambda j, i: (i, j))],
        out_specs=[pl.BlockSpec((bq, LANES), lambda j, i: (i, j)),
                   pl.BlockSpec((T, LANES), lambda j, i: (0, j)),
                   pl.BlockSpec((T, LANES), lambda j, i: (0, j))],
        out_shape=[jax.ShapeDtypeStruct((T, 256), F32)] * 3,
        scratch_shapes=[pltpu.VMEM((bq, LANES), F32), pltpu.VMEM((bq, 1), F32), pltpu.VMEM((bq, 1), F32)],
        compiler_params=_cparams(("parallel", "arbitrary")),
    )(hb, hb, hb, tot, dy)


EP_TM = 256


def _ep_in_specs(tm, rev):
    idx = (lambda i: rev - i) if rev is not None else (lambda i: i)
    bo = (_INT_OFF["b_b"] - N_HB) // 256
    halo = lambda i: jnp.maximum(idx(i) * (tm // 8) - 1, 0)
    return [pl.BlockSpec((tm, 256), lambda i: (idx(i), 0)),
            pl.BlockSpec((tm, 256), lambda i: (idx(i), 0)),
            pl.BlockSpec((tm, 256), lambda i: (idx(i), 0)),
            pl.BlockSpec((tm, D_MODEL), lambda i: (idx(i), 0)),
            pl.BlockSpec((tm, 256), lambda i: (idx(i), bo)),
            pl.BlockSpec((tm, 256), lambda i: (idx(i), bo + 1)),
            pl.BlockSpec((tm, 256), lambda i: (idx(i), bo + 2)),
            pl.BlockSpec((8, 256), lambda i: (halo(i), bo + 1)),
            pl.BlockSpec((8, 256), lambda i: (halo(i), bo + 2)),
            pl.BlockSpec((3, 256), lambda i: (0, 0)),
            pl.BlockSpec((1, 256), lambda i: (0, 0)),
            pl.BlockSpec((1, D_MODEL), lambda i: (0, 0)),
            pl.BlockSpec((D_MODEL, D_MODEL), lambda i: (0, 0)),
            pl.BlockSpec((1, D_MODEL), lambda i: (0, 0))]


def _ep_mix(first, ya_ref, yc_ref, yd_ref, gate_ref, bb_ref, bc_ref, bx_ref, hc_ref, hx_ref, cw_ref, cb_ref, gg_ref):
    tm = ya_ref.shape[0]
    u = bc_ref[...] * bx_ref[...]
    halo = jnp.where(first, 0.0, hc_ref[...] * hx_ref[...])
    row = lax.broadcasted_iota(jnp.int32, (tm, 1), 0)
    u1 = jnp.where(row == 0, halo[7:8, :], pltpu.roll(u, 1, 0))
    u2 = jnp.where(row == 0, halo[6:7, :], jnp.where(row == 1, halo[7:8, :], pltpu.roll(u, 2, 0)))
    cw = cw_ref[...]
    conv = cw[0:1, :] * u2 + cw[1:2, :] * u1 + cw[2:3, :] * u + cb_ref[...]
    bb = bb_ref[...]
    ys = [ya_ref[...], bb * conv, yc_ref[...], yd_ref[...]]
    rs = [_rms(y) for y in ys]
    gg = gg_ref[...]
    yhat = jnp.concatenate([y * r for y, r in zip(ys, rs)], axis=1)
    gate = gate_ref[...]
    sig = 1.0 / (1.0 + jnp.exp(-gate))
    return u, u1, u2, conv, bb, rs, yhat, yhat * gg, gate, sig


def _epilogue_fwd(x, ya, yc, yd, hf, conv_w, conv_b, g_grp, w_out, g_post):
    T = x.shape[0]
    tm = EP_TM

    def body(x_ref, ya_ref, yc_ref, yd_ref, gate_ref, bb_ref, bc_ref, bx_ref, hc_ref, hx_ref, cw_ref, cb_ref,
             gg_ref, wo_ref, gp_ref, o_ref):
        (_, _, _, _, _, _, _, yn, gate, sig) = _ep_mix(
            pl.program_id(0) == 0, ya_ref, yc_ref, yd_ref, gate_ref, bb_ref, bc_ref, bx_ref, hc_ref, hx_ref,
            cw_ref, cb_ref, gg_ref)
        z = _dot((yn * (gate * sig)).astype(BF16), wo_ref[...])
        o_ref[...] = x_ref[...] + z * _rms(z) * gp_ref[...]

    return pl.pallas_call(
        body, name="epilogue_fwd", grid=(T // tm,),
        in_specs=[pl.BlockSpec((tm, D_MODEL), lambda i: (i, 0))] + _ep_in_specs(tm, None),
        out_specs=pl.BlockSpec((tm, D_MODEL), lambda i: (i, 0)),
        out_shape=jax.ShapeDtypeStruct((T, D_MODEL), F32),
        compiler_params=_cparams(("parallel",)),
    )(x, ya, yc, yd, hf, hf, hf, hf, hf, hf, conv_w, conv_b, g_grp, w_out, g_post)


def _epilogue_bwd(dxn, ya, yc, yd, hf, conv_w, conv_b, g_grp, w_out, g_post):
    T = dxn.shape[0]
    tm = EP_TM
    nt = T // tm
    ridx = lambda i: (nt - 1 - i, 0)

    def body(dx_ref, ya_ref, yc_ref, yd_ref, gate_ref, bb_ref, bc_ref, bx_ref, hc_ref, hx_ref, cw_ref, cb_ref,
             gg_ref, wo_ref, gp_ref,
             dya_ref, dyc_ref, dyd_ref, dhf_ref, dwo_ref, dgp_ref, dgg_ref, dcw_ref, dcb_ref, carry_ref):
        i = pl.program_id(0)

        @pl.when(i == 0)
        def _():
            for r in (dwo_ref, dgp_ref, dgg_ref, dcw_ref, dcb_ref, carry_ref):
                r[...] = jnp.zeros_like(r)

        (u, u1, u2, conv, bb, rs, yhat, yn, gate, sig) = _ep_mix(
            i == nt - 1, ya_ref, yc_ref, yd_ref, gate_ref, bb_ref, bc_ref, bx_ref, hc_ref, hx_ref,
            cw_ref, cb_ref, gg_ref)
        silu = gate * sig
        ymix = (yn * silu).astype(BF16)
        z = _dot(ymix, wo_ref[...])
        rz = _rms(z)
        dz, dgrow = _rms_bwd(dx_ref[...], z * rz, rz, gp_ref[...])
        dgp_ref[...] += _colsum(dgrow)
        dzb = dz.astype(BF16)
        dwo_ref[...] += _dot_tn(ymix, dzb)
        dymix = _dot_nt(dzb, wo_ref[...])
        dhf_ref[:, 0:D_MODEL] = dymix * yn * (sig * (1.0 + gate * (1.0 - sig)))
        dyn = dymix * silu
        dgg_ref[...] += _colsum(dyn * yhat)
        gg = gg_ref[...]
        dys = []
        for gi in range(4):
            sl = slice(gi * GROUP, (gi + 1) * GROUP)
            dyh = dyn[:, sl] * gg[:, sl]
            yh = yhat[:, sl]
            dys.append(rs[gi] * (dyh - yh * jnp.mean(dyh * yh, axis=-1, keepdims=True)))
        dya_ref[...] = dys[0]
        dyc_ref[...] = dys[2]
        dyd_ref[...] = dys[3]
        dyb = dys[1]
        dhf_ref[:, D_MODEL:D_MODEL + 256] = dyb * conv
        dconv = dyb * bb
        dcb_ref[...] += _colsum(dconv)
        dcw_ref[0:1, :] += _colsum(dconv * u2)
        dcw_ref[1:2, :] += _colsum(dconv * u1)
        dcw_ref[2:3, :] += _colsum(dconv * u)
        carry = carry_ref[...]
        row = lax.broadcasted_iota(jnp.int32, (tm, 1), 0)
        d1 = jnp.where(row == tm - 1, carry[0:1, :], pltpu.roll(dconv, tm - 1, 0))
        d2 = jnp.where(row == tm - 2, carry[0:1, :],
                       jnp.where(row == tm - 1, carry[1:2, :], pltpu.roll(dconv, tm - 2, 0)))
        cw = cw_ref[...]
        du = cw[2:3, :] * dconv + cw[1:2, :] * d1 + cw[0:1, :] * d2
        dhf_ref[:, D_MODEL + 256:D_MODEL + 512] = du * bx_ref[...]
        dhf_ref[:, D_MODEL + 512:D_MODEL + 768] = du * bc_ref[...]
        carry_ref[...] = dconv[0:8, :]

    in_specs = [pl.BlockSpec((tm, D_MODEL), ridx)] + _ep_in_specs(tm, nt - 1)
    return pl.pallas_call(
        body, name="epilogue_bwd", grid=(nt,), in_specs=in_specs,
        out_specs=[pl.BlockSpec((tm, 256), ridx), pl.BlockSpec((tm, 256), ridx), pl.BlockSpec((tm, 256), ridx),
                   pl.BlockSpec((tm, D_MODEL + 768), ridx),
                   pl.BlockSpec((D_MODEL, D_MODEL), lambda i: (0, 0)),
                   pl.BlockSpec((1, D_MODEL), lambda i: (0, 0)),
                   pl.BlockSpec((1, D_MODEL), lambda i: (0, 0)),
                   pl.BlockSpec((8, 256), lambda i: (0, 0)),
                   pl.BlockSpec((1, 256), lambda i: (0, 0))],
        out_shape=[jax.ShapeDtypeStruct((T, 256), F32)] * 3
                  + [jax.ShapeDtypeStruct((T, D_MODEL + 768), F32),
                     jax.ShapeDtypeStruct((D_MODEL, D_MODEL), F32),
                     jax.ShapeDtypeStruct((1, D_MODEL), F32),
                     jax.ShapeDtypeStruct((1, D_MODEL), F32),
                     jax.ShapeDtypeStruct((8, 256), F32),
                     jax.ShapeDtypeStruct((1, 256), F32)],
        scratch_shapes=[pltpu.VMEM((8, 256), F32)],
        compiler_params=_cparams(("arbitrary",)),
    )(dxn, ya, yc, yd, hf, hf, hf, hf, hf, hf, conv_w, conv_b, g_grp, w_out, g_post)


def _loss_head(y, tgt):
    T = y.shape[0]
    tm = 512

    def body(y_ref, t_ref, dy_ref, l_ref):
        @pl.when(pl.program_id(0) == 0)
        def _():
            l_ref[...] = jnp.zeros_like(l_ref)

        d = y_ref[...] - t_ref[...]
        dy_ref[...] = d * (1.0 / D_MODEL)
        part = jnp.sum(jnp.sum(d * d, axis=1, keepdims=True), axis=0, keepdims=True)
        l_ref[...] += part * (0.5 / D_MODEL)

    return pl.pallas_call(
        body, name="loss_head", grid=(T // tm,),
        in_specs=[pl.BlockSpec((tm, D_MODEL), lambda i: (i, 0))] * 2,
        out_specs=[pl.BlockSpec((tm, D_MODEL), lambda i: (i, 0)), pl.BlockSpec((8, LANES), lambda i: (0, 0))],
        out_shape=[jax.ShapeDtypeStruct((T, D_MODEL), F32), jax.ShapeDtypeStruct((8, LANES), F32)],
        compiler_params=_cparams(("arbitrary",)),
    )(y, tgt)


def _place():
    return lax.axis_index("x"), lax.axis_index("y"), lax.axis_index("c")


def _other_chips(x, y):
    return [(1 - x, y), (x, 1 - y), (1 - x, 1 - y)]


HBM = pl.BlockSpec(memory_space=pl.ANY)


def _gather_chips(big, small):
    def body(big_ref, small_ref, obig_ref, osmall_ref, send_sems, recv_sems, local_sems):
        x, y, c = _place()
        me = 2 * x + y
        pairs = [(big_ref, obig_ref), (small_ref, osmall_ref)]
        local = [pltpu.make_async_copy(s, o.at[me], local_sems.at[n]) for n, (s, o) in enumerate(pairs)]
        for cp in local:
            cp.start()
        sends = []
        for j, (px, py) in enumerate(_other_chips(x, y)):
            for n, (s, o) in enumerate(pairs):
                cp = pltpu.make_async_remote_copy(
                    src_ref=s, dst_ref=o.at[me], send_sem=send_sems.at[2 * j + n], recv_sem=recv_sems.at[2 * j + n],
                    device_id=(px, py, c), device_id_type=MESH)
                cp.start()
                sends.append(cp)
        for j, (px, py) in enumerate(_other_chips(x, y)):
            for n, (s, o) in enumerate(pairs):
                pltpu.make_async_remote_copy(
                    src_ref=s, dst_ref=o.at[2 * px + py], send_sem=send_sems.at[2 * j + n],
                    recv_sem=recv_sems.at[2 * j + n], device_id=(px, py, c), device_id_type=MESH).wait_recv()
        for cp in sends:
            cp.wait_send()
        for cp in local:
            cp.wait()

    return pl.pallas_call(
        body, name="gather_chips",
        in_specs=[HBM, HBM], out_specs=[HBM, HBM],
        out_shape=[jax.ShapeDtypeStruct((4,) + big.shape, big.dtype),
                   jax.ShapeDtypeStruct((4,) + small.shape, small.dtype)],
        scratch_shapes=[pltpu.SemaphoreType.DMA((6,)), pltpu.SemaphoreType.DMA((6,)), pltpu.SemaphoreType.DMA((2,))],
    )(big, small)


def _scatter_chips(chunks, small):
    def body(ch_ref, sm_ref, och_ref, osm_ref, send_sems, recv_sems, ssend_sems, srecv_sems, local_sems):
        x, y, c = _place()
        me = 2 * x + y
        dev = 4 * x + 2 * y + c
        local = [pltpu.make_async_copy(ch_ref.at[me], och_ref.at[me], local_sems.at[0]),
                 pltpu.make_async_copy(sm_ref, osm_ref.at[dev], local_sems.at[1])]
        for cp in local:
            cp.start()
        sends = []
        for j, (px, py) in enumerate(_other_chips(x, y)):
            cp = pltpu.make_async_remote_copy(
                src_ref=ch_ref.at[2 * px + py], dst_ref=och_ref.at[me], send_sem=send_sems.at[j],
                recv_sem=recv_sems.at[j], device_id=(px, py, c), device_id_type=MESH)
            cp.start()
            sends.append(cp)
        flips = [(fx, fy, fc) for fx in (0, 1) for fy in (0, 1) for fc in (0, 1)][1:]
        for j, (fx, fy, fc) in enumerate(flips):
            cp = pltpu.make_async_remote_copy(
                src_ref=sm_ref, dst_ref=osm_ref.at[dev], send_sem=ssend_sems.at[j], recv_sem=srecv_sems.at[j],
                device_id=(x ^ fx, y ^ fy, c ^ fc), device_id_type=MESH)
            cp.start()
            sends.append(cp)
        for j, (px, py) in enumerate(_other_chips(x, y)):
            pltpu.make_async_remote_copy(
                src_ref=ch_ref.at[me], dst_ref=och_ref.at[2 * px + py], send_sem=send_sems.at[j],
                recv_sem=recv_sems.at[j], device_id=(px, py, c), device_id_type=MESH).wait_recv()
        for j, (fx, fy, fc) in enumerate(flips):
            src = 4 * (x ^ fx) + 2 * (y ^ fy) + (c ^ fc)
            pltpu.make_async_remote_copy(
                src_ref=sm_ref, dst_ref=osm_ref.at[src], send_sem=ssend_sems.at[j], recv_sem=srecv_sems.at[j],
                device_id=(x ^ fx, y ^ fy, c ^ fc), device_id_type=MESH).wait_recv()
        for cp in sends:
            cp.wait_send()
        for cp in local:
            cp.wait()

    return pl.pallas_call(
        body, name="scatter_chips",
        in_specs=[HBM, HBM], out_specs=[HBM, HBM],
        out_shape=[jax.ShapeDtypeStruct(chunks.shape, chunks.dtype),
                   jax.ShapeDtypeStruct((8,) + small.shape, small.dtype)],
        scratch_shapes=[pltpu.SemaphoreType.DMA((3,)), pltpu.SemaphoreType.DMA((3,)),
                        pltpu.SemaphoreType.DMA((7,)), pltpu.SemaphoreType.DMA((7,)),
                        pltpu.SemaphoreType.DMA((2,))],
    )(chunks, small)


def _swap_cores(part):
    def body(p_ref, o_ref, send_sem, recv_sem):
        x, y, c = _place()
        cp = pltpu.make_async_remote_copy(src_ref=p_ref, dst_ref=o_ref, send_sem=send_sem, recv_sem=recv_sem,
                                          device_id=(x, y, 1 - c), device_id_type=MESH)
        cp.start()
        cp.wait()

    return pl.pallas_call(
        body, name="swap_cores", in_specs=[HBM], out_specs=HBM,
        out_shape=jax.ShapeDtypeStruct(part.shape, part.dtype),
        scratch_shapes=[pltpu.SemaphoreType.DMA, pltpu.SemaphoreType.DMA],
    )(part)


def _sum_leading(buf, name):
    n, R, C = buf.shape
    tr = R
    for cand in (512, 256, 128, 64, 32, 16, 8):
        if R % cand == 0:
            tr = cand
            break

    def body(b_ref, o_ref):
        acc = b_ref[0]
        for k in range(1, n):
            acc = acc + b_ref[k]
        o_ref[...] = acc

    return pl.pallas_call(
        body, name=name, grid=(R // tr,),
        in_specs=[pl.BlockSpec((n, tr, C), lambda i: (0, i, 0))],
        out_specs=pl.BlockSpec((tr, C), lambda i: (i, 0)),
        out_shape=jax.ShapeDtypeStruct((R, C), F32),
        compiler_params=_cparams(("parallel",)),
    )(buf)


def _adamw(w, grads, m, v, name):
    R, C = w.shape
    tr = R
    for cand in (256, 128, 64, 32, 16, 8):
        if R % cand == 0:
            tr = cand
            break
    ng = len(grads)
    c1 = 1.0 / (1.0 - ADAM_B1 ** ADAM_STEP)
    c2 = 1.0 / (1.0 - ADAM_B2 ** ADAM_STEP)

    def body(w_ref, *rest):
        g_refs = rest[:ng]
        m_ref, v_ref, go_ref, d_ref, mo_ref, vo_ref = rest[ng:]
        g = g_refs[0][...]
        for r in g_refs[1:]:
            g = g + r[...]
        go_ref[...] = g
        mn = ADAM_B1 * m_ref[...] + (1.0 - ADAM_B1) * g
        vn = ADAM_B2 * v_ref[...] + (1.0 - ADAM_B2) * (g * g)
        mo_ref[...] = mn
        vo_ref[...] = vn
        d_ref[...] = -ADAM_LR * ((mn * c1) / (jnp.sqrt(vn * c2) + ADAM_EPS) + ADAM_WD * w_ref[...])

    spec = pl.BlockSpec((tr, C), lambda i: (i, 0))
    return pl.pallas_call(
        body, name=name, grid=(R // tr,),
        in_specs=[spec] * (3 + ng), out_specs=[spec] * 4,
        out_shape=[jax.ShapeDtypeStruct((R, C), F32)] * 4,
        compiler_params=_cparams(("parallel",)),
    )(w, *grads, m, v)


PACK_C = 1024
PACK_MULT = 64
_BIG = ("w_in", "w_out", "mla_w_uq", "mla_w_ukv", "conv_w")


def _pack_rows(parts, mult):
    flat = jnp.concatenate([p.reshape(-1) for p in parts])
    n = flat.shape[0]
    rows = -(-n // PACK_C)
    rows = -(-rows // mult) * mult
    return jnp.pad(flat, (0, rows * PACK_C - n)).reshape(rows, PACK_C)


def _unpack_rows(buf, shapes):
    flat = buf.reshape(-1)
    out, o = [], 0
    for s in shapes:
        n = math.prod(s)
        out.append(flat[o:o + n].reshape(s))
        o += n
    return out


_SMALL = ("norm_pre", "group_norm", "norm_post", "conv_b", "mla_q_norm", "mla_kv_norm", "attn_sinks")


def _pack_small(d):
    return _pack_rows([d[n] for n in _SMALL], 8)


def _unpack_small(buf, shapes):
    return dict(zip(_SMALL, _unpack_rows(buf, [shapes[n] for n in _SMALL])))


def _w_in_internal(w):
    cols = []
    for n in _INT_ORDER:
        o, wd = _REAL_OFF[n]
        cols.append(w[:, o:o + wd])
        if _INT_W[n] != wd:
            cols.append(jnp.zeros((w.shape[0], _INT_W[n] - wd), w.dtype))
    return jnp.concatenate(cols, axis=1)


def _w_in_real(dw):
    return jnp.concatenate([dw[:, _INT_OFF[n]:_INT_OFF[n] + wd] for n, wd in _REAL], axis=1)


def _uq_internal(w):
    return jnp.pad(w.reshape(256, 4, 96), ((0, 0), (0, 0), (0, 32))).reshape(256, 512)


def _uq_real(dw):
    return dw.reshape(256, 4, 128)[:, :, :96].reshape(256, 384)


def _ukv_internal(w):
    w4 = w.reshape(128, 4, 128)
    k = jnp.pad(w4[:, :, :64], ((0, 0), (0, 0), (0, 64))).reshape(128, 512)
    return jnp.concatenate([k, w4[:, :, 64:].reshape(128, 256)], axis=1)


def _ukv_real(dw):
    k = dw[:, :512].reshape(128, 4, 128)[:, :, :64]
    v = dw[:, 512:].reshape(128, 4, 64)
    return jnp.concatenate([k, v], axis=2).reshape(128, 512)


def _layer_fwd(x, pos, p):
    xn, hb, hf = _inproj_fwd(x, p["norm_pre"], p["w_in"])
    ya = _swa_fwd(hb, p["attn_sinks"])
    qm, km, vm = _mla_prep_fwd(hf, pos, p["mla_q_norm"], p["mla_kv_norm"], p["mla_w_uq"], p["mla_w_ukv"])
    yc, lse = _mla_fwd(qm, km, vm)
    yd, tot = _sb_fwd(hb)
    x_next = _epilogue_fwd(x, ya, yc, yd, hf, p["conv_w"], p["conv_b"], p["group_norm"], p["w_out"], p["norm_post"])
    return x_next, dict(x=x, xn=xn, hb=hb, hf=hf, ya=ya, yc=yc, yd=yd, tot=tot, qm=qm, km=km, vm=vm, lse=lse)


def _layer_bwd(dx_next, pos, p, s):
    (dya, dyc, dyd, dhf, dw_out, dg_post, dg_grp, dconv_w, dconv_b) = _epilogue_bwd(
        dx_next, s["ya"], s["yc"], s["yd"], s["hf"], p["conv_w"], p["conv_b"], p["group_norm"], p["w_out"],
        p["norm_post"])
    dq_d, dk_d, dv_d = _sb_bwd(s["hb"], s["tot"], dyd)
    dqm, dkm, dvm = _mla_bwd(s["qm"], s["km"], s["vm"], s["yc"], s["lse"], dyc)
    dc, dw_uq, dw_ukv, dg_q, dg_kv = _mla_prep_bwd(
        s["hf"], pos, p["mla_q_norm"], p["mla_kv_norm"], p["mla_w_uq"], p["mla_w_ukv"], dqm, dkm, dvm)
    dq_a, dk_a, dv_a, dsinks = _swa_bwd(s["hb"], p["attn_sinks"], dya)
    dx, dh, dg_pre = _inproj_bwd_dx(s["x"], p["norm_pre"], p["w_in"], dx_next,
                                    [dq_a, dk_a, dv_a, dq_d, dk_d, dv_d, dhf, dc])
    dw_in = _matmul_tn(s["xn"], dh, "inproj_bwd_dw")
    grads = dict(norm_pre=dg_pre[0], w_in=_w_in_real(dw_in), attn_sinks=dsinks[0, :4], conv_w=dconv_w[:3],
                 conv_b=dconv_b[0], mla_q_norm=dg_q[0], mla_w_uq=_uq_real(dw_uq), mla_kv_norm=dg_kv[0],
                 mla_w_ukv=_ukv_real(dw_ukv), group_norm=dg_grp[0], w_out=dw_out, norm_post=dg_post[0])
    return dx, grads


_WEIGHTS = ["norm_pre", "w_in", "attn_sinks", "conv_w", "conv_b", "mla_q_norm", "mla_w_uq", "mla_kv_norm",
            "mla_w_ukv", "group_norm", "w_out", "norm_post"]


def kernel(x, positions, norm_pre, w_in, attn_sinks, conv_w, conv_b, mla_q_norm, mla_w_uq, mla_kv_norm, mla_w_ukv, group_norm, w_out, norm_post, loss_target, m_norm_pre, m_w_in, m_attn_sinks, m_conv_w, m_conv_b, m_mla_q_norm, m_mla_w_uq, m_mla_kv_norm, m_mla_w_ukv, m_group_norm, m_w_out, m_norm_post, v_norm_pre, v_w_in, v_attn_sinks, v_conv_w, v_conv_b, v_mla_q_norm, v_mla_w_uq, v_mla_kv_norm, v_mla_w_ukv, v_group_norm, v_w_out, v_norm_post):
    w = dict(norm_pre=norm_pre, w_in=w_in, attn_sinks=attn_sinks, conv_w=conv_w, conv_b=conv_b,
             mla_q_norm=mla_q_norm, mla_w_uq=mla_w_uq, mla_kv_norm=mla_kv_norm, mla_w_ukv=mla_w_ukv,
             group_norm=group_norm, w_out=w_out, norm_post=norm_post)
    m = dict(norm_pre=m_norm_pre, w_in=m_w_in, attn_sinks=m_attn_sinks, conv_w=m_conv_w, conv_b=m_conv_b,
             mla_q_norm=m_mla_q_norm, mla_w_uq=m_mla_w_uq, mla_kv_norm=m_mla_kv_norm, mla_w_ukv=m_mla_w_ukv,
             group_norm=m_group_norm, w_out=m_w_out, norm_post=m_norm_post)
    v = dict(norm_pre=v_norm_pre, w_in=v_w_in, attn_sinks=v_attn_sinks, conv_w=v_conv_w, conv_b=v_conv_b,
             mla_q_norm=v_mla_q_norm, mla_w_uq=v_mla_w_uq, mla_kv_norm=v_mla_kv_norm, mla_w_ukv=v_mla_w_ukv,
             group_norm=v_group_norm, w_out=v_w_out, norm_post=v_norm_post)
    T = x.shape[1]
    xs = x[0]
    pos = positions[0].reshape(T, 1)
    tgt = loss_target[0]
    big_shapes = [w[n].shape for n in _BIG]
    mm = _BIG[:4]

    wb = _pack_rows([w[n].astype(BF16) for n in mm], 16)
    wc = _pack_rows([w["conv_w"]], 8)
    gb, gc = _gather_chips(wb, wc)
    per_chip = [_unpack_rows(gb[k], big_shapes[:4]) + _unpack_rows(gc[k], big_shapes[4:]) for k in range(4)]
    full = {}
    for n_i, n in enumerate(_BIG):
        axis = 1 if n == "w_out" else 2
        full[n] = jnp.concatenate([per_chip[k][n_i] for k in range(4)], axis=axis)

    layers = []
    for l in range(DEPTH):
        layers.append(dict(
            norm_pre=norm_pre[l:l + 1], w_in=_w_in_internal(full["w_in"][l]), attn_sinks=attn_sinks[l],
            conv_w=full["conv_w"][l], conv_b=conv_b[l:l + 1], mla_q_norm=mla_q_norm[l:l + 1],
            mla_w_uq=_uq_internal(full["mla_w_uq"][l]), mla_kv_norm=mla_kv_norm[l:l + 1],
            mla_w_ukv=_ukv_internal(full["mla_w_ukv"][l]), group_norm=group_norm[l:l + 1],
            w_out=full["w_out"][l], norm_post=norm_post[l:l + 1]))

    saved = []
    h = xs
    for l in range(DEPTH):
        h, s = _layer_fwd(h, pos, layers[l])
        saved.append(s)
    dy, loss_part = _loss_head(h, tgt)
    loss = lax.psum(loss_part[0, 0], ("x", "y", "c"))

    grads = [None] * DEPTH
    for l in reversed(range(DEPTH)):
        dy, grads[l] = _layer_bwd(dy, pos, layers[l], saved[l])
    g = {n: jnp.stack([grads[l][n] for l in range(DEPTH)]) for n in _WEIGHTS}

    def shard(n, k):
        a = g[n]
        if n == "w_out":
            return a[:, 256 * k:256 * (k + 1), :]
        wd = a.shape[2] // 4
        return a[:, :, wd * k:wd * (k + 1)]

    chunks = jnp.stack([_pack_rows([shard(n, k) for n in _BIG], PACK_MULT) for k in range(4)])
    small = _pack_small(g)
    got, got_small = _scatter_chips(chunks, small)
    part = _sum_leading(got, "sum_chips")
    other = _swap_cores(part)
    g_small = _sum_leading(got_small, "sum_small")

    pack_big = lambda d: _pack_rows([d[n] for n in _BIG], PACK_MULT)
    gb_, db_, mb_, vb_ = _adamw(pack_big(w), [part, other], pack_big(m), pack_big(v), "adamw_big")
    gs_, ds_, ms_, vs_ = _adamw(_pack_small(w), [g_small], _pack_small(m), _pack_small(v), "adamw_small")
    small_shapes = {n: w[n].shape for n in _SMALL}
    outs = []
    for big, sm in ((gb_, gs_), (db_, ds_), (mb_, ms_), (vb_, vs_)):
        d = dict(zip(_BIG, _unpack_rows(big, big_shapes)))
        d.update(_unpack_small(sm, small_shapes))
        outs.append(d)
    return (loss, dy[None], *[outs[0][n] for n in _WEIGHTS], *[outs[1][n] for n in _WEIGHTS],
            *[outs[2][n] for n in _WEIGHTS], *[outs[3][n] for n in _WEIGHTS])
```

```python
import functools
import math

import jax
import jax.numpy as jnp
from jax import lax
from jax.experimental import pallas as pl
from jax.experimental.pallas import tpu as pltpu

F32 = jnp.float32
BF16 = jnp.bfloat16
MESH = pl.DeviceIdType.MESH

D_MODEL = 1024
DEPTH = 2
EPS = 1e-6
BLOCK = 128
HEAD = 64
LANES = 128
GROUP = 256
MLA_SCALE = 96 ** -0.5
ROPE_HALF = 16
ROPE_THETA = 10000.0
ATT_BLK = 256
NEG = -1e30

ADAM_LR, ADAM_B1, ADAM_B2, ADAM_EPS, ADAM_WD, ADAM_STEP = 0.001, 0.9, 0.999, 1e-08, 0.01, 10

_REAL = [("a_q", 256), ("a_k", 128), ("a_v", 128), ("b_b", 256), ("b_c", 256), ("b_x", 256),
         ("c_q", 256), ("c_kv", 128), ("c_kr", 32), ("d_q", 256), ("d_k", 256), ("d_v", 256),
         ("gate", 1024)]
_REAL_OFF = {}
_o = 0
for _n, _w in _REAL:
    _REAL_OFF[_n] = (_o, _w)
    _o += _w
D_IN = _o
_INT_ORDER = ["a_q", "a_k", "a_v", "d_q", "d_k", "d_v", "gate", "b_b", "b_c", "b_x", "c_q", "c_kv", "c_kr"]
_INT_W = dict(_REAL)
_INT_W["c_kr"] = 128
_INT_OFF = {}
_o = 0
for _n in _INT_ORDER:
    _INT_OFF[_n] = _o
    _o += _INT_W[_n]
N_INT = _o
N_HB = _INT_OFF["gate"]
N_HF = N_INT - N_HB

VMEM_LIMIT = 56 * 1024 * 1024


def _cparams(sem):
    return pltpu.CompilerParams(dimension_semantics=sem, vmem_limit_bytes=VMEM_LIMIT)


def _dot(a, b):
    return jnp.dot(a, b, preferred_element_type=F32)


def _dot_nt(a, b):
    return lax.dot_general(a, b, (((1,), (1,)), ((), ())), preferred_element_type=F32)


def _dot_tn(a, b):
    return lax.dot_general(a, b, (((0,), (0,)), ((), ())), preferred_element_type=F32)


def _split(x):
    hi = x.astype(BF16)
    lo = (x - hi.astype(F32)).astype(BF16)
    return hi, lo


def _rms(x):
    return lax.rsqrt(jnp.mean(x * x, axis=-1, keepdims=True) + EPS)


def _rms_bwd(dy, xhat, r, g):
    dxhat = dy * g
    return r * (dxhat - xhat * jnp.mean(dxhat * xhat, axis=-1, keepdims=True)), dy * xhat


def _colsum(x):
    return jnp.sum(x, axis=0, keepdims=True)


def _inproj_fwd(x, g, w):
    T = x.shape[0]
    tm = 256

    def body(x_ref, g_ref, w_ref, xn_ref, hb_ref, hf_ref):
        xv = x_ref[...]
        xn = (xv * _rms(xv) * g_ref[...]).astype(BF16)
        xn_ref[...] = xn
        h = _dot(xn, w_ref[...])
        hb_ref[...] = h[:, :N_HB].astype(BF16)
        hf_ref[...] = h[:, N_HB:]

    return pl.pallas_call(
        body, name="inproj_fwd", grid=(T // tm,),
        in_specs=[pl.BlockSpec((tm, D_MODEL), lambda i: (i, 0)),
                  pl.BlockSpec((1, D_MODEL), lambda i: (0, 0)),
                  pl.BlockSpec((D_MODEL, N_INT), lambda i: (0, 0))],
        out_specs=[pl.BlockSpec((tm, D_MODEL), lambda i: (i, 0)),
                   pl.BlockSpec((tm, N_HB), lambda i: (i, 0)),
                   pl.BlockSpec((tm, N_HF), lambda i: (i, 0))],
        out_shape=[jax.ShapeDtypeStruct((T, D_MODEL), BF16),
                   jax.ShapeDtypeStruct((T, N_HB), BF16),
                   jax.ShapeDtypeStruct((T, N_HF), F32)],
        compiler_params=_cparams(("parallel",)),
    )(x, g, w)


def _inproj_bwd_dx(x, g, w, dx_next, pieces):
    T = x.shape[0]
    tm = 256
    widths = [p.shape[1] for p in pieces]
    assert sum(widths) == N_INT

    def body(x_ref, g_ref, w_ref, dxn_ref, *rest):
        p_refs = rest[:len(pieces)]
        dx_ref, dh_ref, dg_ref = rest[len(pieces):]
        dh = jnp.concatenate([p[...].astype(BF16) for p in p_refs], axis=1)
        dh_ref[...] = dh
        dxn = _dot_nt(dh, w_ref[...])
        xv = x_ref[...]
        r = _rms(xv)
        dx, dgrow = _rms_bwd(dxn, xv * r, r, g_ref[...])
        dx_ref[...] = dx + dxn_ref[...]

        @pl.when(pl.program_id(0) == 0)
        def _():
            dg_ref[...] = jnp.zeros_like(dg_ref)

        dg_ref[...] += _colsum(dgrow)

    return pl.pallas_call(
        body, name="inproj_bwd_dx", grid=(T // tm,),
        in_specs=[pl.BlockSpec((tm, D_MODEL), lambda i: (i, 0)),
                  pl.BlockSpec((1, D_MODEL), lambda i: (0, 0)),
                  pl.BlockSpec((D_MODEL, N_INT), lambda i: (0, 0)),
                  pl.BlockSpec((tm, D_MODEL), lambda i: (i, 0))]
                 + [pl.BlockSpec((tm, wd), lambda i: (i, 0)) for wd in widths],
        out_specs=[pl.BlockSpec((tm, D_MODEL), lambda i: (i, 0)),
                   pl.BlockSpec((tm, N_INT), lambda i: (i, 0)),
                   pl.BlockSpec((1, D_MODEL), lambda i: (0, 0))],
        out_shape=[jax.ShapeDtypeStruct((T, D_MODEL), F32),
                   jax.ShapeDtypeStruct((T, N_INT), BF16),
                   jax.ShapeDtypeStruct((1, D_MODEL), F32)],
        compiler_params=_cparams(("arbitrary",)),
    )(x, g, w, dx_next, *pieces)


def _matmul_tn(a, b, name):
    T, M = a.shape
    N = b.shape[1]
    tm, tn = 512, 512

    def body(a_ref, b_ref, o_ref):
        @pl.when(pl.program_id(1) == 0)
        def _():
            o_ref[...] = jnp.zeros_like(o_ref)

        o_ref[...] += _dot_tn(a_ref[...], b_ref[...])

    return pl.pallas_call(
        body, name=name, grid=(N // tn, T // tm),
        in_specs=[pl.BlockSpec((tm, M), lambda j, t: (t, 0)),
                  pl.BlockSpec((tm, tn), lambda j, t: (t, j))],
        out_specs=pl.BlockSpec((M, tn), lambda j, t: (0, j)),
        out_shape=jax.ShapeDtypeStruct((M, N), F32),
        compiler_params=_cparams(("parallel", "arbitrary")),
    )(a, b)


def _roll_f32(x, shift):
    return pltpu.roll(x.astype(F32), shift, 1)


def _swa_head(h, q_ref, kp_ref, kc_ref, vp_ref, vc_ref, sink, first):
    p, e = h // 2, h % 2
    lane = lax.broadcasted_iota(jnp.int32, (1, LANES), 1) // HEAD
    q = q_ref[:, p * LANES:(p + 1) * LANES]
    k_prev, k_cur, v_prev, v_cur = kp_ref[...], kc_ref[...], vp_ref[...], vc_ref[...]
    if e != p:
        q = _roll_f32(q, HEAD).astype(BF16)
        v_prev = _roll_f32(v_prev, HEAD).astype(BF16)
        v_cur = _roll_f32(v_cur, HEAD).astype(BF16)
    qs = jnp.where(lane == p, q, 0) * 0.125
    v_prev = jnp.where(lane == e, v_prev, 0)
    v_cur = jnp.where(lane == e, v_cur, 0)
    row = lax.broadcasted_iota(jnp.int32, (BLOCK, BLOCK), 0)
    col = lax.broadcasted_iota(jnp.int32, (BLOCK, BLOCK), 1)
    ok_prev = jnp.logical_and(col > row, jnp.logical_not(first))
    ok_cur = col <= row
    s_prev = jnp.where(ok_prev, _dot_nt(qs, k_prev), NEG)
    s_cur = jnp.where(ok_cur, _dot_nt(qs, k_cur), NEG)
    m = jnp.maximum(jnp.maximum(jnp.max(s_prev, axis=1, keepdims=True),
                                jnp.max(s_cur, axis=1, keepdims=True)), sink)
    p_prev = jnp.exp(s_prev - m)
    p_cur = jnp.exp(s_cur - m)
    p_sink = jnp.exp(sink - m)
    inv = 1.0 / (jnp.sum(p_prev, axis=1, keepdims=True) + jnp.sum(p_cur, axis=1, keepdims=True) + p_sink)
    return (p, e, lane, qs, k_prev, k_cur, v_prev, v_cur, p_prev * inv, p_cur * inv, p_sink * inv)


def _swa_specs(T):
    nb = T // BLOCK
    qo, ko, vo = (_INT_OFF[n] // LANES for n in ("a_q", "a_k", "a_v"))
    prev = lambda i: jnp.maximum(i - 1, 0)
    return [pl.BlockSpec((BLOCK, 256), lambda i: (i, qo // 2)),
            pl.BlockSpec((BLOCK, LANES), lambda i: (prev(i), ko)),
            pl.BlockSpec((BLOCK, LANES), lambda i: (i, ko)),
            pl.BlockSpec((BLOCK, LANES), lambda i: (prev(i), vo)),
            pl.BlockSpec((BLOCK, LANES), lambda i: (i, vo)),
            pl.BlockSpec(memory_space=pltpu.SMEM)], nb


def _swa_fwd(hb, sinks):
    T = hb.shape[0]
    specs, nb = _swa_specs(T)

    def body(q_ref, kp_ref, kc_ref, vp_ref, vc_ref, s_ref, o_ref):
        first = pl.program_id(0) == 0
        for p in range(2):
            out = jnp.zeros((BLOCK, LANES), F32)
            for e in range(2):
                h = 2 * p + e
                (_, _, _, _, _, _, v_prev, v_cur, p_prev, p_cur, _) = _swa_head(
                    h, q_ref, kp_ref, kc_ref, vp_ref, vc_ref, s_ref[h], first)
                out += _dot(p_prev.astype(BF16), v_prev) + _dot(p_cur.astype(BF16), v_cur)
            o_ref[:, p * LANES:(p + 1) * LANES] = out

    return pl.pallas_call(
        body, name="swa_fwd", grid=(nb,), in_specs=specs,
        out_specs=pl.BlockSpec((BLOCK, 256), lambda i: (i, 0)),
        out_shape=jax.ShapeDtypeStruct((T, 256), F32),
        compiler_params=_cparams(("parallel",)),
    )(hb, hb, hb, hb, hb, sinks)


def _swa_bwd(hb, sinks, dy):
    T = hb.shape[0]
    specs, nb = _swa_specs(T)

    def body(q_ref, kp_ref, kc_ref, vp_ref, vc_ref, s_ref, dy_ref, dq_ref, dk_ref, dv_ref, ds_ref):
        i = pl.program_id(0)
        first = i == 0
        cur = pl.ds(pl.multiple_of(i * BLOCK, BLOCK), BLOCK)
        prv = pl.ds(pl.multiple_of(jnp.maximum(i - 1, 0) * BLOCK, BLOCK), BLOCK)

        @pl.when(first)
        def _():
            ds_ref[...] = jnp.zeros_like(ds_ref)

        dk_ref[cur, :] = jnp.zeros((BLOCK, LANES), F32)
        dv_ref[cur, :] = jnp.zeros((BLOCK, LANES), F32)
        lane_id = lax.broadcasted_iota(jnp.int32, (8, LANES), 1)
        for p in range(2):
            dq_pair = jnp.zeros((BLOCK, LANES), F32)
            for e in range(2):
                h = 2 * p + e
                (_, _, lane, qs, k_prev, k_cur, v_prev, v_cur, p_prev, p_cur, p_sink) = _swa_head(
                    h, q_ref, kp_ref, kc_ref, vp_ref, vc_ref, s_ref[h], first)
                do = jnp.where(lane == e, dy_ref[:, p * LANES:(p + 1) * LANES], 0.0)
                dob = do.astype(BF16)
                pb_prev, pb_cur = p_prev.astype(BF16), p_cur.astype(BF16)
                o = _dot(pb_prev, v_prev) + _dot(pb_cur, v_cur)
                delta = jnp.sum(do * o, axis=1, keepdims=True)
                ds_prev = (p_prev * (_dot_nt(dob, v_prev) - delta)).astype(BF16)
                ds_cur = (p_cur * (_dot_nt(dob, v_cur) - delta)).astype(BF16)
                dsink = -jnp.sum(p_sink * delta, axis=0, keepdims=True)
                ds_ref[...] += jnp.where(lane_id == h, dsink, 0.0)
                dq = (_dot(ds_prev, k_prev) + _dot(ds_cur, k_cur)) * 0.125
                dq = jnp.where(lane == p, dq, 0.0)
                dob_v = dob
                if e != p:
                    dq = pltpu.roll(dq, HEAD, 1)
                    dob_v = pltpu.roll(do, HEAD, 1).astype(BF16)
                dq_pair += dq
                dk_ref[prv, :] += _dot_tn(ds_prev, qs)
                dk_ref[cur, :] += _dot_tn(ds_cur, qs)
                dv_ref[prv, :] += _dot_tn(pb_prev, dob_v)
                dv_ref[cur, :] += _dot_tn(pb_cur, dob_v)
            dq_ref[:, p * LANES:(p + 1) * LANES] = dq_pair

    return pl.pallas_call(
        body, name="swa_bwd", grid=(nb,),
        in_specs=specs + [pl.BlockSpec((BLOCK, 256), lambda i: (i, 0))],
        out_specs=[pl.BlockSpec((BLOCK, 256), lambda i: (i, 0)),
                   pl.BlockSpec((T, LANES), lambda i: (0, 0)),
                   pl.BlockSpec((T, LANES), lambda i: (0, 0)),
                   pl.BlockSpec((8, LANES), lambda i: (0, 0))],
        out_shape=[jax.ShapeDtypeStruct((T, 256), F32),
                   jax.ShapeDtypeStruct((T, LANES), F32),
                   jax.ShapeDtypeStruct((T, LANES), F32),
                   jax.ShapeDtypeStruct((8, LANES), F32)],
        compiler_params=_cparams(("arbitrary",)),
    )(hb, hb, hb, hb, hb, sinks, dy)


def _rope_tables(pos_ref):
    lane = lax.broadcasted_iota(jnp.int32, (1, LANES), 1)
    active = jnp.logical_and(lane >= HEAD, lane < HEAD + 2 * ROPE_HALF)
    idx = ((lane - HEAD) % ROPE_HALF).astype(F32)
    freq = jnp.exp(idx * (-math.log(ROPE_THETA) / ROPE_HALF))
    ang = pos_ref[...].astype(F32) * freq
    cos, sin = jnp.cos(ang), jnp.sin(ang)
    c = jnp.where(active, cos, 1.0)
    s_up = jnp.where(jnp.logical_and(active, lane >= HEAD + ROPE_HALF), sin, 0.0)
    s_dn = jnp.where(jnp.logical_and(active, lane < HEAD + ROPE_HALF), -sin, 0.0)
    return c, s_up, s_dn


def _rope(x, tabs):
    c, s_up, s_dn = tabs
    return x * c + pltpu.roll(x, ROPE_HALF, 1) * s_up + pltpu.roll(x, LANES - ROPE_HALF, 1) * s_dn


def _rope_t(dy, tabs):
    c, s_up, s_dn = tabs
    return dy * c + pltpu.roll(dy * s_up, LANES - ROPE_HALF, 1) + pltpu.roll(dy * s_dn, ROPE_HALF, 1)


def _mla_lat_specs(tm):
    cq, ckv, ckr = ((_INT_OFF[n] - N_HB) for n in ("c_q", "c_kv", "c_kr"))
    return [pl.BlockSpec((tm, 256), lambda i: (i, cq // 256)),
            pl.BlockSpec((tm, LANES), lambda i: (i, ckv // LANES)),
            pl.BlockSpec((tm, LANES), lambda i: (i, ckr // LANES)),
            pl.BlockSpec((tm, 1), lambda i: (i, 0)),
            pl.BlockSpec((1, 256), lambda i: (0, 0)),
            pl.BlockSpec((1, LANES), lambda i: (0, 0)),
            pl.BlockSpec((256, 512), lambda i: (0, 0)),
            pl.BlockSpec((LANES, 768), lambda i: (0, 0))]


def _mla_prep_fwd(hf, pos, g_q, g_kv, w_uq, w_ukv):
    T = hf.shape[0]
    tm = 512

    def body(cq_ref, ckv_ref, ckr_ref, pos_ref, gq_ref, gkv_ref, wq_ref, wkv_ref, qm_ref, km_ref, vm_ref):
        tabs = _rope_tables(pos_ref)
        cq = cq_ref[...]
        q = _dot((cq * _rms(cq) * gq_ref[...]).astype(BF16), wq_ref[...])
        ckv = ckv_ref[...]
        kv = _dot((ckv * _rms(ckv) * gkv_ref[...]).astype(BF16), wkv_ref[...])
        kr = _rope(pltpu.roll(ckr_ref[...], HEAD, 1), tabs)
        for h in range(4):
            sl = slice(h * LANES, (h + 1) * LANES)
            qm_ref[:, sl] = (_rope(q[:, sl], tabs) * MLA_SCALE).astype(BF16)
            km_ref[:, sl] = (kv[:, sl] + kr).astype(BF16)
        vm_ref[...] = kv[:, 512:].astype(BF16)

    return pl.pallas_call(
        body, name="mla_prep_fwd", grid=(T // tm,), in_specs=_mla_lat_specs(tm),
        out_specs=[pl.BlockSpec((tm, 512), lambda i: (i, 0)),
                   pl.BlockSpec((tm, 512), lambda i: (i, 0)),
                   pl.BlockSpec((tm, 256), lambda i: (i, 0))],
        out_shape=[jax.ShapeDtypeStruct((T, 512), BF16),
                   jax.ShapeDtypeStruct((T, 512), BF16),
                   jax.ShapeDtypeStruct((T, 256), BF16)],
        compiler_params=_cparams(("parallel",)),
    )(hf, hf, hf, pos, g_q, g_kv, w_uq, w_ukv)


def _mla_prep_bwd(hf, pos, g_q, g_kv, w_uq, w_ukv, dqm, dkm, dvm):
    T = hf.shape[0]
    tm = 512

    def body(cq_ref, ckv_ref, ckr_ref, pos_ref, gq_ref, gkv_ref, wq_ref, wkv_ref, dq_ref, dk_ref, dv_ref,
             dc_ref, dwq_ref, dwkv_ref, dgq_ref, dgkv_ref):
        @pl.when(pl.program_id(0) == 0)
        def _():
            dwq_ref[...] = jnp.zeros_like(dwq_ref)
            dwkv_ref[...] = jnp.zeros_like(dwkv_ref)
            dgq_ref[...] = jnp.zeros_like(dgq_ref)
            dgkv_ref[...] = jnp.zeros_like(dgkv_ref)

        tabs = _rope_tables(pos_ref)
        lane = lax.broadcasted_iota(jnp.int32, (1, LANES), 1)
        dq = jnp.concatenate([_rope_t(dq_ref[:, h * LANES:(h + 1) * LANES] * MLA_SCALE, tabs)
                              for h in range(4)], axis=1).astype(BF16)
        cq = cq_ref[...]
        rq = _rms(cq)
        cqn = (cq * rq * gq_ref[...]).astype(BF16)
        dwq_ref[...] += _dot_tn(cqn, dq)
        dcq, dgrow = _rms_bwd(_dot_nt(dq, wq_ref[...]), cq * rq, rq, gq_ref[...])
        dgq_ref[...] += _colsum(dgrow)
        dc_ref[:, 0:256] = dcq

        dk = dk_ref[...]
        dkr = dk[:, 0:LANES] + dk[:, LANES:2 * LANES] + dk[:, 2 * LANES:3 * LANES] + dk[:, 3 * LANES:]
        dkr = pltpu.roll(_rope_t(dkr, tabs), HEAD, 1)
        dc_ref[:, 384:512] = jnp.where(lane < 2 * ROPE_HALF, dkr, 0.0)
        dkv = jnp.concatenate([dk.astype(BF16), dv_ref[...].astype(BF16)], axis=1)
        ckv = ckv_ref[...]
        rkv = _rms(ckv)
        ckvn = (ckv * rkv * gkv_ref[...]).astype(BF16)
        dwkv_ref[...] += _dot_tn(ckvn, dkv)
        dckv, dgrow = _rms_bwd(_dot_nt(dkv, wkv_ref[...]), ckv * rkv, rkv, gkv_ref[...])
        dgkv_ref[...] += _colsum(dgrow)
        dc_ref[:, 256:384] = dckv

    return pl.pallas_call(
        body, name="mla_prep_bwd", grid=(T // tm,),
        in_specs=_mla_lat_specs(tm) + [pl.BlockSpec((tm, 512), lambda i: (i, 0)),
                                       pl.BlockSpec((tm, 512), lambda i: (i, 0)),
                                       pl.BlockSpec((tm, 256), lambda i: (i, 0))],
        out_specs=[pl.BlockSpec((tm, 512), lambda i: (i, 0)),
                   pl.BlockSpec((256, 512), lambda i: (0, 0)),
                   pl.BlockSpec((LANES, 768), lambda i: (0, 0)),
                   pl.BlockSpec((1, 256), lambda i: (0, 0)),
                   pl.BlockSpec((1, LANES), lambda i: (0, 0))],
        out_shape=[jax.ShapeDtypeStruct((T, 512), F32),
                   jax.ShapeDtypeStruct((256, 512), F32),
                   jax.ShapeDtypeStruct((LANES, 768), F32),
                   jax.ShapeDtypeStruct((1, 256), F32),
                   jax.ShapeDtypeStruct((1, LANES), F32)],
        compiler_params=_cparams(("arbitrary",)),
    )(hf, hf, hf, pos, g_q, g_kv, w_uq, w_ukv, dqm, dkm, dvm)


def _causal_masks(bq, bk):
    row = lax.broadcasted_iota(jnp.int32, (bq, bk), 0)
    col = lax.broadcasted_iota(jnp.int32, (bq, bk), 1)
    return row, col


def _mla_fwd(qm, km, vm):
    T = qm.shape[0]
    bq = bk = ATT_BLK
    nq = T // bq

    def body(q_ref, k_ref, v_ref, o_ref, lse_ref, acc_ref, m_ref):
        qi = pl.program_id(1)
        lane = lax.broadcasted_iota(jnp.int32, (1, LANES), 1) // HEAD
        row, col = _causal_masks(bq, bk)
        causal = col <= row
        acc_ref[...] = jnp.zeros_like(acc_ref)
        m_ref[...] = jnp.full_like(m_ref, NEG)

        def step(kb, masked):
            rows = pl.ds(pl.multiple_of(kb * bk, bk), bk)
            v_pair = v_ref[rows, :]
            ss = [_dot_nt(q_ref[:, e * LANES:(e + 1) * LANES], k_ref[rows, e * LANES:(e + 1) * LANES])
                  for e in range(2)]
            ps, alphas = [], []
            for e in range(2):
                s = jnp.where(causal, ss[e], NEG) if masked else ss[e]
                m_prev = m_ref[e]
                m_new = jnp.maximum(m_prev, jnp.max(s, axis=1, keepdims=True))
                ps.append(jnp.exp(s - jnp.concatenate([m_new] * (bk // LANES), axis=1)).astype(BF16))
                alphas.append(jnp.exp(m_prev - m_new))
                m_ref[e] = m_new
            for e in range(2):
                v = jnp.where(lane == e, v_pair, 1)
                acc_ref[e] = alphas[e] * acc_ref[e] + _dot(ps[e], v)

        step(qi, True)

        def loop(kb, c):
            step(kb, False)
            return c

        lax.fori_loop(0, qi, loop, 0)
        out = jnp.zeros((bq, LANES), F32)
        lse = jnp.zeros((bq, LANES), F32)
        for e in range(2):
            acc = acc_ref[e]
            l = pltpu.roll(acc, HEAD, 1)
            out = jnp.where(lane == e, acc / l, out)
            lse = jnp.where(lane == e, m_ref[e] + jnp.log(l), lse)
        o_ref[...] = out
        lse_ref[...] = lse

    return pl.pallas_call(
        body, name="mla_fwd", grid=(2, nq),
        in_specs=[pl.BlockSpec((bq, 256), lambda j, i: (i, j)),
                  pl.BlockSpec((T, 256), lambda j, i: (0, j)),
                  pl.BlockSpec((T, LANES), lambda j, i: (0, j))],
        out_specs=[pl.BlockSpec((bq, LANES), lambda j, i: (i, j)),
                   pl.BlockSpec((bq, LANES), lambda j, i: (i, j))],
        out_shape=[jax.ShapeDtypeStruct((T, 256), F32), jax.ShapeDtypeStruct((T, 256), F32)],
        scratch_shapes=[pltpu.VMEM((2, bq, LANES), F32), pltpu.VMEM((2, bq, LANES), F32)],
        compiler_params=_cparams(("parallel", "arbitrary")),
    )(qm, km, vm)


def _mla_bwd(qm, km, vm, y, lse, dy):
    T = qm.shape[0]
    bq = bk = ATT_BLK
    nq = T // bq

    def body(q_ref, k_ref, v_ref, y_ref, lse_ref, dy_ref, dq_ref, dk_ref, dv_ref, dob_ref, st_ref):
        qi = pl.program_id(1)

        @pl.when(qi == 0)
        def _():
            dk_ref[...] = jnp.zeros_like(dk_ref)
            dv_ref[...] = jnp.zeros_like(dv_ref)

        lane = lax.broadcasted_iota(jnp.int32, (1, LANES), 1) // HEAD
        row, col = _causal_masks(bq, bk)
        causal = col <= row
        dq_ref[...] = jnp.zeros_like(dq_ref)
        lse = lse_ref[...]
        lse_other = pltpu.roll(lse, HEAD, 1)
        for e in range(2):
            do = jnp.where(lane == e, dy_ref[...], 0.0)
            dob_ref[e] = do.astype(BF16)
            st_ref[2 * e] = jnp.where(lane == e, lse, lse_other)
            st_ref[2 * e + 1] = jnp.broadcast_to(jnp.sum(do * y_ref[...], axis=1, keepdims=True), (bq, LANES))

        def step(kb, masked):
            rows = pl.ds(pl.multiple_of(kb * bk, bk), bk)
            v_pair = v_ref[rows, :]
            hss = [slice(e * LANES, (e + 1) * LANES) for e in range(2)]
            tile = lambda a: jnp.concatenate([a] * (bk // LANES), axis=1)
            ss = [_dot_nt(q_ref[:, hss[e]], k_ref[rows, hss[e]]) for e in range(2)]
            dps = [_dot_nt(dob_ref[e], jnp.where(lane == e, v_pair, 0)) for e in range(2)]
            ps, dss = [], []
            for e in range(2):
                s = jnp.where(causal, ss[e], NEG) if masked else ss[e]
                p = jnp.exp(s - tile(st_ref[2 * e]))
                dss.append((p * (dps[e] - tile(st_ref[2 * e + 1]))).astype(BF16))
                ps.append(p.astype(BF16))
            dv_ref[rows, :] += _dot_tn(ps[0], dob_ref[0]) + _dot_tn(ps[1], dob_ref[1])
            for e in range(2):
                dk_ref[rows, hss[e]] += _dot_tn(dss[e], q_ref[:, hss[e]])
            for e in range(2):
                dq_ref[:, hss[e]] += _dot(dss[e], k_ref[rows, hss[e]])

        step(qi, True)

        def loop(kb, c):
            step(kb, False)
            return c

        lax.fori_loop(0, qi, loop, 0)

    return pl.pallas_call(
        body, name="mla_bwd", grid=(2, nq),
        in_specs=[pl.BlockSpec((bq, 256), lambda j, i: (i, j)),
                  pl.BlockSpec((T, 256), lambda j, i: (0, j)),
                  pl.BlockSpec((T, LANES), lambda j, i: (0, j)),
                  pl.BlockSpec((bq, LANES), lambda j, i: (i, j)),
                  pl.BlockSpec((bq, LANES), lambda j, i: (i, j)),
                  pl.BlockSpec((bq, LANES), lambda j, i: (i, j))],
        out_specs=[pl.BlockSpec((bq, 256), lambda j, i: (i, j)),
                   pl.BlockSpec((T, 256), lambda j, i: (0, j)),
                   pl.BlockSpec((T, LANES), lambda j, i: (0, j))],
        out_shape=[jax.ShapeDtypeStruct((T, 512), F32),
                   jax.ShapeDtypeStruct((T, 512), F32),
                   jax.ShapeDtypeStruct((T, 256), F32)],
        scratch_shapes=[pltpu.VMEM((2, bq, LANES), BF16), pltpu.VMEM((4, bq, LANES), F32)],
        compiler_params=_cparams(("parallel", "arbitrary")),
    )(qm, km, vm, y, lse, dy)


def _suffix_ones(n):
    r = lax.broadcasted_iota(jnp.int32, (n, n), 0)
    c = lax.broadcasted_iota(jnp.int32, (n, n), 1)
    return (r >= c).astype(BF16)


def _prefix_ones(n):
    r = lax.broadcasted_iota(jnp.int32, (n, n), 0)
    c = lax.broadcasted_iota(jnp.int32, (n, n), 1)
    return (r <= c).astype(BF16)


def _tri_sum(x, u):
    hi, lo = _split(x)
    return _dot(hi, u) + _dot(lo, u)


def _sb_specs(T, bq):
    qo, ko, vo = (_INT_OFF[n] // LANES for n in ("d_q", "d_k", "d_v"))
    return [pl.BlockSpec((bq, LANES), lambda j, i: (i, qo + j)),
            pl.BlockSpec((T, LANES), lambda j, i: (0, ko + j)),
            pl.BlockSpec((T, LANES), lambda j, i: (0, vo + j))]


def _sb_fwd(hb):
    T = hb.shape[0]
    bq = bk = ATT_BLK
    nq = T // bq

    def body(q_ref, k_ref, v_ref, o_ref, tot_ref, qm_ref, car_ref):
        qi = pl.program_id(1)
        lane = lax.broadcasted_iota(jnp.int32, (1, LANES), 1) // HEAD
        row, col = _causal_masks(bq, bk)
        strict = col < row
        u = _suffix_ones(bk)
        o_ref[...] = jnp.zeros_like(o_ref)
        car_ref[...] = jnp.zeros_like(car_ref)
        for e in range(2):
            qm_ref[e] = jnp.where(lane == e, q_ref[...], 0) * 0.125

        def step(kb, masked):
            rows = pl.ds(pl.multiple_of(kb * bk, bk), bk)
            k_pair, v_pair = k_ref[rows, :], v_ref[rows, :]
            tile = lambda a: jnp.concatenate([a] * (bk // LANES), axis=1)
            zs = [_dot_nt(qm_ref[e], k_pair) for e in range(2)]
            splits = []
            for e in range(2):
                z = zs[e]
                lk = jnp.minimum(-z, 0.0) - jnp.log(1.0 + jnp.exp(-jnp.abs(z)))
                if masked:
                    lk = jnp.where(strict, lk, 0.0)
                splits.append(_split(lk))
            sufs = [_dot(hi, u) + _dot(lo, u) for hi, lo in splits]
            aas = []
            for e in range(2):
                a = jnp.exp(zs[e] + sufs[e] + tile(car_ref[e]))
                if masked:
                    a = jnp.where(strict, a, 0.0)
                aas.append(a.astype(BF16))
                car_ref[e] += jnp.broadcast_to(sufs[e][:, 0:1], (bq, LANES))
            o_ref[...] += (_dot(aas[0], jnp.where(lane == 0, v_pair, 0))
                           + _dot(aas[1], jnp.where(lane == 1, v_pair, 0)))

        step(qi, True)

        def loop(it, c):
            step(qi - 1 - it, False)
            return c

        lax.fori_loop(0, qi, loop, 0)
        tot_ref[...] = jnp.where(lane == 0, car_ref[0], car_ref[1])

    return pl.pallas_call(
        body, name="sb_fwd", grid=(2, nq), in_specs=_sb_specs(T, bq),
        out_specs=[pl.BlockSpec((bq, LANES), lambda j, i: (i, j)), pl.BlockSpec((bq, LANES), lambda j, i: (i, j))],
        out_shape=[jax.ShapeDtypeStruct((T, 256), F32), jax.ShapeDtypeStruct((T, 256), F32)],
        scratch_shapes=[pltpu.VMEM((2, bq, LANES), BF16), pltpu.VMEM((2, bq, LANES), F32)],
        compiler_params=_cparams(("parallel", "arbitrary")),
    )(hb, hb, hb)


def _sb_bwd(hb, tot, dy):
    T = hb.shape[0]
    bq = bk = ATT_BLK
    nq = T // bq

    def body(q_ref, k_ref, v_ref, tot_ref, dy_ref, dq_ref, dk_ref, dv_ref, qm_ref, dob_ref, dqa_ref, rem_ref, cg_ref):
        qi = pl.program_id(1)

        @pl.when(qi == 0)
        def _():
            dk_ref[...] = jnp.zeros_like(dk_ref)
            dv_ref[...] = jnp.zeros_like(dv_ref)

        lane = lax.broadcasted_iota(jnp.int32, (1, LANES), 1) // HEAD
        row, col = _causal_masks(bq, bk)
        strict = col < row
        u = _prefix_ones(bk)
        tot = tot_ref[...]
        tot_other = pltpu.roll(tot, HEAD, 1)
        dqa_ref[...] = jnp.zeros_like(dqa_ref)
        cg_ref[...] = jnp.zeros_like(cg_ref)
        for e in range(2):
            qm_ref[e] = jnp.where(lane == e, q_ref[...], 0) * 0.125
            dob_ref[e] = jnp.where(lane == e, dy_ref[...], 0.0).astype(BF16)
            rem_ref[e] = jnp.where(lane == e, tot, tot_other)

        def step(kb, masked):
            rows = pl.ds(pl.multiple_of(kb * bk, bk), bk)
            k_pair, v_pair = k_ref[rows, :], v_ref[rows, :]
            tile = lambda a: jnp.concatenate([a] * (bk // LANES), axis=1)
            zs = [_dot_nt(qm_ref[e], k_pair) for e in range(2)]
            das = [_dot_nt(dob_ref[e], jnp.where(lane == e, v_pair, 0)) for e in range(2)]
            zls, splits = [], []
            for e in range(2):
                z = zs[e]
                lk = jnp.minimum(-z, 0.0) - jnp.log(1.0 + jnp.exp(-jnp.abs(z)))
                if masked:
                    lk = jnp.where(strict, lk, 0.0)
                zls.append(z + lk)
                splits.append(_split(lk))
            pres = [_dot(hi, u) + _dot(lo, u) for hi, lo in splits]
            aas, gs, gsplits = [], [], []
            for e in range(2):
                a = jnp.exp(zls[e] + (tile(rem_ref[e]) - pres[e]))
                if masked:
                    a = jnp.where(strict, a, 0.0)
                g = a * das[e]
                aas.append(a.astype(BF16))
                gs.append(g)
                gsplits.append(_split(g))
                rem_ref[e] -= jnp.broadcast_to(pres[e][:, bk - 1:bk], (bq, LANES))
            dv_ref[rows, :] += _dot_tn(aas[0], dob_ref[0]) + _dot_tn(aas[1], dob_ref[1])
            gpres = [_dot(hi, u) + _dot(lo, u) for hi, lo in gsplits]
            dzs = []
            for e in range(2):
                dz = gs[e] - jnp.exp(zls[e]) * (tile(cg_ref[e]) + gpres[e])
                if masked:
                    dz = jnp.where(strict, dz, 0.0)
                dzs.append(dz.astype(BF16))
                cg_ref[e] += jnp.broadcast_to(gpres[e][:, bk - 1:bk], (bq, LANES))
            dk_ref[rows, :] += _dot_tn(dzs[0], qm_ref[0]) + _dot_tn(dzs[1], qm_ref[1])
            for e in range(2):
                dqa_ref[e] += _dot(dzs[e], k_pair)

        def loop(kb, c):
            step(kb, False)
            return c

        lax.fori_loop(0, qi, loop, 0)
        step(qi, True)
        dq_ref[...] = jnp.where(lane == 0, dqa_ref[0], dqa_ref[1]) * 0.125

    return pl.pallas_call(
        body, name="sb_bwd", grid=(2, nq),
        in_specs=_sb_specs(T, bq) + [pl.BlockSpec((bq, LANES), lambda j, i: (i, j)),
                                     pl.BlockSpec((bq, LANES), lambda j, i: (i, j))],
        out_specs=[pl.BlockSpec((bq, LANES), lambda j, i: (i, j)),
                   pl.BlockSpec((T, LANES), lambda j, i: (0, j)),
                   pl.BlockSpec((T, LANES), lambda j, i: (0, j))],
        out_shape=[jax.ShapeDtypeStruct((T, 256), F32)] * 3,
        scratch_shapes=[pltpu.VMEM((2, bq, LANES), BF16), pltpu.VMEM((2, bq, LANES), BF16),
                        pltpu.VMEM((2, bq, LANES), F32), pltpu.VMEM((2, bq, LANES), F32),
                        pltpu.VMEM((2, bq, LANES), F32)],
        compiler_params=_cparams(("parallel", "arbitrary")),
    )(hb, hb, hb, tot, dy)


EP_TM = 256


def _ep_in_specs(tm, rev):
    idx = (lambda i: rev - i) if rev is not None else (lambda i: i)
    bo = (_INT_OFF["b_b"] - N_HB) // 256
    halo = lambda i: jnp.maximum(idx(i) * (tm // 8) - 1, 0)
    return [pl.BlockSpec((tm, 256), lambda i: (idx(i), 0)),
            pl.BlockSpec((tm, 256), lambda i: (idx(i), 0)),
            pl.BlockSpec((tm, 256), lambda i: (idx(i), 0)),
            pl.BlockSpec((tm, D_MODEL), lambda i: (idx(i), 0)),
            pl.BlockSpec((tm, 256), lambda i: (idx(i), bo)),
            pl.BlockSpec((tm, 256), lambda i: (idx(i), bo + 1)),
            pl.BlockSpec((tm, 256), lambda i: (idx(i), bo + 2)),
            pl.BlockSpec((8, 256), lambda i: (halo(i), bo + 1)),
            pl.BlockSpec((8, 256), lambda i: (halo(i), bo + 2)),
            pl.BlockSpec((3, 256), lambda i: (0, 0)),
            pl.BlockSpec((1, 256), lambda i: (0, 0)),
            pl.BlockSpec((1, D_MODEL), lambda i: (0, 0)),
            pl.BlockSpec((D_MODEL, D_MODEL), lambda i: (0, 0)),
            pl.BlockSpec((1, D_MODEL), lambda i: (0, 0))]


def _ep_mix(first, ya_ref, yc_ref, yd_ref, gate_ref, bb_ref, bc_ref, bx_ref, hc_ref, hx_ref, cw_ref, cb_ref, gg_ref):
    tm = ya_ref.shape[0]
    u = bc_ref[...] * bx_ref[...]
    halo = jnp.where(first, 0.0, hc_ref[...] * hx_ref[...])
    row = lax.broadcasted_iota(jnp.int32, (tm, 1), 0)
    u1 = jnp.where(row == 0, halo[7:8, :], pltpu.roll(u, 1, 0))
    u2 = jnp.where(row == 0, halo[6:7, :], jnp.where(row == 1, halo[7:8, :], pltpu.roll(u, 2, 0)))
    cw = cw_ref[...]
    conv = cw[0:1, :] * u2 + cw[1:2, :] * u1 + cw[2:3, :] * u + cb_ref[...]
    bb = bb_ref[...]
    ys = [ya_ref[...], bb * conv, yc_ref[...], yd_ref[...]]
    rs = [_rms(y) for y in ys]
    gg = gg_ref[...]
    yhat = jnp.concatenate([y * r for y, r in zip(ys, rs)], axis=1)
    gate = gate_ref[...]
    sig = 1.0 / (1.0 + jnp.exp(-gate))
    return u, u1, u2, conv, bb, rs, yhat, yhat * gg, gate, sig


def _epilogue_fwd(x, ya, yc, yd, hf, conv_w, conv_b, g_grp, w_out, g_post):
    T = x.shape[0]
    tm = EP_TM

    def body(x_ref, ya_ref, yc_ref, yd_ref, gate_ref, bb_ref, bc_ref, bx_ref, hc_ref, hx_ref, cw_ref, cb_ref,
             gg_ref, wo_ref, gp_ref, o_ref):
        (_, _, _, _, _, _, _, yn, gate, sig) = _ep_mix(
            pl.program_id(0) == 0, ya_ref, yc_ref, yd_ref, gate_ref, bb_ref, bc_ref, bx_ref, hc_ref, hx_ref,
            cw_ref, cb_ref, gg_ref)
        z = _dot((yn * (gate * sig)).astype(BF16), wo_ref[...])
        o_ref[...] = x_ref[...] + z * _rms(z) * gp_ref[...]

    return pl.pallas_call(
        body, name="epilogue_fwd", grid=(T // tm,),
        in_specs=[pl.BlockSpec((tm, D_MODEL), lambda i: (i, 0))] + _ep_in_specs(tm, None),
        out_specs=pl.BlockSpec((tm, D_MODEL), lambda i: (i, 0)),
        out_shape=jax.ShapeDtypeStruct((T, D_MODEL), F32),
        compiler_params=_cparams(("parallel",)),
    )(x, ya, yc, yd, hf, hf, hf, hf, hf, hf, conv_w, conv_b, g_grp, w_out, g_post)


def _epilogue_bwd(dxn, ya, yc, yd, hf, conv_w, conv_b, g_grp, w_out, g_post):
    T = dxn.shape[0]
    tm = EP_TM
    nt = T // tm
    ridx = lambda i: (nt - 1 - i, 0)

    def body(dx_ref, ya_ref, yc_ref, yd_ref, gate_ref, bb_ref, bc_ref, bx_ref, hc_ref, hx_ref, cw_ref, cb_ref,
             gg_ref, wo_ref, gp_ref,
             dya_ref, dyc_ref, dyd_ref, dhf_ref, dwo_ref, dgp_ref, dgg_ref, dcw_ref, dcb_ref, carry_ref):
        i = pl.program_id(0)

        @pl.when(i == 0)
        def _():
            for r in (dwo_ref, dgp_ref, dgg_ref, dcw_ref, dcb_ref, carry_ref):
                r[...] = jnp.zeros_like(r)

        (u, u1, u2, conv, bb, rs, yhat, yn, gate, sig) = _ep_mix(
            i == nt - 1, ya_ref, yc_ref, yd_ref, gate_ref, bb_ref, bc_ref, bx_ref, hc_ref, hx_ref,
            cw_ref, cb_ref, gg_ref)
        silu = gate * sig
        ymix = (yn * silu).astype(BF16)
        z = _dot(ymix, wo_ref[...])
        rz = _rms(z)
        dz, dgrow = _rms_bwd(dx_ref[...], z * rz, rz, gp_ref[...])
        dgp_ref[...] += _colsum(dgrow)
        dzb = dz.astype(BF16)
        dwo_ref[...] += _dot_tn(ymix, dzb)
        dymix = _dot_nt(dzb, wo_ref[...])
        dhf_ref[:, 0:D_MODEL] = dymix * yn * (sig * (1.0 + gate * (1.0 - sig)))
        dyn = dymix * silu
        dgg_ref[...] += _colsum(dyn * yhat)
        gg = gg_ref[...]
        dys = []
        for gi in range(4):
            sl = slice(gi * GROUP, (gi + 1) * GROUP)
            dyh = dyn[:, sl] * gg[:, sl]
            yh = yhat[:, sl]
            dys.append(rs[gi] * (dyh - yh * jnp.mean(dyh * yh, axis=-1, keepdims=True)))
        dya_ref[...] = dys[0]
        dyc_ref[...] = dys[2]
        dyd_ref[...] = dys[3]
        dyb = dys[1]
        dhf_ref[:, D_MODEL:D_MODEL + 256] = dyb * conv
        dconv = dyb * bb
        dcb_ref[...] += _colsum(dconv)
        dcw_ref[0:1, :] += _colsum(dconv * u2)
        dcw_ref[1:2, :] += _colsum(dconv * u1)
        dcw_ref[2:3, :] += _colsum(dconv * u)
        carry = carry_ref[...]
        row = lax.broadcasted_iota(jnp.int32, (tm, 1), 0)
        d1 = jnp.where(row == tm - 1, carry[0:1, :], pltpu.roll(dconv, tm - 1, 0))
        d2 = jnp.where(row == tm - 2, carry[0:1, :],
                       jnp.where(row == tm - 1, carry[1:2, :], pltpu.roll(dconv, tm - 2, 0)))
        cw = cw_ref[...]
        du = cw[2:3, :] * dconv + cw[1:2, :] * d1 + cw[0:1, :] * d2
        dhf_ref[:, D_MODEL + 256:D_MODEL + 512] = du * bx_ref[...]
        dhf_ref[:, D_MODEL + 512:D_MODEL + 768] = du * bc_ref[...]
        carry_ref[...] = dconv[0:8, :]

    in_specs = [pl.BlockSpec((tm, D_MODEL), ridx)] + _ep_in_specs(tm, nt - 1)
    return pl.pallas_call(
        body, name="epilogue_bwd", grid=(nt,), in_specs=in_specs,
        out_specs=[pl.BlockSpec((tm, 256), ridx), pl.BlockSpec((tm, 256), ridx), pl.BlockSpec((tm, 256), ridx),
                   pl.BlockSpec((tm, D_MODEL + 768), ridx),
                   pl.BlockSpec((D_MODEL, D_MODEL), lambda i: (0, 0)),
                   pl.BlockSpec((1, D_MODEL), lambda i: (0, 0)),
                   pl.BlockSpec((1, D_MODEL), lambda i: (0, 0)),
                   pl.BlockSpec((8, 256), lambda i: (0, 0)),
                   pl.BlockSpec((1, 256), lambda i: (0, 0))],
        out_shape=[jax.ShapeDtypeStruct((T, 256), F32)] * 3
                  + [jax.ShapeDtypeStruct((T, D_MODEL + 768), F32),
                     jax.ShapeDtypeStruct((D_MODEL, D_MODEL), F32),
                     jax.ShapeDtypeStruct((1, D_MODEL), F32),
                     jax.ShapeDtypeStruct((1, D_MODEL), F32),
                     jax.ShapeDtypeStruct((8, 256), F32),
                     jax.ShapeDtypeStruct((1, 256), F32)],
        scratch_shapes=[pltpu.VMEM((8, 256), F32)],
        compiler_params=_cparams(("arbitrary",)),
    )(dxn, ya, yc, yd, hf, hf, hf, hf, hf, hf, conv_w, conv_b, g_grp, w_out, g_post)


def _loss_head(y, tgt):
    T = y.shape[0]
    tm = 512

    def body(y_ref, t_ref, dy_ref, l_ref):
        @pl.when(pl.program_id(0) == 0)
        def _():
            l_ref[...] = jnp.zeros_like(l_ref)

        d = y_ref[...] - t_ref[...]
        dy_ref[...] = d * (1.0 / D_MODEL)
        part = jnp.sum(jnp.sum(d * d, axis=1, keepdims=True), axis=0, keepdims=True)
        l_ref[...] += part * (0.5 / D_MODEL)

    return pl.pallas_call(
        body, name="loss_head", grid=(T // tm,),
        in_specs=[pl.BlockSpec((tm, D_MODEL), lambda i: (i, 0))] * 2,
        out_specs=[pl.BlockSpec((tm, D_MODEL), lambda i: (i, 0)), pl.BlockSpec((8, LANES), lambda i: (0, 0))],
        out_shape=[jax.ShapeDtypeStruct((T, D_MODEL), F32), jax.ShapeDtypeStruct((8, LANES), F32)],
        compiler_params=_cparams(("arbitrary",)),
    )(y, tgt)


def _place():
    return lax.axis_index("x"), lax.axis_index("y"), lax.axis_index("c")


def _other_chips(x, y):
    return [(1 - x, y), (x, 1 - y), (1 - x, 1 - y)]


HBM = pl.BlockSpec(memory_space=pl.ANY)


def _gather_chips(big, small):
    def body(big_ref, small_ref, obig_ref, osmall_ref, send_sems, recv_sems, local_sems):
        x, y, c = _place()
        me = 2 * x + y
        pairs = [(big_ref, obig_ref), (small_ref, osmall_ref)]
        local = [pltpu.make_async_copy(s, o.at[me], local_sems.at[n]) for n, (s, o) in enumerate(pairs)]
        for cp in local:
            cp.start()
        sends = []
        for j, (px, py) in enumerate(_other_chips(x, y)):
            for n, (s, o) in enumerate(pairs):
                cp = pltpu.make_async_remote_copy(
                    src_ref=s, dst_ref=o.at[me], send_sem=send_sems.at[2 * j + n], recv_sem=recv_sems.at[2 * j + n],
                    device_id=(px, py, c), device_id_type=MESH)
                cp.start()
                sends.append(cp)
        for j, (px, py) in enumerate(_other_chips(x, y)):
            for n, (s, o) in enumerate(pairs):
                pltpu.make_async_remote_copy(
                    src_ref=s, dst_ref=o.at[2 * px + py], send_sem=send_sems.at[2 * j + n],
                    recv_sem=recv_sems.at[2 * j + n], device_id=(px, py, c), device_id_type=MESH).wait_recv()
        for cp in sends:
            cp.wait_send()
        for cp in local:
            cp.wait()

    return pl.pallas_call(
        body, name="gather_chips",
        in_specs=[HBM, HBM], out_specs=[HBM, HBM],
        out_shape=[jax.ShapeDtypeStruct((4,) + big.shape, big.dtype),
                   jax.ShapeDtypeStruct((4,) + small.shape, small.dtype)],
        scratch_shapes=[pltpu.SemaphoreType.DMA((6,)), pltpu.SemaphoreType.DMA((6,)), pltpu.SemaphoreType.DMA((2,))],
    )(big, small)


def _scatter_chips(chunks, small):
    def body(ch_ref, sm_ref, och_ref, osm_ref, send_sems, recv_sems, ssend_sems, srecv_sems, local_sems):
        x, y, c = _place()
        me = 2 * x + y
        dev = 4 * x + 2 * y + c
        local = [pltpu.make_async_copy(ch_ref.at[me], och_ref.at[me], local_sems.at[0]),
                 pltpu.make_async_copy(sm_ref, osm_ref.at[dev], local_sems.at[1])]
        for cp in local:
            cp.start()
        sends = []
        for j, (px, py) in enumerate(_other_chips(x, y)):
            cp = pltpu.make_async_remote_copy(
                src_ref=ch_ref.at[2 * px + py], dst_ref=och_ref.at[me], send_sem=send_sems.at[j],
                recv_sem=recv_sems.at[j], device_id=(px, py, c), device_id_type=MESH)
            cp.start()
            sends.append(cp)
        flips = [(fx, fy, fc) for fx in (0, 1) for fy in (0, 1) for fc in (0, 1)][1:]
        for j, (fx, fy, fc) in enumerate(flips):
            cp = pltpu.make_async_remote_copy(
                src_ref=sm_ref, dst_ref=osm_ref.at[dev], send_sem=ssend_sems.at[j], recv_sem=srecv_sems.at[j],
                device_id=(x ^ fx, y ^ fy, c ^ fc), device_id_type=MESH)
            cp.start()
            sends.append(cp)
        for j, (px, py) in enumerate(_other_chips(x, y)):
            pltpu.make_async_remote_copy(
                src_ref=ch_ref.at[me], dst_ref=och_ref.at[2 * px + py], send_sem=send_sems.at[j],
                recv_sem=recv_sems.at[j], device_id=(px, py, c), device_id_type=MESH).wait_recv()
        for j, (fx, fy, fc) in enumerate(flips):
            src = 4 * (x ^ fx) + 2 * (y ^ fy) + (c ^ fc)
            pltpu.make_async_remote_copy(
                src_ref=sm_ref, dst_ref=osm_ref.at[src], send_sem=ssend_sems.at[j], recv_sem=srecv_sems.at[j],
                device_id=(x ^ fx, y ^ fy, c ^ fc), device_id_type=MESH).wait_recv()
        for cp in sends:
            cp.wait_send()
        for cp in local:
            cp.wait()

    return pl.pallas_call(
        body, name="scatter_chips",
        in_specs=[HBM, HBM], out_specs=[HBM, HBM],
        out_shape=[jax.ShapeDtypeStruct(chunks.shape, chunks.dtype),
                   jax.ShapeDtypeStruct((8,) + small.shape, small.dtype)],
        scratch_shapes=[pltpu.SemaphoreType.DMA((3,)), pltpu.SemaphoreType.DMA((3,)),
                        pltpu.SemaphoreType.DMA((7,)), pltpu.SemaphoreType.DMA((7,)),
                        pltpu.SemaphoreType.DMA((2,))],
    )(chunks, small)


def _swap_cores(part):
    def body(p_ref, o_ref, send_sem, recv_sem):
        x, y, c = _place()
        cp = pltpu.make_async_remote_copy(src_ref=p_ref, dst_ref=o_ref, send_sem=send_sem, recv_sem=recv_sem,
                                          device_id=(x, y, 1 - c), device_id_type=MESH)
        cp.start()
        cp.wait()

    return pl.pallas_call(
        body, name="swap_cores", in_specs=[HBM], out_specs=HBM,
        out_shape=jax.ShapeDtypeStruct(part.shape, part.dtype),
        scratch_shapes=[pltpu.SemaphoreType.DMA, pltpu.SemaphoreType.DMA],
    )(part)


def _sum_leading(buf, name):
    n, R, C = buf.shape
    tr = R
    for cand in (512, 256, 128, 64, 32, 16, 8):
        if R % cand == 0:
            tr = cand
            break

    def body(b_ref, o_ref):
        acc = b_ref[0]
        for k in range(1, n):
            acc = acc + b_ref[k]
        o_ref[...] = acc

    return pl.pallas_call(
        body, name=name, grid=(R // tr,),
        in_specs=[pl.BlockSpec((n, tr, C), lambda i: (0, i, 0))],
        out_specs=pl.BlockSpec((tr, C), lambda i: (i, 0)),
        out_shape=jax.ShapeDtypeStruct((R, C), F32),
        compiler_params=_cparams(("parallel",)),
    )(buf)


def _adamw(w, grads, m, v, name):
    R, C = w.shape
    tr = R
    for cand in (256, 128, 64, 32, 16, 8):
        if R % cand == 0:
            tr = cand
            break
    ng = len(grads)
    c1 = 1.0 / (1.0 - ADAM_B1 ** ADAM_STEP)
    c2 = 1.0 / (1.0 - ADAM_B2 ** ADAM_STEP)

    def body(w_ref, *rest):
        g_refs = rest[:ng]
        m_ref, v_ref, go_ref, d_ref, mo_ref, vo_ref = rest[ng:]
        g = g_refs[0][...]
        for r in g_refs[1:]:
            g = g + r[...]
        go_ref[...] = g
        mn = ADAM_B1 * m_ref[...] + (1.0 - ADAM_B1) * g
        vn = ADAM_B2 * v_ref[...] + (1.0 - ADAM_B2) * (g * g)
        mo_ref[...] = mn
        vo_ref[...] = vn
        d_ref[...] = -ADAM_LR * ((mn * c1) / (jnp.sqrt(vn * c2) + ADAM_EPS) + ADAM_WD * w_ref[...])

    spec = pl.BlockSpec((tr, C), lambda i: (i, 0))
    return pl.pallas_call(
        body, name=name, grid=(R // tr,),
        in_specs=[spec] * (3 + ng), out_specs=[spec] * 4,
        out_shape=[jax.ShapeDtypeStruct((R, C), F32)] * 4,
        compiler_params=_cparams(("parallel",)),
    )(w, *grads, m, v)


PACK_C = 1024
PACK_MULT = 64
_BIG = ("w_in", "w_out", "mla_w_uq", "mla_w_ukv", "conv_w")


def _pack_rows(parts, mult):
    flat = jnp.concatenate([p.reshape(-1) for p in parts])
    n = flat.shape[0]
    rows = -(-n // PACK_C)
    rows = -(-rows // mult) * mult
    return jnp.pad(flat, (0, rows * PACK_C - n)).reshape(rows, PACK_C)


def _unpack_rows(buf, shapes):
    flat = buf.reshape(-1)
    out, o = [], 0
    for s in shapes:
        n = math.prod(s)
        out.append(flat[o:o + n].reshape(s))
        o += n
    return out


_SMALL = ("norm_pre", "group_norm", "norm_post", "conv_b", "mla_q_norm", "mla_kv_norm", "attn_sinks")


def _pack_small(d):
    return _pack_rows([d[n] for n in _SMALL], 8)


def _unpack_small(buf, shapes):
    return dict(zip(_SMALL, _unpack_rows(buf, [shapes[n] for n in _SMALL])))


def _w_in_internal(w):
    cols = []
    for n in _INT_ORDER:
        o, wd = _REAL_OFF[n]
        cols.append(w[:, o:o + wd])
        if _INT_W[n] != wd:
            cols.append(jnp.zeros((w.shape[0], _INT_W[n] - wd), w.dtype))
    return jnp.concatenate(cols, axis=1)


def _w_in_real(dw):
    return jnp.concatenate([dw[:, _INT_OFF[n]:_INT_OFF[n] + wd] for n, wd in _REAL], axis=1)


def _uq_internal(w):
    return jnp.pad(w.reshape(256, 4, 96), ((0, 0), (0, 0), (0, 32))).reshape(256, 512)


def _uq_real(dw):
    return dw.reshape(256, 4, 128)[:, :, :96].reshape(256, 384)


def _ukv_internal(w):
    w4 = w.reshape(128, 4, 128)
    k = jnp.pad(w4[:, :, :64], ((0, 0), (0, 0), (0, 64))).reshape(128, 512)
    return jnp.concatenate([k, w4[:, :, 64:].reshape(128, 256)], axis=1)


def _ukv_real(dw):
    k = dw[:, :512].reshape(128, 4, 128)[:, :, :64]
    v = dw[:, 512:].reshape(128, 4, 64)
    return jnp.concatenate([k, v], axis=2).reshape(128, 512)


def _layer_fwd(x, pos, p):
    xn, hb, hf = _inproj_fwd(x, p["norm_pre"], p["w_in"])
    ya = _swa_fwd(hb, p["attn_sinks"])
    qm, km, vm = _mla_prep_fwd(hf, pos, p["mla_q_norm"], p["mla_kv_norm"], p["mla_w_uq"], p["mla_w_ukv"])
    yc, lse = _mla_fwd(qm, km, vm)
    yd, tot = _sb_fwd(hb)
    x_next = _epilogue_fwd(x, ya, yc, yd, hf, p["conv_w"], p["conv_b"], p["group_norm"], p["w_out"], p["norm_post"])
    return x_next, dict(x=x, xn=xn, hb=hb, hf=hf, ya=ya, yc=yc, yd=yd, tot=tot, qm=qm, km=km, vm=vm, lse=lse)


def _layer_bwd(dx_next, pos, p, s):
    (dya, dyc, dyd, dhf, dw_out, dg_post, dg_grp, dconv_w, dconv_b) = _epilogue_bwd(
        dx_next, s["ya"], s["yc"], s["yd"], s["hf"], p["conv_w"], p["conv_b"], p["group_norm"], p["w_out"],
        p["norm_post"])
    dq_d, dk_d, dv_d = _sb_bwd(s["hb"], s["tot"], dyd)
    dqm, dkm, dvm = _mla_bwd(s["qm"], s["km"], s["vm"], s["yc"], s["lse"], dyc)
    dc, dw_uq, dw_ukv, dg_q, dg_kv = _mla_prep_bwd(
        s["hf"], pos, p["mla_q_norm"], p["mla_kv_norm"], p["mla_w_uq"], p["mla_w_ukv"], dqm, dkm, dvm)
    dq_a, dk_a, dv_a, dsinks = _swa_bwd(s["hb"], p["attn_sinks"], dya)
    dx, dh, dg_pre = _inproj_bwd_dx(s["x"], p["norm_pre"], p["w_in"], dx_next,
                                    [dq_a, dk_a, dv_a, dq_d, dk_d, dv_d, dhf, dc])
    dw_in = _matmul_tn(s["xn"], dh, "inproj_bwd_dw")
    grads = dict(norm_pre=dg_pre[0], w_in=_w_in_real(dw_in), attn_sinks=dsinks[0, :4], conv_w=dconv_w[:3],
                 conv_b=dconv_b[0], mla_q_norm=dg_q[0], mla_w_uq=_uq_real(dw_uq), mla_kv_norm=dg_kv[0],
                 mla_w_ukv=_ukv_real(dw_ukv), group_norm=dg_grp[0], w_out=dw_out, norm_post=dg_post[0])
    return dx, grads


_WEIGHTS = ["norm_pre", "w_in", "attn_sinks", "conv_w", "conv_b", "mla_q_norm", "mla_w_uq", "mla_kv_norm",
            "mla_w_ukv", "group_norm", "w_out", "norm_post"]


def kernel(x, positions, norm_pre, w_in, attn_sinks, conv_w, conv_b, mla_q_norm, mla_w_uq, mla_kv_norm, mla_w_ukv, group_norm, w_out, norm_post, loss_target, m_norm_pre, m_w_in, m_attn_sinks, m_conv_w, m_conv_b, m_mla_q_norm, m_mla_w_uq, m_mla_kv_norm, m_mla_w_ukv, m_group_norm, m_w_out, m_norm_post, v_norm_pre, v_w_in, v_attn_sinks, v_conv_w, v_conv_b, v_mla_q_norm, v_mla_w_uq, v_mla_kv_norm, v_mla_w_ukv, v_group_norm, v_w_out, v_norm_post):
    w = dict(norm_pre=norm_pre, w_in=w_in, attn_sinks=attn_sinks, conv_w=conv_w, conv_b=conv_b,
             mla_q_norm=mla_q_norm, mla_w_uq=mla_w_uq, mla_kv_norm=mla_kv_norm, mla_w_ukv=mla_w_ukv,
             group_norm=group_norm, w_out=w_out, norm_post=norm_post)
    m = dict(norm_pre=m_norm_pre, w_in=m_w_in, attn_sinks=m_attn_sinks, conv_w=m_conv_w, conv_b=m_conv_b,
             mla_q_norm=m_mla_q_norm, mla_w_uq=m_mla_w_uq, mla_kv_norm=m_mla_kv_norm, mla_w_ukv=m_mla_w_ukv,
             group_norm=m_group_norm, w_out=m_w_out, norm_post=m_norm_post)
    v = dict(norm_pre=v_norm_pre, w_in=v_w_in, attn_sinks=v_attn_sinks, conv_w=v_conv_w, conv_b=v_conv_b,
             mla_q_norm=v_mla_q_norm, mla_w_uq=v_mla_w_uq, mla_kv_norm=v_mla_kv_norm, mla_w_ukv=v_mla_w_ukv,
             group_norm=v_group_norm, w_out=v_w_out, norm_post=v_norm_post)
    T = x.shape[1]
    xs = x[0]
    pos = positions[0].reshape(T, 1)
    tgt = loss_target[0]
    big_shapes = [w[n].shape for n in _BIG]
    mm = _BIG[:4]

    wb = _pack_rows([w[n].astype(BF16) for n in mm], 16)
    wc = _pack_rows([w["conv_w"]], 8)
    gb, gc = _gather_chips(wb, wc)
    per_chip = [_unpack_rows(gb[k], big_shapes[:4]) + _unpack_rows(gc[k], big_shapes[4:]) for k in range(4)]
    full = {}
    for n_i, n in enumerate(_BIG):
        axis = 1 if n == "w_out" else 2
        full[n] = jnp.concatenate([per_chip[k][n_i] for k in range(4)], axis=axis)

    layers = []
    for l in range(DEPTH):
        layers.append(dict(
            norm_pre=norm_pre[l:l + 1], w_in=_w_in_internal(full["w_in"][l]), attn_sinks=attn_sinks[l],
            conv_w=full["conv_w"][l], conv_b=conv_b[l:l + 1], mla_q_norm=mla_q_norm[l:l + 1],
            mla_w_uq=_uq_internal(full["mla_w_uq"][l]), mla_kv_norm=mla_kv_norm[l:l + 1],
            mla_w_ukv=_ukv_internal(full["mla_w_ukv"][l]), group_norm=group_norm[l:l + 1],
            w_out=full["w_out"][l], norm_post=norm_post[l:l + 1]))

    saved = []
    h = xs
    for l in range(DEPTH):
        h, s = _layer_fwd(h, pos, layers[l])
        saved.append(s)
    dy, loss_part = _loss_head(h, tgt)
    loss = lax.psum(loss_part[0, 0], ("x", "y", "c"))

    grads = [None] * DEPTH
    for l in reversed(range(DEPTH)):
        dy, grads[l] = _layer_bwd(dy, pos, layers[l], saved[l])
    g = {n: jnp.stack([grads[l][n] for l in range(DEPTH)]) for n in _WEIGHTS}

    def shard(n, k):
        a = g[n]
        if n == "w_out":
            return a[:, 256 * k:256 * (k + 1), :]
        wd = a.shape[2] // 4
        return a[:, :, wd * k:wd * (k + 1)]

    chunks = jnp.stack([_pack_rows([shard(n, k) for n in _BIG], PACK_MULT) for k in range(4)])
    small = _pack_small(g)
    got, got_small = _scatter_chips(chunks, small)
    part = _sum_leading(got, "sum_chips")
    other = _swap_cores(part)
    g_small = _sum_leading(got_small, "sum_small")

    pack_big = lambda d: _pack_rows([d[n] for n in _BIG], PACK_MULT)
    gb_, db_, mb_, vb_ = _adamw(pack_big(w), [part, other], pack_big(m), pack_big(v), "adamw_big")
    gs_, ds_, ms_, vs_ = _adamw(_pack_small(w), [g_small], _pack_small(m), _pack_small(v), "adamw_small")
    small_shapes = {n: w[n].shape for n in _SMALL}
    outs = []
    for big, sm in ((gb_, gs_), (db_, ds_), (mb_, ms_), (vb_, vs_)):
        d = dict(zip(_BIG, _unpack_rows(big, big_shapes)))
        d.update(_unpack_small(sm, small_shapes))
        outs.append(d)
    return (loss, dy[None], *[outs[0][n] for n in _WEIGHTS], *[outs[1][n] for n in _WEIGHTS],
            *[outs[2][n] for n in _WEIGHTS], *[outs[3][n] for n in _WEIGHTS])
```

```python
import functools
import math

import jax
import jax.numpy as jnp
from jax import lax
from jax.experimental import pallas as pl
from jax.experimental.pallas import tpu as pltpu

F32 = jnp.float32
BF16 = jnp.bfloat16
MESH = pl.DeviceIdType.MESH

D_MODEL = 1024
DEPTH = 2
EPS = 1e-6
BLOCK = 128
HEAD = 64
LANES = 128
GROUP = 256
MLA_SCALE = 96 ** -0.5
ROPE_HALF = 16
ROPE_THETA = 10000.0
ATT_BLK = 256
NEG = -1e30
SB_DEAD = -104.0

ADAM_LR, ADAM_B1, ADAM_B2, ADAM_EPS, ADAM_WD, ADAM_STEP = 0.001, 0.9, 0.999, 1e-08, 0.01, 10

_REAL = [("a_q", 256), ("a_k", 128), ("a_v", 128), ("b_b", 256), ("b_c", 256), ("b_x", 256),
         ("c_q", 256), ("c_kv", 128), ("c_kr", 32), ("d_q", 256), ("d_k", 256), ("d_v", 256),
         ("gate", 1024)]
_REAL_OFF = {}
_o = 0
for _n, _w in _REAL:
    _REAL_OFF[_n] = (_o, _w)
    _o += _w
D_IN = _o
_INT_ORDER = ["a_q", "a_k", "a_v", "d_q", "d_k", "d_v", "gate", "b_b", "b_c", "b_x", "c_q", "c_kv", "c_kr"]
_INT_W = dict(_REAL)
_INT_W["c_kr"] = 128
_INT_OFF = {}
_o = 0
for _n in _INT_ORDER:
    _INT_OFF[_n] = _o
    _o += _INT_W[_n]
N_INT = _o
N_HB = _INT_OFF["gate"]
N_HF = N_INT - N_HB

VMEM_LIMIT = 56 * 1024 * 1024


def _cparams(sem):
    return pltpu.CompilerParams(dimension_semantics=sem, vmem_limit_bytes=VMEM_LIMIT)


def _dot(a, b):
    return jnp.dot(a, b, preferred_element_type=F32)


def _dot_nt(a, b):
    return lax.dot_general(a, b, (((1,), (1,)), ((), ())), preferred_element_type=F32)


def _dot_tn(a, b):
    return lax.dot_general(a, b, (((0,), (0,)), ((), ())), preferred_element_type=F32)


def _split(x):
    hi = x.astype(BF16)
    lo = (x - hi.astype(F32)).astype(BF16)
    return hi, lo


def _rms(x):
    return lax.rsqrt(jnp.mean(x * x, axis=-1, keepdims=True) + EPS)


def _rms_bwd(dy, xhat, r, g):
    dxhat = dy * g
    return r * (dxhat - xhat * jnp.mean(dxhat * xhat, axis=-1, keepdims=True)), dy * xhat


def _colsum(x):
    return jnp.sum(x, axis=0, keepdims=True)


def _inproj_fwd(x, g, w):
    T = x.shape[0]
    tm = 256

    def body(x_ref, g_ref, w_ref, xn_ref, hb_ref, hf_ref):
        xv = x_ref[...]
        xn = (xv * _rms(xv) * g_ref[...]).astype(BF16)
        xn_ref[...] = xn
        h = _dot(xn, w_ref[...])
        hb_ref[...] = h[:, :N_HB].astype(BF16)
        hf_ref[...] = h[:, N_HB:]

    return pl.pallas_call(
        body, name="inproj_fwd", grid=(T // tm,),
        in_specs=[pl.BlockSpec((tm, D_MODEL), lambda i: (i, 0)),
                  pl.BlockSpec((1, D_MODEL), lambda i: (0, 0)),
                  pl.BlockSpec((D_MODEL, N_INT), lambda i: (0, 0))],
        out_specs=[pl.BlockSpec((tm, D_MODEL), lambda i: (i, 0)),
                   pl.BlockSpec((tm, N_HB), lambda i: (i, 0)),
                   pl.BlockSpec((tm, N_HF), lambda i: (i, 0))],
        out_shape=[jax.ShapeDtypeStruct((T, D_MODEL), BF16),
                   jax.ShapeDtypeStruct((T, N_HB), BF16),
                   jax.ShapeDtypeStruct((T, N_HF), F32)],
        compiler_params=_cparams(("parallel",)),
    )(x, g, w)


def _inproj_bwd_dx(x, g, w, dx_next, pieces):
    T = x.shape[0]
    tm = 256
    widths = [p.shape[1] for p in pieces]
    assert sum(widths) == N_INT

    def body(x_ref, g_ref, w_ref, dxn_ref, *rest):
        p_refs = rest[:len(pieces)]
        dx_ref, dh_ref, dg_ref = rest[len(pieces):]
        dh = jnp.concatenate([p[...].astype(BF16) for p in p_refs], axis=1)
        dh_ref[...] = dh
        dxn = _dot_nt(dh, w_ref[...])
        xv = x_ref[...]
        r = _rms(xv)
        dx, dgrow = _rms_bwd(dxn, xv * r, r, g_ref[...])
        dx_ref[...] = dx + dxn_ref[...]

        @pl.when(pl.program_id(0) == 0)
        def _():
            dg_ref[...] = jnp.zeros_like(dg_ref)

        dg_ref[...] += _colsum(dgrow)

    return pl.pallas_call(
        body, name="inproj_bwd_dx", grid=(T // tm,),
        in_specs=[pl.BlockSpec((tm, D_MODEL), lambda i: (i, 0)),
                  pl.BlockSpec((1, D_MODEL), lambda i: (0, 0)),
                  pl.BlockSpec((D_MODEL, N_INT), lambda i: (0, 0)),
                  pl.BlockSpec((tm, D_MODEL), lambda i: (i, 0))]
                 + [pl.BlockSpec((tm, wd), lambda i: (i, 0)) for wd in widths],
        out_specs=[pl.BlockSpec((tm, D_MODEL), lambda i: (i, 0)),
                   pl.BlockSpec((tm, N_INT), lambda i: (i, 0)),
                   pl.BlockSpec((1, D_MODEL), lambda i: (0, 0))],
        out_shape=[jax.ShapeDtypeStruct((T, D_MODEL), F32),
                   jax.ShapeDtypeStruct((T, N_INT), BF16),
                   jax.ShapeDtypeStruct((1, D_MODEL), F32)],
        compiler_params=_cparams(("arbitrary",)),
    )(x, g, w, dx_next, *pieces)


def _matmul_tn(a, b, name):
    T, M = a.shape
    N = b.shape[1]
    tm, tn = 512, 512

    def body(a_ref, b_ref, o_ref):
        @pl.when(pl.program_id(1) == 0)
        def _():
            o_ref[...] = jnp.zeros_like(o_ref)

        o_ref[...] += _dot_tn(a_ref[...], b_ref[...])

    return pl.pallas_call(
        body, name=name, grid=(N // tn, T // tm),
        in_specs=[pl.BlockSpec((tm, M), lambda j, t: (t, 0)),
                  pl.BlockSpec((tm, tn), lambda j, t: (t, j))],
        out_specs=pl.BlockSpec((M, tn), lambda j, t: (0, j)),
        out_shape=jax.ShapeDtypeStruct((M, N), F32),
        compiler_params=_cparams(("parallel", "arbitrary")),
    )(a, b)


def _roll_f32(x, shift):
    return pltpu.roll(x.astype(F32), shift, 1)


def _swa_head(h, q_ref, kp_ref, kc_ref, vp_ref, vc_ref, sink, first):
    p, e = h // 2, h % 2
    lane = lax.broadcasted_iota(jnp.int32, (1, LANES), 1) // HEAD
    q = q_ref[:, p * LANES:(p + 1) * LANES]
    k_prev, k_cur, v_prev, v_cur = kp_ref[...], kc_ref[...], vp_ref[...], vc_ref[...]
    if e != p:
        q = _roll_f32(q, HEAD).astype(BF16)
        v_prev = _roll_f32(v_prev, HEAD).astype(BF16)
        v_cur = _roll_f32(v_cur, HEAD).astype(BF16)
    qs = jnp.where(lane == p, q, 0) * 0.125
    v_prev = jnp.where(lane == e, v_prev, 0)
    v_cur = jnp.where(lane == e, v_cur, 0)
    row = lax.broadcasted_iota(jnp.int32, (BLOCK, BLOCK), 0)
    col = lax.broadcasted_iota(jnp.int32, (BLOCK, BLOCK), 1)
    ok_prev = jnp.logical_and(col > row, jnp.logical_not(first))
    ok_cur = col <= row
    s_prev = jnp.where(ok_prev, _dot_nt(qs, k_prev), NEG)
    s_cur = jnp.where(ok_cur, _dot_nt(qs, k_cur), NEG)
    m = jnp.maximum(jnp.maximum(jnp.max(s_prev, axis=1, keepdims=True),
                                jnp.max(s_cur, axis=1, keepdims=True)), sink)
    p_prev = jnp.exp(s_prev - m)
    p_cur = jnp.exp(s_cur - m)
    p_sink = jnp.exp(sink - m)
    inv = 1.0 / (jnp.sum(p_prev, axis=1, keepdims=True) + jnp.sum(p_cur, axis=1, keepdims=True) + p_sink)
    return (p, e, lane, qs, k_prev, k_cur, v_prev, v_cur, p_prev * inv, p_cur * inv, p_sink * inv)


def _swa_specs(T):
    nb = T // BLOCK
    qo, ko, vo = (_INT_OFF[n] // LANES for n in ("a_q", "a_k", "a_v"))
    prev = lambda i: jnp.maximum(i - 1, 0)
    return [pl.BlockSpec((BLOCK, 256), lambda i: (i, qo // 2)),
            pl.BlockSpec((BLOCK, LANES), lambda i: (prev(i), ko)),
            pl.BlockSpec((BLOCK, LANES), lambda i: (i, ko)),
            pl.BlockSpec((BLOCK, LANES), lambda i: (prev(i), vo)),
            pl.BlockSpec((BLOCK, LANES), lambda i: (i, vo)),
            pl.BlockSpec(memory_space=pltpu.SMEM)], nb


def _swa_fwd(hb, sinks):
    T = hb.shape[0]
    specs, nb = _swa_specs(T)

    def body(q_ref, kp_ref, kc_ref, vp_ref, vc_ref, s_ref, o_ref):
        first = pl.program_id(0) == 0
        for p in range(2):
            out = jnp.zeros((BLOCK, LANES), F32)
            for e in range(2):
                h = 2 * p + e
                (_, _, _, _, _, _, v_prev, v_cur, p_prev, p_cur, _) = _swa_head(
                    h, q_ref, kp_ref, kc_ref, vp_ref, vc_ref, s_ref[h], first)
                out += _dot(p_prev.astype(BF16), v_prev) + _dot(p_cur.astype(BF16), v_cur)
            o_ref[:, p * LANES:(p + 1) * LANES] = out

    return pl.pallas_call(
        body, name="swa_fwd", grid=(nb,), in_specs=specs,
        out_specs=pl.BlockSpec((BLOCK, 256), lambda i: (i, 0)),
        out_shape=jax.ShapeDtypeStruct((T, 256), F32),
        compiler_params=_cparams(("parallel",)),
    )(hb, hb, hb, hb, hb, sinks)


def _swa_bwd(hb, sinks, dy):
    T = hb.shape[0]
    specs, nb = _swa_specs(T)

    def body(q_ref, kp_ref, kc_ref, vp_ref, vc_ref, s_ref, dy_ref, dq_ref, dk_ref, dv_ref, ds_ref):
        i = pl.program_id(0)
        first = i == 0
        cur = pl.ds(pl.multiple_of(i * BLOCK, BLOCK), BLOCK)
        prv = pl.ds(pl.multiple_of(jnp.maximum(i - 1, 0) * BLOCK, BLOCK), BLOCK)

        @pl.when(first)
        def _():
            ds_ref[...] = jnp.zeros_like(ds_ref)

        dk_ref[cur, :] = jnp.zeros((BLOCK, LANES), F32)
        dv_ref[cur, :] = jnp.zeros((BLOCK, LANES), F32)
        lane_id = lax.broadcasted_iota(jnp.int32, (8, LANES), 1)
        for p in range(2):
            dq_pair = jnp.zeros((BLOCK, LANES), F32)
            for e in range(2):
                h = 2 * p + e
                (_, _, lane, qs, k_prev, k_cur, v_prev, v_cur, p_prev, p_cur, p_sink) = _swa_head(
                    h, q_ref, kp_ref, kc_ref, vp_ref, vc_ref, s_ref[h], first)
                do = jnp.where(lane == e, dy_ref[:, p * LANES:(p + 1) * LANES], 0.0)
                dob = do.astype(BF16)
                pb_prev, pb_cur = p_prev.astype(BF16), p_cur.astype(BF16)
                o = _dot(pb_prev, v_prev) + _dot(pb_cur, v_cur)
                delta = jnp.sum(do * o, axis=1, keepdims=True)
                ds_prev = (p_prev * (_dot_nt(dob, v_prev) - delta)).astype(BF16)
                ds_cur = (p_cur * (_dot_nt(dob, v_cur) - delta)).astype(BF16)
                dsink = -jnp.sum(p_sink * delta, axis=0, keepdims=True)
                ds_ref[...] += jnp.where(lane_id == h, dsink, 0.0)
                dq = (_dot(ds_prev, k_prev) + _dot(ds_cur, k_cur)) * 0.125
                dq = jnp.where(lane == p, dq, 0.0)
                dob_v = dob
                if e != p:
                    dq = pltpu.roll(dq, HEAD, 1)
                    dob_v = pltpu.roll(do, HEAD, 1).astype(BF16)
                dq_pair += dq
                dk_ref[prv, :] += _dot_tn(ds_prev, qs)
                dk_ref[cur, :] += _dot_tn(ds_cur, qs)
                dv_ref[prv, :] += _dot_tn(pb_prev, dob_v)
                dv_ref[cur, :] += _dot_tn(pb_cur, dob_v)
            dq_ref[:, p * LANES:(p + 1) * LANES] = dq_pair

    return pl.pallas_call(
        body, name="swa_bwd", grid=(nb,),
        in_specs=specs + [pl.BlockSpec((BLOCK, 256), lambda i: (i, 0))],
        out_specs=[pl.BlockSpec((BLOCK, 256), lambda i: (i, 0)),
                   pl.BlockSpec((T, LANES), lambda i: (0, 0)),
                   pl.BlockSpec((T, LANES), lambda i: (0, 0)),
                   pl.BlockSpec((8, LANES), lambda i: (0, 0))],
        out_shape=[jax.ShapeDtypeStruct((T, 256), F32),
                   jax.ShapeDtypeStruct((T, LANES), F32),
                   jax.ShapeDtypeStruct((T, LANES), F32),
                   jax.ShapeDtypeStruct((8, LANES), F32)],
        compiler_params=_cparams(("arbitrary",)),
    )(hb, hb, hb, hb, hb, sinks, dy)


def _rope_tables(pos_ref):
    lane = lax.broadcasted_iota(jnp.int32, (1, LANES), 1)
    active = jnp.logical_and(lane >= HEAD, lane < HEAD + 2 * ROPE_HALF)
    idx = ((lane - HEAD) % ROPE_HALF).astype(F32)
    freq = jnp.exp(idx * (-math.log(ROPE_THETA) / ROPE_HALF))
    ang = pos_ref[...].astype(F32) * freq
    cos, sin = jnp.cos(ang), jnp.sin(ang)
    c = jnp.where(active, cos, 1.0)
    s_up = jnp.where(jnp.logical_and(active, lane >= HEAD + ROPE_HALF), sin, 0.0)
    s_dn = jnp.where(jnp.logical_and(active, lane < HEAD + ROPE_HALF), -sin, 0.0)
    return c, s_up, s_dn


def _rope(x, tabs):
    c, s_up, s_dn = tabs
    return x * c + pltpu.roll(x, ROPE_HALF, 1) * s_up + pltpu.roll(x, LANES - ROPE_HALF, 1) * s_dn


def _rope_t(dy, tabs):
    c, s_up, s_dn = tabs
    return dy * c + pltpu.roll(dy * s_up, LANES - ROPE_HALF, 1) + pltpu.roll(dy * s_dn, ROPE_HALF, 1)


def _mla_lat_specs(tm):
    cq, ckv, ckr = ((_INT_OFF[n] - N_HB) for n in ("c_q", "c_kv", "c_kr"))
    return [pl.BlockSpec((tm, 256), lambda i: (i, cq // 256)),
            pl.BlockSpec((tm, LANES), lambda i: (i, ckv // LANES)),
            pl.BlockSpec((tm, LANES), lambda i: (i, ckr // LANES)),
            pl.BlockSpec((tm, 1), lambda i: (i, 0)),
            pl.BlockSpec((1, 256), lambda i: (0, 0)),
            pl.BlockSpec((1, LANES), lambda i: (0, 0)),
            pl.BlockSpec((256, 512), lambda i: (0, 0)),
            pl.BlockSpec((LANES, 768), lambda i: (0, 0))]


def _mla_prep_fwd(hf, pos, g_q, g_kv, w_uq, w_ukv):
    T = hf.shape[0]
    tm = 512

    def body(cq_ref, ckv_ref, ckr_ref, pos_ref, gq_ref, gkv_ref, wq_ref, wkv_ref, qm_ref, km_ref, vm_ref):
        tabs = _rope_tables(pos_ref)
        cq = cq_ref[...]
        q = _dot((cq * _rms(cq) * gq_ref[...]).astype(BF16), wq_ref[...])
        ckv = ckv_ref[...]
        kv = _dot((ckv * _rms(ckv) * gkv_ref[...]).astype(BF16), wkv_ref[...])
        kr = _rope(pltpu.roll(ckr_ref[...], HEAD, 1), tabs)
        for h in range(4):
            sl = slice(h * LANES, (h + 1) * LANES)
            qm_ref[:, sl] = (_rope(q[:, sl], tabs) * MLA_SCALE).astype(BF16)
            km_ref[:, sl] = (kv[:, sl] + kr).astype(BF16)
        vm_ref[...] = kv[:, 512:].astype(BF16)

    return pl.pallas_call(
        body, name="mla_prep_fwd", grid=(T // tm,), in_specs=_mla_lat_specs(tm),
        out_specs=[pl.BlockSpec((tm, 512), lambda i: (i, 0)),
                   pl.BlockSpec((tm, 512), lambda i: (i, 0)),
                   pl.BlockSpec((tm, 256), lambda i: (i, 0))],
        out_shape=[jax.ShapeDtypeStruct((T, 512), BF16),
                   jax.ShapeDtypeStruct((T, 512), BF16),
                   jax.ShapeDtypeStruct((T, 256), BF16)],
        compiler_params=_cparams(("parallel",)),
    )(hf, hf, hf, pos, g_q, g_kv, w_uq, w_ukv)


def _mla_prep_bwd(hf, pos, g_q, g_kv, w_uq, w_ukv, dqm, dkm, dvm):
    T = hf.shape[0]
    tm = 512

    def body(cq_ref, ckv_ref, ckr_ref, pos_ref, gq_ref, gkv_ref, wq_ref, wkv_ref, dq_ref, dk_ref, dv_ref,
             dc_ref, dwq_ref, dwkv_ref, dgq_ref, dgkv_ref):
        @pl.when(pl.program_id(0) == 0)
        def _():
            dwq_ref[...] = jnp.zeros_like(dwq_ref)
            dwkv_ref[...] = jnp.zeros_like(dwkv_ref)
            dgq_ref[...] = jnp.zeros_like(dgq_ref)
            dgkv_ref[...] = jnp.zeros_like(dgkv_ref)

        tabs = _rope_tables(pos_ref)
        lane = lax.broadcasted_iota(jnp.int32, (1, LANES), 1)
        dq = jnp.concatenate([_rope_t(dq_ref[:, h * LANES:(h + 1) * LANES] * MLA_SCALE, tabs)
                              for h in range(4)], axis=1).astype(BF16)
        cq = cq_ref[...]
        rq = _rms(cq)
        cqn = (cq * rq * gq_ref[...]).astype(BF16)
        dwq_ref[...] += _dot_tn(cqn, dq)
        dcq, dgrow = _rms_bwd(_dot_nt(dq, wq_ref[...]), cq * rq, rq, gq_ref[...])
        dgq_ref[...] += _colsum(dgrow)
        dc_ref[:, 0:256] = dcq

        dk = dk_ref[...]
        dkr = dk[:, 0:LANES] + dk[:, LANES:2 * LANES] + dk[:, 2 * LANES:3 * LANES] + dk[:, 3 * LANES:]
        dkr = pltpu.roll(_rope_t(dkr, tabs), HEAD, 1)
        dc_ref[:, 384:512] = jnp.where(lane < 2 * ROPE_HALF, dkr, 0.0)
        dkv = jnp.concatenate([dk.astype(BF16), dv_ref[...].astype(BF16)], axis=1)
        ckv = ckv_ref[...]
        rkv = _rms(ckv)
        ckvn = (ckv * rkv * gkv_ref[...]).astype(BF16)
        dwkv_ref[...] += _dot_tn(ckvn, dkv)
        dckv, dgrow = _rms_bwd(_dot_nt(dkv, wkv_ref[...]), ckv * rkv, rkv, gkv_ref[...])
        dgkv_ref[...] += _colsum(dgrow)
        dc_ref[:, 256:384] = dckv

    return pl.pallas_call(
        body, name="mla_prep_bwd", grid=(T // tm,),
        in_specs=_mla_lat_specs(tm) + [pl.BlockSpec((tm, 512), lambda i: (i, 0)),
                                       pl.BlockSpec((tm, 512), lambda i: (i, 0)),
                                       pl.BlockSpec((tm, 256), lambda i: (i, 0))],
        out_specs=[pl.BlockSpec((tm, 512), lambda i: (i, 0)),
                   pl.BlockSpec((256, 512), lambda i: (0, 0)),
                   pl.BlockSpec((LANES, 768), lambda i: (0, 0)),
                   pl.BlockSpec((1, 256), lambda i: (0, 0)),
                   pl.BlockSpec((1, LANES), lambda i: (0, 0))],
        out_shape=[jax.ShapeDtypeStruct((T, 512), F32),
                   jax.ShapeDtypeStruct((256, 512), F32),
                   jax.ShapeDtypeStruct((LANES, 768), F32),
                   jax.ShapeDtypeStruct((1, 256), F32),
                   jax.ShapeDtypeStruct((1, LANES), F32)],
        compiler_params=_cparams(("arbitrary",)),
    )(hf, hf, hf, pos, g_q, g_kv, w_uq, w_ukv, dqm, dkm, dvm)


def _causal_masks(bq, bk):
    row = lax.broadcasted_iota(jnp.int32, (bq, bk), 0)
    col = lax.broadcasted_iota(jnp.int32, (bq, bk), 1)
    return row, col


def _mla_fwd(qm, km, vm):
    T = qm.shape[0]
    bq = bk = ATT_BLK
    nq = T // bq

    def body(q_ref, k_ref, v_ref, o_ref, lse_ref, acc_ref, m_ref):
        qi = pl.program_id(1)
        lane = lax.broadcasted_iota(jnp.int32, (1, LANES), 1) // HEAD
        row, col = _causal_masks(bq, bk)
        causal = col <= row
        acc_ref[...] = jnp.zeros_like(acc_ref)
        m_ref[...] = jnp.full_like(m_ref, NEG)

        def scores(kb):
            rows = pl.ds(pl.multiple_of(kb * bk, bk), bk)
            return tuple(_dot_nt(q_ref[:, e * LANES:(e + 1) * LANES], k_ref[rows, e * LANES:(e + 1) * LANES])
                         for e in range(2))

        def step(kb, ss, masked):
            v_pair = v_ref[pl.ds(pl.multiple_of(kb * bk, bk), bk), :]
            ps, alphas = [], []
            for e in range(2):
                s = jnp.where(causal, ss[e], NEG) if masked else ss[e]
                m_prev = m_ref[e]
                m_new = jnp.maximum(m_prev, jnp.max(s, axis=1, keepdims=True))
                ps.append(jnp.exp(s - jnp.concatenate([m_new] * (bk // LANES), axis=1)).astype(BF16))
                alphas.append(jnp.exp(m_prev - m_new))
                m_ref[e] = m_new
            for e in range(2):
                v = jnp.where(lane == e, v_pair, 1)
                acc_ref[e] = alphas[e] * acc_ref[e] + _dot(ps[e], v)

        step(qi, scores(qi), True)

        def loop(kb, ss):
            nxt = scores(jnp.minimum(kb + 1, jnp.maximum(qi - 1, 0)))
            step(kb, ss, False)
            return nxt

        lax.fori_loop(0, qi, loop, scores(0))
        out = jnp.zeros((bq, LANES), F32)
        lse = jnp.zeros((bq, LANES), F32)
        for e in range(2):
            acc = acc_ref[e]
            l = pltpu.roll(acc, HEAD, 1)
            out = jnp.where(lane == e, acc / l, out)
            lse = jnp.where(lane == e, m_ref[e] + jnp.log(l), lse)
        o_ref[...] = out
        lse_ref[...] = lse

    return pl.pallas_call(
        body, name="mla_fwd", grid=(2, nq),
        in_specs=[pl.BlockSpec((bq, 256), lambda j, i: (i, j)),
                  pl.BlockSpec((T, 256), lambda j, i: (0, j)),
                  pl.BlockSpec((T, LANES), lambda j, i: (0, j))],
        out_specs=[pl.BlockSpec((bq, LANES), lambda j, i: (i, j)),
                   pl.BlockSpec((bq, LANES), lambda j, i: (i, j))],
        out_shape=[jax.ShapeDtypeStruct((T, 256), F32), jax.ShapeDtypeStruct((T, 256), F32)],
        scratch_shapes=[pltpu.VMEM((2, bq, LANES), F32), pltpu.VMEM((2, bq, LANES), F32)],
        compiler_params=_cparams(("parallel", "arbitrary")),
    )(qm, km, vm)


def _mla_bwd(qm, km, vm, y, lse, dy):
    T = qm.shape[0]
    bq = bk = ATT_BLK
    nq = T // bq

    def body(q_ref, k_ref, v_ref, y_ref, lse_ref, dy_ref, dq_ref, dk_ref, dv_ref, dob_ref, st_ref):
        qi = pl.program_id(1)

        @pl.when(qi == 0)
        def _():
            dk_ref[...] = jnp.zeros_like(dk_ref)
            dv_ref[...] = jnp.zeros_like(dv_ref)

        lane = lax.broadcasted_iota(jnp.int32, (1, LANES), 1) // HEAD
        row, col = _causal_masks(bq, bk)
        causal = col <= row
        dq_ref[...] = jnp.zeros_like(dq_ref)
        lse = lse_ref[...]
        lse_other = pltpu.roll(lse, HEAD, 1)
        for e in range(2):
            do = jnp.where(lane == e, dy_ref[...], 0.0)
            dob_ref[e] = do.astype(BF16)
            st_ref[2 * e] = jnp.where(lane == e, lse, lse_other)
            st_ref[2 * e + 1] = jnp.broadcast_to(jnp.sum(do * y_ref[...], axis=1, keepdims=True), (bq, LANES))

        hss = [slice(e * LANES, (e + 1) * LANES) for e in range(2)]
        tile = lambda a: jnp.concatenate([a] * (bk // LANES), axis=1)

        def scores(kb):
            rows = pl.ds(pl.multiple_of(kb * bk, bk), bk)
            v_pair = v_ref[rows, :]
            return (tuple(_dot_nt(q_ref[:, hss[e]], k_ref[rows, hss[e]]) for e in range(2))
                    + tuple(_dot_nt(dob_ref[e], jnp.where(lane == e, v_pair, 0)) for e in range(2)))

        def step(kb, sc, masked):
            rows = pl.ds(pl.multiple_of(kb * bk, bk), bk)
            ps, dss = [], []
            for e in range(2):
                s = jnp.where(causal, sc[e], NEG) if masked else sc[e]
                p = jnp.exp(s - tile(st_ref[2 * e]))
                dss.append((p * (sc[2 + e] - tile(st_ref[2 * e + 1]))).astype(BF16))
                ps.append(p.astype(BF16))
            dv_ref[rows, :] += _dot_tn(ps[0], dob_ref[0]) + _dot_tn(ps[1], dob_ref[1])
            for e in range(2):
                dk_ref[rows, hss[e]] += _dot_tn(dss[e], q_ref[:, hss[e]])
            for e in range(2):
                dq_ref[:, hss[e]] += _dot(dss[e], k_ref[rows, hss[e]])

        step(qi, scores(qi), True)

        def loop(kb, sc):
            nxt = scores(jnp.minimum(kb + 1, jnp.maximum(qi - 1, 0)))
            step(kb, sc, False)
            return nxt

        lax.fori_loop(0, qi, loop, scores(0))

    return pl.pallas_call(
        body, name="mla_bwd", grid=(2, nq),
        in_specs=[pl.BlockSpec((bq, 256), lambda j, i: (i, j)),
                  pl.BlockSpec((T, 256), lambda j, i: (0, j)),
                  pl.BlockSpec((T, LANES), lambda j, i: (0, j)),
                  pl.BlockSpec((bq, LANES), lambda j, i: (i, j)),
                  pl.BlockSpec((bq, LANES), lambda j, i: (i, j)),
                  pl.BlockSpec((bq, LANES), lambda j, i: (i, j))],
        out_specs=[pl.BlockSpec((bq, 256), lambda j, i: (i, j)),
                   pl.BlockSpec((T, 256), lambda j, i: (0, j)),
                   pl.BlockSpec((T, LANES), lambda j, i: (0, j))],
        out_shape=[jax.ShapeDtypeStruct((T, 512), F32),
                   jax.ShapeDtypeStruct((T, 512), F32),
                   jax.ShapeDtypeStruct((T, 256), F32)],
        scratch_shapes=[pltpu.VMEM((2, bq, LANES), BF16), pltpu.VMEM((4, bq, LANES), F32)],
        compiler_params=_cparams(("parallel", "arbitrary")),
    )(qm, km, vm, y, lse, dy)


def _suffix_ones(n):
    r = lax.broadcasted_iota(jnp.int32, (n, n), 0)
    c = lax.broadcasted_iota(jnp.int32, (n, n), 1)
    return (r >= c).astype(BF16)


def _prefix_ones(n):
    r = lax.broadcasted_iota(jnp.int32, (n, n), 0)
    c = lax.broadcasted_iota(jnp.int32, (n, n), 1)
    return (r <= c).astype(BF16)


def _tri_sum(x, u):
    hi, lo = _split(x)
    return _dot(hi, u) + _dot(lo, u)


def _sb_specs(T, bq):
    qo, ko, vo = (_INT_OFF[n] // LANES for n in ("d_q", "d_k", "d_v"))
    return [pl.BlockSpec((bq, LANES), lambda j, i: (i, qo + j)),
            pl.BlockSpec((T, LANES), lambda j, i: (0, ko + j)),
            pl.BlockSpec((T, LANES), lambda j, i: (0, vo + j))]


def _sb_fwd(hb):
    T = hb.shape[0]
    bq = bk = ATT_BLK
    nq = T // bq

    def body(q_ref, k_ref, v_ref, o_ref, tot_ref, cnt_ref, qm_ref, car_ref):
        qi = pl.program_id(1)
        lane = lax.broadcasted_iota(jnp.int32, (1, LANES), 1) // HEAD
        row, col = _causal_masks(bq, bk)
        strict = col < row
        u = _suffix_ones(bk)
        o_ref[...] = jnp.zeros_like(o_ref)
        car_ref[...] = jnp.zeros_like(car_ref)
        for e in range(2):
            qm_ref[e] = jnp.where(lane == e, q_ref[...], 0) * 0.125

        def step(kb, masked):
            rows = pl.ds(pl.multiple_of(kb * bk, bk), bk)
            k_pair, v_pair = k_ref[rows, :], v_ref[rows, :]
            tile = lambda a: jnp.concatenate([a] * (bk // LANES), axis=1)
            zs = [_dot_nt(qm_ref[e], k_pair) for e in range(2)]
            splits = []
            for e in range(2):
                z = zs[e]
                lk = jnp.minimum(-z, 0.0) - jnp.log(1.0 + jnp.exp(-jnp.abs(z)))
                if masked:
                    lk = jnp.where(strict, lk, 0.0)
                splits.append(_split(lk))
            sufs = [_dot(hi, u) + _dot(lo, u) for hi, lo in splits]
            aas = []
            for e in range(2):
                a = jnp.exp(zs[e] + sufs[e] + tile(car_ref[e]))
                if masked:
                    a = jnp.where(strict, a, 0.0)
                aas.append(a.astype(BF16))
                car_ref[e] += jnp.broadcast_to(sufs[e][:, 0:1], (bq, LANES))
            o_ref[...] += (_dot(aas[0], jnp.where(lane == 0, v_pair, 0))
                           + _dot(aas[1], jnp.where(lane == 1, v_pair, 0)))

        step(qi, True)

        def live():
            return jnp.max(jnp.maximum(car_ref[0], car_ref[1])) >= SB_DEAD

        def cond(c):
            return jnp.logical_and(c[0] < qi, c[1])

        def loop(c):
            step(qi - 1 - c[0], False)
            return c[0] + 1, live()

        done, _ = lax.while_loop(cond, loop, (jnp.int32(0), live()))
        tot_ref[...] = jnp.where(lane == 0, car_ref[0], car_ref[1])
        cnt_ref[pl.program_id(0), qi] = done.astype(F32)

    return pl.pallas_call(
        body, name="sb_fwd", grid=(2, nq), in_specs=_sb_specs(T, bq),
        out_specs=[pl.BlockSpec((bq, LANES), lambda j, i: (i, j)), pl.BlockSpec((bq, LANES), lambda j, i: (i, j)),
                   pl.BlockSpec(memory_space=pltpu.SMEM)],
        out_shape=[jax.ShapeDtypeStruct((T, 256), F32), jax.ShapeDtypeStruct((T, 256), F32),
                   jax.ShapeDtypeStruct((2, nq), F32)],
        scratch_shapes=[pltpu.VMEM((2, bq, LANES), BF16), pltpu.VMEM((2, bq, LANES), F32)],
        compiler_params=_cparams(("parallel", "arbitrary")),
    )(hb, hb, hb)


def _sb_bwd(hb, tot, cnt, dy):
    T = hb.shape[0]
    bq = bk = ATT_BLK
    nq = T // bq

    def body(q_ref, k_ref, v_ref, tot_ref, dy_ref, cnt_ref, dq_ref, dk_ref, dv_ref, qm_ref, dob_ref, dqa_ref, rem_ref,
             cg_ref):
        qi = pl.program_id(1)

        @pl.when(qi == 0)
        def _():
            dk_ref[...] = jnp.zeros_like(dk_ref)
            dv_ref[...] = jnp.zeros_like(dv_ref)

        lane = lax.broadcasted_iota(jnp.int32, (1, LANES), 1) // HEAD
        row, col = _causal_masks(bq, bk)
        strict = col < row
        u = _prefix_ones(bk)
        tot = tot_ref[...]
        tot_other = pltpu.roll(tot, HEAD, 1)
        dqa_ref[...] = jnp.zeros_like(dqa_ref)
        cg_ref[...] = jnp.zeros_like(cg_ref)
        for e in range(2):
            qm_ref[e] = jnp.where(lane == e, q_ref[...], 0) * 0.125
            dob_ref[e] = jnp.where(lane == e, dy_ref[...], 0.0).astype(BF16)
            rem_ref[e] = jnp.where(lane == e, tot, tot_other)

        def step(kb, masked):
            rows = pl.ds(pl.multiple_of(kb * bk, bk), bk)
            k_pair, v_pair = k_ref[rows, :], v_ref[rows, :]
            tile = lambda a: jnp.concatenate([a] * (bk // LANES), axis=1)
            zs = [_dot_nt(qm_ref[e], k_pair) for e in range(2)]
            das = [_dot_nt(dob_ref[e], jnp.where(lane == e, v_pair, 0)) for e in range(2)]
            zls, splits = [], []
            for e in range(2):
                z = zs[e]
                lk = jnp.minimum(-z, 0.0) - jnp.log(1.0 + jnp.exp(-jnp.abs(z)))
                if masked:
                    lk = jnp.where(strict, lk, 0.0)
                zls.append(z + lk)
                splits.append(_split(lk))
            pres = [_dot(hi, u) + _dot(lo, u) for hi, lo in splits]
            aas, gs, gsplits = [], [], []
            for e in range(2):
                a = jnp.exp(zls[e] + (tile(rem_ref[e]) - pres[e]))
                if masked:
                    a = jnp.where(strict, a, 0.0)
                g = a * das[e]
                aas.append(a.astype(BF16))
                gs.append(g)
                gsplits.append(_split(g))
                rem_ref[e] -= jnp.broadcast_to(pres[e][:, bk - 1:bk], (bq, LANES))
            dv_ref[rows, :] += _dot_tn(aas[0], dob_ref[0]) + _dot_tn(aas[1], dob_ref[1])
            gpres = [_dot(hi, u) + _dot(lo, u) for hi, lo in gsplits]
            dzs = []
            for e in range(2):
                dz = gs[e] - jnp.exp(zls[e]) * (tile(cg_ref[e]) + gpres[e])
                if masked:
                    dz = jnp.where(strict, dz, 0.0)
                dzs.append(dz.astype(BF16))
                cg_ref[e] += jnp.broadcast_to(gpres[e][:, bk - 1:bk], (bq, LANES))
            dk_ref[rows, :] += _dot_tn(dzs[0], qm_ref[0]) + _dot_tn(dzs[1], qm_ref[1])
            for e in range(2):
                dqa_ref[e] += _dot(dzs[e], k_pair)

        def loop(kb, c):
            step(kb, False)
            return c

        start = qi - jnp.clip(cnt_ref[pl.program_id(0), qi].astype(jnp.int32), 0, qi)
        lax.fori_loop(start, qi, loop, 0)
        step(qi, True)
        dq_ref[...] = jnp.where(lane == 0, dqa_ref[0], dqa_ref[1]) * 0.125

    return pl.pallas_call(
        body, name="sb_bwd", grid=(2, nq),
        in_specs=_sb_specs(T, bq) + [pl.BlockSpec((bq, LANES), lambda j, i: (i, j)),
                                     pl.BlockSpec((bq, LANES), lambda j, i: (i, j)),
                                     pl.BlockSpec(memory_space=pltpu.SMEM)],
        out_specs=[pl.BlockSpec((bq, LANES), lambda j, i: (i, j)),
                   pl.BlockSpec((T, LANES), lambda j, i: (0, j)),
                   pl.BlockSpec((T, LANES), lambda j, i: (0, j))],
        out_shape=[jax.ShapeDtypeStruct((T, 256), F32)] * 3,
        scratch_shapes=[pltpu.VMEM((2, bq, LANES), BF16), pltpu.VMEM((2, bq, LANES), BF16),
                        pltpu.VMEM((2, bq, LANES), F32), pltpu.VMEM((2, bq, LANES), F32),
                        pltpu.VMEM((2, bq, LANES), F32)],
        compiler_params=_cparams(("parallel", "arbitrary")),
    )(hb, hb, hb, tot, dy, cnt)


EP_TM = 256


def _ep_in_specs(tm, rev):
    idx = (lambda i: rev - i) if rev is not None else (lambda i: i)
    bo = (_INT_OFF["b_b"] - N_HB) // 256
    halo = lambda i: jnp.maximum(idx(i) * (tm // 8) - 1, 0)
    return [pl.BlockSpec((tm, 256), lambda i: (idx(i), 0)),
            pl.BlockSpec((tm, 256), lambda i: (idx(i), 0)),
            pl.BlockSpec((tm, 256), lambda i: (idx(i), 0)),
            pl.BlockSpec((tm, D_MODEL), lambda i: (idx(i), 0)),
            pl.BlockSpec((tm, 256), lambda i: (idx(i), bo)),
            pl.BlockSpec((tm, 256), lambda i: (idx(i), bo + 1)),
            pl.BlockSpec((tm, 256), lambda i: (idx(i), bo + 2)),
            pl.BlockSpec((8, 256), lambda i: (halo(i), bo + 1)),
            pl.BlockSpec((8, 256), lambda i: (halo(i), bo + 2)),
            pl.BlockSpec((3, 256), lambda i: (0, 0)),
            pl.BlockSpec((1, 256), lambda i: (0, 0)),
            pl.BlockSpec((1, D_MODEL), lambda i: (0, 0)),
            pl.BlockSpec((D_MODEL, D_MODEL), lambda i: (0, 0)),
            pl.BlockSpec((1, D_MODEL), lambda i: (0, 0))]


def _ep_mix(first, ya_ref, yc_ref, yd_ref, gate_ref, bb_ref, bc_ref, bx_ref, hc_ref, hx_ref, cw_ref, cb_ref, gg_ref):
    tm = ya_ref.shape[0]
    u = bc_ref[...] * bx_ref[...]
    halo = jnp.where(first, 0.0, hc_ref[...] * hx_ref[...])
    row = lax.broadcasted_iota(jnp.int32, (tm, 1), 0)
    u1 = jnp.where(row == 0, halo[7:8, :], pltpu.roll(u, 1, 0))
    u2 = jnp.where(row == 0, halo[6:7, :], jnp.where(row == 1, halo[7:8, :], pltpu.roll(u, 2, 0)))
    cw = cw_ref[...]
    conv = cw[0:1, :] * u2 + cw[1:2, :] * u1 + cw[2:3, :] * u + cb_ref[...]
    bb = bb_ref[...]
    ys = [ya_ref[...], bb * conv, yc_ref[...], yd_ref[...]]
    rs = [_rms(y) for y in ys]
    gg = gg_ref[...]
    yhat = jnp.concatenate([y * r for y, r in zip(ys, rs)], axis=1)
    gate = gate_ref[...]
    sig = 1.0 / (1.0 + jnp.exp(-gate))
    return u, u1, u2, conv, bb, rs, yhat, yhat * gg, gate, sig


def _epilogue_fwd(x, ya, yc, yd, hf, conv_w, conv_b, g_grp, w_out, g_post):
    T = x.shape[0]
    tm = EP_TM

    def body(x_ref, ya_ref, yc_ref, yd_ref, gate_ref, bb_ref, bc_ref, bx_ref, hc_ref, hx_ref, cw_ref, cb_ref,
             gg_ref, wo_ref, gp_ref, o_ref):
        (_, _, _, _, _, _, _, yn, gate, sig) = _ep_mix(
            pl.program_id(0) == 0, ya_ref, yc_ref, yd_ref, gate_ref, bb_ref, bc_ref, bx_ref, hc_ref, hx_ref,
            cw_ref, cb_ref, gg_ref)
        z = _dot((yn * (gate * sig)).astype(BF16), wo_ref[...])
        o_ref[...] = x_ref[...] + z * _rms(z) * gp_ref[...]

    return pl.pallas_call(
        body, name="epilogue_fwd", grid=(T // tm,),
        in_specs=[pl.BlockSpec((tm, D_MODEL), lambda i: (i, 0))] + _ep_in_specs(tm, None),
        out_specs=pl.BlockSpec((tm, D_MODEL), lambda i: (i, 0)),
        out_shape=jax.ShapeDtypeStruct((T, D_MODEL), F32),
        compiler_params=_cparams(("parallel",)),
    )(x, ya, yc, yd, hf, hf, hf, hf, hf, hf, conv_w, conv_b, g_grp, w_out, g_post)


def _epilogue_bwd(dxn, ya, yc, yd, hf, conv_w, conv_b, g_grp, w_out, g_post):
    T = dxn.shape[0]
    tm = EP_TM
    nt = T // tm
    ridx = lambda i: (nt - 1 - i, 0)

    def body(dx_ref, ya_ref, yc_ref, yd_ref, gate_ref, bb_ref, bc_ref, bx_ref, hc_ref, hx_ref, cw_ref, cb_ref,
             gg_ref, wo_ref, gp_ref,
             dya_ref, dyc_ref, dyd_ref, dhf_ref, dwo_ref, dgp_ref, dgg_ref, dcw_ref, dcb_ref, carry_ref):
        i = pl.program_id(0)

        @pl.when(i == 0)
        def _():
            for r in (dwo_ref, dgp_ref, dgg_ref, dcw_ref, dcb_ref, carry_ref):
                r[...] = jnp.zeros_like(r)

        (u, u1, u2, conv, bb, rs, yhat, yn, gate, sig) = _ep_mix(
            i == nt - 1, ya_ref, yc_ref, yd_ref, gate_ref, bb_ref, bc_ref, bx_ref, hc_ref, hx_ref,
            cw_ref, cb_ref, gg_ref)
        silu = gate * sig
        ymix = (yn * silu).astype(BF16)
        z = _dot(ymix, wo_ref[...])
        rz = _rms(z)
        dz, dgrow = _rms_bwd(dx_ref[...], z * rz, rz, gp_ref[...])
        dgp_ref[...] += _colsum(dgrow)
        dzb = dz.astype(BF16)
        dwo_ref[...] += _dot_tn(ymix, dzb)
        dymix = _dot_nt(dzb, wo_ref[...])
        dhf_ref[:, 0:D_MODEL] = dymix * yn * (sig * (1.0 + gate * (1.0 - sig)))
        dyn = dymix * silu
        dgg_ref[...] += _colsum(dyn * yhat)
        gg = gg_ref[...]
        dys = []
        for gi in range(4):
            sl = slice(gi * GROUP, (gi + 1) * GROUP)
            dyh = dyn[:, sl] * gg[:, sl]
            yh = yhat[:, sl]
            dys.append(rs[gi] * (dyh - yh * jnp.mean(dyh * yh, axis=-1, keepdims=True)))
        dya_ref[...] = dys[0]
        dyc_ref[...] = dys[2]
        dyd_ref[...] = dys[3]
        dyb = dys[1]
        dhf_ref[:, D_MODEL:D_MODEL + 256] = dyb * conv
        dconv = dyb * bb
        dcb_ref[...] += _colsum(dconv)
        dcw_ref[0:1, :] += _colsum(dconv * u2)
        dcw_ref[1:2, :] += _colsum(dconv * u1)
        dcw_ref[2:3, :] += _colsum(dconv * u)
        carry = carry_ref[...]
        row = lax.broadcasted_iota(jnp.int32, (tm, 1), 0)
        d1 = jnp.where(row == tm - 1, carry[0:1, :], pltpu.roll(dconv, tm - 1, 0))
        d2 = jnp.where(row == tm - 2, carry[0:1, :],
                       jnp.where(row == tm - 1, carry[1:2, :], pltpu.roll(dconv, tm - 2, 0)))
        cw = cw_ref[...]
        du = cw[2:3, :] * dconv + cw[1:2, :] * d1 + cw[0:1, :] * d2
        dhf_ref[:, D_MODEL + 256:D_MODEL + 512] = du * bx_ref[...]
        dhf_ref[:, D_MODEL + 512:D_MODEL + 768] = du * bc_ref[...]
        carry_ref[...] = dconv[0:8, :]

    in_specs = [pl.BlockSpec((tm, D_MODEL), ridx)] + _ep_in_specs(tm, nt - 1)
    return pl.pallas_call(
        body, name="epilogue_bwd", grid=(nt,), in_specs=in_specs,
        out_specs=[pl.BlockSpec((tm, 256), ridx), pl.BlockSpec((tm, 256), ridx), pl.BlockSpec((tm, 256), ridx),
                   pl.BlockSpec((tm, D_MODEL + 768), ridx),
                   pl.BlockSpec((D_MODEL, D_MODEL), lambda i: (0, 0)),
                   pl.BlockSpec((1, D_MODEL), lambda i: (0, 0)),
                   pl.BlockSpec((1, D_MODEL), lambda i: (0, 0)),
                   pl.BlockSpec((8, 256), lambda i: (0, 0)),
                   pl.BlockSpec((1, 256), lambda i: (0, 0))],
        out_shape=[jax.ShapeDtypeStruct((T, 256), F32)] * 3
                  + [jax.ShapeDtypeStruct((T, D_MODEL + 768), F32),
                     jax.ShapeDtypeStruct((D_MODEL, D_MODEL), F32),
                     jax.ShapeDtypeStruct((1, D_MODEL), F32),
                     jax.ShapeDtypeStruct((1, D_MODEL), F32),
                     jax.ShapeDtypeStruct((8, 256), F32),
                     jax.ShapeDtypeStruct((1, 256), F32)],
        scratch_shapes=[pltpu.VMEM((8, 256), F32)],
        compiler_params=_cparams(("arbitrary",)),
    )(dxn, ya, yc, yd, hf, hf, hf, hf, hf, hf, conv_w, conv_b, g_grp, w_out, g_post)


def _loss_head(y, tgt):
    T = y.shape[0]
    tm = 512

    def body(y_ref, t_ref, dy_ref, l_ref):
        @pl.when(pl.program_id(0) == 0)
        def _():
            l_ref[...] = jnp.zeros_like(l_ref)

        d = y_ref[...] - t_ref[...]
        dy_ref[...] = d * (1.0 / D_MODEL)
        part = jnp.sum(jnp.sum(d * d, axis=1, keepdims=True), axis=0, keepdims=True)
        l_ref[...] += part * (0.5 / D_MODEL)

    return pl.pallas_call(
        body, name="loss_head", grid=(T // tm,),
        in_specs=[pl.BlockSpec((tm, D_MODEL), lambda i: (i, 0))] * 2,
        out_specs=[pl.BlockSpec((tm, D_MODEL), lambda i: (i, 0)), pl.BlockSpec((8, LANES), lambda i: (0, 0))],
        out_shape=[jax.ShapeDtypeStruct((T, D_MODEL), F32), jax.ShapeDtypeStruct((8, LANES), F32)],
        compiler_params=_cparams(("arbitrary",)),
    )(y, tgt)


def _place():
    return lax.axis_index("x"), lax.axis_index("y"), lax.axis_index("c")


def _other_chips(x, y):
    return [(1 - x, y), (x, 1 - y), (1 - x, 1 - y)]


HBM = pl.BlockSpec(memory_space=pl.ANY)


def _gather_chips(big, small):
    def body(big_ref, small_ref, obig_ref, osmall_ref, send_sems, recv_sems, local_sems):
        x, y, c = _place()
        me = 2 * x + y
        pairs = [(big_ref, obig_ref), (small_ref, osmall_ref)]
        local = [pltpu.make_async_copy(s, o.at[me], local_sems.at[n]) for n, (s, o) in enumerate(pairs)]
        for cp in local:
            cp.start()
        sends = []
        for j, (px, py) in enumerate(_other_chips(x, y)):
            for n, (s, o) in enumerate(pairs):
                cp = pltpu.make_async_remote_copy(
                    src_ref=s, dst_ref=o.at[me], send_sem=send_sems.at[2 * j + n], recv_sem=recv_sems.at[2 * j + n],
                    device_id=(px, py, c), device_id_type=MESH)
                cp.start()
                sends.append(cp)
        for j, (px, py) in enumerate(_other_chips(x, y)):
            for n, (s, o) in enumerate(pairs):
                pltpu.make_async_remote_copy(
                    src_ref=s, dst_ref=o.at[2 * px + py], send_sem=send_sems.at[2 * j + n],
                    recv_sem=recv_sems.at[2 * j + n], device_id=(px, py, c), device_id_type=MESH).wait_recv()
        for cp in sends:
            cp.wait_send()
        for cp in local:
            cp.wait()

    return pl.pallas_call(
        body, name="gather_chips",
        in_specs=[HBM, HBM], out_specs=[HBM, HBM],
        out_shape=[jax.ShapeDtypeStruct((4,) + big.shape, big.dtype),
                   jax.ShapeDtypeStruct((4,) + small.shape, small.dtype)],
        scratch_shapes=[pltpu.SemaphoreType.DMA((6,)), pltpu.SemaphoreType.DMA((6,)), pltpu.SemaphoreType.DMA((2,))],
    )(big, small)


def _scatter_chips(chunks, small):
    def body(ch_ref, sm_ref, och_ref, osm_ref, send_sems, recv_sems, ssend_sems, srecv_sems, local_sems):
        x, y, c = _place()
        me = 2 * x + y
        dev = 4 * x + 2 * y + c
        local = [pltpu.make_async_copy(ch_ref.at[me], och_ref.at[me], local_sems.at[0]),
                 pltpu.make_async_copy(sm_ref, osm_ref.at[dev], local_sems.at[1])]
        for cp in local:
            cp.start()
        sends = []
        for j, (px, py) in enumerate(_other_chips(x, y)):
            cp = pltpu.make_async_remote_copy(
                src_ref=ch_ref.at[2 * px + py], dst_ref=och_ref.at[me], send_sem=send_sems.at[j],
                recv_sem=recv_sems.at[j], device_id=(px, py, c), device_id_type=MESH)
            cp.start()
            sends.append(cp)
        flips = [(fx, fy, fc) for fx in (0, 1) for fy in (0, 1) for fc in (0, 1)][1:]
        for j, (fx, fy, fc) in enumerate(flips):
            cp = pltpu.make_async_remote_copy(
                src_ref=sm_ref, dst_ref=osm_ref.at[dev], send_sem=ssend_sems.at[j], recv_sem=srecv_sems.at[j],
                device_id=(x ^ fx, y ^ fy, c ^ fc), device_id_type=MESH)
            cp.start()
            sends.append(cp)
        for j, (px, py) in enumerate(_other_chips(x, y)):
            pltpu.make_async_remote_copy(
                src_ref=ch_ref.at[me], dst_ref=och_ref.at[2 * px + py], send_sem=send_sems.at[j],
                recv_sem=recv_sems.at[j], device_id=(px, py, c), device_id_type=MESH).wait_recv()
        for j, (fx, fy, fc) in enumerate(flips):
            src = 4 * (x ^ fx) + 2 * (y ^ fy) + (c ^ fc)
            pltpu.make_async_remote_copy(
                src_ref=sm_ref, dst_ref=osm_ref.at[src], send_sem=ssend_sems.at[j], recv_sem=srecv_sems.at[j],
                device_id=(x ^ fx, y ^ fy, c ^ fc), device_id_type=MESH).wait_recv()
        for cp in sends:
            cp.wait_send()
        for cp in local:
            cp.wait()

    return pl.pallas_call(
        body, name="scatter_chips",
        in_specs=[HBM, HBM], out_specs=[HBM, HBM],
        out_shape=[jax.ShapeDtypeStruct(chunks.shape, chunks.dtype),
                   jax.ShapeDtypeStruct((8,) + small.shape, small.dtype)],
        scratch_shapes=[pltpu.SemaphoreType.DMA((3,)), pltpu.SemaphoreType.DMA((3,)),
                        pltpu.SemaphoreType.DMA((7,)), pltpu.SemaphoreType.DMA((7,)),
                        pltpu.SemaphoreType.DMA((2,))],
    )(chunks, small)


def _swap_cores(part):
    def body(p_ref, o_ref, send_sem, recv_sem):
        x, y, c = _place()
        cp = pltpu.make_async_remote_copy(src_ref=p_ref, dst_ref=o_ref, send_sem=send_sem, recv_sem=recv_sem,
                                          device_id=(x, y, 1 - c), device_id_type=MESH)
        cp.start()
        cp.wait()

    return pl.pallas_call(
        body, name="swap_cores", in_specs=[HBM], out_specs=HBM,
        out_shape=jax.ShapeDtypeStruct(part.shape, part.dtype),
        scratch_shapes=[pltpu.SemaphoreType.DMA, pltpu.SemaphoreType.DMA],
    )(part)


def _sum_leading(buf, name):
    n, R, C = buf.shape
    tr = R
    for cand in (512, 256, 128, 64, 32, 16, 8):
        if R % cand == 0:
            tr = cand
            break

    def body(b_ref, o_ref):
        acc = b_ref[0]
        for k in range(1, n):
            acc = acc + b_ref[k]
        o_ref[...] = acc

    return pl.pallas_call(
        body, name=name, grid=(R // tr,),
        in_specs=[pl.BlockSpec((n, tr, C), lambda i: (0, i, 0))],
        out_specs=pl.BlockSpec((tr, C), lambda i: (i, 0)),
        out_shape=jax.ShapeDtypeStruct((R, C), F32),
        compiler_params=_cparams(("parallel",)),
    )(buf)


def _adamw(w, grads, m, v, name):
    R, C = w.shape
    tr = R
    for cand in (256, 128, 64, 32, 16, 8):
        if R % cand == 0:
            tr = cand
            break
    ng = len(grads)
    c1 = 1.0 / (1.0 - ADAM_B1 ** ADAM_STEP)
    c2 = 1.0 / (1.0 - ADAM_B2 ** ADAM_STEP)

    def body(w_ref, *rest):
        g_refs = rest[:ng]
        m_ref, v_ref, go_ref, d_ref, mo_ref, vo_ref = rest[ng:]
        g = g_refs[0][...]
        for r in g_refs[1:]:
            g = g + r[...]
        go_ref[...] = g
        mn = ADAM_B1 * m_ref[...] + (1.0 - ADAM_B1) * g
        vn = ADAM_B2 * v_ref[...] + (1.0 - ADAM_B2) * (g * g)
        mo_ref[...] = mn
        vo_ref[...] = vn
        d_ref[...] = -ADAM_LR * ((mn * c1) / (jnp.sqrt(vn * c2) + ADAM_EPS) + ADAM_WD * w_ref[...])

    spec = pl.BlockSpec((tr, C), lambda i: (i, 0))
    return pl.pallas_call(
        body, name=name, grid=(R // tr,),
        in_specs=[spec] * (3 + ng), out_specs=[spec] * 4,
        out_shape=[jax.ShapeDtypeStruct((R, C), F32)] * 4,
        compiler_params=_cparams(("parallel",)),
    )(w, *grads, m, v)


PACK_C = 1024
PACK_MULT = 64
_BIG = ("w_in", "w_out", "mla_w_uq", "mla_w_ukv", "conv_w")


def _pack_rows(parts, mult):
    flat = jnp.concatenate([p.reshape(-1) for p in parts])
    n = flat.shape[0]
    rows = -(-n // PACK_C)
    rows = -(-rows // mult) * mult
    return jnp.pad(flat, (0, rows * PACK_C - n)).reshape(rows, PACK_C)


def _unpack_rows(buf, shapes):
    flat = buf.reshape(-1)
    out, o = [], 0
    for s in shapes:
        n = math.prod(s)
        out.append(flat[o:o + n].reshape(s))
        o += n
    return out


_SMALL = ("norm_pre", "group_norm", "norm_post", "conv_b", "mla_q_norm", "mla_kv_norm", "attn_sinks")


def _pack_small(d):
    return _pack_rows([d[n] for n in _SMALL], 8)


def _unpack_small(buf, shapes):
    return dict(zip(_SMALL, _unpack_rows(buf, [shapes[n] for n in _SMALL])))


def _w_in_internal(w):
    cols = []
    for n in _INT_ORDER:
        o, wd = _REAL_OFF[n]
        cols.append(w[:, o:o + wd])
        if _INT_W[n] != wd:
            cols.append(jnp.zeros((w.shape[0], _INT_W[n] - wd), w.dtype))
    return jnp.concatenate(cols, axis=1)


def _w_in_real(dw):
    return jnp.concatenate([dw[:, _INT_OFF[n]:_INT_OFF[n] + wd] for n, wd in _REAL], axis=1)


def _uq_internal(w):
    return jnp.pad(w.reshape(256, 4, 96), ((0, 0), (0, 0), (0, 32))).reshape(256, 512)


def _uq_real(dw):
    return dw.reshape(256, 4, 128)[:, :, :96].reshape(256, 384)


def _ukv_internal(w):
    w4 = w.reshape(128, 4, 128)
    k = jnp.pad(w4[:, :, :64], ((0, 0), (0, 0), (0, 64))).reshape(128, 512)
    return jnp.concatenate([k, w4[:, :, 64:].reshape(128, 256)], axis=1)


def _ukv_real(dw):
    k = dw[:, :512].reshape(128, 4, 128)[:, :, :64]
    v = dw[:, 512:].reshape(128, 4, 64)
    return jnp.concatenate([k, v], axis=2).reshape(128, 512)


def _layer_fwd(x, pos, p):
    xn, hb, hf = _inproj_fwd(x, p["norm_pre"], p["w_in"])
    ya = _swa_fwd(hb, p["attn_sinks"])
    qm, km, vm = _mla_prep_fwd(hf, pos, p["mla_q_norm"], p["mla_kv_norm"], p["mla_w_uq"], p["mla_w_ukv"])
    yc, lse = _mla_fwd(qm, km, vm)
    yd, tot, cnt = _sb_fwd(hb)
    x_next = _epilogue_fwd(x, ya, yc, yd, hf, p["conv_w"], p["conv_b"], p["group_norm"], p["w_out"], p["norm_post"])
    return x_next, dict(x=x, xn=xn, hb=hb, hf=hf, ya=ya, yc=yc, yd=yd, tot=tot, cnt=cnt, qm=qm, km=km, vm=vm, lse=lse)


def _layer_bwd(dx_next, pos, p, s):
    (dya, dyc, dyd, dhf, dw_out, dg_post, dg_grp, dconv_w, dconv_b) = _epilogue_bwd(
        dx_next, s["ya"], s["yc"], s["yd"], s["hf"], p["conv_w"], p["conv_b"], p["group_norm"], p["w_out"],
        p["norm_post"])
    dq_d, dk_d, dv_d = _sb_bwd(s["hb"], s["tot"], s["cnt"], dyd)
    dqm, dkm, dvm = _mla_bwd(s["qm"], s["km"], s["vm"], s["yc"], s["lse"], dyc)
    dc, dw_uq, dw_ukv, dg_q, dg_kv = _mla_prep_bwd(
        s["hf"], pos, p["mla_q_norm"], p["mla_kv_norm"], p["mla_w_uq"], p["mla_w_ukv"], dqm, dkm, dvm)
    dq_a, dk_a, dv_a, dsinks = _swa_bwd(s["hb"], p["attn_sinks"], dya)
    dx, dh, dg_pre = _inproj_bwd_dx(s["x"], p["norm_pre"], p["w_in"], dx_next,
                                    [dq_a, dk_a, dv_a, dq_d, dk_d, dv_d, dhf, dc])
    dw_in = _matmul_tn(s["xn"], dh, "inproj_bwd_dw")
    grads = dict(norm_pre=dg_pre[0], w_in=_w_in_real(dw_in), attn_sinks=dsinks[0, :4], conv_w=dconv_w[:3],
                 conv_b=dconv_b[0], mla_q_norm=dg_q[0], mla_w_uq=_uq_real(dw_uq), mla_kv_norm=dg_kv[0],
                 mla_w_ukv=_ukv_real(dw_ukv), group_norm=dg_grp[0], w_out=dw_out, norm_post=dg_post[0])
    return dx, grads


_WEIGHTS = ["norm_pre", "w_in", "attn_sinks", "conv_w", "conv_b", "mla_q_norm", "mla_w_uq", "mla_kv_norm",
            "mla_w_ukv", "group_norm", "w_out", "norm_post"]


def kernel(x, positions, norm_pre, w_in, attn_sinks, conv_w, conv_b, mla_q_norm, mla_w_uq, mla_kv_norm, mla_w_ukv, group_norm, w_out, norm_post, loss_target, m_norm_pre, m_w_in, m_attn_sinks, m_conv_w, m_conv_b, m_mla_q_norm, m_mla_w_uq, m_mla_kv_norm, m_mla_w_ukv, m_group_norm, m_w_out, m_norm_post, v_norm_pre, v_w_in, v_attn_sinks, v_conv_w, v_conv_b, v_mla_q_norm, v_mla_w_uq, v_mla_kv_norm, v_mla_w_ukv, v_group_norm, v_w_out, v_norm_post):
    w = dict(norm_pre=norm_pre, w_in=w_in, attn_sinks=attn_sinks, conv_w=conv_w, conv_b=conv_b,
             mla_q_norm=mla_q_norm, mla_w_uq=mla_w_uq, mla_kv_norm=mla_kv_norm, mla_w_ukv=mla_w_ukv,
             group_norm=group_norm, w_out=w_out, norm_post=norm_post)
    m = dict(norm_pre=m_norm_pre, w_in=m_w_in, attn_sinks=m_attn_sinks, conv_w=m_conv_w, conv_b=m_conv_b,
             mla_q_norm=m_mla_q_norm, mla_w_uq=m_mla_w_uq, mla_kv_norm=m_mla_kv_norm, mla_w_ukv=m_mla_w_ukv,
             group_norm=m_group_norm, w_out=m_w_out, norm_post=m_norm_post)
    v = dict(norm_pre=v_norm_pre, w_in=v_w_in, attn_sinks=v_attn_sinks, conv_w=v_conv_w, conv_b=v_conv_b,
             mla_q_norm=v_mla_q_norm, mla_w_uq=v_mla_w_uq, mla_kv_norm=v_mla_kv_norm, mla_w_ukv=v_mla_w_ukv,
             group_norm=v_group_norm, w_out=v_w_out, norm_post=v_norm_post)
    T = x.shape[1]
    xs = x[0]
    pos = positions[0].reshape(T, 1)
    tgt = loss_target[0]
    big_shapes = [w[n].shape for n in _BIG]
    mm = _BIG[:4]

    wb = _pack_rows([w[n].astype(BF16) for n in mm], 16)
    wc = _pack_rows([w["conv_w"]], 8)
    gb, gc = _gather_chips(wb, wc)
    per_chip = [_unpack_rows(gb[k], big_shapes[:4]) + _unpack_rows(gc[k], big_shapes[4:]) for k in range(4)]
    full = {}
    for n_i, n in enumerate(_BIG):
        axis = 1 if n == "w_out" else 2
        full[n] = jnp.concatenate([per_chip[k][n_i] for k in range(4)], axis=axis)

    layers = []
    for l in range(DEPTH):
        layers.append(dict(
            norm_pre=norm_pre[l:l + 1], w_in=_w_in_internal(full["w_in"][l]), attn_sinks=attn_sinks[l],
            conv_w=full["conv_w"][l], conv_b=conv_b[l:l + 1], mla_q_norm=mla_q_norm[l:l + 1],
            mla_w_uq=_uq_internal(full["mla_w_uq"][l]), mla_kv_norm=mla_kv_norm[l:l + 1],
            mla_w_ukv=_ukv_internal(full["mla_w_ukv"][l]), group_norm=group_norm[l:l + 1],
            w_out=full["w_out"][l], norm_post=norm_post[l:l + 1]))

    saved = []
    h = xs
    for l in range(DEPTH):
        h, s = _layer_fwd(h, pos, layers[l])
        saved.append(s)
    dy, loss_part = _loss_head(h, tgt)
    loss = lax.psum(loss_part[0, 0], ("x", "y", "c"))

    grads = [None] * DEPTH
    for l in reversed(range(DEPTH)):
        dy, grads[l] = _layer_bwd(dy, pos, layers[l], saved[l])
    g = {n: jnp.stack([grads[l][n] for l in range(DEPTH)]) for n in _WEIGHTS}

    def shard(n, k):
        a = g[n]
        if n == "w_out":
            return a[:, 256 * k:256 * (k + 1), :]
        wd = a.shape[2] // 4
        return a[:, :, wd * k:wd * (k + 1)]

    chunks = jnp.stack([_pack_rows([shard(n, k) for n in _BIG], PACK_MULT) for k in range(4)])
    small = _pack_small(g)
    got, got_small = _scatter_chips(chunks, small)
    part = _sum_leading(got, "sum_chips")
    other = _swap_cores(part)
    g_small = _sum_leading(got_small, "sum_small")

    pack_big = lambda d: _pack_rows([d[n] for n in _BIG], PACK_MULT)
    gb_, db_, mb_, vb_ = _adamw(pack_big(w), [part, other], pack_big(m), pack_big(v), "adamw_big")
    gs_, ds_, ms_, vs_ = _adamw(_pack_small(w), [g_small], _pack_small(m), _pack_small(v), "adamw_small")
    small_shapes = {n: w[n].shape for n in _SMALL}
    outs = []
    for big, sm in ((gb_, gs_), (db_, ds_), (mb_, ms_), (vb_, vs_)):
        d = dict(zip(_BIG, _unpack_rows(big, big_shapes)))
        d.update(_unpack_small(sm, small_shapes))
        outs.append(d)
    return (loss, dy[None], *[outs[0][n] for n in _WEIGHTS], *[outs[1][n] for n in _WEIGHTS],
            *[outs[2][n] for n in _WEIGHTS], *[outs[3][n] for n in _WEIGHTS])
```

```python
import math

import jax
import jax.numpy as jnp
from jax import lax
from jax.experimental import pallas as pl
from jax.experimental.pallas import tpu as pltpu

F32 = jnp.float32
BF16 = jnp.bfloat16
MESH = pl.DeviceIdType.MESH

D_MODEL = 1024
DEPTH = 2
EPS = 1e-6
BLOCK = 128
HEAD = 64
LANES = 128
GROUP = 256
MLA_SCALE = 96 ** -0.5
ROPE_HALF = 16
ROPE_THETA = 10000.0
ATT_BLK = 256
NEG = -1e30
SB_DEAD = -104.0

ADAM_LR, ADAM_B1, ADAM_B2, ADAM_EPS, ADAM_WD, ADAM_STEP = 0.001, 0.9, 0.999, 1e-08, 0.01, 10

_REAL = [("a_q", 256), ("a_k", 128), ("a_v", 128), ("b_b", 256), ("b_c", 256), ("b_x", 256),
         ("c_q", 256), ("c_kv", 128), ("c_kr", 32), ("d_q", 256), ("d_k", 256), ("d_v", 256),
         ("gate", 1024)]
_REAL_OFF = {}
_o = 0
for _n, _w in _REAL:
    _REAL_OFF[_n] = (_o, _w)
    _o += _w
D_IN = _o
_INT_ORDER = ["a_q", "a_k", "a_v", "d_q", "d_k", "d_v", "gate", "b_b", "b_c", "b_x", "c_q", "c_kv", "c_kr"]
_INT_W = dict(_REAL)
_INT_W["c_kr"] = 128
_INT_OFF = {}
_o = 0
for _n in _INT_ORDER:
    _INT_OFF[_n] = _o
    _o += _INT_W[_n]
N_INT = _o
N_HB = _INT_OFF["gate"]
N_HF = N_INT - N_HB

VMEM_LIMIT = 56 * 1024 * 1024


def _cparams(sem):
    return pltpu.CompilerParams(dimension_semantics=sem, vmem_limit_bytes=VMEM_LIMIT)


def _dot(a, b):
    return jnp.dot(a, b, preferred_element_type=F32)


def _dot_nt(a, b):
    return lax.dot_general(a, b, (((1,), (1,)), ((), ())), preferred_element_type=F32)


def _dot_tn(a, b):
    return lax.dot_general(a, b, (((0,), (0,)), ((), ())), preferred_element_type=F32)


def _split(x):
    hi = x.astype(BF16)
    lo = (x - hi.astype(F32)).astype(BF16)
    return hi, lo


def _rms(x):
    return lax.rsqrt(jnp.mean(x * x, axis=-1, keepdims=True) + EPS)


def _rms_bwd(dy, xhat, r, g):
    dxhat = dy * g
    return r * (dxhat - xhat * jnp.mean(dxhat * xhat, axis=-1, keepdims=True)), dy * xhat


def _colsum(x):
    return jnp.sum(x, axis=0, keepdims=True)


def _inproj_fwd(x, g, w):
    T = x.shape[0]
    tm = 256

    def body(x_ref, g_ref, w_ref, xn_ref, hb_ref, hf_ref):
        xv = x_ref[...]
        xn = (xv * _rms(xv) * g_ref[...]).astype(BF16)
        xn_ref[...] = xn
        h = _dot(xn, w_ref[...])
        hb_ref[...] = h[:, :N_HB].astype(BF16)
        hf_ref[...] = h[:, N_HB:]

    return pl.pallas_call(
        body, name="inproj_fwd", grid=(T // tm,),
        in_specs=[pl.BlockSpec((tm, D_MODEL), lambda i: (i, 0)),
                  pl.BlockSpec((1, D_MODEL), lambda i: (0, 0)),
                  pl.BlockSpec((D_MODEL, N_INT), lambda i: (0, 0))],
        out_specs=[pl.BlockSpec((tm, D_MODEL), lambda i: (i, 0)),
                   pl.BlockSpec((tm, N_HB), lambda i: (i, 0)),
                   pl.BlockSpec((tm, N_HF), lambda i: (i, 0))],
        out_shape=[jax.ShapeDtypeStruct((T, D_MODEL), BF16),
                   jax.ShapeDtypeStruct((T, N_HB), BF16),
                   jax.ShapeDtypeStruct((T, N_HF), F32)],
        compiler_params=_cparams(("parallel",)),
    )(x, g, w)


def _inproj_bwd_dx(x, g, w, dx_next, pieces):
    T = x.shape[0]
    tm = 256
    widths = [p.shape[1] for p in pieces]
    assert sum(widths) == N_INT

    def body(x_ref, g_ref, w_ref, dxn_ref, *rest):
        p_refs = rest[:len(pieces)]
        dx_ref, dh_ref, dg_ref = rest[len(pieces):]
        dh = jnp.concatenate([p[...].astype(BF16) for p in p_refs], axis=1)
        dh_ref[...] = dh
        dxn = _dot_nt(dh, w_ref[...])
        xv = x_ref[...]
        r = _rms(xv)
        dx, dgrow = _rms_bwd(dxn, xv * r, r, g_ref[...])
        dx_ref[...] = dx + dxn_ref[...]

        @pl.when(pl.program_id(0) == 0)
        def _():
            dg_ref[...] = jnp.zeros_like(dg_ref)

        dg_ref[...] += _colsum(dgrow)

    return pl.pallas_call(
        body, name="inproj_bwd_dx", grid=(T // tm,),
        in_specs=[pl.BlockSpec((tm, D_MODEL), lambda i: (i, 0)),
                  pl.BlockSpec((1, D_MODEL), lambda i: (0, 0)),
                  pl.BlockSpec((D_MODEL, N_INT), lambda i: (0, 0)),
                  pl.BlockSpec((tm, D_MODEL), lambda i: (i, 0))]
                 + [pl.BlockSpec((tm, wd), lambda i: (i, 0)) for wd in widths],
        out_specs=[pl.BlockSpec((tm, D_MODEL), lambda i: (i, 0)),
                   pl.BlockSpec((tm, N_INT), lambda i: (i, 0)),
                   pl.BlockSpec((1, D_MODEL), lambda i: (0, 0))],
        out_shape=[jax.ShapeDtypeStruct((T, D_MODEL), F32),
                   jax.ShapeDtypeStruct((T, N_INT), BF16),
                   jax.ShapeDtypeStruct((1, D_MODEL), F32)],
        compiler_params=_cparams(("arbitrary",)),
    )(x, g, w, dx_next, *pieces)


def _matmul_tn(a, b, name):
    T, M = a.shape
    N = b.shape[1]
    tm, tn = 512, 512

    def body(a_ref, b_ref, o_ref):
        @pl.when(pl.program_id(1) == 0)
        def _():
            o_ref[...] = jnp.zeros_like(o_ref)

        o_ref[...] += _dot_tn(a_ref[...], b_ref[...])

    return pl.pallas_call(
        body, name=name, grid=(N // tn, T // tm),
        in_specs=[pl.BlockSpec((tm, M), lambda j, t: (t, 0)),
                  pl.BlockSpec((tm, tn), lambda j, t: (t, j))],
        out_specs=pl.BlockSpec((M, tn), lambda j, t: (0, j)),
        out_shape=jax.ShapeDtypeStruct((M, N), F32),
        compiler_params=_cparams(("parallel", "arbitrary")),
    )(a, b)


def _roll_f32(x, shift):
    return pltpu.roll(x.astype(F32), shift, 1)


def _swa_head(h, q_ref, kp_ref, kc_ref, vp_ref, vc_ref, sink, first):
    p, e = h // 2, h % 2
    lane = lax.broadcasted_iota(jnp.int32, (1, LANES), 1) // HEAD
    q = q_ref[:, p * LANES:(p + 1) * LANES]
    k_prev, k_cur, v_prev, v_cur = kp_ref[...], kc_ref[...], vp_ref[...], vc_ref[...]
    if e != p:
        q = _roll_f32(q, HEAD).astype(BF16)
        v_prev = _roll_f32(v_prev, HEAD).astype(BF16)
        v_cur = _roll_f32(v_cur, HEAD).astype(BF16)
    qs = jnp.where(lane == p, q, 0) * 0.125
    v_prev = jnp.where(lane == e, v_prev, 0)
    v_cur = jnp.where(lane == e, v_cur, 0)
    row = lax.broadcasted_iota(jnp.int32, (BLOCK, BLOCK), 0)
    col = lax.broadcasted_iota(jnp.int32, (BLOCK, BLOCK), 1)
    ok_prev = jnp.logical_and(col > row, jnp.logical_not(first))
    ok_cur = col <= row
    s_prev = jnp.where(ok_prev, _dot_nt(qs, k_prev), NEG)
    s_cur = jnp.where(ok_cur, _dot_nt(qs, k_cur), NEG)
    m = jnp.maximum(jnp.maximum(jnp.max(s_prev, axis=1, keepdims=True),
                                jnp.max(s_cur, axis=1, keepdims=True)), sink)
    p_prev = jnp.exp(s_prev - m)
    p_cur = jnp.exp(s_cur - m)
    p_sink = jnp.exp(sink - m)
    inv = 1.0 / (jnp.sum(p_prev, axis=1, keepdims=True) + jnp.sum(p_cur, axis=1, keepdims=True) + p_sink)
    return (p, e, lane, qs, k_prev, k_cur, v_prev, v_cur, p_prev * inv, p_cur * inv, p_sink * inv)


def _swa_specs(T):
    nb = T // BLOCK
    qo, ko, vo = (_INT_OFF[n] // LANES for n in ("a_q", "a_k", "a_v"))
    prev = lambda i: jnp.maximum(i - 1, 0)
    return [pl.BlockSpec((BLOCK, 256), lambda i: (i, qo // 2)),
            pl.BlockSpec((BLOCK, LANES), lambda i: (prev(i), ko)),
            pl.BlockSpec((BLOCK, LANES), lambda i: (i, ko)),
            pl.BlockSpec((BLOCK, LANES), lambda i: (prev(i), vo)),
            pl.BlockSpec((BLOCK, LANES), lambda i: (i, vo)),
            pl.BlockSpec(memory_space=pltpu.SMEM)], nb


def _swa_fwd(hb, sinks):
    T = hb.shape[0]
    specs, nb = _swa_specs(T)

    def body(q_ref, kp_ref, kc_ref, vp_ref, vc_ref, s_ref, o_ref):
        first = pl.program_id(0) == 0
        for p in range(2):
            out = jnp.zeros((BLOCK, LANES), F32)
            for e in range(2):
                h = 2 * p + e
                (_, _, _, _, _, _, v_prev, v_cur, p_prev, p_cur, _) = _swa_head(
                    h, q_ref, kp_ref, kc_ref, vp_ref, vc_ref, s_ref[h], first)
                out += _dot(p_prev.astype(BF16), v_prev) + _dot(p_cur.astype(BF16), v_cur)
            o_ref[:, p * LANES:(p + 1) * LANES] = out

    return pl.pallas_call(
        body, name="swa_fwd", grid=(nb,), in_specs=specs,
        out_specs=pl.BlockSpec((BLOCK, 256), lambda i: (i, 0)),
        out_shape=jax.ShapeDtypeStruct((T, 256), F32),
        compiler_params=_cparams(("parallel",)),
    )(hb, hb, hb, hb, hb, sinks)


def _swa_bwd(hb, sinks, dy):
    T = hb.shape[0]
    specs, nb = _swa_specs(T)

    def body(q_ref, kp_ref, kc_ref, vp_ref, vc_ref, s_ref, dy_ref, dq_ref, dk_ref, dv_ref, ds_ref):
        i = pl.program_id(0)
        first = i == 0
        cur = pl.ds(pl.multiple_of(i * BLOCK, BLOCK), BLOCK)
        prv = pl.ds(pl.multiple_of(jnp.maximum(i - 1, 0) * BLOCK, BLOCK), BLOCK)

        @pl.when(first)
        def _():
            ds_ref[...] = jnp.zeros_like(ds_ref)

        dk_ref[cur, :] = jnp.zeros((BLOCK, LANES), F32)
        dv_ref[cur, :] = jnp.zeros((BLOCK, LANES), F32)
        lane_id = lax.broadcasted_iota(jnp.int32, (8, LANES), 1)
        for p in range(2):
            dq_pair = jnp.zeros((BLOCK, LANES), F32)
            for e in range(2):
                h = 2 * p + e
                (_, _, lane, qs, k_prev, k_cur, v_prev, v_cur, p_prev, p_cur, p_sink) = _swa_head(
                    h, q_ref, kp_ref, kc_ref, vp_ref, vc_ref, s_ref[h], first)
                do = jnp.where(lane == e, dy_ref[:, p * LANES:(p + 1) * LANES], 0.0)
                dob = do.astype(BF16)
                pb_prev, pb_cur = p_prev.astype(BF16), p_cur.astype(BF16)
                o = _dot(pb_prev, v_prev) + _dot(pb_cur, v_cur)
                delta = jnp.sum(do * o, axis=1, keepdims=True)
                ds_prev = (p_prev * (_dot_nt(dob, v_prev) - delta)).astype(BF16)
                ds_cur = (p_cur * (_dot_nt(dob, v_cur) - delta)).astype(BF16)
                dsink = -jnp.sum(p_sink * delta, axis=0, keepdims=True)
                ds_ref[...] += jnp.where(lane_id == h, dsink, 0.0)
                dq = (_dot(ds_prev, k_prev) + _dot(ds_cur, k_cur)) * 0.125
                dq = jnp.where(lane == p, dq, 0.0)
                dob_v = dob
                if e != p:
                    dq = pltpu.roll(dq, HEAD, 1)
                    dob_v = pltpu.roll(do, HEAD, 1).astype(BF16)
                dq_pair += dq
                dk_ref[prv, :] += _dot_tn(ds_prev, qs)
                dk_ref[cur, :] += _dot_tn(ds_cur, qs)
                dv_ref[prv, :] += _dot_tn(pb_prev, dob_v)
                dv_ref[cur, :] += _dot_tn(pb_cur, dob_v)
            dq_ref[:, p * LANES:(p + 1) * LANES] = dq_pair

    return pl.pallas_call(
        body, name="swa_bwd", grid=(nb,),
        in_specs=specs + [pl.BlockSpec((BLOCK, 256), lambda i: (i, 0))],
        out_specs=[pl.BlockSpec((BLOCK, 256), lambda i: (i, 0)),
                   pl.BlockSpec((T, LANES), lambda i: (0, 0)),
                   pl.BlockSpec((T, LANES), lambda i: (0, 0)),
                   pl.BlockSpec((8, LANES), lambda i: (0, 0))],
        out_shape=[jax.ShapeDtypeStruct((T, 256), F32),
                   jax.ShapeDtypeStruct((T, LANES), F32),
                   jax.ShapeDtypeStruct((T, LANES), F32),
                   jax.ShapeDtypeStruct((8, LANES), F32)],
        compiler_params=_cparams(("arbitrary",)),
    )(hb, hb, hb, hb, hb, sinks, dy)


def _rope_tables(pos_ref):
    lane = lax.broadcasted_iota(jnp.int32, (1, LANES), 1)
    active = jnp.logical_and(lane >= HEAD, lane < HEAD + 2 * ROPE_HALF)
    idx = ((lane - HEAD) % ROPE_HALF).astype(F32)
    freq = jnp.exp(idx * (-math.log(ROPE_THETA) / ROPE_HALF))
    ang = pos_ref[...].astype(F32) * freq
    cos, sin = jnp.cos(ang), jnp.sin(ang)
    c = jnp.where(active, cos, 1.0)
    s_up = jnp.where(jnp.logical_and(active, lane >= HEAD + ROPE_HALF), sin, 0.0)
    s_dn = jnp.where(jnp.logical_and(active, lane < HEAD + ROPE_HALF), -sin, 0.0)
    return c, s_up, s_dn


def _rope(x, tabs):
    c, s_up, s_dn = tabs
    return x * c + pltpu.roll(x, ROPE_HALF, 1) * s_up + pltpu.roll(x, LANES - ROPE_HALF, 1) * s_dn


def _rope_t(dy, tabs):
    c, s_up, s_dn = tabs
    return dy * c + pltpu.roll(dy * s_up, LANES - ROPE_HALF, 1) + pltpu.roll(dy * s_dn, ROPE_HALF, 1)


def _mla_lat_specs(tm):
    cq, ckv, ckr = ((_INT_OFF[n] - N_HB) for n in ("c_q", "c_kv", "c_kr"))
    return [pl.BlockSpec((tm, 256), lambda i: (i, cq // 256)),
            pl.BlockSpec((tm, LANES), lambda i: (i, ckv // LANES)),
            pl.BlockSpec((tm, LANES), lambda i: (i, ckr // LANES)),
            pl.BlockSpec((tm, 1), lambda i: (i, 0)),
            pl.BlockSpec((1, 256), lambda i: (0, 0)),
            pl.BlockSpec((1, LANES), lambda i: (0, 0)),
            pl.BlockSpec((256, 512), lambda i: (0, 0)),
            pl.BlockSpec((LANES, 768), lambda i: (0, 0))]


def _mla_prep_fwd(hf, pos, g_q, g_kv, w_uq, w_ukv):
    T = hf.shape[0]
    tm = 512

    def body(cq_ref, ckv_ref, ckr_ref, pos_ref, gq_ref, gkv_ref, wq_ref, wkv_ref, qm_ref, km_ref, vm_ref):
        tabs = _rope_tables(pos_ref)
        cq = cq_ref[...]
        q = _dot((cq * _rms(cq) * gq_ref[...]).astype(BF16), wq_ref[...])
        ckv = ckv_ref[...]
        kv = _dot((ckv * _rms(ckv) * gkv_ref[...]).astype(BF16), wkv_ref[...])
        kr = _rope(pltpu.roll(ckr_ref[...], HEAD, 1), tabs)
        for h in range(4):
            sl = slice(h * LANES, (h + 1) * LANES)
            qm_ref[:, sl] = (_rope(q[:, sl], tabs) * MLA_SCALE).astype(BF16)
            km_ref[:, sl] = (kv[:, sl] + kr).astype(BF16)
        vm_ref[...] = kv[:, 512:].astype(BF16)

    return pl.pallas_call(
        body, name="mla_prep_fwd", grid=(T // tm,), in_specs=_mla_lat_specs(tm),
        out_specs=[pl.BlockSpec((tm, 512), lambda i: (i, 0)),
                   pl.BlockSpec((tm, 512), lambda i: (i, 0)),
                   pl.BlockSpec((tm, 256), lambda i: (i, 0))],
        out_shape=[jax.ShapeDtypeStruct((T, 512), BF16),
                   jax.ShapeDtypeStruct((T, 512), BF16),
                   jax.ShapeDtypeStruct((T, 256), BF16)],
        compiler_params=_cparams(("parallel",)),
    )(hf, hf, hf, pos, g_q, g_kv, w_uq, w_ukv)


def _mla_prep_bwd(hf, pos, g_q, g_kv, w_uq, w_ukv, dqm, dkm, dvm):
    T = hf.shape[0]
    tm = 512

    def body(cq_ref, ckv_ref, ckr_ref, pos_ref, gq_ref, gkv_ref, wq_ref, wkv_ref, dq_ref, dk_ref, dv_ref,
             dc_ref, dwq_ref, dwkv_ref, dgq_ref, dgkv_ref):
        @pl.when(pl.program_id(0) == 0)
        def _():
            dwq_ref[...] = jnp.zeros_like(dwq_ref)
            dwkv_ref[...] = jnp.zeros_like(dwkv_ref)
            dgq_ref[...] = jnp.zeros_like(dgq_ref)
            dgkv_ref[...] = jnp.zeros_like(dgkv_ref)

        tabs = _rope_tables(pos_ref)
        lane = lax.broadcasted_iota(jnp.int32, (1, LANES), 1)
        dq = jnp.concatenate([_rope_t(dq_ref[:, h * LANES:(h + 1) * LANES] * MLA_SCALE, tabs)
                              for h in range(4)], axis=1).astype(BF16)
        cq = cq_ref[...]
        rq = _rms(cq)
        cqn = (cq * rq * gq_ref[...]).astype(BF16)
        dwq_ref[...] += _dot_tn(cqn, dq)
        dcq, dgrow = _rms_bwd(_dot_nt(dq, wq_ref[...]), cq * rq, rq, gq_ref[...])
        dgq_ref[...] += _colsum(dgrow)
        dc_ref[:, 0:256] = dcq

        dk = dk_ref[...]
        dkr = dk[:, 0:LANES] + dk[:, LANES:2 * LANES] + dk[:, 2 * LANES:3 * LANES] + dk[:, 3 * LANES:]
        dkr = pltpu.roll(_rope_t(dkr, tabs), HEAD, 1)
        dc_ref[:, 384:512] = jnp.where(lane < 2 * ROPE_HALF, dkr, 0.0)
        dkv = jnp.concatenate([dk.astype(BF16), dv_ref[...].astype(BF16)], axis=1)
        ckv = ckv_ref[...]
        rkv = _rms(ckv)
        ckvn = (ckv * rkv * gkv_ref[...]).astype(BF16)
        dwkv_ref[...] += _dot_tn(ckvn, dkv)
        dckv, dgrow = _rms_bwd(_dot_nt(dkv, wkv_ref[...]), ckv * rkv, rkv, gkv_ref[...])
        dgkv_ref[...] += _colsum(dgrow)
        dc_ref[:, 256:384] = dckv

    return pl.pallas_call(
        body, name="mla_prep_bwd", grid=(T // tm,),
        in_specs=_mla_lat_specs(tm) + [pl.BlockSpec((tm, 512), lambda i: (i, 0)),
                                       pl.BlockSpec((tm, 512), lambda i: (i, 0)),
                                       pl.BlockSpec((tm, 256), lambda i: (i, 0))],
        out_specs=[pl.BlockSpec((tm, 512), lambda i: (i, 0)),
                   pl.BlockSpec((256, 512), lambda i: (0, 0)),
                   pl.BlockSpec((LANES, 768), lambda i: (0, 0)),
                   pl.BlockSpec((1, 256), lambda i: (0, 0)),
                   pl.BlockSpec((1, LANES), lambda i: (0, 0))],
        out_shape=[jax.ShapeDtypeStruct((T, 512), F32),
                   jax.ShapeDtypeStruct((256, 512), F32),
                   jax.ShapeDtypeStruct((LANES, 768), F32),
                   jax.ShapeDtypeStruct((1, 256), F32),
                   jax.ShapeDtypeStruct((1, LANES), F32)],
        compiler_params=_cparams(("arbitrary",)),
    )(hf, hf, hf, pos, g_q, g_kv, w_uq, w_ukv, dqm, dkm, dvm)


def _causal_masks(bq, bk):
    row = lax.broadcasted_iota(jnp.int32, (bq, bk), 0)
    col = lax.broadcasted_iota(jnp.int32, (bq, bk), 1)
    return row, col


def _mla_fwd(qm, km, vm):
    T = qm.shape[0]
    bq = bk = ATT_BLK
    nq = T // bq

    def body(q_ref, k_ref, v_ref, o_ref, lse_ref, acc_ref, m_ref):
        qi = pl.program_id(1)
        lane = lax.broadcasted_iota(jnp.int32, (1, LANES), 1) // HEAD
        row, col = _causal_masks(bq, bk)
        causal = col <= row
        acc_ref[...] = jnp.zeros_like(acc_ref)
        m_ref[...] = jnp.full_like(m_ref, NEG)

        def scores(kb):
            rows = pl.ds(pl.multiple_of(kb * bk, bk), bk)
            return tuple(_dot_nt(q_ref[:, e * LANES:(e + 1) * LANES], k_ref[rows, e * LANES:(e + 1) * LANES])
                         for e in range(2))

        def step(kb, ss, masked):
            v_pair = v_ref[pl.ds(pl.multiple_of(kb * bk, bk), bk), :]
            ps, alphas = [], []
            for e in range(2):
                s = jnp.where(causal, ss[e], NEG) if masked else ss[e]
                m_prev = m_ref[e]
                m_new = jnp.maximum(m_prev, jnp.max(s, axis=1, keepdims=True))
                ps.append(jnp.exp(s - jnp.concatenate([m_new] * (bk // LANES), axis=1)).astype(BF16))
                alphas.append(jnp.exp(m_prev - m_new))
                m_ref[e] = m_new
            for e in range(2):
                v = jnp.where(lane == e, v_pair, 1)
                acc_ref[e] = alphas[e] * acc_ref[e] + _dot(ps[e], v)

        step(qi, scores(qi), True)

        def loop(kb, ss):
            nxt = scores(jnp.minimum(kb + 1, jnp.maximum(qi - 1, 0)))
            step(kb, ss, False)
            return nxt

        lax.fori_loop(0, qi, loop, scores(0))
        out = jnp.zeros((bq, LANES), F32)
        lse = jnp.zeros((bq, LANES), F32)
        for e in range(2):
            acc = acc_ref[e]
            l = pltpu.roll(acc, HEAD, 1)
            out = jnp.where(lane == e, acc / l, out)
            lse = jnp.where(lane == e, m_ref[e] + jnp.log(l), lse)
        o_ref[...] = out
        lse_ref[...] = lse

    return pl.pallas_call(
        body, name="mla_fwd", grid=(2, nq),
        in_specs=[pl.BlockSpec((bq, 256), lambda j, i: (i, j)),
                  pl.BlockSpec((T, 256), lambda j, i: (0, j)),
                  pl.BlockSpec((T, LANES), lambda j, i: (0, j))],
        out_specs=[pl.BlockSpec((bq, LANES), lambda j, i: (i, j)),
                   pl.BlockSpec((bq, LANES), lambda j, i: (i, j))],
        out_shape=[jax.ShapeDtypeStruct((T, 256), F32), jax.ShapeDtypeStruct((T, 256), F32)],
        scratch_shapes=[pltpu.VMEM((2, bq, LANES), F32), pltpu.VMEM((2, bq, LANES), F32)],
        compiler_params=_cparams(("parallel", "arbitrary")),
    )(qm, km, vm)


def _mla_bwd(qm, km, vm, y, lse, dy):
    T = qm.shape[0]
    bq = bk = ATT_BLK
    nq = T // bq

    def body(q_ref, k_ref, v_ref, y_ref, lse_ref, dy_ref, dq_ref, dk_ref, dv_ref, dob_ref, st_ref):
        qi = pl.program_id(1)

        @pl.when(qi == 0)
        def _():
            dk_ref[...] = jnp.zeros_like(dk_ref)
            dv_ref[...] = jnp.zeros_like(dv_ref)

        lane = lax.broadcasted_iota(jnp.int32, (1, LANES), 1) // HEAD
        row, col = _causal_masks(bq, bk)
        causal = col <= row
        dq_ref[...] = jnp.zeros_like(dq_ref)
        lse = lse_ref[...]
        lse_other = pltpu.roll(lse, HEAD, 1)
        for e in range(2):
            do = jnp.where(lane == e, dy_ref[...], 0.0)
            dob_ref[e] = do.astype(BF16)
            st_ref[2 * e] = jnp.where(lane == e, lse, lse_other)
            st_ref[2 * e + 1] = jnp.broadcast_to(jnp.sum(do * y_ref[...], axis=1, keepdims=True), (bq, LANES))

        hss = [slice(e * LANES, (e + 1) * LANES) for e in range(2)]
        tile = lambda a: jnp.concatenate([a] * (bk // LANES), axis=1)

        def scores(kb):
            rows = pl.ds(pl.multiple_of(kb * bk, bk), bk)
            v_pair = v_ref[rows, :]
            return (tuple(_dot_nt(q_ref[:, hss[e]], k_ref[rows, hss[e]]) for e in range(2))
                    + tuple(_dot_nt(dob_ref[e], jnp.where(lane == e, v_pair, 0)) for e in range(2)))

        def step(kb, sc, masked):
            rows = pl.ds(pl.multiple_of(kb * bk, bk), bk)
            ps, dss = [], []
            for e in range(2):
                s = jnp.where(causal, sc[e], NEG) if masked else sc[e]
                p = jnp.exp(s - tile(st_ref[2 * e]))
                dss.append((p * (sc[2 + e] - tile(st_ref[2 * e + 1]))).astype(BF16))
                ps.append(p.astype(BF16))
            dv_ref[rows, :] += _dot_tn(ps[0], dob_ref[0]) + _dot_tn(ps[1], dob_ref[1])
            for e in range(2):
                dk_ref[rows, hss[e]] += _dot_tn(dss[e], q_ref[:, hss[e]])
            for e in range(2):
                dq_ref[:, hss[e]] += _dot(dss[e], k_ref[rows, hss[e]])

        step(qi, scores(qi), True)

        def loop(kb, sc):
            nxt = scores(jnp.minimum(kb + 1, jnp.maximum(qi - 1, 0)))
            step(kb, sc, False)
            return nxt

        lax.fori_loop(0, qi, loop, scores(0))

    return pl.pallas_call(
        body, name="mla_bwd", grid=(2, nq),
        in_specs=[pl.BlockSpec((bq, 256), lambda j, i: (i, j)),
                  pl.BlockSpec((T, 256), lambda j, i: (0, j)),
                  pl.BlockSpec((T, LANES), lambda j, i: (0, j)),
                  pl.BlockSpec((bq, LANES), lambda j, i: (i, j)),
                  pl.BlockSpec((bq, LANES), lambda j, i: (i, j)),
                  pl.BlockSpec((bq, LANES), lambda j, i: (i, j))],
        out_specs=[pl.BlockSpec((bq, 256), lambda j, i: (i, j)),
                   pl.BlockSpec((T, 256), lambda j, i: (0, j)),
                   pl.BlockSpec((T, LANES), lambda j, i: (0, j))],
        out_shape=[jax.ShapeDtypeStruct((T, 512), F32),
                   jax.ShapeDtypeStruct((T, 512), F32),
                   jax.ShapeDtypeStruct((T, 256), F32)],
        scratch_shapes=[pltpu.VMEM((2, bq, LANES), BF16), pltpu.VMEM((4, bq, LANES), F32)],
        compiler_params=_cparams(("parallel", "arbitrary")),
    )(qm, km, vm, y, lse, dy)


def _suffix_ones(n):
    r = lax.broadcasted_iota(jnp.int32, (n, n), 0)
    c = lax.broadcasted_iota(jnp.int32, (n, n), 1)
    return (r >= c).astype(BF16)


def _prefix_ones(n):
    r = lax.broadcasted_iota(jnp.int32, (n, n), 0)
    c = lax.broadcasted_iota(jnp.int32, (n, n), 1)
    return (r <= c).astype(BF16)


def _tri_sum(x, u):
    hi, lo = _split(x)
    return _dot(hi, u) + _dot(lo, u)


def _sb_specs(T, bq):
    qo, ko, vo = (_INT_OFF[n] // LANES for n in ("d_q", "d_k", "d_v"))
    return [pl.BlockSpec((bq, LANES), lambda j, i: (i, qo + j)),
            pl.BlockSpec((T, LANES), lambda j, i: (0, ko + j)),
            pl.BlockSpec((T, LANES), lambda j, i: (0, vo + j))]


def _sb_fwd(hb):
    T = hb.shape[0]
    bq = bk = ATT_BLK
    nq = T // bq

    def body(q_ref, k_ref, v_ref, o_ref, tot_ref, cnt_ref, qm_ref, car_ref):
        qi = pl.program_id(1)
        lane = lax.broadcasted_iota(jnp.int32, (1, LANES), 1) // HEAD
        row, col = _causal_masks(bq, bk)
        strict = col < row
        u = _suffix_ones(bk)
        o_ref[...] = jnp.zeros_like(o_ref)
        car_ref[...] = jnp.zeros_like(car_ref)
        for e in range(2):
            qm_ref[e] = jnp.where(lane == e, q_ref[...], 0) * 0.125

        def step(kb, masked):
            rows = pl.ds(pl.multiple_of(kb * bk, bk), bk)
            k_pair, v_pair = k_ref[rows, :], v_ref[rows, :]
            tile = lambda a: jnp.concatenate([a] * (bk // LANES), axis=1)
            zs = [_dot_nt(qm_ref[e], k_pair) for e in range(2)]
            splits = []
            for e in range(2):
                z = zs[e]
                lk = jnp.minimum(-z, 0.0) - jnp.log(1.0 + jnp.exp(-jnp.abs(z)))
                if masked:
                    lk = jnp.where(strict, lk, 0.0)
                splits.append(_split(lk))
            sufs = [_dot(hi, u) + _dot(lo, u) for hi, lo in splits]
            aas = []
            for e in range(2):
                a = jnp.exp(zs[e] + sufs[e] + tile(car_ref[e]))
                if masked:
                    a = jnp.where(strict, a, 0.0)
                aas.append(a.astype(BF16))
                car_ref[e] += jnp.broadcast_to(sufs[e][:, 0:1], (bq, LANES))
            o_ref[...] += (_dot(aas[0], jnp.where(lane == 0, v_pair, 0))
                           + _dot(aas[1], jnp.where(lane == 1, v_pair, 0)))

        step(qi, True)

        def live():
            return jnp.max(jnp.maximum(car_ref[0], car_ref[1])) >= SB_DEAD

        def cond(c):
            return jnp.logical_and(c[0] < qi, c[1])

        def loop(c):
            step(qi - 1 - c[0], False)
            return c[0] + 1, live()

        done, _ = lax.while_loop(cond, loop, (jnp.int32(0), live()))
        tot_ref[...] = jnp.where(lane == 0, car_ref[0], car_ref[1])
        cnt_ref[pl.program_id(0), qi] = done.astype(F32)

    return pl.pallas_call(
        body, name="sb_fwd", grid=(2, nq), in_specs=_sb_specs(T, bq),
        out_specs=[pl.BlockSpec((bq, LANES), lambda j, i: (i, j)), pl.BlockSpec((bq, LANES), lambda j, i: (i, j)),
                   pl.BlockSpec(memory_space=pltpu.SMEM)],
        out_shape=[jax.ShapeDtypeStruct((T, 256), F32), jax.ShapeDtypeStruct((T, 256), F32),
                   jax.ShapeDtypeStruct((2, nq), F32)],
        scratch_shapes=[pltpu.VMEM((2, bq, LANES), BF16), pltpu.VMEM((2, bq, LANES), F32)],
        compiler_params=_cparams(("parallel", "arbitrary")),
    )(hb, hb, hb)


def _sb_bwd(hb, tot, cnt, dy):
    T = hb.shape[0]
    bq = bk = ATT_BLK
    nq = T // bq

    def body(q_ref, k_ref, v_ref, tot_ref, dy_ref, cnt_ref, dq_ref, dk_ref, dv_ref, qm_ref, dob_ref, dqa_ref, rem_ref,
             cg_ref):
        qi = pl.program_id(1)

        @pl.when(qi == 0)
        def _():
            dk_ref[...] = jnp.zeros_like(dk_ref)
            dv_ref[...] = jnp.zeros_like(dv_ref)

        lane = lax.broadcasted_iota(jnp.int32, (1, LANES), 1) // HEAD
        row, col = _causal_masks(bq, bk)
        strict = col < row
        u = _prefix_ones(bk)
        tot = tot_ref[...]
        tot_other = pltpu.roll(tot, HEAD, 1)
        dqa_ref[...] = jnp.zeros_like(dqa_ref)
        cg_ref[...] = jnp.zeros_like(cg_ref)
        for e in range(2):
            qm_ref[e] = jnp.where(lane == e, q_ref[...], 0) * 0.125
            dob_ref[e] = jnp.where(lane == e, dy_ref[...], 0.0).astype(BF16)
            rem_ref[e] = jnp.where(lane == e, tot, tot_other)

        def step(kb, masked):
            rows = pl.ds(pl.multiple_of(kb * bk, bk), bk)
            k_pair, v_pair = k_ref[rows, :], v_ref[rows, :]
            tile = lambda a: jnp.concatenate([a] * (bk // LANES), axis=1)
            zs = [_dot_nt(qm_ref[e], k_pair) for e in range(2)]
            das = [_dot_nt(dob_ref[e], jnp.where(lane == e, v_pair, 0)) for e in range(2)]
            zls, splits = [], []
            for e in range(2):
                z = zs[e]
                lk = jnp.minimum(-z, 0.0) - jnp.log(1.0 + jnp.exp(-jnp.abs(z)))
                if masked:
                    lk = jnp.where(strict, lk, 0.0)
                zls.append(z + lk)
                splits.append(_split(lk))
            pres = [_dot(hi, u) + _dot(lo, u) for hi, lo in splits]
            aas, gs, gsplits = [], [], []
            for e in range(2):
                a = jnp.exp(zls[e] + (tile(rem_ref[e]) - pres[e]))
                if masked:
                    a = jnp.where(strict, a, 0.0)
                g = a * das[e]
                aas.append(a.astype(BF16))
                gs.append(g)
                gsplits.append(_split(g))
                rem_ref[e] -= jnp.broadcast_to(pres[e][:, bk - 1:bk], (bq, LANES))
            dv_ref[rows, :] += _dot_tn(aas[0], dob_ref[0]) + _dot_tn(aas[1], dob_ref[1])
            gpres = [_dot(hi, u) + _dot(lo, u) for hi, lo in gsplits]
            dzs = []
            for e in range(2):
                dz = gs[e] - jnp.exp(zls[e]) * (tile(cg_ref[e]) + gpres[e])
                if masked:
                    dz = jnp.where(strict, dz, 0.0)
                dzs.append(dz.astype(BF16))
                cg_ref[e] += jnp.broadcast_to(gpres[e][:, bk - 1:bk], (bq, LANES))
            dk_ref[rows, :] += _dot_tn(dzs[0], qm_ref[0]) + _dot_tn(dzs[1], qm_ref[1])
            for e in range(2):
                dqa_ref[e] += _dot(dzs[e], k_pair)

        def loop(kb, c):
            step(kb, False)
            return c

        start = qi - jnp.clip(cnt_ref[pl.program_id(0), qi].astype(jnp.int32), 0, qi)
        lax.fori_loop(start, qi, loop, 0)
        step(qi, True)
        dq_ref[...] = jnp.where(lane == 0, dqa_ref[0], dqa_ref[1]) * 0.125

    return pl.pallas_call(
        body, name="sb_bwd", grid=(2, nq),
        in_specs=_sb_specs(T, bq) + [pl.BlockSpec((bq, LANES), lambda j, i: (i, j)),
                                     pl.BlockSpec((bq, LANES), lambda j, i: (i, j)),
                                     pl.BlockSpec(memory_space=pltpu.SMEM)],
        out_specs=[pl.BlockSpec((bq, LANES), lambda j, i: (i, j)),
                   pl.BlockSpec((T, LANES), lambda j, i: (0, j)),
                   pl.BlockSpec((T, LANES), lambda j, i: (0, j))],
        out_shape=[jax.ShapeDtypeStruct((T, 256), F32)] * 3,
        scratch_shapes=[pltpu.VMEM((2, bq, LANES), BF16), pltpu.VMEM((2, bq, LANES), BF16),
                        pltpu.VMEM((2, bq, LANES), F32), pltpu.VMEM((2, bq, LANES), F32),
                        pltpu.VMEM((2, bq, LANES), F32)],
        compiler_params=_cparams(("parallel", "arbitrary")),
    )(hb, hb, hb, tot, dy, cnt)


EP_TM = 256


def _ep_in_specs(tm, rev):
    idx = (lambda i: rev - i) if rev is not None else (lambda i: i)
    bo = (_INT_OFF["b_b"] - N_HB) // 256
    halo = lambda i: jnp.maximum(idx(i) * (tm // 8) - 1, 0)
    return [pl.BlockSpec((tm, 256), lambda i: (idx(i), 0)),
            pl.BlockSpec((tm, 256), lambda i: (idx(i), 0)),
            pl.BlockSpec((tm, 256), lambda i: (idx(i), 0)),
            pl.BlockSpec((tm, D_MODEL), lambda i: (idx(i), 0)),
            pl.BlockSpec((tm, 256), lambda i: (idx(i), bo)),
            pl.BlockSpec((tm, 256), lambda i: (idx(i), bo + 1)),
            pl.BlockSpec((tm, 256), lambda i: (idx(i), bo + 2)),
            pl.BlockSpec((8, 256), lambda i: (halo(i), bo + 1)),
            pl.BlockSpec((8, 256), lambda i: (halo(i), bo + 2)),
            pl.BlockSpec((3, 256), lambda i: (0, 0)),
            pl.BlockSpec((1, 256), lambda i: (0, 0)),
            pl.BlockSpec((1, D_MODEL), lambda i: (0, 0)),
            pl.BlockSpec((D_MODEL, D_MODEL), lambda i: (0, 0)),
            pl.BlockSpec((1, D_MODEL), lambda i: (0, 0))]


def _ep_mix(first, ya_ref, yc_ref, yd_ref, gate_ref, bb_ref, bc_ref, bx_ref, hc_ref, hx_ref, cw_ref, cb_ref, gg_ref):
    tm = ya_ref.shape[0]
    u = bc_ref[...] * bx_ref[...]
    halo = jnp.where(first, 0.0, hc_ref[...] * hx_ref[...])
    row = lax.broadcasted_iota(jnp.int32, (tm, 1), 0)
    u1 = jnp.where(row == 0, halo[7:8, :], pltpu.roll(u, 1, 0))
    u2 = jnp.where(row == 0, halo[6:7, :], jnp.where(row == 1, halo[7:8, :], pltpu.roll(u, 2, 0)))
    cw = cw_ref[...]
    conv = cw[0:1, :] * u2 + cw[1:2, :] * u1 + cw[2:3, :] * u + cb_ref[...]
    bb = bb_ref[...]
    ys = [ya_ref[...], bb * conv, yc_ref[...], yd_ref[...]]
    rs = [_rms(y) for y in ys]
    gg = gg_ref[...]
    yhat = jnp.concatenate([y * r for y, r in zip(ys, rs)], axis=1)
    gate = gate_ref[...]
    sig = 1.0 / (1.0 + jnp.exp(-gate))
    return u, u1, u2, conv, bb, rs, yhat, yhat * gg, gate, sig


def _epilogue_fwd(x, ya, yc, yd, hf, conv_w, conv_b, g_grp, w_out, g_post):
    T = x.shape[0]
    tm = EP_TM

    def body(x_ref, ya_ref, yc_ref, yd_ref, gate_ref, bb_ref, bc_ref, bx_ref, hc_ref, hx_ref, cw_ref, cb_ref,
             gg_ref, wo_ref, gp_ref, o_ref):
        (_, _, _, _, _, _, _, yn, gate, sig) = _ep_mix(
            pl.program_id(0) == 0, ya_ref, yc_ref, yd_ref, gate_ref, bb_ref, bc_ref, bx_ref, hc_ref, hx_ref,
            cw_ref, cb_ref, gg_ref)
        z = _dot((yn * (gate * sig)).astype(BF16), wo_ref[...])
        o_ref[...] = x_ref[...] + z * _rms(z) * gp_ref[...]

    return pl.pallas_call(
        body, name="epilogue_fwd", grid=(T // tm,),
        in_specs=[pl.BlockSpec((tm, D_MODEL), lambda i: (i, 0))] + _ep_in_specs(tm, None),
        out_specs=pl.BlockSpec((tm, D_MODEL), lambda i: (i, 0)),
        out_shape=jax.ShapeDtypeStruct((T, D_MODEL), F32),
        compiler_params=_cparams(("parallel",)),
    )(x, ya, yc, yd, hf, hf, hf, hf, hf, hf, conv_w, conv_b, g_grp, w_out, g_post)


def _epilogue_bwd(dxn, ya, yc, yd, hf, conv_w, conv_b, g_grp, w_out, g_post):
    T = dxn.shape[0]
    tm = EP_TM
    nt = T // tm
    ridx = lambda i: (nt - 1 - i, 0)

    def body(dx_ref, ya_ref, yc_ref, yd_ref, gate_ref, bb_ref, bc_ref, bx_ref, hc_ref, hx_ref, cw_ref, cb_ref,
             gg_ref, wo_ref, gp_ref,
             dya_ref, dyc_ref, dyd_ref, dhf_ref, dwo_ref, dgp_ref, dgg_ref, dcw_ref, dcb_ref, carry_ref):
        i = pl.program_id(0)

        @pl.when(i == 0)
        def _():
            for r in (dwo_ref, dgp_ref, dgg_ref, dcw_ref, dcb_ref, carry_ref):
                r[...] = jnp.zeros_like(r)

        (u, u1, u2, conv, bb, rs, yhat, yn, gate, sig) = _ep_mix(
            i == nt - 1, ya_ref, yc_ref, yd_ref, gate_ref, bb_ref, bc_ref, bx_ref, hc_ref, hx_ref,
            cw_ref, cb_ref, gg_ref)
        silu = gate * sig
        ymix = (yn * silu).astype(BF16)
        z = _dot(ymix, wo_ref[...])
        rz = _rms(z)
        dz, dgrow = _rms_bwd(dx_ref[...], z * rz, rz, gp_ref[...])
        dgp_ref[...] += _colsum(dgrow)
        dzb = dz.astype(BF16)
        dwo_ref[...] += _dot_tn(ymix, dzb)
        dymix = _dot_nt(dzb, wo_ref[...])
        dhf_ref[:, 0:D_MODEL] = dymix * yn * (sig * (1.0 + gate * (1.0 - sig)))
        dyn = dymix * silu
        dgg_ref[...] += _colsum(dyn * yhat)
        gg = gg_ref[...]
        dys = []
        for gi in range(4):
            sl = slice(gi * GROUP, (gi + 1) * GROUP)
            dyh = dyn[:, sl] * gg[:, sl]
            yh = yhat[:, sl]
            dys.append(rs[gi] * (dyh - yh * jnp.mean(dyh * yh, axis=-1, keepdims=True)))
        dya_ref[...] = dys[0]
        dyc_ref[...] = dys[2]
        dyd_ref[...] = dys[3]
        dyb = dys[1]
        dhf_ref[:, D_MODEL:D_MODEL + 256] = dyb * conv
        dconv = dyb * bb
        dcb_ref[...] += _colsum(dconv)
        dcw_ref[0:1, :] += _colsum(dconv * u2)
        dcw_ref[1:2, :] += _colsum(dconv * u1)
        dcw_ref[2:3, :] += _colsum(dconv * u)
        carry = carry_ref[...]
        row = lax.broadcasted_iota(jnp.int32, (tm, 1), 0)
        d1 = jnp.where(row == tm - 1, carry[0:1, :], pltpu.roll(dconv, tm - 1, 0))
        d2 = jnp.where(row == tm - 2, carry[0:1, :],
                       jnp.where(row == tm - 1, carry[1:2, :], pltpu.roll(dconv, tm - 2, 0)))
        cw = cw_ref[...]
        du = cw[2:3, :] * dconv + cw[1:2, :] * d1 + cw[0:1, :] * d2
        dhf_ref[:, D_MODEL + 256:D_MODEL + 512] = du * bx_ref[...]
        dhf_ref[:, D_MODEL + 512:D_MODEL + 768] = du * bc_ref[...]
        carry_ref[...] = dconv[0:8, :]

    in_specs = [pl.BlockSpec((tm, D_MODEL), ridx)] + _ep_in_specs(tm, nt - 1)
    return pl.pallas_call(
        body, name="epilogue_bwd", grid=(nt,), in_specs=in_specs,
        out_specs=[pl.BlockSpec((tm, 256), ridx), pl.BlockSpec((tm, 256), ridx), pl.BlockSpec((tm, 256), ridx),
                   pl.BlockSpec((tm, D_MODEL + 768), ridx),
                   pl.BlockSpec((D_MODEL, D_MODEL), lambda i: (0, 0)),
                   pl.BlockSpec((1, D_MODEL), lambda i: (0, 0)),
                   pl.BlockSpec((1, D_MODEL), lambda i: (0, 0)),
                   pl.BlockSpec((8, 256), lambda i: (0, 0)),
                   pl.BlockSpec((1, 256), lambda i: (0, 0))],
        out_shape=[jax.ShapeDtypeStruct((T, 256), F32)] * 3
                  + [jax.ShapeDtypeStruct((T, D_MODEL + 768), F32),
                     jax.ShapeDtypeStruct((D_MODEL, D_MODEL), F32),
                     jax.ShapeDtypeStruct((1, D_MODEL), F32),
                     jax.ShapeDtypeStruct((1, D_MODEL), F32),
                     jax.ShapeDtypeStruct((8, 256), F32),
                     jax.ShapeDtypeStruct((1, 256), F32)],
        scratch_shapes=[pltpu.VMEM((8, 256), F32)],
        compiler_params=_cparams(("arbitrary",)),
    )(dxn, ya, yc, yd, hf, hf, hf, hf, hf, hf, conv_w, conv_b, g_grp, w_out, g_post)


def _loss_head(y, tgt):
    T = y.shape[0]
    tm = 512

    def body(y_ref, t_ref, dy_ref, l_ref):
        @pl.when(pl.program_id(0) == 0)
        def _():
            l_ref[...] = jnp.zeros_like(l_ref)

        d = y_ref[...] - t_ref[...]
        dy_ref[...] = d * (1.0 / D_MODEL)
        part = jnp.sum(jnp.sum(d * d, axis=1, keepdims=True), axis=0, keepdims=True)
        l_ref[...] += part * (0.5 / D_MODEL)

    return pl.pallas_call(
        body, name="loss_head", grid=(T // tm,),
        in_specs=[pl.BlockSpec((tm, D_MODEL), lambda i: (i, 0))] * 2,
        out_specs=[pl.BlockSpec((tm, D_MODEL), lambda i: (i, 0)), pl.BlockSpec((8, LANES), lambda i: (0, 0))],
        out_shape=[jax.ShapeDtypeStruct((T, D_MODEL), F32), jax.ShapeDtypeStruct((8, LANES), F32)],
        compiler_params=_cparams(("arbitrary",)),
    )(y, tgt)


def _place():
    return lax.axis_index("x"), lax.axis_index("y"), lax.axis_index("c")


def _other_chips(x, y):
    return [(1 - x, y), (x, 1 - y), (1 - x, 1 - y)]


HBM = pl.BlockSpec(memory_space=pl.ANY)


def _gather_weights(shards):
    n = len(shards)

    def body(*refs):
        ins, outs = refs[:n], refs[n:2 * n]
        ici_send, ici_recv, d2d_send, d2d_recv, local_sems = refs[2 * n:]
        x, y, c = _place()
        me = 2 * x + y
        chips = _other_chips(x, y)

        def ici(a, j, layer_from):
            px, py = chips[j]
            return pltpu.make_async_remote_copy(
                src_ref=ins[a].at[c], dst_ref=outs[a].at[layer_from, c], send_sem=ici_send.at[3 * a + j],
                recv_sem=ici_recv.at[3 * a + j], device_id=(px, py, c), device_id_type=MESH)

        def d2d(a, j, layer):
            px, py = chips[j]
            blk = outs[a].at[2 * px + py, layer]
            return pltpu.make_async_remote_copy(
                src_ref=blk, dst_ref=blk, send_sem=d2d_send.at[3 * a + j], recv_sem=d2d_recv.at[3 * a + j],
                device_id=(x, y, 1 - c), device_id_type=MESH)

        local = [pltpu.make_async_copy(ins[a], outs[a].at[me], local_sems.at[a]) for a in range(n)]
        for cp in local:
            cp.start()
        sends = [ici(a, j, me) for j in range(3) for a in range(n)]
        for cp in sends:
            cp.start()
        for j in range(3):
            px, py = chips[j]
            for a in range(n):
                ici(a, j, 2 * px + py).wait_recv()
                fwd = d2d(a, j, c)
                fwd.start()
                sends.append(fwd)
        for j in range(3):
            for a in range(n):
                d2d(a, j, 1 - c).wait_recv()
        for cp in sends:
            cp.wait_send()
        for cp in local:
            cp.wait()

    return pl.pallas_call(
        body, name="gather_weights",
        in_specs=[HBM] * n, out_specs=[HBM] * n,
        out_shape=[jax.ShapeDtypeStruct((4,) + s.shape, s.dtype) for s in shards],
        scratch_shapes=[pltpu.SemaphoreType.DMA((3 * n,))] * 4 + [pltpu.SemaphoreType.DMA((n,))],
    )(*shards)


def _exchange_chips(parts, small):
    n = len(parts)

    def body(*refs):
        ins, sm_ref = refs[:n], refs[n]
        outs, osm_ref = refs[n + 1:2 * n + 1], refs[2 * n + 1]
        send_sems, recv_sems, ssend_sems, srecv_sems, local_sems = refs[2 * n + 2:]
        x, y, c = _place()
        me = 2 * x + y
        dev = 4 * x + 2 * y + c
        local = [pltpu.make_async_copy(ins[a].at[me], outs[a].at[me], local_sems.at[a]) for a in range(n)]
        local.append(pltpu.make_async_copy(sm_ref, osm_ref.at[dev], local_sems.at[n]))
        for cp in local:
            cp.start()
        sends = []
        for j, (px, py) in enumerate(_other_chips(x, y)):
            for a in range(n):
                cp = pltpu.make_async_remote_copy(
                    src_ref=ins[a].at[2 * px + py], dst_ref=outs[a].at[me], send_sem=send_sems.at[3 * a + j],
                    recv_sem=recv_sems.at[3 * a + j], device_id=(px, py, c), device_id_type=MESH)
                cp.start()
                sends.append(cp)
        flips = [(fx, fy, fc) for fx in (0, 1) for fy in (0, 1) for fc in (0, 1)][1:]
        for j, (fx, fy, fc) in enumerate(flips):
            cp = pltpu.make_async_remote_copy(
                src_ref=sm_ref, dst_ref=osm_ref.at[dev], send_sem=ssend_sems.at[j], recv_sem=srecv_sems.at[j],
                device_id=(x ^ fx, y ^ fy, c ^ fc), device_id_type=MESH)
            cp.start()
            sends.append(cp)
        for j, (px, py) in enumerate(_other_chips(x, y)):
            for a in range(n):
                pltpu.make_async_remote_copy(
                    src_ref=ins[a].at[me], dst_ref=outs[a].at[2 * px + py], send_sem=send_sems.at[3 * a + j],
                    recv_sem=recv_sems.at[3 * a + j], device_id=(px, py, c), device_id_type=MESH).wait_recv()
        for j, (fx, fy, fc) in enumerate(flips):
            src = 4 * (x ^ fx) + 2 * (y ^ fy) + (c ^ fc)
            pltpu.make_async_remote_copy(
                src_ref=sm_ref, dst_ref=osm_ref.at[src], send_sem=ssend_sems.at[j], recv_sem=srecv_sems.at[j],
                device_id=(x ^ fx, y ^ fy, c ^ fc), device_id_type=MESH).wait_recv()
        for cp in sends:
            cp.wait_send()
        for cp in local:
            cp.wait()

    return pl.pallas_call(
        body, name="exchange_chips",
        in_specs=[HBM] * (n + 1), out_specs=[HBM] * (n + 1),
        out_shape=[jax.ShapeDtypeStruct(p.shape, p.dtype) for p in parts]
                  + [jax.ShapeDtypeStruct((8,) + small.shape, small.dtype)],
        scratch_shapes=[pltpu.SemaphoreType.DMA((3 * n,)), pltpu.SemaphoreType.DMA((3 * n,)),
                        pltpu.SemaphoreType.DMA((7,)), pltpu.SemaphoreType.DMA((7,)),
                        pltpu.SemaphoreType.DMA((n + 1,))],
    )(*parts, small)


def _swap_cores(parts, name):
    n = len(parts)

    def body(*refs):
        ins, outs, send_sems, recv_sems = refs[:n], refs[n:2 * n], refs[2 * n], refs[2 * n + 1]
        x, y, c = _place()
        copies = [pltpu.make_async_remote_copy(
            src_ref=ins[a], dst_ref=outs[a], send_sem=send_sems.at[a], recv_sem=recv_sems.at[a],
            device_id=(x, y, 1 - c), device_id_type=MESH) for a in range(n)]
        for cp in copies:
            cp.start()
        for cp in copies:
            cp.wait()

    return pl.pallas_call(
        body, name=name, in_specs=[HBM] * n, out_specs=[HBM] * n,
        out_shape=[jax.ShapeDtypeStruct(p.shape, p.dtype) for p in parts],
        scratch_shapes=[pltpu.SemaphoreType.DMA((n,)), pltpu.SemaphoreType.DMA((n,))],
    )(*parts)


def _row_block(rows):
    for cand in (256, 128, 64, 32, 16, 8):
        if rows % cand == 0:
            return cand
    return rows


def _add(a, b, name):
    L, R, C = a.shape
    tr = _row_block(R)

    def body(a_ref, b_ref, o_ref):
        o_ref[...] = a_ref[...] + b_ref[...]

    spec = pl.BlockSpec((1, tr, C), lambda l, i: (l, i, 0))
    return pl.pallas_call(
        body, name=name, grid=(L, R // tr), in_specs=[spec, spec], out_specs=spec,
        out_shape=jax.ShapeDtypeStruct((L, R, C), F32), compiler_params=_cparams(("parallel", "parallel")),
    )(a, b)


def _sum_leading(buf, name):
    n, R, C = buf.shape
    tr = _row_block(R)

    def body(b_ref, o_ref):
        acc = b_ref[0]
        for k in range(1, n):
            acc = acc + b_ref[k]
        o_ref[...] = acc

    return pl.pallas_call(
        body, name=name, grid=(R // tr,),
        in_specs=[pl.BlockSpec((n, tr, C), lambda i: (0, i, 0))],
        out_specs=pl.BlockSpec((tr, C), lambda i: (i, 0)),
        out_shape=jax.ShapeDtypeStruct((R, C), F32),
        compiler_params=_cparams(("parallel",)),
    )(buf)


def _adam_update(w, g, m, v):
    c1 = 1.0 / (1.0 - ADAM_B1 ** ADAM_STEP)
    c2 = 1.0 / (1.0 - ADAM_B2 ** ADAM_STEP)
    mn = ADAM_B1 * m + (1.0 - ADAM_B1) * g
    vn = ADAM_B2 * v + (1.0 - ADAM_B2) * (g * g)
    return -ADAM_LR * ((mn * c1) / (jnp.sqrt(vn * c2) + ADAM_EPS) + ADAM_WD * w), mn, vn


def _adamw_layers(w, m, v, g_mine, g_other, name):
    _, R, C = w.shape
    tr = _row_block(R)

    def body(w_ref, m_ref, v_ref, gm_ref, go_ref, g_ref, d_ref, mo_ref, vo_ref):
        g = jnp.where(pl.program_id(0) == lax.axis_index("c"), gm_ref[...], go_ref[...])
        g_ref[0] = g
        d_ref[0], mo_ref[0], vo_ref[0] = _adam_update(w_ref[0], g, m_ref[0], v_ref[0])

    spec3 = pl.BlockSpec((1, tr, C), lambda l, i: (l, i, 0))
    spec2 = pl.BlockSpec((tr, C), lambda l, i: (i, 0))
    return pl.pallas_call(
        body, name=name, grid=(2, R // tr),
        in_specs=[spec3] * 3 + [spec2] * 2, out_specs=[spec3] * 4,
        out_shape=[jax.ShapeDtypeStruct(w.shape, F32)] * 4,
        compiler_params=_cparams(("parallel", "parallel")),
    )(w, m, v, g_mine, g_other)


PACK_C = 1024
_BIG = ("w_in", "w_out", "mla_w_uq", "mla_w_ukv", "conv_w")
_SMALL = ("norm_pre", "group_norm", "norm_post", "conv_b", "mla_q_norm", "mla_kv_norm", "attn_sinks")
_SMALL_W = {"norm_pre": 1024, "group_norm": 1024, "norm_post": 1024, "conv_b": 256, "mla_q_norm": 256,
            "mla_kv_norm": 128, "attn_sinks": 4}


def _pack_small(d):
    flat = jnp.concatenate([d[n].reshape(-1) for n in _SMALL])
    return jnp.pad(flat, (0, 8 * PACK_C - flat.shape[0])).reshape(8, PACK_C)


def _adamw_small(w, m, v, got):
    ns = len(_SMALL)

    def body(*refs):
        got_ref = refs[3 * ns]
        outs = refs[3 * ns + 1:]
        gsum = got_ref[0]
        for d in range(1, 8):
            gsum = gsum + got_ref[d]
        off = 0
        for i, name in enumerate(_SMALL):
            wd = _SMALL_W[name]
            rows = []
            for l in range(DEPTH):
                r, c0 = divmod(off + l * wd, PACK_C)
                rows.append(gsum[r:r + 1, c0:c0 + wd])
            off += DEPTH * wd
            g = jnp.concatenate(rows, axis=0)
            delta, mn, vn = _adam_update(refs[i][...], g, refs[ns + i][...], refs[2 * ns + i][...])
            outs[i][...] = g
            outs[ns + i][...] = delta
            outs[2 * ns + i][...] = mn
            outs[3 * ns + i][...] = vn

    shapes = [jax.ShapeDtypeStruct(w[n].shape, F32) for n in _SMALL]
    res = pl.pallas_call(body, name="adamw_small", out_shape=shapes * 4)(
        *[w[n] for n in _SMALL], *[m[n] for n in _SMALL], *[v[n] for n in _SMALL], got)
    return [dict(zip(_SMALL, res[k * ns:(k + 1) * ns])) for k in range(4)]


def _w_in_internal(w):
    cols = []
    for n in _INT_ORDER:
        o, wd = _REAL_OFF[n]
        cols.append(w[:, o:o + wd])
        if _INT_W[n] != wd:
            cols.append(jnp.zeros((w.shape[0], _INT_W[n] - wd), w.dtype))
    return jnp.concatenate(cols, axis=1)


def _w_in_real(dw):
    return jnp.concatenate([dw[:, _INT_OFF[n]:_INT_OFF[n] + wd] for n, wd in _REAL], axis=1)


def _uq_internal(w):
    return jnp.pad(w.reshape(256, 4, 96), ((0, 0), (0, 0), (0, 32))).reshape(256, 512)


def _uq_real(dw):
    return dw.reshape(256, 4, 128)[:, :, :96].reshape(256, 384)


def _ukv_internal(w):
    w4 = w.reshape(128, 4, 128)
    k = jnp.pad(w4[:, :, :64], ((0, 0), (0, 0), (0, 64))).reshape(128, 512)
    return jnp.concatenate([k, w4[:, :, 64:].reshape(128, 256)], axis=1)


def _ukv_real(dw):
    k = dw[:, :512].reshape(128, 4, 128)[:, :, :64]
    v = dw[:, 512:].reshape(128, 4, 64)
    return jnp.concatenate([k, v], axis=2).reshape(128, 512)


def _layer_fwd(x, pos, p):
    xn, hb, hf = _inproj_fwd(x, p["norm_pre"], p["w_in"])
    ya = _swa_fwd(hb, p["attn_sinks"])
    qm, km, vm = _mla_prep_fwd(hf, pos, p["mla_q_norm"], p["mla_kv_norm"], p["mla_w_uq"], p["mla_w_ukv"])
    yc, lse = _mla_fwd(qm, km, vm)
    yd, tot, cnt = _sb_fwd(hb)
    x_next = _epilogue_fwd(x, ya, yc, yd, hf, p["conv_w"], p["conv_b"], p["group_norm"], p["w_out"], p["norm_post"])
    return x_next, dict(x=x, xn=xn, hb=hb, hf=hf, ya=ya, yc=yc, yd=yd, tot=tot, cnt=cnt, qm=qm, km=km, vm=vm, lse=lse)


def _layer_bwd(dx_next, pos, p, s):
    (dya, dyc, dyd, dhf, dw_out, dg_post, dg_grp, dconv_w, dconv_b) = _epilogue_bwd(
        dx_next, s["ya"], s["yc"], s["yd"], s["hf"], p["conv_w"], p["conv_b"], p["group_norm"], p["w_out"],
        p["norm_post"])
    dq_d, dk_d, dv_d = _sb_bwd(s["hb"], s["tot"], s["cnt"], dyd)
    dqm, dkm, dvm = _mla_bwd(s["qm"], s["km"], s["vm"], s["yc"], s["lse"], dyc)
    dc, dw_uq, dw_ukv, dg_q, dg_kv = _mla_prep_bwd(
        s["hf"], pos, p["mla_q_norm"], p["mla_kv_norm"], p["mla_w_uq"], p["mla_w_ukv"], dqm, dkm, dvm)
    dq_a, dk_a, dv_a, dsinks = _swa_bwd(s["hb"], p["attn_sinks"], dya)
    dx, dh, dg_pre = _inproj_bwd_dx(s["x"], p["norm_pre"], p["w_in"], dx_next,
                                    [dq_a, dk_a, dv_a, dq_d, dk_d, dv_d, dhf, dc])
    dw_in = _matmul_tn(s["xn"], dh, "inproj_bwd_dw")
    grads = dict(norm_pre=dg_pre[0], w_in=_w_in_real(dw_in), attn_sinks=dsinks[0, :4], conv_w=dconv_w[:3],
                 conv_b=dconv_b[0], mla_q_norm=dg_q[0], mla_w_uq=_uq_real(dw_uq), mla_kv_norm=dg_kv[0],
                 mla_w_ukv=_ukv_real(dw_ukv), group_norm=dg_grp[0], w_out=dw_out, norm_post=dg_post[0])
    return dx, grads


_WEIGHTS = ["norm_pre", "w_in", "attn_sinks", "conv_w", "conv_b", "mla_q_norm", "mla_w_uq", "mla_kv_norm",
            "mla_w_ukv", "group_norm", "w_out", "norm_post"]


def kernel(x, positions, norm_pre, w_in, attn_sinks, conv_w, conv_b, mla_q_norm, mla_w_uq, mla_kv_norm, mla_w_ukv, group_norm, w_out, norm_post, loss_target, m_norm_pre, m_w_in, m_attn_sinks, m_conv_w, m_conv_b, m_mla_q_norm, m_mla_w_uq, m_mla_kv_norm, m_mla_w_ukv, m_group_norm, m_w_out, m_norm_post, v_norm_pre, v_w_in, v_attn_sinks, v_conv_w, v_conv_b, v_mla_q_norm, v_mla_w_uq, v_mla_kv_norm, v_mla_w_ukv, v_group_norm, v_w_out, v_norm_post):
    w = dict(norm_pre=norm_pre, w_in=w_in, attn_sinks=attn_sinks, conv_w=conv_w, conv_b=conv_b,
             mla_q_norm=mla_q_norm, mla_w_uq=mla_w_uq, mla_kv_norm=mla_kv_norm, mla_w_ukv=mla_w_ukv,
             group_norm=group_norm, w_out=w_out, norm_post=norm_post)
    m = dict(norm_pre=m_norm_pre, w_in=m_w_in, attn_sinks=m_attn_sinks, conv_w=m_conv_w, conv_b=m_conv_b,
             mla_q_norm=m_mla_q_norm, mla_w_uq=m_mla_w_uq, mla_kv_norm=m_mla_kv_norm, mla_w_ukv=m_mla_w_ukv,
             group_norm=m_group_norm, w_out=m_w_out, norm_post=m_norm_post)
    v = dict(norm_pre=v_norm_pre, w_in=v_w_in, attn_sinks=v_attn_sinks, conv_w=v_conv_w, conv_b=v_conv_b,
             mla_q_norm=v_mla_q_norm, mla_w_uq=v_mla_w_uq, mla_kv_norm=v_mla_kv_norm, mla_w_ukv=v_mla_w_ukv,
             group_norm=v_group_norm, w_out=v_w_out, norm_post=v_norm_post)
    T = x.shape[1]
    xs = x[0]
    pos = positions[0].reshape(T, 1)
    tgt = loss_target[0]
    core = lax.axis_index("c")

    gathered = _gather_weights([w[n].astype(BF16) for n in _BIG[:4]] + [w["conv_w"]])
    full = {}
    for n, got in zip(_BIG, gathered):
        if n == "w_out":
            full[n] = jnp.moveaxis(got, 0, 1).reshape(DEPTH, D_MODEL, D_MODEL)
        else:
            full[n] = jnp.transpose(got, (1, 2, 0, 3)).reshape(DEPTH, got.shape[2], 4 * got.shape[3])

    layers = []
    for l in range(DEPTH):
        layers.append(dict(
            norm_pre=norm_pre[l:l + 1], w_in=_w_in_internal(full["w_in"][l]), attn_sinks=attn_sinks[l],
            conv_w=full["conv_w"][l], conv_b=conv_b[l:l + 1], mla_q_norm=mla_q_norm[l:l + 1],
            mla_w_uq=_uq_internal(full["mla_w_uq"][l]), mla_kv_norm=mla_kv_norm[l:l + 1],
            mla_w_ukv=_ukv_internal(full["mla_w_ukv"][l]), group_norm=group_norm[l:l + 1],
            w_out=full["w_out"][l], norm_post=norm_post[l:l + 1]))

    saved = []
    h = xs
    for l in range(DEPTH):
        h, s = _layer_fwd(h, pos, layers[l])
        saved.append(s)
    dy, loss_part = _loss_head(h, tgt)
    loss = lax.psum(loss_part[0, 0], ("x", "y", "c"))

    grads = [None] * DEPTH
    for l in reversed(range(DEPTH)):
        dy, grads[l] = _layer_bwd(dy, pos, layers[l], saved[l])

    def chunks(n, a):
        if n == "w_out":
            return a.reshape(4, D_MODEL // 4, D_MODEL)
        return jnp.transpose(a.reshape(a.shape[0], 4, a.shape[1] // 4), (1, 0, 2))

    mine = [chunks(n, jnp.where(core == 0, grads[0][n], grads[1][n])) for n in _BIG]
    theirs = [chunks(n, jnp.where(core == 0, grads[1][n], grads[0][n])) for n in _BIG]
    from_sibling = _swap_cores(theirs, "swap_layer_chunks")
    summed = [_add(a, b, "add_cores_" + n) for n, a, b in zip(_BIG, mine, from_sibling)]
    small = _pack_small({n: jnp.stack([grads[l][n] for l in range(DEPTH)]) for n in _SMALL})
    *got, got_small = _exchange_chips(summed, small)
    done = [_sum_leading(b, "sum_chips_" + n) for n, b in zip(_BIG, got)]
    done_other = _swap_cores(done, "swap_layer_shards")

    outs = _adamw_small(w, m, v, got_small)
    for n, gm, go in zip(_BIG, done, done_other):
        for d, a in zip(outs, _adamw_layers(w[n], m[n], v[n], gm, go, "adamw_" + n)):
            d[n] = a
    return (loss, dy[None], *[outs[0][n] for n in _WEIGHTS], *[outs[1][n] for n in _WEIGHTS],
            *[outs[2][n] for n in _WEIGHTS], *[outs[3][n] for n in _WEIGHTS])
```

```python
import math

import jax
import jax.numpy as jnp
from jax import lax
from jax.experimental import pallas as pl
from jax.experimental.pallas import tpu as pltpu

F32 = jnp.float32
BF16 = jnp.bfloat16
MESH = pl.DeviceIdType.MESH

D_MODEL = 1024
DEPTH = 2
EPS = 1e-6
BLOCK = 128
HEAD = 64
LANES = 128
GROUP = 256
MLA_SCALE = 96 ** -0.5
ROPE_HALF = 16
ROPE_THETA = 10000.0
ATT_BLK = 256
MLA_BQ = 512
NEG = -1e30
SB_DEAD = -104.0

ADAM_LR, ADAM_B1, ADAM_B2, ADAM_EPS, ADAM_WD, ADAM_STEP = 0.001, 0.9, 0.999, 1e-08, 0.01, 10

_REAL = [("a_q", 256), ("a_k", 128), ("a_v", 128), ("b_b", 256), ("b_c", 256), ("b_x", 256),
         ("c_q", 256), ("c_kv", 128), ("c_kr", 32), ("d_q", 256), ("d_k", 256), ("d_v", 256),
         ("gate", 1024)]
_REAL_OFF = {}
_o = 0
for _n, _w in _REAL:
    _REAL_OFF[_n] = (_o, _w)
    _o += _w
D_IN = _o
_INT_ORDER = ["a_q", "a_k", "a_v", "d_q", "d_k", "d_v", "gate", "b_b", "b_c", "b_x", "c_q", "c_kv", "c_kr"]
_INT_W = dict(_REAL)
_INT_W["c_kr"] = 128
_INT_OFF = {}
_o = 0
for _n in _INT_ORDER:
    _INT_OFF[_n] = _o
    _o += _INT_W[_n]
N_INT = _o
N_HB = _INT_OFF["gate"]
N_HF = N_INT - N_HB

VMEM_LIMIT = 56 * 1024 * 1024


def _cparams(sem):
    return pltpu.CompilerParams(dimension_semantics=sem, vmem_limit_bytes=VMEM_LIMIT)


def _dot(a, b):
    return jnp.dot(a, b, preferred_element_type=F32)


def _dot_nt(a, b):
    return lax.dot_general(a, b, (((1,), (1,)), ((), ())), preferred_element_type=F32)


def _dot_tn(a, b):
    return lax.dot_general(a, b, (((0,), (0,)), ((), ())), preferred_element_type=F32)


def _split(x):
    hi = x.astype(BF16)
    lo = (x - hi.astype(F32)).astype(BF16)
    return hi, lo


def _rms(x):
    return lax.rsqrt(jnp.mean(x * x, axis=-1, keepdims=True) + EPS)


def _rms_bwd(dy, xhat, r, g):
    dxhat = dy * g
    return r * (dxhat - xhat * jnp.mean(dxhat * xhat, axis=-1, keepdims=True)), dy * xhat


def _colsum(x):
    return jnp.sum(x, axis=0, keepdims=True)


def _inproj_fwd(x, g, w):
    T = x.shape[0]
    tm = 256

    def body(x_ref, g_ref, w_ref, xn_ref, hb_ref, hf_ref):
        xv = x_ref[...]
        xn = (xv * _rms(xv) * g_ref[...]).astype(BF16)
        xn_ref[...] = xn
        h = _dot(xn, w_ref[...])
        hb_ref[...] = h[:, :N_HB].astype(BF16)
        hf_ref[...] = h[:, N_HB:]

    return pl.pallas_call(
        body, name="inproj_fwd", grid=(T // tm,),
        in_specs=[pl.BlockSpec((tm, D_MODEL), lambda i: (i, 0)),
                  pl.BlockSpec((1, D_MODEL), lambda i: (0, 0)),
                  pl.BlockSpec((D_MODEL, N_INT), lambda i: (0, 0))],
        out_specs=[pl.BlockSpec((tm, D_MODEL), lambda i: (i, 0)),
                   pl.BlockSpec((tm, N_HB), lambda i: (i, 0)),
                   pl.BlockSpec((tm, N_HF), lambda i: (i, 0))],
        out_shape=[jax.ShapeDtypeStruct((T, D_MODEL), BF16),
                   jax.ShapeDtypeStruct((T, N_HB), BF16),
                   jax.ShapeDtypeStruct((T, N_HF), F32)],
        compiler_params=_cparams(("parallel",)),
    )(x, g, w)


def _inproj_bwd_dx(x, g, w, dx_next, pieces):
    T = x.shape[0]
    tm = 256
    widths = [p.shape[1] for p in pieces]
    assert sum(widths) == N_INT

    def body(x_ref, g_ref, w_ref, dxn_ref, *rest):
        p_refs = rest[:len(pieces)]
        dx_ref, dh_ref, dg_ref = rest[len(pieces):]
        dh = jnp.concatenate([p[...].astype(BF16) for p in p_refs], axis=1)
        dh_ref[...] = dh
        dxn = _dot_nt(dh, w_ref[...])
        xv = x_ref[...]
        r = _rms(xv)
        dx, dgrow = _rms_bwd(dxn, xv * r, r, g_ref[...])
        dx_ref[...] = dx + dxn_ref[...]

        @pl.when(pl.program_id(0) == 0)
        def _():
            dg_ref[...] = jnp.zeros_like(dg_ref)

        dg_ref[...] += _colsum(dgrow)

    return pl.pallas_call(
        body, name="inproj_bwd_dx", grid=(T // tm,),
        in_specs=[pl.BlockSpec((tm, D_MODEL), lambda i: (i, 0)),
                  pl.BlockSpec((1, D_MODEL), lambda i: (0, 0)),
                  pl.BlockSpec((D_MODEL, N_INT), lambda i: (0, 0)),
                  pl.BlockSpec((tm, D_MODEL), lambda i: (i, 0))]
                 + [pl.BlockSpec((tm, wd), lambda i: (i, 0)) for wd in widths],
        out_specs=[pl.BlockSpec((tm, D_MODEL), lambda i: (i, 0)),
                   pl.BlockSpec((tm, N_INT), lambda i: (i, 0)),
                   pl.BlockSpec((1, D_MODEL), lambda i: (0, 0))],
        out_shape=[jax.ShapeDtypeStruct((T, D_MODEL), F32),
                   jax.ShapeDtypeStruct((T, N_INT), BF16),
                   jax.ShapeDtypeStruct((1, D_MODEL), F32)],
        compiler_params=_cparams(("arbitrary",)),
    )(x, g, w, dx_next, *pieces)


def _matmul_tn(a, b, name):
    T, M = a.shape
    N = b.shape[1]
    tm, tn = 512, 512

    def body(a_ref, b_ref, o_ref):
        @pl.when(pl.program_id(1) == 0)
        def _():
            o_ref[...] = jnp.zeros_like(o_ref)

        o_ref[...] += _dot_tn(a_ref[...], b_ref[...])

    return pl.pallas_call(
        body, name=name, grid=(N // tn, T // tm),
        in_specs=[pl.BlockSpec((tm, M), lambda j, t: (t, 0)),
                  pl.BlockSpec((tm, tn), lambda j, t: (t, j))],
        out_specs=pl.BlockSpec((M, tn), lambda j, t: (0, j)),
        out_shape=jax.ShapeDtypeStruct((M, N), F32),
        compiler_params=_cparams(("parallel", "arbitrary")),
    )(a, b)


def _roll_f32(x, shift):
    return pltpu.roll(x.astype(F32), shift, 1)


def _swa_operands(h, q_ref, kp_ref, kc_ref, vp_ref, vc_ref):
    p, e = h // 2, h % 2
    lane = lax.broadcasted_iota(jnp.int32, (1, LANES), 1) // HEAD
    q = q_ref[:, p * LANES:(p + 1) * LANES]
    k_prev, k_cur, v_prev, v_cur = kp_ref[...], kc_ref[...], vp_ref[...], vc_ref[...]
    if e != p:
        q = _roll_f32(q, HEAD).astype(BF16)
        v_prev = _roll_f32(v_prev, HEAD).astype(BF16)
        v_cur = _roll_f32(v_cur, HEAD).astype(BF16)
    qs = jnp.where(lane == p, q, 0) * 0.125
    return dict(p=p, e=e, lane=lane, qs=qs, k_prev=k_prev, k_cur=k_cur,
                v_prev=jnp.where(lane == e, v_prev, 0), v_cur=jnp.where(lane == e, v_cur, 0),
                s_prev=_dot_nt(qs, k_prev), s_cur=_dot_nt(qs, k_cur))


def _swa_probs(ops, sink, first):
    row = lax.broadcasted_iota(jnp.int32, (BLOCK, BLOCK), 0)
    col = lax.broadcasted_iota(jnp.int32, (BLOCK, BLOCK), 1)
    s_prev = jnp.where(jnp.logical_and(col > row, jnp.logical_not(first)), ops["s_prev"], NEG)
    s_cur = jnp.where(col <= row, ops["s_cur"], NEG)
    m = jnp.maximum(jnp.maximum(jnp.max(s_prev, axis=1, keepdims=True),
                                jnp.max(s_cur, axis=1, keepdims=True)), sink)
    p_prev = jnp.exp(s_prev - m)
    p_cur = jnp.exp(s_cur - m)
    p_sink = jnp.exp(sink - m)
    inv = 1.0 / (jnp.sum(p_prev, axis=1, keepdims=True) + jnp.sum(p_cur, axis=1, keepdims=True) + p_sink)
    return p_prev * inv, p_cur * inv, p_sink * inv


def _swa_specs(T):
    nb = T // BLOCK
    qo, ko, vo = (_INT_OFF[n] // LANES for n in ("a_q", "a_k", "a_v"))
    prev = lambda i: jnp.maximum(i - 1, 0)
    return [pl.BlockSpec((BLOCK, 256), lambda i: (i, qo // 2)),
            pl.BlockSpec((BLOCK, LANES), lambda i: (prev(i), ko)),
            pl.BlockSpec((BLOCK, LANES), lambda i: (i, ko)),
            pl.BlockSpec((BLOCK, LANES), lambda i: (prev(i), vo)),
            pl.BlockSpec((BLOCK, LANES), lambda i: (i, vo)),
            pl.BlockSpec(memory_space=pltpu.SMEM)], nb


def _swa_fwd(hb, sinks):
    T = hb.shape[0]
    specs, nb = _swa_specs(T)

    def body(q_ref, kp_ref, kc_ref, vp_ref, vc_ref, s_ref, o_ref):
        first = pl.program_id(0) == 0
        ops = [_swa_operands(h, q_ref, kp_ref, kc_ref, vp_ref, vc_ref) for h in range(4)]
        probs = [_swa_probs(ops[h], s_ref[h], first) for h in range(4)]
        outs = [_dot(probs[h][0].astype(BF16), ops[h]["v_prev"]) + _dot(probs[h][1].astype(BF16), ops[h]["v_cur"])
                for h in range(4)]
        for p in range(2):
            o_ref[:, p * LANES:(p + 1) * LANES] = outs[2 * p] + outs[2 * p + 1]

    return pl.pallas_call(
        body, name="swa_fwd", grid=(nb,), in_specs=specs,
        out_specs=pl.BlockSpec((BLOCK, 256), lambda i: (i, 0)),
        out_shape=jax.ShapeDtypeStruct((T, 256), F32),
        compiler_params=_cparams(("parallel",)),
    )(hb, hb, hb, hb, hb, sinks)


def _swa_bwd(hb, sinks, dy):
    T = hb.shape[0]
    specs, nb = _swa_specs(T)

    def body(q_ref, kp_ref, kc_ref, vp_ref, vc_ref, s_ref, dy_ref, dq_ref, dk_ref, dv_ref, ds_ref):
        i = pl.program_id(0)
        first = i == 0
        cur = pl.ds(pl.multiple_of(i * BLOCK, BLOCK), BLOCK)
        prv = pl.ds(pl.multiple_of(jnp.maximum(i - 1, 0) * BLOCK, BLOCK), BLOCK)

        @pl.when(first)
        def _():
            ds_ref[...] = jnp.zeros_like(ds_ref)

        dk_ref[cur, :] = jnp.zeros((BLOCK, LANES), F32)
        dv_ref[cur, :] = jnp.zeros((BLOCK, LANES), F32)
        lane_id = lax.broadcasted_iota(jnp.int32, (8, LANES), 1)
        heads = range(4)
        ops = [_swa_operands(h, q_ref, kp_ref, kc_ref, vp_ref, vc_ref) for h in heads]
        probs = [_swa_probs(ops[h], s_ref[h], first) for h in heads]
        dos = [jnp.where(ops[h]["lane"] == ops[h]["e"], dy_ref[:, ops[h]["p"] * LANES:(ops[h]["p"] + 1) * LANES], 0.0)
               for h in heads]
        dobs = [d.astype(BF16) for d in dos]
        pbs = [(probs[h][0].astype(BF16), probs[h][1].astype(BF16)) for h in heads]
        outs = [_dot(pbs[h][0], ops[h]["v_prev"]) + _dot(pbs[h][1], ops[h]["v_cur"]) for h in heads]
        dps = [(_dot_nt(dobs[h], ops[h]["v_prev"]), _dot_nt(dobs[h], ops[h]["v_cur"])) for h in heads]
        dss, dsinks = [], jnp.zeros((8, LANES), F32)
        for h in heads:
            delta = jnp.sum(dos[h] * outs[h], axis=1, keepdims=True)
            dss.append(((probs[h][0] * (dps[h][0] - delta)).astype(BF16),
                        (probs[h][1] * (dps[h][1] - delta)).astype(BF16)))
            dsink = -jnp.sum(probs[h][2] * delta, axis=0, keepdims=True)
            dsinks += jnp.where(lane_id == h, dsink, 0.0)
        ds_ref[...] += dsinks
        dqs = [(_dot(dss[h][0], ops[h]["k_prev"]) + _dot(dss[h][1], ops[h]["k_cur"])) * 0.125 for h in heads]
        dk_prev = dk_cur = dv_prev = dv_cur = jnp.zeros((BLOCK, LANES), F32)
        for h in heads:
            p, e = ops[h]["p"], ops[h]["e"]
            dob_v = dobs[h] if e == p else pltpu.roll(dos[h], HEAD, 1).astype(BF16)
            dk_prev += _dot_tn(dss[h][0], ops[h]["qs"])
            dk_cur += _dot_tn(dss[h][1], ops[h]["qs"])
            dv_prev += _dot_tn(pbs[h][0], dob_v)
            dv_cur += _dot_tn(pbs[h][1], dob_v)
        dk_ref[prv, :] += dk_prev
        dk_ref[cur, :] += dk_cur
        dv_ref[prv, :] += dv_prev
        dv_ref[cur, :] += dv_cur
        for p in range(2):
            dq_pair = jnp.zeros((BLOCK, LANES), F32)
            for e in range(2):
                dq = jnp.where(ops[2 * p + e]["lane"] == p, dqs[2 * p + e], 0.0)
                dq_pair += dq if e == p else pltpu.roll(dq, HEAD, 1)
            dq_ref[:, p * LANES:(p + 1) * LANES] = dq_pair

    return pl.pallas_call(
        body, name="swa_bwd", grid=(nb,),
        in_specs=specs + [pl.BlockSpec((BLOCK, 256), lambda i: (i, 0))],
        out_specs=[pl.BlockSpec((BLOCK, 256), lambda i: (i, 0)),
                   pl.BlockSpec((T, LANES), lambda i: (0, 0)),
                   pl.BlockSpec((T, LANES), lambda i: (0, 0)),
                   pl.BlockSpec((8, LANES), lambda i: (0, 0))],
        out_shape=[jax.ShapeDtypeStruct((T, 256), F32),
                   jax.ShapeDtypeStruct((T, LANES), F32),
                   jax.ShapeDtypeStruct((T, LANES), F32),
                   jax.ShapeDtypeStruct((8, LANES), F32)],
        compiler_params=_cparams(("arbitrary",)),
    )(hb, hb, hb, hb, hb, sinks, dy)


def _rope_tables(pos_ref):
    lane = lax.broadcasted_iota(jnp.int32, (1, LANES), 1)
    active = jnp.logical_and(lane >= HEAD, lane < HEAD + 2 * ROPE_HALF)
    idx = ((lane - HEAD) % ROPE_HALF).astype(F32)
    freq = jnp.exp(idx * (-math.log(ROPE_THETA) / ROPE_HALF))
    ang = pos_ref[...].astype(F32) * freq
    cos, sin = jnp.cos(ang), jnp.sin(ang)
    c = jnp.where(active, cos, 1.0)
    s_up = jnp.where(jnp.logical_and(active, lane >= HEAD + ROPE_HALF), sin, 0.0)
    s_dn = jnp.where(jnp.logical_and(active, lane < HEAD + ROPE_HALF), -sin, 0.0)
    return c, s_up, s_dn


def _rope(x, tabs):
    c, s_up, s_dn = tabs
    return x * c + pltpu.roll(x, ROPE_HALF, 1) * s_up + pltpu.roll(x, LANES - ROPE_HALF, 1) * s_dn


def _rope_t(dy, tabs):
    c, s_up, s_dn = tabs
    return dy * c + pltpu.roll(dy * s_up, LANES - ROPE_HALF, 1) + pltpu.roll(dy * s_dn, ROPE_HALF, 1)


def _mla_lat_specs(tm):
    cq, ckv, ckr = ((_INT_OFF[n] - N_HB) for n in ("c_q", "c_kv", "c_kr"))
    return [pl.BlockSpec((tm, 256), lambda i: (i, cq // 256)),
            pl.BlockSpec((tm, LANES), lambda i: (i, ckv // LANES)),
            pl.BlockSpec((tm, LANES), lambda i: (i, ckr // LANES)),
            pl.BlockSpec((tm, 1), lambda i: (i, 0)),
            pl.BlockSpec((1, 256), lambda i: (0, 0)),
            pl.BlockSpec((1, LANES), lambda i: (0, 0)),
            pl.BlockSpec((256, 512), lambda i: (0, 0)),
            pl.BlockSpec((LANES, 768), lambda i: (0, 0))]


def _mla_prep_fwd(hf, pos, g_q, g_kv, w_uq, w_ukv):
    T = hf.shape[0]
    tm = 512

    def body(cq_ref, ckv_ref, ckr_ref, pos_ref, gq_ref, gkv_ref, wq_ref, wkv_ref, qm_ref, km_ref, vm_ref):
        tabs = _rope_tables(pos_ref)
        cq = cq_ref[...]
        q = _dot((cq * _rms(cq) * gq_ref[...]).astype(BF16), wq_ref[...])
        ckv = ckv_ref[...]
        kv = _dot((ckv * _rms(ckv) * gkv_ref[...]).astype(BF16), wkv_ref[...])
        kr = _rope(pltpu.roll(ckr_ref[...], HEAD, 1), tabs)
        for h in range(4):
            sl = slice(h * LANES, (h + 1) * LANES)
            qm_ref[:, sl] = (_rope(q[:, sl], tabs) * MLA_SCALE).astype(BF16)
            km_ref[:, sl] = (kv[:, sl] + kr).astype(BF16)
        vm_ref[...] = kv[:, 512:].astype(BF16)

    return pl.pallas_call(
        body, name="mla_prep_fwd", grid=(T // tm,), in_specs=_mla_lat_specs(tm),
        out_specs=[pl.BlockSpec((tm, 512), lambda i: (i, 0)),
                   pl.BlockSpec((tm, 512), lambda i: (i, 0)),
                   pl.BlockSpec((tm, 256), lambda i: (i, 0))],
        out_shape=[jax.ShapeDtypeStruct((T, 512), BF16),
                   jax.ShapeDtypeStruct((T, 512), BF16),
                   jax.ShapeDtypeStruct((T, 256), BF16)],
        compiler_params=_cparams(("parallel",)),
    )(hf, hf, hf, pos, g_q, g_kv, w_uq, w_ukv)


def _mla_prep_bwd(hf, pos, g_q, g_kv, w_uq, w_ukv, dqm, dkm, dvm):
    T = hf.shape[0]
    tm = 512

    def body(cq_ref, ckv_ref, ckr_ref, pos_ref, gq_ref, gkv_ref, wq_ref, wkv_ref, dq_ref, dk_ref, dv_ref,
             dc_ref, dwq_ref, dwkv_ref, dgq_ref, dgkv_ref):
        @pl.when(pl.program_id(0) == 0)
        def _():
            dwq_ref[...] = jnp.zeros_like(dwq_ref)
            dwkv_ref[...] = jnp.zeros_like(dwkv_ref)
            dgq_ref[...] = jnp.zeros_like(dgq_ref)
            dgkv_ref[...] = jnp.zeros_like(dgkv_ref)

        tabs = _rope_tables(pos_ref)
        lane = lax.broadcasted_iota(jnp.int32, (1, LANES), 1)
        dq = jnp.concatenate([_rope_t(dq_ref[:, h * LANES:(h + 1) * LANES] * MLA_SCALE, tabs)
                              for h in range(4)], axis=1).astype(BF16)
        cq = cq_ref[...]
        rq = _rms(cq)
        cqn = (cq * rq * gq_ref[...]).astype(BF16)
        dwq_ref[...] += _dot_tn(cqn, dq)
        dcq, dgrow = _rms_bwd(_dot_nt(dq, wq_ref[...]), cq * rq, rq, gq_ref[...])
        dgq_ref[...] += _colsum(dgrow)
        dc_ref[:, 0:256] = dcq

        dk = dk_ref[...]
        dkr = dk[:, 0:LANES] + dk[:, LANES:2 * LANES] + dk[:, 2 * LANES:3 * LANES] + dk[:, 3 * LANES:]
        dkr = pltpu.roll(_rope_t(dkr, tabs), HEAD, 1)
        dc_ref[:, 384:512] = jnp.where(lane < 2 * ROPE_HALF, dkr, 0.0)
        dkv = jnp.concatenate([dk.astype(BF16), dv_ref[...].astype(BF16)], axis=1)
        ckv = ckv_ref[...]
        rkv = _rms(ckv)
        ckvn = (ckv * rkv * gkv_ref[...]).astype(BF16)
        dwkv_ref[...] += _dot_tn(ckvn, dkv)
        dckv, dgrow = _rms_bwd(_dot_nt(dkv, wkv_ref[...]), ckv * rkv, rkv, gkv_ref[...])
        dgkv_ref[...] += _colsum(dgrow)
        dc_ref[:, 256:384] = dckv

    return pl.pallas_call(
        body, name="mla_prep_bwd", grid=(T // tm,),
        in_specs=_mla_lat_specs(tm) + [pl.BlockSpec((tm, 512), lambda i: (i, 0)),
                                       pl.BlockSpec((tm, 512), lambda i: (i, 0)),
                                       pl.BlockSpec((tm, 256), lambda i: (i, 0))],
        out_specs=[pl.BlockSpec((tm, 512), lambda i: (i, 0)),
                   pl.BlockSpec((256, 512), lambda i: (0, 0)),
                   pl.BlockSpec((LANES, 768), lambda i: (0, 0)),
                   pl.BlockSpec((1, 256), lambda i: (0, 0)),
                   pl.BlockSpec((1, LANES), lambda i: (0, 0))],
        out_shape=[jax.ShapeDtypeStruct((T, 512), F32),
                   jax.ShapeDtypeStruct((256, 512), F32),
                   jax.ShapeDtypeStruct((LANES, 768), F32),
                   jax.ShapeDtypeStruct((1, 256), F32),
                   jax.ShapeDtypeStruct((1, LANES), F32)],
        compiler_params=_cparams(("arbitrary",)),
    )(hf, hf, hf, pos, g_q, g_kv, w_uq, w_ukv, dqm, dkm, dvm)


def _causal_masks(bq, bk):
    row = lax.broadcasted_iota(jnp.int32, (bq, bk), 0)
    col = lax.broadcasted_iota(jnp.int32, (bq, bk), 1)
    return row, col


def _mla_fwd(qm, km, vm):
    T = qm.shape[0]
    bq, bk = min(MLA_BQ, T), ATT_BLK
    nq, nsub = T // bq, bq // bk

    def body(q_ref, k_ref, v_ref, o_ref, lse_ref, acc_ref, m_ref):
        qi = pl.program_id(1)
        lane = lax.broadcasted_iota(jnp.int32, (1, LANES), 1) // HEAD
        row, col = _causal_masks(bq, bk)
        acc_ref[...] = jnp.zeros_like(acc_ref)
        m_ref[...] = jnp.full_like(m_ref, NEG)

        def scores(kb):
            rows = pl.ds(pl.multiple_of(kb * bk, bk), bk)
            return tuple(_dot_nt(q_ref[:, e * LANES:(e + 1) * LANES], k_ref[rows, e * LANES:(e + 1) * LANES])
                         for e in range(2))

        def step(kb, ss, causal):
            v_pair = v_ref[pl.ds(pl.multiple_of(kb * bk, bk), bk), :]
            ps, alphas = [], []
            for e in range(2):
                s = ss[e] if causal is None else jnp.where(causal, ss[e], NEG)
                m_prev = m_ref[e]
                m_new = jnp.maximum(m_prev, jnp.max(s, axis=1, keepdims=True))
                ps.append(jnp.exp(s - jnp.concatenate([m_new] * (bk // LANES), axis=1)).astype(BF16))
                alphas.append(jnp.exp(m_prev - m_new))
                m_ref[e] = m_new
            for e in range(2):
                v = jnp.where(lane == e, v_pair, 1)
                acc_ref[e] = alphas[e] * acc_ref[e] + _dot(ps[e], v)

        for d in range(nsub):
            step(qi * nsub + d, scores(qi * nsub + d), col + d * bk <= row)

        def loop(kb, ss):
            nxt = scores(jnp.minimum(kb + 1, jnp.maximum(qi * nsub - 1, 0)))
            step(kb, ss, None)
            return nxt

        lax.fori_loop(0, qi * nsub, loop, scores(0))
        out = jnp.zeros((bq, LANES), F32)
        lse = jnp.zeros((bq, LANES), F32)
        for e in range(2):
            acc = acc_ref[e]
            l = pltpu.roll(acc, HEAD, 1)
            out = jnp.where(lane == e, acc / l, out)
            lse = jnp.where(lane == e, m_ref[e] + jnp.log(l), lse)
        o_ref[...] = out
        lse_ref[...] = lse

    return pl.pallas_call(
        body, name="mla_fwd", grid=(2, nq),
        in_specs=[pl.BlockSpec((bq, 256), lambda j, i: (i, j)),
                  pl.BlockSpec((T, 256), lambda j, i: (0, j)),
                  pl.BlockSpec((T, LANES), lambda j, i: (0, j))],
        out_specs=[pl.BlockSpec((bq, LANES), lambda j, i: (i, j)),
                   pl.BlockSpec((bq, LANES), lambda j, i: (i, j))],
        out_shape=[jax.ShapeDtypeStruct((T, 256), F32), jax.ShapeDtypeStruct((T, 256), F32)],
        scratch_shapes=[pltpu.VMEM((2, bq, LANES), F32), pltpu.VMEM((2, bq, LANES), F32)],
        compiler_params=_cparams(("parallel", "arbitrary")),
    )(qm, km, vm)


def _mla_bwd(qm, km, vm, y, lse, dy):
    T = qm.shape[0]
    bq, bk = min(MLA_BQ, T), ATT_BLK
    nq, nsub = T // bq, bq // bk

    def body(q_ref, k_ref, v_ref, y_ref, lse_ref, dy_ref, dq_ref, dk_ref, dv_ref, dob_ref, st_ref):
        qi = pl.program_id(1)

        @pl.when(qi == 0)
        def _():
            dk_ref[...] = jnp.zeros_like(dk_ref)
            dv_ref[...] = jnp.zeros_like(dv_ref)

        lane = lax.broadcasted_iota(jnp.int32, (1, LANES), 1) // HEAD
        row, col = _causal_masks(bq, bk)
        dq_ref[...] = jnp.zeros_like(dq_ref)
        lse = lse_ref[...]
        lse_other = pltpu.roll(lse, HEAD, 1)
        for e in range(2):
            do = jnp.where(lane == e, dy_ref[...], 0.0)
            dob_ref[e] = do.astype(BF16)
            st_ref[2 * e] = jnp.where(lane == e, lse, lse_other)
            st_ref[2 * e + 1] = jnp.broadcast_to(jnp.sum(do * y_ref[...], axis=1, keepdims=True), (bq, LANES))

        hss = [slice(e * LANES, (e + 1) * LANES) for e in range(2)]
        tile = lambda a: jnp.concatenate([a] * (bk // LANES), axis=1)

        def scores(kb):
            rows = pl.ds(pl.multiple_of(kb * bk, bk), bk)
            v_pair = v_ref[rows, :]
            return (tuple(_dot_nt(q_ref[:, hss[e]], k_ref[rows, hss[e]]) for e in range(2))
                    + tuple(_dot_nt(dob_ref[e], jnp.where(lane == e, v_pair, 0)) for e in range(2)))

        def step(kb, sc, causal):
            rows = pl.ds(pl.multiple_of(kb * bk, bk), bk)
            ps, dss = [], []
            for e in range(2):
                s = sc[e] if causal is None else jnp.where(causal, sc[e], NEG)
                p = jnp.exp(s - tile(st_ref[2 * e]))
                dss.append((p * (sc[2 + e] - tile(st_ref[2 * e + 1]))).astype(BF16))
                ps.append(p.astype(BF16))
            dv_ref[rows, :] += _dot_tn(ps[0], dob_ref[0]) + _dot_tn(ps[1], dob_ref[1])
            for e in range(2):
                dk_ref[rows, hss[e]] += _dot_tn(dss[e], q_ref[:, hss[e]])
            for e in range(2):
                dq_ref[:, hss[e]] += _dot(dss[e], k_ref[rows, hss[e]])

        for d in range(nsub):
            step(qi * nsub + d, scores(qi * nsub + d), col + d * bk <= row)

        def loop(kb, sc):
            nxt = scores(jnp.minimum(kb + 1, jnp.maximum(qi * nsub - 1, 0)))
            step(kb, sc, None)
            return nxt

        lax.fori_loop(0, qi * nsub, loop, scores(0))

    return pl.pallas_call(
        body, name="mla_bwd", grid=(2, nq),
        in_specs=[pl.BlockSpec((bq, 256), lambda j, i: (i, j)),
                  pl.BlockSpec((T, 256), lambda j, i: (0, j)),
                  pl.BlockSpec((T, LANES), lambda j, i: (0, j)),
                  pl.BlockSpec((bq, LANES), lambda j, i: (i, j)),
                  pl.BlockSpec((bq, LANES), lambda j, i: (i, j)),
                  pl.BlockSpec((bq, LANES), lambda j, i: (i, j))],
        out_specs=[pl.BlockSpec((bq, 256), lambda j, i: (i, j)),
                   pl.BlockSpec((T, 256), lambda j, i: (0, j)),
                   pl.BlockSpec((T, LANES), lambda j, i: (0, j))],
        out_shape=[jax.ShapeDtypeStruct((T, 512), F32),
                   jax.ShapeDtypeStruct((T, 512), F32),
                   jax.ShapeDtypeStruct((T, 256), F32)],
        scratch_shapes=[pltpu.VMEM((2, bq, LANES), BF16), pltpu.VMEM((4, bq, LANES), F32)],
        compiler_params=_cparams(("parallel", "arbitrary")),
    )(qm, km, vm, y, lse, dy)


def _suffix_ones(n):
    r = lax.broadcasted_iota(jnp.int32, (n, n), 0)
    c = lax.broadcasted_iota(jnp.int32, (n, n), 1)
    return (r >= c).astype(BF16)


def _prefix_ones(n):
    r = lax.broadcasted_iota(jnp.int32, (n, n), 0)
    c = lax.broadcasted_iota(jnp.int32, (n, n), 1)
    return (r <= c).astype(BF16)


def _tri_sum(x, u):
    hi, lo = _split(x)
    return _dot(hi, u) + _dot(lo, u)


def _sb_specs(T, bq):
    qo, ko, vo = (_INT_OFF[n] // LANES for n in ("d_q", "d_k", "d_v"))
    return [pl.BlockSpec((bq, LANES), lambda j, i: (i, qo + j)),
            pl.BlockSpec((T, LANES), lambda j, i: (0, ko + j)),
            pl.BlockSpec((T, LANES), lambda j, i: (0, vo + j))]


def _sb_fwd(hb):
    T = hb.shape[0]
    bq = bk = ATT_BLK
    nq = T // bq

    def body(q_ref, k_ref, v_ref, o_ref, tot_ref, cnt_ref, qm_ref, car_ref):
        qi = pl.program_id(1)
        lane = lax.broadcasted_iota(jnp.int32, (1, LANES), 1) // HEAD
        row, col = _causal_masks(bq, bk)
        strict = col < row
        u = _suffix_ones(bk)
        o_ref[...] = jnp.zeros_like(o_ref)
        car_ref[...] = jnp.zeros_like(car_ref)
        for e in range(2):
            qm_ref[e] = jnp.where(lane == e, q_ref[...], 0) * 0.125

        def step(kb, masked):
            rows = pl.ds(pl.multiple_of(kb * bk, bk), bk)
            k_pair, v_pair = k_ref[rows, :], v_ref[rows, :]
            tile = lambda a: jnp.concatenate([a] * (bk // LANES), axis=1)
            zs = [_dot_nt(qm_ref[e], k_pair) for e in range(2)]
            splits = []
            for e in range(2):
                z = zs[e]
                lk = jnp.minimum(-z, 0.0) - jnp.log(1.0 + jnp.exp(-jnp.abs(z)))
                if masked:
                    lk = jnp.where(strict, lk, 0.0)
                splits.append(_split(lk))
            sufs = [_dot(hi, u) + _dot(lo, u) for hi, lo in splits]
            aas = []
            for e in range(2):
                a = jnp.exp(zs[e] + sufs[e] + tile(car_ref[e]))
                if masked:
                    a = jnp.where(strict, a, 0.0)
                aas.append(a.astype(BF16))
                car_ref[e] += jnp.broadcast_to(sufs[e][:, 0:1], (bq, LANES))
            o_ref[...] += (_dot(aas[0], jnp.where(lane == 0, v_pair, 0))
                           + _dot(aas[1], jnp.where(lane == 1, v_pair, 0)))

        step(qi, True)

        def live():
            return jnp.max(jnp.maximum(car_ref[0], car_ref[1])) >= SB_DEAD

        def cond(c):
            return jnp.logical_and(c[0] < qi, c[1])

        def loop(c):
            step(qi - 1 - c[0], False)
            return c[0] + 1, live()

        done, _ = lax.while_loop(cond, loop, (jnp.int32(0), live()))
        tot_ref[...] = jnp.where(lane == 0, car_ref[0], car_ref[1])
        cnt_ref[pl.program_id(0), qi] = done.astype(F32)

    return pl.pallas_call(
        body, name="sb_fwd", grid=(2, nq), in_specs=_sb_specs(T, bq),
        out_specs=[pl.BlockSpec((bq, LANES), lambda j, i: (i, j)), pl.BlockSpec((bq, LANES), lambda j, i: (i, j)),
                   pl.BlockSpec(memory_space=pltpu.SMEM)],
        out_shape=[jax.ShapeDtypeStruct((T, 256), F32), jax.ShapeDtypeStruct((T, 256), F32),
                   jax.ShapeDtypeStruct((2, nq), F32)],
        scratch_shapes=[pltpu.VMEM((2, bq, LANES), BF16), pltpu.VMEM((2, bq, LANES), F32)],
        compiler_params=_cparams(("parallel", "arbitrary")),
    )(hb, hb, hb)


def _sb_bwd(hb, tot, cnt, dy):
    T = hb.shape[0]
    bq = bk = ATT_BLK
    nq = T // bq

    def body(q_ref, k_ref, v_ref, tot_ref, dy_ref, cnt_ref, dq_ref, dk_ref, dv_ref, qm_ref, dob_ref, dqa_ref, rem_ref,
             cg_ref):
        qi = pl.program_id(1)

        @pl.when(qi == 0)
        def _():
            dk_ref[...] = jnp.zeros_like(dk_ref)
            dv_ref[...] = jnp.zeros_like(dv_ref)

        lane = lax.broadcasted_iota(jnp.int32, (1, LANES), 1) // HEAD
        row, col = _causal_masks(bq, bk)
        strict = col < row
        u = _prefix_ones(bk)
        tot = tot_ref[...]
        tot_other = pltpu.roll(tot, HEAD, 1)
        dqa_ref[...] = jnp.zeros_like(dqa_ref)
        cg_ref[...] = jnp.zeros_like(cg_ref)
        for e in range(2):
            qm_ref[e] = jnp.where(lane == e, q_ref[...], 0) * 0.125
            dob_ref[e] = jnp.where(lane == e, dy_ref[...], 0.0).astype(BF16)
            rem_ref[e] = jnp.where(lane == e, tot, tot_other)

        def step(kb, masked):
            rows = pl.ds(pl.multiple_of(kb * bk, bk), bk)
            k_pair, v_pair = k_ref[rows, :], v_ref[rows, :]
            tile = lambda a: jnp.concatenate([a] * (bk // LANES), axis=1)
            zs = [_dot_nt(qm_ref[e], k_pair) for e in range(2)]
            das = [_dot_nt(dob_ref[e], jnp.where(lane == e, v_pair, 0)) for e in range(2)]
            zls, splits = [], []
            for e in range(2):
                z = zs[e]
                lk = jnp.minimum(-z, 0.0) - jnp.log(1.0 + jnp.exp(-jnp.abs(z)))
                if masked:
                    lk = jnp.where(strict, lk, 0.0)
                zls.append(z + lk)
                splits.append(_split(lk))
            pres = [_dot(hi, u) + _dot(lo, u) for hi, lo in splits]
            aas, gs, gsplits = [], [], []
            for e in range(2):
                a = jnp.exp(zls[e] + (tile(rem_ref[e]) - pres[e]))
                if masked:
                    a = jnp.where(strict, a, 0.0)
                g = a * das[e]
                aas.append(a.astype(BF16))
                gs.append(g)
                gsplits.append(_split(g))
                rem_ref[e] -= jnp.broadcast_to(pres[e][:, bk - 1:bk], (bq, LANES))
            dv_ref[rows, :] += _dot_tn(aas[0], dob_ref[0]) + _dot_tn(aas[1], dob_ref[1])
            gpres = [_dot(hi, u) + _dot(lo, u) for hi, lo in gsplits]
            dzs = []
            for e in range(2):
                dz = gs[e] - jnp.exp(zls[e]) * (tile(cg_ref[e]) + gpres[e])
                if masked:
                    dz = jnp.where(strict, dz, 0.0)
                dzs.append(dz.astype(BF16))
                cg_ref[e] += jnp.broadcast_to(gpres[e][:, bk - 1:bk], (bq, LANES))
            dk_ref[rows, :] += _dot_tn(dzs[0], qm_ref[0]) + _dot_tn(dzs[1], qm_ref[1])
            for e in range(2):
                dqa_ref[e] += _dot(dzs[e], k_pair)

        def loop(kb, c):
            step(kb, False)
            return c

        start = qi - jnp.clip(cnt_ref[pl.program_id(0), qi].astype(jnp.int32), 0, qi)
        lax.fori_loop(start, qi, loop, 0)
        step(qi, True)
        dq_ref[...] = jnp.where(lane == 0, dqa_ref[0], dqa_ref[1]) * 0.125

    return pl.pallas_call(
        body, name="sb_bwd", grid=(2, nq),
        in_specs=_sb_specs(T, bq) + [pl.BlockSpec((bq, LANES), lambda j, i: (i, j)),
                                     pl.BlockSpec((bq, LANES), lambda j, i: (i, j)),
                                     pl.BlockSpec(memory_space=pltpu.SMEM)],
        out_specs=[pl.BlockSpec((bq, LANES), lambda j, i: (i, j)),
                   pl.BlockSpec((T, LANES), lambda j, i: (0, j)),
                   pl.BlockSpec((T, LANES), lambda j, i: (0, j))],
        out_shape=[jax.ShapeDtypeStruct((T, 256), F32)] * 3,
        scratch_shapes=[pltpu.VMEM((2, bq, LANES), BF16), pltpu.VMEM((2, bq, LANES), BF16),
                        pltpu.VMEM((2, bq, LANES), F32), pltpu.VMEM((2, bq, LANES), F32),
                        pltpu.VMEM((2, bq, LANES), F32)],
        compiler_params=_cparams(("parallel", "arbitrary")),
    )(hb, hb, hb, tot, dy, cnt)


EP_TM = 256


def _ep_in_specs(tm, rev):
    idx = (lambda i: rev - i) if rev is not None else (lambda i: i)
    bo = (_INT_OFF["b_b"] - N_HB) // 256
    halo = lambda i: jnp.maximum(idx(i) * (tm // 8) - 1, 0)
    return [pl.BlockSpec((tm, 256), lambda i: (idx(i), 0)),
            pl.BlockSpec((tm, 256), lambda i: (idx(i), 0)),
            pl.BlockSpec((tm, 256), lambda i: (idx(i), 0)),
            pl.BlockSpec((tm, D_MODEL), lambda i: (idx(i), 0)),
            pl.BlockSpec((tm, 256), lambda i: (idx(i), bo)),
            pl.BlockSpec((tm, 256), lambda i: (idx(i), bo + 1)),
            pl.BlockSpec((tm, 256), lambda i: (idx(i), bo + 2)),
            pl.BlockSpec((8, 256), lambda i: (halo(i), bo + 1)),
            pl.BlockSpec((8, 256), lambda i: (halo(i), bo + 2)),
            pl.BlockSpec((3, 256), lambda i: (0, 0)),
            pl.BlockSpec((1, 256), lambda i: (0, 0)),
            pl.BlockSpec((1, D_MODEL), lambda i: (0, 0)),
            pl.BlockSpec((D_MODEL, D_MODEL), lambda i: (0, 0)),
            pl.BlockSpec((1, D_MODEL), lambda i: (0, 0))]


def _ep_mix(first, ya_ref, yc_ref, yd_ref, gate_ref, bb_ref, bc_ref, bx_ref, hc_ref, hx_ref, cw_ref, cb_ref, gg_ref):
    tm = ya_ref.shape[0]
    u = bc_ref[...] * bx_ref[...]
    halo = jnp.where(first, 0.0, hc_ref[...] * hx_ref[...])
    row = lax.broadcasted_iota(jnp.int32, (tm, 1), 0)
    u1 = jnp.where(row == 0, halo[7:8, :], pltpu.roll(u, 1, 0))
    u2 = jnp.where(row == 0, halo[6:7, :], jnp.where(row == 1, halo[7:8, :], pltpu.roll(u, 2, 0)))
    cw = cw_ref[...]
    conv = cw[0:1, :] * u2 + cw[1:2, :] * u1 + cw[2:3, :] * u + cb_ref[...]
    bb = bb_ref[...]
    ys = [ya_ref[...], bb * conv, yc_ref[...], yd_ref[...]]
    rs = [_rms(y) for y in ys]
    gg = gg_ref[...]
    yhat = jnp.concatenate([y * r for y, r in zip(ys, rs)], axis=1)
    gate = gate_ref[...]
    sig = 1.0 / (1.0 + jnp.exp(-gate))
    return u, u1, u2, conv, bb, rs, yhat, yhat * gg, gate, sig


def _epilogue_fwd(x, ya, yc, yd, hf, conv_w, conv_b, g_grp, w_out, g_post):
    T = x.shape[0]
    tm = EP_TM

    def body(x_ref, ya_ref, yc_ref, yd_ref, gate_ref, bb_ref, bc_ref, bx_ref, hc_ref, hx_ref, cw_ref, cb_ref,
             gg_ref, wo_ref, gp_ref, o_ref):
        (_, _, _, _, _, _, _, yn, gate, sig) = _ep_mix(
            pl.program_id(0) == 0, ya_ref, yc_ref, yd_ref, gate_ref, bb_ref, bc_ref, bx_ref, hc_ref, hx_ref,
            cw_ref, cb_ref, gg_ref)
        z = _dot((yn * (gate * sig)).astype(BF16), wo_ref[...])
        o_ref[...] = x_ref[...] + z * _rms(z) * gp_ref[...]

    return pl.pallas_call(
        body, name="epilogue_fwd", grid=(T // tm,),
        in_specs=[pl.BlockSpec((tm, D_MODEL), lambda i: (i, 0))] + _ep_in_specs(tm, None),
        out_specs=pl.BlockSpec((tm, D_MODEL), lambda i: (i, 0)),
        out_shape=jax.ShapeDtypeStruct((T, D_MODEL), F32),
        compiler_params=_cparams(("parallel",)),
    )(x, ya, yc, yd, hf, hf, hf, hf, hf, hf, conv_w, conv_b, g_grp, w_out, g_post)


def _epilogue_bwd(dxn, ya, yc, yd, hf, conv_w, conv_b, g_grp, w_out, g_post):
    T = dxn.shape[0]
    tm = EP_TM
    nt = T // tm
    ridx = lambda i: (nt - 1 - i, 0)

    def body(dx_ref, ya_ref, yc_ref, yd_ref, gate_ref, bb_ref, bc_ref, bx_ref, hc_ref, hx_ref, cw_ref, cb_ref,
             gg_ref, wo_ref, gp_ref,
             dya_ref, dyc_ref, dyd_ref, dhf_ref, dwo_ref, dgp_ref, dgg_ref, dcw_ref, dcb_ref, carry_ref):
        i = pl.program_id(0)

        @pl.when(i == 0)
        def _():
            for r in (dwo_ref, dgp_ref, dgg_ref, dcw_ref, dcb_ref, carry_ref):
                r[...] = jnp.zeros_like(r)

        (u, u1, u2, conv, bb, rs, yhat, yn, gate, sig) = _ep_mix(
            i == nt - 1, ya_ref, yc_ref, yd_ref, gate_ref, bb_ref, bc_ref, bx_ref, hc_ref, hx_ref,
            cw_ref, cb_ref, gg_ref)
        silu = gate * sig
        ymix = (yn * silu).astype(BF16)
        z = _dot(ymix, wo_ref[...])
        rz = _rms(z)
        dz, dgrow = _rms_bwd(dx_ref[...], z * rz, rz, gp_ref[...])
        dgp_ref[...] += _colsum(dgrow)
        dzb = dz.astype(BF16)
        dwo_ref[...] += _dot_tn(ymix, dzb)
        dymix = _dot_nt(dzb, wo_ref[...])
        dhf_ref[:, 0:D_MODEL] = dymix * yn * (sig * (1.0 + gate * (1.0 - sig)))
        dyn = dymix * silu
        dgg_ref[...] += _colsum(dyn * yhat)
        gg = gg_ref[...]
        dys = []
        for gi in range(4):
            sl = slice(gi * GROUP, (gi + 1) * GROUP)
            dyh = dyn[:, sl] * gg[:, sl]
            yh = yhat[:, sl]
            dys.append(rs[gi] * (dyh - yh * jnp.mean(dyh * yh, axis=-1, keepdims=True)))
        dya_ref[...] = dys[0]
        dyc_ref[...] = dys[2]
        dyd_ref[...] = dys[3]
        dyb = dys[1]
        dhf_ref[:, D_MODEL:D_MODEL + 256] = dyb * conv
        dconv = dyb * bb
        dcb_ref[...] += _colsum(dconv)
        dcw_ref[0:1, :] += _colsum(dconv * u2)
        dcw_ref[1:2, :] += _colsum(dconv * u1)
        dcw_ref[2:3, :] += _colsum(dconv * u)
        carry = carry_ref[...]
        row = lax.broadcasted_iota(jnp.int32, (tm, 1), 0)
        d1 = jnp.where(row == tm - 1, carry[0:1, :], pltpu.roll(dconv, tm - 1, 0))
        d2 = jnp.where(row == tm - 2, carry[0:1, :],
                       jnp.where(row == tm - 1, carry[1:2, :], pltpu.roll(dconv, tm - 2, 0)))
        cw = cw_ref[...]
        du = cw[2:3, :] * dconv + cw[1:2, :] * d1 + cw[0:1, :] * d2
        dhf_ref[:, D_MODEL + 256:D_MODEL + 512] = du * bx_ref[...]
        dhf_ref[:, D_MODEL + 512:D_MODEL + 768] = du * bc_ref[...]
        carry_ref[...] = dconv[0:8, :]

    in_specs = [pl.BlockSpec((tm, D_MODEL), ridx)] + _ep_in_specs(tm, nt - 1)
    return pl.pallas_call(
        body, name="epilogue_bwd", grid=(nt,), in_specs=in_specs,
        out_specs=[pl.BlockSpec((tm, 256), ridx), pl.BlockSpec((tm, 256), ridx), pl.BlockSpec((tm, 256), ridx),
                   pl.BlockSpec((tm, D_MODEL + 768), ridx),
                   pl.BlockSpec((D_MODEL, D_MODEL), lambda i: (0, 0)),
                   pl.BlockSpec((1, D_MODEL), lambda i: (0, 0)),
                   pl.BlockSpec((1, D_MODEL), lambda i: (0, 0)),
                   pl.BlockSpec((8, 256), lambda i: (0, 0)),
                   pl.BlockSpec((1, 256), lambda i: (0, 0))],
        out_shape=[jax.ShapeDtypeStruct((T, 256), F32)] * 3
                  + [jax.ShapeDtypeStruct((T, D_MODEL + 768), F32),
                     jax.ShapeDtypeStruct((D_MODEL, D_MODEL), F32),
                     jax.ShapeDtypeStruct((1, D_MODEL), F32),
                     jax.ShapeDtypeStruct((1, D_MODEL), F32),
                     jax.ShapeDtypeStruct((8, 256), F32),
                     jax.ShapeDtypeStruct((1, 256), F32)],
        scratch_shapes=[pltpu.VMEM((8, 256), F32)],
        compiler_params=_cparams(("arbitrary",)),
    )(dxn, ya, yc, yd, hf, hf, hf, hf, hf, hf, conv_w, conv_b, g_grp, w_out, g_post)


def _loss_head(y, tgt):
    T = y.shape[0]
    tm = 512

    def body(y_ref, t_ref, dy_ref, l_ref):
        @pl.when(pl.program_id(0) == 0)
        def _():
            l_ref[...] = jnp.zeros_like(l_ref)

        d = y_ref[...] - t_ref[...]
        dy_ref[...] = d * (1.0 / D_MODEL)
        part = jnp.sum(jnp.sum(d * d, axis=1, keepdims=True), axis=0, keepdims=True)
        l_ref[...] += part * (0.5 / D_MODEL)

    return pl.pallas_call(
        body, name="loss_head", grid=(T // tm,),
        in_specs=[pl.BlockSpec((tm, D_MODEL), lambda i: (i, 0))] * 2,
        out_specs=[pl.BlockSpec((tm, D_MODEL), lambda i: (i, 0)), pl.BlockSpec((8, LANES), lambda i: (0, 0))],
        out_shape=[jax.ShapeDtypeStruct((T, D_MODEL), F32), jax.ShapeDtypeStruct((8, LANES), F32)],
        compiler_params=_cparams(("arbitrary",)),
    )(y, tgt)


def _place():
    return lax.axis_index("x"), lax.axis_index("y"), lax.axis_index("c")


def _other_chips(x, y):
    return [(1 - x, y), (x, 1 - y), (1 - x, 1 - y)]


HBM = pl.BlockSpec(memory_space=pl.ANY)


def _gather_weights(shards):
    n = len(shards)

    def body(*refs):
        ins, outs = refs[:n], refs[n:2 * n]
        ici_send, ici_recv, d2d_send, d2d_recv, local_sems = refs[2 * n:]
        x, y, c = _place()
        me = 2 * x + y
        chips = _other_chips(x, y)

        def ici(a, j, layer_from):
            px, py = chips[j]
            return pltpu.make_async_remote_copy(
                src_ref=ins[a].at[c], dst_ref=outs[a].at[layer_from, c], send_sem=ici_send.at[3 * a + j],
                recv_sem=ici_recv.at[3 * a + j], device_id=(px, py, c), device_id_type=MESH)

        def d2d(a, j, layer):
            px, py = chips[j]
            blk = outs[a].at[2 * px + py, layer]
            return pltpu.make_async_remote_copy(
                src_ref=blk, dst_ref=blk, send_sem=d2d_send.at[3 * a + j], recv_sem=d2d_recv.at[3 * a + j],
                device_id=(x, y, 1 - c), device_id_type=MESH)

        local = [pltpu.make_async_copy(ins[a], outs[a].at[me], local_sems.at[a]) for a in range(n)]
        for cp in local:
            cp.start()
        sends = [ici(a, j, me) for j in range(3) for a in range(n)]
        for cp in sends:
            cp.start()
        for j in range(3):
            px, py = chips[j]
            for a in range(n):
                ici(a, j, 2 * px + py).wait_recv()
                fwd = d2d(a, j, c)
                fwd.start()
                sends.append(fwd)
        for j in range(3):
            for a in range(n):
                d2d(a, j, 1 - c).wait_recv()
        for cp in sends:
            cp.wait_send()
        for cp in local:
            cp.wait()

    return pl.pallas_call(
        body, name="gather_weights",
        in_specs=[HBM] * n, out_specs=[HBM] * n,
        out_shape=[jax.ShapeDtypeStruct((4,) + s.shape, s.dtype) for s in shards],
        scratch_shapes=[pltpu.SemaphoreType.DMA((3 * n,))] * 4 + [pltpu.SemaphoreType.DMA((n,))],
    )(*shards)


def _exchange_chips(parts, small):
    n = len(parts)

    def body(*refs):
        ins, sm_ref = refs[:n], refs[n]
        outs, osm_ref = refs[n + 1:2 * n + 1], refs[2 * n + 1]
        send_sems, recv_sems, ssend_sems, srecv_sems, local_sems = refs[2 * n + 2:]
        x, y, c = _place()
        me = 2 * x + y
        dev = 4 * x + 2 * y + c
        local = [pltpu.make_async_copy(ins[a].at[me], outs[a].at[me], local_sems.at[a]) for a in range(n)]
        local.append(pltpu.make_async_copy(sm_ref, osm_ref.at[dev], local_sems.at[n]))
        for cp in local:
            cp.start()
        sends = []
        for j, (px, py) in enumerate(_other_chips(x, y)):
            for a in range(n):
                cp = pltpu.make_async_remote_copy(
                    src_ref=ins[a].at[2 * px + py], dst_ref=outs[a].at[me], send_sem=send_sems.at[3 * a + j],
                    recv_sem=recv_sems.at[3 * a + j], device_id=(px, py, c), device_id_type=MESH)
                cp.start()
                sends.append(cp)
        flips = [(fx, fy, fc) for fx in (0, 1) for fy in (0, 1) for fc in (0, 1)][1:]
        for j, (fx, fy, fc) in enumerate(flips):
            cp = pltpu.make_async_remote_copy(
                src_ref=sm_ref, dst_ref=osm_ref.at[dev], send_sem=ssend_sems.at[j], recv_sem=srecv_sems.at[j],
                device_id=(x ^ fx, y ^ fy, c ^ fc), device_id_type=MESH)
            cp.start()
            sends.append(cp)
        for j, (px, py) in enumerate(_other_chips(x, y)):
            for a in range(n):
                pltpu.make_async_remote_copy(
                    src_ref=ins[a].at[me], dst_ref=outs[a].at[2 * px + py], send_sem=send_sems.at[3 * a + j],
                    recv_sem=recv_sems.at[3 * a + j], device_id=(px, py, c), device_id_type=MESH).wait_recv()
        for j, (fx, fy, fc) in enumerate(flips):
            src = 4 * (x ^ fx) + 2 * (y ^ fy) + (c ^ fc)
            pltpu.make_async_remote_copy(
                src_ref=sm_ref, dst_ref=osm_ref.at[src], send_sem=ssend_sems.at[j], recv_sem=srecv_sems.at[j],
                device_id=(x ^ fx, y ^ fy, c ^ fc), device_id_type=MESH).wait_recv()
        for cp in sends:
            cp.wait_send()
        for cp in local:
            cp.wait()

    return pl.pallas_call(
        body, name="exchange_chips",
        in_specs=[HBM] * (n + 1), out_specs=[HBM] * (n + 1),
        out_shape=[jax.ShapeDtypeStruct(p.shape, p.dtype) for p in parts]
                  + [jax.ShapeDtypeStruct((8,) + small.shape, small.dtype)],
        scratch_shapes=[pltpu.SemaphoreType.DMA((3 * n,)), pltpu.SemaphoreType.DMA((3 * n,)),
                        pltpu.SemaphoreType.DMA((7,)), pltpu.SemaphoreType.DMA((7,)),
                        pltpu.SemaphoreType.DMA((n + 1,))],
    )(*parts, small)


def _swap_cores(parts, name):
    n = len(parts)

    def body(*refs):
        ins, outs, send_sems, recv_sems = refs[:n], refs[n:2 * n], refs[2 * n], refs[2 * n + 1]
        x, y, c = _place()
        copies = [pltpu.make_async_remote_copy(
            src_ref=ins[a], dst_ref=outs[a], send_sem=send_sems.at[a], recv_sem=recv_sems.at[a],
            device_id=(x, y, 1 - c), device_id_type=MESH) for a in range(n)]
        for cp in copies:
            cp.start()
        for cp in copies:
            cp.wait()

    return pl.pallas_call(
        body, name=name, in_specs=[HBM] * n, out_specs=[HBM] * n,
        out_shape=[jax.ShapeDtypeStruct(p.shape, p.dtype) for p in parts],
        scratch_shapes=[pltpu.SemaphoreType.DMA((n,)), pltpu.SemaphoreType.DMA((n,))],
    )(*parts)


def _row_block(rows):
    for cand in (256, 128, 64, 32, 16, 8):
        if rows % cand == 0:
            return cand
    return rows


def _add(a, b, name):
    L, R, C = a.shape
    tr = _row_block(R)

    def body(a_ref, b_ref, o_ref):
        o_ref[...] = a_ref[...] + b_ref[...]

    spec = pl.BlockSpec((1, tr, C), lambda l, i: (l, i, 0))
    return pl.pallas_call(
        body, name=name, grid=(L, R // tr), in_specs=[spec, spec], out_specs=spec,
        out_shape=jax.ShapeDtypeStruct((L, R, C), F32), compiler_params=_cparams(("parallel", "parallel")),
    )(a, b)


def _sum_leading(buf, name):
    n, R, C = buf.shape
    tr = _row_block(R)

    def body(b_ref, o_ref):
        acc = b_ref[0]
        for k in range(1, n):
            acc = acc + b_ref[k]
        o_ref[...] = acc

    return pl.pallas_call(
        body, name=name, grid=(R // tr,),
        in_specs=[pl.BlockSpec((n, tr, C), lambda i: (0, i, 0))],
        out_specs=pl.BlockSpec((tr, C), lambda i: (i, 0)),
        out_shape=jax.ShapeDtypeStruct((R, C), F32),
        compiler_params=_cparams(("parallel",)),
    )(buf)


def _adam_update(w, g, m, v):
    c1 = 1.0 / (1.0 - ADAM_B1 ** ADAM_STEP)
    c2 = 1.0 / (1.0 - ADAM_B2 ** ADAM_STEP)
    mn = ADAM_B1 * m + (1.0 - ADAM_B1) * g
    vn = ADAM_B2 * v + (1.0 - ADAM_B2) * (g * g)
    return -ADAM_LR * ((mn * c1) / (jnp.sqrt(vn * c2) + ADAM_EPS) + ADAM_WD * w), mn, vn


def _adamw_layers(w, m, v, g_mine, g_other, name):
    _, R, C = w.shape
    tr = _row_block(R)

    def body(w_ref, m_ref, v_ref, gm_ref, go_ref, g_ref, d_ref, mo_ref, vo_ref):
        g = jnp.where(pl.program_id(0) == lax.axis_index("c"), gm_ref[...], go_ref[...])
        g_ref[0] = g
        d_ref[0], mo_ref[0], vo_ref[0] = _adam_update(w_ref[0], g, m_ref[0], v_ref[0])

    spec3 = pl.BlockSpec((1, tr, C), lambda l, i: (l, i, 0))
    spec2 = pl.BlockSpec((tr, C), lambda l, i: (i, 0))
    return pl.pallas_call(
        body, name=name, grid=(2, R // tr),
        in_specs=[spec3] * 3 + [spec2] * 2, out_specs=[spec3] * 4,
        out_shape=[jax.ShapeDtypeStruct(w.shape, F32)] * 4,
        compiler_params=_cparams(("parallel", "parallel")),
    )(w, m, v, g_mine, g_other)


PACK_C = 1024
_BIG = ("w_in", "w_out", "mla_w_uq", "mla_w_ukv", "conv_w")
_SMALL = ("norm_pre", "group_norm", "norm_post", "conv_b", "mla_q_norm", "mla_kv_norm", "attn_sinks")
_SMALL_W = {"norm_pre": 1024, "group_norm": 1024, "norm_post": 1024, "conv_b": 256, "mla_q_norm": 256,
            "mla_kv_norm": 128, "attn_sinks": 4}


def _pack_small(d):
    flat = jnp.concatenate([d[n].reshape(-1) for n in _SMALL])
    return jnp.pad(flat, (0, 8 * PACK_C - flat.shape[0])).reshape(8, PACK_C)


def _adamw_small(w, m, v, got):
    ns = len(_SMALL)

    def body(*refs):
        got_ref = refs[3 * ns]
        outs = refs[3 * ns + 1:]
        gsum = got_ref[0]
        for d in range(1, 8):
            gsum = gsum + got_ref[d]
        off = 0
        for i, name in enumerate(_SMALL):
            wd = _SMALL_W[name]
            rows = []
            for l in range(DEPTH):
                r, c0 = divmod(off + l * wd, PACK_C)
                rows.append(gsum[r:r + 1, c0:c0 + wd])
            off += DEPTH * wd
            g = jnp.concatenate(rows, axis=0)
            delta, mn, vn = _adam_update(refs[i][...], g, refs[ns + i][...], refs[2 * ns + i][...])
            outs[i][...] = g
            outs[ns + i][...] = delta
            outs[2 * ns + i][...] = mn
            outs[3 * ns + i][...] = vn

    shapes = [jax.ShapeDtypeStruct(w[n].shape, F32) for n in _SMALL]
    res = pl.pallas_call(body, name="adamw_small", out_shape=shapes * 4)(
        *[w[n] for n in _SMALL], *[m[n] for n in _SMALL], *[v[n] for n in _SMALL], got)
    return [dict(zip(_SMALL, res[k * ns:(k + 1) * ns])) for k in range(4)]


def _w_in_internal(w):
    cols = []
    for n in _INT_ORDER:
        o, wd = _REAL_OFF[n]
        cols.append(w[:, o:o + wd])
        if _INT_W[n] != wd:
            cols.append(jnp.zeros((w.shape[0], _INT_W[n] - wd), w.dtype))
    return jnp.concatenate(cols, axis=1)


def _w_in_real(dw):
    return jnp.concatenate([dw[:, _INT_OFF[n]:_INT_OFF[n] + wd] for n, wd in _REAL], axis=1)


def _uq_internal(w):
    return jnp.pad(w.reshape(256, 4, 96), ((0, 0), (0, 0), (0, 32))).reshape(256, 512)


def _uq_real(dw):
    return dw.reshape(256, 4, 128)[:, :, :96].reshape(256, 384)


def _ukv_internal(w):
    w4 = w.reshape(128, 4, 128)
    k = jnp.pad(w4[:, :, :64], ((0, 0), (0, 0), (0, 64))).reshape(128, 512)
    return jnp.concatenate([k, w4[:, :, 64:].reshape(128, 256)], axis=1)


def _ukv_real(dw):
    k = dw[:, :512].reshape(128, 4, 128)[:, :, :64]
    v = dw[:, 512:].reshape(128, 4, 64)
    return jnp.concatenate([k, v], axis=2).reshape(128, 512)


def _layer_fwd(x, pos, p):
    xn, hb, hf = _inproj_fwd(x, p["norm_pre"], p["w_in"])
    ya = _swa_fwd(hb, p["attn_sinks"])
    qm, km, vm = _mla_prep_fwd(hf, pos, p["mla_q_norm"], p["mla_kv_norm"], p["mla_w_uq"], p["mla_w_ukv"])
    yc, lse = _mla_fwd(qm, km, vm)
    yd, tot, cnt = _sb_fwd(hb)
    x_next = _epilogue_fwd(x, ya, yc, yd, hf, p["conv_w"], p["conv_b"], p["group_norm"], p["w_out"], p["norm_post"])
    return x_next, dict(x=x, xn=xn, hb=hb, hf=hf, ya=ya, yc=yc, yd=yd, tot=tot, cnt=cnt, qm=qm, km=km, vm=vm, lse=lse)


def _layer_bwd(dx_next, pos, p, s):
    (dya, dyc, dyd, dhf, dw_out, dg_post, dg_grp, dconv_w, dconv_b) = _epilogue_bwd(
        dx_next, s["ya"], s["yc"], s["yd"], s["hf"], p["conv_w"], p["conv_b"], p["group_norm"], p["w_out"],
        p["norm_post"])
    dq_d, dk_d, dv_d = _sb_bwd(s["hb"], s["tot"], s["cnt"], dyd)
    dqm, dkm, dvm = _mla_bwd(s["qm"], s["km"], s["vm"], s["yc"], s["lse"], dyc)
    dc, dw_uq, dw_ukv, dg_q, dg_kv = _mla_prep_bwd(
        s["hf"], pos, p["mla_q_norm"], p["mla_kv_norm"], p["mla_w_uq"], p["mla_w_ukv"], dqm, dkm, dvm)
    dq_a, dk_a, dv_a, dsinks = _swa_bwd(s["hb"], p["attn_sinks"], dya)
    dx, dh, dg_pre = _inproj_bwd_dx(s["x"], p["norm_pre"], p["w_in"], dx_next,
                                    [dq_a, dk_a, dv_a, dq_d, dk_d, dv_d, dhf, dc])
    dw_in = _matmul_tn(s["xn"], dh, "inproj_bwd_dw")
    grads = dict(norm_pre=dg_pre[0], w_in=_w_in_real(dw_in), attn_sinks=dsinks[0, :4], conv_w=dconv_w[:3],
                 conv_b=dconv_b[0], mla_q_norm=dg_q[0], mla_w_uq=_uq_real(dw_uq), mla_kv_norm=dg_kv[0],
                 mla_w_ukv=_ukv_real(dw_ukv), group_norm=dg_grp[0], w_out=dw_out, norm_post=dg_post[0])
    return dx, grads


_WEIGHTS = ["norm_pre", "w_in", "attn_sinks", "conv_w", "conv_b", "mla_q_norm", "mla_w_uq", "mla_kv_norm",
            "mla_w_ukv", "group_norm", "w_out", "norm_post"]


def kernel(x, positions, norm_pre, w_in, attn_sinks, conv_w, conv_b, mla_q_norm, mla_w_uq, mla_kv_norm, mla_w_ukv, group_norm, w_out, norm_post, loss_target, m_norm_pre, m_w_in, m_attn_sinks, m_conv_w, m_conv_b, m_mla_q_norm, m_mla_w_uq, m_mla_kv_norm, m_mla_w_ukv, m_group_norm, m_w_out, m_norm_post, v_norm_pre, v_w_in, v_attn_sinks, v_conv_w, v_conv_b, v_mla_q_norm, v_mla_w_uq, v_mla_kv_norm, v_mla_w_ukv, v_group_norm, v_w_out, v_norm_post):
    w = dict(norm_pre=norm_pre, w_in=w_in, attn_sinks=attn_sinks, conv_w=conv_w, conv_b=conv_b,
             mla_q_norm=mla_q_norm, mla_w_uq=mla_w_uq, mla_kv_norm=mla_kv_norm, mla_w_ukv=mla_w_ukv,
             group_norm=group_norm, w_out=w_out, norm_post=norm_post)
    m = dict(norm_pre=m_norm_pre, w_in=m_w_in, attn_sinks=m_attn_sinks, conv_w=m_conv_w, conv_b=m_conv_b,
             mla_q_norm=m_mla_q_norm, mla_w_uq=m_mla_w_uq, mla_kv_norm=m_mla_kv_norm, mla_w_ukv=m_mla_w_ukv,
             group_norm=m_group_norm, w_out=m_w_out, norm_post=m_norm_post)
    v = dict(norm_pre=v_norm_pre, w_in=v_w_in, attn_sinks=v_attn_sinks, conv_w=v_conv_w, conv_b=v_conv_b,
             mla_q_norm=v_mla_q_norm, mla_w_uq=v_mla_w_uq, mla_kv_norm=v_mla_kv_norm, mla_w_ukv=v_mla_w_ukv,
             group_norm=v_group_norm, w_out=v_w_out, norm_post=v_norm_post)
    T = x.shape[1]
    xs = x[0]
    pos = positions[0].reshape(T, 1)
    tgt = loss_target[0]
    core = lax.axis_index("c")

    gathered = _gather_weights([w[n].astype(BF16) for n in _BIG[:4]] + [w["conv_w"]])
    full = {}
    for n, got in zip(_BIG, gathered):
        if n == "w_out":
            full[n] = jnp.moveaxis(got, 0, 1).reshape(DEPTH, D_MODEL, D_MODEL)
        else:
            full[n] = jnp.transpose(got, (1, 2, 0, 3)).reshape(DEPTH, got.shape[2], 4 * got.shape[3])

    layers = []
    for l in range(DEPTH):
        layers.append(dict(
            norm_pre=norm_pre[l:l + 1], w_in=_w_in_internal(full["w_in"][l]), attn_sinks=attn_sinks[l],
            conv_w=full["conv_w"][l], conv_b=conv_b[l:l + 1], mla_q_norm=mla_q_norm[l:l + 1],
            mla_w_uq=_uq_internal(full["mla_w_uq"][l]), mla_kv_norm=mla_kv_norm[l:l + 1],
            mla_w_ukv=_ukv_internal(full["mla_w_ukv"][l]), group_norm=group_norm[l:l + 1],
            w_out=full["w_out"][l], norm_post=norm_post[l:l + 1]))

    saved = []
    h = xs
    for l in range(DEPTH):
        h, s = _layer_fwd(h, pos, layers[l])
        saved.append(s)
    dy, loss_part = _loss_head(h, tgt)
    loss = lax.psum(loss_part[0, 0], ("x", "y", "c"))

    grads = [None] * DEPTH
    for l in reversed(range(DEPTH)):
        dy, grads[l] = _layer_bwd(dy, pos, layers[l], saved[l])

    def chunks(n, a):
        if n == "w_out":
            return a.reshape(4, D_MODEL // 4, D_MODEL)
        return jnp.transpose(a.reshape(a.shape[0], 4, a.shape[1] // 4), (1, 0, 2))

    mine = [chunks(n, jnp.where(core == 0, grads[0][n], grads[1][n])) for n in _BIG]
    theirs = [chunks(n, jnp.where(core == 0, grads[1][n], grads[0][n])) for n in _BIG]
    from_sibling = _swap_cores(theirs, "swap_layer_chunks")
    summed = [_add(a, b, "add_cores_" + n) for n, a, b in zip(_BIG, mine, from_sibling)]
    small = _pack_small({n: jnp.stack([grads[l][n] for l in range(DEPTH)]) for n in _SMALL})
    *got, got_small = _exchange_chips(summed, small)
    done = [_sum_leading(b, "sum_chips_" + n) for n, b in zip(_BIG, got)]
    done_other = _swap_cores(done, "swap_layer_shards")

    outs = _adamw_small(w, m, v, got_small)
    for n, gm, go in zip(_BIG, done, done_other):
        for d, a in zip(outs, _adamw_layers(w[n], m[n], v[n], gm, go, "adamw_" + n)):
            d[n] = a
    return (loss, dy[None], *[outs[0][n] for n in _WEIGHTS], *[outs[1][n] for n in _WEIGHTS],
            *[outs[2][n] for n in _WEIGHTS], *[outs[3][n] for n in _WEIGHTS])
```

```python
import math

import jax
import jax.numpy as jnp
from jax import lax
from jax.experimental import pallas as pl
from jax.experimental.pallas import tpu as pltpu

F32 = jnp.float32
BF16 = jnp.bfloat16
MESH = pl.DeviceIdType.MESH

D_MODEL = 1024
DEPTH = 2
EPS = 1e-6
BLOCK = 128
HEAD = 64
LANES = 128
GROUP = 256
MLA_SCALE = 96 ** -0.5
ROPE_HALF = 16
ROPE_THETA = 10000.0
ATT_BLK = 256
MLA_BQ = 512
NEG = -1e30
SB_DEAD = -104.0

ADAM_LR, ADAM_B1, ADAM_B2, ADAM_EPS, ADAM_WD, ADAM_STEP = 0.001, 0.9, 0.999, 1e-08, 0.01, 10

_REAL = [("a_q", 256), ("a_k", 128), ("a_v", 128), ("b_b", 256), ("b_c", 256), ("b_x", 256),
         ("c_q", 256), ("c_kv", 128), ("c_kr", 32), ("d_q", 256), ("d_k", 256), ("d_v", 256),
         ("gate", 1024)]
_REAL_OFF = {}
_o = 0
for _n, _w in _REAL:
    _REAL_OFF[_n] = (_o, _w)
    _o += _w
D_IN = _o
_INT_ORDER = ["a_q", "a_k", "a_v", "d_q", "d_k", "d_v", "gate", "b_b", "b_c", "b_x", "c_q", "c_kv", "c_kr"]
_INT_W = dict(_REAL)
_INT_W["c_kr"] = 128
_INT_OFF = {}
_o = 0
for _n in _INT_ORDER:
    _INT_OFF[_n] = _o
    _o += _INT_W[_n]
N_INT = _o
N_HB = _INT_OFF["gate"]
N_HF = N_INT - N_HB

VMEM_LIMIT = 56 * 1024 * 1024


def _cparams(sem):
    return pltpu.CompilerParams(dimension_semantics=sem, vmem_limit_bytes=VMEM_LIMIT)


def _dot(a, b):
    return jnp.dot(a, b, preferred_element_type=F32)


def _dot_nt(a, b):
    return lax.dot_general(a, b, (((1,), (1,)), ((), ())), preferred_element_type=F32)


def _dot_tn(a, b):
    return lax.dot_general(a, b, (((0,), (0,)), ((), ())), preferred_element_type=F32)


def _split(x):
    hi = x.astype(BF16)
    lo = (x - hi.astype(F32)).astype(BF16)
    return hi, lo


def _rms(x):
    return lax.rsqrt(jnp.mean(x * x, axis=-1, keepdims=True) + EPS)


def _rms_bwd(dy, xhat, r, g):
    dxhat = dy * g
    return r * (dxhat - xhat * jnp.mean(dxhat * xhat, axis=-1, keepdims=True)), dy * xhat


def _colsum(x):
    return jnp.sum(x, axis=0, keepdims=True)


def _inproj_fwd(x, g, w):
    T = x.shape[0]
    tm = 256

    def body(x_ref, g_ref, w_ref, xn_ref, hb_ref, hf_ref):
        xv = x_ref[...]
        xn = (xv * _rms(xv) * g_ref[...]).astype(BF16)
        xn_ref[...] = xn
        h = _dot(xn, w_ref[...])
        hb_ref[...] = h[:, :N_HB].astype(BF16)
        hf_ref[...] = h[:, N_HB:]

    return pl.pallas_call(
        body, name="inproj_fwd", grid=(T // tm,),
        in_specs=[pl.BlockSpec((tm, D_MODEL), lambda i: (i, 0)),
                  pl.BlockSpec((1, D_MODEL), lambda i: (0, 0)),
                  pl.BlockSpec((D_MODEL, N_INT), lambda i: (0, 0))],
        out_specs=[pl.BlockSpec((tm, D_MODEL), lambda i: (i, 0)),
                   pl.BlockSpec((tm, N_HB), lambda i: (i, 0)),
                   pl.BlockSpec((tm, N_HF), lambda i: (i, 0))],
        out_shape=[jax.ShapeDtypeStruct((T, D_MODEL), BF16),
                   jax.ShapeDtypeStruct((T, N_HB), BF16),
                   jax.ShapeDtypeStruct((T, N_HF), F32)],
        compiler_params=_cparams(("parallel",)),
    )(x, g, w)


def _inproj_bwd_dx(x, g, w, dx_next, pieces):
    T = x.shape[0]
    tm = 256
    widths = [p.shape[1] for p in pieces]
    assert sum(widths) == N_INT

    def body(x_ref, g_ref, w_ref, dxn_ref, *rest):
        p_refs = rest[:len(pieces)]
        dx_ref, dh_ref, dg_ref = rest[len(pieces):]
        dh = jnp.concatenate([p[...].astype(BF16) for p in p_refs], axis=1)
        dh_ref[...] = dh
        dxn = _dot_nt(dh, w_ref[...])
        xv = x_ref[...]
        r = _rms(xv)
        dx, dgrow = _rms_bwd(dxn, xv * r, r, g_ref[...])
        dx_ref[...] = dx + dxn_ref[...]

        @pl.when(pl.program_id(0) == 0)
        def _():
            dg_ref[...] = jnp.zeros_like(dg_ref)

        dg_ref[...] += _colsum(dgrow)

    return pl.pallas_call(
        body, name="inproj_bwd_dx", grid=(T // tm,),
        in_specs=[pl.BlockSpec((tm, D_MODEL), lambda i: (i, 0)),
                  pl.BlockSpec((1, D_MODEL), lambda i: (0, 0)),
                  pl.BlockSpec((D_MODEL, N_INT), lambda i: (0, 0)),
                  pl.BlockSpec((tm, D_MODEL), lambda i: (i, 0))]
                 + [pl.BlockSpec((tm, wd), lambda i: (i, 0)) for wd in widths],
        out_specs=[pl.BlockSpec((tm, D_MODEL), lambda i: (i, 0)),
                   pl.BlockSpec((tm, N_INT), lambda i: (i, 0)),
                   pl.BlockSpec((1, D_MODEL), lambda i: (0, 0))],
        out_shape=[jax.ShapeDtypeStruct((T, D_MODEL), F32),
                   jax.ShapeDtypeStruct((T, N_INT), BF16),
                   jax.ShapeDtypeStruct((1, D_MODEL), F32)],
        compiler_params=_cparams(("arbitrary",)),
    )(x, g, w, dx_next, *pieces)


def _matmul_tn(a, b, name):
    T, M = a.shape
    N = b.shape[1]
    tm, tn = 512, 512

    def body(a_ref, b_ref, o_ref):
        @pl.when(pl.program_id(1) == 0)
        def _():
            o_ref[...] = jnp.zeros_like(o_ref)

        o_ref[...] += _dot_tn(a_ref[...], b_ref[...])

    return pl.pallas_call(
        body, name=name, grid=(N // tn, T // tm),
        in_specs=[pl.BlockSpec((tm, M), lambda j, t: (t, 0)),
                  pl.BlockSpec((tm, tn), lambda j, t: (t, j))],
        out_specs=pl.BlockSpec((M, tn), lambda j, t: (0, j)),
        out_shape=jax.ShapeDtypeStruct((M, N), F32),
        compiler_params=_cparams(("parallel", "arbitrary")),
    )(a, b)


def _roll_f32(x, shift):
    return pltpu.roll(x.astype(F32), shift, 1)


def _swa_operands(h, q_ref, kp_ref, kc_ref, vp_ref, vc_ref):
    p, e = h // 2, h % 2
    lane = lax.broadcasted_iota(jnp.int32, (1, LANES), 1) // HEAD
    q = q_ref[:, p * LANES:(p + 1) * LANES]
    k_prev, k_cur, v_prev, v_cur = kp_ref[...], kc_ref[...], vp_ref[...], vc_ref[...]
    if e != p:
        q = _roll_f32(q, HEAD).astype(BF16)
        v_prev = _roll_f32(v_prev, HEAD).astype(BF16)
        v_cur = _roll_f32(v_cur, HEAD).astype(BF16)
    qs = jnp.where(lane == p, q, 0) * 0.125
    return dict(p=p, e=e, lane=lane, qs=qs, k_prev=k_prev, k_cur=k_cur,
                v_prev=jnp.where(lane == e, v_prev, 0), v_cur=jnp.where(lane == e, v_cur, 0),
                s_prev=_dot_nt(qs, k_prev), s_cur=_dot_nt(qs, k_cur))


def _swa_probs(ops, sink, first):
    row = lax.broadcasted_iota(jnp.int32, (BLOCK, BLOCK), 0)
    col = lax.broadcasted_iota(jnp.int32, (BLOCK, BLOCK), 1)
    s_prev = jnp.where(jnp.logical_and(col > row, jnp.logical_not(first)), ops["s_prev"], NEG)
    s_cur = jnp.where(col <= row, ops["s_cur"], NEG)
    m = jnp.maximum(jnp.maximum(jnp.max(s_prev, axis=1, keepdims=True),
                                jnp.max(s_cur, axis=1, keepdims=True)), sink)
    p_prev = jnp.exp(s_prev - m)
    p_cur = jnp.exp(s_cur - m)
    p_sink = jnp.exp(sink - m)
    inv = 1.0 / (jnp.sum(p_prev, axis=1, keepdims=True) + jnp.sum(p_cur, axis=1, keepdims=True) + p_sink)
    return p_prev * inv, p_cur * inv, p_sink * inv


def _swa_specs(T):
    nb = T // BLOCK
    qo, ko, vo = (_INT_OFF[n] // LANES for n in ("a_q", "a_k", "a_v"))
    prev = lambda i: jnp.maximum(i - 1, 0)
    return [pl.BlockSpec((BLOCK, 256), lambda i: (i, qo // 2)),
            pl.BlockSpec((BLOCK, LANES), lambda i: (prev(i), ko)),
            pl.BlockSpec((BLOCK, LANES), lambda i: (i, ko)),
            pl.BlockSpec((BLOCK, LANES), lambda i: (prev(i), vo)),
            pl.BlockSpec((BLOCK, LANES), lambda i: (i, vo)),
            pl.BlockSpec(memory_space=pltpu.SMEM)], nb


def _swa_fwd(hb, sinks):
    T = hb.shape[0]
    specs, nb = _swa_specs(T)

    def body(q_ref, kp_ref, kc_ref, vp_ref, vc_ref, s_ref, o_ref):
        first = pl.program_id(0) == 0
        ops = [_swa_operands(h, q_ref, kp_ref, kc_ref, vp_ref, vc_ref) for h in range(4)]
        probs = [_swa_probs(ops[h], s_ref[h], first) for h in range(4)]
        outs = [_dot(probs[h][0].astype(BF16), ops[h]["v_prev"]) + _dot(probs[h][1].astype(BF16), ops[h]["v_cur"])
                for h in range(4)]
        for p in range(2):
            o_ref[:, p * LANES:(p + 1) * LANES] = outs[2 * p] + outs[2 * p + 1]

    return pl.pallas_call(
        body, name="swa_fwd", grid=(nb,), in_specs=specs,
        out_specs=pl.BlockSpec((BLOCK, 256), lambda i: (i, 0)),
        out_shape=jax.ShapeDtypeStruct((T, 256), F32),
        compiler_params=_cparams(("parallel",)),
    )(hb, hb, hb, hb, hb, sinks)


def _swa_bwd(hb, sinks, dy):
    T = hb.shape[0]
    specs, nb = _swa_specs(T)

    def body(q_ref, kp_ref, kc_ref, vp_ref, vc_ref, s_ref, dy_ref, dq_ref, dk_ref, dv_ref, ds_ref):
        i = pl.program_id(0)
        first = i == 0
        cur = pl.ds(pl.multiple_of(i * BLOCK, BLOCK), BLOCK)
        prv = pl.ds(pl.multiple_of(jnp.maximum(i - 1, 0) * BLOCK, BLOCK), BLOCK)

        @pl.when(first)
        def _():
            ds_ref[...] = jnp.zeros_like(ds_ref)

        dk_ref[cur, :] = jnp.zeros((BLOCK, LANES), F32)
        dv_ref[cur, :] = jnp.zeros((BLOCK, LANES), F32)
        lane_id = lax.broadcasted_iota(jnp.int32, (8, LANES), 1)
        heads = range(4)
        ops = [_swa_operands(h, q_ref, kp_ref, kc_ref, vp_ref, vc_ref) for h in heads]
        probs = [_swa_probs(ops[h], s_ref[h], first) for h in heads]
        dos = [jnp.where(ops[h]["lane"] == ops[h]["e"], dy_ref[:, ops[h]["p"] * LANES:(ops[h]["p"] + 1) * LANES], 0.0)
               for h in heads]
        dobs = [d.astype(BF16) for d in dos]
        pbs = [(probs[h][0].astype(BF16), probs[h][1].astype(BF16)) for h in heads]
        outs = [_dot(pbs[h][0], ops[h]["v_prev"]) + _dot(pbs[h][1], ops[h]["v_cur"]) for h in heads]
        dps = [(_dot_nt(dobs[h], ops[h]["v_prev"]), _dot_nt(dobs[h], ops[h]["v_cur"])) for h in heads]
        dss, dsinks = [], jnp.zeros((8, LANES), F32)
        for h in heads:
            delta = jnp.sum(dos[h] * outs[h], axis=1, keepdims=True)
            dss.append(((probs[h][0] * (dps[h][0] - delta)).astype(BF16),
                        (probs[h][1] * (dps[h][1] - delta)).astype(BF16)))
            dsink = -jnp.sum(probs[h][2] * delta, axis=0, keepdims=True)
            dsinks += jnp.where(lane_id == h, dsink, 0.0)
        ds_ref[...] += dsinks
        dqs = [(_dot(dss[h][0], ops[h]["k_prev"]) + _dot(dss[h][1], ops[h]["k_cur"])) * 0.125 for h in heads]
        dk_prev = dk_cur = dv_prev = dv_cur = jnp.zeros((BLOCK, LANES), F32)
        for h in heads:
            p, e = ops[h]["p"], ops[h]["e"]
            dob_v = dobs[h] if e == p else pltpu.roll(dos[h], HEAD, 1).astype(BF16)
            dk_prev += _dot_tn(dss[h][0], ops[h]["qs"])
            dk_cur += _dot_tn(dss[h][1], ops[h]["qs"])
            dv_prev += _dot_tn(pbs[h][0], dob_v)
            dv_cur += _dot_tn(pbs[h][1], dob_v)
        dk_ref[prv, :] += dk_prev
        dk_ref[cur, :] += dk_cur
        dv_ref[prv, :] += dv_prev
        dv_ref[cur, :] += dv_cur
        for p in range(2):
            dq_pair = jnp.zeros((BLOCK, LANES), F32)
            for e in range(2):
                dq = jnp.where(ops[2 * p + e]["lane"] == p, dqs[2 * p + e], 0.0)
                dq_pair += dq if e == p else pltpu.roll(dq, HEAD, 1)
            dq_ref[:, p * LANES:(p + 1) * LANES] = dq_pair

    return pl.pallas_call(
        body, name="swa_bwd", grid=(nb,),
        in_specs=specs + [pl.BlockSpec((BLOCK, 256), lambda i: (i, 0))],
        out_specs=[pl.BlockSpec((BLOCK, 256), lambda i: (i, 0)),
                   pl.BlockSpec((T, LANES), lambda i: (0, 0)),
                   pl.BlockSpec((T, LANES), lambda i: (0, 0)),
                   pl.BlockSpec((8, LANES), lambda i: (0, 0))],
        out_shape=[jax.ShapeDtypeStruct((T, 256), F32),
                   jax.ShapeDtypeStruct((T, LANES), F32),
                   jax.ShapeDtypeStruct((T, LANES), F32),
                   jax.ShapeDtypeStruct((8, LANES), F32)],
        compiler_params=_cparams(("arbitrary",)),
    )(hb, hb, hb, hb, hb, sinks, dy)


def _rope_tables(pos_ref):
    lane = lax.broadcasted_iota(jnp.int32, (1, LANES), 1)
    active = jnp.logical_and(lane >= HEAD, lane < HEAD + 2 * ROPE_HALF)
    idx = ((lane - HEAD) % ROPE_HALF).astype(F32)
    freq = jnp.exp(idx * (-math.log(ROPE_THETA) / ROPE_HALF))
    ang = pos_ref[...].astype(F32) * freq
    cos, sin = jnp.cos(ang), jnp.sin(ang)
    c = jnp.where(active, cos, 1.0)
    s_up = jnp.where(jnp.logical_and(active, lane >= HEAD + ROPE_HALF), sin, 0.0)
    s_dn = jnp.where(jnp.logical_and(active, lane < HEAD + ROPE_HALF), -sin, 0.0)
    return c, s_up, s_dn


def _rope(x, tabs):
    c, s_up, s_dn = tabs
    return x * c + pltpu.roll(x, ROPE_HALF, 1) * s_up + pltpu.roll(x, LANES - ROPE_HALF, 1) * s_dn


def _rope_t(dy, tabs):
    c, s_up, s_dn = tabs
    return dy * c + pltpu.roll(dy * s_up, LANES - ROPE_HALF, 1) + pltpu.roll(dy * s_dn, ROPE_HALF, 1)


def _mla_lat_specs(tm):
    cq, ckv, ckr = ((_INT_OFF[n] - N_HB) for n in ("c_q", "c_kv", "c_kr"))
    return [pl.BlockSpec((tm, 256), lambda i: (i, cq // 256)),
            pl.BlockSpec((tm, LANES), lambda i: (i, ckv // LANES)),
            pl.BlockSpec((tm, LANES), lambda i: (i, ckr // LANES)),
            pl.BlockSpec((tm, 1), lambda i: (i, 0)),
            pl.BlockSpec((1, 256), lambda i: (0, 0)),
            pl.BlockSpec((1, LANES), lambda i: (0, 0)),
            pl.BlockSpec((256, 512), lambda i: (0, 0)),
            pl.BlockSpec((LANES, 768), lambda i: (0, 0))]


def _mla_prep_fwd(hf, pos, g_q, g_kv, w_uq, w_ukv):
    T = hf.shape[0]
    tm = 512

    def body(cq_ref, ckv_ref, ckr_ref, pos_ref, gq_ref, gkv_ref, wq_ref, wkv_ref, qm_ref, km_ref, vm_ref):
        tabs = _rope_tables(pos_ref)
        cq = cq_ref[...]
        q = _dot((cq * _rms(cq) * gq_ref[...]).astype(BF16), wq_ref[...])
        ckv = ckv_ref[...]
        kv = _dot((ckv * _rms(ckv) * gkv_ref[...]).astype(BF16), wkv_ref[...])
        kr = _rope(pltpu.roll(ckr_ref[...], HEAD, 1), tabs)
        for h in range(4):
            sl = slice(h * LANES, (h + 1) * LANES)
            qm_ref[:, sl] = (_rope(q[:, sl], tabs) * MLA_SCALE).astype(BF16)
            km_ref[:, sl] = (kv[:, sl] + kr).astype(BF16)
        vm_ref[...] = kv[:, 512:].astype(BF16)

    return pl.pallas_call(
        body, name="mla_prep_fwd", grid=(T // tm,), in_specs=_mla_lat_specs(tm),
        out_specs=[pl.BlockSpec((tm, 512), lambda i: (i, 0)),
                   pl.BlockSpec((tm, 512), lambda i: (i, 0)),
                   pl.BlockSpec((tm, 256), lambda i: (i, 0))],
        out_shape=[jax.ShapeDtypeStruct((T, 512), BF16),
                   jax.ShapeDtypeStruct((T, 512), BF16),
                   jax.ShapeDtypeStruct((T, 256), BF16)],
        compiler_params=_cparams(("parallel",)),
    )(hf, hf, hf, pos, g_q, g_kv, w_uq, w_ukv)


def _mla_prep_bwd(hf, pos, g_q, g_kv, w_uq, w_ukv, dqm, dkm, dvm):
    T = hf.shape[0]
    tm = 512

    def body(cq_ref, ckv_ref, ckr_ref, pos_ref, gq_ref, gkv_ref, wq_ref, wkv_ref, dq_ref, dk_ref, dv_ref,
             dc_ref, dwq_ref, dwkv_ref, dgq_ref, dgkv_ref):
        @pl.when(pl.program_id(0) == 0)
        def _():
            dwq_ref[...] = jnp.zeros_like(dwq_ref)
            dwkv_ref[...] = jnp.zeros_like(dwkv_ref)
            dgq_ref[...] = jnp.zeros_like(dgq_ref)
            dgkv_ref[...] = jnp.zeros_like(dgkv_ref)

        tabs = _rope_tables(pos_ref)
        lane = lax.broadcasted_iota(jnp.int32, (1, LANES), 1)
        dq = jnp.concatenate([_rope_t(dq_ref[:, h * LANES:(h + 1) * LANES] * MLA_SCALE, tabs)
                              for h in range(4)], axis=1).astype(BF16)
        cq = cq_ref[...]
        rq = _rms(cq)
        cqn = (cq * rq * gq_ref[...]).astype(BF16)
        dwq_ref[...] += _dot_tn(cqn, dq)
        dcq, dgrow = _rms_bwd(_dot_nt(dq, wq_ref[...]), cq * rq, rq, gq_ref[...])
        dgq_ref[...] += _colsum(dgrow)
        dc_ref[:, 0:256] = dcq

        dk = dk_ref[...]
        dkr = dk[:, 0:LANES] + dk[:, LANES:2 * LANES] + dk[:, 2 * LANES:3 * LANES] + dk[:, 3 * LANES:]
        dkr = pltpu.roll(_rope_t(dkr, tabs), HEAD, 1)
        dc_ref[:, 384:512] = jnp.where(lane < 2 * ROPE_HALF, dkr, 0.0)
        dkv = jnp.concatenate([dk.astype(BF16), dv_ref[...].astype(BF16)], axis=1)
        ckv = ckv_ref[...]
        rkv = _rms(ckv)
        ckvn = (ckv * rkv * gkv_ref[...]).astype(BF16)
        dwkv_ref[...] += _dot_tn(ckvn, dkv)
        dckv, dgrow = _rms_bwd(_dot_nt(dkv, wkv_ref[...]), ckv * rkv, rkv, gkv_ref[...])
        dgkv_ref[...] += _colsum(dgrow)
        dc_ref[:, 256:384] = dckv

    return pl.pallas_call(
        body, name="mla_prep_bwd", grid=(T // tm,),
        in_specs=_mla_lat_specs(tm) + [pl.BlockSpec((tm, 512), lambda i: (i, 0)),
                                       pl.BlockSpec((tm, 512), lambda i: (i, 0)),
                                       pl.BlockSpec((tm, 256), lambda i: (i, 0))],
        out_specs=[pl.BlockSpec((tm, 512), lambda i: (i, 0)),
                   pl.BlockSpec((256, 512), lambda i: (0, 0)),
                   pl.BlockSpec((LANES, 768), lambda i: (0, 0)),
                   pl.BlockSpec((1, 256), lambda i: (0, 0)),
                   pl.BlockSpec((1, LANES), lambda i: (0, 0))],
        out_shape=[jax.ShapeDtypeStruct((T, 512), F32),
                   jax.ShapeDtypeStruct((256, 512), F32),
                   jax.ShapeDtypeStruct((LANES, 768), F32),
                   jax.ShapeDtypeStruct((1, 256), F32),
                   jax.ShapeDtypeStruct((1, LANES), F32)],
        compiler_params=_cparams(("arbitrary",)),
    )(hf, hf, hf, pos, g_q, g_kv, w_uq, w_ukv, dqm, dkm, dvm)


def _causal_masks(bq, bk):
    row = lax.broadcasted_iota(jnp.int32, (bq, bk), 0)
    col = lax.broadcasted_iota(jnp.int32, (bq, bk), 1)
    return row, col


def _mla_fwd(qm, km, vm):
    T = qm.shape[0]
    bq, bk = min(MLA_BQ, T), ATT_BLK
    nq, nsub = T // bq, bq // bk

    def body(q_ref, k_ref, v_ref, o_ref, lse_ref, acc_ref, m_ref):
        qi = pl.program_id(1)
        lane = lax.broadcasted_iota(jnp.int32, (1, LANES), 1) // HEAD
        row, col = _causal_masks(bq, bk)
        acc_ref[...] = jnp.zeros_like(acc_ref)
        m_ref[...] = jnp.full_like(m_ref, NEG)

        def scores(kb):
            rows = pl.ds(pl.multiple_of(kb * bk, bk), bk)
            return tuple(_dot_nt(q_ref[:, e * LANES:(e + 1) * LANES], k_ref[rows, e * LANES:(e + 1) * LANES])
                         for e in range(2))

        def step(kb, ss, causal):
            v_pair = v_ref[pl.ds(pl.multiple_of(kb * bk, bk), bk), :]
            ps, alphas = [], []
            for e in range(2):
                s = ss[e] if causal is None else jnp.where(causal, ss[e], NEG)
                m_prev = m_ref[e]
                m_new = jnp.maximum(m_prev, jnp.max(s, axis=1, keepdims=True))
                ps.append(jnp.exp(s - jnp.concatenate([m_new] * (bk // LANES), axis=1)).astype(BF16))
                alphas.append(jnp.exp(m_prev - m_new))
                m_ref[e] = m_new
            for e in range(2):
                v = jnp.where(lane == e, v_pair, 1)
                acc_ref[e] = alphas[e] * acc_ref[e] + _dot(ps[e], v)

        for d in range(nsub):
            step(qi * nsub + d, scores(qi * nsub + d), col + d * bk <= row)

        def loop(kb, c):
            step(kb, scores(kb), None)
            return c

        lax.fori_loop(0, qi * nsub, loop, 0)
        out = jnp.zeros((bq, LANES), F32)
        lse = jnp.zeros((bq, LANES), F32)
        for e in range(2):
            acc = acc_ref[e]
            l = pltpu.roll(acc, HEAD, 1)
            out = jnp.where(lane == e, acc / l, out)
            lse = jnp.where(lane == e, m_ref[e] + jnp.log(l), lse)
        o_ref[...] = out
        lse_ref[...] = lse

    return pl.pallas_call(
        body, name="mla_fwd", grid=(2, nq),
        in_specs=[pl.BlockSpec((bq, 256), lambda j, i: (i, j)),
                  pl.BlockSpec((T, 256), lambda j, i: (0, j)),
                  pl.BlockSpec((T, LANES), lambda j, i: (0, j))],
        out_specs=[pl.BlockSpec((bq, LANES), lambda j, i: (i, j)),
                   pl.BlockSpec((bq, LANES), lambda j, i: (i, j))],
        out_shape=[jax.ShapeDtypeStruct((T, 256), F32), jax.ShapeDtypeStruct((T, 256), F32)],
        scratch_shapes=[pltpu.VMEM((2, bq, LANES), F32), pltpu.VMEM((2, bq, LANES), F32)],
        compiler_params=_cparams(("parallel", "arbitrary")),
    )(qm, km, vm)


def _mla_bwd(qm, km, vm, y, lse, dy):
    T = qm.shape[0]
    bq, bk = min(MLA_BQ, T), ATT_BLK
    nq, nsub, nk = T // bq, bq // bk, T // bk

    def body(q_ref, k_ref, v_ref, y_ref, lse_ref, dy_ref, dq_ref, dkt_ref, dvt_ref, dob_ref, st_ref, qt_ref, dot_ref):
        qi = pl.program_id(1)

        @pl.when(qi == 0)
        def _():
            dkt_ref[...] = jnp.zeros_like(dkt_ref)
            dvt_ref[...] = jnp.zeros_like(dvt_ref)

        lane = lax.broadcasted_iota(jnp.int32, (1, LANES), 1) // HEAD
        row, col = _causal_masks(bq, bk)
        dq_ref[...] = jnp.zeros_like(dq_ref)
        lse = lse_ref[...]
        lse_other = pltpu.roll(lse, HEAD, 1)
        qt_ref[...] = jnp.transpose(q_ref[...].astype(F32)).astype(BF16)
        dot_ref[...] = jnp.transpose(dy_ref[...]).astype(BF16)
        for e in range(2):
            do = jnp.where(lane == e, dy_ref[...], 0.0)
            dob_ref[e] = do.astype(BF16)
            st_ref[2 * e] = jnp.where(lane == e, lse, lse_other)
            st_ref[2 * e + 1] = jnp.broadcast_to(jnp.sum(do * y_ref[...], axis=1, keepdims=True), (bq, LANES))

        hss = [slice(e * LANES, (e + 1) * LANES) for e in range(2)]
        tile = lambda a: jnp.concatenate([a] * (bk // LANES), axis=1)

        def scores(kb):
            rows = pl.ds(pl.multiple_of(kb * bk, bk), bk)
            v_pair = v_ref[rows, :]
            return (tuple(_dot_nt(q_ref[:, hss[e]], k_ref[rows, hss[e]]) for e in range(2))
                    + tuple(_dot_nt(dob_ref[e], jnp.where(lane == e, v_pair, 0)) for e in range(2)))

        def step(kb, sc, causal):
            rows = pl.ds(pl.multiple_of(kb * bk, bk), bk)
            ps, dss = [], []
            for e in range(2):
                s = sc[e] if causal is None else jnp.where(causal, sc[e], NEG)
                p = jnp.exp(s - tile(st_ref[2 * e]))
                dss.append((p * (sc[2 + e] - tile(st_ref[2 * e + 1]))).astype(BF16))
                ps.append(p.astype(BF16))
            for e in range(2):
                dvt_ref[0, kb, e * HEAD:(e + 1) * HEAD, :] += _dot(dot_ref[e * HEAD:(e + 1) * HEAD, :], ps[e])
            for e in range(2):
                dkt_ref[0, kb, hss[e], :] += _dot(qt_ref[hss[e], :], dss[e])
            for e in range(2):
                dq_ref[:, hss[e]] += _dot(dss[e], k_ref[rows, hss[e]])

        for d in range(nsub):
            step(qi * nsub + d, scores(qi * nsub + d), col + d * bk <= row)

        def loop(kb, c):
            step(kb, scores(kb), None)
            return c

        lax.fori_loop(0, qi * nsub, loop, 0)

    dqm, dkt, dvt = pl.pallas_call(
        body, name="mla_bwd", grid=(2, nq),
        in_specs=[pl.BlockSpec((bq, 256), lambda j, i: (i, j)),
                  pl.BlockSpec((T, 256), lambda j, i: (0, j)),
                  pl.BlockSpec((T, LANES), lambda j, i: (0, j)),
                  pl.BlockSpec((bq, LANES), lambda j, i: (i, j)),
                  pl.BlockSpec((bq, LANES), lambda j, i: (i, j)),
                  pl.BlockSpec((bq, LANES), lambda j, i: (i, j))],
        out_specs=[pl.BlockSpec((bq, 256), lambda j, i: (i, j)),
                   pl.BlockSpec((1, nk, 256, bk), lambda j, i: (j, 0, 0, 0)),
                   pl.BlockSpec((1, nk, LANES, bk), lambda j, i: (j, 0, 0, 0))],
        out_shape=[jax.ShapeDtypeStruct((T, 512), F32),
                   jax.ShapeDtypeStruct((2, nk, 256, bk), F32),
                   jax.ShapeDtypeStruct((2, nk, LANES, bk), F32)],
        scratch_shapes=[pltpu.VMEM((2, bq, LANES), BF16), pltpu.VMEM((4, bq, LANES), F32),
                        pltpu.VMEM((256, bq), BF16), pltpu.VMEM((LANES, bq), BF16)],
        compiler_params=_cparams(("parallel", "arbitrary")),
    )(qm, km, vm, y, lse, dy)
    return (dqm, jnp.transpose(dkt, (1, 3, 0, 2)).reshape(T, 512), jnp.transpose(dvt, (1, 3, 0, 2)).reshape(T, 256))


def _suffix_ones(n):
    r = lax.broadcasted_iota(jnp.int32, (n, n), 0)
    c = lax.broadcasted_iota(jnp.int32, (n, n), 1)
    return (r >= c).astype(BF16)


def _prefix_ones(n):
    r = lax.broadcasted_iota(jnp.int32, (n, n), 0)
    c = lax.broadcasted_iota(jnp.int32, (n, n), 1)
    return (r <= c).astype(BF16)


def _tri_sum(x, u):
    hi, lo = _split(x)
    return _dot(hi, u) + _dot(lo, u)


def _sb_specs(T, bq):
    qo, ko, vo = (_INT_OFF[n] // LANES for n in ("d_q", "d_k", "d_v"))
    return [pl.BlockSpec((bq, LANES), lambda j, i: (i, qo + j)),
            pl.BlockSpec((T, LANES), lambda j, i: (0, ko + j)),
            pl.BlockSpec((T, LANES), lambda j, i: (0, vo + j))]


def _sb_fwd(hb):
    T = hb.shape[0]
    bq = bk = ATT_BLK
    nq = T // bq

    def body(q_ref, k_ref, v_ref, o_ref, tot_ref, cnt_ref, qm_ref, car_ref):
        qi = pl.program_id(1)
        lane = lax.broadcasted_iota(jnp.int32, (1, LANES), 1) // HEAD
        row, col = _causal_masks(bq, bk)
        strict = col < row
        u = _suffix_ones(bk)
        o_ref[...] = jnp.zeros_like(o_ref)
        car_ref[...] = jnp.zeros_like(car_ref)
        for e in range(2):
            qm_ref[e] = jnp.where(lane == e, q_ref[...], 0) * 0.125

        def step(kb, masked):
            rows = pl.ds(pl.multiple_of(kb * bk, bk), bk)
            k_pair, v_pair = k_ref[rows, :], v_ref[rows, :]
            tile = lambda a: jnp.concatenate([a] * (bk // LANES), axis=1)
            zs = [_dot_nt(qm_ref[e], k_pair) for e in range(2)]
            splits = []
            for e in range(2):
                z = zs[e]
                lk = jnp.minimum(-z, 0.0) - jnp.log(1.0 + jnp.exp(-jnp.abs(z)))
                if masked:
                    lk = jnp.where(strict, lk, 0.0)
                splits.append(_split(lk))
            sufs = [_dot(hi, u) + _dot(lo, u) for hi, lo in splits]
            aas = []
            for e in range(2):
                a = jnp.exp(zs[e] + sufs[e] + tile(car_ref[e]))
                if masked:
                    a = jnp.where(strict, a, 0.0)
                aas.append(a.astype(BF16))
                car_ref[e] += jnp.broadcast_to(sufs[e][:, 0:1], (bq, LANES))
            o_ref[...] += (_dot(aas[0], jnp.where(lane == 0, v_pair, 0))
                           + _dot(aas[1], jnp.where(lane == 1, v_pair, 0)))

        step(qi, True)

        def live():
            return jnp.max(jnp.maximum(car_ref[0], car_ref[1])) >= SB_DEAD

        def cond(c):
            return jnp.logical_and(c[0] < qi, c[1])

        def loop(c):
            step(qi - 1 - c[0], False)
            return c[0] + 1, live()

        done, _ = lax.while_loop(cond, loop, (jnp.int32(0), live()))
        tot_ref[...] = jnp.where(lane == 0, car_ref[0], car_ref[1])
        cnt_ref[pl.program_id(0), qi] = done.astype(F32)

    return pl.pallas_call(
        body, name="sb_fwd", grid=(2, nq), in_specs=_sb_specs(T, bq),
        out_specs=[pl.BlockSpec((bq, LANES), lambda j, i: (i, j)), pl.BlockSpec((bq, LANES), lambda j, i: (i, j)),
                   pl.BlockSpec(memory_space=pltpu.SMEM)],
        out_shape=[jax.ShapeDtypeStruct((T, 256), F32), jax.ShapeDtypeStruct((T, 256), F32),
                   jax.ShapeDtypeStruct((2, nq), F32)],
        scratch_shapes=[pltpu.VMEM((2, bq, LANES), BF16), pltpu.VMEM((2, bq, LANES), F32)],
        compiler_params=_cparams(("parallel", "arbitrary")),
    )(hb, hb, hb)


def _sb_bwd(hb, tot, cnt, dy):
    T = hb.shape[0]
    bq = bk = ATT_BLK
    nq = T // bq

    def body(q_ref, k_ref, v_ref, tot_ref, dy_ref, cnt_ref, dq_ref, dk_ref, dv_ref, qm_ref, dob_ref, dqa_ref, rem_ref,
             cg_ref):
        qi = pl.program_id(1)

        @pl.when(qi == 0)
        def _():
            dk_ref[...] = jnp.zeros_like(dk_ref)
            dv_ref[...] = jnp.zeros_like(dv_ref)

        lane = lax.broadcasted_iota(jnp.int32, (1, LANES), 1) // HEAD
        row, col = _causal_masks(bq, bk)
        strict = col < row
        u = _prefix_ones(bk)
        tot = tot_ref[...]
        tot_other = pltpu.roll(tot, HEAD, 1)
        dqa_ref[...] = jnp.zeros_like(dqa_ref)
        cg_ref[...] = jnp.zeros_like(cg_ref)
        for e in range(2):
            qm_ref[e] = jnp.where(lane == e, q_ref[...], 0) * 0.125
            dob_ref[e] = jnp.where(lane == e, dy_ref[...], 0.0).astype(BF16)
            rem_ref[e] = jnp.where(lane == e, tot, tot_other)

        def step(kb, masked):
            rows = pl.ds(pl.multiple_of(kb * bk, bk), bk)
            k_pair, v_pair = k_ref[rows, :], v_ref[rows, :]
            tile = lambda a: jnp.concatenate([a] * (bk // LANES), axis=1)
            zs = [_dot_nt(qm_ref[e], k_pair) for e in range(2)]
            das = [_dot_nt(dob_ref[e], jnp.where(lane == e, v_pair, 0)) for e in range(2)]
            zls, splits = [], []
            for e in range(2):
                z = zs[e]
                lk = jnp.minimum(-z, 0.0) - jnp.log(1.0 + jnp.exp(-jnp.abs(z)))
                if masked:
                    lk = jnp.where(strict, lk, 0.0)
                zls.append(z + lk)
                splits.append(_split(lk))
            pres = [_dot(hi, u) + _dot(lo, u) for hi, lo in splits]
            aas, gs, gsplits = [], [], []
            for e in range(2):
                a = jnp.exp(zls[e] + (tile(rem_ref[e]) - pres[e]))
                if masked:
                    a = jnp.where(strict, a, 0.0)
                g = a * das[e]
                aas.append(a.astype(BF16))
                gs.append(g)
                gsplits.append(_split(g))
                rem_ref[e] -= jnp.broadcast_to(pres[e][:, bk - 1:bk], (bq, LANES))
            dv_ref[rows, :] += _dot_tn(aas[0], dob_ref[0]) + _dot_tn(aas[1], dob_ref[1])
            gpres = [_dot(hi, u) + _dot(lo, u) for hi, lo in gsplits]
            dzs = []
            for e in range(2):
                dz = gs[e] - jnp.exp(zls[e]) * (tile(cg_ref[e]) + gpres[e])
                if masked:
                    dz = jnp.where(strict, dz, 0.0)
                dzs.append(dz.astype(BF16))
                cg_ref[e] += jnp.broadcast_to(gpres[e][:, bk - 1:bk], (bq, LANES))
            dk_ref[rows, :] += _dot_tn(dzs[0], qm_ref[0]) + _dot_tn(dzs[1], qm_ref[1])
            for e in range(2):
                dqa_ref[e] += _dot(dzs[e], k_pair)

        def loop(kb, c):
            step(kb, False)
            return c

        start = qi - jnp.clip(cnt_ref[pl.program_id(0), qi].astype(jnp.int32), 0, qi)
        lax.fori_loop(start, qi, loop, 0)
        step(qi, True)
        dq_ref[...] = jnp.where(lane == 0, dqa_ref[0], dqa_ref[1]) * 0.125

    return pl.pallas_call(
        body, name="sb_bwd", grid=(2, nq),
        in_specs=_sb_specs(T, bq) + [pl.BlockSpec((bq, LANES), lambda j, i: (i, j)),
                                     pl.BlockSpec((bq, LANES), lambda j, i: (i, j)),
                                     pl.BlockSpec(memory_space=pltpu.SMEM)],
        out_specs=[pl.BlockSpec((bq, LANES), lambda j, i: (i, j)),
                   pl.BlockSpec((T, LANES), lambda j, i: (0, j)),
                   pl.BlockSpec((T, LANES), lambda j, i: (0, j))],
        out_shape=[jax.ShapeDtypeStruct((T, 256), F32)] * 3,
        scratch_shapes=[pltpu.VMEM((2, bq, LANES), BF16), pltpu.VMEM((2, bq, LANES), BF16),
                        pltpu.VMEM((2, bq, LANES), F32), pltpu.VMEM((2, bq, LANES), F32),
                        pltpu.VMEM((2, bq, LANES), F32)],
        compiler_params=_cparams(("parallel", "arbitrary")),
    )(hb, hb, hb, tot, dy, cnt)


EP_TM = 256


def _ep_in_specs(tm, rev):
    idx = (lambda i: rev - i) if rev is not None else (lambda i: i)
    bo = (_INT_OFF["b_b"] - N_HB) // 256
    halo = lambda i: jnp.maximum(idx(i) * (tm // 8) - 1, 0)
    return [pl.BlockSpec((tm, 256), lambda i: (idx(i), 0)),
            pl.BlockSpec((tm, 256), lambda i: (idx(i), 0)),
            pl.BlockSpec((tm, 256), lambda i: (idx(i), 0)),
            pl.BlockSpec((tm, D_MODEL), lambda i: (idx(i), 0)),
            pl.BlockSpec((tm, 256), lambda i: (idx(i), bo)),
            pl.BlockSpec((tm, 256), lambda i: (idx(i), bo + 1)),
            pl.BlockSpec((tm, 256), lambda i: (idx(i), bo + 2)),
            pl.BlockSpec((8, 256), lambda i: (halo(i), bo + 1)),
            pl.BlockSpec((8, 256), lambda i: (halo(i), bo + 2)),
            pl.BlockSpec((3, 256), lambda i: (0, 0)),
            pl.BlockSpec((1, 256), lambda i: (0, 0)),
            pl.BlockSpec((1, D_MODEL), lambda i: (0, 0)),
            pl.BlockSpec((D_MODEL, D_MODEL), lambda i: (0, 0)),
            pl.BlockSpec((1, D_MODEL), lambda i: (0, 0))]


def _ep_mix(first, ya_ref, yc_ref, yd_ref, gate_ref, bb_ref, bc_ref, bx_ref, hc_ref, hx_ref, cw_ref, cb_ref, gg_ref):
    tm = ya_ref.shape[0]
    u = bc_ref[...] * bx_ref[...]
    halo = jnp.where(first, 0.0, hc_ref[...] * hx_ref[...])
    row = lax.broadcasted_iota(jnp.int32, (tm, 1), 0)
    u1 = jnp.where(row == 0, halo[7:8, :], pltpu.roll(u, 1, 0))
    u2 = jnp.where(row == 0, halo[6:7, :], jnp.where(row == 1, halo[7:8, :], pltpu.roll(u, 2, 0)))
    cw = cw_ref[...]
    conv = cw[0:1, :] * u2 + cw[1:2, :] * u1 + cw[2:3, :] * u + cb_ref[...]
    bb = bb_ref[...]
    ys = [ya_ref[...], bb * conv, yc_ref[...], yd_ref[...]]
    rs = [_rms(y) for y in ys]
    gg = gg_ref[...]
    yhat = jnp.concatenate([y * r for y, r in zip(ys, rs)], axis=1)
    gate = gate_ref[...]
    sig = 1.0 / (1.0 + jnp.exp(-gate))
    return u, u1, u2, conv, bb, rs, yhat, yhat * gg, gate, sig


def _epilogue_fwd(x, ya, yc, yd, hf, conv_w, conv_b, g_grp, w_out, g_post):
    T = x.shape[0]
    tm = EP_TM

    def body(x_ref, ya_ref, yc_ref, yd_ref, gate_ref, bb_ref, bc_ref, bx_ref, hc_ref, hx_ref, cw_ref, cb_ref,
             gg_ref, wo_ref, gp_ref, o_ref):
        (_, _, _, _, _, _, _, yn, gate, sig) = _ep_mix(
            pl.program_id(0) == 0, ya_ref, yc_ref, yd_ref, gate_ref, bb_ref, bc_ref, bx_ref, hc_ref, hx_ref,
            cw_ref, cb_ref, gg_ref)
        z = _dot((yn * (gate * sig)).astype(BF16), wo_ref[...])
        o_ref[...] = x_ref[...] + z * _rms(z) * gp_ref[...]

    return pl.pallas_call(
        body, name="epilogue_fwd", grid=(T // tm,),
        in_specs=[pl.BlockSpec((tm, D_MODEL), lambda i: (i, 0))] + _ep_in_specs(tm, None),
        out_specs=pl.BlockSpec((tm, D_MODEL), lambda i: (i, 0)),
        out_shape=jax.ShapeDtypeStruct((T, D_MODEL), F32),
        compiler_params=_cparams(("parallel",)),
    )(x, ya, yc, yd, hf, hf, hf, hf, hf, hf, conv_w, conv_b, g_grp, w_out, g_post)


def _epilogue_bwd(dxn, ya, yc, yd, hf, conv_w, conv_b, g_grp, w_out, g_post):
    T = dxn.shape[0]
    tm = EP_TM
    nt = T // tm
    ridx = lambda i: (nt - 1 - i, 0)

    def body(dx_ref, ya_ref, yc_ref, yd_ref, gate_ref, bb_ref, bc_ref, bx_ref, hc_ref, hx_ref, cw_ref, cb_ref,
             gg_ref, wo_ref, gp_ref,
             dya_ref, dyc_ref, dyd_ref, dhf_ref, dwo_ref, dgp_ref, dgg_ref, dcw_ref, dcb_ref, carry_ref):
        i = pl.program_id(0)

        @pl.when(i == 0)
        def _():
            for r in (dwo_ref, dgp_ref, dgg_ref, dcw_ref, dcb_ref, carry_ref):
                r[...] = jnp.zeros_like(r)

        (u, u1, u2, conv, bb, rs, yhat, yn, gate, sig) = _ep_mix(
            i == nt - 1, ya_ref, yc_ref, yd_ref, gate_ref, bb_ref, bc_ref, bx_ref, hc_ref, hx_ref,
            cw_ref, cb_ref, gg_ref)
        silu = gate * sig
        ymix = (yn * silu).astype(BF16)
        z = _dot(ymix, wo_ref[...])
        rz = _rms(z)
        dz, dgrow = _rms_bwd(dx_ref[...], z * rz, rz, gp_ref[...])
        dgp_ref[...] += _colsum(dgrow)
        dzb = dz.astype(BF16)
        dwo_ref[...] += _dot_tn(ymix, dzb)
        dymix = _dot_nt(dzb, wo_ref[...])
        dhf_ref[:, 0:D_MODEL] = dymix * yn * (sig * (1.0 + gate * (1.0 - sig)))
        dyn = dymix * silu
        dgg_ref[...] += _colsum(dyn * yhat)
        gg = gg_ref[...]
        dys = []
        for gi in range(4):
            sl = slice(gi * GROUP, (gi + 1) * GROUP)
            dyh = dyn[:, sl] * gg[:, sl]
            yh = yhat[:, sl]
            dys.append(rs[gi] * (dyh - yh * jnp.mean(dyh * yh, axis=-1, keepdims=True)))
        dya_ref[...] = dys[0]
        dyc_ref[...] = dys[2]
        dyd_ref[...] = dys[3]
        dyb = dys[1]
        dhf_ref[:, D_MODEL:D_MODEL + 256] = dyb * conv
        dconv = dyb * bb
        dcb_ref[...] += _colsum(dconv)
        dcw_ref[0:1, :] += _colsum(dconv * u2)
        dcw_ref[1:2, :] += _colsum(dconv * u1)
        dcw_ref[2:3, :] += _colsum(dconv * u)
        carry = carry_ref[...]
        row = lax.broadcasted_iota(jnp.int32, (tm, 1), 0)
        d1 = jnp.where(row == tm - 1, carry[0:1, :], pltpu.roll(dconv, tm - 1, 0))
        d2 = jnp.where(row == tm - 2, carry[0:1, :],
                       jnp.where(row == tm - 1, carry[1:2, :], pltpu.roll(dconv, tm - 2, 0)))
        cw = cw_ref[...]
        du = cw[2:3, :] * dconv + cw[1:2, :] * d1 + cw[0:1, :] * d2
        dhf_ref[:, D_MODEL + 256:D_MODEL + 512] = du * bx_ref[...]
        dhf_ref[:, D_MODEL + 512:D_MODEL + 768] = du * bc_ref[...]
        carry_ref[...] = dconv[0:8, :]

    in_specs = [pl.BlockSpec((tm, D_MODEL), ridx)] + _ep_in_specs(tm, nt - 1)
    return pl.pallas_call(
        body, name="epilogue_bwd", grid=(nt,), in_specs=in_specs,
        out_specs=[pl.BlockSpec((tm, 256), ridx), pl.BlockSpec((tm, 256), ridx), pl.BlockSpec((tm, 256), ridx),
                   pl.BlockSpec((tm, D_MODEL + 768), ridx),
                   pl.BlockSpec((D_MODEL, D_MODEL), lambda i: (0, 0)),
                   pl.BlockSpec((1, D_MODEL), lambda i: (0, 0)),
                   pl.BlockSpec((1, D_MODEL), lambda i: (0, 0)),
                   pl.BlockSpec((8, 256), lambda i: (0, 0)),
                   pl.BlockSpec((1, 256), lambda i: (0, 0))],
        out_shape=[jax.ShapeDtypeStruct((T, 256), F32)] * 3
                  + [jax.ShapeDtypeStruct((T, D_MODEL + 768), F32),
                     jax.ShapeDtypeStruct((D_MODEL, D_MODEL), F32),
                     jax.ShapeDtypeStruct((1, D_MODEL), F32),
                     jax.ShapeDtypeStruct((1, D_MODEL), F32),
                     jax.ShapeDtypeStruct((8, 256), F32),
                     jax.ShapeDtypeStruct((1, 256), F32)],
        scratch_shapes=[pltpu.VMEM((8, 256), F32)],
        compiler_params=_cparams(("arbitrary",)),
    )(dxn, ya, yc, yd, hf, hf, hf, hf, hf, hf, conv_w, conv_b, g_grp, w_out, g_post)


def _loss_head(y, tgt):
    T = y.shape[0]
    tm = 512

    def body(y_ref, t_ref, dy_ref, l_ref):
        @pl.when(pl.program_id(0) == 0)
        def _():
            l_ref[...] = jnp.zeros_like(l_ref)

        d = y_ref[...] - t_ref[...]
        dy_ref[...] = d * (1.0 / D_MODEL)
        part = jnp.sum(jnp.sum(d * d, axis=1, keepdims=True), axis=0, keepdims=True)
        l_ref[...] += part * (0.5 / D_MODEL)

    return pl.pallas_call(
        body, name="loss_head", grid=(T // tm,),
        in_specs=[pl.BlockSpec((tm, D_MODEL), lambda i: (i, 0))] * 2,
        out_specs=[pl.BlockSpec((tm, D_MODEL), lambda i: (i, 0)), pl.BlockSpec((8, LANES), lambda i: (0, 0))],
        out_shape=[jax.ShapeDtypeStruct((T, D_MODEL), F32), jax.ShapeDtypeStruct((8, LANES), F32)],
        compiler_params=_cparams(("arbitrary",)),
    )(y, tgt)


def _place():
    return lax.axis_index("x"), lax.axis_index("y"), lax.axis_index("c")


def _other_chips(x, y):
    return [(1 - x, y), (x, 1 - y), (1 - x, 1 - y)]


HBM = pl.BlockSpec(memory_space=pl.ANY)


def _gather_weights(shards):
    n = len(shards)

    def body(*refs):
        ins, outs = refs[:n], refs[n:2 * n]
        ici_send, ici_recv, d2d_send, d2d_recv, local_sems = refs[2 * n:]
        x, y, c = _place()
        me = 2 * x + y
        chips = _other_chips(x, y)

        def ici(a, j, layer_from):
            px, py = chips[j]
            return pltpu.make_async_remote_copy(
                src_ref=ins[a].at[c], dst_ref=outs[a].at[layer_from, c], send_sem=ici_send.at[3 * a + j],
                recv_sem=ici_recv.at[3 * a + j], device_id=(px, py, c), device_id_type=MESH)

        def d2d(a, j, layer):
            px, py = chips[j]
            blk = outs[a].at[2 * px + py, layer]
            return pltpu.make_async_remote_copy(
                src_ref=blk, dst_ref=blk, send_sem=d2d_send.at[3 * a + j], recv_sem=d2d_recv.at[3 * a + j],
                device_id=(x, y, 1 - c), device_id_type=MESH)

        local = [pltpu.make_async_copy(ins[a], outs[a].at[me], local_sems.at[a]) for a in range(n)]
        for cp in local:
            cp.start()
        sends = [ici(a, j, me) for j in range(3) for a in range(n)]
        for cp in sends:
            cp.start()
        for j in range(3):
            px, py = chips[j]
            for a in range(n):
                ici(a, j, 2 * px + py).wait_recv()
                fwd = d2d(a, j, c)
                fwd.start()
                sends.append(fwd)
        for j in range(3):
            for a in range(n):
                d2d(a, j, 1 - c).wait_recv()
        for cp in sends:
            cp.wait_send()
        for cp in local:
            cp.wait()

    return pl.pallas_call(
        body, name="gather_weights",
        in_specs=[HBM] * n, out_specs=[HBM] * n,
        out_shape=[jax.ShapeDtypeStruct((4,) + s.shape, s.dtype) for s in shards],
        scratch_shapes=[pltpu.SemaphoreType.DMA((3 * n,))] * 4 + [pltpu.SemaphoreType.DMA((n,))],
    )(*shards)


def _exchange_chips(parts, small):
    n = len(parts)

    def body(*refs):
        ins, sm_ref = refs[:n], refs[n]
        outs, osm_ref = refs[n + 1:2 * n + 1], refs[2 * n + 1]
        send_sems, recv_sems, ssend_sems, srecv_sems, local_sems = refs[2 * n + 2:]
        x, y, c = _place()
        me = 2 * x + y
        dev = 4 * x + 2 * y + c
        local = [pltpu.make_async_copy(ins[a].at[me], outs[a].at[me], local_sems.at[a]) for a in range(n)]
        local.append(pltpu.make_async_copy(sm_ref, osm_ref.at[dev], local_sems.at[n]))
        for cp in local:
            cp.start()
        sends = []
        for j, (px, py) in enumerate(_other_chips(x, y)):
            for a in range(n):
                cp = pltpu.make_async_remote_copy(
                    src_ref=ins[a].at[2 * px + py], dst_ref=outs[a].at[me], send_sem=send_sems.at[3 * a + j],
                    recv_sem=recv_sems.at[3 * a + j], device_id=(px, py, c), device_id_type=MESH)
                cp.start()
                sends.append(cp)
        flips = [(fx, fy, fc) for fx in (0, 1) for fy in (0, 1) for fc in (0, 1)][1:]
        for j, (fx, fy, fc) in enumerate(flips):
            cp = pltpu.make_async_remote_copy(
                src_ref=sm_ref, dst_ref=osm_ref.at[dev], send_sem=ssend_sems.at[j], recv_sem=srecv_sems.at[j],
                device_id=(x ^ fx, y ^ fy, c ^ fc), device_id_type=MESH)
            cp.start()
            sends.append(cp)
        for j, (px, py) in enumerate(_other_chips(x, y)):
            for a in range(n):
                pltpu.make_async_remote_copy(
                    src_ref=ins[a].at[me], dst_ref=outs[a].at[2 * px + py], send_sem=send_sems.at[3 * a + j],
                    recv_sem=recv_sems.at[3 * a + j], device_id=(px, py, c), device_id_type=MESH).wait_recv()
        for j, (fx, fy, fc) in enumerate(flips):
            src = 4 * (x ^ fx) + 2 * (y ^ fy) + (c ^ fc)
            pltpu.make_async_remote_copy(
                src_ref=sm_ref, dst_ref=osm_ref.at[src], send_sem=ssend_sems.at[j], recv_sem=srecv_sems.at[j],
                device_id=(x ^ fx, y ^ fy, c ^ fc), device_id_type=MESH).wait_recv()
        for cp in sends:
            cp.wait_send()
        for cp in local:
            cp.wait()

    return pl.pallas_call(
        body, name="exchange_chips",
        in_specs=[HBM] * (n + 1), out_specs=[HBM] * (n + 1),
        out_shape=[jax.ShapeDtypeStruct(p.shape, p.dtype) for p in parts]
                  + [jax.ShapeDtypeStruct((8,) + small.shape, small.dtype)],
        scratch_shapes=[pltpu.SemaphoreType.DMA((3 * n,)), pltpu.SemaphoreType.DMA((3 * n,)),
                        pltpu.SemaphoreType.DMA((7,)), pltpu.SemaphoreType.DMA((7,)),
                        pltpu.SemaphoreType.DMA((n + 1,))],
    )(*parts, small)


def _swap_cores(parts, name):
    n = len(parts)

    def body(*refs):
        ins, outs, send_sems, recv_sems = refs[:n], refs[n:2 * n], refs[2 * n], refs[2 * n + 1]
        x, y, c = _place()
        copies = [pltpu.make_async_remote_copy(
            src_ref=ins[a], dst_ref=outs[a], send_sem=send_sems.at[a], recv_sem=recv_sems.at[a],
            device_id=(x, y, 1 - c), device_id_type=MESH) for a in range(n)]
        for cp in copies:
            cp.start()
        for cp in copies:
            cp.wait()

    return pl.pallas_call(
        body, name=name, in_specs=[HBM] * n, out_specs=[HBM] * n,
        out_shape=[jax.ShapeDtypeStruct(p.shape, p.dtype) for p in parts],
        scratch_shapes=[pltpu.SemaphoreType.DMA((n,)), pltpu.SemaphoreType.DMA((n,))],
    )(*parts)


def _row_block(rows):
    for cand in (256, 128, 64, 32, 16, 8):
        if rows % cand == 0:
            return cand
    return rows


def _add(a, b, name):
    L, R, C = a.shape
    tr = _row_block(R)

    def body(a_ref, b_ref, o_ref):
        o_ref[...] = a_ref[...] + b_ref[...]

    spec = pl.BlockSpec((1, tr, C), lambda l, i: (l, i, 0))
    return pl.pallas_call(
        body, name=name, grid=(L, R // tr), in_specs=[spec, spec], out_specs=spec,
        out_shape=jax.ShapeDtypeStruct((L, R, C), F32), compiler_params=_cparams(("parallel", "parallel")),
    )(a, b)


def _sum_leading(buf, name):
    n, R, C = buf.shape
    tr = _row_block(R)

    def body(b_ref, o_ref):
        acc = b_ref[0]
        for k in range(1, n):
            acc = acc + b_ref[k]
        o_ref[...] = acc

    return pl.pallas_call(
        body, name=name, grid=(R // tr,),
        in_specs=[pl.BlockSpec((n, tr, C), lambda i: (0, i, 0))],
        out_specs=pl.BlockSpec((tr, C), lambda i: (i, 0)),
        out_shape=jax.ShapeDtypeStruct((R, C), F32),
        compiler_params=_cparams(("parallel",)),
    )(buf)


def _adam_update(w, g, m, v):
    c1 = 1.0 / (1.0 - ADAM_B1 ** ADAM_STEP)
    c2 = 1.0 / (1.0 - ADAM_B2 ** ADAM_STEP)
    mn = ADAM_B1 * m + (1.0 - ADAM_B1) * g
    vn = ADAM_B2 * v + (1.0 - ADAM_B2) * (g * g)
    return -ADAM_LR * ((mn * c1) / (jnp.sqrt(vn * c2) + ADAM_EPS) + ADAM_WD * w), mn, vn


def _adamw_layers(w, m, v, g_mine, g_other, name):
    _, R, C = w.shape
    tr = _row_block(R)

    def body(w_ref, m_ref, v_ref, gm_ref, go_ref, g_ref, d_ref, mo_ref, vo_ref):
        g = jnp.where(pl.program_id(0) == lax.axis_index("c"), gm_ref[...], go_ref[...])
        g_ref[0] = g
        d_ref[0], mo_ref[0], vo_ref[0] = _adam_update(w_ref[0], g, m_ref[0], v_ref[0])

    spec3 = pl.BlockSpec((1, tr, C), lambda l, i: (l, i, 0))
    spec2 = pl.BlockSpec((tr, C), lambda l, i: (i, 0))
    return pl.pallas_call(
        body, name=name, grid=(2, R // tr),
        in_specs=[spec3] * 3 + [spec2] * 2, out_specs=[spec3] * 4,
        out_shape=[jax.ShapeDtypeStruct(w.shape, F32)] * 4,
        compiler_params=_cparams(("parallel", "parallel")),
    )(w, m, v, g_mine, g_other)


PACK_C = 1024
_BIG = ("w_in", "w_out", "mla_w_uq", "mla_w_ukv", "conv_w")
_SMALL = ("norm_pre", "group_norm", "norm_post", "conv_b", "mla_q_norm", "mla_kv_norm", "attn_sinks")
_SMALL_W = {"norm_pre": 1024, "group_norm": 1024, "norm_post": 1024, "conv_b": 256, "mla_q_norm": 256,
            "mla_kv_norm": 128, "attn_sinks": 4}


def _pack_small(d):
    flat = jnp.concatenate([d[n].reshape(-1) for n in _SMALL])
    return jnp.pad(flat, (0, 8 * PACK_C - flat.shape[0])).reshape(8, PACK_C)


def _adamw_small(w, m, v, got):
    ns = len(_SMALL)

    def body(*refs):
        got_ref = refs[3 * ns]
        outs = refs[3 * ns + 1:]
        gsum = got_ref[0]
        for d in range(1, 8):
            gsum = gsum + got_ref[d]
        off = 0
        for i, name in enumerate(_SMALL):
            wd = _SMALL_W[name]
            rows = []
            for l in range(DEPTH):
                r, c0 = divmod(off + l * wd, PACK_C)
                rows.append(gsum[r:r + 1, c0:c0 + wd])
            off += DEPTH * wd
            g = jnp.concatenate(rows, axis=0)
            delta, mn, vn = _adam_update(refs[i][...], g, refs[ns + i][...], refs[2 * ns + i][...])
            outs[i][...] = g
            outs[ns + i][...] = delta
            outs[2 * ns + i][...] = mn
            outs[3 * ns + i][...] = vn

    shapes = [jax.ShapeDtypeStruct(w[n].shape, F32) for n in _SMALL]
    res = pl.pallas_call(body, name="adamw_small", out_shape=shapes * 4)(
        *[w[n] for n in _SMALL], *[m[n] for n in _SMALL], *[v[n] for n in _SMALL], got)
    return [dict(zip(_SMALL, res[k * ns:(k + 1) * ns])) for k in range(4)]


def _w_in_internal(w):
    cols = []
    for n in _INT_ORDER:
        o, wd = _REAL_OFF[n]
        cols.append(w[:, o:o + wd])
        if _INT_W[n] != wd:
            cols.append(jnp.zeros((w.shape[0], _INT_W[n] - wd), w.dtype))
    return jnp.concatenate(cols, axis=1)


def _w_in_real(dw):
    return jnp.concatenate([dw[:, _INT_OFF[n]:_INT_OFF[n] + wd] for n, wd in _REAL], axis=1)


def _uq_internal(w):
    return jnp.pad(w.reshape(256, 4, 96), ((0, 0), (0, 0), (0, 32))).reshape(256, 512)


def _uq_real(dw):
    return dw.reshape(256, 4, 128)[:, :, :96].reshape(256, 384)


def _ukv_internal(w):
    w4 = w.reshape(128, 4, 128)
    k = jnp.pad(w4[:, :, :64], ((0, 0), (0, 0), (0, 64))).reshape(128, 512)
    return jnp.concatenate([k, w4[:, :, 64:].reshape(128, 256)], axis=1)


def _ukv_real(dw):
    k = dw[:, :512].reshape(128, 4, 128)[:, :, :64]
    v = dw[:, 512:].reshape(128, 4, 64)
    return jnp.concatenate([k, v], axis=2).reshape(128, 512)


def _layer_fwd(x, pos, p):
    xn, hb, hf = _inproj_fwd(x, p["norm_pre"], p["w_in"])
    ya = _swa_fwd(hb, p["attn_sinks"])
    qm, km, vm = _mla_prep_fwd(hf, pos, p["mla_q_norm"], p["mla_kv_norm"], p["mla_w_uq"], p["mla_w_ukv"])
    yc, lse = _mla_fwd(qm, km, vm)
    yd, tot, cnt = _sb_fwd(hb)
    x_next = _epilogue_fwd(x, ya, yc, yd, hf, p["conv_w"], p["conv_b"], p["group_norm"], p["w_out"], p["norm_post"])
    return x_next, dict(x=x, xn=xn, hb=hb, hf=hf, ya=ya, yc=yc, yd=yd, tot=tot, cnt=cnt, qm=qm, km=km, vm=vm, lse=lse)


def _layer_bwd(dx_next, pos, p, s):
    (dya, dyc, dyd, dhf, dw_out, dg_post, dg_grp, dconv_w, dconv_b) = _epilogue_bwd(
        dx_next, s["ya"], s["yc"], s["yd"], s["hf"], p["conv_w"], p["conv_b"], p["group_norm"], p["w_out"],
        p["norm_post"])
    dq_d, dk_d, dv_d = _sb_bwd(s["hb"], s["tot"], s["cnt"], dyd)
    dqm, dkm, dvm = _mla_bwd(s["qm"], s["km"], s["vm"], s["yc"], s["lse"], dyc)
    dc, dw_uq, dw_ukv, dg_q, dg_kv = _mla_prep_bwd(
        s["hf"], pos, p["mla_q_norm"], p["mla_kv_norm"], p["mla_w_uq"], p["mla_w_ukv"], dqm, dkm, dvm)
    dq_a, dk_a, dv_a, dsinks = _swa_bwd(s["hb"], p["attn_sinks"], dya)
    dx, dh, dg_pre = _inproj_bwd_dx(s["x"], p["norm_pre"], p["w_in"], dx_next,
                                    [dq_a, dk_a, dv_a, dq_d, dk_d, dv_d, dhf, dc])
    dw_in = _matmul_tn(s["xn"], dh, "inproj_bwd_dw")
    grads = dict(norm_pre=dg_pre[0], w_in=_w_in_real(dw_in), attn_sinks=dsinks[0, :4], conv_w=dconv_w[:3],
                 conv_b=dconv_b[0], mla_q_norm=dg_q[0], mla_w_uq=_uq_real(dw_uq), mla_kv_norm=dg_kv[0],
                 mla_w_ukv=_ukv_real(dw_ukv), group_norm=dg_grp[0], w_out=dw_out, norm_post=dg_post[0])
    return dx, grads


_WEIGHTS = ["norm_pre", "w_in", "attn_sinks", "conv_w", "conv_b", "mla_q_norm", "mla_w_uq", "mla_kv_norm",
            "mla_w_ukv", "group_norm", "w_out", "norm_post"]


def kernel(x, positions, norm_pre, w_in, attn_sinks, conv_w, conv_b, mla_q_norm, mla_w_uq, mla_kv_norm, mla_w_ukv, group_norm, w_out, norm_post, loss_target, m_norm_pre, m_w_in, m_attn_sinks, m_conv_w, m_conv_b, m_mla_q_norm, m_mla_w_uq, m_mla_kv_norm, m_mla_w_ukv, m_group_norm, m_w_out, m_norm_post, v_norm_pre, v_w_in, v_attn_sinks, v_conv_w, v_conv_b, v_mla_q_norm, v_mla_w_uq, v_mla_kv_norm, v_mla_w_ukv, v_group_norm, v_w_out, v_norm_post):
    w = dict(norm_pre=norm_pre, w_in=w_in, attn_sinks=attn_sinks, conv_w=conv_w, conv_b=conv_b,
             mla_q_norm=mla_q_norm, mla_w_uq=mla_w_uq, mla_kv_norm=mla_kv_norm, mla_w_ukv=mla_w_ukv,
             group_norm=group_norm, w_out=w_out, norm_post=norm_post)
    m = dict(norm_pre=m_norm_pre, w_in=m_w_in, attn_sinks=m_attn_sinks, conv_w=m_conv_w, conv_b=m_conv_b,
             mla_q_norm=m_mla_q_norm, mla_w_uq=m_mla_w_uq, mla_kv_norm=m_mla_kv_norm, mla_w_ukv=m_mla_w_ukv,
             group_norm=m_group_norm, w_out=m_w_out, norm_post=m_norm_post)
    v = dict(norm_pre=v_norm_pre, w_in=v_w_in, attn_sinks=v_attn_sinks, conv_w=v_conv_w, conv_b=v_conv_b,
             mla_q_norm=v_mla_q_norm, mla_w_uq=v_mla_w_uq, mla_kv_norm=v_mla_kv_norm, mla_w_ukv=v_mla_w_ukv,
             group_norm=v_group_norm, w_out=v_w_out, norm_post=v_norm_post)
    T = x.shape[1]
    xs = x[0]
    pos = positions[0].reshape(T, 1)
    tgt = loss_target[0]
    core = lax.axis_index("c")

    gathered = _gather_weights([w[n].astype(BF16) for n in _BIG[:4]] + [w["conv_w"]])
    full = {}
    for n, got in zip(_BIG, gathered):
        if n == "w_out":
            full[n] = jnp.moveaxis(got, 0, 1).reshape(DEPTH, D_MODEL, D_MODEL)
        else:
            full[n] = jnp.transpose(got, (1, 2, 0, 3)).reshape(DEPTH, got.shape[2], 4 * got.shape[3])

    layers = []
    for l in range(DEPTH):
        layers.append(dict(
            norm_pre=norm_pre[l:l + 1], w_in=_w_in_internal(full["w_in"][l]), attn_sinks=attn_sinks[l],
            conv_w=full["conv_w"][l], conv_b=conv_b[l:l + 1], mla_q_norm=mla_q_norm[l:l + 1],
            mla_w_uq=_uq_internal(full["mla_w_uq"][l]), mla_kv_norm=mla_kv_norm[l:l + 1],
            mla_w_ukv=_ukv_internal(full["mla_w_ukv"][l]), group_norm=group_norm[l:l + 1],
            w_out=full["w_out"][l], norm_post=norm_post[l:l + 1]))

    saved = []
    h = xs
    for l in range(DEPTH):
        h, s = _layer_fwd(h, pos, layers[l])
        saved.append(s)
    dy, loss_part = _loss_head(h, tgt)
    loss = lax.psum(loss_part[0, 0], ("x", "y", "c"))

    grads = [None] * DEPTH
    for l in reversed(range(DEPTH)):
        dy, grads[l] = _layer_bwd(dy, pos, layers[l], saved[l])

    def chunks(n, a):
        if n == "w_out":
            return a.reshape(4, D_MODEL // 4, D_MODEL)
        return jnp.transpose(a.reshape(a.shape[0], 4, a.shape[1] // 4), (1, 0, 2))

    mine = [chunks(n, jnp.where(core == 0, grads[0][n], grads[1][n])) for n in _BIG]
    theirs = [chunks(n, jnp.where(core == 0, grads[1][n], grads[0][n])) for n in _BIG]
    from_sibling = _swap_cores(theirs, "swap_layer_chunks")
    summed = [_add(a, b, "add_cores_" + n) for n, a, b in zip(_BIG, mine, from_sibling)]
    small = _pack_small({n: jnp.stack([grads[l][n] for l in range(DEPTH)]) for n in _SMALL})
    *got, got_small = _exchange_chips(summed, small)
    done = [_sum_leading(b, "sum_chips_" + n) for n, b in zip(_BIG, got)]
    done_other = _swap_cores(done, "swap_layer_shards")

    outs = _adamw_small(w, m, v, got_small)
    for n, gm, go in zip(_BIG, done, done_other):
        for d, a in zip(outs, _adamw_layers(w[n], m[n], v[n], gm, go, "adamw_" + n)):
            d[n] = a
    return (loss, dy[None], *[outs[0][n] for n in _WEIGHTS], *[outs[1][n] for n in _WEIGHTS],
            *[outs[2][n] for n in _WEIGHTS], *[outs[3][n] for n in _WEIGHTS])
```

```python
import math

import jax
import jax.numpy as jnp
from jax import lax
from jax.experimental import pallas as pl
from jax.experimental.pallas import tpu as pltpu

F32 = jnp.float32
BF16 = jnp.bfloat16
MESH = pl.DeviceIdType.MESH

D_MODEL = 1024
DEPTH = 2
EPS = 1e-6
BLOCK = 128
HEAD = 64
LANES = 128
GROUP = 256
LOG2E = 1.4426950408889634
LN2 = 0.6931471805599453
MLA_QSCALE = 96 ** -0.5 * LOG2E
ROPE_HALF = 16
ROPE_THETA = 10000.0
ATT_BLK = 256
MLA_BQ = 512
NEG = -1e30
SB_DEAD = -104.0

ADAM_LR, ADAM_B1, ADAM_B2, ADAM_EPS, ADAM_WD, ADAM_STEP = 0.001, 0.9, 0.999, 1e-08, 0.01, 10

_REAL = [("a_q", 256), ("a_k", 128), ("a_v", 128), ("b_b", 256), ("b_c", 256), ("b_x", 256),
         ("c_q", 256), ("c_kv", 128), ("c_kr", 32), ("d_q", 256), ("d_k", 256), ("d_v", 256),
         ("gate", 1024)]
_REAL_OFF = {}
_o = 0
for _n, _w in _REAL:
    _REAL_OFF[_n] = (_o, _w)
    _o += _w
D_IN = _o
_INT_ORDER = ["a_q", "a_k", "a_v", "d_q", "d_k", "d_v", "gate", "b_b", "b_c", "b_x", "c_q", "c_kv", "c_kr"]
_INT_W = dict(_REAL)
_INT_W["c_kr"] = 128
_INT_OFF = {}
_o = 0
for _n in _INT_ORDER:
    _INT_OFF[_n] = _o
    _o += _INT_W[_n]
N_INT = _o
N_HB = _INT_OFF["gate"]
N_HF = N_INT - N_HB

VMEM_LIMIT = 56 * 1024 * 1024


def _cparams(sem):
    return pltpu.CompilerParams(dimension_semantics=sem, vmem_limit_bytes=VMEM_LIMIT)


def _dot(a, b):
    return jnp.dot(a, b, preferred_element_type=F32)


def _dot_nt(a, b):
    return lax.dot_general(a, b, (((1,), (1,)), ((), ())), preferred_element_type=F32)


def _dot_tn(a, b):
    return lax.dot_general(a, b, (((0,), (0,)), ((), ())), preferred_element_type=F32)


def _split(x):
    hi = x.astype(BF16)
    lo = (x - hi.astype(F32)).astype(BF16)
    return hi, lo


def _rms(x):
    return lax.rsqrt(jnp.mean(x * x, axis=-1, keepdims=True) + EPS)


def _rms_bwd(dy, xhat, r, g):
    dxhat = dy * g
    return r * (dxhat - xhat * jnp.mean(dxhat * xhat, axis=-1, keepdims=True)), dy * xhat


def _colsum(x):
    return jnp.sum(x, axis=0, keepdims=True)


def _inproj_fwd(x, g, w):
    T = x.shape[0]
    tm = 256

    def body(x_ref, g_ref, w_ref, xn_ref, hb_ref, hf_ref):
        xv = x_ref[...]
        xn = (xv * _rms(xv) * g_ref[...]).astype(BF16)
        xn_ref[...] = xn
        h = _dot(xn, w_ref[...])
        hb_ref[...] = h[:, :N_HB].astype(BF16)
        hf_ref[...] = h[:, N_HB:]

    return pl.pallas_call(
        body, name="inproj_fwd", grid=(T // tm,),
        in_specs=[pl.BlockSpec((tm, D_MODEL), lambda i: (i, 0)),
                  pl.BlockSpec((1, D_MODEL), lambda i: (0, 0)),
                  pl.BlockSpec((D_MODEL, N_INT), lambda i: (0, 0))],
        out_specs=[pl.BlockSpec((tm, D_MODEL), lambda i: (i, 0)),
                   pl.BlockSpec((tm, N_HB), lambda i: (i, 0)),
                   pl.BlockSpec((tm, N_HF), lambda i: (i, 0))],
        out_shape=[jax.ShapeDtypeStruct((T, D_MODEL), BF16),
                   jax.ShapeDtypeStruct((T, N_HB), BF16),
                   jax.ShapeDtypeStruct((T, N_HF), F32)],
        compiler_params=_cparams(("parallel",)),
    )(x, g, w)


def _inproj_bwd_dx(x, g, w, dx_next, pieces):
    T = x.shape[0]
    tm = 256
    widths = [p.shape[1] for p in pieces]
    assert sum(widths) == N_INT

    def body(x_ref, g_ref, w_ref, dxn_ref, *rest):
        p_refs = rest[:len(pieces)]
        dx_ref, dh_ref, dg_ref = rest[len(pieces):]
        dh = jnp.concatenate([p[...].astype(BF16) for p in p_refs], axis=1)
        dh_ref[...] = dh
        dxn = _dot_nt(dh, w_ref[...])
        xv = x_ref[...]
        r = _rms(xv)
        dx, dgrow = _rms_bwd(dxn, xv * r, r, g_ref[...])
        dx_ref[...] = dx + dxn_ref[...]

        @pl.when(pl.program_id(0) == 0)
        def _():
            dg_ref[...] = jnp.zeros_like(dg_ref)

        dg_ref[...] += _colsum(dgrow)

    return pl.pallas_call(
        body, name="inproj_bwd_dx", grid=(T // tm,),
        in_specs=[pl.BlockSpec((tm, D_MODEL), lambda i: (i, 0)),
                  pl.BlockSpec((1, D_MODEL), lambda i: (0, 0)),
                  pl.BlockSpec((D_MODEL, N_INT), lambda i: (0, 0)),
                  pl.BlockSpec((tm, D_MODEL), lambda i: (i, 0))]
                 + [pl.BlockSpec((tm, wd), lambda i: (i, 0)) for wd in widths],
        out_specs=[pl.BlockSpec((tm, D_MODEL), lambda i: (i, 0)),
                   pl.BlockSpec((tm, N_INT), lambda i: (i, 0)),
                   pl.BlockSpec((1, D_MODEL), lambda i: (0, 0))],
        out_shape=[jax.ShapeDtypeStruct((T, D_MODEL), F32),
                   jax.ShapeDtypeStruct((T, N_INT), BF16),
                   jax.ShapeDtypeStruct((1, D_MODEL), F32)],
        compiler_params=_cparams(("arbitrary",)),
    )(x, g, w, dx_next, *pieces)


def _matmul_tn(a, b, name):
    T, M = a.shape
    N = b.shape[1]
    tm, tn = 512, 512

    def body(a_ref, b_ref, o_ref):
        @pl.when(pl.program_id(1) == 0)
        def _():
            o_ref[...] = jnp.zeros_like(o_ref)

        o_ref[...] += _dot_tn(a_ref[...], b_ref[...])

    return pl.pallas_call(
        body, name=name, grid=(N // tn, T // tm),
        in_specs=[pl.BlockSpec((tm, M), lambda j, t: (t, 0)),
                  pl.BlockSpec((tm, tn), lambda j, t: (t, j))],
        out_specs=pl.BlockSpec((M, tn), lambda j, t: (0, j)),
        out_shape=jax.ShapeDtypeStruct((M, N), F32),
        compiler_params=_cparams(("parallel", "arbitrary")),
    )(a, b)


def _roll_f32(x, shift):
    return pltpu.roll(x.astype(F32), shift, 1)


def _swa_operands(h, q_ref, kp_ref, kc_ref, vp_ref, vc_ref):
    p, e = h // 2, h % 2
    lane = lax.broadcasted_iota(jnp.int32, (1, LANES), 1) // HEAD
    q = q_ref[:, p * LANES:(p + 1) * LANES]
    k_prev, k_cur, v_prev, v_cur = kp_ref[...], kc_ref[...], vp_ref[...], vc_ref[...]
    if e != p:
        q = _roll_f32(q, HEAD).astype(BF16)
        v_prev = _roll_f32(v_prev, HEAD).astype(BF16)
        v_cur = _roll_f32(v_cur, HEAD).astype(BF16)
    qs = jnp.where(lane == p, q, 0) * 0.125
    return dict(p=p, e=e, lane=lane, qs=qs, k_prev=k_prev, k_cur=k_cur,
                v_prev=jnp.where(lane == e, v_prev, 0), v_cur=jnp.where(lane == e, v_cur, 0),
                s_prev=_dot_nt(qs, k_prev), s_cur=_dot_nt(qs, k_cur))


def _swa_probs(ops, sink, first):
    row = lax.broadcasted_iota(jnp.int32, (BLOCK, BLOCK), 0)
    col = lax.broadcasted_iota(jnp.int32, (BLOCK, BLOCK), 1)
    s_prev = jnp.where(jnp.logical_and(col > row, jnp.logical_not(first)), ops["s_prev"], NEG)
    s_cur = jnp.where(col <= row, ops["s_cur"], NEG)
    m = jnp.maximum(jnp.maximum(jnp.max(s_prev, axis=1, keepdims=True),
                                jnp.max(s_cur, axis=1, keepdims=True)), sink)
    p_prev = jnp.exp(s_prev - m)
    p_cur = jnp.exp(s_cur - m)
    p_sink = jnp.exp(sink - m)
    inv = 1.0 / (jnp.sum(p_prev, axis=1, keepdims=True) + jnp.sum(p_cur, axis=1, keepdims=True) + p_sink)
    return p_prev * inv, p_cur * inv, p_sink * inv


def _swa_specs(T):
    nb = T // BLOCK
    qo, ko, vo = (_INT_OFF[n] // LANES for n in ("a_q", "a_k", "a_v"))
    prev = lambda i: jnp.maximum(i - 1, 0)
    return [pl.BlockSpec((BLOCK, 256), lambda i: (i, qo // 2)),
            pl.BlockSpec((BLOCK, LANES), lambda i: (prev(i), ko)),
            pl.BlockSpec((BLOCK, LANES), lambda i: (i, ko)),
            pl.BlockSpec((BLOCK, LANES), lambda i: (prev(i), vo)),
            pl.BlockSpec((BLOCK, LANES), lambda i: (i, vo)),
            pl.BlockSpec(memory_space=pltpu.SMEM)], nb


def _swa_fwd(hb, sinks):
    T = hb.shape[0]
    specs, nb = _swa_specs(T)

    def body(q_ref, kp_ref, kc_ref, vp_ref, vc_ref, s_ref, o_ref):
        first = pl.program_id(0) == 0
        ops = [_swa_operands(h, q_ref, kp_ref, kc_ref, vp_ref, vc_ref) for h in range(4)]
        probs = [_swa_probs(ops[h], s_ref[h], first) for h in range(4)]
        outs = [_dot(probs[h][0].astype(BF16), ops[h]["v_prev"]) + _dot(probs[h][1].astype(BF16), ops[h]["v_cur"])
                for h in range(4)]
        for p in range(2):
            o_ref[:, p * LANES:(p + 1) * LANES] = outs[2 * p] + outs[2 * p + 1]

    return pl.pallas_call(
        body, name="swa_fwd", grid=(nb,), in_specs=specs,
        out_specs=pl.BlockSpec((BLOCK, 256), lambda i: (i, 0)),
        out_shape=jax.ShapeDtypeStruct((T, 256), F32),
        compiler_params=_cparams(("parallel",)),
    )(hb, hb, hb, hb, hb, sinks)


def _swa_bwd(hb, sinks, dy):
    T = hb.shape[0]
    specs, nb = _swa_specs(T)

    def body(q_ref, kp_ref, kc_ref, vp_ref, vc_ref, s_ref, dy_ref, dq_ref, dk_ref, dv_ref, ds_ref):
        i = pl.program_id(0)
        first = i == 0
        cur = pl.ds(pl.multiple_of(i * BLOCK, BLOCK), BLOCK)
        prv = pl.ds(pl.multiple_of(jnp.maximum(i - 1, 0) * BLOCK, BLOCK), BLOCK)

        @pl.when(first)
        def _():
            ds_ref[...] = jnp.zeros_like(ds_ref)

        dk_ref[cur, :] = jnp.zeros((BLOCK, LANES), F32)
        dv_ref[cur, :] = jnp.zeros((BLOCK, LANES), F32)
        lane_id = lax.broadcasted_iota(jnp.int32, (8, LANES), 1)
        heads = range(4)
        ops = [_swa_operands(h, q_ref, kp_ref, kc_ref, vp_ref, vc_ref) for h in heads]
        probs = [_swa_probs(ops[h], s_ref[h], first) for h in heads]
        dos = [jnp.where(ops[h]["lane"] == ops[h]["e"], dy_ref[:, ops[h]["p"] * LANES:(ops[h]["p"] + 1) * LANES], 0.0)
               for h in heads]
        dobs = [d.astype(BF16) for d in dos]
        pbs = [(probs[h][0].astype(BF16), probs[h][1].astype(BF16)) for h in heads]
        outs = [_dot(pbs[h][0], ops[h]["v_prev"]) + _dot(pbs[h][1], ops[h]["v_cur"]) for h in heads]
        dps = [(_dot_nt(dobs[h], ops[h]["v_prev"]), _dot_nt(dobs[h], ops[h]["v_cur"])) for h in heads]
        dss, dsinks = [], jnp.zeros((8, LANES), F32)
        for h in heads:
            delta = jnp.sum(dos[h] * outs[h], axis=1, keepdims=True)
            dss.append(((probs[h][0] * (dps[h][0] - delta)).astype(BF16),
                        (probs[h][1] * (dps[h][1] - delta)).astype(BF16)))
            dsink = -jnp.sum(probs[h][2] * delta, axis=0, keepdims=True)
            dsinks += jnp.where(lane_id == h, dsink, 0.0)
        ds_ref[...] += dsinks
        dqs = [(_dot(dss[h][0], ops[h]["k_prev"]) + _dot(dss[h][1], ops[h]["k_cur"])) * 0.125 for h in heads]
        dk_prev = dk_cur = dv_prev = dv_cur = jnp.zeros((BLOCK, LANES), F32)
        for h in heads:
            p, e = ops[h]["p"], ops[h]["e"]
            dob_v = dobs[h] if e == p else pltpu.roll(dos[h], HEAD, 1).astype(BF16)
            dk_prev += _dot_tn(dss[h][0], ops[h]["qs"])
            dk_cur += _dot_tn(dss[h][1], ops[h]["qs"])
            dv_prev += _dot_tn(pbs[h][0], dob_v)
            dv_cur += _dot_tn(pbs[h][1], dob_v)
        dk_ref[prv, :] += dk_prev
        dk_ref[cur, :] += dk_cur
        dv_ref[prv, :] += dv_prev
        dv_ref[cur, :] += dv_cur
        for p in range(2):
            dq_pair = jnp.zeros((BLOCK, LANES), F32)
            for e in range(2):
                dq = jnp.where(ops[2 * p + e]["lane"] == p, dqs[2 * p + e], 0.0)
                dq_pair += dq if e == p else pltpu.roll(dq, HEAD, 1)
            dq_ref[:, p * LANES:(p + 1) * LANES] = dq_pair

    return pl.pallas_call(
        body, name="swa_bwd", grid=(nb,),
        in_specs=specs + [pl.BlockSpec((BLOCK, 256), lambda i: (i, 0))],
        out_specs=[pl.BlockSpec((BLOCK, 256), lambda i: (i, 0)),
                   pl.BlockSpec((T, LANES), lambda i: (0, 0)),
                   pl.BlockSpec((T, LANES), lambda i: (0, 0)),
                   pl.BlockSpec((8, LANES), lambda i: (0, 0))],
        out_shape=[jax.ShapeDtypeStruct((T, 256), F32),
                   jax.ShapeDtypeStruct((T, LANES), F32),
                   jax.ShapeDtypeStruct((T, LANES), F32),
                   jax.ShapeDtypeStruct((8, LANES), F32)],
        compiler_params=_cparams(("arbitrary",)),
    )(hb, hb, hb, hb, hb, sinks, dy)


def _rope_tables(pos_ref):
    lane = lax.broadcasted_iota(jnp.int32, (1, LANES), 1)
    active = jnp.logical_and(lane >= HEAD, lane < HEAD + 2 * ROPE_HALF)
    idx = ((lane - HEAD) % ROPE_HALF).astype(F32)
    freq = jnp.exp(idx * (-math.log(ROPE_THETA) / ROPE_HALF))
    ang = pos_ref[...].astype(F32) * freq
    cos, sin = jnp.cos(ang), jnp.sin(ang)
    c = jnp.where(active, cos, 1.0)
    s_up = jnp.where(jnp.logical_and(active, lane >= HEAD + ROPE_HALF), sin, 0.0)
    s_dn = jnp.where(jnp.logical_and(active, lane < HEAD + ROPE_HALF), -sin, 0.0)
    return c, s_up, s_dn


def _rope(x, tabs):
    c, s_up, s_dn = tabs
    return x * c + pltpu.roll(x, ROPE_HALF, 1) * s_up + pltpu.roll(x, LANES - ROPE_HALF, 1) * s_dn


def _rope_t(dy, tabs):
    c, s_up, s_dn = tabs
    return dy * c + pltpu.roll(dy * s_up, LANES - ROPE_HALF, 1) + pltpu.roll(dy * s_dn, ROPE_HALF, 1)


def _mla_lat_specs(tm):
    cq, ckv, ckr = ((_INT_OFF[n] - N_HB) for n in ("c_q", "c_kv", "c_kr"))
    return [pl.BlockSpec((tm, 256), lambda i: (i, cq // 256)),
            pl.BlockSpec((tm, LANES), lambda i: (i, ckv // LANES)),
            pl.BlockSpec((tm, LANES), lambda i: (i, ckr // LANES)),
            pl.BlockSpec((tm, 1), lambda i: (i, 0)),
            pl.BlockSpec((1, 256), lambda i: (0, 0)),
            pl.BlockSpec((1, LANES), lambda i: (0, 0)),
            pl.BlockSpec((256, 512), lambda i: (0, 0)),
            pl.BlockSpec((LANES, 768), lambda i: (0, 0))]


def _mla_prep_fwd(hf, pos, g_q, g_kv, w_uq, w_ukv):
    T = hf.shape[0]
    tm = 512
    sub = tm // ATT_BLK

    def body(cq_ref, ckv_ref, ckr_ref, pos_ref, gq_ref, gkv_ref, wq_ref, wkv_ref, qm_ref, km_ref, vm_ref, vt_ref):
        tabs = _rope_tables(pos_ref)
        cq = cq_ref[...]
        q = _dot((cq * _rms(cq) * gq_ref[...]).astype(BF16), wq_ref[...])
        ckv = ckv_ref[...]
        kv = _dot((ckv * _rms(ckv) * gkv_ref[...]).astype(BF16), wkv_ref[...])
        kr = _rope(pltpu.roll(ckr_ref[...], HEAD, 1), tabs)
        for h in range(4):
            sl = slice(h * LANES, (h + 1) * LANES)
            qm_ref[:, sl] = (_rope(q[:, sl], tabs) * MLA_QSCALE).astype(BF16)
            km_ref[:, sl] = (kv[:, sl] + kr).astype(BF16)
        vm_ref[...] = kv[:, 512:].astype(BF16)
        for p in range(2):
            for s in range(sub):
                tile = kv[s * ATT_BLK:(s + 1) * ATT_BLK, 512 + p * LANES:512 + (p + 1) * LANES]
                vt_ref[p, s] = jnp.transpose(tile).astype(BF16)

    return pl.pallas_call(
        body, name="mla_prep_fwd", grid=(T // tm,), in_specs=_mla_lat_specs(tm),
        out_specs=[pl.BlockSpec((tm, 512), lambda i: (i, 0)),
                   pl.BlockSpec((tm, 512), lambda i: (i, 0)),
                   pl.BlockSpec((tm, 256), lambda i: (i, 0)),
                   pl.BlockSpec((2, sub, LANES, ATT_BLK), lambda i: (0, i, 0, 0))],
        out_shape=[jax.ShapeDtypeStruct((T, 512), BF16),
                   jax.ShapeDtypeStruct((T, 512), BF16),
                   jax.ShapeDtypeStruct((T, 256), BF16),
                   jax.ShapeDtypeStruct((2, T // ATT_BLK, LANES, ATT_BLK), BF16)],
        compiler_params=_cparams(("parallel",)),
    )(hf, hf, hf, pos, g_q, g_kv, w_uq, w_ukv)


def _mla_prep_bwd(hf, pos, g_q, g_kv, w_uq, w_ukv, dqm, dkm, dvm):
    T = hf.shape[0]
    tm = 512

    def body(cq_ref, ckv_ref, ckr_ref, pos_ref, gq_ref, gkv_ref, wq_ref, wkv_ref, dq_ref, dk_ref, dv_ref,
             dc_ref, dwq_ref, dwkv_ref, dgq_ref, dgkv_ref):
        @pl.when(pl.program_id(0) == 0)
        def _():
            dwq_ref[...] = jnp.zeros_like(dwq_ref)
            dwkv_ref[...] = jnp.zeros_like(dwkv_ref)
            dgq_ref[...] = jnp.zeros_like(dgq_ref)
            dgkv_ref[...] = jnp.zeros_like(dgkv_ref)

        tabs = _rope_tables(pos_ref)
        lane = lax.broadcasted_iota(jnp.int32, (1, LANES), 1)
        dq = jnp.concatenate([_rope_t(dq_ref[:, h * LANES:(h + 1) * LANES] * MLA_QSCALE, tabs)
                              for h in range(4)], axis=1).astype(BF16)
        cq = cq_ref[...]
        rq = _rms(cq)
        cqn = (cq * rq * gq_ref[...]).astype(BF16)
        dwq_ref[...] += _dot_tn(cqn, dq)
        dcq, dgrow = _rms_bwd(_dot_nt(dq, wq_ref[...]), cq * rq, rq, gq_ref[...])
        dgq_ref[...] += _colsum(dgrow)
        dc_ref[:, 0:256] = dcq

        dk = dk_ref[...]
        dkr = dk[:, 0:LANES] + dk[:, LANES:2 * LANES] + dk[:, 2 * LANES:3 * LANES] + dk[:, 3 * LANES:]
        dkr = pltpu.roll(_rope_t(dkr, tabs), HEAD, 1)
        dc_ref[:, 384:512] = jnp.where(lane < 2 * ROPE_HALF, dkr, 0.0)
        dkv = jnp.concatenate([dk.astype(BF16), dv_ref[...].astype(BF16)], axis=1)
        ckv = ckv_ref[...]
        rkv = _rms(ckv)
        ckvn = (ckv * rkv * gkv_ref[...]).astype(BF16)
        dwkv_ref[...] += _dot_tn(ckvn, dkv)
        dckv, dgrow = _rms_bwd(_dot_nt(dkv, wkv_ref[...]), ckv * rkv, rkv, gkv_ref[...])
        dgkv_ref[...] += _colsum(dgrow)
        dc_ref[:, 256:384] = dckv

    return pl.pallas_call(
        body, name="mla_prep_bwd", grid=(T // tm,),
        in_specs=_mla_lat_specs(tm) + [pl.BlockSpec((tm, 512), lambda i: (i, 0)),
                                       pl.BlockSpec((tm, 512), lambda i: (i, 0)),
                                       pl.BlockSpec((tm, 256), lambda i: (i, 0))],
        out_specs=[pl.BlockSpec((tm, 512), lambda i: (i, 0)),
                   pl.BlockSpec((256, 512), lambda i: (0, 0)),
                   pl.BlockSpec((LANES, 768), lambda i: (0, 0)),
                   pl.BlockSpec((1, 256), lambda i: (0, 0)),
                   pl.BlockSpec((1, LANES), lambda i: (0, 0))],
        out_shape=[jax.ShapeDtypeStruct((T, 512), F32),
                   jax.ShapeDtypeStruct((256, 512), F32),
                   jax.ShapeDtypeStruct((LANES, 768), F32),
                   jax.ShapeDtypeStruct((1, 256), F32),
                   jax.ShapeDtypeStruct((1, LANES), F32)],
        compiler_params=_cparams(("arbitrary",)),
    )(hf, hf, hf, pos, g_q, g_kv, w_uq, w_ukv, dqm, dkm, dvm)


def _causal_masks(bq, bk):
    row = lax.broadcasted_iota(jnp.int32, (bq, bk), 0)
    col = lax.broadcasted_iota(jnp.int32, (bq, bk), 1)
    return row, col


def _mla_fwd(qm, km, vt):
    T = qm.shape[0]
    bq, bk = min(MLA_BQ, T), ATT_BLK
    nq, nsub, nk = T // bq, bq // bk, T // bk

    def body(q_ref, k_ref, vt_ref, o_ref, lse_ref, acc_ref, m_ref, l_ref):
        qi = pl.program_id(1)
        key = lax.broadcasted_iota(jnp.int32, (bk, bq), 0)
        qry = lax.broadcasted_iota(jnp.int32, (bk, bq), 1)
        ones = jnp.ones((8, bk), BF16)
        acc_ref[...] = jnp.zeros_like(acc_ref)
        m_ref[...] = jnp.full_like(m_ref, NEG)
        l_ref[...] = jnp.zeros_like(l_ref)

        def step(kb0, masked):
            kbs = [kb0 + d for d in range(nsub)]
            sts = [[_dot_nt(k_ref[pl.ds(pl.multiple_of(kb * bk, bk), bk), e * LANES:(e + 1) * LANES],
                            q_ref[:, e * LANES:(e + 1) * LANES]) for kb in kbs] for e in range(2)]
            pts, alphas = [], []
            for e in range(2):
                st = [jnp.where(key + d * bk <= qry, sts[e][d], NEG) for d in range(nsub)] if masked else sts[e]
                m_prev = m_ref[e, 0:1, :]
                m_new = m_prev
                for d in range(nsub):
                    m_new = jnp.maximum(m_new, jnp.max(st[d], axis=0, keepdims=True))
                alpha = jnp.exp2(m_prev - m_new)
                pt = [jnp.exp2(st[d] - m_new).astype(BF16) for d in range(nsub)]
                l_new = alpha * l_ref[e]
                for d in range(nsub):
                    l_new = l_new + _dot(ones, pt[d])
                l_ref[e] = l_new
                m_ref[e] = jnp.broadcast_to(m_new, (8, bq))
                pts.append(pt)
                alphas.append(alpha)
            for e in range(2):
                acc = alphas[e] * acc_ref[e]
                for d in range(nsub):
                    acc = acc + _dot(vt_ref[0, kbs[d], e * HEAD:(e + 1) * HEAD, :], pts[e][d])
                acc_ref[e] = acc

        step(qi * nsub, True)

        def loop(t, c):
            step(t * nsub, False)
            return c

        lax.fori_loop(0, qi, loop, 0)
        outs, lses = [], []
        for e in range(2):
            l = l_ref[e, 0:1, :]
            outs.append(acc_ref[e] / l)
            lses.append(jnp.broadcast_to(m_ref[e, 0:1, :] * LN2 + jnp.log(l), (HEAD, bq)))
        o_ref[...] = jnp.transpose(jnp.concatenate(outs, axis=0))
        lse_ref[...] = jnp.transpose(jnp.concatenate(lses, axis=0))

    return pl.pallas_call(
        body, name="mla_fwd", grid=(2, nq),
        in_specs=[pl.BlockSpec((bq, 256), lambda j, i: (i, j)),
                  pl.BlockSpec((T, 256), lambda j, i: (0, j)),
                  pl.BlockSpec((1, nk, LANES, bk), lambda j, i: (j, 0, 0, 0))],
        out_specs=[pl.BlockSpec((bq, LANES), lambda j, i: (i, j)),
                   pl.BlockSpec((bq, LANES), lambda j, i: (i, j))],
        out_shape=[jax.ShapeDtypeStruct((T, 256), F32), jax.ShapeDtypeStruct((T, 256), F32)],
        scratch_shapes=[pltpu.VMEM((2, HEAD, bq), F32), pltpu.VMEM((2, 8, bq), F32), pltpu.VMEM((2, 8, bq), F32)],
        compiler_params=_cparams(("parallel", "arbitrary")),
    )(qm, km, vt)


def _mla_bwd(qm, km, vm, y, lse, dy):
    T = qm.shape[0]
    bq, bk = min(MLA_BQ, T), ATT_BLK
    nq, nsub, nk = T // bq, bq // bk, T // bk

    def body(q_ref, k_ref, v_ref, y_ref, lse_ref, dy_ref, dq_ref, dkt_ref, dvt_ref, dob_ref, st_ref, qt_ref, dot_ref):
        qi = pl.program_id(1)

        @pl.when(qi == 0)
        def _():
            dkt_ref[...] = jnp.zeros_like(dkt_ref)
            dvt_ref[...] = jnp.zeros_like(dvt_ref)

        lane = lax.broadcasted_iota(jnp.int32, (1, LANES), 1) // HEAD
        row, col = _causal_masks(bq, bk)
        dq_ref[...] = jnp.zeros_like(dq_ref)
        lse = lse_ref[...]
        lse_other = pltpu.roll(lse, HEAD, 1)
        qt_ref[...] = jnp.transpose(q_ref[...].astype(F32)).astype(BF16)
        dot_ref[...] = jnp.transpose(dy_ref[...]).astype(BF16)
        for e in range(2):
            do = jnp.where(lane == e, dy_ref[...], 0.0)
            dob_ref[e] = do.astype(BF16)
            st_ref[2 * e] = jnp.where(lane == e, lse, lse_other) * LOG2E
            st_ref[2 * e + 1] = jnp.broadcast_to(jnp.sum(do * y_ref[...], axis=1, keepdims=True), (bq, LANES))

        hss = [slice(e * LANES, (e + 1) * LANES) for e in range(2)]
        tile = lambda a: jnp.concatenate([a] * (bk // LANES), axis=1)

        def step(kb0, masked):
            kbs = [kb0 + d for d in range(nsub)]
            rows = [pl.ds(pl.multiple_of(kb * bk, bk), bk) for kb in kbs]
            pairs = [(d, e) for d in range(nsub) for e in range(2)]
            ss = {(d, e): _dot_nt(q_ref[:, hss[e]], k_ref[rows[d], hss[e]]) for d, e in pairs}
            dps = {(d, e): _dot_nt(dob_ref[e], jnp.where(lane == e, v_ref[rows[d], :], 0)) for d, e in pairs}
            ps, dss = {}, {}
            for d, e in pairs:
                s = jnp.where(col + d * bk <= row, ss[d, e], NEG) if masked else ss[d, e]
                p = jnp.exp2(s - tile(st_ref[2 * e]))
                dss[d, e] = (p * (dps[d, e] - tile(st_ref[2 * e + 1]))).astype(BF16)
                ps[d, e] = p.astype(BF16)
            for d, e in pairs:
                dvt_ref[0, kbs[d], e * HEAD:(e + 1) * HEAD, :] += _dot(dot_ref[e * HEAD:(e + 1) * HEAD, :], ps[d, e])
            for d, e in pairs:
                dkt_ref[0, kbs[d], hss[e], :] += _dot(qt_ref[hss[e], :], dss[d, e])
            for e in range(2):
                dq = dq_ref[:, hss[e]]
                for d in range(nsub):
                    dq = dq + _dot(dss[d, e], k_ref[rows[d], hss[e]])
                dq_ref[:, hss[e]] = dq

        step(qi * nsub, True)

        def loop(t, c):
            step(t * nsub, False)
            return c

        lax.fori_loop(0, qi, loop, 0)
        dq_ref[...] *= LN2

    dqm, dkt, dvt = pl.pallas_call(
        body, name="mla_bwd", grid=(2, nq),
        in_specs=[pl.BlockSpec((bq, 256), lambda j, i: (i, j)),
                  pl.BlockSpec((T, 256), lambda j, i: (0, j)),
                  pl.BlockSpec((T, LANES), lambda j, i: (0, j)),
                  pl.BlockSpec((bq, LANES), lambda j, i: (i, j)),
                  pl.BlockSpec((bq, LANES), lambda j, i: (i, j)),
                  pl.BlockSpec((bq, LANES), lambda j, i: (i, j))],
        out_specs=[pl.BlockSpec((bq, 256), lambda j, i: (i, j)),
                   pl.BlockSpec((1, nk, 256, bk), lambda j, i: (j, 0, 0, 0)),
                   pl.BlockSpec((1, nk, LANES, bk), lambda j, i: (j, 0, 0, 0))],
        out_shape=[jax.ShapeDtypeStruct((T, 512), F32),
                   jax.ShapeDtypeStruct((2, nk, 256, bk), F32),
                   jax.ShapeDtypeStruct((2, nk, LANES, bk), F32)],
        scratch_shapes=[pltpu.VMEM((2, bq, LANES), BF16), pltpu.VMEM((4, bq, LANES), F32),
                        pltpu.VMEM((256, bq), BF16), pltpu.VMEM((LANES, bq), BF16)],
        compiler_params=_cparams(("parallel", "arbitrary")),
    )(qm, km, vm, y, lse, dy)
    return (dqm, (jnp.transpose(dkt, (1, 3, 0, 2)) * LN2).reshape(T, 512),
            jnp.transpose(dvt, (1, 3, 0, 2)).reshape(T, 256))


def _suffix_ones(n):
    r = lax.broadcasted_iota(jnp.int32, (n, n), 0)
    c = lax.broadcasted_iota(jnp.int32, (n, n), 1)
    return (r >= c).astype(BF16)


def _prefix_ones(n):
    r = lax.broadcasted_iota(jnp.int32, (n, n), 0)
    c = lax.broadcasted_iota(jnp.int32, (n, n), 1)
    return (r <= c).astype(BF16)


def _tri_sum(x, u):
    hi, lo = _split(x)
    return _dot(hi, u) + _dot(lo, u)


def _sb_specs(T, bq):
    qo, ko, vo = (_INT_OFF[n] // LANES for n in ("d_q", "d_k", "d_v"))
    return [pl.BlockSpec((bq, LANES), lambda j, i: (i, qo + j)),
            pl.BlockSpec((T, LANES), lambda j, i: (0, ko + j)),
            pl.BlockSpec((T, LANES), lambda j, i: (0, vo + j))]


def _sb_fwd(hb):
    T = hb.shape[0]
    bq = bk = ATT_BLK
    nq = T // bq

    def body(q_ref, k_ref, v_ref, o_ref, tot_ref, cnt_ref, qm_ref, car_ref):
        qi = pl.program_id(1)
        lane = lax.broadcasted_iota(jnp.int32, (1, LANES), 1) // HEAD
        row, col = _causal_masks(bq, bk)
        strict = col < row
        u = _suffix_ones(bk)
        o_ref[...] = jnp.zeros_like(o_ref)
        car_ref[...] = jnp.zeros_like(car_ref)
        for e in range(2):
            qm_ref[e] = jnp.where(lane == e, q_ref[...], 0) * 0.125

        def step(kb, masked):
            rows = pl.ds(pl.multiple_of(kb * bk, bk), bk)
            k_pair, v_pair = k_ref[rows, :], v_ref[rows, :]
            tile = lambda a: jnp.concatenate([a] * (bk // LANES), axis=1)
            zs = [_dot_nt(qm_ref[e], k_pair) for e in range(2)]
            splits = []
            for e in range(2):
                z = zs[e]
                lk = jnp.minimum(-z, 0.0) - jnp.log(1.0 + jnp.exp(-jnp.abs(z)))
                if masked:
                    lk = jnp.where(strict, lk, 0.0)
                splits.append(_split(lk))
            sufs = [_dot(hi, u) + _dot(lo, u) for hi, lo in splits]
            aas = []
            for e in range(2):
                a = jnp.exp(zs[e] + sufs[e] + tile(car_ref[e]))
                if masked:
                    a = jnp.where(strict, a, 0.0)
                aas.append(a.astype(BF16))
                car_ref[e] += jnp.broadcast_to(sufs[e][:, 0:1], (bq, LANES))
            o_ref[...] += (_dot(aas[0], jnp.where(lane == 0, v_pair, 0))
                           + _dot(aas[1], jnp.where(lane == 1, v_pair, 0)))

        step(qi, True)

        def live():
            return jnp.max(jnp.maximum(car_ref[0], car_ref[1])) >= SB_DEAD

        def cond(c):
            return jnp.logical_and(c[0] < qi, c[1])

        def loop(c):
            step(qi - 1 - c[0], False)
            return c[0] + 1, live()

        done, _ = lax.while_loop(cond, loop, (jnp.int32(0), live()))
        tot_ref[...] = jnp.where(lane == 0, car_ref[0], car_ref[1])
        cnt_ref[pl.program_id(0), qi] = done.astype(F32)

    return pl.pallas_call(
        body, name="sb_fwd", grid=(2, nq), in_specs=_sb_specs(T, bq),
        out_specs=[pl.BlockSpec((bq, LANES), lambda j, i: (i, j)), pl.BlockSpec((bq, LANES), lambda j, i: (i, j)),
                   pl.BlockSpec(memory_space=pltpu.SMEM)],
        out_shape=[jax.ShapeDtypeStruct((T, 256), F32), jax.ShapeDtypeStruct((T, 256), F32),
                   jax.ShapeDtypeStruct((2, nq), F32)],
        scratch_shapes=[pltpu.VMEM((2, bq, LANES), BF16), pltpu.VMEM((2, bq, LANES), F32)],
        compiler_params=_cparams(("parallel", "arbitrary")),
    )(hb, hb, hb)


def _sb_bwd(hb, tot, cnt, dy):
    T = hb.shape[0]
    bq = bk = ATT_BLK
    nq = T // bq

    def body(q_ref, k_ref, v_ref, tot_ref, dy_ref, cnt_ref, dq_ref, dk_ref, dv_ref, qm_ref, dob_ref, dqa_ref, rem_ref,
             cg_ref):
        qi = pl.program_id(1)

        @pl.when(qi == 0)
        def _():
            dk_ref[...] = jnp.zeros_like(dk_ref)
            dv_ref[...] = jnp.zeros_like(dv_ref)

        lane = lax.broadcasted_iota(jnp.int32, (1, LANES), 1) // HEAD
        row, col = _causal_masks(bq, bk)
        strict = col < row
        u = _prefix_ones(bk)
        tot = tot_ref[...]
        tot_other = pltpu.roll(tot, HEAD, 1)
        dqa_ref[...] = jnp.zeros_like(dqa_ref)
        cg_ref[...] = jnp.zeros_like(cg_ref)
        for e in range(2):
            qm_ref[e] = jnp.where(lane == e, q_ref[...], 0) * 0.125
            dob_ref[e] = jnp.where(lane == e, dy_ref[...], 0.0).astype(BF16)
            rem_ref[e] = jnp.where(lane == e, tot, tot_other)

        def step(kb, masked):
            rows = pl.ds(pl.multiple_of(kb * bk, bk), bk)
            k_pair, v_pair = k_ref[rows, :], v_ref[rows, :]
            tile = lambda a: jnp.concatenate([a] * (bk // LANES), axis=1)
            zs = [_dot_nt(qm_ref[e], k_pair) for e in range(2)]
            das = [_dot_nt(dob_ref[e], jnp.where(lane == e, v_pair, 0)) for e in range(2)]
            zls, splits = [], []
            for e in range(2):
                z = zs[e]
                lk = jnp.minimum(-z, 0.0) - jnp.log(1.0 + jnp.exp(-jnp.abs(z)))
                if masked:
                    lk = jnp.where(strict, lk, 0.0)
                zls.append(z + lk)
                splits.append(_split(lk))
            pres = [_dot(hi, u) + _dot(lo, u) for hi, lo in splits]
            aas, gs, gsplits = [], [], []
            for e in range(2):
                a = jnp.exp(zls[e] + (tile(rem_ref[e]) - pres[e]))
                if masked:
                    a = jnp.where(strict, a, 0.0)
                g = a * das[e]
                aas.append(a.astype(BF16))
                gs.append(g)
                gsplits.append(_split(g))
                rem_ref[e] -= jnp.broadcast_to(pres[e][:, bk - 1:bk], (bq, LANES))
            dv_ref[rows, :] += _dot_tn(aas[0], dob_ref[0]) + _dot_tn(aas[1], dob_ref[1])
            gpres = [_dot(hi, u) + _dot(lo, u) for hi, lo in gsplits]
            dzs = []
            for e in range(2):
                dz = gs[e] - jnp.exp(zls[e]) * (tile(cg_ref[e]) + gpres[e])
                if masked:
                    dz = jnp.where(strict, dz, 0.0)
                dzs.append(dz.astype(BF16))
                cg_ref[e] += jnp.broadcast_to(gpres[e][:, bk - 1:bk], (bq, LANES))
            dk_ref[rows, :] += _dot_tn(dzs[0], qm_ref[0]) + _dot_tn(dzs[1], qm_ref[1])
            for e in range(2):
                dqa_ref[e] += _dot(dzs[e], k_pair)

        def loop(kb, c):
            step(kb, False)
            return c

        start = qi - jnp.clip(cnt_ref[pl.program_id(0), qi].astype(jnp.int32), 0, qi)
        lax.fori_loop(start, qi, loop, 0)
        step(qi, True)
        dq_ref[...] = jnp.where(lane == 0, dqa_ref[0], dqa_ref[1]) * 0.125

    return pl.pallas_call(
        body, name="sb_bwd", grid=(2, nq),
        in_specs=_sb_specs(T, bq) + [pl.BlockSpec((bq, LANES), lambda j, i: (i, j)),
                                     pl.BlockSpec((bq, LANES), lambda j, i: (i, j)),
                                     pl.BlockSpec(memory_space=pltpu.SMEM)],
        out_specs=[pl.BlockSpec((bq, LANES), lambda j, i: (i, j)),
                   pl.BlockSpec((T, LANES), lambda j, i: (0, j)),
                   pl.BlockSpec((T, LANES), lambda j, i: (0, j))],
        out_shape=[jax.ShapeDtypeStruct((T, 256), F32)] * 3,
        scratch_shapes=[pltpu.VMEM((2, bq, LANES), BF16), pltpu.VMEM((2, bq, LANES), BF16),
                        pltpu.VMEM((2, bq, LANES), F32), pltpu.VMEM((2, bq, LANES), F32),
                        pltpu.VMEM((2, bq, LANES), F32)],
        compiler_params=_cparams(("parallel", "arbitrary")),
    )(hb, hb, hb, tot, dy, cnt)


EP_TM = 256


def _ep_in_specs(tm, rev):
    idx = (lambda i: rev - i) if rev is not None else (lambda i: i)
    bo = (_INT_OFF["b_b"] - N_HB) // 256
    halo = lambda i: jnp.maximum(idx(i) * (tm // 8) - 1, 0)
    return [pl.BlockSpec((tm, 256), lambda i: (idx(i), 0)),
            pl.BlockSpec((tm, 256), lambda i: (idx(i), 0)),
            pl.BlockSpec((tm, 256), lambda i: (idx(i), 0)),
            pl.BlockSpec((tm, D_MODEL), lambda i: (idx(i), 0)),
            pl.BlockSpec((tm, 256), lambda i: (idx(i), bo)),
            pl.BlockSpec((tm, 256), lambda i: (idx(i), bo + 1)),
            pl.BlockSpec((tm, 256), lambda i: (idx(i), bo + 2)),
            pl.BlockSpec((8, 256), lambda i: (halo(i), bo + 1)),
            pl.BlockSpec((8, 256), lambda i: (halo(i), bo + 2)),
            pl.BlockSpec((3, 256), lambda i: (0, 0)),
            pl.BlockSpec((1, 256), lambda i: (0, 0)),
            pl.BlockSpec((1, D_MODEL), lambda i: (0, 0)),
            pl.BlockSpec((D_MODEL, D_MODEL), lambda i: (0, 0)),
            pl.BlockSpec((1, D_MODEL), lambda i: (0, 0))]


def _ep_mix(first, ya_ref, yc_ref, yd_ref, gate_ref, bb_ref, bc_ref, bx_ref, hc_ref, hx_ref, cw_ref, cb_ref, gg_ref):
    tm = ya_ref.shape[0]
    u = bc_ref[...] * bx_ref[...]
    halo = jnp.where(first, 0.0, hc_ref[...] * hx_ref[...])
    row = lax.broadcasted_iota(jnp.int32, (tm, 1), 0)
    u1 = jnp.where(row == 0, halo[7:8, :], pltpu.roll(u, 1, 0))
    u2 = jnp.where(row == 0, halo[6:7, :], jnp.where(row == 1, halo[7:8, :], pltpu.roll(u, 2, 0)))
    cw = cw_ref[...]
    conv = cw[0:1, :] * u2 + cw[1:2, :] * u1 + cw[2:3, :] * u + cb_ref[...]
    bb = bb_ref[...]
    ys = [ya_ref[...], bb * conv, yc_ref[...], yd_ref[...]]
    rs = [_rms(y) for y in ys]
    gg = gg_ref[...]
    yhat = jnp.concatenate([y * r for y, r in zip(ys, rs)], axis=1)
    gate = gate_ref[...]
    sig = 1.0 / (1.0 + jnp.exp(-gate))
    return u, u1, u2, conv, bb, rs, yhat, yhat * gg, gate, sig


def _epilogue_fwd(x, ya, yc, yd, hf, conv_w, conv_b, g_grp, w_out, g_post):
    T = x.shape[0]
    tm = EP_TM

    def body(x_ref, ya_ref, yc_ref, yd_ref, gate_ref, bb_ref, bc_ref, bx_ref, hc_ref, hx_ref, cw_ref, cb_ref,
             gg_ref, wo_ref, gp_ref, o_ref):
        (_, _, _, _, _, _, _, yn, gate, sig) = _ep_mix(
            pl.program_id(0) == 0, ya_ref, yc_ref, yd_ref, gate_ref, bb_ref, bc_ref, bx_ref, hc_ref, hx_ref,
            cw_ref, cb_ref, gg_ref)
        z = _dot((yn * (gate * sig)).astype(BF16), wo_ref[...])
        o_ref[...] = x_ref[...] + z * _rms(z) * gp_ref[...]

    return pl.pallas_call(
        body, name="epilogue_fwd", grid=(T // tm,),
        in_specs=[pl.BlockSpec((tm, D_MODEL), lambda i: (i, 0))] + _ep_in_specs(tm, None),
        out_specs=pl.BlockSpec((tm, D_MODEL), lambda i: (i, 0)),
        out_shape=jax.ShapeDtypeStruct((T, D_MODEL), F32),
        compiler_params=_cparams(("parallel",)),
    )(x, ya, yc, yd, hf, hf, hf, hf, hf, hf, conv_w, conv_b, g_grp, w_out, g_post)


def _epilogue_bwd(dxn, ya, yc, yd, hf, conv_w, conv_b, g_grp, w_out, g_post):
    T = dxn.shape[0]
    tm = EP_TM
    nt = T // tm
    ridx = lambda i: (nt - 1 - i, 0)

    def body(dx_ref, ya_ref, yc_ref, yd_ref, gate_ref, bb_ref, bc_ref, bx_ref, hc_ref, hx_ref, cw_ref, cb_ref,
             gg_ref, wo_ref, gp_ref,
             dya_ref, dyc_ref, dyd_ref, dhf_ref, dwo_ref, dgp_ref, dgg_ref, dcw_ref, dcb_ref, carry_ref):
        i = pl.program_id(0)

        @pl.when(i == 0)
        def _():
            for r in (dwo_ref, dgp_ref, dgg_ref, dcw_ref, dcb_ref, carry_ref):
                r[...] = jnp.zeros_like(r)

        (u, u1, u2, conv, bb, rs, yhat, yn, gate, sig) = _ep_mix(
            i == nt - 1, ya_ref, yc_ref, yd_ref, gate_ref, bb_ref, bc_ref, bx_ref, hc_ref, hx_ref,
            cw_ref, cb_ref, gg_ref)
        silu = gate * sig
        ymix = (yn * silu).astype(BF16)
        z = _dot(ymix, wo_ref[...])
        rz = _rms(z)
        dz, dgrow = _rms_bwd(dx_ref[...], z * rz, rz, gp_ref[...])
        dgp_ref[...] += _colsum(dgrow)
        dzb = dz.astype(BF16)
        dwo_ref[...] += _dot_tn(ymix, dzb)
        dymix = _dot_nt(dzb, wo_ref[...])
        dhf_ref[:, 0:D_MODEL] = dymix * yn * (sig * (1.0 + gate * (1.0 - sig)))
        dyn = dymix * silu
        dgg_ref[...] += _colsum(dyn * yhat)
        gg = gg_ref[...]
        dys = []
        for gi in range(4):
            sl = slice(gi * GROUP, (gi + 1) * GROUP)
            dyh = dyn[:, sl] * gg[:, sl]
            yh = yhat[:, sl]
            dys.append(rs[gi] * (dyh - yh * jnp.mean(dyh * yh, axis=-1, keepdims=True)))
        dya_ref[...] = dys[0]
        dyc_ref[...] = dys[2]
        dyd_ref[...] = dys[3]
        dyb = dys[1]
        dhf_ref[:, D_MODEL:D_MODEL + 256] = dyb * conv
        dconv = dyb * bb
        dcb_ref[...] += _colsum(dconv)
        dcw_ref[0:1, :] += _colsum(dconv * u2)
        dcw_ref[1:2, :] += _colsum(dconv * u1)
        dcw_ref[2:3, :] += _colsum(dconv * u)
        carry = carry_ref[...]
        row = lax.broadcasted_iota(jnp.int32, (tm, 1), 0)
        d1 = jnp.where(row == tm - 1, carry[0:1, :], pltpu.roll(dconv, tm - 1, 0))
        d2 = jnp.where(row == tm - 2, carry[0:1, :],
                       jnp.where(row == tm - 1, carry[1:2, :], pltpu.roll(dconv, tm - 2, 0)))
        cw = cw_ref[...]
        du = cw[2:3, :] * dconv + cw[1:2, :] * d1 + cw[0:1, :] * d2
        dhf_ref[:, D_MODEL + 256:D_MODEL + 512] = du * bx_ref[...]
        dhf_ref[:, D_MODEL + 512:D_MODEL + 768] = du * bc_ref[...]
        carry_ref[...] = dconv[0:8, :]

    in_specs = [pl.BlockSpec((tm, D_MODEL), ridx)] + _ep_in_specs(tm, nt - 1)
    return pl.pallas_call(
        body, name="epilogue_bwd", grid=(nt,), in_specs=in_specs,
        out_specs=[pl.BlockSpec((tm, 256), ridx), pl.BlockSpec((tm, 256), ridx), pl.BlockSpec((tm, 256), ridx),
                   pl.BlockSpec((tm, D_MODEL + 768), ridx),
                   pl.BlockSpec((D_MODEL, D_MODEL), lambda i: (0, 0)),
                   pl.BlockSpec((1, D_MODEL), lambda i: (0, 0)),
                   pl.BlockSpec((1, D_MODEL), lambda i: (0, 0)),
                   pl.BlockSpec((8, 256), lambda i: (0, 0)),
                   pl.BlockSpec((1, 256), lambda i: (0, 0))],
        out_shape=[jax.ShapeDtypeStruct((T, 256), F32)] * 3
                  + [jax.ShapeDtypeStruct((T, D_MODEL + 768), F32),
                     jax.ShapeDtypeStruct((D_MODEL, D_MODEL), F32),
                     jax.ShapeDtypeStruct((1, D_MODEL), F32),
                     jax.ShapeDtypeStruct((1, D_MODEL), F32),
                     jax.ShapeDtypeStruct((8, 256), F32),
                     jax.ShapeDtypeStruct((1, 256), F32)],
        scratch_shapes=[pltpu.VMEM((8, 256), F32)],
        compiler_params=_cparams(("arbitrary",)),
    )(dxn, ya, yc, yd, hf, hf, hf, hf, hf, hf, conv_w, conv_b, g_grp, w_out, g_post)


def _loss_head(y, tgt):
    T = y.shape[0]
    tm = 512

    def body(y_ref, t_ref, dy_ref, l_ref):
        @pl.when(pl.program_id(0) == 0)
        def _():
            l_ref[...] = jnp.zeros_like(l_ref)

        d = y_ref[...] - t_ref[...]
        dy_ref[...] = d * (1.0 / D_MODEL)
        part = jnp.sum(jnp.sum(d * d, axis=1, keepdims=True), axis=0, keepdims=True)
        l_ref[...] += part * (0.5 / D_MODEL)

    return pl.pallas_call(
        body, name="loss_head", grid=(T // tm,),
        in_specs=[pl.BlockSpec((tm, D_MODEL), lambda i: (i, 0))] * 2,
        out_specs=[pl.BlockSpec((tm, D_MODEL), lambda i: (i, 0)), pl.BlockSpec((8, LANES), lambda i: (0, 0))],
        out_shape=[jax.ShapeDtypeStruct((T, D_MODEL), F32), jax.ShapeDtypeStruct((8, LANES), F32)],
        compiler_params=_cparams(("arbitrary",)),
    )(y, tgt)


def _place():
    return lax.axis_index("x"), lax.axis_index("y"), lax.axis_index("c")


def _other_chips(x, y):
    return [(1 - x, y), (x, 1 - y), (1 - x, 1 - y)]


HBM = pl.BlockSpec(memory_space=pl.ANY)


def _gather_weights(shards):
    n = len(shards)

    def body(*refs):
        ins, outs = refs[:n], refs[n:2 * n]
        ici_send, ici_recv, d2d_send, d2d_recv, local_sems = refs[2 * n:]
        x, y, c = _place()
        me = 2 * x + y
        chips = _other_chips(x, y)

        def ici(a, j, layer_from):
            px, py = chips[j]
            return pltpu.make_async_remote_copy(
                src_ref=ins[a].at[c], dst_ref=outs[a].at[layer_from, c], send_sem=ici_send.at[3 * a + j],
                recv_sem=ici_recv.at[3 * a + j], device_id=(px, py, c), device_id_type=MESH)

        def d2d(a, j, layer):
            px, py = chips[j]
            blk = outs[a].at[2 * px + py, layer]
            return pltpu.make_async_remote_copy(
                src_ref=blk, dst_ref=blk, send_sem=d2d_send.at[3 * a + j], recv_sem=d2d_recv.at[3 * a + j],
                device_id=(x, y, 1 - c), device_id_type=MESH)

        local = [pltpu.make_async_copy(ins[a], outs[a].at[me], local_sems.at[a]) for a in range(n)]
        for cp in local:
            cp.start()
        sends = [ici(a, j, me) for j in range(3) for a in range(n)]
        for cp in sends:
            cp.start()
        for j in range(3):
            px, py = chips[j]
            for a in range(n):
                ici(a, j, 2 * px + py).wait_recv()
                fwd = d2d(a, j, c)
                fwd.start()
                sends.append(fwd)
        for j in range(3):
            for a in range(n):
                d2d(a, j, 1 - c).wait_recv()
        for cp in sends:
            cp.wait_send()
        for cp in local:
            cp.wait()

    return pl.pallas_call(
        body, name="gather_weights",
        in_specs=[HBM] * n, out_specs=[HBM] * n,
        out_shape=[jax.ShapeDtypeStruct((4,) + s.shape, s.dtype) for s in shards],
        scratch_shapes=[pltpu.SemaphoreType.DMA((3 * n,))] * 4 + [pltpu.SemaphoreType.DMA((n,))],
    )(*shards)


def _exchange_chips(parts, small):
    n = len(parts)

    def body(*refs):
        ins, sm_ref = refs[:n], refs[n]
        outs, osm_ref = refs[n + 1:2 * n + 1], refs[2 * n + 1]
        send_sems, recv_sems, ssend_sems, srecv_sems, local_sems = refs[2 * n + 2:]
        x, y, c = _place()
        me = 2 * x + y
        dev = 4 * x + 2 * y + c
        local = [pltpu.make_async_copy(ins[a].at[me], outs[a].at[me], local_sems.at[a]) for a in range(n)]
        local.append(pltpu.make_async_copy(sm_ref, osm_ref.at[dev], local_sems.at[n]))
        for cp in local:
            cp.start()
        sends = []
        for j, (px, py) in enumerate(_other_chips(x, y)):
            for a in range(n):
                cp = pltpu.make_async_remote_copy(
                    src_ref=ins[a].at[2 * px + py], dst_ref=outs[a].at[me], send_sem=send_sems.at[3 * a + j],
                    recv_sem=recv_sems.at[3 * a + j], device_id=(px, py, c), device_id_type=MESH)
                cp.start()
                sends.append(cp)
        flips = [(fx, fy, fc) for fx in (0, 1) for fy in (0, 1) for fc in (0, 1)][1:]
        for j, (fx, fy, fc) in enumerate(flips):
            cp = pltpu.make_async_remote_copy(
                src_ref=sm_ref, dst_ref=osm_ref.at[dev], send_sem=ssend_sems.at[j], recv_sem=srecv_sems.at[j],
                device_id=(x ^ fx, y ^ fy, c ^ fc), device_id_type=MESH)
            cp.start()
            sends.append(cp)
        for j, (px, py) in enumerate(_other_chips(x, y)):
            for a in range(n):
                pltpu.make_async_remote_copy(
                    src_ref=ins[a].at[me], dst_ref=outs[a].at[2 * px + py], send_sem=send_sems.at[3 * a + j],
                    recv_sem=recv_sems.at[3 * a + j], device_id=(px, py, c), device_id_type=MESH).wait_recv()
        for j, (fx, fy, fc) in enumerate(flips):
            src = 4 * (x ^ fx) + 2 * (y ^ fy) + (c ^ fc)
            pltpu.make_async_remote_copy(
                src_ref=sm_ref, dst_ref=osm_ref.at[src], send_sem=ssend_sems.at[j], recv_sem=srecv_sems.at[j],
                device_id=(x ^ fx, y ^ fy, c ^ fc), device_id_type=MESH).wait_recv()
        for cp in sends:
            cp.wait_send()
        for cp in local:
            cp.wait()

    return pl.pallas_call(
        body, name="exchange_chips",
        in_specs=[HBM] * (n + 1), out_specs=[HBM] * (n + 1),
        out_shape=[jax.ShapeDtypeStruct(p.shape, p.dtype) for p in parts]
                  + [jax.ShapeDtypeStruct((8,) + small.shape, small.dtype)],
        scratch_shapes=[pltpu.SemaphoreType.DMA((3 * n,)), pltpu.SemaphoreType.DMA((3 * n,)),
                        pltpu.SemaphoreType.DMA((7,)), pltpu.SemaphoreType.DMA((7,)),
                        pltpu.SemaphoreType.DMA((n + 1,))],
    )(*parts, small)


def _swap_cores(parts, name):
    n = len(parts)

    def body(*refs):
        ins, outs, send_sems, recv_sems = refs[:n], refs[n:2 * n], refs[2 * n], refs[2 * n + 1]
        x, y, c = _place()
        copies = [pltpu.make_async_remote_copy(
            src_ref=ins[a], dst_ref=outs[a], send_sem=send_sems.at[a], recv_sem=recv_sems.at[a],
            device_id=(x, y, 1 - c), device_id_type=MESH) for a in range(n)]
        for cp in copies:
            cp.start()
        for cp in copies:
            cp.wait()

    return pl.pallas_call(
        body, name=name, in_specs=[HBM] * n, out_specs=[HBM] * n,
        out_shape=[jax.ShapeDtypeStruct(p.shape, p.dtype) for p in parts],
        scratch_shapes=[pltpu.SemaphoreType.DMA((n,)), pltpu.SemaphoreType.DMA((n,))],
    )(*parts)


def _row_block(rows):
    for cand in (256, 128, 64, 32, 16, 8):
        if rows % cand == 0:
            return cand
    return rows


def _add(a, b, name):
    L, R, C = a.shape
    tr = _row_block(R)

    def body(a_ref, b_ref, o_ref):
        o_ref[...] = a_ref[...] + b_ref[...]

    spec = pl.BlockSpec((1, tr, C), lambda l, i: (l, i, 0))
    return pl.pallas_call(
        body, name=name, grid=(L, R // tr), in_specs=[spec, spec], out_specs=spec,
        out_shape=jax.ShapeDtypeStruct((L, R, C), F32), compiler_params=_cparams(("parallel", "parallel")),
    )(a, b)


def _sum_leading(buf, name):
    n, R, C = buf.shape
    tr = _row_block(R)

    def body(b_ref, o_ref):
        acc = b_ref[0]
        for k in range(1, n):
            acc = acc + b_ref[k]
        o_ref[...] = acc

    return pl.pallas_call(
        body, name=name, grid=(R // tr,),
        in_specs=[pl.BlockSpec((n, tr, C), lambda i: (0, i, 0))],
        out_specs=pl.BlockSpec((tr, C), lambda i: (i, 0)),
        out_shape=jax.ShapeDtypeStruct((R, C), F32),
        compiler_params=_cparams(("parallel",)),
    )(buf)


def _adam_update(w, g, m, v):
    c1 = 1.0 / (1.0 - ADAM_B1 ** ADAM_STEP)
    c2 = 1.0 / (1.0 - ADAM_B2 ** ADAM_STEP)
    mn = ADAM_B1 * m + (1.0 - ADAM_B1) * g
    vn = ADAM_B2 * v + (1.0 - ADAM_B2) * (g * g)
    return -ADAM_LR * ((mn * c1) / (jnp.sqrt(vn * c2) + ADAM_EPS) + ADAM_WD * w), mn, vn


def _adamw_layers(w, m, v, g_mine, g_other, name):
    _, R, C = w.shape
    tr = _row_block(R)

    def body(w_ref, m_ref, v_ref, gm_ref, go_ref, g_ref, d_ref, mo_ref, vo_ref):
        g = jnp.where(pl.program_id(0) == lax.axis_index("c"), gm_ref[...], go_ref[...])
        g_ref[0] = g
        d_ref[0], mo_ref[0], vo_ref[0] = _adam_update(w_ref[0], g, m_ref[0], v_ref[0])

    spec3 = pl.BlockSpec((1, tr, C), lambda l, i: (l, i, 0))
    spec2 = pl.BlockSpec((tr, C), lambda l, i: (i, 0))
    return pl.pallas_call(
        body, name=name, grid=(2, R // tr),
        in_specs=[spec3] * 3 + [spec2] * 2, out_specs=[spec3] * 4,
        out_shape=[jax.ShapeDtypeStruct(w.shape, F32)] * 4,
        compiler_params=_cparams(("parallel", "parallel")),
    )(w, m, v, g_mine, g_other)


PACK_C = 1024
_BIG = ("w_in", "w_out", "mla_w_uq", "mla_w_ukv", "conv_w")
_SMALL = ("norm_pre", "group_norm", "norm_post", "conv_b", "mla_q_norm", "mla_kv_norm", "attn_sinks")
_SMALL_W = {"norm_pre": 1024, "group_norm": 1024, "norm_post": 1024, "conv_b": 256, "mla_q_norm": 256,
            "mla_kv_norm": 128, "attn_sinks": 4}


def _pack_small(d):
    flat = jnp.concatenate([d[n].reshape(-1) for n in _SMALL])
    return jnp.pad(flat, (0, 8 * PACK_C - flat.shape[0])).reshape(8, PACK_C)


def _adamw_small(w, m, v, got):
    ns = len(_SMALL)

    def body(*refs):
        got_ref = refs[3 * ns]
        outs = refs[3 * ns + 1:]
        gsum = got_ref[0]
        for d in range(1, 8):
            gsum = gsum + got_ref[d]
        off = 0
        for i, name in enumerate(_SMALL):
            wd = _SMALL_W[name]
            rows = []
            for l in range(DEPTH):
                r, c0 = divmod(off + l * wd, PACK_C)
                rows.append(gsum[r:r + 1, c0:c0 + wd])
            off += DEPTH * wd
            g = jnp.concatenate(rows, axis=0)
            delta, mn, vn = _adam_update(refs[i][...], g, refs[ns + i][...], refs[2 * ns + i][...])
            outs[i][...] = g
            outs[ns + i][...] = delta
            outs[2 * ns + i][...] = mn
            outs[3 * ns + i][...] = vn

    shapes = [jax.ShapeDtypeStruct(w[n].shape, F32) for n in _SMALL]
    res = pl.pallas_call(body, name="adamw_small", out_shape=shapes * 4)(
        *[w[n] for n in _SMALL], *[m[n] for n in _SMALL], *[v[n] for n in _SMALL], got)
    return [dict(zip(_SMALL, res[k * ns:(k + 1) * ns])) for k in range(4)]


def _w_in_internal(w):
    cols = []
    for n in _INT_ORDER:
        o, wd = _REAL_OFF[n]
        cols.append(w[:, o:o + wd])
        if _INT_W[n] != wd:
            cols.append(jnp.zeros((w.shape[0], _INT_W[n] - wd), w.dtype))
    return jnp.concatenate(cols, axis=1)


def _w_in_real(dw):
    return jnp.concatenate([dw[:, _INT_OFF[n]:_INT_OFF[n] + wd] for n, wd in _REAL], axis=1)


def _uq_internal(w):
    return jnp.pad(w.reshape(256, 4, 96), ((0, 0), (0, 0), (0, 32))).reshape(256, 512)


def _uq_real(dw):
    return dw.reshape(256, 4, 128)[:, :, :96].reshape(256, 384)


def _ukv_internal(w):
    w4 = w.reshape(128, 4, 128)
    k = jnp.pad(w4[:, :, :64], ((0, 0), (0, 0), (0, 64))).reshape(128, 512)
    return jnp.concatenate([k, w4[:, :, 64:].reshape(128, 256)], axis=1)


def _ukv_real(dw):
    k = dw[:, :512].reshape(128, 4, 128)[:, :, :64]
    v = dw[:, 512:].reshape(128, 4, 64)
    return jnp.concatenate([k, v], axis=2).reshape(128, 512)


def _layer_fwd(x, pos, p):
    xn, hb, hf = _inproj_fwd(x, p["norm_pre"], p["w_in"])
    ya = _swa_fwd(hb, p["attn_sinks"])
    qm, km, vm, vt = _mla_prep_fwd(hf, pos, p["mla_q_norm"], p["mla_kv_norm"], p["mla_w_uq"], p["mla_w_ukv"])
    yc, lse = _mla_fwd(qm, km, vt)
    yd, tot, cnt = _sb_fwd(hb)
    x_next = _epilogue_fwd(x, ya, yc, yd, hf, p["conv_w"], p["conv_b"], p["group_norm"], p["w_out"], p["norm_post"])
    return x_next, dict(x=x, xn=xn, hb=hb, hf=hf, ya=ya, yc=yc, yd=yd, tot=tot, cnt=cnt, qm=qm, km=km, vm=vm, lse=lse)


def _layer_bwd(dx_next, pos, p, s):
    (dya, dyc, dyd, dhf, dw_out, dg_post, dg_grp, dconv_w, dconv_b) = _epilogue_bwd(
        dx_next, s["ya"], s["yc"], s["yd"], s["hf"], p["conv_w"], p["conv_b"], p["group_norm"], p["w_out"],
        p["norm_post"])
    dq_d, dk_d, dv_d = _sb_bwd(s["hb"], s["tot"], s["cnt"], dyd)
    dqm, dkm, dvm = _mla_bwd(s["qm"], s["km"], s["vm"], s["yc"], s["lse"], dyc)
    dc, dw_uq, dw_ukv, dg_q, dg_kv = _mla_prep_bwd(
        s["hf"], pos, p["mla_q_norm"], p["mla_kv_norm"], p["mla_w_uq"], p["mla_w_ukv"], dqm, dkm, dvm)
    dq_a, dk_a, dv_a, dsinks = _swa_bwd(s["hb"], p["attn_sinks"], dya)
    dx, dh, dg_pre = _inproj_bwd_dx(s["x"], p["norm_pre"], p["w_in"], dx_next,
                                    [dq_a, dk_a, dv_a, dq_d, dk_d, dv_d, dhf, dc])
    dw_in = _matmul_tn(s["xn"], dh, "inproj_bwd_dw")
    grads = dict(norm_pre=dg_pre[0], w_in=_w_in_real(dw_in), attn_sinks=dsinks[0, :4], conv_w=dconv_w[:3],
                 conv_b=dconv_b[0], mla_q_norm=dg_q[0], mla_w_uq=_uq_real(dw_uq), mla_kv_norm=dg_kv[0],
                 mla_w_ukv=_ukv_real(dw_ukv), group_norm=dg_grp[0], w_out=dw_out, norm_post=dg_post[0])
    return dx, grads


_WEIGHTS = ["norm_pre", "w_in", "attn_sinks", "conv_w", "conv_b", "mla_q_norm", "mla_w_uq", "mla_kv_norm",
            "mla_w_ukv", "group_norm", "w_out", "norm_post"]


def kernel(x, positions, norm_pre, w_in, attn_sinks, conv_w, conv_b, mla_q_norm, mla_w_uq, mla_kv_norm, mla_w_ukv, group_norm, w_out, norm_post, loss_target, m_norm_pre, m_w_in, m_attn_sinks, m_conv_w, m_conv_b, m_mla_q_norm, m_mla_w_uq, m_mla_kv_norm, m_mla_w_ukv, m_group_norm, m_w_out, m_norm_post, v_norm_pre, v_w_in, v_attn_sinks, v_conv_w, v_conv_b, v_mla_q_norm, v_mla_w_uq, v_mla_kv_norm, v_mla_w_ukv, v_group_norm, v_w_out, v_norm_post):
    w = dict(norm_pre=norm_pre, w_in=w_in, attn_sinks=attn_sinks, conv_w=conv_w, conv_b=conv_b,
             mla_q_norm=mla_q_norm, mla_w_uq=mla_w_uq, mla_kv_norm=mla_kv_norm, mla_w_ukv=mla_w_ukv,
             group_norm=group_norm, w_out=w_out, norm_post=norm_post)
    m = dict(norm_pre=m_norm_pre, w_in=m_w_in, attn_sinks=m_attn_sinks, conv_w=m_conv_w, conv_b=m_conv_b,
             mla_q_norm=m_mla_q_norm, mla_w_uq=m_mla_w_uq, mla_kv_norm=m_mla_kv_norm, mla_w_ukv=m_mla_w_ukv,
             group_norm=m_group_norm, w_out=m_w_out, norm_post=m_norm_post)
    v = dict(norm_pre=v_norm_pre, w_in=v_w_in, attn_sinks=v_attn_sinks, conv_w=v_conv_w, conv_b=v_conv_b,
             mla_q_norm=v_mla_q_norm, mla_w_uq=v_mla_w_uq, mla_kv_norm=v_mla_kv_norm, mla_w_ukv=v_mla_w_ukv,
             group_norm=v_group_norm, w_out=v_w_out, norm_post=v_norm_post)
    T = x.shape[1]
    xs = x[0]
    pos = positions[0].reshape(T, 1)
    tgt = loss_target[0]
    core = lax.axis_index("c")

    gathered = _gather_weights([w[n].astype(BF16) for n in _BIG[:4]] + [w["conv_w"]])
    full = {}
    for n, got in zip(_BIG, gathered):
        if n == "w_out":
            full[n] = jnp.moveaxis(got, 0, 1).reshape(DEPTH, D_MODEL, D_MODEL)
        else:
            full[n] = jnp.transpose(got, (1, 2, 0, 3)).reshape(DEPTH, got.shape[2], 4 * got.shape[3])

    layers = []
    for l in range(DEPTH):
        layers.append(dict(
            norm_pre=norm_pre[l:l + 1], w_in=_w_in_internal(full["w_in"][l]), attn_sinks=attn_sinks[l],
            conv_w=full["conv_w"][l], conv_b=conv_b[l:l + 1], mla_q_norm=mla_q_norm[l:l + 1],
            mla_w_uq=_uq_internal(full["mla_w_uq"][l]), mla_kv_norm=mla_kv_norm[l:l + 1],
            mla_w_ukv=_ukv_internal(full["mla_w_ukv"][l]), group_norm=group_norm[l:l + 1],
            w_out=full["w_out"][l], norm_post=norm_post[l:l + 1]))

    saved = []
    h = xs
    for l in range(DEPTH):
        h, s = _layer_fwd(h, pos, layers[l])
        saved.append(s)
    dy, loss_part = _loss_head(h, tgt)
    loss = lax.psum(loss_part[0, 0], ("x", "y", "c"))

    grads = [None] * DEPTH
    for l in reversed(range(DEPTH)):
        dy, grads[l] = _layer_bwd(dy, pos, layers[l], saved[l])

    def chunks(n, a):
        if n == "w_out":
            return a.reshape(4, D_MODEL // 4, D_MODEL)
        return jnp.transpose(a.reshape(a.shape[0], 4, a.shape[1] // 4), (1, 0, 2))

    mine = [chunks(n, jnp.where(core == 0, grads[0][n], grads[1][n])) for n in _BIG]
    theirs = [chunks(n, jnp.where(core == 0, grads[1][n], grads[0][n])) for n in _BIG]
    from_sibling = _swap_cores(theirs, "swap_layer_chunks")
    summed = [_add(a, b, "add_cores_" + n) for n, a, b in zip(_BIG, mine, from_sibling)]
    small = _pack_small({n: jnp.stack([grads[l][n] for l in range(DEPTH)]) for n in _SMALL})
    *got, got_small = _exchange_chips(summed, small)
    done = [_sum_leading(b, "sum_chips_" + n) for n, b in zip(_BIG, got)]
    done_other = _swap_cores(done, "swap_layer_shards")

    outs = _adamw_small(w, m, v, got_small)
    for n, gm, go in zip(_BIG, done, done_other):
        for d, a in zip(outs, _adamw_layers(w[n], m[n], v[n], gm, go, "adamw_" + n)):
            d[n] = a
    return (loss, dy[None], *[outs[0][n] for n in _WEIGHTS], *[outs[1][n] for n in _WEIGHTS],
            *[outs[2][n] for n in _WEIGHTS], *[outs[3][n] for n in _WEIGHTS])
```

```python
import math

import jax
import jax.numpy as jnp
from jax import lax
from jax.experimental import pallas as pl
from jax.experimental.pallas import tpu as pltpu

F32 = jnp.float32
BF16 = jnp.bfloat16
MESH = pl.DeviceIdType.MESH

D_MODEL = 1024
DEPTH = 2
EPS = 1e-6
BLOCK = 128
HEAD = 64
LANES = 128
GROUP = 256
LOG2E = 1.4426950408889634
LN2 = 0.6931471805599453
MLA_QSCALE = 96 ** -0.5 * LOG2E
ROPE_HALF = 16
ROPE_THETA = 10000.0
ATT_BLK = 256
MLA_BQ = 512
NEG = -1e30
SB_DEAD = -104.0

ADAM_LR, ADAM_B1, ADAM_B2, ADAM_EPS, ADAM_WD, ADAM_STEP = 0.001, 0.9, 0.999, 1e-08, 0.01, 10

_REAL = [("a_q", 256), ("a_k", 128), ("a_v", 128), ("b_b", 256), ("b_c", 256), ("b_x", 256),
         ("c_q", 256), ("c_kv", 128), ("c_kr", 32), ("d_q", 256), ("d_k", 256), ("d_v", 256),
         ("gate", 1024)]
_REAL_OFF = {}
_o = 0
for _n, _w in _REAL:
    _REAL_OFF[_n] = (_o, _w)
    _o += _w
D_IN = _o
_INT_ORDER = ["a_q", "a_k", "a_v", "d_q", "d_k", "d_v", "gate", "b_b", "b_c", "b_x", "c_q", "c_kv", "c_kr"]
_INT_W = dict(_REAL)
_INT_W["c_kr"] = 128
_INT_OFF = {}
_o = 0
for _n in _INT_ORDER:
    _INT_OFF[_n] = _o
    _o += _INT_W[_n]
N_INT = _o
N_HB = _INT_OFF["gate"]
N_HF = N_INT - N_HB

VMEM_LIMIT = 56 * 1024 * 1024


def _cparams(sem):
    return pltpu.CompilerParams(dimension_semantics=sem, vmem_limit_bytes=VMEM_LIMIT)


def _dot(a, b):
    return jnp.dot(a, b, preferred_element_type=F32)


def _dot_nt(a, b):
    return lax.dot_general(a, b, (((1,), (1,)), ((), ())), preferred_element_type=F32)


def _dot_tn(a, b):
    return lax.dot_general(a, b, (((0,), (0,)), ((), ())), preferred_element_type=F32)


def _split(x):
    hi = x.astype(BF16)
    lo = (x - hi.astype(F32)).astype(BF16)
    return hi, lo


def _rms(x):
    return lax.rsqrt(jnp.mean(x * x, axis=-1, keepdims=True) + EPS)


def _rms_bwd(dy, xhat, r, g):
    dxhat = dy * g
    return r * (dxhat - xhat * jnp.mean(dxhat * xhat, axis=-1, keepdims=True)), dy * xhat


def _colsum(x):
    return jnp.sum(x, axis=0, keepdims=True)


def _inproj_fwd(x, g, w):
    T = x.shape[0]
    tm = 256

    def body(x_ref, g_ref, w_ref, xt_ref, hb_ref, hf_ref):
        xv = x_ref[...]
        xn32 = xv * _rms(xv) * g_ref[...]
        xt_ref[...] = jnp.transpose(xn32).astype(BF16)
        h = _dot(xn32.astype(BF16), w_ref[...])
        hb_ref[...] = h[:, :N_HB].astype(BF16)
        hf_ref[...] = h[:, N_HB:]

    return pl.pallas_call(
        body, name="inproj_fwd", grid=(T // tm,),
        in_specs=[pl.BlockSpec((tm, D_MODEL), lambda i: (i, 0)),
                  pl.BlockSpec((1, D_MODEL), lambda i: (0, 0)),
                  pl.BlockSpec((D_MODEL, N_INT), lambda i: (0, 0))],
        out_specs=[pl.BlockSpec((D_MODEL, tm), lambda i: (0, i)),
                   pl.BlockSpec((tm, N_HB), lambda i: (i, 0)),
                   pl.BlockSpec((tm, N_HF), lambda i: (i, 0))],
        out_shape=[jax.ShapeDtypeStruct((D_MODEL, T), BF16),
                   jax.ShapeDtypeStruct((T, N_HB), BF16),
                   jax.ShapeDtypeStruct((T, N_HF), F32)],
        compiler_params=_cparams(("parallel",)),
    )(x, g, w)


def _inproj_bwd_dx(x, g, w, dx_next, pieces):
    T = x.shape[0]
    tm = 256
    widths = [p.shape[1] for p in pieces]
    assert sum(widths) == N_INT

    def body(x_ref, g_ref, w_ref, dxn_ref, *rest):
        p_refs = rest[:len(pieces)]
        dx_ref, dh_ref, dg_ref = rest[len(pieces):]
        dh = jnp.concatenate([p[...].astype(BF16) for p in p_refs], axis=1)
        dh_ref[...] = dh
        dxn = _dot_nt(dh, w_ref[...])
        xv = x_ref[...]
        r = _rms(xv)
        dx, dgrow = _rms_bwd(dxn, xv * r, r, g_ref[...])
        dx_ref[...] = dx + dxn_ref[...]

        @pl.when(pl.program_id(0) == 0)
        def _():
            dg_ref[...] = jnp.zeros_like(dg_ref)

        dg_ref[...] += _colsum(dgrow)

    return pl.pallas_call(
        body, name="inproj_bwd_dx", grid=(T // tm,),
        in_specs=[pl.BlockSpec((tm, D_MODEL), lambda i: (i, 0)),
                  pl.BlockSpec((1, D_MODEL), lambda i: (0, 0)),
                  pl.BlockSpec((D_MODEL, N_INT), lambda i: (0, 0)),
                  pl.BlockSpec((tm, D_MODEL), lambda i: (i, 0))]
                 + [pl.BlockSpec((tm, wd), lambda i: (i, 0)) for wd in widths],
        out_specs=[pl.BlockSpec((tm, D_MODEL), lambda i: (i, 0)),
                   pl.BlockSpec((tm, N_INT), lambda i: (i, 0)),
                   pl.BlockSpec((1, D_MODEL), lambda i: (0, 0))],
        out_shape=[jax.ShapeDtypeStruct((T, D_MODEL), F32),
                   jax.ShapeDtypeStruct((T, N_INT), BF16),
                   jax.ShapeDtypeStruct((1, D_MODEL), F32)],
        compiler_params=_cparams(("arbitrary",)),
    )(x, g, w, dx_next, *pieces)


def _matmul_over_tokens(at, b, name):
    M, T = at.shape
    N = b.shape[1]
    tm, tn = min(1024, T), 512

    def body(a_ref, b_ref, o_ref):
        @pl.when(pl.program_id(1) == 0)
        def _():
            o_ref[...] = jnp.zeros_like(o_ref)

        o_ref[...] += _dot(a_ref[...], b_ref[...])

    return pl.pallas_call(
        body, name=name, grid=(N // tn, T // tm),
        in_specs=[pl.BlockSpec((M, tm), lambda j, t: (0, t)),
                  pl.BlockSpec((tm, tn), lambda j, t: (t, j))],
        out_specs=pl.BlockSpec((M, tn), lambda j, t: (0, j)),
        out_shape=jax.ShapeDtypeStruct((M, N), F32),
        compiler_params=_cparams(("parallel", "arbitrary")),
    )(at, b)


def _roll_f32(x, shift):
    return pltpu.roll(x.astype(F32), shift, 1)


def _swa_operands(h, q_ref, kp_ref, kc_ref, vp_ref, vc_ref):
    p, e = h // 2, h % 2
    lane = lax.broadcasted_iota(jnp.int32, (1, LANES), 1) // HEAD
    q = q_ref[:, p * LANES:(p + 1) * LANES]
    k_prev, k_cur, v_prev, v_cur = kp_ref[...], kc_ref[...], vp_ref[...], vc_ref[...]
    if e != p:
        q = _roll_f32(q, HEAD).astype(BF16)
        v_prev = _roll_f32(v_prev, HEAD).astype(BF16)
        v_cur = _roll_f32(v_cur, HEAD).astype(BF16)
    qs = jnp.where(lane == p, q, 0) * 0.125
    return dict(p=p, e=e, lane=lane, qs=qs, k_prev=k_prev, k_cur=k_cur,
                v_prev=jnp.where(lane == e, v_prev, 0), v_cur=jnp.where(lane == e, v_cur, 0),
                s_prev=_dot_nt(qs, k_prev), s_cur=_dot_nt(qs, k_cur))


def _swa_probs(ops, sink, first):
    row = lax.broadcasted_iota(jnp.int32, (BLOCK, BLOCK), 0)
    col = lax.broadcasted_iota(jnp.int32, (BLOCK, BLOCK), 1)
    s_prev = jnp.where(jnp.logical_and(col > row, jnp.logical_not(first)), ops["s_prev"], NEG)
    s_cur = jnp.where(col <= row, ops["s_cur"], NEG)
    m = jnp.maximum(jnp.maximum(jnp.max(s_prev, axis=1, keepdims=True),
                                jnp.max(s_cur, axis=1, keepdims=True)), sink)
    p_prev = jnp.exp(s_prev - m)
    p_cur = jnp.exp(s_cur - m)
    p_sink = jnp.exp(sink - m)
    inv = 1.0 / (jnp.sum(p_prev, axis=1, keepdims=True) + jnp.sum(p_cur, axis=1, keepdims=True) + p_sink)
    return p_prev * inv, p_cur * inv, p_sink * inv


def _swa_specs(T):
    nb = T // BLOCK
    qo, ko, vo = (_INT_OFF[n] // LANES for n in ("a_q", "a_k", "a_v"))
    prev = lambda i: jnp.maximum(i - 1, 0)
    return [pl.BlockSpec((BLOCK, 256), lambda i: (i, qo // 2)),
            pl.BlockSpec((BLOCK, LANES), lambda i: (prev(i), ko)),
            pl.BlockSpec((BLOCK, LANES), lambda i: (i, ko)),
            pl.BlockSpec((BLOCK, LANES), lambda i: (prev(i), vo)),
            pl.BlockSpec((BLOCK, LANES), lambda i: (i, vo)),
            pl.BlockSpec(memory_space=pltpu.SMEM)], nb


def _swa_fwd(hb, sinks):
    T = hb.shape[0]
    specs, nb = _swa_specs(T)

    def body(q_ref, kp_ref, kc_ref, vp_ref, vc_ref, s_ref, o_ref):
        first = pl.program_id(0) == 0
        ops = [_swa_operands(h, q_ref, kp_ref, kc_ref, vp_ref, vc_ref) for h in range(4)]
        probs = [_swa_probs(ops[h], s_ref[h], first) for h in range(4)]
        outs = [_dot(probs[h][0].astype(BF16), ops[h]["v_prev"]) + _dot(probs[h][1].astype(BF16), ops[h]["v_cur"])
                for h in range(4)]
        for p in range(2):
            o_ref[:, p * LANES:(p + 1) * LANES] = outs[2 * p] + outs[2 * p + 1]

    return pl.pallas_call(
        body, name="swa_fwd", grid=(nb,), in_specs=specs,
        out_specs=pl.BlockSpec((BLOCK, 256), lambda i: (i, 0)),
        out_shape=jax.ShapeDtypeStruct((T, 256), F32),
        compiler_params=_cparams(("parallel",)),
    )(hb, hb, hb, hb, hb, sinks)


def _swa_bwd(hb, sinks, dy):
    T = hb.shape[0]
    specs, nb = _swa_specs(T)

    def body(q_ref, kp_ref, kc_ref, vp_ref, vc_ref, s_ref, dy_ref, dq_ref, dk_ref, dv_ref, ds_ref):
        i = pl.program_id(0)
        first = i == 0
        cur = pl.ds(pl.multiple_of(i * BLOCK, BLOCK), BLOCK)
        prv = pl.ds(pl.multiple_of(jnp.maximum(i - 1, 0) * BLOCK, BLOCK), BLOCK)

        @pl.when(first)
        def _():
            ds_ref[...] = jnp.zeros_like(ds_ref)

        dk_ref[cur, :] = jnp.zeros((BLOCK, LANES), F32)
        dv_ref[cur, :] = jnp.zeros((BLOCK, LANES), F32)
        lane_id = lax.broadcasted_iota(jnp.int32, (8, LANES), 1)
        heads = range(4)
        ops = [_swa_operands(h, q_ref, kp_ref, kc_ref, vp_ref, vc_ref) for h in heads]
        probs = [_swa_probs(ops[h], s_ref[h], first) for h in heads]
        dos = [jnp.where(ops[h]["lane"] == ops[h]["e"], dy_ref[:, ops[h]["p"] * LANES:(ops[h]["p"] + 1) * LANES], 0.0)
               for h in heads]
        dobs = [d.astype(BF16) for d in dos]
        pbs = [(probs[h][0].astype(BF16), probs[h][1].astype(BF16)) for h in heads]
        outs = [_dot(pbs[h][0], ops[h]["v_prev"]) + _dot(pbs[h][1], ops[h]["v_cur"]) for h in heads]
        dps = [(_dot_nt(dobs[h], ops[h]["v_prev"]), _dot_nt(dobs[h], ops[h]["v_cur"])) for h in heads]
        dss, dsinks = [], jnp.zeros((8, LANES), F32)
        for h in heads:
            delta = jnp.sum(dos[h] * outs[h], axis=1, keepdims=True)
            dss.append(((probs[h][0] * (dps[h][0] - delta)).astype(BF16),
                        (probs[h][1] * (dps[h][1] - delta)).astype(BF16)))
            dsink = -jnp.sum(probs[h][2] * delta, axis=0, keepdims=True)
            dsinks += jnp.where(lane_id == h, dsink, 0.0)
        ds_ref[...] += dsinks
        dqs = [(_dot(dss[h][0], ops[h]["k_prev"]) + _dot(dss[h][1], ops[h]["k_cur"])) * 0.125 for h in heads]
        dk_prev = dk_cur = dv_prev = dv_cur = jnp.zeros((BLOCK, LANES), F32)
        for h in heads:
            p, e = ops[h]["p"], ops[h]["e"]
            dob_v = dobs[h] if e == p else pltpu.roll(dos[h], HEAD, 1).astype(BF16)
            dk_prev += _dot_tn(dss[h][0], ops[h]["qs"])
            dk_cur += _dot_tn(dss[h][1], ops[h]["qs"])
            dv_prev += _dot_tn(pbs[h][0], dob_v)
            dv_cur += _dot_tn(pbs[h][1], dob_v)
        dk_ref[prv, :] += dk_prev
        dk_ref[cur, :] += dk_cur
        dv_ref[prv, :] += dv_prev
        dv_ref[cur, :] += dv_cur
        for p in range(2):
            dq_pair = jnp.zeros((BLOCK, LANES), F32)
            for e in range(2):
                dq = jnp.where(ops[2 * p + e]["lane"] == p, dqs[2 * p + e], 0.0)
                dq_pair += dq if e == p else pltpu.roll(dq, HEAD, 1)
            dq_ref[:, p * LANES:(p + 1) * LANES] = dq_pair

    return pl.pallas_call(
        body, name="swa_bwd", grid=(nb,),
        in_specs=specs + [pl.BlockSpec((BLOCK, 256), lambda i: (i, 0))],
        out_specs=[pl.BlockSpec((BLOCK, 256), lambda i: (i, 0)),
                   pl.BlockSpec((T, LANES), lambda i: (0, 0)),
                   pl.BlockSpec((T, LANES), lambda i: (0, 0)),
                   pl.BlockSpec((8, LANES), lambda i: (0, 0))],
        out_shape=[jax.ShapeDtypeStruct((T, 256), F32),
                   jax.ShapeDtypeStruct((T, LANES), F32),
                   jax.ShapeDtypeStruct((T, LANES), F32),
                   jax.ShapeDtypeStruct((8, LANES), F32)],
        compiler_params=_cparams(("arbitrary",)),
    )(hb, hb, hb, hb, hb, sinks, dy)


def _rope_tables(pos_ref):
    lane = lax.broadcasted_iota(jnp.int32, (1, LANES), 1)
    active = jnp.logical_and(lane >= HEAD, lane < HEAD + 2 * ROPE_HALF)
    idx = ((lane - HEAD) % ROPE_HALF).astype(F32)
    freq = jnp.exp(idx * (-math.log(ROPE_THETA) / ROPE_HALF))
    ang = pos_ref[...].astype(F32) * freq
    cos, sin = jnp.cos(ang), jnp.sin(ang)
    c = jnp.where(active, cos, 1.0)
    s_up = jnp.where(jnp.logical_and(active, lane >= HEAD + ROPE_HALF), sin, 0.0)
    s_dn = jnp.where(jnp.logical_and(active, lane < HEAD + ROPE_HALF), -sin, 0.0)
    return c, s_up, s_dn


def _rope(x, tabs):
    c, s_up, s_dn = tabs
    return x * c + pltpu.roll(x, ROPE_HALF, 1) * s_up + pltpu.roll(x, LANES - ROPE_HALF, 1) * s_dn


def _rope_t(dy, tabs):
    c, s_up, s_dn = tabs
    return dy * c + pltpu.roll(dy * s_up, LANES - ROPE_HALF, 1) + pltpu.roll(dy * s_dn, ROPE_HALF, 1)


def _mla_lat_specs(tm):
    cq, ckv, ckr = ((_INT_OFF[n] - N_HB) for n in ("c_q", "c_kv", "c_kr"))
    return [pl.BlockSpec((tm, 256), lambda i: (i, cq // 256)),
            pl.BlockSpec((tm, LANES), lambda i: (i, ckv // LANES)),
            pl.BlockSpec((tm, LANES), lambda i: (i, ckr // LANES)),
            pl.BlockSpec((tm, 1), lambda i: (i, 0)),
            pl.BlockSpec((1, 256), lambda i: (0, 0)),
            pl.BlockSpec((1, LANES), lambda i: (0, 0)),
            pl.BlockSpec((256, 512), lambda i: (0, 0)),
            pl.BlockSpec((LANES, 768), lambda i: (0, 0))]


def _mla_prep_fwd(hf, pos, g_q, g_kv, w_uq, w_ukv):
    T = hf.shape[0]
    tm = 512
    sub = tm // ATT_BLK

    def body(cq_ref, ckv_ref, ckr_ref, pos_ref, gq_ref, gkv_ref, wq_ref, wkv_ref, qm_ref, km_ref, vm_ref, vt_ref):
        tabs = _rope_tables(pos_ref)
        cq = cq_ref[...]
        q = _dot((cq * _rms(cq) * gq_ref[...]).astype(BF16), wq_ref[...])
        ckv = ckv_ref[...]
        kv = _dot((ckv * _rms(ckv) * gkv_ref[...]).astype(BF16), wkv_ref[...])
        kr = _rope(pltpu.roll(ckr_ref[...], HEAD, 1), tabs)
        for h in range(4):
            sl = slice(h * LANES, (h + 1) * LANES)
            qm_ref[:, sl] = (_rope(q[:, sl], tabs) * MLA_QSCALE).astype(BF16)
            km_ref[:, sl] = (kv[:, sl] + kr).astype(BF16)
        vm_ref[...] = kv[:, 512:].astype(BF16)
        for p in range(2):
            for s in range(sub):
                tile = kv[s * ATT_BLK:(s + 1) * ATT_BLK, 512 + p * LANES:512 + (p + 1) * LANES]
                vt_ref[p, s] = jnp.transpose(tile).astype(BF16)

    return pl.pallas_call(
        body, name="mla_prep_fwd", grid=(T // tm,), in_specs=_mla_lat_specs(tm),
        out_specs=[pl.BlockSpec((tm, 512), lambda i: (i, 0)),
                   pl.BlockSpec((tm, 512), lambda i: (i, 0)),
                   pl.BlockSpec((tm, 256), lambda i: (i, 0)),
                   pl.BlockSpec((2, sub, LANES, ATT_BLK), lambda i: (0, i, 0, 0))],
        out_shape=[jax.ShapeDtypeStruct((T, 512), BF16),
                   jax.ShapeDtypeStruct((T, 512), BF16),
                   jax.ShapeDtypeStruct((T, 256), BF16),
                   jax.ShapeDtypeStruct((2, T // ATT_BLK, LANES, ATT_BLK), BF16)],
        compiler_params=_cparams(("parallel",)),
    )(hf, hf, hf, pos, g_q, g_kv, w_uq, w_ukv)


def _mla_prep_bwd(hf, pos, g_q, g_kv, w_uq, w_ukv, dqm, dkm, dvm):
    T = hf.shape[0]
    tm = 512

    def body(cq_ref, ckv_ref, ckr_ref, pos_ref, gq_ref, gkv_ref, wq_ref, wkv_ref, dq_ref, dk_ref, dv_ref,
             dc_ref, dwq_ref, dwkv_ref, dgq_ref, dgkv_ref):
        @pl.when(pl.program_id(0) == 0)
        def _():
            dwq_ref[...] = jnp.zeros_like(dwq_ref)
            dwkv_ref[...] = jnp.zeros_like(dwkv_ref)
            dgq_ref[...] = jnp.zeros_like(dgq_ref)
            dgkv_ref[...] = jnp.zeros_like(dgkv_ref)

        tabs = _rope_tables(pos_ref)
        lane = lax.broadcasted_iota(jnp.int32, (1, LANES), 1)
        dq = jnp.concatenate([_rope_t(dq_ref[:, h * LANES:(h + 1) * LANES] * MLA_QSCALE, tabs)
                              for h in range(4)], axis=1).astype(BF16)
        cq = cq_ref[...]
        rq = _rms(cq)
        cqn = (cq * rq * gq_ref[...]).astype(BF16)
        dwq_ref[...] += _dot_tn(cqn, dq)
        dcq, dgrow = _rms_bwd(_dot_nt(dq, wq_ref[...]), cq * rq, rq, gq_ref[...])
        dgq_ref[...] += _colsum(dgrow)
        dc_ref[:, 0:256] = dcq

        dk = dk_ref[...]
        dkr = dk[:, 0:LANES] + dk[:, LANES:2 * LANES] + dk[:, 2 * LANES:3 * LANES] + dk[:, 3 * LANES:]
        dkr = pltpu.roll(_rope_t(dkr, tabs), HEAD, 1)
        dc_ref[:, 384:512] = jnp.where(lane < 2 * ROPE_HALF, dkr, 0.0)
        dkv = jnp.concatenate([dk.astype(BF16), dv_ref[...].astype(BF16)], axis=1)
        ckv = ckv_ref[...]
        rkv = _rms(ckv)
        ckvn = (ckv * rkv * gkv_ref[...]).astype(BF16)
        dwkv_ref[...] += _dot_tn(ckvn, dkv)
        dckv, dgrow = _rms_bwd(_dot_nt(dkv, wkv_ref[...]), ckv * rkv, rkv, gkv_ref[...])
        dgkv_ref[...] += _colsum(dgrow)
        dc_ref[:, 256:384] = dckv

    return pl.pallas_call(
        body, name="mla_prep_bwd", grid=(T // tm,),
        in_specs=_mla_lat_specs(tm) + [pl.BlockSpec((tm, 512), lambda i: (i, 0)),
                                       pl.BlockSpec((tm, 512), lambda i: (i, 0)),
                                       pl.BlockSpec((tm, 256), lambda i: (i, 0))],
        out_specs=[pl.BlockSpec((tm, 512), lambda i: (i, 0)),
                   pl.BlockSpec((256, 512), lambda i: (0, 0)),
                   pl.BlockSpec((LANES, 768), lambda i: (0, 0)),
                   pl.BlockSpec((1, 256), lambda i: (0, 0)),
                   pl.BlockSpec((1, LANES), lambda i: (0, 0))],
        out_shape=[jax.ShapeDtypeStruct((T, 512), F32),
                   jax.ShapeDtypeStruct((256, 512), F32),
                   jax.ShapeDtypeStruct((LANES, 768), F32),
                   jax.ShapeDtypeStruct((1, 256), F32),
                   jax.ShapeDtypeStruct((1, LANES), F32)],
        compiler_params=_cparams(("arbitrary",)),
    )(hf, hf, hf, pos, g_q, g_kv, w_uq, w_ukv, dqm, dkm, dvm)


def _causal_masks(bq, bk):
    row = lax.broadcasted_iota(jnp.int32, (bq, bk), 0)
    col = lax.broadcasted_iota(jnp.int32, (bq, bk), 1)
    return row, col


def _mla_fwd(qm, km, vt):
    T = qm.shape[0]
    bq, bk = min(MLA_BQ, T), ATT_BLK
    nq, nsub, nk = T // bq, bq // bk, T // bk

    def body(q_ref, k_ref, vt_ref, o_ref, lse_ref, acc_ref, m_ref, l_ref):
        qi = pl.program_id(1)
        key = lax.broadcasted_iota(jnp.int32, (bk, bq), 0)
        qry = lax.broadcasted_iota(jnp.int32, (bk, bq), 1)
        ones = jnp.ones((8, bk), BF16)
        acc_ref[...] = jnp.zeros_like(acc_ref)
        m_ref[...] = jnp.full_like(m_ref, NEG)
        l_ref[...] = jnp.zeros_like(l_ref)

        def step(kb0, masked):
            kbs = [kb0 + d for d in range(nsub)]
            sts = [[_dot_nt(k_ref[pl.ds(pl.multiple_of(kb * bk, bk), bk), e * LANES:(e + 1) * LANES],
                            q_ref[:, e * LANES:(e + 1) * LANES]) for kb in kbs] for e in range(2)]
            pts, alphas = [], []
            for e in range(2):
                st = [jnp.where(key + d * bk <= qry, sts[e][d], NEG) for d in range(nsub)] if masked else sts[e]
                m_prev = m_ref[e, 0:1, :]
                m_new = m_prev
                for d in range(nsub):
                    m_new = jnp.maximum(m_new, jnp.max(st[d], axis=0, keepdims=True))
                alpha = jnp.exp2(m_prev - m_new)
                pt = [jnp.exp2(st[d] - m_new).astype(BF16) for d in range(nsub)]
                l_new = alpha * l_ref[e]
                for d in range(nsub):
                    l_new = l_new + _dot(ones, pt[d])
                l_ref[e] = l_new
                m_ref[e] = jnp.broadcast_to(m_new, (8, bq))
                pts.append(pt)
                alphas.append(alpha)
            for e in range(2):
                acc = alphas[e] * acc_ref[e]
                for d in range(nsub):
                    acc = acc + _dot(vt_ref[0, kbs[d], e * HEAD:(e + 1) * HEAD, :], pts[e][d])
                acc_ref[e] = acc

        step(qi * nsub, True)

        def loop(t, c):
            step(t * nsub, False)
            return c

        lax.fori_loop(0, qi, loop, 0)
        outs, lses = [], []
        for e in range(2):
            l = l_ref[e, 0:1, :]
            outs.append(acc_ref[e] / l)
            lses.append(jnp.broadcast_to(m_ref[e, 0:1, :] * LN2 + jnp.log(l), (HEAD, bq)))
        o_ref[...] = jnp.transpose(jnp.concatenate(outs, axis=0))
        lse_ref[...] = jnp.transpose(jnp.concatenate(lses, axis=0))

    return pl.pallas_call(
        body, name="mla_fwd", grid=(2, nq),
        in_specs=[pl.BlockSpec((bq, 256), lambda j, i: (i, j)),
                  pl.BlockSpec((T, 256), lambda j, i: (0, j)),
                  pl.BlockSpec((1, nk, LANES, bk), lambda j, i: (j, 0, 0, 0))],
        out_specs=[pl.BlockSpec((bq, LANES), lambda j, i: (i, j)),
                   pl.BlockSpec((bq, LANES), lambda j, i: (i, j))],
        out_shape=[jax.ShapeDtypeStruct((T, 256), F32), jax.ShapeDtypeStruct((T, 256), F32)],
        scratch_shapes=[pltpu.VMEM((2, HEAD, bq), F32), pltpu.VMEM((2, 8, bq), F32), pltpu.VMEM((2, 8, bq), F32)],
        compiler_params=_cparams(("parallel", "arbitrary")),
    )(qm, km, vt)


def _mla_bwd(qm, km, vm, y, lse, dy):
    T = qm.shape[0]
    bq, bk = min(MLA_BQ, T), ATT_BLK
    nq, nsub, nk = T // bq, bq // bk, T // bk

    def body(q_ref, k_ref, v_ref, y_ref, lse_ref, dy_ref, dq_ref, dkt_ref, dvt_ref, dob_ref, st_ref, qt_ref, dot_ref):
        qi = pl.program_id(1)

        @pl.when(qi == 0)
        def _():
            dkt_ref[...] = jnp.zeros_like(dkt_ref)
            dvt_ref[...] = jnp.zeros_like(dvt_ref)

        lane = lax.broadcasted_iota(jnp.int32, (1, LANES), 1) // HEAD
        row, col = _causal_masks(bq, bk)
        dq_ref[...] = jnp.zeros_like(dq_ref)
        lse = lse_ref[...]
        lse_other = pltpu.roll(lse, HEAD, 1)
        qt_ref[...] = jnp.transpose(q_ref[...].astype(F32)).astype(BF16)
        dot_ref[...] = jnp.transpose(dy_ref[...]).astype(BF16)
        for e in range(2):
            do = jnp.where(lane == e, dy_ref[...], 0.0)
            dob_ref[e] = do.astype(BF16)
            st_ref[2 * e] = jnp.where(lane == e, lse, lse_other) * LOG2E
            st_ref[2 * e + 1] = jnp.broadcast_to(jnp.sum(do * y_ref[...], axis=1, keepdims=True), (bq, LANES))

        hss = [slice(e * LANES, (e + 1) * LANES) for e in range(2)]
        tile = lambda a: jnp.concatenate([a] * (bk // LANES), axis=1)

        def step(kb0, masked):
            kbs = [kb0 + d for d in range(nsub)]
            rows = [pl.ds(pl.multiple_of(kb * bk, bk), bk) for kb in kbs]
            pairs = [(d, e) for d in range(nsub) for e in range(2)]
            ss = {(d, e): _dot_nt(q_ref[:, hss[e]], k_ref[rows[d], hss[e]]) for d, e in pairs}
            dps = {(d, e): _dot_nt(dob_ref[e], jnp.where(lane == e, v_ref[rows[d], :], 0)) for d, e in pairs}
            ps, dss = {}, {}
            for d, e in pairs:
                s = jnp.where(col + d * bk <= row, ss[d, e], NEG) if masked else ss[d, e]
                p = jnp.exp2(s - tile(st_ref[2 * e]))
                dss[d, e] = (p * (dps[d, e] - tile(st_ref[2 * e + 1]))).astype(BF16)
                ps[d, e] = p.astype(BF16)
            for d, e in pairs:
                dvt_ref[0, kbs[d], e * HEAD:(e + 1) * HEAD, :] += _dot(dot_ref[e * HEAD:(e + 1) * HEAD, :], ps[d, e])
            for d, e in pairs:
                dkt_ref[0, kbs[d], hss[e], :] += _dot(qt_ref[hss[e], :], dss[d, e])
            for e in range(2):
                dq = dq_ref[:, hss[e]]
                for d in range(nsub):
                    dq = dq + _dot(dss[d, e], k_ref[rows[d], hss[e]])
                dq_ref[:, hss[e]] = dq

        step(qi * nsub, True)

        def loop(t, c):
            step(t * nsub, False)
            return c

        lax.fori_loop(0, qi, loop, 0)
        dq_ref[...] *= LN2

    dqm, dkt, dvt = pl.pallas_call(
        body, name="mla_bwd", grid=(2, nq),
        in_specs=[pl.BlockSpec((bq, 256), lambda j, i: (i, j)),
                  pl.BlockSpec((T, 256), lambda j, i: (0, j)),
                  pl.BlockSpec((T, LANES), lambda j, i: (0, j)),
                  pl.BlockSpec((bq, LANES), lambda j, i: (i, j)),
                  pl.BlockSpec((bq, LANES), lambda j, i: (i, j)),
                  pl.BlockSpec((bq, LANES), lambda j, i: (i, j))],
        out_specs=[pl.BlockSpec((bq, 256), lambda j, i: (i, j)),
                   pl.BlockSpec((1, nk, 256, bk), lambda j, i: (j, 0, 0, 0)),
                   pl.BlockSpec((1, nk, LANES, bk), lambda j, i: (j, 0, 0, 0))],
        out_shape=[jax.ShapeDtypeStruct((T, 512), F32),
                   jax.ShapeDtypeStruct((2, nk, 256, bk), F32),
                   jax.ShapeDtypeStruct((2, nk, LANES, bk), F32)],
        scratch_shapes=[pltpu.VMEM((2, bq, LANES), BF16), pltpu.VMEM((4, bq, LANES), F32),
                        pltpu.VMEM((256, bq), BF16), pltpu.VMEM((LANES, bq), BF16)],
        compiler_params=_cparams(("parallel", "arbitrary")),
    )(qm, km, vm, y, lse, dy)
    return (dqm, (jnp.transpose(dkt, (1, 3, 0, 2)) * LN2).reshape(T, 512),
            jnp.transpose(dvt, (1, 3, 0, 2)).reshape(T, 256))


def _suffix_ones(n):
    r = lax.broadcasted_iota(jnp.int32, (n, n), 0)
    c = lax.broadcasted_iota(jnp.int32, (n, n), 1)
    return (r >= c).astype(BF16)


def _prefix_ones(n):
    r = lax.broadcasted_iota(jnp.int32, (n, n), 0)
    c = lax.broadcasted_iota(jnp.int32, (n, n), 1)
    return (r <= c).astype(BF16)


def _tri_sum(x, u):
    hi, lo = _split(x)
    return _dot(hi, u) + _dot(lo, u)


def _sb_specs(T, bq):
    qo, ko, vo = (_INT_OFF[n] // LANES for n in ("d_q", "d_k", "d_v"))
    return [pl.BlockSpec((bq, LANES), lambda j, i: (i, qo + j)),
            pl.BlockSpec((T, LANES), lambda j, i: (0, ko + j)),
            pl.BlockSpec((T, LANES), lambda j, i: (0, vo + j))]


def _sb_fwd(hb):
    T = hb.shape[0]
    bq = bk = ATT_BLK
    nq = T // bq

    def body(q_ref, k_ref, v_ref, o_ref, tot_ref, cnt_ref, qm_ref, car_ref):
        qi = pl.program_id(1)
        lane = lax.broadcasted_iota(jnp.int32, (1, LANES), 1) // HEAD
        row, col = _causal_masks(bq, bk)
        strict = col < row
        u = _suffix_ones(bk)
        o_ref[...] = jnp.zeros_like(o_ref)
        car_ref[...] = jnp.zeros_like(car_ref)
        for e in range(2):
            qm_ref[e] = jnp.where(lane == e, q_ref[...], 0) * 0.125

        def step(blocks):
            tile = lambda a: jnp.concatenate([a] * (bk // LANES), axis=1)
            rows = [pl.ds(pl.multiple_of(kb * bk, bk), bk) for kb, _ in blocks]
            pairs = [(b, e) for b in range(len(blocks)) for e in range(2)]
            zs = {(b, e): _dot_nt(qm_ref[e], k_ref[rows[b], :]) for b, e in pairs}
            splits = {}
            for b, e in pairs:
                z = zs[b, e]
                lk = jnp.minimum(-z, 0.0) - jnp.log(1.0 + jnp.exp(-jnp.abs(z)))
                if blocks[b][1] is not None:
                    lk = jnp.where(blocks[b][1], lk, 0.0)
                splits[b, e] = _split(lk)
            sufs = {be: _dot(hi, u) + _dot(lo, u) for be, (hi, lo) in splits.items()}
            car = [car_ref[0], car_ref[1]]
            aas = {}
            for b, e in pairs:
                a = jnp.exp(zs[b, e] + sufs[b, e] + tile(car[e]))
                if blocks[b][1] is not None:
                    a = jnp.where(blocks[b][1], a, 0.0)
                aas[b, e] = a.astype(BF16)
                car[e] = car[e] + jnp.broadcast_to(sufs[b, e][:, 0:1], (bq, LANES))
            acc = o_ref[...]
            for b, e in pairs:
                acc = acc + _dot(aas[b, e], jnp.where(lane == e, v_ref[rows[b], :], 0))
            o_ref[...] = acc
            car_ref[0], car_ref[1] = car

        step([(qi, strict), (jnp.maximum(qi - 1, 0), qi > 0)])

        def live():
            return jnp.max(jnp.maximum(car_ref[0], car_ref[1])) >= SB_DEAD

        def cond(c):
            return jnp.logical_and(c[0] < qi, c[1])

        def loop(c):
            step([(qi - 1 - c[0], None)])
            return c[0] + 1, live()

        done, _ = lax.while_loop(cond, loop, (jnp.minimum(qi, 1), live()))
        tot_ref[...] = jnp.where(lane == 0, car_ref[0], car_ref[1])
        cnt_ref[pl.program_id(0), qi] = done.astype(F32)

    return pl.pallas_call(
        body, name="sb_fwd", grid=(2, nq), in_specs=_sb_specs(T, bq),
        out_specs=[pl.BlockSpec((bq, LANES), lambda j, i: (i, j)), pl.BlockSpec((bq, LANES), lambda j, i: (i, j)),
                   pl.BlockSpec(memory_space=pltpu.SMEM)],
        out_shape=[jax.ShapeDtypeStruct((T, 256), F32), jax.ShapeDtypeStruct((T, 256), F32),
                   jax.ShapeDtypeStruct((2, nq), F32)],
        scratch_shapes=[pltpu.VMEM((2, bq, LANES), BF16), pltpu.VMEM((2, bq, LANES), F32)],
        compiler_params=_cparams(("parallel", "arbitrary")),
    )(hb, hb, hb)


def _sb_bwd(hb, tot, cnt, dy):
    T = hb.shape[0]
    bq = bk = ATT_BLK
    nq = T // bq

    def body(q_ref, k_ref, v_ref, tot_ref, dy_ref, cnt_ref, dq_ref, dk_ref, dv_ref, qm_ref, dob_ref, dqa_ref, rem_ref,
             cg_ref):
        qi = pl.program_id(1)

        @pl.when(qi == 0)
        def _():
            dk_ref[...] = jnp.zeros_like(dk_ref)
            dv_ref[...] = jnp.zeros_like(dv_ref)

        lane = lax.broadcasted_iota(jnp.int32, (1, LANES), 1) // HEAD
        row, col = _causal_masks(bq, bk)
        strict = col < row
        u = _prefix_ones(bk)
        tot = tot_ref[...]
        tot_other = pltpu.roll(tot, HEAD, 1)
        dqa_ref[...] = jnp.zeros_like(dqa_ref)
        cg_ref[...] = jnp.zeros_like(cg_ref)
        for e in range(2):
            qm_ref[e] = jnp.where(lane == e, q_ref[...], 0) * 0.125
            dob_ref[e] = jnp.where(lane == e, dy_ref[...], 0.0).astype(BF16)
            rem_ref[e] = jnp.where(lane == e, tot, tot_other)

        def step(blocks):
            tile = lambda a: jnp.concatenate([a] * (bk // LANES), axis=1)
            nb = len(blocks)
            rows = [pl.ds(pl.multiple_of(kb * bk, bk), bk) for kb, _ in blocks]
            pairs = [(b, e) for b in range(nb) for e in range(2)]
            mask = lambda b, x: x if blocks[b][1] is None else jnp.where(blocks[b][1], x, 0.0)
            zs = {(b, e): _dot_nt(qm_ref[e], k_ref[rows[b], :]) for b, e in pairs}
            das = {(b, e): _dot_nt(dob_ref[e], jnp.where(lane == e, v_ref[rows[b], :], 0)) for b, e in pairs}
            zls, splits = {}, {}
            for b, e in pairs:
                z = zs[b, e]
                lk = mask(b, jnp.minimum(-z, 0.0) - jnp.log(1.0 + jnp.exp(-jnp.abs(z))))
                zls[b, e] = z + lk
                splits[b, e] = _split(lk)
            pres = {be: _dot(hi, u) + _dot(lo, u) for be, (hi, lo) in splits.items()}
            rem = [rem_ref[0], rem_ref[1]]
            aas, gs, gsplits = {}, {}, {}
            for b, e in pairs:
                a = mask(b, jnp.exp(zls[b, e] + (tile(rem[e]) - pres[b, e])))
                gs[b, e] = a * das[b, e]
                aas[b, e] = a.astype(BF16)
                gsplits[b, e] = _split(gs[b, e])
                rem[e] = rem[e] - jnp.broadcast_to(pres[b, e][:, bk - 1:bk], (bq, LANES))
            for b in range(nb):
                dv_ref[rows[b], :] += _dot_tn(aas[b, 0], dob_ref[0]) + _dot_tn(aas[b, 1], dob_ref[1])
            gpres = {be: _dot(hi, u) + _dot(lo, u) for be, (hi, lo) in gsplits.items()}
            cg = [cg_ref[0], cg_ref[1]]
            dzs = {}
            for b, e in pairs:
                dz = mask(b, gs[b, e] - jnp.exp(zls[b, e]) * (tile(cg[e]) + gpres[b, e]))
                dzs[b, e] = dz.astype(BF16)
                cg[e] = cg[e] + jnp.broadcast_to(gpres[b, e][:, bk - 1:bk], (bq, LANES))
            for b in range(nb):
                dk_ref[rows[b], :] += _dot_tn(dzs[b, 0], qm_ref[0]) + _dot_tn(dzs[b, 1], qm_ref[1])
            for e in range(2):
                dq = dqa_ref[e]
                for b in range(nb):
                    dq = dq + _dot(dzs[b, e], k_ref[rows[b], :])
                dqa_ref[e] = dq
            rem_ref[0], rem_ref[1] = rem
            cg_ref[0], cg_ref[1] = cg

        def loop(kb, c):
            step([(kb, None)])
            return c

        start = qi - jnp.clip(cnt_ref[pl.program_id(0), qi].astype(jnp.int32), 0, qi)
        lax.fori_loop(start, qi - 1, loop, 0)
        step([(jnp.maximum(qi - 1, 0), qi > 0), (qi, strict)])
        dq_ref[...] = jnp.where(lane == 0, dqa_ref[0], dqa_ref[1]) * 0.125

    return pl.pallas_call(
        body, name="sb_bwd", grid=(2, nq),
        in_specs=_sb_specs(T, bq) + [pl.BlockSpec((bq, LANES), lambda j, i: (i, j)),
                                     pl.BlockSpec((bq, LANES), lambda j, i: (i, j)),
                                     pl.BlockSpec(memory_space=pltpu.SMEM)],
        out_specs=[pl.BlockSpec((bq, LANES), lambda j, i: (i, j)),
                   pl.BlockSpec((T, LANES), lambda j, i: (0, j)),
                   pl.BlockSpec((T, LANES), lambda j, i: (0, j))],
        out_shape=[jax.ShapeDtypeStruct((T, 256), F32)] * 3,
        scratch_shapes=[pltpu.VMEM((2, bq, LANES), BF16), pltpu.VMEM((2, bq, LANES), BF16),
                        pltpu.VMEM((2, bq, LANES), F32), pltpu.VMEM((2, bq, LANES), F32),
                        pltpu.VMEM((2, bq, LANES), F32)],
        compiler_params=_cparams(("parallel", "arbitrary")),
    )(hb, hb, hb, tot, dy, cnt)


EP_TM = 256


def _ep_in_specs(tm, rev):
    idx = (lambda i: rev - i) if rev is not None else (lambda i: i)
    bo = (_INT_OFF["b_b"] - N_HB) // 256
    halo = lambda i: jnp.maximum(idx(i) * (tm // 8) - 1, 0)
    return [pl.BlockSpec((tm, 256), lambda i: (idx(i), 0)),
            pl.BlockSpec((tm, 256), lambda i: (idx(i), 0)),
            pl.BlockSpec((tm, 256), lambda i: (idx(i), 0)),
            pl.BlockSpec((tm, D_MODEL), lambda i: (idx(i), 0)),
            pl.BlockSpec((tm, 256), lambda i: (idx(i), bo)),
            pl.BlockSpec((tm, 256), lambda i: (idx(i), bo + 1)),
            pl.BlockSpec((tm, 256), lambda i: (idx(i), bo + 2)),
            pl.BlockSpec((8, 256), lambda i: (halo(i), bo + 1)),
            pl.BlockSpec((8, 256), lambda i: (halo(i), bo + 2)),
            pl.BlockSpec((3, 256), lambda i: (0, 0)),
            pl.BlockSpec((1, 256), lambda i: (0, 0)),
            pl.BlockSpec((1, D_MODEL), lambda i: (0, 0)),
            pl.BlockSpec((D_MODEL, D_MODEL), lambda i: (0, 0)),
            pl.BlockSpec((1, D_MODEL), lambda i: (0, 0))]


def _ep_mix(first, ya_ref, yc_ref, yd_ref, gate_ref, bb_ref, bc_ref, bx_ref, hc_ref, hx_ref, cw_ref, cb_ref, gg_ref):
    tm = ya_ref.shape[0]
    u = bc_ref[...] * bx_ref[...]
    halo = jnp.where(first, 0.0, hc_ref[...] * hx_ref[...])
    row = lax.broadcasted_iota(jnp.int32, (tm, 1), 0)
    u1 = jnp.where(row == 0, halo[7:8, :], pltpu.roll(u, 1, 0))
    u2 = jnp.where(row == 0, halo[6:7, :], jnp.where(row == 1, halo[7:8, :], pltpu.roll(u, 2, 0)))
    cw = cw_ref[...]
    conv = cw[0:1, :] * u2 + cw[1:2, :] * u1 + cw[2:3, :] * u + cb_ref[...]
    bb = bb_ref[...]
    ys = [ya_ref[...], bb * conv, yc_ref[...], yd_ref[...]]
    rs = [_rms(y) for y in ys]
    gg = gg_ref[...]
    yhat = jnp.concatenate([y * r for y, r in zip(ys, rs)], axis=1)
    gate = gate_ref[...]
    sig = 1.0 / (1.0 + jnp.exp(-gate))
    return u, u1, u2, conv, bb, rs, yhat, yhat * gg, gate, sig


def _epilogue_fwd(x, ya, yc, yd, hf, conv_w, conv_b, g_grp, w_out, g_post):
    T = x.shape[0]
    tm = EP_TM

    def body(x_ref, ya_ref, yc_ref, yd_ref, gate_ref, bb_ref, bc_ref, bx_ref, hc_ref, hx_ref, cw_ref, cb_ref,
             gg_ref, wo_ref, gp_ref, o_ref):
        (_, _, _, _, _, _, _, yn, gate, sig) = _ep_mix(
            pl.program_id(0) == 0, ya_ref, yc_ref, yd_ref, gate_ref, bb_ref, bc_ref, bx_ref, hc_ref, hx_ref,
            cw_ref, cb_ref, gg_ref)
        z = _dot((yn * (gate * sig)).astype(BF16), wo_ref[...])
        o_ref[...] = x_ref[...] + z * _rms(z) * gp_ref[...]

    return pl.pallas_call(
        body, name="epilogue_fwd", grid=(T // tm,),
        in_specs=[pl.BlockSpec((tm, D_MODEL), lambda i: (i, 0))] + _ep_in_specs(tm, None),
        out_specs=pl.BlockSpec((tm, D_MODEL), lambda i: (i, 0)),
        out_shape=jax.ShapeDtypeStruct((T, D_MODEL), F32),
        compiler_params=_cparams(("parallel",)),
    )(x, ya, yc, yd, hf, hf, hf, hf, hf, hf, conv_w, conv_b, g_grp, w_out, g_post)


def _epilogue_bwd(dxn, ya, yc, yd, hf, conv_w, conv_b, g_grp, w_out, g_post):
    T = dxn.shape[0]
    tm = EP_TM
    nt = T // tm
    ridx = lambda i: (nt - 1 - i, 0)

    def body(dx_ref, ya_ref, yc_ref, yd_ref, gate_ref, bb_ref, bc_ref, bx_ref, hc_ref, hx_ref, cw_ref, cb_ref,
             gg_ref, wo_ref, gp_ref,
             dya_ref, dyc_ref, dyd_ref, dhf_ref, dwo_ref, dgp_ref, dgg_ref, dcw_ref, dcb_ref, carry_ref):
        i = pl.program_id(0)

        @pl.when(i == 0)
        def _():
            for r in (dwo_ref, dgp_ref, dgg_ref, dcw_ref, dcb_ref, carry_ref):
                r[...] = jnp.zeros_like(r)

        (u, u1, u2, conv, bb, rs, yhat, yn, gate, sig) = _ep_mix(
            i == nt - 1, ya_ref, yc_ref, yd_ref, gate_ref, bb_ref, bc_ref, bx_ref, hc_ref, hx_ref,
            cw_ref, cb_ref, gg_ref)
        silu = gate * sig
        ymix = (yn * silu).astype(BF16)
        z = _dot(ymix, wo_ref[...])
        rz = _rms(z)
        dz, dgrow = _rms_bwd(dx_ref[...], z * rz, rz, gp_ref[...])
        dgp_ref[...] += _colsum(dgrow)
        dzb = dz.astype(BF16)
        dwo_ref[...] += _dot_tn(ymix, dzb)
        dymix = _dot_nt(dzb, wo_ref[...])
        dhf_ref[:, 0:D_MODEL] = dymix * yn * (sig * (1.0 + gate * (1.0 - sig)))
        dyn = dymix * silu
        dgg_ref[...] += _colsum(dyn * yhat)
        gg = gg_ref[...]
        dys = []
        for gi in range(4):
            sl = slice(gi * GROUP, (gi + 1) * GROUP)
            dyh = dyn[:, sl] * gg[:, sl]
            yh = yhat[:, sl]
            dys.append(rs[gi] * (dyh - yh * jnp.mean(dyh * yh, axis=-1, keepdims=True)))
        dya_ref[...] = dys[0]
        dyc_ref[...] = dys[2]
        dyd_ref[...] = dys[3]
        dyb = dys[1]
        dhf_ref[:, D_MODEL:D_MODEL + 256] = dyb * conv
        dconv = dyb * bb
        dcb_ref[...] += _colsum(dconv)
        dcw_ref[0:1, :] += _colsum(dconv * u2)
        dcw_ref[1:2, :] += _colsum(dconv * u1)
        dcw_ref[2:3, :] += _colsum(dconv * u)
        carry = carry_ref[...]
        row = lax.broadcasted_iota(jnp.int32, (tm, 1), 0)
        d1 = jnp.where(row == tm - 1, carry[0:1, :], pltpu.roll(dconv, tm - 1, 0))
        d2 = jnp.where(row == tm - 2, carry[0:1, :],
                       jnp.where(row == tm - 1, carry[1:2, :], pltpu.roll(dconv, tm - 2, 0)))
        cw = cw_ref[...]
        du = cw[2:3, :] * dconv + cw[1:2, :] * d1 + cw[0:1, :] * d2
        dhf_ref[:, D_MODEL + 256:D_MODEL + 512] = du * bx_ref[...]
        dhf_ref[:, D_MODEL + 512:D_MODEL + 768] = du * bc_ref[...]
        carry_ref[...] = dconv[0:8, :]

    in_specs = [pl.BlockSpec((tm, D_MODEL), ridx)] + _ep_in_specs(tm, nt - 1)
    return pl.pallas_call(
        body, name="epilogue_bwd", grid=(nt,), in_specs=in_specs,
        out_specs=[pl.BlockSpec((tm, 256), ridx), pl.BlockSpec((tm, 256), ridx), pl.BlockSpec((tm, 256), ridx),
                   pl.BlockSpec((tm, D_MODEL + 768), ridx),
                   pl.BlockSpec((D_MODEL, D_MODEL), lambda i: (0, 0)),
                   pl.BlockSpec((1, D_MODEL), lambda i: (0, 0)),
                   pl.BlockSpec((1, D_MODEL), lambda i: (0, 0)),
                   pl.BlockSpec((8, 256), lambda i: (0, 0)),
                   pl.BlockSpec((1, 256), lambda i: (0, 0))],
        out_shape=[jax.ShapeDtypeStruct((T, 256), F32)] * 3
                  + [jax.ShapeDtypeStruct((T, D_MODEL + 768), F32),
                     jax.ShapeDtypeStruct((D_MODEL, D_MODEL), F32),
                     jax.ShapeDtypeStruct((1, D_MODEL), F32),
                     jax.ShapeDtypeStruct((1, D_MODEL), F32),
                     jax.ShapeDtypeStruct((8, 256), F32),
                     jax.ShapeDtypeStruct((1, 256), F32)],
        scratch_shapes=[pltpu.VMEM((8, 256), F32)],
        compiler_params=_cparams(("arbitrary",)),
    )(dxn, ya, yc, yd, hf, hf, hf, hf, hf, hf, conv_w, conv_b, g_grp, w_out, g_post)


def _loss_head(y, tgt):
    T = y.shape[0]
    tm = 512

    def body(y_ref, t_ref, dy_ref, l_ref):
        @pl.when(pl.program_id(0) == 0)
        def _():
            l_ref[...] = jnp.zeros_like(l_ref)

        d = y_ref[...] - t_ref[...]
        dy_ref[...] = d * (1.0 / D_MODEL)
        part = jnp.sum(jnp.sum(d * d, axis=1, keepdims=True), axis=0, keepdims=True)
        l_ref[...] += part * (0.5 / D_MODEL)

    return pl.pallas_call(
        body, name="loss_head", grid=(T // tm,),
        in_specs=[pl.BlockSpec((tm, D_MODEL), lambda i: (i, 0))] * 2,
        out_specs=[pl.BlockSpec((tm, D_MODEL), lambda i: (i, 0)), pl.BlockSpec((8, LANES), lambda i: (0, 0))],
        out_shape=[jax.ShapeDtypeStruct((T, D_MODEL), F32), jax.ShapeDtypeStruct((8, LANES), F32)],
        compiler_params=_cparams(("arbitrary",)),
    )(y, tgt)


def _place():
    return lax.axis_index("x"), lax.axis_index("y"), lax.axis_index("c")


def _other_chips(x, y):
    return [(1 - x, y), (x, 1 - y), (1 - x, 1 - y)]


HBM = pl.BlockSpec(memory_space=pl.ANY)


def _gather_weights(shards):
    n = len(shards)

    def body(*refs):
        ins, outs = refs[:n], refs[n:2 * n]
        ici_send, ici_recv, d2d_send, d2d_recv, local_sems = refs[2 * n:]
        x, y, c = _place()
        me = 2 * x + y
        chips = _other_chips(x, y)

        def ici(a, j, layer_from):
            px, py = chips[j]
            return pltpu.make_async_remote_copy(
                src_ref=ins[a].at[c], dst_ref=outs[a].at[layer_from, c], send_sem=ici_send.at[3 * a + j],
                recv_sem=ici_recv.at[3 * a + j], device_id=(px, py, c), device_id_type=MESH)

        def d2d(a, j, layer):
            px, py = chips[j]
            blk = outs[a].at[2 * px + py, layer]
            return pltpu.make_async_remote_copy(
                src_ref=blk, dst_ref=blk, send_sem=d2d_send.at[3 * a + j], recv_sem=d2d_recv.at[3 * a + j],
                device_id=(x, y, 1 - c), device_id_type=MESH)

        local = [pltpu.make_async_copy(ins[a], outs[a].at[me], local_sems.at[a]) for a in range(n)]
        for cp in local:
            cp.start()
        sends = [ici(a, j, me) for j in range(3) for a in range(n)]
        for cp in sends:
            cp.start()
        for j in range(3):
            px, py = chips[j]
            for a in range(n):
                ici(a, j, 2 * px + py).wait_recv()
                fwd = d2d(a, j, c)
                fwd.start()
                sends.append(fwd)
        for j in range(3):
            for a in range(n):
                d2d(a, j, 1 - c).wait_recv()
        for cp in sends:
            cp.wait_send()
        for cp in local:
            cp.wait()

    return pl.pallas_call(
        body, name="gather_weights",
        in_specs=[HBM] * n, out_specs=[HBM] * n,
        out_shape=[jax.ShapeDtypeStruct((4,) + s.shape, s.dtype) for s in shards],
        scratch_shapes=[pltpu.SemaphoreType.DMA((3 * n,))] * 4 + [pltpu.SemaphoreType.DMA((n,))],
    )(*shards)


def _exchange_chips(parts, small):
    n = len(parts)

    def body(*refs):
        ins, sm_ref = refs[:n], refs[n]
        outs, osm_ref = refs[n + 1:2 * n + 1], refs[2 * n + 1]
        send_sems, recv_sems, ssend_sems, srecv_sems, local_sems = refs[2 * n + 2:]
        x, y, c = _place()
        me = 2 * x + y
        dev = 4 * x + 2 * y + c
        local = [pltpu.make_async_copy(ins[a].at[me], outs[a].at[me], local_sems.at[a]) for a in range(n)]
        local.append(pltpu.make_async_copy(sm_ref, osm_ref.at[dev], local_sems.at[n]))
        for cp in local:
            cp.start()
        sends = []
        for j, (px, py) in enumerate(_other_chips(x, y)):
            for a in range(n):
                cp = pltpu.make_async_remote_copy(
                    src_ref=ins[a].at[2 * px + py], dst_ref=outs[a].at[me], send_sem=send_sems.at[3 * a + j],
                    recv_sem=recv_sems.at[3 * a + j], device_id=(px, py, c), device_id_type=MESH)
                cp.start()
                sends.append(cp)
        flips = [(fx, fy, fc) for fx in (0, 1) for fy in (0, 1) for fc in (0, 1)][1:]
        for j, (fx, fy, fc) in enumerate(flips):
            cp = pltpu.make_async_remote_copy(
                src_ref=sm_ref, dst_ref=osm_ref.at[dev], send_sem=ssend_sems.at[j], recv_sem=srecv_sems.at[j],
                device_id=(x ^ fx, y ^ fy, c ^ fc), device_id_type=MESH)
            cp.start()
            sends.append(cp)
        for j, (px, py) in enumerate(_other_chips(x, y)):
            for a in range(n):
                pltpu.make_async_remote_copy(
                    src_ref=ins[a].at[me], dst_ref=outs[a].at[2 * px + py], send_sem=send_sems.at[3 * a + j],
                    recv_sem=recv_sems.at[3 * a + j], device_id=(px, py, c), device_id_type=MESH).wait_recv()
        for j, (fx, fy, fc) in enumerate(flips):
            src = 4 * (x ^ fx) + 2 * (y ^ fy) + (c ^ fc)
            pltpu.make_async_remote_copy(
                src_ref=sm_ref, dst_ref=osm_ref.at[src], send_sem=ssend_sems.at[j], recv_sem=srecv_sems.at[j],
                device_id=(x ^ fx, y ^ fy, c ^ fc), device_id_type=MESH).wait_recv()
        for cp in sends:
            cp.wait_send()
        for cp in local:
            cp.wait()

    return pl.pallas_call(
        body, name="exchange_chips",
        in_specs=[HBM] * (n + 1), out_specs=[HBM] * (n + 1),
        out_shape=[jax.ShapeDtypeStruct(p.shape, p.dtype) for p in parts]
                  + [jax.ShapeDtypeStruct((8,) + small.shape, small.dtype)],
        scratch_shapes=[pltpu.SemaphoreType.DMA((3 * n,)), pltpu.SemaphoreType.DMA((3 * n,)),
                        pltpu.SemaphoreType.DMA((7,)), pltpu.SemaphoreType.DMA((7,)),
                        pltpu.SemaphoreType.DMA((n + 1,))],
    )(*parts, small)


def _swap_cores(parts, name):
    n = len(parts)

    def body(*refs):
        ins, outs, send_sems, recv_sems = refs[:n], refs[n:2 * n], refs[2 * n], refs[2 * n + 1]
        x, y, c = _place()
        copies = [pltpu.make_async_remote_copy(
            src_ref=ins[a], dst_ref=outs[a], send_sem=send_sems.at[a], recv_sem=recv_sems.at[a],
            device_id=(x, y, 1 - c), device_id_type=MESH) for a in range(n)]
        for cp in copies:
            cp.start()
        for cp in copies:
            cp.wait()

    return pl.pallas_call(
        body, name=name, in_specs=[HBM] * n, out_specs=[HBM] * n,
        out_shape=[jax.ShapeDtypeStruct(p.shape, p.dtype) for p in parts],
        scratch_shapes=[pltpu.SemaphoreType.DMA((n,)), pltpu.SemaphoreType.DMA((n,))],
    )(*parts)


def _row_block(rows):
    for cand in (256, 128, 64, 32, 16, 8):
        if rows % cand == 0:
            return cand
    return rows


def _add(a, b, name):
    L, R, C = a.shape
    tr = _row_block(R)

    def body(a_ref, b_ref, o_ref):
        o_ref[...] = a_ref[...] + b_ref[...]

    spec = pl.BlockSpec((1, tr, C), lambda l, i: (l, i, 0))
    return pl.pallas_call(
        body, name=name, grid=(L, R // tr), in_specs=[spec, spec], out_specs=spec,
        out_shape=jax.ShapeDtypeStruct((L, R, C), F32), compiler_params=_cparams(("parallel", "parallel")),
    )(a, b)


def _sum_leading(buf, name):
    n, R, C = buf.shape
    tr = _row_block(R)

    def body(b_ref, o_ref):
        acc = b_ref[0]
        for k in range(1, n):
            acc = acc + b_ref[k]
        o_ref[...] = acc

    return pl.pallas_call(
        body, name=name, grid=(R // tr,),
        in_specs=[pl.BlockSpec((n, tr, C), lambda i: (0, i, 0))],
        out_specs=pl.BlockSpec((tr, C), lambda i: (i, 0)),
        out_shape=jax.ShapeDtypeStruct((R, C), F32),
        compiler_params=_cparams(("parallel",)),
    )(buf)


def _adam_update(w, g, m, v):
    c1 = 1.0 / (1.0 - ADAM_B1 ** ADAM_STEP)
    c2 = 1.0 / (1.0 - ADAM_B2 ** ADAM_STEP)
    mn = ADAM_B1 * m + (1.0 - ADAM_B1) * g
    vn = ADAM_B2 * v + (1.0 - ADAM_B2) * (g * g)
    return -ADAM_LR * ((mn * c1) / (jnp.sqrt(vn * c2) + ADAM_EPS) + ADAM_WD * w), mn, vn


def _adamw_layers(w, m, v, g_mine, g_other, name):
    _, R, C = w.shape
    tr = _row_block(R)

    def body(w_ref, m_ref, v_ref, gm_ref, go_ref, g_ref, d_ref, mo_ref, vo_ref):
        g = jnp.where(pl.program_id(0) == lax.axis_index("c"), gm_ref[...], go_ref[...])
        g_ref[0] = g
        d_ref[0], mo_ref[0], vo_ref[0] = _adam_update(w_ref[0], g, m_ref[0], v_ref[0])

    spec3 = pl.BlockSpec((1, tr, C), lambda l, i: (l, i, 0))
    spec2 = pl.BlockSpec((tr, C), lambda l, i: (i, 0))
    return pl.pallas_call(
        body, name=name, grid=(2, R // tr),
        in_specs=[spec3] * 3 + [spec2] * 2, out_specs=[spec3] * 4,
        out_shape=[jax.ShapeDtypeStruct(w.shape, F32)] * 4,
        compiler_params=_cparams(("parallel", "parallel")),
    )(w, m, v, g_mine, g_other)


PACK_C = 1024
_BIG = ("w_in", "w_out", "mla_w_uq", "mla_w_ukv", "conv_w")
_SMALL = ("norm_pre", "group_norm", "norm_post", "conv_b", "mla_q_norm", "mla_kv_norm", "attn_sinks")
_SMALL_W = {"norm_pre": 1024, "group_norm": 1024, "norm_post": 1024, "conv_b": 256, "mla_q_norm": 256,
            "mla_kv_norm": 128, "attn_sinks": 4}


def _pack_small(d):
    flat = jnp.concatenate([d[n].reshape(-1) for n in _SMALL])
    return jnp.pad(flat, (0, 8 * PACK_C - flat.shape[0])).reshape(8, PACK_C)


def _adamw_small(w, m, v, got):
    ns = len(_SMALL)

    def body(*refs):
        got_ref = refs[3 * ns]
        outs = refs[3 * ns + 1:]
        gsum = got_ref[0]
        for d in range(1, 8):
            gsum = gsum + got_ref[d]
        off = 0
        for i, name in enumerate(_SMALL):
            wd = _SMALL_W[name]
            rows = []
            for l in range(DEPTH):
                r, c0 = divmod(off + l * wd, PACK_C)
                rows.append(gsum[r:r + 1, c0:c0 + wd])
            off += DEPTH * wd
            g = jnp.concatenate(rows, axis=0)
            delta, mn, vn = _adam_update(refs[i][...], g, refs[ns + i][...], refs[2 * ns + i][...])
            outs[i][...] = g
            outs[ns + i][...] = delta
            outs[2 * ns + i][...] = mn
            outs[3 * ns + i][...] = vn

    shapes = [jax.ShapeDtypeStruct(w[n].shape, F32) for n in _SMALL]
    res = pl.pallas_call(body, name="adamw_small", out_shape=shapes * 4)(
        *[w[n] for n in _SMALL], *[m[n] for n in _SMALL], *[v[n] for n in _SMALL], got)
    return [dict(zip(_SMALL, res[k * ns:(k + 1) * ns])) for k in range(4)]


def _w_in_internal(w):
    cols = []
    for n in _INT_ORDER:
        o, wd = _REAL_OFF[n]
        cols.append(w[:, o:o + wd])
        if _INT_W[n] != wd:
            cols.append(jnp.zeros((w.shape[0], _INT_W[n] - wd), w.dtype))
    return jnp.concatenate(cols, axis=1)


def _w_in_real(dw):
    return jnp.concatenate([dw[:, _INT_OFF[n]:_INT_OFF[n] + wd] for n, wd in _REAL], axis=1)


def _uq_internal(w):
    return jnp.pad(w.reshape(256, 4, 96), ((0, 0), (0, 0), (0, 32))).reshape(256, 512)


def _uq_real(dw):
    return dw.reshape(256, 4, 128)[:, :, :96].reshape(256, 384)


def _ukv_internal(w):
    w4 = w.reshape(128, 4, 128)
    k = jnp.pad(w4[:, :, :64], ((0, 0), (0, 0), (0, 64))).reshape(128, 512)
    return jnp.concatenate([k, w4[:, :, 64:].reshape(128, 256)], axis=1)


def _ukv_real(dw):
    k = dw[:, :512].reshape(128, 4, 128)[:, :, :64]
    v = dw[:, 512:].reshape(128, 4, 64)
    return jnp.concatenate([k, v], axis=2).reshape(128, 512)


def _layer_fwd(x, pos, p):
    xt, hb, hf = _inproj_fwd(x, p["norm_pre"], p["w_in"])
    ya = _swa_fwd(hb, p["attn_sinks"])
    qm, km, vm, vt = _mla_prep_fwd(hf, pos, p["mla_q_norm"], p["mla_kv_norm"], p["mla_w_uq"], p["mla_w_ukv"])
    yc, lse = _mla_fwd(qm, km, vt)
    yd, tot, cnt = _sb_fwd(hb)
    x_next = _epilogue_fwd(x, ya, yc, yd, hf, p["conv_w"], p["conv_b"], p["group_norm"], p["w_out"], p["norm_post"])
    return x_next, dict(x=x, xt=xt, hb=hb, hf=hf, ya=ya, yc=yc, yd=yd, tot=tot, cnt=cnt, qm=qm, km=km, vm=vm, lse=lse)


def _layer_bwd(dx_next, pos, p, s):
    (dya, dyc, dyd, dhf, dw_out, dg_post, dg_grp, dconv_w, dconv_b) = _epilogue_bwd(
        dx_next, s["ya"], s["yc"], s["yd"], s["hf"], p["conv_w"], p["conv_b"], p["group_norm"], p["w_out"],
        p["norm_post"])
    dq_d, dk_d, dv_d = _sb_bwd(s["hb"], s["tot"], s["cnt"], dyd)
    dqm, dkm, dvm = _mla_bwd(s["qm"], s["km"], s["vm"], s["yc"], s["lse"], dyc)
    dc, dw_uq, dw_ukv, dg_q, dg_kv = _mla_prep_bwd(
        s["hf"], pos, p["mla_q_norm"], p["mla_kv_norm"], p["mla_w_uq"], p["mla_w_ukv"], dqm, dkm, dvm)
    dq_a, dk_a, dv_a, dsinks = _swa_bwd(s["hb"], p["attn_sinks"], dya)
    dx, dh, dg_pre = _inproj_bwd_dx(s["x"], p["norm_pre"], p["w_in"], dx_next,
                                    [dq_a, dk_a, dv_a, dq_d, dk_d, dv_d, dhf, dc])
    dw_in = _matmul_over_tokens(s["xt"], dh, "inproj_bwd_dw")
    grads = dict(norm_pre=dg_pre[0], w_in=_w_in_real(dw_in), attn_sinks=dsinks[0, :4], conv_w=dconv_w[:3],
                 conv_b=dconv_b[0], mla_q_norm=dg_q[0], mla_w_uq=_uq_real(dw_uq), mla_kv_norm=dg_kv[0],
                 mla_w_ukv=_ukv_real(dw_ukv), group_norm=dg_grp[0], w_out=dw_out, norm_post=dg_post[0])
    return dx, grads


_WEIGHTS = ["norm_pre", "w_in", "attn_sinks", "conv_w", "conv_b", "mla_q_norm", "mla_w_uq", "mla_kv_norm",
            "mla_w_ukv", "group_norm", "w_out", "norm_post"]


def kernel(x, positions, norm_pre, w_in, attn_sinks, conv_w, conv_b, mla_q_norm, mla_w_uq, mla_kv_norm, mla_w_ukv, group_norm, w_out, norm_post, loss_target, m_norm_pre, m_w_in, m_attn_sinks, m_conv_w, m_conv_b, m_mla_q_norm, m_mla_w_uq, m_mla_kv_norm, m_mla_w_ukv, m_group_norm, m_w_out, m_norm_post, v_norm_pre, v_w_in, v_attn_sinks, v_conv_w, v_conv_b, v_mla_q_norm, v_mla_w_uq, v_mla_kv_norm, v_mla_w_ukv, v_group_norm, v_w_out, v_norm_post):
    w = dict(norm_pre=norm_pre, w_in=w_in, attn_sinks=attn_sinks, conv_w=conv_w, conv_b=conv_b,
             mla_q_norm=mla_q_norm, mla_w_uq=mla_w_uq, mla_kv_norm=mla_kv_norm, mla_w_ukv=mla_w_ukv,
             group_norm=group_norm, w_out=w_out, norm_post=norm_post)
    m = dict(norm_pre=m_norm_pre, w_in=m_w_in, attn_sinks=m_attn_sinks, conv_w=m_conv_w, conv_b=m_conv_b,
             mla_q_norm=m_mla_q_norm, mla_w_uq=m_mla_w_uq, mla_kv_norm=m_mla_kv_norm, mla_w_ukv=m_mla_w_ukv,
             group_norm=m_group_norm, w_out=m_w_out, norm_post=m_norm_post)
    v = dict(norm_pre=v_norm_pre, w_in=v_w_in, attn_sinks=v_attn_sinks, conv_w=v_conv_w, conv_b=v_conv_b,
             mla_q_norm=v_mla_q_norm, mla_w_uq=v_mla_w_uq, mla_kv_norm=v_mla_kv_norm, mla_w_ukv=v_mla_w_ukv,
             group_norm=v_group_norm, w_out=v_w_out, norm_post=v_norm_post)
    T = x.shape[1]
    xs = x[0]
    pos = positions[0].reshape(T, 1)
    tgt = loss_target[0]
    core = lax.axis_index("c")

    gathered = _gather_weights([w[n].astype(BF16) for n in _BIG[:4]] + [w["conv_w"]])
    full = {}
    for n, got in zip(_BIG, gathered):
        if n == "w_out":
            full[n] = jnp.moveaxis(got, 0, 1).reshape(DEPTH, D_MODEL, D_MODEL)
        else:
            full[n] = jnp.transpose(got, (1, 2, 0, 3)).reshape(DEPTH, got.shape[2], 4 * got.shape[3])

    layers = []
    for l in range(DEPTH):
        layers.append(dict(
            norm_pre=norm_pre[l:l + 1], w_in=_w_in_internal(full["w_in"][l]), attn_sinks=attn_sinks[l],
            conv_w=full["conv_w"][l], conv_b=conv_b[l:l + 1], mla_q_norm=mla_q_norm[l:l + 1],
            mla_w_uq=_uq_internal(full["mla_w_uq"][l]), mla_kv_norm=mla_kv_norm[l:l + 1],
            mla_w_ukv=_ukv_internal(full["mla_w_ukv"][l]), group_norm=group_norm[l:l + 1],
            w_out=full["w_out"][l], norm_post=norm_post[l:l + 1]))

    saved = []
    h = xs
    for l in range(DEPTH):
        h, s = _layer_fwd(h, pos, layers[l])
        saved.append(s)
    dy, loss_part = _loss_head(h, tgt)
    loss = lax.psum(loss_part[0, 0], ("x", "y", "c"))

    grads = [None] * DEPTH
    for l in reversed(range(DEPTH)):
        dy, grads[l] = _layer_bwd(dy, pos, layers[l], saved[l])

    def chunks(n, a):
        if n == "w_out":
            return a.reshape(4, D_MODEL // 4, D_MODEL)
        return jnp.transpose(a.reshape(a.shape[0], 4, a.shape[1] // 4), (1, 0, 2))

    mine = [chunks(n, jnp.where(core == 0, grads[0][n], grads[1][n])) for n in _BIG]
    theirs = [chunks(n, jnp.where(core == 0, grads[1][n], grads[0][n])) for n in _BIG]
    from_sibling = _swap_cores(theirs, "swap_layer_chunks")
    summed = [_add(a, b, "add_cores_" + n) for n, a, b in zip(_BIG, mine, from_sibling)]
    small = _pack_small({n: jnp.stack([grads[l][n] for l in range(DEPTH)]) for n in _SMALL})
    *got, got_small = _exchange_chips(summed, small)
    done = [_sum_leading(b, "sum_chips_" + n) for n, b in zip(_BIG, got)]
    done_other = _swap_cores(done, "swap_layer_shards")

    outs = _adamw_small(w, m, v, got_small)
    for n, gm, go in zip(_BIG, done, done_other):
        for d, a in zip(outs, _adamw_layers(w[n], m[n], v[n], gm, go, "adamw_" + n)):
            d[n] = a
    return (loss, dy[None], *[outs[0][n] for n in _WEIGHTS], *[outs[1][n] for n in _WEIGHTS],
            *[outs[2][n] for n in _WEIGHTS], *[outs[3][n] for n in _WEIGHTS])
```

```python
import math

import jax
import jax.numpy as jnp
from jax import lax
from jax.experimental import pallas as pl
from jax.experimental.pallas import tpu as pltpu

F32 = jnp.float32
BF16 = jnp.bfloat16
MESH = pl.DeviceIdType.MESH

D_MODEL = 1024
DEPTH = 2
EPS = 1e-6
BLOCK = 128
HEAD = 64
LANES = 128
GROUP = 256
LOG2E = 1.4426950408889634
LN2 = 0.6931471805599453
MLA_QSCALE = 96 ** -0.5 * LOG2E
ROPE_HALF = 16
ROPE_THETA = 10000.0
ATT_BLK = 256
MLA_BQ = 512
NEG = -1e30
SB_DEAD = -104.0

ADAM_LR, ADAM_B1, ADAM_B2, ADAM_EPS, ADAM_WD, ADAM_STEP = 0.001, 0.9, 0.999, 1e-08, 0.01, 10

_REAL = [("a_q", 256), ("a_k", 128), ("a_v", 128), ("b_b", 256), ("b_c", 256), ("b_x", 256),
         ("c_q", 256), ("c_kv", 128), ("c_kr", 32), ("d_q", 256), ("d_k", 256), ("d_v", 256),
         ("gate", 1024)]
_REAL_OFF = {}
_o = 0
for _n, _w in _REAL:
    _REAL_OFF[_n] = (_o, _w)
    _o += _w
D_IN = _o
_INT_ORDER = ["a_q", "a_k", "a_v", "d_q", "d_k", "d_v", "gate", "b_b", "b_c", "b_x", "c_q", "c_kv", "c_kr"]
_INT_W = dict(_REAL)
_INT_W["c_kr"] = 128
_INT_OFF = {}
_o = 0
for _n in _INT_ORDER:
    _INT_OFF[_n] = _o
    _o += _INT_W[_n]
N_INT = _o
N_HB = _INT_OFF["gate"]
N_HF = N_INT - N_HB

VMEM_LIMIT = 56 * 1024 * 1024


def _cparams(sem):
    return pltpu.CompilerParams(dimension_semantics=sem, vmem_limit_bytes=VMEM_LIMIT)


def _dot(a, b):
    return jnp.dot(a, b, preferred_element_type=F32)


def _dot_nt(a, b):
    return lax.dot_general(a, b, (((1,), (1,)), ((), ())), preferred_element_type=F32)


def _dot_tn(a, b):
    return lax.dot_general(a, b, (((0,), (0,)), ((), ())), preferred_element_type=F32)


def _split(x):
    hi = x.astype(BF16)
    lo = (x - hi.astype(F32)).astype(BF16)
    return hi, lo


def _rms(x):
    return lax.rsqrt(jnp.mean(x * x, axis=-1, keepdims=True) + EPS)


def _rms_bwd(dy, xhat, r, g):
    dxhat = dy * g
    return r * (dxhat - xhat * jnp.mean(dxhat * xhat, axis=-1, keepdims=True)), dy * xhat


def _colsum(x):
    return jnp.sum(x, axis=0, keepdims=True)


def _inproj_fwd(x, g, w):
    T = x.shape[0]
    tm = 256

    def body(x_ref, g_ref, w_ref, xt_ref, hb_ref, hf_ref):
        xv = x_ref[...]
        xn32 = xv * _rms(xv) * g_ref[...]
        xt_ref[...] = jnp.transpose(xn32).astype(BF16)
        h = _dot(xn32.astype(BF16), w_ref[...])
        hb_ref[...] = h[:, :N_HB].astype(BF16)
        hf_ref[...] = h[:, N_HB:]

    return pl.pallas_call(
        body, name="inproj_fwd", grid=(T // tm,),
        in_specs=[pl.BlockSpec((tm, D_MODEL), lambda i: (i, 0)),
                  pl.BlockSpec((1, D_MODEL), lambda i: (0, 0)),
                  pl.BlockSpec((D_MODEL, N_INT), lambda i: (0, 0))],
        out_specs=[pl.BlockSpec((D_MODEL, tm), lambda i: (0, i)),
                   pl.BlockSpec((tm, N_HB), lambda i: (i, 0)),
                   pl.BlockSpec((tm, N_HF), lambda i: (i, 0))],
        out_shape=[jax.ShapeDtypeStruct((D_MODEL, T), BF16),
                   jax.ShapeDtypeStruct((T, N_HB), BF16),
                   jax.ShapeDtypeStruct((T, N_HF), F32)],
        compiler_params=_cparams(("parallel",)),
    )(x, g, w)


def _inproj_bwd_dx(x, g, w, dx_next, pieces):
    T = x.shape[0]
    tm = 256
    widths = [p.shape[1] for p in pieces]
    assert sum(widths) == N_INT

    def body(x_ref, g_ref, w_ref, dxn_ref, *rest):
        p_refs = rest[:len(pieces)]
        dx_ref, dh_ref, dg_ref = rest[len(pieces):]
        dh = jnp.concatenate([p[...].astype(BF16) for p in p_refs], axis=1)
        dh_ref[...] = dh
        dxn = _dot_nt(dh, w_ref[...])
        xv = x_ref[...]
        r = _rms(xv)
        dx, dgrow = _rms_bwd(dxn, xv * r, r, g_ref[...])
        dx_ref[...] = dx + dxn_ref[...]

        @pl.when(pl.program_id(0) == 0)
        def _():
            dg_ref[...] = jnp.zeros_like(dg_ref)

        dg_ref[...] += _colsum(dgrow)

    return pl.pallas_call(
        body, name="inproj_bwd_dx", grid=(T // tm,),
        in_specs=[pl.BlockSpec((tm, D_MODEL), lambda i: (i, 0)),
                  pl.BlockSpec((1, D_MODEL), lambda i: (0, 0)),
                  pl.BlockSpec((D_MODEL, N_INT), lambda i: (0, 0)),
                  pl.BlockSpec((tm, D_MODEL), lambda i: (i, 0))]
                 + [pl.BlockSpec((tm, wd), lambda i: (i, 0)) for wd in widths],
        out_specs=[pl.BlockSpec((tm, D_MODEL), lambda i: (i, 0)),
                   pl.BlockSpec((tm, N_INT), lambda i: (i, 0)),
                   pl.BlockSpec((1, D_MODEL), lambda i: (0, 0))],
        out_shape=[jax.ShapeDtypeStruct((T, D_MODEL), F32),
                   jax.ShapeDtypeStruct((T, N_INT), BF16),
                   jax.ShapeDtypeStruct((1, D_MODEL), F32)],
        compiler_params=_cparams(("arbitrary",)),
    )(x, g, w, dx_next, *pieces)


def _matmul_over_tokens(at, b, name):
    M, T = at.shape
    N = b.shape[1]
    tm, tn = min(1024, T), 512

    def body(a_ref, b_ref, o_ref):
        @pl.when(pl.program_id(1) == 0)
        def _():
            o_ref[...] = jnp.zeros_like(o_ref)

        o_ref[...] += _dot(a_ref[...], b_ref[...])

    return pl.pallas_call(
        body, name=name, grid=(N // tn, T // tm),
        in_specs=[pl.BlockSpec((M, tm), lambda j, t: (0, t)),
                  pl.BlockSpec((tm, tn), lambda j, t: (t, j))],
        out_specs=pl.BlockSpec((M, tn), lambda j, t: (0, j)),
        out_shape=jax.ShapeDtypeStruct((M, N), F32),
        compiler_params=_cparams(("parallel", "arbitrary")),
    )(at, b)


def _roll_f32(x, shift):
    return pltpu.roll(x.astype(F32), shift, 1)


def _swa_operands(h, q_ref, kp_ref, kc_ref, vp_ref, vc_ref):
    p, e = h // 2, h % 2
    lane = lax.broadcasted_iota(jnp.int32, (1, LANES), 1) // HEAD
    q = q_ref[:, p * LANES:(p + 1) * LANES]
    k_prev, k_cur, v_prev, v_cur = kp_ref[...], kc_ref[...], vp_ref[...], vc_ref[...]
    if e != p:
        q = _roll_f32(q, HEAD).astype(BF16)
        v_prev = _roll_f32(v_prev, HEAD).astype(BF16)
        v_cur = _roll_f32(v_cur, HEAD).astype(BF16)
    qs = jnp.where(lane == p, q, 0) * 0.125
    return dict(p=p, e=e, lane=lane, qs=qs, k_prev=k_prev, k_cur=k_cur,
                v_prev=jnp.where(lane == e, v_prev, 0), v_cur=jnp.where(lane == e, v_cur, 0),
                s_prev=_dot_nt(qs, k_prev), s_cur=_dot_nt(qs, k_cur))


def _swa_probs(ops, sink, first):
    row = lax.broadcasted_iota(jnp.int32, (BLOCK, BLOCK), 0)
    col = lax.broadcasted_iota(jnp.int32, (BLOCK, BLOCK), 1)
    s_prev = jnp.where(jnp.logical_and(col > row, jnp.logical_not(first)), ops["s_prev"], NEG)
    s_cur = jnp.where(col <= row, ops["s_cur"], NEG)
    m = jnp.maximum(jnp.maximum(jnp.max(s_prev, axis=1, keepdims=True),
                                jnp.max(s_cur, axis=1, keepdims=True)), sink)
    p_prev = jnp.exp(s_prev - m)
    p_cur = jnp.exp(s_cur - m)
    p_sink = jnp.exp(sink - m)
    inv = 1.0 / (jnp.sum(p_prev, axis=1, keepdims=True) + jnp.sum(p_cur, axis=1, keepdims=True) + p_sink)
    return p_prev * inv, p_cur * inv, p_sink * inv


def _swa_specs(T):
    nb = T // BLOCK
    qo, ko, vo = (_INT_OFF[n] // LANES for n in ("a_q", "a_k", "a_v"))
    prev = lambda i: jnp.maximum(i - 1, 0)
    return [pl.BlockSpec((BLOCK, 256), lambda i: (i, qo // 2)),
            pl.BlockSpec((BLOCK, LANES), lambda i: (prev(i), ko)),
            pl.BlockSpec((BLOCK, LANES), lambda i: (i, ko)),
            pl.BlockSpec((BLOCK, LANES), lambda i: (prev(i), vo)),
            pl.BlockSpec((BLOCK, LANES), lambda i: (i, vo)),
            pl.BlockSpec(memory_space=pltpu.SMEM)], nb


def _swa_fwd(hb, sinks):
    T = hb.shape[0]
    specs, nb = _swa_specs(T)

    def body(q_ref, kp_ref, kc_ref, vp_ref, vc_ref, s_ref, o_ref):
        first = pl.program_id(0) == 0
        ops = [_swa_operands(h, q_ref, kp_ref, kc_ref, vp_ref, vc_ref) for h in range(4)]
        probs = [_swa_probs(ops[h], s_ref[h], first) for h in range(4)]
        outs = [_dot(probs[h][0].astype(BF16), ops[h]["v_prev"]) + _dot(probs[h][1].astype(BF16), ops[h]["v_cur"])
                for h in range(4)]
        for p in range(2):
            o_ref[:, p * LANES:(p + 1) * LANES] = outs[2 * p] + outs[2 * p + 1]

    return pl.pallas_call(
        body, name="swa_fwd", grid=(nb,), in_specs=specs,
        out_specs=pl.BlockSpec((BLOCK, 256), lambda i: (i, 0)),
        out_shape=jax.ShapeDtypeStruct((T, 256), F32),
        compiler_params=_cparams(("parallel",)),
    )(hb, hb, hb, hb, hb, sinks)


def _swa_bwd(hb, sinks, dy):
    T = hb.shape[0]
    specs, nb = _swa_specs(T)

    def body(q_ref, kp_ref, kc_ref, vp_ref, vc_ref, s_ref, dy_ref, dq_ref, dk_ref, dv_ref, ds_ref):
        i = pl.program_id(0)
        first = i == 0
        cur = pl.ds(pl.multiple_of(i * BLOCK, BLOCK), BLOCK)
        prv = pl.ds(pl.multiple_of(jnp.maximum(i - 1, 0) * BLOCK, BLOCK), BLOCK)

        @pl.when(first)
        def _():
            ds_ref[...] = jnp.zeros_like(ds_ref)

        dk_ref[cur, :] = jnp.zeros((BLOCK, LANES), F32)
        dv_ref[cur, :] = jnp.zeros((BLOCK, LANES), F32)
        lane_id = lax.broadcasted_iota(jnp.int32, (8, LANES), 1)
        heads = range(4)
        ops = [_swa_operands(h, q_ref, kp_ref, kc_ref, vp_ref, vc_ref) for h in heads]
        probs = [_swa_probs(ops[h], s_ref[h], first) for h in heads]
        dos = [jnp.where(ops[h]["lane"] == ops[h]["e"], dy_ref[:, ops[h]["p"] * LANES:(ops[h]["p"] + 1) * LANES], 0.0)
               for h in heads]
        dobs = [d.astype(BF16) for d in dos]
        pbs = [(probs[h][0].astype(BF16), probs[h][1].astype(BF16)) for h in heads]
        outs = [_dot(pbs[h][0], ops[h]["v_prev"]) + _dot(pbs[h][1], ops[h]["v_cur"]) for h in heads]
        dps = [(_dot_nt(dobs[h], ops[h]["v_prev"]), _dot_nt(dobs[h], ops[h]["v_cur"])) for h in heads]
        dss, dsinks = [], jnp.zeros((8, LANES), F32)
        for h in heads:
            delta = jnp.sum(dos[h] * outs[h], axis=1, keepdims=True)
            dss.append(((probs[h][0] * (dps[h][0] - delta)).astype(BF16),
                        (probs[h][1] * (dps[h][1] - delta)).astype(BF16)))
            dsink = -jnp.sum(probs[h][2] * delta, axis=0, keepdims=True)
            dsinks += jnp.where(lane_id == h, dsink, 0.0)
        ds_ref[...] += dsinks
        dqs = [(_dot(dss[h][0], ops[h]["k_prev"]) + _dot(dss[h][1], ops[h]["k_cur"])) * 0.125 for h in heads]
        dk_prev = dk_cur = dv_prev = dv_cur = jnp.zeros((BLOCK, LANES), F32)
        for h in heads:
            p, e = ops[h]["p"], ops[h]["e"]
            dob_v = dobs[h] if e == p else pltpu.roll(dos[h], HEAD, 1).astype(BF16)
            dk_prev += _dot_tn(dss[h][0], ops[h]["qs"])
            dk_cur += _dot_tn(dss[h][1], ops[h]["qs"])
            dv_prev += _dot_tn(pbs[h][0], dob_v)
            dv_cur += _dot_tn(pbs[h][1], dob_v)
        dk_ref[prv, :] += dk_prev
        dk_ref[cur, :] += dk_cur
        dv_ref[prv, :] += dv_prev
        dv_ref[cur, :] += dv_cur
        for p in range(2):
            dq_pair = jnp.zeros((BLOCK, LANES), F32)
            for e in range(2):
                dq = jnp.where(ops[2 * p + e]["lane"] == p, dqs[2 * p + e], 0.0)
                dq_pair += dq if e == p else pltpu.roll(dq, HEAD, 1)
            dq_ref[:, p * LANES:(p + 1) * LANES] = dq_pair

    return pl.pallas_call(
        body, name="swa_bwd", grid=(nb,),
        in_specs=specs + [pl.BlockSpec((BLOCK, 256), lambda i: (i, 0))],
        out_specs=[pl.BlockSpec((BLOCK, 256), lambda i: (i, 0)),
                   pl.BlockSpec((T, LANES), lambda i: (0, 0)),
                   pl.BlockSpec((T, LANES), lambda i: (0, 0)),
                   pl.BlockSpec((8, LANES), lambda i: (0, 0))],
        out_shape=[jax.ShapeDtypeStruct((T, 256), F32),
                   jax.ShapeDtypeStruct((T, LANES), F32),
                   jax.ShapeDtypeStruct((T, LANES), F32),
                   jax.ShapeDtypeStruct((8, LANES), F32)],
        compiler_params=_cparams(("arbitrary",)),
    )(hb, hb, hb, hb, hb, sinks, dy)


def _rope_tables(pos_ref):
    lane = lax.broadcasted_iota(jnp.int32, (1, LANES), 1)
    active = jnp.logical_and(lane >= HEAD, lane < HEAD + 2 * ROPE_HALF)
    idx = ((lane - HEAD) % ROPE_HALF).astype(F32)
    freq = jnp.exp(idx * (-math.log(ROPE_THETA) / ROPE_HALF))
    ang = pos_ref[...].astype(F32) * freq
    cos, sin = jnp.cos(ang), jnp.sin(ang)
    c = jnp.where(active, cos, 1.0)
    s_up = jnp.where(jnp.logical_and(active, lane >= HEAD + ROPE_HALF), sin, 0.0)
    s_dn = jnp.where(jnp.logical_and(active, lane < HEAD + ROPE_HALF), -sin, 0.0)
    return c, s_up, s_dn


def _rope(x, tabs):
    c, s_up, s_dn = tabs
    return x * c + pltpu.roll(x, ROPE_HALF, 1) * s_up + pltpu.roll(x, LANES - ROPE_HALF, 1) * s_dn


def _rope_t(dy, tabs):
    c, s_up, s_dn = tabs
    return dy * c + pltpu.roll(dy * s_up, LANES - ROPE_HALF, 1) + pltpu.roll(dy * s_dn, ROPE_HALF, 1)


def _mla_lat_specs(tm):
    cq, ckv, ckr = ((_INT_OFF[n] - N_HB) for n in ("c_q", "c_kv", "c_kr"))
    return [pl.BlockSpec((tm, 256), lambda i: (i, cq // 256)),
            pl.BlockSpec((tm, LANES), lambda i: (i, ckv // LANES)),
            pl.BlockSpec((tm, LANES), lambda i: (i, ckr // LANES)),
            pl.BlockSpec((tm, 1), lambda i: (i, 0)),
            pl.BlockSpec((1, 256), lambda i: (0, 0)),
            pl.BlockSpec((1, LANES), lambda i: (0, 0)),
            pl.BlockSpec((256, 512), lambda i: (0, 0)),
            pl.BlockSpec((LANES, 768), lambda i: (0, 0))]


def _mla_prep_fwd(hf, pos, g_q, g_kv, w_uq, w_ukv):
    T = hf.shape[0]
    tm = 512
    sub = tm // ATT_BLK

    def body(cq_ref, ckv_ref, ckr_ref, pos_ref, gq_ref, gkv_ref, wq_ref, wkv_ref, qm_ref, km_ref, vm_ref, vt_ref):
        tabs = _rope_tables(pos_ref)
        cq = cq_ref[...]
        q = _dot((cq * _rms(cq) * gq_ref[...]).astype(BF16), wq_ref[...])
        ckv = ckv_ref[...]
        kv = _dot((ckv * _rms(ckv) * gkv_ref[...]).astype(BF16), wkv_ref[...])
        kr = _rope(pltpu.roll(ckr_ref[...], HEAD, 1), tabs)
        for h in range(4):
            sl = slice(h * LANES, (h + 1) * LANES)
            qm_ref[:, sl] = (_rope(q[:, sl], tabs) * MLA_QSCALE).astype(BF16)
            km_ref[:, sl] = (kv[:, sl] + kr).astype(BF16)
        vm_ref[...] = kv[:, 512:].astype(BF16)
        for p in range(2):
            for s in range(sub):
                tile = kv[s * ATT_BLK:(s + 1) * ATT_BLK, 512 + p * LANES:512 + (p + 1) * LANES]
                vt_ref[p, s] = jnp.transpose(tile).astype(BF16)

    return pl.pallas_call(
        body, name="mla_prep_fwd", grid=(T // tm,), in_specs=_mla_lat_specs(tm),
        out_specs=[pl.BlockSpec((tm, 512), lambda i: (i, 0)),
                   pl.BlockSpec((tm, 512), lambda i: (i, 0)),
                   pl.BlockSpec((tm, 256), lambda i: (i, 0)),
                   pl.BlockSpec((2, sub, LANES, ATT_BLK), lambda i: (0, i, 0, 0))],
        out_shape=[jax.ShapeDtypeStruct((T, 512), BF16),
                   jax.ShapeDtypeStruct((T, 512), BF16),
                   jax.ShapeDtypeStruct((T, 256), BF16),
                   jax.ShapeDtypeStruct((2, T // ATT_BLK, LANES, ATT_BLK), BF16)],
        compiler_params=_cparams(("parallel",)),
    )(hf, hf, hf, pos, g_q, g_kv, w_uq, w_ukv)


def _mla_prep_bwd(hf, pos, g_q, g_kv, w_uq, w_ukv, dqm, dkm, dvm):
    T = hf.shape[0]
    tm = 512

    def body(cq_ref, ckv_ref, ckr_ref, pos_ref, gq_ref, gkv_ref, wq_ref, wkv_ref, dq_ref, dk_ref, dv_ref,
             dc_ref, dwq_ref, dwkv_ref, dgq_ref, dgkv_ref):
        @pl.when(pl.program_id(0) == 0)
        def _():
            dwq_ref[...] = jnp.zeros_like(dwq_ref)
            dwkv_ref[...] = jnp.zeros_like(dwkv_ref)
            dgq_ref[...] = jnp.zeros_like(dgq_ref)
            dgkv_ref[...] = jnp.zeros_like(dgkv_ref)

        tabs = _rope_tables(pos_ref)
        lane = lax.broadcasted_iota(jnp.int32, (1, LANES), 1)
        dq = jnp.concatenate([_rope_t(dq_ref[:, h * LANES:(h + 1) * LANES] * MLA_QSCALE, tabs)
                              for h in range(4)], axis=1).astype(BF16)
        cq = cq_ref[...]
        rq = _rms(cq)
        cqn = (cq * rq * gq_ref[...]).astype(BF16)
        dwq_ref[...] += _dot_tn(cqn, dq)
        dcq, dgrow = _rms_bwd(_dot_nt(dq, wq_ref[...]), cq * rq, rq, gq_ref[...])
        dgq_ref[...] += _colsum(dgrow)
        dc_ref[:, 0:256] = dcq

        dk = dk_ref[...]
        dkr = dk[:, 0:LANES] + dk[:, LANES:2 * LANES] + dk[:, 2 * LANES:3 * LANES] + dk[:, 3 * LANES:]
        dkr = pltpu.roll(_rope_t(dkr, tabs), HEAD, 1)
        dc_ref[:, 384:512] = jnp.where(lane < 2 * ROPE_HALF, dkr, 0.0)
        dkv = jnp.concatenate([dk.astype(BF16), dv_ref[...].astype(BF16)], axis=1)
        ckv = ckv_ref[...]
        rkv = _rms(ckv)
        ckvn = (ckv * rkv * gkv_ref[...]).astype(BF16)
        dwkv_ref[...] += _dot_tn(ckvn, dkv)
        dckv, dgrow = _rms_bwd(_dot_nt(dkv, wkv_ref[...]), ckv * rkv, rkv, gkv_ref[...])
        dgkv_ref[...] += _colsum(dgrow)
        dc_ref[:, 256:384] = dckv

    return pl.pallas_call(
        body, name="mla_prep_bwd", grid=(T // tm,),
        in_specs=_mla_lat_specs(tm) + [pl.BlockSpec((tm, 512), lambda i: (i, 0)),
                                       pl.BlockSpec((tm, 512), lambda i: (i, 0)),
                                       pl.BlockSpec((tm, 256), lambda i: (i, 0))],
        out_specs=[pl.BlockSpec((tm, 512), lambda i: (i, 0)),
                   pl.BlockSpec((256, 512), lambda i: (0, 0)),
                   pl.BlockSpec((LANES, 768), lambda i: (0, 0)),
                   pl.BlockSpec((1, 256), lambda i: (0, 0)),
                   pl.BlockSpec((1, LANES), lambda i: (0, 0))],
        out_shape=[jax.ShapeDtypeStruct((T, 512), F32),
                   jax.ShapeDtypeStruct((256, 512), F32),
                   jax.ShapeDtypeStruct((LANES, 768), F32),
                   jax.ShapeDtypeStruct((1, 256), F32),
                   jax.ShapeDtypeStruct((1, LANES), F32)],
        compiler_params=_cparams(("arbitrary",)),
    )(hf, hf, hf, pos, g_q, g_kv, w_uq, w_ukv, dqm, dkm, dvm)


def _causal_masks(bq, bk):
    row = lax.broadcasted_iota(jnp.int32, (bq, bk), 0)
    col = lax.broadcasted_iota(jnp.int32, (bq, bk), 1)
    return row, col


def _mla_fwd(qm, km, vt):
    T = qm.shape[0]
    bq, bk = min(MLA_BQ, T), ATT_BLK
    nq, nsub, nk = T // bq, bq // bk, T // bk

    def body(q_ref, k_ref, vt_ref, o_ref, lse_ref, acc_ref, m_ref, l_ref):
        qi = pl.program_id(1)
        key = lax.broadcasted_iota(jnp.int32, (bk, bq), 0)
        qry = lax.broadcasted_iota(jnp.int32, (bk, bq), 1)
        ones = jnp.ones((8, bk), BF16)
        acc_ref[...] = jnp.zeros_like(acc_ref)
        m_ref[...] = jnp.full_like(m_ref, NEG)
        l_ref[...] = jnp.zeros_like(l_ref)

        def step(kb0, masked):
            kbs = [kb0 + d for d in range(nsub)]
            sts = [[_dot_nt(k_ref[pl.ds(pl.multiple_of(kb * bk, bk), bk), e * LANES:(e + 1) * LANES],
                            q_ref[:, e * LANES:(e + 1) * LANES]) for kb in kbs] for e in range(2)]
            pts, alphas = [], []
            for e in range(2):
                st = [jnp.where(key + d * bk <= qry, sts[e][d], NEG) for d in range(nsub)] if masked else sts[e]
                m_prev = m_ref[e, 0:1, :]
                m_new = m_prev
                for d in range(nsub):
                    m_new = jnp.maximum(m_new, jnp.max(st[d], axis=0, keepdims=True))
                alpha = jnp.exp2(m_prev - m_new)
                pt = [jnp.exp2(st[d] - m_new).astype(BF16) for d in range(nsub)]
                l_new = alpha * l_ref[e]
                for d in range(nsub):
                    l_new = l_new + _dot(ones, pt[d])
                l_ref[e] = l_new
                m_ref[e] = jnp.broadcast_to(m_new, (8, bq))
                pts.append(pt)
                alphas.append(alpha)
            for e in range(2):
                acc = alphas[e] * acc_ref[e]
                for d in range(nsub):
                    acc = acc + _dot(vt_ref[0, kbs[d], e * HEAD:(e + 1) * HEAD, :], pts[e][d])
                acc_ref[e] = acc

        step(qi * nsub, True)

        def loop(t, c):
            step(t * nsub, False)
            return c

        lax.fori_loop(0, qi, loop, 0)
        outs, lses = [], []
        for e in range(2):
            l = l_ref[e, 0:1, :]
            outs.append(acc_ref[e] / l)
            lses.append(jnp.broadcast_to(m_ref[e, 0:1, :] * LN2 + jnp.log(l), (HEAD, bq)))
        o_ref[...] = jnp.transpose(jnp.concatenate(outs, axis=0))
        lse_ref[...] = jnp.transpose(jnp.concatenate(lses, axis=0))

    return pl.pallas_call(
        body, name="mla_fwd", grid=(2, nq),
        in_specs=[pl.BlockSpec((bq, 256), lambda j, i: (i, j)),
                  pl.BlockSpec((T, 256), lambda j, i: (0, j)),
                  pl.BlockSpec((1, nk, LANES, bk), lambda j, i: (j, 0, 0, 0))],
        out_specs=[pl.BlockSpec((bq, LANES), lambda j, i: (i, j)),
                   pl.BlockSpec((bq, LANES), lambda j, i: (i, j))],
        out_shape=[jax.ShapeDtypeStruct((T, 256), F32), jax.ShapeDtypeStruct((T, 256), F32)],
        scratch_shapes=[pltpu.VMEM((2, HEAD, bq), F32), pltpu.VMEM((2, 8, bq), F32), pltpu.VMEM((2, 8, bq), F32)],
        compiler_params=_cparams(("parallel", "arbitrary")),
    )(qm, km, vt)


def _mla_bwd(qm, km, vm, y, lse, dy):
    T = qm.shape[0]
    bq, bk = min(MLA_BQ, T), ATT_BLK
    nq, nsub, nk = T // bq, bq // bk, T // bk

    def body(q_ref, k_ref, v_ref, y_ref, lse_ref, dy_ref, dq_ref, dkt_ref, dvt_ref, dob_ref, st_ref, qt_ref, dot_ref):
        qi = pl.program_id(1)

        @pl.when(qi == 0)
        def _():
            dkt_ref[...] = jnp.zeros_like(dkt_ref)
            dvt_ref[...] = jnp.zeros_like(dvt_ref)

        lane = lax.broadcasted_iota(jnp.int32, (1, LANES), 1) // HEAD
        row, col = _causal_masks(bq, bk)
        dq_ref[...] = jnp.zeros_like(dq_ref)
        lse = lse_ref[...]
        lse_other = pltpu.roll(lse, HEAD, 1)
        qt_ref[...] = jnp.transpose(q_ref[...].astype(F32)).astype(BF16)
        dot_ref[...] = jnp.transpose(dy_ref[...]).astype(BF16)
        for e in range(2):
            do = jnp.where(lane == e, dy_ref[...], 0.0)
            dob_ref[e] = do.astype(BF16)
            st_ref[2 * e] = jnp.where(lane == e, lse, lse_other) * LOG2E
            st_ref[2 * e + 1] = jnp.broadcast_to(jnp.sum(do * y_ref[...], axis=1, keepdims=True), (bq, LANES))

        hss = [slice(e * LANES, (e + 1) * LANES) for e in range(2)]
        tile = lambda a: jnp.concatenate([a] * (bk // LANES), axis=1)

        def step(kb0, masked):
            kbs = [kb0 + d for d in range(nsub)]
            rows = [pl.ds(pl.multiple_of(kb * bk, bk), bk) for kb in kbs]
            pairs = [(d, e) for d in range(nsub) for e in range(2)]
            ss = {(d, e): _dot_nt(q_ref[:, hss[e]], k_ref[rows[d], hss[e]]) for d, e in pairs}
            dps = {(d, e): _dot_nt(dob_ref[e], jnp.where(lane == e, v_ref[rows[d], :], 0)) for d, e in pairs}
            ps, dss = {}, {}
            for d, e in pairs:
                s = jnp.where(col + d * bk <= row, ss[d, e], NEG) if masked else ss[d, e]
                p = jnp.exp2(s - tile(st_ref[2 * e]))
                dss[d, e] = (p * (dps[d, e] - tile(st_ref[2 * e + 1]))).astype(BF16)
                ps[d, e] = p.astype(BF16)
            for d, e in pairs:
                dvt_ref[0, kbs[d], e * HEAD:(e + 1) * HEAD, :] += _dot(dot_ref[e * HEAD:(e + 1) * HEAD, :], ps[d, e])
            for d, e in pairs:
                dkt_ref[0, kbs[d], hss[e], :] += _dot(qt_ref[hss[e], :], dss[d, e])
            for e in range(2):
                dq = dq_ref[:, hss[e]]
                for d in range(nsub):
                    dq = dq + _dot(dss[d, e], k_ref[rows[d], hss[e]])
                dq_ref[:, hss[e]] = dq

        step(qi * nsub, True)

        def loop(t, c):
            step(t * nsub, False)
            return c

        lax.fori_loop(0, qi, loop, 0)
        dq_ref[...] *= LN2

    dqm, dkt, dvt = pl.pallas_call(
        body, name="mla_bwd", grid=(2, nq),
        in_specs=[pl.BlockSpec((bq, 256), lambda j, i: (i, j)),
                  pl.BlockSpec((T, 256), lambda j, i: (0, j)),
                  pl.BlockSpec((T, LANES), lambda j, i: (0, j)),
                  pl.BlockSpec((bq, LANES), lambda j, i: (i, j)),
                  pl.BlockSpec((bq, LANES), lambda j, i: (i, j)),
                  pl.BlockSpec((bq, LANES), lambda j, i: (i, j))],
        out_specs=[pl.BlockSpec((bq, 256), lambda j, i: (i, j)),
                   pl.BlockSpec((1, nk, 256, bk), lambda j, i: (j, 0, 0, 0)),
                   pl.BlockSpec((1, nk, LANES, bk), lambda j, i: (j, 0, 0, 0))],
        out_shape=[jax.ShapeDtypeStruct((T, 512), F32),
                   jax.ShapeDtypeStruct((2, nk, 256, bk), F32),
                   jax.ShapeDtypeStruct((2, nk, LANES, bk), F32)],
        scratch_shapes=[pltpu.VMEM((2, bq, LANES), BF16), pltpu.VMEM((4, bq, LANES), F32),
                        pltpu.VMEM((256, bq), BF16), pltpu.VMEM((LANES, bq), BF16)],
        compiler_params=_cparams(("parallel", "arbitrary")),
    )(qm, km, vm, y, lse, dy)
    return (dqm, (jnp.transpose(dkt, (1, 3, 0, 2)) * LN2).reshape(T, 512),
            jnp.transpose(dvt, (1, 3, 0, 2)).reshape(T, 256))


def _suffix_ones(n):
    r = lax.broadcasted_iota(jnp.int32, (n, n), 0)
    c = lax.broadcasted_iota(jnp.int32, (n, n), 1)
    return (r >= c).astype(BF16)


def _prefix_ones(n):
    r = lax.broadcasted_iota(jnp.int32, (n, n), 0)
    c = lax.broadcasted_iota(jnp.int32, (n, n), 1)
    return (r <= c).astype(BF16)


def _tri_sum(x, u):
    hi, lo = _split(x)
    return _dot(hi, u) + _dot(lo, u)


def _sb_specs(T, bq):
    qo, ko, vo = (_INT_OFF[n] // LANES for n in ("d_q", "d_k", "d_v"))
    return [pl.BlockSpec((bq, LANES), lambda j, i: (i, qo + j)),
            pl.BlockSpec((T, LANES), lambda j, i: (0, ko + j)),
            pl.BlockSpec((T, LANES), lambda j, i: (0, vo + j))]


def _sb_fwd(hb):
    T = hb.shape[0]
    bq = bk = ATT_BLK
    nq = T // bq

    def body(q_ref, k_ref, v_ref, o_ref, tot_ref, cnt_ref, qm_ref, car_ref):
        qi = pl.program_id(1)
        lane = lax.broadcasted_iota(jnp.int32, (1, LANES), 1) // HEAD
        row, col = _causal_masks(bq, bk)
        strict = col < row
        u = _suffix_ones(bk)
        o_ref[...] = jnp.zeros_like(o_ref)
        car_ref[...] = jnp.zeros_like(car_ref)
        for e in range(2):
            qm_ref[e] = jnp.where(lane == e, q_ref[...], 0) * 0.125

        def step(blocks):
            tile = lambda a: jnp.concatenate([a] * (bk // LANES), axis=1)
            rows = [pl.ds(pl.multiple_of(kb * bk, bk), bk) for kb, _ in blocks]
            pairs = [(b, e) for b in range(len(blocks)) for e in range(2)]
            zs = {(b, e): _dot_nt(qm_ref[e], k_ref[rows[b], :]) for b, e in pairs}
            splits = {}
            for b, e in pairs:
                z = zs[b, e]
                lk = jnp.minimum(-z, 0.0) - jnp.log(1.0 + jnp.exp(-jnp.abs(z)))
                if blocks[b][1] is not None:
                    lk = jnp.where(blocks[b][1], lk, 0.0)
                splits[b, e] = _split(lk)
            sufs = {be: _dot(hi, u) + _dot(lo, u) for be, (hi, lo) in splits.items()}
            car = [car_ref[0], car_ref[1]]
            aas = {}
            for b, e in pairs:
                a = jnp.exp(zs[b, e] + sufs[b, e] + tile(car[e]))
                if blocks[b][1] is not None:
                    a = jnp.where(blocks[b][1], a, 0.0)
                aas[b, e] = a.astype(BF16)
                car[e] = car[e] + jnp.broadcast_to(sufs[b, e][:, 0:1], (bq, LANES))
            acc = o_ref[...]
            for b, e in pairs:
                acc = acc + _dot(aas[b, e], jnp.where(lane == e, v_ref[rows[b], :], 0))
            o_ref[...] = acc
            car_ref[0], car_ref[1] = car

        step([(qi, strict), (jnp.maximum(qi - 1, 0), qi > 0)])

        def live():
            return jnp.max(jnp.maximum(car_ref[0], car_ref[1])) >= SB_DEAD

        def cond(c):
            return jnp.logical_and(c[0] < qi, c[1])

        def loop(c):
            step([(qi - 1 - c[0], None)])
            return c[0] + 1, live()

        done, _ = lax.while_loop(cond, loop, (jnp.minimum(qi, 1), live()))
        tot_ref[...] = jnp.where(lane == 0, car_ref[0], car_ref[1])
        cnt_ref[pl.program_id(0), qi] = done.astype(F32)

    return pl.pallas_call(
        body, name="sb_fwd", grid=(2, nq), in_specs=_sb_specs(T, bq),
        out_specs=[pl.BlockSpec((bq, LANES), lambda j, i: (i, j)), pl.BlockSpec((bq, LANES), lambda j, i: (i, j)),
                   pl.BlockSpec(memory_space=pltpu.SMEM)],
        out_shape=[jax.ShapeDtypeStruct((T, 256), F32), jax.ShapeDtypeStruct((T, 256), F32),
                   jax.ShapeDtypeStruct((2, nq), F32)],
        scratch_shapes=[pltpu.VMEM((2, bq, LANES), BF16), pltpu.VMEM((2, bq, LANES), F32)],
        compiler_params=_cparams(("parallel", "arbitrary")),
    )(hb, hb, hb)


def _sb_bwd(hb, tot, cnt, dy):
    T = hb.shape[0]
    bq = bk = ATT_BLK
    nq = T // bq

    def body(q_ref, k_ref, v_ref, tot_ref, dy_ref, cnt_ref, dq_ref, dk_ref, dv_ref, qm_ref, dob_ref, dqa_ref, rem_ref,
             cg_ref):
        qi = pl.program_id(1)

        @pl.when(qi == 0)
        def _():
            dk_ref[...] = jnp.zeros_like(dk_ref)
            dv_ref[...] = jnp.zeros_like(dv_ref)

        lane = lax.broadcasted_iota(jnp.int32, (1, LANES), 1) // HEAD
        row, col = _causal_masks(bq, bk)
        strict = col < row
        u = _prefix_ones(bk)
        tot = tot_ref[...]
        tot_other = pltpu.roll(tot, HEAD, 1)
        dqa_ref[...] = jnp.zeros_like(dqa_ref)
        cg_ref[...] = jnp.zeros_like(cg_ref)
        for e in range(2):
            qm_ref[e] = jnp.where(lane == e, q_ref[...], 0) * 0.125
            dob_ref[e] = jnp.where(lane == e, dy_ref[...], 0.0).astype(BF16)
            rem_ref[e] = jnp.where(lane == e, tot, tot_other)

        def step(blocks):
            tile = lambda a: jnp.concatenate([a] * (bk // LANES), axis=1)
            nb = len(blocks)
            rows = [pl.ds(pl.multiple_of(kb * bk, bk), bk) for kb, _ in blocks]
            pairs = [(b, e) for b in range(nb) for e in range(2)]
            mask = lambda b, x: x if blocks[b][1] is None else jnp.where(blocks[b][1], x, 0.0)
            zs = {(b, e): _dot_nt(qm_ref[e], k_ref[rows[b], :]) for b, e in pairs}
            das = {(b, e): _dot_nt(dob_ref[e], jnp.where(lane == e, v_ref[rows[b], :], 0)) for b, e in pairs}
            zls, splits = {}, {}
            for b, e in pairs:
                z = zs[b, e]
                lk = mask(b, jnp.minimum(-z, 0.0) - jnp.log(1.0 + jnp.exp(-jnp.abs(z))))
                zls[b, e] = z + lk
                splits[b, e] = _split(lk)
            pres = {be: _dot(hi, u) + _dot(lo, u) for be, (hi, lo) in splits.items()}
            rem = [rem_ref[0], rem_ref[1]]
            aas, gs, gsplits = {}, {}, {}
            for b, e in pairs:
                a = mask(b, jnp.exp(zls[b, e] + (tile(rem[e]) - pres[b, e])))
                gs[b, e] = a * das[b, e]
                aas[b, e] = a.astype(BF16)
                gsplits[b, e] = _split(gs[b, e])
                rem[e] = rem[e] - jnp.broadcast_to(pres[b, e][:, bk - 1:bk], (bq, LANES))
            for b in range(nb):
                dv_ref[rows[b], :] += _dot_tn(aas[b, 0], dob_ref[0]) + _dot_tn(aas[b, 1], dob_ref[1])
            gpres = {be: _dot(hi, u) + _dot(lo, u) for be, (hi, lo) in gsplits.items()}
            cg = [cg_ref[0], cg_ref[1]]
            dzs = {}
            for b, e in pairs:
                dz = mask(b, gs[b, e] - jnp.exp(zls[b, e]) * (tile(cg[e]) + gpres[b, e]))
                dzs[b, e] = dz.astype(BF16)
                cg[e] = cg[e] + jnp.broadcast_to(gpres[b, e][:, bk - 1:bk], (bq, LANES))
            for b in range(nb):
                dk_ref[rows[b], :] += _dot_tn(dzs[b, 0], qm_ref[0]) + _dot_tn(dzs[b, 1], qm_ref[1])
            for e in range(2):
                dq = dqa_ref[e]
                for b in range(nb):
                    dq = dq + _dot(dzs[b, e], k_ref[rows[b], :])
                dqa_ref[e] = dq
            rem_ref[0], rem_ref[1] = rem
            cg_ref[0], cg_ref[1] = cg

        def loop(kb, c):
            step([(kb, None)])
            return c

        start = qi - jnp.clip(cnt_ref[pl.program_id(0), qi].astype(jnp.int32), 0, qi)
        lax.fori_loop(start, qi - 1, loop, 0)
        step([(jnp.maximum(qi - 1, 0), qi > 0), (qi, strict)])
        dq_ref[...] = jnp.where(lane == 0, dqa_ref[0], dqa_ref[1]) * 0.125

    return pl.pallas_call(
        body, name="sb_bwd", grid=(2, nq),
        in_specs=_sb_specs(T, bq) + [pl.BlockSpec((bq, LANES), lambda j, i: (i, j)),
                                     pl.BlockSpec((bq, LANES), lambda j, i: (i, j)),
                                     pl.BlockSpec(memory_space=pltpu.SMEM)],
        out_specs=[pl.BlockSpec((bq, LANES), lambda j, i: (i, j)),
                   pl.BlockSpec((T, LANES), lambda j, i: (0, j)),
                   pl.BlockSpec((T, LANES), lambda j, i: (0, j))],
        out_shape=[jax.ShapeDtypeStruct((T, 256), F32)] * 3,
        scratch_shapes=[pltpu.VMEM((2, bq, LANES), BF16), pltpu.VMEM((2, bq, LANES), BF16),
                        pltpu.VMEM((2, bq, LANES), F32), pltpu.VMEM((2, bq, LANES), F32),
                        pltpu.VMEM((2, bq, LANES), F32)],
        compiler_params=_cparams(("parallel", "arbitrary")),
    )(hb, hb, hb, tot, dy, cnt)


EP_TM = 256


def _ep_in_specs(tm, rev):
    idx = (lambda i: rev - i) if rev is not None else (lambda i: i)
    bo = (_INT_OFF["b_b"] - N_HB) // 256
    halo = lambda i: jnp.maximum(idx(i) * (tm // 8) - 1, 0)
    return [pl.BlockSpec((tm, 256), lambda i: (idx(i), 0)),
            pl.BlockSpec((tm, 256), lambda i: (idx(i), 0)),
            pl.BlockSpec((tm, 256), lambda i: (idx(i), 0)),
            pl.BlockSpec((tm, D_MODEL), lambda i: (idx(i), 0)),
            pl.BlockSpec((tm, 256), lambda i: (idx(i), bo)),
            pl.BlockSpec((tm, 256), lambda i: (idx(i), bo + 1)),
            pl.BlockSpec((tm, 256), lambda i: (idx(i), bo + 2)),
            pl.BlockSpec((8, 256), lambda i: (halo(i), bo + 1)),
            pl.BlockSpec((8, 256), lambda i: (halo(i), bo + 2)),
            pl.BlockSpec((3, 256), lambda i: (0, 0)),
            pl.BlockSpec((1, 256), lambda i: (0, 0)),
            pl.BlockSpec((1, D_MODEL), lambda i: (0, 0)),
            pl.BlockSpec((D_MODEL, D_MODEL), lambda i: (0, 0)),
            pl.BlockSpec((1, D_MODEL), lambda i: (0, 0))]


def _ep_mix(first, ya_ref, yc_ref, yd_ref, gate_ref, bb_ref, bc_ref, bx_ref, hc_ref, hx_ref, cw_ref, cb_ref, gg_ref):
    tm = ya_ref.shape[0]
    u = bc_ref[...] * bx_ref[...]
    halo = jnp.where(first, 0.0, hc_ref[...] * hx_ref[...])
    row = lax.broadcasted_iota(jnp.int32, (tm, 1), 0)
    u1 = jnp.where(row == 0, halo[7:8, :], pltpu.roll(u, 1, 0))
    u2 = jnp.where(row == 0, halo[6:7, :], jnp.where(row == 1, halo[7:8, :], pltpu.roll(u, 2, 0)))
    cw = cw_ref[...]
    conv = cw[0:1, :] * u2 + cw[1:2, :] * u1 + cw[2:3, :] * u + cb_ref[...]
    bb = bb_ref[...]
    ys = [ya_ref[...], bb * conv, yc_ref[...], yd_ref[...]]
    rs = [_rms(y) for y in ys]
    gg = gg_ref[...]
    yhat = jnp.concatenate([y * r for y, r in zip(ys, rs)], axis=1)
    gate = gate_ref[...]
    sig = 1.0 / (1.0 + jnp.exp(-gate))
    return u, u1, u2, conv, bb, rs, yhat, yhat * gg, gate, sig


def _epilogue_fwd(x, ya, yc, yd, hf, conv_w, conv_b, g_grp, w_out, g_post):
    T = x.shape[0]
    tm = EP_TM

    def body(x_ref, ya_ref, yc_ref, yd_ref, gate_ref, bb_ref, bc_ref, bx_ref, hc_ref, hx_ref, cw_ref, cb_ref,
             gg_ref, wo_ref, gp_ref, o_ref):
        (_, _, _, _, _, _, _, yn, gate, sig) = _ep_mix(
            pl.program_id(0) == 0, ya_ref, yc_ref, yd_ref, gate_ref, bb_ref, bc_ref, bx_ref, hc_ref, hx_ref,
            cw_ref, cb_ref, gg_ref)
        z = _dot((yn * (gate * sig)).astype(BF16), wo_ref[...])
        o_ref[...] = x_ref[...] + z * _rms(z) * gp_ref[...]

    return pl.pallas_call(
        body, name="epilogue_fwd", grid=(T // tm,),
        in_specs=[pl.BlockSpec((tm, D_MODEL), lambda i: (i, 0))] + _ep_in_specs(tm, None),
        out_specs=pl.BlockSpec((tm, D_MODEL), lambda i: (i, 0)),
        out_shape=jax.ShapeDtypeStruct((T, D_MODEL), F32),
        compiler_params=_cparams(("parallel",)),
    )(x, ya, yc, yd, hf, hf, hf, hf, hf, hf, conv_w, conv_b, g_grp, w_out, g_post)


def _epilogue_bwd(dxn, ya, yc, yd, hf, conv_w, conv_b, g_grp, w_out, g_post):
    T = dxn.shape[0]
    tm = EP_TM
    nt = T // tm
    ridx = lambda i: (nt - 1 - i, 0)

    def body(dx_ref, ya_ref, yc_ref, yd_ref, gate_ref, bb_ref, bc_ref, bx_ref, hc_ref, hx_ref, cw_ref, cb_ref,
             gg_ref, wo_ref, gp_ref,
             dya_ref, dyc_ref, dyd_ref, dhf_ref, dwo_ref, dgp_ref, dgg_ref, dcw_ref, dcb_ref, carry_ref):
        i = pl.program_id(0)

        @pl.when(i == 0)
        def _():
            for r in (dwo_ref, dgp_ref, dgg_ref, dcw_ref, dcb_ref, carry_ref):
                r[...] = jnp.zeros_like(r)

        (u, u1, u2, conv, bb, rs, yhat, yn, gate, sig) = _ep_mix(
            i == nt - 1, ya_ref, yc_ref, yd_ref, gate_ref, bb_ref, bc_ref, bx_ref, hc_ref, hx_ref,
            cw_ref, cb_ref, gg_ref)
        silu = gate * sig
        ymix = (yn * silu).astype(BF16)
        z = _dot(ymix, wo_ref[...])
        rz = _rms(z)
        dz, dgrow = _rms_bwd(dx_ref[...], z * rz, rz, gp_ref[...])
        dgp_ref[...] += _colsum(dgrow)
        dzb = dz.astype(BF16)
        dwo_ref[...] += _dot_tn(ymix, dzb)
        dymix = _dot_nt(dzb, wo_ref[...])
        dhf_ref[:, 0:D_MODEL] = dymix * yn * (sig * (1.0 + gate * (1.0 - sig)))
        dyn = dymix * silu
        dgg_ref[...] += _colsum(dyn * yhat)
        gg = gg_ref[...]
        dys = []
        for gi in range(4):
            sl = slice(gi * GROUP, (gi + 1) * GROUP)
            dyh = dyn[:, sl] * gg[:, sl]
            yh = yhat[:, sl]
            dys.append(rs[gi] * (dyh - yh * jnp.mean(dyh * yh, axis=-1, keepdims=True)))
        dya_ref[...] = dys[0]
        dyc_ref[...] = dys[2]
        dyd_ref[...] = dys[3]
        dyb = dys[1]
        dhf_ref[:, D_MODEL:D_MODEL + 256] = dyb * conv
        dconv = dyb * bb
        dcb_ref[...] += _colsum(dconv)
        dcw_ref[0:1, :] += _colsum(dconv * u2)
        dcw_ref[1:2, :] += _colsum(dconv * u1)
        dcw_ref[2:3, :] += _colsum(dconv * u)
        carry = carry_ref[...]
        row = lax.broadcasted_iota(jnp.int32, (tm, 1), 0)
        d1 = jnp.where(row == tm - 1, carry[0:1, :], pltpu.roll(dconv, tm - 1, 0))
        d2 = jnp.where(row == tm - 2, carry[0:1, :],
                       jnp.where(row == tm - 1, carry[1:2, :], pltpu.roll(dconv, tm - 2, 0)))
        cw = cw_ref[...]
        du = cw[2:3, :] * dconv + cw[1:2, :] * d1 + cw[0:1, :] * d2
        dhf_ref[:, D_MODEL + 256:D_MODEL + 512] = du * bx_ref[...]
        dhf_ref[:, D_MODEL + 512:D_MODEL + 768] = du * bc_ref[...]
        carry_ref[...] = dconv[0:8, :]

    in_specs = [pl.BlockSpec((tm, D_MODEL), ridx)] + _ep_in_specs(tm, nt - 1)
    return pl.pallas_call(
        body, name="epilogue_bwd", grid=(nt,), in_specs=in_specs,
        out_specs=[pl.BlockSpec((tm, 256), ridx), pl.BlockSpec((tm, 256), ridx), pl.BlockSpec((tm, 256), ridx),
                   pl.BlockSpec((tm, D_MODEL + 768), ridx),
                   pl.BlockSpec((D_MODEL, D_MODEL), lambda i: (0, 0)),
                   pl.BlockSpec((1, D_MODEL), lambda i: (0, 0)),
                   pl.BlockSpec((1, D_MODEL), lambda i: (0, 0)),
                   pl.BlockSpec((8, 256), lambda i: (0, 0)),
                   pl.BlockSpec((1, 256), lambda i: (0, 0))],
        out_shape=[jax.ShapeDtypeStruct((T, 256), F32)] * 3
                  + [jax.ShapeDtypeStruct((T, D_MODEL + 768), F32),
                     jax.ShapeDtypeStruct((D_MODEL, D_MODEL), F32),
                     jax.ShapeDtypeStruct((1, D_MODEL), F32),
                     jax.ShapeDtypeStruct((1, D_MODEL), F32),
                     jax.ShapeDtypeStruct((8, 256), F32),
                     jax.ShapeDtypeStruct((1, 256), F32)],
        scratch_shapes=[pltpu.VMEM((8, 256), F32)],
        compiler_params=_cparams(("arbitrary",)),
    )(dxn, ya, yc, yd, hf, hf, hf, hf, hf, hf, conv_w, conv_b, g_grp, w_out, g_post)


def _loss_head(y, tgt):
    T = y.shape[0]
    tm = 512

    def body(y_ref, t_ref, dy_ref, l_ref):
        @pl.when(pl.program_id(0) == 0)
        def _():
            l_ref[...] = jnp.zeros_like(l_ref)

        d = y_ref[...] - t_ref[...]
        dy_ref[...] = d * (1.0 / D_MODEL)
        part = jnp.sum(jnp.sum(d * d, axis=1, keepdims=True), axis=0, keepdims=True)
        l_ref[...] += part * (0.5 / D_MODEL)

    return pl.pallas_call(
        body, name="loss_head", grid=(T // tm,),
        in_specs=[pl.BlockSpec((tm, D_MODEL), lambda i: (i, 0))] * 2,
        out_specs=[pl.BlockSpec((tm, D_MODEL), lambda i: (i, 0)), pl.BlockSpec((8, LANES), lambda i: (0, 0))],
        out_shape=[jax.ShapeDtypeStruct((T, D_MODEL), F32), jax.ShapeDtypeStruct((8, LANES), F32)],
        compiler_params=_cparams(("arbitrary",)),
    )(y, tgt)


def _place():
    return lax.axis_index("x"), lax.axis_index("y"), lax.axis_index("c")


def _other_chips(x, y):
    return [(1 - x, y), (x, 1 - y), (1 - x, 1 - y)]


HBM = pl.BlockSpec(memory_space=pl.ANY)


def _gather_weights(shards):
    n = len(shards)

    def body(*refs):
        ins, outs = refs[:n], refs[n:2 * n]
        ici_send, ici_recv, d2d_send, d2d_recv, local_sems = refs[2 * n:]
        x, y, c = _place()
        me = 2 * x + y
        chips = _other_chips(x, y)

        def ici(a, j, layer_from):
            px, py = chips[j]
            return pltpu.make_async_remote_copy(
                src_ref=ins[a].at[c], dst_ref=outs[a].at[layer_from, c], send_sem=ici_send.at[3 * a + j],
                recv_sem=ici_recv.at[3 * a + j], device_id=(px, py, c), device_id_type=MESH)

        def d2d(a, j, layer):
            px, py = chips[j]
            blk = outs[a].at[2 * px + py, layer]
            return pltpu.make_async_remote_copy(
                src_ref=blk, dst_ref=blk, send_sem=d2d_send.at[3 * a + j], recv_sem=d2d_recv.at[3 * a + j],
                device_id=(x, y, 1 - c), device_id_type=MESH)

        local = [pltpu.make_async_copy(ins[a], outs[a].at[me], local_sems.at[a]) for a in range(n)]
        for cp in local:
            cp.start()
        sends = [ici(a, j, me) for j in range(3) for a in range(n)]
        for cp in sends:
            cp.start()
        for j in range(3):
            px, py = chips[j]
            for a in range(n):
                ici(a, j, 2 * px + py).wait_recv()
                fwd = d2d(a, j, c)
                fwd.start()
                sends.append(fwd)
        for j in range(3):
            for a in range(n):
                d2d(a, j, 1 - c).wait_recv()
        for cp in sends:
            cp.wait_send()
        for cp in local:
            cp.wait()

    return pl.pallas_call(
        body, name="gather_weights",
        in_specs=[HBM] * n, out_specs=[HBM] * n,
        out_shape=[jax.ShapeDtypeStruct((4,) + s.shape, s.dtype) for s in shards],
        scratch_shapes=[pltpu.SemaphoreType.DMA((3 * n,))] * 4 + [pltpu.SemaphoreType.DMA((n,))],
    )(*shards)


def _exchange_chips(parts, small):
    n = len(parts)

    def body(*refs):
        ins, sm_ref = refs[:n], refs[n]
        outs, osm_ref = refs[n + 1:2 * n + 1], refs[2 * n + 1]
        send_sems, recv_sems, ssend_sems, srecv_sems, local_sems = refs[2 * n + 2:]
        x, y, c = _place()
        me = 2 * x + y
        dev = 4 * x + 2 * y + c
        local = [pltpu.make_async_copy(ins[a].at[me], outs[a].at[me], local_sems.at[a]) for a in range(n)]
        local.append(pltpu.make_async_copy(sm_ref, osm_ref.at[dev], local_sems.at[n]))
        for cp in local:
            cp.start()
        sends = []
        for j, (px, py) in enumerate(_other_chips(x, y)):
            for a in range(n):
                cp = pltpu.make_async_remote_copy(
                    src_ref=ins[a].at[2 * px + py], dst_ref=outs[a].at[me], send_sem=send_sems.at[3 * a + j],
                    recv_sem=recv_sems.at[3 * a + j], device_id=(px, py, c), device_id_type=MESH)
                cp.start()
                sends.append(cp)
        flips = [(fx, fy, fc) for fx in (0, 1) for fy in (0, 1) for fc in (0, 1)][1:]
        for j, (fx, fy, fc) in enumerate(flips):
            cp = pltpu.make_async_remote_copy(
                src_ref=sm_ref, dst_ref=osm_ref.at[dev], send_sem=ssend_sems.at[j], recv_sem=srecv_sems.at[j],
                device_id=(x ^ fx, y ^ fy, c ^ fc), device_id_type=MESH)
            cp.start()
            sends.append(cp)
        for j, (px, py) in enumerate(_other_chips(x, y)):
            for a in range(n):
                pltpu.make_async_remote_copy(
                    src_ref=ins[a].at[me], dst_ref=outs[a].at[2 * px + py], send_sem=send_sems.at[3 * a + j],
                    recv_sem=recv_sems.at[3 * a + j], device_id=(px, py, c), device_id_type=MESH).wait_recv()
        for j, (fx, fy, fc) in enumerate(flips):
            src = 4 * (x ^ fx) + 2 * (y ^ fy) + (c ^ fc)
            pltpu.make_async_remote_copy(
                src_ref=sm_ref, dst_ref=osm_ref.at[src], send_sem=ssend_sems.at[j], recv_sem=srecv_sems.at[j],
                device_id=(x ^ fx, y ^ fy, c ^ fc), device_id_type=MESH).wait_recv()
        for cp in sends:
            cp.wait_send()
        for cp in local:
            cp.wait()

    return pl.pallas_call(
        body, name="exchange_chips",
        in_specs=[HBM] * (n + 1), out_specs=[HBM] * (n + 1),
        out_shape=[jax.ShapeDtypeStruct(p.shape, p.dtype) for p in parts]
                  + [jax.ShapeDtypeStruct((8,) + small.shape, small.dtype)],
        scratch_shapes=[pltpu.SemaphoreType.DMA((3 * n,)), pltpu.SemaphoreType.DMA((3 * n,)),
                        pltpu.SemaphoreType.DMA((7,)), pltpu.SemaphoreType.DMA((7,)),
                        pltpu.SemaphoreType.DMA((n + 1,))],
    )(*parts, small)


def _swap_cores(parts, name):
    n = len(parts)

    def body(*refs):
        ins, outs, send_sems, recv_sems = refs[:n], refs[n:2 * n], refs[2 * n], refs[2 * n + 1]
        x, y, c = _place()
        copies = [pltpu.make_async_remote_copy(
            src_ref=ins[a], dst_ref=outs[a], send_sem=send_sems.at[a], recv_sem=recv_sems.at[a],
            device_id=(x, y, 1 - c), device_id_type=MESH) for a in range(n)]
        for cp in copies:
            cp.start()
        for cp in copies:
            cp.wait()

    return pl.pallas_call(
        body, name=name, in_specs=[HBM] * n, out_specs=[HBM] * n,
        out_shape=[jax.ShapeDtypeStruct(p.shape, p.dtype) for p in parts],
        scratch_shapes=[pltpu.SemaphoreType.DMA((n,)), pltpu.SemaphoreType.DMA((n,))],
    )(*parts)


def _row_block(rows):
    for cand in (256, 128, 64, 32, 16, 8):
        if rows % cand == 0:
            return cand
    return rows


def _add(a, b, name):
    L, R, C = a.shape
    tr = _row_block(R)

    def body(a_ref, b_ref, o_ref):
        o_ref[...] = (a_ref[...] + b_ref[...]).astype(BF16)

    spec = pl.BlockSpec((1, tr, C), lambda l, i: (l, i, 0))
    return pl.pallas_call(
        body, name=name, grid=(L, R // tr), in_specs=[spec, spec], out_specs=spec,
        out_shape=jax.ShapeDtypeStruct((L, R, C), BF16), compiler_params=_cparams(("parallel", "parallel")),
    )(a, b)


def _sum_leading(buf, name):
    n, R, C = buf.shape
    tr = _row_block(R)

    def body(b_ref, o_ref):
        acc = b_ref[0].astype(F32)
        for k in range(1, n):
            acc = acc + b_ref[k].astype(F32)
        o_ref[...] = acc

    return pl.pallas_call(
        body, name=name, grid=(R // tr,),
        in_specs=[pl.BlockSpec((n, tr, C), lambda i: (0, i, 0))],
        out_specs=pl.BlockSpec((tr, C), lambda i: (i, 0)),
        out_shape=jax.ShapeDtypeStruct((R, C), F32),
        compiler_params=_cparams(("parallel",)),
    )(buf)


def _adam_update(w, g, m, v):
    c1 = 1.0 / (1.0 - ADAM_B1 ** ADAM_STEP)
    c2 = 1.0 / (1.0 - ADAM_B2 ** ADAM_STEP)
    mn = ADAM_B1 * m + (1.0 - ADAM_B1) * g
    vn = ADAM_B2 * v + (1.0 - ADAM_B2) * (g * g)
    return -ADAM_LR * ((mn * c1) / (jnp.sqrt(vn * c2) + ADAM_EPS) + ADAM_WD * w), mn, vn


def _adamw_layers(w, m, v, g_mine, g_other, name):
    _, R, C = w.shape
    tr = _row_block(R)

    def body(w_ref, m_ref, v_ref, gm_ref, go_ref, g_ref, d_ref, mo_ref, vo_ref):
        g = jnp.where(pl.program_id(0) == lax.axis_index("c"), gm_ref[...], go_ref[...])
        g_ref[0] = g
        d_ref[0], mo_ref[0], vo_ref[0] = _adam_update(w_ref[0], g, m_ref[0], v_ref[0])

    spec3 = pl.BlockSpec((1, tr, C), lambda l, i: (l, i, 0))
    spec2 = pl.BlockSpec((tr, C), lambda l, i: (i, 0))
    return pl.pallas_call(
        body, name=name, grid=(2, R // tr),
        in_specs=[spec3] * 3 + [spec2] * 2, out_specs=[spec3] * 4,
        out_shape=[jax.ShapeDtypeStruct(w.shape, F32)] * 4,
        compiler_params=_cparams(("parallel", "parallel")),
    )(w, m, v, g_mine, g_other)


PACK_C = 1024
_BIG = ("w_in", "w_out", "mla_w_uq", "mla_w_ukv", "conv_w")
_SMALL = ("norm_pre", "group_norm", "norm_post", "conv_b", "mla_q_norm", "mla_kv_norm", "attn_sinks")
_SMALL_W = {"norm_pre": 1024, "group_norm": 1024, "norm_post": 1024, "conv_b": 256, "mla_q_norm": 256,
            "mla_kv_norm": 128, "attn_sinks": 4}


_LOSS_AT = divmod(DEPTH * sum(_SMALL_W.values()), PACK_C)


def _pack_small(d, loss):
    flat = jnp.concatenate([d[n].reshape(-1) for n in _SMALL] + [loss.reshape(1)])
    return jnp.pad(flat, (0, 8 * PACK_C - flat.shape[0])).reshape(8, PACK_C)


def _adamw_small(w, m, v, got):
    ns = len(_SMALL)

    def body(*refs):
        got_ref = refs[3 * ns]
        outs = refs[3 * ns + 1:]
        gsum = got_ref[0]
        for d in range(1, 8):
            gsum = gsum + got_ref[d]
        outs[4 * ns][...] = gsum[_LOSS_AT[0]:_LOSS_AT[0] + 1, _LOSS_AT[1]:_LOSS_AT[1] + 1]
        off = 0
        for i, name in enumerate(_SMALL):
            wd = _SMALL_W[name]
            rows = []
            for l in range(DEPTH):
                r, c0 = divmod(off + l * wd, PACK_C)
                rows.append(gsum[r:r + 1, c0:c0 + wd])
            off += DEPTH * wd
            g = jnp.concatenate(rows, axis=0)
            delta, mn, vn = _adam_update(refs[i][...], g, refs[ns + i][...], refs[2 * ns + i][...])
            outs[i][...] = g
            outs[ns + i][...] = delta
            outs[2 * ns + i][...] = mn
            outs[3 * ns + i][...] = vn

    shapes = [jax.ShapeDtypeStruct(w[n].shape, F32) for n in _SMALL]
    res = pl.pallas_call(body, name="adamw_small", out_shape=shapes * 4 + [jax.ShapeDtypeStruct((1, 1), F32)])(
        *[w[n] for n in _SMALL], *[m[n] for n in _SMALL], *[v[n] for n in _SMALL], got)
    return [dict(zip(_SMALL, res[k * ns:(k + 1) * ns])) for k in range(4)], res[4 * ns]


def _w_in_internal(w):
    cols = []
    for n in _INT_ORDER:
        o, wd = _REAL_OFF[n]
        cols.append(w[:, o:o + wd])
        if _INT_W[n] != wd:
            cols.append(jnp.zeros((w.shape[0], _INT_W[n] - wd), w.dtype))
    return jnp.concatenate(cols, axis=1)


def _w_in_real(dw):
    return jnp.concatenate([dw[:, _INT_OFF[n]:_INT_OFF[n] + wd] for n, wd in _REAL], axis=1)


def _uq_internal(w):
    return jnp.pad(w.reshape(256, 4, 96), ((0, 0), (0, 0), (0, 32))).reshape(256, 512)


def _uq_real(dw):
    return dw.reshape(256, 4, 128)[:, :, :96].reshape(256, 384)


def _ukv_internal(w):
    w4 = w.reshape(128, 4, 128)
    k = jnp.pad(w4[:, :, :64], ((0, 0), (0, 0), (0, 64))).reshape(128, 512)
    return jnp.concatenate([k, w4[:, :, 64:].reshape(128, 256)], axis=1)


def _ukv_real(dw):
    k = dw[:, :512].reshape(128, 4, 128)[:, :, :64]
    v = dw[:, 512:].reshape(128, 4, 64)
    return jnp.concatenate([k, v], axis=2).reshape(128, 512)


def _layer_fwd(x, pos, p):
    xt, hb, hf = _inproj_fwd(x, p["norm_pre"], p["w_in"])
    ya = _swa_fwd(hb, p["attn_sinks"])
    qm, km, vm, vt = _mla_prep_fwd(hf, pos, p["mla_q_norm"], p["mla_kv_norm"], p["mla_w_uq"], p["mla_w_ukv"])
    yc, lse = _mla_fwd(qm, km, vt)
    yd, tot, cnt = _sb_fwd(hb)
    x_next = _epilogue_fwd(x, ya, yc, yd, hf, p["conv_w"], p["conv_b"], p["group_norm"], p["w_out"], p["norm_post"])
    return x_next, dict(x=x, xt=xt, hb=hb, hf=hf, ya=ya, yc=yc, yd=yd, tot=tot, cnt=cnt, qm=qm, km=km, vm=vm, lse=lse)


def _layer_bwd(dx_next, pos, p, s):
    (dya, dyc, dyd, dhf, dw_out, dg_post, dg_grp, dconv_w, dconv_b) = _epilogue_bwd(
        dx_next, s["ya"], s["yc"], s["yd"], s["hf"], p["conv_w"], p["conv_b"], p["group_norm"], p["w_out"],
        p["norm_post"])
    dq_d, dk_d, dv_d = _sb_bwd(s["hb"], s["tot"], s["cnt"], dyd)
    dqm, dkm, dvm = _mla_bwd(s["qm"], s["km"], s["vm"], s["yc"], s["lse"], dyc)
    dc, dw_uq, dw_ukv, dg_q, dg_kv = _mla_prep_bwd(
        s["hf"], pos, p["mla_q_norm"], p["mla_kv_norm"], p["mla_w_uq"], p["mla_w_ukv"], dqm, dkm, dvm)
    dq_a, dk_a, dv_a, dsinks = _swa_bwd(s["hb"], p["attn_sinks"], dya)
    dx, dh, dg_pre = _inproj_bwd_dx(s["x"], p["norm_pre"], p["w_in"], dx_next,
                                    [dq_a, dk_a, dv_a, dq_d, dk_d, dv_d, dhf, dc])
    dw_in = _matmul_over_tokens(s["xt"], dh, "inproj_bwd_dw")
    grads = dict(norm_pre=dg_pre[0], w_in=_w_in_real(dw_in), attn_sinks=dsinks[0, :4], conv_w=dconv_w[:3],
                 conv_b=dconv_b[0], mla_q_norm=dg_q[0], mla_w_uq=_uq_real(dw_uq), mla_kv_norm=dg_kv[0],
                 mla_w_ukv=_ukv_real(dw_ukv), group_norm=dg_grp[0], w_out=dw_out, norm_post=dg_post[0])
    return dx, grads


_WEIGHTS = ["norm_pre", "w_in", "attn_sinks", "conv_w", "conv_b", "mla_q_norm", "mla_w_uq", "mla_kv_norm",
            "mla_w_ukv", "group_norm", "w_out", "norm_post"]


def kernel(x, positions, norm_pre, w_in, attn_sinks, conv_w, conv_b, mla_q_norm, mla_w_uq, mla_kv_norm, mla_w_ukv, group_norm, w_out, norm_post, loss_target, m_norm_pre, m_w_in, m_attn_sinks, m_conv_w, m_conv_b, m_mla_q_norm, m_mla_w_uq, m_mla_kv_norm, m_mla_w_ukv, m_group_norm, m_w_out, m_norm_post, v_norm_pre, v_w_in, v_attn_sinks, v_conv_w, v_conv_b, v_mla_q_norm, v_mla_w_uq, v_mla_kv_norm, v_mla_w_ukv, v_group_norm, v_w_out, v_norm_post):
    w = dict(norm_pre=norm_pre, w_in=w_in, attn_sinks=attn_sinks, conv_w=conv_w, conv_b=conv_b,
             mla_q_norm=mla_q_norm, mla_w_uq=mla_w_uq, mla_kv_norm=mla_kv_norm, mla_w_ukv=mla_w_ukv,
             group_norm=group_norm, w_out=w_out, norm_post=norm_post)
    m = dict(norm_pre=m_norm_pre, w_in=m_w_in, attn_sinks=m_attn_sinks, conv_w=m_conv_w, conv_b=m_conv_b,
             mla_q_norm=m_mla_q_norm, mla_w_uq=m_mla_w_uq, mla_kv_norm=m_mla_kv_norm, mla_w_ukv=m_mla_w_ukv,
             group_norm=m_group_norm, w_out=m_w_out, norm_post=m_norm_post)
    v = dict(norm_pre=v_norm_pre, w_in=v_w_in, attn_sinks=v_attn_sinks, conv_w=v_conv_w, conv_b=v_conv_b,
             mla_q_norm=v_mla_q_norm, mla_w_uq=v_mla_w_uq, mla_kv_norm=v_mla_kv_norm, mla_w_ukv=v_mla_w_ukv,
             group_norm=v_group_norm, w_out=v_w_out, norm_post=v_norm_post)
    T = x.shape[1]
    xs = x[0]
    pos = positions[0].reshape(T, 1)
    tgt = loss_target[0]
    core = lax.axis_index("c")

    gathered = _gather_weights([w[n].astype(BF16) for n in _BIG[:4]] + [w["conv_w"]])
    full = {}
    for n, got in zip(_BIG, gathered):
        if n == "w_out":
            full[n] = jnp.moveaxis(got, 0, 1).reshape(DEPTH, D_MODEL, D_MODEL)
        else:
            full[n] = jnp.transpose(got, (1, 2, 0, 3)).reshape(DEPTH, got.shape[2], 4 * got.shape[3])

    layers = []
    for l in range(DEPTH):
        layers.append(dict(
            norm_pre=norm_pre[l:l + 1], w_in=_w_in_internal(full["w_in"][l]), attn_sinks=attn_sinks[l],
            conv_w=full["conv_w"][l], conv_b=conv_b[l:l + 1], mla_q_norm=mla_q_norm[l:l + 1],
            mla_w_uq=_uq_internal(full["mla_w_uq"][l]), mla_kv_norm=mla_kv_norm[l:l + 1],
            mla_w_ukv=_ukv_internal(full["mla_w_ukv"][l]), group_norm=group_norm[l:l + 1],
            w_out=full["w_out"][l], norm_post=norm_post[l:l + 1]))

    saved = []
    h = xs
    for l in range(DEPTH):
        h, s = _layer_fwd(h, pos, layers[l])
        saved.append(s)
    dy, loss_part = _loss_head(h, tgt)

    grads = [None] * DEPTH
    for l in reversed(range(DEPTH)):
        dy, grads[l] = _layer_bwd(dy, pos, layers[l], saved[l])

    def chunks(n, a):
        if n == "w_out":
            return a.reshape(4, D_MODEL // 4, D_MODEL)
        return jnp.transpose(a.reshape(a.shape[0], 4, a.shape[1] // 4), (1, 0, 2))

    mine = [chunks(n, jnp.where(core == 0, grads[0][n], grads[1][n])) for n in _BIG]
    theirs = [chunks(n, jnp.where(core == 0, grads[1][n], grads[0][n])) for n in _BIG]
    from_sibling = _swap_cores(theirs, "swap_layer_chunks")
    summed = [_add(a, b, "add_cores_" + n) for n, a, b in zip(_BIG, mine, from_sibling)]
    small = _pack_small({n: jnp.stack([grads[l][n] for l in range(DEPTH)]) for n in _SMALL}, loss_part[0, 0])
    *got, got_small = _exchange_chips(summed, small)
    done = [_sum_leading(b, "sum_chips_" + n) for n, b in zip(_BIG, got)]
    done_other = _swap_cores(done, "swap_layer_shards")

    outs, loss = _adamw_small(w, m, v, got_small)
    for n, gm, go in zip(_BIG, done, done_other):
        for d, a in zip(outs, _adamw_layers(w[n], m[n], v[n], gm, go, "adamw_" + n)):
            d[n] = a
    return (loss[0, 0], dy[None], *[outs[0][n] for n in _WEIGHTS], *[outs[1][n] for n in _WEIGHTS],
            *[outs[2][n] for n in _WEIGHTS], *[outs[3][n] for n in _WEIGHTS])
```

```python
import math

import jax
import jax.numpy as jnp
from jax import lax
from jax.experimental import pallas as pl
from jax.experimental.pallas import tpu as pltpu

F32 = jnp.float32
BF16 = jnp.bfloat16
MESH = pl.DeviceIdType.MESH

D_MODEL = 1024
DEPTH = 2
EPS = 1e-6
BLOCK = 128
HEAD = 64
LANES = 128
GROUP = 256
LOG2E = 1.4426950408889634
LN2 = 0.6931471805599453
MLA_QSCALE = 96 ** -0.5 * LOG2E
ROPE_HALF = 16
ROPE_THETA = 10000.0
ATT_BLK = 256
MLA_BQ = 512
NEG = -1e30
SB_DEAD = -104.0

ADAM_LR, ADAM_B1, ADAM_B2, ADAM_EPS, ADAM_WD, ADAM_STEP = 0.001, 0.9, 0.999, 1e-08, 0.01, 10

_REAL = [("a_q", 256), ("a_k", 128), ("a_v", 128), ("b_b", 256), ("b_c", 256), ("b_x", 256),
         ("c_q", 256), ("c_kv", 128), ("c_kr", 32), ("d_q", 256), ("d_k", 256), ("d_v", 256),
         ("gate", 1024)]
_REAL_OFF = {}
_o = 0
for _n, _w in _REAL:
    _REAL_OFF[_n] = (_o, _w)
    _o += _w
D_IN = _o
_INT_ORDER = ["a_q", "a_k", "a_v", "d_q", "d_k", "d_v", "gate", "b_b", "b_c", "b_x", "c_q", "c_kv", "c_kr"]
_INT_W = dict(_REAL)
_INT_W["c_kr"] = 128
_INT_OFF = {}
_o = 0
for _n in _INT_ORDER:
    _INT_OFF[_n] = _o
    _o += _INT_W[_n]
N_INT = _o
N_HB = _INT_OFF["gate"]
N_HF = N_INT - N_HB

VMEM_LIMIT = 56 * 1024 * 1024


def _cparams(sem):
    return pltpu.CompilerParams(dimension_semantics=sem, vmem_limit_bytes=VMEM_LIMIT)


def _dot(a, b):
    return jnp.dot(a, b, preferred_element_type=F32)


def _dot_nt(a, b):
    return lax.dot_general(a, b, (((1,), (1,)), ((), ())), preferred_element_type=F32)


def _dot_tn(a, b):
    return lax.dot_general(a, b, (((0,), (0,)), ((), ())), preferred_element_type=F32)


def _split(x):
    hi = x.astype(BF16)
    lo = (x - hi.astype(F32)).astype(BF16)
    return hi, lo


def _rms(x):
    return lax.rsqrt(jnp.mean(x * x, axis=-1, keepdims=True) + EPS)


def _rms_bwd(dy, xhat, r, g):
    dxhat = dy * g
    return r * (dxhat - xhat * jnp.mean(dxhat * xhat, axis=-1, keepdims=True)), dy * xhat


def _colsum(x):
    return jnp.sum(x, axis=0, keepdims=True)


def _inproj_fwd(x, g, w):
    T = x.shape[0]
    tm = 256

    def body(x_ref, g_ref, w_ref, xt_ref, hb_ref, hf_ref):
        xv = x_ref[...]
        xn32 = xv * _rms(xv) * g_ref[...]
        xt_ref[...] = jnp.transpose(xn32).astype(BF16)
        h = _dot(xn32.astype(BF16), w_ref[...])
        hb_ref[...] = h[:, :N_HB].astype(BF16)
        hf_ref[...] = h[:, N_HB:]

    return pl.pallas_call(
        body, name="inproj_fwd", grid=(T // tm,),
        in_specs=[pl.BlockSpec((tm, D_MODEL), lambda i: (i, 0)),
                  pl.BlockSpec((1, D_MODEL), lambda i: (0, 0)),
                  pl.BlockSpec((D_MODEL, N_INT), lambda i: (0, 0))],
        out_specs=[pl.BlockSpec((D_MODEL, tm), lambda i: (0, i)),
                   pl.BlockSpec((tm, N_HB), lambda i: (i, 0)),
                   pl.BlockSpec((tm, N_HF), lambda i: (i, 0))],
        out_shape=[jax.ShapeDtypeStruct((D_MODEL, T), BF16),
                   jax.ShapeDtypeStruct((T, N_HB), BF16),
                   jax.ShapeDtypeStruct((T, N_HF), F32)],
        compiler_params=_cparams(("parallel",)),
    )(x, g, w)


def _inproj_bwd_dx(x, g, w, dx_next, pieces):
    T = x.shape[0]
    tm = 256
    widths = [p.shape[1] for p in pieces]
    assert sum(widths) == N_INT

    def body(x_ref, g_ref, w_ref, dxn_ref, *rest):
        p_refs = rest[:len(pieces)]
        dx_ref, dh_ref, dg_ref = rest[len(pieces):]
        dh = jnp.concatenate([p[...].astype(BF16) for p in p_refs], axis=1)
        dh_ref[...] = dh
        dxn = _dot_nt(dh, w_ref[...])
        xv = x_ref[...]
        r = _rms(xv)
        dx, dgrow = _rms_bwd(dxn, xv * r, r, g_ref[...])
        dx_ref[...] = dx + dxn_ref[...]

        @pl.when(pl.program_id(0) == 0)
        def _():
            dg_ref[...] = jnp.zeros_like(dg_ref)

        dg_ref[...] += _colsum(dgrow)

    return pl.pallas_call(
        body, name="inproj_bwd_dx", grid=(T // tm,),
        in_specs=[pl.BlockSpec((tm, D_MODEL), lambda i: (i, 0)),
                  pl.BlockSpec((1, D_MODEL), lambda i: (0, 0)),
                  pl.BlockSpec((D_MODEL, N_INT), lambda i: (0, 0)),
                  pl.BlockSpec((tm, D_MODEL), lambda i: (i, 0))]
                 + [pl.BlockSpec((tm, wd), lambda i: (i, 0)) for wd in widths],
        out_specs=[pl.BlockSpec((tm, D_MODEL), lambda i: (i, 0)),
                   pl.BlockSpec((tm, N_INT), lambda i: (i, 0)),
                   pl.BlockSpec((1, D_MODEL), lambda i: (0, 0))],
        out_shape=[jax.ShapeDtypeStruct((T, D_MODEL), F32),
                   jax.ShapeDtypeStruct((T, N_INT), BF16),
                   jax.ShapeDtypeStruct((1, D_MODEL), F32)],
        compiler_params=_cparams(("arbitrary",)),
    )(x, g, w, dx_next, *pieces)


def _matmul_over_tokens(at, b, name):
    M, T = at.shape
    N = b.shape[1]
    tm, tn = min(1024, T), 512

    def body(a_ref, b_ref, o_ref):
        @pl.when(pl.program_id(1) == 0)
        def _():
            o_ref[...] = jnp.zeros_like(o_ref)

        o_ref[...] += _dot(a_ref[...], b_ref[...])

    return pl.pallas_call(
        body, name=name, grid=(N // tn, T // tm),
        in_specs=[pl.BlockSpec((M, tm), lambda j, t: (0, t)),
                  pl.BlockSpec((tm, tn), lambda j, t: (t, j))],
        out_specs=pl.BlockSpec((M, tn), lambda j, t: (0, j)),
        out_shape=jax.ShapeDtypeStruct((M, N), F32),
        compiler_params=_cparams(("parallel", "arbitrary")),
    )(at, b)


def _roll_f32(x, shift):
    return pltpu.roll(x.astype(F32), shift, 1)


def _swa_operands(h, q_ref, kp_ref, kc_ref, vp_ref, vc_ref):
    p, e = h // 2, h % 2
    lane = lax.broadcasted_iota(jnp.int32, (1, LANES), 1) // HEAD
    q = q_ref[:, p * LANES:(p + 1) * LANES]
    k_prev, k_cur, v_prev, v_cur = kp_ref[...], kc_ref[...], vp_ref[...], vc_ref[...]
    if e != p:
        q = _roll_f32(q, HEAD).astype(BF16)
        v_prev = _roll_f32(v_prev, HEAD).astype(BF16)
        v_cur = _roll_f32(v_cur, HEAD).astype(BF16)
    qs = jnp.where(lane == p, q, 0) * 0.125
    return dict(p=p, e=e, lane=lane, qs=qs, k_prev=k_prev, k_cur=k_cur,
                v_prev=jnp.where(lane == e, v_prev, 0), v_cur=jnp.where(lane == e, v_cur, 0),
                s_prev=_dot_nt(qs, k_prev), s_cur=_dot_nt(qs, k_cur))


def _swa_probs(ops, sink, first):
    row = lax.broadcasted_iota(jnp.int32, (BLOCK, BLOCK), 0)
    col = lax.broadcasted_iota(jnp.int32, (BLOCK, BLOCK), 1)
    s_prev = jnp.where(jnp.logical_and(col > row, jnp.logical_not(first)), ops["s_prev"], NEG)
    s_cur = jnp.where(col <= row, ops["s_cur"], NEG)
    m = jnp.maximum(jnp.maximum(jnp.max(s_prev, axis=1, keepdims=True),
                                jnp.max(s_cur, axis=1, keepdims=True)), sink)
    p_prev = jnp.exp(s_prev - m)
    p_cur = jnp.exp(s_cur - m)
    p_sink = jnp.exp(sink - m)
    inv = 1.0 / (jnp.sum(p_prev, axis=1, keepdims=True) + jnp.sum(p_cur, axis=1, keepdims=True) + p_sink)
    return p_prev * inv, p_cur * inv, p_sink * inv


def _swa_specs(T):
    nb = T // BLOCK
    qo, ko, vo = (_INT_OFF[n] // LANES for n in ("a_q", "a_k", "a_v"))
    prev = lambda i: jnp.maximum(i - 1, 0)
    return [pl.BlockSpec((BLOCK, 256), lambda i: (i, qo // 2)),
            pl.BlockSpec((BLOCK, LANES), lambda i: (prev(i), ko)),
            pl.BlockSpec((BLOCK, LANES), lambda i: (i, ko)),
            pl.BlockSpec((BLOCK, LANES), lambda i: (prev(i), vo)),
            pl.BlockSpec((BLOCK, LANES), lambda i: (i, vo)),
            pl.BlockSpec(memory_space=pltpu.SMEM)], nb


def _swa_fwd(hb, sinks):
    T = hb.shape[0]
    specs, nb = _swa_specs(T)

    def body(q_ref, kp_ref, kc_ref, vp_ref, vc_ref, s_ref, o_ref):
        first = pl.program_id(0) == 0
        ops = [_swa_operands(h, q_ref, kp_ref, kc_ref, vp_ref, vc_ref) for h in range(4)]
        probs = [_swa_probs(ops[h], s_ref[h], first) for h in range(4)]
        outs = [_dot(probs[h][0].astype(BF16), ops[h]["v_prev"]) + _dot(probs[h][1].astype(BF16), ops[h]["v_cur"])
                for h in range(4)]
        for p in range(2):
            o_ref[:, p * LANES:(p + 1) * LANES] = outs[2 * p] + outs[2 * p + 1]

    return pl.pallas_call(
        body, name="swa_fwd", grid=(nb,), in_specs=specs,
        out_specs=pl.BlockSpec((BLOCK, 256), lambda i: (i, 0)),
        out_shape=jax.ShapeDtypeStruct((T, 256), F32),
        compiler_params=_cparams(("parallel",)),
    )(hb, hb, hb, hb, hb, sinks)


def _swa_bwd(hb, sinks, dy):
    T = hb.shape[0]
    specs, nb = _swa_specs(T)

    def body(q_ref, kp_ref, kc_ref, vp_ref, vc_ref, s_ref, dy_ref, dq_ref, dk_ref, dv_ref, ds_ref):
        i = pl.program_id(0)
        first = i == 0
        cur = pl.ds(pl.multiple_of(i * BLOCK, BLOCK), BLOCK)
        prv = pl.ds(pl.multiple_of(jnp.maximum(i - 1, 0) * BLOCK, BLOCK), BLOCK)

        @pl.when(first)
        def _():
            ds_ref[...] = jnp.zeros_like(ds_ref)

        dk_ref[cur, :] = jnp.zeros((BLOCK, LANES), F32)
        dv_ref[cur, :] = jnp.zeros((BLOCK, LANES), F32)
        lane_id = lax.broadcasted_iota(jnp.int32, (8, LANES), 1)
        heads = range(4)
        ops = [_swa_operands(h, q_ref, kp_ref, kc_ref, vp_ref, vc_ref) for h in heads]
        probs = [_swa_probs(ops[h], s_ref[h], first) for h in heads]
        dos = [jnp.where(ops[h]["lane"] == ops[h]["e"], dy_ref[:, ops[h]["p"] * LANES:(ops[h]["p"] + 1) * LANES], 0.0)
               for h in heads]
        dobs = [d.astype(BF16) for d in dos]
        pbs = [(probs[h][0].astype(BF16), probs[h][1].astype(BF16)) for h in heads]
        outs = [_dot(pbs[h][0], ops[h]["v_prev"]) + _dot(pbs[h][1], ops[h]["v_cur"]) for h in heads]
        dps = [(_dot_nt(dobs[h], ops[h]["v_prev"]), _dot_nt(dobs[h], ops[h]["v_cur"])) for h in heads]
        dss, dsinks = [], jnp.zeros((8, LANES), F32)
        for h in heads:
            delta = jnp.sum(dos[h] * outs[h], axis=1, keepdims=True)
            dss.append(((probs[h][0] * (dps[h][0] - delta)).astype(BF16),
                        (probs[h][1] * (dps[h][1] - delta)).astype(BF16)))
            dsink = -jnp.sum(probs[h][2] * delta, axis=0, keepdims=True)
            dsinks += jnp.where(lane_id == h, dsink, 0.0)
        ds_ref[...] += dsinks
        dqs = [(_dot(dss[h][0], ops[h]["k_prev"]) + _dot(dss[h][1], ops[h]["k_cur"])) * 0.125 for h in heads]
        dk_prev = dk_cur = dv_prev = dv_cur = jnp.zeros((BLOCK, LANES), F32)
        for h in heads:
            p, e = ops[h]["p"], ops[h]["e"]
            dob_v = dobs[h] if e == p else pltpu.roll(dos[h], HEAD, 1).astype(BF16)
            dk_prev += _dot_tn(dss[h][0], ops[h]["qs"])
            dk_cur += _dot_tn(dss[h][1], ops[h]["qs"])
            dv_prev += _dot_tn(pbs[h][0], dob_v)
            dv_cur += _dot_tn(pbs[h][1], dob_v)
        dk_ref[prv, :] += dk_prev
        dk_ref[cur, :] += dk_cur
        dv_ref[prv, :] += dv_prev
        dv_ref[cur, :] += dv_cur
        for p in range(2):
            dq_pair = jnp.zeros((BLOCK, LANES), F32)
            for e in range(2):
                dq = jnp.where(ops[2 * p + e]["lane"] == p, dqs[2 * p + e], 0.0)
                dq_pair += dq if e == p else pltpu.roll(dq, HEAD, 1)
            dq_ref[:, p * LANES:(p + 1) * LANES] = dq_pair.astype(BF16)

    return pl.pallas_call(
        body, name="swa_bwd", grid=(nb,),
        in_specs=specs + [pl.BlockSpec((BLOCK, 256), lambda i: (i, 0))],
        out_specs=[pl.BlockSpec((BLOCK, 256), lambda i: (i, 0)),
                   pl.BlockSpec((T, LANES), lambda i: (0, 0)),
                   pl.BlockSpec((T, LANES), lambda i: (0, 0)),
                   pl.BlockSpec((8, LANES), lambda i: (0, 0))],
        out_shape=[jax.ShapeDtypeStruct((T, 256), BF16),
                   jax.ShapeDtypeStruct((T, LANES), F32),
                   jax.ShapeDtypeStruct((T, LANES), F32),
                   jax.ShapeDtypeStruct((8, LANES), F32)],
        compiler_params=_cparams(("arbitrary",)),
    )(hb, hb, hb, hb, hb, sinks, dy)


def _rope_tables(pos_ref):
    lane = lax.broadcasted_iota(jnp.int32, (1, LANES), 1)
    active = jnp.logical_and(lane >= HEAD, lane < HEAD + 2 * ROPE_HALF)
    idx = ((lane - HEAD) % ROPE_HALF).astype(F32)
    freq = jnp.exp(idx * (-math.log(ROPE_THETA) / ROPE_HALF))
    ang = pos_ref[...].astype(F32) * freq
    cos, sin = jnp.cos(ang), jnp.sin(ang)
    c = jnp.where(active, cos, 1.0)
    s_up = jnp.where(jnp.logical_and(active, lane >= HEAD + ROPE_HALF), sin, 0.0)
    s_dn = jnp.where(jnp.logical_and(active, lane < HEAD + ROPE_HALF), -sin, 0.0)
    return c, s_up, s_dn


def _rope(x, tabs):
    c, s_up, s_dn = tabs
    return x * c + pltpu.roll(x, ROPE_HALF, 1) * s_up + pltpu.roll(x, LANES - ROPE_HALF, 1) * s_dn


def _rope_t(dy, tabs):
    c, s_up, s_dn = tabs
    return dy * c + pltpu.roll(dy * s_up, LANES - ROPE_HALF, 1) + pltpu.roll(dy * s_dn, ROPE_HALF, 1)


def _mla_lat_specs(tm):
    cq, ckv, ckr = ((_INT_OFF[n] - N_HB) for n in ("c_q", "c_kv", "c_kr"))
    return [pl.BlockSpec((tm, 256), lambda i: (i, cq // 256)),
            pl.BlockSpec((tm, LANES), lambda i: (i, ckv // LANES)),
            pl.BlockSpec((tm, LANES), lambda i: (i, ckr // LANES)),
            pl.BlockSpec((tm, 1), lambda i: (i, 0)),
            pl.BlockSpec((1, 256), lambda i: (0, 0)),
            pl.BlockSpec((1, LANES), lambda i: (0, 0)),
            pl.BlockSpec((256, 512), lambda i: (0, 0)),
            pl.BlockSpec((LANES, 768), lambda i: (0, 0))]


def _mla_prep_fwd(hf, pos, g_q, g_kv, w_uq, w_ukv):
    T = hf.shape[0]
    tm = 512
    sub = tm // ATT_BLK

    def body(cq_ref, ckv_ref, ckr_ref, pos_ref, gq_ref, gkv_ref, wq_ref, wkv_ref, qm_ref, km_ref, vm_ref, vt_ref):
        tabs = _rope_tables(pos_ref)
        cq = cq_ref[...]
        q = _dot((cq * _rms(cq) * gq_ref[...]).astype(BF16), wq_ref[...])
        ckv = ckv_ref[...]
        kv = _dot((ckv * _rms(ckv) * gkv_ref[...]).astype(BF16), wkv_ref[...])
        kr = _rope(pltpu.roll(ckr_ref[...], HEAD, 1), tabs)
        for h in range(4):
            sl = slice(h * LANES, (h + 1) * LANES)
            qm_ref[:, sl] = (_rope(q[:, sl], tabs) * MLA_QSCALE).astype(BF16)
            km_ref[:, sl] = (kv[:, sl] + kr).astype(BF16)
        vm_ref[...] = kv[:, 512:].astype(BF16)
        for p in range(2):
            for s in range(sub):
                tile = kv[s * ATT_BLK:(s + 1) * ATT_BLK, 512 + p * LANES:512 + (p + 1) * LANES]
                vt_ref[p, s] = jnp.transpose(tile).astype(BF16)

    return pl.pallas_call(
        body, name="mla_prep_fwd", grid=(T // tm,), in_specs=_mla_lat_specs(tm),
        out_specs=[pl.BlockSpec((tm, 512), lambda i: (i, 0)),
                   pl.BlockSpec((tm, 512), lambda i: (i, 0)),
                   pl.BlockSpec((tm, 256), lambda i: (i, 0)),
                   pl.BlockSpec((2, sub, LANES, ATT_BLK), lambda i: (0, i, 0, 0))],
        out_shape=[jax.ShapeDtypeStruct((T, 512), BF16),
                   jax.ShapeDtypeStruct((T, 512), BF16),
                   jax.ShapeDtypeStruct((T, 256), BF16),
                   jax.ShapeDtypeStruct((2, T // ATT_BLK, LANES, ATT_BLK), BF16)],
        compiler_params=_cparams(("parallel",)),
    )(hf, hf, hf, pos, g_q, g_kv, w_uq, w_ukv)


def _mla_prep_bwd(hf, pos, g_q, g_kv, w_uq, w_ukv, dqm, dkt, dvt):
    T = hf.shape[0]
    tm = 512
    sub = tm // ATT_BLK

    def body(cq_ref, ckv_ref, ckr_ref, pos_ref, gq_ref, gkv_ref, wq_ref, wkv_ref, dq_ref, dk_ref, dv_ref,
             dc_ref, dwq_ref, dwkv_ref, dgq_ref, dgkv_ref):
        @pl.when(pl.program_id(0) == 0)
        def _():
            dwq_ref[...] = jnp.zeros_like(dwq_ref)
            dwkv_ref[...] = jnp.zeros_like(dwkv_ref)
            dgq_ref[...] = jnp.zeros_like(dgq_ref)
            dgkv_ref[...] = jnp.zeros_like(dgkv_ref)

        tabs = _rope_tables(pos_ref)
        lane = lax.broadcasted_iota(jnp.int32, (1, LANES), 1)
        dq = jnp.concatenate([_rope_t(dq_ref[:, h * LANES:(h + 1) * LANES] * MLA_QSCALE, tabs)
                              for h in range(4)], axis=1).astype(BF16)
        cq = cq_ref[...]
        rq = _rms(cq)
        cqn = (cq * rq * gq_ref[...]).astype(BF16)
        dwq_ref[...] += _dot_tn(cqn, dq)
        dcq, dgrow = _rms_bwd(_dot_nt(dq, wq_ref[...]), cq * rq, rq, gq_ref[...])
        dgq_ref[...] += _colsum(dgrow)
        dc_ref[:, 0:256] = dcq.astype(BF16)

        dk = jnp.concatenate([jnp.concatenate([jnp.transpose(dk_ref[p, s]) for p in range(2)], axis=1)
                              for s in range(sub)], axis=0) * LN2
        dv = jnp.concatenate([jnp.concatenate([jnp.transpose(dv_ref[p, s]) for p in range(2)], axis=1)
                              for s in range(sub)], axis=0)
        dkr = dk[:, 0:LANES] + dk[:, LANES:2 * LANES] + dk[:, 2 * LANES:3 * LANES] + dk[:, 3 * LANES:]
        dkr = pltpu.roll(_rope_t(dkr, tabs), HEAD, 1)
        dc_ref[:, 384:512] = jnp.where(lane < 2 * ROPE_HALF, dkr, 0.0).astype(BF16)
        dkv = jnp.concatenate([dk.astype(BF16), dv.astype(BF16)], axis=1)
        ckv = ckv_ref[...]
        rkv = _rms(ckv)
        ckvn = (ckv * rkv * gkv_ref[...]).astype(BF16)
        dwkv_ref[...] += _dot_tn(ckvn, dkv)
        dckv, dgrow = _rms_bwd(_dot_nt(dkv, wkv_ref[...]), ckv * rkv, rkv, gkv_ref[...])
        dgkv_ref[...] += _colsum(dgrow)
        dc_ref[:, 256:384] = dckv.astype(BF16)

    return pl.pallas_call(
        body, name="mla_prep_bwd", grid=(T // tm,),
        in_specs=_mla_lat_specs(tm) + [pl.BlockSpec((tm, 512), lambda i: (i, 0)),
                                       pl.BlockSpec((2, sub, 256, ATT_BLK), lambda i: (0, i, 0, 0)),
                                       pl.BlockSpec((2, sub, LANES, ATT_BLK), lambda i: (0, i, 0, 0))],
        out_specs=[pl.BlockSpec((tm, 512), lambda i: (i, 0)),
                   pl.BlockSpec((256, 512), lambda i: (0, 0)),
                   pl.BlockSpec((LANES, 768), lambda i: (0, 0)),
                   pl.BlockSpec((1, 256), lambda i: (0, 0)),
                   pl.BlockSpec((1, LANES), lambda i: (0, 0))],
        out_shape=[jax.ShapeDtypeStruct((T, 512), BF16),
                   jax.ShapeDtypeStruct((256, 512), F32),
                   jax.ShapeDtypeStruct((LANES, 768), F32),
                   jax.ShapeDtypeStruct((1, 256), F32),
                   jax.ShapeDtypeStruct((1, LANES), F32)],
        compiler_params=_cparams(("arbitrary",)),
    )(hf, hf, hf, pos, g_q, g_kv, w_uq, w_ukv, dqm, dkt, dvt)


def _causal_masks(bq, bk):
    row = lax.broadcasted_iota(jnp.int32, (bq, bk), 0)
    col = lax.broadcasted_iota(jnp.int32, (bq, bk), 1)
    return row, col


def _mla_fwd(qm, km, vt):
    T = qm.shape[0]
    bq, bk = min(MLA_BQ, T), ATT_BLK
    nq, nsub, nk = T // bq, bq // bk, T // bk

    def body(q_ref, k_ref, vt_ref, o_ref, lse_ref, acc_ref, m_ref, l_ref):
        qi = pl.program_id(1)
        key = lax.broadcasted_iota(jnp.int32, (bk, bq), 0)
        qry = lax.broadcasted_iota(jnp.int32, (bk, bq), 1)
        ones = jnp.ones((8, bk), BF16)
        acc_ref[...] = jnp.zeros_like(acc_ref)
        m_ref[...] = jnp.full_like(m_ref, NEG)
        l_ref[...] = jnp.zeros_like(l_ref)

        def step(kb0, masked):
            kbs = [kb0 + d for d in range(nsub)]
            sts = [[_dot_nt(k_ref[pl.ds(pl.multiple_of(kb * bk, bk), bk), e * LANES:(e + 1) * LANES],
                            q_ref[:, e * LANES:(e + 1) * LANES]) for kb in kbs] for e in range(2)]
            pts, alphas = [], []
            for e in range(2):
                st = [jnp.where(key + d * bk <= qry, sts[e][d], NEG) for d in range(nsub)] if masked else sts[e]
                m_prev = m_ref[e, 0:1, :]
                m_new = m_prev
                for d in range(nsub):
                    m_new = jnp.maximum(m_new, jnp.max(st[d], axis=0, keepdims=True))
                alpha = jnp.exp2(m_prev - m_new)
                pt = [jnp.exp2(st[d] - m_new).astype(BF16) for d in range(nsub)]
                l_new = alpha * l_ref[e]
                for d in range(nsub):
                    l_new = l_new + _dot(ones, pt[d])
                l_ref[e] = l_new
                m_ref[e] = jnp.broadcast_to(m_new, (8, bq))
                pts.append(pt)
                alphas.append(alpha)
            for e in range(2):
                acc = alphas[e] * acc_ref[e]
                for d in range(nsub):
                    acc = acc + _dot(vt_ref[0, kbs[d], e * HEAD:(e + 1) * HEAD, :], pts[e][d])
                acc_ref[e] = acc

        step(qi * nsub, True)

        def loop(t, c):
            step(t * nsub, False)
            return c

        lax.fori_loop(0, qi, loop, 0)
        outs, lses = [], []
        for e in range(2):
            l = l_ref[e, 0:1, :]
            outs.append(acc_ref[e] / l)
            lses.append(jnp.broadcast_to(m_ref[e, 0:1, :] * LN2 + jnp.log(l), (HEAD, bq)))
        o_ref[...] = jnp.transpose(jnp.concatenate(outs, axis=0))
        lse_ref[...] = jnp.transpose(jnp.concatenate(lses, axis=0))

    return pl.pallas_call(
        body, name="mla_fwd", grid=(2, nq),
        in_specs=[pl.BlockSpec((bq, 256), lambda j, i: (i, j)),
                  pl.BlockSpec((T, 256), lambda j, i: (0, j)),
                  pl.BlockSpec((1, nk, LANES, bk), lambda j, i: (j, 0, 0, 0))],
        out_specs=[pl.BlockSpec((bq, LANES), lambda j, i: (i, j)),
                   pl.BlockSpec((bq, LANES), lambda j, i: (i, j))],
        out_shape=[jax.ShapeDtypeStruct((T, 256), F32), jax.ShapeDtypeStruct((T, 256), F32)],
        scratch_shapes=[pltpu.VMEM((2, HEAD, bq), F32), pltpu.VMEM((2, 8, bq), F32), pltpu.VMEM((2, 8, bq), F32)],
        compiler_params=_cparams(("parallel", "arbitrary")),
    )(qm, km, vt)


def _mla_bwd(qm, km, vm, y, lse, dy):
    T = qm.shape[0]
    bq, bk = min(MLA_BQ, T), ATT_BLK
    nq, nsub, nk = T // bq, bq // bk, T // bk

    def body(q_ref, k_ref, v_ref, y_ref, lse_ref, dy_ref, dq_ref, dkt_ref, dvt_ref, dob_ref, st_ref, qt_ref, dot_ref):
        qi = pl.program_id(1)

        @pl.when(qi == 0)
        def _():
            dkt_ref[...] = jnp.zeros_like(dkt_ref)
            dvt_ref[...] = jnp.zeros_like(dvt_ref)

        lane = lax.broadcasted_iota(jnp.int32, (1, LANES), 1) // HEAD
        row, col = _causal_masks(bq, bk)
        dq_ref[...] = jnp.zeros_like(dq_ref)
        lse = lse_ref[...]
        lse_other = pltpu.roll(lse, HEAD, 1)
        qt_ref[...] = jnp.transpose(q_ref[...].astype(F32)).astype(BF16)
        dot_ref[...] = jnp.transpose(dy_ref[...]).astype(BF16)
        for e in range(2):
            do = jnp.where(lane == e, dy_ref[...], 0.0)
            dob_ref[e] = do.astype(BF16)
            st_ref[2 * e] = jnp.where(lane == e, lse, lse_other) * LOG2E
            st_ref[2 * e + 1] = jnp.broadcast_to(jnp.sum(do * y_ref[...], axis=1, keepdims=True), (bq, LANES))

        hss = [slice(e * LANES, (e + 1) * LANES) for e in range(2)]
        tile = lambda a: jnp.concatenate([a] * (bk // LANES), axis=1)

        def step(kb0, masked):
            kbs = [kb0 + d for d in range(nsub)]
            rows = [pl.ds(pl.multiple_of(kb * bk, bk), bk) for kb in kbs]
            pairs = [(d, e) for d in range(nsub) for e in range(2)]
            ss = {(d, e): _dot_nt(q_ref[:, hss[e]], k_ref[rows[d], hss[e]]) for d, e in pairs}
            dps = {(d, e): _dot_nt(dob_ref[e], jnp.where(lane == e, v_ref[rows[d], :], 0)) for d, e in pairs}
            ps, dss = {}, {}
            for d, e in pairs:
                s = jnp.where(col + d * bk <= row, ss[d, e], NEG) if masked else ss[d, e]
                p = jnp.exp2(s - tile(st_ref[2 * e]))
                dss[d, e] = (p * (dps[d, e] - tile(st_ref[2 * e + 1]))).astype(BF16)
                ps[d, e] = p.astype(BF16)
            for d, e in pairs:
                dvt_ref[0, kbs[d], e * HEAD:(e + 1) * HEAD, :] += _dot(dot_ref[e * HEAD:(e + 1) * HEAD, :], ps[d, e])
            for d, e in pairs:
                dkt_ref[0, kbs[d], hss[e], :] += _dot(qt_ref[hss[e], :], dss[d, e])
            for e in range(2):
                dq = dq_ref[:, hss[e]]
                for d in range(nsub):
                    dq = dq + _dot(dss[d, e], k_ref[rows[d], hss[e]])
                dq_ref[:, hss[e]] = dq

        step(qi * nsub, True)

        def loop(t, c):
            step(t * nsub, False)
            return c

        lax.fori_loop(0, qi, loop, 0)
        dq_ref[...] *= LN2

    return pl.pallas_call(
        body, name="mla_bwd", grid=(2, nq),
        in_specs=[pl.BlockSpec((bq, 256), lambda j, i: (i, j)),
                  pl.BlockSpec((T, 256), lambda j, i: (0, j)),
                  pl.BlockSpec((T, LANES), lambda j, i: (0, j)),
                  pl.BlockSpec((bq, LANES), lambda j, i: (i, j)),
                  pl.BlockSpec((bq, LANES), lambda j, i: (i, j)),
                  pl.BlockSpec((bq, LANES), lambda j, i: (i, j))],
        out_specs=[pl.BlockSpec((bq, 256), lambda j, i: (i, j)),
                   pl.BlockSpec((1, nk, 256, bk), lambda j, i: (j, 0, 0, 0)),
                   pl.BlockSpec((1, nk, LANES, bk), lambda j, i: (j, 0, 0, 0))],
        out_shape=[jax.ShapeDtypeStruct((T, 512), F32),
                   jax.ShapeDtypeStruct((2, nk, 256, bk), F32),
                   jax.ShapeDtypeStruct((2, nk, LANES, bk), F32)],
        scratch_shapes=[pltpu.VMEM((2, bq, LANES), BF16), pltpu.VMEM((4, bq, LANES), F32),
                        pltpu.VMEM((256, bq), BF16), pltpu.VMEM((LANES, bq), BF16)],
        compiler_params=_cparams(("parallel", "arbitrary")),
    )(qm, km, vm, y, lse, dy)


def _suffix_ones(n):
    r = lax.broadcasted_iota(jnp.int32, (n, n), 0)
    c = lax.broadcasted_iota(jnp.int32, (n, n), 1)
    return (r >= c).astype(BF16)


def _prefix_ones(n):
    r = lax.broadcasted_iota(jnp.int32, (n, n), 0)
    c = lax.broadcasted_iota(jnp.int32, (n, n), 1)
    return (r <= c).astype(BF16)


def _tri_sum(x, u):
    hi, lo = _split(x)
    return _dot(hi, u) + _dot(lo, u)


def _sb_specs(T, bq):
    qo, ko, vo = (_INT_OFF[n] // LANES for n in ("d_q", "d_k", "d_v"))
    return [pl.BlockSpec((bq, LANES), lambda j, i: (i, qo + j)),
            pl.BlockSpec((T, LANES), lambda j, i: (0, ko + j)),
            pl.BlockSpec((T, LANES), lambda j, i: (0, vo + j))]


def _sb_fwd(hb):
    T = hb.shape[0]
    bq = bk = ATT_BLK
    nq = T // bq

    def body(q_ref, k_ref, v_ref, o_ref, tot_ref, cnt_ref, qm_ref, car_ref):
        qi = pl.program_id(1)
        lane = lax.broadcasted_iota(jnp.int32, (1, LANES), 1) // HEAD
        row, col = _causal_masks(bq, bk)
        strict = col < row
        u = _suffix_ones(bk)
        o_ref[...] = jnp.zeros_like(o_ref)
        car_ref[...] = jnp.zeros_like(car_ref)
        for e in range(2):
            qm_ref[e] = jnp.where(lane == e, q_ref[...], 0) * 0.125

        def step(blocks):
            tile = lambda a: jnp.concatenate([a] * (bk // LANES), axis=1)
            rows = [pl.ds(pl.multiple_of(kb * bk, bk), bk) for kb, _ in blocks]
            pairs = [(b, e) for b in range(len(blocks)) for e in range(2)]
            zs = {(b, e): _dot_nt(qm_ref[e], k_ref[rows[b], :]) for b, e in pairs}
            splits = {}
            for b, e in pairs:
                z = zs[b, e]
                lk = jnp.minimum(-z, 0.0) - jnp.log(1.0 + jnp.exp(-jnp.abs(z)))
                if blocks[b][1] is not None:
                    lk = jnp.where(blocks[b][1], lk, 0.0)
                splits[b, e] = _split(lk)
            sufs = {be: _dot(hi, u) + _dot(lo, u) for be, (hi, lo) in splits.items()}
            car = [car_ref[0], car_ref[1]]
            aas = {}
            for b, e in pairs:
                a = jnp.exp(zs[b, e] + sufs[b, e] + tile(car[e]))
                if blocks[b][1] is not None:
                    a = jnp.where(blocks[b][1], a, 0.0)
                aas[b, e] = a.astype(BF16)
                car[e] = car[e] + jnp.broadcast_to(sufs[b, e][:, 0:1], (bq, LANES))
            acc = o_ref[...]
            for b, e in pairs:
                acc = acc + _dot(aas[b, e], jnp.where(lane == e, v_ref[rows[b], :], 0))
            o_ref[...] = acc
            car_ref[0], car_ref[1] = car

        step([(qi, strict), (jnp.maximum(qi - 1, 0), qi > 0)])

        def live():
            return jnp.max(jnp.maximum(car_ref[0], car_ref[1])) >= SB_DEAD

        def cond(c):
            return jnp.logical_and(c[0] < qi, c[1])

        def loop(c):
            step([(qi - 1 - c[0], None)])
            return c[0] + 1, live()

        done, _ = lax.while_loop(cond, loop, (jnp.minimum(qi, 1), live()))
        tot_ref[...] = jnp.where(lane == 0, car_ref[0], car_ref[1])
        cnt_ref[pl.program_id(0), qi] = done.astype(F32)

    return pl.pallas_call(
        body, name="sb_fwd", grid=(2, nq), in_specs=_sb_specs(T, bq),
        out_specs=[pl.BlockSpec((bq, LANES), lambda j, i: (i, j)), pl.BlockSpec((bq, LANES), lambda j, i: (i, j)),
                   pl.BlockSpec(memory_space=pltpu.SMEM)],
        out_shape=[jax.ShapeDtypeStruct((T, 256), F32), jax.ShapeDtypeStruct((T, 256), F32),
                   jax.ShapeDtypeStruct((2, nq), F32)],
        scratch_shapes=[pltpu.VMEM((2, bq, LANES), BF16), pltpu.VMEM((2, bq, LANES), F32)],
        compiler_params=_cparams(("parallel", "arbitrary")),
    )(hb, hb, hb)


def _sb_bwd(hb, tot, cnt, dy):
    T = hb.shape[0]
    bq = bk = ATT_BLK
    nq = T // bq

    def body(q_ref, k_ref, v_ref, tot_ref, dy_ref, cnt_ref, dq_ref, dk_ref, dv_ref, qm_ref, dob_ref, dqa_ref, rem_ref,
             cg_ref):
        qi = pl.program_id(1)

        @pl.when(qi == 0)
        def _():
            dk_ref[...] = jnp.zeros_like(dk_ref)
            dv_ref[...] = jnp.zeros_like(dv_ref)

        lane = lax.broadcasted_iota(jnp.int32, (1, LANES), 1) // HEAD
        row, col = _causal_masks(bq, bk)
        strict = col < row
        u = _prefix_ones(bk)
        tot = tot_ref[...]
        tot_other = pltpu.roll(tot, HEAD, 1)
        dqa_ref[...] = jnp.zeros_like(dqa_ref)
        cg_ref[...] = jnp.zeros_like(cg_ref)
        for e in range(2):
            qm_ref[e] = jnp.where(lane == e, q_ref[...], 0) * 0.125
            dob_ref[e] = jnp.where(lane == e, dy_ref[...], 0.0).astype(BF16)
            rem_ref[e] = jnp.where(lane == e, tot, tot_other)

        def step(blocks):
            tile = lambda a: jnp.concatenate([a] * (bk // LANES), axis=1)
            nb = len(blocks)
            rows = [pl.ds(pl.multiple_of(kb * bk, bk), bk) for kb, _ in blocks]
            pairs = [(b, e) for b in range(nb) for e in range(2)]
            mask = lambda b, x: x if blocks[b][1] is None else jnp.where(blocks[b][1], x, 0.0)
            zs = {(b, e): _dot_nt(qm_ref[e], k_ref[rows[b], :]) for b, e in pairs}
            das = {(b, e): _dot_nt(dob_ref[e], jnp.where(lane == e, v_ref[rows[b], :], 0)) for b, e in pairs}
            zls, splits = {}, {}
            for b, e in pairs:
                z = zs[b, e]
                lk = mask(b, jnp.minimum(-z, 0.0) - jnp.log(1.0 + jnp.exp(-jnp.abs(z))))
                zls[b, e] = z + lk
                splits[b, e] = _split(lk)
            pres = {be: _dot(hi, u) + _dot(lo, u) for be, (hi, lo) in splits.items()}
            rem = [rem_ref[0], rem_ref[1]]
            aas, gs, gsplits = {}, {}, {}
            for b, e in pairs:
                a = mask(b, jnp.exp(zls[b, e] + (tile(rem[e]) - pres[b, e])))
                gs[b, e] = a * das[b, e]
                aas[b, e] = a.astype(BF16)
                gsplits[b, e] = _split(gs[b, e])
                rem[e] = rem[e] - jnp.broadcast_to(pres[b, e][:, bk - 1:bk], (bq, LANES))
            for b in range(nb):
                dv_ref[rows[b], :] += _dot_tn(aas[b, 0], dob_ref[0]) + _dot_tn(aas[b, 1], dob_ref[1])
            gpres = {be: _dot(hi, u) + _dot(lo, u) for be, (hi, lo) in gsplits.items()}
            cg = [cg_ref[0], cg_ref[1]]
            dzs = {}
            for b, e in pairs:
                dz = mask(b, gs[b, e] - jnp.exp(zls[b, e]) * (tile(cg[e]) + gpres[b, e]))
                dzs[b, e] = dz.astype(BF16)
                cg[e] = cg[e] + jnp.broadcast_to(gpres[b, e][:, bk - 1:bk], (bq, LANES))
            for b in range(nb):
                dk_ref[rows[b], :] += _dot_tn(dzs[b, 0], qm_ref[0]) + _dot_tn(dzs[b, 1], qm_ref[1])
            for e in range(2):
                dq = dqa_ref[e]
                for b in range(nb):
                    dq = dq + _dot(dzs[b, e], k_ref[rows[b], :])
                dqa_ref[e] = dq
            rem_ref[0], rem_ref[1] = rem
            cg_ref[0], cg_ref[1] = cg

        def loop(kb, c):
            step([(kb, None)])
            return c

        start = qi - jnp.clip(cnt_ref[pl.program_id(0), qi].astype(jnp.int32), 0, qi)
        lax.fori_loop(start, qi - 1, loop, 0)
        step([(jnp.maximum(qi - 1, 0), qi > 0), (qi, strict)])
        dq_ref[...] = (jnp.where(lane == 0, dqa_ref[0], dqa_ref[1]) * 0.125).astype(BF16)

    return pl.pallas_call(
        body, name="sb_bwd", grid=(2, nq),
        in_specs=_sb_specs(T, bq) + [pl.BlockSpec((bq, LANES), lambda j, i: (i, j)),
                                     pl.BlockSpec((bq, LANES), lambda j, i: (i, j)),
                                     pl.BlockSpec(memory_space=pltpu.SMEM)],
        out_specs=[pl.BlockSpec((bq, LANES), lambda j, i: (i, j)),
                   pl.BlockSpec((T, LANES), lambda j, i: (0, j)),
                   pl.BlockSpec((T, LANES), lambda j, i: (0, j))],
        out_shape=[jax.ShapeDtypeStruct((T, 256), BF16)] + [jax.ShapeDtypeStruct((T, 256), F32)] * 2,
        scratch_shapes=[pltpu.VMEM((2, bq, LANES), BF16), pltpu.VMEM((2, bq, LANES), BF16),
                        pltpu.VMEM((2, bq, LANES), F32), pltpu.VMEM((2, bq, LANES), F32),
                        pltpu.VMEM((2, bq, LANES), F32)],
        compiler_params=_cparams(("parallel", "arbitrary")),
    )(hb, hb, hb, tot, dy, cnt)


EP_TM = 256


def _ep_in_specs(tm, rev):
    idx = (lambda i: rev - i) if rev is not None else (lambda i: i)
    bo = (_INT_OFF["b_b"] - N_HB) // 256
    halo = lambda i: jnp.maximum(idx(i) * (tm // 8) - 1, 0)
    return [pl.BlockSpec((tm, 256), lambda i: (idx(i), 0)),
            pl.BlockSpec((tm, 256), lambda i: (idx(i), 0)),
            pl.BlockSpec((tm, 256), lambda i: (idx(i), 0)),
            pl.BlockSpec((tm, D_MODEL), lambda i: (idx(i), 0)),
            pl.BlockSpec((tm, 256), lambda i: (idx(i), bo)),
            pl.BlockSpec((tm, 256), lambda i: (idx(i), bo + 1)),
            pl.BlockSpec((tm, 256), lambda i: (idx(i), bo + 2)),
            pl.BlockSpec((8, 256), lambda i: (halo(i), bo + 1)),
            pl.BlockSpec((8, 256), lambda i: (halo(i), bo + 2)),
            pl.BlockSpec((3, 256), lambda i: (0, 0)),
            pl.BlockSpec((1, 256), lambda i: (0, 0)),
            pl.BlockSpec((1, D_MODEL), lambda i: (0, 0)),
            pl.BlockSpec((D_MODEL, D_MODEL), lambda i: (0, 0)),
            pl.BlockSpec((1, D_MODEL), lambda i: (0, 0))]


def _ep_mix(first, ya_ref, yc_ref, yd_ref, gate_ref, bb_ref, bc_ref, bx_ref, hc_ref, hx_ref, cw_ref, cb_ref, gg_ref):
    tm = ya_ref.shape[0]
    u = bc_ref[...] * bx_ref[...]
    halo = jnp.where(first, 0.0, hc_ref[...] * hx_ref[...])
    row = lax.broadcasted_iota(jnp.int32, (tm, 1), 0)
    u1 = jnp.where(row == 0, halo[7:8, :], pltpu.roll(u, 1, 0))
    u2 = jnp.where(row == 0, halo[6:7, :], jnp.where(row == 1, halo[7:8, :], pltpu.roll(u, 2, 0)))
    cw = cw_ref[...]
    conv = cw[0:1, :] * u2 + cw[1:2, :] * u1 + cw[2:3, :] * u + cb_ref[...]
    bb = bb_ref[...]
    ys = [ya_ref[...], bb * conv, yc_ref[...], yd_ref[...]]
    rs = [_rms(y) for y in ys]
    gg = gg_ref[...]
    yhat = jnp.concatenate([y * r for y, r in zip(ys, rs)], axis=1)
    gate = gate_ref[...]
    sig = 1.0 / (1.0 + jnp.exp(-gate))
    return u, u1, u2, conv, bb, rs, yhat, yhat * gg, gate, sig


def _epilogue_fwd(x, ya, yc, yd, hf, conv_w, conv_b, g_grp, w_out, g_post):
    T = x.shape[0]
    tm = EP_TM

    def body(x_ref, ya_ref, yc_ref, yd_ref, gate_ref, bb_ref, bc_ref, bx_ref, hc_ref, hx_ref, cw_ref, cb_ref,
             gg_ref, wo_ref, gp_ref, o_ref):
        (_, _, _, _, _, _, _, yn, gate, sig) = _ep_mix(
            pl.program_id(0) == 0, ya_ref, yc_ref, yd_ref, gate_ref, bb_ref, bc_ref, bx_ref, hc_ref, hx_ref,
            cw_ref, cb_ref, gg_ref)
        z = _dot((yn * (gate * sig)).astype(BF16), wo_ref[...])
        o_ref[...] = x_ref[...] + z * _rms(z) * gp_ref[...]

    return pl.pallas_call(
        body, name="epilogue_fwd", grid=(T // tm,),
        in_specs=[pl.BlockSpec((tm, D_MODEL), lambda i: (i, 0))] + _ep_in_specs(tm, None),
        out_specs=pl.BlockSpec((tm, D_MODEL), lambda i: (i, 0)),
        out_shape=jax.ShapeDtypeStruct((T, D_MODEL), F32),
        compiler_params=_cparams(("parallel",)),
    )(x, ya, yc, yd, hf, hf, hf, hf, hf, hf, conv_w, conv_b, g_grp, w_out, g_post)


def _epilogue_bwd(dxn, ya, yc, yd, hf, conv_w, conv_b, g_grp, w_out, g_post):
    T = dxn.shape[0]
    tm = EP_TM
    nt = T // tm
    ridx = lambda i: (nt - 1 - i, 0)

    def body(dx_ref, ya_ref, yc_ref, yd_ref, gate_ref, bb_ref, bc_ref, bx_ref, hc_ref, hx_ref, cw_ref, cb_ref,
             gg_ref, wo_ref, gp_ref,
             dya_ref, dyc_ref, dyd_ref, dhf_ref, dwo_ref, dgp_ref, dgg_ref, dcw_ref, dcb_ref, carry_ref):
        i = pl.program_id(0)

        @pl.when(i == 0)
        def _():
            for r in (dwo_ref, dgp_ref, dgg_ref, dcw_ref, dcb_ref, carry_ref):
                r[...] = jnp.zeros_like(r)

        (u, u1, u2, conv, bb, rs, yhat, yn, gate, sig) = _ep_mix(
            i == nt - 1, ya_ref, yc_ref, yd_ref, gate_ref, bb_ref, bc_ref, bx_ref, hc_ref, hx_ref,
            cw_ref, cb_ref, gg_ref)
        silu = gate * sig
        ymix = (yn * silu).astype(BF16)
        z = _dot(ymix, wo_ref[...])
        rz = _rms(z)
        dz, dgrow = _rms_bwd(dx_ref[...], z * rz, rz, gp_ref[...])
        dgp_ref[...] += _colsum(dgrow)
        dzb = dz.astype(BF16)
        dwo_ref[...] += _dot_tn(ymix, dzb)
        dymix = _dot_nt(dzb, wo_ref[...])
        dhf_ref[:, 0:D_MODEL] = (dymix * yn * (sig * (1.0 + gate * (1.0 - sig)))).astype(BF16)
        dyn = dymix * silu
        dgg_ref[...] += _colsum(dyn * yhat)
        gg = gg_ref[...]
        dys = []
        for gi in range(4):
            sl = slice(gi * GROUP, (gi + 1) * GROUP)
            dyh = dyn[:, sl] * gg[:, sl]
            yh = yhat[:, sl]
            dys.append(rs[gi] * (dyh - yh * jnp.mean(dyh * yh, axis=-1, keepdims=True)))
        dya_ref[...] = dys[0]
        dyc_ref[...] = dys[2]
        dyd_ref[...] = dys[3]
        dyb = dys[1]
        dhf_ref[:, D_MODEL:D_MODEL + 256] = (dyb * conv).astype(BF16)
        dconv = dyb * bb
        dcb_ref[...] += _colsum(dconv)
        dcw_ref[0:1, :] += _colsum(dconv * u2)
        dcw_ref[1:2, :] += _colsum(dconv * u1)
        dcw_ref[2:3, :] += _colsum(dconv * u)
        carry = carry_ref[...]
        row = lax.broadcasted_iota(jnp.int32, (tm, 1), 0)
        d1 = jnp.where(row == tm - 1, carry[0:1, :], pltpu.roll(dconv, tm - 1, 0))
        d2 = jnp.where(row == tm - 2, carry[0:1, :],
                       jnp.where(row == tm - 1, carry[1:2, :], pltpu.roll(dconv, tm - 2, 0)))
        cw = cw_ref[...]
        du = cw[2:3, :] * dconv + cw[1:2, :] * d1 + cw[0:1, :] * d2
        dhf_ref[:, D_MODEL + 256:D_MODEL + 512] = (du * bx_ref[...]).astype(BF16)
        dhf_ref[:, D_MODEL + 512:D_MODEL + 768] = (du * bc_ref[...]).astype(BF16)
        carry_ref[...] = dconv[0:8, :]

    in_specs = [pl.BlockSpec((tm, D_MODEL), ridx)] + _ep_in_specs(tm, nt - 1)
    return pl.pallas_call(
        body, name="epilogue_bwd", grid=(nt,), in_specs=in_specs,
        out_specs=[pl.BlockSpec((tm, 256), ridx), pl.BlockSpec((tm, 256), ridx), pl.BlockSpec((tm, 256), ridx),
                   pl.BlockSpec((tm, D_MODEL + 768), ridx),
                   pl.BlockSpec((D_MODEL, D_MODEL), lambda i: (0, 0)),
                   pl.BlockSpec((1, D_MODEL), lambda i: (0, 0)),
                   pl.BlockSpec((1, D_MODEL), lambda i: (0, 0)),
                   pl.BlockSpec((8, 256), lambda i: (0, 0)),
                   pl.BlockSpec((1, 256), lambda i: (0, 0))],
        out_shape=[jax.ShapeDtypeStruct((T, 256), F32)] * 3
                  + [jax.ShapeDtypeStruct((T, D_MODEL + 768), BF16),
                     jax.ShapeDtypeStruct((D_MODEL, D_MODEL), F32),
                     jax.ShapeDtypeStruct((1, D_MODEL), F32),
                     jax.ShapeDtypeStruct((1, D_MODEL), F32),
                     jax.ShapeDtypeStruct((8, 256), F32),
                     jax.ShapeDtypeStruct((1, 256), F32)],
        scratch_shapes=[pltpu.VMEM((8, 256), F32)],
        compiler_params=_cparams(("arbitrary",)),
    )(dxn, ya, yc, yd, hf, hf, hf, hf, hf, hf, conv_w, conv_b, g_grp, w_out, g_post)


def _loss_head(y, tgt):
    T = y.shape[0]
    tm = 512

    def body(y_ref, t_ref, dy_ref, l_ref):
        @pl.when(pl.program_id(0) == 0)
        def _():
            l_ref[...] = jnp.zeros_like(l_ref)

        d = y_ref[...] - t_ref[...]
        dy_ref[...] = d * (1.0 / D_MODEL)
        part = jnp.sum(jnp.sum(d * d, axis=1, keepdims=True), axis=0, keepdims=True)
        l_ref[...] += part * (0.5 / D_MODEL)

    return pl.pallas_call(
        body, name="loss_head", grid=(T // tm,),
        in_specs=[pl.BlockSpec((tm, D_MODEL), lambda i: (i, 0))] * 2,
        out_specs=[pl.BlockSpec((tm, D_MODEL), lambda i: (i, 0)), pl.BlockSpec((8, LANES), lambda i: (0, 0))],
        out_shape=[jax.ShapeDtypeStruct((T, D_MODEL), F32), jax.ShapeDtypeStruct((8, LANES), F32)],
        compiler_params=_cparams(("arbitrary",)),
    )(y, tgt)


def _place():
    return lax.axis_index("x"), lax.axis_index("y"), lax.axis_index("c")


def _other_chips(x, y):
    return [(1 - x, y), (x, 1 - y), (1 - x, 1 - y)]


HBM = pl.BlockSpec(memory_space=pl.ANY)


def _gather_weights(shards):
    n = len(shards)

    def body(*refs):
        ins, outs = refs[:n], refs[n:2 * n]
        ici_send, ici_recv, d2d_send, d2d_recv, local_sems = refs[2 * n:]
        x, y, c = _place()
        me = 2 * x + y
        chips = _other_chips(x, y)

        def ici(a, j, layer_from):
            px, py = chips[j]
            return pltpu.make_async_remote_copy(
                src_ref=ins[a].at[c], dst_ref=outs[a].at[layer_from, c], send_sem=ici_send.at[3 * a + j],
                recv_sem=ici_recv.at[3 * a + j], device_id=(px, py, c), device_id_type=MESH)

        def d2d(a, j, layer):
            px, py = chips[j]
            blk = outs[a].at[2 * px + py, layer]
            return pltpu.make_async_remote_copy(
                src_ref=blk, dst_ref=blk, send_sem=d2d_send.at[3 * a + j], recv_sem=d2d_recv.at[3 * a + j],
                device_id=(x, y, 1 - c), device_id_type=MESH)

        local = [pltpu.make_async_copy(ins[a], outs[a].at[me], local_sems.at[a]) for a in range(n)]
        for cp in local:
            cp.start()
        sends = [ici(a, j, me) for j in range(3) for a in range(n)]
        for cp in sends:
            cp.start()
        for j in range(3):
            px, py = chips[j]
            for a in range(n):
                ici(a, j, 2 * px + py).wait_recv()
                fwd = d2d(a, j, c)
                fwd.start()
                sends.append(fwd)
        for j in range(3):
            for a in range(n):
                d2d(a, j, 1 - c).wait_recv()
        for cp in sends:
            cp.wait_send()
        for cp in local:
            cp.wait()

    return pl.pallas_call(
        body, name="gather_weights",
        in_specs=[HBM] * n, out_specs=[HBM] * n,
        out_shape=[jax.ShapeDtypeStruct((4,) + s.shape, s.dtype) for s in shards],
        scratch_shapes=[pltpu.SemaphoreType.DMA((3 * n,))] * 4 + [pltpu.SemaphoreType.DMA((n,))],
    )(*shards)


def _exchange_chips(parts, small):
    n = len(parts)

    def body(*refs):
        ins, sm_ref = refs[:n], refs[n]
        outs, osm_ref = refs[n + 1:2 * n + 1], refs[2 * n + 1]
        send_sems, recv_sems, ssend_sems, srecv_sems, local_sems = refs[2 * n + 2:]
        x, y, c = _place()
        me = 2 * x + y
        dev = 4 * x + 2 * y + c
        local = [pltpu.make_async_copy(ins[a].at[me], outs[a].at[me], local_sems.at[a]) for a in range(n)]
        local.append(pltpu.make_async_copy(sm_ref, osm_ref.at[dev], local_sems.at[n]))
        for cp in local:
            cp.start()
        sends = []
        for j, (px, py) in enumerate(_other_chips(x, y)):
            for a in range(n):
                cp = pltpu.make_async_remote_copy(
                    src_ref=ins[a].at[2 * px + py], dst_ref=outs[a].at[me], send_sem=send_sems.at[3 * a + j],
                    recv_sem=recv_sems.at[3 * a + j], device_id=(px, py, c), device_id_type=MESH)
                cp.start()
                sends.append(cp)
        flips = [(fx, fy, fc) for fx in (0, 1) for fy in (0, 1) for fc in (0, 1)][1:]
        for j, (fx, fy, fc) in enumerate(flips):
            cp = pltpu.make_async_remote_copy(
                src_ref=sm_ref, dst_ref=osm_ref.at[dev], send_sem=ssend_sems.at[j], recv_sem=srecv_sems.at[j],
                device_id=(x ^ fx, y ^ fy, c ^ fc), device_id_type=MESH)
            cp.start()
            sends.append(cp)
        for j, (px, py) in enumerate(_other_chips(x, y)):
            for a in range(n):
                pltpu.make_async_remote_copy(
                    src_ref=ins[a].at[me], dst_ref=outs[a].at[2 * px + py], send_sem=send_sems.at[3 * a + j],
                    recv_sem=recv_sems.at[3 * a + j], device_id=(px, py, c), device_id_type=MESH).wait_recv()
        for j, (fx, fy, fc) in enumerate(flips):
            src = 4 * (x ^ fx) + 2 * (y ^ fy) + (c ^ fc)
            pltpu.make_async_remote_copy(
                src_ref=sm_ref, dst_ref=osm_ref.at[src], send_sem=ssend_sems.at[j], recv_sem=srecv_sems.at[j],
                device_id=(x ^ fx, y ^ fy, c ^ fc), device_id_type=MESH).wait_recv()
        for cp in sends:
            cp.wait_send()
        for cp in local:
            cp.wait()

    return pl.pallas_call(
        body, name="exchange_chips",
        in_specs=[HBM] * (n + 1), out_specs=[HBM] * (n + 1),
        out_shape=[jax.ShapeDtypeStruct(p.shape, p.dtype) for p in parts]
                  + [jax.ShapeDtypeStruct((8,) + small.shape, small.dtype)],
        scratch_shapes=[pltpu.SemaphoreType.DMA((3 * n,)), pltpu.SemaphoreType.DMA((3 * n,)),
                        pltpu.SemaphoreType.DMA((7,)), pltpu.SemaphoreType.DMA((7,)),
                        pltpu.SemaphoreType.DMA((n + 1,))],
    )(*parts, small)


def _swap_cores(parts, name):
    n = len(parts)

    def body(*refs):
        ins, outs, send_sems, recv_sems = refs[:n], refs[n:2 * n], refs[2 * n], refs[2 * n + 1]
        x, y, c = _place()
        copies = [pltpu.make_async_remote_copy(
            src_ref=ins[a], dst_ref=outs[a], send_sem=send_sems.at[a], recv_sem=recv_sems.at[a],
            device_id=(x, y, 1 - c), device_id_type=MESH) for a in range(n)]
        for cp in copies:
            cp.start()
        for cp in copies:
            cp.wait()

    return pl.pallas_call(
        body, name=name, in_specs=[HBM] * n, out_specs=[HBM] * n,
        out_shape=[jax.ShapeDtypeStruct(p.shape, p.dtype) for p in parts],
        scratch_shapes=[pltpu.SemaphoreType.DMA((n,)), pltpu.SemaphoreType.DMA((n,))],
    )(*parts)


def _row_block(rows):
    for cand in (256, 128, 64, 32, 16, 8):
        if rows % cand == 0:
            return cand
    return rows


def _add(a, b, name):
    L, R, C = a.shape
    tr = _row_block(R)

    def body(a_ref, b_ref, o_ref):
        o_ref[...] = (a_ref[...] + b_ref[...]).astype(BF16)

    spec = pl.BlockSpec((1, tr, C), lambda l, i: (l, i, 0))
    return pl.pallas_call(
        body, name=name, grid=(L, R // tr), in_specs=[spec, spec], out_specs=spec,
        out_shape=jax.ShapeDtypeStruct((L, R, C), BF16), compiler_params=_cparams(("parallel", "parallel")),
    )(a, b)


def _sum_leading(buf, name):
    n, R, C = buf.shape
    tr = _row_block(R)

    def body(b_ref, o_ref):
        acc = b_ref[0].astype(F32)
        for k in range(1, n):
            acc = acc + b_ref[k].astype(F32)
        o_ref[...] = acc

    return pl.pallas_call(
        body, name=name, grid=(R // tr,),
        in_specs=[pl.BlockSpec((n, tr, C), lambda i: (0, i, 0))],
        out_specs=pl.BlockSpec((tr, C), lambda i: (i, 0)),
        out_shape=jax.ShapeDtypeStruct((R, C), F32),
        compiler_params=_cparams(("parallel",)),
    )(buf)


def _adam_update(w, g, m, v):
    c1 = 1.0 / (1.0 - ADAM_B1 ** ADAM_STEP)
    c2 = 1.0 / (1.0 - ADAM_B2 ** ADAM_STEP)
    mn = ADAM_B1 * m + (1.0 - ADAM_B1) * g
    vn = ADAM_B2 * v + (1.0 - ADAM_B2) * (g * g)
    return -ADAM_LR * ((mn * c1) / (jnp.sqrt(vn * c2) + ADAM_EPS) + ADAM_WD * w), mn, vn


def _adamw_layers(w, m, v, g_mine, g_other, name):
    _, R, C = w.shape
    tr = _row_block(R)

    def body(w_ref, m_ref, v_ref, gm_ref, go_ref, g_ref, d_ref, mo_ref, vo_ref):
        g = jnp.where(pl.program_id(0) == lax.axis_index("c"), gm_ref[...], go_ref[...])
        g_ref[0] = g
        d_ref[0], mo_ref[0], vo_ref[0] = _adam_update(w_ref[0], g, m_ref[0], v_ref[0])

    spec3 = pl.BlockSpec((1, tr, C), lambda l, i: (l, i, 0))
    spec2 = pl.BlockSpec((tr, C), lambda l, i: (i, 0))
    return pl.pallas_call(
        body, name=name, grid=(2, R // tr),
        in_specs=[spec3] * 3 + [spec2] * 2, out_specs=[spec3] * 4,
        out_shape=[jax.ShapeDtypeStruct(w.shape, F32)] * 4,
        compiler_params=_cparams(("parallel", "parallel")),
    )(w, m, v, g_mine, g_other)


PACK_C = 1024
_BIG = ("w_in", "w_out", "mla_w_uq", "mla_w_ukv", "conv_w")
_SMALL = ("norm_pre", "group_norm", "norm_post", "conv_b", "mla_q_norm", "mla_kv_norm", "attn_sinks")
_SMALL_W = {"norm_pre": 1024, "group_norm": 1024, "norm_post": 1024, "conv_b": 256, "mla_q_norm": 256,
            "mla_kv_norm": 128, "attn_sinks": 4}


_LOSS_AT = divmod(DEPTH * sum(_SMALL_W.values()), PACK_C)


def _pack_small(d, loss):
    flat = jnp.concatenate([d[n].reshape(-1) for n in _SMALL] + [loss.reshape(1)])
    return jnp.pad(flat, (0, 8 * PACK_C - flat.shape[0])).reshape(8, PACK_C)


def _adamw_small(w, m, v, got):
    ns = len(_SMALL)

    def body(*refs):
        got_ref = refs[3 * ns]
        outs = refs[3 * ns + 1:]
        gsum = got_ref[0]
        for d in range(1, 8):
            gsum = gsum + got_ref[d]
        outs[4 * ns][...] = gsum[_LOSS_AT[0]:_LOSS_AT[0] + 1, _LOSS_AT[1]:_LOSS_AT[1] + 1]
        off = 0
        for i, name in enumerate(_SMALL):
            wd = _SMALL_W[name]
            rows = []
            for l in range(DEPTH):
                r, c0 = divmod(off + l * wd, PACK_C)
                rows.append(gsum[r:r + 1, c0:c0 + wd])
            off += DEPTH * wd
            g = jnp.concatenate(rows, axis=0)
            delta, mn, vn = _adam_update(refs[i][...], g, refs[ns + i][...], refs[2 * ns + i][...])
            outs[i][...] = g
            outs[ns + i][...] = delta
            outs[2 * ns + i][...] = mn
            outs[3 * ns + i][...] = vn

    shapes = [jax.ShapeDtypeStruct(w[n].shape, F32) for n in _SMALL]
    res = pl.pallas_call(body, name="adamw_small", out_shape=shapes * 4 + [jax.ShapeDtypeStruct((1, 1), F32)])(
        *[w[n] for n in _SMALL], *[m[n] for n in _SMALL], *[v[n] for n in _SMALL], got)
    return [dict(zip(_SMALL, res[k * ns:(k + 1) * ns])) for k in range(4)], res[4 * ns]


def _w_in_internal(w):
    cols = []
    for n in _INT_ORDER:
        o, wd = _REAL_OFF[n]
        cols.append(w[:, o:o + wd])
        if _INT_W[n] != wd:
            cols.append(jnp.zeros((w.shape[0], _INT_W[n] - wd), w.dtype))
    return jnp.concatenate(cols, axis=1)


def _w_in_real(dw):
    return jnp.concatenate([dw[:, _INT_OFF[n]:_INT_OFF[n] + wd] for n, wd in _REAL], axis=1)


def _uq_internal(w):
    return jnp.pad(w.reshape(256, 4, 96), ((0, 0), (0, 0), (0, 32))).reshape(256, 512)


def _uq_real(dw):
    return dw.reshape(256, 4, 128)[:, :, :96].reshape(256, 384)


def _ukv_internal(w):
    w4 = w.reshape(128, 4, 128)
    k = jnp.pad(w4[:, :, :64], ((0, 0), (0, 0), (0, 64))).reshape(128, 512)
    return jnp.concatenate([k, w4[:, :, 64:].reshape(128, 256)], axis=1)


def _ukv_real(dw):
    k = dw[:, :512].reshape(128, 4, 128)[:, :, :64]
    v = dw[:, 512:].reshape(128, 4, 64)
    return jnp.concatenate([k, v], axis=2).reshape(128, 512)


def _layer_fwd(x, pos, p):
    xt, hb, hf = _inproj_fwd(x, p["norm_pre"], p["w_in"])
    ya = _swa_fwd(hb, p["attn_sinks"])
    qm, km, vm, vt = _mla_prep_fwd(hf, pos, p["mla_q_norm"], p["mla_kv_norm"], p["mla_w_uq"], p["mla_w_ukv"])
    yc, lse = _mla_fwd(qm, km, vt)
    yd, tot, cnt = _sb_fwd(hb)
    x_next = _epilogue_fwd(x, ya, yc, yd, hf, p["conv_w"], p["conv_b"], p["group_norm"], p["w_out"], p["norm_post"])
    return x_next, dict(x=x, xt=xt, hb=hb, hf=hf, ya=ya, yc=yc, yd=yd, tot=tot, cnt=cnt, qm=qm, km=km, vm=vm, lse=lse)


def _layer_bwd(dx_next, pos, p, s):
    (dya, dyc, dyd, dhf, dw_out, dg_post, dg_grp, dconv_w, dconv_b) = _epilogue_bwd(
        dx_next, s["ya"], s["yc"], s["yd"], s["hf"], p["conv_w"], p["conv_b"], p["group_norm"], p["w_out"],
        p["norm_post"])
    dq_d, dk_d, dv_d = _sb_bwd(s["hb"], s["tot"], s["cnt"], dyd)
    dqm, dkt, dvt = _mla_bwd(s["qm"], s["km"], s["vm"], s["yc"], s["lse"], dyc)
    dc, dw_uq, dw_ukv, dg_q, dg_kv = _mla_prep_bwd(
        s["hf"], pos, p["mla_q_norm"], p["mla_kv_norm"], p["mla_w_uq"], p["mla_w_ukv"], dqm, dkt, dvt)
    dq_a, dk_a, dv_a, dsinks = _swa_bwd(s["hb"], p["attn_sinks"], dya)
    dx, dh, dg_pre = _inproj_bwd_dx(s["x"], p["norm_pre"], p["w_in"], dx_next,
                                    [dq_a, dk_a, dv_a, dq_d, dk_d, dv_d, dhf, dc])
    dw_in = _matmul_over_tokens(s["xt"], dh, "inproj_bwd_dw")
    grads = dict(norm_pre=dg_pre[0], w_in=_w_in_real(dw_in), attn_sinks=dsinks[0, :4], conv_w=dconv_w[:3],
                 conv_b=dconv_b[0], mla_q_norm=dg_q[0], mla_w_uq=_uq_real(dw_uq), mla_kv_norm=dg_kv[0],
                 mla_w_ukv=_ukv_real(dw_ukv), group_norm=dg_grp[0], w_out=dw_out, norm_post=dg_post[0])
    return dx, grads


_WEIGHTS = ["norm_pre", "w_in", "attn_sinks", "conv_w", "conv_b", "mla_q_norm", "mla_w_uq", "mla_kv_norm",
            "mla_w_ukv", "group_norm", "w_out", "norm_post"]


def kernel(x, positions, norm_pre, w_in, attn_sinks, conv_w, conv_b, mla_q_norm, mla_w_uq, mla_kv_norm, mla_w_ukv, group_norm, w_out, norm_post, loss_target, m_norm_pre, m_w_in, m_attn_sinks, m_conv_w, m_conv_b, m_mla_q_norm, m_mla_w_uq, m_mla_kv_norm, m_mla_w_ukv, m_group_norm, m_w_out, m_norm_post, v_norm_pre, v_w_in, v_attn_sinks, v_conv_w, v_conv_b, v_mla_q_norm, v_mla_w_uq, v_mla_kv_norm, v_mla_w_ukv, v_group_norm, v_w_out, v_norm_post):
    w = dict(norm_pre=norm_pre, w_in=w_in, attn_sinks=attn_sinks, conv_w=conv_w, conv_b=conv_b,
             mla_q_norm=mla_q_norm, mla_w_uq=mla_w_uq, mla_kv_norm=mla_kv_norm, mla_w_ukv=mla_w_ukv,
             group_norm=group_norm, w_out=w_out, norm_post=norm_post)
    m = dict(norm_pre=m_norm_pre, w_in=m_w_in, attn_sinks=m_attn_sinks, conv_w=m_conv_w, conv_b=m_conv_b,
             mla_q_norm=m_mla_q_norm, mla_w_uq=m_mla_w_uq, mla_kv_norm=m_mla_kv_norm, mla_w_ukv=m_mla_w_ukv,
             group_norm=m_group_norm, w_out=m_w_out, norm_post=m_norm_post)
    v = dict(norm_pre=v_norm_pre, w_in=v_w_in, attn_sinks=v_attn_sinks, conv_w=v_conv_w, conv_b=v_conv_b,
             mla_q_norm=v_mla_q_norm, mla_w_uq=v_mla_w_uq, mla_kv_norm=v_mla_kv_norm, mla_w_ukv=v_mla_w_ukv,
             group_norm=v_group_norm, w_out=v_w_out, norm_post=v_norm_post)
    T = x.shape[1]
    xs = x[0]
    pos = positions[0].reshape(T, 1)
    tgt = loss_target[0]
    core = lax.axis_index("c")

    gathered = _gather_weights([w[n].astype(BF16) for n in _BIG[:4]] + [w["conv_w"]])
    full = {}
    for n, got in zip(_BIG, gathered):
        if n == "w_out":
            full[n] = jnp.moveaxis(got, 0, 1).reshape(DEPTH, D_MODEL, D_MODEL)
        else:
            full[n] = jnp.transpose(got, (1, 2, 0, 3)).reshape(DEPTH, got.shape[2], 4 * got.shape[3])

    layers = []
    for l in range(DEPTH):
        layers.append(dict(
            norm_pre=norm_pre[l:l + 1], w_in=_w_in_internal(full["w_in"][l]), attn_sinks=attn_sinks[l],
            conv_w=full["conv_w"][l], conv_b=conv_b[l:l + 1], mla_q_norm=mla_q_norm[l:l + 1],
            mla_w_uq=_uq_internal(full["mla_w_uq"][l]), mla_kv_norm=mla_kv_norm[l:l + 1],
            mla_w_ukv=_ukv_internal(full["mla_w_ukv"][l]), group_norm=group_norm[l:l + 1],
            w_out=full["w_out"][l], norm_post=norm_post[l:l + 1]))

    saved = []
    h = xs
    for l in range(DEPTH):
        h, s = _layer_fwd(h, pos, layers[l])
        saved.append(s)
    dy, loss_part = _loss_head(h, tgt)

    grads = [None] * DEPTH
    for l in reversed(range(DEPTH)):
        dy, grads[l] = _layer_bwd(dy, pos, layers[l], saved[l])

    def chunks(n, a):
        if n == "w_out":
            return a.reshape(4, D_MODEL // 4, D_MODEL)
        return jnp.transpose(a.reshape(a.shape[0], 4, a.shape[1] // 4), (1, 0, 2))

    mine = [chunks(n, jnp.where(core == 0, grads[0][n], grads[1][n])) for n in _BIG]
    theirs = [chunks(n, jnp.where(core == 0, grads[1][n], grads[0][n])) for n in _BIG]
    from_sibling = _swap_cores(theirs, "swap_layer_chunks")
    summed = [_add(a, b, "add_cores_" + n) for n, a, b in zip(_BIG, mine, from_sibling)]
    small = _pack_small({n: jnp.stack([grads[l][n] for l in range(DEPTH)]) for n in _SMALL}, loss_part[0, 0])
    *got, got_small = _exchange_chips(summed, small)
    done = [_sum_leading(b, "sum_chips_" + n) for n, b in zip(_BIG, got)]
    done_other = _swap_cores(done, "swap_layer_shards")

    outs, loss = _adamw_small(w, m, v, got_small)
    for n, gm, go in zip(_BIG, done, done_other):
        for d, a in zip(outs, _adamw_layers(w[n], m[n], v[n], gm, go, "adamw_" + n)):
            d[n] = a
    return (loss[0, 0], dy[None], *[outs[0][n] for n in _WEIGHTS], *[outs[1][n] for n in _WEIGHTS],
            *[outs[2][n] for n in _WEIGHTS], *[outs[3][n] for n in _WEIGHTS])
```

```python
import math

import jax
import jax.numpy as jnp
from jax import lax
from jax.experimental import pallas as pl
from jax.experimental.pallas import tpu as pltpu

F32 = jnp.float32
BF16 = jnp.bfloat16
MESH = pl.DeviceIdType.MESH

D_MODEL = 1024
DEPTH = 2
EPS = 1e-6
BLOCK = 128
HEAD = 64
LANES = 128
GROUP = 256
LOG2E = 1.4426950408889634
LN2 = 0.6931471805599453
MLA_QSCALE = 96 ** -0.5 * LOG2E
ROPE_HALF = 16
ROPE_THETA = 10000.0
ATT_BLK = 256
MLA_BQ = 512
NEG = -1e30
SB_DEAD = -104.0

ADAM_LR, ADAM_B1, ADAM_B2, ADAM_EPS, ADAM_WD, ADAM_STEP = 0.001, 0.9, 0.999, 1e-08, 0.01, 10

_REAL = [("a_q", 256), ("a_k", 128), ("a_v", 128), ("b_b", 256), ("b_c", 256), ("b_x", 256),
         ("c_q", 256), ("c_kv", 128), ("c_kr", 32), ("d_q", 256), ("d_k", 256), ("d_v", 256),
         ("gate", 1024)]
_REAL_OFF = {}
_o = 0
for _n, _w in _REAL:
    _REAL_OFF[_n] = (_o, _w)
    _o += _w
D_IN = _o
_INT_ORDER = ["a_q", "a_k", "a_v", "d_q", "d_k", "d_v", "gate", "b_b", "b_c", "b_x", "c_q", "c_kv", "c_kr"]
_INT_W = dict(_REAL)
_INT_W["c_kr"] = 128
_INT_OFF = {}
_o = 0
for _n in _INT_ORDER:
    _INT_OFF[_n] = _o
    _o += _INT_W[_n]
N_INT = _o
N_HB = _INT_OFF["gate"]
N_HF = N_INT - N_HB

VMEM_LIMIT = 56 * 1024 * 1024


def _cparams(sem):
    return pltpu.CompilerParams(dimension_semantics=sem, vmem_limit_bytes=VMEM_LIMIT)


def _dot(a, b):
    return jnp.dot(a, b, preferred_element_type=F32)


def _dot_nt(a, b):
    return lax.dot_general(a, b, (((1,), (1,)), ((), ())), preferred_element_type=F32)


def _dot_tn(a, b):
    return lax.dot_general(a, b, (((0,), (0,)), ((), ())), preferred_element_type=F32)


def _split(x):
    hi = x.astype(BF16)
    lo = (x - hi.astype(F32)).astype(BF16)
    return hi, lo


def _rms(x):
    return lax.rsqrt(jnp.mean(x * x, axis=-1, keepdims=True) + EPS)


def _rms_bwd(dy, xhat, r, g):
    dxhat = dy * g
    return r * (dxhat - xhat * jnp.mean(dxhat * xhat, axis=-1, keepdims=True)), dy * xhat


def _colsum(x):
    return jnp.sum(x, axis=0, keepdims=True)


def _inproj_fwd(x, g, w):
    T = x.shape[0]
    tm = 256

    def body(x_ref, g_ref, w_ref, xt_ref, hb_ref, hf_ref):
        xv = x_ref[...]
        xn32 = xv * _rms(xv) * g_ref[...]
        xt_ref[...] = jnp.transpose(xn32).astype(BF16)
        h = _dot(xn32.astype(BF16), w_ref[...])
        hb_ref[...] = h[:, :N_HB].astype(BF16)
        hf_ref[...] = h[:, N_HB:]

    return pl.pallas_call(
        body, name="inproj_fwd", grid=(T // tm,),
        in_specs=[pl.BlockSpec((tm, D_MODEL), lambda i: (i, 0)),
                  pl.BlockSpec((1, D_MODEL), lambda i: (0, 0)),
                  pl.BlockSpec((D_MODEL, N_INT), lambda i: (0, 0))],
        out_specs=[pl.BlockSpec((D_MODEL, tm), lambda i: (0, i)),
                   pl.BlockSpec((tm, N_HB), lambda i: (i, 0)),
                   pl.BlockSpec((tm, N_HF), lambda i: (i, 0))],
        out_shape=[jax.ShapeDtypeStruct((D_MODEL, T), BF16),
                   jax.ShapeDtypeStruct((T, N_HB), BF16),
                   jax.ShapeDtypeStruct((T, N_HF), F32)],
        compiler_params=_cparams(("parallel",)),
    )(x, g, w)


def _inproj_bwd_dx(x, g, w, dx_next, pieces):
    T = x.shape[0]
    tm = 256
    widths = [p.shape[1] for p in pieces]
    assert sum(widths) == N_INT

    def body(x_ref, g_ref, w_ref, dxn_ref, *rest):
        p_refs = rest[:len(pieces)]
        dx_ref, dh_ref, dg_ref = rest[len(pieces):]
        dh = jnp.concatenate([p[...].astype(BF16) for p in p_refs], axis=1)
        dh_ref[...] = dh
        dxn = _dot_nt(dh, w_ref[...])
        xv = x_ref[...]
        r = _rms(xv)
        dx, dgrow = _rms_bwd(dxn, xv * r, r, g_ref[...])
        dx_ref[...] = dx + dxn_ref[...]

        @pl.when(pl.program_id(0) == 0)
        def _():
            dg_ref[...] = jnp.zeros_like(dg_ref)

        dg_ref[...] += _colsum(dgrow)

    return pl.pallas_call(
        body, name="inproj_bwd_dx", grid=(T // tm,),
        in_specs=[pl.BlockSpec((tm, D_MODEL), lambda i: (i, 0)),
                  pl.BlockSpec((1, D_MODEL), lambda i: (0, 0)),
                  pl.BlockSpec((D_MODEL, N_INT), lambda i: (0, 0)),
                  pl.BlockSpec((tm, D_MODEL), lambda i: (i, 0))]
                 + [pl.BlockSpec((tm, wd), lambda i: (i, 0)) for wd in widths],
        out_specs=[pl.BlockSpec((tm, D_MODEL), lambda i: (i, 0)),
                   pl.BlockSpec((tm, N_INT), lambda i: (i, 0)),
                   pl.BlockSpec((1, D_MODEL), lambda i: (0, 0))],
        out_shape=[jax.ShapeDtypeStruct((T, D_MODEL), F32),
                   jax.ShapeDtypeStruct((T, N_INT), BF16),
                   jax.ShapeDtypeStruct((1, D_MODEL), F32)],
        compiler_params=_cparams(("arbitrary",)),
    )(x, g, w, dx_next, *pieces)


def _matmul_over_tokens(at, b, name):
    M, T = at.shape
    N = b.shape[1]
    tm, tn = min(1024, T), 512

    def body(a_ref, b_ref, o_ref):
        @pl.when(pl.program_id(1) == 0)
        def _():
            o_ref[...] = jnp.zeros_like(o_ref)

        o_ref[...] += _dot(a_ref[...], b_ref[...])

    return pl.pallas_call(
        body, name=name, grid=(N // tn, T // tm),
        in_specs=[pl.BlockSpec((M, tm), lambda j, t: (0, t)),
                  pl.BlockSpec((tm, tn), lambda j, t: (t, j))],
        out_specs=pl.BlockSpec((M, tn), lambda j, t: (0, j)),
        out_shape=jax.ShapeDtypeStruct((M, N), F32),
        compiler_params=_cparams(("parallel", "arbitrary")),
    )(at, b)


def _roll_f32(x, shift):
    return pltpu.roll(x.astype(F32), shift, 1)


def _swa_operands(h, q_ref, kp_ref, kc_ref, vp_ref, vc_ref):
    p, e = h // 2, h % 2
    lane = lax.broadcasted_iota(jnp.int32, (1, LANES), 1) // HEAD
    q = q_ref[:, p * LANES:(p + 1) * LANES]
    k_prev, k_cur, v_prev, v_cur = kp_ref[...], kc_ref[...], vp_ref[...], vc_ref[...]
    if e != p:
        q = _roll_f32(q, HEAD).astype(BF16)
        v_prev = _roll_f32(v_prev, HEAD).astype(BF16)
        v_cur = _roll_f32(v_cur, HEAD).astype(BF16)
    qs = jnp.where(lane == p, q, 0) * 0.125
    return dict(p=p, e=e, lane=lane, qs=qs, k_prev=k_prev, k_cur=k_cur,
                v_prev=jnp.where(lane == e, v_prev, 0), v_cur=jnp.where(lane == e, v_cur, 0),
                s_prev=_dot_nt(qs, k_prev), s_cur=_dot_nt(qs, k_cur))


def _swa_probs(ops, sink, first):
    row = lax.broadcasted_iota(jnp.int32, (BLOCK, BLOCK), 0)
    col = lax.broadcasted_iota(jnp.int32, (BLOCK, BLOCK), 1)
    s_prev = jnp.where(jnp.logical_and(col > row, jnp.logical_not(first)), ops["s_prev"], NEG)
    s_cur = jnp.where(col <= row, ops["s_cur"], NEG)
    m = jnp.maximum(jnp.maximum(jnp.max(s_prev, axis=1, keepdims=True),
                                jnp.max(s_cur, axis=1, keepdims=True)), sink)
    p_prev = jnp.exp(s_prev - m)
    p_cur = jnp.exp(s_cur - m)
    p_sink = jnp.exp(sink - m)
    inv = 1.0 / (jnp.sum(p_prev, axis=1, keepdims=True) + jnp.sum(p_cur, axis=1, keepdims=True) + p_sink)
    return p_prev * inv, p_cur * inv, p_sink * inv


def _swa_specs(T):
    nb = T // BLOCK
    qo, ko, vo = (_INT_OFF[n] // LANES for n in ("a_q", "a_k", "a_v"))
    prev = lambda i: jnp.maximum(i - 1, 0)
    return [pl.BlockSpec((BLOCK, 256), lambda i: (i, qo // 2)),
            pl.BlockSpec((BLOCK, LANES), lambda i: (prev(i), ko)),
            pl.BlockSpec((BLOCK, LANES), lambda i: (i, ko)),
            pl.BlockSpec((BLOCK, LANES), lambda i: (prev(i), vo)),
            pl.BlockSpec((BLOCK, LANES), lambda i: (i, vo)),
            pl.BlockSpec(memory_space=pltpu.SMEM)], nb


def _swa_fwd(hb, sinks):
    T = hb.shape[0]
    specs, nb = _swa_specs(T)

    def body(q_ref, kp_ref, kc_ref, vp_ref, vc_ref, s_ref, o_ref):
        first = pl.program_id(0) == 0
        ops = [_swa_operands(h, q_ref, kp_ref, kc_ref, vp_ref, vc_ref) for h in range(4)]
        probs = [_swa_probs(ops[h], s_ref[h], first) for h in range(4)]
        outs = [_dot(probs[h][0].astype(BF16), ops[h]["v_prev"]) + _dot(probs[h][1].astype(BF16), ops[h]["v_cur"])
                for h in range(4)]
        for p in range(2):
            o_ref[:, p * LANES:(p + 1) * LANES] = outs[2 * p] + outs[2 * p + 1]

    return pl.pallas_call(
        body, name="swa_fwd", grid=(nb,), in_specs=specs,
        out_specs=pl.BlockSpec((BLOCK, 256), lambda i: (i, 0)),
        out_shape=jax.ShapeDtypeStruct((T, 256), F32),
        compiler_params=_cparams(("parallel",)),
    )(hb, hb, hb, hb, hb, sinks)


def _swa_bwd(hb, sinks, dy):
    T = hb.shape[0]
    specs, nb = _swa_specs(T)

    def body(q_ref, kp_ref, kc_ref, vp_ref, vc_ref, s_ref, dy_ref, dq_ref, dk_ref, dv_ref, ds_ref):
        i = pl.program_id(0)
        first = i == 0
        cur = pl.ds(pl.multiple_of(i * BLOCK, BLOCK), BLOCK)
        prv = pl.ds(pl.multiple_of(jnp.maximum(i - 1, 0) * BLOCK, BLOCK), BLOCK)

        @pl.when(first)
        def _():
            ds_ref[...] = jnp.zeros_like(ds_ref)

        dk_ref[cur, :] = jnp.zeros((BLOCK, LANES), F32)
        dv_ref[cur, :] = jnp.zeros((BLOCK, LANES), F32)
        lane_id = lax.broadcasted_iota(jnp.int32, (8, LANES), 1)
        heads = range(4)
        ops = [_swa_operands(h, q_ref, kp_ref, kc_ref, vp_ref, vc_ref) for h in heads]
        probs = [_swa_probs(ops[h], s_ref[h], first) for h in heads]
        dos = [jnp.where(ops[h]["lane"] == ops[h]["e"], dy_ref[:, ops[h]["p"] * LANES:(ops[h]["p"] + 1) * LANES], 0.0)
               for h in heads]
        dobs = [d.astype(BF16) for d in dos]
        pbs = [(probs[h][0].astype(BF16), probs[h][1].astype(BF16)) for h in heads]
        outs = [_dot(pbs[h][0], ops[h]["v_prev"]) + _dot(pbs[h][1], ops[h]["v_cur"]) for h in heads]
        dps = [(_dot_nt(dobs[h], ops[h]["v_prev"]), _dot_nt(dobs[h], ops[h]["v_cur"])) for h in heads]
        dss, dsinks = [], jnp.zeros((8, LANES), F32)
        for h in heads:
            delta = jnp.sum(dos[h] * outs[h], axis=1, keepdims=True)
            dss.append(((probs[h][0] * (dps[h][0] - delta)).astype(BF16),
                        (probs[h][1] * (dps[h][1] - delta)).astype(BF16)))
            dsink = -jnp.sum(probs[h][2] * delta, axis=0, keepdims=True)
            dsinks += jnp.where(lane_id == h, dsink, 0.0)
        ds_ref[...] += dsinks
        dqs = [(_dot(dss[h][0], ops[h]["k_prev"]) + _dot(dss[h][1], ops[h]["k_cur"])) * 0.125 for h in heads]
        dk_prev = dk_cur = dv_prev = dv_cur = jnp.zeros((BLOCK, LANES), F32)
        for h in heads:
            p, e = ops[h]["p"], ops[h]["e"]
            dob_v = dobs[h] if e == p else pltpu.roll(dos[h], HEAD, 1).astype(BF16)
            dk_prev += _dot_tn(dss[h][0], ops[h]["qs"])
            dk_cur += _dot_tn(dss[h][1], ops[h]["qs"])
            dv_prev += _dot_tn(pbs[h][0], dob_v)
            dv_cur += _dot_tn(pbs[h][1], dob_v)
        dk_ref[prv, :] += dk_prev
        dk_ref[cur, :] += dk_cur
        dv_ref[prv, :] += dv_prev
        dv_ref[cur, :] += dv_cur
        for p in range(2):
            dq_pair = jnp.zeros((BLOCK, LANES), F32)
            for e in range(2):
                dq = jnp.where(ops[2 * p + e]["lane"] == p, dqs[2 * p + e], 0.0)
                dq_pair += dq if e == p else pltpu.roll(dq, HEAD, 1)
            dq_ref[:, p * LANES:(p + 1) * LANES] = dq_pair.astype(BF16)

    return pl.pallas_call(
        body, name="swa_bwd", grid=(nb,),
        in_specs=specs + [pl.BlockSpec((BLOCK, 256), lambda i: (i, 0))],
        out_specs=[pl.BlockSpec((BLOCK, 256), lambda i: (i, 0)),
                   pl.BlockSpec((T, LANES), lambda i: (0, 0)),
                   pl.BlockSpec((T, LANES), lambda i: (0, 0)),
                   pl.BlockSpec((8, LANES), lambda i: (0, 0))],
        out_shape=[jax.ShapeDtypeStruct((T, 256), BF16),
                   jax.ShapeDtypeStruct((T, LANES), F32),
                   jax.ShapeDtypeStruct((T, LANES), F32),
                   jax.ShapeDtypeStruct((8, LANES), F32)],
        compiler_params=_cparams(("arbitrary",)),
    )(hb, hb, hb, hb, hb, sinks, dy)


def _rope_tables(pos_ref):
    lane = lax.broadcasted_iota(jnp.int32, (1, LANES), 1)
    active = jnp.logical_and(lane >= HEAD, lane < HEAD + 2 * ROPE_HALF)
    idx = ((lane - HEAD) % ROPE_HALF).astype(F32)
    freq = jnp.exp(idx * (-math.log(ROPE_THETA) / ROPE_HALF))
    ang = pos_ref[...].astype(F32) * freq
    cos, sin = jnp.cos(ang), jnp.sin(ang)
    c = jnp.where(active, cos, 1.0)
    s_up = jnp.where(jnp.logical_and(active, lane >= HEAD + ROPE_HALF), sin, 0.0)
    s_dn = jnp.where(jnp.logical_and(active, lane < HEAD + ROPE_HALF), -sin, 0.0)
    return c, s_up, s_dn


def _rope(x, tabs):
    c, s_up, s_dn = tabs
    return x * c + pltpu.roll(x, ROPE_HALF, 1) * s_up + pltpu.roll(x, LANES - ROPE_HALF, 1) * s_dn


def _rope_t(dy, tabs):
    c, s_up, s_dn = tabs
    return dy * c + pltpu.roll(dy * s_up, LANES - ROPE_HALF, 1) + pltpu.roll(dy * s_dn, ROPE_HALF, 1)


def _mla_lat_specs(tm):
    cq, ckv, ckr = ((_INT_OFF[n] - N_HB) for n in ("c_q", "c_kv", "c_kr"))
    return [pl.BlockSpec((tm, 256), lambda i: (i, cq // 256)),
            pl.BlockSpec((tm, LANES), lambda i: (i, ckv // LANES)),
            pl.BlockSpec((tm, LANES), lambda i: (i, ckr // LANES)),
            pl.BlockSpec((tm, 1), lambda i: (i, 0)),
            pl.BlockSpec((1, 256), lambda i: (0, 0)),
            pl.BlockSpec((1, LANES), lambda i: (0, 0)),
            pl.BlockSpec((256, 512), lambda i: (0, 0)),
            pl.BlockSpec((LANES, 768), lambda i: (0, 0))]


def _mla_prep_fwd(hf, pos, g_q, g_kv, w_uq, w_ukv):
    T = hf.shape[0]
    tm = 512
    sub = tm // ATT_BLK

    def body(cq_ref, ckv_ref, ckr_ref, pos_ref, gq_ref, gkv_ref, wq_ref, wkv_ref, qm_ref, km_ref, vm_ref, vt_ref):
        tabs = _rope_tables(pos_ref)
        cq = cq_ref[...]
        q = _dot((cq * _rms(cq) * gq_ref[...]).astype(BF16), wq_ref[...])
        ckv = ckv_ref[...]
        kv = _dot((ckv * _rms(ckv) * gkv_ref[...]).astype(BF16), wkv_ref[...])
        kr = _rope(pltpu.roll(ckr_ref[...], HEAD, 1), tabs)
        for h in range(4):
            sl = slice(h * LANES, (h + 1) * LANES)
            qm_ref[:, sl] = (_rope(q[:, sl], tabs) * MLA_QSCALE).astype(BF16)
            km_ref[:, sl] = (kv[:, sl] + kr).astype(BF16)
        vm_ref[...] = kv[:, 512:].astype(BF16)
        for p in range(2):
            for s in range(sub):
                tile = kv[s * ATT_BLK:(s + 1) * ATT_BLK, 512 + p * LANES:512 + (p + 1) * LANES]
                vt_ref[p, s] = jnp.transpose(tile).astype(BF16)

    return pl.pallas_call(
        body, name="mla_prep_fwd", grid=(T // tm,), in_specs=_mla_lat_specs(tm),
        out_specs=[pl.BlockSpec((tm, 512), lambda i: (i, 0)),
                   pl.BlockSpec((tm, 512), lambda i: (i, 0)),
                   pl.BlockSpec((tm, 256), lambda i: (i, 0)),
                   pl.BlockSpec((2, sub, LANES, ATT_BLK), lambda i: (0, i, 0, 0))],
        out_shape=[jax.ShapeDtypeStruct((T, 512), BF16),
                   jax.ShapeDtypeStruct((T, 512), BF16),
                   jax.ShapeDtypeStruct((T, 256), BF16),
                   jax.ShapeDtypeStruct((2, T // ATT_BLK, LANES, ATT_BLK), BF16)],
        compiler_params=_cparams(("parallel",)),
    )(hf, hf, hf, pos, g_q, g_kv, w_uq, w_ukv)


def _mla_prep_bwd(hf, pos, g_q, g_kv, w_uq, w_ukv, dqm, dkt, dvt):
    T = hf.shape[0]
    tm = 512
    sub = tm // ATT_BLK

    def body(cq_ref, ckv_ref, ckr_ref, pos_ref, gq_ref, gkv_ref, wq_ref, wkv_ref, dq_ref, dk_ref, dv_ref,
             dc_ref, dwq_ref, dwkv_ref, dgq_ref, dgkv_ref):
        @pl.when(pl.program_id(0) == 0)
        def _():
            dwq_ref[...] = jnp.zeros_like(dwq_ref)
            dwkv_ref[...] = jnp.zeros_like(dwkv_ref)
            dgq_ref[...] = jnp.zeros_like(dgq_ref)
            dgkv_ref[...] = jnp.zeros_like(dgkv_ref)

        tabs = _rope_tables(pos_ref)
        lane = lax.broadcasted_iota(jnp.int32, (1, LANES), 1)
        dq = jnp.concatenate([_rope_t(dq_ref[:, h * LANES:(h + 1) * LANES] * MLA_QSCALE, tabs)
                              for h in range(4)], axis=1).astype(BF16)
        cq = cq_ref[...]
        rq = _rms(cq)
        cqn = (cq * rq * gq_ref[...]).astype(BF16)
        dwq_ref[...] += _dot_tn(cqn, dq)
        dcq, dgrow = _rms_bwd(_dot_nt(dq, wq_ref[...]), cq * rq, rq, gq_ref[...])
        dgq_ref[...] += _colsum(dgrow)
        dc_ref[:, 0:256] = dcq.astype(BF16)

        dk = jnp.concatenate([jnp.concatenate([jnp.transpose(dk_ref[p, s]) for p in range(2)], axis=1)
                              for s in range(sub)], axis=0) * LN2
        dv = jnp.concatenate([jnp.concatenate([jnp.transpose(dv_ref[p, s]) for p in range(2)], axis=1)
                              for s in range(sub)], axis=0)
        dkr = dk[:, 0:LANES] + dk[:, LANES:2 * LANES] + dk[:, 2 * LANES:3 * LANES] + dk[:, 3 * LANES:]
        dkr = pltpu.roll(_rope_t(dkr, tabs), HEAD, 1)
        dc_ref[:, 384:512] = jnp.where(lane < 2 * ROPE_HALF, dkr, 0.0).astype(BF16)
        dkv = jnp.concatenate([dk.astype(BF16), dv.astype(BF16)], axis=1)
        ckv = ckv_ref[...]
        rkv = _rms(ckv)
        ckvn = (ckv * rkv * gkv_ref[...]).astype(BF16)
        dwkv_ref[...] += _dot_tn(ckvn, dkv)
        dckv, dgrow = _rms_bwd(_dot_nt(dkv, wkv_ref[...]), ckv * rkv, rkv, gkv_ref[...])
        dgkv_ref[...] += _colsum(dgrow)
        dc_ref[:, 256:384] = dckv.astype(BF16)

    return pl.pallas_call(
        body, name="mla_prep_bwd", grid=(T // tm,),
        in_specs=_mla_lat_specs(tm) + [pl.BlockSpec((tm, 512), lambda i: (i, 0)),
                                       pl.BlockSpec((2, sub, 256, ATT_BLK), lambda i: (0, i, 0, 0)),
                                       pl.BlockSpec((2, sub, LANES, ATT_BLK), lambda i: (0, i, 0, 0))],
        out_specs=[pl.BlockSpec((tm, 512), lambda i: (i, 0)),
                   pl.BlockSpec((256, 512), lambda i: (0, 0)),
                   pl.BlockSpec((LANES, 768), lambda i: (0, 0)),
                   pl.BlockSpec((1, 256), lambda i: (0, 0)),
                   pl.BlockSpec((1, LANES), lambda i: (0, 0))],
        out_shape=[jax.ShapeDtypeStruct((T, 512), BF16),
                   jax.ShapeDtypeStruct((256, 512), F32),
                   jax.ShapeDtypeStruct((LANES, 768), F32),
                   jax.ShapeDtypeStruct((1, 256), F32),
                   jax.ShapeDtypeStruct((1, LANES), F32)],
        compiler_params=_cparams(("arbitrary",)),
    )(hf, hf, hf, pos, g_q, g_kv, w_uq, w_ukv, dqm, dkt, dvt)


def _causal_masks(bq, bk):
    row = lax.broadcasted_iota(jnp.int32, (bq, bk), 0)
    col = lax.broadcasted_iota(jnp.int32, (bq, bk), 1)
    return row, col


def _mla_fwd(qm, km, vt):
    T = qm.shape[0]
    bq, bk = min(MLA_BQ, T), ATT_BLK
    nq, nsub, nk = T // bq, bq // bk, T // bk

    def body(q_ref, k_ref, vt_ref, o_ref, lse_ref, acc_ref, m_ref, l_ref):
        qi = pl.program_id(1)
        key = lax.broadcasted_iota(jnp.int32, (bk, bq), 0)
        qry = lax.broadcasted_iota(jnp.int32, (bk, bq), 1)
        ones = jnp.ones((8, bk), BF16)
        acc_ref[...] = jnp.zeros_like(acc_ref)
        m_ref[...] = jnp.full_like(m_ref, NEG)
        l_ref[...] = jnp.zeros_like(l_ref)

        def step(kb0, masked):
            kbs = [kb0 + d for d in range(nsub)]
            sts = [[_dot_nt(k_ref[pl.ds(pl.multiple_of(kb * bk, bk), bk), e * LANES:(e + 1) * LANES],
                            q_ref[:, e * LANES:(e + 1) * LANES]) for kb in kbs] for e in range(2)]
            pts, alphas = [], []
            for e in range(2):
                st = [jnp.where(key + d * bk <= qry, sts[e][d], NEG) for d in range(nsub)] if masked else sts[e]
                m_prev = m_ref[e, 0:1, :]
                m_new = m_prev
                for d in range(nsub):
                    m_new = jnp.maximum(m_new, jnp.max(st[d], axis=0, keepdims=True))
                alpha = jnp.exp2(m_prev - m_new)
                pt = [jnp.exp2(st[d] - m_new).astype(BF16) for d in range(nsub)]
                l_new = alpha * l_ref[e]
                for d in range(nsub):
                    l_new = l_new + _dot(ones, pt[d])
                l_ref[e] = l_new
                m_ref[e] = jnp.broadcast_to(m_new, (8, bq))
                pts.append(pt)
                alphas.append(alpha)
            for e in range(2):
                acc = alphas[e] * acc_ref[e]
                for d in range(nsub):
                    acc = acc + _dot(vt_ref[0, kbs[d], e * HEAD:(e + 1) * HEAD, :], pts[e][d])
                acc_ref[e] = acc

        step(qi * nsub, True)

        def loop(t, c):
            step(t * nsub, False)
            return c

        lax.fori_loop(0, qi, loop, 0)
        outs, lses = [], []
        for e in range(2):
            l = l_ref[e, 0:1, :]
            outs.append(acc_ref[e] / l)
            lses.append(jnp.broadcast_to(m_ref[e, 0:1, :] * LN2 + jnp.log(l), (HEAD, bq)))
        o_ref[...] = jnp.transpose(jnp.concatenate(outs, axis=0))
        lse_ref[...] = jnp.transpose(jnp.concatenate(lses, axis=0))

    return pl.pallas_call(
        body, name="mla_fwd", grid=(2, nq),
        in_specs=[pl.BlockSpec((bq, 256), lambda j, i: (i, j)),
                  pl.BlockSpec((T, 256), lambda j, i: (0, j)),
                  pl.BlockSpec((1, nk, LANES, bk), lambda j, i: (j, 0, 0, 0))],
        out_specs=[pl.BlockSpec((bq, LANES), lambda j, i: (i, j)),
                   pl.BlockSpec((bq, LANES), lambda j, i: (i, j))],
        out_shape=[jax.ShapeDtypeStruct((T, 256), F32), jax.ShapeDtypeStruct((T, 256), F32)],
        scratch_shapes=[pltpu.VMEM((2, HEAD, bq), F32), pltpu.VMEM((2, 8, bq), F32), pltpu.VMEM((2, 8, bq), F32)],
        compiler_params=_cparams(("parallel", "arbitrary")),
    )(qm, km, vt)


def _mla_bwd(qm, km, vm, y, lse, dy):
    T = qm.shape[0]
    bq, bk = min(MLA_BQ, T), ATT_BLK
    nq, nsub, nk = T // bq, bq // bk, T // bk

    def body(q_ref, k_ref, v_ref, y_ref, lse_ref, dy_ref, dq_ref, dkt_ref, dvt_ref, dob_ref, st_ref, qt_ref, dot_ref):
        qi = pl.program_id(1)

        @pl.when(qi == 0)
        def _():
            dkt_ref[...] = jnp.zeros_like(dkt_ref)
            dvt_ref[...] = jnp.zeros_like(dvt_ref)

        lane = lax.broadcasted_iota(jnp.int32, (1, LANES), 1) // HEAD
        row, col = _causal_masks(bq, bk)
        dq_ref[...] = jnp.zeros_like(dq_ref)
        lse = lse_ref[...]
        lse_other = pltpu.roll(lse, HEAD, 1)
        qt_ref[...] = jnp.transpose(q_ref[...].astype(F32)).astype(BF16)
        dot_ref[...] = jnp.transpose(dy_ref[...]).astype(BF16)
        for e in range(2):
            do = jnp.where(lane == e, dy_ref[...], 0.0)
            dob_ref[e] = do.astype(BF16)
            st_ref[2 * e] = jnp.where(lane == e, lse, lse_other) * LOG2E
            st_ref[2 * e + 1] = jnp.broadcast_to(jnp.sum(do * y_ref[...], axis=1, keepdims=True), (bq, LANES))

        hss = [slice(e * LANES, (e + 1) * LANES) for e in range(2)]
        tile = lambda a: jnp.concatenate([a] * (bk // LANES), axis=1)

        def step(kb0, masked):
            kbs = [kb0 + d for d in range(nsub)]
            rows = [pl.ds(pl.multiple_of(kb * bk, bk), bk) for kb in kbs]
            pairs = [(d, e) for d in range(nsub) for e in range(2)]
            ss = {(d, e): _dot_nt(q_ref[:, hss[e]], k_ref[rows[d], hss[e]]) for d, e in pairs}
            dps = {(d, e): _dot_nt(dob_ref[e], jnp.where(lane == e, v_ref[rows[d], :], 0)) for d, e in pairs}
            ps, dss = {}, {}
            for d, e in pairs:
                s = jnp.where(col + d * bk <= row, ss[d, e], NEG) if masked else ss[d, e]
                p = jnp.exp2(s - tile(st_ref[2 * e]))
                dss[d, e] = (p * (dps[d, e] - tile(st_ref[2 * e + 1]))).astype(BF16)
                ps[d, e] = p.astype(BF16)
            for d, e in pairs:
                dvt_ref[0, kbs[d], e * HEAD:(e + 1) * HEAD, :] += _dot(dot_ref[e * HEAD:(e + 1) * HEAD, :], ps[d, e])
            for d, e in pairs:
                dkt_ref[0, kbs[d], hss[e], :] += _dot(qt_ref[hss[e], :], dss[d, e])
            for e in range(2):
                dq = dq_ref[:, hss[e]]
                for d in range(nsub):
                    dq = dq + _dot(dss[d, e], k_ref[rows[d], hss[e]])
                dq_ref[:, hss[e]] = dq

        step(qi * nsub, True)

        def loop(t, c):
            step(t * nsub, False)
            return c

        lax.fori_loop(0, qi, loop, 0)
        dq_ref[...] *= LN2

    return pl.pallas_call(
        body, name="mla_bwd", grid=(2, nq),
        in_specs=[pl.BlockSpec((bq, 256), lambda j, i: (i, j)),
                  pl.BlockSpec((T, 256), lambda j, i: (0, j)),
                  pl.BlockSpec((T, LANES), lambda j, i: (0, j)),
                  pl.BlockSpec((bq, LANES), lambda j, i: (i, j)),
                  pl.BlockSpec((bq, LANES), lambda j, i: (i, j)),
                  pl.BlockSpec((bq, LANES), lambda j, i: (i, j))],
        out_specs=[pl.BlockSpec((bq, 256), lambda j, i: (i, j)),
                   pl.BlockSpec((1, nk, 256, bk), lambda j, i: (j, 0, 0, 0)),
                   pl.BlockSpec((1, nk, LANES, bk), lambda j, i: (j, 0, 0, 0))],
        out_shape=[jax.ShapeDtypeStruct((T, 512), F32),
                   jax.ShapeDtypeStruct((2, nk, 256, bk), F32),
                   jax.ShapeDtypeStruct((2, nk, LANES, bk), F32)],
        scratch_shapes=[pltpu.VMEM((2, bq, LANES), BF16), pltpu.VMEM((4, bq, LANES), F32),
                        pltpu.VMEM((256, bq), BF16), pltpu.VMEM((LANES, bq), BF16)],
        compiler_params=_cparams(("parallel", "arbitrary")),
    )(qm, km, vm, y, lse, dy)


def _suffix_ones(n):
    r = lax.broadcasted_iota(jnp.int32, (n, n), 0)
    c = lax.broadcasted_iota(jnp.int32, (n, n), 1)
    return (r >= c).astype(BF16)


def _prefix_ones(n):
    r = lax.broadcasted_iota(jnp.int32, (n, n), 0)
    c = lax.broadcasted_iota(jnp.int32, (n, n), 1)
    return (r <= c).astype(BF16)


def _tri_sum(x, u):
    hi, lo = _split(x)
    return _dot(hi, u) + _dot(lo, u)


def _sb_specs(T, bq):
    qo, ko, vo = (_INT_OFF[n] // LANES for n in ("d_q", "d_k", "d_v"))
    return [pl.BlockSpec((bq, LANES), lambda j, i: (i, qo + j)),
            pl.BlockSpec((T, LANES), lambda j, i: (0, ko + j)),
            pl.BlockSpec((T, LANES), lambda j, i: (0, vo + j))]


def _sb_fwd(hb):
    T = hb.shape[0]
    bq = bk = ATT_BLK
    nq = T // bq

    def body(q_ref, k_ref, v_ref, o_ref, tot_ref, cnt_ref, qm_ref, car_ref):
        qi = pl.program_id(1)
        lane = lax.broadcasted_iota(jnp.int32, (1, LANES), 1) // HEAD
        row, col = _causal_masks(bq, bk)
        strict = col < row
        u = _suffix_ones(bk)
        o_ref[...] = jnp.zeros_like(o_ref)
        car_ref[...] = jnp.zeros_like(car_ref)
        for e in range(2):
            qm_ref[e] = jnp.where(lane == e, q_ref[...], 0) * 0.125

        def step(blocks):
            tile = lambda a: jnp.concatenate([a] * (bk // LANES), axis=1)
            rows = [pl.ds(pl.multiple_of(kb * bk, bk), bk) for kb, _ in blocks]
            pairs = [(b, e) for b in range(len(blocks)) for e in range(2)]
            zs = {(b, e): _dot_nt(qm_ref[e], k_ref[rows[b], :]) for b, e in pairs}
            splits = {}
            for b, e in pairs:
                z = zs[b, e]
                lk = jnp.minimum(-z, 0.0) - jnp.log(1.0 + jnp.exp(-jnp.abs(z)))
                if blocks[b][1] is not None:
                    lk = jnp.where(blocks[b][1], lk, 0.0)
                splits[b, e] = _split(lk)
            sufs = {be: _dot(hi, u) + _dot(lo, u) for be, (hi, lo) in splits.items()}
            car = [car_ref[0], car_ref[1]]
            aas = {}
            for b, e in pairs:
                a = jnp.exp(zs[b, e] + sufs[b, e] + tile(car[e]))
                if blocks[b][1] is not None:
                    a = jnp.where(blocks[b][1], a, 0.0)
                aas[b, e] = a.astype(BF16)
                car[e] = car[e] + jnp.broadcast_to(sufs[b, e][:, 0:1], (bq, LANES))
            acc = o_ref[...]
            for b, e in pairs:
                acc = acc + _dot(aas[b, e], jnp.where(lane == e, v_ref[rows[b], :], 0))
            o_ref[...] = acc
            car_ref[0], car_ref[1] = car

        step([(qi, strict), (jnp.maximum(qi - 1, 0), qi > 0)])

        def live():
            return jnp.max(jnp.maximum(car_ref[0], car_ref[1])) >= SB_DEAD

        def cond(c):
            return jnp.logical_and(c[0] < qi, c[1])

        def loop(c):
            step([(qi - 1 - c[0], None)])
            return c[0] + 1, live()

        done, _ = lax.while_loop(cond, loop, (jnp.minimum(qi, 1), live()))
        tot_ref[...] = jnp.where(lane == 0, car_ref[0], car_ref[1])
        cnt_ref[pl.program_id(0), qi] = done.astype(F32)

    return pl.pallas_call(
        body, name="sb_fwd", grid=(2, nq), in_specs=_sb_specs(T, bq),
        out_specs=[pl.BlockSpec((bq, LANES), lambda j, i: (i, j)), pl.BlockSpec((bq, LANES), lambda j, i: (i, j)),
                   pl.BlockSpec(memory_space=pltpu.SMEM)],
        out_shape=[jax.ShapeDtypeStruct((T, 256), F32), jax.ShapeDtypeStruct((T, 256), F32),
                   jax.ShapeDtypeStruct((2, nq), F32)],
        scratch_shapes=[pltpu.VMEM((2, bq, LANES), BF16), pltpu.VMEM((2, bq, LANES), F32)],
        compiler_params=_cparams(("parallel", "arbitrary")),
    )(hb, hb, hb)


def _sb_bwd(hb, tot, cnt, dy):
    T = hb.shape[0]
    bq = bk = ATT_BLK
    nq = T // bq

    def body(q_ref, k_ref, v_ref, tot_ref, dy_ref, cnt_ref, dq_ref, dk_ref, dv_ref, qm_ref, dob_ref, dqa_ref, rem_ref,
             cg_ref):
        qi = pl.program_id(1)

        @pl.when(qi == 0)
        def _():
            dk_ref[...] = jnp.zeros_like(dk_ref)
            dv_ref[...] = jnp.zeros_like(dv_ref)

        lane = lax.broadcasted_iota(jnp.int32, (1, LANES), 1) // HEAD
        row, col = _causal_masks(bq, bk)
        strict = col < row
        u = _prefix_ones(bk)
        tot = tot_ref[...]
        tot_other = pltpu.roll(tot, HEAD, 1)
        dqa_ref[...] = jnp.zeros_like(dqa_ref)
        cg_ref[...] = jnp.zeros_like(cg_ref)
        for e in range(2):
            qm_ref[e] = jnp.where(lane == e, q_ref[...], 0) * 0.125
            dob_ref[e] = jnp.where(lane == e, dy_ref[...], 0.0).astype(BF16)
            rem_ref[e] = jnp.where(lane == e, tot, tot_other)

        def step(blocks):
            tile = lambda a: jnp.concatenate([a] * (bk // LANES), axis=1)
            nb = len(blocks)
            rows = [pl.ds(pl.multiple_of(kb * bk, bk), bk) for kb, _ in blocks]
            pairs = [(b, e) for b in range(nb) for e in range(2)]
            mask = lambda b, x: x if blocks[b][1] is None else jnp.where(blocks[b][1], x, 0.0)
            zs = {(b, e): _dot_nt(qm_ref[e], k_ref[rows[b], :]) for b, e in pairs}
            das = {(b, e): _dot_nt(dob_ref[e], jnp.where(lane == e, v_ref[rows[b], :], 0)) for b, e in pairs}
            zls, splits = {}, {}
            for b, e in pairs:
                z = zs[b, e]
                lk = mask(b, jnp.minimum(-z, 0.0) - jnp.log(1.0 + jnp.exp(-jnp.abs(z))))
                zls[b, e] = z + lk
                splits[b, e] = _split(lk)
            pres = {be: _dot(hi, u) + _dot(lo, u) for be, (hi, lo) in splits.items()}
            rem = [rem_ref[0], rem_ref[1]]
            aas, gs, gsplits = {}, {}, {}
            for b, e in pairs:
                a = mask(b, jnp.exp(zls[b, e] + (tile(rem[e]) - pres[b, e])))
                gs[b, e] = a * das[b, e]
                aas[b, e] = a.astype(BF16)
                gsplits[b, e] = _split(gs[b, e])
                rem[e] = rem[e] - jnp.broadcast_to(pres[b, e][:, bk - 1:bk], (bq, LANES))
            for b in range(nb):
                dv_ref[rows[b], :] += _dot_tn(aas[b, 0], dob_ref[0]) + _dot_tn(aas[b, 1], dob_ref[1])
            gpres = {be: _dot(hi, u) + _dot(lo, u) for be, (hi, lo) in gsplits.items()}
            cg = [cg_ref[0], cg_ref[1]]
            dzs = {}
            for b, e in pairs:
                dz = mask(b, gs[b, e] - jnp.exp(zls[b, e]) * (tile(cg[e]) + gpres[b, e]))
                dzs[b, e] = dz.astype(BF16)
                cg[e] = cg[e] + jnp.broadcast_to(gpres[b, e][:, bk - 1:bk], (bq, LANES))
            for b in range(nb):
                dk_ref[rows[b], :] += _dot_tn(dzs[b, 0], qm_ref[0]) + _dot_tn(dzs[b, 1], qm_ref[1])
            for e in range(2):
                dq = dqa_ref[e]
                for b in range(nb):
                    dq = dq + _dot(dzs[b, e], k_ref[rows[b], :])
                dqa_ref[e] = dq
            rem_ref[0], rem_ref[1] = rem
            cg_ref[0], cg_ref[1] = cg

        def loop(kb, c):
            step([(kb, None)])
            return c

        start = qi - jnp.clip(cnt_ref[pl.program_id(0), qi].astype(jnp.int32), 0, qi)
        lax.fori_loop(start, qi - 1, loop, 0)
        step([(jnp.maximum(qi - 1, 0), qi > 0), (qi, strict)])
        dq_ref[...] = (jnp.where(lane == 0, dqa_ref[0], dqa_ref[1]) * 0.125).astype(BF16)

    return pl.pallas_call(
        body, name="sb_bwd", grid=(2, nq),
        in_specs=_sb_specs(T, bq) + [pl.BlockSpec((bq, LANES), lambda j, i: (i, j)),
                                     pl.BlockSpec((bq, LANES), lambda j, i: (i, j)),
                                     pl.BlockSpec(memory_space=pltpu.SMEM)],
        out_specs=[pl.BlockSpec((bq, LANES), lambda j, i: (i, j)),
                   pl.BlockSpec((T, LANES), lambda j, i: (0, j)),
                   pl.BlockSpec((T, LANES), lambda j, i: (0, j))],
        out_shape=[jax.ShapeDtypeStruct((T, 256), BF16)] + [jax.ShapeDtypeStruct((T, 256), F32)] * 2,
        scratch_shapes=[pltpu.VMEM((2, bq, LANES), BF16), pltpu.VMEM((2, bq, LANES), BF16),
                        pltpu.VMEM((2, bq, LANES), F32), pltpu.VMEM((2, bq, LANES), F32),
                        pltpu.VMEM((2, bq, LANES), F32)],
        compiler_params=_cparams(("parallel", "arbitrary")),
    )(hb, hb, hb, tot, dy, cnt)


EP_TM = 256


def _ep_in_specs(tm, rev):
    idx = (lambda i: rev - i) if rev is not None else (lambda i: i)
    bo = (_INT_OFF["b_b"] - N_HB) // 256
    halo = lambda i: jnp.maximum(idx(i) * (tm // 8) - 1, 0)
    return [pl.BlockSpec((tm, 256), lambda i: (idx(i), 0)),
            pl.BlockSpec((tm, 256), lambda i: (idx(i), 0)),
            pl.BlockSpec((tm, 256), lambda i: (idx(i), 0)),
            pl.BlockSpec((tm, D_MODEL), lambda i: (idx(i), 0)),
            pl.BlockSpec((tm, 256), lambda i: (idx(i), bo)),
            pl.BlockSpec((tm, 256), lambda i: (idx(i), bo + 1)),
            pl.BlockSpec((tm, 256), lambda i: (idx(i), bo + 2)),
            pl.BlockSpec((8, 256), lambda i: (halo(i), bo + 1)),
            pl.BlockSpec((8, 256), lambda i: (halo(i), bo + 2)),
            pl.BlockSpec((3, 256), lambda i: (0, 0)),
            pl.BlockSpec((1, 256), lambda i: (0, 0)),
            pl.BlockSpec((1, D_MODEL), lambda i: (0, 0)),
            pl.BlockSpec((D_MODEL, D_MODEL), lambda i: (0, 0)),
            pl.BlockSpec((1, D_MODEL), lambda i: (0, 0))]


def _ep_mix(first, ya_ref, yc_ref, yd_ref, gate_ref, bb_ref, bc_ref, bx_ref, hc_ref, hx_ref, cw_ref, cb_ref, gg_ref):
    tm = ya_ref.shape[0]
    u = bc_ref[...] * bx_ref[...]
    halo = jnp.where(first, 0.0, hc_ref[...] * hx_ref[...])
    row = lax.broadcasted_iota(jnp.int32, (tm, 1), 0)
    u1 = jnp.where(row == 0, halo[7:8, :], pltpu.roll(u, 1, 0))
    u2 = jnp.where(row == 0, halo[6:7, :], jnp.where(row == 1, halo[7:8, :], pltpu.roll(u, 2, 0)))
    cw = cw_ref[...]
    conv = cw[0:1, :] * u2 + cw[1:2, :] * u1 + cw[2:3, :] * u + cb_ref[...]
    bb = bb_ref[...]
    ys = [ya_ref[...], bb * conv, yc_ref[...], yd_ref[...]]
    rs = [_rms(y) for y in ys]
    gg = gg_ref[...]
    yhat = jnp.concatenate([y * r for y, r in zip(ys, rs)], axis=1)
    gate = gate_ref[...]
    sig = 1.0 / (1.0 + jnp.exp(-gate))
    return u, u1, u2, conv, bb, rs, yhat, yhat * gg, gate, sig


def _epilogue_fwd(x, ya, yc, yd, hf, conv_w, conv_b, g_grp, w_out, g_post):
    T = x.shape[0]
    tm = EP_TM

    def body(x_ref, ya_ref, yc_ref, yd_ref, gate_ref, bb_ref, bc_ref, bx_ref, hc_ref, hx_ref, cw_ref, cb_ref,
             gg_ref, wo_ref, gp_ref, o_ref):
        (_, _, _, _, _, _, _, yn, gate, sig) = _ep_mix(
            pl.program_id(0) == 0, ya_ref, yc_ref, yd_ref, gate_ref, bb_ref, bc_ref, bx_ref, hc_ref, hx_ref,
            cw_ref, cb_ref, gg_ref)
        z = _dot((yn * (gate * sig)).astype(BF16), wo_ref[...])
        o_ref[...] = x_ref[...] + z * _rms(z) * gp_ref[...]

    return pl.pallas_call(
        body, name="epilogue_fwd", grid=(T // tm,),
        in_specs=[pl.BlockSpec((tm, D_MODEL), lambda i: (i, 0))] + _ep_in_specs(tm, None),
        out_specs=pl.BlockSpec((tm, D_MODEL), lambda i: (i, 0)),
        out_shape=jax.ShapeDtypeStruct((T, D_MODEL), F32),
        compiler_params=_cparams(("parallel",)),
    )(x, ya, yc, yd, hf, hf, hf, hf, hf, hf, conv_w, conv_b, g_grp, w_out, g_post)


def _epilogue_bwd(dxn, ya, yc, yd, hf, conv_w, conv_b, g_grp, w_out, g_post):
    T = dxn.shape[0]
    tm = EP_TM
    nt = T // tm
    ridx = lambda i: (nt - 1 - i, 0)

    def body(dx_ref, ya_ref, yc_ref, yd_ref, gate_ref, bb_ref, bc_ref, bx_ref, hc_ref, hx_ref, cw_ref, cb_ref,
             gg_ref, wo_ref, gp_ref,
             dya_ref, dyc_ref, dyd_ref, dhf_ref, dwo_ref, dgp_ref, dgg_ref, dcw_ref, dcb_ref, carry_ref):
        i = pl.program_id(0)

        @pl.when(i == 0)
        def _():
            for r in (dwo_ref, dgp_ref, dgg_ref, dcw_ref, dcb_ref, carry_ref):
                r[...] = jnp.zeros_like(r)

        (u, u1, u2, conv, bb, rs, yhat, yn, gate, sig) = _ep_mix(
            i == nt - 1, ya_ref, yc_ref, yd_ref, gate_ref, bb_ref, bc_ref, bx_ref, hc_ref, hx_ref,
            cw_ref, cb_ref, gg_ref)
        silu = gate * sig
        ymix = (yn * silu).astype(BF16)
        z = _dot(ymix, wo_ref[...])
        rz = _rms(z)
        dz, dgrow = _rms_bwd(dx_ref[...], z * rz, rz, gp_ref[...])
        dgp_ref[...] += _colsum(dgrow)
        dzb = dz.astype(BF16)
        dwo_ref[...] += _dot_tn(ymix, dzb)
        dymix = _dot_nt(dzb, wo_ref[...])
        dhf_ref[:, 0:D_MODEL] = (dymix * yn * (sig * (1.0 + gate * (1.0 - sig)))).astype(BF16)
        dyn = dymix * silu
        dgg_ref[...] += _colsum(dyn * yhat)
        gg = gg_ref[...]
        dys = []
        for gi in range(4):
            sl = slice(gi * GROUP, (gi + 1) * GROUP)
            dyh = dyn[:, sl] * gg[:, sl]
            yh = yhat[:, sl]
            dys.append(rs[gi] * (dyh - yh * jnp.mean(dyh * yh, axis=-1, keepdims=True)))
        dya_ref[...] = dys[0]
        dyc_ref[...] = dys[2]
        dyd_ref[...] = dys[3]
        dyb = dys[1]
        dhf_ref[:, D_MODEL:D_MODEL + 256] = (dyb * conv).astype(BF16)
        dconv = dyb * bb
        dcb_ref[...] += _colsum(dconv)
        dcw_ref[0:1, :] += _colsum(dconv * u2)
        dcw_ref[1:2, :] += _colsum(dconv * u1)
        dcw_ref[2:3, :] += _colsum(dconv * u)
        carry = carry_ref[...]
        row = lax.broadcasted_iota(jnp.int32, (tm, 1), 0)
        d1 = jnp.where(row == tm - 1, carry[0:1, :], pltpu.roll(dconv, tm - 1, 0))
        d2 = jnp.where(row == tm - 2, carry[0:1, :],
                       jnp.where(row == tm - 1, carry[1:2, :], pltpu.roll(dconv, tm - 2, 0)))
        cw = cw_ref[...]
        du = cw[2:3, :] * dconv + cw[1:2, :] * d1 + cw[0:1, :] * d2
        dhf_ref[:, D_MODEL + 256:D_MODEL + 512] = (du * bx_ref[...]).astype(BF16)
        dhf_ref[:, D_MODEL + 512:D_MODEL + 768] = (du * bc_ref[...]).astype(BF16)
        carry_ref[...] = dconv[0:8, :]

    in_specs = [pl.BlockSpec((tm, D_MODEL), ridx)] + _ep_in_specs(tm, nt - 1)
    return pl.pallas_call(
        body, name="epilogue_bwd", grid=(nt,), in_specs=in_specs,
        out_specs=[pl.BlockSpec((tm, 256), ridx), pl.BlockSpec((tm, 256), ridx), pl.BlockSpec((tm, 256), ridx),
                   pl.BlockSpec((tm, D_MODEL + 768), ridx),
                   pl.BlockSpec((D_MODEL, D_MODEL), lambda i: (0, 0)),
                   pl.BlockSpec((1, D_MODEL), lambda i: (0, 0)),
                   pl.BlockSpec((1, D_MODEL), lambda i: (0, 0)),
                   pl.BlockSpec((8, 256), lambda i: (0, 0)),
                   pl.BlockSpec((1, 256), lambda i: (0, 0))],
        out_shape=[jax.ShapeDtypeStruct((T, 256), F32)] * 3
                  + [jax.ShapeDtypeStruct((T, D_MODEL + 768), BF16),
                     jax.ShapeDtypeStruct((D_MODEL, D_MODEL), F32),
                     jax.ShapeDtypeStruct((1, D_MODEL), F32),
                     jax.ShapeDtypeStruct((1, D_MODEL), F32),
                     jax.ShapeDtypeStruct((8, 256), F32),
                     jax.ShapeDtypeStruct((1, 256), F32)],
        scratch_shapes=[pltpu.VMEM((8, 256), F32)],
        compiler_params=_cparams(("arbitrary",)),
    )(dxn, ya, yc, yd, hf, hf, hf, hf, hf, hf, conv_w, conv_b, g_grp, w_out, g_post)


def _loss_head(y, tgt):
    T = y.shape[0]
    tm = 512

    def body(y_ref, t_ref, dy_ref, l_ref):
        @pl.when(pl.program_id(0) == 0)
        def _():
            l_ref[...] = jnp.zeros_like(l_ref)

        d = y_ref[...] - t_ref[...]
        dy_ref[...] = d * (1.0 / D_MODEL)
        part = jnp.sum(jnp.sum(d * d, axis=1, keepdims=True), axis=0, keepdims=True)
        l_ref[...] += part * (0.5 / D_MODEL)

    return pl.pallas_call(
        body, name="loss_head", grid=(T // tm,),
        in_specs=[pl.BlockSpec((tm, D_MODEL), lambda i: (i, 0))] * 2,
        out_specs=[pl.BlockSpec((tm, D_MODEL), lambda i: (i, 0)), pl.BlockSpec((8, LANES), lambda i: (0, 0))],
        out_shape=[jax.ShapeDtypeStruct((T, D_MODEL), F32), jax.ShapeDtypeStruct((8, LANES), F32)],
        compiler_params=_cparams(("arbitrary",)),
    )(y, tgt)


def _place():
    return lax.axis_index("x"), lax.axis_index("y"), lax.axis_index("c")


def _other_chips(x, y):
    return [(1 - x, y), (x, 1 - y), (1 - x, 1 - y)]


HBM = pl.BlockSpec(memory_space=pl.ANY)


def _gather_weights(shards):
    n = len(shards)

    def body(*refs):
        ins, outs = refs[:n], refs[n:2 * n]
        ici_send, ici_recv, d2d_send, d2d_recv, local_sems = refs[2 * n:]
        x, y, c = _place()
        me = 2 * x + y
        chips = _other_chips(x, y)

        def ici(a, j, layer_from):
            px, py = chips[j]
            return pltpu.make_async_remote_copy(
                src_ref=ins[a].at[c], dst_ref=outs[a].at[layer_from, c], send_sem=ici_send.at[3 * a + j],
                recv_sem=ici_recv.at[3 * a + j], device_id=(px, py, c), device_id_type=MESH)

        def d2d(a, j, layer):
            px, py = chips[j]
            blk = outs[a].at[2 * px + py, layer]
            return pltpu.make_async_remote_copy(
                src_ref=blk, dst_ref=blk, send_sem=d2d_send.at[3 * a + j], recv_sem=d2d_recv.at[3 * a + j],
                device_id=(x, y, 1 - c), device_id_type=MESH)

        local = [pltpu.make_async_copy(ins[a], outs[a].at[me], local_sems.at[a]) for a in range(n)]
        for cp in local:
            cp.start()
        sends = [ici(a, j, me) for j in range(3) for a in range(n)]
        for cp in sends:
            cp.start()
        for j in range(3):
            px, py = chips[j]
            for a in range(n):
                ici(a, j, 2 * px + py).wait_recv()
                fwd = d2d(a, j, c)
                fwd.start()
                sends.append(fwd)
        for j in range(3):
            for a in range(n):
                d2d(a, j, 1 - c).wait_recv()
        for cp in sends:
            cp.wait_send()
        for cp in local:
            cp.wait()

    return pl.pallas_call(
        body, name="gather_weights",
        in_specs=[HBM] * n, out_specs=[HBM] * n,
        out_shape=[jax.ShapeDtypeStruct((4,) + s.shape, s.dtype) for s in shards],
        scratch_shapes=[pltpu.SemaphoreType.DMA((3 * n,))] * 4 + [pltpu.SemaphoreType.DMA((n,))],
    )(*shards)


def _exchange_chips(parts, small):
    n = len(parts)

    def body(*refs):
        ins, sm_ref = refs[:n], refs[n]
        outs, osm_ref = refs[n + 1:2 * n + 1], refs[2 * n + 1]
        send_sems, recv_sems, ssend_sems, srecv_sems, local_sems = refs[2 * n + 2:]
        x, y, c = _place()
        me = 2 * x + y
        dev = 4 * x + 2 * y + c
        local = [pltpu.make_async_copy(ins[a].at[me], outs[a].at[me], local_sems.at[a]) for a in range(n)]
        local.append(pltpu.make_async_copy(sm_ref, osm_ref.at[dev], local_sems.at[n]))
        for cp in local:
            cp.start()
        sends = []
        for j, (px, py) in enumerate(_other_chips(x, y)):
            for a in range(n):
                cp = pltpu.make_async_remote_copy(
                    src_ref=ins[a].at[2 * px + py], dst_ref=outs[a].at[me], send_sem=send_sems.at[3 * a + j],
                    recv_sem=recv_sems.at[3 * a + j], device_id=(px, py, c), device_id_type=MESH)
                cp.start()
                sends.append(cp)
        flips = [(fx, fy, fc) for fx in (0, 1) for fy in (0, 1) for fc in (0, 1)][1:]
        for j, (fx, fy, fc) in enumerate(flips):
            cp = pltpu.make_async_remote_copy(
                src_ref=sm_ref, dst_ref=osm_ref.at[dev], send_sem=ssend_sems.at[j], recv_sem=srecv_sems.at[j],
                device_id=(x ^ fx, y ^ fy, c ^ fc), device_id_type=MESH)
            cp.start()
            sends.append(cp)
        for j, (px, py) in enumerate(_other_chips(x, y)):
            for a in range(n):
                pltpu.make_async_remote_copy(
                    src_ref=ins[a].at[me], dst_ref=outs[a].at[2 * px + py], send_sem=send_sems.at[3 * a + j],
                    recv_sem=recv_sems.at[3 * a + j], device_id=(px, py, c), device_id_type=MESH).wait_recv()
        for j, (fx, fy, fc) in enumerate(flips):
            src = 4 * (x ^ fx) + 2 * (y ^ fy) + (c ^ fc)
            pltpu.make_async_remote_copy(
                src_ref=sm_ref, dst_ref=osm_ref.at[src], send_sem=ssend_sems.at[j], recv_sem=srecv_sems.at[j],
                device_id=(x ^ fx, y ^ fy, c ^ fc), device_id_type=MESH).wait_recv()
        for cp in sends:
            cp.wait_send()
        for cp in local:
            cp.wait()

    return pl.pallas_call(
        body, name="exchange_chips",
        in_specs=[HBM] * (n + 1), out_specs=[HBM] * (n + 1),
        out_shape=[jax.ShapeDtypeStruct(p.shape, p.dtype) for p in parts]
                  + [jax.ShapeDtypeStruct((8,) + small.shape, small.dtype)],
        scratch_shapes=[pltpu.SemaphoreType.DMA((3 * n,)), pltpu.SemaphoreType.DMA((3 * n,)),
                        pltpu.SemaphoreType.DMA((7,)), pltpu.SemaphoreType.DMA((7,)),
                        pltpu.SemaphoreType.DMA((n + 1,))],
    )(*parts, small)


def _swap_cores(parts, name):
    n = len(parts)

    def body(*refs):
        ins, outs, send_sems, recv_sems = refs[:n], refs[n:2 * n], refs[2 * n], refs[2 * n + 1]
        x, y, c = _place()
        copies = [pltpu.make_async_remote_copy(
            src_ref=ins[a], dst_ref=outs[a], send_sem=send_sems.at[a], recv_sem=recv_sems.at[a],
            device_id=(x, y, 1 - c), device_id_type=MESH) for a in range(n)]
        for cp in copies:
            cp.start()
        for cp in copies:
            cp.wait()

    return pl.pallas_call(
        body, name=name, in_specs=[HBM] * n, out_specs=[HBM] * n,
        out_shape=[jax.ShapeDtypeStruct(p.shape, p.dtype) for p in parts],
        scratch_shapes=[pltpu.SemaphoreType.DMA((n,)), pltpu.SemaphoreType.DMA((n,))],
    )(*parts)


def _tile(rows, cols):
    for cand in (256, 128, 64):
        if rows % cand == 0:
            return cand, cols
    if rows > 64 and cols % 256 == 0:
        return rows, 256
    return rows, cols


def _add(a, b, name):
    L, R, C = a.shape
    tr, tc = _tile(R, C)

    def body(a_ref, b_ref, o_ref):
        o_ref[...] = (a_ref[...] + b_ref[...]).astype(BF16)

    spec = pl.BlockSpec((1, tr, tc), lambda l, i, j: (l, i, j))
    return pl.pallas_call(
        body, name=name, grid=(L, R // tr, C // tc), in_specs=[spec, spec], out_specs=spec,
        out_shape=jax.ShapeDtypeStruct((L, R, C), BF16),
        compiler_params=_cparams(("parallel", "parallel", "parallel")),
    )(a, b)


def _sum_leading(buf, name):
    n, R, C = buf.shape
    tr, tc = _tile(R, C)

    def body(b_ref, o_ref):
        acc = b_ref[0].astype(F32)
        for k in range(1, n):
            acc = acc + b_ref[k].astype(F32)
        o_ref[...] = acc

    return pl.pallas_call(
        body, name=name, grid=(R // tr, C // tc),
        in_specs=[pl.BlockSpec((n, tr, tc), lambda i, j: (0, i, j))],
        out_specs=pl.BlockSpec((tr, tc), lambda i, j: (i, j)),
        out_shape=jax.ShapeDtypeStruct((R, C), F32),
        compiler_params=_cparams(("parallel", "parallel")),
    )(buf)


def _adam_update(w, g, m, v):
    c1 = 1.0 / (1.0 - ADAM_B1 ** ADAM_STEP)
    c2 = 1.0 / (1.0 - ADAM_B2 ** ADAM_STEP)
    mn = ADAM_B1 * m + (1.0 - ADAM_B1) * g
    vn = ADAM_B2 * v + (1.0 - ADAM_B2) * (g * g)
    return -ADAM_LR * ((mn * c1) / (jnp.sqrt(vn * c2) + ADAM_EPS) + ADAM_WD * w), mn, vn


def _adamw_layers(w, m, v, g_mine, g_other, name):
    _, R, C = w.shape
    tr, tc = _tile(R, C)

    def body(w_ref, m_ref, v_ref, gm_ref, go_ref, g_ref, d_ref, mo_ref, vo_ref):
        g = jnp.where(pl.program_id(0) == lax.axis_index("c"), gm_ref[...], go_ref[...])
        g_ref[0] = g
        d_ref[0], mo_ref[0], vo_ref[0] = _adam_update(w_ref[0], g, m_ref[0], v_ref[0])

    spec3 = pl.BlockSpec((1, tr, tc), lambda l, i, j: (l, i, j))
    spec2 = pl.BlockSpec((tr, tc), lambda l, i, j: (i, j))
    return pl.pallas_call(
        body, name=name, grid=(2, R // tr, C // tc),
        in_specs=[spec3] * 3 + [spec2] * 2, out_specs=[spec3] * 4,
        out_shape=[jax.ShapeDtypeStruct(w.shape, F32)] * 4,
        compiler_params=_cparams(("parallel", "parallel", "parallel")),
    )(w, m, v, g_mine, g_other)


PACK_C = 1024
_BIG = ("w_in", "w_out", "mla_w_uq", "mla_w_ukv", "conv_w")
_SMALL = ("norm_pre", "group_norm", "norm_post", "conv_b", "mla_q_norm", "mla_kv_norm", "attn_sinks")
_SMALL_W = {"norm_pre": 1024, "group_norm": 1024, "norm_post": 1024, "conv_b": 256, "mla_q_norm": 256,
            "mla_kv_norm": 128, "attn_sinks": 4}


_LOSS_AT = divmod(DEPTH * sum(_SMALL_W.values()), PACK_C)


def _pack_small(d, loss):
    flat = jnp.concatenate([d[n].reshape(-1) for n in _SMALL] + [loss.reshape(1)])
    return jnp.pad(flat, (0, 8 * PACK_C - flat.shape[0])).reshape(8, PACK_C)


def _adamw_small(w, m, v, got):
    ns = len(_SMALL)

    def body(*refs):
        got_ref = refs[3 * ns]
        outs = refs[3 * ns + 1:]
        gsum = got_ref[0]
        for d in range(1, 8):
            gsum = gsum + got_ref[d]
        outs[4 * ns][...] = gsum[_LOSS_AT[0]:_LOSS_AT[0] + 1, _LOSS_AT[1]:_LOSS_AT[1] + 1]
        off = 0
        for i, name in enumerate(_SMALL):
            wd = _SMALL_W[name]
            rows = []
            for l in range(DEPTH):
                r, c0 = divmod(off + l * wd, PACK_C)
                rows.append(gsum[r:r + 1, c0:c0 + wd])
            off += DEPTH * wd
            g = jnp.concatenate(rows, axis=0)
            delta, mn, vn = _adam_update(refs[i][...], g, refs[ns + i][...], refs[2 * ns + i][...])
            outs[i][...] = g
            outs[ns + i][...] = delta
            outs[2 * ns + i][...] = mn
            outs[3 * ns + i][...] = vn

    shapes = [jax.ShapeDtypeStruct(w[n].shape, F32) for n in _SMALL]
    res = pl.pallas_call(body, name="adamw_small", out_shape=shapes * 4 + [jax.ShapeDtypeStruct((1, 1), F32)])(
        *[w[n] for n in _SMALL], *[m[n] for n in _SMALL], *[v[n] for n in _SMALL], got)
    return [dict(zip(_SMALL, res[k * ns:(k + 1) * ns])) for k in range(4)], res[4 * ns]


def _w_in_internal(w):
    cols = []
    for n in _INT_ORDER:
        o, wd = _REAL_OFF[n]
        cols.append(w[:, o:o + wd])
        if _INT_W[n] != wd:
            cols.append(jnp.zeros((w.shape[0], _INT_W[n] - wd), w.dtype))
    return jnp.concatenate(cols, axis=1)


def _w_in_real(dw):
    return jnp.concatenate([dw[:, _INT_OFF[n]:_INT_OFF[n] + wd] for n, wd in _REAL], axis=1)


def _uq_internal(w):
    return jnp.pad(w.reshape(256, 4, 96), ((0, 0), (0, 0), (0, 32))).reshape(256, 512)


def _uq_real(dw):
    return dw.reshape(256, 4, 128)[:, :, :96].reshape(256, 384)


def _ukv_internal(w):
    w4 = w.reshape(128, 4, 128)
    k = jnp.pad(w4[:, :, :64], ((0, 0), (0, 0), (0, 64))).reshape(128, 512)
    return jnp.concatenate([k, w4[:, :, 64:].reshape(128, 256)], axis=1)


def _ukv_real(dw):
    k = dw[:, :512].reshape(128, 4, 128)[:, :, :64]
    v = dw[:, 512:].reshape(128, 4, 64)
    return jnp.concatenate([k, v], axis=2).reshape(128, 512)


def _layer_fwd(x, pos, p):
    xt, hb, hf = _inproj_fwd(x, p["norm_pre"], p["w_in"])
    ya = _swa_fwd(hb, p["attn_sinks"])
    qm, km, vm, vt = _mla_prep_fwd(hf, pos, p["mla_q_norm"], p["mla_kv_norm"], p["mla_w_uq"], p["mla_w_ukv"])
    yc, lse = _mla_fwd(qm, km, vt)
    yd, tot, cnt = _sb_fwd(hb)
    x_next = _epilogue_fwd(x, ya, yc, yd, hf, p["conv_w"], p["conv_b"], p["group_norm"], p["w_out"], p["norm_post"])
    return x_next, dict(x=x, xt=xt, hb=hb, hf=hf, ya=ya, yc=yc, yd=yd, tot=tot, cnt=cnt, qm=qm, km=km, vm=vm, lse=lse)


def _layer_bwd(dx_next, pos, p, s):
    (dya, dyc, dyd, dhf, dw_out, dg_post, dg_grp, dconv_w, dconv_b) = _epilogue_bwd(
        dx_next, s["ya"], s["yc"], s["yd"], s["hf"], p["conv_w"], p["conv_b"], p["group_norm"], p["w_out"],
        p["norm_post"])
    dq_d, dk_d, dv_d = _sb_bwd(s["hb"], s["tot"], s["cnt"], dyd)
    dqm, dkt, dvt = _mla_bwd(s["qm"], s["km"], s["vm"], s["yc"], s["lse"], dyc)
    dc, dw_uq, dw_ukv, dg_q, dg_kv = _mla_prep_bwd(
        s["hf"], pos, p["mla_q_norm"], p["mla_kv_norm"], p["mla_w_uq"], p["mla_w_ukv"], dqm, dkt, dvt)
    dq_a, dk_a, dv_a, dsinks = _swa_bwd(s["hb"], p["attn_sinks"], dya)
    dx, dh, dg_pre = _inproj_bwd_dx(s["x"], p["norm_pre"], p["w_in"], dx_next,
                                    [dq_a, dk_a, dv_a, dq_d, dk_d, dv_d, dhf, dc])
    dw_in = _matmul_over_tokens(s["xt"], dh, "inproj_bwd_dw")
    grads = dict(norm_pre=dg_pre[0], w_in=_w_in_real(dw_in), attn_sinks=dsinks[0, :4], conv_w=dconv_w[:3],
                 conv_b=dconv_b[0], mla_q_norm=dg_q[0], mla_w_uq=_uq_real(dw_uq), mla_kv_norm=dg_kv[0],
                 mla_w_ukv=_ukv_real(dw_ukv), group_norm=dg_grp[0], w_out=dw_out, norm_post=dg_post[0])
    return dx, grads


_WEIGHTS = ["norm_pre", "w_in", "attn_sinks", "conv_w", "conv_b", "mla_q_norm", "mla_w_uq", "mla_kv_norm",
            "mla_w_ukv", "group_norm", "w_out", "norm_post"]


def kernel(x, positions, norm_pre, w_in, attn_sinks, conv_w, conv_b, mla_q_norm, mla_w_uq, mla_kv_norm, mla_w_ukv, group_norm, w_out, norm_post, loss_target, m_norm_pre, m_w_in, m_attn_sinks, m_conv_w, m_conv_b, m_mla_q_norm, m_mla_w_uq, m_mla_kv_norm, m_mla_w_ukv, m_group_norm, m_w_out, m_norm_post, v_norm_pre, v_w_in, v_attn_sinks, v_conv_w, v_conv_b, v_mla_q_norm, v_mla_w_uq, v_mla_kv_norm, v_mla_w_ukv, v_group_norm, v_w_out, v_norm_post):
    w = dict(norm_pre=norm_pre, w_in=w_in, attn_sinks=attn_sinks, conv_w=conv_w, conv_b=conv_b,
             mla_q_norm=mla_q_norm, mla_w_uq=mla_w_uq, mla_kv_norm=mla_kv_norm, mla_w_ukv=mla_w_ukv,
             group_norm=group_norm, w_out=w_out, norm_post=norm_post)
    m = dict(norm_pre=m_norm_pre, w_in=m_w_in, attn_sinks=m_attn_sinks, conv_w=m_conv_w, conv_b=m_conv_b,
             mla_q_norm=m_mla_q_norm, mla_w_uq=m_mla_w_uq, mla_kv_norm=m_mla_kv_norm, mla_w_ukv=m_mla_w_ukv,
             group_norm=m_group_norm, w_out=m_w_out, norm_post=m_norm_post)
    v = dict(norm_pre=v_norm_pre, w_in=v_w_in, attn_sinks=v_attn_sinks, conv_w=v_conv_w, conv_b=v_conv_b,
             mla_q_norm=v_mla_q_norm, mla_w_uq=v_mla_w_uq, mla_kv_norm=v_mla_kv_norm, mla_w_ukv=v_mla_w_ukv,
             group_norm=v_group_norm, w_out=v_w_out, norm_post=v_norm_post)
    T = x.shape[1]
    xs = x[0]
    pos = positions[0].reshape(T, 1)
    tgt = loss_target[0]
    core = lax.axis_index("c")

    gathered = _gather_weights([w[n].astype(BF16) for n in _BIG[:4]] + [w["conv_w"]])
    full = {}
    for n, got in zip(_BIG, gathered):
        if n == "w_out":
            full[n] = jnp.moveaxis(got, 0, 1).reshape(DEPTH, D_MODEL, D_MODEL)
        else:
            full[n] = jnp.transpose(got, (1, 2, 0, 3)).reshape(DEPTH, got.shape[2], 4 * got.shape[3])

    layers = []
    for l in range(DEPTH):
        layers.append(dict(
            norm_pre=norm_pre[l:l + 1], w_in=_w_in_internal(full["w_in"][l]), attn_sinks=attn_sinks[l],
            conv_w=full["conv_w"][l], conv_b=conv_b[l:l + 1], mla_q_norm=mla_q_norm[l:l + 1],
            mla_w_uq=_uq_internal(full["mla_w_uq"][l]), mla_kv_norm=mla_kv_norm[l:l + 1],
            mla_w_ukv=_ukv_internal(full["mla_w_ukv"][l]), group_norm=group_norm[l:l + 1],
            w_out=full["w_out"][l], norm_post=norm_post[l:l + 1]))

    saved = []
    h = xs
    for l in range(DEPTH):
        h, s = _layer_fwd(h, pos, layers[l])
        saved.append(s)
    dy, loss_part = _loss_head(h, tgt)

    grads = [None] * DEPTH
    for l in reversed(range(DEPTH)):
        dy, grads[l] = _layer_bwd(dy, pos, layers[l], saved[l])

    turned = ("w_in", "mla_w_uq")
    turn = lambda n, a: jnp.swapaxes(a, -1, -2) if n in turned else a

    def chunks(n, a):
        if n == "w_out":
            return a.reshape(4, D_MODEL // 4, D_MODEL)
        if n in turned:
            return a.T.reshape(4, a.shape[1] // 4, a.shape[0])
        return jnp.transpose(a.reshape(a.shape[0], 4, a.shape[1] // 4), (1, 0, 2))

    mine = [chunks(n, jnp.where(core == 0, grads[0][n], grads[1][n])) for n in _BIG]
    theirs = [chunks(n, jnp.where(core == 0, grads[1][n], grads[0][n])) for n in _BIG]
    from_sibling = _swap_cores(theirs, "swap_layer_chunks")
    summed = [_add(a, b, "add_cores_" + n) for n, a, b in zip(_BIG, mine, from_sibling)]
    small = _pack_small({n: jnp.stack([grads[l][n] for l in range(DEPTH)]) for n in _SMALL}, loss_part[0, 0])
    *got, got_small = _exchange_chips(summed, small)
    done = [_sum_leading(b, "sum_chips_" + n) for n, b in zip(_BIG, got)]
    done_other = _swap_cores(done, "swap_layer_shards")

    outs, loss = _adamw_small(w, m, v, got_small)
    for n, gm, go in zip(_BIG, done, done_other):
        for d, a in zip(outs, _adamw_layers(turn(n, w[n]), turn(n, m[n]), turn(n, v[n]), gm, go, "adamw_" + n)):
            d[n] = turn(n, a)
    return (loss[0, 0], dy[None], *[outs[0][n] for n in _WEIGHTS], *[outs[1][n] for n in _WEIGHTS],
            *[outs[2][n] for n in _WEIGHTS], *[outs[3][n] for n in _WEIGHTS])
```

```python
import math

import jax
import jax.numpy as jnp
from jax import lax
from jax.experimental import pallas as pl
from jax.experimental.pallas import tpu as pltpu

F32 = jnp.float32
BF16 = jnp.bfloat16
MESH = pl.DeviceIdType.MESH

D_MODEL = 1024
DEPTH = 2
EPS = 1e-6
BLOCK = 128
HEAD = 64
LANES = 128
GROUP = 256
LOG2E = 1.4426950408889634
LN2 = 0.6931471805599453
MLA_QSCALE = 96 ** -0.5 * LOG2E
ROPE_HALF = 16
ROPE_THETA = 10000.0
ATT_BLK = 256
MLA_BQ = 512
NEG = -1e30
SB_DEAD = -104.0

ADAM_LR, ADAM_B1, ADAM_B2, ADAM_EPS, ADAM_WD, ADAM_STEP = 0.001, 0.9, 0.999, 1e-08, 0.01, 10

_REAL = [("a_q", 256), ("a_k", 128), ("a_v", 128), ("b_b", 256), ("b_c", 256), ("b_x", 256),
         ("c_q", 256), ("c_kv", 128), ("c_kr", 32), ("d_q", 256), ("d_k", 256), ("d_v", 256),
         ("gate", 1024)]
_REAL_OFF = {}
_o = 0
for _n, _w in _REAL:
    _REAL_OFF[_n] = (_o, _w)
    _o += _w
D_IN = _o
_INT_ORDER = ["a_q", "a_k", "a_v", "d_q", "d_k", "d_v", "gate", "b_b", "b_c", "b_x", "c_q", "c_kv", "c_kr"]
_INT_W = dict(_REAL)
_INT_W["c_kr"] = 128
_INT_OFF = {}
_o = 0
for _n in _INT_ORDER:
    _INT_OFF[_n] = _o
    _o += _INT_W[_n]
N_INT = _o
N_HB = _INT_OFF["gate"]
N_HF = N_INT - N_HB

VMEM_LIMIT = 56 * 1024 * 1024


def _cparams(sem):
    return pltpu.CompilerParams(dimension_semantics=sem, vmem_limit_bytes=VMEM_LIMIT)


def _dot(a, b):
    return jnp.dot(a, b, preferred_element_type=F32)


def _dot_nt(a, b):
    return lax.dot_general(a, b, (((1,), (1,)), ((), ())), preferred_element_type=F32)


def _dot_tn(a, b):
    return lax.dot_general(a, b, (((0,), (0,)), ((), ())), preferred_element_type=F32)


def _split(x):
    hi = x.astype(BF16)
    lo = (x - hi.astype(F32)).astype(BF16)
    return hi, lo


def _rms(x):
    return lax.rsqrt(jnp.mean(x * x, axis=-1, keepdims=True) + EPS)


def _rms_bwd(dy, xhat, r, g):
    dxhat = dy * g
    return r * (dxhat - xhat * jnp.mean(dxhat * xhat, axis=-1, keepdims=True)), dy * xhat


def _colsum(x):
    return jnp.sum(x, axis=0, keepdims=True)


def _inproj_fwd(x, g, wt):
    T = x.shape[0]
    tm = 256

    def body(x_ref, g_ref, w_ref, xt_ref, hb_ref, hf_ref):
        xv = x_ref[...]
        xn32 = xv * _rms(xv) * g_ref[...]
        xt_ref[...] = jnp.transpose(xn32).astype(BF16)
        h = _dot_nt(xn32.astype(BF16), w_ref[...])
        hb_ref[...] = h[:, :N_HB].astype(BF16)
        hf_ref[...] = h[:, N_HB:]

    return pl.pallas_call(
        body, name="inproj_fwd", grid=(T // tm,),
        in_specs=[pl.BlockSpec((tm, D_MODEL), lambda i: (i, 0)),
                  pl.BlockSpec((1, D_MODEL), lambda i: (0, 0)),
                  pl.BlockSpec((N_INT, D_MODEL), lambda i: (0, 0))],
        out_specs=[pl.BlockSpec((D_MODEL, tm), lambda i: (0, i)),
                   pl.BlockSpec((tm, N_HB), lambda i: (i, 0)),
                   pl.BlockSpec((tm, N_HF), lambda i: (i, 0))],
        out_shape=[jax.ShapeDtypeStruct((D_MODEL, T), BF16),
                   jax.ShapeDtypeStruct((T, N_HB), BF16),
                   jax.ShapeDtypeStruct((T, N_HF), F32)],
        compiler_params=_cparams(("parallel",)),
    )(x, g, wt)


def _inproj_bwd_dx(x, g, wt, dx_next, pieces):
    T = x.shape[0]
    tm = 256
    widths = [p.shape[1] for p in pieces]
    assert sum(widths) == N_INT

    def body(x_ref, g_ref, w_ref, dxn_ref, *rest):
        p_refs = rest[:len(pieces)]
        dx_ref, dh_ref, dg_ref = rest[len(pieces):]
        dh = jnp.concatenate([p[...].astype(BF16) for p in p_refs], axis=1)
        dh_ref[...] = dh
        dxn = _dot(dh, w_ref[...])
        xv = x_ref[...]
        r = _rms(xv)
        dx, dgrow = _rms_bwd(dxn, xv * r, r, g_ref[...])
        dx_ref[...] = dx + dxn_ref[...]

        @pl.when(pl.program_id(0) == 0)
        def _():
            dg_ref[...] = jnp.zeros_like(dg_ref)

        dg_ref[...] += _colsum(dgrow)

    return pl.pallas_call(
        body, name="inproj_bwd_dx", grid=(T // tm,),
        in_specs=[pl.BlockSpec((tm, D_MODEL), lambda i: (i, 0)),
                  pl.BlockSpec((1, D_MODEL), lambda i: (0, 0)),
                  pl.BlockSpec((N_INT, D_MODEL), lambda i: (0, 0)),
                  pl.BlockSpec((tm, D_MODEL), lambda i: (i, 0))]
                 + [pl.BlockSpec((tm, wd), lambda i: (i, 0)) for wd in widths],
        out_specs=[pl.BlockSpec((tm, D_MODEL), lambda i: (i, 0)),
                   pl.BlockSpec((tm, N_INT), lambda i: (i, 0)),
                   pl.BlockSpec((1, D_MODEL), lambda i: (0, 0))],
        out_shape=[jax.ShapeDtypeStruct((T, D_MODEL), F32),
                   jax.ShapeDtypeStruct((T, N_INT), BF16),
                   jax.ShapeDtypeStruct((1, D_MODEL), F32)],
        compiler_params=_cparams(("arbitrary",)),
    )(x, g, wt, dx_next, *pieces)


def _matmul_over_tokens(at, b, name):
    M, T = at.shape
    N = b.shape[1]
    tm, tn = min(1024, T), 512

    def body(a_ref, b_ref, o_ref):
        @pl.when(pl.program_id(1) == 0)
        def _():
            o_ref[...] = jnp.zeros_like(o_ref)

        o_ref[...] += _dot(a_ref[...], b_ref[...])

    return pl.pallas_call(
        body, name=name, grid=(N // tn, T // tm),
        in_specs=[pl.BlockSpec((M, tm), lambda j, t: (0, t)),
                  pl.BlockSpec((tm, tn), lambda j, t: (t, j))],
        out_specs=pl.BlockSpec((M, tn), lambda j, t: (0, j)),
        out_shape=jax.ShapeDtypeStruct((M, N), F32),
        compiler_params=_cparams(("parallel", "arbitrary")),
    )(at, b)


def _roll_f32(x, shift):
    return pltpu.roll(x.astype(F32), shift, 1)


def _swa_operands(h, q_ref, kp_ref, kc_ref, vp_ref, vc_ref):
    p, e = h // 2, h % 2
    lane = lax.broadcasted_iota(jnp.int32, (1, LANES), 1) // HEAD
    q = q_ref[:, p * LANES:(p + 1) * LANES]
    k_prev, k_cur, v_prev, v_cur = kp_ref[...], kc_ref[...], vp_ref[...], vc_ref[...]
    if e != p:
        q = _roll_f32(q, HEAD).astype(BF16)
        v_prev = _roll_f32(v_prev, HEAD).astype(BF16)
        v_cur = _roll_f32(v_cur, HEAD).astype(BF16)
    qs = jnp.where(lane == p, q, 0) * 0.125
    return dict(p=p, e=e, lane=lane, qs=qs, k_prev=k_prev, k_cur=k_cur,
                v_prev=jnp.where(lane == e, v_prev, 0), v_cur=jnp.where(lane == e, v_cur, 0),
                s_prev=_dot_nt(qs, k_prev), s_cur=_dot_nt(qs, k_cur))


def _swa_probs(ops, sink, first):
    row = lax.broadcasted_iota(jnp.int32, (BLOCK, BLOCK), 0)
    col = lax.broadcasted_iota(jnp.int32, (BLOCK, BLOCK), 1)
    s_prev = jnp.where(jnp.logical_and(col > row, jnp.logical_not(first)), ops["s_prev"], NEG)
    s_cur = jnp.where(col <= row, ops["s_cur"], NEG)
    m = jnp.maximum(jnp.maximum(jnp.max(s_prev, axis=1, keepdims=True),
                                jnp.max(s_cur, axis=1, keepdims=True)), sink)
    p_prev = jnp.exp(s_prev - m)
    p_cur = jnp.exp(s_cur - m)
    p_sink = jnp.exp(sink - m)
    inv = 1.0 / (jnp.sum(p_prev, axis=1, keepdims=True) + jnp.sum(p_cur, axis=1, keepdims=True) + p_sink)
    return p_prev * inv, p_cur * inv, p_sink * inv


def _swa_specs(T):
    nb = T // BLOCK
    qo, ko, vo = (_INT_OFF[n] // LANES for n in ("a_q", "a_k", "a_v"))
    prev = lambda i: jnp.maximum(i - 1, 0)
    return [pl.BlockSpec((BLOCK, 256), lambda i: (i, qo // 2)),
            pl.BlockSpec((BLOCK, LANES), lambda i: (prev(i), ko)),
            pl.BlockSpec((BLOCK, LANES), lambda i: (i, ko)),
            pl.BlockSpec((BLOCK, LANES), lambda i: (prev(i), vo)),
            pl.BlockSpec((BLOCK, LANES), lambda i: (i, vo)),
            pl.BlockSpec(memory_space=pltpu.SMEM)], nb


def _swa_fwd(hb, sinks):
    T = hb.shape[0]
    specs, nb = _swa_specs(T)

    def body(q_ref, kp_ref, kc_ref, vp_ref, vc_ref, s_ref, o_ref):
        first = pl.program_id(0) == 0
        ops = [_swa_operands(h, q_ref, kp_ref, kc_ref, vp_ref, vc_ref) for h in range(4)]
        probs = [_swa_probs(ops[h], s_ref[h], first) for h in range(4)]
        outs = [_dot(probs[h][0].astype(BF16), ops[h]["v_prev"]) + _dot(probs[h][1].astype(BF16), ops[h]["v_cur"])
                for h in range(4)]
        for p in range(2):
            o_ref[:, p * LANES:(p + 1) * LANES] = outs[2 * p] + outs[2 * p + 1]

    return pl.pallas_call(
        body, name="swa_fwd", grid=(nb,), in_specs=specs,
        out_specs=pl.BlockSpec((BLOCK, 256), lambda i: (i, 0)),
        out_shape=jax.ShapeDtypeStruct((T, 256), F32),
        compiler_params=_cparams(("parallel",)),
    )(hb, hb, hb, hb, hb, sinks)


def _swa_bwd(hb, sinks, dy):
    T = hb.shape[0]
    specs, nb = _swa_specs(T)

    def body(q_ref, kp_ref, kc_ref, vp_ref, vc_ref, s_ref, dy_ref, dq_ref, dk_ref, dv_ref, ds_ref):
        i = pl.program_id(0)
        first = i == 0
        cur = pl.ds(pl.multiple_of(i * BLOCK, BLOCK), BLOCK)
        prv = pl.ds(pl.multiple_of(jnp.maximum(i - 1, 0) * BLOCK, BLOCK), BLOCK)

        @pl.when(first)
        def _():
            ds_ref[...] = jnp.zeros_like(ds_ref)

        dk_ref[cur, :] = jnp.zeros((BLOCK, LANES), F32)
        dv_ref[cur, :] = jnp.zeros((BLOCK, LANES), F32)
        lane_id = lax.broadcasted_iota(jnp.int32, (8, LANES), 1)
        heads = range(4)
        ops = [_swa_operands(h, q_ref, kp_ref, kc_ref, vp_ref, vc_ref) for h in heads]
        probs = [_swa_probs(ops[h], s_ref[h], first) for h in heads]
        dos = [jnp.where(ops[h]["lane"] == ops[h]["e"], dy_ref[:, ops[h]["p"] * LANES:(ops[h]["p"] + 1) * LANES], 0.0)
               for h in heads]
        dobs = [d.astype(BF16) for d in dos]
        pbs = [(probs[h][0].astype(BF16), probs[h][1].astype(BF16)) for h in heads]
        outs = [_dot(pbs[h][0], ops[h]["v_prev"]) + _dot(pbs[h][1], ops[h]["v_cur"]) for h in heads]
        dps = [(_dot_nt(dobs[h], ops[h]["v_prev"]), _dot_nt(dobs[h], ops[h]["v_cur"])) for h in heads]
        dss, dsinks = [], jnp.zeros((8, LANES), F32)
        for h in heads:
            delta = jnp.sum(dos[h] * outs[h], axis=1, keepdims=True)
            dss.append(((probs[h][0] * (dps[h][0] - delta)).astype(BF16),
                        (probs[h][1] * (dps[h][1] - delta)).astype(BF16)))
            dsink = -jnp.sum(probs[h][2] * delta, axis=0, keepdims=True)
            dsinks += jnp.where(lane_id == h, dsink, 0.0)
        ds_ref[...] += dsinks
        dqs = [(_dot(dss[h][0], ops[h]["k_prev"]) + _dot(dss[h][1], ops[h]["k_cur"])) * 0.125 for h in heads]
        dk_prev = dk_cur = dv_prev = dv_cur = jnp.zeros((BLOCK, LANES), F32)
        for h in heads:
            p, e = ops[h]["p"], ops[h]["e"]
            dob_v = dobs[h] if e == p else pltpu.roll(dos[h], HEAD, 1).astype(BF16)
            dk_prev += _dot_tn(dss[h][0], ops[h]["qs"])
            dk_cur += _dot_tn(dss[h][1], ops[h]["qs"])
            dv_prev += _dot_tn(pbs[h][0], dob_v)
            dv_cur += _dot_tn(pbs[h][1], dob_v)
        dk_ref[prv, :] += dk_prev
        dk_ref[cur, :] += dk_cur
        dv_ref[prv, :] += dv_prev
        dv_ref[cur, :] += dv_cur
        for p in range(2):
            dq_pair = jnp.zeros((BLOCK, LANES), F32)
            for e in range(2):
                dq = jnp.where(ops[2 * p + e]["lane"] == p, dqs[2 * p + e], 0.0)
                dq_pair += dq if e == p else pltpu.roll(dq, HEAD, 1)
            dq_ref[:, p * LANES:(p + 1) * LANES] = dq_pair.astype(BF16)

    return pl.pallas_call(
        body, name="swa_bwd", grid=(nb,),
        in_specs=specs + [pl.BlockSpec((BLOCK, 256), lambda i: (i, 0))],
        out_specs=[pl.BlockSpec((BLOCK, 256), lambda i: (i, 0)),
                   pl.BlockSpec((T, LANES), lambda i: (0, 0)),
                   pl.BlockSpec((T, LANES), lambda i: (0, 0)),
                   pl.BlockSpec((8, LANES), lambda i: (0, 0))],
        out_shape=[jax.ShapeDtypeStruct((T, 256), BF16),
                   jax.ShapeDtypeStruct((T, LANES), F32),
                   jax.ShapeDtypeStruct((T, LANES), F32),
                   jax.ShapeDtypeStruct((8, LANES), F32)],
        compiler_params=_cparams(("arbitrary",)),
    )(hb, hb, hb, hb, hb, sinks, dy)


def _rope_tables(pos_ref):
    lane = lax.broadcasted_iota(jnp.int32, (1, LANES), 1)
    active = jnp.logical_and(lane >= HEAD, lane < HEAD + 2 * ROPE_HALF)
    idx = ((lane - HEAD) % ROPE_HALF).astype(F32)
    freq = jnp.exp(idx * (-math.log(ROPE_THETA) / ROPE_HALF))
    ang = pos_ref[...].astype(F32) * freq
    cos, sin = jnp.cos(ang), jnp.sin(ang)
    c = jnp.where(active, cos, 1.0)
    s_up = jnp.where(jnp.logical_and(active, lane >= HEAD + ROPE_HALF), sin, 0.0)
    s_dn = jnp.where(jnp.logical_and(active, lane < HEAD + ROPE_HALF), -sin, 0.0)
    return c, s_up, s_dn


def _rope(x, tabs):
    c, s_up, s_dn = tabs
    return x * c + pltpu.roll(x, ROPE_HALF, 1) * s_up + pltpu.roll(x, LANES - ROPE_HALF, 1) * s_dn


def _rope_t(dy, tabs):
    c, s_up, s_dn = tabs
    return dy * c + pltpu.roll(dy * s_up, LANES - ROPE_HALF, 1) + pltpu.roll(dy * s_dn, ROPE_HALF, 1)


def _mla_lat_specs(tm):
    cq, ckv, ckr = ((_INT_OFF[n] - N_HB) for n in ("c_q", "c_kv", "c_kr"))
    return [pl.BlockSpec((tm, 256), lambda i: (i, cq // 256)),
            pl.BlockSpec((tm, LANES), lambda i: (i, ckv // LANES)),
            pl.BlockSpec((tm, LANES), lambda i: (i, ckr // LANES)),
            pl.BlockSpec((tm, 1), lambda i: (i, 0)),
            pl.BlockSpec((1, 256), lambda i: (0, 0)),
            pl.BlockSpec((1, LANES), lambda i: (0, 0)),
            pl.BlockSpec((256, 512), lambda i: (0, 0)),
            pl.BlockSpec((LANES, 768), lambda i: (0, 0))]


def _mla_prep_fwd(hf, pos, g_q, g_kv, w_uq, w_ukv):
    T = hf.shape[0]
    tm = 512
    sub = tm // ATT_BLK

    def body(cq_ref, ckv_ref, ckr_ref, pos_ref, gq_ref, gkv_ref, wq_ref, wkv_ref, qm_ref, km_ref, vm_ref, vt_ref):
        tabs = _rope_tables(pos_ref)
        cq = cq_ref[...]
        q = _dot((cq * _rms(cq) * gq_ref[...]).astype(BF16), wq_ref[...])
        ckv = ckv_ref[...]
        kv = _dot((ckv * _rms(ckv) * gkv_ref[...]).astype(BF16), wkv_ref[...])
        kr = _rope(pltpu.roll(ckr_ref[...], HEAD, 1), tabs)
        for h in range(4):
            sl = slice(h * LANES, (h + 1) * LANES)
            qm_ref[:, sl] = (_rope(q[:, sl], tabs) * MLA_QSCALE).astype(BF16)
            km_ref[:, sl] = (kv[:, sl] + kr).astype(BF16)
        vm_ref[...] = kv[:, 512:].astype(BF16)
        for p in range(2):
            for s in range(sub):
                tile = kv[s * ATT_BLK:(s + 1) * ATT_BLK, 512 + p * LANES:512 + (p + 1) * LANES]
                vt_ref[p, s] = jnp.transpose(tile).astype(BF16)

    return pl.pallas_call(
        body, name="mla_prep_fwd", grid=(T // tm,), in_specs=_mla_lat_specs(tm),
        out_specs=[pl.BlockSpec((tm, 512), lambda i: (i, 0)),
                   pl.BlockSpec((tm, 512), lambda i: (i, 0)),
                   pl.BlockSpec((tm, 256), lambda i: (i, 0)),
                   pl.BlockSpec((2, sub, LANES, ATT_BLK), lambda i: (0, i, 0, 0))],
        out_shape=[jax.ShapeDtypeStruct((T, 512), BF16),
                   jax.ShapeDtypeStruct((T, 512), BF16),
                   jax.ShapeDtypeStruct((T, 256), BF16),
                   jax.ShapeDtypeStruct((2, T // ATT_BLK, LANES, ATT_BLK), BF16)],
        compiler_params=_cparams(("parallel",)),
    )(hf, hf, hf, pos, g_q, g_kv, w_uq, w_ukv)


def _mla_prep_bwd(hf, pos, g_q, g_kv, w_uq, w_ukv, dqm, dkt, dvt):
    T = hf.shape[0]
    tm = 512
    sub = tm // ATT_BLK

    def body(cq_ref, ckv_ref, ckr_ref, pos_ref, gq_ref, gkv_ref, wq_ref, wkv_ref, dq_ref, dk_ref, dv_ref,
             dc_ref, dwq_ref, dwkv_ref, dgq_ref, dgkv_ref):
        @pl.when(pl.program_id(0) == 0)
        def _():
            dwq_ref[...] = jnp.zeros_like(dwq_ref)
            dwkv_ref[...] = jnp.zeros_like(dwkv_ref)
            dgq_ref[...] = jnp.zeros_like(dgq_ref)
            dgkv_ref[...] = jnp.zeros_like(dgkv_ref)

        tabs = _rope_tables(pos_ref)
        lane = lax.broadcasted_iota(jnp.int32, (1, LANES), 1)
        dq = jnp.concatenate([_rope_t(dq_ref[:, h * LANES:(h + 1) * LANES] * MLA_QSCALE, tabs)
                              for h in range(4)], axis=1).astype(BF16)
        cq = cq_ref[...]
        rq = _rms(cq)
        cqn = (cq * rq * gq_ref[...]).astype(BF16)
        dwq_ref[...] += _dot_tn(cqn, dq)
        dcq, dgrow = _rms_bwd(_dot_nt(dq, wq_ref[...]), cq * rq, rq, gq_ref[...])
        dgq_ref[...] += _colsum(dgrow)
        dc_ref[:, 0:256] = dcq.astype(BF16)

        dk = jnp.concatenate([jnp.concatenate([jnp.transpose(dk_ref[p, s]) for p in range(2)], axis=1)
                              for s in range(sub)], axis=0) * LN2
        dv = jnp.concatenate([jnp.concatenate([jnp.transpose(dv_ref[p, s]) for p in range(2)], axis=1)
                              for s in range(sub)], axis=0)
        dkr = dk[:, 0:LANES] + dk[:, LANES:2 * LANES] + dk[:, 2 * LANES:3 * LANES] + dk[:, 3 * LANES:]
        dkr = pltpu.roll(_rope_t(dkr, tabs), HEAD, 1)
        dc_ref[:, 384:512] = jnp.where(lane < 2 * ROPE_HALF, dkr, 0.0).astype(BF16)
        dkv = jnp.concatenate([dk.astype(BF16), dv.astype(BF16)], axis=1)
        ckv = ckv_ref[...]
        rkv = _rms(ckv)
        ckvn = (ckv * rkv * gkv_ref[...]).astype(BF16)
        dwkv_ref[...] += _dot_tn(ckvn, dkv)
        dckv, dgrow = _rms_bwd(_dot_nt(dkv, wkv_ref[...]), ckv * rkv, rkv, gkv_ref[...])
        dgkv_ref[...] += _colsum(dgrow)
        dc_ref[:, 256:384] = dckv.astype(BF16)

    return pl.pallas_call(
        body, name="mla_prep_bwd", grid=(T // tm,),
        in_specs=_mla_lat_specs(tm) + [pl.BlockSpec((tm, 512), lambda i: (i, 0)),
                                       pl.BlockSpec((2, sub, 256, ATT_BLK), lambda i: (0, i, 0, 0)),
                                       pl.BlockSpec((2, sub, LANES, ATT_BLK), lambda i: (0, i, 0, 0))],
        out_specs=[pl.BlockSpec((tm, 512), lambda i: (i, 0)),
                   pl.BlockSpec((256, 512), lambda i: (0, 0)),
                   pl.BlockSpec((LANES, 768), lambda i: (0, 0)),
                   pl.BlockSpec((1, 256), lambda i: (0, 0)),
                   pl.BlockSpec((1, LANES), lambda i: (0, 0))],
        out_shape=[jax.ShapeDtypeStruct((T, 512), BF16),
                   jax.ShapeDtypeStruct((256, 512), F32),
                   jax.ShapeDtypeStruct((LANES, 768), F32),
                   jax.ShapeDtypeStruct((1, 256), F32),
                   jax.ShapeDtypeStruct((1, LANES), F32)],
        compiler_params=_cparams(("arbitrary",)),
    )(hf, hf, hf, pos, g_q, g_kv, w_uq, w_ukv, dqm, dkt, dvt)


def _causal_masks(bq, bk):
    row = lax.broadcasted_iota(jnp.int32, (bq, bk), 0)
    col = lax.broadcasted_iota(jnp.int32, (bq, bk), 1)
    return row, col


def _mla_fwd(qm, km, vt):
    T = qm.shape[0]
    bq, bk = min(MLA_BQ, T), ATT_BLK
    nq, nsub, nk = T // bq, bq // bk, T // bk

    def body(q_ref, k_ref, vt_ref, o_ref, lse_ref, acc_ref, m_ref, l_ref):
        qi = pl.program_id(1)
        key = lax.broadcasted_iota(jnp.int32, (bk, bq), 0)
        qry = lax.broadcasted_iota(jnp.int32, (bk, bq), 1)
        ones = jnp.ones((8, bk), BF16)
        acc_ref[...] = jnp.zeros_like(acc_ref)
        m_ref[...] = jnp.full_like(m_ref, NEG)
        l_ref[...] = jnp.zeros_like(l_ref)

        def step(kb0, masked):
            kbs = [kb0 + d for d in range(nsub)]
            sts = [[_dot_nt(k_ref[pl.ds(pl.multiple_of(kb * bk, bk), bk), e * LANES:(e + 1) * LANES],
                            q_ref[:, e * LANES:(e + 1) * LANES]) for kb in kbs] for e in range(2)]
            pts, alphas = [], []
            for e in range(2):
                st = [jnp.where(key + d * bk <= qry, sts[e][d], NEG) for d in range(nsub)] if masked else sts[e]
                m_prev = m_ref[e, 0:1, :]
                m_new = m_prev
                for d in range(nsub):
                    m_new = jnp.maximum(m_new, jnp.max(st[d], axis=0, keepdims=True))
                alpha = jnp.exp2(m_prev - m_new)
                pt = [jnp.exp2(st[d] - m_new).astype(BF16) for d in range(nsub)]
                l_new = alpha * l_ref[e]
                for d in range(nsub):
                    l_new = l_new + _dot(ones, pt[d])
                l_ref[e] = l_new
                m_ref[e] = jnp.broadcast_to(m_new, (8, bq))
                pts.append(pt)
                alphas.append(alpha)
            for e in range(2):
                acc = alphas[e] * acc_ref[e]
                for d in range(nsub):
                    acc = acc + _dot(vt_ref[0, kbs[d], e * HEAD:(e + 1) * HEAD, :], pts[e][d])
                acc_ref[e] = acc

        step(qi * nsub, True)

        def loop(t, c):
            step(t * nsub, False)
            return c

        lax.fori_loop(0, qi, loop, 0)
        outs, lses = [], []
        for e in range(2):
            l = l_ref[e, 0:1, :]
            outs.append(acc_ref[e] / l)
            lses.append(jnp.broadcast_to(m_ref[e, 0:1, :] * LN2 + jnp.log(l), (HEAD, bq)))
        o_ref[...] = jnp.transpose(jnp.concatenate(outs, axis=0))
        lse_ref[...] = jnp.transpose(jnp.concatenate(lses, axis=0))

    return pl.pallas_call(
        body, name="mla_fwd", grid=(2, nq),
        in_specs=[pl.BlockSpec((bq, 256), lambda j, i: (i, j)),
                  pl.BlockSpec((T, 256), lambda j, i: (0, j)),
                  pl.BlockSpec((1, nk, LANES, bk), lambda j, i: (j, 0, 0, 0))],
        out_specs=[pl.BlockSpec((bq, LANES), lambda j, i: (i, j)),
                   pl.BlockSpec((bq, LANES), lambda j, i: (i, j))],
        out_shape=[jax.ShapeDtypeStruct((T, 256), F32), jax.ShapeDtypeStruct((T, 256), F32)],
        scratch_shapes=[pltpu.VMEM((2, HEAD, bq), F32), pltpu.VMEM((2, 8, bq), F32), pltpu.VMEM((2, 8, bq), F32)],
        compiler_params=_cparams(("parallel", "arbitrary")),
    )(qm, km, vt)


def _mla_bwd(qm, km, vm, y, lse, dy):
    T = qm.shape[0]
    bq, bk = min(MLA_BQ, T), ATT_BLK
    nq, nsub, nk = T // bq, bq // bk, T // bk

    def body(q_ref, k_ref, v_ref, y_ref, lse_ref, dy_ref, dq_ref, dkt_ref, dvt_ref, dob_ref, st_ref, qt_ref, dot_ref):
        qi = pl.program_id(1)

        @pl.when(qi == 0)
        def _():
            dkt_ref[...] = jnp.zeros_like(dkt_ref)
            dvt_ref[...] = jnp.zeros_like(dvt_ref)

        lane = lax.broadcasted_iota(jnp.int32, (1, LANES), 1) // HEAD
        row, col = _causal_masks(bq, bk)
        dq_ref[...] = jnp.zeros_like(dq_ref)
        lse = lse_ref[...]
        lse_other = pltpu.roll(lse, HEAD, 1)
        qt_ref[...] = jnp.transpose(q_ref[...].astype(F32)).astype(BF16)
        dot_ref[...] = jnp.transpose(dy_ref[...]).astype(BF16)
        for e in range(2):
            do = jnp.where(lane == e, dy_ref[...], 0.0)
            dob_ref[e] = do.astype(BF16)
            st_ref[2 * e] = jnp.where(lane == e, lse, lse_other) * LOG2E
            st_ref[2 * e + 1] = jnp.broadcast_to(jnp.sum(do * y_ref[...], axis=1, keepdims=True), (bq, LANES))

        hss = [slice(e * LANES, (e + 1) * LANES) for e in range(2)]
        tile = lambda a: jnp.concatenate([a] * (bk // LANES), axis=1)

        def step(kb0, masked):
            kbs = [kb0 + d for d in range(nsub)]
            rows = [pl.ds(pl.multiple_of(kb * bk, bk), bk) for kb in kbs]
            pairs = [(d, e) for d in range(nsub) for e in range(2)]
            ss = {(d, e): _dot_nt(q_ref[:, hss[e]], k_ref[rows[d], hss[e]]) for d, e in pairs}
            dps = {(d, e): _dot_nt(dob_ref[e], jnp.where(lane == e, v_ref[rows[d], :], 0)) for d, e in pairs}
            ps, dss = {}, {}
            for d, e in pairs:
                s = jnp.where(col + d * bk <= row, ss[d, e], NEG) if masked else ss[d, e]
                p = jnp.exp2(s - tile(st_ref[2 * e]))
                dss[d, e] = (p * (dps[d, e] - tile(st_ref[2 * e + 1]))).astype(BF16)
                ps[d, e] = p.astype(BF16)
            for d, e in pairs:
                dvt_ref[0, kbs[d], e * HEAD:(e + 1) * HEAD, :] += _dot(dot_ref[e * HEAD:(e + 1) * HEAD, :], ps[d, e])
            for d, e in pairs:
                dkt_ref[0, kbs[d], hss[e], :] += _dot(qt_ref[hss[e], :], dss[d, e])
            for e in range(2):
                dq = dq_ref[:, hss[e]]
                for d in range(nsub):
                    dq = dq + _dot(dss[d, e], k_ref[rows[d], hss[e]])
                dq_ref[:, hss[e]] = dq

        step(qi * nsub, True)

        def loop(t, c):
            step(t * nsub, False)
            return c

        lax.fori_loop(0, qi, loop, 0)
        dq_ref[...] *= LN2

    return pl.pallas_call(
        body, name="mla_bwd", grid=(2, nq),
        in_specs=[pl.BlockSpec((bq, 256), lambda j, i: (i, j)),
                  pl.BlockSpec((T, 256), lambda j, i: (0, j)),
                  pl.BlockSpec((T, LANES), lambda j, i: (0, j)),
                  pl.BlockSpec((bq, LANES), lambda j, i: (i, j)),
                  pl.BlockSpec((bq, LANES), lambda j, i: (i, j)),
                  pl.BlockSpec((bq, LANES), lambda j, i: (i, j))],
        out_specs=[pl.BlockSpec((bq, 256), lambda j, i: (i, j)),
                   pl.BlockSpec((1, nk, 256, bk), lambda j, i: (j, 0, 0, 0)),
                   pl.BlockSpec((1, nk, LANES, bk), lambda j, i: (j, 0, 0, 0))],
        out_shape=[jax.ShapeDtypeStruct((T, 512), F32),
                   jax.ShapeDtypeStruct((2, nk, 256, bk), F32),
                   jax.ShapeDtypeStruct((2, nk, LANES, bk), F32)],
        scratch_shapes=[pltpu.VMEM((2, bq, LANES), BF16), pltpu.VMEM((4, bq, LANES), F32),
                        pltpu.VMEM((256, bq), BF16), pltpu.VMEM((LANES, bq), BF16)],
        compiler_params=_cparams(("parallel", "arbitrary")),
    )(qm, km, vm, y, lse, dy)


def _suffix_ones(n):
    r = lax.broadcasted_iota(jnp.int32, (n, n), 0)
    c = lax.broadcasted_iota(jnp.int32, (n, n), 1)
    return (r >= c).astype(BF16)


def _prefix_ones(n):
    r = lax.broadcasted_iota(jnp.int32, (n, n), 0)
    c = lax.broadcasted_iota(jnp.int32, (n, n), 1)
    return (r <= c).astype(BF16)


def _tri_sum(x, u):
    hi, lo = _split(x)
    return _dot(hi, u) + _dot(lo, u)


def _sb_specs(T, bq):
    qo, ko, vo = (_INT_OFF[n] // LANES for n in ("d_q", "d_k", "d_v"))
    return [pl.BlockSpec((bq, LANES), lambda j, i: (i, qo + j)),
            pl.BlockSpec((T, LANES), lambda j, i: (0, ko + j)),
            pl.BlockSpec((T, LANES), lambda j, i: (0, vo + j))]


def _sb_fwd(hb):
    T = hb.shape[0]
    bq = bk = ATT_BLK
    nq = T // bq

    def body(q_ref, k_ref, v_ref, o_ref, tot_ref, cnt_ref, qm_ref, car_ref):
        qi = pl.program_id(1)
        lane = lax.broadcasted_iota(jnp.int32, (1, LANES), 1) // HEAD
        row, col = _causal_masks(bq, bk)
        strict = col < row
        u = _suffix_ones(bk)
        o_ref[...] = jnp.zeros_like(o_ref)
        car_ref[...] = jnp.zeros_like(car_ref)
        for e in range(2):
            qm_ref[e] = jnp.where(lane == e, q_ref[...], 0) * 0.125

        def step(blocks):
            tile = lambda a: jnp.concatenate([a] * (bk // LANES), axis=1)
            rows = [pl.ds(pl.multiple_of(kb * bk, bk), bk) for kb, _ in blocks]
            pairs = [(b, e) for b in range(len(blocks)) for e in range(2)]
            zs = {(b, e): _dot_nt(qm_ref[e], k_ref[rows[b], :]) for b, e in pairs}
            splits = {}
            for b, e in pairs:
                z = zs[b, e]
                lk = jnp.minimum(-z, 0.0) - jnp.log(1.0 + jnp.exp(-jnp.abs(z)))
                if blocks[b][1] is not None:
                    lk = jnp.where(blocks[b][1], lk, 0.0)
                splits[b, e] = _split(lk)
            sufs = {be: _dot(hi, u) + _dot(lo, u) for be, (hi, lo) in splits.items()}
            car = [car_ref[0], car_ref[1]]
            aas = {}
            for b, e in pairs:
                a = jnp.exp(zs[b, e] + sufs[b, e] + tile(car[e]))
                if blocks[b][1] is not None:
                    a = jnp.where(blocks[b][1], a, 0.0)
                aas[b, e] = a.astype(BF16)
                car[e] = car[e] + jnp.broadcast_to(sufs[b, e][:, 0:1], (bq, LANES))
            acc = o_ref[...]
            for b, e in pairs:
                acc = acc + _dot(aas[b, e], jnp.where(lane == e, v_ref[rows[b], :], 0))
            o_ref[...] = acc
            car_ref[0], car_ref[1] = car

        step([(qi, strict), (jnp.maximum(qi - 1, 0), qi > 0)])

        def live():
            return jnp.max(jnp.maximum(car_ref[0], car_ref[1])) >= SB_DEAD

        def cond(c):
            return jnp.logical_and(c[0] < qi, c[1])

        def loop(c):
            step([(qi - 1 - c[0], None)])
            return c[0] + 1, live()

        done, _ = lax.while_loop(cond, loop, (jnp.minimum(qi, 1), live()))
        tot_ref[...] = jnp.where(lane == 0, car_ref[0], car_ref[1])
        cnt_ref[pl.program_id(0), qi] = done.astype(F32)

    return pl.pallas_call(
        body, name="sb_fwd", grid=(2, nq), in_specs=_sb_specs(T, bq),
        out_specs=[pl.BlockSpec((bq, LANES), lambda j, i: (i, j)), pl.BlockSpec((bq, LANES), lambda j, i: (i, j)),
                   pl.BlockSpec(memory_space=pltpu.SMEM)],
        out_shape=[jax.ShapeDtypeStruct((T, 256), F32), jax.ShapeDtypeStruct((T, 256), F32),
                   jax.ShapeDtypeStruct((2, nq), F32)],
        scratch_shapes=[pltpu.VMEM((2, bq, LANES), BF16), pltpu.VMEM((2, bq, LANES), F32)],
        compiler_params=_cparams(("parallel", "arbitrary")),
    )(hb, hb, hb)


def _sb_bwd(hb, tot, cnt, dy):
    T = hb.shape[0]
    bq = bk = ATT_BLK
    nq = T // bq

    def body(q_ref, k_ref, v_ref, tot_ref, dy_ref, cnt_ref, dq_ref, dk_ref, dv_ref, qm_ref, dob_ref, dqa_ref, rem_ref,
             cg_ref):
        qi = pl.program_id(1)

        @pl.when(qi == 0)
        def _():
            dk_ref[...] = jnp.zeros_like(dk_ref)
            dv_ref[...] = jnp.zeros_like(dv_ref)

        lane = lax.broadcasted_iota(jnp.int32, (1, LANES), 1) // HEAD
        row, col = _causal_masks(bq, bk)
        strict = col < row
        u = _prefix_ones(bk)
        tot = tot_ref[...]
        tot_other = pltpu.roll(tot, HEAD, 1)
        dqa_ref[...] = jnp.zeros_like(dqa_ref)
        cg_ref[...] = jnp.zeros_like(cg_ref)
        for e in range(2):
            qm_ref[e] = jnp.where(lane == e, q_ref[...], 0) * 0.125
            dob_ref[e] = jnp.where(lane == e, dy_ref[...], 0.0).astype(BF16)
            rem_ref[e] = jnp.where(lane == e, tot, tot_other)

        def step(blocks):
            tile = lambda a: jnp.concatenate([a] * (bk // LANES), axis=1)
            nb = len(blocks)
            rows = [pl.ds(pl.multiple_of(kb * bk, bk), bk) for kb, _ in blocks]
            pairs = [(b, e) for b in range(nb) for e in range(2)]
            mask = lambda b, x: x if blocks[b][1] is None else jnp.where(blocks[b][1], x, 0.0)
            zs = {(b, e): _dot_nt(qm_ref[e], k_ref[rows[b], :]) for b, e in pairs}
            das = {(b, e): _dot_nt(dob_ref[e], jnp.where(lane == e, v_ref[rows[b], :], 0)) for b, e in pairs}
            zls, splits = {}, {}
            for b, e in pairs:
                z = zs[b, e]
                lk = mask(b, jnp.minimum(-z, 0.0) - jnp.log(1.0 + jnp.exp(-jnp.abs(z))))
                zls[b, e] = z + lk
                splits[b, e] = _split(lk)
            pres = {be: _dot(hi, u) + _dot(lo, u) for be, (hi, lo) in splits.items()}
            rem = [rem_ref[0], rem_ref[1]]
            aas, gs, gsplits = {}, {}, {}
            for b, e in pairs:
                a = mask(b, jnp.exp(zls[b, e] + (tile(rem[e]) - pres[b, e])))
                gs[b, e] = a * das[b, e]
                aas[b, e] = a.astype(BF16)
                gsplits[b, e] = _split(gs[b, e])
                rem[e] = rem[e] - jnp.broadcast_to(pres[b, e][:, bk - 1:bk], (bq, LANES))
            for b in range(nb):
                dv_ref[rows[b], :] += _dot_tn(aas[b, 0], dob_ref[0]) + _dot_tn(aas[b, 1], dob_ref[1])
            gpres = {be: _dot(hi, u) + _dot(lo, u) for be, (hi, lo) in gsplits.items()}
            cg = [cg_ref[0], cg_ref[1]]
            dzs = {}
            for b, e in pairs:
                dz = mask(b, gs[b, e] - jnp.exp(zls[b, e]) * (tile(cg[e]) + gpres[b, e]))
                dzs[b, e] = dz.astype(BF16)
                cg[e] = cg[e] + jnp.broadcast_to(gpres[b, e][:, bk - 1:bk], (bq, LANES))
            for b in range(nb):
                dk_ref[rows[b], :] += _dot_tn(dzs[b, 0], qm_ref[0]) + _dot_tn(dzs[b, 1], qm_ref[1])
            for e in range(2):
                dq = dqa_ref[e]
                for b in range(nb):
                    dq = dq + _dot(dzs[b, e], k_ref[rows[b], :])
                dqa_ref[e] = dq
            rem_ref[0], rem_ref[1] = rem
            cg_ref[0], cg_ref[1] = cg

        def loop(kb, c):
            step([(kb, None)])
            return c

        start = qi - jnp.clip(cnt_ref[pl.program_id(0), qi].astype(jnp.int32), 0, qi)
        lax.fori_loop(start, qi - 1, loop, 0)
        step([(jnp.maximum(qi - 1, 0), qi > 0), (qi, strict)])
        dq_ref[...] = (jnp.where(lane == 0, dqa_ref[0], dqa_ref[1]) * 0.125).astype(BF16)

    return pl.pallas_call(
        body, name="sb_bwd", grid=(2, nq),
        in_specs=_sb_specs(T, bq) + [pl.BlockSpec((bq, LANES), lambda j, i: (i, j)),
                                     pl.BlockSpec((bq, LANES), lambda j, i: (i, j)),
                                     pl.BlockSpec(memory_space=pltpu.SMEM)],
        out_specs=[pl.BlockSpec((bq, LANES), lambda j, i: (i, j)),
                   pl.BlockSpec((T, LANES), lambda j, i: (0, j)),
                   pl.BlockSpec((T, LANES), lambda j, i: (0, j))],
        out_shape=[jax.ShapeDtypeStruct((T, 256), BF16)] + [jax.ShapeDtypeStruct((T, 256), F32)] * 2,
        scratch_shapes=[pltpu.VMEM((2, bq, LANES), BF16), pltpu.VMEM((2, bq, LANES), BF16),
                        pltpu.VMEM((2, bq, LANES), F32), pltpu.VMEM((2, bq, LANES), F32),
                        pltpu.VMEM((2, bq, LANES), F32)],
        compiler_params=_cparams(("parallel", "arbitrary")),
    )(hb, hb, hb, tot, dy, cnt)


EP_TM = 256


def _ep_in_specs(tm, rev):
    idx = (lambda i: rev - i) if rev is not None else (lambda i: i)
    bo = (_INT_OFF["b_b"] - N_HB) // 256
    halo = lambda i: jnp.maximum(idx(i) * (tm // 8) - 1, 0)
    return [pl.BlockSpec((tm, 256), lambda i: (idx(i), 0)),
            pl.BlockSpec((tm, 256), lambda i: (idx(i), 0)),
            pl.BlockSpec((tm, 256), lambda i: (idx(i), 0)),
            pl.BlockSpec((tm, D_MODEL), lambda i: (idx(i), 0)),
            pl.BlockSpec((tm, 256), lambda i: (idx(i), bo)),
            pl.BlockSpec((tm, 256), lambda i: (idx(i), bo + 1)),
            pl.BlockSpec((tm, 256), lambda i: (idx(i), bo + 2)),
            pl.BlockSpec((8, 256), lambda i: (halo(i), bo + 1)),
            pl.BlockSpec((8, 256), lambda i: (halo(i), bo + 2)),
            pl.BlockSpec((3, 256), lambda i: (0, 0)),
            pl.BlockSpec((1, 256), lambda i: (0, 0)),
            pl.BlockSpec((1, D_MODEL), lambda i: (0, 0)),
            pl.BlockSpec((D_MODEL, D_MODEL), lambda i: (0, 0)),
            pl.BlockSpec((1, D_MODEL), lambda i: (0, 0))]


def _ep_mix(first, ya_ref, yc_ref, yd_ref, gate_ref, bb_ref, bc_ref, bx_ref, hc_ref, hx_ref, cw_ref, cb_ref, gg_ref):
    tm = ya_ref.shape[0]
    u = bc_ref[...] * bx_ref[...]
    halo = jnp.where(first, 0.0, hc_ref[...] * hx_ref[...])
    row = lax.broadcasted_iota(jnp.int32, (tm, 1), 0)
    u1 = jnp.where(row == 0, halo[7:8, :], pltpu.roll(u, 1, 0))
    u2 = jnp.where(row == 0, halo[6:7, :], jnp.where(row == 1, halo[7:8, :], pltpu.roll(u, 2, 0)))
    cw = cw_ref[...]
    conv = cw[0:1, :] * u2 + cw[1:2, :] * u1 + cw[2:3, :] * u + cb_ref[...]
    bb = bb_ref[...]
    ys = [ya_ref[...], bb * conv, yc_ref[...], yd_ref[...]]
    rs = [_rms(y) for y in ys]
    gg = gg_ref[...]
    yhat = jnp.concatenate([y * r for y, r in zip(ys, rs)], axis=1)
    gate = gate_ref[...]
    sig = 1.0 / (1.0 + jnp.exp(-gate))
    return u, u1, u2, conv, bb, rs, yhat, yhat * gg, gate, sig


def _epilogue_fwd(x, ya, yc, yd, hf, conv_w, conv_b, g_grp, w_out, g_post):
    T = x.shape[0]
    tm = EP_TM

    def body(x_ref, ya_ref, yc_ref, yd_ref, gate_ref, bb_ref, bc_ref, bx_ref, hc_ref, hx_ref, cw_ref, cb_ref,
             gg_ref, wo_ref, gp_ref, o_ref):
        (_, _, _, _, _, _, _, yn, gate, sig) = _ep_mix(
            pl.program_id(0) == 0, ya_ref, yc_ref, yd_ref, gate_ref, bb_ref, bc_ref, bx_ref, hc_ref, hx_ref,
            cw_ref, cb_ref, gg_ref)
        z = _dot((yn * (gate * sig)).astype(BF16), wo_ref[...])
        o_ref[...] = x_ref[...] + z * _rms(z) * gp_ref[...]

    return pl.pallas_call(
        body, name="epilogue_fwd", grid=(T // tm,),
        in_specs=[pl.BlockSpec((tm, D_MODEL), lambda i: (i, 0))] + _ep_in_specs(tm, None),
        out_specs=pl.BlockSpec((tm, D_MODEL), lambda i: (i, 0)),
        out_shape=jax.ShapeDtypeStruct((T, D_MODEL), F32),
        compiler_params=_cparams(("parallel",)),
    )(x, ya, yc, yd, hf, hf, hf, hf, hf, hf, conv_w, conv_b, g_grp, w_out, g_post)


def _epilogue_bwd(dxn, ya, yc, yd, hf, conv_w, conv_b, g_grp, w_out, g_post):
    T = dxn.shape[0]
    tm = EP_TM
    nt = T // tm
    ridx = lambda i: (nt - 1 - i, 0)

    def body(dx_ref, ya_ref, yc_ref, yd_ref, gate_ref, bb_ref, bc_ref, bx_ref, hc_ref, hx_ref, cw_ref, cb_ref,
             gg_ref, wo_ref, gp_ref,
             dya_ref, dyc_ref, dyd_ref, dhf_ref, dwo_ref, dgp_ref, dgg_ref, dcw_ref, dcb_ref, carry_ref):
        i = pl.program_id(0)

        @pl.when(i == 0)
        def _():
            for r in (dwo_ref, dgp_ref, dgg_ref, dcw_ref, dcb_ref, carry_ref):
                r[...] = jnp.zeros_like(r)

        (u, u1, u2, conv, bb, rs, yhat, yn, gate, sig) = _ep_mix(
            i == nt - 1, ya_ref, yc_ref, yd_ref, gate_ref, bb_ref, bc_ref, bx_ref, hc_ref, hx_ref,
            cw_ref, cb_ref, gg_ref)
        silu = gate * sig
        ymix = (yn * silu).astype(BF16)
        z = _dot(ymix, wo_ref[...])
        rz = _rms(z)
        dz, dgrow = _rms_bwd(dx_ref[...], z * rz, rz, gp_ref[...])
        dgp_ref[...] += _colsum(dgrow)
        dzb = dz.astype(BF16)
        dwo_ref[...] += _dot_tn(ymix, dzb)
        dymix = _dot_nt(dzb, wo_ref[...])
        dhf_ref[:, 0:D_MODEL] = (dymix * yn * (sig * (1.0 + gate * (1.0 - sig)))).astype(BF16)
        dyn = dymix * silu
        dgg_ref[...] += _colsum(dyn * yhat)
        gg = gg_ref[...]
        dys = []
        for gi in range(4):
            sl = slice(gi * GROUP, (gi + 1) * GROUP)
            dyh = dyn[:, sl] * gg[:, sl]
            yh = yhat[:, sl]
            dys.append(rs[gi] * (dyh - yh * jnp.mean(dyh * yh, axis=-1, keepdims=True)))
        dya_ref[...] = dys[0]
        dyc_ref[...] = dys[2]
        dyd_ref[...] = dys[3]
        dyb = dys[1]
        dhf_ref[:, D_MODEL:D_MODEL + 256] = (dyb * conv).astype(BF16)
        dconv = dyb * bb
        dcb_ref[...] += _colsum(dconv)
        dcw_ref[0:1, :] += _colsum(dconv * u2)
        dcw_ref[1:2, :] += _colsum(dconv * u1)
        dcw_ref[2:3, :] += _colsum(dconv * u)
        carry = carry_ref[...]
        row = lax.broadcasted_iota(jnp.int32, (tm, 1), 0)
        d1 = jnp.where(row == tm - 1, carry[0:1, :], pltpu.roll(dconv, tm - 1, 0))
        d2 = jnp.where(row == tm - 2, carry[0:1, :],
                       jnp.where(row == tm - 1, carry[1:2, :], pltpu.roll(dconv, tm - 2, 0)))
        cw = cw_ref[...]
        du = cw[2:3, :] * dconv + cw[1:2, :] * d1 + cw[0:1, :] * d2
        dhf_ref[:, D_MODEL + 256:D_MODEL + 512] = (du * bx_ref[...]).astype(BF16)
        dhf_ref[:, D_MODEL + 512:D_MODEL + 768] = (du * bc_ref[...]).astype(BF16)
        carry_ref[...] = dconv[0:8, :]

    in_specs = [pl.BlockSpec((tm, D_MODEL), ridx)] + _ep_in_specs(tm, nt - 1)
    return pl.pallas_call(
        body, name="epilogue_bwd", grid=(nt,), in_specs=in_specs,
        out_specs=[pl.BlockSpec((tm, 256), ridx), pl.BlockSpec((tm, 256), ridx), pl.BlockSpec((tm, 256), ridx),
                   pl.BlockSpec((tm, D_MODEL + 768), ridx),
                   pl.BlockSpec((D_MODEL, D_MODEL), lambda i: (0, 0)),
                   pl.BlockSpec((1, D_MODEL), lambda i: (0, 0)),
                   pl.BlockSpec((1, D_MODEL), lambda i: (0, 0)),
                   pl.BlockSpec((8, 256), lambda i: (0, 0)),
                   pl.BlockSpec((1, 256), lambda i: (0, 0))],
        out_shape=[jax.ShapeDtypeStruct((T, 256), F32)] * 3
                  + [jax.ShapeDtypeStruct((T, D_MODEL + 768), BF16),
                     jax.ShapeDtypeStruct((D_MODEL, D_MODEL), F32),
                     jax.ShapeDtypeStruct((1, D_MODEL), F32),
                     jax.ShapeDtypeStruct((1, D_MODEL), F32),
                     jax.ShapeDtypeStruct((8, 256), F32),
                     jax.ShapeDtypeStruct((1, 256), F32)],
        scratch_shapes=[pltpu.VMEM((8, 256), F32)],
        compiler_params=_cparams(("arbitrary",)),
    )(dxn, ya, yc, yd, hf, hf, hf, hf, hf, hf, conv_w, conv_b, g_grp, w_out, g_post)


def _loss_head(y, tgt):
    T = y.shape[0]
    tm = 512

    def body(y_ref, t_ref, dy_ref, l_ref):
        @pl.when(pl.program_id(0) == 0)
        def _():
            l_ref[...] = jnp.zeros_like(l_ref)

        d = y_ref[...] - t_ref[...]
        dy_ref[...] = d * (1.0 / D_MODEL)
        part = jnp.sum(jnp.sum(d * d, axis=1, keepdims=True), axis=0, keepdims=True)
        l_ref[...] += part * (0.5 / D_MODEL)

    return pl.pallas_call(
        body, name="loss_head", grid=(T // tm,),
        in_specs=[pl.BlockSpec((tm, D_MODEL), lambda i: (i, 0))] * 2,
        out_specs=[pl.BlockSpec((tm, D_MODEL), lambda i: (i, 0)), pl.BlockSpec((8, LANES), lambda i: (0, 0))],
        out_shape=[jax.ShapeDtypeStruct((T, D_MODEL), F32), jax.ShapeDtypeStruct((8, LANES), F32)],
        compiler_params=_cparams(("arbitrary",)),
    )(y, tgt)


def _place():
    return lax.axis_index("x"), lax.axis_index("y"), lax.axis_index("c")


def _other_chips(x, y):
    return [(1 - x, y), (x, 1 - y), (1 - x, 1 - y)]


HBM = pl.BlockSpec(memory_space=pl.ANY)


def _gather_weights(shards):
    n = len(shards)

    def body(*refs):
        ins, outs = refs[:n], refs[n:2 * n]
        ici_send, ici_recv, d2d_send, d2d_recv, local_sems = refs[2 * n:]
        x, y, c = _place()
        me = 2 * x + y
        chips = _other_chips(x, y)

        def ici(a, j, layer_from):
            px, py = chips[j]
            return pltpu.make_async_remote_copy(
                src_ref=ins[a].at[c], dst_ref=outs[a].at[layer_from, c], send_sem=ici_send.at[3 * a + j],
                recv_sem=ici_recv.at[3 * a + j], device_id=(px, py, c), device_id_type=MESH)

        def d2d(a, j, layer):
            px, py = chips[j]
            blk = outs[a].at[2 * px + py, layer]
            return pltpu.make_async_remote_copy(
                src_ref=blk, dst_ref=blk, send_sem=d2d_send.at[3 * a + j], recv_sem=d2d_recv.at[3 * a + j],
                device_id=(x, y, 1 - c), device_id_type=MESH)

        local = [pltpu.make_async_copy(ins[a], outs[a].at[me], local_sems.at[a]) for a in range(n)]
        for cp in local:
            cp.start()
        sends = [ici(a, j, me) for j in range(3) for a in range(n)]
        for cp in sends:
            cp.start()
        for j in range(3):
            px, py = chips[j]
            for a in range(n):
                ici(a, j, 2 * px + py).wait_recv()
                fwd = d2d(a, j, c)
                fwd.start()
                sends.append(fwd)
        for j in range(3):
            for a in range(n):
                d2d(a, j, 1 - c).wait_recv()
        for cp in sends:
            cp.wait_send()
        for cp in local:
            cp.wait()

    return pl.pallas_call(
        body, name="gather_weights",
        in_specs=[HBM] * n, out_specs=[HBM] * n,
        out_shape=[jax.ShapeDtypeStruct((4,) + s.shape, s.dtype) for s in shards],
        scratch_shapes=[pltpu.SemaphoreType.DMA((3 * n,))] * 4 + [pltpu.SemaphoreType.DMA((n,))],
    )(*shards)


def _exchange_chips(parts, small):
    n = len(parts)

    def body(*refs):
        ins, sm_ref = refs[:n], refs[n]
        outs, osm_ref = refs[n + 1:2 * n + 1], refs[2 * n + 1]
        send_sems, recv_sems, ssend_sems, srecv_sems, local_sems = refs[2 * n + 2:]
        x, y, c = _place()
        me = 2 * x + y
        dev = 4 * x + 2 * y + c
        local = [pltpu.make_async_copy(ins[a].at[me], outs[a].at[me], local_sems.at[a]) for a in range(n)]
        local.append(pltpu.make_async_copy(sm_ref, osm_ref.at[dev], local_sems.at[n]))
        for cp in local:
            cp.start()
        sends = []
        for j, (px, py) in enumerate(_other_chips(x, y)):
            for a in range(n):
                cp = pltpu.make_async_remote_copy(
                    src_ref=ins[a].at[2 * px + py], dst_ref=outs[a].at[me], send_sem=send_sems.at[3 * a + j],
                    recv_sem=recv_sems.at[3 * a + j], device_id=(px, py, c), device_id_type=MESH)
                cp.start()
                sends.append(cp)
        flips = [(fx, fy, fc) for fx in (0, 1) for fy in (0, 1) for fc in (0, 1)][1:]
        for j, (fx, fy, fc) in enumerate(flips):
            cp = pltpu.make_async_remote_copy(
                src_ref=sm_ref, dst_ref=osm_ref.at[dev], send_sem=ssend_sems.at[j], recv_sem=srecv_sems.at[j],
                device_id=(x ^ fx, y ^ fy, c ^ fc), device_id_type=MESH)
            cp.start()
            sends.append(cp)
        for j, (px, py) in enumerate(_other_chips(x, y)):
            for a in range(n):
                pltpu.make_async_remote_copy(
                    src_ref=ins[a].at[me], dst_ref=outs[a].at[2 * px + py], send_sem=send_sems.at[3 * a + j],
                    recv_sem=recv_sems.at[3 * a + j], device_id=(px, py, c), device_id_type=MESH).wait_recv()
        for j, (fx, fy, fc) in enumerate(flips):
            src = 4 * (x ^ fx) + 2 * (y ^ fy) + (c ^ fc)
            pltpu.make_async_remote_copy(
                src_ref=sm_ref, dst_ref=osm_ref.at[src], send_sem=ssend_sems.at[j], recv_sem=srecv_sems.at[j],
                device_id=(x ^ fx, y ^ fy, c ^ fc), device_id_type=MESH).wait_recv()
        for cp in sends:
            cp.wait_send()
        for cp in local:
            cp.wait()

    return pl.pallas_call(
        body, name="exchange_chips",
        in_specs=[HBM] * (n + 1), out_specs=[HBM] * (n + 1),
        out_shape=[jax.ShapeDtypeStruct(p.shape, p.dtype) for p in parts]
                  + [jax.ShapeDtypeStruct((8,) + small.shape, small.dtype)],
        scratch_shapes=[pltpu.SemaphoreType.DMA((3 * n,)), pltpu.SemaphoreType.DMA((3 * n,)),
                        pltpu.SemaphoreType.DMA((7,)), pltpu.SemaphoreType.DMA((7,)),
                        pltpu.SemaphoreType.DMA((n + 1,))],
    )(*parts, small)


def _swap_cores(parts, name):
    n = len(parts)

    def body(*refs):
        ins, outs, send_sems, recv_sems = refs[:n], refs[n:2 * n], refs[2 * n], refs[2 * n + 1]
        x, y, c = _place()
        copies = [pltpu.make_async_remote_copy(
            src_ref=ins[a], dst_ref=outs[a], send_sem=send_sems.at[a], recv_sem=recv_sems.at[a],
            device_id=(x, y, 1 - c), device_id_type=MESH) for a in range(n)]
        for cp in copies:
            cp.start()
        for cp in copies:
            cp.wait()

    return pl.pallas_call(
        body, name=name, in_specs=[HBM] * n, out_specs=[HBM] * n,
        out_shape=[jax.ShapeDtypeStruct(p.shape, p.dtype) for p in parts],
        scratch_shapes=[pltpu.SemaphoreType.DMA((n,)), pltpu.SemaphoreType.DMA((n,))],
    )(*parts)


def _tile(rows, cols):
    for cand in (256, 128, 64):
        if rows % cand == 0:
            return cand, cols
    if rows > 64 and cols % 256 == 0:
        return rows, 256
    return rows, cols


def _add(a, b, name):
    L, R, C = a.shape
    tr, tc = _tile(R, C)

    def body(a_ref, b_ref, o_ref):
        o_ref[...] = (a_ref[...] + b_ref[...]).astype(BF16)

    spec = pl.BlockSpec((1, tr, tc), lambda l, i, j: (l, i, j))
    return pl.pallas_call(
        body, name=name, grid=(L, R // tr, C // tc), in_specs=[spec, spec], out_specs=spec,
        out_shape=jax.ShapeDtypeStruct((L, R, C), BF16),
        compiler_params=_cparams(("parallel", "parallel", "parallel")),
    )(a, b)


def _sum_leading(buf, name):
    n, R, C = buf.shape
    tr, tc = _tile(R, C)

    def body(b_ref, o_ref):
        acc = b_ref[0].astype(F32)
        for k in range(1, n):
            acc = acc + b_ref[k].astype(F32)
        o_ref[...] = acc

    return pl.pallas_call(
        body, name=name, grid=(R // tr, C // tc),
        in_specs=[pl.BlockSpec((n, tr, tc), lambda i, j: (0, i, j))],
        out_specs=pl.BlockSpec((tr, tc), lambda i, j: (i, j)),
        out_shape=jax.ShapeDtypeStruct((R, C), F32),
        compiler_params=_cparams(("parallel", "parallel")),
    )(buf)


def _adam_update(w, g, m, v):
    c1 = 1.0 / (1.0 - ADAM_B1 ** ADAM_STEP)
    c2 = 1.0 / (1.0 - ADAM_B2 ** ADAM_STEP)
    mn = ADAM_B1 * m + (1.0 - ADAM_B1) * g
    vn = ADAM_B2 * v + (1.0 - ADAM_B2) * (g * g)
    return -ADAM_LR * ((mn * c1) / (jnp.sqrt(vn * c2) + ADAM_EPS) + ADAM_WD * w), mn, vn


def _adamw_layers(w, m, v, g_mine, g_other, name):
    _, R, C = w.shape
    tr, tc = _tile(R, C)

    def body(w_ref, m_ref, v_ref, gm_ref, go_ref, g_ref, d_ref, mo_ref, vo_ref):
        g = jnp.where(pl.program_id(0) == lax.axis_index("c"), gm_ref[...], go_ref[...])
        g_ref[0] = g
        d_ref[0], mo_ref[0], vo_ref[0] = _adam_update(w_ref[0], g, m_ref[0], v_ref[0])

    spec3 = pl.BlockSpec((1, tr, tc), lambda l, i, j: (l, i, j))
    spec2 = pl.BlockSpec((tr, tc), lambda l, i, j: (i, j))
    return pl.pallas_call(
        body, name=name, grid=(2, R // tr, C // tc),
        in_specs=[spec3] * 3 + [spec2] * 2, out_specs=[spec3] * 4,
        out_shape=[jax.ShapeDtypeStruct(w.shape, F32)] * 4,
        compiler_params=_cparams(("parallel", "parallel", "parallel")),
    )(w, m, v, g_mine, g_other)


PACK_C = 1024
_BIG = ("w_in", "w_out", "mla_w_uq", "mla_w_ukv", "conv_w")
_SMALL = ("norm_pre", "group_norm", "norm_post", "conv_b", "mla_q_norm", "mla_kv_norm", "attn_sinks")
_SMALL_W = {"norm_pre": 1024, "group_norm": 1024, "norm_post": 1024, "conv_b": 256, "mla_q_norm": 256,
            "mla_kv_norm": 128, "attn_sinks": 4}


_LOSS_AT = divmod(DEPTH * sum(_SMALL_W.values()), PACK_C)


def _pack_small(d, loss):
    flat = jnp.concatenate([d[n].reshape(-1) for n in _SMALL] + [loss.reshape(1)])
    return jnp.pad(flat, (0, 8 * PACK_C - flat.shape[0])).reshape(8, PACK_C)


def _adamw_small(w, m, v, got):
    ns = len(_SMALL)

    def body(*refs):
        got_ref = refs[3 * ns]
        outs = refs[3 * ns + 1:]
        gsum = got_ref[0]
        for d in range(1, 8):
            gsum = gsum + got_ref[d]
        outs[4 * ns][...] = gsum[_LOSS_AT[0]:_LOSS_AT[0] + 1, _LOSS_AT[1]:_LOSS_AT[1] + 1]
        off = 0
        for i, name in enumerate(_SMALL):
            wd = _SMALL_W[name]
            rows = []
            for l in range(DEPTH):
                r, c0 = divmod(off + l * wd, PACK_C)
                rows.append(gsum[r:r + 1, c0:c0 + wd])
            off += DEPTH * wd
            g = jnp.concatenate(rows, axis=0)
            delta, mn, vn = _adam_update(refs[i][...], g, refs[ns + i][...], refs[2 * ns + i][...])
            outs[i][...] = g
            outs[ns + i][...] = delta
            outs[2 * ns + i][...] = mn
            outs[3 * ns + i][...] = vn

    shapes = [jax.ShapeDtypeStruct(w[n].shape, F32) for n in _SMALL]
    res = pl.pallas_call(body, name="adamw_small", out_shape=shapes * 4 + [jax.ShapeDtypeStruct((1, 1), F32)])(
        *[w[n] for n in _SMALL], *[m[n] for n in _SMALL], *[v[n] for n in _SMALL], got)
    return [dict(zip(_SMALL, res[k * ns:(k + 1) * ns])) for k in range(4)], res[4 * ns]


def _w_in_internal(wt):
    rows = []
    for n in _INT_ORDER:
        o, wd = _REAL_OFF[n]
        rows.append(wt[o:o + wd])
        if _INT_W[n] != wd:
            rows.append(jnp.zeros((_INT_W[n] - wd, wt.shape[1]), wt.dtype))
    return jnp.concatenate(rows, axis=0)


def _w_in_real(dw):
    return jnp.concatenate([dw[:, _INT_OFF[n]:_INT_OFF[n] + wd] for n, wd in _REAL], axis=1)


def _uq_internal(w):
    return jnp.pad(w.reshape(256, 4, 96), ((0, 0), (0, 0), (0, 32))).reshape(256, 512)


def _uq_real(dw):
    return dw.reshape(256, 4, 128)[:, :, :96].reshape(256, 384)


def _ukv_internal(w):
    w4 = w.reshape(128, 4, 128)
    k = jnp.pad(w4[:, :, :64], ((0, 0), (0, 0), (0, 64))).reshape(128, 512)
    return jnp.concatenate([k, w4[:, :, 64:].reshape(128, 256)], axis=1)


def _ukv_real(dw):
    k = dw[:, :512].reshape(128, 4, 128)[:, :, :64]
    v = dw[:, 512:].reshape(128, 4, 64)
    return jnp.concatenate([k, v], axis=2).reshape(128, 512)


def _layer_fwd(x, pos, p):
    xt, hb, hf = _inproj_fwd(x, p["norm_pre"], p["w_in"])
    ya = _swa_fwd(hb, p["attn_sinks"])
    qm, km, vm, vt = _mla_prep_fwd(hf, pos, p["mla_q_norm"], p["mla_kv_norm"], p["mla_w_uq"], p["mla_w_ukv"])
    yc, lse = _mla_fwd(qm, km, vt)
    yd, tot, cnt = _sb_fwd(hb)
    x_next = _epilogue_fwd(x, ya, yc, yd, hf, p["conv_w"], p["conv_b"], p["group_norm"], p["w_out"], p["norm_post"])
    return x_next, dict(x=x, xt=xt, hb=hb, hf=hf, ya=ya, yc=yc, yd=yd, tot=tot, cnt=cnt, qm=qm, km=km, vm=vm, lse=lse)


def _layer_bwd(dx_next, pos, p, s):
    (dya, dyc, dyd, dhf, dw_out, dg_post, dg_grp, dconv_w, dconv_b) = _epilogue_bwd(
        dx_next, s["ya"], s["yc"], s["yd"], s["hf"], p["conv_w"], p["conv_b"], p["group_norm"], p["w_out"],
        p["norm_post"])
    dq_d, dk_d, dv_d = _sb_bwd(s["hb"], s["tot"], s["cnt"], dyd)
    dqm, dkt, dvt = _mla_bwd(s["qm"], s["km"], s["vm"], s["yc"], s["lse"], dyc)
    dc, dw_uq, dw_ukv, dg_q, dg_kv = _mla_prep_bwd(
        s["hf"], pos, p["mla_q_norm"], p["mla_kv_norm"], p["mla_w_uq"], p["mla_w_ukv"], dqm, dkt, dvt)
    dq_a, dk_a, dv_a, dsinks = _swa_bwd(s["hb"], p["attn_sinks"], dya)
    dx, dh, dg_pre = _inproj_bwd_dx(s["x"], p["norm_pre"], p["w_in"], dx_next,
                                    [dq_a, dk_a, dv_a, dq_d, dk_d, dv_d, dhf, dc])
    dw_in = _matmul_over_tokens(s["xt"], dh, "inproj_bwd_dw")
    grads = dict(norm_pre=dg_pre[0], w_in=_w_in_real(dw_in), attn_sinks=dsinks[0, :4], conv_w=dconv_w[:3],
                 conv_b=dconv_b[0], mla_q_norm=dg_q[0], mla_w_uq=_uq_real(dw_uq), mla_kv_norm=dg_kv[0],
                 mla_w_ukv=_ukv_real(dw_ukv), group_norm=dg_grp[0], w_out=dw_out, norm_post=dg_post[0])
    return dx, grads


_WEIGHTS = ["norm_pre", "w_in", "attn_sinks", "conv_w", "conv_b", "mla_q_norm", "mla_w_uq", "mla_kv_norm",
            "mla_w_ukv", "group_norm", "w_out", "norm_post"]


def kernel(x, positions, norm_pre, w_in, attn_sinks, conv_w, conv_b, mla_q_norm, mla_w_uq, mla_kv_norm, mla_w_ukv, group_norm, w_out, norm_post, loss_target, m_norm_pre, m_w_in, m_attn_sinks, m_conv_w, m_conv_b, m_mla_q_norm, m_mla_w_uq, m_mla_kv_norm, m_mla_w_ukv, m_group_norm, m_w_out, m_norm_post, v_norm_pre, v_w_in, v_attn_sinks, v_conv_w, v_conv_b, v_mla_q_norm, v_mla_w_uq, v_mla_kv_norm, v_mla_w_ukv, v_group_norm, v_w_out, v_norm_post):
    w = dict(norm_pre=norm_pre, w_in=w_in, attn_sinks=attn_sinks, conv_w=conv_w, conv_b=conv_b,
             mla_q_norm=mla_q_norm, mla_w_uq=mla_w_uq, mla_kv_norm=mla_kv_norm, mla_w_ukv=mla_w_ukv,
             group_norm=group_norm, w_out=w_out, norm_post=norm_post)
    m = dict(norm_pre=m_norm_pre, w_in=m_w_in, attn_sinks=m_attn_sinks, conv_w=m_conv_w, conv_b=m_conv_b,
             mla_q_norm=m_mla_q_norm, mla_w_uq=m_mla_w_uq, mla_kv_norm=m_mla_kv_norm, mla_w_ukv=m_mla_w_ukv,
             group_norm=m_group_norm, w_out=m_w_out, norm_post=m_norm_post)
    v = dict(norm_pre=v_norm_pre, w_in=v_w_in, attn_sinks=v_attn_sinks, conv_w=v_conv_w, conv_b=v_conv_b,
             mla_q_norm=v_mla_q_norm, mla_w_uq=v_mla_w_uq, mla_kv_norm=v_mla_kv_norm, mla_w_ukv=v_mla_w_ukv,
             group_norm=v_group_norm, w_out=v_w_out, norm_post=v_norm_post)
    T = x.shape[1]
    xs = x[0]
    pos = positions[0].reshape(T, 1)
    tgt = loss_target[0]
    core = lax.axis_index("c")

    gathered = _gather_weights([jnp.swapaxes(w["w_in"], 1, 2).astype(BF16)]
                               + [w[n].astype(BF16) for n in _BIG[1:4]] + [w["conv_w"]])
    full = {}
    for n, got in zip(_BIG, gathered):
        if n in ("w_in", "w_out"):
            full[n] = jnp.moveaxis(got, 0, 1).reshape(DEPTH, 4 * got.shape[2], got.shape[3])
        else:
            full[n] = jnp.transpose(got, (1, 2, 0, 3)).reshape(DEPTH, got.shape[2], 4 * got.shape[3])

    layers = []
    for l in range(DEPTH):
        layers.append(dict(
            norm_pre=norm_pre[l:l + 1], w_in=_w_in_internal(full["w_in"][l]), attn_sinks=attn_sinks[l],
            conv_w=full["conv_w"][l], conv_b=conv_b[l:l + 1], mla_q_norm=mla_q_norm[l:l + 1],
            mla_w_uq=_uq_internal(full["mla_w_uq"][l]), mla_kv_norm=mla_kv_norm[l:l + 1],
            mla_w_ukv=_ukv_internal(full["mla_w_ukv"][l]), group_norm=group_norm[l:l + 1],
            w_out=full["w_out"][l], norm_post=norm_post[l:l + 1]))

    saved = []
    h = xs
    for l in range(DEPTH):
        h, s = _layer_fwd(h, pos, layers[l])
        saved.append(s)
    dy, loss_part = _loss_head(h, tgt)

    grads = [None] * DEPTH
    for l in reversed(range(DEPTH)):
        dy, grads[l] = _layer_bwd(dy, pos, layers[l], saved[l])

    turned = ("w_in", "mla_w_uq")
    turn = lambda n, a: jnp.swapaxes(a, -1, -2) if n in turned else a

    def chunks(n, a):
        if n == "w_out":
            return a.reshape(4, D_MODEL // 4, D_MODEL)
        if n in turned:
            return a.T.reshape(4, a.shape[1] // 4, a.shape[0])
        return jnp.transpose(a.reshape(a.shape[0], 4, a.shape[1] // 4), (1, 0, 2))

    mine = [chunks(n, jnp.where(core == 0, grads[0][n], grads[1][n])) for n in _BIG]
    theirs = [chunks(n, jnp.where(core == 0, grads[1][n], grads[0][n])) for n in _BIG]
    from_sibling = _swap_cores(theirs, "swap_layer_chunks")
    summed = [_add(a, b, "add_cores_" + n) for n, a, b in zip(_BIG, mine, from_sibling)]
    small = _pack_small({n: jnp.stack([grads[l][n] for l in range(DEPTH)]) for n in _SMALL}, loss_part[0, 0])
    *got, got_small = _exchange_chips(summed, small)
    done = [_sum_leading(b, "sum_chips_" + n) for n, b in zip(_BIG, got)]
    done_other = _swap_cores(done, "swap_layer_shards")

    outs, loss = _adamw_small(w, m, v, got_small)
    for n, gm, go in zip(_BIG, done, done_other):
        for d, a in zip(outs, _adamw_layers(turn(n, w[n]), turn(n, m[n]), turn(n, v[n]), gm, go, "adamw_" + n)):
            d[n] = turn(n, a)
    return (loss[0, 0], dy[None], *[outs[0][n] for n in _WEIGHTS], *[outs[1][n] for n in _WEIGHTS],
            *[outs[2][n] for n in _WEIGHTS], *[outs[3][n] for n in _WEIGHTS])
```

```python
import math

import jax
import jax.numpy as jnp
from jax import lax
from jax.experimental import pallas as pl
from jax.experimental.pallas import tpu as pltpu

F32 = jnp.float32
BF16 = jnp.bfloat16
MESH = pl.DeviceIdType.MESH

D_MODEL = 1024
DEPTH = 2
EPS = 1e-6
BLOCK = 128
HEAD = 64
LANES = 128
GROUP = 256
LOG2E = 1.4426950408889634
LN2 = 0.6931471805599453
MLA_QSCALE = 96 ** -0.5 * LOG2E
ROPE_HALF = 16
ROPE_THETA = 10000.0
ATT_BLK = 256
MLA_BQ = 512
NEG = -1e30
SB_DEAD = -104.0

ADAM_LR, ADAM_B1, ADAM_B2, ADAM_EPS, ADAM_WD, ADAM_STEP = 0.001, 0.9, 0.999, 1e-08, 0.01, 10

_REAL = [("a_q", 256), ("a_k", 128), ("a_v", 128), ("b_b", 256), ("b_c", 256), ("b_x", 256),
         ("c_q", 256), ("c_kv", 128), ("c_kr", 32), ("d_q", 256), ("d_k", 256), ("d_v", 256),
         ("gate", 1024)]
_REAL_OFF = {}
_o = 0
for _n, _w in _REAL:
    _REAL_OFF[_n] = (_o, _w)
    _o += _w
D_IN = _o
_INT_ORDER = ["a_q", "a_k", "a_v", "d_q", "d_k", "d_v", "gate", "b_b", "b_c", "b_x", "c_q", "c_kv", "c_kr"]
_INT_W = dict(_REAL)
_INT_W["c_kr"] = 128
_INT_OFF = {}
_o = 0
for _n in _INT_ORDER:
    _INT_OFF[_n] = _o
    _o += _INT_W[_n]
N_INT = _o
N_HB = _INT_OFF["gate"]
N_HF = N_INT - N_HB

VMEM_LIMIT = 56 * 1024 * 1024


def _cparams(sem):
    return pltpu.CompilerParams(dimension_semantics=sem, vmem_limit_bytes=VMEM_LIMIT)


def _dot(a, b):
    return jnp.dot(a, b, preferred_element_type=F32)


def _dot_nt(a, b):
    return lax.dot_general(a, b, (((1,), (1,)), ((), ())), preferred_element_type=F32)


def _dot_tn(a, b):
    return lax.dot_general(a, b, (((0,), (0,)), ((), ())), preferred_element_type=F32)


def _split(x):
    hi = x.astype(BF16)
    lo = (x - hi.astype(F32)).astype(BF16)
    return hi, lo


def _rms(x):
    return lax.rsqrt(jnp.mean(x * x, axis=-1, keepdims=True) + EPS)


def _rms_bwd(dy, xhat, r, g):
    dxhat = dy * g
    return r * (dxhat - xhat * jnp.mean(dxhat * xhat, axis=-1, keepdims=True)), dy * xhat


def _colsum(x):
    return jnp.sum(x, axis=0, keepdims=True)


def _inproj_fwd(x, g, wt):
    T = x.shape[0]
    tm = 256

    def body(x_ref, g_ref, w_ref, xn_ref, hb_ref, hf_ref):
        xv = x_ref[...]
        xn = (xv * _rms(xv) * g_ref[...]).astype(BF16)
        xn_ref[...] = xn
        h = _dot_nt(xn, w_ref[...])
        hb_ref[...] = h[:, :N_HB].astype(BF16)
        hf_ref[...] = h[:, N_HB:]

    return pl.pallas_call(
        body, name="inproj_fwd", grid=(T // tm,),
        in_specs=[pl.BlockSpec((tm, D_MODEL), lambda i: (i, 0)),
                  pl.BlockSpec((1, D_MODEL), lambda i: (0, 0)),
                  pl.BlockSpec((N_INT, D_MODEL), lambda i: (0, 0))],
        out_specs=[pl.BlockSpec((tm, D_MODEL), lambda i: (i, 0)),
                   pl.BlockSpec((tm, N_HB), lambda i: (i, 0)),
                   pl.BlockSpec((tm, N_HF), lambda i: (i, 0))],
        out_shape=[jax.ShapeDtypeStruct((T, D_MODEL), BF16),
                   jax.ShapeDtypeStruct((T, N_HB), BF16),
                   jax.ShapeDtypeStruct((T, N_HF), F32)],
        compiler_params=_cparams(("parallel",)),
    )(x, g, wt)


def _inproj_bwd_dx(x, g, wt, dx_next, pieces):
    T = x.shape[0]
    tm = 256
    widths = [p.shape[1] for p in pieces]
    assert sum(widths) == N_INT

    def body(x_ref, g_ref, w_ref, dxn_ref, *rest):
        p_refs = rest[:len(pieces)]
        dx_ref, dh_ref, dg_ref = rest[len(pieces):]
        dh = jnp.concatenate([p[...].astype(BF16) for p in p_refs], axis=1)
        dh_ref[...] = dh
        dxn = _dot(dh, w_ref[...])
        xv = x_ref[...]
        r = _rms(xv)
        dx, dgrow = _rms_bwd(dxn, xv * r, r, g_ref[...])
        dx_ref[...] = dx + dxn_ref[...]

        @pl.when(pl.program_id(0) == 0)
        def _():
            dg_ref[...] = jnp.zeros_like(dg_ref)

        dg_ref[...] += _colsum(dgrow)

    return pl.pallas_call(
        body, name="inproj_bwd_dx", grid=(T // tm,),
        in_specs=[pl.BlockSpec((tm, D_MODEL), lambda i: (i, 0)),
                  pl.BlockSpec((1, D_MODEL), lambda i: (0, 0)),
                  pl.BlockSpec((N_INT, D_MODEL), lambda i: (0, 0)),
                  pl.BlockSpec((tm, D_MODEL), lambda i: (i, 0))]
                 + [pl.BlockSpec((tm, wd), lambda i: (i, 0)) for wd in widths],
        out_specs=[pl.BlockSpec((tm, D_MODEL), lambda i: (i, 0)),
                   pl.BlockSpec((tm, N_INT), lambda i: (i, 0)),
                   pl.BlockSpec((1, D_MODEL), lambda i: (0, 0))],
        out_shape=[jax.ShapeDtypeStruct((T, D_MODEL), F32),
                   jax.ShapeDtypeStruct((T, N_INT), BF16),
                   jax.ShapeDtypeStruct((1, D_MODEL), F32)],
        compiler_params=_cparams(("arbitrary",)),
    )(x, g, wt, dx_next, *pieces)


def _grad_over_tokens(a, b, name):
    T, M = a.shape
    N = b.shape[1]
    tm, tn = min(1024, T), 512

    def body(a_ref, b_ref, o_ref):
        @pl.when(pl.program_id(1) == 0)
        def _():
            o_ref[...] = jnp.zeros_like(o_ref)

        o_ref[...] += _dot_tn(b_ref[...], a_ref[...])

    return pl.pallas_call(
        body, name=name, grid=(N // tn, T // tm),
        in_specs=[pl.BlockSpec((tm, M), lambda j, t: (t, 0)),
                  pl.BlockSpec((tm, tn), lambda j, t: (t, j))],
        out_specs=pl.BlockSpec((tn, M), lambda j, t: (j, 0)),
        out_shape=jax.ShapeDtypeStruct((N, M), F32),
        compiler_params=_cparams(("parallel", "arbitrary")),
    )(a, b)


def _roll_f32(x, shift):
    return pltpu.roll(x.astype(F32), shift, 1)


def _swa_operands(h, q_ref, kp_ref, kc_ref, vp_ref, vc_ref):
    p, e = h // 2, h % 2
    lane = lax.broadcasted_iota(jnp.int32, (1, LANES), 1) // HEAD
    q = q_ref[:, p * LANES:(p + 1) * LANES]
    k_prev, k_cur, v_prev, v_cur = kp_ref[...], kc_ref[...], vp_ref[...], vc_ref[...]
    if e != p:
        q = _roll_f32(q, HEAD).astype(BF16)
        v_prev = _roll_f32(v_prev, HEAD).astype(BF16)
        v_cur = _roll_f32(v_cur, HEAD).astype(BF16)
    qs = jnp.where(lane == p, q, 0) * 0.125
    return dict(p=p, e=e, lane=lane, qs=qs, k_prev=k_prev, k_cur=k_cur,
                v_prev=jnp.where(lane == e, v_prev, 0), v_cur=jnp.where(lane == e, v_cur, 0),
                s_prev=_dot_nt(qs, k_prev), s_cur=_dot_nt(qs, k_cur))


def _swa_probs(ops, sink, first):
    row = lax.broadcasted_iota(jnp.int32, (BLOCK, BLOCK), 0)
    col = lax.broadcasted_iota(jnp.int32, (BLOCK, BLOCK), 1)
    s_prev = jnp.where(jnp.logical_and(col > row, jnp.logical_not(first)), ops["s_prev"], NEG)
    s_cur = jnp.where(col <= row, ops["s_cur"], NEG)
    m = jnp.maximum(jnp.maximum(jnp.max(s_prev, axis=1, keepdims=True),
                                jnp.max(s_cur, axis=1, keepdims=True)), sink)
    p_prev = jnp.exp(s_prev - m)
    p_cur = jnp.exp(s_cur - m)
    p_sink = jnp.exp(sink - m)
    inv = 1.0 / (jnp.sum(p_prev, axis=1, keepdims=True) + jnp.sum(p_cur, axis=1, keepdims=True) + p_sink)
    return p_prev * inv, p_cur * inv, p_sink * inv


def _swa_specs(T):
    nb = T // BLOCK
    qo, ko, vo = (_INT_OFF[n] // LANES for n in ("a_q", "a_k", "a_v"))
    prev = lambda i: jnp.maximum(i - 1, 0)
    return [pl.BlockSpec((BLOCK, 256), lambda i: (i, qo // 2)),
            pl.BlockSpec((BLOCK, LANES), lambda i: (prev(i), ko)),
            pl.BlockSpec((BLOCK, LANES), lambda i: (i, ko)),
            pl.BlockSpec((BLOCK, LANES), lambda i: (prev(i), vo)),
            pl.BlockSpec((BLOCK, LANES), lambda i: (i, vo)),
            pl.BlockSpec(memory_space=pltpu.SMEM)], nb


def _swa_fwd(hb, sinks):
    T = hb.shape[0]
    specs, nb = _swa_specs(T)

    def body(q_ref, kp_ref, kc_ref, vp_ref, vc_ref, s_ref, o_ref):
        first = pl.program_id(0) == 0
        ops = [_swa_operands(h, q_ref, kp_ref, kc_ref, vp_ref, vc_ref) for h in range(4)]
        probs = [_swa_probs(ops[h], s_ref[h], first) for h in range(4)]
        outs = [_dot(probs[h][0].astype(BF16), ops[h]["v_prev"]) + _dot(probs[h][1].astype(BF16), ops[h]["v_cur"])
                for h in range(4)]
        for p in range(2):
            o_ref[:, p * LANES:(p + 1) * LANES] = outs[2 * p] + outs[2 * p + 1]

    return pl.pallas_call(
        body, name="swa_fwd", grid=(nb,), in_specs=specs,
        out_specs=pl.BlockSpec((BLOCK, 256), lambda i: (i, 0)),
        out_shape=jax.ShapeDtypeStruct((T, 256), F32),
        compiler_params=_cparams(("parallel",)),
    )(hb, hb, hb, hb, hb, sinks)


def _swa_bwd(hb, sinks, dy):
    T = hb.shape[0]
    specs, nb = _swa_specs(T)

    def body(q_ref, kp_ref, kc_ref, vp_ref, vc_ref, s_ref, dy_ref, dq_ref, dk_ref, dv_ref, ds_ref):
        i = pl.program_id(0)
        first = i == 0
        cur = pl.ds(pl.multiple_of(i * BLOCK, BLOCK), BLOCK)
        prv = pl.ds(pl.multiple_of(jnp.maximum(i - 1, 0) * BLOCK, BLOCK), BLOCK)

        @pl.when(first)
        def _():
            ds_ref[...] = jnp.zeros_like(ds_ref)

        dk_ref[cur, :] = jnp.zeros((BLOCK, LANES), F32)
        dv_ref[cur, :] = jnp.zeros((BLOCK, LANES), F32)
        lane_id = lax.broadcasted_iota(jnp.int32, (8, LANES), 1)
        heads = range(4)
        ops = [_swa_operands(h, q_ref, kp_ref, kc_ref, vp_ref, vc_ref) for h in heads]
        probs = [_swa_probs(ops[h], s_ref[h], first) for h in heads]
        dos = [jnp.where(ops[h]["lane"] == ops[h]["e"], dy_ref[:, ops[h]["p"] * LANES:(ops[h]["p"] + 1) * LANES], 0.0)
               for h in heads]
        dobs = [d.astype(BF16) for d in dos]
        pbs = [(probs[h][0].astype(BF16), probs[h][1].astype(BF16)) for h in heads]
        outs = [_dot(pbs[h][0], ops[h]["v_prev"]) + _dot(pbs[h][1], ops[h]["v_cur"]) for h in heads]
        dps = [(_dot_nt(dobs[h], ops[h]["v_prev"]), _dot_nt(dobs[h], ops[h]["v_cur"])) for h in heads]
        dss, dsinks = [], jnp.zeros((8, LANES), F32)
        for h in heads:
            delta = jnp.sum(dos[h] * outs[h], axis=1, keepdims=True)
            dss.append(((probs[h][0] * (dps[h][0] - delta)).astype(BF16),
                        (probs[h][1] * (dps[h][1] - delta)).astype(BF16)))
            dsink = -jnp.sum(probs[h][2] * delta, axis=0, keepdims=True)
            dsinks += jnp.where(lane_id == h, dsink, 0.0)
        ds_ref[...] += dsinks
        dqs = [(_dot(dss[h][0], ops[h]["k_prev"]) + _dot(dss[h][1], ops[h]["k_cur"])) * 0.125 for h in heads]
        dk_prev = dk_cur = dv_prev = dv_cur = jnp.zeros((BLOCK, LANES), F32)
        for h in heads:
            p, e = ops[h]["p"], ops[h]["e"]
            dob_v = dobs[h] if e == p else pltpu.roll(dos[h], HEAD, 1).astype(BF16)
            dk_prev += _dot_tn(dss[h][0], ops[h]["qs"])
            dk_cur += _dot_tn(dss[h][1], ops[h]["qs"])
            dv_prev += _dot_tn(pbs[h][0], dob_v)
            dv_cur += _dot_tn(pbs[h][1], dob_v)
        dk_ref[prv, :] += dk_prev
        dk_ref[cur, :] += dk_cur
        dv_ref[prv, :] += dv_prev
        dv_ref[cur, :] += dv_cur
        for p in range(2):
            dq_pair = jnp.zeros((BLOCK, LANES), F32)
            for e in range(2):
                dq = jnp.where(ops[2 * p + e]["lane"] == p, dqs[2 * p + e], 0.0)
                dq_pair += dq if e == p else pltpu.roll(dq, HEAD, 1)
            dq_ref[:, p * LANES:(p + 1) * LANES] = dq_pair.astype(BF16)

    return pl.pallas_call(
        body, name="swa_bwd", grid=(nb,),
        in_specs=specs + [pl.BlockSpec((BLOCK, 256), lambda i: (i, 0))],
        out_specs=[pl.BlockSpec((BLOCK, 256), lambda i: (i, 0)),
                   pl.BlockSpec((T, LANES), lambda i: (0, 0)),
                   pl.BlockSpec((T, LANES), lambda i: (0, 0)),
                   pl.BlockSpec((8, LANES), lambda i: (0, 0))],
        out_shape=[jax.ShapeDtypeStruct((T, 256), BF16),
                   jax.ShapeDtypeStruct((T, LANES), F32),
                   jax.ShapeDtypeStruct((T, LANES), F32),
                   jax.ShapeDtypeStruct((8, LANES), F32)],
        compiler_params=_cparams(("arbitrary",)),
    )(hb, hb, hb, hb, hb, sinks, dy)


def _rope_tables(pos_ref):
    lane = lax.broadcasted_iota(jnp.int32, (1, LANES), 1)
    active = jnp.logical_and(lane >= HEAD, lane < HEAD + 2 * ROPE_HALF)
    idx = ((lane - HEAD) % ROPE_HALF).astype(F32)
    freq = jnp.exp(idx * (-math.log(ROPE_THETA) / ROPE_HALF))
    ang = pos_ref[...].astype(F32) * freq
    cos, sin = jnp.cos(ang), jnp.sin(ang)
    c = jnp.where(active, cos, 1.0)
    s_up = jnp.where(jnp.logical_and(active, lane >= HEAD + ROPE_HALF), sin, 0.0)
    s_dn = jnp.where(jnp.logical_and(active, lane < HEAD + ROPE_HALF), -sin, 0.0)
    return c, s_up, s_dn


def _rope(x, tabs):
    c, s_up, s_dn = tabs
    return x * c + pltpu.roll(x, ROPE_HALF, 1) * s_up + pltpu.roll(x, LANES - ROPE_HALF, 1) * s_dn


def _rope_t(dy, tabs):
    c, s_up, s_dn = tabs
    return dy * c + pltpu.roll(dy * s_up, LANES - ROPE_HALF, 1) + pltpu.roll(dy * s_dn, ROPE_HALF, 1)


def _mla_lat_specs(tm):
    cq, ckv, ckr = ((_INT_OFF[n] - N_HB) for n in ("c_q", "c_kv", "c_kr"))
    return [pl.BlockSpec((tm, 256), lambda i: (i, cq // 256)),
            pl.BlockSpec((tm, LANES), lambda i: (i, ckv // LANES)),
            pl.BlockSpec((tm, LANES), lambda i: (i, ckr // LANES)),
            pl.BlockSpec((tm, 1), lambda i: (i, 0)),
            pl.BlockSpec((1, 256), lambda i: (0, 0)),
            pl.BlockSpec((1, LANES), lambda i: (0, 0)),
            pl.BlockSpec((256, 512), lambda i: (0, 0)),
            pl.BlockSpec((LANES, 768), lambda i: (0, 0))]


def _mla_prep_fwd(hf, pos, g_q, g_kv, w_uq, w_ukv):
    T = hf.shape[0]
    tm = 512
    sub = tm // ATT_BLK

    def body(cq_ref, ckv_ref, ckr_ref, pos_ref, gq_ref, gkv_ref, wq_ref, wkv_ref, qm_ref, km_ref, vm_ref, vt_ref):
        tabs = _rope_tables(pos_ref)
        cq = cq_ref[...]
        q = _dot((cq * _rms(cq) * gq_ref[...]).astype(BF16), wq_ref[...])
        ckv = ckv_ref[...]
        kv = _dot((ckv * _rms(ckv) * gkv_ref[...]).astype(BF16), wkv_ref[...])
        kr = _rope(pltpu.roll(ckr_ref[...], HEAD, 1), tabs)
        for h in range(4):
            sl = slice(h * LANES, (h + 1) * LANES)
            qm_ref[:, sl] = (_rope(q[:, sl], tabs) * MLA_QSCALE).astype(BF16)
            km_ref[:, sl] = (kv[:, sl] + kr).astype(BF16)
        vm_ref[...] = kv[:, 512:].astype(BF16)
        for p in range(2):
            for s in range(sub):
                tile = kv[s * ATT_BLK:(s + 1) * ATT_BLK, 512 + p * LANES:512 + (p + 1) * LANES]
                vt_ref[p, s] = jnp.transpose(tile).astype(BF16)

    return pl.pallas_call(
        body, name="mla_prep_fwd", grid=(T // tm,), in_specs=_mla_lat_specs(tm),
        out_specs=[pl.BlockSpec((tm, 512), lambda i: (i, 0)),
                   pl.BlockSpec((tm, 512), lambda i: (i, 0)),
                   pl.BlockSpec((tm, 256), lambda i: (i, 0)),
                   pl.BlockSpec((2, sub, LANES, ATT_BLK), lambda i: (0, i, 0, 0))],
        out_shape=[jax.ShapeDtypeStruct((T, 512), BF16),
                   jax.ShapeDtypeStruct((T, 512), BF16),
                   jax.ShapeDtypeStruct((T, 256), BF16),
                   jax.ShapeDtypeStruct((2, T // ATT_BLK, LANES, ATT_BLK), BF16)],
        compiler_params=_cparams(("parallel",)),
    )(hf, hf, hf, pos, g_q, g_kv, w_uq, w_ukv)


def _mla_prep_bwd(hf, pos, g_q, g_kv, w_uq, w_ukv, dqm, dkt, dvt):
    T = hf.shape[0]
    tm = 512
    sub = tm // ATT_BLK

    def body(cq_ref, ckv_ref, ckr_ref, pos_ref, gq_ref, gkv_ref, wq_ref, wkv_ref, dq_ref, dk_ref, dv_ref,
             dc_ref, dwq_ref, dwkv_ref, dgq_ref, dgkv_ref):
        @pl.when(pl.program_id(0) == 0)
        def _():
            dwq_ref[...] = jnp.zeros_like(dwq_ref)
            dwkv_ref[...] = jnp.zeros_like(dwkv_ref)
            dgq_ref[...] = jnp.zeros_like(dgq_ref)
            dgkv_ref[...] = jnp.zeros_like(dgkv_ref)

        tabs = _rope_tables(pos_ref)
        lane = lax.broadcasted_iota(jnp.int32, (1, LANES), 1)
        dq = jnp.concatenate([_rope_t(dq_ref[:, h * LANES:(h + 1) * LANES] * MLA_QSCALE, tabs)
                              for h in range(4)], axis=1).astype(BF16)
        cq = cq_ref[...]
        rq = _rms(cq)
        cqn = (cq * rq * gq_ref[...]).astype(BF16)
        dwq_ref[...] += _dot_tn(cqn, dq)
        dcq, dgrow = _rms_bwd(_dot_nt(dq, wq_ref[...]), cq * rq, rq, gq_ref[...])
        dgq_ref[...] += _colsum(dgrow)
        dc_ref[:, 0:256] = dcq.astype(BF16)

        dk = jnp.concatenate([jnp.concatenate([jnp.transpose(dk_ref[p, s]) for p in range(2)], axis=1)
                              for s in range(sub)], axis=0) * LN2
        dv = jnp.concatenate([jnp.concatenate([jnp.transpose(dv_ref[p, s]) for p in range(2)], axis=1)
                              for s in range(sub)], axis=0)
        dkr = dk[:, 0:LANES] + dk[:, LANES:2 * LANES] + dk[:, 2 * LANES:3 * LANES] + dk[:, 3 * LANES:]
        dkr = pltpu.roll(_rope_t(dkr, tabs), HEAD, 1)
        dc_ref[:, 384:512] = jnp.where(lane < 2 * ROPE_HALF, dkr, 0.0).astype(BF16)
        dkv = jnp.concatenate([dk.astype(BF16), dv.astype(BF16)], axis=1)
        ckv = ckv_ref[...]
        rkv = _rms(ckv)
        ckvn = (ckv * rkv * gkv_ref[...]).astype(BF16)
        dwkv_ref[...] += _dot_tn(ckvn, dkv)
        dckv, dgrow = _rms_bwd(_dot_nt(dkv, wkv_ref[...]), ckv * rkv, rkv, gkv_ref[...])
        dgkv_ref[...] += _colsum(dgrow)
        dc_ref[:, 256:384] = dckv.astype(BF16)

    return pl.pallas_call(
        body, name="mla_prep_bwd", grid=(T // tm,),
        in_specs=_mla_lat_specs(tm) + [pl.BlockSpec((tm, 512), lambda i: (i, 0)),
                                       pl.BlockSpec((2, sub, 256, ATT_BLK), lambda i: (0, i, 0, 0)),
                                       pl.BlockSpec((2, sub, LANES, ATT_BLK), lambda i: (0, i, 0, 0))],
        out_specs=[pl.BlockSpec((tm, 512), lambda i: (i, 0)),
                   pl.BlockSpec((256, 512), lambda i: (0, 0)),
                   pl.BlockSpec((LANES, 768), lambda i: (0, 0)),
                   pl.BlockSpec((1, 256), lambda i: (0, 0)),
                   pl.BlockSpec((1, LANES), lambda i: (0, 0))],
        out_shape=[jax.ShapeDtypeStruct((T, 512), BF16),
                   jax.ShapeDtypeStruct((256, 512), F32),
                   jax.ShapeDtypeStruct((LANES, 768), F32),
                   jax.ShapeDtypeStruct((1, 256), F32),
                   jax.ShapeDtypeStruct((1, LANES), F32)],
        compiler_params=_cparams(("arbitrary",)),
    )(hf, hf, hf, pos, g_q, g_kv, w_uq, w_ukv, dqm, dkt, dvt)


def _causal_masks(bq, bk):
    row = lax.broadcasted_iota(jnp.int32, (bq, bk), 0)
    col = lax.broadcasted_iota(jnp.int32, (bq, bk), 1)
    return row, col


def _mla_fwd(qm, km, vt):
    T = qm.shape[0]
    bq, bk = min(MLA_BQ, T), ATT_BLK
    nq, nsub, nk = T // bq, bq // bk, T // bk

    def body(q_ref, k_ref, vt_ref, o_ref, lse_ref, acc_ref, m_ref, l_ref):
        qi = pl.program_id(1)
        key = lax.broadcasted_iota(jnp.int32, (bk, bq), 0)
        qry = lax.broadcasted_iota(jnp.int32, (bk, bq), 1)
        ones = jnp.ones((8, bk), BF16)
        acc_ref[...] = jnp.zeros_like(acc_ref)
        m_ref[...] = jnp.full_like(m_ref, NEG)
        l_ref[...] = jnp.zeros_like(l_ref)

        def step(kb0, masked):
            kbs = [kb0 + d for d in range(nsub)]
            sts = [[_dot_nt(k_ref[pl.ds(pl.multiple_of(kb * bk, bk), bk), e * LANES:(e + 1) * LANES],
                            q_ref[:, e * LANES:(e + 1) * LANES]) for kb in kbs] for e in range(2)]
            pts, alphas = [], []
            for e in range(2):
                st = [jnp.where(key + d * bk <= qry, sts[e][d], NEG) for d in range(nsub)] if masked else sts[e]
                m_prev = m_ref[e, 0:1, :]
                m_new = m_prev
                for d in range(nsub):
                    m_new = jnp.maximum(m_new, jnp.max(st[d], axis=0, keepdims=True))
                alpha = jnp.exp2(m_prev - m_new)
                pt = [jnp.exp2(st[d] - m_new).astype(BF16) for d in range(nsub)]
                l_new = alpha * l_ref[e]
                for d in range(nsub):
                    l_new = l_new + _dot(ones, pt[d])
                l_ref[e] = l_new
                m_ref[e] = jnp.broadcast_to(m_new, (8, bq))
                pts.append(pt)
                alphas.append(alpha)
            for e in range(2):
                acc = alphas[e] * acc_ref[e]
                for d in range(nsub):
                    acc = acc + _dot(vt_ref[0, kbs[d], e * HEAD:(e + 1) * HEAD, :], pts[e][d])
                acc_ref[e] = acc

        step(qi * nsub, True)

        def loop(t, c):
            step(t * nsub, False)
            return c

        lax.fori_loop(0, qi, loop, 0)
        outs, lses = [], []
        for e in range(2):
            l = l_ref[e, 0:1, :]
            outs.append(acc_ref[e] / l)
            lses.append(jnp.broadcast_to(m_ref[e, 0:1, :] * LN2 + jnp.log(l), (HEAD, bq)))
        o_ref[...] = jnp.transpose(jnp.concatenate(outs, axis=0))
        lse_ref[...] = jnp.transpose(jnp.concatenate(lses, axis=0))

    return pl.pallas_call(
        body, name="mla_fwd", grid=(2, nq),
        in_specs=[pl.BlockSpec((bq, 256), lambda j, i: (i, j)),
                  pl.BlockSpec((T, 256), lambda j, i: (0, j)),
                  pl.BlockSpec((1, nk, LANES, bk), lambda j, i: (j, 0, 0, 0))],
        out_specs=[pl.BlockSpec((bq, LANES), lambda j, i: (i, j)),
                   pl.BlockSpec((bq, LANES), lambda j, i: (i, j))],
        out_shape=[jax.ShapeDtypeStruct((T, 256), F32), jax.ShapeDtypeStruct((T, 256), F32)],
        scratch_shapes=[pltpu.VMEM((2, HEAD, bq), F32), pltpu.VMEM((2, 8, bq), F32), pltpu.VMEM((2, 8, bq), F32)],
        compiler_params=_cparams(("parallel", "arbitrary")),
    )(qm, km, vt)


def _mla_bwd(qm, km, vm, y, lse, dy):
    T = qm.shape[0]
    bq, bk = min(MLA_BQ, T), ATT_BLK
    nq, nsub, nk = T // bq, bq // bk, T // bk

    def body(q_ref, k_ref, v_ref, y_ref, lse_ref, dy_ref, dq_ref, dkt_ref, dvt_ref, dob_ref, st_ref, qt_ref, dot_ref):
        qi = pl.program_id(1)

        @pl.when(qi == 0)
        def _():
            dkt_ref[...] = jnp.zeros_like(dkt_ref)
            dvt_ref[...] = jnp.zeros_like(dvt_ref)

        lane = lax.broadcasted_iota(jnp.int32, (1, LANES), 1) // HEAD
        row, col = _causal_masks(bq, bk)
        dq_ref[...] = jnp.zeros_like(dq_ref)
        lse = lse_ref[...]
        lse_other = pltpu.roll(lse, HEAD, 1)
        qt_ref[...] = jnp.transpose(q_ref[...].astype(F32)).astype(BF16)
        dot_ref[...] = jnp.transpose(dy_ref[...]).astype(BF16)
        for e in range(2):
            do = jnp.where(lane == e, dy_ref[...], 0.0)
            dob_ref[e] = do.astype(BF16)
            st_ref[2 * e] = jnp.where(lane == e, lse, lse_other) * LOG2E
            st_ref[2 * e + 1] = jnp.broadcast_to(jnp.sum(do * y_ref[...], axis=1, keepdims=True), (bq, LANES))

        hss = [slice(e * LANES, (e + 1) * LANES) for e in range(2)]
        tile = lambda a: jnp.concatenate([a] * (bk // LANES), axis=1)

        def step(kb0, masked):
            kbs = [kb0 + d for d in range(nsub)]
            rows = [pl.ds(pl.multiple_of(kb * bk, bk), bk) for kb in kbs]
            pairs = [(d, e) for d in range(nsub) for e in range(2)]
            ss = {(d, e): _dot_nt(q_ref[:, hss[e]], k_ref[rows[d], hss[e]]) for d, e in pairs}
            dps = {(d, e): _dot_nt(dob_ref[e], jnp.where(lane == e, v_ref[rows[d], :], 0)) for d, e in pairs}
            ps, dss = {}, {}
            for d, e in pairs:
                s = jnp.where(col + d * bk <= row, ss[d, e], NEG) if masked else ss[d, e]
                p = jnp.exp2(s - tile(st_ref[2 * e]))
                dss[d, e] = (p * (dps[d, e] - tile(st_ref[2 * e + 1]))).astype(BF16)
                ps[d, e] = p.astype(BF16)
            for d, e in pairs:
                dvt_ref[0, kbs[d], e * HEAD:(e + 1) * HEAD, :] += _dot(dot_ref[e * HEAD:(e + 1) * HEAD, :], ps[d, e])
            for d, e in pairs:
                dkt_ref[0, kbs[d], hss[e], :] += _dot(qt_ref[hss[e], :], dss[d, e])
            for e in range(2):
                dq = dq_ref[:, hss[e]]
                for d in range(nsub):
                    dq = dq + _dot(dss[d, e], k_ref[rows[d], hss[e]])
                dq_ref[:, hss[e]] = dq

        step(qi * nsub, True)

        def loop(t, c):
            step(t * nsub, False)
            return c

        lax.fori_loop(0, qi, loop, 0)
        dq_ref[...] *= LN2

    return pl.pallas_call(
        body, name="mla_bwd", grid=(2, nq),
        in_specs=[pl.BlockSpec((bq, 256), lambda j, i: (i, j)),
                  pl.BlockSpec((T, 256), lambda j, i: (0, j)),
                  pl.BlockSpec((T, LANES), lambda j, i: (0, j)),
                  pl.BlockSpec((bq, LANES), lambda j, i: (i, j)),
                  pl.BlockSpec((bq, LANES), lambda j, i: (i, j)),
                  pl.BlockSpec((bq, LANES), lambda j, i: (i, j))],
        out_specs=[pl.BlockSpec((bq, 256), lambda j, i: (i, j)),
                   pl.BlockSpec((1, nk, 256, bk), lambda j, i: (j, 0, 0, 0)),
                   pl.BlockSpec((1, nk, LANES, bk), lambda j, i: (j, 0, 0, 0))],
        out_shape=[jax.ShapeDtypeStruct((T, 512), F32),
                   jax.ShapeDtypeStruct((2, nk, 256, bk), F32),
                   jax.ShapeDtypeStruct((2, nk, LANES, bk), F32)],
        scratch_shapes=[pltpu.VMEM((2, bq, LANES), BF16), pltpu.VMEM((4, bq, LANES), F32),
                        pltpu.VMEM((256, bq), BF16), pltpu.VMEM((LANES, bq), BF16)],
        compiler_params=_cparams(("parallel", "arbitrary")),
    )(qm, km, vm, y, lse, dy)


def _suffix_ones(n):
    r = lax.broadcasted_iota(jnp.int32, (n, n), 0)
    c = lax.broadcasted_iota(jnp.int32, (n, n), 1)
    return (r >= c).astype(BF16)


def _prefix_ones(n):
    r = lax.broadcasted_iota(jnp.int32, (n, n), 0)
    c = lax.broadcasted_iota(jnp.int32, (n, n), 1)
    return (r <= c).astype(BF16)


def _tri_sum(x, u):
    hi, lo = _split(x)
    return _dot(hi, u) + _dot(lo, u)


def _sb_specs(T, bq):
    qo, ko, vo = (_INT_OFF[n] // LANES for n in ("d_q", "d_k", "d_v"))
    return [pl.BlockSpec((bq, LANES), lambda j, i: (i, qo + j)),
            pl.BlockSpec((T, LANES), lambda j, i: (0, ko + j)),
            pl.BlockSpec((T, LANES), lambda j, i: (0, vo + j))]


def _sb_fwd(hb):
    T = hb.shape[0]
    bq = bk = ATT_BLK
    nq = T // bq

    def body(q_ref, k_ref, v_ref, o_ref, tot_ref, cnt_ref, qm_ref, car_ref):
        qi = pl.program_id(1)
        lane = lax.broadcasted_iota(jnp.int32, (1, LANES), 1) // HEAD
        row, col = _causal_masks(bq, bk)
        strict = col < row
        u = _suffix_ones(bk)
        o_ref[...] = jnp.zeros_like(o_ref)
        car_ref[...] = jnp.zeros_like(car_ref)
        for e in range(2):
            qm_ref[e] = jnp.where(lane == e, q_ref[...], 0) * 0.125

        def step(blocks):
            tile = lambda a: jnp.concatenate([a] * (bk // LANES), axis=1)
            rows = [pl.ds(pl.multiple_of(kb * bk, bk), bk) for kb, _ in blocks]
            pairs = [(b, e) for b in range(len(blocks)) for e in range(2)]
            zs = {(b, e): _dot_nt(qm_ref[e], k_ref[rows[b], :]) for b, e in pairs}
            splits = {}
            for b, e in pairs:
                z = zs[b, e]
                lk = jnp.minimum(-z, 0.0) - jnp.log(1.0 + jnp.exp(-jnp.abs(z)))
                if blocks[b][1] is not None:
                    lk = jnp.where(blocks[b][1], lk, 0.0)
                splits[b, e] = _split(lk)
            sufs = {be: _dot(hi, u) + _dot(lo, u) for be, (hi, lo) in splits.items()}
            car = [car_ref[0], car_ref[1]]
            aas = {}
            for b, e in pairs:
                a = jnp.exp(zs[b, e] + sufs[b, e] + tile(car[e]))
                if blocks[b][1] is not None:
                    a = jnp.where(blocks[b][1], a, 0.0)
                aas[b, e] = a.astype(BF16)
                car[e] = car[e] + jnp.broadcast_to(sufs[b, e][:, 0:1], (bq, LANES))
            acc = o_ref[...]
            for b, e in pairs:
                acc = acc + _dot(aas[b, e], jnp.where(lane == e, v_ref[rows[b], :], 0))
            o_ref[...] = acc
            car_ref[0], car_ref[1] = car

        step([(qi, strict), (jnp.maximum(qi - 1, 0), qi > 0)])

        def live():
            return jnp.max(jnp.maximum(car_ref[0], car_ref[1])) >= SB_DEAD

        def cond(c):
            return jnp.logical_and(c[0] < qi, c[1])

        def loop(c):
            step([(qi - 1 - c[0], None)])
            return c[0] + 1, live()

        done, _ = lax.while_loop(cond, loop, (jnp.minimum(qi, 1), live()))
        tot_ref[...] = jnp.where(lane == 0, car_ref[0], car_ref[1])
        cnt_ref[pl.program_id(0), qi] = done.astype(F32)

    return pl.pallas_call(
        body, name="sb_fwd", grid=(2, nq), in_specs=_sb_specs(T, bq),
        out_specs=[pl.BlockSpec((bq, LANES), lambda j, i: (i, j)), pl.BlockSpec((bq, LANES), lambda j, i: (i, j)),
                   pl.BlockSpec(memory_space=pltpu.SMEM)],
        out_shape=[jax.ShapeDtypeStruct((T, 256), F32), jax.ShapeDtypeStruct((T, 256), F32),
                   jax.ShapeDtypeStruct((2, nq), F32)],
        scratch_shapes=[pltpu.VMEM((2, bq, LANES), BF16), pltpu.VMEM((2, bq, LANES), F32)],
        compiler_params=_cparams(("parallel", "arbitrary")),
    )(hb, hb, hb)


def _sb_bwd(hb, tot, cnt, dy):
    T = hb.shape[0]
    bq = bk = ATT_BLK
    nq = T // bq

    def body(q_ref, k_ref, v_ref, tot_ref, dy_ref, cnt_ref, dq_ref, dk_ref, dv_ref, qm_ref, dob_ref, dqa_ref, rem_ref,
             cg_ref):
        qi = pl.program_id(1)

        @pl.when(qi == 0)
        def _():
            dk_ref[...] = jnp.zeros_like(dk_ref)
            dv_ref[...] = jnp.zeros_like(dv_ref)

        lane = lax.broadcasted_iota(jnp.int32, (1, LANES), 1) // HEAD
        row, col = _causal_masks(bq, bk)
        strict = col < row
        u = _prefix_ones(bk)
        tot = tot_ref[...]
        tot_other = pltpu.roll(tot, HEAD, 1)
        dqa_ref[...] = jnp.zeros_like(dqa_ref)
        cg_ref[...] = jnp.zeros_like(cg_ref)
        for e in range(2):
            qm_ref[e] = jnp.where(lane == e, q_ref[...], 0) * 0.125
            dob_ref[e] = jnp.where(lane == e, dy_ref[...], 0.0).astype(BF16)
            rem_ref[e] = jnp.where(lane == e, tot, tot_other)

        def step(blocks):
            tile = lambda a: jnp.concatenate([a] * (bk // LANES), axis=1)
            nb = len(blocks)
            rows = [pl.ds(pl.multiple_of(kb * bk, bk), bk) for kb, _ in blocks]
            pairs = [(b, e) for b in range(nb) for e in range(2)]
            mask = lambda b, x: x if blocks[b][1] is None else jnp.where(blocks[b][1], x, 0.0)
            zs = {(b, e): _dot_nt(qm_ref[e], k_ref[rows[b], :]) for b, e in pairs}
            das = {(b, e): _dot_nt(dob_ref[e], jnp.where(lane == e, v_ref[rows[b], :], 0)) for b, e in pairs}
            zls, splits = {}, {}
            for b, e in pairs:
                z = zs[b, e]
                lk = mask(b, jnp.minimum(-z, 0.0) - jnp.log(1.0 + jnp.exp(-jnp.abs(z))))
                zls[b, e] = z + lk
                splits[b, e] = _split(lk)
            pres = {be: _dot(hi, u) + _dot(lo, u) for be, (hi, lo) in splits.items()}
            rem = [rem_ref[0], rem_ref[1]]
            aas, gs, gsplits = {}, {}, {}
            for b, e in pairs:
                a = mask(b, jnp.exp(zls[b, e] + (tile(rem[e]) - pres[b, e])))
                gs[b, e] = a * das[b, e]
                aas[b, e] = a.astype(BF16)
                gsplits[b, e] = _split(gs[b, e])
                rem[e] = rem[e] - jnp.broadcast_to(pres[b, e][:, bk - 1:bk], (bq, LANES))
            for b in range(nb):
                dv_ref[rows[b], :] += _dot_tn(aas[b, 0], dob_ref[0]) + _dot_tn(aas[b, 1], dob_ref[1])
            gpres = {be: _dot(hi, u) + _dot(lo, u) for be, (hi, lo) in gsplits.items()}
            cg = [cg_ref[0], cg_ref[1]]
            dzs = {}
            for b, e in pairs:
                dz = mask(b, gs[b, e] - jnp.exp(zls[b, e]) * (tile(cg[e]) + gpres[b, e]))
                dzs[b, e] = dz.astype(BF16)
                cg[e] = cg[e] + jnp.broadcast_to(gpres[b, e][:, bk - 1:bk], (bq, LANES))
            for b in range(nb):
                dk_ref[rows[b], :] += _dot_tn(dzs[b, 0], qm_ref[0]) + _dot_tn(dzs[b, 1], qm_ref[1])
            for e in range(2):
                dq = dqa_ref[e]
                for b in range(nb):
                    dq = dq + _dot(dzs[b, e], k_ref[rows[b], :])
                dqa_ref[e] = dq
            rem_ref[0], rem_ref[1] = rem
            cg_ref[0], cg_ref[1] = cg

        def loop(kb, c):
            step([(kb, None)])
            return c

        start = qi - jnp.clip(cnt_ref[pl.program_id(0), qi].astype(jnp.int32), 0, qi)
        lax.fori_loop(start, qi - 1, loop, 0)
        step([(jnp.maximum(qi - 1, 0), qi > 0), (qi, strict)])
        dq_ref[...] = (jnp.where(lane == 0, dqa_ref[0], dqa_ref[1]) * 0.125).astype(BF16)

    return pl.pallas_call(
        body, name="sb_bwd", grid=(2, nq),
        in_specs=_sb_specs(T, bq) + [pl.BlockSpec((bq, LANES), lambda j, i: (i, j)),
                                     pl.BlockSpec((bq, LANES), lambda j, i: (i, j)),
                                     pl.BlockSpec(memory_space=pltpu.SMEM)],
        out_specs=[pl.BlockSpec((bq, LANES), lambda j, i: (i, j)),
                   pl.BlockSpec((T, LANES), lambda j, i: (0, j)),
                   pl.BlockSpec((T, LANES), lambda j, i: (0, j))],
        out_shape=[jax.ShapeDtypeStruct((T, 256), BF16)] + [jax.ShapeDtypeStruct((T, 256), F32)] * 2,
        scratch_shapes=[pltpu.VMEM((2, bq, LANES), BF16), pltpu.VMEM((2, bq, LANES), BF16),
                        pltpu.VMEM((2, bq, LANES), F32), pltpu.VMEM((2, bq, LANES), F32),
                        pltpu.VMEM((2, bq, LANES), F32)],
        compiler_params=_cparams(("parallel", "arbitrary")),
    )(hb, hb, hb, tot, dy, cnt)


EP_TM = 256


def _ep_in_specs(tm, rev):
    idx = (lambda i: rev - i) if rev is not None else (lambda i: i)
    bo = (_INT_OFF["b_b"] - N_HB) // 256
    halo = lambda i: jnp.maximum(idx(i) * (tm // 8) - 1, 0)
    return [pl.BlockSpec((tm, 256), lambda i: (idx(i), 0)),
            pl.BlockSpec((tm, 256), lambda i: (idx(i), 0)),
            pl.BlockSpec((tm, 256), lambda i: (idx(i), 0)),
            pl.BlockSpec((tm, D_MODEL), lambda i: (idx(i), 0)),
            pl.BlockSpec((tm, 256), lambda i: (idx(i), bo)),
            pl.BlockSpec((tm, 256), lambda i: (idx(i), bo + 1)),
            pl.BlockSpec((tm, 256), lambda i: (idx(i), bo + 2)),
            pl.BlockSpec((8, 256), lambda i: (halo(i), bo + 1)),
            pl.BlockSpec((8, 256), lambda i: (halo(i), bo + 2)),
            pl.BlockSpec((3, 256), lambda i: (0, 0)),
            pl.BlockSpec((1, 256), lambda i: (0, 0)),
            pl.BlockSpec((1, D_MODEL), lambda i: (0, 0)),
            pl.BlockSpec((D_MODEL, D_MODEL), lambda i: (0, 0)),
            pl.BlockSpec((1, D_MODEL), lambda i: (0, 0))]


def _ep_mix(first, ya_ref, yc_ref, yd_ref, gate_ref, bb_ref, bc_ref, bx_ref, hc_ref, hx_ref, cw_ref, cb_ref, gg_ref):
    tm = ya_ref.shape[0]
    u = bc_ref[...] * bx_ref[...]
    halo = jnp.where(first, 0.0, hc_ref[...] * hx_ref[...])
    row = lax.broadcasted_iota(jnp.int32, (tm, 1), 0)
    u1 = jnp.where(row == 0, halo[7:8, :], pltpu.roll(u, 1, 0))
    u2 = jnp.where(row == 0, halo[6:7, :], jnp.where(row == 1, halo[7:8, :], pltpu.roll(u, 2, 0)))
    cw = cw_ref[...]
    conv = cw[0:1, :] * u2 + cw[1:2, :] * u1 + cw[2:3, :] * u + cb_ref[...]
    bb = bb_ref[...]
    ys = [ya_ref[...], bb * conv, yc_ref[...], yd_ref[...]]
    rs = [_rms(y) for y in ys]
    gg = gg_ref[...]
    yhat = jnp.concatenate([y * r for y, r in zip(ys, rs)], axis=1)
    gate = gate_ref[...]
    sig = 1.0 / (1.0 + jnp.exp(-gate))
    return u, u1, u2, conv, bb, rs, yhat, yhat * gg, gate, sig


def _epilogue_fwd(x, ya, yc, yd, hf, conv_w, conv_b, g_grp, w_out, g_post):
    T = x.shape[0]
    tm = EP_TM

    def body(x_ref, ya_ref, yc_ref, yd_ref, gate_ref, bb_ref, bc_ref, bx_ref, hc_ref, hx_ref, cw_ref, cb_ref,
             gg_ref, wo_ref, gp_ref, o_ref):
        (_, _, _, _, _, _, _, yn, gate, sig) = _ep_mix(
            pl.program_id(0) == 0, ya_ref, yc_ref, yd_ref, gate_ref, bb_ref, bc_ref, bx_ref, hc_ref, hx_ref,
            cw_ref, cb_ref, gg_ref)
        z = _dot((yn * (gate * sig)).astype(BF16), wo_ref[...])
        o_ref[...] = x_ref[...] + z * _rms(z) * gp_ref[...]

    return pl.pallas_call(
        body, name="epilogue_fwd", grid=(T // tm,),
        in_specs=[pl.BlockSpec((tm, D_MODEL), lambda i: (i, 0))] + _ep_in_specs(tm, None),
        out_specs=pl.BlockSpec((tm, D_MODEL), lambda i: (i, 0)),
        out_shape=jax.ShapeDtypeStruct((T, D_MODEL), F32),
        compiler_params=_cparams(("parallel",)),
    )(x, ya, yc, yd, hf, hf, hf, hf, hf, hf, conv_w, conv_b, g_grp, w_out, g_post)


def _epilogue_bwd(dxn, ya, yc, yd, hf, conv_w, conv_b, g_grp, w_out, g_post):
    T = dxn.shape[0]
    tm = EP_TM
    nt = T // tm
    ridx = lambda i: (nt - 1 - i, 0)

    def body(dx_ref, ya_ref, yc_ref, yd_ref, gate_ref, bb_ref, bc_ref, bx_ref, hc_ref, hx_ref, cw_ref, cb_ref,
             gg_ref, wo_ref, gp_ref,
             dya_ref, dyc_ref, dyd_ref, dhf_ref, dwo_ref, dgp_ref, dgg_ref, dcw_ref, dcb_ref, carry_ref):
        i = pl.program_id(0)

        @pl.when(i == 0)
        def _():
            for r in (dwo_ref, dgp_ref, dgg_ref, dcw_ref, dcb_ref, carry_ref):
                r[...] = jnp.zeros_like(r)

        (u, u1, u2, conv, bb, rs, yhat, yn, gate, sig) = _ep_mix(
            i == nt - 1, ya_ref, yc_ref, yd_ref, gate_ref, bb_ref, bc_ref, bx_ref, hc_ref, hx_ref,
            cw_ref, cb_ref, gg_ref)
        silu = gate * sig
        ymix = (yn * silu).astype(BF16)
        z = _dot(ymix, wo_ref[...])
        rz = _rms(z)
        dz, dgrow = _rms_bwd(dx_ref[...], z * rz, rz, gp_ref[...])
        dgp_ref[...] += _colsum(dgrow)
        dzb = dz.astype(BF16)
        dwo_ref[...] += _dot_tn(ymix, dzb)
        dymix = _dot_nt(dzb, wo_ref[...])
        dhf_ref[:, 0:D_MODEL] = (dymix * yn * (sig * (1.0 + gate * (1.0 - sig)))).astype(BF16)
        dyn = dymix * silu
        dgg_ref[...] += _colsum(dyn * yhat)
        gg = gg_ref[...]
        dys = []
        for gi in range(4):
            sl = slice(gi * GROUP, (gi + 1) * GROUP)
            dyh = dyn[:, sl] * gg[:, sl]
            yh = yhat[:, sl]
            dys.append(rs[gi] * (dyh - yh * jnp.mean(dyh * yh, axis=-1, keepdims=True)))
        dya_ref[...] = dys[0]
        dyc_ref[...] = dys[2]
        dyd_ref[...] = dys[3]
        dyb = dys[1]
        dhf_ref[:, D_MODEL:D_MODEL + 256] = (dyb * conv).astype(BF16)
        dconv = dyb * bb
        dcb_ref[...] += _colsum(dconv)
        dcw_ref[0:1, :] += _colsum(dconv * u2)
        dcw_ref[1:2, :] += _colsum(dconv * u1)
        dcw_ref[2:3, :] += _colsum(dconv * u)
        carry = carry_ref[...]
        row = lax.broadcasted_iota(jnp.int32, (tm, 1), 0)
        d1 = jnp.where(row == tm - 1, carry[0:1, :], pltpu.roll(dconv, tm - 1, 0))
        d2 = jnp.where(row == tm - 2, carry[0:1, :],
                       jnp.where(row == tm - 1, carry[1:2, :], pltpu.roll(dconv, tm - 2, 0)))
        cw = cw_ref[...]
        du = cw[2:3, :] * dconv + cw[1:2, :] * d1 + cw[0:1, :] * d2
        dhf_ref[:, D_MODEL + 256:D_MODEL + 512] = (du * bx_ref[...]).astype(BF16)
        dhf_ref[:, D_MODEL + 512:D_MODEL + 768] = (du * bc_ref[...]).astype(BF16)
        carry_ref[...] = dconv[0:8, :]

    in_specs = [pl.BlockSpec((tm, D_MODEL), ridx)] + _ep_in_specs(tm, nt - 1)
    return pl.pallas_call(
        body, name="epilogue_bwd", grid=(nt,), in_specs=in_specs,
        out_specs=[pl.BlockSpec((tm, 256), ridx), pl.BlockSpec((tm, 256), ridx), pl.BlockSpec((tm, 256), ridx),
                   pl.BlockSpec((tm, D_MODEL + 768), ridx),
                   pl.BlockSpec((D_MODEL, D_MODEL), lambda i: (0, 0)),
                   pl.BlockSpec((1, D_MODEL), lambda i: (0, 0)),
                   pl.BlockSpec((1, D_MODEL), lambda i: (0, 0)),
                   pl.BlockSpec((8, 256), lambda i: (0, 0)),
                   pl.BlockSpec((1, 256), lambda i: (0, 0))],
        out_shape=[jax.ShapeDtypeStruct((T, 256), F32)] * 3
                  + [jax.ShapeDtypeStruct((T, D_MODEL + 768), BF16),
                     jax.ShapeDtypeStruct((D_MODEL, D_MODEL), F32),
                     jax.ShapeDtypeStruct((1, D_MODEL), F32),
                     jax.ShapeDtypeStruct((1, D_MODEL), F32),
                     jax.ShapeDtypeStruct((8, 256), F32),
                     jax.ShapeDtypeStruct((1, 256), F32)],
        scratch_shapes=[pltpu.VMEM((8, 256), F32)],
        compiler_params=_cparams(("arbitrary",)),
    )(dxn, ya, yc, yd, hf, hf, hf, hf, hf, hf, conv_w, conv_b, g_grp, w_out, g_post)


def _loss_head(y, tgt):
    T = y.shape[0]
    tm = 512

    def body(y_ref, t_ref, dy_ref, l_ref):
        @pl.when(pl.program_id(0) == 0)
        def _():
            l_ref[...] = jnp.zeros_like(l_ref)

        d = y_ref[...] - t_ref[...]
        dy_ref[...] = d * (1.0 / D_MODEL)
        part = jnp.sum(jnp.sum(d * d, axis=1, keepdims=True), axis=0, keepdims=True)
        l_ref[...] += part * (0.5 / D_MODEL)

    return pl.pallas_call(
        body, name="loss_head", grid=(T // tm,),
        in_specs=[pl.BlockSpec((tm, D_MODEL), lambda i: (i, 0))] * 2,
        out_specs=[pl.BlockSpec((tm, D_MODEL), lambda i: (i, 0)), pl.BlockSpec((8, LANES), lambda i: (0, 0))],
        out_shape=[jax.ShapeDtypeStruct((T, D_MODEL), F32), jax.ShapeDtypeStruct((8, LANES), F32)],
        compiler_params=_cparams(("arbitrary",)),
    )(y, tgt)


def _place():
    return lax.axis_index("x"), lax.axis_index("y"), lax.axis_index("c")


def _other_chips(x, y):
    return [(1 - x, y), (x, 1 - y), (1 - x, 1 - y)]


HBM = pl.BlockSpec(memory_space=pl.ANY)


def _gather_weights(shards):
    n = len(shards)

    def body(*refs):
        ins, outs = refs[:n], refs[n:2 * n]
        ici_send, ici_recv, d2d_send, d2d_recv, local_sems = refs[2 * n:]
        x, y, c = _place()
        me = 2 * x + y
        chips = _other_chips(x, y)

        def ici(a, j, layer_from):
            px, py = chips[j]
            return pltpu.make_async_remote_copy(
                src_ref=ins[a].at[c], dst_ref=outs[a].at[layer_from, c], send_sem=ici_send.at[3 * a + j],
                recv_sem=ici_recv.at[3 * a + j], device_id=(px, py, c), device_id_type=MESH)

        def d2d(a, j, layer):
            px, py = chips[j]
            blk = outs[a].at[2 * px + py, layer]
            return pltpu.make_async_remote_copy(
                src_ref=blk, dst_ref=blk, send_sem=d2d_send.at[3 * a + j], recv_sem=d2d_recv.at[3 * a + j],
                device_id=(x, y, 1 - c), device_id_type=MESH)

        local = [pltpu.make_async_copy(ins[a], outs[a].at[me], local_sems.at[a]) for a in range(n)]
        for cp in local:
            cp.start()
        sends = [ici(a, j, me) for j in range(3) for a in range(n)]
        for cp in sends:
            cp.start()
        for j in range(3):
            px, py = chips[j]
            for a in range(n):
                ici(a, j, 2 * px + py).wait_recv()
                fwd = d2d(a, j, c)
                fwd.start()
                sends.append(fwd)
        for j in range(3):
            for a in range(n):
                d2d(a, j, 1 - c).wait_recv()
        for cp in sends:
            cp.wait_send()
        for cp in local:
            cp.wait()

    return pl.pallas_call(
        body, name="gather_weights",
        in_specs=[HBM] * n, out_specs=[HBM] * n,
        out_shape=[jax.ShapeDtypeStruct((4,) + s.shape, s.dtype) for s in shards],
        scratch_shapes=[pltpu.SemaphoreType.DMA((3 * n,))] * 4 + [pltpu.SemaphoreType.DMA((n,))],
    )(*shards)


def _exchange_chips(parts, small):
    n = len(parts)

    def body(*refs):
        ins, sm_ref = refs[:n], refs[n]
        outs, osm_ref = refs[n + 1:2 * n + 1], refs[2 * n + 1]
        send_sems, recv_sems, ssend_sems, srecv_sems, local_sems = refs[2 * n + 2:]
        x, y, c = _place()
        me = 2 * x + y
        dev = 4 * x + 2 * y + c
        local = [pltpu.make_async_copy(ins[a].at[me], outs[a].at[me], local_sems.at[a]) for a in range(n)]
        local.append(pltpu.make_async_copy(sm_ref, osm_ref.at[dev], local_sems.at[n]))
        for cp in local:
            cp.start()
        sends = []
        for j, (px, py) in enumerate(_other_chips(x, y)):
            for a in range(n):
                cp = pltpu.make_async_remote_copy(
                    src_ref=ins[a].at[2 * px + py], dst_ref=outs[a].at[me], send_sem=send_sems.at[3 * a + j],
                    recv_sem=recv_sems.at[3 * a + j], device_id=(px, py, c), device_id_type=MESH)
                cp.start()
                sends.append(cp)
        flips = [(fx, fy, fc) for fx in (0, 1) for fy in (0, 1) for fc in (0, 1)][1:]
        for j, (fx, fy, fc) in enumerate(flips):
            cp = pltpu.make_async_remote_copy(
                src_ref=sm_ref, dst_ref=osm_ref.at[dev], send_sem=ssend_sems.at[j], recv_sem=srecv_sems.at[j],
                device_id=(x ^ fx, y ^ fy, c ^ fc), device_id_type=MESH)
            cp.start()
            sends.append(cp)
        for j, (px, py) in enumerate(_other_chips(x, y)):
            for a in range(n):
                pltpu.make_async_remote_copy(
                    src_ref=ins[a].at[me], dst_ref=outs[a].at[2 * px + py], send_sem=send_sems.at[3 * a + j],
                    recv_sem=recv_sems.at[3 * a + j], device_id=(px, py, c), device_id_type=MESH).wait_recv()
        for j, (fx, fy, fc) in enumerate(flips):
            src = 4 * (x ^ fx) + 2 * (y ^ fy) + (c ^ fc)
            pltpu.make_async_remote_copy(
                src_ref=sm_ref, dst_ref=osm_ref.at[src], send_sem=ssend_sems.at[j], recv_sem=srecv_sems.at[j],
                device_id=(x ^ fx, y ^ fy, c ^ fc), device_id_type=MESH).wait_recv()
        for cp in sends:
            cp.wait_send()
        for cp in local:
            cp.wait()

    return pl.pallas_call(
        body, name="exchange_chips",
        in_specs=[HBM] * (n + 1), out_specs=[HBM] * (n + 1),
        out_shape=[jax.ShapeDtypeStruct(p.shape, p.dtype) for p in parts]
                  + [jax.ShapeDtypeStruct((8,) + small.shape, small.dtype)],
        scratch_shapes=[pltpu.SemaphoreType.DMA((3 * n,)), pltpu.SemaphoreType.DMA((3 * n,)),
                        pltpu.SemaphoreType.DMA((7,)), pltpu.SemaphoreType.DMA((7,)),
                        pltpu.SemaphoreType.DMA((n + 1,))],
    )(*parts, small)


def _swap_cores(parts, name):
    n = len(parts)

    def body(*refs):
        ins, outs, send_sems, recv_sems = refs[:n], refs[n:2 * n], refs[2 * n], refs[2 * n + 1]
        x, y, c = _place()
        copies = [pltpu.make_async_remote_copy(
            src_ref=ins[a], dst_ref=outs[a], send_sem=send_sems.at[a], recv_sem=recv_sems.at[a],
            device_id=(x, y, 1 - c), device_id_type=MESH) for a in range(n)]
        for cp in copies:
            cp.start()
        for cp in copies:
            cp.wait()

    return pl.pallas_call(
        body, name=name, in_specs=[HBM] * n, out_specs=[HBM] * n,
        out_shape=[jax.ShapeDtypeStruct(p.shape, p.dtype) for p in parts],
        scratch_shapes=[pltpu.SemaphoreType.DMA((n,)), pltpu.SemaphoreType.DMA((n,))],
    )(*parts)


def _tile(rows, cols):
    for cand in (256, 128, 64):
        if rows % cand == 0:
            return cand, cols
    if rows > 64 and cols % 256 == 0:
        return rows, 256
    return rows, cols


def _add(a, b, name):
    L, R, C = a.shape
    tr, tc = _tile(R, C)

    def body(a_ref, b_ref, o_ref):
        o_ref[...] = (a_ref[...] + b_ref[...]).astype(BF16)

    spec = pl.BlockSpec((1, tr, tc), lambda l, i, j: (l, i, j))
    return pl.pallas_call(
        body, name=name, grid=(L, R // tr, C // tc), in_specs=[spec, spec], out_specs=spec,
        out_shape=jax.ShapeDtypeStruct((L, R, C), BF16),
        compiler_params=_cparams(("parallel", "parallel", "parallel")),
    )(a, b)


def _sum_leading(buf, name):
    n, R, C = buf.shape
    tr, tc = _tile(R, C)

    def body(b_ref, o_ref):
        acc = b_ref[0].astype(F32)
        for k in range(1, n):
            acc = acc + b_ref[k].astype(F32)
        o_ref[...] = acc

    return pl.pallas_call(
        body, name=name, grid=(R // tr, C // tc),
        in_specs=[pl.BlockSpec((n, tr, tc), lambda i, j: (0, i, j))],
        out_specs=pl.BlockSpec((tr, tc), lambda i, j: (i, j)),
        out_shape=jax.ShapeDtypeStruct((R, C), F32),
        compiler_params=_cparams(("parallel", "parallel")),
    )(buf)


def _adam_update(w, g, m, v):
    c1 = 1.0 / (1.0 - ADAM_B1 ** ADAM_STEP)
    c2 = 1.0 / (1.0 - ADAM_B2 ** ADAM_STEP)
    mn = ADAM_B1 * m + (1.0 - ADAM_B1) * g
    vn = ADAM_B2 * v + (1.0 - ADAM_B2) * (g * g)
    return -ADAM_LR * ((mn * c1) / (jnp.sqrt(vn * c2) + ADAM_EPS) + ADAM_WD * w), mn, vn


def _adamw_layers(w, m, v, g_mine, g_other, name):
    _, R, C = w.shape
    tr, tc = _tile(R, C)

    def body(w_ref, m_ref, v_ref, gm_ref, go_ref, g_ref, d_ref, mo_ref, vo_ref):
        g = jnp.where(pl.program_id(0) == lax.axis_index("c"), gm_ref[...], go_ref[...])
        g_ref[0] = g
        d_ref[0], mo_ref[0], vo_ref[0] = _adam_update(w_ref[0], g, m_ref[0], v_ref[0])

    spec3 = pl.BlockSpec((1, tr, tc), lambda l, i, j: (l, i, j))
    spec2 = pl.BlockSpec((tr, tc), lambda l, i, j: (i, j))
    return pl.pallas_call(
        body, name=name, grid=(2, R // tr, C // tc),
        in_specs=[spec3] * 3 + [spec2] * 2, out_specs=[spec3] * 4,
        out_shape=[jax.ShapeDtypeStruct(w.shape, F32)] * 4,
        compiler_params=_cparams(("parallel", "parallel", "parallel")),
    )(w, m, v, g_mine, g_other)


PACK_C = 1024
_BIG = ("w_in", "w_out", "mla_w_uq", "mla_w_ukv", "conv_w")
_SMALL = ("norm_pre", "group_norm", "norm_post", "conv_b", "mla_q_norm", "mla_kv_norm", "attn_sinks")
_SMALL_W = {"norm_pre": 1024, "group_norm": 1024, "norm_post": 1024, "conv_b": 256, "mla_q_norm": 256,
            "mla_kv_norm": 128, "attn_sinks": 4}


_LOSS_AT = divmod(DEPTH * sum(_SMALL_W.values()), PACK_C)


def _pack_small(d, loss):
    flat = jnp.concatenate([d[n].reshape(-1) for n in _SMALL] + [loss.reshape(1)])
    return jnp.pad(flat, (0, 8 * PACK_C - flat.shape[0])).reshape(8, PACK_C)


def _adamw_small(w, m, v, got):
    ns = len(_SMALL)

    def body(*refs):
        got_ref = refs[3 * ns]
        outs = refs[3 * ns + 1:]
        gsum = got_ref[0]
        for d in range(1, 8):
            gsum = gsum + got_ref[d]
        outs[4 * ns][...] = gsum[_LOSS_AT[0]:_LOSS_AT[0] + 1, _LOSS_AT[1]:_LOSS_AT[1] + 1]
        off = 0
        for i, name in enumerate(_SMALL):
            wd = _SMALL_W[name]
            rows = []
            for l in range(DEPTH):
                r, c0 = divmod(off + l * wd, PACK_C)
                rows.append(gsum[r:r + 1, c0:c0 + wd])
            off += DEPTH * wd
            g = jnp.concatenate(rows, axis=0)
            delta, mn, vn = _adam_update(refs[i][...], g, refs[ns + i][...], refs[2 * ns + i][...])
            outs[i][...] = g
            outs[ns + i][...] = delta
            outs[2 * ns + i][...] = mn
            outs[3 * ns + i][...] = vn

    shapes = [jax.ShapeDtypeStruct(w[n].shape, F32) for n in _SMALL]
    res = pl.pallas_call(body, name="adamw_small", out_shape=shapes * 4 + [jax.ShapeDtypeStruct((1, 1), F32)])(
        *[w[n] for n in _SMALL], *[m[n] for n in _SMALL], *[v[n] for n in _SMALL], got)
    return [dict(zip(_SMALL, res[k * ns:(k + 1) * ns])) for k in range(4)], res[4 * ns]


def _w_in_internal(wt):
    rows = []
    for n in _INT_ORDER:
        o, wd = _REAL_OFF[n]
        rows.append(wt[o:o + wd])
        if _INT_W[n] != wd:
            rows.append(jnp.zeros((_INT_W[n] - wd, wt.shape[1]), wt.dtype))
    return jnp.concatenate(rows, axis=0)


def _w_in_real(dwt):
    return jnp.concatenate([dwt[_INT_OFF[n]:_INT_OFF[n] + wd] for n, wd in _REAL], axis=0)


def _uq_internal(w):
    return jnp.pad(w.reshape(256, 4, 96), ((0, 0), (0, 0), (0, 32))).reshape(256, 512)


def _uq_real(dw):
    return dw.reshape(256, 4, 128)[:, :, :96].reshape(256, 384)


def _ukv_internal(w):
    w4 = w.reshape(128, 4, 128)
    k = jnp.pad(w4[:, :, :64], ((0, 0), (0, 0), (0, 64))).reshape(128, 512)
    return jnp.concatenate([k, w4[:, :, 64:].reshape(128, 256)], axis=1)


def _ukv_real(dw):
    k = dw[:, :512].reshape(128, 4, 128)[:, :, :64]
    v = dw[:, 512:].reshape(128, 4, 64)
    return jnp.concatenate([k, v], axis=2).reshape(128, 512)


def _layer_fwd(x, pos, p):
    xn, hb, hf = _inproj_fwd(x, p["norm_pre"], p["w_in"])
    ya = _swa_fwd(hb, p["attn_sinks"])
    qm, km, vm, vt = _mla_prep_fwd(hf, pos, p["mla_q_norm"], p["mla_kv_norm"], p["mla_w_uq"], p["mla_w_ukv"])
    yc, lse = _mla_fwd(qm, km, vt)
    yd, tot, cnt = _sb_fwd(hb)
    x_next = _epilogue_fwd(x, ya, yc, yd, hf, p["conv_w"], p["conv_b"], p["group_norm"], p["w_out"], p["norm_post"])
    return x_next, dict(x=x, xn=xn, hb=hb, hf=hf, ya=ya, yc=yc, yd=yd, tot=tot, cnt=cnt, qm=qm, km=km, vm=vm, lse=lse)


def _layer_bwd(dx_next, pos, p, s):
    (dya, dyc, dyd, dhf, dw_out, dg_post, dg_grp, dconv_w, dconv_b) = _epilogue_bwd(
        dx_next, s["ya"], s["yc"], s["yd"], s["hf"], p["conv_w"], p["conv_b"], p["group_norm"], p["w_out"],
        p["norm_post"])
    dq_d, dk_d, dv_d = _sb_bwd(s["hb"], s["tot"], s["cnt"], dyd)
    dqm, dkt, dvt = _mla_bwd(s["qm"], s["km"], s["vm"], s["yc"], s["lse"], dyc)
    dc, dw_uq, dw_ukv, dg_q, dg_kv = _mla_prep_bwd(
        s["hf"], pos, p["mla_q_norm"], p["mla_kv_norm"], p["mla_w_uq"], p["mla_w_ukv"], dqm, dkt, dvt)
    dq_a, dk_a, dv_a, dsinks = _swa_bwd(s["hb"], p["attn_sinks"], dya)
    dx, dh, dg_pre = _inproj_bwd_dx(s["x"], p["norm_pre"], p["w_in"], dx_next,
                                    [dq_a, dk_a, dv_a, dq_d, dk_d, dv_d, dhf, dc])
    dwt_in = _grad_over_tokens(s["xn"], dh, "inproj_bwd_dw")
    grads = dict(norm_pre=dg_pre[0], w_in_t=_w_in_real(dwt_in), attn_sinks=dsinks[0, :4], conv_w=dconv_w[:3],
                 conv_b=dconv_b[0], mla_q_norm=dg_q[0], mla_w_uq=_uq_real(dw_uq), mla_kv_norm=dg_kv[0],
                 mla_w_ukv=_ukv_real(dw_ukv), group_norm=dg_grp[0], w_out=dw_out, norm_post=dg_post[0])
    return dx, grads


_WEIGHTS = ["norm_pre", "w_in", "attn_sinks", "conv_w", "conv_b", "mla_q_norm", "mla_w_uq", "mla_kv_norm",
            "mla_w_ukv", "group_norm", "w_out", "norm_post"]


def kernel(x, positions, norm_pre, w_in, attn_sinks, conv_w, conv_b, mla_q_norm, mla_w_uq, mla_kv_norm, mla_w_ukv, group_norm, w_out, norm_post, loss_target, m_norm_pre, m_w_in, m_attn_sinks, m_conv_w, m_conv_b, m_mla_q_norm, m_mla_w_uq, m_mla_kv_norm, m_mla_w_ukv, m_group_norm, m_w_out, m_norm_post, v_norm_pre, v_w_in, v_attn_sinks, v_conv_w, v_conv_b, v_mla_q_norm, v_mla_w_uq, v_mla_kv_norm, v_mla_w_ukv, v_group_norm, v_w_out, v_norm_post):
    w = dict(norm_pre=norm_pre, w_in=w_in, attn_sinks=attn_sinks, conv_w=conv_w, conv_b=conv_b,
             mla_q_norm=mla_q_norm, mla_w_uq=mla_w_uq, mla_kv_norm=mla_kv_norm, mla_w_ukv=mla_w_ukv,
             group_norm=group_norm, w_out=w_out, norm_post=norm_post)
    m = dict(norm_pre=m_norm_pre, w_in=m_w_in, attn_sinks=m_attn_sinks, conv_w=m_conv_w, conv_b=m_conv_b,
             mla_q_norm=m_mla_q_norm, mla_w_uq=m_mla_w_uq, mla_kv_norm=m_mla_kv_norm, mla_w_ukv=m_mla_w_ukv,
             group_norm=m_group_norm, w_out=m_w_out, norm_post=m_norm_post)
    v = dict(norm_pre=v_norm_pre, w_in=v_w_in, attn_sinks=v_attn_sinks, conv_w=v_conv_w, conv_b=v_conv_b,
             mla_q_norm=v_mla_q_norm, mla_w_uq=v_mla_w_uq, mla_kv_norm=v_mla_kv_norm, mla_w_ukv=v_mla_w_ukv,
             group_norm=v_group_norm, w_out=v_w_out, norm_post=v_norm_post)
    T = x.shape[1]
    xs = x[0]
    pos = positions[0].reshape(T, 1)
    tgt = loss_target[0]
    core = lax.axis_index("c")

    gathered = _gather_weights([jnp.swapaxes(w["w_in"], 1, 2).astype(BF16)]
                               + [w[n].astype(BF16) for n in _BIG[1:4]] + [w["conv_w"]])
    full = {}
    for n, got in zip(_BIG, gathered):
        if n in ("w_in", "w_out"):
            full[n] = jnp.moveaxis(got, 0, 1).reshape(DEPTH, 4 * got.shape[2], got.shape[3])
        else:
            full[n] = jnp.transpose(got, (1, 2, 0, 3)).reshape(DEPTH, got.shape[2], 4 * got.shape[3])

    layers = []
    for l in range(DEPTH):
        layers.append(dict(
            norm_pre=norm_pre[l:l + 1], w_in=_w_in_internal(full["w_in"][l]), attn_sinks=attn_sinks[l],
            conv_w=full["conv_w"][l], conv_b=conv_b[l:l + 1], mla_q_norm=mla_q_norm[l:l + 1],
            mla_w_uq=_uq_internal(full["mla_w_uq"][l]), mla_kv_norm=mla_kv_norm[l:l + 1],
            mla_w_ukv=_ukv_internal(full["mla_w_ukv"][l]), group_norm=group_norm[l:l + 1],
            w_out=full["w_out"][l], norm_post=norm_post[l:l + 1]))

    saved = []
    h = xs
    for l in range(DEPTH):
        h, s = _layer_fwd(h, pos, layers[l])
        saved.append(s)
    dy, loss_part = _loss_head(h, tgt)

    grads = [None] * DEPTH
    for l in reversed(range(DEPTH)):
        dy, grads[l] = _layer_bwd(dy, pos, layers[l], saved[l])

    turned = ("w_in", "mla_w_uq")
    turn = lambda n, a: jnp.swapaxes(a, -1, -2) if n in turned else a

    def chunks(n, a):
        if n in ("w_out", "w_in"):
            return a.reshape(4, a.shape[0] // 4, a.shape[1])
        if n in turned:
            return a.T.reshape(4, a.shape[1] // 4, a.shape[0])
        return jnp.transpose(a.reshape(a.shape[0], 4, a.shape[1] // 4), (1, 0, 2))

    grad = lambda l, n: grads[l]["w_in_t" if n == "w_in" else n]
    mine = [chunks(n, jnp.where(core == 0, grad(0, n), grad(1, n))) for n in _BIG]
    theirs = [chunks(n, jnp.where(core == 0, grad(1, n), grad(0, n))) for n in _BIG]
    from_sibling = _swap_cores(theirs, "swap_layer_chunks")
    summed = [_add(a, b, "add_cores_" + n) for n, a, b in zip(_BIG, mine, from_sibling)]
    small = _pack_small({n: jnp.stack([grads[l][n] for l in range(DEPTH)]) for n in _SMALL}, loss_part[0, 0])
    *got, got_small = _exchange_chips(summed, small)
    done = [_sum_leading(b, "sum_chips_" + n) for n, b in zip(_BIG, got)]
    done_other = _swap_cores(done, "swap_layer_shards")

    outs, loss = _adamw_small(w, m, v, got_small)
    for n, gm, go in zip(_BIG, done, done_other):
        for d, a in zip(outs, _adamw_layers(turn(n, w[n]), turn(n, m[n]), turn(n, v[n]), gm, go, "adamw_" + n)):
            d[n] = turn(n, a)
    return (loss[0, 0], dy[None], *[outs[0][n] for n in _WEIGHTS], *[outs[1][n] for n in _WEIGHTS],
            *[outs[2][n] for n in _WEIGHTS], *[outs[3][n] for n in _WEIGHTS])
```

```python
import math

import jax
import jax.numpy as jnp
from jax import lax
from jax.experimental import pallas as pl
from jax.experimental.pallas import tpu as pltpu

F32 = jnp.float32
BF16 = jnp.bfloat16
MESH = pl.DeviceIdType.MESH

D_MODEL = 1024
DEPTH = 2
EPS = 1e-6
BLOCK = 128
HEAD = 64
LANES = 128
GROUP = 256
LOG2E = 1.4426950408889634
LN2 = 0.6931471805599453
MLA_QSCALE = 96 ** -0.5 * LOG2E
ROPE_HALF = 16
ROPE_THETA = 10000.0
SWA_SUB = 2
ATT_BLK = 256
MLA_BQ = 512
NEG = -1e30
SB_DEAD = -104.0

ADAM_LR, ADAM_B1, ADAM_B2, ADAM_EPS, ADAM_WD, ADAM_STEP = 0.001, 0.9, 0.999, 1e-08, 0.01, 10

_REAL = [("a_q", 256), ("a_k", 128), ("a_v", 128), ("b_b", 256), ("b_c", 256), ("b_x", 256),
         ("c_q", 256), ("c_kv", 128), ("c_kr", 32), ("d_q", 256), ("d_k", 256), ("d_v", 256),
         ("gate", 1024)]
_REAL_OFF = {}
_o = 0
for _n, _w in _REAL:
    _REAL_OFF[_n] = (_o, _w)
    _o += _w
D_IN = _o
_INT_ORDER = ["a_q", "a_k", "a_v", "d_q", "d_k", "d_v", "gate", "b_b", "b_c", "b_x", "c_q", "c_kv", "c_kr"]
_INT_W = dict(_REAL)
_INT_W["c_kr"] = 128
_INT_OFF = {}
_o = 0
for _n in _INT_ORDER:
    _INT_OFF[_n] = _o
    _o += _INT_W[_n]
N_INT = _o
N_HB = _INT_OFF["gate"]
N_HF = N_INT - N_HB

VMEM_LIMIT = 56 * 1024 * 1024


def _cparams(sem):
    return pltpu.CompilerParams(dimension_semantics=sem, vmem_limit_bytes=VMEM_LIMIT)


def _dot(a, b):
    return jnp.dot(a, b, preferred_element_type=F32)


def _dot_nt(a, b):
    return lax.dot_general(a, b, (((1,), (1,)), ((), ())), preferred_element_type=F32)


def _dot_tn(a, b):
    return lax.dot_general(a, b, (((0,), (0,)), ((), ())), preferred_element_type=F32)


def _split(x):
    hi = x.astype(BF16)
    lo = (x - hi.astype(F32)).astype(BF16)
    return hi, lo


def _rms(x):
    return lax.rsqrt(jnp.mean(x * x, axis=-1, keepdims=True) + EPS)


def _rms_bwd(dy, xhat, r, g):
    dxhat = dy * g
    return r * (dxhat - xhat * jnp.mean(dxhat * xhat, axis=-1, keepdims=True)), dy * xhat


def _colsum(x):
    return jnp.sum(x, axis=0, keepdims=True)


def _inproj_fwd(x, g, wt):
    T = x.shape[0]
    tm = 256

    def body(x_ref, g_ref, w_ref, xn_ref, hb_ref, hf_ref):
        xv = x_ref[...]
        xn = (xv * _rms(xv) * g_ref[...]).astype(BF16)
        xn_ref[...] = xn
        h = _dot_nt(xn, w_ref[...])
        hb_ref[...] = h[:, :N_HB].astype(BF16)
        hf_ref[...] = h[:, N_HB:]

    return pl.pallas_call(
        body, name="inproj_fwd", grid=(T // tm,),
        in_specs=[pl.BlockSpec((tm, D_MODEL), lambda i: (i, 0)),
                  pl.BlockSpec((1, D_MODEL), lambda i: (0, 0)),
                  pl.BlockSpec((N_INT, D_MODEL), lambda i: (0, 0))],
        out_specs=[pl.BlockSpec((tm, D_MODEL), lambda i: (i, 0)),
                   pl.BlockSpec((tm, N_HB), lambda i: (i, 0)),
                   pl.BlockSpec((tm, N_HF), lambda i: (i, 0))],
        out_shape=[jax.ShapeDtypeStruct((T, D_MODEL), BF16),
                   jax.ShapeDtypeStruct((T, N_HB), BF16),
                   jax.ShapeDtypeStruct((T, N_HF), F32)],
        compiler_params=_cparams(("parallel",)),
    )(x, g, wt)


def _inproj_bwd_dx(x, g, wt, dx_next, pieces):
    T = x.shape[0]
    tm = 256
    widths = [p.shape[1] for p in pieces]
    assert sum(widths) == N_INT

    def body(x_ref, g_ref, w_ref, dxn_ref, *rest):
        p_refs = rest[:len(pieces)]
        dx_ref, dh_ref, dg_ref = rest[len(pieces):]
        dh = jnp.concatenate([p[...].astype(BF16) for p in p_refs], axis=1)
        dh_ref[...] = dh
        dxn = _dot(dh, w_ref[...])
        xv = x_ref[...]
        r = _rms(xv)
        dx, dgrow = _rms_bwd(dxn, xv * r, r, g_ref[...])
        dx_ref[...] = dx + dxn_ref[...]

        @pl.when(pl.program_id(0) == 0)
        def _():
            dg_ref[...] = jnp.zeros_like(dg_ref)

        dg_ref[...] += _colsum(dgrow)

    return pl.pallas_call(
        body, name="inproj_bwd_dx", grid=(T // tm,),
        in_specs=[pl.BlockSpec((tm, D_MODEL), lambda i: (i, 0)),
                  pl.BlockSpec((1, D_MODEL), lambda i: (0, 0)),
                  pl.BlockSpec((N_INT, D_MODEL), lambda i: (0, 0)),
                  pl.BlockSpec((tm, D_MODEL), lambda i: (i, 0))]
                 + [pl.BlockSpec((tm, wd), lambda i: (i, 0)) for wd in widths],
        out_specs=[pl.BlockSpec((tm, D_MODEL), lambda i: (i, 0)),
                   pl.BlockSpec((tm, N_INT), lambda i: (i, 0)),
                   pl.BlockSpec((1, D_MODEL), lambda i: (0, 0))],
        out_shape=[jax.ShapeDtypeStruct((T, D_MODEL), F32),
                   jax.ShapeDtypeStruct((T, N_INT), BF16),
                   jax.ShapeDtypeStruct((1, D_MODEL), F32)],
        compiler_params=_cparams(("arbitrary",)),
    )(x, g, wt, dx_next, *pieces)


def _grad_over_tokens(a, b, name):
    T, M = a.shape
    N = b.shape[1]
    tm, tn = min(1024, T), 512

    def body(a_ref, b_ref, o_ref):
        @pl.when(pl.program_id(1) == 0)
        def _():
            o_ref[...] = jnp.zeros_like(o_ref)

        o_ref[...] += _dot_tn(b_ref[...], a_ref[...])

    return pl.pallas_call(
        body, name=name, grid=(N // tn, T // tm),
        in_specs=[pl.BlockSpec((tm, M), lambda j, t: (t, 0)),
                  pl.BlockSpec((tm, tn), lambda j, t: (t, j))],
        out_specs=pl.BlockSpec((tn, M), lambda j, t: (j, 0)),
        out_shape=jax.ShapeDtypeStruct((N, M), F32),
        compiler_params=_cparams(("parallel", "arbitrary")),
    )(a, b)


def _roll_f32(x, shift):
    return pltpu.roll(x.astype(F32), shift, 1)


def _swa_operands(h, q, k_prev, k_cur, v_prev, v_cur):
    p, e = h // 2, h % 2
    lane = lax.broadcasted_iota(jnp.int32, (1, LANES), 1) // HEAD
    q = q[:, p * LANES:(p + 1) * LANES]
    if e != p:
        q = _roll_f32(q, HEAD).astype(BF16)
        v_prev = _roll_f32(v_prev, HEAD).astype(BF16)
        v_cur = _roll_f32(v_cur, HEAD).astype(BF16)
    qs = jnp.where(lane == p, q, 0) * 0.125
    return dict(p=p, e=e, lane=lane, qs=qs, k_prev=k_prev, k_cur=k_cur,
                v_prev=jnp.where(lane == e, v_prev, 0), v_cur=jnp.where(lane == e, v_cur, 0),
                s_prev=_dot_nt(qs, k_prev), s_cur=_dot_nt(qs, k_cur))


def _swa_probs(ops, sink, no_prev):
    row = lax.broadcasted_iota(jnp.int32, (BLOCK, BLOCK), 0)
    col = lax.broadcasted_iota(jnp.int32, (BLOCK, BLOCK), 1)
    ok_prev = col > row if no_prev is None else jnp.logical_and(col > row, jnp.logical_not(no_prev))
    s_prev = jnp.where(ok_prev, ops["s_prev"], NEG)
    s_cur = jnp.where(col <= row, ops["s_cur"], NEG)
    m = jnp.maximum(jnp.maximum(jnp.max(s_prev, axis=1, keepdims=True),
                                jnp.max(s_cur, axis=1, keepdims=True)), sink)
    p_prev = jnp.exp(s_prev - m)
    p_cur = jnp.exp(s_cur - m)
    p_sink = jnp.exp(sink - m)
    inv = 1.0 / (jnp.sum(p_prev, axis=1, keepdims=True) + jnp.sum(p_cur, axis=1, keepdims=True) + p_sink)
    return p_prev * inv, p_cur * inv, p_sink * inv


def _swa_specs(T):
    n = T // (BLOCK * SWA_SUB)
    qo, ko, vo = (_INT_OFF[name] // LANES for name in ("a_q", "a_k", "a_v"))
    halo = lambda i: jnp.maximum(i * SWA_SUB - 1, 0)
    return [pl.BlockSpec((BLOCK * SWA_SUB, 256), lambda i: (i, qo // 2)),
            pl.BlockSpec((BLOCK, LANES), lambda i: (halo(i), ko)),
            pl.BlockSpec((BLOCK * SWA_SUB, LANES), lambda i: (i, ko)),
            pl.BlockSpec((BLOCK, LANES), lambda i: (halo(i), vo)),
            pl.BlockSpec((BLOCK * SWA_SUB, LANES), lambda i: (i, vo)),
            pl.BlockSpec(memory_space=pltpu.SMEM)], n


def _swa_units(q_ref, kh_ref, kc_ref, vh_ref, vc_ref, s_ref):
    blk = lambda a: slice(a * BLOCK, (a + 1) * BLOCK)
    units = [(a, h) for a in range(SWA_SUB) for h in range(4)]
    ops = {}
    for a, h in units:
        k_prev, v_prev = (kh_ref[...], vh_ref[...]) if a == 0 else (kc_ref[blk(a - 1), :], vc_ref[blk(a - 1), :])
        ops[a, h] = _swa_operands(h, q_ref[blk(a), :], k_prev, kc_ref[blk(a), :], v_prev, vc_ref[blk(a), :])
    probs = {(a, h): _swa_probs(ops[a, h], s_ref[h], pl.program_id(0) == 0 if a == 0 else None) for a, h in units}
    return units, ops, probs, blk


def _swa_fwd(hb, sinks):
    T = hb.shape[0]
    specs, n = _swa_specs(T)

    def body(q_ref, kh_ref, kc_ref, vh_ref, vc_ref, s_ref, o_ref):
        units, ops, probs, blk = _swa_units(q_ref, kh_ref, kc_ref, vh_ref, vc_ref, s_ref)
        outs = {u: _dot(probs[u][0].astype(BF16), ops[u]["v_prev"]) + _dot(probs[u][1].astype(BF16), ops[u]["v_cur"])
                for u in units}
        for a in range(SWA_SUB):
            for p in range(2):
                o_ref[blk(a), p * LANES:(p + 1) * LANES] = outs[a, 2 * p] + outs[a, 2 * p + 1]

    return pl.pallas_call(
        body, name="swa_fwd", grid=(n,), in_specs=specs,
        out_specs=pl.BlockSpec((BLOCK * SWA_SUB, 256), lambda i: (i, 0)),
        out_shape=jax.ShapeDtypeStruct((T, 256), F32),
        compiler_params=_cparams(("parallel",)),
    )(hb, hb, hb, hb, hb, sinks)


def _swa_bwd(hb, sinks, dy):
    T = hb.shape[0]
    specs, n = _swa_specs(T)

    def body(q_ref, kh_ref, kc_ref, vh_ref, vc_ref, s_ref, dy_ref, dq_ref, dk_ref, dv_ref, ds_ref):
        i = pl.program_id(0)

        @pl.when(i == 0)
        def _():
            ds_ref[...] = jnp.zeros_like(ds_ref)

        lane_id = lax.broadcasted_iota(jnp.int32, (8, LANES), 1)
        units, ops, probs, blk = _swa_units(q_ref, kh_ref, kc_ref, vh_ref, vc_ref, s_ref)
        dos = {(a, h): jnp.where(ops[a, h]["lane"] == ops[a, h]["e"],
                                 dy_ref[blk(a), ops[a, h]["p"] * LANES:(ops[a, h]["p"] + 1) * LANES], 0.0)
               for a, h in units}
        dobs = {u: dos[u].astype(BF16) for u in units}
        pbs = {u: (probs[u][0].astype(BF16), probs[u][1].astype(BF16)) for u in units}
        outs = {u: _dot(pbs[u][0], ops[u]["v_prev"]) + _dot(pbs[u][1], ops[u]["v_cur"]) for u in units}
        dps = {u: (_dot_nt(dobs[u], ops[u]["v_prev"]), _dot_nt(dobs[u], ops[u]["v_cur"])) for u in units}
        dss, dsinks = {}, jnp.zeros((8, LANES), F32)
        for u in units:
            delta = jnp.sum(dos[u] * outs[u], axis=1, keepdims=True)
            dss[u] = ((probs[u][0] * (dps[u][0] - delta)).astype(BF16),
                      (probs[u][1] * (dps[u][1] - delta)).astype(BF16))
            dsink = -jnp.sum(probs[u][2] * delta, axis=0, keepdims=True)
            dsinks += jnp.where(lane_id == u[1], dsink, 0.0)
        ds_ref[...] += dsinks
        dqs = {u: (_dot(dss[u][0], ops[u]["k_prev"]) + _dot(dss[u][1], ops[u]["k_cur"])) * 0.125 for u in units}
        zero = jnp.zeros((BLOCK, LANES), F32)
        dk_as_prev, dk_as_cur = [zero] * SWA_SUB, [zero] * SWA_SUB
        dv_as_prev, dv_as_cur = [zero] * SWA_SUB, [zero] * SWA_SUB
        for a, h in units:
            p, e = ops[a, h]["p"], ops[a, h]["e"]
            dob_v = dobs[a, h] if e == p else pltpu.roll(dos[a, h], HEAD, 1).astype(BF16)
            dk_as_prev[a] = dk_as_prev[a] + _dot_tn(dss[a, h][0], ops[a, h]["qs"])
            dk_as_cur[a] = dk_as_cur[a] + _dot_tn(dss[a, h][1], ops[a, h]["qs"])
            dv_as_prev[a] = dv_as_prev[a] + _dot_tn(pbs[a, h][0], dob_v)
            dv_as_cur[a] = dv_as_cur[a] + _dot_tn(pbs[a, h][1], dob_v)
        base = i * SWA_SUB
        for a in range(SWA_SUB):
            rows = pl.ds(pl.multiple_of((base + a) * BLOCK, BLOCK), BLOCK)
            more = a + 1 < SWA_SUB
            dk_ref[rows, :] = dk_as_cur[a] + (dk_as_prev[a + 1] if more else 0.0)
            dv_ref[rows, :] = dv_as_cur[a] + (dv_as_prev[a + 1] if more else 0.0)
        halo = pl.ds(pl.multiple_of(jnp.maximum(base - 1, 0) * BLOCK, BLOCK), BLOCK)
        dk_ref[halo, :] += dk_as_prev[0]
        dv_ref[halo, :] += dv_as_prev[0]
        for a in range(SWA_SUB):
            for p in range(2):
                dq_pair = jnp.zeros((BLOCK, LANES), F32)
                for e in range(2):
                    dq = jnp.where(ops[a, 2 * p + e]["lane"] == p, dqs[a, 2 * p + e], 0.0)
                    dq_pair += dq if e == p else pltpu.roll(dq, HEAD, 1)
                dq_ref[blk(a), p * LANES:(p + 1) * LANES] = dq_pair.astype(BF16)

    return pl.pallas_call(
        body, name="swa_bwd", grid=(n,),
        in_specs=specs + [pl.BlockSpec((BLOCK * SWA_SUB, 256), lambda i: (i, 0))],
        out_specs=[pl.BlockSpec((BLOCK * SWA_SUB, 256), lambda i: (i, 0)),
                   pl.BlockSpec((T, LANES), lambda i: (0, 0)),
                   pl.BlockSpec((T, LANES), lambda i: (0, 0)),
                   pl.BlockSpec((8, LANES), lambda i: (0, 0))],
        out_shape=[jax.ShapeDtypeStruct((T, 256), BF16),
                   jax.ShapeDtypeStruct((T, LANES), F32),
                   jax.ShapeDtypeStruct((T, LANES), F32),
                   jax.ShapeDtypeStruct((8, LANES), F32)],
        compiler_params=_cparams(("arbitrary",)),
    )(hb, hb, hb, hb, hb, sinks, dy)


def _rope_tables(pos_ref):
    lane = lax.broadcasted_iota(jnp.int32, (1, LANES), 1)
    active = jnp.logical_and(lane >= HEAD, lane < HEAD + 2 * ROPE_HALF)
    idx = ((lane - HEAD) % ROPE_HALF).astype(F32)
    freq = jnp.exp(idx * (-math.log(ROPE_THETA) / ROPE_HALF))
    ang = pos_ref[...].astype(F32) * freq
    cos, sin = jnp.cos(ang), jnp.sin(ang)
    c = jnp.where(active, cos, 1.0)
    s_up = jnp.where(jnp.logical_and(active, lane >= HEAD + ROPE_HALF), sin, 0.0)
    s_dn = jnp.where(jnp.logical_and(active, lane < HEAD + ROPE_HALF), -sin, 0.0)
    return c, s_up, s_dn


def _rope(x, tabs):
    c, s_up, s_dn = tabs
    return x * c + pltpu.roll(x, ROPE_HALF, 1) * s_up + pltpu.roll(x, LANES - ROPE_HALF, 1) * s_dn


def _rope_t(dy, tabs):
    c, s_up, s_dn = tabs
    return dy * c + pltpu.roll(dy * s_up, LANES - ROPE_HALF, 1) + pltpu.roll(dy * s_dn, ROPE_HALF, 1)


def _mla_lat_specs(tm):
    cq, ckv, ckr = ((_INT_OFF[n] - N_HB) for n in ("c_q", "c_kv", "c_kr"))
    return [pl.BlockSpec((tm, 256), lambda i: (i, cq // 256)),
            pl.BlockSpec((tm, LANES), lambda i: (i, ckv // LANES)),
            pl.BlockSpec((tm, LANES), lambda i: (i, ckr // LANES)),
            pl.BlockSpec((tm, 1), lambda i: (i, 0)),
            pl.BlockSpec((1, 256), lambda i: (0, 0)),
            pl.BlockSpec((1, LANES), lambda i: (0, 0)),
            pl.BlockSpec((256, 512), lambda i: (0, 0)),
            pl.BlockSpec((LANES, 768), lambda i: (0, 0))]


def _mla_prep_fwd(hf, pos, g_q, g_kv, w_uq, w_ukv):
    T = hf.shape[0]
    tm = 512
    sub = tm // ATT_BLK

    def body(cq_ref, ckv_ref, ckr_ref, pos_ref, gq_ref, gkv_ref, wq_ref, wkv_ref, qm_ref, km_ref, vm_ref, vt_ref):
        tabs = _rope_tables(pos_ref)
        cq = cq_ref[...]
        q = _dot((cq * _rms(cq) * gq_ref[...]).astype(BF16), wq_ref[...])
        ckv = ckv_ref[...]
        kv = _dot((ckv * _rms(ckv) * gkv_ref[...]).astype(BF16), wkv_ref[...])
        kr = _rope(pltpu.roll(ckr_ref[...], HEAD, 1), tabs)
        for h in range(4):
            sl = slice(h * LANES, (h + 1) * LANES)
            qm_ref[:, sl] = (_rope(q[:, sl], tabs) * MLA_QSCALE).astype(BF16)
            km_ref[:, sl] = (kv[:, sl] + kr).astype(BF16)
        vm_ref[...] = kv[:, 512:].astype(BF16)
        for p in range(2):
            for s in range(sub):
                tile = kv[s * ATT_BLK:(s + 1) * ATT_BLK, 512 + p * LANES:512 + (p + 1) * LANES]
                vt_ref[p, s] = jnp.transpose(tile).astype(BF16)

    return pl.pallas_call(
        body, name="mla_prep_fwd", grid=(T // tm,), in_specs=_mla_lat_specs(tm),
        out_specs=[pl.BlockSpec((tm, 512), lambda i: (i, 0)),
                   pl.BlockSpec((tm, 512), lambda i: (i, 0)),
                   pl.BlockSpec((tm, 256), lambda i: (i, 0)),
                   pl.BlockSpec((2, sub, LANES, ATT_BLK), lambda i: (0, i, 0, 0))],
        out_shape=[jax.ShapeDtypeStruct((T, 512), BF16),
                   jax.ShapeDtypeStruct((T, 512), BF16),
                   jax.ShapeDtypeStruct((T, 256), BF16),
                   jax.ShapeDtypeStruct((2, T // ATT_BLK, LANES, ATT_BLK), BF16)],
        compiler_params=_cparams(("parallel",)),
    )(hf, hf, hf, pos, g_q, g_kv, w_uq, w_ukv)


def _mla_prep_bwd(hf, pos, g_q, g_kv, w_uq, w_ukv, dqm, dkt, dvt):
    T = hf.shape[0]
    tm = 512
    sub = tm // ATT_BLK

    def body(cq_ref, ckv_ref, ckr_ref, pos_ref, gq_ref, gkv_ref, wq_ref, wkv_ref, dq_ref, dk_ref, dv_ref,
             dc_ref, dwq_ref, dwkv_ref, dgq_ref, dgkv_ref):
        @pl.when(pl.program_id(0) == 0)
        def _():
            dwq_ref[...] = jnp.zeros_like(dwq_ref)
            dwkv_ref[...] = jnp.zeros_like(dwkv_ref)
            dgq_ref[...] = jnp.zeros_like(dgq_ref)
            dgkv_ref[...] = jnp.zeros_like(dgkv_ref)

        tabs = _rope_tables(pos_ref)
        lane = lax.broadcasted_iota(jnp.int32, (1, LANES), 1)
        dq = jnp.concatenate([_rope_t(dq_ref[:, h * LANES:(h + 1) * LANES] * MLA_QSCALE, tabs)
                              for h in range(4)], axis=1).astype(BF16)
        cq = cq_ref[...]
        rq = _rms(cq)
        cqn = (cq * rq * gq_ref[...]).astype(BF16)
        dwq_ref[...] += _dot_tn(cqn, dq)
        dcq, dgrow = _rms_bwd(_dot_nt(dq, wq_ref[...]), cq * rq, rq, gq_ref[...])
        dgq_ref[...] += _colsum(dgrow)
        dc_ref[:, 0:256] = dcq.astype(BF16)

        dk = jnp.concatenate([jnp.concatenate([jnp.transpose(dk_ref[p, s]) for p in range(2)], axis=1)
                              for s in range(sub)], axis=0) * LN2
        dv = jnp.concatenate([jnp.concatenate([jnp.transpose(dv_ref[p, s]) for p in range(2)], axis=1)
                              for s in range(sub)], axis=0)
        dkr = dk[:, 0:LANES] + dk[:, LANES:2 * LANES] + dk[:, 2 * LANES:3 * LANES] + dk[:, 3 * LANES:]
        dkr = pltpu.roll(_rope_t(dkr, tabs), HEAD, 1)
        dc_ref[:, 384:512] = jnp.where(lane < 2 * ROPE_HALF, dkr, 0.0).astype(BF16)
        dkv = jnp.concatenate([dk.astype(BF16), dv.astype(BF16)], axis=1)
        ckv = ckv_ref[...]
        rkv = _rms(ckv)
        ckvn = (ckv * rkv * gkv_ref[...]).astype(BF16)
        dwkv_ref[...] += _dot_tn(ckvn, dkv)
        dckv, dgrow = _rms_bwd(_dot_nt(dkv, wkv_ref[...]), ckv * rkv, rkv, gkv_ref[...])
        dgkv_ref[...] += _colsum(dgrow)
        dc_ref[:, 256:384] = dckv.astype(BF16)

    return pl.pallas_call(
        body, name="mla_prep_bwd", grid=(T // tm,),
        in_specs=_mla_lat_specs(tm) + [pl.BlockSpec((tm, 512), lambda i: (i, 0)),
                                       pl.BlockSpec((2, sub, 256, ATT_BLK), lambda i: (0, i, 0, 0)),
                                       pl.BlockSpec((2, sub, LANES, ATT_BLK), lambda i: (0, i, 0, 0))],
        out_specs=[pl.BlockSpec((tm, 512), lambda i: (i, 0)),
                   pl.BlockSpec((256, 512), lambda i: (0, 0)),
                   pl.BlockSpec((LANES, 768), lambda i: (0, 0)),
                   pl.BlockSpec((1, 256), lambda i: (0, 0)),
                   pl.BlockSpec((1, LANES), lambda i: (0, 0))],
        out_shape=[jax.ShapeDtypeStruct((T, 512), BF16),
                   jax.ShapeDtypeStruct((256, 512), F32),
                   jax.ShapeDtypeStruct((LANES, 768), F32),
                   jax.ShapeDtypeStruct((1, 256), F32),
                   jax.ShapeDtypeStruct((1, LANES), F32)],
        compiler_params=_cparams(("arbitrary",)),
    )(hf, hf, hf, pos, g_q, g_kv, w_uq, w_ukv, dqm, dkt, dvt)


def _causal_masks(bq, bk):
    row = lax.broadcasted_iota(jnp.int32, (bq, bk), 0)
    col = lax.broadcasted_iota(jnp.int32, (bq, bk), 1)
    return row, col


def _mla_fwd(qm, km, vt):
    T = qm.shape[0]
    bq, bk = min(MLA_BQ, T), ATT_BLK
    nq, nsub, nk = T // bq, bq // bk, T // bk

    def body(q_ref, k_ref, vt_ref, o_ref, lse_ref, acc_ref, m_ref, l_ref):
        qi = pl.program_id(1)
        key = lax.broadcasted_iota(jnp.int32, (bk, bq), 0)
        qry = lax.broadcasted_iota(jnp.int32, (bk, bq), 1)
        ones = jnp.ones((8, bk), BF16)
        acc_ref[...] = jnp.zeros_like(acc_ref)
        m_ref[...] = jnp.full_like(m_ref, NEG)
        l_ref[...] = jnp.zeros_like(l_ref)

        def step(kb0, masked):
            kbs = [kb0 + d for d in range(nsub)]
            sts = [[_dot_nt(k_ref[pl.ds(pl.multiple_of(kb * bk, bk), bk), e * LANES:(e + 1) * LANES],
                            q_ref[:, e * LANES:(e + 1) * LANES]) for kb in kbs] for e in range(2)]
            pts, alphas = [], []
            for e in range(2):
                st = [jnp.where(key + d * bk <= qry, sts[e][d], NEG) for d in range(nsub)] if masked else sts[e]
                m_prev = m_ref[e, 0:1, :]
                m_new = m_prev
                for d in range(nsub):
                    m_new = jnp.maximum(m_new, jnp.max(st[d], axis=0, keepdims=True))
                alpha = jnp.exp2(m_prev - m_new)
                pt = [jnp.exp2(st[d] - m_new).astype(BF16) for d in range(nsub)]
                l_new = alpha * l_ref[e]
                for d in range(nsub):
                    l_new = l_new + _dot(ones, pt[d])
                l_ref[e] = l_new
                m_ref[e] = jnp.broadcast_to(m_new, (8, bq))
                pts.append(pt)
                alphas.append(alpha)
            for e in range(2):
                acc = alphas[e] * acc_ref[e]
                for d in range(nsub):
                    acc = acc + _dot(vt_ref[0, kbs[d], e * HEAD:(e + 1) * HEAD, :], pts[e][d])
                acc_ref[e] = acc

        step(qi * nsub, True)

        def loop(t, c):
            step(t * nsub, False)
            return c

        lax.fori_loop(0, qi, loop, 0)
        outs, lses = [], []
        for e in range(2):
            l = l_ref[e, 0:1, :]
            outs.append(acc_ref[e] / l)
            lses.append(jnp.broadcast_to(m_ref[e, 0:1, :] * LN2 + jnp.log(l), (HEAD, bq)))
        o_ref[...] = jnp.transpose(jnp.concatenate(outs, axis=0))
        lse_ref[...] = jnp.transpose(jnp.concatenate(lses, axis=0))

    return pl.pallas_call(
        body, name="mla_fwd", grid=(2, nq),
        in_specs=[pl.BlockSpec((bq, 256), lambda j, i: (i, j)),
                  pl.BlockSpec((T, 256), lambda j, i: (0, j)),
                  pl.BlockSpec((1, nk, LANES, bk), lambda j, i: (j, 0, 0, 0))],
        out_specs=[pl.BlockSpec((bq, LANES), lambda j, i: (i, j)),
                   pl.BlockSpec((bq, LANES), lambda j, i: (i, j))],
        out_shape=[jax.ShapeDtypeStruct((T, 256), F32), jax.ShapeDtypeStruct((T, 256), F32)],
        scratch_shapes=[pltpu.VMEM((2, HEAD, bq), F32), pltpu.VMEM((2, 8, bq), F32), pltpu.VMEM((2, 8, bq), F32)],
        compiler_params=_cparams(("parallel", "arbitrary")),
    )(qm, km, vt)


def _mla_bwd(qm, km, vm, y, lse, dy):
    T = qm.shape[0]
    bq, bk = min(MLA_BQ, T), ATT_BLK
    nq, nsub, nk = T // bq, bq // bk, T // bk

    def body(q_ref, k_ref, v_ref, y_ref, lse_ref, dy_ref, dq_ref, dkt_ref, dvt_ref, dob_ref, st_ref, qt_ref, dot_ref):
        qi = pl.program_id(1)

        @pl.when(qi == 0)
        def _():
            dkt_ref[...] = jnp.zeros_like(dkt_ref)
            dvt_ref[...] = jnp.zeros_like(dvt_ref)

        lane = lax.broadcasted_iota(jnp.int32, (1, LANES), 1) // HEAD
        row, col = _causal_masks(bq, bk)
        dq_ref[...] = jnp.zeros_like(dq_ref)
        lse = lse_ref[...]
        lse_other = pltpu.roll(lse, HEAD, 1)
        qt_ref[...] = jnp.transpose(q_ref[...].astype(F32)).astype(BF16)
        dot_ref[...] = jnp.transpose(dy_ref[...]).astype(BF16)
        for e in range(2):
            do = jnp.where(lane == e, dy_ref[...], 0.0)
            dob_ref[e] = do.astype(BF16)
            st_ref[2 * e] = jnp.where(lane == e, lse, lse_other) * LOG2E
            st_ref[2 * e + 1] = jnp.broadcast_to(jnp.sum(do * y_ref[...], axis=1, keepdims=True), (bq, LANES))

        hss = [slice(e * LANES, (e + 1) * LANES) for e in range(2)]
        tile = lambda a: jnp.concatenate([a] * (bk // LANES), axis=1)

        def step(kb0, masked):
            kbs = [kb0 + d for d in range(nsub)]
            rows = [pl.ds(pl.multiple_of(kb * bk, bk), bk) for kb in kbs]
            pairs = [(d, e) for d in range(nsub) for e in range(2)]
            ss = {(d, e): _dot_nt(q_ref[:, hss[e]], k_ref[rows[d], hss[e]]) for d, e in pairs}
            dps = {(d, e): _dot_nt(dob_ref[e], jnp.where(lane == e, v_ref[rows[d], :], 0)) for d, e in pairs}
            ps, dss = {}, {}
            for d, e in pairs:
                s = jnp.where(col + d * bk <= row, ss[d, e], NEG) if masked else ss[d, e]
                p = jnp.exp2(s - tile(st_ref[2 * e]))
                dss[d, e] = (p * (dps[d, e] - tile(st_ref[2 * e + 1]))).astype(BF16)
                ps[d, e] = p.astype(BF16)
            for d, e in pairs:
                dvt_ref[0, kbs[d], e * HEAD:(e + 1) * HEAD, :] += _dot(dot_ref[e * HEAD:(e + 1) * HEAD, :], ps[d, e])
            for d, e in pairs:
                dkt_ref[0, kbs[d], hss[e], :] += _dot(qt_ref[hss[e], :], dss[d, e])
            for e in range(2):
                dq = dq_ref[:, hss[e]]
                for d in range(nsub):
                    dq = dq + _dot(dss[d, e], k_ref[rows[d], hss[e]])
                dq_ref[:, hss[e]] = dq

        step(qi * nsub, True)

        def loop(t, c):
            step(t * nsub, False)
            return c

        lax.fori_loop(0, qi, loop, 0)
        dq_ref[...] *= LN2

    return pl.pallas_call(
        body, name="mla_bwd", grid=(2, nq),
        in_specs=[pl.BlockSpec((bq, 256), lambda j, i: (i, j)),
                  pl.BlockSpec((T, 256), lambda j, i: (0, j)),
                  pl.BlockSpec((T, LANES), lambda j, i: (0, j)),
                  pl.BlockSpec((bq, LANES), lambda j, i: (i, j)),
                  pl.BlockSpec((bq, LANES), lambda j, i: (i, j)),
                  pl.BlockSpec((bq, LANES), lambda j, i: (i, j))],
        out_specs=[pl.BlockSpec((bq, 256), lambda j, i: (i, j)),
                   pl.BlockSpec((1, nk, 256, bk), lambda j, i: (j, 0, 0, 0)),
                   pl.BlockSpec((1, nk, LANES, bk), lambda j, i: (j, 0, 0, 0))],
        out_shape=[jax.ShapeDtypeStruct((T, 512), F32),
                   jax.ShapeDtypeStruct((2, nk, 256, bk), F32),
                   jax.ShapeDtypeStruct((2, nk, LANES, bk), F32)],
        scratch_shapes=[pltpu.VMEM((2, bq, LANES), BF16), pltpu.VMEM((4, bq, LANES), F32),
                        pltpu.VMEM((256, bq), BF16), pltpu.VMEM((LANES, bq), BF16)],
        compiler_params=_cparams(("parallel", "arbitrary")),
    )(qm, km, vm, y, lse, dy)


def _suffix_ones(n):
    r = lax.broadcasted_iota(jnp.int32, (n, n), 0)
    c = lax.broadcasted_iota(jnp.int32, (n, n), 1)
    return (r >= c).astype(BF16)


def _prefix_ones(n):
    r = lax.broadcasted_iota(jnp.int32, (n, n), 0)
    c = lax.broadcasted_iota(jnp.int32, (n, n), 1)
    return (r <= c).astype(BF16)


def _sb_specs(T, bq):
    qo, ko, vo = (_INT_OFF[n] // LANES for n in ("d_q", "d_k", "d_v"))
    return [pl.BlockSpec((bq, LANES), lambda j, i: (i, qo + j)),
            pl.BlockSpec((T, LANES), lambda j, i: (0, ko + j)),
            pl.BlockSpec((T, LANES), lambda j, i: (0, vo + j))]


def _sb_fwd(hb):
    T = hb.shape[0]
    bq = bk = ATT_BLK
    nq = T // bq

    def body(q_ref, k_ref, v_ref, o_ref, tot_ref, cnt_ref, qm_ref, car_ref):
        qi = pl.program_id(1)
        lane = lax.broadcasted_iota(jnp.int32, (1, LANES), 1) // HEAD
        row, col = _causal_masks(bq, bk)
        strict = col < row
        u = _suffix_ones(bk)
        o_ref[...] = jnp.zeros_like(o_ref)
        car_ref[...] = jnp.zeros_like(car_ref)
        for e in range(2):
            qm_ref[e] = jnp.where(lane == e, q_ref[...], 0) * 0.125

        def step(blocks):
            tile = lambda a: jnp.concatenate([a] * (bk // LANES), axis=1)
            rows = [pl.ds(pl.multiple_of(kb * bk, bk), bk) for kb, _ in blocks]
            pairs = [(b, e) for b in range(len(blocks)) for e in range(2)]
            zs = {(b, e): _dot_nt(qm_ref[e], k_ref[rows[b], :]) for b, e in pairs}
            splits = {}
            for b, e in pairs:
                z = zs[b, e]
                lk = jnp.minimum(-z, 0.0) - jnp.log(1.0 + jnp.exp(-jnp.abs(z)))
                if blocks[b][1] is not None:
                    lk = jnp.where(blocks[b][1], lk, 0.0)
                splits[b, e] = _split(lk)
            sufs = {be: _dot(hi, u) + _dot(lo, u) for be, (hi, lo) in splits.items()}
            car = [car_ref[0], car_ref[1]]
            aas = {}
            for b, e in pairs:
                a = jnp.exp(zs[b, e] + sufs[b, e] + tile(car[e]))
                if blocks[b][1] is not None:
                    a = jnp.where(blocks[b][1], a, 0.0)
                aas[b, e] = a.astype(BF16)
                car[e] = car[e] + jnp.broadcast_to(sufs[b, e][:, 0:1], (bq, LANES))
            acc = o_ref[...]
            for b, e in pairs:
                acc = acc + _dot(aas[b, e], jnp.where(lane == e, v_ref[rows[b], :], 0))
            o_ref[...] = acc
            car_ref[0], car_ref[1] = car

        step([(qi, strict), (jnp.maximum(qi - 1, 0), qi > 0)])

        def live():
            return jnp.max(jnp.maximum(car_ref[0], car_ref[1])) >= SB_DEAD

        def cond(c):
            return jnp.logical_and(c[0] < qi, c[1])

        def loop(c):
            step([(qi - 1 - c[0], None)])
            return c[0] + 1, live()

        done, _ = lax.while_loop(cond, loop, (jnp.minimum(qi, 1), live()))
        tot_ref[...] = jnp.where(lane == 0, car_ref[0], car_ref[1])
        cnt_ref[pl.program_id(0), qi] = done.astype(F32)

    return pl.pallas_call(
        body, name="sb_fwd", grid=(2, nq), in_specs=_sb_specs(T, bq),
        out_specs=[pl.BlockSpec((bq, LANES), lambda j, i: (i, j)), pl.BlockSpec((bq, LANES), lambda j, i: (i, j)),
                   pl.BlockSpec(memory_space=pltpu.SMEM)],
        out_shape=[jax.ShapeDtypeStruct((T, 256), F32), jax.ShapeDtypeStruct((T, 256), F32),
                   jax.ShapeDtypeStruct((2, nq), F32)],
        scratch_shapes=[pltpu.VMEM((2, bq, LANES), BF16), pltpu.VMEM((2, bq, LANES), F32)],
        compiler_params=_cparams(("parallel", "arbitrary")),
    )(hb, hb, hb)


def _sb_bwd(hb, tot, cnt, dy):
    T = hb.shape[0]
    bq = bk = ATT_BLK
    nq = T // bq

    def body(q_ref, k_ref, v_ref, tot_ref, dy_ref, cnt_ref, dq_ref, dk_ref, dv_ref, qm_ref, dob_ref, dqa_ref, rem_ref,
             cg_ref):
        qi = pl.program_id(1)

        @pl.when(qi == 0)
        def _():
            dk_ref[...] = jnp.zeros_like(dk_ref)
            dv_ref[...] = jnp.zeros_like(dv_ref)

        lane = lax.broadcasted_iota(jnp.int32, (1, LANES), 1) // HEAD
        row, col = _causal_masks(bq, bk)
        strict = col < row
        u = _prefix_ones(bk)
        tot = tot_ref[...]
        tot_other = pltpu.roll(tot, HEAD, 1)
        dqa_ref[...] = jnp.zeros_like(dqa_ref)
        cg_ref[...] = jnp.zeros_like(cg_ref)
        for e in range(2):
            qm_ref[e] = jnp.where(lane == e, q_ref[...], 0) * 0.125
            dob_ref[e] = jnp.where(lane == e, dy_ref[...], 0.0).astype(BF16)
            rem_ref[e] = jnp.where(lane == e, tot, tot_other)

        def step(blocks):
            tile = lambda a: jnp.concatenate([a] * (bk // LANES), axis=1)
            nb = len(blocks)
            rows = [pl.ds(pl.multiple_of(kb * bk, bk), bk) for kb, _ in blocks]
            pairs = [(b, e) for b in range(nb) for e in range(2)]
            mask = lambda b, x: x if blocks[b][1] is None else jnp.where(blocks[b][1], x, 0.0)
            zs = {(b, e): _dot_nt(qm_ref[e], k_ref[rows[b], :]) for b, e in pairs}
            das = {(b, e): _dot_nt(dob_ref[e], jnp.where(lane == e, v_ref[rows[b], :], 0)) for b, e in pairs}
            zls, splits = {}, {}
            for b, e in pairs:
                z = zs[b, e]
                lk = mask(b, jnp.minimum(-z, 0.0) - jnp.log(1.0 + jnp.exp(-jnp.abs(z))))
                zls[b, e] = z + lk
                splits[b, e] = _split(lk)
            pres = {be: _dot(hi, u) + _dot(lo, u) for be, (hi, lo) in splits.items()}
            rem = [rem_ref[0], rem_ref[1]]
            aas, gs, gsplits = {}, {}, {}
            for b, e in pairs:
                a = mask(b, jnp.exp(zls[b, e] + (tile(rem[e]) - pres[b, e])))
                gs[b, e] = a * das[b, e]
                aas[b, e] = a.astype(BF16)
                gsplits[b, e] = _split(gs[b, e])
                rem[e] = rem[e] - jnp.broadcast_to(pres[b, e][:, bk - 1:bk], (bq, LANES))
            for b in range(nb):
                dv_ref[rows[b], :] += _dot_tn(aas[b, 0], dob_ref[0]) + _dot_tn(aas[b, 1], dob_ref[1])
            gpres = {be: _dot(hi, u) + _dot(lo, u) for be, (hi, lo) in gsplits.items()}
            cg = [cg_ref[0], cg_ref[1]]
            dzs = {}
            for b, e in pairs:
                dz = mask(b, gs[b, e] - jnp.exp(zls[b, e]) * (tile(cg[e]) + gpres[b, e]))
                dzs[b, e] = dz.astype(BF16)
                cg[e] = cg[e] + jnp.broadcast_to(gpres[b, e][:, bk - 1:bk], (bq, LANES))
            for b in range(nb):
                dk_ref[rows[b], :] += _dot_tn(dzs[b, 0], qm_ref[0]) + _dot_tn(dzs[b, 1], qm_ref[1])
            for e in range(2):
                dq = dqa_ref[e]
                for b in range(nb):
                    dq = dq + _dot(dzs[b, e], k_ref[rows[b], :])
                dqa_ref[e] = dq
            rem_ref[0], rem_ref[1] = rem
            cg_ref[0], cg_ref[1] = cg

        def loop(kb, c):
            step([(kb, None)])
            return c

        start = qi - jnp.clip(cnt_ref[pl.program_id(0), qi].astype(jnp.int32), 0, qi)
        lax.fori_loop(start, qi - 1, loop, 0)
        step([(jnp.maximum(qi - 1, 0), qi > 0), (qi, strict)])
        dq_ref[...] = (jnp.where(lane == 0, dqa_ref[0], dqa_ref[1]) * 0.125).astype(BF16)

    return pl.pallas_call(
        body, name="sb_bwd", grid=(2, nq),
        in_specs=_sb_specs(T, bq) + [pl.BlockSpec((bq, LANES), lambda j, i: (i, j)),
                                     pl.BlockSpec((bq, LANES), lambda j, i: (i, j)),
                                     pl.BlockSpec(memory_space=pltpu.SMEM)],
        out_specs=[pl.BlockSpec((bq, LANES), lambda j, i: (i, j)),
                   pl.BlockSpec((T, LANES), lambda j, i: (0, j)),
                   pl.BlockSpec((T, LANES), lambda j, i: (0, j))],
        out_shape=[jax.ShapeDtypeStruct((T, 256), BF16)] + [jax.ShapeDtypeStruct((T, 256), F32)] * 2,
        scratch_shapes=[pltpu.VMEM((2, bq, LANES), BF16), pltpu.VMEM((2, bq, LANES), BF16),
                        pltpu.VMEM((2, bq, LANES), F32), pltpu.VMEM((2, bq, LANES), F32),
                        pltpu.VMEM((2, bq, LANES), F32)],
        compiler_params=_cparams(("parallel", "arbitrary")),
    )(hb, hb, hb, tot, dy, cnt)


EP_TM = 256


def _ep_in_specs(tm, rev):
    idx = (lambda i: rev - i) if rev is not None else (lambda i: i)
    bo = (_INT_OFF["b_b"] - N_HB) // 256
    halo = lambda i: jnp.maximum(idx(i) * (tm // 8) - 1, 0)
    return [pl.BlockSpec((tm, 256), lambda i: (idx(i), 0)),
            pl.BlockSpec((tm, 256), lambda i: (idx(i), 0)),
            pl.BlockSpec((tm, 256), lambda i: (idx(i), 0)),
            pl.BlockSpec((tm, D_MODEL), lambda i: (idx(i), 0)),
            pl.BlockSpec((tm, 256), lambda i: (idx(i), bo)),
            pl.BlockSpec((tm, 256), lambda i: (idx(i), bo + 1)),
            pl.BlockSpec((tm, 256), lambda i: (idx(i), bo + 2)),
            pl.BlockSpec((8, 256), lambda i: (halo(i), bo + 1)),
            pl.BlockSpec((8, 256), lambda i: (halo(i), bo + 2)),
            pl.BlockSpec((3, 256), lambda i: (0, 0)),
            pl.BlockSpec((1, 256), lambda i: (0, 0)),
            pl.BlockSpec((1, D_MODEL), lambda i: (0, 0)),
            pl.BlockSpec((D_MODEL, D_MODEL), lambda i: (0, 0)),
            pl.BlockSpec((1, D_MODEL), lambda i: (0, 0))]


def _ep_mix(first, ya_ref, yc_ref, yd_ref, gate_ref, bb_ref, bc_ref, bx_ref, hc_ref, hx_ref, cw_ref, cb_ref, gg_ref):
    tm = ya_ref.shape[0]
    u = bc_ref[...] * bx_ref[...]
    halo = jnp.where(first, 0.0, hc_ref[...] * hx_ref[...])
    row = lax.broadcasted_iota(jnp.int32, (tm, 1), 0)
    u1 = jnp.where(row == 0, halo[7:8, :], pltpu.roll(u, 1, 0))
    u2 = jnp.where(row == 0, halo[6:7, :], jnp.where(row == 1, halo[7:8, :], pltpu.roll(u, 2, 0)))
    cw = cw_ref[...]
    conv = cw[0:1, :] * u2 + cw[1:2, :] * u1 + cw[2:3, :] * u + cb_ref[...]
    bb = bb_ref[...]
    ys = [ya_ref[...], bb * conv, yc_ref[...], yd_ref[...]]
    rs = [_rms(y) for y in ys]
    gg = gg_ref[...]
    yhat = jnp.concatenate([y * r for y, r in zip(ys, rs)], axis=1)
    gate = gate_ref[...]
    sig = 1.0 / (1.0 + jnp.exp(-gate))
    return u, u1, u2, conv, bb, rs, yhat, yhat * gg, gate, sig


def _epilogue_fwd(x, ya, yc, yd, hf, conv_w, conv_b, g_grp, w_out, g_post):
    T = x.shape[0]
    tm = EP_TM

    def body(x_ref, ya_ref, yc_ref, yd_ref, gate_ref, bb_ref, bc_ref, bx_ref, hc_ref, hx_ref, cw_ref, cb_ref,
             gg_ref, wo_ref, gp_ref, o_ref):
        (_, _, _, _, _, _, _, yn, gate, sig) = _ep_mix(
            pl.program_id(0) == 0, ya_ref, yc_ref, yd_ref, gate_ref, bb_ref, bc_ref, bx_ref, hc_ref, hx_ref,
            cw_ref, cb_ref, gg_ref)
        z = _dot((yn * (gate * sig)).astype(BF16), wo_ref[...])
        o_ref[...] = x_ref[...] + z * _rms(z) * gp_ref[...]

    return pl.pallas_call(
        body, name="epilogue_fwd", grid=(T // tm,),
        in_specs=[pl.BlockSpec((tm, D_MODEL), lambda i: (i, 0))] + _ep_in_specs(tm, None),
        out_specs=pl.BlockSpec((tm, D_MODEL), lambda i: (i, 0)),
        out_shape=jax.ShapeDtypeStruct((T, D_MODEL), F32),
        compiler_params=_cparams(("parallel",)),
    )(x, ya, yc, yd, hf, hf, hf, hf, hf, hf, conv_w, conv_b, g_grp, w_out, g_post)


def _epilogue_bwd(dxn, ya, yc, yd, hf, conv_w, conv_b, g_grp, w_out, g_post):
    T = dxn.shape[0]
    tm = EP_TM
    nt = T // tm
    ridx = lambda i: (nt - 1 - i, 0)

    def body(dx_ref, ya_ref, yc_ref, yd_ref, gate_ref, bb_ref, bc_ref, bx_ref, hc_ref, hx_ref, cw_ref, cb_ref,
             gg_ref, wo_ref, gp_ref,
             dya_ref, dyc_ref, dyd_ref, dhf_ref, dwo_ref, dgp_ref, dgg_ref, dcw_ref, dcb_ref, carry_ref):
        i = pl.program_id(0)

        @pl.when(i == 0)
        def _():
            for r in (dwo_ref, dgp_ref, dgg_ref, dcw_ref, dcb_ref, carry_ref):
                r[...] = jnp.zeros_like(r)

        (u, u1, u2, conv, bb, rs, yhat, yn, gate, sig) = _ep_mix(
            i == nt - 1, ya_ref, yc_ref, yd_ref, gate_ref, bb_ref, bc_ref, bx_ref, hc_ref, hx_ref,
            cw_ref, cb_ref, gg_ref)
        silu = gate * sig
        ymix = (yn * silu).astype(BF16)
        z = _dot(ymix, wo_ref[...])
        rz = _rms(z)
        dz, dgrow = _rms_bwd(dx_ref[...], z * rz, rz, gp_ref[...])
        dgp_ref[...] += _colsum(dgrow)
        dzb = dz.astype(BF16)
        dwo_ref[...] += _dot_tn(ymix, dzb)
        dymix = _dot_nt(dzb, wo_ref[...])
        dhf_ref[:, 0:D_MODEL] = (dymix * yn * (sig * (1.0 + gate * (1.0 - sig)))).astype(BF16)
        dyn = dymix * silu
        dgg_ref[...] += _colsum(dyn * yhat)
        gg = gg_ref[...]
        dys = []
        for gi in range(4):
            sl = slice(gi * GROUP, (gi + 1) * GROUP)
            dyh = dyn[:, sl] * gg[:, sl]
            yh = yhat[:, sl]
            dys.append(rs[gi] * (dyh - yh * jnp.mean(dyh * yh, axis=-1, keepdims=True)))
        dya_ref[...] = dys[0]
        dyc_ref[...] = dys[2]
        dyd_ref[...] = dys[3]
        dyb = dys[1]
        dhf_ref[:, D_MODEL:D_MODEL + 256] = (dyb * conv).astype(BF16)
        dconv = dyb * bb
        dcb_ref[...] += _colsum(dconv)
        dcw_ref[0:1, :] += _colsum(dconv * u2)
        dcw_ref[1:2, :] += _colsum(dconv * u1)
        dcw_ref[2:3, :] += _colsum(dconv * u)
        carry = carry_ref[...]
        row = lax.broadcasted_iota(jnp.int32, (tm, 1), 0)
        d1 = jnp.where(row == tm - 1, carry[0:1, :], pltpu.roll(dconv, tm - 1, 0))
        d2 = jnp.where(row == tm - 2, carry[0:1, :],
                       jnp.where(row == tm - 1, carry[1:2, :], pltpu.roll(dconv, tm - 2, 0)))
        cw = cw_ref[...]
        du = cw[2:3, :] * dconv + cw[1:2, :] * d1 + cw[0:1, :] * d2
        dhf_ref[:, D_MODEL + 256:D_MODEL + 512] = (du * bx_ref[...]).astype(BF16)
        dhf_ref[:, D_MODEL + 512:D_MODEL + 768] = (du * bc_ref[...]).astype(BF16)
        carry_ref[...] = dconv[0:8, :]

    in_specs = [pl.BlockSpec((tm, D_MODEL), ridx)] + _ep_in_specs(tm, nt - 1)
    return pl.pallas_call(
        body, name="epilogue_bwd", grid=(nt,), in_specs=in_specs,
        out_specs=[pl.BlockSpec((tm, 256), ridx), pl.BlockSpec((tm, 256), ridx), pl.BlockSpec((tm, 256), ridx),
                   pl.BlockSpec((tm, D_MODEL + 768), ridx),
                   pl.BlockSpec((D_MODEL, D_MODEL), lambda i: (0, 0)),
                   pl.BlockSpec((1, D_MODEL), lambda i: (0, 0)),
                   pl.BlockSpec((1, D_MODEL), lambda i: (0, 0)),
                   pl.BlockSpec((8, 256), lambda i: (0, 0)),
                   pl.BlockSpec((1, 256), lambda i: (0, 0))],
        out_shape=[jax.ShapeDtypeStruct((T, 256), F32)] * 3
                  + [jax.ShapeDtypeStruct((T, D_MODEL + 768), BF16),
                     jax.ShapeDtypeStruct((D_MODEL, D_MODEL), F32),
                     jax.ShapeDtypeStruct((1, D_MODEL), F32),
                     jax.ShapeDtypeStruct((1, D_MODEL), F32),
                     jax.ShapeDtypeStruct((8, 256), F32),
                     jax.ShapeDtypeStruct((1, 256), F32)],
        scratch_shapes=[pltpu.VMEM((8, 256), F32)],
        compiler_params=_cparams(("arbitrary",)),
    )(dxn, ya, yc, yd, hf, hf, hf, hf, hf, hf, conv_w, conv_b, g_grp, w_out, g_post)


def _loss_head(y, tgt):
    T = y.shape[0]
    tm = 512

    def body(y_ref, t_ref, dy_ref, l_ref):
        @pl.when(pl.program_id(0) == 0)
        def _():
            l_ref[...] = jnp.zeros_like(l_ref)

        d = y_ref[...] - t_ref[...]
        dy_ref[...] = d * (1.0 / D_MODEL)
        part = jnp.sum(jnp.sum(d * d, axis=1, keepdims=True), axis=0, keepdims=True)
        l_ref[...] += part * (0.5 / D_MODEL)

    return pl.pallas_call(
        body, name="loss_head", grid=(T // tm,),
        in_specs=[pl.BlockSpec((tm, D_MODEL), lambda i: (i, 0))] * 2,
        out_specs=[pl.BlockSpec((tm, D_MODEL), lambda i: (i, 0)), pl.BlockSpec((8, LANES), lambda i: (0, 0))],
        out_shape=[jax.ShapeDtypeStruct((T, D_MODEL), F32), jax.ShapeDtypeStruct((8, LANES), F32)],
        compiler_params=_cparams(("arbitrary",)),
    )(y, tgt)


def _place():
    return lax.axis_index("x"), lax.axis_index("y"), lax.axis_index("c")


def _other_chips(x, y):
    return [(1 - x, y), (x, 1 - y), (1 - x, 1 - y)]


HBM = pl.BlockSpec(memory_space=pl.ANY)


def _gather_weights(shards):
    n = len(shards)

    def body(*refs):
        ins, outs = refs[:n], refs[n:2 * n]
        ici_send, ici_recv, d2d_send, d2d_recv, local_sems = refs[2 * n:]
        x, y, c = _place()
        me = 2 * x + y
        chips = _other_chips(x, y)

        def ici(a, j, layer_from):
            px, py = chips[j]
            return pltpu.make_async_remote_copy(
                src_ref=ins[a].at[c], dst_ref=outs[a].at[layer_from, c], send_sem=ici_send.at[3 * a + j],
                recv_sem=ici_recv.at[3 * a + j], device_id=(px, py, c), device_id_type=MESH)

        def d2d(a, j, layer):
            px, py = chips[j]
            blk = outs[a].at[2 * px + py, layer]
            return pltpu.make_async_remote_copy(
                src_ref=blk, dst_ref=blk, send_sem=d2d_send.at[3 * a + j], recv_sem=d2d_recv.at[3 * a + j],
                device_id=(x, y, 1 - c), device_id_type=MESH)

        local = [pltpu.make_async_copy(ins[a], outs[a].at[me], local_sems.at[a]) for a in range(n)]
        for cp in local:
            cp.start()
        sends = [ici(a, j, me) for j in range(3) for a in range(n)]
        for cp in sends:
            cp.start()
        for j in range(3):
            px, py = chips[j]
            for a in range(n):
                ici(a, j, 2 * px + py).wait_recv()
                fwd = d2d(a, j, c)
                fwd.start()
                sends.append(fwd)
        for j in range(3):
            for a in range(n):
                d2d(a, j, 1 - c).wait_recv()
        for cp in sends:
            cp.wait_send()
        for cp in local:
            cp.wait()

    return pl.pallas_call(
        body, name="gather_weights",
        in_specs=[HBM] * n, out_specs=[HBM] * n,
        out_shape=[jax.ShapeDtypeStruct((4,) + s.shape, s.dtype) for s in shards],
        scratch_shapes=[pltpu.SemaphoreType.DMA((3 * n,))] * 4 + [pltpu.SemaphoreType.DMA((n,))],
    )(*shards)


def _exchange_chips(parts, small):
    n = len(parts)

    def body(*refs):
        ins, sm_ref = refs[:n], refs[n]
        outs, osm_ref = refs[n + 1:2 * n + 1], refs[2 * n + 1]
        send_sems, recv_sems, ssend_sems, srecv_sems, local_sems = refs[2 * n + 2:]
        x, y, c = _place()
        me = 2 * x + y
        dev = 4 * x + 2 * y + c
        local = [pltpu.make_async_copy(ins[a].at[me], outs[a].at[me], local_sems.at[a]) for a in range(n)]
        local.append(pltpu.make_async_copy(sm_ref, osm_ref.at[dev], local_sems.at[n]))
        for cp in local:
            cp.start()
        sends = []
        for j, (px, py) in enumerate(_other_chips(x, y)):
            for a in range(n):
                cp = pltpu.make_async_remote_copy(
                    src_ref=ins[a].at[2 * px + py], dst_ref=outs[a].at[me], send_sem=send_sems.at[3 * a + j],
                    recv_sem=recv_sems.at[3 * a + j], device_id=(px, py, c), device_id_type=MESH)
                cp.start()
                sends.append(cp)
        flips = [(fx, fy, fc) for fx in (0, 1) for fy in (0, 1) for fc in (0, 1)][1:]
        for j, (fx, fy, fc) in enumerate(flips):
            cp = pltpu.make_async_remote_copy(
                src_ref=sm_ref, dst_ref=osm_ref.at[dev], send_sem=ssend_sems.at[j], recv_sem=srecv_sems.at[j],
                device_id=(x ^ fx, y ^ fy, c ^ fc), device_id_type=MESH)
            cp.start()
            sends.append(cp)
        for j, (px, py) in enumerate(_other_chips(x, y)):
            for a in range(n):
                pltpu.make_async_remote_copy(
                    src_ref=ins[a].at[me], dst_ref=outs[a].at[2 * px + py], send_sem=send_sems.at[3 * a + j],
                    recv_sem=recv_sems.at[3 * a + j], device_id=(px, py, c), device_id_type=MESH).wait_recv()
        for j, (fx, fy, fc) in enumerate(flips):
            src = 4 * (x ^ fx) + 2 * (y ^ fy) + (c ^ fc)
            pltpu.make_async_remote_copy(
                src_ref=sm_ref, dst_ref=osm_ref.at[src], send_sem=ssend_sems.at[j], recv_sem=srecv_sems.at[j],
                device_id=(x ^ fx, y ^ fy, c ^ fc), device_id_type=MESH).wait_recv()
        for cp in sends:
            cp.wait_send()
        for cp in local:
            cp.wait()

    return pl.pallas_call(
        body, name="exchange_chips",
        in_specs=[HBM] * (n + 1), out_specs=[HBM] * (n + 1),
        out_shape=[jax.ShapeDtypeStruct(p.shape, p.dtype) for p in parts]
                  + [jax.ShapeDtypeStruct((8,) + small.shape, small.dtype)],
        scratch_shapes=[pltpu.SemaphoreType.DMA((3 * n,)), pltpu.SemaphoreType.DMA((3 * n,)),
                        pltpu.SemaphoreType.DMA((7,)), pltpu.SemaphoreType.DMA((7,)),
                        pltpu.SemaphoreType.DMA((n + 1,))],
    )(*parts, small)


def _swap_cores(parts, name):
    n = len(parts)

    def body(*refs):
        ins, outs, send_sems, recv_sems = refs[:n], refs[n:2 * n], refs[2 * n], refs[2 * n + 1]
        x, y, c = _place()
        copies = [pltpu.make_async_remote_copy(
            src_ref=ins[a], dst_ref=outs[a], send_sem=send_sems.at[a], recv_sem=recv_sems.at[a],
            device_id=(x, y, 1 - c), device_id_type=MESH) for a in range(n)]
        for cp in copies:
            cp.start()
        for cp in copies:
            cp.wait()

    return pl.pallas_call(
        body, name=name, in_specs=[HBM] * n, out_specs=[HBM] * n,
        out_shape=[jax.ShapeDtypeStruct(p.shape, p.dtype) for p in parts],
        scratch_shapes=[pltpu.SemaphoreType.DMA((n,)), pltpu.SemaphoreType.DMA((n,))],
    )(*parts)


def _tile(rows, cols):
    for cand in (256, 128, 64):
        if rows % cand == 0:
            return cand, cols
    if rows > 64 and cols % 256 == 0:
        return rows, 256
    return rows, cols


def _add(a, b, name):
    L, R, C = a.shape
    tr, tc = _tile(R, C)

    def body(a_ref, b_ref, o_ref):
        o_ref[...] = (a_ref[...] + b_ref[...]).astype(BF16)

    spec = pl.BlockSpec((1, tr, tc), lambda l, i, j: (l, i, j))
    return pl.pallas_call(
        body, name=name, grid=(L, R // tr, C // tc), in_specs=[spec, spec], out_specs=spec,
        out_shape=jax.ShapeDtypeStruct((L, R, C), BF16),
        compiler_params=_cparams(("parallel", "parallel", "parallel")),
    )(a, b)


def _sum_leading(buf, name):
    n, R, C = buf.shape
    tr, tc = _tile(R, C)

    def body(b_ref, o_ref):
        acc = b_ref[0].astype(F32)
        for k in range(1, n):
            acc = acc + b_ref[k].astype(F32)
        o_ref[...] = acc

    return pl.pallas_call(
        body, name=name, grid=(R // tr, C // tc),
        in_specs=[pl.BlockSpec((n, tr, tc), lambda i, j: (0, i, j))],
        out_specs=pl.BlockSpec((tr, tc), lambda i, j: (i, j)),
        out_shape=jax.ShapeDtypeStruct((R, C), F32),
        compiler_params=_cparams(("parallel", "parallel")),
    )(buf)


def _adam_update(w, g, m, v):
    c1 = 1.0 / (1.0 - ADAM_B1 ** ADAM_STEP)
    c2 = 1.0 / (1.0 - ADAM_B2 ** ADAM_STEP)
    mn = ADAM_B1 * m + (1.0 - ADAM_B1) * g
    vn = ADAM_B2 * v + (1.0 - ADAM_B2) * (g * g)
    return -ADAM_LR * ((mn * c1) / (jnp.sqrt(vn * c2) + ADAM_EPS) + ADAM_WD * w), mn, vn


def _adamw_layers(w, m, v, g_mine, g_other, name):
    _, R, C = w.shape
    tr, tc = _tile(R, C)

    def body(w_ref, m_ref, v_ref, gm_ref, go_ref, g_ref, d_ref, mo_ref, vo_ref):
        g = jnp.where(pl.program_id(0) == lax.axis_index("c"), gm_ref[...], go_ref[...])
        g_ref[0] = g
        d_ref[0], mo_ref[0], vo_ref[0] = _adam_update(w_ref[0], g, m_ref[0], v_ref[0])

    spec3 = pl.BlockSpec((1, tr, tc), lambda l, i, j: (l, i, j))
    spec2 = pl.BlockSpec((tr, tc), lambda l, i, j: (i, j))
    return pl.pallas_call(
        body, name=name, grid=(2, R // tr, C // tc),
        in_specs=[spec3] * 3 + [spec2] * 2, out_specs=[spec3] * 4,
        out_shape=[jax.ShapeDtypeStruct(w.shape, F32)] * 4,
        compiler_params=_cparams(("parallel", "parallel", "parallel")),
    )(w, m, v, g_mine, g_other)


PACK_C = 1024
_BIG = ("w_in", "w_out", "mla_w_uq", "mla_w_ukv", "conv_w")
_SMALL = ("norm_pre", "group_norm", "norm_post", "conv_b", "mla_q_norm", "mla_kv_norm", "attn_sinks")
_SMALL_W = {"norm_pre": 1024, "group_norm": 1024, "norm_post": 1024, "conv_b": 256, "mla_q_norm": 256,
            "mla_kv_norm": 128, "attn_sinks": 4}


_LOSS_AT = divmod(DEPTH * sum(_SMALL_W.values()), PACK_C)


def _pack_small(d, loss):
    flat = jnp.concatenate([d[n].reshape(-1) for n in _SMALL] + [loss.reshape(1)])
    return jnp.pad(flat, (0, 8 * PACK_C - flat.shape[0])).reshape(8, PACK_C)


def _adamw_small(w, m, v, got):
    ns = len(_SMALL)

    def body(*refs):
        got_ref = refs[3 * ns]
        outs = refs[3 * ns + 1:]
        gsum = got_ref[0]
        for d in range(1, 8):
            gsum = gsum + got_ref[d]
        outs[4 * ns][...] = gsum[_LOSS_AT[0]:_LOSS_AT[0] + 1, _LOSS_AT[1]:_LOSS_AT[1] + 1]
        off = 0
        for i, name in enumerate(_SMALL):
            wd = _SMALL_W[name]
            rows = []
            for l in range(DEPTH):
                r, c0 = divmod(off + l * wd, PACK_C)
                rows.append(gsum[r:r + 1, c0:c0 + wd])
            off += DEPTH * wd
            g = jnp.concatenate(rows, axis=0)
            delta, mn, vn = _adam_update(refs[i][...], g, refs[ns + i][...], refs[2 * ns + i][...])
            outs[i][...] = g
            outs[ns + i][...] = delta
            outs[2 * ns + i][...] = mn
            outs[3 * ns + i][...] = vn

    shapes = [jax.ShapeDtypeStruct(w[n].shape, F32) for n in _SMALL]
    res = pl.pallas_call(body, name="adamw_small", out_shape=shapes * 4 + [jax.ShapeDtypeStruct((1, 1), F32)])(
        *[w[n] for n in _SMALL], *[m[n] for n in _SMALL], *[v[n] for n in _SMALL], got)
    return [dict(zip(_SMALL, res[k * ns:(k + 1) * ns])) for k in range(4)], res[4 * ns]


def _w_in_internal(wt):
    rows = []
    for n in _INT_ORDER:
        o, wd = _REAL_OFF[n]
        rows.append(wt[o:o + wd])
        if _INT_W[n] != wd:
            rows.append(jnp.zeros((_INT_W[n] - wd, wt.shape[1]), wt.dtype))
    return jnp.concatenate(rows, axis=0)


def _w_in_real(dwt):
    return jnp.concatenate([dwt[_INT_OFF[n]:_INT_OFF[n] + wd] for n, wd in _REAL], axis=0)


def _uq_internal(w):
    return jnp.pad(w.reshape(256, 4, 96), ((0, 0), (0, 0), (0, 32))).reshape(256, 512)


def _uq_real(dw):
    return dw.reshape(256, 4, 128)[:, :, :96].reshape(256, 384)


def _ukv_internal(w):
    w4 = w.reshape(128, 4, 128)
    k = jnp.pad(w4[:, :, :64], ((0, 0), (0, 0), (0, 64))).reshape(128, 512)
    return jnp.concatenate([k, w4[:, :, 64:].reshape(128, 256)], axis=1)


def _ukv_real(dw):
    k = dw[:, :512].reshape(128, 4, 128)[:, :, :64]
    v = dw[:, 512:].reshape(128, 4, 64)
    return jnp.concatenate([k, v], axis=2).reshape(128, 512)


def _layer_fwd(x, pos, p):
    xn, hb, hf = _inproj_fwd(x, p["norm_pre"], p["w_in"])
    ya = _swa_fwd(hb, p["attn_sinks"])
    qm, km, vm, vt = _mla_prep_fwd(hf, pos, p["mla_q_norm"], p["mla_kv_norm"], p["mla_w_uq"], p["mla_w_ukv"])
    yc, lse = _mla_fwd(qm, km, vt)
    yd, tot, cnt = _sb_fwd(hb)
    x_next = _epilogue_fwd(x, ya, yc, yd, hf, p["conv_w"], p["conv_b"], p["group_norm"], p["w_out"], p["norm_post"])
    return x_next, dict(x=x, xn=xn, hb=hb, hf=hf, ya=ya, yc=yc, yd=yd, tot=tot, cnt=cnt, qm=qm, km=km, vm=vm, lse=lse)


def _layer_bwd(dx_next, pos, p, s):
    (dya, dyc, dyd, dhf, dw_out, dg_post, dg_grp, dconv_w, dconv_b) = _epilogue_bwd(
        dx_next, s["ya"], s["yc"], s["yd"], s["hf"], p["conv_w"], p["conv_b"], p["group_norm"], p["w_out"],
        p["norm_post"])
    dq_d, dk_d, dv_d = _sb_bwd(s["hb"], s["tot"], s["cnt"], dyd)
    dqm, dkt, dvt = _mla_bwd(s["qm"], s["km"], s["vm"], s["yc"], s["lse"], dyc)
    dc, dw_uq, dw_ukv, dg_q, dg_kv = _mla_prep_bwd(
        s["hf"], pos, p["mla_q_norm"], p["mla_kv_norm"], p["mla_w_uq"], p["mla_w_ukv"], dqm, dkt, dvt)
    dq_a, dk_a, dv_a, dsinks = _swa_bwd(s["hb"], p["attn_sinks"], dya)
    dx, dh, dg_pre = _inproj_bwd_dx(s["x"], p["norm_pre"], p["w_in"], dx_next,
                                    [dq_a, dk_a, dv_a, dq_d, dk_d, dv_d, dhf, dc])
    dwt_in = _grad_over_tokens(s["xn"], dh, "inproj_bwd_dw")
    grads = dict(norm_pre=dg_pre[0], w_in_t=_w_in_real(dwt_in), attn_sinks=dsinks[0, :4], conv_w=dconv_w[:3],
                 conv_b=dconv_b[0], mla_q_norm=dg_q[0], mla_w_uq=_uq_real(dw_uq), mla_kv_norm=dg_kv[0],
                 mla_w_ukv=_ukv_real(dw_ukv), group_norm=dg_grp[0], w_out=dw_out, norm_post=dg_post[0])
    return dx, grads


_WEIGHTS = ["norm_pre", "w_in", "attn_sinks", "conv_w", "conv_b", "mla_q_norm", "mla_w_uq", "mla_kv_norm",
            "mla_w_ukv", "group_norm", "w_out", "norm_post"]


def kernel(x, positions, norm_pre, w_in, attn_sinks, conv_w, conv_b, mla_q_norm, mla_w_uq, mla_kv_norm, mla_w_ukv, group_norm, w_out, norm_post, loss_target, m_norm_pre, m_w_in, m_attn_sinks, m_conv_w, m_conv_b, m_mla_q_norm, m_mla_w_uq, m_mla_kv_norm, m_mla_w_ukv, m_group_norm, m_w_out, m_norm_post, v_norm_pre, v_w_in, v_attn_sinks, v_conv_w, v_conv_b, v_mla_q_norm, v_mla_w_uq, v_mla_kv_norm, v_mla_w_ukv, v_group_norm, v_w_out, v_norm_post):
    w = dict(norm_pre=norm_pre, w_in=w_in, attn_sinks=attn_sinks, conv_w=conv_w, conv_b=conv_b,
             mla_q_norm=mla_q_norm, mla_w_uq=mla_w_uq, mla_kv_norm=mla_kv_norm, mla_w_ukv=mla_w_ukv,
             group_norm=group_norm, w_out=w_out, norm_post=norm_post)
    m = dict(norm_pre=m_norm_pre, w_in=m_w_in, attn_sinks=m_attn_sinks, conv_w=m_conv_w, conv_b=m_conv_b,
             mla_q_norm=m_mla_q_norm, mla_w_uq=m_mla_w_uq, mla_kv_norm=m_mla_kv_norm, mla_w_ukv=m_mla_w_ukv,
             group_norm=m_group_norm, w_out=m_w_out, norm_post=m_norm_post)
    v = dict(norm_pre=v_norm_pre, w_in=v_w_in, attn_sinks=v_attn_sinks, conv_w=v_conv_w, conv_b=v_conv_b,
             mla_q_norm=v_mla_q_norm, mla_w_uq=v_mla_w_uq, mla_kv_norm=v_mla_kv_norm, mla_w_ukv=v_mla_w_ukv,
             group_norm=v_group_norm, w_out=v_w_out, norm_post=v_norm_post)
    T = x.shape[1]
    xs = x[0]
    pos = positions[0].reshape(T, 1)
    tgt = loss_target[0]
    core = lax.axis_index("c")

    gathered = _gather_weights([jnp.swapaxes(w["w_in"], 1, 2).astype(BF16)]
                               + [w[n].astype(BF16) for n in _BIG[1:4]] + [w["conv_w"]])
    full = {}
    for n, got in zip(_BIG, gathered):
        if n in ("w_in", "w_out"):
            full[n] = jnp.moveaxis(got, 0, 1).reshape(DEPTH, 4 * got.shape[2], got.shape[3])
        else:
            full[n] = jnp.transpose(got, (1, 2, 0, 3)).reshape(DEPTH, got.shape[2], 4 * got.shape[3])

    layers = []
    for l in range(DEPTH):
        layers.append(dict(
            norm_pre=norm_pre[l:l + 1], w_in=_w_in_internal(full["w_in"][l]), attn_sinks=attn_sinks[l],
            conv_w=full["conv_w"][l], conv_b=conv_b[l:l + 1], mla_q_norm=mla_q_norm[l:l + 1],
            mla_w_uq=_uq_internal(full["mla_w_uq"][l]), mla_kv_norm=mla_kv_norm[l:l + 1],
            mla_w_ukv=_ukv_internal(full["mla_w_ukv"][l]), group_norm=group_norm[l:l + 1],
            w_out=full["w_out"][l], norm_post=norm_post[l:l + 1]))

    saved = []
    h = xs
    for l in range(DEPTH):
        h, s = _layer_fwd(h, pos, layers[l])
        saved.append(s)
    dy, loss_part = _loss_head(h, tgt)

    grads = [None] * DEPTH
    for l in reversed(range(DEPTH)):
        dy, grads[l] = _layer_bwd(dy, pos, layers[l], saved[l])

    turned = ("w_in", "mla_w_uq")
    turn = lambda n, a: jnp.swapaxes(a, -1, -2) if n in turned else a

    def chunks(n, a):
        if n in ("w_out", "w_in"):
            return a.reshape(4, a.shape[0] // 4, a.shape[1])
        if n in turned:
            return a.T.reshape(4, a.shape[1] // 4, a.shape[0])
        return jnp.transpose(a.reshape(a.shape[0], 4, a.shape[1] // 4), (1, 0, 2))

    grad = lambda l, n: grads[l]["w_in_t" if n == "w_in" else n]
    mine = [chunks(n, jnp.where(core == 0, grad(0, n), grad(1, n))) for n in _BIG]
    theirs = [chunks(n, jnp.where(core == 0, grad(1, n), grad(0, n))) for n in _BIG]
    from_sibling = _swap_cores(theirs, "swap_layer_chunks")
    summed = [_add(a, b, "add_cores_" + n) for n, a, b in zip(_BIG, mine, from_sibling)]
    small = _pack_small({n: jnp.stack([grads[l][n] for l in range(DEPTH)]) for n in _SMALL}, loss_part[0, 0])
    *got, got_small = _exchange_chips(summed, small)
    done = [_sum_leading(b, "sum_chips_" + n) for n, b in zip(_BIG, got)]
    done_other = _swap_cores(done, "swap_layer_shards")

    outs, loss = _adamw_small(w, m, v, got_small)
    for n, gm, go in zip(_BIG, done, done_other):
        for d, a in zip(outs, _adamw_layers(turn(n, w[n]), turn(n, m[n]), turn(n, v[n]), gm, go, "adamw_" + n)):
            d[n] = turn(n, a)
    return (loss[0, 0], dy[None], *[outs[0][n] for n in _WEIGHTS], *[outs[1][n] for n in _WEIGHTS],
            *[outs[2][n] for n in _WEIGHTS], *[outs[3][n] for n in _WEIGHTS])
```

```python
import math

import jax
import jax.numpy as jnp
from jax import lax
from jax.experimental import pallas as pl
from jax.experimental.pallas import tpu as pltpu

F32 = jnp.float32
BF16 = jnp.bfloat16
MESH = pl.DeviceIdType.MESH

D_MODEL = 1024
DEPTH = 2
EPS = 1e-6
BLOCK = 128
HEAD = 64
LANES = 128
GROUP = 256
LOG2E = 1.4426950408889634
LN2 = 0.6931471805599453
MLA_QSCALE = 96 ** -0.5 * LOG2E
ROPE_HALF = 16
ROPE_THETA = 10000.0
SWA_SUB = 2
ATT_BLK = 256
MLA_BQ = 512
NEG = -1e30
SB_DEAD = -104.0

ADAM_LR, ADAM_B1, ADAM_B2, ADAM_EPS, ADAM_WD, ADAM_STEP = 0.001, 0.9, 0.999, 1e-08, 0.01, 10

_REAL = [("a_q", 256), ("a_k", 128), ("a_v", 128), ("b_b", 256), ("b_c", 256), ("b_x", 256),
         ("c_q", 256), ("c_kv", 128), ("c_kr", 32), ("d_q", 256), ("d_k", 256), ("d_v", 256),
         ("gate", 1024)]
_REAL_OFF = {}
_o = 0
for _n, _w in _REAL:
    _REAL_OFF[_n] = (_o, _w)
    _o += _w
D_IN = _o
_INT_ORDER = ["a_q", "a_k", "a_v", "d_q", "d_k", "d_v", "gate", "b_b", "b_c", "b_x", "c_q", "c_kv", "c_kr"]
_INT_W = dict(_REAL)
_INT_W["c_kr"] = 128
_INT_OFF = {}
_o = 0
for _n in _INT_ORDER:
    _INT_OFF[_n] = _o
    _o += _INT_W[_n]
N_INT = _o
N_HB = _INT_OFF["gate"]
N_HF = N_INT - N_HB

VMEM_LIMIT = 56 * 1024 * 1024


def _cparams(sem):
    return pltpu.CompilerParams(dimension_semantics=sem, vmem_limit_bytes=VMEM_LIMIT)


def _dot(a, b):
    return jnp.dot(a, b, preferred_element_type=F32)


def _dot_nt(a, b):
    return lax.dot_general(a, b, (((1,), (1,)), ((), ())), preferred_element_type=F32)


def _dot_tn(a, b):
    return lax.dot_general(a, b, (((0,), (0,)), ((), ())), preferred_element_type=F32)


def _split(x):
    hi = x.astype(BF16)
    lo = (x - hi.astype(F32)).astype(BF16)
    return hi, lo


def _rms(x):
    return lax.rsqrt(jnp.mean(x * x, axis=-1, keepdims=True) + EPS)


def _rms_bwd(dy, xhat, r, g):
    dxhat = dy * g
    return r * (dxhat - xhat * jnp.mean(dxhat * xhat, axis=-1, keepdims=True)), dy * xhat


def _colsum(x):
    return jnp.sum(x, axis=0, keepdims=True)


def _inproj_fwd(x, g, wt):
    T = x.shape[0]
    tm = 512

    def body(x_ref, g_ref, w_ref, xn_ref, hb_ref, hf_ref):
        xv = x_ref[...]
        xn = (xv * _rms(xv) * g_ref[...]).astype(BF16)
        xn_ref[...] = xn
        h = _dot_nt(xn, w_ref[...])
        hb_ref[...] = h[:, :N_HB].astype(BF16)
        hf_ref[...] = h[:, N_HB:]

    return pl.pallas_call(
        body, name="inproj_fwd", grid=(T // tm,),
        in_specs=[pl.BlockSpec((tm, D_MODEL), lambda i: (i, 0)),
                  pl.BlockSpec((1, D_MODEL), lambda i: (0, 0)),
                  pl.BlockSpec((N_INT, D_MODEL), lambda i: (0, 0))],
        out_specs=[pl.BlockSpec((tm, D_MODEL), lambda i: (i, 0)),
                   pl.BlockSpec((tm, N_HB), lambda i: (i, 0)),
                   pl.BlockSpec((tm, N_HF), lambda i: (i, 0))],
        out_shape=[jax.ShapeDtypeStruct((T, D_MODEL), BF16),
                   jax.ShapeDtypeStruct((T, N_HB), BF16),
                   jax.ShapeDtypeStruct((T, N_HF), F32)],
        compiler_params=_cparams(("parallel",)),
    )(x, g, wt)


def _inproj_bwd_dx(x, g, wt, dx_next, pieces):
    T = x.shape[0]
    tm = 512
    widths = [p.shape[1] for p in pieces]
    assert sum(widths) == N_INT

    def body(x_ref, g_ref, w_ref, dxn_ref, *rest):
        p_refs = rest[:len(pieces)]
        dx_ref, dh_ref, dg_ref = rest[len(pieces):]
        dh = jnp.concatenate([p[...].astype(BF16) for p in p_refs], axis=1)
        dh_ref[...] = dh
        dxn = _dot(dh, w_ref[...])
        xv = x_ref[...]
        r = _rms(xv)
        dx, dgrow = _rms_bwd(dxn, xv * r, r, g_ref[...])
        dx_ref[...] = dx + dxn_ref[...]

        @pl.when(pl.program_id(0) == 0)
        def _():
            dg_ref[...] = jnp.zeros_like(dg_ref)

        dg_ref[...] += _colsum(dgrow)

    return pl.pallas_call(
        body, name="inproj_bwd_dx", grid=(T // tm,),
        in_specs=[pl.BlockSpec((tm, D_MODEL), lambda i: (i, 0)),
                  pl.BlockSpec((1, D_MODEL), lambda i: (0, 0)),
                  pl.BlockSpec((N_INT, D_MODEL), lambda i: (0, 0)),
                  pl.BlockSpec((tm, D_MODEL), lambda i: (i, 0))]
                 + [pl.BlockSpec((tm, wd), lambda i: (i, 0)) for wd in widths],
        out_specs=[pl.BlockSpec((tm, D_MODEL), lambda i: (i, 0)),
                   pl.BlockSpec((tm, N_INT), lambda i: (i, 0)),
                   pl.BlockSpec((1, D_MODEL), lambda i: (0, 0))],
        out_shape=[jax.ShapeDtypeStruct((T, D_MODEL), F32),
                   jax.ShapeDtypeStruct((T, N_INT), BF16),
                   jax.ShapeDtypeStruct((1, D_MODEL), F32)],
        compiler_params=_cparams(("arbitrary",)),
    )(x, g, wt, dx_next, *pieces)


def _grad_over_tokens(a, b, name):
    T, M = a.shape
    N = b.shape[1]
    tm, tn = min(1024, T), 512

    def body(a_ref, b_ref, o_ref):
        @pl.when(pl.program_id(1) == 0)
        def _():
            o_ref[...] = jnp.zeros_like(o_ref)

        o_ref[...] += _dot_tn(b_ref[...], a_ref[...])

    return pl.pallas_call(
        body, name=name, grid=(N // tn, T // tm),
        in_specs=[pl.BlockSpec((tm, M), lambda j, t: (t, 0)),
                  pl.BlockSpec((tm, tn), lambda j, t: (t, j))],
        out_specs=pl.BlockSpec((tn, M), lambda j, t: (j, 0)),
        out_shape=jax.ShapeDtypeStruct((N, M), F32),
        compiler_params=_cparams(("parallel", "arbitrary")),
    )(a, b)


def _roll_f32(x, shift):
    return pltpu.roll(x.astype(F32), shift, 1)


def _swa_operands(h, q, k_prev, k_cur, v_prev, v_cur):
    p, e = h // 2, h % 2
    lane = lax.broadcasted_iota(jnp.int32, (1, LANES), 1) // HEAD
    q = q[:, p * LANES:(p + 1) * LANES]
    if e != p:
        q = _roll_f32(q, HEAD).astype(BF16)
        v_prev = _roll_f32(v_prev, HEAD).astype(BF16)
        v_cur = _roll_f32(v_cur, HEAD).astype(BF16)
    qs = jnp.where(lane == p, q, 0) * 0.125
    return dict(p=p, e=e, lane=lane, qs=qs, k_prev=k_prev, k_cur=k_cur,
                v_prev=jnp.where(lane == e, v_prev, 0), v_cur=jnp.where(lane == e, v_cur, 0),
                s_prev=_dot_nt(qs, k_prev), s_cur=_dot_nt(qs, k_cur))


def _swa_probs(ops, sink, no_prev):
    row = lax.broadcasted_iota(jnp.int32, (BLOCK, BLOCK), 0)
    col = lax.broadcasted_iota(jnp.int32, (BLOCK, BLOCK), 1)
    ok_prev = col > row if no_prev is None else jnp.logical_and(col > row, jnp.logical_not(no_prev))
    s_prev = jnp.where(ok_prev, ops["s_prev"], NEG)
    s_cur = jnp.where(col <= row, ops["s_cur"], NEG)
    m = jnp.maximum(jnp.maximum(jnp.max(s_prev, axis=1, keepdims=True),
                                jnp.max(s_cur, axis=1, keepdims=True)), sink)
    p_prev = jnp.exp(s_prev - m)
    p_cur = jnp.exp(s_cur - m)
    p_sink = jnp.exp(sink - m)
    inv = 1.0 / (jnp.sum(p_prev, axis=1, keepdims=True) + jnp.sum(p_cur, axis=1, keepdims=True) + p_sink)
    return p_prev * inv, p_cur * inv, p_sink * inv


def _swa_specs(T):
    n = T // (BLOCK * SWA_SUB)
    qo, ko, vo = (_INT_OFF[name] // LANES for name in ("a_q", "a_k", "a_v"))
    halo = lambda i: jnp.maximum(i * SWA_SUB - 1, 0)
    return [pl.BlockSpec((BLOCK * SWA_SUB, 256), lambda i: (i, qo // 2)),
            pl.BlockSpec((BLOCK, LANES), lambda i: (halo(i), ko)),
            pl.BlockSpec((BLOCK * SWA_SUB, LANES), lambda i: (i, ko)),
            pl.BlockSpec((BLOCK, LANES), lambda i: (halo(i), vo)),
            pl.BlockSpec((BLOCK * SWA_SUB, LANES), lambda i: (i, vo)),
            pl.BlockSpec(memory_space=pltpu.SMEM)], n


def _swa_units(q_ref, kh_ref, kc_ref, vh_ref, vc_ref, s_ref):
    blk = lambda a: slice(a * BLOCK, (a + 1) * BLOCK)
    units = [(a, h) for a in range(SWA_SUB) for h in range(4)]
    ops = {}
    for a, h in units:
        k_prev, v_prev = (kh_ref[...], vh_ref[...]) if a == 0 else (kc_ref[blk(a - 1), :], vc_ref[blk(a - 1), :])
        ops[a, h] = _swa_operands(h, q_ref[blk(a), :], k_prev, kc_ref[blk(a), :], v_prev, vc_ref[blk(a), :])
    probs = {(a, h): _swa_probs(ops[a, h], s_ref[h], pl.program_id(0) == 0 if a == 0 else None) for a, h in units}
    return units, ops, probs, blk


def _swa_fwd(hb, sinks):
    T = hb.shape[0]
    specs, n = _swa_specs(T)

    def body(q_ref, kh_ref, kc_ref, vh_ref, vc_ref, s_ref, o_ref):
        units, ops, probs, blk = _swa_units(q_ref, kh_ref, kc_ref, vh_ref, vc_ref, s_ref)
        outs = {u: _dot(probs[u][0].astype(BF16), ops[u]["v_prev"]) + _dot(probs[u][1].astype(BF16), ops[u]["v_cur"])
                for u in units}
        for a in range(SWA_SUB):
            for p in range(2):
                o_ref[blk(a), p * LANES:(p + 1) * LANES] = outs[a, 2 * p] + outs[a, 2 * p + 1]

    return pl.pallas_call(
        body, name="swa_fwd", grid=(n,), in_specs=specs,
        out_specs=pl.BlockSpec((BLOCK * SWA_SUB, 256), lambda i: (i, 0)),
        out_shape=jax.ShapeDtypeStruct((T, 256), F32),
        compiler_params=_cparams(("parallel",)),
    )(hb, hb, hb, hb, hb, sinks)


def _swa_bwd(hb, sinks, dy):
    T = hb.shape[0]
    specs, n = _swa_specs(T)

    def body(q_ref, kh_ref, kc_ref, vh_ref, vc_ref, s_ref, dy_ref, dq_ref, dk_ref, dv_ref, ds_ref):
        i = pl.program_id(0)

        @pl.when(i == 0)
        def _():
            ds_ref[...] = jnp.zeros_like(ds_ref)

        lane_id = lax.broadcasted_iota(jnp.int32, (8, LANES), 1)
        units, ops, probs, blk = _swa_units(q_ref, kh_ref, kc_ref, vh_ref, vc_ref, s_ref)
        dos = {(a, h): jnp.where(ops[a, h]["lane"] == ops[a, h]["e"],
                                 dy_ref[blk(a), ops[a, h]["p"] * LANES:(ops[a, h]["p"] + 1) * LANES], 0.0)
               for a, h in units}
        dobs = {u: dos[u].astype(BF16) for u in units}
        pbs = {u: (probs[u][0].astype(BF16), probs[u][1].astype(BF16)) for u in units}
        outs = {u: _dot(pbs[u][0], ops[u]["v_prev"]) + _dot(pbs[u][1], ops[u]["v_cur"]) for u in units}
        dps = {u: (_dot_nt(dobs[u], ops[u]["v_prev"]), _dot_nt(dobs[u], ops[u]["v_cur"])) for u in units}
        dss, dsinks = {}, jnp.zeros((8, LANES), F32)
        for u in units:
            delta = jnp.sum(dos[u] * outs[u], axis=1, keepdims=True)
            dss[u] = ((probs[u][0] * (dps[u][0] - delta)).astype(BF16),
                      (probs[u][1] * (dps[u][1] - delta)).astype(BF16))
            dsink = -jnp.sum(probs[u][2] * delta, axis=0, keepdims=True)
            dsinks += jnp.where(lane_id == u[1], dsink, 0.0)
        ds_ref[...] += dsinks
        dqs = {u: (_dot(dss[u][0], ops[u]["k_prev"]) + _dot(dss[u][1], ops[u]["k_cur"])) * 0.125 for u in units}
        zero = jnp.zeros((BLOCK, LANES), F32)
        dk_as_prev, dk_as_cur = [zero] * SWA_SUB, [zero] * SWA_SUB
        dv_as_prev, dv_as_cur = [zero] * SWA_SUB, [zero] * SWA_SUB
        for a, h in units:
            p, e = ops[a, h]["p"], ops[a, h]["e"]
            dob_v = dobs[a, h] if e == p else pltpu.roll(dos[a, h], HEAD, 1).astype(BF16)
            dk_as_prev[a] = dk_as_prev[a] + _dot_tn(dss[a, h][0], ops[a, h]["qs"])
            dk_as_cur[a] = dk_as_cur[a] + _dot_tn(dss[a, h][1], ops[a, h]["qs"])
            dv_as_prev[a] = dv_as_prev[a] + _dot_tn(pbs[a, h][0], dob_v)
            dv_as_cur[a] = dv_as_cur[a] + _dot_tn(pbs[a, h][1], dob_v)
        base = i * SWA_SUB
        for a in range(SWA_SUB):
            rows = pl.ds(pl.multiple_of((base + a) * BLOCK, BLOCK), BLOCK)
            more = a + 1 < SWA_SUB
            dk_ref[rows, :] = dk_as_cur[a] + (dk_as_prev[a + 1] if more else 0.0)
            dv_ref[rows, :] = dv_as_cur[a] + (dv_as_prev[a + 1] if more else 0.0)
        halo = pl.ds(pl.multiple_of(jnp.maximum(base - 1, 0) * BLOCK, BLOCK), BLOCK)
        dk_ref[halo, :] += dk_as_prev[0]
        dv_ref[halo, :] += dv_as_prev[0]
        for a in range(SWA_SUB):
            for p in range(2):
                dq_pair = jnp.zeros((BLOCK, LANES), F32)
                for e in range(2):
                    dq = jnp.where(ops[a, 2 * p + e]["lane"] == p, dqs[a, 2 * p + e], 0.0)
                    dq_pair += dq if e == p else pltpu.roll(dq, HEAD, 1)
                dq_ref[blk(a), p * LANES:(p + 1) * LANES] = dq_pair.astype(BF16)

    return pl.pallas_call(
        body, name="swa_bwd", grid=(n,),
        in_specs=specs + [pl.BlockSpec((BLOCK * SWA_SUB, 256), lambda i: (i, 0))],
        out_specs=[pl.BlockSpec((BLOCK * SWA_SUB, 256), lambda i: (i, 0)),
                   pl.BlockSpec((T, LANES), lambda i: (0, 0)),
                   pl.BlockSpec((T, LANES), lambda i: (0, 0)),
                   pl.BlockSpec((8, LANES), lambda i: (0, 0))],
        out_shape=[jax.ShapeDtypeStruct((T, 256), BF16),
                   jax.ShapeDtypeStruct((T, LANES), F32),
                   jax.ShapeDtypeStruct((T, LANES), F32),
                   jax.ShapeDtypeStruct((8, LANES), F32)],
        compiler_params=_cparams(("arbitrary",)),
    )(hb, hb, hb, hb, hb, sinks, dy)


def _rope_tables(pos_ref):
    lane = lax.broadcasted_iota(jnp.int32, (1, LANES), 1)
    active = jnp.logical_and(lane >= HEAD, lane < HEAD + 2 * ROPE_HALF)
    idx = ((lane - HEAD) % ROPE_HALF).astype(F32)
    freq = jnp.exp(idx * (-math.log(ROPE_THETA) / ROPE_HALF))
    ang = pos_ref[...].astype(F32) * freq
    cos, sin = jnp.cos(ang), jnp.sin(ang)
    c = jnp.where(active, cos, 1.0)
    s_up = jnp.where(jnp.logical_and(active, lane >= HEAD + ROPE_HALF), sin, 0.0)
    s_dn = jnp.where(jnp.logical_and(active, lane < HEAD + ROPE_HALF), -sin, 0.0)
    return c, s_up, s_dn


def _rope(x, tabs):
    c, s_up, s_dn = tabs
    return x * c + pltpu.roll(x, ROPE_HALF, 1) * s_up + pltpu.roll(x, LANES - ROPE_HALF, 1) * s_dn


def _rope_t(dy, tabs):
    c, s_up, s_dn = tabs
    return dy * c + pltpu.roll(dy * s_up, LANES - ROPE_HALF, 1) + pltpu.roll(dy * s_dn, ROPE_HALF, 1)


def _mla_lat_specs(tm):
    cq, ckv, ckr = ((_INT_OFF[n] - N_HB) for n in ("c_q", "c_kv", "c_kr"))
    return [pl.BlockSpec((tm, 256), lambda i: (i, cq // 256)),
            pl.BlockSpec((tm, LANES), lambda i: (i, ckv // LANES)),
            pl.BlockSpec((tm, LANES), lambda i: (i, ckr // LANES)),
            pl.BlockSpec((tm, 1), lambda i: (i, 0)),
            pl.BlockSpec((1, 256), lambda i: (0, 0)),
            pl.BlockSpec((1, LANES), lambda i: (0, 0)),
            pl.BlockSpec((256, 512), lambda i: (0, 0)),
            pl.BlockSpec((LANES, 768), lambda i: (0, 0))]


def _mla_prep_fwd(hf, pos, g_q, g_kv, w_uq, w_ukv):
    T = hf.shape[0]
    tm = 512
    sub = tm // ATT_BLK

    def body(cq_ref, ckv_ref, ckr_ref, pos_ref, gq_ref, gkv_ref, wq_ref, wkv_ref, qm_ref, km_ref, vm_ref, vt_ref):
        tabs = _rope_tables(pos_ref)
        cq = cq_ref[...]
        q = _dot((cq * _rms(cq) * gq_ref[...]).astype(BF16), wq_ref[...])
        ckv = ckv_ref[...]
        kv = _dot((ckv * _rms(ckv) * gkv_ref[...]).astype(BF16), wkv_ref[...])
        kr = _rope(pltpu.roll(ckr_ref[...], HEAD, 1), tabs)
        for h in range(4):
            sl = slice(h * LANES, (h + 1) * LANES)
            qm_ref[:, sl] = (_rope(q[:, sl], tabs) * MLA_QSCALE).astype(BF16)
            km_ref[:, sl] = (kv[:, sl] + kr).astype(BF16)
        vm_ref[...] = kv[:, 512:].astype(BF16)
        for p in range(2):
            for s in range(sub):
                tile = kv[s * ATT_BLK:(s + 1) * ATT_BLK, 512 + p * LANES:512 + (p + 1) * LANES]
                vt_ref[p, s] = jnp.transpose(tile).astype(BF16)

    return pl.pallas_call(
        body, name="mla_prep_fwd", grid=(T // tm,), in_specs=_mla_lat_specs(tm),
        out_specs=[pl.BlockSpec((tm, 512), lambda i: (i, 0)),
                   pl.BlockSpec((tm, 512), lambda i: (i, 0)),
                   pl.BlockSpec((tm, 256), lambda i: (i, 0)),
                   pl.BlockSpec((2, sub, LANES, ATT_BLK), lambda i: (0, i, 0, 0))],
        out_shape=[jax.ShapeDtypeStruct((T, 512), BF16),
                   jax.ShapeDtypeStruct((T, 512), BF16),
                   jax.ShapeDtypeStruct((T, 256), BF16),
                   jax.ShapeDtypeStruct((2, T // ATT_BLK, LANES, ATT_BLK), BF16)],
        compiler_params=_cparams(("parallel",)),
    )(hf, hf, hf, pos, g_q, g_kv, w_uq, w_ukv)


def _mla_prep_bwd(hf, pos, g_q, g_kv, w_uq, w_ukv, dqm, dkt, dvt):
    T = hf.shape[0]
    tm = 512
    sub = tm // ATT_BLK

    def body(cq_ref, ckv_ref, ckr_ref, pos_ref, gq_ref, gkv_ref, wq_ref, wkv_ref, dq_ref, dk_ref, dv_ref,
             dc_ref, dwq_ref, dwkv_ref, dgq_ref, dgkv_ref):
        @pl.when(pl.program_id(0) == 0)
        def _():
            dwq_ref[...] = jnp.zeros_like(dwq_ref)
            dwkv_ref[...] = jnp.zeros_like(dwkv_ref)
            dgq_ref[...] = jnp.zeros_like(dgq_ref)
            dgkv_ref[...] = jnp.zeros_like(dgkv_ref)

        tabs = _rope_tables(pos_ref)
        lane = lax.broadcasted_iota(jnp.int32, (1, LANES), 1)
        dq = jnp.concatenate([_rope_t(dq_ref[:, h * LANES:(h + 1) * LANES] * MLA_QSCALE, tabs)
                              for h in range(4)], axis=1).astype(BF16)
        cq = cq_ref[...]
        rq = _rms(cq)
        cqn = (cq * rq * gq_ref[...]).astype(BF16)
        dwq_ref[...] += _dot_tn(cqn, dq)
        dcq, dgrow = _rms_bwd(_dot_nt(dq, wq_ref[...]), cq * rq, rq, gq_ref[...])
        dgq_ref[...] += _colsum(dgrow)
        dc_ref[:, 0:256] = dcq.astype(BF16)

        dk = jnp.concatenate([jnp.concatenate([jnp.transpose(dk_ref[p, s]) for p in range(2)], axis=1)
                              for s in range(sub)], axis=0) * LN2
        dv = jnp.concatenate([jnp.concatenate([jnp.transpose(dv_ref[p, s]) for p in range(2)], axis=1)
                              for s in range(sub)], axis=0)
        dkr = dk[:, 0:LANES] + dk[:, LANES:2 * LANES] + dk[:, 2 * LANES:3 * LANES] + dk[:, 3 * LANES:]
        dkr = pltpu.roll(_rope_t(dkr, tabs), HEAD, 1)
        dc_ref[:, 384:512] = jnp.where(lane < 2 * ROPE_HALF, dkr, 0.0).astype(BF16)
        dkv = jnp.concatenate([dk.astype(BF16), dv.astype(BF16)], axis=1)
        ckv = ckv_ref[...]
        rkv = _rms(ckv)
        ckvn = (ckv * rkv * gkv_ref[...]).astype(BF16)
        dwkv_ref[...] += _dot_tn(ckvn, dkv)
        dckv, dgrow = _rms_bwd(_dot_nt(dkv, wkv_ref[...]), ckv * rkv, rkv, gkv_ref[...])
        dgkv_ref[...] += _colsum(dgrow)
        dc_ref[:, 256:384] = dckv.astype(BF16)

    return pl.pallas_call(
        body, name="mla_prep_bwd", grid=(T // tm,),
        in_specs=_mla_lat_specs(tm) + [pl.BlockSpec((tm, 512), lambda i: (i, 0)),
                                       pl.BlockSpec((2, sub, 256, ATT_BLK), lambda i: (0, i, 0, 0)),
                                       pl.BlockSpec((2, sub, LANES, ATT_BLK), lambda i: (0, i, 0, 0))],
        out_specs=[pl.BlockSpec((tm, 512), lambda i: (i, 0)),
                   pl.BlockSpec((256, 512), lambda i: (0, 0)),
                   pl.BlockSpec((LANES, 768), lambda i: (0, 0)),
                   pl.BlockSpec((1, 256), lambda i: (0, 0)),
                   pl.BlockSpec((1, LANES), lambda i: (0, 0))],
        out_shape=[jax.ShapeDtypeStruct((T, 512), BF16),
                   jax.ShapeDtypeStruct((256, 512), F32),
                   jax.ShapeDtypeStruct((LANES, 768), F32),
                   jax.ShapeDtypeStruct((1, 256), F32),
                   jax.ShapeDtypeStruct((1, LANES), F32)],
        compiler_params=_cparams(("arbitrary",)),
    )(hf, hf, hf, pos, g_q, g_kv, w_uq, w_ukv, dqm, dkt, dvt)


def _causal_masks(bq, bk):
    row = lax.broadcasted_iota(jnp.int32, (bq, bk), 0)
    col = lax.broadcasted_iota(jnp.int32, (bq, bk), 1)
    return row, col


def _mla_fwd(qm, km, vt):
    T = qm.shape[0]
    bq, bk = min(MLA_BQ, T), ATT_BLK
    nq, nsub, nk = T // bq, bq // bk, T // bk

    def body(q_ref, k_ref, vt_ref, o_ref, lse_ref, acc_ref, m_ref, l_ref):
        qi = pl.program_id(1)
        key = lax.broadcasted_iota(jnp.int32, (bk, bq), 0)
        qry = lax.broadcasted_iota(jnp.int32, (bk, bq), 1)
        ones = jnp.ones((8, bk), BF16)
        acc_ref[...] = jnp.zeros_like(acc_ref)
        m_ref[...] = jnp.full_like(m_ref, NEG)
        l_ref[...] = jnp.zeros_like(l_ref)

        def step(kb0, masked):
            kbs = [kb0 + d for d in range(nsub)]
            sts = [[_dot_nt(k_ref[pl.ds(pl.multiple_of(kb * bk, bk), bk), e * LANES:(e + 1) * LANES],
                            q_ref[:, e * LANES:(e + 1) * LANES]) for kb in kbs] for e in range(2)]
            pts, alphas = [], []
            for e in range(2):
                st = [jnp.where(key + d * bk <= qry, sts[e][d], NEG) for d in range(nsub)] if masked else sts[e]
                m_prev = m_ref[e, 0:1, :]
                m_new = m_prev
                for d in range(nsub):
                    m_new = jnp.maximum(m_new, jnp.max(st[d], axis=0, keepdims=True))
                alpha = jnp.exp2(m_prev - m_new)
                pt = [jnp.exp2(st[d] - m_new).astype(BF16) for d in range(nsub)]
                l_new = alpha * l_ref[e]
                for d in range(nsub):
                    l_new = l_new + _dot(ones, pt[d])
                l_ref[e] = l_new
                m_ref[e] = jnp.broadcast_to(m_new, (8, bq))
                pts.append(pt)
                alphas.append(alpha)
            for e in range(2):
                acc = alphas[e] * acc_ref[e]
                for d in range(nsub):
                    acc = acc + _dot(vt_ref[0, kbs[d], e * HEAD:(e + 1) * HEAD, :], pts[e][d])
                acc_ref[e] = acc

        step(qi * nsub, True)

        def loop(t, c):
            step(t * nsub, False)
            return c

        lax.fori_loop(0, qi, loop, 0)
        outs, lses = [], []
        for e in range(2):
            l = l_ref[e, 0:1, :]
            outs.append(acc_ref[e] / l)
            lses.append(jnp.broadcast_to(m_ref[e, 0:1, :] * LN2 + jnp.log(l), (HEAD, bq)))
        o_ref[...] = jnp.transpose(jnp.concatenate(outs, axis=0))
        lse_ref[...] = jnp.transpose(jnp.concatenate(lses, axis=0))

    return pl.pallas_call(
        body, name="mla_fwd", grid=(2, nq),
        in_specs=[pl.BlockSpec((bq, 256), lambda j, i: (i, j)),
                  pl.BlockSpec((T, 256), lambda j, i: (0, j)),
                  pl.BlockSpec((1, nk, LANES, bk), lambda j, i: (j, 0, 0, 0))],
        out_specs=[pl.BlockSpec((bq, LANES), lambda j, i: (i, j)),
                   pl.BlockSpec((bq, LANES), lambda j, i: (i, j))],
        out_shape=[jax.ShapeDtypeStruct((T, 256), F32), jax.ShapeDtypeStruct((T, 256), F32)],
        scratch_shapes=[pltpu.VMEM((2, HEAD, bq), F32), pltpu.VMEM((2, 8, bq), F32), pltpu.VMEM((2, 8, bq), F32)],
        compiler_params=_cparams(("parallel", "arbitrary")),
    )(qm, km, vt)


def _mla_bwd(qm, km, vm, y, lse, dy):
    T = qm.shape[0]
    bq, bk = min(MLA_BQ, T), ATT_BLK
    nq, nsub, nk = T // bq, bq // bk, T // bk

    def body(q_ref, k_ref, v_ref, y_ref, lse_ref, dy_ref, dq_ref, dkt_ref, dvt_ref, dob_ref, st_ref, qt_ref, dot_ref):
        qi = pl.program_id(1)

        @pl.when(qi == 0)
        def _():
            dkt_ref[...] = jnp.zeros_like(dkt_ref)
            dvt_ref[...] = jnp.zeros_like(dvt_ref)

        lane = lax.broadcasted_iota(jnp.int32, (1, LANES), 1) // HEAD
        row, col = _causal_masks(bq, bk)
        dq_ref[...] = jnp.zeros_like(dq_ref)
        lse = lse_ref[...]
        lse_other = pltpu.roll(lse, HEAD, 1)
        qt_ref[...] = jnp.transpose(q_ref[...].astype(F32)).astype(BF16)
        dot_ref[...] = jnp.transpose(dy_ref[...]).astype(BF16)
        for e in range(2):
            do = jnp.where(lane == e, dy_ref[...], 0.0)
            dob_ref[e] = do.astype(BF16)
            st_ref[2 * e] = jnp.where(lane == e, lse, lse_other) * LOG2E
            st_ref[2 * e + 1] = jnp.broadcast_to(jnp.sum(do * y_ref[...], axis=1, keepdims=True), (bq, LANES))

        hss = [slice(e * LANES, (e + 1) * LANES) for e in range(2)]
        tile = lambda a: jnp.concatenate([a] * (bk // LANES), axis=1)

        def step(kb0, masked):
            kbs = [kb0 + d for d in range(nsub)]
            rows = [pl.ds(pl.multiple_of(kb * bk, bk), bk) for kb in kbs]
            pairs = [(d, e) for d in range(nsub) for e in range(2)]
            ss = {(d, e): _dot_nt(q_ref[:, hss[e]], k_ref[rows[d], hss[e]]) for d, e in pairs}
            dps = {(d, e): _dot_nt(dob_ref[e], jnp.where(lane == e, v_ref[rows[d], :], 0)) for d, e in pairs}
            ps, dss = {}, {}
            for d, e in pairs:
                s = jnp.where(col + d * bk <= row, ss[d, e], NEG) if masked else ss[d, e]
                p = jnp.exp2(s - tile(st_ref[2 * e]))
                dss[d, e] = (p * (dps[d, e] - tile(st_ref[2 * e + 1]))).astype(BF16)
                ps[d, e] = p.astype(BF16)
            for d, e in pairs:
                dvt_ref[0, kbs[d], e * HEAD:(e + 1) * HEAD, :] += _dot(dot_ref[e * HEAD:(e + 1) * HEAD, :], ps[d, e])
            for d, e in pairs:
                dkt_ref[0, kbs[d], hss[e], :] += _dot(qt_ref[hss[e], :], dss[d, e])
            for e in range(2):
                dq = dq_ref[:, hss[e]]
                for d in range(nsub):
                    dq = dq + _dot(dss[d, e], k_ref[rows[d], hss[e]])
                dq_ref[:, hss[e]] = dq

        step(qi * nsub, True)

        def loop(t, c):
            step(t * nsub, False)
            return c

        lax.fori_loop(0, qi, loop, 0)
        dq_ref[...] *= LN2

    return pl.pallas_call(
        body, name="mla_bwd", grid=(2, nq),
        in_specs=[pl.BlockSpec((bq, 256), lambda j, i: (i, j)),
                  pl.BlockSpec((T, 256), lambda j, i: (0, j)),
                  pl.BlockSpec((T, LANES), lambda j, i: (0, j)),
                  pl.BlockSpec((bq, LANES), lambda j, i: (i, j)),
                  pl.BlockSpec((bq, LANES), lambda j, i: (i, j)),
                  pl.BlockSpec((bq, LANES), lambda j, i: (i, j))],
        out_specs=[pl.BlockSpec((bq, 256), lambda j, i: (i, j)),
                   pl.BlockSpec((1, nk, 256, bk), lambda j, i: (j, 0, 0, 0)),
                   pl.BlockSpec((1, nk, LANES, bk), lambda j, i: (j, 0, 0, 0))],
        out_shape=[jax.ShapeDtypeStruct((T, 512), F32),
                   jax.ShapeDtypeStruct((2, nk, 256, bk), F32),
                   jax.ShapeDtypeStruct((2, nk, LANES, bk), F32)],
        scratch_shapes=[pltpu.VMEM((2, bq, LANES), BF16), pltpu.VMEM((4, bq, LANES), F32),
                        pltpu.VMEM((256, bq), BF16), pltpu.VMEM((LANES, bq), BF16)],
        compiler_params=_cparams(("parallel", "arbitrary")),
    )(qm, km, vm, y, lse, dy)


def _suffix_ones(n):
    r = lax.broadcasted_iota(jnp.int32, (n, n), 0)
    c = lax.broadcasted_iota(jnp.int32, (n, n), 1)
    return (r >= c).astype(BF16)


def _prefix_ones(n):
    r = lax.broadcasted_iota(jnp.int32, (n, n), 0)
    c = lax.broadcasted_iota(jnp.int32, (n, n), 1)
    return (r <= c).astype(BF16)


def _sb_specs(T, bq):
    qo, ko, vo = (_INT_OFF[n] // LANES for n in ("d_q", "d_k", "d_v"))
    return [pl.BlockSpec((bq, LANES), lambda j, i: (i, qo + j)),
            pl.BlockSpec((T, LANES), lambda j, i: (0, ko + j)),
            pl.BlockSpec((T, LANES), lambda j, i: (0, vo + j))]


def _sb_fwd(hb):
    T = hb.shape[0]
    bq = bk = ATT_BLK
    nq = T // bq

    def body(q_ref, k_ref, v_ref, o_ref, tot_ref, cnt_ref, qm_ref, car_ref):
        qi = pl.program_id(1)
        lane = lax.broadcasted_iota(jnp.int32, (1, LANES), 1) // HEAD
        row, col = _causal_masks(bq, bk)
        strict = col < row
        u = _suffix_ones(bk)
        o_ref[...] = jnp.zeros_like(o_ref)
        car_ref[...] = jnp.zeros_like(car_ref)
        for e in range(2):
            qm_ref[e] = jnp.where(lane == e, q_ref[...], 0) * 0.125

        def step(blocks):
            tile = lambda a: jnp.concatenate([a] * (bk // LANES), axis=1)
            rows = [pl.ds(pl.multiple_of(kb * bk, bk), bk) for kb, _ in blocks]
            pairs = [(b, e) for b in range(len(blocks)) for e in range(2)]
            zs = {(b, e): _dot_nt(qm_ref[e], k_ref[rows[b], :]) for b, e in pairs}
            splits = {}
            for b, e in pairs:
                z = zs[b, e]
                lk = jnp.minimum(-z, 0.0) - jnp.log(1.0 + jnp.exp(-jnp.abs(z)))
                if blocks[b][1] is not None:
                    lk = jnp.where(blocks[b][1], lk, 0.0)
                splits[b, e] = _split(lk)
            sufs = {be: _dot(hi, u) + _dot(lo, u) for be, (hi, lo) in splits.items()}
            car = [car_ref[0], car_ref[1]]
            aas = {}
            for b, e in pairs:
                a = jnp.exp(zs[b, e] + sufs[b, e] + tile(car[e]))
                if blocks[b][1] is not None:
                    a = jnp.where(blocks[b][1], a, 0.0)
                aas[b, e] = a.astype(BF16)
                car[e] = car[e] + jnp.broadcast_to(sufs[b, e][:, 0:1], (bq, LANES))
            acc = o_ref[...]
            for b, e in pairs:
                acc = acc + _dot(aas[b, e], jnp.where(lane == e, v_ref[rows[b], :], 0))
            o_ref[...] = acc
            car_ref[0], car_ref[1] = car

        step([(qi, strict), (jnp.maximum(qi - 1, 0), qi > 0)])

        def live():
            return jnp.max(jnp.maximum(car_ref[0], car_ref[1])) >= SB_DEAD

        def cond(c):
            return jnp.logical_and(c[0] < qi, c[1])

        def loop(c):
            step([(qi - 1 - c[0], None)])
            return c[0] + 1, live()

        done, _ = lax.while_loop(cond, loop, (jnp.minimum(qi, 1), live()))
        tot_ref[...] = jnp.where(lane == 0, car_ref[0], car_ref[1])
        cnt_ref[pl.program_id(0), qi] = done.astype(F32)

    return pl.pallas_call(
        body, name="sb_fwd", grid=(2, nq), in_specs=_sb_specs(T, bq),
        out_specs=[pl.BlockSpec((bq, LANES), lambda j, i: (i, j)), pl.BlockSpec((bq, LANES), lambda j, i: (i, j)),
                   pl.BlockSpec(memory_space=pltpu.SMEM)],
        out_shape=[jax.ShapeDtypeStruct((T, 256), F32), jax.ShapeDtypeStruct((T, 256), F32),
                   jax.ShapeDtypeStruct((2, nq), F32)],
        scratch_shapes=[pltpu.VMEM((2, bq, LANES), BF16), pltpu.VMEM((2, bq, LANES), F32)],
        compiler_params=_cparams(("parallel", "arbitrary")),
    )(hb, hb, hb)


def _sb_bwd(hb, tot, cnt, dy):
    T = hb.shape[0]
    bq = bk = ATT_BLK
    nq = T // bq

    def body(q_ref, k_ref, v_ref, tot_ref, dy_ref, cnt_ref, dq_ref, dk_ref, dv_ref, qm_ref, dob_ref, dqa_ref, rem_ref,
             cg_ref):
        qi = pl.program_id(1)

        @pl.when(qi == 0)
        def _():
            dk_ref[...] = jnp.zeros_like(dk_ref)
            dv_ref[...] = jnp.zeros_like(dv_ref)

        lane = lax.broadcasted_iota(jnp.int32, (1, LANES), 1) // HEAD
        row, col = _causal_masks(bq, bk)
        strict = col < row
        u = _prefix_ones(bk)
        tot = tot_ref[...]
        tot_other = pltpu.roll(tot, HEAD, 1)
        dqa_ref[...] = jnp.zeros_like(dqa_ref)
        cg_ref[...] = jnp.zeros_like(cg_ref)
        for e in range(2):
            qm_ref[e] = jnp.where(lane == e, q_ref[...], 0) * 0.125
            dob_ref[e] = jnp.where(lane == e, dy_ref[...], 0.0).astype(BF16)
            rem_ref[e] = jnp.where(lane == e, tot, tot_other)

        def step(blocks):
            tile = lambda a: jnp.concatenate([a] * (bk // LANES), axis=1)
            nb = len(blocks)
            rows = [pl.ds(pl.multiple_of(kb * bk, bk), bk) for kb, _ in blocks]
            pairs = [(b, e) for b in range(nb) for e in range(2)]
            mask = lambda b, x: x if blocks[b][1] is None else jnp.where(blocks[b][1], x, 0.0)
            zs = {(b, e): _dot_nt(qm_ref[e], k_ref[rows[b], :]) for b, e in pairs}
            das = {(b, e): _dot_nt(dob_ref[e], jnp.where(lane == e, v_ref[rows[b], :], 0)) for b, e in pairs}
            zls, splits = {}, {}
            for b, e in pairs:
                z = zs[b, e]
                lk = mask(b, jnp.minimum(-z, 0.0) - jnp.log(1.0 + jnp.exp(-jnp.abs(z))))
                zls[b, e] = z + lk
                splits[b, e] = _split(lk)
            pres = {be: _dot(hi, u) + _dot(lo, u) for be, (hi, lo) in splits.items()}
            rem = [rem_ref[0], rem_ref[1]]
            aas, gs, gsplits = {}, {}, {}
            for b, e in pairs:
                a = mask(b, jnp.exp(zls[b, e] + (tile(rem[e]) - pres[b, e])))
                gs[b, e] = a * das[b, e]
                aas[b, e] = a.astype(BF16)
                gsplits[b, e] = _split(gs[b, e])
                rem[e] = rem[e] - jnp.broadcast_to(pres[b, e][:, bk - 1:bk], (bq, LANES))
            for b in range(nb):
                dv_ref[rows[b], :] += _dot_tn(aas[b, 0], dob_ref[0]) + _dot_tn(aas[b, 1], dob_ref[1])
            gpres = {be: _dot(hi, u) + _dot(lo, u) for be, (hi, lo) in gsplits.items()}
            cg = [cg_ref[0], cg_ref[1]]
            dzs = {}
            for b, e in pairs:
                dz = mask(b, gs[b, e] - jnp.exp(zls[b, e]) * (tile(cg[e]) + gpres[b, e]))
                dzs[b, e] = dz.astype(BF16)
                cg[e] = cg[e] + jnp.broadcast_to(gpres[b, e][:, bk - 1:bk], (bq, LANES))
            for b in range(nb):
                dk_ref[rows[b], :] += _dot_tn(dzs[b, 0], qm_ref[0]) + _dot_tn(dzs[b, 1], qm_ref[1])
            for e in range(2):
                dq = dqa_ref[e]
                for b in range(nb):
                    dq = dq + _dot(dzs[b, e], k_ref[rows[b], :])
                dqa_ref[e] = dq
            rem_ref[0], rem_ref[1] = rem
            cg_ref[0], cg_ref[1] = cg

        def loop(kb, c):
            step([(kb, None)])
            return c

        start = qi - jnp.clip(cnt_ref[pl.program_id(0), qi].astype(jnp.int32), 0, qi)
        lax.fori_loop(start, qi - 1, loop, 0)
        step([(jnp.maximum(qi - 1, 0), qi > 0), (qi, strict)])
        dq_ref[...] = (jnp.where(lane == 0, dqa_ref[0], dqa_ref[1]) * 0.125).astype(BF16)

    return pl.pallas_call(
        body, name="sb_bwd", grid=(2, nq),
        in_specs=_sb_specs(T, bq) + [pl.BlockSpec((bq, LANES), lambda j, i: (i, j)),
                                     pl.BlockSpec((bq, LANES), lambda j, i: (i, j)),
                                     pl.BlockSpec(memory_space=pltpu.SMEM)],
        out_specs=[pl.BlockSpec((bq, LANES), lambda j, i: (i, j)),
                   pl.BlockSpec((T, LANES), lambda j, i: (0, j)),
                   pl.BlockSpec((T, LANES), lambda j, i: (0, j))],
        out_shape=[jax.ShapeDtypeStruct((T, 256), BF16)] + [jax.ShapeDtypeStruct((T, 256), F32)] * 2,
        scratch_shapes=[pltpu.VMEM((2, bq, LANES), BF16), pltpu.VMEM((2, bq, LANES), BF16),
                        pltpu.VMEM((2, bq, LANES), F32), pltpu.VMEM((2, bq, LANES), F32),
                        pltpu.VMEM((2, bq, LANES), F32)],
        compiler_params=_cparams(("parallel", "arbitrary")),
    )(hb, hb, hb, tot, dy, cnt)


EP_TM = 512


def _ep_in_specs(tm, rev):
    idx = (lambda i: rev - i) if rev is not None else (lambda i: i)
    bo = (_INT_OFF["b_b"] - N_HB) // 256
    halo = lambda i: jnp.maximum(idx(i) * (tm // 8) - 1, 0)
    return [pl.BlockSpec((tm, 256), lambda i: (idx(i), 0)),
            pl.BlockSpec((tm, 256), lambda i: (idx(i), 0)),
            pl.BlockSpec((tm, 256), lambda i: (idx(i), 0)),
            pl.BlockSpec((tm, D_MODEL), lambda i: (idx(i), 0)),
            pl.BlockSpec((tm, 256), lambda i: (idx(i), bo)),
            pl.BlockSpec((tm, 256), lambda i: (idx(i), bo + 1)),
            pl.BlockSpec((tm, 256), lambda i: (idx(i), bo + 2)),
            pl.BlockSpec((8, 256), lambda i: (halo(i), bo + 1)),
            pl.BlockSpec((8, 256), lambda i: (halo(i), bo + 2)),
            pl.BlockSpec((3, 256), lambda i: (0, 0)),
            pl.BlockSpec((1, 256), lambda i: (0, 0)),
            pl.BlockSpec((1, D_MODEL), lambda i: (0, 0)),
            pl.BlockSpec((D_MODEL, D_MODEL), lambda i: (0, 0)),
            pl.BlockSpec((1, D_MODEL), lambda i: (0, 0))]


def _ep_mix(first, ya_ref, yc_ref, yd_ref, gate_ref, bb_ref, bc_ref, bx_ref, hc_ref, hx_ref, cw_ref, cb_ref, gg_ref):
    tm = ya_ref.shape[0]
    u = bc_ref[...] * bx_ref[...]
    halo = jnp.where(first, 0.0, hc_ref[...] * hx_ref[...])
    row = lax.broadcasted_iota(jnp.int32, (tm, 1), 0)
    u1 = jnp.where(row == 0, halo[7:8, :], pltpu.roll(u, 1, 0))
    u2 = jnp.where(row == 0, halo[6:7, :], jnp.where(row == 1, halo[7:8, :], pltpu.roll(u, 2, 0)))
    cw = cw_ref[...]
    conv = cw[0:1, :] * u2 + cw[1:2, :] * u1 + cw[2:3, :] * u + cb_ref[...]
    bb = bb_ref[...]
    ys = [ya_ref[...], bb * conv, yc_ref[...], yd_ref[...]]
    rs = [_rms(y) for y in ys]
    gg = gg_ref[...]
    yhat = jnp.concatenate([y * r for y, r in zip(ys, rs)], axis=1)
    gate = gate_ref[...]
    sig = 1.0 / (1.0 + jnp.exp(-gate))
    return u, u1, u2, conv, bb, rs, yhat, yhat * gg, gate, sig


def _epilogue_fwd(x, ya, yc, yd, hf, conv_w, conv_b, g_grp, w_out, g_post, tgt=None):
    T = x.shape[0]
    tm = EP_TM
    row_spec = pl.BlockSpec((tm, D_MODEL), lambda i: (i, 0))

    def layer_out(refs):
        (x_ref, ya_ref, yc_ref, yd_ref, gate_ref, bb_ref, bc_ref, bx_ref, hc_ref, hx_ref, cw_ref, cb_ref,
         gg_ref, wo_ref, gp_ref) = refs
        (_, _, _, _, _, _, _, yn, gate, sig) = _ep_mix(
            pl.program_id(0) == 0, ya_ref, yc_ref, yd_ref, gate_ref, bb_ref, bc_ref, bx_ref, hc_ref, hx_ref,
            cw_ref, cb_ref, gg_ref)
        z = _dot((yn * (gate * sig)).astype(BF16), wo_ref[...])
        return x_ref[...] + z * _rms(z) * gp_ref[...]

    args = (x, ya, yc, yd, hf, hf, hf, hf, hf, hf, conv_w, conv_b, g_grp, w_out, g_post)
    in_specs = [row_spec] + _ep_in_specs(tm, None)
    if tgt is None:
        def body(*refs):
            refs[-1][...] = layer_out(refs[:-1])

        return pl.pallas_call(
            body, name="epilogue_fwd", grid=(T // tm,), in_specs=in_specs, out_specs=row_spec,
            out_shape=jax.ShapeDtypeStruct((T, D_MODEL), F32), compiler_params=_cparams(("parallel",)),
        )(*args)

    def body_loss(*refs):
        t_ref, dy_ref, l_ref = refs[-3:]

        @pl.when(pl.program_id(0) == 0)
        def _():
            l_ref[...] = jnp.zeros_like(l_ref)

        d = layer_out(refs[:-3]) - t_ref[...]
        dy_ref[...] = d * (1.0 / D_MODEL)
        part = jnp.sum(jnp.sum(d * d, axis=1, keepdims=True), axis=0, keepdims=True)
        l_ref[...] += part * (0.5 / D_MODEL)

    return pl.pallas_call(
        body_loss, name="epilogue_fwd_loss", grid=(T // tm,), in_specs=in_specs + [row_spec],
        out_specs=[row_spec, pl.BlockSpec((8, LANES), lambda i: (0, 0))],
        out_shape=[jax.ShapeDtypeStruct((T, D_MODEL), F32), jax.ShapeDtypeStruct((8, LANES), F32)],
        compiler_params=_cparams(("arbitrary",)),
    )(*args, tgt)


def _epilogue_bwd(dxn, ya, yc, yd, hf, conv_w, conv_b, g_grp, w_out, g_post):
    T = dxn.shape[0]
    tm = EP_TM
    nt = T // tm
    ridx = lambda i: (nt - 1 - i, 0)

    def body(dx_ref, ya_ref, yc_ref, yd_ref, gate_ref, bb_ref, bc_ref, bx_ref, hc_ref, hx_ref, cw_ref, cb_ref,
             gg_ref, wo_ref, gp_ref,
             dya_ref, dyc_ref, dyd_ref, dhf_ref, dwo_ref, dgp_ref, dgg_ref, dcw_ref, dcb_ref, carry_ref):
        i = pl.program_id(0)

        @pl.when(i == 0)
        def _():
            for r in (dwo_ref, dgp_ref, dgg_ref, dcw_ref, dcb_ref, carry_ref):
                r[...] = jnp.zeros_like(r)

        (u, u1, u2, conv, bb, rs, yhat, yn, gate, sig) = _ep_mix(
            i == nt - 1, ya_ref, yc_ref, yd_ref, gate_ref, bb_ref, bc_ref, bx_ref, hc_ref, hx_ref,
            cw_ref, cb_ref, gg_ref)
        silu = gate * sig
        ymix = (yn * silu).astype(BF16)
        z = _dot(ymix, wo_ref[...])
        rz = _rms(z)
        dz, dgrow = _rms_bwd(dx_ref[...], z * rz, rz, gp_ref[...])
        dgp_ref[...] += _colsum(dgrow)
        dzb = dz.astype(BF16)
        dwo_ref[...] += _dot_tn(ymix, dzb)
        dymix = _dot_nt(dzb, wo_ref[...])
        dhf_ref[:, 0:D_MODEL] = (dymix * yn * (sig * (1.0 + gate * (1.0 - sig)))).astype(BF16)
        dyn = dymix * silu
        dgg_ref[...] += _colsum(dyn * yhat)
        gg = gg_ref[...]
        dys = []
        for gi in range(4):
            sl = slice(gi * GROUP, (gi + 1) * GROUP)
            dyh = dyn[:, sl] * gg[:, sl]
            yh = yhat[:, sl]
            dys.append(rs[gi] * (dyh - yh * jnp.mean(dyh * yh, axis=-1, keepdims=True)))
        dya_ref[...] = dys[0]
        dyc_ref[...] = dys[2]
        dyd_ref[...] = dys[3]
        dyb = dys[1]
        dhf_ref[:, D_MODEL:D_MODEL + 256] = (dyb * conv).astype(BF16)
        dconv = dyb * bb
        dcb_ref[...] += _colsum(dconv)
        dcw_ref[0:1, :] += _colsum(dconv * u2)
        dcw_ref[1:2, :] += _colsum(dconv * u1)
        dcw_ref[2:3, :] += _colsum(dconv * u)
        carry = carry_ref[...]
        row = lax.broadcasted_iota(jnp.int32, (tm, 1), 0)
        d1 = jnp.where(row == tm - 1, carry[0:1, :], pltpu.roll(dconv, tm - 1, 0))
        d2 = jnp.where(row == tm - 2, carry[0:1, :],
                       jnp.where(row == tm - 1, carry[1:2, :], pltpu.roll(dconv, tm - 2, 0)))
        cw = cw_ref[...]
        du = cw[2:3, :] * dconv + cw[1:2, :] * d1 + cw[0:1, :] * d2
        dhf_ref[:, D_MODEL + 256:D_MODEL + 512] = (du * bx_ref[...]).astype(BF16)
        dhf_ref[:, D_MODEL + 512:D_MODEL + 768] = (du * bc_ref[...]).astype(BF16)
        carry_ref[...] = dconv[0:8, :]

    in_specs = [pl.BlockSpec((tm, D_MODEL), ridx)] + _ep_in_specs(tm, nt - 1)
    return pl.pallas_call(
        body, name="epilogue_bwd", grid=(nt,), in_specs=in_specs,
        out_specs=[pl.BlockSpec((tm, 256), ridx), pl.BlockSpec((tm, 256), ridx), pl.BlockSpec((tm, 256), ridx),
                   pl.BlockSpec((tm, D_MODEL + 768), ridx),
                   pl.BlockSpec((D_MODEL, D_MODEL), lambda i: (0, 0)),
                   pl.BlockSpec((1, D_MODEL), lambda i: (0, 0)),
                   pl.BlockSpec((1, D_MODEL), lambda i: (0, 0)),
                   pl.BlockSpec((8, 256), lambda i: (0, 0)),
                   pl.BlockSpec((1, 256), lambda i: (0, 0))],
        out_shape=[jax.ShapeDtypeStruct((T, 256), F32)] * 3
                  + [jax.ShapeDtypeStruct((T, D_MODEL + 768), BF16),
                     jax.ShapeDtypeStruct((D_MODEL, D_MODEL), F32),
                     jax.ShapeDtypeStruct((1, D_MODEL), F32),
                     jax.ShapeDtypeStruct((1, D_MODEL), F32),
                     jax.ShapeDtypeStruct((8, 256), F32),
                     jax.ShapeDtypeStruct((1, 256), F32)],
        scratch_shapes=[pltpu.VMEM((8, 256), F32)],
        compiler_params=_cparams(("arbitrary",)),
    )(dxn, ya, yc, yd, hf, hf, hf, hf, hf, hf, conv_w, conv_b, g_grp, w_out, g_post)


def _place():
    return lax.axis_index("x"), lax.axis_index("y"), lax.axis_index("c")


def _other_chips(x, y):
    return [(1 - x, y), (x, 1 - y), (1 - x, 1 - y)]


HBM = pl.BlockSpec(memory_space=pl.ANY)


def _gather_weights(shards):
    n = len(shards)

    def body(*refs):
        ins, outs = refs[:n], refs[n:2 * n]
        ici_send, ici_recv, d2d_send, d2d_recv, local_sems = refs[2 * n:]
        x, y, c = _place()
        me = 2 * x + y
        chips = _other_chips(x, y)

        def ici(a, j, layer_from):
            px, py = chips[j]
            return pltpu.make_async_remote_copy(
                src_ref=ins[a].at[c], dst_ref=outs[a].at[layer_from, c], send_sem=ici_send.at[3 * a + j],
                recv_sem=ici_recv.at[3 * a + j], device_id=(px, py, c), device_id_type=MESH)

        def d2d(a, j, layer):
            px, py = chips[j]
            blk = outs[a].at[2 * px + py, layer]
            return pltpu.make_async_remote_copy(
                src_ref=blk, dst_ref=blk, send_sem=d2d_send.at[3 * a + j], recv_sem=d2d_recv.at[3 * a + j],
                device_id=(x, y, 1 - c), device_id_type=MESH)

        local = [pltpu.make_async_copy(ins[a], outs[a].at[me], local_sems.at[a]) for a in range(n)]
        for cp in local:
            cp.start()
        sends = [ici(a, j, me) for j in range(3) for a in range(n)]
        for cp in sends:
            cp.start()
        for j in range(3):
            px, py = chips[j]
            for a in range(n):
                ici(a, j, 2 * px + py).wait_recv()
                fwd = d2d(a, j, c)
                fwd.start()
                sends.append(fwd)
        for j in range(3):
            for a in range(n):
                d2d(a, j, 1 - c).wait_recv()
        for cp in sends:
            cp.wait_send()
        for cp in local:
            cp.wait()

    return pl.pallas_call(
        body, name="gather_weights",
        in_specs=[HBM] * n, out_specs=[HBM] * n,
        out_shape=[jax.ShapeDtypeStruct((4,) + s.shape, s.dtype) for s in shards],
        scratch_shapes=[pltpu.SemaphoreType.DMA((3 * n,))] * 4 + [pltpu.SemaphoreType.DMA((n,))],
    )(*shards)


def _exchange_chips(parts, small):
    n = len(parts)

    def body(*refs):
        ins, sm_ref = refs[:n], refs[n]
        outs, osm_ref = refs[n + 1:2 * n + 1], refs[2 * n + 1]
        send_sems, recv_sems, ssend_sems, srecv_sems, local_sems = refs[2 * n + 2:]
        x, y, c = _place()
        me = 2 * x + y
        dev = 4 * x + 2 * y + c
        local = [pltpu.make_async_copy(ins[a].at[me], outs[a].at[me], local_sems.at[a]) for a in range(n)]
        local.append(pltpu.make_async_copy(sm_ref, osm_ref.at[dev], local_sems.at[n]))
        for cp in local:
            cp.start()
        sends = []
        for j, (px, py) in enumerate(_other_chips(x, y)):
            for a in range(n):
                cp = pltpu.make_async_remote_copy(
                    src_ref=ins[a].at[2 * px + py], dst_ref=outs[a].at[me], send_sem=send_sems.at[3 * a + j],
                    recv_sem=recv_sems.at[3 * a + j], device_id=(px, py, c), device_id_type=MESH)
                cp.start()
                sends.append(cp)
        flips = [(fx, fy, fc) for fx in (0, 1) for fy in (0, 1) for fc in (0, 1)][1:]
        for j, (fx, fy, fc) in enumerate(flips):
            cp = pltpu.make_async_remote_copy(
                src_ref=sm_ref, dst_ref=osm_ref.at[dev], send_sem=ssend_sems.at[j], recv_sem=srecv_sems.at[j],
                device_id=(x ^ fx, y ^ fy, c ^ fc), device_id_type=MESH)
            cp.start()
            sends.append(cp)
        for j, (px, py) in enumerate(_other_chips(x, y)):
            for a in range(n):
                pltpu.make_async_remote_copy(
                    src_ref=ins[a].at[me], dst_ref=outs[a].at[2 * px + py], send_sem=send_sems.at[3 * a + j],
                    recv_sem=recv_sems.at[3 * a + j], device_id=(px, py, c), device_id_type=MESH).wait_recv()
        for j, (fx, fy, fc) in enumerate(flips):
            src = 4 * (x ^ fx) + 2 * (y ^ fy) + (c ^ fc)
            pltpu.make_async_remote_copy(
                src_ref=sm_ref, dst_ref=osm_ref.at[src], send_sem=ssend_sems.at[j], recv_sem=srecv_sems.at[j],
                device_id=(x ^ fx, y ^ fy, c ^ fc), device_id_type=MESH).wait_recv()
        for cp in sends:
            cp.wait_send()
        for cp in local:
            cp.wait()

    return pl.pallas_call(
        body, name="exchange_chips",
        in_specs=[HBM] * (n + 1), out_specs=[HBM] * (n + 1),
        out_shape=[jax.ShapeDtypeStruct(p.shape, p.dtype) for p in parts]
                  + [jax.ShapeDtypeStruct((8,) + small.shape, small.dtype)],
        scratch_shapes=[pltpu.SemaphoreType.DMA((3 * n,)), pltpu.SemaphoreType.DMA((3 * n,)),
                        pltpu.SemaphoreType.DMA((7,)), pltpu.SemaphoreType.DMA((7,)),
                        pltpu.SemaphoreType.DMA((n + 1,))],
    )(*parts, small)


def _swap_cores(parts, name):
    n = len(parts)

    def body(*refs):
        ins, outs, send_sems, recv_sems = refs[:n], refs[n:2 * n], refs[2 * n], refs[2 * n + 1]
        x, y, c = _place()
        copies = [pltpu.make_async_remote_copy(
            src_ref=ins[a], dst_ref=outs[a], send_sem=send_sems.at[a], recv_sem=recv_sems.at[a],
            device_id=(x, y, 1 - c), device_id_type=MESH) for a in range(n)]
        for cp in copies:
            cp.start()
        for cp in copies:
            cp.wait()

    return pl.pallas_call(
        body, name=name, in_specs=[HBM] * n, out_specs=[HBM] * n,
        out_shape=[jax.ShapeDtypeStruct(p.shape, p.dtype) for p in parts],
        scratch_shapes=[pltpu.SemaphoreType.DMA((n,)), pltpu.SemaphoreType.DMA((n,))],
    )(*parts)


def _tile(rows, cols):
    for cand in (256, 128, 64):
        if rows % cand == 0:
            return cand, cols
    if rows > 64 and cols % 256 == 0:
        return rows, 256
    return rows, cols


def _add(a, b, name):
    L, R, C = a.shape
    tr, tc = _tile(R, C)

    def body(a_ref, b_ref, o_ref):
        o_ref[...] = (a_ref[...] + b_ref[...]).astype(BF16)

    spec = pl.BlockSpec((1, tr, tc), lambda l, i, j: (l, i, j))
    return pl.pallas_call(
        body, name=name, grid=(L, R // tr, C // tc), in_specs=[spec, spec], out_specs=spec,
        out_shape=jax.ShapeDtypeStruct((L, R, C), BF16),
        compiler_params=_cparams(("parallel", "parallel", "parallel")),
    )(a, b)


def _sum_leading(buf, name):
    n, R, C = buf.shape
    tr, tc = _tile(R, C)

    def body(b_ref, o_ref):
        acc = b_ref[0].astype(F32)
        for k in range(1, n):
            acc = acc + b_ref[k].astype(F32)
        o_ref[...] = acc

    return pl.pallas_call(
        body, name=name, grid=(R // tr, C // tc),
        in_specs=[pl.BlockSpec((n, tr, tc), lambda i, j: (0, i, j))],
        out_specs=pl.BlockSpec((tr, tc), lambda i, j: (i, j)),
        out_shape=jax.ShapeDtypeStruct((R, C), F32),
        compiler_params=_cparams(("parallel", "parallel")),
    )(buf)


def _adam_update(w, g, m, v):
    c1 = 1.0 / (1.0 - ADAM_B1 ** ADAM_STEP)
    c2 = 1.0 / (1.0 - ADAM_B2 ** ADAM_STEP)
    mn = ADAM_B1 * m + (1.0 - ADAM_B1) * g
    vn = ADAM_B2 * v + (1.0 - ADAM_B2) * (g * g)
    return -ADAM_LR * ((mn * c1) / (jnp.sqrt(vn * c2) + ADAM_EPS) + ADAM_WD * w), mn, vn


def _adamw_layers(w, m, v, g_mine, g_other, name):
    _, R, C = w.shape
    tr, tc = _tile(R, C)

    def body(w_ref, m_ref, v_ref, gm_ref, go_ref, g_ref, d_ref, mo_ref, vo_ref):
        g = jnp.where(pl.program_id(0) == lax.axis_index("c"), gm_ref[...], go_ref[...])
        g_ref[0] = g
        d_ref[0], mo_ref[0], vo_ref[0] = _adam_update(w_ref[0], g, m_ref[0], v_ref[0])

    spec3 = pl.BlockSpec((1, tr, tc), lambda l, i, j: (l, i, j))
    spec2 = pl.BlockSpec((tr, tc), lambda l, i, j: (i, j))
    return pl.pallas_call(
        body, name=name, grid=(2, R // tr, C // tc),
        in_specs=[spec3] * 3 + [spec2] * 2, out_specs=[spec3] * 4,
        out_shape=[jax.ShapeDtypeStruct(w.shape, F32)] * 4,
        compiler_params=_cparams(("parallel", "parallel", "parallel")),
    )(w, m, v, g_mine, g_other)


PACK_C = 1024
_BIG = ("w_in", "w_out", "mla_w_uq", "mla_w_ukv", "conv_w")
_SMALL = ("norm_pre", "group_norm", "norm_post", "conv_b", "mla_q_norm", "mla_kv_norm", "attn_sinks")
_SMALL_W = {"norm_pre": 1024, "group_norm": 1024, "norm_post": 1024, "conv_b": 256, "mla_q_norm": 256,
            "mla_kv_norm": 128, "attn_sinks": 4}


_LOSS_AT = divmod(DEPTH * sum(_SMALL_W.values()), PACK_C)


def _pack_small(d, loss):
    flat = jnp.concatenate([d[n].reshape(-1) for n in _SMALL] + [loss.reshape(1)])
    return jnp.pad(flat, (0, 8 * PACK_C - flat.shape[0])).reshape(8, PACK_C)


def _adamw_small(w, m, v, got):
    ns = len(_SMALL)

    def body(*refs):
        got_ref = refs[3 * ns]
        outs = refs[3 * ns + 1:]
        gsum = got_ref[0]
        for d in range(1, 8):
            gsum = gsum + got_ref[d]
        outs[4 * ns][...] = gsum[_LOSS_AT[0]:_LOSS_AT[0] + 1, _LOSS_AT[1]:_LOSS_AT[1] + 1]
        off = 0
        for i, name in enumerate(_SMALL):
            wd = _SMALL_W[name]
            rows = []
            for l in range(DEPTH):
                r, c0 = divmod(off + l * wd, PACK_C)
                rows.append(gsum[r:r + 1, c0:c0 + wd])
            off += DEPTH * wd
            g = jnp.concatenate(rows, axis=0)
            delta, mn, vn = _adam_update(refs[i][...], g, refs[ns + i][...], refs[2 * ns + i][...])
            outs[i][...] = g
            outs[ns + i][...] = delta
            outs[2 * ns + i][...] = mn
            outs[3 * ns + i][...] = vn

    shapes = [jax.ShapeDtypeStruct(w[n].shape, F32) for n in _SMALL]
    res = pl.pallas_call(body, name="adamw_small", out_shape=shapes * 4 + [jax.ShapeDtypeStruct((1, 1), F32)])(
        *[w[n] for n in _SMALL], *[m[n] for n in _SMALL], *[v[n] for n in _SMALL], got)
    return [dict(zip(_SMALL, res[k * ns:(k + 1) * ns])) for k in range(4)], res[4 * ns]


def _w_in_internal(wt):
    rows = []
    for n in _INT_ORDER:
        o, wd = _REAL_OFF[n]
        rows.append(wt[o:o + wd])
        if _INT_W[n] != wd:
            rows.append(jnp.zeros((_INT_W[n] - wd, wt.shape[1]), wt.dtype))
    return jnp.concatenate(rows, axis=0)


def _w_in_real(dwt):
    return jnp.concatenate([dwt[_INT_OFF[n]:_INT_OFF[n] + wd] for n, wd in _REAL], axis=0)


def _uq_internal(w):
    return jnp.pad(w.reshape(256, 4, 96), ((0, 0), (0, 0), (0, 32))).reshape(256, 512)


def _uq_real(dw):
    return dw.reshape(256, 4, 128)[:, :, :96].reshape(256, 384)


def _ukv_internal(w):
    w4 = w.reshape(128, 4, 128)
    k = jnp.pad(w4[:, :, :64], ((0, 0), (0, 0), (0, 64))).reshape(128, 512)
    return jnp.concatenate([k, w4[:, :, 64:].reshape(128, 256)], axis=1)


def _ukv_real(dw):
    k = dw[:, :512].reshape(128, 4, 128)[:, :, :64]
    v = dw[:, 512:].reshape(128, 4, 64)
    return jnp.concatenate([k, v], axis=2).reshape(128, 512)


def _layer_fwd(x, pos, p, tgt=None):
    xn, hb, hf = _inproj_fwd(x, p["norm_pre"], p["w_in"])
    ya = _swa_fwd(hb, p["attn_sinks"])
    qm, km, vm, vt = _mla_prep_fwd(hf, pos, p["mla_q_norm"], p["mla_kv_norm"], p["mla_w_uq"], p["mla_w_ukv"])
    yc, lse = _mla_fwd(qm, km, vt)
    yd, tot, cnt = _sb_fwd(hb)
    x_next = _epilogue_fwd(x, ya, yc, yd, hf, p["conv_w"], p["conv_b"], p["group_norm"], p["w_out"], p["norm_post"],
                           tgt)
    return x_next, dict(x=x, xn=xn, hb=hb, hf=hf, ya=ya, yc=yc, yd=yd, tot=tot, cnt=cnt, qm=qm, km=km, vm=vm, lse=lse)


def _layer_bwd(dx_next, pos, p, s):
    (dya, dyc, dyd, dhf, dw_out, dg_post, dg_grp, dconv_w, dconv_b) = _epilogue_bwd(
        dx_next, s["ya"], s["yc"], s["yd"], s["hf"], p["conv_w"], p["conv_b"], p["group_norm"], p["w_out"],
        p["norm_post"])
    dq_d, dk_d, dv_d = _sb_bwd(s["hb"], s["tot"], s["cnt"], dyd)
    dqm, dkt, dvt = _mla_bwd(s["qm"], s["km"], s["vm"], s["yc"], s["lse"], dyc)
    dc, dw_uq, dw_ukv, dg_q, dg_kv = _mla_prep_bwd(
        s["hf"], pos, p["mla_q_norm"], p["mla_kv_norm"], p["mla_w_uq"], p["mla_w_ukv"], dqm, dkt, dvt)
    dq_a, dk_a, dv_a, dsinks = _swa_bwd(s["hb"], p["attn_sinks"], dya)
    dx, dh, dg_pre = _inproj_bwd_dx(s["x"], p["norm_pre"], p["w_in"], dx_next,
                                    [dq_a, dk_a, dv_a, dq_d, dk_d, dv_d, dhf, dc])
    dwt_in = _grad_over_tokens(s["xn"], dh, "inproj_bwd_dw")
    grads = dict(norm_pre=dg_pre[0], w_in_t=_w_in_real(dwt_in), attn_sinks=dsinks[0, :4], conv_w=dconv_w[:3],
                 conv_b=dconv_b[0], mla_q_norm=dg_q[0], mla_w_uq=_uq_real(dw_uq), mla_kv_norm=dg_kv[0],
                 mla_w_ukv=_ukv_real(dw_ukv), group_norm=dg_grp[0], w_out=dw_out, norm_post=dg_post[0])
    return dx, grads


_WEIGHTS = ["norm_pre", "w_in", "attn_sinks", "conv_w", "conv_b", "mla_q_norm", "mla_w_uq", "mla_kv_norm",
            "mla_w_ukv", "group_norm", "w_out", "norm_post"]


def kernel(x, positions, norm_pre, w_in, attn_sinks, conv_w, conv_b, mla_q_norm, mla_w_uq, mla_kv_norm, mla_w_ukv, group_norm, w_out, norm_post, loss_target, m_norm_pre, m_w_in, m_attn_sinks, m_conv_w, m_conv_b, m_mla_q_norm, m_mla_w_uq, m_mla_kv_norm, m_mla_w_ukv, m_group_norm, m_w_out, m_norm_post, v_norm_pre, v_w_in, v_attn_sinks, v_conv_w, v_conv_b, v_mla_q_norm, v_mla_w_uq, v_mla_kv_norm, v_mla_w_ukv, v_group_norm, v_w_out, v_norm_post):
    w = dict(norm_pre=norm_pre, w_in=w_in, attn_sinks=attn_sinks, conv_w=conv_w, conv_b=conv_b,
             mla_q_norm=mla_q_norm, mla_w_uq=mla_w_uq, mla_kv_norm=mla_kv_norm, mla_w_ukv=mla_w_ukv,
             group_norm=group_norm, w_out=w_out, norm_post=norm_post)
    m = dict(norm_pre=m_norm_pre, w_in=m_w_in, attn_sinks=m_attn_sinks, conv_w=m_conv_w, conv_b=m_conv_b,
             mla_q_norm=m_mla_q_norm, mla_w_uq=m_mla_w_uq, mla_kv_norm=m_mla_kv_norm, mla_w_ukv=m_mla_w_ukv,
             group_norm=m_group_norm, w_out=m_w_out, norm_post=m_norm_post)
    v = dict(norm_pre=v_norm_pre, w_in=v_w_in, attn_sinks=v_attn_sinks, conv_w=v_conv_w, conv_b=v_conv_b,
             mla_q_norm=v_mla_q_norm, mla_w_uq=v_mla_w_uq, mla_kv_norm=v_mla_kv_norm, mla_w_ukv=v_mla_w_ukv,
             group_norm=v_group_norm, w_out=v_w_out, norm_post=v_norm_post)
    T = x.shape[1]
    xs = x[0]
    pos = positions[0].reshape(T, 1)
    tgt = loss_target[0]
    core = lax.axis_index("c")

    gathered = _gather_weights([jnp.swapaxes(w["w_in"], 1, 2).astype(BF16)]
                               + [w[n].astype(BF16) for n in _BIG[1:4]] + [w["conv_w"]])
    full = {}
    for n, got in zip(_BIG, gathered):
        if n in ("w_in", "w_out"):
            full[n] = jnp.moveaxis(got, 0, 1).reshape(DEPTH, 4 * got.shape[2], got.shape[3])
        else:
            full[n] = jnp.transpose(got, (1, 2, 0, 3)).reshape(DEPTH, got.shape[2], 4 * got.shape[3])

    layers = []
    for l in range(DEPTH):
        layers.append(dict(
            norm_pre=norm_pre[l:l + 1], w_in=_w_in_internal(full["w_in"][l]), attn_sinks=attn_sinks[l],
            conv_w=full["conv_w"][l], conv_b=conv_b[l:l + 1], mla_q_norm=mla_q_norm[l:l + 1],
            mla_w_uq=_uq_internal(full["mla_w_uq"][l]), mla_kv_norm=mla_kv_norm[l:l + 1],
            mla_w_ukv=_ukv_internal(full["mla_w_ukv"][l]), group_norm=group_norm[l:l + 1],
            w_out=full["w_out"][l], norm_post=norm_post[l:l + 1]))

    saved = []
    h = xs
    for l in range(DEPTH):
        h, s = _layer_fwd(h, pos, layers[l], tgt if l == DEPTH - 1 else None)
        saved.append(s)
    dy, loss_part = h

    grads = [None] * DEPTH
    for l in reversed(range(DEPTH)):
        dy, grads[l] = _layer_bwd(dy, pos, layers[l], saved[l])

    turned = ("w_in", "mla_w_uq")
    turn = lambda n, a: jnp.swapaxes(a, -1, -2) if n in turned else a

    def chunks(n, a):
        if n in ("w_out", "w_in"):
            return a.reshape(4, a.shape[0] // 4, a.shape[1])
        if n in turned:
            return a.T.reshape(4, a.shape[1] // 4, a.shape[0])
        return jnp.transpose(a.reshape(a.shape[0], 4, a.shape[1] // 4), (1, 0, 2))

    grad = lambda l, n: grads[l]["w_in_t" if n == "w_in" else n]
    mine = [chunks(n, jnp.where(core == 0, grad(0, n), grad(1, n))) for n in _BIG]
    theirs = [chunks(n, jnp.where(core == 0, grad(1, n), grad(0, n))) for n in _BIG]
    from_sibling = _swap_cores(theirs, "swap_layer_chunks")
    summed = [_add(a, b, "add_cores_" + n) for n, a, b in zip(_BIG, mine, from_sibling)]
    small = _pack_small({n: jnp.stack([grads[l][n] for l in range(DEPTH)]) for n in _SMALL}, loss_part[0, 0])
    *got, got_small = _exchange_chips(summed, small)
    done = [_sum_leading(b, "sum_chips_" + n) for n, b in zip(_BIG, got)]
    done_other = _swap_cores(done, "swap_layer_shards")

    outs, loss = _adamw_small(w, m, v, got_small)
    for n, gm, go in zip(_BIG, done, done_other):
        for d, a in zip(outs, _adamw_layers(turn(n, w[n]), turn(n, m[n]), turn(n, v[n]), gm, go, "adamw_" + n)):
            d[n] = turn(n, a)
    return (loss[0, 0], dy[None], *[outs[0][n] for n in _WEIGHTS], *[outs[1][n] for n in _WEIGHTS],
            *[outs[2][n] for n in _WEIGHTS], *[outs[3][n] for n in _WEIGHTS])
```

```python
import math

import jax
import jax.numpy as jnp
from jax import lax
from jax.experimental import pallas as pl
from jax.experimental.pallas import tpu as pltpu

F32 = jnp.float32
BF16 = jnp.bfloat16
MESH = pl.DeviceIdType.MESH

D_MODEL = 1024
DEPTH = 2
EPS = 1e-6
BLOCK = 128
HEAD = 64
LANES = 128
GROUP = 256
LOG2E = 1.4426950408889634
LN2 = 0.6931471805599453
MLA_QSCALE = 96 ** -0.5 * LOG2E
ROPE_HALF = 16
ROPE_THETA = 10000.0
SWA_SUB = 2
ATT_BLK = 256
MLA_BQ = 512
NEG = -1e30
SB_DEAD = -104.0

ADAM_LR, ADAM_B1, ADAM_B2, ADAM_EPS, ADAM_WD, ADAM_STEP = 0.001, 0.9, 0.999, 1e-08, 0.01, 10

_REAL = [("a_q", 256), ("a_k", 128), ("a_v", 128), ("b_b", 256), ("b_c", 256), ("b_x", 256),
         ("c_q", 256), ("c_kv", 128), ("c_kr", 32), ("d_q", 256), ("d_k", 256), ("d_v", 256),
         ("gate", 1024)]
_REAL_OFF = {}
_o = 0
for _n, _w in _REAL:
    _REAL_OFF[_n] = (_o, _w)
    _o += _w
D_IN = _o
_INT_ORDER = ["a_q", "a_k", "a_v", "d_q", "d_k", "d_v", "gate", "b_b", "b_c", "b_x", "c_q", "c_kv", "c_kr"]
_INT_W = dict(_REAL)
_INT_W["c_kr"] = 128
_INT_OFF = {}
_o = 0
for _n in _INT_ORDER:
    _INT_OFF[_n] = _o
    _o += _INT_W[_n]
N_INT = _o
N_HB = _INT_OFF["gate"]
N_HF = N_INT - N_HB

VMEM_LIMIT = 56 * 1024 * 1024


def _cparams(sem):
    return pltpu.CompilerParams(dimension_semantics=sem, vmem_limit_bytes=VMEM_LIMIT)


def _dot(a, b):
    return jnp.dot(a, b, preferred_element_type=F32)


def _dot_nt(a, b):
    return lax.dot_general(a, b, (((1,), (1,)), ((), ())), preferred_element_type=F32)


def _dot_tn(a, b):
    return lax.dot_general(a, b, (((0,), (0,)), ((), ())), preferred_element_type=F32)


def _split(x):
    hi = x.astype(BF16)
    lo = (x - hi.astype(F32)).astype(BF16)
    return hi, lo


def _rms(x):
    return lax.rsqrt(jnp.mean(x * x, axis=-1, keepdims=True) + EPS)


def _rms_bwd(dy, xhat, r, g):
    dxhat = dy * g
    return r * (dxhat - xhat * jnp.mean(dxhat * xhat, axis=-1, keepdims=True)), dy * xhat


def _colsum(x):
    return jnp.sum(x, axis=0, keepdims=True)


def _inproj_fwd(x, g, wt):
    T = x.shape[0]
    tm = 512

    def body(x_ref, g_ref, w_ref, xn_ref, hb_ref, hf_ref):
        xv = x_ref[...]
        xn = (xv * _rms(xv) * g_ref[...]).astype(BF16)
        xn_ref[...] = xn
        h = _dot_nt(xn, w_ref[...])
        hb_ref[...] = h[:, :N_HB].astype(BF16)
        hf_ref[...] = h[:, N_HB:]

    return pl.pallas_call(
        body, name="inproj_fwd", grid=(T // tm,),
        in_specs=[pl.BlockSpec((tm, D_MODEL), lambda i: (i, 0)),
                  pl.BlockSpec((1, D_MODEL), lambda i: (0, 0)),
                  pl.BlockSpec((N_INT, D_MODEL), lambda i: (0, 0))],
        out_specs=[pl.BlockSpec((tm, D_MODEL), lambda i: (i, 0)),
                   pl.BlockSpec((tm, N_HB), lambda i: (i, 0)),
                   pl.BlockSpec((tm, N_HF), lambda i: (i, 0))],
        out_shape=[jax.ShapeDtypeStruct((T, D_MODEL), BF16),
                   jax.ShapeDtypeStruct((T, N_HB), BF16),
                   jax.ShapeDtypeStruct((T, N_HF), F32)],
        compiler_params=_cparams(("parallel",)),
    )(x, g, wt)


def _inproj_bwd_dx(x, g, wt, dx_next, pieces):
    T = x.shape[0]
    tm = 512
    widths = [p.shape[1] for p in pieces]
    assert sum(widths) == N_INT

    def body(x_ref, g_ref, w_ref, dxn_ref, *rest):
        p_refs = rest[:len(pieces)]
        dx_ref, dh_ref, dg_ref = rest[len(pieces):]
        dh = jnp.concatenate([p[...].astype(BF16) for p in p_refs], axis=1)
        dh_ref[...] = dh
        dxn = _dot(dh, w_ref[...])
        xv = x_ref[...]
        r = _rms(xv)
        dx, dgrow = _rms_bwd(dxn, xv * r, r, g_ref[...])
        dx_ref[...] = dx + dxn_ref[...]

        @pl.when(pl.program_id(0) == 0)
        def _():
            dg_ref[...] = jnp.zeros_like(dg_ref)

        dg_ref[...] += _colsum(dgrow)

    return pl.pallas_call(
        body, name="inproj_bwd_dx", grid=(T // tm,),
        in_specs=[pl.BlockSpec((tm, D_MODEL), lambda i: (i, 0)),
                  pl.BlockSpec((1, D_MODEL), lambda i: (0, 0)),
                  pl.BlockSpec((N_INT, D_MODEL), lambda i: (0, 0)),
                  pl.BlockSpec((tm, D_MODEL), lambda i: (i, 0))]
                 + [pl.BlockSpec((tm, wd), lambda i: (i, 0)) for wd in widths],
        out_specs=[pl.BlockSpec((tm, D_MODEL), lambda i: (i, 0)),
                   pl.BlockSpec((tm, N_INT), lambda i: (i, 0)),
                   pl.BlockSpec((1, D_MODEL), lambda i: (0, 0))],
        out_shape=[jax.ShapeDtypeStruct((T, D_MODEL), F32),
                   jax.ShapeDtypeStruct((T, N_INT), BF16),
                   jax.ShapeDtypeStruct((1, D_MODEL), F32)],
        compiler_params=_cparams(("arbitrary",)),
    )(x, g, wt, dx_next, *pieces)


def _grad_over_tokens(a, b, name):
    T, M = a.shape
    N = b.shape[1]
    tm, tn = min(1024, T), 512

    def body(a_ref, b_ref, o_ref):
        @pl.when(pl.program_id(1) == 0)
        def _():
            o_ref[...] = jnp.zeros_like(o_ref)

        o_ref[...] += _dot_tn(b_ref[...], a_ref[...])

    return pl.pallas_call(
        body, name=name, grid=(N // tn, T // tm),
        in_specs=[pl.BlockSpec((tm, M), lambda j, t: (t, 0)),
                  pl.BlockSpec((tm, tn), lambda j, t: (t, j))],
        out_specs=pl.BlockSpec((tn, M), lambda j, t: (j, 0)),
        out_shape=jax.ShapeDtypeStruct((N, M), F32),
        compiler_params=_cparams(("parallel", "arbitrary")),
    )(a, b)


def _roll_f32(x, shift):
    return pltpu.roll(x.astype(F32), shift, 1)


def _swa_operands(h, q, k_prev, k_cur, v_prev, v_cur):
    p, e = h // 2, h % 2
    lane = lax.broadcasted_iota(jnp.int32, (1, LANES), 1) // HEAD
    q = q[:, p * LANES:(p + 1) * LANES]
    if e != p:
        q = _roll_f32(q, HEAD).astype(BF16)
        v_prev = _roll_f32(v_prev, HEAD).astype(BF16)
        v_cur = _roll_f32(v_cur, HEAD).astype(BF16)
    qs = jnp.where(lane == p, q, 0) * 0.125
    return dict(p=p, e=e, lane=lane, qs=qs, k_prev=k_prev, k_cur=k_cur,
                v_prev=jnp.where(lane == e, v_prev, 0), v_cur=jnp.where(lane == e, v_cur, 0),
                s_prev=_dot_nt(qs, k_prev), s_cur=_dot_nt(qs, k_cur))


def _swa_probs(ops, sink, no_prev):
    row = lax.broadcasted_iota(jnp.int32, (BLOCK, BLOCK), 0)
    col = lax.broadcasted_iota(jnp.int32, (BLOCK, BLOCK), 1)
    ok_prev = col > row if no_prev is None else jnp.logical_and(col > row, jnp.logical_not(no_prev))
    s_prev = jnp.where(ok_prev, ops["s_prev"], NEG)
    s_cur = jnp.where(col <= row, ops["s_cur"], NEG)
    m = jnp.maximum(jnp.maximum(jnp.max(s_prev, axis=1, keepdims=True),
                                jnp.max(s_cur, axis=1, keepdims=True)), sink)
    p_prev = jnp.exp(s_prev - m)
    p_cur = jnp.exp(s_cur - m)
    p_sink = jnp.exp(sink - m)
    inv = 1.0 / (jnp.sum(p_prev, axis=1, keepdims=True) + jnp.sum(p_cur, axis=1, keepdims=True) + p_sink)
    return p_prev * inv, p_cur * inv, p_sink * inv


def _swa_specs(T):
    n = T // (BLOCK * SWA_SUB)
    qo, ko, vo = (_INT_OFF[name] // LANES for name in ("a_q", "a_k", "a_v"))
    halo = lambda i: jnp.maximum(i * SWA_SUB - 1, 0)
    return [pl.BlockSpec((BLOCK * SWA_SUB, 256), lambda i: (i, qo // 2)),
            pl.BlockSpec((BLOCK, LANES), lambda i: (halo(i), ko)),
            pl.BlockSpec((BLOCK * SWA_SUB, LANES), lambda i: (i, ko)),
            pl.BlockSpec((BLOCK, LANES), lambda i: (halo(i), vo)),
            pl.BlockSpec((BLOCK * SWA_SUB, LANES), lambda i: (i, vo)),
            pl.BlockSpec(memory_space=pltpu.SMEM)], n


def _swa_units(q_ref, kh_ref, kc_ref, vh_ref, vc_ref, s_ref):
    blk = lambda a: slice(a * BLOCK, (a + 1) * BLOCK)
    units = [(a, h) for a in range(SWA_SUB) for h in range(4)]
    ops = {}
    for a, h in units:
        k_prev, v_prev = (kh_ref[...], vh_ref[...]) if a == 0 else (kc_ref[blk(a - 1), :], vc_ref[blk(a - 1), :])
        ops[a, h] = _swa_operands(h, q_ref[blk(a), :], k_prev, kc_ref[blk(a), :], v_prev, vc_ref[blk(a), :])
    probs = {(a, h): _swa_probs(ops[a, h], s_ref[h], pl.program_id(0) == 0 if a == 0 else None) for a, h in units}
    return units, ops, probs, blk


def _swa_fwd(hb, sinks):
    T = hb.shape[0]
    specs, n = _swa_specs(T)

    def body(q_ref, kh_ref, kc_ref, vh_ref, vc_ref, s_ref, o_ref):
        units, ops, probs, blk = _swa_units(q_ref, kh_ref, kc_ref, vh_ref, vc_ref, s_ref)
        outs = {u: _dot(probs[u][0].astype(BF16), ops[u]["v_prev"]) + _dot(probs[u][1].astype(BF16), ops[u]["v_cur"])
                for u in units}
        for a in range(SWA_SUB):
            for p in range(2):
                o_ref[blk(a), p * LANES:(p + 1) * LANES] = outs[a, 2 * p] + outs[a, 2 * p + 1]

    return pl.pallas_call(
        body, name="swa_fwd", grid=(n,), in_specs=specs,
        out_specs=pl.BlockSpec((BLOCK * SWA_SUB, 256), lambda i: (i, 0)),
        out_shape=jax.ShapeDtypeStruct((T, 256), F32),
        compiler_params=_cparams(("parallel",)),
    )(hb, hb, hb, hb, hb, sinks)


def _swa_bwd(hb, sinks, dy):
    T = hb.shape[0]
    specs, n = _swa_specs(T)

    def body(q_ref, kh_ref, kc_ref, vh_ref, vc_ref, s_ref, dy_ref, dq_ref, dk_ref, dv_ref, ds_ref):
        i = pl.program_id(0)

        @pl.when(i == 0)
        def _():
            ds_ref[...] = jnp.zeros_like(ds_ref)

        lane_id = lax.broadcasted_iota(jnp.int32, (8, LANES), 1)
        units, ops, probs, blk = _swa_units(q_ref, kh_ref, kc_ref, vh_ref, vc_ref, s_ref)
        dos = {(a, h): jnp.where(ops[a, h]["lane"] == ops[a, h]["e"],
                                 dy_ref[blk(a), ops[a, h]["p"] * LANES:(ops[a, h]["p"] + 1) * LANES], 0.0)
               for a, h in units}
        dobs = {u: dos[u].astype(BF16) for u in units}
        pbs = {u: (probs[u][0].astype(BF16), probs[u][1].astype(BF16)) for u in units}
        outs = {u: _dot(pbs[u][0], ops[u]["v_prev"]) + _dot(pbs[u][1], ops[u]["v_cur"]) for u in units}
        dps = {u: (_dot_nt(dobs[u], ops[u]["v_prev"]), _dot_nt(dobs[u], ops[u]["v_cur"])) for u in units}
        dss, dsinks = {}, jnp.zeros((8, LANES), F32)
        for u in units:
            delta = jnp.sum(dos[u] * outs[u], axis=1, keepdims=True)
            dss[u] = ((probs[u][0] * (dps[u][0] - delta)).astype(BF16),
                      (probs[u][1] * (dps[u][1] - delta)).astype(BF16))
            dsink = -jnp.sum(probs[u][2] * delta, axis=0, keepdims=True)
            dsinks += jnp.where(lane_id == u[1], dsink, 0.0)
        ds_ref[...] += dsinks
        dqs = {u: (_dot(dss[u][0], ops[u]["k_prev"]) + _dot(dss[u][1], ops[u]["k_cur"])) * 0.125 for u in units}
        zero = jnp.zeros((BLOCK, LANES), F32)
        dk_as_prev, dk_as_cur = [zero] * SWA_SUB, [zero] * SWA_SUB
        dv_as_prev, dv_as_cur = [zero] * SWA_SUB, [zero] * SWA_SUB
        for a, h in units:
            p, e = ops[a, h]["p"], ops[a, h]["e"]
            dob_v = dobs[a, h] if e == p else pltpu.roll(dos[a, h], HEAD, 1).astype(BF16)
            dk_as_prev[a] = dk_as_prev[a] + _dot_tn(dss[a, h][0], ops[a, h]["qs"])
            dk_as_cur[a] = dk_as_cur[a] + _dot_tn(dss[a, h][1], ops[a, h]["qs"])
            dv_as_prev[a] = dv_as_prev[a] + _dot_tn(pbs[a, h][0], dob_v)
            dv_as_cur[a] = dv_as_cur[a] + _dot_tn(pbs[a, h][1], dob_v)
        base = i * SWA_SUB
        for a in range(SWA_SUB):
            rows = pl.ds(pl.multiple_of((base + a) * BLOCK, BLOCK), BLOCK)
            more = a + 1 < SWA_SUB
            dk_ref[rows, :] = dk_as_cur[a] + (dk_as_prev[a + 1] if more else 0.0)
            dv_ref[rows, :] = dv_as_cur[a] + (dv_as_prev[a + 1] if more else 0.0)
        halo = pl.ds(pl.multiple_of(jnp.maximum(base - 1, 0) * BLOCK, BLOCK), BLOCK)
        dk_ref[halo, :] += dk_as_prev[0]
        dv_ref[halo, :] += dv_as_prev[0]
        for a in range(SWA_SUB):
            for p in range(2):
                dq_pair = jnp.zeros((BLOCK, LANES), F32)
                for e in range(2):
                    dq = jnp.where(ops[a, 2 * p + e]["lane"] == p, dqs[a, 2 * p + e], 0.0)
                    dq_pair += dq if e == p else pltpu.roll(dq, HEAD, 1)
                dq_ref[blk(a), p * LANES:(p + 1) * LANES] = dq_pair.astype(BF16)

    return pl.pallas_call(
        body, name="swa_bwd", grid=(n,),
        in_specs=specs + [pl.BlockSpec((BLOCK * SWA_SUB, 256), lambda i: (i, 0))],
        out_specs=[pl.BlockSpec((BLOCK * SWA_SUB, 256), lambda i: (i, 0)),
                   pl.BlockSpec((T, LANES), lambda i: (0, 0)),
                   pl.BlockSpec((T, LANES), lambda i: (0, 0)),
                   pl.BlockSpec((8, LANES), lambda i: (0, 0))],
        out_shape=[jax.ShapeDtypeStruct((T, 256), BF16),
                   jax.ShapeDtypeStruct((T, LANES), F32),
                   jax.ShapeDtypeStruct((T, LANES), F32),
                   jax.ShapeDtypeStruct((8, LANES), F32)],
        compiler_params=_cparams(("arbitrary",)),
    )(hb, hb, hb, hb, hb, sinks, dy)


def _rope_tables(pos_ref):
    lane = lax.broadcasted_iota(jnp.int32, (1, LANES), 1)
    active = jnp.logical_and(lane >= HEAD, lane < HEAD + 2 * ROPE_HALF)
    idx = ((lane - HEAD) % ROPE_HALF).astype(F32)
    freq = jnp.exp(idx * (-math.log(ROPE_THETA) / ROPE_HALF))
    ang = pos_ref[...].astype(F32) * freq
    cos, sin = jnp.cos(ang), jnp.sin(ang)
    c = jnp.where(active, cos, 1.0)
    s_up = jnp.where(jnp.logical_and(active, lane >= HEAD + ROPE_HALF), sin, 0.0)
    s_dn = jnp.where(jnp.logical_and(active, lane < HEAD + ROPE_HALF), -sin, 0.0)
    return c, s_up, s_dn


def _rope(x, tabs):
    c, s_up, s_dn = tabs
    return x * c + pltpu.roll(x, ROPE_HALF, 1) * s_up + pltpu.roll(x, LANES - ROPE_HALF, 1) * s_dn


def _rope_t(dy, tabs):
    c, s_up, s_dn = tabs
    return dy * c + pltpu.roll(dy * s_up, LANES - ROPE_HALF, 1) + pltpu.roll(dy * s_dn, ROPE_HALF, 1)


def _mla_lat_specs(tm):
    cq, ckv, ckr = ((_INT_OFF[n] - N_HB) for n in ("c_q", "c_kv", "c_kr"))
    return [pl.BlockSpec((tm, 256), lambda i: (i, cq // 256)),
            pl.BlockSpec((tm, LANES), lambda i: (i, ckv // LANES)),
            pl.BlockSpec((tm, LANES), lambda i: (i, ckr // LANES)),
            pl.BlockSpec((tm, 1), lambda i: (i, 0)),
            pl.BlockSpec((1, 256), lambda i: (0, 0)),
            pl.BlockSpec((1, LANES), lambda i: (0, 0)),
            pl.BlockSpec((256, 512), lambda i: (0, 0)),
            pl.BlockSpec((LANES, 768), lambda i: (0, 0))]


def _mla_prep_fwd(hf, pos, g_q, g_kv, w_uq, w_ukv):
    T = hf.shape[0]
    tm = 512
    sub = tm // ATT_BLK

    def body(cq_ref, ckv_ref, ckr_ref, pos_ref, gq_ref, gkv_ref, wq_ref, wkv_ref, qm_ref, km_ref, vm_ref, vt_ref):
        tabs = _rope_tables(pos_ref)
        cq = cq_ref[...]
        q = _dot((cq * _rms(cq) * gq_ref[...]).astype(BF16), wq_ref[...])
        ckv = ckv_ref[...]
        kv = _dot((ckv * _rms(ckv) * gkv_ref[...]).astype(BF16), wkv_ref[...])
        kr = _rope(pltpu.roll(ckr_ref[...], HEAD, 1), tabs)
        for h in range(4):
            sl = slice(h * LANES, (h + 1) * LANES)
            qm_ref[:, sl] = (_rope(q[:, sl], tabs) * MLA_QSCALE).astype(BF16)
            km_ref[:, sl] = (kv[:, sl] + kr).astype(BF16)
        vm_ref[...] = kv[:, 512:].astype(BF16)
        for p in range(2):
            for s in range(sub):
                tile = kv[s * ATT_BLK:(s + 1) * ATT_BLK, 512 + p * LANES:512 + (p + 1) * LANES]
                vt_ref[p, s] = jnp.transpose(tile).astype(BF16)

    return pl.pallas_call(
        body, name="mla_prep_fwd", grid=(T // tm,), in_specs=_mla_lat_specs(tm),
        out_specs=[pl.BlockSpec((tm, 512), lambda i: (i, 0)),
                   pl.BlockSpec((tm, 512), lambda i: (i, 0)),
                   pl.BlockSpec((tm, 256), lambda i: (i, 0)),
                   pl.BlockSpec((2, sub, LANES, ATT_BLK), lambda i: (0, i, 0, 0))],
        out_shape=[jax.ShapeDtypeStruct((T, 512), BF16),
                   jax.ShapeDtypeStruct((T, 512), BF16),
                   jax.ShapeDtypeStruct((T, 256), BF16),
                   jax.ShapeDtypeStruct((2, T // ATT_BLK, LANES, ATT_BLK), BF16)],
        compiler_params=_cparams(("parallel",)),
    )(hf, hf, hf, pos, g_q, g_kv, w_uq, w_ukv)


def _mla_prep_bwd(hf, pos, g_q, g_kv, w_uq, w_ukv, dqm, dkt, dvt):
    T = hf.shape[0]
    tm = 512
    sub = tm // ATT_BLK

    def body(cq_ref, ckv_ref, ckr_ref, pos_ref, gq_ref, gkv_ref, wq_ref, wkv_ref, dq_ref, dk_ref, dv_ref,
             dc_ref, dwq_ref, dwkv_ref, dgq_ref, dgkv_ref):
        @pl.when(pl.program_id(0) == 0)
        def _():
            dwq_ref[...] = jnp.zeros_like(dwq_ref)
            dwkv_ref[...] = jnp.zeros_like(dwkv_ref)
            dgq_ref[...] = jnp.zeros_like(dgq_ref)
            dgkv_ref[...] = jnp.zeros_like(dgkv_ref)

        tabs = _rope_tables(pos_ref)
        lane = lax.broadcasted_iota(jnp.int32, (1, LANES), 1)
        dq = jnp.concatenate([_rope_t(dq_ref[:, h * LANES:(h + 1) * LANES] * MLA_QSCALE, tabs)
                              for h in range(4)], axis=1).astype(BF16)
        cq = cq_ref[...]
        rq = _rms(cq)
        cqn = (cq * rq * gq_ref[...]).astype(BF16)
        dwq_ref[...] += _dot_tn(cqn, dq)
        dcq, dgrow = _rms_bwd(_dot_nt(dq, wq_ref[...]), cq * rq, rq, gq_ref[...])
        dgq_ref[...] += _colsum(dgrow)
        dc_ref[:, 0:256] = dcq.astype(BF16)

        dk = jnp.concatenate([jnp.concatenate([jnp.transpose(dk_ref[p, s]) for p in range(2)], axis=1)
                              for s in range(sub)], axis=0) * LN2
        dv = jnp.concatenate([jnp.concatenate([jnp.transpose(dv_ref[p, s]) for p in range(2)], axis=1)
                              for s in range(sub)], axis=0)
        dkr = dk[:, 0:LANES] + dk[:, LANES:2 * LANES] + dk[:, 2 * LANES:3 * LANES] + dk[:, 3 * LANES:]
        dkr = pltpu.roll(_rope_t(dkr, tabs), HEAD, 1)
        dc_ref[:, 384:512] = jnp.where(lane < 2 * ROPE_HALF, dkr, 0.0).astype(BF16)
        dkv = jnp.concatenate([dk.astype(BF16), dv.astype(BF16)], axis=1)
        ckv = ckv_ref[...]
        rkv = _rms(ckv)
        ckvn = (ckv * rkv * gkv_ref[...]).astype(BF16)
        dwkv_ref[...] += _dot_tn(ckvn, dkv)
        dckv, dgrow = _rms_bwd(_dot_nt(dkv, wkv_ref[...]), ckv * rkv, rkv, gkv_ref[...])
        dgkv_ref[...] += _colsum(dgrow)
        dc_ref[:, 256:384] = dckv.astype(BF16)

    return pl.pallas_call(
        body, name="mla_prep_bwd", grid=(T // tm,),
        in_specs=_mla_lat_specs(tm) + [pl.BlockSpec((tm, 512), lambda i: (i, 0)),
                                       pl.BlockSpec((2, sub, 256, ATT_BLK), lambda i: (0, i, 0, 0)),
                                       pl.BlockSpec((2, sub, LANES, ATT_BLK), lambda i: (0, i, 0, 0))],
        out_specs=[pl.BlockSpec((tm, 512), lambda i: (i, 0)),
                   pl.BlockSpec((256, 512), lambda i: (0, 0)),
                   pl.BlockSpec((LANES, 768), lambda i: (0, 0)),
                   pl.BlockSpec((1, 256), lambda i: (0, 0)),
                   pl.BlockSpec((1, LANES), lambda i: (0, 0))],
        out_shape=[jax.ShapeDtypeStruct((T, 512), BF16),
                   jax.ShapeDtypeStruct((256, 512), F32),
                   jax.ShapeDtypeStruct((LANES, 768), F32),
                   jax.ShapeDtypeStruct((1, 256), F32),
                   jax.ShapeDtypeStruct((1, LANES), F32)],
        compiler_params=_cparams(("arbitrary",)),
    )(hf, hf, hf, pos, g_q, g_kv, w_uq, w_ukv, dqm, dkt, dvt)


def _causal_masks(bq, bk):
    row = lax.broadcasted_iota(jnp.int32, (bq, bk), 0)
    col = lax.broadcasted_iota(jnp.int32, (bq, bk), 1)
    return row, col


def _mla_fwd(qm, km, vt):
    T = qm.shape[0]
    bq, bk = min(MLA_BQ, T), ATT_BLK
    nq, nsub, nk = T // bq, bq // bk, T // bk

    def body(q_ref, k_ref, vt_ref, o_ref, lse_ref, acc_ref, m_ref, l_ref):
        qi = pl.program_id(0)
        key = lax.broadcasted_iota(jnp.int32, (bk, bq), 0)
        qry = lax.broadcasted_iota(jnp.int32, (bk, bq), 1)
        ones = jnp.ones((8, bk), BF16)
        acc_ref[...] = jnp.zeros_like(acc_ref)
        m_ref[...] = jnp.full_like(m_ref, NEG)
        l_ref[...] = jnp.zeros_like(l_ref)

        def step(kb0, masked):
            kbs = [kb0 + d for d in range(nsub)]
            sts = [[_dot_nt(k_ref[pl.ds(pl.multiple_of(kb * bk, bk), bk), e * LANES:(e + 1) * LANES],
                            q_ref[:, e * LANES:(e + 1) * LANES]) for kb in kbs] for e in range(4)]
            pts, alphas = [], []
            for e in range(4):
                st = [jnp.where(key + d * bk <= qry, sts[e][d], NEG) for d in range(nsub)] if masked else sts[e]
                m_prev = m_ref[e, 0:1, :]
                m_new = m_prev
                for d in range(nsub):
                    m_new = jnp.maximum(m_new, jnp.max(st[d], axis=0, keepdims=True))
                alpha = jnp.exp2(m_prev - m_new)
                pt = [jnp.exp2(st[d] - m_new).astype(BF16) for d in range(nsub)]
                l_new = alpha * l_ref[e]
                for d in range(nsub):
                    l_new = l_new + _dot(ones, pt[d])
                l_ref[e] = l_new
                m_ref[e] = jnp.broadcast_to(m_new, (8, bq))
                pts.append(pt)
                alphas.append(alpha)
            for e in range(4):
                acc = alphas[e] * acc_ref[e]
                for d in range(nsub):
                    v_t = vt_ref[e // 2, kbs[d], (e % 2) * HEAD:(e % 2 + 1) * HEAD, :]
                    acc = acc + _dot(v_t, pts[e][d])
                acc_ref[e] = acc

        step(qi * nsub, True)

        def loop(t, c):
            step(t * nsub, False)
            return c

        lax.fori_loop(0, qi, loop, 0)
        outs, lses = [], []
        for e in range(4):
            l = l_ref[e, 0:1, :]
            outs.append(acc_ref[e] / l)
            lses.append(jnp.broadcast_to(m_ref[e, 0:1, :] * LN2 + jnp.log(l), (HEAD, bq)))
        o_ref[...] = jnp.transpose(jnp.concatenate(outs, axis=0))
        lse_ref[...] = jnp.transpose(jnp.concatenate(lses, axis=0))

    return pl.pallas_call(
        body, name="mla_fwd", grid=(nq,),
        in_specs=[pl.BlockSpec((bq, 512), lambda i: (i, 0)),
                  pl.BlockSpec((T, 512), lambda i: (0, 0)),
                  pl.BlockSpec((2, nk, LANES, bk), lambda i: (0, 0, 0, 0))],
        out_specs=[pl.BlockSpec((bq, 256), lambda i: (i, 0)),
                   pl.BlockSpec((bq, 256), lambda i: (i, 0))],
        out_shape=[jax.ShapeDtypeStruct((T, 256), F32), jax.ShapeDtypeStruct((T, 256), F32)],
        scratch_shapes=[pltpu.VMEM((4, HEAD, bq), F32), pltpu.VMEM((4, 8, bq), F32), pltpu.VMEM((4, 8, bq), F32)],
        compiler_params=_cparams(("arbitrary",)),
    )(qm, km, vt)


def _mla_bwd(qm, km, vm, y, lse, dy):
    T = qm.shape[0]
    bq, bk = min(MLA_BQ, T), ATT_BLK
    nq, nsub, nk = T // bq, bq // bk, T // bk

    def body(q_ref, k_ref, v_ref, y_ref, lse_ref, dy_ref, dq_ref, dkt_ref, dvt_ref, dob_ref, st_ref, qt_ref, dot_ref):
        qi = pl.program_id(1)

        @pl.when(qi == 0)
        def _():
            dkt_ref[...] = jnp.zeros_like(dkt_ref)
            dvt_ref[...] = jnp.zeros_like(dvt_ref)

        lane = lax.broadcasted_iota(jnp.int32, (1, LANES), 1) // HEAD
        row, col = _causal_masks(bq, bk)
        dq_ref[...] = jnp.zeros_like(dq_ref)
        lse = lse_ref[...]
        lse_other = pltpu.roll(lse, HEAD, 1)
        qt_ref[...] = jnp.transpose(q_ref[...].astype(F32)).astype(BF16)
        dot_ref[...] = jnp.transpose(dy_ref[...]).astype(BF16)
        for e in range(2):
            do = jnp.where(lane == e, dy_ref[...], 0.0)
            dob_ref[e] = do.astype(BF16)
            st_ref[2 * e] = jnp.where(lane == e, lse, lse_other) * LOG2E
            st_ref[2 * e + 1] = jnp.broadcast_to(jnp.sum(do * y_ref[...], axis=1, keepdims=True), (bq, LANES))

        hss = [slice(e * LANES, (e + 1) * LANES) for e in range(2)]
        tile = lambda a: jnp.concatenate([a] * (bk // LANES), axis=1)

        def step(kb0, masked):
            kbs = [kb0 + d for d in range(nsub)]
            rows = [pl.ds(pl.multiple_of(kb * bk, bk), bk) for kb in kbs]
            pairs = [(d, e) for d in range(nsub) for e in range(2)]
            ss = {(d, e): _dot_nt(q_ref[:, hss[e]], k_ref[rows[d], hss[e]]) for d, e in pairs}
            dps = {(d, e): _dot_nt(dob_ref[e], jnp.where(lane == e, v_ref[rows[d], :], 0)) for d, e in pairs}
            ps, dss = {}, {}
            for d, e in pairs:
                s = jnp.where(col + d * bk <= row, ss[d, e], NEG) if masked else ss[d, e]
                p = jnp.exp2(s - tile(st_ref[2 * e]))
                dss[d, e] = (p * (dps[d, e] - tile(st_ref[2 * e + 1]))).astype(BF16)
                ps[d, e] = p.astype(BF16)
            for d, e in pairs:
                dvt_ref[0, kbs[d], e * HEAD:(e + 1) * HEAD, :] += _dot(dot_ref[e * HEAD:(e + 1) * HEAD, :], ps[d, e])
            for d, e in pairs:
                dkt_ref[0, kbs[d], hss[e], :] += _dot(qt_ref[hss[e], :], dss[d, e])
            for e in range(2):
                dq = dq_ref[:, hss[e]]
                for d in range(nsub):
                    dq = dq + _dot(dss[d, e], k_ref[rows[d], hss[e]])
                dq_ref[:, hss[e]] = dq

        step(qi * nsub, True)

        def loop(t, c):
            step(t * nsub, False)
            return c

        lax.fori_loop(0, qi, loop, 0)
        dq_ref[...] *= LN2

    return pl.pallas_call(
        body, name="mla_bwd", grid=(2, nq),
        in_specs=[pl.BlockSpec((bq, 256), lambda j, i: (i, j)),
                  pl.BlockSpec((T, 256), lambda j, i: (0, j)),
                  pl.BlockSpec((T, LANES), lambda j, i: (0, j)),
                  pl.BlockSpec((bq, LANES), lambda j, i: (i, j)),
                  pl.BlockSpec((bq, LANES), lambda j, i: (i, j)),
                  pl.BlockSpec((bq, LANES), lambda j, i: (i, j))],
        out_specs=[pl.BlockSpec((bq, 256), lambda j, i: (i, j)),
                   pl.BlockSpec((1, nk, 256, bk), lambda j, i: (j, 0, 0, 0)),
                   pl.BlockSpec((1, nk, LANES, bk), lambda j, i: (j, 0, 0, 0))],
        out_shape=[jax.ShapeDtypeStruct((T, 512), F32),
                   jax.ShapeDtypeStruct((2, nk, 256, bk), F32),
                   jax.ShapeDtypeStruct((2, nk, LANES, bk), F32)],
        scratch_shapes=[pltpu.VMEM((2, bq, LANES), BF16), pltpu.VMEM((4, bq, LANES), F32),
                        pltpu.VMEM((256, bq), BF16), pltpu.VMEM((LANES, bq), BF16)],
        compiler_params=_cparams(("parallel", "arbitrary")),
    )(qm, km, vm, y, lse, dy)


def _suffix_ones(n):
    r = lax.broadcasted_iota(jnp.int32, (n, n), 0)
    c = lax.broadcasted_iota(jnp.int32, (n, n), 1)
    return (r >= c).astype(BF16)


def _prefix_ones(n):
    r = lax.broadcasted_iota(jnp.int32, (n, n), 0)
    c = lax.broadcasted_iota(jnp.int32, (n, n), 1)
    return (r <= c).astype(BF16)


def _sb_specs(T, bq):
    qo, ko, vo = (_INT_OFF[n] // LANES for n in ("d_q", "d_k", "d_v"))
    return [pl.BlockSpec((bq, LANES), lambda j, i: (i, qo + j)),
            pl.BlockSpec((T, LANES), lambda j, i: (0, ko + j)),
            pl.BlockSpec((T, LANES), lambda j, i: (0, vo + j))]


def _sb_fwd(hb):
    T = hb.shape[0]
    bq = bk = ATT_BLK
    nq = T // bq

    def body(q_ref, k_ref, v_ref, o_ref, tot_ref, cnt_ref, qm_ref, car_ref):
        qi = pl.program_id(0)
        lane = lax.broadcasted_iota(jnp.int32, (1, LANES), 1) // HEAD
        row, col = _causal_masks(bq, bk)
        strict = col < row
        u = _suffix_ones(bk)
        o_ref[...] = jnp.zeros_like(o_ref)
        car_ref[...] = jnp.zeros_like(car_ref)
        pair = lambda h: slice((h // 2) * LANES, (h // 2 + 1) * LANES)
        for h in range(4):
            qm_ref[h] = jnp.where(lane == h % 2, q_ref[:, pair(h)], 0) * 0.125

        def step(blocks):
            tile = lambda a: jnp.concatenate([a] * (bk // LANES), axis=1)
            rows = [pl.ds(pl.multiple_of(kb * bk, bk), bk) for kb, _ in blocks]
            pairs = [(b, h) for b in range(len(blocks)) for h in range(4)]
            zs = {(b, h): _dot_nt(qm_ref[h], k_ref[rows[b], pair(h)]) for b, h in pairs}
            splits = {}
            for b, h in pairs:
                z = zs[b, h]
                lk = jnp.minimum(-z, 0.0) - jnp.log(1.0 + jnp.exp(-jnp.abs(z)))
                if blocks[b][1] is not None:
                    lk = jnp.where(blocks[b][1], lk, 0.0)
                splits[b, h] = _split(lk)
            sufs = {bh: _dot(hi, u) + _dot(lo, u) for bh, (hi, lo) in splits.items()}
            car = [car_ref[h] for h in range(4)]
            aas = {}
            for b, h in pairs:
                a = jnp.exp(zs[b, h] + sufs[b, h] + tile(car[h]))
                if blocks[b][1] is not None:
                    a = jnp.where(blocks[b][1], a, 0.0)
                aas[b, h] = a.astype(BF16)
                car[h] = car[h] + jnp.broadcast_to(sufs[b, h][:, 0:1], (bq, LANES))
            acc = [o_ref[:, pair(0)], o_ref[:, pair(2)]]
            for b, h in pairs:
                acc[h // 2] = acc[h // 2] + _dot(aas[b, h], jnp.where(lane == h % 2, v_ref[rows[b], pair(h)], 0))
            o_ref[:, pair(0)], o_ref[:, pair(2)] = acc
            for h in range(4):
                car_ref[h] = car[h]

        step([(qi, strict), (jnp.maximum(qi - 1, 0), qi > 0)])

        def live():
            worst = jnp.maximum(jnp.maximum(car_ref[0], car_ref[1]), jnp.maximum(car_ref[2], car_ref[3]))
            return jnp.max(worst) >= SB_DEAD

        def cond(c):
            return jnp.logical_and(c[0] < qi, c[1])

        def loop(c):
            step([(qi - 1 - c[0], None)])
            return c[0] + 1, live()

        done, _ = lax.while_loop(cond, loop, (jnp.minimum(qi, 1), live()))
        tot_ref[:, pair(0)] = jnp.where(lane == 0, car_ref[0], car_ref[1])
        tot_ref[:, pair(2)] = jnp.where(lane == 0, car_ref[2], car_ref[3])
        cnt_ref[0, qi] = done.astype(F32)

    col_block = lambda name: _INT_OFF[name] // 256
    return pl.pallas_call(
        body, name="sb_fwd", grid=(nq,),
        in_specs=[pl.BlockSpec((bq, 256), lambda i: (i, col_block("d_q"))),
                  pl.BlockSpec((T, 256), lambda i: (0, col_block("d_k"))),
                  pl.BlockSpec((T, 256), lambda i: (0, col_block("d_v")))],
        out_specs=[pl.BlockSpec((bq, 256), lambda i: (i, 0)), pl.BlockSpec((bq, 256), lambda i: (i, 0)),
                   pl.BlockSpec(memory_space=pltpu.SMEM)],
        out_shape=[jax.ShapeDtypeStruct((T, 256), F32), jax.ShapeDtypeStruct((T, 256), F32),
                   jax.ShapeDtypeStruct((1, nq), F32)],
        scratch_shapes=[pltpu.VMEM((4, bq, LANES), BF16), pltpu.VMEM((4, bq, LANES), F32)],
        compiler_params=_cparams(("arbitrary",)),
    )(hb, hb, hb)


def _sb_bwd(hb, tot, cnt, dy):
    T = hb.shape[0]
    bq = bk = ATT_BLK
    nq = T // bq

    def body(q_ref, k_ref, v_ref, tot_ref, dy_ref, cnt_ref, dq_ref, dk_ref, dv_ref, qm_ref, dob_ref, dqa_ref, rem_ref,
             cg_ref):
        qi = pl.program_id(1)

        @pl.when(qi == 0)
        def _():
            dk_ref[...] = jnp.zeros_like(dk_ref)
            dv_ref[...] = jnp.zeros_like(dv_ref)

        lane = lax.broadcasted_iota(jnp.int32, (1, LANES), 1) // HEAD
        row, col = _causal_masks(bq, bk)
        strict = col < row
        u = _prefix_ones(bk)
        tot = tot_ref[...]
        tot_other = pltpu.roll(tot, HEAD, 1)
        dqa_ref[...] = jnp.zeros_like(dqa_ref)
        cg_ref[...] = jnp.zeros_like(cg_ref)
        for e in range(2):
            qm_ref[e] = jnp.where(lane == e, q_ref[...], 0) * 0.125
            dob_ref[e] = jnp.where(lane == e, dy_ref[...], 0.0).astype(BF16)
            rem_ref[e] = jnp.where(lane == e, tot, tot_other)

        def step(blocks):
            tile = lambda a: jnp.concatenate([a] * (bk // LANES), axis=1)
            nb = len(blocks)
            rows = [pl.ds(pl.multiple_of(kb * bk, bk), bk) for kb, _ in blocks]
            pairs = [(b, e) for b in range(nb) for e in range(2)]
            mask = lambda b, x: x if blocks[b][1] is None else jnp.where(blocks[b][1], x, 0.0)
            zs = {(b, e): _dot_nt(qm_ref[e], k_ref[rows[b], :]) for b, e in pairs}
            das = {(b, e): _dot_nt(dob_ref[e], jnp.where(lane == e, v_ref[rows[b], :], 0)) for b, e in pairs}
            zls, splits = {}, {}
            for b, e in pairs:
                z = zs[b, e]
                lk = mask(b, jnp.minimum(-z, 0.0) - jnp.log(1.0 + jnp.exp(-jnp.abs(z))))
                zls[b, e] = z + lk
                splits[b, e] = _split(lk)
            pres = {be: _dot(hi, u) + _dot(lo, u) for be, (hi, lo) in splits.items()}
            rem = [rem_ref[0], rem_ref[1]]
            aas, gs, gsplits = {}, {}, {}
            for b, e in pairs:
                a = mask(b, jnp.exp(zls[b, e] + (tile(rem[e]) - pres[b, e])))
                gs[b, e] = a * das[b, e]
                aas[b, e] = a.astype(BF16)
                gsplits[b, e] = _split(gs[b, e])
                rem[e] = rem[e] - jnp.broadcast_to(pres[b, e][:, bk - 1:bk], (bq, LANES))
            for b in range(nb):
                dv_ref[rows[b], :] += _dot_tn(aas[b, 0], dob_ref[0]) + _dot_tn(aas[b, 1], dob_ref[1])
            gpres = {be: _dot(hi, u) + _dot(lo, u) for be, (hi, lo) in gsplits.items()}
            cg = [cg_ref[0], cg_ref[1]]
            dzs = {}
            for b, e in pairs:
                dz = mask(b, gs[b, e] - jnp.exp(zls[b, e]) * (tile(cg[e]) + gpres[b, e]))
                dzs[b, e] = dz.astype(BF16)
                cg[e] = cg[e] + jnp.broadcast_to(gpres[b, e][:, bk - 1:bk], (bq, LANES))
            for b in range(nb):
                dk_ref[rows[b], :] += _dot_tn(dzs[b, 0], qm_ref[0]) + _dot_tn(dzs[b, 1], qm_ref[1])
            for e in range(2):
                dq = dqa_ref[e]
                for b in range(nb):
                    dq = dq + _dot(dzs[b, e], k_ref[rows[b], :])
                dqa_ref[e] = dq
            rem_ref[0], rem_ref[1] = rem
            cg_ref[0], cg_ref[1] = cg

        def loop(kb, c):
            step([(kb, None)])
            return c

        start = qi - jnp.clip(cnt_ref[0, qi].astype(jnp.int32), 0, qi)
        lax.fori_loop(start, qi - 1, loop, 0)
        step([(jnp.maximum(qi - 1, 0), qi > 0), (qi, strict)])
        dq_ref[...] = (jnp.where(lane == 0, dqa_ref[0], dqa_ref[1]) * 0.125).astype(BF16)

    return pl.pallas_call(
        body, name="sb_bwd", grid=(2, nq),
        in_specs=_sb_specs(T, bq) + [pl.BlockSpec((bq, LANES), lambda j, i: (i, j)),
                                     pl.BlockSpec((bq, LANES), lambda j, i: (i, j)),
                                     pl.BlockSpec(memory_space=pltpu.SMEM)],
        out_specs=[pl.BlockSpec((bq, LANES), lambda j, i: (i, j)),
                   pl.BlockSpec((T, LANES), lambda j, i: (0, j)),
                   pl.BlockSpec((T, LANES), lambda j, i: (0, j))],
        out_shape=[jax.ShapeDtypeStruct((T, 256), BF16)] + [jax.ShapeDtypeStruct((T, 256), F32)] * 2,
        scratch_shapes=[pltpu.VMEM((2, bq, LANES), BF16), pltpu.VMEM((2, bq, LANES), BF16),
                        pltpu.VMEM((2, bq, LANES), F32), pltpu.VMEM((2, bq, LANES), F32),
                        pltpu.VMEM((2, bq, LANES), F32)],
        compiler_params=_cparams(("parallel", "arbitrary")),
    )(hb, hb, hb, tot, dy, cnt)


EP_TM = 512


def _ep_in_specs(tm, rev):
    idx = (lambda i: rev - i) if rev is not None else (lambda i: i)
    bo = (_INT_OFF["b_b"] - N_HB) // 256
    halo = lambda i: jnp.maximum(idx(i) * (tm // 8) - 1, 0)
    return [pl.BlockSpec((tm, 256), lambda i: (idx(i), 0)),
            pl.BlockSpec((tm, 256), lambda i: (idx(i), 0)),
            pl.BlockSpec((tm, 256), lambda i: (idx(i), 0)),
            pl.BlockSpec((tm, D_MODEL), lambda i: (idx(i), 0)),
            pl.BlockSpec((tm, 256), lambda i: (idx(i), bo)),
            pl.BlockSpec((tm, 256), lambda i: (idx(i), bo + 1)),
            pl.BlockSpec((tm, 256), lambda i: (idx(i), bo + 2)),
            pl.BlockSpec((8, 256), lambda i: (halo(i), bo + 1)),
            pl.BlockSpec((8, 256), lambda i: (halo(i), bo + 2)),
            pl.BlockSpec((3, 256), lambda i: (0, 0)),
            pl.BlockSpec((1, 256), lambda i: (0, 0)),
            pl.BlockSpec((1, D_MODEL), lambda i: (0, 0)),
            pl.BlockSpec((D_MODEL, D_MODEL), lambda i: (0, 0)),
            pl.BlockSpec((1, D_MODEL), lambda i: (0, 0))]


def _ep_mix(first, ya_ref, yc_ref, yd_ref, gate_ref, bb_ref, bc_ref, bx_ref, hc_ref, hx_ref, cw_ref, cb_ref, gg_ref):
    tm = ya_ref.shape[0]
    u = bc_ref[...] * bx_ref[...]
    halo = jnp.where(first, 0.0, hc_ref[...] * hx_ref[...])
    row = lax.broadcasted_iota(jnp.int32, (tm, 1), 0)
    u1 = jnp.where(row == 0, halo[7:8, :], pltpu.roll(u, 1, 0))
    u2 = jnp.where(row == 0, halo[6:7, :], jnp.where(row == 1, halo[7:8, :], pltpu.roll(u, 2, 0)))
    cw = cw_ref[...]
    conv = cw[0:1, :] * u2 + cw[1:2, :] * u1 + cw[2:3, :] * u + cb_ref[...]
    bb = bb_ref[...]
    ys = [ya_ref[...], bb * conv, yc_ref[...], yd_ref[...]]
    rs = [_rms(y) for y in ys]
    gg = gg_ref[...]
    yhat = jnp.concatenate([y * r for y, r in zip(ys, rs)], axis=1)
    gate = gate_ref[...]
    sig = 1.0 / (1.0 + jnp.exp(-gate))
    return u, u1, u2, conv, bb, rs, yhat, yhat * gg, gate, sig


def _epilogue_fwd(x, ya, yc, yd, hf, conv_w, conv_b, g_grp, w_out, g_post, tgt=None):
    T = x.shape[0]
    tm = EP_TM
    row_spec = pl.BlockSpec((tm, D_MODEL), lambda i: (i, 0))

    def layer_out(refs):
        (x_ref, ya_ref, yc_ref, yd_ref, gate_ref, bb_ref, bc_ref, bx_ref, hc_ref, hx_ref, cw_ref, cb_ref,
         gg_ref, wo_ref, gp_ref) = refs
        (_, _, _, _, _, _, _, yn, gate, sig) = _ep_mix(
            pl.program_id(0) == 0, ya_ref, yc_ref, yd_ref, gate_ref, bb_ref, bc_ref, bx_ref, hc_ref, hx_ref,
            cw_ref, cb_ref, gg_ref)
        z = _dot((yn * (gate * sig)).astype(BF16), wo_ref[...])
        return x_ref[...] + z * _rms(z) * gp_ref[...]

    args = (x, ya, yc, yd, hf, hf, hf, hf, hf, hf, conv_w, conv_b, g_grp, w_out, g_post)
    in_specs = [row_spec] + _ep_in_specs(tm, None)
    if tgt is None:
        def body(*refs):
            refs[-1][...] = layer_out(refs[:-1])

        return pl.pallas_call(
            body, name="epilogue_fwd", grid=(T // tm,), in_specs=in_specs, out_specs=row_spec,
            out_shape=jax.ShapeDtypeStruct((T, D_MODEL), F32), compiler_params=_cparams(("parallel",)),
        )(*args)

    def body_loss(*refs):
        t_ref, dy_ref, l_ref = refs[-3:]

        @pl.when(pl.program_id(0) == 0)
        def _():
            l_ref[...] = jnp.zeros_like(l_ref)

        d = layer_out(refs[:-3]) - t_ref[...]
        dy_ref[...] = d * (1.0 / D_MODEL)
        part = jnp.sum(jnp.sum(d * d, axis=1, keepdims=True), axis=0, keepdims=True)
        l_ref[...] += part * (0.5 / D_MODEL)

    return pl.pallas_call(
        body_loss, name="epilogue_fwd_loss", grid=(T // tm,), in_specs=in_specs + [row_spec],
        out_specs=[row_spec, pl.BlockSpec((8, LANES), lambda i: (0, 0))],
        out_shape=[jax.ShapeDtypeStruct((T, D_MODEL), F32), jax.ShapeDtypeStruct((8, LANES), F32)],
        compiler_params=_cparams(("arbitrary",)),
    )(*args, tgt)


def _epilogue_bwd(dxn, ya, yc, yd, hf, conv_w, conv_b, g_grp, w_out, g_post):
    T = dxn.shape[0]
    tm = EP_TM
    nt = T // tm
    ridx = lambda i: (nt - 1 - i, 0)

    def body(dx_ref, ya_ref, yc_ref, yd_ref, gate_ref, bb_ref, bc_ref, bx_ref, hc_ref, hx_ref, cw_ref, cb_ref,
             gg_ref, wo_ref, gp_ref,
             dya_ref, dyc_ref, dyd_ref, dhf_ref, dwo_ref, dgp_ref, dgg_ref, dcw_ref, dcb_ref, carry_ref):
        i = pl.program_id(0)

        @pl.when(i == 0)
        def _():
            for r in (dwo_ref, dgp_ref, dgg_ref, dcw_ref, dcb_ref, carry_ref):
                r[...] = jnp.zeros_like(r)

        (u, u1, u2, conv, bb, rs, yhat, yn, gate, sig) = _ep_mix(
            i == nt - 1, ya_ref, yc_ref, yd_ref, gate_ref, bb_ref, bc_ref, bx_ref, hc_ref, hx_ref,
            cw_ref, cb_ref, gg_ref)
        silu = gate * sig
        ymix = (yn * silu).astype(BF16)
        z = _dot(ymix, wo_ref[...])
        rz = _rms(z)
        dz, dgrow = _rms_bwd(dx_ref[...], z * rz, rz, gp_ref[...])
        dgp_ref[...] += _colsum(dgrow)
        dzb = dz.astype(BF16)
        dwo_ref[...] += _dot_tn(ymix, dzb)
        dymix = _dot_nt(dzb, wo_ref[...])
        dhf_ref[:, 0:D_MODEL] = (dymix * yn * (sig * (1.0 + gate * (1.0 - sig)))).astype(BF16)
        dyn = dymix * silu
        dgg_ref[...] += _colsum(dyn * yhat)
        gg = gg_ref[...]
        dys = []
        for gi in range(4):
            sl = slice(gi * GROUP, (gi + 1) * GROUP)
            dyh = dyn[:, sl] * gg[:, sl]
            yh = yhat[:, sl]
            dys.append(rs[gi] * (dyh - yh * jnp.mean(dyh * yh, axis=-1, keepdims=True)))
        dya_ref[...] = dys[0]
        dyc_ref[...] = dys[2]
        dyd_ref[...] = dys[3]
        dyb = dys[1]
        dhf_ref[:, D_MODEL:D_MODEL + 256] = (dyb * conv).astype(BF16)
        dconv = dyb * bb
        dcb_ref[...] += _colsum(dconv)
        dcw_ref[0:1, :] += _colsum(dconv * u2)
        dcw_ref[1:2, :] += _colsum(dconv * u1)
        dcw_ref[2:3, :] += _colsum(dconv * u)
        carry = carry_ref[...]
        row = lax.broadcasted_iota(jnp.int32, (tm, 1), 0)
        d1 = jnp.where(row == tm - 1, carry[0:1, :], pltpu.roll(dconv, tm - 1, 0))
        d2 = jnp.where(row == tm - 2, carry[0:1, :],
                       jnp.where(row == tm - 1, carry[1:2, :], pltpu.roll(dconv, tm - 2, 0)))
        cw = cw_ref[...]
        du = cw[2:3, :] * dconv + cw[1:2, :] * d1 + cw[0:1, :] * d2
        dhf_ref[:, D_MODEL + 256:D_MODEL + 512] = (du * bx_ref[...]).astype(BF16)
        dhf_ref[:, D_MODEL + 512:D_MODEL + 768] = (du * bc_ref[...]).astype(BF16)
        carry_ref[...] = dconv[0:8, :]

    in_specs = [pl.BlockSpec((tm, D_MODEL), ridx)] + _ep_in_specs(tm, nt - 1)
    return pl.pallas_call(
        body, name="epilogue_bwd", grid=(nt,), in_specs=in_specs,
        out_specs=[pl.BlockSpec((tm, 256), ridx), pl.BlockSpec((tm, 256), ridx), pl.BlockSpec((tm, 256), ridx),
                   pl.BlockSpec((tm, D_MODEL + 768), ridx),
                   pl.BlockSpec((D_MODEL, D_MODEL), lambda i: (0, 0)),
                   pl.BlockSpec((1, D_MODEL), lambda i: (0, 0)),
                   pl.BlockSpec((1, D_MODEL), lambda i: (0, 0)),
                   pl.BlockSpec((8, 256), lambda i: (0, 0)),
                   pl.BlockSpec((1, 256), lambda i: (0, 0))],
        out_shape=[jax.ShapeDtypeStruct((T, 256), F32)] * 3
                  + [jax.ShapeDtypeStruct((T, D_MODEL + 768), BF16),
                     jax.ShapeDtypeStruct((D_MODEL, D_MODEL), F32),
                     jax.ShapeDtypeStruct((1, D_MODEL), F32),
                     jax.ShapeDtypeStruct((1, D_MODEL), F32),
                     jax.ShapeDtypeStruct((8, 256), F32),
                     jax.ShapeDtypeStruct((1, 256), F32)],
        scratch_shapes=[pltpu.VMEM((8, 256), F32)],
        compiler_params=_cparams(("arbitrary",)),
    )(dxn, ya, yc, yd, hf, hf, hf, hf, hf, hf, conv_w, conv_b, g_grp, w_out, g_post)


def _place():
    return lax.axis_index("x"), lax.axis_index("y"), lax.axis_index("c")


def _other_chips(x, y):
    return [(1 - x, y), (x, 1 - y), (1 - x, 1 - y)]


HBM = pl.BlockSpec(memory_space=pl.ANY)


def _gather_weights(shards):
    n = len(shards)

    def body(*refs):
        ins, outs = refs[:n], refs[n:2 * n]
        ici_send, ici_recv, d2d_send, d2d_recv, local_sems = refs[2 * n:]
        x, y, c = _place()
        me = 2 * x + y
        chips = _other_chips(x, y)

        def ici(a, j, layer_from):
            px, py = chips[j]
            return pltpu.make_async_remote_copy(
                src_ref=ins[a].at[c], dst_ref=outs[a].at[layer_from, c], send_sem=ici_send.at[3 * a + j],
                recv_sem=ici_recv.at[3 * a + j], device_id=(px, py, c), device_id_type=MESH)

        def d2d(a, j, layer):
            px, py = chips[j]
            blk = outs[a].at[2 * px + py, layer]
            return pltpu.make_async_remote_copy(
                src_ref=blk, dst_ref=blk, send_sem=d2d_send.at[3 * a + j], recv_sem=d2d_recv.at[3 * a + j],
                device_id=(x, y, 1 - c), device_id_type=MESH)

        local = [pltpu.make_async_copy(ins[a], outs[a].at[me], local_sems.at[a]) for a in range(n)]
        for cp in local:
            cp.start()
        sends = [ici(a, j, me) for j in range(3) for a in range(n)]
        for cp in sends:
            cp.start()
        for j in range(3):
            px, py = chips[j]
            for a in range(n):
                ici(a, j, 2 * px + py).wait_recv()
                fwd = d2d(a, j, c)
                fwd.start()
                sends.append(fwd)
        for j in range(3):
            for a in range(n):
                d2d(a, j, 1 - c).wait_recv()
        for cp in sends:
            cp.wait_send()
        for cp in local:
            cp.wait()

    return pl.pallas_call(
        body, name="gather_weights",
        in_specs=[HBM] * n, out_specs=[HBM] * n,
        out_shape=[jax.ShapeDtypeStruct((4,) + s.shape, s.dtype) for s in shards],
        scratch_shapes=[pltpu.SemaphoreType.DMA((3 * n,))] * 4 + [pltpu.SemaphoreType.DMA((n,))],
    )(*shards)


def _exchange_chips(parts, small):
    n = len(parts)

    def body(*refs):
        ins, sm_ref = refs[:n], refs[n]
        outs, osm_ref = refs[n + 1:2 * n + 1], refs[2 * n + 1]
        send_sems, recv_sems, ssend_sems, srecv_sems, local_sems = refs[2 * n + 2:]
        x, y, c = _place()
        me = 2 * x + y
        dev = 4 * x + 2 * y + c
        local = [pltpu.make_async_copy(ins[a].at[me], outs[a].at[me], local_sems.at[a]) for a in range(n)]
        local.append(pltpu.make_async_copy(sm_ref, osm_ref.at[dev], local_sems.at[n]))
        for cp in local:
            cp.start()
        sends = []
        for j, (px, py) in enumerate(_other_chips(x, y)):
            for a in range(n):
                cp = pltpu.make_async_remote_copy(
                    src_ref=ins[a].at[2 * px + py], dst_ref=outs[a].at[me], send_sem=send_sems.at[3 * a + j],
                    recv_sem=recv_sems.at[3 * a + j], device_id=(px, py, c), device_id_type=MESH)
                cp.start()
                sends.append(cp)
        flips = [(fx, fy, fc) for fx in (0, 1) for fy in (0, 1) for fc in (0, 1)][1:]
        for j, (fx, fy, fc) in enumerate(flips):
            cp = pltpu.make_async_remote_copy(
                src_ref=sm_ref, dst_ref=osm_ref.at[dev], send_sem=ssend_sems.at[j], recv_sem=srecv_sems.at[j],
                device_id=(x ^ fx, y ^ fy, c ^ fc), device_id_type=MESH)
            cp.start()
            sends.append(cp)
        for j, (px, py) in enumerate(_other_chips(x, y)):
            for a in range(n):
                pltpu.make_async_remote_copy(
                    src_ref=ins[a].at[me], dst_ref=outs[a].at[2 * px + py], send_sem=send_sems.at[3 * a + j],
                    recv_sem=recv_sems.at[3 * a + j], device_id=(px, py, c), device_id_type=MESH).wait_recv()
        for j, (fx, fy, fc) in enumerate(flips):
            src = 4 * (x ^ fx) + 2 * (y ^ fy) + (c ^ fc)
            pltpu.make_async_remote_copy(
                src_ref=sm_ref, dst_ref=osm_ref.at[src], send_sem=ssend_sems.at[j], recv_sem=srecv_sems.at[j],
                device_id=(x ^ fx, y ^ fy, c ^ fc), device_id_type=MESH).wait_recv()
        for cp in sends:
            cp.wait_send()
        for cp in local:
            cp.wait()

    return pl.pallas_call(
        body, name="exchange_chips",
        in_specs=[HBM] * (n + 1), out_specs=[HBM] * (n + 1),
        out_shape=[jax.ShapeDtypeStruct(p.shape, p.dtype) for p in parts]
                  + [jax.ShapeDtypeStruct((8,) + small.shape, small.dtype)],
        scratch_shapes=[pltpu.SemaphoreType.DMA((3 * n,)), pltpu.SemaphoreType.DMA((3 * n,)),
                        pltpu.SemaphoreType.DMA((7,)), pltpu.SemaphoreType.DMA((7,)),
                        pltpu.SemaphoreType.DMA((n + 1,))],
    )(*parts, small)


def _swap_cores(parts, name):
    n = len(parts)

    def body(*refs):
        ins, outs, send_sems, recv_sems = refs[:n], refs[n:2 * n], refs[2 * n], refs[2 * n + 1]
        x, y, c = _place()
        copies = [pltpu.make_async_remote_copy(
            src_ref=ins[a], dst_ref=outs[a], send_sem=send_sems.at[a], recv_sem=recv_sems.at[a],
            device_id=(x, y, 1 - c), device_id_type=MESH) for a in range(n)]
        for cp in copies:
            cp.start()
        for cp in copies:
            cp.wait()

    return pl.pallas_call(
        body, name=name, in_specs=[HBM] * n, out_specs=[HBM] * n,
        out_shape=[jax.ShapeDtypeStruct(p.shape, p.dtype) for p in parts],
        scratch_shapes=[pltpu.SemaphoreType.DMA((n,)), pltpu.SemaphoreType.DMA((n,))],
    )(*parts)


def _tile(rows, cols):
    for cand in (256, 128, 64):
        if rows % cand == 0:
            return cand, cols
    if rows > 64 and cols % 256 == 0:
        return rows, 256
    return rows, cols


def _add(a, b, name):
    L, R, C = a.shape
    tr, tc = _tile(R, C)

    def body(a_ref, b_ref, o_ref):
        o_ref[...] = (a_ref[...] + b_ref[...]).astype(BF16)

    spec = pl.BlockSpec((1, tr, tc), lambda l, i, j: (l, i, j))
    return pl.pallas_call(
        body, name=name, grid=(L, R // tr, C // tc), in_specs=[spec, spec], out_specs=spec,
        out_shape=jax.ShapeDtypeStruct((L, R, C), BF16),
        compiler_params=_cparams(("parallel", "parallel", "parallel")),
    )(a, b)


def _sum_leading(buf, name):
    n, R, C = buf.shape
    tr, tc = _tile(R, C)

    def body(b_ref, o_ref):
        acc = b_ref[0].astype(F32)
        for k in range(1, n):
            acc = acc + b_ref[k].astype(F32)
        o_ref[...] = acc

    return pl.pallas_call(
        body, name=name, grid=(R // tr, C // tc),
        in_specs=[pl.BlockSpec((n, tr, tc), lambda i, j: (0, i, j))],
        out_specs=pl.BlockSpec((tr, tc), lambda i, j: (i, j)),
        out_shape=jax.ShapeDtypeStruct((R, C), F32),
        compiler_params=_cparams(("parallel", "parallel")),
    )(buf)


def _adam_update(w, g, m, v):
    c1 = 1.0 / (1.0 - ADAM_B1 ** ADAM_STEP)
    c2 = 1.0 / (1.0 - ADAM_B2 ** ADAM_STEP)
    mn = ADAM_B1 * m + (1.0 - ADAM_B1) * g
    vn = ADAM_B2 * v + (1.0 - ADAM_B2) * (g * g)
    return -ADAM_LR * ((mn * c1) / (jnp.sqrt(vn * c2) + ADAM_EPS) + ADAM_WD * w), mn, vn


def _adamw_layers(w, m, v, g_mine, g_other, name):
    _, R, C = w.shape
    tr, tc = _tile(R, C)

    def body(w_ref, m_ref, v_ref, gm_ref, go_ref, g_ref, d_ref, mo_ref, vo_ref):
        g = jnp.where(pl.program_id(0) == lax.axis_index("c"), gm_ref[...], go_ref[...])
        g_ref[0] = g
        d_ref[0], mo_ref[0], vo_ref[0] = _adam_update(w_ref[0], g, m_ref[0], v_ref[0])

    spec3 = pl.BlockSpec((1, tr, tc), lambda l, i, j: (l, i, j))
    spec2 = pl.BlockSpec((tr, tc), lambda l, i, j: (i, j))
    return pl.pallas_call(
        body, name=name, grid=(2, R // tr, C // tc),
        in_specs=[spec3] * 3 + [spec2] * 2, out_specs=[spec3] * 4,
        out_shape=[jax.ShapeDtypeStruct(w.shape, F32)] * 4,
        compiler_params=_cparams(("parallel", "parallel", "parallel")),
    )(w, m, v, g_mine, g_other)


PACK_C = 1024
_BIG = ("w_in", "w_out", "mla_w_uq", "mla_w_ukv", "conv_w")
_SMALL = ("norm_pre", "group_norm", "norm_post", "conv_b", "mla_q_norm", "mla_kv_norm", "attn_sinks")
_SMALL_W = {"norm_pre": 1024, "group_norm": 1024, "norm_post": 1024, "conv_b": 256, "mla_q_norm": 256,
            "mla_kv_norm": 128, "attn_sinks": 4}


_LOSS_AT = divmod(DEPTH * sum(_SMALL_W.values()), PACK_C)


def _pack_small(d, loss):
    flat = jnp.concatenate([d[n].reshape(-1) for n in _SMALL] + [loss.reshape(1)])
    return jnp.pad(flat, (0, 8 * PACK_C - flat.shape[0])).reshape(8, PACK_C)


def _adamw_small(w, m, v, got):
    ns = len(_SMALL)

    def body(*refs):
        got_ref = refs[3 * ns]
        outs = refs[3 * ns + 1:]
        gsum = got_ref[0]
        for d in range(1, 8):
            gsum = gsum + got_ref[d]
        outs[4 * ns][...] = gsum[_LOSS_AT[0]:_LOSS_AT[0] + 1, _LOSS_AT[1]:_LOSS_AT[1] + 1]
        off = 0
        for i, name in enumerate(_SMALL):
            wd = _SMALL_W[name]
            rows = []
            for l in range(DEPTH):
                r, c0 = divmod(off + l * wd, PACK_C)
                rows.append(gsum[r:r + 1, c0:c0 + wd])
            off += DEPTH * wd
            g = jnp.concatenate(rows, axis=0)
            delta, mn, vn = _adam_update(refs[i][...], g, refs[ns + i][...], refs[2 * ns + i][...])
            outs[i][...] = g
            outs[ns + i][...] = delta
            outs[2 * ns + i][...] = mn
            outs[3 * ns + i][...] = vn

    shapes = [jax.ShapeDtypeStruct(w[n].shape, F32) for n in _SMALL]
    res = pl.pallas_call(body, name="adamw_small", out_shape=shapes * 4 + [jax.ShapeDtypeStruct((1, 1), F32)])(
        *[w[n] for n in _SMALL], *[m[n] for n in _SMALL], *[v[n] for n in _SMALL], got)
    return [dict(zip(_SMALL, res[k * ns:(k + 1) * ns])) for k in range(4)], res[4 * ns]


def _w_in_internal(wt):
    rows = []
    for n in _INT_ORDER:
        o, wd = _REAL_OFF[n]
        rows.append(wt[o:o + wd])
        if _INT_W[n] != wd:
            rows.append(jnp.zeros((_INT_W[n] - wd, wt.shape[1]), wt.dtype))
    return jnp.concatenate(rows, axis=0)


def _w_in_real(dwt):
    return jnp.concatenate([dwt[_INT_OFF[n]:_INT_OFF[n] + wd] for n, wd in _REAL], axis=0)


def _uq_internal(w):
    return jnp.pad(w.reshape(256, 4, 96), ((0, 0), (0, 0), (0, 32))).reshape(256, 512)


def _uq_real(dw):
    return dw.reshape(256, 4, 128)[:, :, :96].reshape(256, 384)


def _ukv_internal(w):
    w4 = w.reshape(128, 4, 128)
    k = jnp.pad(w4[:, :, :64], ((0, 0), (0, 0), (0, 64))).reshape(128, 512)
    return jnp.concatenate([k, w4[:, :, 64:].reshape(128, 256)], axis=1)


def _ukv_real(dw):
    k = dw[:, :512].reshape(128, 4, 128)[:, :, :64]
    v = dw[:, 512:].reshape(128, 4, 64)
    return jnp.concatenate([k, v], axis=2).reshape(128, 512)


def _layer_fwd(x, pos, p, tgt=None):
    xn, hb, hf = _inproj_fwd(x, p["norm_pre"], p["w_in"])
    ya = _swa_fwd(hb, p["attn_sinks"])
    qm, km, vm, vt = _mla_prep_fwd(hf, pos, p["mla_q_norm"], p["mla_kv_norm"], p["mla_w_uq"], p["mla_w_ukv"])
    yc, lse = _mla_fwd(qm, km, vt)
    yd, tot, cnt = _sb_fwd(hb)
    x_next = _epilogue_fwd(x, ya, yc, yd, hf, p["conv_w"], p["conv_b"], p["group_norm"], p["w_out"], p["norm_post"],
                           tgt)
    return x_next, dict(x=x, xn=xn, hb=hb, hf=hf, ya=ya, yc=yc, yd=yd, tot=tot, cnt=cnt, qm=qm, km=km, vm=vm, lse=lse)


def _layer_bwd(dx_next, pos, p, s):
    (dya, dyc, dyd, dhf, dw_out, dg_post, dg_grp, dconv_w, dconv_b) = _epilogue_bwd(
        dx_next, s["ya"], s["yc"], s["yd"], s["hf"], p["conv_w"], p["conv_b"], p["group_norm"], p["w_out"],
        p["norm_post"])
    dq_d, dk_d, dv_d = _sb_bwd(s["hb"], s["tot"], s["cnt"], dyd)
    dqm, dkt, dvt = _mla_bwd(s["qm"], s["km"], s["vm"], s["yc"], s["lse"], dyc)
    dc, dw_uq, dw_ukv, dg_q, dg_kv = _mla_prep_bwd(
        s["hf"], pos, p["mla_q_norm"], p["mla_kv_norm"], p["mla_w_uq"], p["mla_w_ukv"], dqm, dkt, dvt)
    dq_a, dk_a, dv_a, dsinks = _swa_bwd(s["hb"], p["attn_sinks"], dya)
    dx, dh, dg_pre = _inproj_bwd_dx(s["x"], p["norm_pre"], p["w_in"], dx_next,
                                    [dq_a, dk_a, dv_a, dq_d, dk_d, dv_d, dhf, dc])
    dwt_in = _grad_over_tokens(s["xn"], dh, "inproj_bwd_dw")
    grads = dict(norm_pre=dg_pre[0], w_in_t=_w_in_real(dwt_in), attn_sinks=dsinks[0, :4], conv_w=dconv_w[:3],
                 conv_b=dconv_b[0], mla_q_norm=dg_q[0], mla_w_uq=_uq_real(dw_uq), mla_kv_norm=dg_kv[0],
                 mla_w_ukv=_ukv_real(dw_ukv), group_norm=dg_grp[0], w_out=dw_out, norm_post=dg_post[0])
    return dx, grads


_WEIGHTS = ["norm_pre", "w_in", "attn_sinks", "conv_w", "conv_b", "mla_q_norm", "mla_w_uq", "mla_kv_norm",
            "mla_w_ukv", "group_norm", "w_out", "norm_post"]


def kernel(x, positions, norm_pre, w_in, attn_sinks, conv_w, conv_b, mla_q_norm, mla_w_uq, mla_kv_norm, mla_w_ukv, group_norm, w_out, norm_post, loss_target, m_norm_pre, m_w_in, m_attn_sinks, m_conv_w, m_conv_b, m_mla_q_norm, m_mla_w_uq, m_mla_kv_norm, m_mla_w_ukv, m_group_norm, m_w_out, m_norm_post, v_norm_pre, v_w_in, v_attn_sinks, v_conv_w, v_conv_b, v_mla_q_norm, v_mla_w_uq, v_mla_kv_norm, v_mla_w_ukv, v_group_norm, v_w_out, v_norm_post):
    w = dict(norm_pre=norm_pre, w_in=w_in, attn_sinks=attn_sinks, conv_w=conv_w, conv_b=conv_b,
             mla_q_norm=mla_q_norm, mla_w_uq=mla_w_uq, mla_kv_norm=mla_kv_norm, mla_w_ukv=mla_w_ukv,
             group_norm=group_norm, w_out=w_out, norm_post=norm_post)
    m = dict(norm_pre=m_norm_pre, w_in=m_w_in, attn_sinks=m_attn_sinks, conv_w=m_conv_w, conv_b=m_conv_b,
             mla_q_norm=m_mla_q_norm, mla_w_uq=m_mla_w_uq, mla_kv_norm=m_mla_kv_norm, mla_w_ukv=m_mla_w_ukv,
             group_norm=m_group_norm, w_out=m_w_out, norm_post=m_norm_post)
    v = dict(norm_pre=v_norm_pre, w_in=v_w_in, attn_sinks=v_attn_sinks, conv_w=v_conv_w, conv_b=v_conv_b,
             mla_q_norm=v_mla_q_norm, mla_w_uq=v_mla_w_uq, mla_kv_norm=v_mla_kv_norm, mla_w_ukv=v_mla_w_ukv,
             group_norm=v_group_norm, w_out=v_w_out, norm_post=v_norm_post)
    T = x.shape[1]
    xs = x[0]
    pos = positions[0].reshape(T, 1)
    tgt = loss_target[0]
    core = lax.axis_index("c")

    gathered = _gather_weights([jnp.swapaxes(w["w_in"], 1, 2).astype(BF16)]
                               + [w[n].astype(BF16) for n in _BIG[1:4]] + [w["conv_w"]])
    full = {}
    for n, got in zip(_BIG, gathered):
        if n in ("w_in", "w_out"):
            full[n] = jnp.moveaxis(got, 0, 1).reshape(DEPTH, 4 * got.shape[2], got.shape[3])
        else:
            full[n] = jnp.transpose(got, (1, 2, 0, 3)).reshape(DEPTH, got.shape[2], 4 * got.shape[3])

    layers = []
    for l in range(DEPTH):
        layers.append(dict(
            norm_pre=norm_pre[l:l + 1], w_in=_w_in_internal(full["w_in"][l]), attn_sinks=attn_sinks[l],
            conv_w=full["conv_w"][l], conv_b=conv_b[l:l + 1], mla_q_norm=mla_q_norm[l:l + 1],
            mla_w_uq=_uq_internal(full["mla_w_uq"][l]), mla_kv_norm=mla_kv_norm[l:l + 1],
            mla_w_ukv=_ukv_internal(full["mla_w_ukv"][l]), group_norm=group_norm[l:l + 1],
            w_out=full["w_out"][l], norm_post=norm_post[l:l + 1]))

    saved = []
    h = xs
    for l in range(DEPTH):
        h, s = _layer_fwd(h, pos, layers[l], tgt if l == DEPTH - 1 else None)
        saved.append(s)
    dy, loss_part = h

    grads = [None] * DEPTH
    for l in reversed(range(DEPTH)):
        dy, grads[l] = _layer_bwd(dy, pos, layers[l], saved[l])

    turned = ("w_in", "mla_w_uq")
    turn = lambda n, a: jnp.swapaxes(a, -1, -2) if n in turned else a

    def chunks(n, a):
        if n in ("w_out", "w_in"):
            return a.reshape(4, a.shape[0] // 4, a.shape[1])
        if n in turned:
            return a.T.reshape(4, a.shape[1] // 4, a.shape[0])
        return jnp.transpose(a.reshape(a.shape[0], 4, a.shape[1] // 4), (1, 0, 2))

    grad = lambda l, n: grads[l]["w_in_t" if n == "w_in" else n]
    mine = [chunks(n, jnp.where(core == 0, grad(0, n), grad(1, n))) for n in _BIG]
    theirs = [chunks(n, jnp.where(core == 0, grad(1, n), grad(0, n))) for n in _BIG]
    from_sibling = _swap_cores(theirs, "swap_layer_chunks")
    summed = [_add(a, b, "add_cores_" + n) for n, a, b in zip(_BIG, mine, from_sibling)]
    small = _pack_small({n: jnp.stack([grads[l][n] for l in range(DEPTH)]) for n in _SMALL}, loss_part[0, 0])
    *got, got_small = _exchange_chips(summed, small)
    done = [_sum_leading(b, "sum_chips_" + n) for n, b in zip(_BIG, got)]
    done_other = _swap_cores(done, "swap_layer_shards")

    outs, loss = _adamw_small(w, m, v, got_small)
    for n, gm, go in zip(_BIG, done, done_other):
        for d, a in zip(outs, _adamw_layers(turn(n, w[n]), turn(n, m[n]), turn(n, v[n]), gm, go, "adamw_" + n)):
            d[n] = turn(n, a)
    return (loss[0, 0], dy[None], *[outs[0][n] for n in _WEIGHTS], *[outs[1][n] for n in _WEIGHTS],
            *[outs[2][n] for n in _WEIGHTS], *[outs[3][n] for n in _WEIGHTS])
```

```python
import math

import jax
import jax.numpy as jnp
from jax import lax
from jax.experimental import pallas as pl
from jax.experimental.pallas import tpu as pltpu

F32 = jnp.float32
BF16 = jnp.bfloat16
MESH = pl.DeviceIdType.MESH

D_MODEL = 1024
DEPTH = 2
EPS = 1e-6
BLOCK = 128
HEAD = 64
LANES = 128
GROUP = 256
LOG2E = 1.4426950408889634
LN2 = 0.6931471805599453
MLA_QSCALE = 96 ** -0.5 * LOG2E
ROPE_HALF = 16
ROPE_THETA = 10000.0
SWA_SUB = 2
ATT_BLK = 256
MLA_BQ = 512
NEG = -1e30
SB_DEAD = -104.0

ADAM_LR, ADAM_B1, ADAM_B2, ADAM_EPS, ADAM_WD, ADAM_STEP = 0.001, 0.9, 0.999, 1e-08, 0.01, 10

_REAL = [("a_q", 256), ("a_k", 128), ("a_v", 128), ("b_b", 256), ("b_c", 256), ("b_x", 256),
         ("c_q", 256), ("c_kv", 128), ("c_kr", 32), ("d_q", 256), ("d_k", 256), ("d_v", 256),
         ("gate", 1024)]
_REAL_OFF = {}
_o = 0
for _n, _w in _REAL:
    _REAL_OFF[_n] = (_o, _w)
    _o += _w
D_IN = _o
_INT_ORDER = ["a_q", "a_k", "a_v", "d_q", "d_k", "d_v", "gate", "b_b", "b_c", "b_x", "c_q", "c_kv", "c_kr"]
_INT_W = dict(_REAL)
_INT_W["c_kr"] = 128
_INT_OFF = {}
_o = 0
for _n in _INT_ORDER:
    _INT_OFF[_n] = _o
    _o += _INT_W[_n]
N_INT = _o
N_HB = _INT_OFF["gate"]
N_HF = N_INT - N_HB

VMEM_LIMIT = 56 * 1024 * 1024


def _cparams(sem):
    return pltpu.CompilerParams(dimension_semantics=sem, vmem_limit_bytes=VMEM_LIMIT)


def _dot(a, b):
    return jnp.dot(a, b, preferred_element_type=F32)


def _dot_nt(a, b):
    return lax.dot_general(a, b, (((1,), (1,)), ((), ())), preferred_element_type=F32)


def _dot_tn(a, b):
    return lax.dot_general(a, b, (((0,), (0,)), ((), ())), preferred_element_type=F32)


def _split(x):
    hi = x.astype(BF16)
    lo = (x - hi.astype(F32)).astype(BF16)
    return hi, lo


def _rms(x):
    return lax.rsqrt(jnp.mean(x * x, axis=-1, keepdims=True) + EPS)


def _rms_bwd(dy, xhat, r, g):
    dxhat = dy * g
    return r * (dxhat - xhat * jnp.mean(dxhat * xhat, axis=-1, keepdims=True)), dy * xhat


def _colsum(x):
    return jnp.sum(x, axis=0, keepdims=True)


def _inproj_fwd(x, g, wt):
    T = x.shape[0]
    tm = 512

    def body(x_ref, g_ref, w_ref, xn_ref, hb_ref, hf_ref):
        xv = x_ref[...]
        xn = (xv * _rms(xv) * g_ref[...]).astype(BF16)
        xn_ref[...] = xn
        h = _dot_nt(xn, w_ref[...])
        hb_ref[...] = h[:, :N_HB].astype(BF16)
        hf_ref[...] = h[:, N_HB:]

    return pl.pallas_call(
        body, name="inproj_fwd", grid=(T // tm,),
        in_specs=[pl.BlockSpec((tm, D_MODEL), lambda i: (i, 0)),
                  pl.BlockSpec((1, D_MODEL), lambda i: (0, 0)),
                  pl.BlockSpec((N_INT, D_MODEL), lambda i: (0, 0))],
        out_specs=[pl.BlockSpec((tm, D_MODEL), lambda i: (i, 0)),
                   pl.BlockSpec((tm, N_HB), lambda i: (i, 0)),
                   pl.BlockSpec((tm, N_HF), lambda i: (i, 0))],
        out_shape=[jax.ShapeDtypeStruct((T, D_MODEL), BF16),
                   jax.ShapeDtypeStruct((T, N_HB), BF16),
                   jax.ShapeDtypeStruct((T, N_HF), F32)],
        compiler_params=_cparams(("parallel",)),
    )(x, g, wt)


def _inproj_bwd_dx(x, g, wt, dx_next, pieces):
    T = x.shape[0]
    tm = 512
    widths = [p.shape[1] for p in pieces]
    assert sum(widths) == N_INT

    def body(x_ref, g_ref, w_ref, dxn_ref, *rest):
        p_refs = rest[:len(pieces)]
        dx_ref, dh_ref, dg_ref = rest[len(pieces):]
        dh = jnp.concatenate([p[...].astype(BF16) for p in p_refs], axis=1)
        dh_ref[...] = dh
        dxn = _dot(dh, w_ref[...])
        xv = x_ref[...]
        r = _rms(xv)
        dx, dgrow = _rms_bwd(dxn, xv * r, r, g_ref[...])
        dx_ref[...] = dx + dxn_ref[...]

        @pl.when(pl.program_id(0) == 0)
        def _():
            dg_ref[...] = jnp.zeros_like(dg_ref)

        dg_ref[...] += _colsum(dgrow)

    return pl.pallas_call(
        body, name="inproj_bwd_dx", grid=(T // tm,),
        in_specs=[pl.BlockSpec((tm, D_MODEL), lambda i: (i, 0)),
                  pl.BlockSpec((1, D_MODEL), lambda i: (0, 0)),
                  pl.BlockSpec((N_INT, D_MODEL), lambda i: (0, 0)),
                  pl.BlockSpec((tm, D_MODEL), lambda i: (i, 0))]
                 + [pl.BlockSpec((tm, wd), lambda i: (i, 0)) for wd in widths],
        out_specs=[pl.BlockSpec((tm, D_MODEL), lambda i: (i, 0)),
                   pl.BlockSpec((tm, N_INT), lambda i: (i, 0)),
                   pl.BlockSpec((1, D_MODEL), lambda i: (0, 0))],
        out_shape=[jax.ShapeDtypeStruct((T, D_MODEL), F32),
                   jax.ShapeDtypeStruct((T, N_INT), BF16),
                   jax.ShapeDtypeStruct((1, D_MODEL), F32)],
        compiler_params=_cparams(("arbitrary",)),
    )(x, g, wt, dx_next, *pieces)


def _grad_over_tokens(a, b, name):
    T, M = a.shape
    N = b.shape[1]
    tm, tn = min(1024, T), 512

    def body(a_ref, b_ref, o_ref):
        @pl.when(pl.program_id(1) == 0)
        def _():
            o_ref[...] = jnp.zeros_like(o_ref)

        o_ref[...] += _dot_tn(b_ref[...], a_ref[...])

    return pl.pallas_call(
        body, name=name, grid=(N // tn, T // tm),
        in_specs=[pl.BlockSpec((tm, M), lambda j, t: (t, 0)),
                  pl.BlockSpec((tm, tn), lambda j, t: (t, j))],
        out_specs=pl.BlockSpec((tn, M), lambda j, t: (j, 0)),
        out_shape=jax.ShapeDtypeStruct((N, M), F32),
        compiler_params=_cparams(("parallel", "arbitrary")),
    )(a, b)


def _roll_f32(x, shift):
    return pltpu.roll(x.astype(F32), shift, 1)


def _swa_operands(h, q, k_prev, k_cur, v_prev, v_cur):
    p, e = h // 2, h % 2
    lane = lax.broadcasted_iota(jnp.int32, (1, LANES), 1) // HEAD
    q = q[:, p * LANES:(p + 1) * LANES]
    if e != p:
        q = _roll_f32(q, HEAD).astype(BF16)
        v_prev = _roll_f32(v_prev, HEAD).astype(BF16)
        v_cur = _roll_f32(v_cur, HEAD).astype(BF16)
    qs = jnp.where(lane == p, q, 0) * 0.125
    return dict(p=p, e=e, lane=lane, qs=qs, k_prev=k_prev, k_cur=k_cur,
                v_prev=jnp.where(lane == e, v_prev, 0), v_cur=jnp.where(lane == e, v_cur, 0),
                s_prev=_dot_nt(qs, k_prev), s_cur=_dot_nt(qs, k_cur))


def _swa_probs(ops, sink, no_prev):
    row = lax.broadcasted_iota(jnp.int32, (BLOCK, BLOCK), 0)
    col = lax.broadcasted_iota(jnp.int32, (BLOCK, BLOCK), 1)
    ok_prev = col > row if no_prev is None else jnp.logical_and(col > row, jnp.logical_not(no_prev))
    s_prev = jnp.where(ok_prev, ops["s_prev"], NEG)
    s_cur = jnp.where(col <= row, ops["s_cur"], NEG)
    m = jnp.maximum(jnp.maximum(jnp.max(s_prev, axis=1, keepdims=True),
                                jnp.max(s_cur, axis=1, keepdims=True)), sink)
    p_prev = jnp.exp(s_prev - m)
    p_cur = jnp.exp(s_cur - m)
    p_sink = jnp.exp(sink - m)
    inv = 1.0 / (jnp.sum(p_prev, axis=1, keepdims=True) + jnp.sum(p_cur, axis=1, keepdims=True) + p_sink)
    return p_prev * inv, p_cur * inv, p_sink * inv


def _swa_specs(T):
    n = T // (BLOCK * SWA_SUB)
    qo, ko, vo = (_INT_OFF[name] // LANES for name in ("a_q", "a_k", "a_v"))
    halo = lambda i: jnp.maximum(i * SWA_SUB - 1, 0)
    return [pl.BlockSpec((BLOCK * SWA_SUB, 256), lambda i: (i, qo // 2)),
            pl.BlockSpec((BLOCK, LANES), lambda i: (halo(i), ko)),
            pl.BlockSpec((BLOCK * SWA_SUB, LANES), lambda i: (i, ko)),
            pl.BlockSpec((BLOCK, LANES), lambda i: (halo(i), vo)),
            pl.BlockSpec((BLOCK * SWA_SUB, LANES), lambda i: (i, vo)),
            pl.BlockSpec(memory_space=pltpu.SMEM)], n


def _swa_units(q_ref, kh_ref, kc_ref, vh_ref, vc_ref, s_ref):
    blk = lambda a: slice(a * BLOCK, (a + 1) * BLOCK)
    units = [(a, h) for a in range(SWA_SUB) for h in range(4)]
    ops = {}
    for a, h in units:
        k_prev, v_prev = (kh_ref[...], vh_ref[...]) if a == 0 else (kc_ref[blk(a - 1), :], vc_ref[blk(a - 1), :])
        ops[a, h] = _swa_operands(h, q_ref[blk(a), :], k_prev, kc_ref[blk(a), :], v_prev, vc_ref[blk(a), :])
    probs = {(a, h): _swa_probs(ops[a, h], s_ref[h], pl.program_id(0) == 0 if a == 0 else None) for a, h in units}
    return units, ops, probs, blk


def _swa_fwd(hb, sinks):
    T = hb.shape[0]
    specs, n = _swa_specs(T)

    def body(q_ref, kh_ref, kc_ref, vh_ref, vc_ref, s_ref, o_ref):
        units, ops, probs, blk = _swa_units(q_ref, kh_ref, kc_ref, vh_ref, vc_ref, s_ref)
        outs = {u: _dot(probs[u][0].astype(BF16), ops[u]["v_prev"]) + _dot(probs[u][1].astype(BF16), ops[u]["v_cur"])
                for u in units}
        for a in range(SWA_SUB):
            for p in range(2):
                o_ref[blk(a), p * LANES:(p + 1) * LANES] = outs[a, 2 * p] + outs[a, 2 * p + 1]

    return pl.pallas_call(
        body, name="swa_fwd", grid=(n,), in_specs=specs,
        out_specs=pl.BlockSpec((BLOCK * SWA_SUB, 256), lambda i: (i, 0)),
        out_shape=jax.ShapeDtypeStruct((T, 256), F32),
        compiler_params=_cparams(("parallel",)),
    )(hb, hb, hb, hb, hb, sinks)


def _swa_bwd(hb, sinks, dy):
    T = hb.shape[0]
    specs, n = _swa_specs(T)

    def body(q_ref, kh_ref, kc_ref, vh_ref, vc_ref, s_ref, dy_ref, dq_ref, dk_ref, dv_ref, ds_ref):
        i = pl.program_id(0)

        @pl.when(i == 0)
        def _():
            ds_ref[...] = jnp.zeros_like(ds_ref)

        lane_id = lax.broadcasted_iota(jnp.int32, (8, LANES), 1)
        units, ops, probs, blk = _swa_units(q_ref, kh_ref, kc_ref, vh_ref, vc_ref, s_ref)
        dos = {(a, h): jnp.where(ops[a, h]["lane"] == ops[a, h]["e"],
                                 dy_ref[blk(a), ops[a, h]["p"] * LANES:(ops[a, h]["p"] + 1) * LANES], 0.0)
               for a, h in units}
        dobs = {u: dos[u].astype(BF16) for u in units}
        pbs = {u: (probs[u][0].astype(BF16), probs[u][1].astype(BF16)) for u in units}
        outs = {u: _dot(pbs[u][0], ops[u]["v_prev"]) + _dot(pbs[u][1], ops[u]["v_cur"]) for u in units}
        dps = {u: (_dot_nt(dobs[u], ops[u]["v_prev"]), _dot_nt(dobs[u], ops[u]["v_cur"])) for u in units}
        dss, dsinks = {}, jnp.zeros((8, LANES), F32)
        for u in units:
            delta = jnp.sum(dos[u] * outs[u], axis=1, keepdims=True)
            dss[u] = ((probs[u][0] * (dps[u][0] - delta)).astype(BF16),
                      (probs[u][1] * (dps[u][1] - delta)).astype(BF16))
            dsink = -jnp.sum(probs[u][2] * delta, axis=0, keepdims=True)
            dsinks += jnp.where(lane_id == u[1], dsink, 0.0)
        ds_ref[...] += dsinks
        dqs = {u: (_dot(dss[u][0], ops[u]["k_prev"]) + _dot(dss[u][1], ops[u]["k_cur"])) * 0.125 for u in units}
        zero = jnp.zeros((BLOCK, LANES), F32)
        dk_as_prev, dk_as_cur = [zero] * SWA_SUB, [zero] * SWA_SUB
        dv_as_prev, dv_as_cur = [zero] * SWA_SUB, [zero] * SWA_SUB
        for a, h in units:
            p, e = ops[a, h]["p"], ops[a, h]["e"]
            dob_v = dobs[a, h] if e == p else pltpu.roll(dos[a, h], HEAD, 1).astype(BF16)
            dk_as_prev[a] = dk_as_prev[a] + _dot_tn(dss[a, h][0], ops[a, h]["qs"])
            dk_as_cur[a] = dk_as_cur[a] + _dot_tn(dss[a, h][1], ops[a, h]["qs"])
            dv_as_prev[a] = dv_as_prev[a] + _dot_tn(pbs[a, h][0], dob_v)
            dv_as_cur[a] = dv_as_cur[a] + _dot_tn(pbs[a, h][1], dob_v)
        base = i * SWA_SUB
        for a in range(SWA_SUB):
            rows = pl.ds(pl.multiple_of((base + a) * BLOCK, BLOCK), BLOCK)
            more = a + 1 < SWA_SUB
            dk_ref[rows, :] = dk_as_cur[a] + (dk_as_prev[a + 1] if more else 0.0)
            dv_ref[rows, :] = dv_as_cur[a] + (dv_as_prev[a + 1] if more else 0.0)
        halo = pl.ds(pl.multiple_of(jnp.maximum(base - 1, 0) * BLOCK, BLOCK), BLOCK)
        dk_ref[halo, :] += dk_as_prev[0]
        dv_ref[halo, :] += dv_as_prev[0]
        for a in range(SWA_SUB):
            for p in range(2):
                dq_pair = jnp.zeros((BLOCK, LANES), F32)
                for e in range(2):
                    dq = jnp.where(ops[a, 2 * p + e]["lane"] == p, dqs[a, 2 * p + e], 0.0)
                    dq_pair += dq if e == p else pltpu.roll(dq, HEAD, 1)
                dq_ref[blk(a), p * LANES:(p + 1) * LANES] = dq_pair.astype(BF16)

    return pl.pallas_call(
        body, name="swa_bwd", grid=(n,),
        in_specs=specs + [pl.BlockSpec((BLOCK * SWA_SUB, 256), lambda i: (i, 0))],
        out_specs=[pl.BlockSpec((BLOCK * SWA_SUB, 256), lambda i: (i, 0)),
                   pl.BlockSpec((T, LANES), lambda i: (0, 0)),
                   pl.BlockSpec((T, LANES), lambda i: (0, 0)),
                   pl.BlockSpec((8, LANES), lambda i: (0, 0))],
        out_shape=[jax.ShapeDtypeStruct((T, 256), BF16),
                   jax.ShapeDtypeStruct((T, LANES), F32),
                   jax.ShapeDtypeStruct((T, LANES), F32),
                   jax.ShapeDtypeStruct((8, LANES), F32)],
        compiler_params=_cparams(("arbitrary",)),
    )(hb, hb, hb, hb, hb, sinks, dy)


def _rope_tables(pos_ref):
    lane = lax.broadcasted_iota(jnp.int32, (1, LANES), 1)
    active = jnp.logical_and(lane >= HEAD, lane < HEAD + 2 * ROPE_HALF)
    idx = ((lane - HEAD) % ROPE_HALF).astype(F32)
    freq = jnp.exp(idx * (-math.log(ROPE_THETA) / ROPE_HALF))
    ang = pos_ref[...].astype(F32) * freq
    cos, sin = jnp.cos(ang), jnp.sin(ang)
    c = jnp.where(active, cos, 1.0)
    s_up = jnp.where(jnp.logical_and(active, lane >= HEAD + ROPE_HALF), sin, 0.0)
    s_dn = jnp.where(jnp.logical_and(active, lane < HEAD + ROPE_HALF), -sin, 0.0)
    return c, s_up, s_dn


def _rope(x, tabs):
    c, s_up, s_dn = tabs
    return x * c + pltpu.roll(x, ROPE_HALF, 1) * s_up + pltpu.roll(x, LANES - ROPE_HALF, 1) * s_dn


def _rope_t(dy, tabs):
    c, s_up, s_dn = tabs
    return dy * c + pltpu.roll(dy * s_up, LANES - ROPE_HALF, 1) + pltpu.roll(dy * s_dn, ROPE_HALF, 1)


def _mla_lat_specs(tm):
    cq, ckv, ckr = ((_INT_OFF[n] - N_HB) for n in ("c_q", "c_kv", "c_kr"))
    return [pl.BlockSpec((tm, 256), lambda i: (i, cq // 256)),
            pl.BlockSpec((tm, LANES), lambda i: (i, ckv // LANES)),
            pl.BlockSpec((tm, LANES), lambda i: (i, ckr // LANES)),
            pl.BlockSpec((tm, 1), lambda i: (i, 0)),
            pl.BlockSpec((1, 256), lambda i: (0, 0)),
            pl.BlockSpec((1, LANES), lambda i: (0, 0)),
            pl.BlockSpec((256, 512), lambda i: (0, 0)),
            pl.BlockSpec((LANES, 768), lambda i: (0, 0))]


def _mla_prep_fwd(hf, pos, g_q, g_kv, w_uq, w_ukv):
    T = hf.shape[0]
    tm = 512
    sub = tm // ATT_BLK

    def body(cq_ref, ckv_ref, ckr_ref, pos_ref, gq_ref, gkv_ref, wq_ref, wkv_ref, qm_ref, km_ref, vm_ref, vt_ref):
        tabs = _rope_tables(pos_ref)
        cq = cq_ref[...]
        q = _dot((cq * _rms(cq) * gq_ref[...]).astype(BF16), wq_ref[...])
        ckv = ckv_ref[...]
        kv = _dot((ckv * _rms(ckv) * gkv_ref[...]).astype(BF16), wkv_ref[...])
        kr = _rope(pltpu.roll(ckr_ref[...], HEAD, 1), tabs)
        for h in range(4):
            sl = slice(h * LANES, (h + 1) * LANES)
            qm_ref[:, sl] = (_rope(q[:, sl], tabs) * MLA_QSCALE).astype(BF16)
            km_ref[:, sl] = (kv[:, sl] + kr).astype(BF16)
        vm_ref[...] = kv[:, 512:].astype(BF16)
        for p in range(2):
            for s in range(sub):
                tile = kv[s * ATT_BLK:(s + 1) * ATT_BLK, 512 + p * LANES:512 + (p + 1) * LANES]
                vt_ref[p, s] = jnp.transpose(tile).astype(BF16)

    return pl.pallas_call(
        body, name="mla_prep_fwd", grid=(T // tm,), in_specs=_mla_lat_specs(tm),
        out_specs=[pl.BlockSpec((tm, 512), lambda i: (i, 0)),
                   pl.BlockSpec((tm, 512), lambda i: (i, 0)),
                   pl.BlockSpec((tm, 256), lambda i: (i, 0)),
                   pl.BlockSpec((2, sub, LANES, ATT_BLK), lambda i: (0, i, 0, 0))],
        out_shape=[jax.ShapeDtypeStruct((T, 512), BF16),
                   jax.ShapeDtypeStruct((T, 512), BF16),
                   jax.ShapeDtypeStruct((T, 256), BF16),
                   jax.ShapeDtypeStruct((2, T // ATT_BLK, LANES, ATT_BLK), BF16)],
        compiler_params=_cparams(("parallel",)),
    )(hf, hf, hf, pos, g_q, g_kv, w_uq, w_ukv)


def _mla_prep_bwd(hf, pos, g_q, g_kv, w_uq, w_ukv, dqm, dkt, dvt):
    T = hf.shape[0]
    tm = 512
    sub = tm // ATT_BLK

    def body(cq_ref, ckv_ref, ckr_ref, pos_ref, gq_ref, gkv_ref, wq_ref, wkv_ref, dq_ref, dk_ref, dv_ref,
             dc_ref, dwq_ref, dwkv_ref, dgq_ref, dgkv_ref):
        @pl.when(pl.program_id(0) == 0)
        def _():
            dwq_ref[...] = jnp.zeros_like(dwq_ref)
            dwkv_ref[...] = jnp.zeros_like(dwkv_ref)
            dgq_ref[...] = jnp.zeros_like(dgq_ref)
            dgkv_ref[...] = jnp.zeros_like(dgkv_ref)

        tabs = _rope_tables(pos_ref)
        lane = lax.broadcasted_iota(jnp.int32, (1, LANES), 1)
        dq = jnp.concatenate([_rope_t(dq_ref[:, h * LANES:(h + 1) * LANES] * MLA_QSCALE, tabs)
                              for h in range(4)], axis=1).astype(BF16)
        cq = cq_ref[...]
        rq = _rms(cq)
        cqn = (cq * rq * gq_ref[...]).astype(BF16)
        dwq_ref[...] += _dot_tn(cqn, dq)
        dcq, dgrow = _rms_bwd(_dot_nt(dq, wq_ref[...]), cq * rq, rq, gq_ref[...])
        dgq_ref[...] += _colsum(dgrow)
        dc_ref[:, 0:256] = dcq.astype(BF16)

        dk = jnp.concatenate([jnp.concatenate([jnp.transpose(dk_ref[p, s]) for p in range(2)], axis=1)
                              for s in range(sub)], axis=0) * LN2
        dv = jnp.concatenate([jnp.concatenate([jnp.transpose(dv_ref[p, s]) for p in range(2)], axis=1)
                              for s in range(sub)], axis=0)
        dkr = dk[:, 0:LANES] + dk[:, LANES:2 * LANES] + dk[:, 2 * LANES:3 * LANES] + dk[:, 3 * LANES:]
        dkr = pltpu.roll(_rope_t(dkr, tabs), HEAD, 1)
        dc_ref[:, 384:512] = jnp.where(lane < 2 * ROPE_HALF, dkr, 0.0).astype(BF16)
        dkv = jnp.concatenate([dk.astype(BF16), dv.astype(BF16)], axis=1)
        ckv = ckv_ref[...]
        rkv = _rms(ckv)
        ckvn = (ckv * rkv * gkv_ref[...]).astype(BF16)
        dwkv_ref[...] += _dot_tn(ckvn, dkv)
        dckv, dgrow = _rms_bwd(_dot_nt(dkv, wkv_ref[...]), ckv * rkv, rkv, gkv_ref[...])
        dgkv_ref[...] += _colsum(dgrow)
        dc_ref[:, 256:384] = dckv.astype(BF16)

    return pl.pallas_call(
        body, name="mla_prep_bwd", grid=(T // tm,),
        in_specs=_mla_lat_specs(tm) + [pl.BlockSpec((tm, 512), lambda i: (i, 0)),
                                       pl.BlockSpec((2, sub, 256, ATT_BLK), lambda i: (0, i, 0, 0)),
                                       pl.BlockSpec((2, sub, LANES, ATT_BLK), lambda i: (0, i, 0, 0))],
        out_specs=[pl.BlockSpec((tm, 512), lambda i: (i, 0)),
                   pl.BlockSpec((256, 512), lambda i: (0, 0)),
                   pl.BlockSpec((LANES, 768), lambda i: (0, 0)),
                   pl.BlockSpec((1, 256), lambda i: (0, 0)),
                   pl.BlockSpec((1, LANES), lambda i: (0, 0))],
        out_shape=[jax.ShapeDtypeStruct((T, 512), BF16),
                   jax.ShapeDtypeStruct((256, 512), F32),
                   jax.ShapeDtypeStruct((LANES, 768), F32),
                   jax.ShapeDtypeStruct((1, 256), F32),
                   jax.ShapeDtypeStruct((1, LANES), F32)],
        compiler_params=_cparams(("arbitrary",)),
    )(hf, hf, hf, pos, g_q, g_kv, w_uq, w_ukv, dqm, dkt, dvt)


def _causal_masks(bq, bk):
    row = lax.broadcasted_iota(jnp.int32, (bq, bk), 0)
    col = lax.broadcasted_iota(jnp.int32, (bq, bk), 1)
    return row, col


def _mla_fwd(qm, km, vt):
    T = qm.shape[0]
    bq, bk = min(MLA_BQ, T), ATT_BLK
    nq, nsub, nk = T // bq, bq // bk, T // bk

    def body(q_ref, k_ref, vt_ref, o_ref, lse_ref, acc_ref, m_ref, l_ref):
        qi = pl.program_id(0)
        key = lax.broadcasted_iota(jnp.int32, (bk, bq), 0)
        qry = lax.broadcasted_iota(jnp.int32, (bk, bq), 1)
        ones = jnp.ones((8, bk), BF16)
        acc_ref[...] = jnp.zeros_like(acc_ref)
        m_ref[...] = jnp.full_like(m_ref, NEG)
        l_ref[...] = jnp.zeros_like(l_ref)

        def step(kb0, masked):
            kbs = [kb0 + d for d in range(nsub)]
            sts = [[_dot_nt(k_ref[pl.ds(pl.multiple_of(kb * bk, bk), bk), e * LANES:(e + 1) * LANES],
                            q_ref[:, e * LANES:(e + 1) * LANES]) for kb in kbs] for e in range(4)]
            pts, alphas = [], []
            for e in range(4):
                st = [jnp.where(key + d * bk <= qry, sts[e][d], NEG) for d in range(nsub)] if masked else sts[e]
                m_prev = m_ref[e, 0:1, :]
                m_new = m_prev
                for d in range(nsub):
                    m_new = jnp.maximum(m_new, jnp.max(st[d], axis=0, keepdims=True))
                alpha = jnp.exp2(m_prev - m_new)
                pt = [jnp.exp2(st[d] - m_new).astype(BF16) for d in range(nsub)]
                l_new = alpha * l_ref[e]
                for d in range(nsub):
                    l_new = l_new + _dot(ones, pt[d])
                l_ref[e] = l_new
                m_ref[e] = jnp.broadcast_to(m_new, (8, bq))
                pts.append(pt)
                alphas.append(alpha)
            for e in range(4):
                acc = alphas[e] * acc_ref[e]
                for d in range(nsub):
                    v_t = vt_ref[e // 2, kbs[d], (e % 2) * HEAD:(e % 2 + 1) * HEAD, :]
                    acc = acc + _dot(v_t, pts[e][d])
                acc_ref[e] = acc

        step(qi * nsub, True)

        def loop(t, c):
            step(t * nsub, False)
            return c

        lax.fori_loop(0, qi, loop, 0)
        outs, lses = [], []
        for e in range(4):
            l = l_ref[e, 0:1, :]
            outs.append(acc_ref[e] / l)
            lses.append(jnp.broadcast_to(m_ref[e, 0:1, :] * LN2 + jnp.log(l), (HEAD, bq)))
        o_ref[...] = jnp.transpose(jnp.concatenate(outs, axis=0))
        lse_ref[...] = jnp.transpose(jnp.concatenate(lses, axis=0))

    return pl.pallas_call(
        body, name="mla_fwd", grid=(nq,),
        in_specs=[pl.BlockSpec((bq, 512), lambda i: (i, 0)),
                  pl.BlockSpec((T, 512), lambda i: (0, 0)),
                  pl.BlockSpec((2, nk, LANES, bk), lambda i: (0, 0, 0, 0))],
        out_specs=[pl.BlockSpec((bq, 256), lambda i: (i, 0)),
                   pl.BlockSpec((bq, 256), lambda i: (i, 0))],
        out_shape=[jax.ShapeDtypeStruct((T, 256), F32), jax.ShapeDtypeStruct((T, 256), F32)],
        scratch_shapes=[pltpu.VMEM((4, HEAD, bq), F32), pltpu.VMEM((4, 8, bq), F32), pltpu.VMEM((4, 8, bq), F32)],
        compiler_params=_cparams(("arbitrary",)),
    )(qm, km, vt)


def _mla_bwd(qm, km, vm, y, lse, dy):
    T = qm.shape[0]
    bq, bk = min(MLA_BQ, T), ATT_BLK
    nq, nsub, nk = T // bq, bq // bk, T // bk

    def body(q_ref, k_ref, v_ref, y_ref, lse_ref, dy_ref, dq_ref, dkt_ref, dvt_ref, dob_ref, st_ref, qt_ref, dot_ref):
        qi = pl.program_id(1)

        @pl.when(qi == 0)
        def _():
            dkt_ref[...] = jnp.zeros_like(dkt_ref)
            dvt_ref[...] = jnp.zeros_like(dvt_ref)

        lane = lax.broadcasted_iota(jnp.int32, (1, LANES), 1) // HEAD
        row, col = _causal_masks(bq, bk)
        dq_ref[...] = jnp.zeros_like(dq_ref)
        lse = lse_ref[...]
        lse_other = pltpu.roll(lse, HEAD, 1)
        qt_ref[...] = jnp.transpose(q_ref[...].astype(F32)).astype(BF16)
        dot_ref[...] = jnp.transpose(dy_ref[...]).astype(BF16)
        for e in range(2):
            do = jnp.where(lane == e, dy_ref[...], 0.0)
            dob_ref[e] = do.astype(BF16)
            st_ref[2 * e] = jnp.where(lane == e, lse, lse_other) * LOG2E
            st_ref[2 * e + 1] = jnp.broadcast_to(jnp.sum(do * y_ref[...], axis=1, keepdims=True), (bq, LANES))

        hss = [slice(e * LANES, (e + 1) * LANES) for e in range(2)]
        tile = lambda a: jnp.concatenate([a] * (bk // LANES), axis=1)

        def step(kb0, masked):
            kbs = [kb0 + d for d in range(nsub)]
            rows = [pl.ds(pl.multiple_of(kb * bk, bk), bk) for kb in kbs]
            pairs = [(d, e) for d in range(nsub) for e in range(2)]
            ss = {(d, e): _dot_nt(q_ref[:, hss[e]], k_ref[rows[d], hss[e]]) for d, e in pairs}
            dps = {(d, e): _dot_nt(dob_ref[e], jnp.where(lane == e, v_ref[rows[d], :], 0)) for d, e in pairs}
            ps, dss = {}, {}
            for d, e in pairs:
                s = jnp.where(col + d * bk <= row, ss[d, e], NEG) if masked else ss[d, e]
                p = jnp.exp2(s - tile(st_ref[2 * e]))
                dss[d, e] = (p * (dps[d, e] - tile(st_ref[2 * e + 1]))).astype(BF16)
                ps[d, e] = p.astype(BF16)
            for d, e in pairs:
                dvt_ref[0, kbs[d], e * HEAD:(e + 1) * HEAD, :] += _dot(dot_ref[e * HEAD:(e + 1) * HEAD, :], ps[d, e])
            for d, e in pairs:
                dkt_ref[0, kbs[d], hss[e], :] += _dot(qt_ref[hss[e], :], dss[d, e])
            for e in range(2):
                dq = dq_ref[:, hss[e]]
                for d in range(nsub):
                    dq = dq + _dot(dss[d, e], k_ref[rows[d], hss[e]])
                dq_ref[:, hss[e]] = dq

        step(qi * nsub, True)

        def loop(t, c):
            step(t * nsub, False)
            return c

        lax.fori_loop(0, qi, loop, 0)
        dq_ref[...] *= LN2

    return pl.pallas_call(
        body, name="mla_bwd", grid=(2, nq),
        in_specs=[pl.BlockSpec((bq, 256), lambda j, i: (i, j)),
                  pl.BlockSpec((T, 256), lambda j, i: (0, j)),
                  pl.BlockSpec((T, LANES), lambda j, i: (0, j)),
                  pl.BlockSpec((bq, LANES), lambda j, i: (i, j)),
                  pl.BlockSpec((bq, LANES), lambda j, i: (i, j)),
                  pl.BlockSpec((bq, LANES), lambda j, i: (i, j))],
        out_specs=[pl.BlockSpec((bq, 256), lambda j, i: (i, j)),
                   pl.BlockSpec((1, nk, 256, bk), lambda j, i: (j, 0, 0, 0)),
                   pl.BlockSpec((1, nk, LANES, bk), lambda j, i: (j, 0, 0, 0))],
        out_shape=[jax.ShapeDtypeStruct((T, 512), F32),
                   jax.ShapeDtypeStruct((2, nk, 256, bk), F32),
                   jax.ShapeDtypeStruct((2, nk, LANES, bk), F32)],
        scratch_shapes=[pltpu.VMEM((2, bq, LANES), BF16), pltpu.VMEM((4, bq, LANES), F32),
                        pltpu.VMEM((256, bq), BF16), pltpu.VMEM((LANES, bq), BF16)],
        compiler_params=_cparams(("parallel", "arbitrary")),
    )(qm, km, vm, y, lse, dy)


def _suffix_ones(n):
    r = lax.broadcasted_iota(jnp.int32, (n, n), 0)
    c = lax.broadcasted_iota(jnp.int32, (n, n), 1)
    return (r >= c).astype(BF16)


def _prefix_ones(n):
    r = lax.broadcasted_iota(jnp.int32, (n, n), 0)
    c = lax.broadcasted_iota(jnp.int32, (n, n), 1)
    return (r <= c).astype(BF16)


def _sb_specs(T, bq):
    qo, ko, vo = (_INT_OFF[n] // 256 for n in ("d_q", "d_k", "d_v"))
    return [pl.BlockSpec((bq, 256), lambda i: (i, qo)),
            pl.BlockSpec((T, 256), lambda i: (0, ko)),
            pl.BlockSpec((T, 256), lambda i: (0, vo))]


def _sb_fwd(hb):
    T = hb.shape[0]
    bq = bk = ATT_BLK
    nq = T // bq

    def body(q_ref, k_ref, v_ref, o_ref, tot_ref, cnt_ref, qm_ref, car_ref):
        qi = pl.program_id(0)
        lane = lax.broadcasted_iota(jnp.int32, (1, LANES), 1) // HEAD
        row, col = _causal_masks(bq, bk)
        strict = col < row
        u = _suffix_ones(bk)
        o_ref[...] = jnp.zeros_like(o_ref)
        car_ref[...] = jnp.zeros_like(car_ref)
        pair = lambda h: slice((h // 2) * LANES, (h // 2 + 1) * LANES)
        for h in range(4):
            qm_ref[h] = jnp.where(lane == h % 2, q_ref[:, pair(h)], 0) * 0.125

        def step(blocks):
            tile = lambda a: jnp.concatenate([a] * (bk // LANES), axis=1)
            rows = [pl.ds(pl.multiple_of(kb * bk, bk), bk) for kb, _ in blocks]
            pairs = [(b, h) for b in range(len(blocks)) for h in range(4)]
            zs = {(b, h): _dot_nt(qm_ref[h], k_ref[rows[b], pair(h)]) for b, h in pairs}
            splits = {}
            for b, h in pairs:
                z = zs[b, h]
                lk = jnp.minimum(-z, 0.0) - jnp.log(1.0 + jnp.exp(-jnp.abs(z)))
                if blocks[b][1] is not None:
                    lk = jnp.where(blocks[b][1], lk, 0.0)
                splits[b, h] = _split(lk)
            sufs = {bh: _dot(hi, u) + _dot(lo, u) for bh, (hi, lo) in splits.items()}
            car = [car_ref[h] for h in range(4)]
            aas = {}
            for b, h in pairs:
                a = jnp.exp(zs[b, h] + sufs[b, h] + tile(car[h]))
                if blocks[b][1] is not None:
                    a = jnp.where(blocks[b][1], a, 0.0)
                aas[b, h] = a.astype(BF16)
                car[h] = car[h] + jnp.broadcast_to(sufs[b, h][:, 0:1], (bq, LANES))
            acc = [o_ref[:, pair(0)], o_ref[:, pair(2)]]
            for b, h in pairs:
                acc[h // 2] = acc[h // 2] + _dot(aas[b, h], jnp.where(lane == h % 2, v_ref[rows[b], pair(h)], 0))
            o_ref[:, pair(0)], o_ref[:, pair(2)] = acc
            for h in range(4):
                car_ref[h] = car[h]

        step([(qi, strict), (jnp.maximum(qi - 1, 0), qi > 0)])

        def live():
            worst = jnp.maximum(jnp.maximum(car_ref[0], car_ref[1]), jnp.maximum(car_ref[2], car_ref[3]))
            return jnp.max(worst) >= SB_DEAD

        def cond(c):
            return jnp.logical_and(c[0] < qi, c[1])

        def loop(c):
            step([(qi - 1 - c[0], None)])
            return c[0] + 1, live()

        done, _ = lax.while_loop(cond, loop, (jnp.minimum(qi, 1), live()))
        tot_ref[:, pair(0)] = jnp.where(lane == 0, car_ref[0], car_ref[1])
        tot_ref[:, pair(2)] = jnp.where(lane == 0, car_ref[2], car_ref[3])
        cnt_ref[0, qi] = done.astype(F32)

    return pl.pallas_call(
        body, name="sb_fwd", grid=(nq,), in_specs=_sb_specs(T, bq),
        out_specs=[pl.BlockSpec((bq, 256), lambda i: (i, 0)), pl.BlockSpec((bq, 256), lambda i: (i, 0)),
                   pl.BlockSpec(memory_space=pltpu.SMEM)],
        out_shape=[jax.ShapeDtypeStruct((T, 256), F32), jax.ShapeDtypeStruct((T, 256), F32),
                   jax.ShapeDtypeStruct((1, nq), F32)],
        scratch_shapes=[pltpu.VMEM((4, bq, LANES), BF16), pltpu.VMEM((4, bq, LANES), F32)],
        compiler_params=_cparams(("arbitrary",)),
    )(hb, hb, hb)


def _sb_bwd(hb, tot, cnt, dy):
    T = hb.shape[0]
    bq = bk = ATT_BLK
    nq = T // bq

    def body(q_ref, k_ref, v_ref, tot_ref, dy_ref, cnt_ref, dq_ref, dk_ref, dv_ref, qm_ref, dob_ref, dqa_ref, rem_ref,
             cg_ref):
        qi = pl.program_id(0)

        @pl.when(qi == 0)
        def _():
            dk_ref[...] = jnp.zeros_like(dk_ref)
            dv_ref[...] = jnp.zeros_like(dv_ref)

        lane = lax.broadcasted_iota(jnp.int32, (1, LANES), 1) // HEAD
        row, col = _causal_masks(bq, bk)
        strict = col < row
        u = _prefix_ones(bk)
        pair = lambda h: slice((h // 2) * LANES, (h // 2 + 1) * LANES)
        dqa_ref[...] = jnp.zeros_like(dqa_ref)
        cg_ref[...] = jnp.zeros_like(cg_ref)
        for h in range(4):
            tot = tot_ref[:, pair(h)]
            qm_ref[h] = jnp.where(lane == h % 2, q_ref[:, pair(h)], 0) * 0.125
            dob_ref[h] = jnp.where(lane == h % 2, dy_ref[:, pair(h)], 0.0).astype(BF16)
            rem_ref[h] = jnp.where(lane == h % 2, tot, pltpu.roll(tot, HEAD, 1))

        def step(blocks):
            tile = lambda a: jnp.concatenate([a] * (bk // LANES), axis=1)
            nb = len(blocks)
            rows = [pl.ds(pl.multiple_of(kb * bk, bk), bk) for kb, _ in blocks]
            pairs = [(b, h) for b in range(nb) for h in range(4)]
            mask = lambda b, x: x if blocks[b][1] is None else jnp.where(blocks[b][1], x, 0.0)
            zs = {(b, h): _dot_nt(qm_ref[h], k_ref[rows[b], pair(h)]) for b, h in pairs}
            das = {(b, h): _dot_nt(dob_ref[h], jnp.where(lane == h % 2, v_ref[rows[b], pair(h)], 0)) for b, h in pairs}
            zls, splits = {}, {}
            for b, h in pairs:
                z = zs[b, h]
                lk = mask(b, jnp.minimum(-z, 0.0) - jnp.log(1.0 + jnp.exp(-jnp.abs(z))))
                zls[b, h] = z + lk
                splits[b, h] = _split(lk)
            pres = {bh: _dot(hi, u) + _dot(lo, u) for bh, (hi, lo) in splits.items()}
            rem = [rem_ref[h] for h in range(4)]
            aas, gs, gsplits = {}, {}, {}
            for b, h in pairs:
                a = mask(b, jnp.exp(zls[b, h] + (tile(rem[h]) - pres[b, h])))
                gs[b, h] = a * das[b, h]
                aas[b, h] = a.astype(BF16)
                gsplits[b, h] = _split(gs[b, h])
                rem[h] = rem[h] - jnp.broadcast_to(pres[b, h][:, bk - 1:bk], (bq, LANES))
            for b in range(nb):
                for p in (0, 2):
                    dv_ref[rows[b], pair(p)] += _dot_tn(aas[b, p], dob_ref[p]) + _dot_tn(aas[b, p + 1], dob_ref[p + 1])
            gpres = {bh: _dot(hi, u) + _dot(lo, u) for bh, (hi, lo) in gsplits.items()}
            cg = [cg_ref[h] for h in range(4)]
            dzs = {}
            for b, h in pairs:
                dz = mask(b, gs[b, h] - jnp.exp(zls[b, h]) * (tile(cg[h]) + gpres[b, h]))
                dzs[b, h] = dz.astype(BF16)
                cg[h] = cg[h] + jnp.broadcast_to(gpres[b, h][:, bk - 1:bk], (bq, LANES))
            for b in range(nb):
                for p in (0, 2):
                    dk_ref[rows[b], pair(p)] += _dot_tn(dzs[b, p], qm_ref[p]) + _dot_tn(dzs[b, p + 1], qm_ref[p + 1])
            for h in range(4):
                dq = dqa_ref[h]
                for b in range(nb):
                    dq = dq + _dot(dzs[b, h], k_ref[rows[b], pair(h)])
                dqa_ref[h] = dq
                rem_ref[h] = rem[h]
                cg_ref[h] = cg[h]

        def loop(kb, c):
            step([(kb, None)])
            return c

        start = qi - jnp.clip(cnt_ref[0, qi].astype(jnp.int32), 0, qi)
        lax.fori_loop(start, qi - 1, loop, 0)
        step([(jnp.maximum(qi - 1, 0), qi > 0), (qi, strict)])
        for p in (0, 2):
            dq_ref[:, pair(p)] = (jnp.where(lane == 0, dqa_ref[p], dqa_ref[p + 1]) * 0.125).astype(BF16)

    return pl.pallas_call(
        body, name="sb_bwd", grid=(nq,),
        in_specs=_sb_specs(T, bq) + [pl.BlockSpec((bq, 256), lambda i: (i, 0)),
                                     pl.BlockSpec((bq, 256), lambda i: (i, 0)),
                                     pl.BlockSpec(memory_space=pltpu.SMEM)],
        out_specs=[pl.BlockSpec((bq, 256), lambda i: (i, 0)),
                   pl.BlockSpec((T, 256), lambda i: (0, 0)),
                   pl.BlockSpec((T, 256), lambda i: (0, 0))],
        out_shape=[jax.ShapeDtypeStruct((T, 256), BF16)] + [jax.ShapeDtypeStruct((T, 256), F32)] * 2,
        scratch_shapes=[pltpu.VMEM((4, bq, LANES), BF16), pltpu.VMEM((4, bq, LANES), BF16),
                        pltpu.VMEM((4, bq, LANES), F32), pltpu.VMEM((4, bq, LANES), F32),
                        pltpu.VMEM((4, bq, LANES), F32)],
        compiler_params=_cparams(("arbitrary",)),
    )(hb, hb, hb, tot, dy, cnt)


EP_TM = 512


def _ep_in_specs(tm, rev):
    idx = (lambda i: rev - i) if rev is not None else (lambda i: i)
    bo = (_INT_OFF["b_b"] - N_HB) // 256
    halo = lambda i: jnp.maximum(idx(i) * (tm // 8) - 1, 0)
    return [pl.BlockSpec((tm, 256), lambda i: (idx(i), 0)),
            pl.BlockSpec((tm, 256), lambda i: (idx(i), 0)),
            pl.BlockSpec((tm, 256), lambda i: (idx(i), 0)),
            pl.BlockSpec((tm, D_MODEL), lambda i: (idx(i), 0)),
            pl.BlockSpec((tm, 256), lambda i: (idx(i), bo)),
            pl.BlockSpec((tm, 256), lambda i: (idx(i), bo + 1)),
            pl.BlockSpec((tm, 256), lambda i: (idx(i), bo + 2)),
            pl.BlockSpec((8, 256), lambda i: (halo(i), bo + 1)),
            pl.BlockSpec((8, 256), lambda i: (halo(i), bo + 2)),
            pl.BlockSpec((3, 256), lambda i: (0, 0)),
            pl.BlockSpec((1, 256), lambda i: (0, 0)),
            pl.BlockSpec((1, D_MODEL), lambda i: (0, 0)),
            pl.BlockSpec((D_MODEL, D_MODEL), lambda i: (0, 0)),
            pl.BlockSpec((1, D_MODEL), lambda i: (0, 0))]


def _ep_mix(first, ya_ref, yc_ref, yd_ref, gate_ref, bb_ref, bc_ref, bx_ref, hc_ref, hx_ref, cw_ref, cb_ref, gg_ref):
    tm = ya_ref.shape[0]
    u = bc_ref[...] * bx_ref[...]
    halo = jnp.where(first, 0.0, hc_ref[...] * hx_ref[...])
    row = lax.broadcasted_iota(jnp.int32, (tm, 1), 0)
    u1 = jnp.where(row == 0, halo[7:8, :], pltpu.roll(u, 1, 0))
    u2 = jnp.where(row == 0, halo[6:7, :], jnp.where(row == 1, halo[7:8, :], pltpu.roll(u, 2, 0)))
    cw = cw_ref[...]
    conv = cw[0:1, :] * u2 + cw[1:2, :] * u1 + cw[2:3, :] * u + cb_ref[...]
    bb = bb_ref[...]
    ys = [ya_ref[...], bb * conv, yc_ref[...], yd_ref[...]]
    rs = [_rms(y) for y in ys]
    gg = gg_ref[...]
    yhat = jnp.concatenate([y * r for y, r in zip(ys, rs)], axis=1)
    gate = gate_ref[...]
    sig = 1.0 / (1.0 + jnp.exp(-gate))
    return u, u1, u2, conv, bb, rs, yhat, yhat * gg, gate, sig


def _epilogue_fwd(x, ya, yc, yd, hf, conv_w, conv_b, g_grp, w_out, g_post, tgt=None):
    T = x.shape[0]
    tm = EP_TM
    row_spec = pl.BlockSpec((tm, D_MODEL), lambda i: (i, 0))

    def layer_out(refs):
        (x_ref, ya_ref, yc_ref, yd_ref, gate_ref, bb_ref, bc_ref, bx_ref, hc_ref, hx_ref, cw_ref, cb_ref,
         gg_ref, wo_ref, gp_ref) = refs
        (_, _, _, _, _, _, _, yn, gate, sig) = _ep_mix(
            pl.program_id(0) == 0, ya_ref, yc_ref, yd_ref, gate_ref, bb_ref, bc_ref, bx_ref, hc_ref, hx_ref,
            cw_ref, cb_ref, gg_ref)
        z = _dot((yn * (gate * sig)).astype(BF16), wo_ref[...])
        return x_ref[...] + z * _rms(z) * gp_ref[...]

    args = (x, ya, yc, yd, hf, hf, hf, hf, hf, hf, conv_w, conv_b, g_grp, w_out, g_post)
    in_specs = [row_spec] + _ep_in_specs(tm, None)
    if tgt is None:
        def body(*refs):
            refs[-1][...] = layer_out(refs[:-1])

        return pl.pallas_call(
            body, name="epilogue_fwd", grid=(T // tm,), in_specs=in_specs, out_specs=row_spec,
            out_shape=jax.ShapeDtypeStruct((T, D_MODEL), F32), compiler_params=_cparams(("parallel",)),
        )(*args)

    def body_loss(*refs):
        t_ref, dy_ref, l_ref = refs[-3:]

        @pl.when(pl.program_id(0) == 0)
        def _():
            l_ref[...] = jnp.zeros_like(l_ref)

        d = layer_out(refs[:-3]) - t_ref[...]
        dy_ref[...] = d * (1.0 / D_MODEL)
        part = jnp.sum(jnp.sum(d * d, axis=1, keepdims=True), axis=0, keepdims=True)
        l_ref[...] += part * (0.5 / D_MODEL)

    return pl.pallas_call(
        body_loss, name="epilogue_fwd_loss", grid=(T // tm,), in_specs=in_specs + [row_spec],
        out_specs=[row_spec, pl.BlockSpec((8, LANES), lambda i: (0, 0))],
        out_shape=[jax.ShapeDtypeStruct((T, D_MODEL), F32), jax.ShapeDtypeStruct((8, LANES), F32)],
        compiler_params=_cparams(("arbitrary",)),
    )(*args, tgt)


def _epilogue_bwd(dxn, ya, yc, yd, hf, conv_w, conv_b, g_grp, w_out, g_post):
    T = dxn.shape[0]
    tm = EP_TM
    nt = T // tm
    ridx = lambda i: (nt - 1 - i, 0)

    def body(dx_ref, ya_ref, yc_ref, yd_ref, gate_ref, bb_ref, bc_ref, bx_ref, hc_ref, hx_ref, cw_ref, cb_ref,
             gg_ref, wo_ref, gp_ref,
             dya_ref, dyc_ref, dyd_ref, dhf_ref, dwo_ref, dgp_ref, dgg_ref, dcw_ref, dcb_ref, carry_ref):
        i = pl.program_id(0)

        @pl.when(i == 0)
        def _():
            for r in (dwo_ref, dgp_ref, dgg_ref, dcw_ref, dcb_ref, carry_ref):
                r[...] = jnp.zeros_like(r)

        (u, u1, u2, conv, bb, rs, yhat, yn, gate, sig) = _ep_mix(
            i == nt - 1, ya_ref, yc_ref, yd_ref, gate_ref, bb_ref, bc_ref, bx_ref, hc_ref, hx_ref,
            cw_ref, cb_ref, gg_ref)
        silu = gate * sig
        ymix = (yn * silu).astype(BF16)
        z = _dot(ymix, wo_ref[...])
        rz = _rms(z)
        dz, dgrow = _rms_bwd(dx_ref[...], z * rz, rz, gp_ref[...])
        dgp_ref[...] += _colsum(dgrow)
        dzb = dz.astype(BF16)
        dwo_ref[...] += _dot_tn(ymix, dzb)
        dymix = _dot_nt(dzb, wo_ref[...])
        dhf_ref[:, 0:D_MODEL] = (dymix * yn * (sig * (1.0 + gate * (1.0 - sig)))).astype(BF16)
        dyn = dymix * silu
        dgg_ref[...] += _colsum(dyn * yhat)
        gg = gg_ref[...]
        dys = []
        for gi in range(4):
            sl = slice(gi * GROUP, (gi + 1) * GROUP)
            dyh = dyn[:, sl] * gg[:, sl]
            yh = yhat[:, sl]
            dys.append(rs[gi] * (dyh - yh * jnp.mean(dyh * yh, axis=-1, keepdims=True)))
        dya_ref[...] = dys[0]
        dyc_ref[...] = dys[2]
        dyd_ref[...] = dys[3]
        dyb = dys[1]
        dhf_ref[:, D_MODEL:D_MODEL + 256] = (dyb * conv).astype(BF16)
        dconv = dyb * bb
        dcb_ref[...] += _colsum(dconv)
        dcw_ref[0:1, :] += _colsum(dconv * u2)
        dcw_ref[1:2, :] += _colsum(dconv * u1)
        dcw_ref[2:3, :] += _colsum(dconv * u)
        carry = carry_ref[...]
        row = lax.broadcasted_iota(jnp.int32, (tm, 1), 0)
        d1 = jnp.where(row == tm - 1, carry[0:1, :], pltpu.roll(dconv, tm - 1, 0))
        d2 = jnp.where(row == tm - 2, carry[0:1, :],
                       jnp.where(row == tm - 1, carry[1:2, :], pltpu.roll(dconv, tm - 2, 0)))
        cw = cw_ref[...]
        du = cw[2:3, :] * dconv + cw[1:2, :] * d1 + cw[0:1, :] * d2
        dhf_ref[:, D_MODEL + 256:D_MODEL + 512] = (du * bx_ref[...]).astype(BF16)
        dhf_ref[:, D_MODEL + 512:D_MODEL + 768] = (du * bc_ref[...]).astype(BF16)
        carry_ref[...] = dconv[0:8, :]

    in_specs = [pl.BlockSpec((tm, D_MODEL), ridx)] + _ep_in_specs(tm, nt - 1)
    return pl.pallas_call(
        body, name="epilogue_bwd", grid=(nt,), in_specs=in_specs,
        out_specs=[pl.BlockSpec((tm, 256), ridx), pl.BlockSpec((tm, 256), ridx), pl.BlockSpec((tm, 256), ridx),
                   pl.BlockSpec((tm, D_MODEL + 768), ridx),
                   pl.BlockSpec((D_MODEL, D_MODEL), lambda i: (0, 0)),
                   pl.BlockSpec((1, D_MODEL), lambda i: (0, 0)),
                   pl.BlockSpec((1, D_MODEL), lambda i: (0, 0)),
                   pl.BlockSpec((8, 256), lambda i: (0, 0)),
                   pl.BlockSpec((1, 256), lambda i: (0, 0))],
        out_shape=[jax.ShapeDtypeStruct((T, 256), F32)] * 3
                  + [jax.ShapeDtypeStruct((T, D_MODEL + 768), BF16),
                     jax.ShapeDtypeStruct((D_MODEL, D_MODEL), F32),
                     jax.ShapeDtypeStruct((1, D_MODEL), F32),
                     jax.ShapeDtypeStruct((1, D_MODEL), F32),
                     jax.ShapeDtypeStruct((8, 256), F32),
                     jax.ShapeDtypeStruct((1, 256), F32)],
        scratch_shapes=[pltpu.VMEM((8, 256), F32)],
        compiler_params=_cparams(("arbitrary",)),
    )(dxn, ya, yc, yd, hf, hf, hf, hf, hf, hf, conv_w, conv_b, g_grp, w_out, g_post)


def _place():
    return lax.axis_index("x"), lax.axis_index("y"), lax.axis_index("c")


def _other_chips(x, y):
    return [(1 - x, y), (x, 1 - y), (1 - x, 1 - y)]


HBM = pl.BlockSpec(memory_space=pl.ANY)


def _gather_weights(shards):
    n = len(shards)

    def body(*refs):
        ins, outs = refs[:n], refs[n:2 * n]
        ici_send, ici_recv, d2d_send, d2d_recv, local_sems = refs[2 * n:]
        x, y, c = _place()
        me = 2 * x + y
        chips = _other_chips(x, y)

        def ici(a, j, layer_from):
            px, py = chips[j]
            return pltpu.make_async_remote_copy(
                src_ref=ins[a].at[c], dst_ref=outs[a].at[layer_from, c], send_sem=ici_send.at[3 * a + j],
                recv_sem=ici_recv.at[3 * a + j], device_id=(px, py, c), device_id_type=MESH)

        def d2d(a, j, layer):
            px, py = chips[j]
            blk = outs[a].at[2 * px + py, layer]
            return pltpu.make_async_remote_copy(
                src_ref=blk, dst_ref=blk, send_sem=d2d_send.at[3 * a + j], recv_sem=d2d_recv.at[3 * a + j],
                device_id=(x, y, 1 - c), device_id_type=MESH)

        local = [pltpu.make_async_copy(ins[a], outs[a].at[me], local_sems.at[a]) for a in range(n)]
        for cp in local:
            cp.start()
        sends = [ici(a, j, me) for j in range(3) for a in range(n)]
        for cp in sends:
            cp.start()
        for j in range(3):
            px, py = chips[j]
            for a in range(n):
                ici(a, j, 2 * px + py).wait_recv()
                fwd = d2d(a, j, c)
                fwd.start()
                sends.append(fwd)
        for j in range(3):
            for a in range(n):
                d2d(a, j, 1 - c).wait_recv()
        for cp in sends:
            cp.wait_send()
        for cp in local:
            cp.wait()

    return pl.pallas_call(
        body, name="gather_weights",
        in_specs=[HBM] * n, out_specs=[HBM] * n,
        out_shape=[jax.ShapeDtypeStruct((4,) + s.shape, s.dtype) for s in shards],
        scratch_shapes=[pltpu.SemaphoreType.DMA((3 * n,))] * 4 + [pltpu.SemaphoreType.DMA((n,))],
    )(*shards)


def _exchange_chips(parts, small):
    n = len(parts)

    def body(*refs):
        ins, sm_ref = refs[:n], refs[n]
        outs, osm_ref = refs[n + 1:2 * n + 1], refs[2 * n + 1]
        send_sems, recv_sems, ssend_sems, srecv_sems, local_sems = refs[2 * n + 2:]
        x, y, c = _place()
        me = 2 * x + y
        dev = 4 * x + 2 * y + c
        local = [pltpu.make_async_copy(ins[a].at[me], outs[a].at[me], local_sems.at[a]) for a in range(n)]
        local.append(pltpu.make_async_copy(sm_ref, osm_ref.at[dev], local_sems.at[n]))
        for cp in local:
            cp.start()
        sends = []
        for j, (px, py) in enumerate(_other_chips(x, y)):
            for a in range(n):
                cp = pltpu.make_async_remote_copy(
                    src_ref=ins[a].at[2 * px + py], dst_ref=outs[a].at[me], send_sem=send_sems.at[3 * a + j],
                    recv_sem=recv_sems.at[3 * a + j], device_id=(px, py, c), device_id_type=MESH)
                cp.start()
                sends.append(cp)
        flips = [(fx, fy, fc) for fx in (0, 1) for fy in (0, 1) for fc in (0, 1)][1:]
        for j, (fx, fy, fc) in enumerate(flips):
            cp = pltpu.make_async_remote_copy(
                src_ref=sm_ref, dst_ref=osm_ref.at[dev], send_sem=ssend_sems.at[j], recv_sem=srecv_sems.at[j],
                device_id=(x ^ fx, y ^ fy, c ^ fc), device_id_type=MESH)
            cp.start()
            sends.append(cp)
        for j, (px, py) in enumerate(_other_chips(x, y)):
            for a in range(n):
                pltpu.make_async_remote_copy(
                    src_ref=ins[a].at[me], dst_ref=outs[a].at[2 * px + py], send_sem=send_sems.at[3 * a + j],
                    recv_sem=recv_sems.at[3 * a + j], device_id=(px, py, c), device_id_type=MESH).wait_recv()
        for j, (fx, fy, fc) in enumerate(flips):
            src = 4 * (x ^ fx) + 2 * (y ^ fy) + (c ^ fc)
            pltpu.make_async_remote_copy(
                src_ref=sm_ref, dst_ref=osm_ref.at[src], send_sem=ssend_sems.at[j], recv_sem=srecv_sems.at[j],
                device_id=(x ^ fx, y ^ fy, c ^ fc), device_id_type=MESH).wait_recv()
        for cp in sends:
            cp.wait_send()
        for cp in local:
            cp.wait()

    return pl.pallas_call(
        body, name="exchange_chips",
        in_specs=[HBM] * (n + 1), out_specs=[HBM] * (n + 1),
        out_shape=[jax.ShapeDtypeStruct(p.shape, p.dtype) for p in parts]
                  + [jax.ShapeDtypeStruct((8,) + small.shape, small.dtype)],
        scratch_shapes=[pltpu.SemaphoreType.DMA((3 * n,)), pltpu.SemaphoreType.DMA((3 * n,)),
                        pltpu.SemaphoreType.DMA((7,)), pltpu.SemaphoreType.DMA((7,)),
                        pltpu.SemaphoreType.DMA((n + 1,))],
    )(*parts, small)


def _swap_cores(parts, name):
    n = len(parts)

    def body(*refs):
        ins, outs, send_sems, recv_sems = refs[:n], refs[n:2 * n], refs[2 * n], refs[2 * n + 1]
        x, y, c = _place()
        copies = [pltpu.make_async_remote_copy(
            src_ref=ins[a], dst_ref=outs[a], send_sem=send_sems.at[a], recv_sem=recv_sems.at[a],
            device_id=(x, y, 1 - c), device_id_type=MESH) for a in range(n)]
        for cp in copies:
            cp.start()
        for cp in copies:
            cp.wait()

    return pl.pallas_call(
        body, name=name, in_specs=[HBM] * n, out_specs=[HBM] * n,
        out_shape=[jax.ShapeDtypeStruct(p.shape, p.dtype) for p in parts],
        scratch_shapes=[pltpu.SemaphoreType.DMA((n,)), pltpu.SemaphoreType.DMA((n,))],
    )(*parts)


def _tile(rows, cols):
    for cand in (256, 128, 64):
        if rows % cand == 0:
            return cand, cols
    if rows > 64 and cols % 256 == 0:
        return rows, 256
    return rows, cols


def _add(a, b, name):
    L, R, C = a.shape
    tr, tc = _tile(R, C)

    def body(a_ref, b_ref, o_ref):
        o_ref[...] = (a_ref[...] + b_ref[...]).astype(BF16)

    spec = pl.BlockSpec((1, tr, tc), lambda l, i, j: (l, i, j))
    return pl.pallas_call(
        body, name=name, grid=(L, R // tr, C // tc), in_specs=[spec, spec], out_specs=spec,
        out_shape=jax.ShapeDtypeStruct((L, R, C), BF16),
        compiler_params=_cparams(("parallel", "parallel", "parallel")),
    )(a, b)


def _sum_leading(buf, name):
    n, R, C = buf.shape
    tr, tc = _tile(R, C)

    def body(b_ref, o_ref):
        acc = b_ref[0].astype(F32)
        for k in range(1, n):
            acc = acc + b_ref[k].astype(F32)
        o_ref[...] = acc

    return pl.pallas_call(
        body, name=name, grid=(R // tr, C // tc),
        in_specs=[pl.BlockSpec((n, tr, tc), lambda i, j: (0, i, j))],
        out_specs=pl.BlockSpec((tr, tc), lambda i, j: (i, j)),
        out_shape=jax.ShapeDtypeStruct((R, C), F32),
        compiler_params=_cparams(("parallel", "parallel")),
    )(buf)


def _adam_update(w, g, m, v):
    c1 = 1.0 / (1.0 - ADAM_B1 ** ADAM_STEP)
    c2 = 1.0 / (1.0 - ADAM_B2 ** ADAM_STEP)
    mn = ADAM_B1 * m + (1.0 - ADAM_B1) * g
    vn = ADAM_B2 * v + (1.0 - ADAM_B2) * (g * g)
    return -ADAM_LR * ((mn * c1) / (jnp.sqrt(vn * c2) + ADAM_EPS) + ADAM_WD * w), mn, vn


def _adamw_layers(w, m, v, g_mine, g_other, name):
    _, R, C = w.shape
    tr, tc = _tile(R, C)

    def body(w_ref, m_ref, v_ref, gm_ref, go_ref, g_ref, d_ref, mo_ref, vo_ref):
        g = jnp.where(pl.program_id(0) == lax.axis_index("c"), gm_ref[...], go_ref[...])
        g_ref[0] = g
        d_ref[0], mo_ref[0], vo_ref[0] = _adam_update(w_ref[0], g, m_ref[0], v_ref[0])

    spec3 = pl.BlockSpec((1, tr, tc), lambda l, i, j: (l, i, j))
    spec2 = pl.BlockSpec((tr, tc), lambda l, i, j: (i, j))
    return pl.pallas_call(
        body, name=name, grid=(2, R // tr, C // tc),
        in_specs=[spec3] * 3 + [spec2] * 2, out_specs=[spec3] * 4,
        out_shape=[jax.ShapeDtypeStruct(w.shape, F32)] * 4,
        compiler_params=_cparams(("parallel", "parallel", "parallel")),
    )(w, m, v, g_mine, g_other)


PACK_C = 1024
_BIG = ("w_in", "w_out", "mla_w_uq", "mla_w_ukv", "conv_w")
_SMALL = ("norm_pre", "group_norm", "norm_post", "conv_b", "mla_q_norm", "mla_kv_norm", "attn_sinks")
_SMALL_W = {"norm_pre": 1024, "group_norm": 1024, "norm_post": 1024, "conv_b": 256, "mla_q_norm": 256,
            "mla_kv_norm": 128, "attn_sinks": 4}


_LOSS_AT = divmod(DEPTH * sum(_SMALL_W.values()), PACK_C)


def _pack_small(d, loss):
    flat = jnp.concatenate([d[n].reshape(-1) for n in _SMALL] + [loss.reshape(1)])
    return jnp.pad(flat, (0, 8 * PACK_C - flat.shape[0])).reshape(8, PACK_C)


def _adamw_small(w, m, v, got):
    ns = len(_SMALL)

    def body(*refs):
        got_ref = refs[3 * ns]
        outs = refs[3 * ns + 1:]
        gsum = got_ref[0]
        for d in range(1, 8):
            gsum = gsum + got_ref[d]
        outs[4 * ns][...] = gsum[_LOSS_AT[0]:_LOSS_AT[0] + 1, _LOSS_AT[1]:_LOSS_AT[1] + 1]
        off = 0
        for i, name in enumerate(_SMALL):
            wd = _SMALL_W[name]
            rows = []
            for l in range(DEPTH):
                r, c0 = divmod(off + l * wd, PACK_C)
                rows.append(gsum[r:r + 1, c0:c0 + wd])
            off += DEPTH * wd
            g = jnp.concatenate(rows, axis=0)
            delta, mn, vn = _adam_update(refs[i][...], g, refs[ns + i][...], refs[2 * ns + i][...])
            outs[i][...] = g
            outs[ns + i][...] = delta
            outs[2 * ns + i][...] = mn
            outs[3 * ns + i][...] = vn

    shapes = [jax.ShapeDtypeStruct(w[n].shape, F32) for n in _SMALL]
    res = pl.pallas_call(body, name="adamw_small", out_shape=shapes * 4 + [jax.ShapeDtypeStruct((1, 1), F32)])(
        *[w[n] for n in _SMALL], *[m[n] for n in _SMALL], *[v[n] for n in _SMALL], got)
    return [dict(zip(_SMALL, res[k * ns:(k + 1) * ns])) for k in range(4)], res[4 * ns]


def _w_in_internal(wt):
    rows = []
    for n in _INT_ORDER:
        o, wd = _REAL_OFF[n]
        rows.append(wt[o:o + wd])
        if _INT_W[n] != wd:
            rows.append(jnp.zeros((_INT_W[n] - wd, wt.shape[1]), wt.dtype))
    return jnp.concatenate(rows, axis=0)


def _w_in_real(dwt):
    return jnp.concatenate([dwt[_INT_OFF[n]:_INT_OFF[n] + wd] for n, wd in _REAL], axis=0)


def _uq_internal(w):
    return jnp.pad(w.reshape(256, 4, 96), ((0, 0), (0, 0), (0, 32))).reshape(256, 512)


def _uq_real(dw):
    return dw.reshape(256, 4, 128)[:, :, :96].reshape(256, 384)


def _ukv_internal(w):
    w4 = w.reshape(128, 4, 128)
    k = jnp.pad(w4[:, :, :64], ((0, 0), (0, 0), (0, 64))).reshape(128, 512)
    return jnp.concatenate([k, w4[:, :, 64:].reshape(128, 256)], axis=1)


def _ukv_real(dw):
    k = dw[:, :512].reshape(128, 4, 128)[:, :, :64]
    v = dw[:, 512:].reshape(128, 4, 64)
    return jnp.concatenate([k, v], axis=2).reshape(128, 512)


def _layer_fwd(x, pos, p, tgt=None):
    xn, hb, hf = _inproj_fwd(x, p["norm_pre"], p["w_in"])
    ya = _swa_fwd(hb, p["attn_sinks"])
    qm, km, vm, vt = _mla_prep_fwd(hf, pos, p["mla_q_norm"], p["mla_kv_norm"], p["mla_w_uq"], p["mla_w_ukv"])
    yc, lse = _mla_fwd(qm, km, vt)
    yd, tot, cnt = _sb_fwd(hb)
    x_next = _epilogue_fwd(x, ya, yc, yd, hf, p["conv_w"], p["conv_b"], p["group_norm"], p["w_out"], p["norm_post"],
                           tgt)
    return x_next, dict(x=x, xn=xn, hb=hb, hf=hf, ya=ya, yc=yc, yd=yd, tot=tot, cnt=cnt, qm=qm, km=km, vm=vm, lse=lse)


def _layer_bwd(dx_next, pos, p, s):
    (dya, dyc, dyd, dhf, dw_out, dg_post, dg_grp, dconv_w, dconv_b) = _epilogue_bwd(
        dx_next, s["ya"], s["yc"], s["yd"], s["hf"], p["conv_w"], p["conv_b"], p["group_norm"], p["w_out"],
        p["norm_post"])
    dq_d, dk_d, dv_d = _sb_bwd(s["hb"], s["tot"], s["cnt"], dyd)
    dqm, dkt, dvt = _mla_bwd(s["qm"], s["km"], s["vm"], s["yc"], s["lse"], dyc)
    dc, dw_uq, dw_ukv, dg_q, dg_kv = _mla_prep_bwd(
        s["hf"], pos, p["mla_q_norm"], p["mla_kv_norm"], p["mla_w_uq"], p["mla_w_ukv"], dqm, dkt, dvt)
    dq_a, dk_a, dv_a, dsinks = _swa_bwd(s["hb"], p["attn_sinks"], dya)
    dx, dh, dg_pre = _inproj_bwd_dx(s["x"], p["norm_pre"], p["w_in"], dx_next,
                                    [dq_a, dk_a, dv_a, dq_d, dk_d, dv_d, dhf, dc])
    dwt_in = _grad_over_tokens(s["xn"], dh, "inproj_bwd_dw")
    grads = dict(norm_pre=dg_pre[0], w_in_t=_w_in_real(dwt_in), attn_sinks=dsinks[0, :4], conv_w=dconv_w[:3],
                 conv_b=dconv_b[0], mla_q_norm=dg_q[0], mla_w_uq=_uq_real(dw_uq), mla_kv_norm=dg_kv[0],
                 mla_w_ukv=_ukv_real(dw_ukv), group_norm=dg_grp[0], w_out=dw_out, norm_post=dg_post[0])
    return dx, grads


_WEIGHTS = ["norm_pre", "w_in", "attn_sinks", "conv_w", "conv_b", "mla_q_norm", "mla_w_uq", "mla_kv_norm",
            "mla_w_ukv", "group_norm", "w_out", "norm_post"]


def kernel(x, positions, norm_pre, w_in, attn_sinks, conv_w, conv_b, mla_q_norm, mla_w_uq, mla_kv_norm, mla_w_ukv, group_norm, w_out, norm_post, loss_target, m_norm_pre, m_w_in, m_attn_sinks, m_conv_w, m_conv_b, m_mla_q_norm, m_mla_w_uq, m_mla_kv_norm, m_mla_w_ukv, m_group_norm, m_w_out, m_norm_post, v_norm_pre, v_w_in, v_attn_sinks, v_conv_w, v_conv_b, v_mla_q_norm, v_mla_w_uq, v_mla_kv_norm, v_mla_w_ukv, v_group_norm, v_w_out, v_norm_post):
    w = dict(norm_pre=norm_pre, w_in=w_in, attn_sinks=attn_sinks, conv_w=conv_w, conv_b=conv_b,
             mla_q_norm=mla_q_norm, mla_w_uq=mla_w_uq, mla_kv_norm=mla_kv_norm, mla_w_ukv=mla_w_ukv,
             group_norm=group_norm, w_out=w_out, norm_post=norm_post)
    m = dict(norm_pre=m_norm_pre, w_in=m_w_in, attn_sinks=m_attn_sinks, conv_w=m_conv_w, conv_b=m_conv_b,
             mla_q_norm=m_mla_q_norm, mla_w_uq=m_mla_w_uq, mla_kv_norm=m_mla_kv_norm, mla_w_ukv=m_mla_w_ukv,
             group_norm=m_group_norm, w_out=m_w_out, norm_post=m_norm_post)
    v = dict(norm_pre=v_norm_pre, w_in=v_w_in, attn_sinks=v_attn_sinks, conv_w=v_conv_w, conv_b=v_conv_b,
             mla_q_norm=v_mla_q_norm, mla_w_uq=v_mla_w_uq, mla_kv_norm=v_mla_kv_norm, mla_w_ukv=v_mla_w_ukv,
             group_norm=v_group_norm, w_out=v_w_out, norm_post=v_norm_post)
    T = x.shape[1]
    xs = x[0]
    pos = positions[0].reshape(T, 1)
    tgt = loss_target[0]
    core = lax.axis_index("c")

    gathered = _gather_weights([jnp.swapaxes(w["w_in"], 1, 2).astype(BF16)]
                               + [w[n].astype(BF16) for n in _BIG[1:4]] + [w["conv_w"]])
    full = {}
    for n, got in zip(_BIG, gathered):
        if n in ("w_in", "w_out"):
            full[n] = jnp.moveaxis(got, 0, 1).reshape(DEPTH, 4 * got.shape[2], got.shape[3])
        else:
            full[n] = jnp.transpose(got, (1, 2, 0, 3)).reshape(DEPTH, got.shape[2], 4 * got.shape[3])

    layers = []
    for l in range(DEPTH):
        layers.append(dict(
            norm_pre=norm_pre[l:l + 1], w_in=_w_in_internal(full["w_in"][l]), attn_sinks=attn_sinks[l],
            conv_w=full["conv_w"][l], conv_b=conv_b[l:l + 1], mla_q_norm=mla_q_norm[l:l + 1],
            mla_w_uq=_uq_internal(full["mla_w_uq"][l]), mla_kv_norm=mla_kv_norm[l:l + 1],
            mla_w_ukv=_ukv_internal(full["mla_w_ukv"][l]), group_norm=group_norm[l:l + 1],
            w_out=full["w_out"][l], norm_post=norm_post[l:l + 1]))

    saved = []
    h = xs
    for l in range(DEPTH):
        h, s = _layer_fwd(h, pos, layers[l], tgt if l == DEPTH - 1 else None)
        saved.append(s)
    dy, loss_part = h

    grads = [None] * DEPTH
    for l in reversed(range(DEPTH)):
        dy, grads[l] = _layer_bwd(dy, pos, layers[l], saved[l])

    turned = ("w_in", "mla_w_uq")
    turn = lambda n, a: jnp.swapaxes(a, -1, -2) if n in turned else a

    def chunks(n, a):
        if n in ("w_out", "w_in"):
            return a.reshape(4, a.shape[0] // 4, a.shape[1])
        if n in turned:
            return a.T.reshape(4, a.shape[1] // 4, a.shape[0])
        return jnp.transpose(a.reshape(a.shape[0], 4, a.shape[1] // 4), (1, 0, 2))

    grad = lambda l, n: grads[l]["w_in_t" if n == "w_in" else n]
    mine = [chunks(n, jnp.where(core == 0, grad(0, n), grad(1, n))) for n in _BIG]
    theirs = [chunks(n, jnp.where(core == 0, grad(1, n), grad(0, n))) for n in _BIG]
    from_sibling = _swap_cores(theirs, "swap_layer_chunks")
    summed = [_add(a, b, "add_cores_" + n) for n, a, b in zip(_BIG, mine, from_sibling)]
    small = _pack_small({n: jnp.stack([grads[l][n] for l in range(DEPTH)]) for n in _SMALL}, loss_part[0, 0])
    *got, got_small = _exchange_chips(summed, small)
    done = [_sum_leading(b, "sum_chips_" + n) for n, b in zip(_BIG, got)]
    done_other = _swap_cores(done, "swap_layer_shards")

    outs, loss = _adamw_small(w, m, v, got_small)
    for n, gm, go in zip(_BIG, done, done_other):
        for d, a in zip(outs, _adamw_layers(turn(n, w[n]), turn(n, m[n]), turn(n, v[n]), gm, go, "adamw_" + n)):
            d[n] = turn(n, a)
    return (loss[0, 0], dy[None], *[outs[0][n] for n in _WEIGHTS], *[outs[1][n] for n in _WEIGHTS],
            *[outs[2][n] for n in _WEIGHTS], *[outs[3][n] for n in _WEIGHTS])
```

```python
import math

import jax
import jax.numpy as jnp
from jax import lax
from jax.experimental import pallas as pl
from jax.experimental.pallas import tpu as pltpu

F32 = jnp.float32
BF16 = jnp.bfloat16
MESH = pl.DeviceIdType.MESH

D_MODEL = 1024
DEPTH = 2
EPS = 1e-6
BLOCK = 128
HEAD = 64
LANES = 128
GROUP = 256
LOG2E = 1.4426950408889634
LN2 = 0.6931471805599453
MLA_QSCALE = 96 ** -0.5 * LOG2E
ROPE_HALF = 16
ROPE_THETA = 10000.0
SWA_SUB = 2
ATT_BLK = 256
MLA_BQ = 512
NEG = -1e30
SB_DEAD = -104.0

ADAM_LR, ADAM_B1, ADAM_B2, ADAM_EPS, ADAM_WD, ADAM_STEP = 0.001, 0.9, 0.999, 1e-08, 0.01, 10

_REAL = [("a_q", 256), ("a_k", 128), ("a_v", 128), ("b_b", 256), ("b_c", 256), ("b_x", 256),
         ("c_q", 256), ("c_kv", 128), ("c_kr", 32), ("d_q", 256), ("d_k", 256), ("d_v", 256),
         ("gate", 1024)]
_REAL_OFF = {}
_o = 0
for _n, _w in _REAL:
    _REAL_OFF[_n] = (_o, _w)
    _o += _w
D_IN = _o
_INT_ORDER = ["a_q", "a_k", "a_v", "d_q", "d_k", "d_v", "gate", "b_b", "b_c", "b_x", "c_q", "c_kv", "c_kr"]
_INT_W = dict(_REAL)
_INT_W["c_kr"] = 128
_INT_OFF = {}
_o = 0
for _n in _INT_ORDER:
    _INT_OFF[_n] = _o
    _o += _INT_W[_n]
N_INT = _o
N_HB = _INT_OFF["gate"]
N_HF = N_INT - N_HB

VMEM_LIMIT = 56 * 1024 * 1024


def _cparams(sem):
    return pltpu.CompilerParams(dimension_semantics=sem, vmem_limit_bytes=VMEM_LIMIT)


def _dot(a, b):
    return jnp.dot(a, b, preferred_element_type=F32)


def _dot_nt(a, b):
    return lax.dot_general(a, b, (((1,), (1,)), ((), ())), preferred_element_type=F32)


def _dot_tn(a, b):
    return lax.dot_general(a, b, (((0,), (0,)), ((), ())), preferred_element_type=F32)


def _split(x):
    hi = x.astype(BF16)
    lo = (x - hi.astype(F32)).astype(BF16)
    return hi, lo


def _rms(x):
    return lax.rsqrt(jnp.mean(x * x, axis=-1, keepdims=True) + EPS)


def _rms_bwd(dy, xhat, r, g):
    dxhat = dy * g
    return r * (dxhat - xhat * jnp.mean(dxhat * xhat, axis=-1, keepdims=True)), dy * xhat


def _colsum(x):
    return jnp.sum(x, axis=0, keepdims=True)


def _inproj_fwd(x, g, wt):
    T = x.shape[0]
    tm = 512

    def body(x_ref, g_ref, w_ref, xn_ref, hb_ref, hf_ref):
        xv = x_ref[...]
        xn = (xv * _rms(xv) * g_ref[...]).astype(BF16)
        xn_ref[...] = xn
        h = _dot_nt(xn, w_ref[...])
        hb_ref[...] = h[:, :N_HB].astype(BF16)
        hf_ref[...] = h[:, N_HB:]

    return pl.pallas_call(
        body, name="inproj_fwd", grid=(T // tm,),
        in_specs=[pl.BlockSpec((tm, D_MODEL), lambda i: (i, 0)),
                  pl.BlockSpec((1, D_MODEL), lambda i: (0, 0)),
                  pl.BlockSpec((N_INT, D_MODEL), lambda i: (0, 0))],
        out_specs=[pl.BlockSpec((tm, D_MODEL), lambda i: (i, 0)),
                   pl.BlockSpec((tm, N_HB), lambda i: (i, 0)),
                   pl.BlockSpec((tm, N_HF), lambda i: (i, 0))],
        out_shape=[jax.ShapeDtypeStruct((T, D_MODEL), BF16),
                   jax.ShapeDtypeStruct((T, N_HB), BF16),
                   jax.ShapeDtypeStruct((T, N_HF), F32)],
        compiler_params=_cparams(("parallel",)),
    )(x, g, wt)


def _inproj_bwd_dx(x, g, wt, dx_next, pieces):
    T = x.shape[0]
    tm = 512
    widths = [p.shape[1] for p in pieces]
    assert sum(widths) == N_INT

    def body(x_ref, g_ref, w_ref, dxn_ref, *rest):
        p_refs = rest[:len(pieces)]
        dx_ref, dh_ref, dg_ref = rest[len(pieces):]
        dh = jnp.concatenate([p[...].astype(BF16) for p in p_refs], axis=1)
        dh_ref[...] = dh
        dxn = _dot(dh, w_ref[...])
        xv = x_ref[...]
        r = _rms(xv)
        dx, dgrow = _rms_bwd(dxn, xv * r, r, g_ref[...])
        dx_ref[...] = dx + dxn_ref[...]

        @pl.when(pl.program_id(0) == 0)
        def _():
            dg_ref[...] = jnp.zeros_like(dg_ref)

        dg_ref[...] += _colsum(dgrow)

    return pl.pallas_call(
        body, name="inproj_bwd_dx", grid=(T // tm,),
        in_specs=[pl.BlockSpec((tm, D_MODEL), lambda i: (i, 0)),
                  pl.BlockSpec((1, D_MODEL), lambda i: (0, 0)),
                  pl.BlockSpec((N_INT, D_MODEL), lambda i: (0, 0)),
                  pl.BlockSpec((tm, D_MODEL), lambda i: (i, 0))]
                 + [pl.BlockSpec((tm, wd), lambda i: (i, 0)) for wd in widths],
        out_specs=[pl.BlockSpec((tm, D_MODEL), lambda i: (i, 0)),
                   pl.BlockSpec((tm, N_INT), lambda i: (i, 0)),
                   pl.BlockSpec((1, D_MODEL), lambda i: (0, 0))],
        out_shape=[jax.ShapeDtypeStruct((T, D_MODEL), F32),
                   jax.ShapeDtypeStruct((T, N_INT), BF16),
                   jax.ShapeDtypeStruct((1, D_MODEL), F32)],
        compiler_params=_cparams(("arbitrary",)),
    )(x, g, wt, dx_next, *pieces)


def _grad_over_tokens(a, b, name):
    T, M = a.shape
    N = b.shape[1]
    tm, tn = min(1024, T), 896

    def body(a_ref, b_ref, o_ref):
        @pl.when(pl.program_id(1) == 0)
        def _():
            o_ref[...] = jnp.zeros_like(o_ref)

        o_ref[...] += _dot_tn(b_ref[...], a_ref[...])

    return pl.pallas_call(
        body, name=name, grid=(N // tn, T // tm),
        in_specs=[pl.BlockSpec((tm, M), lambda j, t: (t, 0)),
                  pl.BlockSpec((tm, tn), lambda j, t: (t, j))],
        out_specs=pl.BlockSpec((tn, M), lambda j, t: (j, 0)),
        out_shape=jax.ShapeDtypeStruct((N, M), F32),
        compiler_params=_cparams(("parallel", "arbitrary")),
    )(a, b)


def _roll_f32(x, shift):
    return pltpu.roll(x.astype(F32), shift, 1)


def _swa_operands(h, q, k_prev, k_cur, v_prev, v_cur):
    p, e = h // 2, h % 2
    lane = lax.broadcasted_iota(jnp.int32, (1, LANES), 1) // HEAD
    q = q[:, p * LANES:(p + 1) * LANES]
    if e != p:
        q = _roll_f32(q, HEAD).astype(BF16)
        v_prev = _roll_f32(v_prev, HEAD).astype(BF16)
        v_cur = _roll_f32(v_cur, HEAD).astype(BF16)
    qs = jnp.where(lane == p, q, 0) * 0.125
    return dict(p=p, e=e, lane=lane, qs=qs, k_prev=k_prev, k_cur=k_cur,
                v_prev=jnp.where(lane == e, v_prev, 0), v_cur=jnp.where(lane == e, v_cur, 0),
                s_prev=_dot_nt(qs, k_prev), s_cur=_dot_nt(qs, k_cur))


def _swa_probs(ops, sink, no_prev):
    row = lax.broadcasted_iota(jnp.int32, (BLOCK, BLOCK), 0)
    col = lax.broadcasted_iota(jnp.int32, (BLOCK, BLOCK), 1)
    ok_prev = col > row if no_prev is None else jnp.logical_and(col > row, jnp.logical_not(no_prev))
    s_prev = jnp.where(ok_prev, ops["s_prev"], NEG)
    s_cur = jnp.where(col <= row, ops["s_cur"], NEG)
    m = jnp.maximum(jnp.maximum(jnp.max(s_prev, axis=1, keepdims=True),
                                jnp.max(s_cur, axis=1, keepdims=True)), sink)
    p_prev = jnp.exp(s_prev - m)
    p_cur = jnp.exp(s_cur - m)
    p_sink = jnp.exp(sink - m)
    inv = 1.0 / (jnp.sum(p_prev, axis=1, keepdims=True) + jnp.sum(p_cur, axis=1, keepdims=True) + p_sink)
    return p_prev * inv, p_cur * inv, p_sink * inv


def _swa_specs(T):
    n = T // (BLOCK * SWA_SUB)
    qo, ko, vo = (_INT_OFF[name] // LANES for name in ("a_q", "a_k", "a_v"))
    halo = lambda i: jnp.maximum(i * SWA_SUB - 1, 0)
    return [pl.BlockSpec((BLOCK * SWA_SUB, 256), lambda i: (i, qo // 2)),
            pl.BlockSpec((BLOCK, LANES), lambda i: (halo(i), ko)),
            pl.BlockSpec((BLOCK * SWA_SUB, LANES), lambda i: (i, ko)),
            pl.BlockSpec((BLOCK, LANES), lambda i: (halo(i), vo)),
            pl.BlockSpec((BLOCK * SWA_SUB, LANES), lambda i: (i, vo)),
            pl.BlockSpec(memory_space=pltpu.SMEM)], n


def _swa_units(q_ref, kh_ref, kc_ref, vh_ref, vc_ref, s_ref):
    blk = lambda a: slice(a * BLOCK, (a + 1) * BLOCK)
    units = [(a, h) for a in range(SWA_SUB) for h in range(4)]
    ops = {}
    for a, h in units:
        k_prev, v_prev = (kh_ref[...], vh_ref[...]) if a == 0 else (kc_ref[blk(a - 1), :], vc_ref[blk(a - 1), :])
        ops[a, h] = _swa_operands(h, q_ref[blk(a), :], k_prev, kc_ref[blk(a), :], v_prev, vc_ref[blk(a), :])
    probs = {(a, h): _swa_probs(ops[a, h], s_ref[h], pl.program_id(0) == 0 if a == 0 else None) for a, h in units}
    return units, ops, probs, blk


def _swa_fwd(hb, sinks):
    T = hb.shape[0]
    specs, n = _swa_specs(T)

    def body(q_ref, kh_ref, kc_ref, vh_ref, vc_ref, s_ref, o_ref):
        units, ops, probs, blk = _swa_units(q_ref, kh_ref, kc_ref, vh_ref, vc_ref, s_ref)
        outs = {u: _dot(probs[u][0].astype(BF16), ops[u]["v_prev"]) + _dot(probs[u][1].astype(BF16), ops[u]["v_cur"])
                for u in units}
        for a in range(SWA_SUB):
            for p in range(2):
                o_ref[blk(a), p * LANES:(p + 1) * LANES] = outs[a, 2 * p] + outs[a, 2 * p + 1]

    return pl.pallas_call(
        body, name="swa_fwd", grid=(n,), in_specs=specs,
        out_specs=pl.BlockSpec((BLOCK * SWA_SUB, 256), lambda i: (i, 0)),
        out_shape=jax.ShapeDtypeStruct((T, 256), F32),
        compiler_params=_cparams(("parallel",)),
    )(hb, hb, hb, hb, hb, sinks)


def _swa_bwd(hb, sinks, dy):
    T = hb.shape[0]
    specs, n = _swa_specs(T)

    def body(q_ref, kh_ref, kc_ref, vh_ref, vc_ref, s_ref, dy_ref, dq_ref, dk_ref, dv_ref, ds_ref):
        i = pl.program_id(0)

        @pl.when(i == 0)
        def _():
            ds_ref[...] = jnp.zeros_like(ds_ref)

        lane_id = lax.broadcasted_iota(jnp.int32, (8, LANES), 1)
        units, ops, probs, blk = _swa_units(q_ref, kh_ref, kc_ref, vh_ref, vc_ref, s_ref)
        dos = {(a, h): jnp.where(ops[a, h]["lane"] == ops[a, h]["e"],
                                 dy_ref[blk(a), ops[a, h]["p"] * LANES:(ops[a, h]["p"] + 1) * LANES], 0.0)
               for a, h in units}
        dobs = {u: dos[u].astype(BF16) for u in units}
        pbs = {u: (probs[u][0].astype(BF16), probs[u][1].astype(BF16)) for u in units}
        outs = {u: _dot(pbs[u][0], ops[u]["v_prev"]) + _dot(pbs[u][1], ops[u]["v_cur"]) for u in units}
        dps = {u: (_dot_nt(dobs[u], ops[u]["v_prev"]), _dot_nt(dobs[u], ops[u]["v_cur"])) for u in units}
        dss, dsinks = {}, jnp.zeros((8, LANES), F32)
        for u in units:
            delta = jnp.sum(dos[u] * outs[u], axis=1, keepdims=True)
            dss[u] = ((probs[u][0] * (dps[u][0] - delta)).astype(BF16),
                      (probs[u][1] * (dps[u][1] - delta)).astype(BF16))
            dsink = -jnp.sum(probs[u][2] * delta, axis=0, keepdims=True)
            dsinks += jnp.where(lane_id == u[1], dsink, 0.0)
        ds_ref[...] += dsinks
        dqs = {u: (_dot(dss[u][0], ops[u]["k_prev"]) + _dot(dss[u][1], ops[u]["k_cur"])) * 0.125 for u in units}
        zero = jnp.zeros((BLOCK, LANES), F32)
        dk_as_prev, dk_as_cur = [zero] * SWA_SUB, [zero] * SWA_SUB
        dv_as_prev, dv_as_cur = [zero] * SWA_SUB, [zero] * SWA_SUB
        for a, h in units:
            p, e = ops[a, h]["p"], ops[a, h]["e"]
            dob_v = dobs[a, h] if e == p else pltpu.roll(dos[a, h], HEAD, 1).astype(BF16)
            dk_as_prev[a] = dk_as_prev[a] + _dot_tn(dss[a, h][0], ops[a, h]["qs"])
            dk_as_cur[a] = dk_as_cur[a] + _dot_tn(dss[a, h][1], ops[a, h]["qs"])
            dv_as_prev[a] = dv_as_prev[a] + _dot_tn(pbs[a, h][0], dob_v)
            dv_as_cur[a] = dv_as_cur[a] + _dot_tn(pbs[a, h][1], dob_v)
        base = i * SWA_SUB
        for a in range(SWA_SUB):
            rows = pl.ds(pl.multiple_of((base + a) * BLOCK, BLOCK), BLOCK)
            more = a + 1 < SWA_SUB
            dk_ref[rows, :] = dk_as_cur[a] + (dk_as_prev[a + 1] if more else 0.0)
            dv_ref[rows, :] = dv_as_cur[a] + (dv_as_prev[a + 1] if more else 0.0)
        halo = pl.ds(pl.multiple_of(jnp.maximum(base - 1, 0) * BLOCK, BLOCK), BLOCK)
        dk_ref[halo, :] += dk_as_prev[0]
        dv_ref[halo, :] += dv_as_prev[0]
        for a in range(SWA_SUB):
            for p in range(2):
                dq_pair = jnp.zeros((BLOCK, LANES), F32)
                for e in range(2):
                    dq = jnp.where(ops[a, 2 * p + e]["lane"] == p, dqs[a, 2 * p + e], 0.0)
                    dq_pair += dq if e == p else pltpu.roll(dq, HEAD, 1)
                dq_ref[blk(a), p * LANES:(p + 1) * LANES] = dq_pair.astype(BF16)

    return pl.pallas_call(
        body, name="swa_bwd", grid=(n,),
        in_specs=specs + [pl.BlockSpec((BLOCK * SWA_SUB, 256), lambda i: (i, 0))],
        out_specs=[pl.BlockSpec((BLOCK * SWA_SUB, 256), lambda i: (i, 0)),
                   pl.BlockSpec((T, LANES), lambda i: (0, 0)),
                   pl.BlockSpec((T, LANES), lambda i: (0, 0)),
                   pl.BlockSpec((8, LANES), lambda i: (0, 0))],
        out_shape=[jax.ShapeDtypeStruct((T, 256), BF16),
                   jax.ShapeDtypeStruct((T, LANES), F32),
                   jax.ShapeDtypeStruct((T, LANES), F32),
                   jax.ShapeDtypeStruct((8, LANES), F32)],
        compiler_params=_cparams(("arbitrary",)),
    )(hb, hb, hb, hb, hb, sinks, dy)


def _rope_tables(pos):
    T = pos.shape[0]
    tm = 512

    def body(pos_ref, o_ref):
        lane = lax.broadcasted_iota(jnp.int32, (1, LANES), 1)
        active = jnp.logical_and(lane >= HEAD, lane < HEAD + 2 * ROPE_HALF)
        idx = ((lane - HEAD) % ROPE_HALF).astype(F32)
        freq = jnp.exp(idx * (-math.log(ROPE_THETA) / ROPE_HALF))
        ang = pos_ref[...].astype(F32) * freq
        cos, sin = jnp.cos(ang), jnp.sin(ang)
        o_ref[:, 0:LANES] = jnp.where(active, cos, 1.0)
        o_ref[:, LANES:2 * LANES] = jnp.where(jnp.logical_and(active, lane >= HEAD + ROPE_HALF), sin, 0.0)
        o_ref[:, 2 * LANES:] = jnp.where(jnp.logical_and(active, lane < HEAD + ROPE_HALF), -sin, 0.0)

    return pl.pallas_call(
        body, name="rope_tables", grid=(T // tm,),
        in_specs=[pl.BlockSpec((tm, 1), lambda i: (i, 0))],
        out_specs=pl.BlockSpec((tm, 3 * LANES), lambda i: (i, 0)),
        out_shape=jax.ShapeDtypeStruct((T, 3 * LANES), F32),
        compiler_params=_cparams(("parallel",)),
    )(pos)


def _rope_factors(tab_ref):
    return tab_ref[:, 0:LANES], tab_ref[:, LANES:2 * LANES], tab_ref[:, 2 * LANES:]


def _rope(x, tabs):
    c, s_up, s_dn = tabs
    return x * c + pltpu.roll(x, ROPE_HALF, 1) * s_up + pltpu.roll(x, LANES - ROPE_HALF, 1) * s_dn


def _rope_t(dy, tabs):
    c, s_up, s_dn = tabs
    return dy * c + pltpu.roll(dy * s_up, LANES - ROPE_HALF, 1) + pltpu.roll(dy * s_dn, ROPE_HALF, 1)


def _mla_lat_specs(tm):
    cq, ckv, ckr = ((_INT_OFF[n] - N_HB) for n in ("c_q", "c_kv", "c_kr"))
    return [pl.BlockSpec((tm, 256), lambda i: (i, cq // 256)),
            pl.BlockSpec((tm, LANES), lambda i: (i, ckv // LANES)),
            pl.BlockSpec((tm, LANES), lambda i: (i, ckr // LANES)),
            pl.BlockSpec((tm, 3 * LANES), lambda i: (i, 0)),
            pl.BlockSpec((1, 256), lambda i: (0, 0)),
            pl.BlockSpec((1, LANES), lambda i: (0, 0)),
            pl.BlockSpec((256, 512), lambda i: (0, 0)),
            pl.BlockSpec((LANES, 768), lambda i: (0, 0))]


def _mla_prep_fwd(hf, rope, g_q, g_kv, w_uq, w_ukv):
    T = hf.shape[0]
    tm = 512
    sub = tm // ATT_BLK

    def body(cq_ref, ckv_ref, ckr_ref, tab_ref, gq_ref, gkv_ref, wq_ref, wkv_ref, qm_ref, km_ref, vm_ref, vt_ref):
        tabs = _rope_factors(tab_ref)
        cq = cq_ref[...]
        q = _dot((cq * _rms(cq) * gq_ref[...]).astype(BF16), wq_ref[...])
        ckv = ckv_ref[...]
        kv = _dot((ckv * _rms(ckv) * gkv_ref[...]).astype(BF16), wkv_ref[...])
        kr = _rope(pltpu.roll(ckr_ref[...], HEAD, 1), tabs)
        for h in range(4):
            sl = slice(h * LANES, (h + 1) * LANES)
            qm_ref[:, sl] = (_rope(q[:, sl], tabs) * MLA_QSCALE).astype(BF16)
            km_ref[:, sl] = (kv[:, sl] + kr).astype(BF16)
        vm_ref[...] = kv[:, 512:].astype(BF16)
        for p in range(2):
            for s in range(sub):
                tile = kv[s * ATT_BLK:(s + 1) * ATT_BLK, 512 + p * LANES:512 + (p + 1) * LANES]
                vt_ref[p, s] = jnp.transpose(tile).astype(BF16)

    return pl.pallas_call(
        body, name="mla_prep_fwd", grid=(T // tm,), in_specs=_mla_lat_specs(tm),
        out_specs=[pl.BlockSpec((tm, 512), lambda i: (i, 0)),
                   pl.BlockSpec((tm, 512), lambda i: (i, 0)),
                   pl.BlockSpec((tm, 256), lambda i: (i, 0)),
                   pl.BlockSpec((2, sub, LANES, ATT_BLK), lambda i: (0, i, 0, 0))],
        out_shape=[jax.ShapeDtypeStruct((T, 512), BF16),
                   jax.ShapeDtypeStruct((T, 512), BF16),
                   jax.ShapeDtypeStruct((T, 256), BF16),
                   jax.ShapeDtypeStruct((2, T // ATT_BLK, LANES, ATT_BLK), BF16)],
        compiler_params=_cparams(("parallel",)),
    )(hf, hf, hf, rope, g_q, g_kv, w_uq, w_ukv)


def _mla_prep_bwd(hf, rope, g_q, g_kv, w_uq, w_ukv, dqm, dkt, dvt):
    T = hf.shape[0]
    tm = 512
    sub = tm // ATT_BLK

    def body(cq_ref, ckv_ref, ckr_ref, tab_ref, gq_ref, gkv_ref, wq_ref, wkv_ref, dq_ref, dk_ref, dv_ref,
             dc_ref, dwq_ref, dwkv_ref, dgq_ref, dgkv_ref):
        @pl.when(pl.program_id(0) == 0)
        def _():
            dwq_ref[...] = jnp.zeros_like(dwq_ref)
            dwkv_ref[...] = jnp.zeros_like(dwkv_ref)
            dgq_ref[...] = jnp.zeros_like(dgq_ref)
            dgkv_ref[...] = jnp.zeros_like(dgkv_ref)

        tabs = _rope_factors(tab_ref)
        lane =lax.broadcasted_iota(jnp.int32, (1, LANES), 1)
        dq = jnp.concatenate([_rope_t(dq_ref[:, h * LANES:(h + 1) * LANES] * MLA_QSCALE, tabs)
                              for h in range(4)], axis=1).astype(BF16)
        cq = cq_ref[...]
        rq = _rms(cq)
        cqn = (cq * rq * gq_ref[...]).astype(BF16)
        dwq_ref[...] += _dot_tn(cqn, dq)
        dcq, dgrow = _rms_bwd(_dot_nt(dq, wq_ref[...]), cq * rq, rq, gq_ref[...])
        dgq_ref[...] += _colsum(dgrow)
        dc_ref[:, 0:256] = dcq.astype(BF16)

        dk = jnp.concatenate([jnp.concatenate([jnp.transpose(dk_ref[p, s]) for p in range(2)], axis=1)
                              for s in range(sub)], axis=0) * LN2
        dv = jnp.concatenate([jnp.concatenate([jnp.transpose(dv_ref[p, s]) for p in range(2)], axis=1)
                              for s in range(sub)], axis=0)
        dkr = dk[:, 0:LANES] + dk[:, LANES:2 * LANES] + dk[:, 2 * LANES:3 * LANES] + dk[:, 3 * LANES:]
        dkr = pltpu.roll(_rope_t(dkr, tabs), HEAD, 1)
        dc_ref[:, 384:512] = jnp.where(lane < 2 * ROPE_HALF, dkr, 0.0).astype(BF16)
        dkv = jnp.concatenate([dk.astype(BF16), dv.astype(BF16)], axis=1)
        ckv = ckv_ref[...]
        rkv = _rms(ckv)
        ckvn = (ckv * rkv * gkv_ref[...]).astype(BF16)
        dwkv_ref[...] += _dot_tn(ckvn, dkv)
        dckv, dgrow = _rms_bwd(_dot_nt(dkv, wkv_ref[...]), ckv * rkv, rkv, gkv_ref[...])
        dgkv_ref[...] += _colsum(dgrow)
        dc_ref[:, 256:384] = dckv.astype(BF16)

    return pl.pallas_call(
        body, name="mla_prep_bwd", grid=(T // tm,),
        in_specs=_mla_lat_specs(tm) + [pl.BlockSpec((tm, 512), lambda i: (i, 0)),
                                       pl.BlockSpec((2, sub, 256, ATT_BLK), lambda i: (0, i, 0, 0)),
                                       pl.BlockSpec((2, sub, LANES, ATT_BLK), lambda i: (0, i, 0, 0))],
        out_specs=[pl.BlockSpec((tm, 512), lambda i: (i, 0)),
                   pl.BlockSpec((256, 512), lambda i: (0, 0)),
                   pl.BlockSpec((LANES, 768), lambda i: (0, 0)),
                   pl.BlockSpec((1, 256), lambda i: (0, 0)),
                   pl.BlockSpec((1, LANES), lambda i: (0, 0))],
        out_shape=[jax.ShapeDtypeStruct((T, 512), BF16),
                   jax.ShapeDtypeStruct((256, 512), F32),
                   jax.ShapeDtypeStruct((LANES, 768), F32),
                   jax.ShapeDtypeStruct((1, 256), F32),
                   jax.ShapeDtypeStruct((1, LANES), F32)],
        compiler_params=_cparams(("arbitrary",)),
    )(hf, hf, hf, rope, g_q, g_kv, w_uq, w_ukv, dqm, dkt, dvt)


def _causal_masks(bq, bk):
    row = lax.broadcasted_iota(jnp.int32, (bq, bk), 0)
    col = lax.broadcasted_iota(jnp.int32, (bq, bk), 1)
    return row, col


def _mla_fwd(qm, km, vt):
    T = qm.shape[0]
    bq, bk = min(MLA_BQ, T), ATT_BLK
    nq, nsub, nk = T // bq, bq // bk, T // bk

    def body(q_ref, k_ref, vt_ref, o_ref, lse_ref, acc_ref, m_ref, l_ref):
        qi = pl.program_id(0)
        key = lax.broadcasted_iota(jnp.int32, (bk, bq), 0)
        qry = lax.broadcasted_iota(jnp.int32, (bk, bq), 1)
        ones = jnp.ones((8, bk), BF16)
        acc_ref[...] = jnp.zeros_like(acc_ref)
        m_ref[...] = jnp.full_like(m_ref, NEG)
        l_ref[...] = jnp.zeros_like(l_ref)

        def step(kb0, masked):
            kbs = [kb0 + d for d in range(nsub)]
            sts = [[_dot_nt(k_ref[pl.ds(pl.multiple_of(kb * bk, bk), bk), e * LANES:(e + 1) * LANES],
                            q_ref[:, e * LANES:(e + 1) * LANES]) for kb in kbs] for e in range(4)]
            pts, alphas = [], []
            for e in range(4):
                st = [jnp.where(key + d * bk <= qry, sts[e][d], NEG) for d in range(nsub)] if masked else sts[e]
                m_prev = m_ref[e, 0:1, :]
                m_new = m_prev
                for d in range(nsub):
                    m_new = jnp.maximum(m_new, jnp.max(st[d], axis=0, keepdims=True))
                alpha = jnp.exp2(m_prev - m_new)
                pt = [jnp.exp2(st[d] - m_new).astype(BF16) for d in range(nsub)]
                l_new = alpha * l_ref[e]
                for d in range(nsub):
                    l_new = l_new + _dot(ones, pt[d])
                l_ref[e] = l_new
                m_ref[e] = jnp.broadcast_to(m_new, (8, bq))
                pts.append(pt)
                alphas.append(alpha)
            for e in range(4):
                acc = alphas[e] * acc_ref[e]
                for d in range(nsub):
                    v_t = vt_ref[e // 2, kbs[d], (e % 2) * HEAD:(e % 2 + 1) * HEAD, :]
                    acc = acc + _dot(v_t, pts[e][d])
                acc_ref[e] = acc

        step(qi * nsub, True)

        def loop(t, c):
            step(t * nsub, False)
            return c

        lax.fori_loop(0, qi, loop, 0)
        outs, lses = [], []
        for e in range(4):
            l = l_ref[e, 0:1, :]
            outs.append(acc_ref[e] / l)
            lses.append(jnp.broadcast_to(m_ref[e, 0:1, :] * LN2 + jnp.log(l), (HEAD, bq)))
        o_ref[...] = jnp.transpose(jnp.concatenate(outs, axis=0))
        lse_ref[...] = jnp.transpose(jnp.concatenate(lses, axis=0))

    return pl.pallas_call(
        body, name="mla_fwd", grid=(nq,),
        in_specs=[pl.BlockSpec((bq, 512), lambda i: (i, 0)),
                  pl.BlockSpec((T, 512), lambda i: (0, 0)),
                  pl.BlockSpec((2, nk, LANES, bk), lambda i: (0, 0, 0, 0))],
        out_specs=[pl.BlockSpec((bq, 256), lambda i: (i, 0)),
                   pl.BlockSpec((bq, 256), lambda i: (i, 0))],
        out_shape=[jax.ShapeDtypeStruct((T, 256), F32), jax.ShapeDtypeStruct((T, 256), F32)],
        scratch_shapes=[pltpu.VMEM((4, HEAD, bq), F32), pltpu.VMEM((4, 8, bq), F32), pltpu.VMEM((4, 8, bq), F32)],
        compiler_params=_cparams(("arbitrary",)),
    )(qm, km, vt)


def _mla_bwd(qm, km, vm, y, lse, dy):
    T = qm.shape[0]
    bq, bk = min(MLA_BQ, T), ATT_BLK
    nq, nsub, nk = T // bq, bq // bk, T // bk

    def body(q_ref, k_ref, v_ref, y_ref, lse_ref, dy_ref, dq_ref, dkt_ref, dvt_ref, dob_ref, st_ref, qt_ref, dot_ref):
        qi = pl.program_id(1)

        @pl.when(qi == 0)
        def _():
            dkt_ref[...] = jnp.zeros_like(dkt_ref)
            dvt_ref[...] = jnp.zeros_like(dvt_ref)

        lane = lax.broadcasted_iota(jnp.int32, (1, LANES), 1) // HEAD
        row, col = _causal_masks(bq, bk)
        dq_ref[...] = jnp.zeros_like(dq_ref)
        lse = lse_ref[...]
        lse_other = pltpu.roll(lse, HEAD, 1)
        qt_ref[...] = jnp.transpose(q_ref[...].astype(F32)).astype(BF16)
        dot_ref[...] = jnp.transpose(dy_ref[...]).astype(BF16)
        for e in range(2):
            do = jnp.where(lane == e, dy_ref[...], 0.0)
            dob_ref[e] = do.astype(BF16)
            st_ref[2 * e] = jnp.where(lane == e, lse, lse_other) * LOG2E
            st_ref[2 * e + 1] = jnp.broadcast_to(jnp.sum(do * y_ref[...], axis=1, keepdims=True), (bq, LANES))

        hss = [slice(e * LANES, (e + 1) * LANES) for e in range(2)]
        tile = lambda a: jnp.concatenate([a] * (bk // LANES), axis=1)

        def step(kb0, masked):
            kbs = [kb0 + d for d in range(nsub)]
            rows = [pl.ds(pl.multiple_of(kb * bk, bk), bk) for kb in kbs]
            pairs = [(d, e) for d in range(nsub) for e in range(2)]
            ss = {(d, e): _dot_nt(q_ref[:, hss[e]], k_ref[rows[d], hss[e]]) for d, e in pairs}
            dps = {(d, e): _dot_nt(dob_ref[e], jnp.where(lane == e, v_ref[rows[d], :], 0)) for d, e in pairs}
            ps, dss = {}, {}
            for d, e in pairs:
                s = jnp.where(col + d * bk <= row, ss[d, e], NEG) if masked else ss[d, e]
                p = jnp.exp2(s - tile(st_ref[2 * e]))
                dss[d, e] = (p * (dps[d, e] - tile(st_ref[2 * e + 1]))).astype(BF16)
                ps[d, e] = p.astype(BF16)
            for d, e in pairs:
                dvt_ref[0, kbs[d], e * HEAD:(e + 1) * HEAD, :] += _dot(dot_ref[e * HEAD:(e + 1) * HEAD, :], ps[d, e])
            for d, e in pairs:
                dkt_ref[0, kbs[d], hss[e], :] += _dot(qt_ref[hss[e], :], dss[d, e])
            for e in range(2):
                dq = dq_ref[:, hss[e]]
                for d in range(nsub):
                    dq = dq + _dot(dss[d, e], k_ref[rows[d], hss[e]])
                dq_ref[:, hss[e]] = dq

        step(qi * nsub, True)

        def loop(t, c):
            step(t * nsub, False)
            return c

        lax.fori_loop(0, qi, loop, 0)
        dq_ref[...] *= LN2

    return pl.pallas_call(
        body, name="mla_bwd", grid=(2, nq),
        in_specs=[pl.BlockSpec((bq, 256), lambda j, i: (i, j)),
                  pl.BlockSpec((T, 256), lambda j, i: (0, j)),
                  pl.BlockSpec((T, LANES), lambda j, i: (0, j)),
                  pl.BlockSpec((bq, LANES), lambda j, i: (i, j)),
                  pl.BlockSpec((bq, LANES), lambda j, i: (i, j)),
                  pl.BlockSpec((bq, LANES), lambda j, i: (i, j))],
        out_specs=[pl.BlockSpec((bq, 256), lambda j, i: (i, j)),
                   pl.BlockSpec((1, nk, 256, bk), lambda j, i: (j, 0, 0, 0)),
                   pl.BlockSpec((1, nk, LANES, bk), lambda j, i: (j, 0, 0, 0))],
        out_shape=[jax.ShapeDtypeStruct((T, 512), F32),
                   jax.ShapeDtypeStruct((2, nk, 256, bk), F32),
                   jax.ShapeDtypeStruct((2, nk, LANES, bk), F32)],
        scratch_shapes=[pltpu.VMEM((2, bq, LANES), BF16), pltpu.VMEM((4, bq, LANES), F32),
                        pltpu.VMEM((256, bq), BF16), pltpu.VMEM((LANES, bq), BF16)],
        compiler_params=_cparams(("parallel", "arbitrary")),
    )(qm, km, vm, y, lse, dy)


def _suffix_ones(n):
    r = lax.broadcasted_iota(jnp.int32, (n, n), 0)
    c = lax.broadcasted_iota(jnp.int32, (n, n), 1)
    return (r >= c).astype(BF16)


def _prefix_ones(n):
    r = lax.broadcasted_iota(jnp.int32, (n, n), 0)
    c = lax.broadcasted_iota(jnp.int32, (n, n), 1)
    return (r <= c).astype(BF16)


def _sb_specs(T, bq):
    qo, ko, vo = (_INT_OFF[n] // 256 for n in ("d_q", "d_k", "d_v"))
    return [pl.BlockSpec((bq, 256), lambda i: (i, qo)),
            pl.BlockSpec((T, 256), lambda i: (0, ko)),
            pl.BlockSpec((T, 256), lambda i: (0, vo))]


def _sb_fwd(hb):
    T = hb.shape[0]
    bq = bk = ATT_BLK
    nq = T // bq

    def body(q_ref, k_ref, v_ref, o_ref, tot_ref, cnt_ref, qm_ref, car_ref):
        qi = pl.program_id(0)
        lane = lax.broadcasted_iota(jnp.int32, (1, LANES), 1) // HEAD
        row, col = _causal_masks(bq, bk)
        strict = col < row
        u = _suffix_ones(bk)
        o_ref[...] = jnp.zeros_like(o_ref)
        car_ref[...] = jnp.zeros_like(car_ref)
        pair = lambda h: slice((h // 2) * LANES, (h // 2 + 1) * LANES)
        for h in range(4):
            qm_ref[h] = jnp.where(lane == h % 2, q_ref[:, pair(h)], 0) * 0.125

        def step(blocks):
            tile = lambda a: jnp.concatenate([a] * (bk // LANES), axis=1)
            rows = [pl.ds(pl.multiple_of(kb * bk, bk), bk) for kb, _ in blocks]
            pairs = [(b, h) for b in range(len(blocks)) for h in range(4)]
            zs = {(b, h): _dot_nt(qm_ref[h], k_ref[rows[b], pair(h)]) for b, h in pairs}
            splits = {}
            for b, h in pairs:
                z = zs[b, h]
                lk = jnp.minimum(-z, 0.0) - jnp.log(1.0 + jnp.exp(-jnp.abs(z)))
                if blocks[b][1] is not None:
                    lk = jnp.where(blocks[b][1], lk, 0.0)
                splits[b, h] = _split(lk)
            sufs = {bh: _dot(hi, u) + _dot(lo, u) for bh, (hi, lo) in splits.items()}
            car = [car_ref[h] for h in range(4)]
            aas = {}
            for b, h in pairs:
                a = jnp.exp(zs[b, h] + sufs[b, h] + tile(car[h]))
                if blocks[b][1] is not None:
                    a = jnp.where(blocks[b][1], a, 0.0)
                aas[b, h] = a.astype(BF16)
                car[h] = car[h] + jnp.broadcast_to(sufs[b, h][:, 0:1], (bq, LANES))
            acc = [o_ref[:, pair(0)], o_ref[:, pair(2)]]
            for b, h in pairs:
                acc[h // 2] = acc[h // 2] + _dot(aas[b, h], jnp.where(lane == h % 2, v_ref[rows[b], pair(h)], 0))
            o_ref[:, pair(0)], o_ref[:, pair(2)] = acc
            for h in range(4):
                car_ref[h] = car[h]

        step([(qi, strict), (jnp.maximum(qi - 1, 0), qi > 0)])

        def live():
            worst = jnp.maximum(jnp.maximum(car_ref[0], car_ref[1]), jnp.maximum(car_ref[2], car_ref[3]))
            return jnp.max(worst) >= SB_DEAD

        def cond(c):
            return jnp.logical_and(c[0] < qi, c[1])

        def loop(c):
            step([(qi - 1 - c[0], None)])
            return c[0] + 1, live()

        done, _ = lax.while_loop(cond, loop, (jnp.minimum(qi, 1), live()))
        tot_ref[:, pair(0)] = jnp.where(lane == 0, car_ref[0], car_ref[1])
        tot_ref[:, pair(2)] = jnp.where(lane == 0, car_ref[2], car_ref[3])
        cnt_ref[0, qi] = done.astype(F32)

    return pl.pallas_call(
        body, name="sb_fwd", grid=(nq,), in_specs=_sb_specs(T, bq),
        out_specs=[pl.BlockSpec((bq, 256), lambda i: (i, 0)), pl.BlockSpec((bq, 256), lambda i: (i, 0)),
                   pl.BlockSpec(memory_space=pltpu.SMEM)],
        out_shape=[jax.ShapeDtypeStruct((T, 256), F32), jax.ShapeDtypeStruct((T, 256), F32),
                   jax.ShapeDtypeStruct((1, nq), F32)],
        scratch_shapes=[pltpu.VMEM((4, bq, LANES), BF16), pltpu.VMEM((4, bq, LANES), F32)],
        compiler_params=_cparams(("arbitrary",)),
    )(hb, hb, hb)


def _sb_bwd(hb, tot, cnt, dy):
    T = hb.shape[0]
    bq = bk = ATT_BLK
    nq = T // bq

    def body(q_ref, k_ref, v_ref, tot_ref, dy_ref, cnt_ref, dq_ref, dk_ref, dv_ref, qm_ref, dob_ref, dqa_ref, rem_ref,
             cg_ref):
        qi = pl.program_id(0)

        @pl.when(qi == 0)
        def _():
            dk_ref[...] = jnp.zeros_like(dk_ref)
            dv_ref[...] = jnp.zeros_like(dv_ref)

        lane = lax.broadcasted_iota(jnp.int32, (1, LANES), 1) // HEAD
        row, col = _causal_masks(bq, bk)
        strict = col < row
        u = _prefix_ones(bk)
        pair = lambda h: slice((h // 2) * LANES, (h // 2 + 1) * LANES)
        dqa_ref[...] = jnp.zeros_like(dqa_ref)
        cg_ref[...] = jnp.zeros_like(cg_ref)
        for h in range(4):
            tot = tot_ref[:, pair(h)]
            qm_ref[h] = jnp.where(lane == h % 2, q_ref[:, pair(h)], 0) * 0.125
            dob_ref[h] = jnp.where(lane == h % 2, dy_ref[:, pair(h)], 0.0).astype(BF16)
            rem_ref[h] = jnp.where(lane == h % 2, tot, pltpu.roll(tot, HEAD, 1))

        def step(blocks):
            tile = lambda a: jnp.concatenate([a] * (bk // LANES), axis=1)
            nb = len(blocks)
            rows = [pl.ds(pl.multiple_of(kb * bk, bk), bk) for kb, _ in blocks]
            pairs = [(b, h) for b in range(nb) for h in range(4)]
            mask = lambda b, x: x if blocks[b][1] is None else jnp.where(blocks[b][1], x, 0.0)
            zs = {(b, h): _dot_nt(qm_ref[h], k_ref[rows[b], pair(h)]) for b, h in pairs}
            das = {(b, h): _dot_nt(dob_ref[h], jnp.where(lane == h % 2, v_ref[rows[b], pair(h)], 0)) for b, h in pairs}
            zls, splits = {}, {}
            for b, h in pairs:
                z = zs[b, h]
                lk = mask(b, jnp.minimum(-z, 0.0) - jnp.log(1.0 + jnp.exp(-jnp.abs(z))))
                zls[b, h] = z + lk
                splits[b, h] = _split(lk)
            pres = {bh: _dot(hi, u) + _dot(lo, u) for bh, (hi, lo) in splits.items()}
            rem = [rem_ref[h] for h in range(4)]
            aas, gs, gsplits = {}, {}, {}
            for b, h in pairs:
                a = mask(b, jnp.exp(zls[b, h] + (tile(rem[h]) - pres[b, h])))
                gs[b, h] = a * das[b, h]
                aas[b, h] = a.astype(BF16)
                gsplits[b, h] = _split(gs[b, h])
                rem[h] = rem[h] - jnp.broadcast_to(pres[b, h][:, bk - 1:bk], (bq, LANES))
            for b in range(nb):
                for p in (0, 2):
                    dv_ref[rows[b], pair(p)] += _dot_tn(aas[b, p], dob_ref[p]) + _dot_tn(aas[b, p + 1], dob_ref[p + 1])
            gpres = {bh: _dot(hi, u) + _dot(lo, u) for bh, (hi, lo) in gsplits.items()}
            cg = [cg_ref[h] for h in range(4)]
            dzs = {}
            for b, h in pairs:
                dz = mask(b, gs[b, h] - jnp.exp(zls[b, h]) * (tile(cg[h]) + gpres[b, h]))
                dzs[b, h] = dz.astype(BF16)
                cg[h] = cg[h] + jnp.broadcast_to(gpres[b, h][:, bk - 1:bk], (bq, LANES))
            for b in range(nb):
                for p in (0, 2):
                    dk_ref[rows[b], pair(p)] += _dot_tn(dzs[b, p], qm_ref[p]) + _dot_tn(dzs[b, p + 1], qm_ref[p + 1])
            for h in range(4):
                dq = dqa_ref[h]
                for b in range(nb):
                    dq = dq + _dot(dzs[b, h], k_ref[rows[b], pair(h)])
                dqa_ref[h] = dq
                rem_ref[h] = rem[h]
                cg_ref[h] = cg[h]

        def loop(kb, c):
            step([(kb, None)])
            return c

        start = qi - jnp.clip(cnt_ref[0, qi].astype(jnp.int32), 0, qi)
        lax.fori_loop(start, qi - 1, loop, 0)
        step([(jnp.maximum(qi - 1, 0), qi > 0), (qi, strict)])
        for p in (0, 2):
            dq_ref[:, pair(p)] = (jnp.where(lane == 0, dqa_ref[p], dqa_ref[p + 1]) * 0.125).astype(BF16)

    return pl.pallas_call(
        body, name="sb_bwd", grid=(nq,),
        in_specs=_sb_specs(T, bq) + [pl.BlockSpec((bq, 256), lambda i: (i, 0)),
                                     pl.BlockSpec((bq, 256), lambda i: (i, 0)),
                                     pl.BlockSpec(memory_space=pltpu.SMEM)],
        out_specs=[pl.BlockSpec((bq, 256), lambda i: (i, 0)),
                   pl.BlockSpec((T, 256), lambda i: (0, 0)),
                   pl.BlockSpec((T, 256), lambda i: (0, 0))],
        out_shape=[jax.ShapeDtypeStruct((T, 256), BF16)] + [jax.ShapeDtypeStruct((T, 256), F32)] * 2,
        scratch_shapes=[pltpu.VMEM((4, bq, LANES), BF16), pltpu.VMEM((4, bq, LANES), BF16),
                        pltpu.VMEM((4, bq, LANES), F32), pltpu.VMEM((4, bq, LANES), F32),
                        pltpu.VMEM((4, bq, LANES), F32)],
        compiler_params=_cparams(("arbitrary",)),
    )(hb, hb, hb, tot, dy, cnt)


EP_TM = 512


def _ep_in_specs(tm, rev):
    idx = (lambda i: rev - i) if rev is not None else (lambda i: i)
    bo = (_INT_OFF["b_b"] - N_HB) // 256
    halo = lambda i: jnp.maximum(idx(i) * (tm // 8) - 1, 0)
    return [pl.BlockSpec((tm, 256), lambda i: (idx(i), 0)),
            pl.BlockSpec((tm, 256), lambda i: (idx(i), 0)),
            pl.BlockSpec((tm, 256), lambda i: (idx(i), 0)),
            pl.BlockSpec((tm, D_MODEL), lambda i: (idx(i), 0)),
            pl.BlockSpec((tm, 256), lambda i: (idx(i), bo)),
            pl.BlockSpec((tm, 256), lambda i: (idx(i), bo + 1)),
            pl.BlockSpec((tm, 256), lambda i: (idx(i), bo + 2)),
            pl.BlockSpec((8, 256), lambda i: (halo(i), bo + 1)),
            pl.BlockSpec((8, 256), lambda i: (halo(i), bo + 2)),
            pl.BlockSpec((3, 256), lambda i: (0, 0)),
            pl.BlockSpec((1, 256), lambda i: (0, 0)),
            pl.BlockSpec((1, D_MODEL), lambda i: (0, 0)),
            pl.BlockSpec((D_MODEL, D_MODEL), lambda i: (0, 0)),
            pl.BlockSpec((1, D_MODEL), lambda i: (0, 0))]


def _ep_mix(first, ya_ref, yc_ref, yd_ref, gate_ref, bb_ref, bc_ref, bx_ref, hc_ref, hx_ref, cw_ref, cb_ref, gg_ref):
    tm = ya_ref.shape[0]
    u = bc_ref[...] * bx_ref[...]
    halo = jnp.where(first, 0.0, hc_ref[...] * hx_ref[...])
    row = lax.broadcasted_iota(jnp.int32, (tm, 1), 0)
    u1 = jnp.where(row == 0, halo[7:8, :], pltpu.roll(u, 1, 0))
    u2 = jnp.where(row == 0, halo[6:7, :], jnp.where(row == 1, halo[7:8, :], pltpu.roll(u, 2, 0)))
    cw = cw_ref[...]
    conv = cw[0:1, :] * u2 + cw[1:2, :] * u1 + cw[2:3, :] * u + cb_ref[...]
    bb = bb_ref[...]
    ys = [ya_ref[...], bb * conv, yc_ref[...], yd_ref[...]]
    rs = [_rms(y) for y in ys]
    gg = gg_ref[...]
    yhat = jnp.concatenate([y * r for y, r in zip(ys, rs)], axis=1)
    gate = gate_ref[...]
    sig = 1.0 / (1.0 + jnp.exp(-gate))
    return u, u1, u2, conv, bb, rs, yhat, yhat * gg, gate, sig


def _epilogue_fwd(x, ya, yc, yd, hf, conv_w, conv_b, g_grp, w_out, g_post, tgt=None):
    T = x.shape[0]
    tm = EP_TM
    row_spec = pl.BlockSpec((tm, D_MODEL), lambda i: (i, 0))

    def layer_out(refs):
        (x_ref, ya_ref, yc_ref, yd_ref, gate_ref, bb_ref, bc_ref, bx_ref, hc_ref, hx_ref, cw_ref, cb_ref,
         gg_ref, wo_ref, gp_ref) = refs
        (_, _, _, _, _, _, _, yn, gate, sig) = _ep_mix(
            pl.program_id(0) == 0, ya_ref, yc_ref, yd_ref, gate_ref, bb_ref, bc_ref, bx_ref, hc_ref, hx_ref,
            cw_ref, cb_ref, gg_ref)
        z = _dot((yn * (gate * sig)).astype(BF16), wo_ref[...])
        return x_ref[...] + z * _rms(z) * gp_ref[...]

    args = (x, ya, yc, yd, hf, hf, hf, hf, hf, hf, conv_w, conv_b, g_grp, w_out, g_post)
    in_specs = [row_spec] + _ep_in_specs(tm, None)
    if tgt is None:
        def body(*refs):
            refs[-1][...] = layer_out(refs[:-1])

        return pl.pallas_call(
            body, name="epilogue_fwd", grid=(T // tm,), in_specs=in_specs, out_specs=row_spec,
            out_shape=jax.ShapeDtypeStruct((T, D_MODEL), F32), compiler_params=_cparams(("parallel",)),
        )(*args)

    def body_loss(*refs):
        t_ref, dy_ref, l_ref = refs[-3:]

        @pl.when(pl.program_id(0) == 0)
        def _():
            l_ref[...] = jnp.zeros_like(l_ref)

        d = layer_out(refs[:-3]) - t_ref[...]
        dy_ref[...] = d * (1.0 / D_MODEL)
        part = jnp.sum(jnp.sum(d * d, axis=1, keepdims=True), axis=0, keepdims=True)
        l_ref[...] += part * (0.5 / D_MODEL)

    return pl.pallas_call(
        body_loss, name="epilogue_fwd_loss", grid=(T // tm,), in_specs=in_specs + [row_spec],
        out_specs=[row_spec, pl.BlockSpec((8, LANES), lambda i: (0, 0))],
        out_shape=[jax.ShapeDtypeStruct((T, D_MODEL), F32), jax.ShapeDtypeStruct((8, LANES), F32)],
        compiler_params=_cparams(("arbitrary",)),
    )(*args, tgt)


def _epilogue_bwd(dxn, ya, yc, yd, hf, conv_w, conv_b, g_grp, w_out, g_post):
    T = dxn.shape[0]
    tm = EP_TM
    nt = T // tm
    ridx = lambda i: (nt - 1 - i, 0)

    def body(dx_ref, ya_ref, yc_ref, yd_ref, gate_ref, bb_ref, bc_ref, bx_ref, hc_ref, hx_ref, cw_ref, cb_ref,
             gg_ref, wo_ref, gp_ref,
             dya_ref, dyc_ref, dyd_ref, dhf_ref, dwo_ref, dgp_ref, dgg_ref, dcw_ref, dcb_ref, carry_ref):
        i = pl.program_id(0)

        @pl.when(i == 0)
        def _():
            for r in (dwo_ref, dgp_ref, dgg_ref, dcw_ref, dcb_ref, carry_ref):
                r[...] = jnp.zeros_like(r)

        (u, u1, u2, conv, bb, rs, yhat, yn, gate, sig) = _ep_mix(
            i == nt - 1, ya_ref, yc_ref, yd_ref, gate_ref, bb_ref, bc_ref, bx_ref, hc_ref, hx_ref,
            cw_ref, cb_ref, gg_ref)
        silu = gate * sig
        ymix = (yn * silu).astype(BF16)
        z = _dot(ymix, wo_ref[...])
        rz = _rms(z)
        dz, dgrow = _rms_bwd(dx_ref[...], z * rz, rz, gp_ref[...])
        dgp_ref[...] += _colsum(dgrow)
        dzb = dz.astype(BF16)
        dwo_ref[...] += _dot_tn(ymix, dzb)
        dymix = _dot_nt(dzb, wo_ref[...])
        dhf_ref[:, 0:D_MODEL] = (dymix * yn * (sig * (1.0 + gate * (1.0 - sig)))).astype(BF16)
        dyn = dymix * silu
        dgg_ref[...] += _colsum(dyn * yhat)
        gg = gg_ref[...]
        dys = []
        for gi in range(4):
            sl = slice(gi * GROUP, (gi + 1) * GROUP)
            dyh = dyn[:, sl] * gg[:, sl]
            yh = yhat[:, sl]
            dys.append(rs[gi] * (dyh - yh * jnp.mean(dyh * yh, axis=-1, keepdims=True)))
        dya_ref[...] = dys[0]
        dyc_ref[...] = dys[2]
        dyd_ref[...] = dys[3]
        dyb = dys[1]
        dhf_ref[:, D_MODEL:D_MODEL + 256] = (dyb * conv).astype(BF16)
        dconv = dyb * bb
        dcb_ref[...] += _colsum(dconv)
        dcw_ref[0:1, :] += _colsum(dconv * u2)
        dcw_ref[1:2, :] += _colsum(dconv * u1)
        dcw_ref[2:3, :] += _colsum(dconv * u)
        carry = carry_ref[...]
        row = lax.broadcasted_iota(jnp.int32, (tm, 1), 0)
        d1 = jnp.where(row == tm - 1, carry[0:1, :], pltpu.roll(dconv, tm - 1, 0))
        d2 = jnp.where(row == tm - 2, carry[0:1, :],
                       jnp.where(row == tm - 1, carry[1:2, :], pltpu.roll(dconv, tm - 2, 0)))
        cw = cw_ref[...]
        du = cw[2:3, :] * dconv + cw[1:2, :] * d1 + cw[0:1, :] * d2
        dhf_ref[:, D_MODEL + 256:D_MODEL + 512] = (du * bx_ref[...]).astype(BF16)
        dhf_ref[:, D_MODEL + 512:D_MODEL + 768] = (du * bc_ref[...]).astype(BF16)
        carry_ref[...] = dconv[0:8, :]

    in_specs = [pl.BlockSpec((tm, D_MODEL), ridx)] + _ep_in_specs(tm, nt - 1)
    return pl.pallas_call(
        body, name="epilogue_bwd", grid=(nt,), in_specs=in_specs,
        out_specs=[pl.BlockSpec((tm, 256), ridx), pl.BlockSpec((tm, 256), ridx), pl.BlockSpec((tm, 256), ridx),
                   pl.BlockSpec((tm, D_MODEL + 768), ridx),
                   pl.BlockSpec((D_MODEL, D_MODEL), lambda i: (0, 0)),
                   pl.BlockSpec((1, D_MODEL), lambda i: (0, 0)),
                   pl.BlockSpec((1, D_MODEL), lambda i: (0, 0)),
                   pl.BlockSpec((8, 256), lambda i: (0, 0)),
                   pl.BlockSpec((1, 256), lambda i: (0, 0))],
        out_shape=[jax.ShapeDtypeStruct((T, 256), F32)] * 3
                  + [jax.ShapeDtypeStruct((T, D_MODEL + 768), BF16),
                     jax.ShapeDtypeStruct((D_MODEL, D_MODEL), F32),
                     jax.ShapeDtypeStruct((1, D_MODEL), F32),
                     jax.ShapeDtypeStruct((1, D_MODEL), F32),
                     jax.ShapeDtypeStruct((8, 256), F32),
                     jax.ShapeDtypeStruct((1, 256), F32)],
        scratch_shapes=[pltpu.VMEM((8, 256), F32)],
        compiler_params=_cparams(("arbitrary",)),
    )(dxn, ya, yc, yd, hf, hf, hf, hf, hf, hf, conv_w, conv_b, g_grp, w_out, g_post)


def _place():
    return lax.axis_index("x"), lax.axis_index("y"), lax.axis_index("c")


def _other_chips(x, y):
    return [(1 - x, y), (x, 1 - y), (1 - x, 1 - y)]


HBM = pl.BlockSpec(memory_space=pl.ANY)


def _gather_weights(shards):
    n = len(shards)

    def body(*refs):
        ins, outs = refs[:n], refs[n:2 * n]
        ici_send, ici_recv, d2d_send, d2d_recv, local_sems = refs[2 * n:]
        x, y, c = _place()
        me = 2 * x + y
        chips = _other_chips(x, y)

        def ici(a, j, layer_from):
            px, py = chips[j]
            return pltpu.make_async_remote_copy(
                src_ref=ins[a].at[c], dst_ref=outs[a].at[layer_from, c], send_sem=ici_send.at[3 * a + j],
                recv_sem=ici_recv.at[3 * a + j], device_id=(px, py, c), device_id_type=MESH)

        def d2d(a, j, layer):
            px, py = chips[j]
            blk = outs[a].at[2 * px + py, layer]
            return pltpu.make_async_remote_copy(
                src_ref=blk, dst_ref=blk, send_sem=d2d_send.at[3 * a + j], recv_sem=d2d_recv.at[3 * a + j],
                device_id=(x, y, 1 - c), device_id_type=MESH)

        local = [pltpu.make_async_copy(ins[a], outs[a].at[me], local_sems.at[a]) for a in range(n)]
        for cp in local:
            cp.start()
        sends = [ici(a, j, me) for j in range(3) for a in range(n)]
        for cp in sends:
            cp.start()
        for j in range(3):
            px, py = chips[j]
            for a in range(n):
                ici(a, j, 2 * px + py).wait_recv()
                fwd = d2d(a, j, c)
                fwd.start()
                sends.append(fwd)
        for j in range(3):
            for a in range(n):
                d2d(a, j, 1 - c).wait_recv()
        for cp in sends:
            cp.wait_send()
        for cp in local:
            cp.wait()

    return pl.pallas_call(
        body, name="gather_weights",
        in_specs=[HBM] * n, out_specs=[HBM] * n,
        out_shape=[jax.ShapeDtypeStruct((4,) + s.shape, s.dtype) for s in shards],
        scratch_shapes=[pltpu.SemaphoreType.DMA((3 * n,))] * 4 + [pltpu.SemaphoreType.DMA((n,))],
    )(*shards)


def _exchange_chips(parts, small):
    n = len(parts)

    def body(*refs):
        ins, sm_ref = refs[:n], refs[n]
        outs, osm_ref = refs[n + 1:2 * n + 1], refs[2 * n + 1]
        send_sems, recv_sems, ssend_sems, srecv_sems, local_sems = refs[2 * n + 2:]
        x, y, c = _place()
        me = 2 * x + y
        dev = 4 * x + 2 * y + c
        local = [pltpu.make_async_copy(ins[a].at[me], outs[a].at[me], local_sems.at[a]) for a in range(n)]
        local.append(pltpu.make_async_copy(sm_ref, osm_ref.at[dev], local_sems.at[n]))
        for cp in local:
            cp.start()
        sends = []
        for j, (px, py) in enumerate(_other_chips(x, y)):
            for a in range(n):
                cp = pltpu.make_async_remote_copy(
                    src_ref=ins[a].at[2 * px + py], dst_ref=outs[a].at[me], send_sem=send_sems.at[3 * a + j],
                    recv_sem=recv_sems.at[3 * a + j], device_id=(px, py, c), device_id_type=MESH)
                cp.start()
                sends.append(cp)
        flips = [(fx, fy, fc) for fx in (0, 1) for fy in (0, 1) for fc in (0, 1)][1:]
        for j, (fx, fy, fc) in enumerate(flips):
            cp = pltpu.make_async_remote_copy(
                src_ref=sm_ref, dst_ref=osm_ref.at[dev], send_sem=ssend_sems.at[j], recv_sem=srecv_sems.at[j],
                device_id=(x ^ fx, y ^ fy, c ^ fc), device_id_type=MESH)
            cp.start()
            sends.append(cp)
        for j, (px, py) in enumerate(_other_chips(x, y)):
            for a in range(n):
                pltpu.make_async_remote_copy(
                    src_ref=ins[a].at[me], dst_ref=outs[a].at[2 * px + py], send_sem=send_sems.at[3 * a + j],
                    recv_sem=recv_sems.at[3 * a + j], device_id=(px, py, c), device_id_type=MESH).wait_recv()
        for j, (fx, fy, fc) in enumerate(flips):
            src = 4 * (x ^ fx) + 2 * (y ^ fy) + (c ^ fc)
            pltpu.make_async_remote_copy(
                src_ref=sm_ref, dst_ref=osm_ref.at[src], send_sem=ssend_sems.at[j], recv_sem=srecv_sems.at[j],
                device_id=(x ^ fx, y ^ fy, c ^ fc), device_id_type=MESH).wait_recv()
        for cp in sends:
            cp.wait_send()
        for cp in local:
            cp.wait()

    return pl.pallas_call(
        body, name="exchange_chips",
        in_specs=[HBM] * (n + 1), out_specs=[HBM] * (n + 1),
        out_shape=[jax.ShapeDtypeStruct(p.shape, p.dtype) for p in parts]
                  + [jax.ShapeDtypeStruct((8,) + small.shape, small.dtype)],
        scratch_shapes=[pltpu.SemaphoreType.DMA((3 * n,)), pltpu.SemaphoreType.DMA((3 * n,)),
                        pltpu.SemaphoreType.DMA((7,)), pltpu.SemaphoreType.DMA((7,)),
                        pltpu.SemaphoreType.DMA((n + 1,))],
    )(*parts, small)


def _swap_cores(parts, name):
    n = len(parts)

    def body(*refs):
        ins, outs, send_sems, recv_sems = refs[:n], refs[n:2 * n], refs[2 * n], refs[2 * n + 1]
        x, y, c = _place()
        copies = [pltpu.make_async_remote_copy(
            src_ref=ins[a], dst_ref=outs[a], send_sem=send_sems.at[a], recv_sem=recv_sems.at[a],
            device_id=(x, y, 1 - c), device_id_type=MESH) for a in range(n)]
        for cp in copies:
            cp.start()
        for cp in copies:
            cp.wait()

    return pl.pallas_call(
        body, name=name, in_specs=[HBM] * n, out_specs=[HBM] * n,
        out_shape=[jax.ShapeDtypeStruct(p.shape, p.dtype) for p in parts],
        scratch_shapes=[pltpu.SemaphoreType.DMA((n,)), pltpu.SemaphoreType.DMA((n,))],
    )(*parts)


def _tile(rows, cols):
    for cand in (256, 128, 64):
        if rows % cand == 0:
            return cand, cols
    if rows > 64 and cols % 256 == 0:
        return rows, 256
    return rows, cols


def _add(a, b, name):
    L, R, C = a.shape
    tr, tc = _tile(R, C)

    def body(a_ref, b_ref, o_ref):
        o_ref[...] = (a_ref[...] + b_ref[...]).astype(BF16)

    spec = pl.BlockSpec((1, tr, tc), lambda l, i, j: (l, i, j))
    return pl.pallas_call(
        body, name=name, grid=(L, R // tr, C // tc), in_specs=[spec, spec], out_specs=spec,
        out_shape=jax.ShapeDtypeStruct((L, R, C), BF16),
        compiler_params=_cparams(("parallel", "parallel", "parallel")),
    )(a, b)


def _sum_leading(buf, name):
    n, R, C = buf.shape
    tr, tc = _tile(R, C)

    def body(b_ref, o_ref):
        acc = b_ref[0].astype(F32)
        for k in range(1, n):
            acc = acc + b_ref[k].astype(F32)
        o_ref[...] = acc

    return pl.pallas_call(
        body, name=name, grid=(R // tr, C // tc),
        in_specs=[pl.BlockSpec((n, tr, tc), lambda i, j: (0, i, j))],
        out_specs=pl.BlockSpec((tr, tc), lambda i, j: (i, j)),
        out_shape=jax.ShapeDtypeStruct((R, C), F32),
        compiler_params=_cparams(("parallel", "parallel")),
    )(buf)


def _adam_update(w, g, m, v):
    c1 = 1.0 / (1.0 - ADAM_B1 ** ADAM_STEP)
    c2 = 1.0 / (1.0 - ADAM_B2 ** ADAM_STEP)
    mn = ADAM_B1 * m + (1.0 - ADAM_B1) * g
    vn = ADAM_B2 * v + (1.0 - ADAM_B2) * (g * g)
    return -ADAM_LR * ((mn * c1) / (jnp.sqrt(vn * c2) + ADAM_EPS) + ADAM_WD * w), mn, vn


def _adamw_layers(w, m, v, g_mine, g_other, name):
    _, R, C = w.shape
    tr, tc = _tile(R, C)

    def body(w_ref, m_ref, v_ref, gm_ref, go_ref, g_ref, d_ref, mo_ref, vo_ref):
        g = jnp.where(pl.program_id(0) == lax.axis_index("c"), gm_ref[...], go_ref[...])
        g_ref[0] = g
        d_ref[0], mo_ref[0], vo_ref[0] = _adam_update(w_ref[0], g, m_ref[0], v_ref[0])

    spec3 = pl.BlockSpec((1, tr, tc), lambda l, i, j: (l, i, j))
    spec2 = pl.BlockSpec((tr, tc), lambda l, i, j: (i, j))
    return pl.pallas_call(
        body, name=name, grid=(2, R // tr, C // tc),
        in_specs=[spec3] * 3 + [spec2] * 2, out_specs=[spec3] * 4,
        out_shape=[jax.ShapeDtypeStruct(w.shape, F32)] * 4,
        compiler_params=_cparams(("parallel", "parallel", "parallel")),
    )(w, m, v, g_mine, g_other)


PACK_C = 1024
_BIG = ("w_in", "w_out", "mla_w_uq", "mla_w_ukv", "conv_w")
_SMALL = ("norm_pre", "group_norm", "norm_post", "conv_b", "mla_q_norm", "mla_kv_norm", "attn_sinks")
_SMALL_W = {"norm_pre": 1024, "group_norm": 1024, "norm_post": 1024, "conv_b": 256, "mla_q_norm": 256,
            "mla_kv_norm": 128, "attn_sinks": 4}


_LOSS_AT = divmod(DEPTH * sum(_SMALL_W.values()), PACK_C)


def _pack_small(d, loss):
    flat = jnp.concatenate([d[n].reshape(-1) for n in _SMALL] + [loss.reshape(1)])
    return jnp.pad(flat, (0, 8 * PACK_C - flat.shape[0])).reshape(8, PACK_C)


def _adamw_small(w, m, v, got):
    ns = len(_SMALL)

    def body(*refs):
        got_ref = refs[3 * ns]
        outs = refs[3 * ns + 1:]
        gsum = got_ref[0]
        for d in range(1, 8):
            gsum = gsum + got_ref[d]
        outs[4 * ns][...] = gsum[_LOSS_AT[0]:_LOSS_AT[0] + 1, _LOSS_AT[1]:_LOSS_AT[1] + 1]
        off = 0
        for i, name in enumerate(_SMALL):
            wd = _SMALL_W[name]
            rows = []
            for l in range(DEPTH):
                r, c0 = divmod(off + l * wd, PACK_C)
                rows.append(gsum[r:r + 1, c0:c0 + wd])
            off += DEPTH * wd
            g = jnp.concatenate(rows, axis=0)
            delta, mn, vn = _adam_update(refs[i][...], g, refs[ns + i][...], refs[2 * ns + i][...])
            outs[i][...] = g
            outs[ns + i][...] = delta
            outs[2 * ns + i][...] = mn
            outs[3 * ns + i][...] = vn

    shapes = [jax.ShapeDtypeStruct(w[n].shape, F32) for n in _SMALL]
    res = pl.pallas_call(body, name="adamw_small", out_shape=shapes * 4 + [jax.ShapeDtypeStruct((1, 1), F32)])(
        *[w[n] for n in _SMALL], *[m[n] for n in _SMALL], *[v[n] for n in _SMALL], got)
    return [dict(zip(_SMALL, res[k * ns:(k + 1) * ns])) for k in range(4)], res[4 * ns]


def _w_in_internal(wt):
    rows = []
    for n in _INT_ORDER:
        o, wd = _REAL_OFF[n]
        rows.append(wt[o:o + wd])
        if _INT_W[n] != wd:
            rows.append(jnp.zeros((_INT_W[n] - wd, wt.shape[1]), wt.dtype))
    return jnp.concatenate(rows, axis=0)


def _w_in_real(dwt):
    return jnp.concatenate([dwt[_INT_OFF[n]:_INT_OFF[n] + wd] for n, wd in _REAL], axis=0)


def _uq_internal(w):
    return jnp.pad(w.reshape(256, 4, 96), ((0, 0), (0, 0), (0, 32))).reshape(256, 512)


def _uq_real(dw):
    return dw.reshape(256, 4, 128)[:, :, :96].reshape(256, 384)


def _ukv_internal(w):
    w4 = w.reshape(128, 4, 128)
    k = jnp.pad(w4[:, :, :64], ((0, 0), (0, 0), (0, 64))).reshape(128, 512)
    return jnp.concatenate([k, w4[:, :, 64:].reshape(128, 256)], axis=1)


def _ukv_real(dw):
    k = dw[:, :512].reshape(128, 4, 128)[:, :, :64]
    v = dw[:, 512:].reshape(128, 4, 64)
    return jnp.concatenate([k, v], axis=2).reshape(128, 512)


def _layer_fwd(x, rope, p, tgt=None):
    xn, hb, hf = _inproj_fwd(x, p["norm_pre"], p["w_in"])
    ya = _swa_fwd(hb, p["attn_sinks"])
    qm, km, vm, vt = _mla_prep_fwd(hf, rope, p["mla_q_norm"], p["mla_kv_norm"], p["mla_w_uq"], p["mla_w_ukv"])
    yc, lse = _mla_fwd(qm, km, vt)
    yd, tot, cnt = _sb_fwd(hb)
    x_next = _epilogue_fwd(x, ya, yc, yd, hf, p["conv_w"], p["conv_b"], p["group_norm"], p["w_out"], p["norm_post"],
                           tgt)
    return x_next, dict(x=x, xn=xn, hb=hb, hf=hf, ya=ya, yc=yc, yd=yd, tot=tot, cnt=cnt, qm=qm, km=km, vm=vm, lse=lse)


def _layer_bwd(dx_next, rope, p, s):
    (dya, dyc, dyd, dhf, dw_out, dg_post, dg_grp, dconv_w, dconv_b) = _epilogue_bwd(
        dx_next, s["ya"], s["yc"], s["yd"], s["hf"], p["conv_w"], p["conv_b"], p["group_norm"], p["w_out"],
        p["norm_post"])
    dq_d, dk_d, dv_d = _sb_bwd(s["hb"], s["tot"], s["cnt"], dyd)
    dqm, dkt, dvt = _mla_bwd(s["qm"], s["km"], s["vm"], s["yc"], s["lse"], dyc)
    dc, dw_uq, dw_ukv, dg_q, dg_kv = _mla_prep_bwd(
        s["hf"], rope, p["mla_q_norm"], p["mla_kv_norm"], p["mla_w_uq"], p["mla_w_ukv"], dqm, dkt, dvt)
    dq_a, dk_a, dv_a, dsinks = _swa_bwd(s["hb"], p["attn_sinks"], dya)
    dx, dh, dg_pre = _inproj_bwd_dx(s["x"], p["norm_pre"], p["w_in"], dx_next,
                                    [dq_a, dk_a, dv_a, dq_d, dk_d, dv_d, dhf, dc])
    dwt_in = _grad_over_tokens(s["xn"], dh, "inproj_bwd_dw")
    grads = dict(norm_pre=dg_pre[0], w_in_t=_w_in_real(dwt_in), attn_sinks=dsinks[0, :4], conv_w=dconv_w[:3],
                 conv_b=dconv_b[0], mla_q_norm=dg_q[0], mla_w_uq=_uq_real(dw_uq), mla_kv_norm=dg_kv[0],
                 mla_w_ukv=_ukv_real(dw_ukv), group_norm=dg_grp[0], w_out=dw_out, norm_post=dg_post[0])
    return dx, grads


_WEIGHTS = ["norm_pre", "w_in", "attn_sinks", "conv_w", "conv_b", "mla_q_norm", "mla_w_uq", "mla_kv_norm",
            "mla_w_ukv", "group_norm", "w_out", "norm_post"]


def kernel(x, positions, norm_pre, w_in, attn_sinks, conv_w, conv_b, mla_q_norm, mla_w_uq, mla_kv_norm, mla_w_ukv, group_norm, w_out, norm_post, loss_target, m_norm_pre, m_w_in, m_attn_sinks, m_conv_w, m_conv_b, m_mla_q_norm, m_mla_w_uq, m_mla_kv_norm, m_mla_w_ukv, m_group_norm, m_w_out, m_norm_post, v_norm_pre, v_w_in, v_attn_sinks, v_conv_w, v_conv_b, v_mla_q_norm, v_mla_w_uq, v_mla_kv_norm, v_mla_w_ukv, v_group_norm, v_w_out, v_norm_post):
    w = dict(norm_pre=norm_pre, w_in=w_in, attn_sinks=attn_sinks, conv_w=conv_w, conv_b=conv_b,
             mla_q_norm=mla_q_norm, mla_w_uq=mla_w_uq, mla_kv_norm=mla_kv_norm, mla_w_ukv=mla_w_ukv,
             group_norm=group_norm, w_out=w_out, norm_post=norm_post)
    m = dict(norm_pre=m_norm_pre, w_in=m_w_in, attn_sinks=m_attn_sinks, conv_w=m_conv_w, conv_b=m_conv_b,
             mla_q_norm=m_mla_q_norm, mla_w_uq=m_mla_w_uq, mla_kv_norm=m_mla_kv_norm, mla_w_ukv=m_mla_w_ukv,
             group_norm=m_group_norm, w_out=m_w_out, norm_post=m_norm_post)
    v = dict(norm_pre=v_norm_pre, w_in=v_w_in, attn_sinks=v_attn_sinks, conv_w=v_conv_w, conv_b=v_conv_b,
             mla_q_norm=v_mla_q_norm, mla_w_uq=v_mla_w_uq, mla_kv_norm=v_mla_kv_norm, mla_w_ukv=v_mla_w_ukv,
             group_norm=v_group_norm, w_out=v_w_out, norm_post=v_norm_post)
    T = x.shape[1]
    xs = x[0]
    rope = _rope_tables(positions[0].reshape(T, 1))
    tgt = loss_target[0]
    core = lax.axis_index("c")

    gathered = _gather_weights([jnp.swapaxes(w["w_in"], 1, 2).astype(BF16)]
                               + [w[n].astype(BF16) for n in _BIG[1:4]] + [w["conv_w"]])
    full = {}
    for n, got in zip(_BIG, gathered):
        if n in ("w_in", "w_out"):
            full[n] = jnp.moveaxis(got, 0, 1).reshape(DEPTH, 4 * got.shape[2], got.shape[3])
        else:
            full[n] = jnp.transpose(got, (1, 2, 0, 3)).reshape(DEPTH, got.shape[2], 4 * got.shape[3])

    layers = []
    for l in range(DEPTH):
        layers.append(dict(
            norm_pre=norm_pre[l:l + 1], w_in=_w_in_internal(full["w_in"][l]), attn_sinks=attn_sinks[l],
            conv_w=full["conv_w"][l], conv_b=conv_b[l:l + 1], mla_q_norm=mla_q_norm[l:l + 1],
            mla_w_uq=_uq_internal(full["mla_w_uq"][l]), mla_kv_norm=mla_kv_norm[l:l + 1],
            mla_w_ukv=_ukv_internal(full["mla_w_ukv"][l]), group_norm=group_norm[l:l + 1],
            w_out=full["w_out"][l], norm_post=norm_post[l:l + 1]))

    saved = []
    h = xs
    for l in range(DEPTH):
        h, s = _layer_fwd(h, rope, layers[l], tgt if l == DEPTH - 1 else None)
        saved.append(s)
    dy, loss_part = h

    grads = [None] * DEPTH
    for l in reversed(range(DEPTH)):
        dy, grads[l] = _layer_bwd(dy, rope, layers[l], saved[l])

    turned = ("w_in", "mla_w_uq")
    turn = lambda n, a: jnp.swapaxes(a, -1, -2) if n in turned else a

    def chunks(n, a):
        if n in ("w_out", "w_in"):
            return a.reshape(4, a.shape[0] // 4, a.shape[1])
        if n in turned:
            return a.T.reshape(4, a.shape[1] // 4, a.shape[0])
        return jnp.transpose(a.reshape(a.shape[0], 4, a.shape[1] // 4), (1, 0, 2))

    grad = lambda l, n: grads[l]["w_in_t" if n == "w_in" else n]
    mine = [chunks(n, jnp.where(core == 0, grad(0, n), grad(1, n))) for n in _BIG]
    theirs = [chunks(n, jnp.where(core == 0, grad(1, n), grad(0, n))) for n in _BIG]
    from_sibling = _swap_cores(theirs, "swap_layer_chunks")
    summed = [_add(a, b, "add_cores_" + n) for n, a, b in zip(_BIG, mine, from_sibling)]
    small = _pack_small({n: jnp.stack([grads[l][n] for l in range(DEPTH)]) for n in _SMALL}, loss_part[0, 0])
    *got, got_small = _exchange_chips(summed, small)
    done = [_sum_leading(b, "sum_chips_" + n) for n, b in zip(_BIG, got)]
    done_other = _swap_cores(done, "swap_layer_shards")

    outs, loss = _adamw_small(w, m, v, got_small)
    for n, gm, go in zip(_BIG, done, done_other):
        for d, a in zip(outs, _adamw_layers(turn(n, w[n]), turn(n, m[n]), turn(n, v[n]), gm, go, "adamw_" + n)):
            d[n] = turn(n, a)
    return (loss[0, 0], dy[None], *[outs[0][n] for n in _WEIGHTS], *[outs[1][n] for n in _WEIGHTS],
            *[outs[2][n] for n in _WEIGHTS], *[outs[3][n] for n in _WEIGHTS])
```

```python
import math

import jax
import jax.numpy as jnp
from jax import lax
from jax.experimental import pallas as pl
from jax.experimental.pallas import tpu as pltpu

F32 = jnp.float32
BF16 = jnp.bfloat16
MESH = pl.DeviceIdType.MESH

D_MODEL = 1024
DEPTH = 2
EPS = 1e-6
BLOCK = 128
HEAD = 64
LANES = 128
GROUP = 256
LOG2E = 1.4426950408889634
LN2 = 0.6931471805599453
MLA_QSCALE = 96 ** -0.5 * LOG2E
ROPE_HALF = 16
ROPE_THETA = 10000.0
SWA_SUB = 2
ATT_BLK = 256
MLA_BQ = 512
NEG = -1e30
SB_DEAD = -104.0

ADAM_LR, ADAM_B1, ADAM_B2, ADAM_EPS, ADAM_WD, ADAM_STEP = 0.001, 0.9, 0.999, 1e-08, 0.01, 10

_REAL = [("a_q", 256), ("a_k", 128), ("a_v", 128), ("b_b", 256), ("b_c", 256), ("b_x", 256),
         ("c_q", 256), ("c_kv", 128), ("c_kr", 32), ("d_q", 256), ("d_k", 256), ("d_v", 256),
         ("gate", 1024)]
_REAL_OFF = {}
_o = 0
for _n, _w in _REAL:
    _REAL_OFF[_n] = (_o, _w)
    _o += _w
D_IN = _o
_INT_ORDER = ["a_q", "a_k", "a_v", "d_q", "d_k", "d_v", "gate", "b_b", "b_c", "b_x", "c_q", "c_kv", "c_kr"]
_INT_W = dict(_REAL)
_INT_W["c_kr"] = 128
_INT_OFF = {}
_o = 0
for _n in _INT_ORDER:
    _INT_OFF[_n] = _o
    _o += _INT_W[_n]
N_INT = _o
N_HB = _INT_OFF["gate"]
N_HF = N_INT - N_HB

VMEM_LIMIT = 56 * 1024 * 1024


def _cparams(sem):
    return pltpu.CompilerParams(dimension_semantics=sem, vmem_limit_bytes=VMEM_LIMIT)


def _dot(a, b):
    return jnp.dot(a, b, preferred_element_type=F32)


def _dot_nt(a, b):
    return lax.dot_general(a, b, (((1,), (1,)), ((), ())), preferred_element_type=F32)


def _dot_tn(a, b):
    return lax.dot_general(a, b, (((0,), (0,)), ((), ())), preferred_element_type=F32)


def _split(x):
    hi = x.astype(BF16)
    lo = (x - hi.astype(F32)).astype(BF16)
    return hi, lo


def _rms(x):
    return lax.rsqrt(jnp.mean(x * x, axis=-1, keepdims=True) + EPS)


def _rms_bwd(dy, xhat, r, g):
    dxhat = dy * g
    return r * (dxhat - xhat * jnp.mean(dxhat * xhat, axis=-1, keepdims=True)), dy * xhat


def _colsum(x):
    return jnp.sum(x, axis=0, keepdims=True)


def _inproj_fwd(x, g, wt):
    T = x.shape[0]
    tm = 512

    def body(x_ref, g_ref, w_ref, xn_ref, hb_ref, hf_ref):
        xv = x_ref[...]
        xn = (xv * _rms(xv) * g_ref[...]).astype(BF16)
        xn_ref[...] = xn
        h = _dot_nt(xn, w_ref[...])
        hb_ref[...] = h[:, :N_HB].astype(BF16)
        hf_ref[...] = h[:, N_HB:]

    return pl.pallas_call(
        body, name="inproj_fwd", grid=(T // tm,),
        in_specs=[pl.BlockSpec((tm, D_MODEL), lambda i: (i, 0)),
                  pl.BlockSpec((1, D_MODEL), lambda i: (0, 0)),
                  pl.BlockSpec((N_INT, D_MODEL), lambda i: (0, 0))],
        out_specs=[pl.BlockSpec((tm, D_MODEL), lambda i: (i, 0)),
                   pl.BlockSpec((tm, N_HB), lambda i: (i, 0)),
                   pl.BlockSpec((tm, N_HF), lambda i: (i, 0))],
        out_shape=[jax.ShapeDtypeStruct((T, D_MODEL), BF16),
                   jax.ShapeDtypeStruct((T, N_HB), BF16),
                   jax.ShapeDtypeStruct((T, N_HF), F32)],
        compiler_params=_cparams(("parallel",)),
    )(x, g, wt)


def _inproj_bwd_dx(x, g, wt, dx_next, pieces):
    T = x.shape[0]
    tm = 512
    widths = [p.shape[1] for p in pieces]
    assert sum(widths) == N_INT

    def body(x_ref, g_ref, w_ref, dxn_ref, *rest):
        p_refs = rest[:len(pieces)]
        dx_ref, dh_ref, dg_ref = rest[len(pieces):]
        dh = jnp.concatenate([p[...].astype(BF16) for p in p_refs], axis=1)
        dh_ref[...] = dh
        dxn = _dot(dh, w_ref[...])
        xv = x_ref[...]
        r = _rms(xv)
        dx, dgrow = _rms_bwd(dxn, xv * r, r, g_ref[...])
        dx_ref[...] = dx + dxn_ref[...]

        @pl.when(pl.program_id(0) == 0)
        def _():
            dg_ref[...] = jnp.zeros_like(dg_ref)

        dg_ref[...] += _colsum(dgrow)

    return pl.pallas_call(
        body, name="inproj_bwd_dx", grid=(T // tm,),
        in_specs=[pl.BlockSpec((tm, D_MODEL), lambda i: (i, 0)),
                  pl.BlockSpec((1, D_MODEL), lambda i: (0, 0)),
                  pl.BlockSpec((N_INT, D_MODEL), lambda i: (0, 0)),
                  pl.BlockSpec((tm, D_MODEL), lambda i: (i, 0))]
                 + [pl.BlockSpec((tm, wd), lambda i: (i, 0)) for wd in widths],
        out_specs=[pl.BlockSpec((tm, D_MODEL), lambda i: (i, 0)),
                   pl.BlockSpec((tm, N_INT), lambda i: (i, 0)),
                   pl.BlockSpec((1, D_MODEL), lambda i: (0, 0))],
        out_shape=[jax.ShapeDtypeStruct((T, D_MODEL), F32),
                   jax.ShapeDtypeStruct((T, N_INT), BF16),
                   jax.ShapeDtypeStruct((1, D_MODEL), F32)],
        compiler_params=_cparams(("arbitrary",)),
    )(x, g, wt, dx_next, *pieces)


def _grad_over_tokens(a, b, name):
    T, M = a.shape
    N = b.shape[1]
    tm, tn = min(1024, T), 896

    def body(a_ref, b_ref, o_ref):
        @pl.when(pl.program_id(1) == 0)
        def _():
            o_ref[...] = jnp.zeros_like(o_ref)

        o_ref[...] += _dot_tn(b_ref[...], a_ref[...])

    return pl.pallas_call(
        body, name=name, grid=(N // tn, T // tm),
        in_specs=[pl.BlockSpec((tm, M), lambda j, t: (t, 0)),
                  pl.BlockSpec((tm, tn), lambda j, t: (t, j))],
        out_specs=pl.BlockSpec((tn, M), lambda j, t: (j, 0)),
        out_shape=jax.ShapeDtypeStruct((N, M), F32),
        compiler_params=_cparams(("parallel", "arbitrary")),
    )(a, b)


def _roll_f32(x, shift):
    return pltpu.roll(x.astype(F32), shift, 1)


def _swa_operands(h, q, k_prev, k_cur, v_prev, v_cur):
    p, e = h // 2, h % 2
    lane = lax.broadcasted_iota(jnp.int32, (1, LANES), 1) // HEAD
    q = q[:, p * LANES:(p + 1) * LANES]
    if e != p:
        q = _roll_f32(q, HEAD).astype(BF16)
        v_prev = _roll_f32(v_prev, HEAD).astype(BF16)
        v_cur = _roll_f32(v_cur, HEAD).astype(BF16)
    qs = jnp.where(lane == p, q, 0) * 0.125
    return dict(p=p, e=e, lane=lane, qs=qs, k_prev=k_prev, k_cur=k_cur,
                v_prev=jnp.where(lane == e, v_prev, 0), v_cur=jnp.where(lane == e, v_cur, 0),
                s_prev=_dot_nt(qs, k_prev), s_cur=_dot_nt(qs, k_cur))


def _swa_probs(ops, sink, no_prev):
    row = lax.broadcasted_iota(jnp.int32, (BLOCK, BLOCK), 0)
    col = lax.broadcasted_iota(jnp.int32, (BLOCK, BLOCK), 1)
    ok_prev = col > row if no_prev is None else jnp.logical_and(col > row, jnp.logical_not(no_prev))
    s_prev = jnp.where(ok_prev, ops["s_prev"], NEG)
    s_cur = jnp.where(col <= row, ops["s_cur"], NEG)
    m = jnp.maximum(jnp.maximum(jnp.max(s_prev, axis=1, keepdims=True),
                                jnp.max(s_cur, axis=1, keepdims=True)), sink)
    p_prev = jnp.exp(s_prev - m)
    p_cur = jnp.exp(s_cur - m)
    p_sink = jnp.exp(sink - m)
    inv = 1.0 / (jnp.sum(p_prev, axis=1, keepdims=True) + jnp.sum(p_cur, axis=1, keepdims=True) + p_sink)
    return p_prev * inv, p_cur * inv, p_sink * inv


def _swa_specs(T):
    n = T // (BLOCK * SWA_SUB)
    qo, ko, vo = (_INT_OFF[name] // LANES for name in ("a_q", "a_k", "a_v"))
    halo = lambda i: jnp.maximum(i * SWA_SUB - 1, 0)
    return [pl.BlockSpec((BLOCK * SWA_SUB, 256), lambda i: (i, qo // 2)),
            pl.BlockSpec((BLOCK, LANES), lambda i: (halo(i), ko)),
            pl.BlockSpec((BLOCK * SWA_SUB, LANES), lambda i: (i, ko)),
            pl.BlockSpec((BLOCK, LANES), lambda i: (halo(i), vo)),
            pl.BlockSpec((BLOCK * SWA_SUB, LANES), lambda i: (i, vo)),
            pl.BlockSpec(memory_space=pltpu.SMEM)], n


def _swa_units(q_ref, kh_ref, kc_ref, vh_ref, vc_ref, s_ref):
    blk = lambda a: slice(a * BLOCK, (a + 1) * BLOCK)
    units = [(a, h) for a in range(SWA_SUB) for h in range(4)]
    ops = {}
    for a, h in units:
        k_prev, v_prev = (kh_ref[...], vh_ref[...]) if a == 0 else (kc_ref[blk(a - 1), :], vc_ref[blk(a - 1), :])
        ops[a, h] = _swa_operands(h, q_ref[blk(a), :], k_prev, kc_ref[blk(a), :], v_prev, vc_ref[blk(a), :])
    probs = {(a, h): _swa_probs(ops[a, h], s_ref[h], pl.program_id(0) == 0 if a == 0 else None) for a, h in units}
    return units, ops, probs, blk


def _swa_fwd(hb, sinks):
    T = hb.shape[0]
    specs, n = _swa_specs(T)

    def body(q_ref, kh_ref, kc_ref, vh_ref, vc_ref, s_ref, o_ref):
        units, ops, probs, blk = _swa_units(q_ref, kh_ref, kc_ref, vh_ref, vc_ref, s_ref)
        outs = {u: _dot(probs[u][0].astype(BF16), ops[u]["v_prev"]) + _dot(probs[u][1].astype(BF16), ops[u]["v_cur"])
                for u in units}
        for a in range(SWA_SUB):
            for p in range(2):
                o_ref[blk(a), p * LANES:(p + 1) * LANES] = outs[a, 2 * p] + outs[a, 2 * p + 1]

    return pl.pallas_call(
        body, name="swa_fwd", grid=(n,), in_specs=specs,
        out_specs=pl.BlockSpec((BLOCK * SWA_SUB, 256), lambda i: (i, 0)),
        out_shape=jax.ShapeDtypeStruct((T, 256), F32),
        compiler_params=_cparams(("parallel",)),
    )(hb, hb, hb, hb, hb, sinks)


def _swa_bwd(hb, sinks, dy):
    T = hb.shape[0]
    specs, n = _swa_specs(T)

    def body(q_ref, kh_ref, kc_ref, vh_ref, vc_ref, s_ref, dy_ref, dq_ref, dk_ref, dv_ref, ds_ref):
        i = pl.program_id(0)

        @pl.when(i == 0)
        def _():
            ds_ref[...] = jnp.zeros_like(ds_ref)

        lane_id = lax.broadcasted_iota(jnp.int32, (8, LANES), 1)
        units, ops, probs, blk = _swa_units(q_ref, kh_ref, kc_ref, vh_ref, vc_ref, s_ref)
        dos = {(a, h): jnp.where(ops[a, h]["lane"] == ops[a, h]["e"],
                                 dy_ref[blk(a), ops[a, h]["p"] * LANES:(ops[a, h]["p"] + 1) * LANES], 0.0)
               for a, h in units}
        dobs = {u: dos[u].astype(BF16) for u in units}
        pbs = {u: (probs[u][0].astype(BF16), probs[u][1].astype(BF16)) for u in units}
        outs = {u: _dot(pbs[u][0], ops[u]["v_prev"]) + _dot(pbs[u][1], ops[u]["v_cur"]) for u in units}
        dps = {u: (_dot_nt(dobs[u], ops[u]["v_prev"]), _dot_nt(dobs[u], ops[u]["v_cur"])) for u in units}
        dss, dsinks = {}, jnp.zeros((8, LANES), F32)
        for u in units:
            delta = jnp.sum(dos[u] * outs[u], axis=1, keepdims=True)
            dss[u] = ((probs[u][0] * (dps[u][0] - delta)).astype(BF16),
                      (probs[u][1] * (dps[u][1] - delta)).astype(BF16))
            dsink = -jnp.sum(probs[u][2] * delta, axis=0, keepdims=True)
            dsinks += jnp.where(lane_id == u[1], dsink, 0.0)
        ds_ref[...] += dsinks
        dqs = {u: (_dot(dss[u][0], ops[u]["k_prev"]) + _dot(dss[u][1], ops[u]["k_cur"])) * 0.125 for u in units}
        zero = jnp.zeros((BLOCK, LANES), F32)
        dk_as_prev, dk_as_cur = [zero] * SWA_SUB, [zero] * SWA_SUB
        dv_as_prev, dv_as_cur = [zero] * SWA_SUB, [zero] * SWA_SUB
        for a, h in units:
            p, e = ops[a, h]["p"], ops[a, h]["e"]
            dob_v = dobs[a, h] if e == p else pltpu.roll(dos[a, h], HEAD, 1).astype(BF16)
            dk_as_prev[a] = dk_as_prev[a] + _dot_tn(dss[a, h][0], ops[a, h]["qs"])
            dk_as_cur[a] = dk_as_cur[a] + _dot_tn(dss[a, h][1], ops[a, h]["qs"])
            dv_as_prev[a] = dv_as_prev[a] + _dot_tn(pbs[a, h][0], dob_v)
            dv_as_cur[a] = dv_as_cur[a] + _dot_tn(pbs[a, h][1], dob_v)
        base = i * SWA_SUB
        for a in range(SWA_SUB):
            rows = pl.ds(pl.multiple_of((base + a) * BLOCK, BLOCK), BLOCK)
            more = a + 1 < SWA_SUB
            dk_ref[rows, :] = dk_as_cur[a] + (dk_as_prev[a + 1] if more else 0.0)
            dv_ref[rows, :] = dv_as_cur[a] + (dv_as_prev[a + 1] if more else 0.0)
        halo = pl.ds(pl.multiple_of(jnp.maximum(base - 1, 0) * BLOCK, BLOCK), BLOCK)
        dk_ref[halo, :] += dk_as_prev[0]
        dv_ref[halo, :] += dv_as_prev[0]
        for a in range(SWA_SUB):
            for p in range(2):
                dq_pair = jnp.zeros((BLOCK, LANES), F32)
                for e in range(2):
                    dq = jnp.where(ops[a, 2 * p + e]["lane"] == p, dqs[a, 2 * p + e], 0.0)
                    dq_pair += dq if e == p else pltpu.roll(dq, HEAD, 1)
                dq_ref[blk(a), p * LANES:(p + 1) * LANES] = dq_pair.astype(BF16)

    return pl.pallas_call(
        body, name="swa_bwd", grid=(n,),
        in_specs=specs + [pl.BlockSpec((BLOCK * SWA_SUB, 256), lambda i: (i, 0))],
        out_specs=[pl.BlockSpec((BLOCK * SWA_SUB, 256), lambda i: (i, 0)),
                   pl.BlockSpec((T, LANES), lambda i: (0, 0)),
                   pl.BlockSpec((T, LANES), lambda i: (0, 0)),
                   pl.BlockSpec((8, LANES), lambda i: (0, 0))],
        out_shape=[jax.ShapeDtypeStruct((T, 256), BF16),
                   jax.ShapeDtypeStruct((T, LANES), F32),
                   jax.ShapeDtypeStruct((T, LANES), F32),
                   jax.ShapeDtypeStruct((8, LANES), F32)],
        compiler_params=_cparams(("arbitrary",)),
    )(hb, hb, hb, hb, hb, sinks, dy)


def _rope_tables(pos):
    T = pos.shape[0]
    tm = 512

    def body(pos_ref, o_ref):
        lane = lax.broadcasted_iota(jnp.int32, (1, LANES), 1)
        active = jnp.logical_and(lane >= HEAD, lane < HEAD + 2 * ROPE_HALF)
        idx = ((lane - HEAD) % ROPE_HALF).astype(F32)
        freq = jnp.exp(idx * (-math.log(ROPE_THETA) / ROPE_HALF))
        ang = pos_ref[...].astype(F32) * freq
        cos, sin = jnp.cos(ang), jnp.sin(ang)
        o_ref[:, 0:LANES] = jnp.where(active, cos, 1.0)
        o_ref[:, LANES:2 * LANES] = jnp.where(jnp.logical_and(active, lane >= HEAD + ROPE_HALF), sin, 0.0)
        o_ref[:, 2 * LANES:] = jnp.where(jnp.logical_and(active, lane < HEAD + ROPE_HALF), -sin, 0.0)

    return pl.pallas_call(
        body, name="rope_tables", grid=(T // tm,),
        in_specs=[pl.BlockSpec((tm, 1), lambda i: (i, 0))],
        out_specs=pl.BlockSpec((tm, 3 * LANES), lambda i: (i, 0)),
        out_shape=jax.ShapeDtypeStruct((T, 3 * LANES), F32),
        compiler_params=_cparams(("parallel",)),
    )(pos)


def _rope_factors(tab_ref):
    return tab_ref[:, 0:LANES], tab_ref[:, LANES:2 * LANES], tab_ref[:, 2 * LANES:]


def _rope(x, tabs):
    c, s_up, s_dn = tabs
    return x * c + pltpu.roll(x, ROPE_HALF, 1) * s_up + pltpu.roll(x, LANES - ROPE_HALF, 1) * s_dn


def _rope_t(dy, tabs):
    c, s_up, s_dn = tabs
    return dy * c + pltpu.roll(dy * s_up, LANES - ROPE_HALF, 1) + pltpu.roll(dy * s_dn, ROPE_HALF, 1)


def _mla_lat_specs(tm):
    cq, ckv, ckr = ((_INT_OFF[n] - N_HB) for n in ("c_q", "c_kv", "c_kr"))
    return [pl.BlockSpec((tm, 256), lambda i: (i, cq // 256)),
            pl.BlockSpec((tm, LANES), lambda i: (i, ckv // LANES)),
            pl.BlockSpec((tm, LANES), lambda i: (i, ckr // LANES)),
            pl.BlockSpec((tm, 3 * LANES), lambda i: (i, 0)),
            pl.BlockSpec((1, 256), lambda i: (0, 0)),
            pl.BlockSpec((1, LANES), lambda i: (0, 0)),
            pl.BlockSpec((256, 512), lambda i: (0, 0)),
            pl.BlockSpec((LANES, 768), lambda i: (0, 0))]


def _mla_prep_fwd(hf, rope, g_q, g_kv, w_uq, w_ukv):
    T = hf.shape[0]
    tm = 512
    sub = tm // ATT_BLK

    def body(cq_ref, ckv_ref, ckr_ref, tab_ref, gq_ref, gkv_ref, wq_ref, wkv_ref, qm_ref, km_ref, vm_ref, vt_ref):
        tabs = _rope_factors(tab_ref)
        cq = cq_ref[...]
        q = _dot((cq * _rms(cq) * gq_ref[...]).astype(BF16), wq_ref[...])
        ckv = ckv_ref[...]
        kv = _dot((ckv * _rms(ckv) * gkv_ref[...]).astype(BF16), wkv_ref[...])
        kr = _rope(pltpu.roll(ckr_ref[...], HEAD, 1), tabs)
        for h in range(4):
            sl = slice(h * LANES, (h + 1) * LANES)
            qm_ref[:, sl] = (_rope(q[:, sl], tabs) * MLA_QSCALE).astype(BF16)
            km_ref[:, sl] = (kv[:, sl] + kr).astype(BF16)
        vm_ref[...] = kv[:, 512:].astype(BF16)
        for p in range(2):
            for s in range(sub):
                tile = kv[s * ATT_BLK:(s + 1) * ATT_BLK, 512 + p * LANES:512 + (p + 1) * LANES]
                vt_ref[p, s] = jnp.transpose(tile).astype(BF16)

    return pl.pallas_call(
        body, name="mla_prep_fwd", grid=(T // tm,), in_specs=_mla_lat_specs(tm),
        out_specs=[pl.BlockSpec((tm, 512), lambda i: (i, 0)),
                   pl.BlockSpec((tm, 512), lambda i: (i, 0)),
                   pl.BlockSpec((tm, 256), lambda i: (i, 0)),
                   pl.BlockSpec((2, sub, LANES, ATT_BLK), lambda i: (0, i, 0, 0))],
        out_shape=[jax.ShapeDtypeStruct((T, 512), BF16),
                   jax.ShapeDtypeStruct((T, 512), BF16),
                   jax.ShapeDtypeStruct((T, 256), BF16),
                   jax.ShapeDtypeStruct((2, T // ATT_BLK, LANES, ATT_BLK), BF16)],
        compiler_params=_cparams(("parallel",)),
    )(hf, hf, hf, rope, g_q, g_kv, w_uq, w_ukv)


def _mla_prep_bwd(hf, rope, g_q, g_kv, w_uq, w_ukv, dqm, dkt, dvt):
    T = hf.shape[0]
    tm = 512
    sub = tm // ATT_BLK

    def body(cq_ref, ckv_ref, ckr_ref, tab_ref, gq_ref, gkv_ref, wq_ref, wkv_ref, dq_ref, dk_ref, dv_ref,
             dc_ref, dwq_ref, dwkv_ref, dgq_ref, dgkv_ref):
        @pl.when(pl.program_id(0) == 0)
        def _():
            dwq_ref[...] = jnp.zeros_like(dwq_ref)
            dwkv_ref[...] = jnp.zeros_like(dwkv_ref)
            dgq_ref[...] = jnp.zeros_like(dgq_ref)
            dgkv_ref[...] = jnp.zeros_like(dgkv_ref)

        tabs = _rope_factors(tab_ref)
        lane =lax.broadcasted_iota(jnp.int32, (1, LANES), 1)
        dq = jnp.concatenate([_rope_t(dq_ref[:, h * LANES:(h + 1) * LANES] * MLA_QSCALE, tabs)
                              for h in range(4)], axis=1).astype(BF16)
        cq = cq_ref[...]
        rq = _rms(cq)
        cqn = (cq * rq * gq_ref[...]).astype(BF16)
        dwq_ref[...] += _dot_tn(cqn, dq)
        dcq, dgrow = _rms_bwd(_dot_nt(dq, wq_ref[...]), cq * rq, rq, gq_ref[...])
        dgq_ref[...] += _colsum(dgrow)
        dc_ref[:, 0:256] = dcq.astype(BF16)

        dk = jnp.concatenate([jnp.concatenate([jnp.transpose(dk_ref[p, s]) for p in range(2)], axis=1)
                              for s in range(sub)], axis=0) * LN2
        dv = jnp.concatenate([jnp.concatenate([jnp.transpose(dv_ref[p, s]) for p in range(2)], axis=1)
                              for s in range(sub)], axis=0)
        dkr = dk[:, 0:LANES] + dk[:, LANES:2 * LANES] + dk[:, 2 * LANES:3 * LANES] + dk[:, 3 * LANES:]
        dkr = pltpu.roll(_rope_t(dkr, tabs), HEAD, 1)
        dc_ref[:, 384:512] = jnp.where(lane < 2 * ROPE_HALF, dkr, 0.0).astype(BF16)
        dkv = jnp.concatenate([dk.astype(BF16), dv.astype(BF16)], axis=1)
        ckv = ckv_ref[...]
        rkv = _rms(ckv)
        ckvn = (ckv * rkv * gkv_ref[...]).astype(BF16)
        dwkv_ref[...] += _dot_tn(ckvn, dkv)
        dckv, dgrow = _rms_bwd(_dot_nt(dkv, wkv_ref[...]), ckv * rkv, rkv, gkv_ref[...])
        dgkv_ref[...] += _colsum(dgrow)
        dc_ref[:, 256:384] = dckv.astype(BF16)

    return pl.pallas_call(
        body, name="mla_prep_bwd", grid=(T // tm,),
        in_specs=_mla_lat_specs(tm) + [pl.BlockSpec((tm, 512), lambda i: (i, 0)),
                                       pl.BlockSpec((2, sub, 256, ATT_BLK), lambda i: (0, i, 0, 0)),
                                       pl.BlockSpec((2, sub, LANES, ATT_BLK), lambda i: (0, i, 0, 0))],
        out_specs=[pl.BlockSpec((tm, 512), lambda i: (i, 0)),
                   pl.BlockSpec((256, 512), lambda i: (0, 0)),
                   pl.BlockSpec((LANES, 768), lambda i: (0, 0)),
                   pl.BlockSpec((1, 256), lambda i: (0, 0)),
                   pl.BlockSpec((1, LANES), lambda i: (0, 0))],
        out_shape=[jax.ShapeDtypeStruct((T, 512), BF16),
                   jax.ShapeDtypeStruct((256, 512), F32),
                   jax.ShapeDtypeStruct((LANES, 768), F32),
                   jax.ShapeDtypeStruct((1, 256), F32),
                   jax.ShapeDtypeStruct((1, LANES), F32)],
        compiler_params=_cparams(("arbitrary",)),
    )(hf, hf, hf, rope, g_q, g_kv, w_uq, w_ukv, dqm, dkt, dvt)


def _causal_masks(bq, bk):
    row = lax.broadcasted_iota(jnp.int32, (bq, bk), 0)
    col = lax.broadcasted_iota(jnp.int32, (bq, bk), 1)
    return row, col


def _mla_fwd(qm, km, vt):
    T = qm.shape[0]
    bq, bk = min(MLA_BQ, T), ATT_BLK
    nq, nsub, nk = T // bq, bq // bk, T // bk

    def body(q_ref, k_ref, vt_ref, o_ref, lse_ref, acc_ref, m_ref, l_ref):
        qi = pl.program_id(0)
        key = lax.broadcasted_iota(jnp.int32, (bk, bq), 0)
        qry = lax.broadcasted_iota(jnp.int32, (bk, bq), 1)
        ones = jnp.ones((8, bk), BF16)
        acc_ref[...] = jnp.zeros_like(acc_ref)
        m_ref[...] = jnp.full_like(m_ref, NEG)
        l_ref[...] = jnp.zeros_like(l_ref)

        def step(kb0, masked):
            kbs = [kb0 + d for d in range(nsub)]
            sts = [[_dot_nt(k_ref[pl.ds(pl.multiple_of(kb * bk, bk), bk), e * LANES:(e + 1) * LANES],
                            q_ref[:, e * LANES:(e + 1) * LANES]) for kb in kbs] for e in range(4)]
            pts, alphas = [], []
            for e in range(4):
                st = [jnp.where(key + d * bk <= qry, sts[e][d], NEG) for d in range(nsub)] if masked else sts[e]
                m_prev = m_ref[e, 0:1, :]
                m_new = m_prev
                for d in range(nsub):
                    m_new = jnp.maximum(m_new, jnp.max(st[d], axis=0, keepdims=True))
                alpha = jnp.exp2(m_prev - m_new)
                pt = [jnp.exp2(st[d] - m_new).astype(BF16) for d in range(nsub)]
                l_new = alpha * l_ref[e]
                for d in range(nsub):
                    l_new = l_new + _dot(ones, pt[d])
                l_ref[e] = l_new
                m_ref[e] = jnp.broadcast_to(m_new, (8, bq))
                pts.append(pt)
                alphas.append(alpha)
            for e in range(4):
                acc = alphas[e] * acc_ref[e]
                for d in range(nsub):
                    v_t = vt_ref[e // 2, kbs[d], (e % 2) * HEAD:(e % 2 + 1) * HEAD, :]
                    acc = acc + _dot(v_t, pts[e][d])
                acc_ref[e] = acc

        step(qi * nsub, True)

        def loop(t, c):
            step(t * nsub, False)
            return c

        lax.fori_loop(0, qi, loop, 0)
        outs, lses = [], []
        for e in range(4):
            l = l_ref[e, 0:1, :]
            outs.append(acc_ref[e] / l)
            lses.append(jnp.broadcast_to(m_ref[e, 0:1, :] * LN2 + jnp.log(l), (HEAD, bq)))
        o_ref[...] = jnp.transpose(jnp.concatenate(outs, axis=0))
        lse_ref[...] = jnp.transpose(jnp.concatenate(lses, axis=0))

    return pl.pallas_call(
        body, name="mla_fwd", grid=(nq,),
        in_specs=[pl.BlockSpec((bq, 512), lambda i: (i, 0)),
                  pl.BlockSpec((T, 512), lambda i: (0, 0)),
                  pl.BlockSpec((2, nk, LANES, bk), lambda i: (0, 0, 0, 0))],
        out_specs=[pl.BlockSpec((bq, 256), lambda i: (i, 0)),
                   pl.BlockSpec((bq, 256), lambda i: (i, 0))],
        out_shape=[jax.ShapeDtypeStruct((T, 256), F32), jax.ShapeDtypeStruct((T, 256), F32)],
        scratch_shapes=[pltpu.VMEM((4, HEAD, bq), F32), pltpu.VMEM((4, 8, bq), F32), pltpu.VMEM((4, 8, bq), F32)],
        compiler_params=_cparams(("arbitrary",)),
    )(qm, km, vt)


def _mla_bwd(qm, km, vm, y, lse, dy):
    T = qm.shape[0]
    bq, bk = min(MLA_BQ, T), ATT_BLK
    nq, nsub, nk = T // bq, bq // bk, T // bk

    def body(q_ref, k_ref, v_ref, y_ref, lse_ref, dy_ref, dq_ref, dkt_ref, dvt_ref, dob_ref, st_ref, qt_ref, dot_ref):
        qi = pl.program_id(1)

        @pl.when(qi == 0)
        def _():
            dkt_ref[...] = jnp.zeros_like(dkt_ref)
            dvt_ref[...] = jnp.zeros_like(dvt_ref)

        lane = lax.broadcasted_iota(jnp.int32, (1, LANES), 1) // HEAD
        row, col = _causal_masks(bq, bk)
        dq_ref[...] = jnp.zeros_like(dq_ref)
        lse = lse_ref[...]
        lse_other = pltpu.roll(lse, HEAD, 1)
        qt_ref[...] = jnp.transpose(q_ref[...].astype(F32)).astype(BF16)
        dot_ref[...] = jnp.transpose(dy_ref[...]).astype(BF16)
        for e in range(2):
            do = jnp.where(lane == e, dy_ref[...], 0.0)
            dob_ref[e] = do.astype(BF16)
            st_ref[2 * e] = jnp.where(lane == e, lse, lse_other) * LOG2E
            st_ref[2 * e + 1] = jnp.broadcast_to(jnp.sum(do * y_ref[...], axis=1, keepdims=True), (bq, LANES))

        hss = [slice(e * LANES, (e + 1) * LANES) for e in range(2)]
        tile = lambda a: jnp.concatenate([a] * (bk // LANES), axis=1)

        def step(kb0, masked):
            kbs = [kb0 + d for d in range(nsub)]
            rows = [pl.ds(pl.multiple_of(kb * bk, bk), bk) for kb in kbs]
            pairs = [(d, e) for d in range(nsub) for e in range(2)]
            ss = {(d, e): _dot_nt(q_ref[:, hss[e]], k_ref[rows[d], hss[e]]) for d, e in pairs}
            dps = {(d, e): _dot_nt(dob_ref[e], jnp.where(lane == e, v_ref[rows[d], :], 0)) for d, e in pairs}
            ps, dss = {}, {}
            for d, e in pairs:
                s = jnp.where(col + d * bk <= row, ss[d, e], NEG) if masked else ss[d, e]
                p = jnp.exp2(s - tile(st_ref[2 * e]))
                dss[d, e] = (p * (dps[d, e] - tile(st_ref[2 * e + 1]))).astype(BF16)
                ps[d, e] = p.astype(BF16)
            for d, e in pairs:
                dvt_ref[0, kbs[d], e * HEAD:(e + 1) * HEAD, :] += _dot(dot_ref[e * HEAD:(e + 1) * HEAD, :], ps[d, e])
            for d, e in pairs:
                dkt_ref[0, kbs[d], hss[e], :] += _dot(qt_ref[hss[e], :], dss[d, e])
            for e in range(2):
                dq = dq_ref[:, hss[e]]
                for d in range(nsub):
                    dq = dq + _dot(dss[d, e], k_ref[rows[d], hss[e]])
                dq_ref[:, hss[e]] = dq

        step(qi * nsub, True)

        def loop(t, c):
            step(t * nsub, False)
            return c

        lax.fori_loop(0, qi, loop, 0)
        dq_ref[...] *= LN2

    return pl.pallas_call(
        body, name="mla_bwd", grid=(2, nq),
        in_specs=[pl.BlockSpec((bq, 256), lambda j, i: (i, j)),
                  pl.BlockSpec((T, 256), lambda j, i: (0, j)),
                  pl.BlockSpec((T, LANES), lambda j, i: (0, j)),
                  pl.BlockSpec((bq, LANES), lambda j, i: (i, j)),
                  pl.BlockSpec((bq, LANES), lambda j, i: (i, j)),
                  pl.BlockSpec((bq, LANES), lambda j, i: (i, j))],
        out_specs=[pl.BlockSpec((bq, 256), lambda j, i: (i, j)),
                   pl.BlockSpec((1, nk, 256, bk), lambda j, i: (j, 0, 0, 0)),
                   pl.BlockSpec((1, nk, LANES, bk), lambda j, i: (j, 0, 0, 0))],
        out_shape=[jax.ShapeDtypeStruct((T, 512), F32),
                   jax.ShapeDtypeStruct((2, nk, 256, bk), F32),
                   jax.ShapeDtypeStruct((2, nk, LANES, bk), F32)],
        scratch_shapes=[pltpu.VMEM((2, bq, LANES), BF16), pltpu.VMEM((4, bq, LANES), F32),
                        pltpu.VMEM((256, bq), BF16), pltpu.VMEM((LANES, bq), BF16)],
        compiler_params=_cparams(("parallel", "arbitrary")),
    )(qm, km, vm, y, lse, dy)


def _suffix_ones(n):
    r = lax.broadcasted_iota(jnp.int32, (n, n), 0)
    c = lax.broadcasted_iota(jnp.int32, (n, n), 1)
    return (r >= c).astype(BF16)


def _prefix_ones(n):
    r = lax.broadcasted_iota(jnp.int32, (n, n), 0)
    c = lax.broadcasted_iota(jnp.int32, (n, n), 1)
    return (r <= c).astype(BF16)


def _sb_specs(T, bq):
    qo, ko, vo = (_INT_OFF[n] // 256 for n in ("d_q", "d_k", "d_v"))
    return [pl.BlockSpec((bq, 256), lambda i: (i, qo)),
            pl.BlockSpec((T, 256), lambda i: (0, ko)),
            pl.BlockSpec((T, 256), lambda i: (0, vo))]


def _sb_fwd(hb):
    T = hb.shape[0]
    bq = bk = ATT_BLK
    nq = T // bq

    def body(q_ref, k_ref, v_ref, o_ref, tot_ref, cnt_ref, qm_ref, car_ref):
        qi = pl.program_id(0)
        lane = lax.broadcasted_iota(jnp.int32, (1, LANES), 1) // HEAD
        row, col = _causal_masks(bq, bk)
        strict = col < row
        u = _suffix_ones(bk)
        o_ref[...] = jnp.zeros_like(o_ref)
        car_ref[...] = jnp.zeros_like(car_ref)
        pair = lambda h: slice((h // 2) * LANES, (h // 2 + 1) * LANES)
        for h in range(4):
            qm_ref[h] = jnp.where(lane == h % 2, q_ref[:, pair(h)], 0) * 0.125

        def step(blocks):
            tile = lambda a: jnp.concatenate([a] * (bk // LANES), axis=1)
            rows = [pl.ds(pl.multiple_of(kb * bk, bk), bk) for kb, _ in blocks]
            pairs = [(b, h) for b in range(len(blocks)) for h in range(4)]
            zs = {(b, h): _dot_nt(qm_ref[h], k_ref[rows[b], pair(h)]) for b, h in pairs}
            splits = {}
            for b, h in pairs:
                z = zs[b, h]
                lk = jnp.minimum(-z, 0.0) - jnp.log(1.0 + jnp.exp(-jnp.abs(z)))
                if blocks[b][1] is not None:
                    lk = jnp.where(blocks[b][1], lk, 0.0)
                splits[b, h] = _split(lk)
            sufs = {bh: _dot(hi, u) + _dot(lo, u) for bh, (hi, lo) in splits.items()}
            car = [car_ref[h] for h in range(4)]
            aas = {}
            for b, h in pairs:
                a = jnp.exp(zs[b, h] + sufs[b, h] + tile(car[h]))
                if blocks[b][1] is not None:
                    a = jnp.where(blocks[b][1], a, 0.0)
                aas[b, h] = a.astype(BF16)
                car[h] = car[h] + jnp.broadcast_to(sufs[b, h][:, 0:1], (bq, LANES))
            acc = [o_ref[:, pair(0)], o_ref[:, pair(2)]]
            for b, h in pairs:
                acc[h // 2] = acc[h // 2] + _dot(aas[b, h], jnp.where(lane == h % 2, v_ref[rows[b], pair(h)], 0))
            o_ref[:, pair(0)], o_ref[:, pair(2)] = acc
            for h in range(4):
                car_ref[h] = car[h]

        step([(qi, strict), (jnp.maximum(qi - 1, 0), qi > 0)])

        def live():
            worst = jnp.maximum(jnp.maximum(car_ref[0], car_ref[1]), jnp.maximum(car_ref[2], car_ref[3]))
            return jnp.max(worst) >= SB_DEAD

        def cond(c):
            return jnp.logical_and(c[0] < qi, c[1])

        def loop(c):
            step([(qi - 1 - c[0], None)])
            return c[0] + 1, live()

        done, _ = lax.while_loop(cond, loop, (jnp.minimum(qi, 1), live()))
        tot_ref[:, pair(0)] = jnp.where(lane == 0, car_ref[0], car_ref[1])
        tot_ref[:, pair(2)] = jnp.where(lane == 0, car_ref[2], car_ref[3])
        cnt_ref[0, qi] = done.astype(F32)

    return pl.pallas_call(
        body, name="sb_fwd", grid=(nq,), in_specs=_sb_specs(T, bq),
        out_specs=[pl.BlockSpec((bq, 256), lambda i: (i, 0)), pl.BlockSpec((bq, 256), lambda i: (i, 0)),
                   pl.BlockSpec(memory_space=pltpu.SMEM)],
        out_shape=[jax.ShapeDtypeStruct((T, 256), F32), jax.ShapeDtypeStruct((T, 256), F32),
                   jax.ShapeDtypeStruct((1, nq), F32)],
        scratch_shapes=[pltpu.VMEM((4, bq, LANES), BF16), pltpu.VMEM((4, bq, LANES), F32)],
        compiler_params=_cparams(("arbitrary",)),
    )(hb, hb, hb)


def _sb_bwd(hb, tot, cnt, dy):
    T = hb.shape[0]
    bq = bk = ATT_BLK
    nq = T // bq

    def body(q_ref, k_ref, v_ref, tot_ref, dy_ref, cnt_ref, dq_ref, dk_ref, dv_ref, qm_ref, dob_ref, dqa_ref, rem_ref,
             cg_ref):
        qi = pl.program_id(0)

        @pl.when(qi == 0)
        def _():
            dk_ref[...] = jnp.zeros_like(dk_ref)
            dv_ref[...] = jnp.zeros_like(dv_ref)

        lane = lax.broadcasted_iota(jnp.int32, (1, LANES), 1) // HEAD
        row, col = _causal_masks(bq, bk)
        strict = col < row
        u = _prefix_ones(bk)
        pair = lambda h: slice((h // 2) * LANES, (h // 2 + 1) * LANES)
        dqa_ref[...] = jnp.zeros_like(dqa_ref)
        cg_ref[...] = jnp.zeros_like(cg_ref)
        for h in range(4):
            tot = tot_ref[:, pair(h)]
            qm_ref[h] = jnp.where(lane == h % 2, q_ref[:, pair(h)], 0) * 0.125
            dob_ref[h] = jnp.where(lane == h % 2, dy_ref[:, pair(h)], 0.0).astype(BF16)
            rem_ref[h] = jnp.where(lane == h % 2, tot, pltpu.roll(tot, HEAD, 1))

        def step(blocks):
            tile = lambda a: jnp.concatenate([a] * (bk // LANES), axis=1)
            nb = len(blocks)
            rows = [pl.ds(pl.multiple_of(kb * bk, bk), bk) for kb, _ in blocks]
            pairs = [(b, h) for b in range(nb) for h in range(4)]
            mask = lambda b, x: x if blocks[b][1] is None else jnp.where(blocks[b][1], x, 0.0)
            zs = {(b, h): _dot_nt(qm_ref[h], k_ref[rows[b], pair(h)]) for b, h in pairs}
            das = {(b, h): _dot_nt(dob_ref[h], jnp.where(lane == h % 2, v_ref[rows[b], pair(h)], 0)) for b, h in pairs}
            zls, splits = {}, {}
            for b, h in pairs:
                z = zs[b, h]
                lk = mask(b, jnp.minimum(-z, 0.0) - jnp.log(1.0 + jnp.exp(-jnp.abs(z))))
                zls[b, h] = z + lk
                splits[b, h] = _split(lk)
            pres = {bh: _dot(hi, u) + _dot(lo, u) for bh, (hi, lo) in splits.items()}
            rem = [rem_ref[h] for h in range(4)]
            aas, gs, gsplits = {}, {}, {}
            for b, h in pairs:
                a = mask(b, jnp.exp(zls[b, h] + (tile(rem[h]) - pres[b, h])))
                gs[b, h] = a * das[b, h]
                aas[b, h] = a.astype(BF16)
                gsplits[b, h] = _split(gs[b, h])
                rem[h] = rem[h] - jnp.broadcast_to(pres[b, h][:, bk - 1:bk], (bq, LANES))
            for b in range(nb):
                for p in (0, 2):
                    dv_ref[rows[b], pair(p)] += _dot_tn(aas[b, p], dob_ref[p]) + _dot_tn(aas[b, p + 1], dob_ref[p + 1])
            gpres = {bh: _dot(hi, u) + _dot(lo, u) for bh, (hi, lo) in gsplits.items()}
            cg = [cg_ref[h] for h in range(4)]
            dzs = {}
            for b, h in pairs:
                dz = mask(b, gs[b, h] - jnp.exp(zls[b, h]) * (tile(cg[h]) + gpres[b, h]))
                dzs[b, h] = dz.astype(BF16)
                cg[h] = cg[h] + jnp.broadcast_to(gpres[b, h][:, bk - 1:bk], (bq, LANES))
            for b in range(nb):
                for p in (0, 2):
                    dk_ref[rows[b], pair(p)] += _dot_tn(dzs[b, p], qm_ref[p]) + _dot_tn(dzs[b, p + 1], qm_ref[p + 1])
            for h in range(4):
                dq = dqa_ref[h]
                for b in range(nb):
                    dq = dq + _dot(dzs[b, h], k_ref[rows[b], pair(h)])
                dqa_ref[h] = dq
                rem_ref[h] = rem[h]
                cg_ref[h] = cg[h]

        def loop(kb, c):
            step([(kb, None)])
            return c

        start = qi - jnp.clip(cnt_ref[0, qi].astype(jnp.int32), 0, qi)
        lax.fori_loop(start, qi - 1, loop, 0)
        step([(jnp.maximum(qi - 1, 0), qi > 0), (qi, strict)])
        for p in (0, 2):
            dq_ref[:, pair(p)] = (jnp.where(lane == 0, dqa_ref[p], dqa_ref[p + 1]) * 0.125).astype(BF16)

    return pl.pallas_call(
        body, name="sb_bwd", grid=(nq,),
        in_specs=_sb_specs(T, bq) + [pl.BlockSpec((bq, 256), lambda i: (i, 0)),
                                     pl.BlockSpec((bq, 256), lambda i: (i, 0)),
                                     pl.BlockSpec(memory_space=pltpu.SMEM)],
        out_specs=[pl.BlockSpec((bq, 256), lambda i: (i, 0)),
                   pl.BlockSpec((T, 256), lambda i: (0, 0)),
                   pl.BlockSpec((T, 256), lambda i: (0, 0))],
        out_shape=[jax.ShapeDtypeStruct((T, 256), BF16)] + [jax.ShapeDtypeStruct((T, 256), F32)] * 2,
        scratch_shapes=[pltpu.VMEM((4, bq, LANES), BF16), pltpu.VMEM((4, bq, LANES), BF16),
                        pltpu.VMEM((4, bq, LANES), F32), pltpu.VMEM((4, bq, LANES), F32),
                        pltpu.VMEM((4, bq, LANES), F32)],
        compiler_params=_cparams(("arbitrary",)),
    )(hb, hb, hb, tot, dy, cnt)


EP_TM = 512


def _ep_in_specs(tm, rev):
    idx = (lambda i: rev - i) if rev is not None else (lambda i: i)
    bo = (_INT_OFF["b_b"] - N_HB) // 256
    halo = lambda i: jnp.maximum(idx(i) * (tm // 8) - 1, 0)
    return [pl.BlockSpec((tm, 256), lambda i: (idx(i), 0)),
            pl.BlockSpec((tm, 256), lambda i: (idx(i), 0)),
            pl.BlockSpec((tm, 256), lambda i: (idx(i), 0)),
            pl.BlockSpec((tm, D_MODEL), lambda i: (idx(i), 0)),
            pl.BlockSpec((tm, 256), lambda i: (idx(i), bo)),
            pl.BlockSpec((tm, 256), lambda i: (idx(i), bo + 1)),
            pl.BlockSpec((tm, 256), lambda i: (idx(i), bo + 2)),
            pl.BlockSpec((8, 256), lambda i: (halo(i), bo + 1)),
            pl.BlockSpec((8, 256), lambda i: (halo(i), bo + 2)),
            pl.BlockSpec((3, 256), lambda i: (0, 0)),
            pl.BlockSpec((1, 256), lambda i: (0, 0)),
            pl.BlockSpec((1, D_MODEL), lambda i: (0, 0)),
            pl.BlockSpec((D_MODEL, D_MODEL), lambda i: (0, 0)),
            pl.BlockSpec((1, D_MODEL), lambda i: (0, 0))]


def _ep_mix(first, ya_ref, yc_ref, yd_ref, gate_ref, bb_ref, bc_ref, bx_ref, hc_ref, hx_ref, cw_ref, cb_ref, gg_ref):
    tm = ya_ref.shape[0]
    u = bc_ref[...] * bx_ref[...]
    halo = jnp.where(first, 0.0, hc_ref[...] * hx_ref[...])
    row = lax.broadcasted_iota(jnp.int32, (tm, 1), 0)
    u1 = jnp.where(row == 0, halo[7:8, :], pltpu.roll(u, 1, 0))
    u2 = jnp.where(row == 0, halo[6:7, :], jnp.where(row == 1, halo[7:8, :], pltpu.roll(u, 2, 0)))
    cw = cw_ref[...]
    conv = cw[0:1, :] * u2 + cw[1:2, :] * u1 + cw[2:3, :] * u + cb_ref[...]
    bb = bb_ref[...]
    ys = [ya_ref[...], bb * conv, yc_ref[...], yd_ref[...]]
    rs = [_rms(y) for y in ys]
    gg = gg_ref[...]
    yhat = jnp.concatenate([y * r for y, r in zip(ys, rs)], axis=1)
    gate = gate_ref[...]
    sig = 1.0 / (1.0 + jnp.exp(-gate))
    return u, u1, u2, conv, bb, rs, yhat, yhat * gg, gate, sig


def _epilogue_fwd(x, ya, yc, yd, hf, conv_w, conv_b, g_grp, w_out, g_post, tgt=None):
    T = x.shape[0]
    tm = EP_TM
    row_spec = pl.BlockSpec((tm, D_MODEL), lambda i: (i, 0))

    def layer_out(refs):
        (x_ref, ya_ref, yc_ref, yd_ref, gate_ref, bb_ref, bc_ref, bx_ref, hc_ref, hx_ref, cw_ref, cb_ref,
         gg_ref, wo_ref, gp_ref) = refs
        (_, _, _, _, _, _, _, yn, gate, sig) = _ep_mix(
            pl.program_id(0) == 0, ya_ref, yc_ref, yd_ref, gate_ref, bb_ref, bc_ref, bx_ref, hc_ref, hx_ref,
            cw_ref, cb_ref, gg_ref)
        z = _dot((yn * (gate * sig)).astype(BF16), wo_ref[...])
        return x_ref[...] + z * _rms(z) * gp_ref[...]

    args = (x, ya, yc, yd, hf, hf, hf, hf, hf, hf, conv_w, conv_b, g_grp, w_out, g_post)
    in_specs = [row_spec] + _ep_in_specs(tm, None)
    if tgt is None:
        def body(*refs):
            refs[-1][...] = layer_out(refs[:-1])

        return pl.pallas_call(
            body, name="epilogue_fwd", grid=(T // tm,), in_specs=in_specs, out_specs=row_spec,
            out_shape=jax.ShapeDtypeStruct((T, D_MODEL), F32), compiler_params=_cparams(("parallel",)),
        )(*args)

    def body_loss(*refs):
        t_ref, dy_ref, l_ref = refs[-3:]

        @pl.when(pl.program_id(0) == 0)
        def _():
            l_ref[...] = jnp.zeros_like(l_ref)

        d = layer_out(refs[:-3]) - t_ref[...]
        dy_ref[...] = d * (1.0 / D_MODEL)
        part = jnp.sum(jnp.sum(d * d, axis=1, keepdims=True), axis=0, keepdims=True)
        l_ref[...] += part * (0.5 / D_MODEL)

    return pl.pallas_call(
        body_loss, name="epilogue_fwd_loss", grid=(T // tm,), in_specs=in_specs + [row_spec],
        out_specs=[row_spec, pl.BlockSpec((8, LANES), lambda i: (0, 0))],
        out_shape=[jax.ShapeDtypeStruct((T, D_MODEL), F32), jax.ShapeDtypeStruct((8, LANES), F32)],
        compiler_params=_cparams(("arbitrary",)),
    )(*args, tgt)


def _epilogue_bwd(dxn, ya, yc, yd, hf, conv_w, conv_b, g_grp, w_out, g_post):
    T = dxn.shape[0]
    tm = EP_TM
    nt = T // tm
    ridx = lambda i: (nt - 1 - i, 0)

    def body(dx_ref, ya_ref, yc_ref, yd_ref, gate_ref, bb_ref, bc_ref, bx_ref, hc_ref, hx_ref, cw_ref, cb_ref,
             gg_ref, wo_ref, gp_ref,
             dya_ref, dyc_ref, dyd_ref, dhf_ref, dwo_ref, dgp_ref, dgg_ref, dcw_ref, dcb_ref, carry_ref):
        i = pl.program_id(0)

        @pl.when(i == 0)
        def _():
            for r in (dwo_ref, dgp_ref, dgg_ref, dcw_ref, dcb_ref, carry_ref):
                r[...] = jnp.zeros_like(r)

        (u, u1, u2, conv, bb, rs, yhat, yn, gate, sig) = _ep_mix(
            i == nt - 1, ya_ref, yc_ref, yd_ref, gate_ref, bb_ref, bc_ref, bx_ref, hc_ref, hx_ref,
            cw_ref, cb_ref, gg_ref)
        silu = gate * sig
        ymix = (yn * silu).astype(BF16)
        z = _dot(ymix, wo_ref[...])
        rz = _rms(z)
        dz, dgrow = _rms_bwd(dx_ref[...], z * rz, rz, gp_ref[...])
        dgp_ref[...] += _colsum(dgrow)
        dzb = dz.astype(BF16)
        dwo_ref[...] += _dot_tn(ymix, dzb)
        dymix = _dot_nt(dzb, wo_ref[...])
        dhf_ref[:, 0:D_MODEL] = (dymix * yn * (sig * (1.0 + gate * (1.0 - sig)))).astype(BF16)
        dyn = dymix * silu
        dgg_ref[...] += _colsum(dyn * yhat)
        gg = gg_ref[...]
        dys = []
        for gi in range(4):
            sl = slice(gi * GROUP, (gi + 1) * GROUP)
            dyh = dyn[:, sl] * gg[:, sl]
            yh = yhat[:, sl]
            dys.append(rs[gi] * (dyh - yh * jnp.mean(dyh * yh, axis=-1, keepdims=True)))
        dya_ref[...] = dys[0]
        dyc_ref[...] = dys[2]
        dyd_ref[...] = dys[3]
        dyb = dys[1]
        dhf_ref[:, D_MODEL:D_MODEL + 256] = (dyb * conv).astype(BF16)
        dconv = dyb * bb
        dcb_ref[...] += _colsum(dconv)
        dcw_ref[0:1, :] += _colsum(dconv * u2)
        dcw_ref[1:2, :] += _colsum(dconv * u1)
        dcw_ref[2:3, :] += _colsum(dconv * u)
        carry = carry_ref[...]
        row = lax.broadcasted_iota(jnp.int32, (tm, 1), 0)
        d1 = jnp.where(row == tm - 1, carry[0:1, :], pltpu.roll(dconv, tm - 1, 0))
        d2 = jnp.where(row == tm - 2, carry[0:1, :],
                       jnp.where(row == tm - 1, carry[1:2, :], pltpu.roll(dconv, tm - 2, 0)))
        cw = cw_ref[...]
        du = cw[2:3, :] * dconv + cw[1:2, :] * d1 + cw[0:1, :] * d2
        dhf_ref[:, D_MODEL + 256:D_MODEL + 512] = (du * bx_ref[...]).astype(BF16)
        dhf_ref[:, D_MODEL + 512:D_MODEL + 768] = (du * bc_ref[...]).astype(BF16)
        carry_ref[...] = dconv[0:8, :]

    in_specs = [pl.BlockSpec((tm, D_MODEL), ridx)] + _ep_in_specs(tm, nt - 1)
    return pl.pallas_call(
        body, name="epilogue_bwd", grid=(nt,), in_specs=in_specs,
        out_specs=[pl.BlockSpec((tm, 256), ridx), pl.BlockSpec((tm, 256), ridx), pl.BlockSpec((tm, 256), ridx),
                   pl.BlockSpec((tm, D_MODEL + 768), ridx),
                   pl.BlockSpec((D_MODEL, D_MODEL), lambda i: (0, 0)),
                   pl.BlockSpec((1, D_MODEL), lambda i: (0, 0)),
                   pl.BlockSpec((1, D_MODEL), lambda i: (0, 0)),
                   pl.BlockSpec((8, 256), lambda i: (0, 0)),
                   pl.BlockSpec((1, 256), lambda i: (0, 0))],
        out_shape=[jax.ShapeDtypeStruct((T, 256), F32)] * 3
                  + [jax.ShapeDtypeStruct((T, D_MODEL + 768), BF16),
                     jax.ShapeDtypeStruct((D_MODEL, D_MODEL), F32),
                     jax.ShapeDtypeStruct((1, D_MODEL), F32),
                     jax.ShapeDtypeStruct((1, D_MODEL), F32),
                     jax.ShapeDtypeStruct((8, 256), F32),
                     jax.ShapeDtypeStruct((1, 256), F32)],
        scratch_shapes=[pltpu.VMEM((8, 256), F32)],
        compiler_params=_cparams(("arbitrary",)),
    )(dxn, ya, yc, yd, hf, hf, hf, hf, hf, hf, conv_w, conv_b, g_grp, w_out, g_post)


def _place():
    return lax.axis_index("x"), lax.axis_index("y"), lax.axis_index("c")


def _other_chips(x, y):
    return [(1 - x, y), (x, 1 - y), (1 - x, 1 - y)]


HBM = pl.BlockSpec(memory_space=pl.ANY)


def _gather_weights(shards):
    n = len(shards)

    def body(*refs):
        ins, outs = refs[:n], refs[n:2 * n]
        ici_send, ici_recv, d2d_send, d2d_recv, local_sems = refs[2 * n:]
        x, y, c = _place()
        me = 2 * x + y
        chips = _other_chips(x, y)

        def ici(a, j, layer_from):
            px, py = chips[j]
            return pltpu.make_async_remote_copy(
                src_ref=ins[a].at[c], dst_ref=outs[a].at[layer_from, c], send_sem=ici_send.at[3 * a + j],
                recv_sem=ici_recv.at[3 * a + j], device_id=(px, py, c), device_id_type=MESH)

        def d2d(a, j, layer):
            px, py = chips[j]
            blk = outs[a].at[2 * px + py, layer]
            return pltpu.make_async_remote_copy(
                src_ref=blk, dst_ref=blk, send_sem=d2d_send.at[3 * a + j], recv_sem=d2d_recv.at[3 * a + j],
                device_id=(x, y, 1 - c), device_id_type=MESH)

        local = [pltpu.make_async_copy(ins[a], outs[a].at[me], local_sems.at[a]) for a in range(n)]
        for cp in local:
            cp.start()
        sends = [ici(a, j, me) for j in (2, 0, 1) for a in range(n)]
        for cp in sends:
            cp.start()
        for j in range(3):
            px, py = chips[j]
            for a in range(n):
                ici(a, j, 2 * px + py).wait_recv()
                fwd = d2d(a, j, c)
                fwd.start()
                sends.append(fwd)
        for j in range(3):
            for a in range(n):
                d2d(a, j, 1 - c).wait_recv()
        for cp in sends:
            cp.wait_send()
        for cp in local:
            cp.wait()

    return pl.pallas_call(
        body, name="gather_weights",
        in_specs=[HBM] * n, out_specs=[HBM] * n,
        out_shape=[jax.ShapeDtypeStruct((4,) + s.shape, s.dtype) for s in shards],
        scratch_shapes=[pltpu.SemaphoreType.DMA((3 * n,))] * 4 + [pltpu.SemaphoreType.DMA((n,))],
    )(*shards)


def _exchange_chips(parts, small):
    n = len(parts)

    def body(*refs):
        ins, sm_ref = refs[:n], refs[n]
        outs, osm_ref = refs[n + 1:2 * n + 1], refs[2 * n + 1]
        send_sems, recv_sems, ssend_sems, srecv_sems, local_sems = refs[2 * n + 2:]
        x, y, c = _place()
        me = 2 * x + y
        dev = 4 * x + 2 * y + c
        local = [pltpu.make_async_copy(ins[a].at[me], outs[a].at[me], local_sems.at[a]) for a in range(n)]
        local.append(pltpu.make_async_copy(sm_ref, osm_ref.at[dev], local_sems.at[n]))
        for cp in local:
            cp.start()
        sends = []
        for j, (px, py) in enumerate(_other_chips(x, y)):
            for a in range(n):
                cp = pltpu.make_async_remote_copy(
                    src_ref=ins[a].at[2 * px + py], dst_ref=outs[a].at[me], send_sem=send_sems.at[3 * a + j],
                    recv_sem=recv_sems.at[3 * a + j], device_id=(px, py, c), device_id_type=MESH)
                cp.start()
                sends.append(cp)
        flips = [(fx, fy, fc) for fx in (0, 1) for fy in (0, 1) for fc in (0, 1)][1:]
        for j, (fx, fy, fc) in enumerate(flips):
            cp = pltpu.make_async_remote_copy(
                src_ref=sm_ref, dst_ref=osm_ref.at[dev], send_sem=ssend_sems.at[j], recv_sem=srecv_sems.at[j],
                device_id=(x ^ fx, y ^ fy, c ^ fc), device_id_type=MESH)
            cp.start()
            sends.append(cp)
        for j, (px, py) in enumerate(_other_chips(x, y)):
            for a in range(n):
                pltpu.make_async_remote_copy(
                    src_ref=ins[a].at[me], dst_ref=outs[a].at[2 * px + py], send_sem=send_sems.at[3 * a + j],
                    recv_sem=recv_sems.at[3 * a + j], device_id=(px, py, c), device_id_type=MESH).wait_recv()
        for j, (fx, fy, fc) in enumerate(flips):
            src = 4 * (x ^ fx) + 2 * (y ^ fy) + (c ^ fc)
            pltpu.make_async_remote_copy(
                src_ref=sm_ref, dst_ref=osm_ref.at[src], send_sem=ssend_sems.at[j], recv_sem=srecv_sems.at[j],
                device_id=(x ^ fx, y ^ fy, c ^ fc), device_id_type=MESH).wait_recv()
        for cp in sends:
            cp.wait_send()
        for cp in local:
            cp.wait()

    return pl.pallas_call(
        body, name="exchange_chips",
        in_specs=[HBM] * (n + 1), out_specs=[HBM] * (n + 1),
        out_shape=[jax.ShapeDtypeStruct(p.shape, p.dtype) for p in parts]
                  + [jax.ShapeDtypeStruct((8,) + small.shape, small.dtype)],
        scratch_shapes=[pltpu.SemaphoreType.DMA((3 * n,)), pltpu.SemaphoreType.DMA((3 * n,)),
                        pltpu.SemaphoreType.DMA((7,)), pltpu.SemaphoreType.DMA((7,)),
                        pltpu.SemaphoreType.DMA((n + 1,))],
    )(*parts, small)


def _swap_cores(parts, name):
    n = len(parts)

    def body(*refs):
        ins, outs, send_sems, recv_sems = refs[:n], refs[n:2 * n], refs[2 * n], refs[2 * n + 1]
        x, y, c = _place()
        copies = [pltpu.make_async_remote_copy(
            src_ref=ins[a], dst_ref=outs[a], send_sem=send_sems.at[a], recv_sem=recv_sems.at[a],
            device_id=(x, y, 1 - c), device_id_type=MESH) for a in range(n)]
        for cp in copies:
            cp.start()
        for cp in copies:
            cp.wait()

    return pl.pallas_call(
        body, name=name, in_specs=[HBM] * n, out_specs=[HBM] * n,
        out_shape=[jax.ShapeDtypeStruct(p.shape, p.dtype) for p in parts],
        scratch_shapes=[pltpu.SemaphoreType.DMA((n,)), pltpu.SemaphoreType.DMA((n,))],
    )(*parts)


def _tile(rows, cols):
    for cand in (256, 128, 64):
        if rows % cand == 0:
            return cand, cols
    if rows > 64 and cols % 256 == 0:
        return rows, 256
    return rows, cols


def _add(a, b, name):
    L, R, C = a.shape
    tr, tc = _tile(R, C)

    def body(a_ref, b_ref, o_ref):
        o_ref[...] = (a_ref[...] + b_ref[...].astype(F32)).astype(BF16)

    spec = pl.BlockSpec((1, tr, tc), lambda l, i, j: (l, i, j))
    return pl.pallas_call(
        body, name=name, grid=(L, R // tr, C // tc), in_specs=[spec, spec], out_specs=spec,
        out_shape=jax.ShapeDtypeStruct((L, R, C), BF16),
        compiler_params=_cparams(("parallel", "parallel", "parallel")),
    )(a, b)


def _sum_leading(buf, name):
    n, R, C = buf.shape
    tr, tc = _tile(R, C)

    def body(b_ref, o_ref):
        acc = b_ref[0].astype(F32)
        for k in range(1, n):
            acc = acc + b_ref[k].astype(F32)
        o_ref[...] = acc

    return pl.pallas_call(
        body, name=name, grid=(R // tr, C // tc),
        in_specs=[pl.BlockSpec((n, tr, tc), lambda i, j: (0, i, j))],
        out_specs=pl.BlockSpec((tr, tc), lambda i, j: (i, j)),
        out_shape=jax.ShapeDtypeStruct((R, C), F32),
        compiler_params=_cparams(("parallel", "parallel")),
    )(buf)


def _adam_update(w, g, m, v):
    c1 = 1.0 / (1.0 - ADAM_B1 ** ADAM_STEP)
    c2 = 1.0 / (1.0 - ADAM_B2 ** ADAM_STEP)
    mn = ADAM_B1 * m + (1.0 - ADAM_B1) * g
    vn = ADAM_B2 * v + (1.0 - ADAM_B2) * (g * g)
    return -ADAM_LR * ((mn * c1) / (jnp.sqrt(vn * c2) + ADAM_EPS) + ADAM_WD * w), mn, vn


def _adamw_layers(w, m, v, g_mine, g_other, name):
    _, R, C = w.shape
    tr, tc = _tile(R, C)

    def body(w_ref, m_ref, v_ref, gm_ref, go_ref, g_ref, d_ref, mo_ref, vo_ref):
        g = jnp.where(pl.program_id(0) == lax.axis_index("c"), gm_ref[...], go_ref[...])
        g_ref[0] = g
        d_ref[0], mo_ref[0], vo_ref[0] = _adam_update(w_ref[0], g, m_ref[0], v_ref[0])

    spec3 = pl.BlockSpec((1, tr, tc), lambda l, i, j: (l, i, j))
    spec2 = pl.BlockSpec((tr, tc), lambda l, i, j: (i, j))
    return pl.pallas_call(
        body, name=name, grid=(2, R // tr, C // tc),
        in_specs=[spec3] * 3 + [spec2] * 2, out_specs=[spec3] * 4,
        out_shape=[jax.ShapeDtypeStruct(w.shape, F32)] * 4,
        compiler_params=_cparams(("parallel", "parallel", "parallel")),
    )(w, m, v, g_mine, g_other)


PACK_C = 1024
_BIG = ("w_in", "w_out", "mla_w_uq", "mla_w_ukv", "conv_w")
_SMALL = ("norm_pre", "group_norm", "norm_post", "conv_b", "mla_q_norm", "mla_kv_norm", "attn_sinks")
_SMALL_W = {"norm_pre": 1024, "group_norm": 1024, "norm_post": 1024, "conv_b": 256, "mla_q_norm": 256,
            "mla_kv_norm": 128, "attn_sinks": 4}


_LOSS_AT = divmod(DEPTH * sum(_SMALL_W.values()), PACK_C)


def _pack_small(d, loss):
    flat = jnp.concatenate([d[n].reshape(-1) for n in _SMALL] + [loss.reshape(1)])
    return jnp.pad(flat, (0, 8 * PACK_C - flat.shape[0])).reshape(8, PACK_C)


def _adamw_small(w, m, v, got):
    ns = len(_SMALL)

    def body(*refs):
        got_ref = refs[3 * ns]
        outs = refs[3 * ns + 1:]
        gsum = got_ref[0]
        for d in range(1, 8):
            gsum = gsum + got_ref[d]
        outs[4 * ns][...] = gsum[_LOSS_AT[0]:_LOSS_AT[0] + 1, _LOSS_AT[1]:_LOSS_AT[1] + 1]
        off = 0
        for i, name in enumerate(_SMALL):
            wd = _SMALL_W[name]
            rows = []
            for l in range(DEPTH):
                r, c0 = divmod(off + l * wd, PACK_C)
                rows.append(gsum[r:r + 1, c0:c0 + wd])
            off += DEPTH * wd
            g = jnp.concatenate(rows, axis=0)
            delta, mn, vn = _adam_update(refs[i][...], g, refs[ns + i][...], refs[2 * ns + i][...])
            outs[i][...] = g
            outs[ns + i][...] = delta
            outs[2 * ns + i][...] = mn
            outs[3 * ns + i][...] = vn

    shapes = [jax.ShapeDtypeStruct(w[n].shape, F32) for n in _SMALL]
    res = pl.pallas_call(body, name="adamw_small", out_shape=shapes * 4 + [jax.ShapeDtypeStruct((1, 1), F32)])(
        *[w[n] for n in _SMALL], *[m[n] for n in _SMALL], *[v[n] for n in _SMALL], got)
    return [dict(zip(_SMALL, res[k * ns:(k + 1) * ns])) for k in range(4)], res[4 * ns]


def _w_in_internal(wt):
    rows = []
    for n in _INT_ORDER:
        o, wd = _REAL_OFF[n]
        rows.append(wt[o:o + wd])
        if _INT_W[n] != wd:
            rows.append(jnp.zeros((_INT_W[n] - wd, wt.shape[1]), wt.dtype))
    return jnp.concatenate(rows, axis=0)


def _w_in_real(dwt):
    return jnp.concatenate([dwt[_INT_OFF[n]:_INT_OFF[n] + wd] for n, wd in _REAL], axis=0)


def _uq_internal(w):
    return jnp.pad(w.reshape(256, 4, 96), ((0, 0), (0, 0), (0, 32))).reshape(256, 512)


def _uq_real(dw):
    return dw.reshape(256, 4, 128)[:, :, :96].reshape(256, 384)


def _ukv_internal(w):
    w4 = w.reshape(128, 4, 128)
    k = jnp.pad(w4[:, :, :64], ((0, 0), (0, 0), (0, 64))).reshape(128, 512)
    return jnp.concatenate([k, w4[:, :, 64:].reshape(128, 256)], axis=1)


def _ukv_real(dw):
    k = dw[:, :512].reshape(128, 4, 128)[:, :, :64]
    v = dw[:, 512:].reshape(128, 4, 64)
    return jnp.concatenate([k, v], axis=2).reshape(128, 512)


def _layer_fwd(x, rope, p, tgt=None):
    xn, hb, hf = _inproj_fwd(x, p["norm_pre"], p["w_in"])
    ya = _swa_fwd(hb, p["attn_sinks"])
    qm, km, vm, vt = _mla_prep_fwd(hf, rope, p["mla_q_norm"], p["mla_kv_norm"], p["mla_w_uq"], p["mla_w_ukv"])
    yc, lse = _mla_fwd(qm, km, vt)
    yd, tot, cnt = _sb_fwd(hb)
    x_next = _epilogue_fwd(x, ya, yc, yd, hf, p["conv_w"], p["conv_b"], p["group_norm"], p["w_out"], p["norm_post"],
                           tgt)
    return x_next, dict(x=x, xn=xn, hb=hb, hf=hf, ya=ya, yc=yc, yd=yd, tot=tot, cnt=cnt, qm=qm, km=km, vm=vm, lse=lse)


def _layer_bwd(dx_next, rope, p, s):
    (dya, dyc, dyd, dhf, dw_out, dg_post, dg_grp, dconv_w, dconv_b) = _epilogue_bwd(
        dx_next, s["ya"], s["yc"], s["yd"], s["hf"], p["conv_w"], p["conv_b"], p["group_norm"], p["w_out"],
        p["norm_post"])
    dq_d, dk_d, dv_d = _sb_bwd(s["hb"], s["tot"], s["cnt"], dyd)
    dqm, dkt, dvt = _mla_bwd(s["qm"], s["km"], s["vm"], s["yc"], s["lse"], dyc)
    dc, dw_uq, dw_ukv, dg_q, dg_kv = _mla_prep_bwd(
        s["hf"], rope, p["mla_q_norm"], p["mla_kv_norm"], p["mla_w_uq"], p["mla_w_ukv"], dqm, dkt, dvt)
    dq_a, dk_a, dv_a, dsinks = _swa_bwd(s["hb"], p["attn_sinks"], dya)
    dx, dh, dg_pre = _inproj_bwd_dx(s["x"], p["norm_pre"], p["w_in"], dx_next,
                                    [dq_a, dk_a, dv_a, dq_d, dk_d, dv_d, dhf, dc])
    dwt_in = _grad_over_tokens(s["xn"], dh, "inproj_bwd_dw")
    grads = dict(norm_pre=dg_pre[0], w_in_t=_w_in_real(dwt_in), attn_sinks=dsinks[0, :4], conv_w=dconv_w[:3],
                 conv_b=dconv_b[0], mla_q_norm=dg_q[0], mla_w_uq=_uq_real(dw_uq), mla_kv_norm=dg_kv[0],
                 mla_w_ukv=_ukv_real(dw_ukv), group_norm=dg_grp[0], w_out=dw_out, norm_post=dg_post[0])
    return dx, grads


_WEIGHTS = ["norm_pre", "w_in", "attn_sinks", "conv_w", "conv_b", "mla_q_norm", "mla_w_uq", "mla_kv_norm",
            "mla_w_ukv", "group_norm", "w_out", "norm_post"]


def kernel(x, positions, norm_pre, w_in, attn_sinks, conv_w, conv_b, mla_q_norm, mla_w_uq, mla_kv_norm, mla_w_ukv, group_norm, w_out, norm_post, loss_target, m_norm_pre, m_w_in, m_attn_sinks, m_conv_w, m_conv_b, m_mla_q_norm, m_mla_w_uq, m_mla_kv_norm, m_mla_w_ukv, m_group_norm, m_w_out, m_norm_post, v_norm_pre, v_w_in, v_attn_sinks, v_conv_w, v_conv_b, v_mla_q_norm, v_mla_w_uq, v_mla_kv_norm, v_mla_w_ukv, v_group_norm, v_w_out, v_norm_post):
    w = dict(norm_pre=norm_pre, w_in=w_in, attn_sinks=attn_sinks, conv_w=conv_w, conv_b=conv_b,
             mla_q_norm=mla_q_norm, mla_w_uq=mla_w_uq, mla_kv_norm=mla_kv_norm, mla_w_ukv=mla_w_ukv,
             group_norm=group_norm, w_out=w_out, norm_post=norm_post)
    m = dict(norm_pre=m_norm_pre, w_in=m_w_in, attn_sinks=m_attn_sinks, conv_w=m_conv_w, conv_b=m_conv_b,
             mla_q_norm=m_mla_q_norm, mla_w_uq=m_mla_w_uq, mla_kv_norm=m_mla_kv_norm, mla_w_ukv=m_mla_w_ukv,
             group_norm=m_group_norm, w_out=m_w_out, norm_post=m_norm_post)
    v = dict(norm_pre=v_norm_pre, w_in=v_w_in, attn_sinks=v_attn_sinks, conv_w=v_conv_w, conv_b=v_conv_b,
             mla_q_norm=v_mla_q_norm, mla_w_uq=v_mla_w_uq, mla_kv_norm=v_mla_kv_norm, mla_w_ukv=v_mla_w_ukv,
             group_norm=v_group_norm, w_out=v_w_out, norm_post=v_norm_post)
    T = x.shape[1]
    xs = x[0]
    rope = _rope_tables(positions[0].reshape(T, 1))
    tgt = loss_target[0]
    core = lax.axis_index("c")

    gathered = _gather_weights([jnp.swapaxes(w["w_in"], 1, 2).astype(BF16)]
                               + [w[n].astype(BF16) for n in _BIG[1:4]] + [w["conv_w"]])
    full = {}
    for n, got in zip(_BIG, gathered):
        if n in ("w_in", "w_out"):
            full[n] = jnp.moveaxis(got, 0, 1).reshape(DEPTH, 4 * got.shape[2], got.shape[3])
        else:
            full[n] = jnp.transpose(got, (1, 2, 0, 3)).reshape(DEPTH, got.shape[2], 4 * got.shape[3])

    layers = []
    for l in range(DEPTH):
        layers.append(dict(
            norm_pre=norm_pre[l:l + 1], w_in=_w_in_internal(full["w_in"][l]), attn_sinks=attn_sinks[l],
            conv_w=full["conv_w"][l], conv_b=conv_b[l:l + 1], mla_q_norm=mla_q_norm[l:l + 1],
            mla_w_uq=_uq_internal(full["mla_w_uq"][l]), mla_kv_norm=mla_kv_norm[l:l + 1],
            mla_w_ukv=_ukv_internal(full["mla_w_ukv"][l]), group_norm=group_norm[l:l + 1],
            w_out=full["w_out"][l], norm_post=norm_post[l:l + 1]))

    saved = []
    h = xs
    for l in range(DEPTH):
        h, s = _layer_fwd(h, rope, layers[l], tgt if l == DEPTH - 1 else None)
        saved.append(s)
    dy, loss_part = h

    grads = [None] * DEPTH
    for l in reversed(range(DEPTH)):
        dy, grads[l] = _layer_bwd(dy, rope, layers[l], saved[l])

    turned = ("w_in", "mla_w_uq")
    turn = lambda n, a: jnp.swapaxes(a, -1, -2) if n in turned else a

    def chunks(n, a):
        if n in ("w_out", "w_in"):
            return a.reshape(4, a.shape[0] // 4, a.shape[1])
        if n in turned:
            return a.T.reshape(4, a.shape[1] // 4, a.shape[0])
        return jnp.transpose(a.reshape(a.shape[0], 4, a.shape[1] // 4), (1, 0, 2))

    grad = lambda l, n: grads[l]["w_in_t" if n == "w_in" else n]
    mine = [chunks(n, jnp.where(core == 0, grad(0, n), grad(1, n))) for n in _BIG]
    theirs = [chunks(n, jnp.where(core == 0, grad(1, n), grad(0, n))).astype(BF16) for n in _BIG]
    from_sibling = _swap_cores(theirs, "swap_layer_chunks")
    summed = [_add(a, b, "add_cores_" + n) for n, a, b in zip(_BIG, mine, from_sibling)]
    small = _pack_small({n: jnp.stack([grads[l][n] for l in range(DEPTH)]) for n in _SMALL}, loss_part[0, 0])
    *got, got_small = _exchange_chips(summed, small)
    done = [_sum_leading(b, "sum_chips_" + n) for n, b in zip(_BIG, got)]
    done_other = _swap_cores(done, "swap_layer_shards")

    outs, loss = _adamw_small(w, m, v, got_small)
    for n, gm, go in zip(_BIG, done, done_other):
        for d, a in zip(outs, _adamw_layers(turn(n, w[n]), turn(n, m[n]), turn(n, v[n]), gm, go, "adamw_" + n)):
            d[n] = turn(n, a)
    return (loss[0, 0], dy[None], *[outs[0][n] for n in _WEIGHTS], *[outs[1][n] for n in _WEIGHTS],
            *[outs[2][n] for n in _WEIGHTS], *[outs[3][n] for n in _WEIGHTS])
```

```python
import math

import jax
import jax.numpy as jnp
from jax import lax
from jax.experimental import pallas as pl
from jax.experimental.pallas import tpu as pltpu

F32 = jnp.float32
BF16 = jnp.bfloat16
MESH = pl.DeviceIdType.MESH

D_MODEL = 1024
DEPTH = 2
EPS = 1e-6
BLOCK = 128
HEAD = 64
LANES = 128
GROUP = 256
LOG2E = 1.4426950408889634
LN2 = 0.6931471805599453
MLA_QSCALE = 96 ** -0.5 * LOG2E
ROPE_HALF = 16
ROPE_THETA = 10000.0
SWA_SUB = 2
ATT_BLK = 256
MLA_BQ = 512
NEG = -1e30
SB_DEAD = -104.0

ADAM_LR, ADAM_B1, ADAM_B2, ADAM_EPS, ADAM_WD, ADAM_STEP = 0.001, 0.9, 0.999, 1e-08, 0.01, 10

_REAL = [("a_q", 256), ("a_k", 128), ("a_v", 128), ("b_b", 256), ("b_c", 256), ("b_x", 256),
         ("c_q", 256), ("c_kv", 128), ("c_kr", 32), ("d_q", 256), ("d_k", 256), ("d_v", 256),
         ("gate", 1024)]
_REAL_OFF = {}
_o = 0
for _n, _w in _REAL:
    _REAL_OFF[_n] = (_o, _w)
    _o += _w
D_IN = _o
_INT_ORDER = ["a_q", "a_k", "a_v", "d_q", "d_k", "d_v", "gate", "b_b", "b_c", "b_x", "c_q", "c_kv", "c_kr"]
_INT_W = dict(_REAL)
_INT_W["c_kr"] = 128
_INT_OFF = {}
_o = 0
for _n in _INT_ORDER:
    _INT_OFF[_n] = _o
    _o += _INT_W[_n]
N_INT = _o
N_HB = _INT_OFF["gate"]
N_HF = N_INT - N_HB

VMEM_LIMIT = 56 * 1024 * 1024


def _cparams(sem):
    return pltpu.CompilerParams(dimension_semantics=sem, vmem_limit_bytes=VMEM_LIMIT)


def _dot(a, b):
    return jnp.dot(a, b, preferred_element_type=F32)


def _dot_nt(a, b):
    return lax.dot_general(a, b, (((1,), (1,)), ((), ())), preferred_element_type=F32)


def _dot_tn(a, b):
    return lax.dot_general(a, b, (((0,), (0,)), ((), ())), preferred_element_type=F32)


def _split(x):
    hi = x.astype(BF16)
    lo = (x - hi.astype(F32)).astype(BF16)
    return hi, lo


def _rms(x):
    return lax.rsqrt(jnp.mean(x * x, axis=-1, keepdims=True) + EPS)


def _rms_bwd(dy, xhat, r, g):
    dxhat = dy * g
    return r * (dxhat - xhat * jnp.mean(dxhat * xhat, axis=-1, keepdims=True)), dy * xhat


def _colsum(x):
    return jnp.sum(x, axis=0, keepdims=True)


def _inproj_fwd(x, g, wt, fetch=()):
    T = x.shape[0]
    tm = 512
    nt, n = T // tm, len(fetch)

    def body(x_ref, g_ref, w_ref, *rest):
        xn_ref, hb_ref, hf_ref = rest[n:n + 3]
        if n:
            start, finish = _gather_plan(rest[:n], rest[n + 3:2 * n + 3], rest[2 * n + 3:])
            pl.when(pl.program_id(0) == 0)(start)
        xv = x_ref[...]
        xn = (xv * _rms(xv) * g_ref[...]).astype(BF16)
        xn_ref[...] = xn
        h = _dot_nt(xn, w_ref[...])
        hb_ref[...] = h[:, :N_HB].astype(BF16)
        hf_ref[...] = h[:, N_HB:]
        if n:
            pl.when(pl.program_id(0) == nt - 1)(finish)

    return pl.pallas_call(
        body, name="inproj_fwd_fetch" if n else "inproj_fwd", grid=(nt,),
        in_specs=[pl.BlockSpec((tm, D_MODEL), lambda i: (i, 0)),
                  pl.BlockSpec((1, D_MODEL), lambda i: (0, 0)),
                  pl.BlockSpec((N_INT, D_MODEL), lambda i: (0, 0))] + [HBM] * n,
        out_specs=[pl.BlockSpec((tm, D_MODEL), lambda i: (i, 0)),
                   pl.BlockSpec((tm, N_HB), lambda i: (i, 0)),
                   pl.BlockSpec((tm, N_HF), lambda i: (i, 0))] + [HBM] * n,
        out_shape=[jax.ShapeDtypeStruct((T, D_MODEL), BF16),
                   jax.ShapeDtypeStruct((T, N_HB), BF16),
                   jax.ShapeDtypeStruct((T, N_HF), F32)]
                  + [jax.ShapeDtypeStruct((4,) + s.shape, s.dtype) for s in fetch],
        scratch_shapes=_gather_sems(n) if n else [],
        compiler_params=_cparams(("arbitrary",) if n else ("parallel",)),
    )(x, g, wt, *fetch)


def _inproj_bwd_dx(x, g, wt, dx_next, pieces):
    T = x.shape[0]
    tm = 512
    widths = [p.shape[1] for p in pieces]
    assert sum(widths) == N_INT

    def body(x_ref, g_ref, w_ref, dxn_ref, *rest):
        p_refs = rest[:len(pieces)]
        dx_ref, dh_ref, dg_ref = rest[len(pieces):]
        dh = jnp.concatenate([p[...].astype(BF16) for p in p_refs], axis=1)
        dh_ref[...] = dh
        dxn = _dot(dh, w_ref[...])
        xv = x_ref[...]
        r = _rms(xv)
        dx, dgrow = _rms_bwd(dxn, xv * r, r, g_ref[...])
        dx_ref[...] = dx + dxn_ref[...]

        @pl.when(pl.program_id(0) == 0)
        def _():
            dg_ref[...] = jnp.zeros_like(dg_ref)

        dg_ref[...] += _colsum(dgrow)

    return pl.pallas_call(
        body, name="inproj_bwd_dx", grid=(T // tm,),
        in_specs=[pl.BlockSpec((tm, D_MODEL), lambda i: (i, 0)),
                  pl.BlockSpec((1, D_MODEL), lambda i: (0, 0)),
                  pl.BlockSpec((N_INT, D_MODEL), lambda i: (0, 0)),
                  pl.BlockSpec((tm, D_MODEL), lambda i: (i, 0))]
                 + [pl.BlockSpec((tm, wd), lambda i: (i, 0)) for wd in widths],
        out_specs=[pl.BlockSpec((tm, D_MODEL), lambda i: (i, 0)),
                   pl.BlockSpec((tm, N_INT), lambda i: (i, 0)),
                   pl.BlockSpec((1, D_MODEL), lambda i: (0, 0))],
        out_shape=[jax.ShapeDtypeStruct((T, D_MODEL), F32),
                   jax.ShapeDtypeStruct((T, N_INT), BF16),
                   jax.ShapeDtypeStruct((1, D_MODEL), F32)],
        compiler_params=_cparams(("arbitrary",)),
    )(x, g, wt, dx_next, *pieces)


def _grad_over_tokens(a, b, name):
    T, M = a.shape
    N = b.shape[1]
    tm, tn = min(1024, T), 896

    def body(a_ref, b_ref, o_ref):
        @pl.when(pl.program_id(1) == 0)
        def _():
            o_ref[...] = jnp.zeros_like(o_ref)

        o_ref[...] += _dot_tn(b_ref[...], a_ref[...])

    return pl.pallas_call(
        body, name=name, grid=(N // tn, T // tm),
        in_specs=[pl.BlockSpec((tm, M), lambda j, t: (t, 0)),
                  pl.BlockSpec((tm, tn), lambda j, t: (t, j))],
        out_specs=pl.BlockSpec((tn, M), lambda j, t: (j, 0)),
        out_shape=jax.ShapeDtypeStruct((N, M), F32),
        compiler_params=_cparams(("parallel", "arbitrary")),
    )(a, b)


def _roll_f32(x, shift):
    return pltpu.roll(x.astype(F32), shift, 1)


def _swa_operands(h, q, k_prev, k_cur, v_prev, v_cur):
    p, e = h // 2, h % 2
    lane = lax.broadcasted_iota(jnp.int32, (1, LANES), 1) // HEAD
    q = q[:, p * LANES:(p + 1) * LANES]
    if e != p:
        q = _roll_f32(q, HEAD).astype(BF16)
        v_prev = _roll_f32(v_prev, HEAD).astype(BF16)
        v_cur = _roll_f32(v_cur, HEAD).astype(BF16)
    qs = jnp.where(lane == p, q, 0) * 0.125
    return dict(p=p, e=e, lane=lane, qs=qs, k_prev=k_prev, k_cur=k_cur,
                v_prev=jnp.where(lane == e, v_prev, 0), v_cur=jnp.where(lane == e, v_cur, 0),
                s_prev=_dot_nt(qs, k_prev), s_cur=_dot_nt(qs, k_cur))


def _swa_probs(ops, sink, no_prev):
    row = lax.broadcasted_iota(jnp.int32, (BLOCK, BLOCK), 0)
    col = lax.broadcasted_iota(jnp.int32, (BLOCK, BLOCK), 1)
    ok_prev = col > row if no_prev is None else jnp.logical_and(col > row, jnp.logical_not(no_prev))
    s_prev = jnp.where(ok_prev, ops["s_prev"], NEG)
    s_cur = jnp.where(col <= row, ops["s_cur"], NEG)
    m = jnp.maximum(jnp.maximum(jnp.max(s_prev, axis=1, keepdims=True),
                                jnp.max(s_cur, axis=1, keepdims=True)), sink)
    p_prev = jnp.exp(s_prev - m)
    p_cur = jnp.exp(s_cur - m)
    p_sink = jnp.exp(sink - m)
    inv = 1.0 / (jnp.sum(p_prev, axis=1, keepdims=True) + jnp.sum(p_cur, axis=1, keepdims=True) + p_sink)
    return p_prev * inv, p_cur * inv, p_sink * inv


def _swa_specs(T):
    n = T // (BLOCK * SWA_SUB)
    qo, ko, vo = (_INT_OFF[name] // LANES for name in ("a_q", "a_k", "a_v"))
    halo = lambda i: jnp.maximum(i * SWA_SUB - 1, 0)
    return [pl.BlockSpec((BLOCK * SWA_SUB, 256), lambda i: (i, qo // 2)),
            pl.BlockSpec((BLOCK, LANES), lambda i: (halo(i), ko)),
            pl.BlockSpec((BLOCK * SWA_SUB, LANES), lambda i: (i, ko)),
            pl.BlockSpec((BLOCK, LANES), lambda i: (halo(i), vo)),
            pl.BlockSpec((BLOCK * SWA_SUB, LANES), lambda i: (i, vo)),
            pl.BlockSpec(memory_space=pltpu.SMEM)], n


def _swa_units(q_ref, kh_ref, kc_ref, vh_ref, vc_ref, s_ref):
    blk = lambda a: slice(a * BLOCK, (a + 1) * BLOCK)
    units = [(a, h) for a in range(SWA_SUB) for h in range(4)]
    ops = {}
    for a, h in units:
        k_prev, v_prev = (kh_ref[...], vh_ref[...]) if a == 0 else (kc_ref[blk(a - 1), :], vc_ref[blk(a - 1), :])
        ops[a, h] = _swa_operands(h, q_ref[blk(a), :], k_prev, kc_ref[blk(a), :], v_prev, vc_ref[blk(a), :])
    probs = {(a, h): _swa_probs(ops[a, h], s_ref[h], pl.program_id(0) == 0 if a == 0 else None) for a, h in units}
    return units, ops, probs, blk


def _swa_fwd(hb, sinks):
    T = hb.shape[0]
    specs, n = _swa_specs(T)

    def body(q_ref, kh_ref, kc_ref, vh_ref, vc_ref, s_ref, o_ref):
        units, ops, probs, blk = _swa_units(q_ref, kh_ref, kc_ref, vh_ref, vc_ref, s_ref)
        outs = {u: _dot(probs[u][0].astype(BF16), ops[u]["v_prev"]) + _dot(probs[u][1].astype(BF16), ops[u]["v_cur"])
                for u in units}
        for a in range(SWA_SUB):
            for p in range(2):
                o_ref[blk(a), p * LANES:(p + 1) * LANES] = outs[a, 2 * p] + outs[a, 2 * p + 1]

    return pl.pallas_call(
        body, name="swa_fwd", grid=(n,), in_specs=specs,
        out_specs=pl.BlockSpec((BLOCK * SWA_SUB, 256), lambda i: (i, 0)),
        out_shape=jax.ShapeDtypeStruct((T, 256), F32),
        compiler_params=_cparams(("parallel",)),
    )(hb, hb, hb, hb, hb, sinks)


def _swa_bwd(hb, sinks, dy):
    T = hb.shape[0]
    specs, n = _swa_specs(T)

    def body(q_ref, kh_ref, kc_ref, vh_ref, vc_ref, s_ref, dy_ref, dq_ref, dk_ref, dv_ref, ds_ref):
        i = pl.program_id(0)

        @pl.when(i == 0)
        def _():
            ds_ref[...] = jnp.zeros_like(ds_ref)

        lane_id = lax.broadcasted_iota(jnp.int32, (8, LANES), 1)
        units, ops, probs, blk = _swa_units(q_ref, kh_ref, kc_ref, vh_ref, vc_ref, s_ref)
        dos = {(a, h): jnp.where(ops[a, h]["lane"] == ops[a, h]["e"],
                                 dy_ref[blk(a), ops[a, h]["p"] * LANES:(ops[a, h]["p"] + 1) * LANES], 0.0)
               for a, h in units}
        dobs = {u: dos[u].astype(BF16) for u in units}
        pbs = {u: (probs[u][0].astype(BF16), probs[u][1].astype(BF16)) for u in units}
        outs = {u: _dot(pbs[u][0], ops[u]["v_prev"]) + _dot(pbs[u][1], ops[u]["v_cur"]) for u in units}
        dps = {u: (_dot_nt(dobs[u], ops[u]["v_prev"]), _dot_nt(dobs[u], ops[u]["v_cur"])) for u in units}
        dss, dsinks = {}, jnp.zeros((8, LANES), F32)
        for u in units:
            delta = jnp.sum(dos[u] * outs[u], axis=1, keepdims=True)
            dss[u] = ((probs[u][0] * (dps[u][0] - delta)).astype(BF16),
                      (probs[u][1] * (dps[u][1] - delta)).astype(BF16))
            dsink = -jnp.sum(probs[u][2] * delta, axis=0, keepdims=True)
            dsinks += jnp.where(lane_id == u[1], dsink, 0.0)
        ds_ref[...] += dsinks
        dqs = {u: (_dot(dss[u][0], ops[u]["k_prev"]) + _dot(dss[u][1], ops[u]["k_cur"])) * 0.125 for u in units}
        zero = jnp.zeros((BLOCK, LANES), F32)
        dk_as_prev, dk_as_cur = [zero] * SWA_SUB, [zero] * SWA_SUB
        dv_as_prev, dv_as_cur = [zero] * SWA_SUB, [zero] * SWA_SUB
        for a, h in units:
            p, e = ops[a, h]["p"], ops[a, h]["e"]
            dob_v = dobs[a, h] if e == p else pltpu.roll(dos[a, h], HEAD, 1).astype(BF16)
            dk_as_prev[a] = dk_as_prev[a] + _dot_tn(dss[a, h][0], ops[a, h]["qs"])
            dk_as_cur[a] = dk_as_cur[a] + _dot_tn(dss[a, h][1], ops[a, h]["qs"])
            dv_as_prev[a] = dv_as_prev[a] + _dot_tn(pbs[a, h][0], dob_v)
            dv_as_cur[a] = dv_as_cur[a] + _dot_tn(pbs[a, h][1], dob_v)
        base = i * SWA_SUB
        for a in range(SWA_SUB):
            rows = pl.ds(pl.multiple_of((base + a) * BLOCK, BLOCK), BLOCK)
            more = a + 1 < SWA_SUB
            dk_ref[rows, :] = dk_as_cur[a] + (dk_as_prev[a + 1] if more else 0.0)
            dv_ref[rows, :] = dv_as_cur[a] + (dv_as_prev[a + 1] if more else 0.0)
        halo = pl.ds(pl.multiple_of(jnp.maximum(base - 1, 0) * BLOCK, BLOCK), BLOCK)
        dk_ref[halo, :] += dk_as_prev[0]
        dv_ref[halo, :] += dv_as_prev[0]
        for a in range(SWA_SUB):
            for p in range(2):
                dq_pair = jnp.zeros((BLOCK, LANES), F32)
                for e in range(2):
                    dq = jnp.where(ops[a, 2 * p + e]["lane"] == p, dqs[a, 2 * p + e], 0.0)
                    dq_pair += dq if e == p else pltpu.roll(dq, HEAD, 1)
                dq_ref[blk(a), p * LANES:(p + 1) * LANES] = dq_pair.astype(BF16)

    return pl.pallas_call(
        body, name="swa_bwd", grid=(n,),
        in_specs=specs + [pl.BlockSpec((BLOCK * SWA_SUB, 256), lambda i: (i, 0))],
        out_specs=[pl.BlockSpec((BLOCK * SWA_SUB, 256), lambda i: (i, 0)),
                   pl.BlockSpec((T, LANES), lambda i: (0, 0)),
                   pl.BlockSpec((T, LANES), lambda i: (0, 0)),
                   pl.BlockSpec((8, LANES), lambda i: (0, 0))],
        out_shape=[jax.ShapeDtypeStruct((T, 256), BF16),
                   jax.ShapeDtypeStruct((T, LANES), F32),
                   jax.ShapeDtypeStruct((T, LANES), F32),
                   jax.ShapeDtypeStruct((8, LANES), F32)],
        compiler_params=_cparams(("arbitrary",)),
    )(hb, hb, hb, hb, hb, sinks, dy)


def _rope_tables(pos):
    T = pos.shape[0]
    tm = 512

    def body(pos_ref, o_ref):
        lane = lax.broadcasted_iota(jnp.int32, (1, LANES), 1)
        active = jnp.logical_and(lane >= HEAD, lane < HEAD + 2 * ROPE_HALF)
        idx = ((lane - HEAD) % ROPE_HALF).astype(F32)
        freq = jnp.exp(idx * (-math.log(ROPE_THETA) / ROPE_HALF))
        ang = pos_ref[...].astype(F32) * freq
        cos, sin = jnp.cos(ang), jnp.sin(ang)
        o_ref[:, 0:LANES] = jnp.where(active, cos, 1.0)
        o_ref[:, LANES:2 * LANES] = jnp.where(jnp.logical_and(active, lane >= HEAD + ROPE_HALF), sin, 0.0)
        o_ref[:, 2 * LANES:] = jnp.where(jnp.logical_and(active, lane < HEAD + ROPE_HALF), -sin, 0.0)

    return pl.pallas_call(
        body, name="rope_tables", grid=(T // tm,),
        in_specs=[pl.BlockSpec((tm, 1), lambda i: (i, 0))],
        out_specs=pl.BlockSpec((tm, 3 * LANES), lambda i: (i, 0)),
        out_shape=jax.ShapeDtypeStruct((T, 3 * LANES), F32),
        compiler_params=_cparams(("parallel",)),
    )(pos)


def _rope_factors(tab_ref):
    return tab_ref[:, 0:LANES], tab_ref[:, LANES:2 * LANES], tab_ref[:, 2 * LANES:]


def _rope(x, tabs):
    c, s_up, s_dn = tabs
    return x * c + pltpu.roll(x, ROPE_HALF, 1) * s_up + pltpu.roll(x, LANES - ROPE_HALF, 1) * s_dn


def _rope_t(dy, tabs):
    c, s_up, s_dn = tabs
    return dy * c + pltpu.roll(dy * s_up, LANES - ROPE_HALF, 1) + pltpu.roll(dy * s_dn, ROPE_HALF, 1)


def _mla_lat_specs(tm):
    cq, ckv, ckr = ((_INT_OFF[n] - N_HB) for n in ("c_q", "c_kv", "c_kr"))
    return [pl.BlockSpec((tm, 256), lambda i: (i, cq // 256)),
            pl.BlockSpec((tm, LANES), lambda i: (i, ckv // LANES)),
            pl.BlockSpec((tm, LANES), lambda i: (i, ckr // LANES)),
            pl.BlockSpec((tm, 3 * LANES), lambda i: (i, 0)),
            pl.BlockSpec((1, 256), lambda i: (0, 0)),
            pl.BlockSpec((1, LANES), lambda i: (0, 0)),
            pl.BlockSpec((256, 512), lambda i: (0, 0)),
            pl.BlockSpec((LANES, 768), lambda i: (0, 0))]


def _mla_prep_fwd(hf, rope, g_q, g_kv, w_uq, w_ukv):
    T = hf.shape[0]
    tm = 512
    sub = tm // ATT_BLK

    def body(cq_ref, ckv_ref, ckr_ref, tab_ref, gq_ref, gkv_ref, wq_ref, wkv_ref, qm_ref, km_ref, vm_ref, vt_ref):
        tabs = _rope_factors(tab_ref)
        cq = cq_ref[...]
        q = _dot((cq * _rms(cq) * gq_ref[...]).astype(BF16), wq_ref[...])
        ckv = ckv_ref[...]
        kv = _dot((ckv * _rms(ckv) * gkv_ref[...]).astype(BF16), wkv_ref[...])
        kr = _rope(pltpu.roll(ckr_ref[...], HEAD, 1), tabs)
        for h in range(4):
            sl = slice(h * LANES, (h + 1) * LANES)
            qm_ref[:, sl] = (_rope(q[:, sl], tabs) * MLA_QSCALE).astype(BF16)
            km_ref[:, sl] = (kv[:, sl] + kr).astype(BF16)
        vm_ref[...] = kv[:, 512:].astype(BF16)
        for p in range(2):
            for s in range(sub):
                tile = kv[s * ATT_BLK:(s + 1) * ATT_BLK, 512 + p * LANES:512 + (p + 1) * LANES]
                vt_ref[p, s] = jnp.transpose(tile).astype(BF16)

    return pl.pallas_call(
        body, name="mla_prep_fwd", grid=(T // tm,), in_specs=_mla_lat_specs(tm),
        out_specs=[pl.BlockSpec((tm, 512), lambda i: (i, 0)),
                   pl.BlockSpec((tm, 512), lambda i: (i, 0)),
                   pl.BlockSpec((tm, 256), lambda i: (i, 0)),
                   pl.BlockSpec((2, sub, LANES, ATT_BLK), lambda i: (0, i, 0, 0))],
        out_shape=[jax.ShapeDtypeStruct((T, 512), BF16),
                   jax.ShapeDtypeStruct((T, 512), BF16),
                   jax.ShapeDtypeStruct((T, 256), BF16),
                   jax.ShapeDtypeStruct((2, T // ATT_BLK, LANES, ATT_BLK), BF16)],
        compiler_params=_cparams(("parallel",)),
    )(hf, hf, hf, rope, g_q, g_kv, w_uq, w_ukv)


def _mla_prep_bwd(hf, rope, g_q, g_kv, w_uq, w_ukv, dqm, dkt, dvt):
    T = hf.shape[0]
    tm = 512
    sub = tm // ATT_BLK

    def body(cq_ref, ckv_ref, ckr_ref, tab_ref, gq_ref, gkv_ref, wq_ref, wkv_ref, dq_ref, dk_ref, dv_ref,
             dc_ref, dwq_ref, dwkv_ref, dgq_ref, dgkv_ref):
        @pl.when(pl.program_id(0) == 0)
        def _():
            dwq_ref[...] = jnp.zeros_like(dwq_ref)
            dwkv_ref[...] = jnp.zeros_like(dwkv_ref)
            dgq_ref[...] = jnp.zeros_like(dgq_ref)
            dgkv_ref[...] = jnp.zeros_like(dgkv_ref)

        tabs = _rope_factors(tab_ref)
        lane =lax.broadcasted_iota(jnp.int32, (1, LANES), 1)
        dq = jnp.concatenate([_rope_t(dq_ref[:, h * LANES:(h + 1) * LANES] * MLA_QSCALE, tabs)
                              for h in range(4)], axis=1).astype(BF16)
        cq = cq_ref[...]
        rq = _rms(cq)
        cqn = (cq * rq * gq_ref[...]).astype(BF16)
        dwq_ref[...] += _dot_tn(cqn, dq)
        dcq, dgrow = _rms_bwd(_dot_nt(dq, wq_ref[...]), cq * rq, rq, gq_ref[...])
        dgq_ref[...] += _colsum(dgrow)
        dc_ref[:, 0:256] = dcq.astype(BF16)

        dk = jnp.concatenate([jnp.concatenate([jnp.transpose(dk_ref[p, s]) for p in range(2)], axis=1)
                              for s in range(sub)], axis=0) * LN2
        dv = jnp.concatenate([jnp.concatenate([jnp.transpose(dv_ref[p, s]) for p in range(2)], axis=1)
                              for s in range(sub)], axis=0)
        dkr = dk[:, 0:LANES] + dk[:, LANES:2 * LANES] + dk[:, 2 * LANES:3 * LANES] + dk[:, 3 * LANES:]
        dkr = pltpu.roll(_rope_t(dkr, tabs), HEAD, 1)
        dc_ref[:, 384:512] = jnp.where(lane < 2 * ROPE_HALF, dkr, 0.0).astype(BF16)
        dkv = jnp.concatenate([dk.astype(BF16), dv.astype(BF16)], axis=1)
        ckv = ckv_ref[...]
        rkv = _rms(ckv)
        ckvn = (ckv * rkv * gkv_ref[...]).astype(BF16)
        dwkv_ref[...] += _dot_tn(ckvn, dkv)
        dckv, dgrow = _rms_bwd(_dot_nt(dkv, wkv_ref[...]), ckv * rkv, rkv, gkv_ref[...])
        dgkv_ref[...] += _colsum(dgrow)
        dc_ref[:, 256:384] = dckv.astype(BF16)

    return pl.pallas_call(
        body, name="mla_prep_bwd", grid=(T // tm,),
        in_specs=_mla_lat_specs(tm) + [pl.BlockSpec((tm, 512), lambda i: (i, 0)),
                                       pl.BlockSpec((2, sub, 256, ATT_BLK), lambda i: (0, i, 0, 0)),
                                       pl.BlockSpec((2, sub, LANES, ATT_BLK), lambda i: (0, i, 0, 0))],
        out_specs=[pl.BlockSpec((tm, 512), lambda i: (i, 0)),
                   pl.BlockSpec((256, 512), lambda i: (0, 0)),
                   pl.BlockSpec((LANES, 768), lambda i: (0, 0)),
                   pl.BlockSpec((1, 256), lambda i: (0, 0)),
                   pl.BlockSpec((1, LANES), lambda i: (0, 0))],
        out_shape=[jax.ShapeDtypeStruct((T, 512), BF16),
                   jax.ShapeDtypeStruct((256, 512), F32),
                   jax.ShapeDtypeStruct((LANES, 768), F32),
                   jax.ShapeDtypeStruct((1, 256), F32),
                   jax.ShapeDtypeStruct((1, LANES), F32)],
        compiler_params=_cparams(("arbitrary",)),
    )(hf, hf, hf, rope, g_q, g_kv, w_uq, w_ukv, dqm, dkt, dvt)


def _causal_masks(bq, bk):
    row = lax.broadcasted_iota(jnp.int32, (bq, bk), 0)
    col = lax.broadcasted_iota(jnp.int32, (bq, bk), 1)
    return row, col


def _mla_fwd(qm, km, vt):
    T = qm.shape[0]
    bq, bk = min(MLA_BQ, T), ATT_BLK
    nq, nsub, nk = T // bq, bq // bk, T // bk

    def body(q_ref, k_ref, vt_ref, o_ref, lse_ref, acc_ref, m_ref, l_ref):
        qi = pl.program_id(0)
        key = lax.broadcasted_iota(jnp.int32, (bk, bq), 0)
        qry = lax.broadcasted_iota(jnp.int32, (bk, bq), 1)
        ones = jnp.ones((8, bk), BF16)
        acc_ref[...] = jnp.zeros_like(acc_ref)
        m_ref[...] = jnp.full_like(m_ref, NEG)
        l_ref[...] = jnp.zeros_like(l_ref)

        def step(kb0, masked):
            kbs = [kb0 + d for d in range(nsub)]
            sts = [[_dot_nt(k_ref[pl.ds(pl.multiple_of(kb * bk, bk), bk), e * LANES:(e + 1) * LANES],
                            q_ref[:, e * LANES:(e + 1) * LANES]) for kb in kbs] for e in range(4)]
            pts, alphas = [], []
            for e in range(4):
                st = [jnp.where(key + d * bk <= qry, sts[e][d], NEG) for d in range(nsub)] if masked else sts[e]
                m_prev = m_ref[e, 0:1, :]
                m_new = m_prev
                for d in range(nsub):
                    m_new = jnp.maximum(m_new, jnp.max(st[d], axis=0, keepdims=True))
                alpha = jnp.exp2(m_prev - m_new)
                pt = [jnp.exp2(st[d] - m_new).astype(BF16) for d in range(nsub)]
                l_new = alpha * l_ref[e]
                for d in range(nsub):
                    l_new = l_new + _dot(ones, pt[d])
                l_ref[e] = l_new
                m_ref[e] = jnp.broadcast_to(m_new, (8, bq))
                pts.append(pt)
                alphas.append(alpha)
            for e in range(4):
                acc = alphas[e] * acc_ref[e]
                for d in range(nsub):
                    v_t = vt_ref[e // 2, kbs[d], (e % 2) * HEAD:(e % 2 + 1) * HEAD, :]
                    acc = acc + _dot(v_t, pts[e][d])
                acc_ref[e] = acc

        step(qi * nsub, True)

        def loop(t, c):
            step(t * nsub, False)
            return c

        lax.fori_loop(0, qi, loop, 0)
        outs, lses = [], []
        for e in range(4):
            l = l_ref[e, 0:1, :]
            outs.append(acc_ref[e] / l)
            lses.append(jnp.broadcast_to(m_ref[e, 0:1, :] * LN2 + jnp.log(l), (HEAD, bq)))
        o_ref[...] = jnp.transpose(jnp.concatenate(outs, axis=0))
        lse_ref[...] = jnp.transpose(jnp.concatenate(lses, axis=0))

    return pl.pallas_call(
        body, name="mla_fwd", grid=(nq,),
        in_specs=[pl.BlockSpec((bq, 512), lambda i: (i, 0)),
                  pl.BlockSpec((T, 512), lambda i: (0, 0)),
                  pl.BlockSpec((2, nk, LANES, bk), lambda i: (0, 0, 0, 0))],
        out_specs=[pl.BlockSpec((bq, 256), lambda i: (i, 0)),
                   pl.BlockSpec((bq, 256), lambda i: (i, 0))],
        out_shape=[jax.ShapeDtypeStruct((T, 256), F32), jax.ShapeDtypeStruct((T, 256), F32)],
        scratch_shapes=[pltpu.VMEM((4, HEAD, bq), F32), pltpu.VMEM((4, 8, bq), F32), pltpu.VMEM((4, 8, bq), F32)],
        compiler_params=_cparams(("arbitrary",)),
    )(qm, km, vt)


def _mla_bwd(qm, km, vm, y, lse, dy):
    T = qm.shape[0]
    bq, bk = min(MLA_BQ, T), ATT_BLK
    nq, nsub, nk = T // bq, bq // bk, T // bk

    def body(q_ref, k_ref, v_ref, y_ref, lse_ref, dy_ref, dq_ref, dkt_ref, dvt_ref, dob_ref, st_ref, qt_ref, dot_ref):
        qi = pl.program_id(1)

        @pl.when(qi == 0)
        def _():
            dkt_ref[...] = jnp.zeros_like(dkt_ref)
            dvt_ref[...] = jnp.zeros_like(dvt_ref)

        lane = lax.broadcasted_iota(jnp.int32, (1, LANES), 1) // HEAD
        row, col = _causal_masks(bq, bk)
        dq_ref[...] = jnp.zeros_like(dq_ref)
        lse = lse_ref[...]
        lse_other = pltpu.roll(lse, HEAD, 1)
        qt_ref[...] = jnp.transpose(q_ref[...].astype(F32)).astype(BF16)
        dot_ref[...] = jnp.transpose(dy_ref[...]).astype(BF16)
        for e in range(2):
            do = jnp.where(lane == e, dy_ref[...], 0.0)
            dob_ref[e] = do.astype(BF16)
            st_ref[2 * e] = jnp.where(lane == e, lse, lse_other) * LOG2E
            st_ref[2 * e + 1] = jnp.broadcast_to(jnp.sum(do * y_ref[...], axis=1, keepdims=True), (bq, LANES))

        hss = [slice(e * LANES, (e + 1) * LANES) for e in range(2)]
        tile = lambda a: jnp.concatenate([a] * (bk // LANES), axis=1)

        def step(kb0, masked):
            kbs = [kb0 + d for d in range(nsub)]
            rows = [pl.ds(pl.multiple_of(kb * bk, bk), bk) for kb in kbs]
            pairs = [(d, e) for d in range(nsub) for e in range(2)]
            ss = {(d, e): _dot_nt(q_ref[:, hss[e]], k_ref[rows[d], hss[e]]) for d, e in pairs}
            dps = {(d, e): _dot_nt(dob_ref[e], jnp.where(lane == e, v_ref[rows[d], :], 0)) for d, e in pairs}
            ps, dss = {}, {}
            for d, e in pairs:
                s = jnp.where(col + d * bk <= row, ss[d, e], NEG) if masked else ss[d, e]
                p = jnp.exp2(s - tile(st_ref[2 * e]))
                dss[d, e] = (p * (dps[d, e] - tile(st_ref[2 * e + 1]))).astype(BF16)
                ps[d, e] = p.astype(BF16)
            for d, e in pairs:
                dvt_ref[0, kbs[d], e * HEAD:(e + 1) * HEAD, :] += _dot(dot_ref[e * HEAD:(e + 1) * HEAD, :], ps[d, e])
            for d, e in pairs:
                dkt_ref[0, kbs[d], hss[e], :] += _dot(qt_ref[hss[e], :], dss[d, e])
            for e in range(2):
                dq = dq_ref[:, hss[e]]
                for d in range(nsub):
                    dq = dq + _dot(dss[d, e], k_ref[rows[d], hss[e]])
                dq_ref[:, hss[e]] = dq

        step(qi * nsub, True)

        def loop(t, c):
            step(t * nsub, False)
            return c

        lax.fori_loop(0, qi, loop, 0)
        dq_ref[...] *= LN2

    return pl.pallas_call(
        body, name="mla_bwd", grid=(2, nq),
        in_specs=[pl.BlockSpec((bq, 256), lambda j, i: (i, j)),
                  pl.BlockSpec((T, 256), lambda j, i: (0, j)),
                  pl.BlockSpec((T, LANES), lambda j, i: (0, j)),
                  pl.BlockSpec((bq, LANES), lambda j, i: (i, j)),
                  pl.BlockSpec((bq, LANES), lambda j, i: (i, j)),
                  pl.BlockSpec((bq, LANES), lambda j, i: (i, j))],
        out_specs=[pl.BlockSpec((bq, 256), lambda j, i: (i, j)),
                   pl.BlockSpec((1, nk, 256, bk), lambda j, i: (j, 0, 0, 0)),
                   pl.BlockSpec((1, nk, LANES, bk), lambda j, i: (j, 0, 0, 0))],
        out_shape=[jax.ShapeDtypeStruct((T, 512), F32),
                   jax.ShapeDtypeStruct((2, nk, 256, bk), F32),
                   jax.ShapeDtypeStruct((2, nk, LANES, bk), F32)],
        scratch_shapes=[pltpu.VMEM((2, bq, LANES), BF16), pltpu.VMEM((4, bq, LANES), F32),
                        pltpu.VMEM((256, bq), BF16), pltpu.VMEM((LANES, bq), BF16)],
        compiler_params=_cparams(("parallel", "arbitrary")),
    )(qm, km, vm, y, lse, dy)


def _suffix_ones(n):
    r = lax.broadcasted_iota(jnp.int32, (n, n), 0)
    c = lax.broadcasted_iota(jnp.int32, (n, n), 1)
    return (r >= c).astype(BF16)


def _prefix_ones(n):
    r = lax.broadcasted_iota(jnp.int32, (n, n), 0)
    c = lax.broadcasted_iota(jnp.int32, (n, n), 1)
    return (r <= c).astype(BF16)


def _sb_specs(T, bq):
    qo, ko, vo = (_INT_OFF[n] // 256 for n in ("d_q", "d_k", "d_v"))
    return [pl.BlockSpec((bq, 256), lambda i: (i, qo)),
            pl.BlockSpec((T, 256), lambda i: (0, ko)),
            pl.BlockSpec((T, 256), lambda i: (0, vo))]


def _sb_fwd(hb):
    T = hb.shape[0]
    bq = bk = ATT_BLK
    nq = T // bq

    def body(q_ref, k_ref, v_ref, o_ref, tot_ref, cnt_ref, qm_ref, car_ref):
        qi = pl.program_id(0)
        lane = lax.broadcasted_iota(jnp.int32, (1, LANES), 1) // HEAD
        row, col = _causal_masks(bq, bk)
        strict = col < row
        u = _suffix_ones(bk)
        o_ref[...] = jnp.zeros_like(o_ref)
        car_ref[...] = jnp.zeros_like(car_ref)
        pair = lambda h: slice((h // 2) * LANES, (h // 2 + 1) * LANES)
        for h in range(4):
            qm_ref[h] = jnp.where(lane == h % 2, q_ref[:, pair(h)], 0) * 0.125

        def step(blocks):
            tile = lambda a: jnp.concatenate([a] * (bk // LANES), axis=1)
            rows = [pl.ds(pl.multiple_of(kb * bk, bk), bk) for kb, _ in blocks]
            pairs = [(b, h) for b in range(len(blocks)) for h in range(4)]
            zs = {(b, h): _dot_nt(qm_ref[h], k_ref[rows[b], pair(h)]) for b, h in pairs}
            splits = {}
            for b, h in pairs:
                z = zs[b, h]
                lk = jnp.minimum(-z, 0.0) - jnp.log(1.0 + jnp.exp(-jnp.abs(z)))
                if blocks[b][1] is not None:
                    lk = jnp.where(blocks[b][1], lk, 0.0)
                splits[b, h] = _split(lk)
            sufs = {bh: _dot(hi, u) + _dot(lo, u) for bh, (hi, lo) in splits.items()}
            car = [car_ref[h] for h in range(4)]
            aas = {}
            for b, h in pairs:
                a = jnp.exp(zs[b, h] + sufs[b, h] + tile(car[h]))
                if blocks[b][1] is not None:
                    a = jnp.where(blocks[b][1], a, 0.0)
                aas[b, h] = a.astype(BF16)
                car[h] = car[h] + jnp.broadcast_to(sufs[b, h][:, 0:1], (bq, LANES))
            acc = [o_ref[:, pair(0)], o_ref[:, pair(2)]]
            for b, h in pairs:
                acc[h // 2] = acc[h // 2] + _dot(aas[b, h], jnp.where(lane == h % 2, v_ref[rows[b], pair(h)], 0))
            o_ref[:, pair(0)], o_ref[:, pair(2)] = acc
            for h in range(4):
                car_ref[h] = car[h]

        step([(qi, strict), (jnp.maximum(qi - 1, 0), qi > 0)])

        def live():
            worst = jnp.maximum(jnp.maximum(car_ref[0], car_ref[1]), jnp.maximum(car_ref[2], car_ref[3]))
            return jnp.max(worst) >= SB_DEAD

        def cond(c):
            return jnp.logical_and(c[0] < qi, c[1])

        def loop(c):
            step([(qi - 1 - c[0], None)])
            return c[0] + 1, live()

        done, _ = lax.while_loop(cond, loop, (jnp.minimum(qi, 1), live()))
        tot_ref[:, pair(0)] = jnp.where(lane == 0, car_ref[0], car_ref[1])
        tot_ref[:, pair(2)] = jnp.where(lane == 0, car_ref[2], car_ref[3])
        cnt_ref[0, qi] = done.astype(F32)

    return pl.pallas_call(
        body, name="sb_fwd", grid=(nq,), in_specs=_sb_specs(T, bq),
        out_specs=[pl.BlockSpec((bq, 256), lambda i: (i, 0)), pl.BlockSpec((bq, 256), lambda i: (i, 0)),
                   pl.BlockSpec(memory_space=pltpu.SMEM)],
        out_shape=[jax.ShapeDtypeStruct((T, 256), F32), jax.ShapeDtypeStruct((T, 256), F32),
                   jax.ShapeDtypeStruct((1, nq), F32)],
        scratch_shapes=[pltpu.VMEM((4, bq, LANES), BF16), pltpu.VMEM((4, bq, LANES), F32)],
        compiler_params=_cparams(("arbitrary",)),
    )(hb, hb, hb)


def _sb_bwd(hb, tot, cnt, dy):
    T = hb.shape[0]
    bq = bk = ATT_BLK
    nq = T // bq

    def body(q_ref, k_ref, v_ref, tot_ref, dy_ref, cnt_ref, dq_ref, dk_ref, dv_ref, qm_ref, dob_ref, dqa_ref, rem_ref,
             cg_ref):
        qi = pl.program_id(0)

        @pl.when(qi == 0)
        def _():
            dk_ref[...] = jnp.zeros_like(dk_ref)
            dv_ref[...] = jnp.zeros_like(dv_ref)

        lane = lax.broadcasted_iota(jnp.int32, (1, LANES), 1) // HEAD
        row, col = _causal_masks(bq, bk)
        strict = col < row
        u = _prefix_ones(bk)
        pair = lambda h: slice((h // 2) * LANES, (h // 2 + 1) * LANES)
        dqa_ref[...] = jnp.zeros_like(dqa_ref)
        cg_ref[...] = jnp.zeros_like(cg_ref)
        for h in range(4):
            tot = tot_ref[:, pair(h)]
            qm_ref[h] = jnp.where(lane == h % 2, q_ref[:, pair(h)], 0) * 0.125
            dob_ref[h] = jnp.where(lane == h % 2, dy_ref[:, pair(h)], 0.0).astype(BF16)
            rem_ref[h] = jnp.where(lane == h % 2, tot, pltpu.roll(tot, HEAD, 1))

        def step(blocks):
            tile = lambda a: jnp.concatenate([a] * (bk // LANES), axis=1)
            nb = len(blocks)
            rows = [pl.ds(pl.multiple_of(kb * bk, bk), bk) for kb, _ in blocks]
            pairs = [(b, h) for b in range(nb) for h in range(4)]
            mask = lambda b, x: x if blocks[b][1] is None else jnp.where(blocks[b][1], x, 0.0)
            zs = {(b, h): _dot_nt(qm_ref[h], k_ref[rows[b], pair(h)]) for b, h in pairs}
            das = {(b, h): _dot_nt(dob_ref[h], jnp.where(lane == h % 2, v_ref[rows[b], pair(h)], 0)) for b, h in pairs}
            zls, splits = {}, {}
            for b, h in pairs:
                z = zs[b, h]
                lk = mask(b, jnp.minimum(-z, 0.0) - jnp.log(1.0 + jnp.exp(-jnp.abs(z))))
                zls[b, h] = z + lk
                splits[b, h] = _split(lk)
            pres = {bh: _dot(hi, u) + _dot(lo, u) for bh, (hi, lo) in splits.items()}
            rem = [rem_ref[h] for h in range(4)]
            aas, gs, gsplits = {}, {}, {}
            for b, h in pairs:
                a = mask(b, jnp.exp(zls[b, h] + (tile(rem[h]) - pres[b, h])))
                gs[b, h] = a * das[b, h]
                aas[b, h] = a.astype(BF16)
                gsplits[b, h] = _split(gs[b, h])
                rem[h] = rem[h] - jnp.broadcast_to(pres[b, h][:, bk - 1:bk], (bq, LANES))
            for b in range(nb):
                for p in (0, 2):
                    dv_ref[rows[b], pair(p)] += _dot_tn(aas[b, p], dob_ref[p]) + _dot_tn(aas[b, p + 1], dob_ref[p + 1])
            gpres = {bh: _dot(hi, u) + _dot(lo, u) for bh, (hi, lo) in gsplits.items()}
            cg = [cg_ref[h] for h in range(4)]
            dzs = {}
            for b, h in pairs:
                dz = mask(b, gs[b, h] - jnp.exp(zls[b, h]) * (tile(cg[h]) + gpres[b, h]))
                dzs[b, h] = dz.astype(BF16)
                cg[h] = cg[h] + jnp.broadcast_to(gpres[b, h][:, bk - 1:bk], (bq, LANES))
            for b in range(nb):
                for p in (0, 2):
                    dk_ref[rows[b], pair(p)] += _dot_tn(dzs[b, p], qm_ref[p]) + _dot_tn(dzs[b, p + 1], qm_ref[p + 1])
            for h in range(4):
                dq = dqa_ref[h]
                for b in range(nb):
                    dq = dq + _dot(dzs[b, h], k_ref[rows[b], pair(h)])
                dqa_ref[h] = dq
                rem_ref[h] = rem[h]
                cg_ref[h] = cg[h]

        def loop(kb, c):
            step([(kb, None)])
            return c

        start = qi - jnp.clip(cnt_ref[0, qi].astype(jnp.int32), 0, qi)
        lax.fori_loop(start, qi - 1, loop, 0)
        step([(jnp.maximum(qi - 1, 0), qi > 0), (qi, strict)])
        for p in (0, 2):
            dq_ref[:, pair(p)] = (jnp.where(lane == 0, dqa_ref[p], dqa_ref[p + 1]) * 0.125).astype(BF16)

    return pl.pallas_call(
        body, name="sb_bwd", grid=(nq,),
        in_specs=_sb_specs(T, bq) + [pl.BlockSpec((bq, 256), lambda i: (i, 0)),
                                     pl.BlockSpec((bq, 256), lambda i: (i, 0)),
                                     pl.BlockSpec(memory_space=pltpu.SMEM)],
        out_specs=[pl.BlockSpec((bq, 256), lambda i: (i, 0)),
                   pl.BlockSpec((T, 256), lambda i: (0, 0)),
                   pl.BlockSpec((T, 256), lambda i: (0, 0))],
        out_shape=[jax.ShapeDtypeStruct((T, 256), BF16)] + [jax.ShapeDtypeStruct((T, 256), F32)] * 2,
        scratch_shapes=[pltpu.VMEM((4, bq, LANES), BF16), pltpu.VMEM((4, bq, LANES), BF16),
                        pltpu.VMEM((4, bq, LANES), F32), pltpu.VMEM((4, bq, LANES), F32),
                        pltpu.VMEM((4, bq, LANES), F32)],
        compiler_params=_cparams(("arbitrary",)),
    )(hb, hb, hb, tot, dy, cnt)


EP_TM = 512


def _ep_in_specs(tm, rev):
    idx = (lambda i: rev - i) if rev is not None else (lambda i: i)
    bo = (_INT_OFF["b_b"] - N_HB) // 256
    halo = lambda i: jnp.maximum(idx(i) * (tm // 8) - 1, 0)
    return [pl.BlockSpec((tm, 256), lambda i: (idx(i), 0)),
            pl.BlockSpec((tm, 256), lambda i: (idx(i), 0)),
            pl.BlockSpec((tm, 256), lambda i: (idx(i), 0)),
            pl.BlockSpec((tm, D_MODEL), lambda i: (idx(i), 0)),
            pl.BlockSpec((tm, 256), lambda i: (idx(i), bo)),
            pl.BlockSpec((tm, 256), lambda i: (idx(i), bo + 1)),
            pl.BlockSpec((tm, 256), lambda i: (idx(i), bo + 2)),
            pl.BlockSpec((8, 256), lambda i: (halo(i), bo + 1)),
            pl.BlockSpec((8, 256), lambda i: (halo(i), bo + 2)),
            pl.BlockSpec((3, 256), lambda i: (0, 0)),
            pl.BlockSpec((1, 256), lambda i: (0, 0)),
            pl.BlockSpec((1, D_MODEL), lambda i: (0, 0)),
            pl.BlockSpec((D_MODEL, D_MODEL), lambda i: (0, 0)),
            pl.BlockSpec((1, D_MODEL), lambda i: (0, 0))]


def _ep_mix(first, ya_ref, yc_ref, yd_ref, gate_ref, bb_ref, bc_ref, bx_ref, hc_ref, hx_ref, cw_ref, cb_ref, gg_ref):
    tm = ya_ref.shape[0]
    u = bc_ref[...] * bx_ref[...]
    halo = jnp.where(first, 0.0, hc_ref[...] * hx_ref[...])
    row = lax.broadcasted_iota(jnp.int32, (tm, 1), 0)
    u1 = jnp.where(row == 0, halo[7:8, :], pltpu.roll(u, 1, 0))
    u2 = jnp.where(row == 0, halo[6:7, :], jnp.where(row == 1, halo[7:8, :], pltpu.roll(u, 2, 0)))
    cw = cw_ref[...]
    conv = cw[0:1, :] * u2 + cw[1:2, :] * u1 + cw[2:3, :] * u + cb_ref[...]
    bb = bb_ref[...]
    ys = [ya_ref[...], bb * conv, yc_ref[...], yd_ref[...]]
    rs = [_rms(y) for y in ys]
    gg = gg_ref[...]
    yhat = jnp.concatenate([y * r for y, r in zip(ys, rs)], axis=1)
    gate = gate_ref[...]
    sig = 1.0 / (1.0 + jnp.exp(-gate))
    return u, u1, u2, conv, bb, rs, yhat, yhat * gg, gate, sig


def _epilogue_fwd(x, ya, yc, yd, hf, conv_w, conv_b, g_grp, w_out, g_post, tgt=None):
    T = x.shape[0]
    tm = EP_TM
    row_spec = pl.BlockSpec((tm, D_MODEL), lambda i: (i, 0))

    def layer_out(refs):
        (x_ref, ya_ref, yc_ref, yd_ref, gate_ref, bb_ref, bc_ref, bx_ref, hc_ref, hx_ref, cw_ref, cb_ref,
         gg_ref, wo_ref, gp_ref) = refs
        (_, _, _, _, _, _, _, yn, gate, sig) = _ep_mix(
            pl.program_id(0) == 0, ya_ref, yc_ref, yd_ref, gate_ref, bb_ref, bc_ref, bx_ref, hc_ref, hx_ref,
            cw_ref, cb_ref, gg_ref)
        z = _dot((yn * (gate * sig)).astype(BF16), wo_ref[...])
        return x_ref[...] + z * _rms(z) * gp_ref[...]

    args = (x, ya, yc, yd, hf, hf, hf, hf, hf, hf, conv_w, conv_b, g_grp, w_out, g_post)
    in_specs = [row_spec] + _ep_in_specs(tm, None)
    if tgt is None:
        def body(*refs):
            refs[-1][...] = layer_out(refs[:-1])

        return pl.pallas_call(
            body, name="epilogue_fwd", grid=(T // tm,), in_specs=in_specs, out_specs=row_spec,
            out_shape=jax.ShapeDtypeStruct((T, D_MODEL), F32), compiler_params=_cparams(("parallel",)),
        )(*args)

    def body_loss(*refs):
        t_ref, dy_ref, l_ref = refs[-3:]

        @pl.when(pl.program_id(0) == 0)
        def _():
            l_ref[...] = jnp.zeros_like(l_ref)

        d = layer_out(refs[:-3]) - t_ref[...]
        dy_ref[...] = d * (1.0 / D_MODEL)
        part = jnp.sum(jnp.sum(d * d, axis=1, keepdims=True), axis=0, keepdims=True)
        l_ref[...] += part * (0.5 / D_MODEL)

    return pl.pallas_call(
        body_loss, name="epilogue_fwd_loss", grid=(T // tm,), in_specs=in_specs + [row_spec],
        out_specs=[row_spec, pl.BlockSpec((8, LANES), lambda i: (0, 0))],
        out_shape=[jax.ShapeDtypeStruct((T, D_MODEL), F32), jax.ShapeDtypeStruct((8, LANES), F32)],
        compiler_params=_cparams(("arbitrary",)),
    )(*args, tgt)


def _epilogue_bwd(dxn, ya, yc, yd, hf, conv_w, conv_b, g_grp, w_out, g_post):
    T = dxn.shape[0]
    tm = EP_TM
    nt = T // tm
    ridx = lambda i: (nt - 1 - i, 0)

    def body(dx_ref, ya_ref, yc_ref, yd_ref, gate_ref, bb_ref, bc_ref, bx_ref, hc_ref, hx_ref, cw_ref, cb_ref,
             gg_ref, wo_ref, gp_ref,
             dya_ref, dyc_ref, dyd_ref, dhf_ref, dwo_ref, dgp_ref, dgg_ref, dcw_ref, dcb_ref, carry_ref):
        i = pl.program_id(0)

        @pl.when(i == 0)
        def _():
            for r in (dwo_ref, dgp_ref, dgg_ref, dcw_ref, dcb_ref, carry_ref):
                r[...] = jnp.zeros_like(r)

        (u, u1, u2, conv, bb, rs, yhat, yn, gate, sig) = _ep_mix(
            i == nt - 1, ya_ref, yc_ref, yd_ref, gate_ref, bb_ref, bc_ref, bx_ref, hc_ref, hx_ref,
            cw_ref, cb_ref, gg_ref)
        silu = gate * sig
        ymix = (yn * silu).astype(BF16)
        z = _dot(ymix, wo_ref[...])
        rz = _rms(z)
        dz, dgrow = _rms_bwd(dx_ref[...], z * rz, rz, gp_ref[...])
        dgp_ref[...] += _colsum(dgrow)
        dzb = dz.astype(BF16)
        dwo_ref[...] += _dot_tn(ymix, dzb)
        dymix = _dot_nt(dzb, wo_ref[...])
        dhf_ref[:, 0:D_MODEL] = (dymix * yn * (sig * (1.0 + gate * (1.0 - sig)))).astype(BF16)
        dyn = dymix * silu
        dgg_ref[...] += _colsum(dyn * yhat)
        gg = gg_ref[...]
        dys = []
        for gi in range(4):
            sl = slice(gi * GROUP, (gi + 1) * GROUP)
            dyh = dyn[:, sl] * gg[:, sl]
            yh = yhat[:, sl]
            dys.append(rs[gi] * (dyh - yh * jnp.mean(dyh * yh, axis=-1, keepdims=True)))
        dya_ref[...] = dys[0]
        dyc_ref[...] = dys[2]
        dyd_ref[...] = dys[3]
        dyb = dys[1]
        dhf_ref[:, D_MODEL:D_MODEL + 256] = (dyb * conv).astype(BF16)
        dconv = dyb * bb
        dcb_ref[...] += _colsum(dconv)
        dcw_ref[0:1, :] += _colsum(dconv * u2)
        dcw_ref[1:2, :] += _colsum(dconv * u1)
        dcw_ref[2:3, :] += _colsum(dconv * u)
        carry = carry_ref[...]
        row = lax.broadcasted_iota(jnp.int32, (tm, 1), 0)
        d1 = jnp.where(row == tm - 1, carry[0:1, :], pltpu.roll(dconv, tm - 1, 0))
        d2 = jnp.where(row == tm - 2, carry[0:1, :],
                       jnp.where(row == tm - 1, carry[1:2, :], pltpu.roll(dconv, tm - 2, 0)))
        cw = cw_ref[...]
        du = cw[2:3, :] * dconv + cw[1:2, :] * d1 + cw[0:1, :] * d2
        dhf_ref[:, D_MODEL + 256:D_MODEL + 512] = (du * bx_ref[...]).astype(BF16)
        dhf_ref[:, D_MODEL + 512:D_MODEL + 768] = (du * bc_ref[...]).astype(BF16)
        carry_ref[...] = dconv[0:8, :]

    in_specs = [pl.BlockSpec((tm, D_MODEL), ridx)] + _ep_in_specs(tm, nt - 1)
    return pl.pallas_call(
        body, name="epilogue_bwd", grid=(nt,), in_specs=in_specs,
        out_specs=[pl.BlockSpec((tm, 256), ridx), pl.BlockSpec((tm, 256), ridx), pl.BlockSpec((tm, 256), ridx),
                   pl.BlockSpec((tm, D_MODEL + 768), ridx),
                   pl.BlockSpec((D_MODEL, D_MODEL), lambda i: (0, 0)),
                   pl.BlockSpec((1, D_MODEL), lambda i: (0, 0)),
                   pl.BlockSpec((1, D_MODEL), lambda i: (0, 0)),
                   pl.BlockSpec((8, 256), lambda i: (0, 0)),
                   pl.BlockSpec((1, 256), lambda i: (0, 0))],
        out_shape=[jax.ShapeDtypeStruct((T, 256), F32)] * 3
                  + [jax.ShapeDtypeStruct((T, D_MODEL + 768), BF16),
                     jax.ShapeDtypeStruct((D_MODEL, D_MODEL), F32),
                     jax.ShapeDtypeStruct((1, D_MODEL), F32),
                     jax.ShapeDtypeStruct((1, D_MODEL), F32),
                     jax.ShapeDtypeStruct((8, 256), F32),
                     jax.ShapeDtypeStruct((1, 256), F32)],
        scratch_shapes=[pltpu.VMEM((8, 256), F32)],
        compiler_params=_cparams(("arbitrary",)),
    )(dxn, ya, yc, yd, hf, hf, hf, hf, hf, hf, conv_w, conv_b, g_grp, w_out, g_post)


def _place():
    return lax.axis_index("x"), lax.axis_index("y"), lax.axis_index("c")


def _other_chips(x, y):
    return [(1 - x, y), (x, 1 - y), (1 - x, 1 - y)]


HBM = pl.BlockSpec(memory_space=pl.ANY)


def _gather_plan(ins, outs, sems):
    n = len(ins)
    ici_send, ici_recv, d2d_send, d2d_recv, local_sems = sems
    x, y, c = _place()
    me = 2 * x + y
    chips = _other_chips(x, y)

    def ici(a, j, chip_from):
        px, py = chips[j]
        return pltpu.make_async_remote_copy(
            src_ref=ins[a].at[c], dst_ref=outs[a].at[chip_from, c], send_sem=ici_send.at[3 * a + j],
            recv_sem=ici_recv.at[3 * a + j], device_id=(px, py, c), device_id_type=MESH)

    def d2d(a, j, part):
        px, py = chips[j]
        blk = outs[a].at[2 * px + py, part]
        return pltpu.make_async_remote_copy(
            src_ref=blk, dst_ref=blk, send_sem=d2d_send.at[3 * a + j], recv_sem=d2d_recv.at[3 * a + j],
            device_id=(x, y, 1 - c), device_id_type=MESH)

    def local(a):
        return pltpu.make_async_copy(ins[a], outs[a].at[me], local_sems.at[a])

    hops = [(j, a) for j in range(3) for a in range(n)]

    def start():
        for a in range(n):
            local(a).start()
        for j, a in hops:
            ici(a, j, me).start()

    def finish():
        for j, a in hops:
            ici(a, j, 2 * chips[j][0] + chips[j][1]).wait_recv()
            d2d(a, j, c).start()
        for j, a in hops:
            d2d(a, j, 1 - c).wait_recv()
        for j, a in hops:
            ici(a, j, me).wait_send()
            d2d(a, j, c).wait_send()
        for a in range(n):
            local(a).wait()

    return start, finish


def _gather_sems(n):
    return [pltpu.SemaphoreType.DMA((3 * n,))] * 4 + [pltpu.SemaphoreType.DMA((n,))]


def _gather_weights(shards):
    n = len(shards)

    def body(*refs):
        start, finish = _gather_plan(refs[:n], refs[n:2 * n], refs[2 * n:])
        start()
        finish()

    return pl.pallas_call(
        body, name="gather_weights",
        in_specs=[HBM] * n, out_specs=[HBM] * n,
        out_shape=[jax.ShapeDtypeStruct((4,) + s.shape, s.dtype) for s in shards],
        scratch_shapes=_gather_sems(n),
    )(*shards)


def _exchange_chips(parts, small):
    n = len(parts)

    def body(*refs):
        ins, sm_ref = refs[:n], refs[n]
        outs, osm_ref = refs[n + 1:2 * n + 1], refs[2 * n + 1]
        send_sems, recv_sems, ssend_sems, srecv_sems, local_sems = refs[2 * n + 2:]
        x, y, c = _place()
        me = 2 * x + y
        dev = 4 * x + 2 * y + c
        local = [pltpu.make_async_copy(ins[a].at[me], outs[a].at[me], local_sems.at[a]) for a in range(n)]
        local.append(pltpu.make_async_copy(sm_ref, osm_ref.at[dev], local_sems.at[n]))
        for cp in local:
            cp.start()
        sends = []
        for j, (px, py) in enumerate(_other_chips(x, y)):
            for a in range(n):
                cp = pltpu.make_async_remote_copy(
                    src_ref=ins[a].at[2 * px + py], dst_ref=outs[a].at[me], send_sem=send_sems.at[3 * a + j],
                    recv_sem=recv_sems.at[3 * a + j], device_id=(px, py, c), device_id_type=MESH)
                cp.start()
                sends.append(cp)
        flips = [(fx, fy, fc) for fx in (0, 1) for fy in (0, 1) for fc in (0, 1)][1:]
        for j, (fx, fy, fc) in enumerate(flips):
            cp = pltpu.make_async_remote_copy(
                src_ref=sm_ref, dst_ref=osm_ref.at[dev], send_sem=ssend_sems.at[j], recv_sem=srecv_sems.at[j],
                device_id=(x ^ fx, y ^ fy, c ^ fc), device_id_type=MESH)
            cp.start()
            sends.append(cp)
        for j, (px, py) in enumerate(_other_chips(x, y)):
            for a in range(n):
                pltpu.make_async_remote_copy(
                    src_ref=ins[a].at[me], dst_ref=outs[a].at[2 * px + py], send_sem=send_sems.at[3 * a + j],
                    recv_sem=recv_sems.at[3 * a + j], device_id=(px, py, c), device_id_type=MESH).wait_recv()
        for j, (fx, fy, fc) in enumerate(flips):
            src = 4 * (x ^ fx) + 2 * (y ^ fy) + (c ^ fc)
            pltpu.make_async_remote_copy(
                src_ref=sm_ref, dst_ref=osm_ref.at[src], send_sem=ssend_sems.at[j], recv_sem=srecv_sems.at[j],
                device_id=(x ^ fx, y ^ fy, c ^ fc), device_id_type=MESH).wait_recv()
        for cp in sends:
            cp.wait_send()
        for cp in local:
            cp.wait()

    return pl.pallas_call(
        body, name="exchange_chips",
        in_specs=[HBM] * (n + 1), out_specs=[HBM] * (n + 1),
        out_shape=[jax.ShapeDtypeStruct(p.shape, p.dtype) for p in parts]
                  + [jax.ShapeDtypeStruct((8,) + small.shape, small.dtype)],
        scratch_shapes=[pltpu.SemaphoreType.DMA((3 * n,)), pltpu.SemaphoreType.DMA((3 * n,)),
                        pltpu.SemaphoreType.DMA((7,)), pltpu.SemaphoreType.DMA((7,)),
                        pltpu.SemaphoreType.DMA((n + 1,))],
    )(*parts, small)


def _swap_cores(parts, name):
    n = len(parts)

    def body(*refs):
        ins, outs, send_sems, recv_sems = refs[:n], refs[n:2 * n], refs[2 * n], refs[2 * n + 1]
        x, y, c = _place()
        copies = [pltpu.make_async_remote_copy(
            src_ref=ins[a], dst_ref=outs[a], send_sem=send_sems.at[a], recv_sem=recv_sems.at[a],
            device_id=(x, y, 1 - c), device_id_type=MESH) for a in range(n)]
        for cp in copies:
            cp.start()
        for cp in copies:
            cp.wait()

    return pl.pallas_call(
        body, name=name, in_specs=[HBM] * n, out_specs=[HBM] * n,
        out_shape=[jax.ShapeDtypeStruct(p.shape, p.dtype) for p in parts],
        scratch_shapes=[pltpu.SemaphoreType.DMA((n,)), pltpu.SemaphoreType.DMA((n,))],
    )(*parts)


def _tile(rows, cols):
    for cand in (256, 128, 64):
        if rows % cand == 0:
            return cand, cols
    if rows > 64 and cols % 256 == 0:
        return rows, 256
    return rows, cols


def _add(a, b, name):
    L, R, C = a.shape
    tr, tc = _tile(R, C)

    def body(a_ref, b_ref, o_ref):
        o_ref[...] = (a_ref[...] + b_ref[...]).astype(BF16)

    spec = pl.BlockSpec((1, tr, tc), lambda l, i, j: (l, i, j))
    return pl.pallas_call(
        body, name=name, grid=(L, R // tr, C // tc), in_specs=[spec, spec], out_specs=spec,
        out_shape=jax.ShapeDtypeStruct((L, R, C), BF16),
        compiler_params=_cparams(("parallel", "parallel", "parallel")),
    )(a, b)


def _sum_leading(buf, name):
    n, R, C = buf.shape
    tr, tc = _tile(R, C)

    def body(b_ref, o_ref):
        acc = b_ref[0].astype(F32)
        for k in range(1, n):
            acc = acc + b_ref[k].astype(F32)
        o_ref[...] = acc

    return pl.pallas_call(
        body, name=name, grid=(R // tr, C // tc),
        in_specs=[pl.BlockSpec((n, tr, tc), lambda i, j: (0, i, j))],
        out_specs=pl.BlockSpec((tr, tc), lambda i, j: (i, j)),
        out_shape=jax.ShapeDtypeStruct((R, C), F32),
        compiler_params=_cparams(("parallel", "parallel")),
    )(buf)


def _adam_update(w, g, m, v):
    c1 = 1.0 / (1.0 - ADAM_B1 ** ADAM_STEP)
    c2 = 1.0 / (1.0 - ADAM_B2 ** ADAM_STEP)
    mn = ADAM_B1 * m + (1.0 - ADAM_B1) * g
    vn = ADAM_B2 * v + (1.0 - ADAM_B2) * (g * g)
    return -ADAM_LR * ((mn * c1) / (jnp.sqrt(vn * c2) + ADAM_EPS) + ADAM_WD * w), mn, vn


def _adamw_layers(w, m, v, g_mine, g_other, name):
    _, R, C = w.shape
    tr, tc = _tile(R, C)

    def body(w_ref, m_ref, v_ref, gm_ref, go_ref, g_ref, d_ref, mo_ref, vo_ref):
        g = jnp.where(pl.program_id(0) == lax.axis_index("c"), gm_ref[...], go_ref[...])
        g_ref[0] = g
        d_ref[0], mo_ref[0], vo_ref[0] = _adam_update(w_ref[0], g, m_ref[0], v_ref[0])

    spec3 = pl.BlockSpec((1, tr, tc), lambda l, i, j: (l, i, j))
    spec2 = pl.BlockSpec((tr, tc), lambda l, i, j: (i, j))
    return pl.pallas_call(
        body, name=name, grid=(2, R // tr, C // tc),
        in_specs=[spec3] * 3 + [spec2] * 2, out_specs=[spec3] * 4,
        out_shape=[jax.ShapeDtypeStruct(w.shape, F32)] * 4,
        compiler_params=_cparams(("parallel", "parallel", "parallel")),
    )(w, m, v, g_mine, g_other)


PACK_C = 1024
_BIG = ("w_in", "w_out", "mla_w_uq", "mla_w_ukv", "conv_w")
_SMALL = ("norm_pre", "group_norm", "norm_post", "conv_b", "mla_q_norm", "mla_kv_norm", "attn_sinks")
_SMALL_W = {"norm_pre": 1024, "group_norm": 1024, "norm_post": 1024, "conv_b": 256, "mla_q_norm": 256,
            "mla_kv_norm": 128, "attn_sinks": 4}


_LOSS_AT = divmod(DEPTH * sum(_SMALL_W.values()), PACK_C)


def _pack_small(d, loss):
    flat = jnp.concatenate([d[n].reshape(-1) for n in _SMALL] + [loss.reshape(1)])
    return jnp.pad(flat, (0, 8 * PACK_C - flat.shape[0])).reshape(8, PACK_C)


def _adamw_small(w, m, v, got):
    ns = len(_SMALL)

    def body(*refs):
        got_ref = refs[3 * ns]
        outs = refs[3 * ns + 1:]
        gsum = got_ref[0]
        for d in range(1, 8):
            gsum = gsum + got_ref[d]
        outs[4 * ns][...] = gsum[_LOSS_AT[0]:_LOSS_AT[0] + 1, _LOSS_AT[1]:_LOSS_AT[1] + 1]
        off = 0
        for i, name in enumerate(_SMALL):
            wd = _SMALL_W[name]
            rows = []
            for l in range(DEPTH):
                r, c0 = divmod(off + l * wd, PACK_C)
                rows.append(gsum[r:r + 1, c0:c0 + wd])
            off += DEPTH * wd
            g = jnp.concatenate(rows, axis=0)
            delta, mn, vn = _adam_update(refs[i][...], g, refs[ns + i][...], refs[2 * ns + i][...])
            outs[i][...] = g
            outs[ns + i][...] = delta
            outs[2 * ns + i][...] = mn
            outs[3 * ns + i][...] = vn

    shapes = [jax.ShapeDtypeStruct(w[n].shape, F32) for n in _SMALL]
    res = pl.pallas_call(body, name="adamw_small", out_shape=shapes * 4 + [jax.ShapeDtypeStruct((1, 1), F32)])(
        *[w[n] for n in _SMALL], *[m[n] for n in _SMALL], *[v[n] for n in _SMALL], got)
    return [dict(zip(_SMALL, res[k * ns:(k + 1) * ns])) for k in range(4)], res[4 * ns]


def _w_in_internal(wt):
    rows = []
    for n in _INT_ORDER:
        o, wd = _REAL_OFF[n]
        rows.append(wt[o:o + wd])
        if _INT_W[n] != wd:
            rows.append(jnp.zeros((_INT_W[n] - wd, wt.shape[1]), wt.dtype))
    return jnp.concatenate(rows, axis=0)


def _w_in_real(dwt):
    return jnp.concatenate([dwt[_INT_OFF[n]:_INT_OFF[n] + wd] for n, wd in _REAL], axis=0)


def _uq_internal(w):
    return jnp.pad(w.reshape(256, 4, 96), ((0, 0), (0, 0), (0, 32))).reshape(256, 512)


def _uq_real(dw):
    return dw.reshape(256, 4, 128)[:, :, :96].reshape(256, 384)


def _ukv_internal(w):
    w4 = w.reshape(128, 4, 128)
    k = jnp.pad(w4[:, :, :64], ((0, 0), (0, 0), (0, 64))).reshape(128, 512)
    return jnp.concatenate([k, w4[:, :, 64:].reshape(128, 256)], axis=1)


def _ukv_real(dw):
    k = dw[:, :512].reshape(128, 4, 128)[:, :, :64]
    v = dw[:, 512:].reshape(128, 4, 64)
    return jnp.concatenate([k, v], axis=2).reshape(128, 512)


def _layer_fwd(x, rope, p, tgt=None, fetch=()):
    xn, hb, hf, *fetched = _inproj_fwd(x, p["norm_pre"], p["w_in"], fetch)
    ya = _swa_fwd(hb, p["attn_sinks"])
    qm, km, vm, vt = _mla_prep_fwd(hf, rope, p["mla_q_norm"], p["mla_kv_norm"], p["mla_w_uq"], p["mla_w_ukv"])
    yc, lse = _mla_fwd(qm, km, vt)
    yd, tot, cnt = _sb_fwd(hb)
    x_next = _epilogue_fwd(x, ya, yc, yd, hf, p["conv_w"], p["conv_b"], p["group_norm"], p["w_out"], p["norm_post"],
                           tgt)
    saved = dict(x=x, xn=xn, hb=hb, hf=hf, ya=ya, yc=yc, yd=yd, tot=tot, cnt=cnt, qm=qm, km=km, vm=vm, lse=lse)
    return x_next, saved, fetched


def _layer_bwd(dx_next, rope, p, s):
    (dya, dyc, dyd, dhf, dw_out, dg_post, dg_grp, dconv_w, dconv_b) = _epilogue_bwd(
        dx_next, s["ya"], s["yc"], s["yd"], s["hf"], p["conv_w"], p["conv_b"], p["group_norm"], p["w_out"],
        p["norm_post"])
    dq_d, dk_d, dv_d = _sb_bwd(s["hb"], s["tot"], s["cnt"], dyd)
    dqm, dkt, dvt = _mla_bwd(s["qm"], s["km"], s["vm"], s["yc"], s["lse"], dyc)
    dc, dw_uq, dw_ukv, dg_q, dg_kv = _mla_prep_bwd(
        s["hf"], rope, p["mla_q_norm"], p["mla_kv_norm"], p["mla_w_uq"], p["mla_w_ukv"], dqm, dkt, dvt)
    dq_a, dk_a, dv_a, dsinks = _swa_bwd(s["hb"], p["attn_sinks"], dya)
    dx, dh, dg_pre = _inproj_bwd_dx(s["x"], p["norm_pre"], p["w_in"], dx_next,
                                    [dq_a, dk_a, dv_a, dq_d, dk_d, dv_d, dhf, dc])
    dwt_in = _grad_over_tokens(s["xn"], dh, "inproj_bwd_dw")
    grads = dict(norm_pre=dg_pre[0], w_in_t=_w_in_real(dwt_in), attn_sinks=dsinks[0, :4], conv_w=dconv_w[:3],
                 conv_b=dconv_b[0], mla_q_norm=dg_q[0], mla_w_uq=_uq_real(dw_uq), mla_kv_norm=dg_kv[0],
                 mla_w_ukv=_ukv_real(dw_ukv), group_norm=dg_grp[0], w_out=dw_out, norm_post=dg_post[0])
    return dx, grads


_WEIGHTS = ["norm_pre", "w_in", "attn_sinks", "conv_w", "conv_b", "mla_q_norm", "mla_w_uq", "mla_kv_norm",
            "mla_w_ukv", "group_norm", "w_out", "norm_post"]


def kernel(x, positions, norm_pre, w_in, attn_sinks, conv_w, conv_b, mla_q_norm, mla_w_uq, mla_kv_norm, mla_w_ukv, group_norm, w_out, norm_post, loss_target, m_norm_pre, m_w_in, m_attn_sinks, m_conv_w, m_conv_b, m_mla_q_norm, m_mla_w_uq, m_mla_kv_norm, m_mla_w_ukv, m_group_norm, m_w_out, m_norm_post, v_norm_pre, v_w_in, v_attn_sinks, v_conv_w, v_conv_b, v_mla_q_norm, v_mla_w_uq, v_mla_kv_norm, v_mla_w_ukv, v_group_norm, v_w_out, v_norm_post):
    w = dict(norm_pre=norm_pre, w_in=w_in, attn_sinks=attn_sinks, conv_w=conv_w, conv_b=conv_b,
             mla_q_norm=mla_q_norm, mla_w_uq=mla_w_uq, mla_kv_norm=mla_kv_norm, mla_w_ukv=mla_w_ukv,
             group_norm=group_norm, w_out=w_out, norm_post=norm_post)
    m = dict(norm_pre=m_norm_pre, w_in=m_w_in, attn_sinks=m_attn_sinks, conv_w=m_conv_w, conv_b=m_conv_b,
             mla_q_norm=m_mla_q_norm, mla_w_uq=m_mla_w_uq, mla_kv_norm=m_mla_kv_norm, mla_w_ukv=m_mla_w_ukv,
             group_norm=m_group_norm, w_out=m_w_out, norm_post=m_norm_post)
    v = dict(norm_pre=v_norm_pre, w_in=v_w_in, attn_sinks=v_attn_sinks, conv_w=v_conv_w, conv_b=v_conv_b,
             mla_q_norm=v_mla_q_norm, mla_w_uq=v_mla_w_uq, mla_kv_norm=v_mla_kv_norm, mla_w_ukv=v_mla_w_ukv,
             group_norm=v_group_norm, w_out=v_w_out, norm_post=v_norm_post)
    T = x.shape[1]
    xs = x[0]
    rope = _rope_tables(positions[0].reshape(T, 1))
    tgt = loss_target[0]
    core = lax.axis_index("c")

    def shard_parts(l):
        halves = lambda a: a.reshape((2, a.shape[0] // 2) + a.shape[1:])
        return [halves(jnp.swapaxes(w["w_in"][l], 0, 1).astype(BF16))] + [
            halves(w[n][l].astype(BF16)) for n in _BIG[1:4]] + [jnp.stack([w["conv_w"][l]] * 2)]

    def layer_params(l, got):
        whole = lambda a: a.reshape((4, 2 * a.shape[2]) + a.shape[3:])
        by_cols = lambda a: jnp.transpose(a, (1, 0, 2)).reshape(a.shape[1], 4 * a.shape[2])
        return dict(
            norm_pre=norm_pre[l:l + 1], w_in=_w_in_internal(whole(got[0]).reshape(D_IN, D_MODEL)),
            attn_sinks=attn_sinks[l], conv_w=by_cols(got[4][:, 0]), conv_b=conv_b[l:l + 1],
            mla_q_norm=mla_q_norm[l:l + 1], mla_w_uq=_uq_internal(by_cols(whole(got[2]))),
            mla_kv_norm=mla_kv_norm[l:l + 1], mla_w_ukv=_ukv_internal(by_cols(whole(got[3]))),
            group_norm=group_norm[l:l + 1], w_out=whole(got[1]).reshape(D_MODEL, D_MODEL),
            norm_post=norm_post[l:l + 1])

    layers, saved = [], []
    h, got = xs, _gather_weights(shard_parts(0))
    for l in range(DEPTH):
        last = l == DEPTH - 1
        layers.append(layer_params(l, got))
        h, s, got = _layer_fwd(h, rope, layers[l], tgt if last else None, () if last else shard_parts(l + 1))
        saved.append(s)
    dy, loss_part = h

    grads = [None] * DEPTH
    for l in reversed(range(DEPTH)):
        dy, grads[l] = _layer_bwd(dy, rope, layers[l], saved[l])

    turned = ("w_in", "mla_w_uq")
    turn = lambda n, a: jnp.swapaxes(a, -1, -2) if n in turned else a

    def chunks(n, a):
        if n in ("w_out", "w_in"):
            return a.reshape(4, a.shape[0] // 4, a.shape[1])
        if n in turned:
            return a.T.reshape(4, a.shape[1] // 4, a.shape[0])
        return jnp.transpose(a.reshape(a.shape[0], 4, a.shape[1] // 4), (1, 0, 2))

    grad = lambda l, n: grads[l]["w_in_t" if n == "w_in" else n]
    mine = [chunks(n, jnp.where(core == 0, grad(0, n), grad(1, n))) for n in _BIG]
    theirs = [chunks(n, jnp.where(core == 0, grad(1, n), grad(0, n))) for n in _BIG]
    from_sibling = _swap_cores(theirs, "swap_layer_chunks")
    summed = [_add(a, b, "add_cores_" + n) for n, a, b in zip(_BIG, mine, from_sibling)]
    small = _pack_small({n: jnp.stack([grads[l][n] for l in range(DEPTH)]) for n in _SMALL}, loss_part[0, 0])
    *got, got_small = _exchange_chips(summed, small)
    done = [_sum_leading(b, "sum_chips_" + n) for n, b in zip(_BIG, got)]
    done_other = _swap_cores(done, "swap_layer_shards")

    outs, loss = _adamw_small(w, m, v, got_small)
    for n, gm, go in zip(_BIG, done, done_other):
        for d, a in zip(outs, _adamw_layers(turn(n, w[n]), turn(n, m[n]), turn(n, v[n]), gm, go, "adamw_" + n)):
            d[n] = turn(n, a)
    return (loss[0, 0], dy[None], *[outs[0][n] for n in _WEIGHTS], *[outs[1][n] for n in _WEIGHTS],
            *[outs[2][n] for n in _WEIGHTS], *[outs[3][n] for n in _WEIGHTS])
```

```python
import math

import jax
import jax.numpy as jnp
from jax import lax
from jax.experimental import pallas as pl
from jax.experimental.pallas import tpu as pltpu

F32 = jnp.float32
BF16 = jnp.bfloat16
MESH = pl.DeviceIdType.MESH

D_MODEL = 1024
DEPTH = 2
EPS = 1e-6
BLOCK = 128
HEAD = 64
LANES = 128
GROUP = 256
LOG2E = 1.4426950408889634
LN2 = 0.6931471805599453
MLA_QSCALE = 96 ** -0.5 * LOG2E
ROPE_HALF = 16
ROPE_THETA = 10000.0
SWA_SUB = 2
ATT_BLK = 256
MLA_BQ = 512
NEG = -1e30
SB_DEAD = -104.0

ADAM_LR, ADAM_B1, ADAM_B2, ADAM_EPS, ADAM_WD, ADAM_STEP = 0.001, 0.9, 0.999, 1e-08, 0.01, 10

_REAL = [("a_q", 256), ("a_k", 128), ("a_v", 128), ("b_b", 256), ("b_c", 256), ("b_x", 256),
         ("c_q", 256), ("c_kv", 128), ("c_kr", 32), ("d_q", 256), ("d_k", 256), ("d_v", 256),
         ("gate", 1024)]
_REAL_OFF = {}
_o = 0
for _n, _w in _REAL:
    _REAL_OFF[_n] = (_o, _w)
    _o += _w
D_IN = _o
_INT_ORDER = ["a_q", "a_k", "a_v", "d_q", "d_k", "d_v", "gate", "b_b", "b_c", "b_x", "c_q", "c_kv", "c_kr"]
_INT_W = dict(_REAL)
_INT_W["c_kr"] = 128
_INT_OFF = {}
_o = 0
for _n in _INT_ORDER:
    _INT_OFF[_n] = _o
    _o += _INT_W[_n]
N_INT = _o
N_HB = _INT_OFF["gate"]
N_HF = N_INT - N_HB

VMEM_LIMIT = 56 * 1024 * 1024


def _cparams(sem):
    return pltpu.CompilerParams(dimension_semantics=sem, vmem_limit_bytes=VMEM_LIMIT)


def _dot(a, b):
    return jnp.dot(a, b, preferred_element_type=F32)


def _dot_nt(a, b):
    return lax.dot_general(a, b, (((1,), (1,)), ((), ())), preferred_element_type=F32)


def _dot_tn(a, b):
    return lax.dot_general(a, b, (((0,), (0,)), ((), ())), preferred_element_type=F32)


def _split(x):
    hi = x.astype(BF16)
    lo = (x - hi.astype(F32)).astype(BF16)
    return hi, lo


def _rms(x):
    return lax.rsqrt(jnp.mean(x * x, axis=-1, keepdims=True) + EPS)


def _rms_bwd(dy, xhat, r, g):
    dxhat = dy * g
    return r * (dxhat - xhat * jnp.mean(dxhat * xhat, axis=-1, keepdims=True)), dy * xhat


def _colsum(x):
    return jnp.sum(x, axis=0, keepdims=True)


def _inproj_fwd(x, g, wt, fetch=()):
    T = x.shape[0]
    tm = 512
    nt, n = T // tm, len(fetch)

    def body(x_ref, g_ref, w_ref, *rest):
        xn_ref, hb_ref, hf_ref = rest[n:n + 3]
        if n:
            start, finish = _gather_plan(rest[:n], rest[n + 3:2 * n + 3], rest[2 * n + 3:])
            pl.when(pl.program_id(0) == 0)(start)
        xv = x_ref[...]
        xn = (xv * _rms(xv) * g_ref[...]).astype(BF16)
        xn_ref[...] = xn
        h = _dot_nt(xn, w_ref[...])
        hb_ref[...] = h[:, :N_HB].astype(BF16)
        hf_ref[...] = h[:, N_HB:]
        if n:
            pl.when(pl.program_id(0) == nt - 1)(finish)

    return pl.pallas_call(
        body, name="inproj_fwd_fetch" if n else "inproj_fwd", grid=(nt,),
        in_specs=[pl.BlockSpec((tm, D_MODEL), lambda i: (i, 0)),
                  pl.BlockSpec((1, D_MODEL), lambda i: (0, 0)),
                  pl.BlockSpec((N_INT, D_MODEL), lambda i: (0, 0))] + [HBM] * n,
        out_specs=[pl.BlockSpec((tm, D_MODEL), lambda i: (i, 0)),
                   pl.BlockSpec((tm, N_HB), lambda i: (i, 0)),
                   pl.BlockSpec((tm, N_HF), lambda i: (i, 0))] + [HBM] * n,
        out_shape=[jax.ShapeDtypeStruct((T, D_MODEL), BF16),
                   jax.ShapeDtypeStruct((T, N_HB), BF16),
                   jax.ShapeDtypeStruct((T, N_HF), F32)]
                  + [jax.ShapeDtypeStruct((4,) + s.shape, s.dtype) for s in fetch],
        scratch_shapes=_gather_sems(n) if n else [],
        compiler_params=_cparams(("arbitrary",) if n else ("parallel",)),
    )(x, g, wt, *fetch)


def _inproj_bwd_dx(x, g, wt, dx_next, pieces):
    T = x.shape[0]
    tm = 512
    widths = [p.shape[1] for p in pieces]
    assert sum(widths) == N_INT

    def body(x_ref, g_ref, w_ref, dxn_ref, *rest):
        p_refs = rest[:len(pieces)]
        dx_ref, dh_ref, dg_ref = rest[len(pieces):]
        dh = jnp.concatenate([p[...].astype(BF16) for p in p_refs], axis=1)
        dh_ref[...] = dh
        dxn = _dot(dh, w_ref[...])
        xv = x_ref[...]
        r = _rms(xv)
        dx, dgrow = _rms_bwd(dxn, xv * r, r, g_ref[...])
        dx_ref[...] = dx + dxn_ref[...]

        @pl.when(pl.program_id(0) == 0)
        def _():
            dg_ref[...] = jnp.zeros_like(dg_ref)

        dg_ref[...] += _colsum(dgrow)

    return pl.pallas_call(
        body, name="inproj_bwd_dx", grid=(T // tm,),
        in_specs=[pl.BlockSpec((tm, D_MODEL), lambda i: (i, 0)),
                  pl.BlockSpec((1, D_MODEL), lambda i: (0, 0)),
                  pl.BlockSpec((N_INT, D_MODEL), lambda i: (0, 0)),
                  pl.BlockSpec((tm, D_MODEL), lambda i: (i, 0))]
                 + [pl.BlockSpec((tm, wd), lambda i: (i, 0)) for wd in widths],
        out_specs=[pl.BlockSpec((tm, D_MODEL), lambda i: (i, 0)),
                   pl.BlockSpec((tm, N_INT), lambda i: (i, 0)),
                   pl.BlockSpec((1, D_MODEL), lambda i: (0, 0))],
        out_shape=[jax.ShapeDtypeStruct((T, D_MODEL), F32),
                   jax.ShapeDtypeStruct((T, N_INT), BF16),
                   jax.ShapeDtypeStruct((1, D_MODEL), F32)],
        compiler_params=_cparams(("arbitrary",)),
    )(x, g, wt, dx_next, *pieces)


def _grad_over_tokens(a, b, name):
    T, M = a.shape
    N = b.shape[1]
    tm, tn = min(1024, T), 896

    def body(a_ref, b_ref, o_ref):
        @pl.when(pl.program_id(1) == 0)
        def _():
            o_ref[...] = jnp.zeros_like(o_ref)

        o_ref[...] += _dot_tn(b_ref[...], a_ref[...])

    return pl.pallas_call(
        body, name=name, grid=(N // tn, T // tm),
        in_specs=[pl.BlockSpec((tm, M), lambda j, t: (t, 0)),
                  pl.BlockSpec((tm, tn), lambda j, t: (t, j))],
        out_specs=pl.BlockSpec((tn, M), lambda j, t: (j, 0)),
        out_shape=jax.ShapeDtypeStruct((N, M), F32),
        compiler_params=_cparams(("parallel", "arbitrary")),
    )(a, b)


def _roll_f32(x, shift):
    return pltpu.roll(x.astype(F32), shift, 1)


def _swa_operands(h, q, k_prev, k_cur, v_prev, v_cur):
    p, e = h // 2, h % 2
    lane = lax.broadcasted_iota(jnp.int32, (1, LANES), 1) // HEAD
    q = q[:, p * LANES:(p + 1) * LANES]
    if e != p:
        q = _roll_f32(q, HEAD).astype(BF16)
        v_prev = _roll_f32(v_prev, HEAD).astype(BF16)
        v_cur = _roll_f32(v_cur, HEAD).astype(BF16)
    qs = jnp.where(lane == p, q, 0) * 0.125
    return dict(p=p, e=e, lane=lane, qs=qs, k_prev=k_prev, k_cur=k_cur,
                v_prev=jnp.where(lane == e, v_prev, 0), v_cur=jnp.where(lane == e, v_cur, 0),
                s_prev=_dot_nt(qs, k_prev), s_cur=_dot_nt(qs, k_cur))


def _swa_probs(ops, sink, no_prev):
    row = lax.broadcasted_iota(jnp.int32, (BLOCK, BLOCK), 0)
    col = lax.broadcasted_iota(jnp.int32, (BLOCK, BLOCK), 1)
    ok_prev = col > row if no_prev is None else jnp.logical_and(col > row, jnp.logical_not(no_prev))
    s_prev = jnp.where(ok_prev, ops["s_prev"], NEG)
    s_cur = jnp.where(col <= row, ops["s_cur"], NEG)
    m = jnp.maximum(jnp.maximum(jnp.max(s_prev, axis=1, keepdims=True),
                                jnp.max(s_cur, axis=1, keepdims=True)), sink)
    p_prev = jnp.exp(s_prev - m)
    p_cur = jnp.exp(s_cur - m)
    p_sink = jnp.exp(sink - m)
    inv = 1.0 / (jnp.sum(p_prev, axis=1, keepdims=True) + jnp.sum(p_cur, axis=1, keepdims=True) + p_sink)
    return p_prev * inv, p_cur * inv, p_sink * inv


def _swa_specs(T):
    n = T // (BLOCK * SWA_SUB)
    qo, ko, vo = (_INT_OFF[name] // LANES for name in ("a_q", "a_k", "a_v"))
    halo = lambda i: jnp.maximum(i * SWA_SUB - 1, 0)
    return [pl.BlockSpec((BLOCK * SWA_SUB, 256), lambda i: (i, qo // 2)),
            pl.BlockSpec((BLOCK, LANES), lambda i: (halo(i), ko)),
            pl.BlockSpec((BLOCK * SWA_SUB, LANES), lambda i: (i, ko)),
            pl.BlockSpec((BLOCK, LANES), lambda i: (halo(i), vo)),
            pl.BlockSpec((BLOCK * SWA_SUB, LANES), lambda i: (i, vo)),
            pl.BlockSpec(memory_space=pltpu.SMEM)], n


def _swa_units(q_ref, kh_ref, kc_ref, vh_ref, vc_ref, s_ref):
    blk = lambda a: slice(a * BLOCK, (a + 1) * BLOCK)
    units = [(a, h) for a in range(SWA_SUB) for h in range(4)]
    ops = {}
    for a, h in units:
        k_prev, v_prev = (kh_ref[...], vh_ref[...]) if a == 0 else (kc_ref[blk(a - 1), :], vc_ref[blk(a - 1), :])
        ops[a, h] = _swa_operands(h, q_ref[blk(a), :], k_prev, kc_ref[blk(a), :], v_prev, vc_ref[blk(a), :])
    probs = {(a, h): _swa_probs(ops[a, h], s_ref[h], pl.program_id(0) == 0 if a == 0 else None) for a, h in units}
    return units, ops, probs, blk


def _swa_fwd(hb, sinks):
    T = hb.shape[0]
    specs, n = _swa_specs(T)

    def body(q_ref, kh_ref, kc_ref, vh_ref, vc_ref, s_ref, o_ref):
        units, ops, probs, blk = _swa_units(q_ref, kh_ref, kc_ref, vh_ref, vc_ref, s_ref)
        outs = {u: _dot(probs[u][0].astype(BF16), ops[u]["v_prev"]) + _dot(probs[u][1].astype(BF16), ops[u]["v_cur"])
                for u in units}
        for a in range(SWA_SUB):
            for p in range(2):
                o_ref[blk(a), p * LANES:(p + 1) * LANES] = outs[a, 2 * p] + outs[a, 2 * p + 1]

    return pl.pallas_call(
        body, name="swa_fwd", grid=(n,), in_specs=specs,
        out_specs=pl.BlockSpec((BLOCK * SWA_SUB, 256), lambda i: (i, 0)),
        out_shape=jax.ShapeDtypeStruct((T, 256), F32),
        compiler_params=_cparams(("parallel",)),
    )(hb, hb, hb, hb, hb, sinks)


def _swa_bwd(hb, sinks, dy):
    T = hb.shape[0]
    specs, n = _swa_specs(T)

    def body(q_ref, kh_ref, kc_ref, vh_ref, vc_ref, s_ref, dy_ref, dq_ref, dk_ref, dv_ref, ds_ref):
        i = pl.program_id(0)

        @pl.when(i == 0)
        def _():
            ds_ref[...] = jnp.zeros_like(ds_ref)

        lane_id = lax.broadcasted_iota(jnp.int32, (8, LANES), 1)
        units, ops, probs, blk = _swa_units(q_ref, kh_ref, kc_ref, vh_ref, vc_ref, s_ref)
        dos = {(a, h): jnp.where(ops[a, h]["lane"] == ops[a, h]["e"],
                                 dy_ref[blk(a), ops[a, h]["p"] * LANES:(ops[a, h]["p"] + 1) * LANES], 0.0)
               for a, h in units}
        dobs = {u: dos[u].astype(BF16) for u in units}
        pbs = {u: (probs[u][0].astype(BF16), probs[u][1].astype(BF16)) for u in units}
        outs = {u: _dot(pbs[u][0], ops[u]["v_prev"]) + _dot(pbs[u][1], ops[u]["v_cur"]) for u in units}
        dps = {u: (_dot_nt(dobs[u], ops[u]["v_prev"]), _dot_nt(dobs[u], ops[u]["v_cur"])) for u in units}
        dss, dsinks = {}, jnp.zeros((8, LANES), F32)
        for u in units:
            delta = jnp.sum(dos[u] * outs[u], axis=1, keepdims=True)
            dss[u] = ((probs[u][0] * (dps[u][0] - delta)).astype(BF16),
                      (probs[u][1] * (dps[u][1] - delta)).astype(BF16))
            dsink = -jnp.sum(probs[u][2] * delta, axis=0, keepdims=True)
            dsinks += jnp.where(lane_id == u[1], dsink, 0.0)
        ds_ref[...] += dsinks
        dqs = {u: (_dot(dss[u][0], ops[u]["k_prev"]) + _dot(dss[u][1], ops[u]["k_cur"])) * 0.125 for u in units}
        zero = jnp.zeros((BLOCK, LANES), F32)
        dk_as_prev, dk_as_cur = [zero] * SWA_SUB, [zero] * SWA_SUB
        dv_as_prev, dv_as_cur = [zero] * SWA_SUB, [zero] * SWA_SUB
        for a, h in units:
            p, e = ops[a, h]["p"], ops[a, h]["e"]
            dob_v = dobs[a, h] if e == p else pltpu.roll(dos[a, h], HEAD, 1).astype(BF16)
            dk_as_prev[a] = dk_as_prev[a] + _dot_tn(dss[a, h][0], ops[a, h]["qs"])
            dk_as_cur[a] = dk_as_cur[a] + _dot_tn(dss[a, h][1], ops[a, h]["qs"])
            dv_as_prev[a] = dv_as_prev[a] + _dot_tn(pbs[a, h][0], dob_v)
            dv_as_cur[a] = dv_as_cur[a] + _dot_tn(pbs[a, h][1], dob_v)
        base = i * SWA_SUB
        for a in range(SWA_SUB):
            rows = pl.ds(pl.multiple_of((base + a) * BLOCK, BLOCK), BLOCK)
            more = a + 1 < SWA_SUB
            dk_ref[rows, :] = dk_as_cur[a] + (dk_as_prev[a + 1] if more else 0.0)
            dv_ref[rows, :] = dv_as_cur[a] + (dv_as_prev[a + 1] if more else 0.0)
        halo = pl.ds(pl.multiple_of(jnp.maximum(base - 1, 0) * BLOCK, BLOCK), BLOCK)
        dk_ref[halo, :] += dk_as_prev[0]
        dv_ref[halo, :] += dv_as_prev[0]
        for a in range(SWA_SUB):
            for p in range(2):
                dq_pair = jnp.zeros((BLOCK, LANES), F32)
                for e in range(2):
                    dq = jnp.where(ops[a, 2 * p + e]["lane"] == p, dqs[a, 2 * p + e], 0.0)
                    dq_pair += dq if e == p else pltpu.roll(dq, HEAD, 1)
                dq_ref[blk(a), p * LANES:(p + 1) * LANES] = dq_pair.astype(BF16)

    return pl.pallas_call(
        body, name="swa_bwd", grid=(n,),
        in_specs=specs + [pl.BlockSpec((BLOCK * SWA_SUB, 256), lambda i: (i, 0))],
        out_specs=[pl.BlockSpec((BLOCK * SWA_SUB, 256), lambda i: (i, 0)),
                   pl.BlockSpec((T, LANES), lambda i: (0, 0)),
                   pl.BlockSpec((T, LANES), lambda i: (0, 0)),
                   pl.BlockSpec((8, LANES), lambda i: (0, 0))],
        out_shape=[jax.ShapeDtypeStruct((T, 256), BF16),
                   jax.ShapeDtypeStruct((T, LANES), F32),
                   jax.ShapeDtypeStruct((T, LANES), F32),
                   jax.ShapeDtypeStruct((8, LANES), F32)],
        compiler_params=_cparams(("arbitrary",)),
    )(hb, hb, hb, hb, hb, sinks, dy)


def _rope_tables(pos):
    T = pos.shape[0]
    tm = 512

    def body(pos_ref, o_ref):
        lane = lax.broadcasted_iota(jnp.int32, (1, LANES), 1)
        active = jnp.logical_and(lane >= HEAD, lane < HEAD + 2 * ROPE_HALF)
        idx = ((lane - HEAD) % ROPE_HALF).astype(F32)
        freq = jnp.exp(idx * (-math.log(ROPE_THETA) / ROPE_HALF))
        ang = pos_ref[...].astype(F32) * freq
        cos, sin = jnp.cos(ang), jnp.sin(ang)
        o_ref[:, 0:LANES] = jnp.where(active, cos, 1.0)
        o_ref[:, LANES:2 * LANES] = jnp.where(jnp.logical_and(active, lane >= HEAD + ROPE_HALF), sin, 0.0)
        o_ref[:, 2 * LANES:] = jnp.where(jnp.logical_and(active, lane < HEAD + ROPE_HALF), -sin, 0.0)

    return pl.pallas_call(
        body, name="rope_tables", grid=(T // tm,),
        in_specs=[pl.BlockSpec((tm, 1), lambda i: (i, 0))],
        out_specs=pl.BlockSpec((tm, 3 * LANES), lambda i: (i, 0)),
        out_shape=jax.ShapeDtypeStruct((T, 3 * LANES), F32),
        compiler_params=_cparams(("parallel",)),
    )(pos)


def _rope_factors(tab_ref):
    return tab_ref[:, 0:LANES], tab_ref[:, LANES:2 * LANES], tab_ref[:, 2 * LANES:]


def _rope(x, tabs):
    c, s_up, s_dn = tabs
    return x * c + pltpu.roll(x, ROPE_HALF, 1) * s_up + pltpu.roll(x, LANES - ROPE_HALF, 1) * s_dn


def _rope_t(dy, tabs):
    c, s_up, s_dn = tabs
    return dy * c + pltpu.roll(dy * s_up, LANES - ROPE_HALF, 1) + pltpu.roll(dy * s_dn, ROPE_HALF, 1)


def _mla_lat_specs(tm):
    cq, ckv, ckr = ((_INT_OFF[n] - N_HB) for n in ("c_q", "c_kv", "c_kr"))
    return [pl.BlockSpec((tm, 256), lambda i: (i, cq // 256)),
            pl.BlockSpec((tm, LANES), lambda i: (i, ckv // LANES)),
            pl.BlockSpec((tm, LANES), lambda i: (i, ckr // LANES)),
            pl.BlockSpec((tm, 3 * LANES), lambda i: (i, 0)),
            pl.BlockSpec((1, 256), lambda i: (0, 0)),
            pl.BlockSpec((1, LANES), lambda i: (0, 0)),
            pl.BlockSpec((256, 512), lambda i: (0, 0)),
            pl.BlockSpec((LANES, 768), lambda i: (0, 0))]


def _mla_prep_fwd(hf, rope, g_q, g_kv, w_uq, w_ukv):
    T = hf.shape[0]
    tm = 512
    sub = tm // ATT_BLK

    def body(cq_ref, ckv_ref, ckr_ref, tab_ref, gq_ref, gkv_ref, wq_ref, wkv_ref, qm_ref, km_ref, vm_ref, vt_ref):
        tabs = _rope_factors(tab_ref)
        cq = cq_ref[...]
        q = _dot((cq * _rms(cq) * gq_ref[...]).astype(BF16), wq_ref[...])
        ckv = ckv_ref[...]
        kv = _dot((ckv * _rms(ckv) * gkv_ref[...]).astype(BF16), wkv_ref[...])
        kr = _rope(pltpu.roll(ckr_ref[...], HEAD, 1), tabs)
        for h in range(4):
            sl = slice(h * LANES, (h + 1) * LANES)
            qm_ref[:, sl] = (_rope(q[:, sl], tabs) * MLA_QSCALE).astype(BF16)
            km_ref[:, sl] = (kv[:, sl] + kr).astype(BF16)
        vm_ref[...] = kv[:, 512:].astype(BF16)
        for p in range(2):
            for s in range(sub):
                tile = kv[s * ATT_BLK:(s + 1) * ATT_BLK, 512 + p * LANES:512 + (p + 1) * LANES]
                vt_ref[p, s] = jnp.transpose(tile).astype(BF16)

    return pl.pallas_call(
        body, name="mla_prep_fwd", grid=(T // tm,), in_specs=_mla_lat_specs(tm),
        out_specs=[pl.BlockSpec((tm, 512), lambda i: (i, 0)),
                   pl.BlockSpec((tm, 512), lambda i: (i, 0)),
                   pl.BlockSpec((tm, 256), lambda i: (i, 0)),
                   pl.BlockSpec((2, sub, LANES, ATT_BLK), lambda i: (0, i, 0, 0))],
        out_shape=[jax.ShapeDtypeStruct((T, 512), BF16),
                   jax.ShapeDtypeStruct((T, 512), BF16),
                   jax.ShapeDtypeStruct((T, 256), BF16),
                   jax.ShapeDtypeStruct((2, T // ATT_BLK, LANES, ATT_BLK), BF16)],
        compiler_params=_cparams(("parallel",)),
    )(hf, hf, hf, rope, g_q, g_kv, w_uq, w_ukv)


def _mla_prep_bwd(hf, rope, g_q, g_kv, w_uq, w_ukv, dqm, dkt, dvt):
    T = hf.shape[0]
    tm = 512
    sub = tm // ATT_BLK

    def body(cq_ref, ckv_ref, ckr_ref, tab_ref, gq_ref, gkv_ref, wq_ref, wkv_ref, dq_ref, dk_ref, dv_ref,
             dc_ref, dwq_ref, dwkv_ref, dgq_ref, dgkv_ref):
        @pl.when(pl.program_id(0) == 0)
        def _():
            dwq_ref[...] = jnp.zeros_like(dwq_ref)
            dwkv_ref[...] = jnp.zeros_like(dwkv_ref)
            dgq_ref[...] = jnp.zeros_like(dgq_ref)
            dgkv_ref[...] = jnp.zeros_like(dgkv_ref)

        tabs = _rope_factors(tab_ref)
        lane =lax.broadcasted_iota(jnp.int32, (1, LANES), 1)
        dq = jnp.concatenate([_rope_t(dq_ref[:, h * LANES:(h + 1) * LANES] * MLA_QSCALE, tabs)
                              for h in range(4)], axis=1).astype(BF16)
        cq = cq_ref[...]
        rq = _rms(cq)
        cqn = (cq * rq * gq_ref[...]).astype(BF16)
        dwq_ref[...] += _dot_tn(cqn, dq)
        dcq, dgrow = _rms_bwd(_dot_nt(dq, wq_ref[...]), cq * rq, rq, gq_ref[...])
        dgq_ref[...] += _colsum(dgrow)
        dc_ref[:, 0:256] = dcq.astype(BF16)

        dk = jnp.concatenate([jnp.concatenate([jnp.transpose(dk_ref[p, s]) for p in range(2)], axis=1)
                              for s in range(sub)], axis=0) * LN2
        dv = jnp.concatenate([jnp.concatenate([jnp.transpose(dv_ref[p, s]) for p in range(2)], axis=1)
                              for s in range(sub)], axis=0)
        dkr = dk[:, 0:LANES] + dk[:, LANES:2 * LANES] + dk[:, 2 * LANES:3 * LANES] + dk[:, 3 * LANES:]
        dkr = pltpu.roll(_rope_t(dkr, tabs), HEAD, 1)
        dc_ref[:, 384:512] = jnp.where(lane < 2 * ROPE_HALF, dkr, 0.0).astype(BF16)
        dkv = jnp.concatenate([dk.astype(BF16), dv.astype(BF16)], axis=1)
        ckv = ckv_ref[...]
        rkv = _rms(ckv)
        ckvn = (ckv * rkv * gkv_ref[...]).astype(BF16)
        dwkv_ref[...] += _dot_tn(ckvn, dkv)
        dckv, dgrow = _rms_bwd(_dot_nt(dkv, wkv_ref[...]), ckv * rkv, rkv, gkv_ref[...])
        dgkv_ref[...] += _colsum(dgrow)
        dc_ref[:, 256:384] = dckv.astype(BF16)

    return pl.pallas_call(
        body, name="mla_prep_bwd", grid=(T // tm,),
        in_specs=_mla_lat_specs(tm) + [pl.BlockSpec((tm, 512), lambda i: (i, 0)),
                                       pl.BlockSpec((2, sub, 256, ATT_BLK), lambda i: (0, i, 0, 0)),
                                       pl.BlockSpec((2, sub, LANES, ATT_BLK), lambda i: (0, i, 0, 0))],
        out_specs=[pl.BlockSpec((tm, 512), lambda i: (i, 0)),
                   pl.BlockSpec((256, 512), lambda i: (0, 0)),
                   pl.BlockSpec((LANES, 768), lambda i: (0, 0)),
                   pl.BlockSpec((1, 256), lambda i: (0, 0)),
                   pl.BlockSpec((1, LANES), lambda i: (0, 0))],
        out_shape=[jax.ShapeDtypeStruct((T, 512), BF16),
                   jax.ShapeDtypeStruct((256, 512), F32),
                   jax.ShapeDtypeStruct((LANES, 768), F32),
                   jax.ShapeDtypeStruct((1, 256), F32),
                   jax.ShapeDtypeStruct((1, LANES), F32)],
        compiler_params=_cparams(("arbitrary",)),
    )(hf, hf, hf, rope, g_q, g_kv, w_uq, w_ukv, dqm, dkt, dvt)


def _causal_masks(bq, bk):
    row = lax.broadcasted_iota(jnp.int32, (bq, bk), 0)
    col = lax.broadcasted_iota(jnp.int32, (bq, bk), 1)
    return row, col


def _mla_fwd(qm, km, vt):
    T = qm.shape[0]
    bq, bk = min(MLA_BQ, T), ATT_BLK
    nq, nsub, nk = T // bq, bq // bk, T // bk

    def body(q_ref, k_ref, vt_ref, o_ref, lse_ref, acc_ref, m_ref, l_ref):
        qi = pl.program_id(0)
        key = lax.broadcasted_iota(jnp.int32, (bk, bq), 0)
        qry = lax.broadcasted_iota(jnp.int32, (bk, bq), 1)
        ones = jnp.ones((8, bk), BF16)
        acc_ref[...] = jnp.zeros_like(acc_ref)
        m_ref[...] = jnp.full_like(m_ref, NEG)
        l_ref[...] = jnp.zeros_like(l_ref)

        def step(kb0, masked):
            kbs = [kb0 + d for d in range(nsub)]
            sts = [[_dot_nt(k_ref[pl.ds(pl.multiple_of(kb * bk, bk), bk), e * LANES:(e + 1) * LANES],
                            q_ref[:, e * LANES:(e + 1) * LANES]) for kb in kbs] for e in range(4)]
            pts, alphas = [], []
            for e in range(4):
                st = [jnp.where(key + d * bk <= qry, sts[e][d], NEG) for d in range(nsub)] if masked else sts[e]
                m_prev = m_ref[e, 0:1, :]
                m_new = m_prev
                for d in range(nsub):
                    m_new = jnp.maximum(m_new, jnp.max(st[d], axis=0, keepdims=True))
                alpha = jnp.exp2(m_prev - m_new)
                pt = [jnp.exp2(st[d] - m_new).astype(BF16) for d in range(nsub)]
                l_new = alpha * l_ref[e]
                for d in range(nsub):
                    l_new = l_new + _dot(ones, pt[d])
                l_ref[e] = l_new
                m_ref[e] = jnp.broadcast_to(m_new, (8, bq))
                pts.append(pt)
                alphas.append(alpha)
            for e in range(4):
                acc = alphas[e] * acc_ref[e]
                for d in range(nsub):
                    v_t = vt_ref[e // 2, kbs[d], (e % 2) * HEAD:(e % 2 + 1) * HEAD, :]
                    acc = acc + _dot(v_t, pts[e][d])
                acc_ref[e] = acc

        step(qi * nsub, True)

        def loop(t, c):
            step(t * nsub, False)
            return c

        lax.fori_loop(0, qi, loop, 0)
        outs, lses = [], []
        for e in range(4):
            l = l_ref[e, 0:1, :]
            outs.append(acc_ref[e] / l)
            lses.append(jnp.broadcast_to(m_ref[e, 0:1, :] * LN2 + jnp.log(l), (HEAD, bq)))
        o_ref[...] = jnp.transpose(jnp.concatenate(outs, axis=0))
        lse_ref[...] = jnp.transpose(jnp.concatenate(lses, axis=0))

    return pl.pallas_call(
        body, name="mla_fwd", grid=(nq,),
        in_specs=[pl.BlockSpec((bq, 512), lambda i: (i, 0)),
                  pl.BlockSpec((T, 512), lambda i: (0, 0)),
                  pl.BlockSpec((2, nk, LANES, bk), lambda i: (0, 0, 0, 0))],
        out_specs=[pl.BlockSpec((bq, 256), lambda i: (i, 0)),
                   pl.BlockSpec((bq, 256), lambda i: (i, 0))],
        out_shape=[jax.ShapeDtypeStruct((T, 256), F32), jax.ShapeDtypeStruct((T, 256), F32)],
        scratch_shapes=[pltpu.VMEM((4, HEAD, bq), F32), pltpu.VMEM((4, 8, bq), F32), pltpu.VMEM((4, 8, bq), F32)],
        compiler_params=_cparams(("arbitrary",)),
    )(qm, km, vt)


def _mla_bwd(qm, km, vm, y, lse, dy):
    T = qm.shape[0]
    bq, bk = min(MLA_BQ, T), ATT_BLK
    nq, nsub, nk = T // bq, bq // bk, T // bk

    def body(q_ref, k_ref, v_ref, y_ref, lse_ref, dy_ref, dq_ref, dkt_ref, dvt_ref, dob_ref, st_ref, qt_ref, dot_ref):
        qi = pl.program_id(1)

        @pl.when(qi == 0)
        def _():
            dkt_ref[...] = jnp.zeros_like(dkt_ref)
            dvt_ref[...] = jnp.zeros_like(dvt_ref)

        lane = lax.broadcasted_iota(jnp.int32, (1, LANES), 1) // HEAD
        row, col = _causal_masks(bq, bk)
        dq_ref[...] = jnp.zeros_like(dq_ref)
        lse = lse_ref[...]
        lse_other = pltpu.roll(lse, HEAD, 1)
        qt_ref[...] = jnp.transpose(q_ref[...].astype(F32)).astype(BF16)
        dot_ref[...] = jnp.transpose(dy_ref[...]).astype(BF16)
        for e in range(2):
            do = jnp.where(lane == e, dy_ref[...], 0.0)
            dob_ref[e] = do.astype(BF16)
            st_ref[2 * e] = jnp.where(lane == e, lse, lse_other) * LOG2E
            st_ref[2 * e + 1] = jnp.broadcast_to(jnp.sum(do * y_ref[...], axis=1, keepdims=True), (bq, LANES))

        hss = [slice(e * LANES, (e + 1) * LANES) for e in range(2)]
        tile = lambda a: jnp.concatenate([a] * (bk // LANES), axis=1)

        def step(kb0, masked):
            kbs = [kb0 + d for d in range(nsub)]
            rows = [pl.ds(pl.multiple_of(kb * bk, bk), bk) for kb in kbs]
            pairs = [(d, e) for d in range(nsub) for e in range(2)]
            ss = {(d, e): _dot_nt(q_ref[:, hss[e]], k_ref[rows[d], hss[e]]) for d, e in pairs}
            dps = {(d, e): _dot_nt(dob_ref[e], jnp.where(lane == e, v_ref[rows[d], :], 0)) for d, e in pairs}
            ps, dss = {}, {}
            for d, e in pairs:
                s = jnp.where(col + d * bk <= row, ss[d, e], NEG) if masked else ss[d, e]
                p = jnp.exp2(s - tile(st_ref[2 * e]))
                dss[d, e] = (p * (dps[d, e] - tile(st_ref[2 * e + 1]))).astype(BF16)
                ps[d, e] = p.astype(BF16)
            for d, e in pairs:
                dvt_ref[0, kbs[d], e * HEAD:(e + 1) * HEAD, :] += _dot(dot_ref[e * HEAD:(e + 1) * HEAD, :], ps[d, e])
            for d, e in pairs:
                dkt_ref[0, kbs[d], hss[e], :] += _dot(qt_ref[hss[e], :], dss[d, e])
            for e in range(2):
                dq = dq_ref[:, hss[e]]
                for d in range(nsub):
                    dq = dq + _dot(dss[d, e], k_ref[rows[d], hss[e]])
                dq_ref[:, hss[e]] = dq

        step(qi * nsub, True)

        def loop(t, c):
            step(t * nsub, False)
            return c

        lax.fori_loop(0, qi, loop, 0)
        dq_ref[...] *= LN2

    return pl.pallas_call(
        body, name="mla_bwd", grid=(2, nq),
        in_specs=[pl.BlockSpec((bq, 256), lambda j, i: (i, j)),
                  pl.BlockSpec((T, 256), lambda j, i: (0, j)),
                  pl.BlockSpec((T, LANES), lambda j, i: (0, j)),
                  pl.BlockSpec((bq, LANES), lambda j, i: (i, j)),
                  pl.BlockSpec((bq, LANES), lambda j, i: (i, j)),
                  pl.BlockSpec((bq, LANES), lambda j, i: (i, j))],
        out_specs=[pl.BlockSpec((bq, 256), lambda j, i: (i, j)),
                   pl.BlockSpec((1, nk, 256, bk), lambda j, i: (j, 0, 0, 0)),
                   pl.BlockSpec((1, nk, LANES, bk), lambda j, i: (j, 0, 0, 0))],
        out_shape=[jax.ShapeDtypeStruct((T, 512), F32),
                   jax.ShapeDtypeStruct((2, nk, 256, bk), F32),
                   jax.ShapeDtypeStruct((2, nk, LANES, bk), F32)],
        scratch_shapes=[pltpu.VMEM((2, bq, LANES), BF16), pltpu.VMEM((4, bq, LANES), F32),
                        pltpu.VMEM((256, bq), BF16), pltpu.VMEM((LANES, bq), BF16)],
        compiler_params=_cparams(("parallel", "arbitrary")),
    )(qm, km, vm, y, lse, dy)


def _suffix_ones(n):
    r = lax.broadcasted_iota(jnp.int32, (n, n), 0)
    c = lax.broadcasted_iota(jnp.int32, (n, n), 1)
    return (r >= c).astype(BF16)


def _prefix_ones(n):
    r = lax.broadcasted_iota(jnp.int32, (n, n), 0)
    c = lax.broadcasted_iota(jnp.int32, (n, n), 1)
    return (r <= c).astype(BF16)


def _sb_specs(T, bq):
    qo, ko, vo = (_INT_OFF[n] // 256 for n in ("d_q", "d_k", "d_v"))
    return [pl.BlockSpec((bq, 256), lambda i: (i, qo)),
            pl.BlockSpec((T, 256), lambda i: (0, ko)),
            pl.BlockSpec((T, 256), lambda i: (0, vo))]


def _sb_fwd(hb):
    T = hb.shape[0]
    bq = bk = ATT_BLK
    nq = T // bq

    def body(q_ref, k_ref, v_ref, o_ref, tot_ref, cnt_ref, qm_ref, car_ref):
        qi = pl.program_id(0)
        lane = lax.broadcasted_iota(jnp.int32, (1, LANES), 1) // HEAD
        row, col = _causal_masks(bq, bk)
        strict = col < row
        u = _suffix_ones(bk)
        o_ref[...] = jnp.zeros_like(o_ref)
        car_ref[...] = jnp.zeros_like(car_ref)
        pair = lambda h: slice((h // 2) * LANES, (h // 2 + 1) * LANES)
        for h in range(4):
            qm_ref[h] = jnp.where(lane == h % 2, q_ref[:, pair(h)], 0) * 0.125

        def step(blocks):
            tile = lambda a: jnp.concatenate([a] * (bk // LANES), axis=1)
            rows = [pl.ds(pl.multiple_of(kb * bk, bk), bk) for kb, _ in blocks]
            pairs = [(b, h) for b in range(len(blocks)) for h in range(4)]
            zs = {(b, h): _dot_nt(qm_ref[h], k_ref[rows[b], pair(h)]) for b, h in pairs}
            splits = {}
            for b, h in pairs:
                z = zs[b, h]
                lk = jnp.minimum(-z, 0.0) - jnp.log(1.0 + jnp.exp(-jnp.abs(z)))
                if blocks[b][1] is not None:
                    lk = jnp.where(blocks[b][1], lk, 0.0)
                splits[b, h] = _split(lk)
            sufs = {bh: _dot(hi, u) + _dot(lo, u) for bh, (hi, lo) in splits.items()}
            car = [car_ref[h] for h in range(4)]
            aas = {}
            for b, h in pairs:
                a = jnp.exp(zs[b, h] + sufs[b, h] + tile(car[h]))
                if blocks[b][1] is not None:
                    a = jnp.where(blocks[b][1], a, 0.0)
                aas[b, h] = a.astype(BF16)
                car[h] = car[h] + jnp.broadcast_to(sufs[b, h][:, 0:1], (bq, LANES))
            acc = [o_ref[:, pair(0)], o_ref[:, pair(2)]]
            for b, h in pairs:
                acc[h // 2] = acc[h // 2] + _dot(aas[b, h], jnp.where(lane == h % 2, v_ref[rows[b], pair(h)], 0))
            o_ref[:, pair(0)], o_ref[:, pair(2)] = acc
            for h in range(4):
                car_ref[h] = car[h]

        step([(qi, strict), (jnp.maximum(qi - 1, 0), qi > 0)])

        def live():
            worst = jnp.maximum(jnp.maximum(car_ref[0], car_ref[1]), jnp.maximum(car_ref[2], car_ref[3]))
            return jnp.max(worst) >= SB_DEAD

        def cond(c):
            return jnp.logical_and(c[0] < qi, c[1])

        def loop(c):
            step([(qi - 1 - c[0], None)])
            return c[0] + 1, live()

        done, _ = lax.while_loop(cond, loop, (jnp.minimum(qi, 1), live()))
        tot_ref[:, pair(0)] = jnp.where(lane == 0, car_ref[0], car_ref[1])
        tot_ref[:, pair(2)] = jnp.where(lane == 0, car_ref[2], car_ref[3])
        cnt_ref[0, qi] = done.astype(F32)

    return pl.pallas_call(
        body, name="sb_fwd", grid=(nq,), in_specs=_sb_specs(T, bq),
        out_specs=[pl.BlockSpec((bq, 256), lambda i: (i, 0)), pl.BlockSpec((bq, 256), lambda i: (i, 0)),
                   pl.BlockSpec(memory_space=pltpu.SMEM)],
        out_shape=[jax.ShapeDtypeStruct((T, 256), F32), jax.ShapeDtypeStruct((T, 256), F32),
                   jax.ShapeDtypeStruct((1, nq), F32)],
        scratch_shapes=[pltpu.VMEM((4, bq, LANES), BF16), pltpu.VMEM((4, bq, LANES), F32)],
        compiler_params=_cparams(("arbitrary",)),
    )(hb, hb, hb)


def _sb_bwd(hb, tot, cnt, dy):
    T = hb.shape[0]
    bq = bk = ATT_BLK
    nq = T // bq

    def body(q_ref, k_ref, v_ref, tot_ref, dy_ref, cnt_ref, dq_ref, dk_ref, dv_ref, qm_ref, dob_ref, dqa_ref, rem_ref,
             cg_ref):
        qi = pl.program_id(0)

        @pl.when(qi == 0)
        def _():
            dk_ref[...] = jnp.zeros_like(dk_ref)
            dv_ref[...] = jnp.zeros_like(dv_ref)

        lane = lax.broadcasted_iota(jnp.int32, (1, LANES), 1) // HEAD
        row, col = _causal_masks(bq, bk)
        strict = col < row
        u = _prefix_ones(bk)
        pair = lambda h: slice((h // 2) * LANES, (h // 2 + 1) * LANES)
        dqa_ref[...] = jnp.zeros_like(dqa_ref)
        cg_ref[...] = jnp.zeros_like(cg_ref)
        for h in range(4):
            tot = tot_ref[:, pair(h)]
            qm_ref[h] = jnp.where(lane == h % 2, q_ref[:, pair(h)], 0) * 0.125
            dob_ref[h] = jnp.where(lane == h % 2, dy_ref[:, pair(h)], 0.0).astype(BF16)
            rem_ref[h] = jnp.where(lane == h % 2, tot, pltpu.roll(tot, HEAD, 1))

        def step(blocks):
            tile = lambda a: jnp.concatenate([a] * (bk // LANES), axis=1)
            nb = len(blocks)
            rows = [pl.ds(pl.multiple_of(kb * bk, bk), bk) for kb, _ in blocks]
            pairs = [(b, h) for b in range(nb) for h in range(4)]
            mask = lambda b, x: x if blocks[b][1] is None else jnp.where(blocks[b][1], x, 0.0)
            zs = {(b, h): _dot_nt(qm_ref[h], k_ref[rows[b], pair(h)]) for b, h in pairs}
            das = {(b, h): _dot_nt(dob_ref[h], jnp.where(lane == h % 2, v_ref[rows[b], pair(h)], 0)) for b, h in pairs}
            zls, splits = {}, {}
            for b, h in pairs:
                z = zs[b, h]
                lk = mask(b, jnp.minimum(-z, 0.0) - jnp.log(1.0 + jnp.exp(-jnp.abs(z))))
                zls[b, h] = z + lk
                splits[b, h] = _split(lk)
            pres = {bh: _dot(hi, u) + _dot(lo, u) for bh, (hi, lo) in splits.items()}
            rem = [rem_ref[h] for h in range(4)]
            aas, gs, gsplits = {}, {}, {}
            for b, h in pairs:
                a = mask(b, jnp.exp(zls[b, h] + (tile(rem[h]) - pres[b, h])))
                gs[b, h] = a * das[b, h]
                aas[b, h] = a.astype(BF16)
                gsplits[b, h] = _split(gs[b, h])
                rem[h] = rem[h] - jnp.broadcast_to(pres[b, h][:, bk - 1:bk], (bq, LANES))
            for b in range(nb):
                for p in (0, 2):
                    dv_ref[rows[b], pair(p)] += _dot_tn(aas[b, p], dob_ref[p]) + _dot_tn(aas[b, p + 1], dob_ref[p + 1])
            gpres = {bh: _dot(hi, u) + _dot(lo, u) for bh, (hi, lo) in gsplits.items()}
            cg = [cg_ref[h] for h in range(4)]
            dzs = {}
            for b, h in pairs:
                dz = mask(b, gs[b, h] - jnp.exp(zls[b, h]) * (tile(cg[h]) + gpres[b, h]))
                dzs[b, h] = dz.astype(BF16)
                cg[h] = cg[h] + jnp.broadcast_to(gpres[b, h][:, bk - 1:bk], (bq, LANES))
            for b in range(nb):
                for p in (0, 2):
                    dk_ref[rows[b], pair(p)] += _dot_tn(dzs[b, p], qm_ref[p]) + _dot_tn(dzs[b, p + 1], qm_ref[p + 1])
            for h in range(4):
                dq = dqa_ref[h]
                for b in range(nb):
                    dq = dq + _dot(dzs[b, h], k_ref[rows[b], pair(h)])
                dqa_ref[h] = dq
                rem_ref[h] = rem[h]
                cg_ref[h] = cg[h]

        def loop(kb, c):
            step([(kb, None)])
            return c

        start = qi - jnp.clip(cnt_ref[0, qi].astype(jnp.int32), 0, qi)
        lax.fori_loop(start, qi - 1, loop, 0)
        step([(jnp.maximum(qi - 1, 0), qi > 0), (qi, strict)])
        for p in (0, 2):
            dq_ref[:, pair(p)] = (jnp.where(lane == 0, dqa_ref[p], dqa_ref[p + 1]) * 0.125).astype(BF16)

    return pl.pallas_call(
        body, name="sb_bwd", grid=(nq,),
        in_specs=_sb_specs(T, bq) + [pl.BlockSpec((bq, 256), lambda i: (i, 0)),
                                     pl.BlockSpec((bq, 256), lambda i: (i, 0)),
                                     pl.BlockSpec(memory_space=pltpu.SMEM)],
        out_specs=[pl.BlockSpec((bq, 256), lambda i: (i, 0)),
                   pl.BlockSpec((T, 256), lambda i: (0, 0)),
                   pl.BlockSpec((T, 256), lambda i: (0, 0))],
        out_shape=[jax.ShapeDtypeStruct((T, 256), BF16)] + [jax.ShapeDtypeStruct((T, 256), F32)] * 2,
        scratch_shapes=[pltpu.VMEM((4, bq, LANES), BF16), pltpu.VMEM((4, bq, LANES), BF16),
                        pltpu.VMEM((4, bq, LANES), F32), pltpu.VMEM((4, bq, LANES), F32),
                        pltpu.VMEM((4, bq, LANES), F32)],
        compiler_params=_cparams(("arbitrary",)),
    )(hb, hb, hb, tot, dy, cnt)


EP_TM = 512


def _ep_in_specs(tm, rev):
    idx = (lambda i: rev - i) if rev is not None else (lambda i: i)
    bo = (_INT_OFF["b_b"] - N_HB) // 256
    halo = lambda i: jnp.maximum(idx(i) * (tm // 8) - 1, 0)
    return [pl.BlockSpec((tm, 256), lambda i: (idx(i), 0)),
            pl.BlockSpec((tm, 256), lambda i: (idx(i), 0)),
            pl.BlockSpec((tm, 256), lambda i: (idx(i), 0)),
            pl.BlockSpec((tm, D_MODEL), lambda i: (idx(i), 0)),
            pl.BlockSpec((tm, 256), lambda i: (idx(i), bo)),
            pl.BlockSpec((tm, 256), lambda i: (idx(i), bo + 1)),
            pl.BlockSpec((tm, 256), lambda i: (idx(i), bo + 2)),
            pl.BlockSpec((8, 256), lambda i: (halo(i), bo + 1)),
            pl.BlockSpec((8, 256), lambda i: (halo(i), bo + 2)),
            pl.BlockSpec((3, 256), lambda i: (0, 0)),
            pl.BlockSpec((1, 256), lambda i: (0, 0)),
            pl.BlockSpec((1, D_MODEL), lambda i: (0, 0)),
            pl.BlockSpec((D_MODEL, D_MODEL), lambda i: (0, 0)),
            pl.BlockSpec((1, D_MODEL), lambda i: (0, 0))]


def _ep_mix(first, ya_ref, yc_ref, yd_ref, gate_ref, bb_ref, bc_ref, bx_ref, hc_ref, hx_ref, cw_ref, cb_ref, gg_ref):
    tm = ya_ref.shape[0]
    u = bc_ref[...] * bx_ref[...]
    halo = jnp.where(first, 0.0, hc_ref[...] * hx_ref[...])
    row = lax.broadcasted_iota(jnp.int32, (tm, 1), 0)
    u1 = jnp.where(row == 0, halo[7:8, :], pltpu.roll(u, 1, 0))
    u2 = jnp.where(row == 0, halo[6:7, :], jnp.where(row == 1, halo[7:8, :], pltpu.roll(u, 2, 0)))
    cw = cw_ref[...]
    conv = cw[0:1, :] * u2 + cw[1:2, :] * u1 + cw[2:3, :] * u + cb_ref[...]
    bb = bb_ref[...]
    ys = [ya_ref[...], bb * conv, yc_ref[...], yd_ref[...]]
    rs = [_rms(y) for y in ys]
    gg = gg_ref[...]
    yhat = jnp.concatenate([y * r for y, r in zip(ys, rs)], axis=1)
    gate = gate_ref[...]
    sig = 1.0 / (1.0 + jnp.exp(-gate))
    return u, u1, u2, conv, bb, rs, yhat, yhat * gg, gate, sig


def _epilogue_fwd(x, ya, yc, yd, hf, conv_w, conv_b, g_grp, w_out, g_post, tgt=None):
    T = x.shape[0]
    tm = EP_TM
    row_spec = pl.BlockSpec((tm, D_MODEL), lambda i: (i, 0))

    def layer_out(refs):
        (x_ref, ya_ref, yc_ref, yd_ref, gate_ref, bb_ref, bc_ref, bx_ref, hc_ref, hx_ref, cw_ref, cb_ref,
         gg_ref, wo_ref, gp_ref) = refs
        (_, _, _, _, _, _, _, yn, gate, sig) = _ep_mix(
            pl.program_id(0) == 0, ya_ref, yc_ref, yd_ref, gate_ref, bb_ref, bc_ref, bx_ref, hc_ref, hx_ref,
            cw_ref, cb_ref, gg_ref)
        z = _dot((yn * (gate * sig)).astype(BF16), wo_ref[...])
        return x_ref[...] + z * _rms(z) * gp_ref[...]

    args = (x, ya, yc, yd, hf, hf, hf, hf, hf, hf, conv_w, conv_b, g_grp, w_out, g_post)
    in_specs = [row_spec] + _ep_in_specs(tm, None)
    if tgt is None:
        def body(*refs):
            refs[-1][...] = layer_out(refs[:-1])

        return pl.pallas_call(
            body, name="epilogue_fwd", grid=(T // tm,), in_specs=in_specs, out_specs=row_spec,
            out_shape=jax.ShapeDtypeStruct((T, D_MODEL), F32), compiler_params=_cparams(("parallel",)),
        )(*args)

    def body_loss(*refs):
        t_ref, dy_ref, l_ref = refs[-3:]

        @pl.when(pl.program_id(0) == 0)
        def _():
            l_ref[...] = jnp.zeros_like(l_ref)

        d = layer_out(refs[:-3]) - t_ref[...]
        dy_ref[...] = d * (1.0 / D_MODEL)
        part = jnp.sum(jnp.sum(d * d, axis=1, keepdims=True), axis=0, keepdims=True)
        l_ref[...] += part * (0.5 / D_MODEL)

    return pl.pallas_call(
        body_loss, name="epilogue_fwd_loss", grid=(T // tm,), in_specs=in_specs + [row_spec],
        out_specs=[row_spec, pl.BlockSpec((8, LANES), lambda i: (0, 0))],
        out_shape=[jax.ShapeDtypeStruct((T, D_MODEL), F32), jax.ShapeDtypeStruct((8, LANES), F32)],
        compiler_params=_cparams(("arbitrary",)),
    )(*args, tgt)


def _epilogue_bwd(dxn, ya, yc, yd, hf, conv_w, conv_b, g_grp, w_out, g_post, exchange=()):
    T = dxn.shape[0]
    tm = EP_TM
    nt = T // tm
    nx = len(exchange)
    ridx = lambda i: (nt - 1 - i, 0)

    def body(dx_ref, ya_ref, yc_ref, yd_ref, gate_ref, bb_ref, bc_ref, bx_ref, hc_ref, hx_ref, cw_ref, cb_ref,
             gg_ref, wo_ref, gp_ref,
             dya_ref, dyc_ref, dyd_ref, dhf_ref, dwo_ref, dgp_ref, dgg_ref, dcw_ref, dcb_ref, carry_ref):
        i = pl.program_id(0)

        @pl.when(i == 0)
        def _():
            for r in (dwo_ref, dgp_ref, dgg_ref, dcw_ref, dcb_ref, carry_ref):
                r[...] = jnp.zeros_like(r)

        (u, u1, u2, conv, bb, rs, yhat, yn, gate, sig) = _ep_mix(
            i == nt - 1, ya_ref, yc_ref, yd_ref, gate_ref, bb_ref, bc_ref, bx_ref, hc_ref, hx_ref,
            cw_ref, cb_ref, gg_ref)
        silu = gate * sig
        ymix = (yn * silu).astype(BF16)
        z = _dot(ymix, wo_ref[...])
        rz = _rms(z)
        dz, dgrow = _rms_bwd(dx_ref[...], z * rz, rz, gp_ref[...])
        dgp_ref[...] += _colsum(dgrow)
        dzb = dz.astype(BF16)
        dwo_ref[...] += _dot_tn(ymix, dzb)
        dymix = _dot_nt(dzb, wo_ref[...])
        dhf_ref[:, 0:D_MODEL] = (dymix * yn * (sig * (1.0 + gate * (1.0 - sig)))).astype(BF16)
        dyn = dymix * silu
        dgg_ref[...] += _colsum(dyn * yhat)
        gg = gg_ref[...]
        dys = []
        for gi in range(4):
            sl = slice(gi * GROUP, (gi + 1) * GROUP)
            dyh = dyn[:, sl] * gg[:, sl]
            yh = yhat[:, sl]
            dys.append(rs[gi] * (dyh - yh * jnp.mean(dyh * yh, axis=-1, keepdims=True)))
        dya_ref[...] = dys[0]
        dyc_ref[...] = dys[2]
        dyd_ref[...] = dys[3]
        dyb = dys[1]
        dhf_ref[:, D_MODEL:D_MODEL + 256] = (dyb * conv).astype(BF16)
        dconv = dyb * bb
        dcb_ref[...] += _colsum(dconv)
        dcw_ref[0:1, :] += _colsum(dconv * u2)
        dcw_ref[1:2, :] += _colsum(dconv * u1)
        dcw_ref[2:3, :] += _colsum(dconv * u)
        carry = carry_ref[...]
        row = lax.broadcasted_iota(jnp.int32, (tm, 1), 0)
        d1 = jnp.where(row == tm - 1, carry[0:1, :], pltpu.roll(dconv, tm - 1, 0))
        d2 = jnp.where(row == tm - 2, carry[0:1, :],
                       jnp.where(row == tm - 1, carry[1:2, :], pltpu.roll(dconv, tm - 2, 0)))
        cw = cw_ref[...]
        du = cw[2:3, :] * dconv + cw[1:2, :] * d1 + cw[0:1, :] * d2
        dhf_ref[:, D_MODEL + 256:D_MODEL + 512] = (du * bx_ref[...]).astype(BF16)
        dhf_ref[:, D_MODEL + 512:D_MODEL + 768] = (du * bc_ref[...]).astype(BF16)
        carry_ref[...] = dconv[0:8, :]

    n_in, n_out = 15, 9

    def body_with_exchange(*refs):
        ins, ex_in = refs[:n_in], refs[n_in:n_in + nx]
        outs, ex_out = refs[n_in + nx:n_in + nx + n_out], refs[n_in + nx + n_out:n_in + 2 * nx + n_out]
        carry_ref, sems = refs[n_in + 2 * nx + n_out], refs[n_in + 2 * nx + n_out + 1:]
        start, finish = _exchange_plan(ex_in, ex_out, sems)
        pl.when(pl.program_id(0) == 0)(start)
        body(*ins, *outs, carry_ref)
        pl.when(pl.program_id(0) == nt - 1)(finish)

    in_specs = [pl.BlockSpec((tm, D_MODEL), ridx)] + _ep_in_specs(tm, nt - 1) + [HBM] * nx
    return pl.pallas_call(
        body_with_exchange if nx else body, name="epilogue_bwd_exchange" if nx else "epilogue_bwd", grid=(nt,),
        in_specs=in_specs,
        out_specs=[pl.BlockSpec((tm, 256), ridx), pl.BlockSpec((tm, 256), ridx), pl.BlockSpec((tm, 256), ridx),
                   pl.BlockSpec((tm, D_MODEL + 768), ridx),
                   pl.BlockSpec((D_MODEL, D_MODEL), lambda i: (0, 0)),
                   pl.BlockSpec((1, D_MODEL), lambda i: (0, 0)),
                   pl.BlockSpec((1, D_MODEL), lambda i: (0, 0)),
                   pl.BlockSpec((8, 256), lambda i: (0, 0)),
                   pl.BlockSpec((1, 256), lambda i: (0, 0))] + [HBM] * nx,
        out_shape=[jax.ShapeDtypeStruct((T, 256), F32)] * 3
                  + [jax.ShapeDtypeStruct((T, D_MODEL + 768), BF16),
                     jax.ShapeDtypeStruct((D_MODEL, D_MODEL), F32),
                     jax.ShapeDtypeStruct((1, D_MODEL), F32),
                     jax.ShapeDtypeStruct((1, D_MODEL), F32),
                     jax.ShapeDtypeStruct((8, 256), F32),
                     jax.ShapeDtypeStruct((1, 256), F32)]
                  + [jax.ShapeDtypeStruct(p.shape, p.dtype) for p in exchange],
        scratch_shapes=[pltpu.VMEM((8, 256), F32)] + (_exchange_sems(nx) if nx else []),
        compiler_params=_cparams(("arbitrary",)),
    )(dxn, ya, yc, yd, hf, hf, hf, hf, hf, hf, conv_w, conv_b, g_grp, w_out, g_post, *exchange)


def _place():
    return lax.axis_index("x"), lax.axis_index("y"), lax.axis_index("c")


def _other_chips(x, y):
    return [(1 - x, y), (x, 1 - y), (1 - x, 1 - y)]


HBM = pl.BlockSpec(memory_space=pl.ANY)


def _gather_plan(ins, outs, sems):
    n = len(ins)
    ici_send, ici_recv, d2d_send, d2d_recv, local_sems = sems
    x, y, c = _place()
    me = 2 * x + y
    chips = _other_chips(x, y)

    def ici(a, j, chip_from):
        px, py = chips[j]
        return pltpu.make_async_remote_copy(
            src_ref=ins[a].at[c], dst_ref=outs[a].at[chip_from, c], send_sem=ici_send.at[3 * a + j],
            recv_sem=ici_recv.at[3 * a + j], device_id=(px, py, c), device_id_type=MESH)

    def d2d(a, j, part):
        px, py = chips[j]
        blk = outs[a].at[2 * px + py, part]
        return pltpu.make_async_remote_copy(
            src_ref=blk, dst_ref=blk, send_sem=d2d_send.at[3 * a + j], recv_sem=d2d_recv.at[3 * a + j],
            device_id=(x, y, 1 - c), device_id_type=MESH)

    def local(a):
        return pltpu.make_async_copy(ins[a], outs[a].at[me], local_sems.at[a])

    hops = [(j, a) for j in range(3) for a in range(n)]

    def start():
        for a in range(n):
            local(a).start()
        for j, a in hops:
            ici(a, j, me).start()

    def finish():
        for j, a in hops:
            ici(a, j, 2 * chips[j][0] + chips[j][1]).wait_recv()
            d2d(a, j, c).start()
        for j, a in hops:
            d2d(a, j, 1 - c).wait_recv()
        for j, a in hops:
            ici(a, j, me).wait_send()
            d2d(a, j, c).wait_send()
        for a in range(n):
            local(a).wait()

    return start, finish


def _gather_sems(n):
    return [pltpu.SemaphoreType.DMA((3 * n,))] * 4 + [pltpu.SemaphoreType.DMA((n,))]


def _gather_weights(shards):
    n = len(shards)

    def body(*refs):
        start, finish = _gather_plan(refs[:n], refs[n:2 * n], refs[2 * n:])
        start()
        finish()

    return pl.pallas_call(
        body, name="gather_weights",
        in_specs=[HBM] * n, out_specs=[HBM] * n,
        out_shape=[jax.ShapeDtypeStruct((4,) + s.shape, s.dtype) for s in shards],
        scratch_shapes=_gather_sems(n),
    )(*shards)


def _exchange_plan(ins, outs, sems, small=None):
    n = len(ins)
    send_sems, recv_sems, ssend_sems, srecv_sems, local_sems = sems
    x, y, c = _place()
    me = 2 * x + y
    dev = 4 * x + 2 * y + c
    chips = _other_chips(x, y)
    hops = [(j, a) for j in range(3) for a in range(n)]
    flips = [(fx, fy, fc) for fx in (0, 1) for fy in (0, 1) for fc in (0, 1)][1:] if small is not None else []

    def chunk(a, j, chip_from):
        px, py = chips[j]
        return pltpu.make_async_remote_copy(
            src_ref=ins[a].at[2 * px + py], dst_ref=outs[a].at[chip_from], send_sem=send_sems.at[3 * a + j],
            recv_sem=recv_sems.at[3 * a + j], device_id=(px, py, c), device_id_type=MESH)

    def to_all(j, dev_from):
        fx, fy, fc = flips[j]
        return pltpu.make_async_remote_copy(
            src_ref=small[0], dst_ref=small[1].at[dev_from], send_sem=ssend_sems.at[j], recv_sem=srecv_sems.at[j],
            device_id=(x ^ fx, y ^ fy, c ^ fc), device_id_type=MESH)

    def local():
        own = [pltpu.make_async_copy(ins[a].at[me], outs[a].at[me], local_sems.at[a]) for a in range(n)]
        return own + ([pltpu.make_async_copy(small[0], small[1].at[dev], local_sems.at[n])] if flips else [])

    def start():
        for cp in local():
            cp.start()
        for j, a in hops:
            chunk(a, j, me).start()
        for j in range(len(flips)):
            to_all(j, dev).start()

    def finish():
        for j, a in hops:
            chunk(a, j, 2 * chips[j][0] + chips[j][1]).wait_recv()
        for j, (fx, fy, fc) in enumerate(flips):
            to_all(j, 4 * (x ^ fx) + 2 * (y ^ fy) + (c ^ fc)).wait_recv()
        for j, a in hops:
            chunk(a, j, me).wait_send()
        for j in range(len(flips)):
            to_all(j, dev).wait_send()
        for cp in local():
            cp.wait()

    return start, finish


def _exchange_sems(n):
    return [pltpu.SemaphoreType.DMA((3 * n,)), pltpu.SemaphoreType.DMA((3 * n,)),
            pltpu.SemaphoreType.DMA((7,)), pltpu.SemaphoreType.DMA((7,)), pltpu.SemaphoreType.DMA((n + 1,))]


def _exchange_chips(parts, small):
    n = len(parts)

    def body(*refs):
        start, finish = _exchange_plan(refs[:n], refs[n + 1:2 * n + 1], refs[2 * n + 2:], (refs[n], refs[2 * n + 1]))
        start()
        finish()

    return pl.pallas_call(
        body, name="exchange_chips",
        in_specs=[HBM] * (n + 1), out_specs=[HBM] * (n + 1),
        out_shape=[jax.ShapeDtypeStruct(p.shape, p.dtype) for p in parts]
                  + [jax.ShapeDtypeStruct((8,) + small.shape, small.dtype)],
        scratch_shapes=_exchange_sems(n),
    )(*parts, small)


def _swap_cores(parts, name):
    n = len(parts)

    def body(*refs):
        ins, outs, send_sems, recv_sems = refs[:n], refs[n:2 * n], refs[2 * n], refs[2 * n + 1]
        x, y, c = _place()
        copies = [pltpu.make_async_remote_copy(
            src_ref=ins[a], dst_ref=outs[a], send_sem=send_sems.at[a], recv_sem=recv_sems.at[a],
            device_id=(x, y, 1 - c), device_id_type=MESH) for a in range(n)]
        for cp in copies:
            cp.start()
        for cp in copies:
            cp.wait()

    return pl.pallas_call(
        body, name=name, in_specs=[HBM] * n, out_specs=[HBM] * n,
        out_shape=[jax.ShapeDtypeStruct(p.shape, p.dtype) for p in parts],
        scratch_shapes=[pltpu.SemaphoreType.DMA((n,)), pltpu.SemaphoreType.DMA((n,))],
    )(*parts)


def _tile(rows, cols):
    for cand in (256, 128, 64):
        if rows % cand == 0:
            return cand, cols
    if rows > 64 and cols % 256 == 0:
        return rows, 256
    return rows, cols


def _add(a, b, name):
    L, R, C = a.shape
    tr, tc = _tile(R, C)

    def body(a_ref, b_ref, o_ref):
        o_ref[...] = (a_ref[...] + b_ref[...]).astype(BF16)

    spec = pl.BlockSpec((1, tr, tc), lambda l, i, j: (l, i, j))
    return pl.pallas_call(
        body, name=name, grid=(L, R // tr, C // tc), in_specs=[spec, spec], out_specs=spec,
        out_shape=jax.ShapeDtypeStruct((L, R, C), BF16),
        compiler_params=_cparams(("parallel", "parallel", "parallel")),
    )(a, b)


def _sum_leading(buf, name):
    n, R, C = buf.shape
    tr, tc = _tile(R, C)

    def body(b_ref, o_ref):
        acc = b_ref[0].astype(F32)
        for k in range(1, n):
            acc = acc + b_ref[k].astype(F32)
        o_ref[...] = acc

    return pl.pallas_call(
        body, name=name, grid=(R // tr, C // tc),
        in_specs=[pl.BlockSpec((n, tr, tc), lambda i, j: (0, i, j))],
        out_specs=pl.BlockSpec((tr, tc), lambda i, j: (i, j)),
        out_shape=jax.ShapeDtypeStruct((R, C), F32),
        compiler_params=_cparams(("parallel", "parallel")),
    )(buf)


def _adam_update(w, g, m, v):
    c1 = 1.0 / (1.0 - ADAM_B1 ** ADAM_STEP)
    c2 = 1.0 / (1.0 - ADAM_B2 ** ADAM_STEP)
    mn = ADAM_B1 * m + (1.0 - ADAM_B1) * g
    vn = ADAM_B2 * v + (1.0 - ADAM_B2) * (g * g)
    return -ADAM_LR * ((mn * c1) / (jnp.sqrt(vn * c2) + ADAM_EPS) + ADAM_WD * w), mn, vn


def _adamw_layers(w, m, v, g, name):
    _, R, C = w.shape
    tr, tc = _tile(R, C)

    def body(w_ref, m_ref, v_ref, g_ref, d_ref, mo_ref, vo_ref):
        d_ref[...], mo_ref[...], vo_ref[...] = _adam_update(w_ref[...], g_ref[...], m_ref[...], v_ref[...])

    spec = pl.BlockSpec((1, tr, tc), lambda l, i, j: (l, i, j))
    return pl.pallas_call(
        body, name=name, grid=(2, R // tr, C // tc),
        in_specs=[spec] * 4, out_specs=[spec] * 3,
        out_shape=[jax.ShapeDtypeStruct(w.shape, F32)] * 3,
        compiler_params=_cparams(("parallel", "parallel", "parallel")),
    )(w, m, v, g)


PACK_C = 1024
_BIG = ("w_in", "w_out", "mla_w_uq", "mla_w_ukv", "conv_w")
_SMALL = ("norm_pre", "group_norm", "norm_post", "conv_b", "mla_q_norm", "mla_kv_norm", "attn_sinks")
_SMALL_W = {"norm_pre": 1024, "group_norm": 1024, "norm_post": 1024, "conv_b": 256, "mla_q_norm": 256,
            "mla_kv_norm": 128, "attn_sinks": 4}


_LOSS_AT = divmod(DEPTH * sum(_SMALL_W.values()), PACK_C)


def _pack_small(d, loss):
    flat = jnp.concatenate([d[n].reshape(-1) for n in _SMALL] + [loss.reshape(1)])
    return jnp.pad(flat, (0, 8 * PACK_C - flat.shape[0])).reshape(8, PACK_C)


def _adamw_small(w, m, v, got):
    ns = len(_SMALL)

    def body(*refs):
        got_ref = refs[3 * ns]
        outs = refs[3 * ns + 1:]
        gsum = got_ref[0]
        for d in range(1, 8):
            gsum = gsum + got_ref[d]
        outs[4 * ns][...] = gsum[_LOSS_AT[0]:_LOSS_AT[0] + 1, _LOSS_AT[1]:_LOSS_AT[1] + 1]
        off = 0
        for i, name in enumerate(_SMALL):
            wd = _SMALL_W[name]
            rows = []
            for l in range(DEPTH):
                r, c0 = divmod(off + l * wd, PACK_C)
                rows.append(gsum[r:r + 1, c0:c0 + wd])
            off += DEPTH * wd
            g = jnp.concatenate(rows, axis=0)
            delta, mn, vn = _adam_update(refs[i][...], g, refs[ns + i][...], refs[2 * ns + i][...])
            outs[i][...] = g
            outs[ns + i][...] = delta
            outs[2 * ns + i][...] = mn
            outs[3 * ns + i][...] = vn

    shapes = [jax.ShapeDtypeStruct(w[n].shape, F32) for n in _SMALL]
    res = pl.pallas_call(body, name="adamw_small", out_shape=shapes * 4 + [jax.ShapeDtypeStruct((1, 1), F32)])(
        *[w[n] for n in _SMALL], *[m[n] for n in _SMALL], *[v[n] for n in _SMALL], got)
    return [dict(zip(_SMALL, res[k * ns:(k + 1) * ns])) for k in range(4)], res[4 * ns]


def _w_in_internal(wt):
    rows = []
    for n in _INT_ORDER:
        o, wd = _REAL_OFF[n]
        rows.append(wt[o:o + wd])
        if _INT_W[n] != wd:
            rows.append(jnp.zeros((_INT_W[n] - wd, wt.shape[1]), wt.dtype))
    return jnp.concatenate(rows, axis=0)


def _w_in_real(dwt):
    return jnp.concatenate([dwt[_INT_OFF[n]:_INT_OFF[n] + wd] for n, wd in _REAL], axis=0)


def _uq_internal(w):
    return jnp.pad(w.reshape(256, 4, 96), ((0, 0), (0, 0), (0, 32))).reshape(256, 512)


def _uq_real(dw):
    return dw.reshape(256, 4, 128)[:, :, :96].reshape(256, 384)


def _ukv_internal(w):
    w4 = w.reshape(128, 4, 128)
    k = jnp.pad(w4[:, :, :64], ((0, 0), (0, 0), (0, 64))).reshape(128, 512)
    return jnp.concatenate([k, w4[:, :, 64:].reshape(128, 256)], axis=1)


def _ukv_real(dw):
    k = dw[:, :512].reshape(128, 4, 128)[:, :, :64]
    v = dw[:, 512:].reshape(128, 4, 64)
    return jnp.concatenate([k, v], axis=2).reshape(128, 512)


def _layer_fwd(x, rope, p, tgt=None, fetch=()):
    xn, hb, hf, *fetched = _inproj_fwd(x, p["norm_pre"], p["w_in"], fetch)
    ya = _swa_fwd(hb, p["attn_sinks"])
    qm, km, vm, vt = _mla_prep_fwd(hf, rope, p["mla_q_norm"], p["mla_kv_norm"], p["mla_w_uq"], p["mla_w_ukv"])
    yc, lse = _mla_fwd(qm, km, vt)
    yd, tot, cnt = _sb_fwd(hb)
    x_next = _epilogue_fwd(x, ya, yc, yd, hf, p["conv_w"], p["conv_b"], p["group_norm"], p["w_out"], p["norm_post"],
                           tgt)
    saved = dict(x=x, xn=xn, hb=hb, hf=hf, ya=ya, yc=yc, yd=yd, tot=tot, cnt=cnt, qm=qm, km=km, vm=vm, lse=lse)
    return x_next, saved, fetched


def _layer_bwd(dx_next, rope, p, s, exchange=()):
    (dya, dyc, dyd, dhf, dw_out, dg_post, dg_grp, dconv_w, dconv_b, *received) = _epilogue_bwd(
        dx_next, s["ya"], s["yc"], s["yd"], s["hf"], p["conv_w"], p["conv_b"], p["group_norm"], p["w_out"],
        p["norm_post"], exchange)
    dq_d, dk_d, dv_d = _sb_bwd(s["hb"], s["tot"], s["cnt"], dyd)
    dqm, dkt, dvt = _mla_bwd(s["qm"], s["km"], s["vm"], s["yc"], s["lse"], dyc)
    dc, dw_uq, dw_ukv, dg_q, dg_kv = _mla_prep_bwd(
        s["hf"], rope, p["mla_q_norm"], p["mla_kv_norm"], p["mla_w_uq"], p["mla_w_ukv"], dqm, dkt, dvt)
    dq_a, dk_a, dv_a, dsinks = _swa_bwd(s["hb"], p["attn_sinks"], dya)
    dx, dh, dg_pre = _inproj_bwd_dx(s["x"], p["norm_pre"], p["w_in"], dx_next,
                                    [dq_a, dk_a, dv_a, dq_d, dk_d, dv_d, dhf, dc])
    dwt_in = _grad_over_tokens(s["xn"], dh, "inproj_bwd_dw")
    grads = dict(norm_pre=dg_pre[0], w_in_t=_w_in_real(dwt_in), attn_sinks=dsinks[0, :4], conv_w=dconv_w[:3],
                 conv_b=dconv_b[0], mla_q_norm=dg_q[0], mla_w_uq=_uq_real(dw_uq), mla_kv_norm=dg_kv[0],
                 mla_w_ukv=_ukv_real(dw_ukv), group_norm=dg_grp[0], w_out=dw_out, norm_post=dg_post[0])
    return dx, grads, received


_WEIGHTS = ["norm_pre", "w_in", "attn_sinks", "conv_w", "conv_b", "mla_q_norm", "mla_w_uq", "mla_kv_norm",
            "mla_w_ukv", "group_norm", "w_out", "norm_post"]


def kernel(x, positions, norm_pre, w_in, attn_sinks, conv_w, conv_b, mla_q_norm, mla_w_uq, mla_kv_norm, mla_w_ukv, group_norm, w_out, norm_post, loss_target, m_norm_pre, m_w_in, m_attn_sinks, m_conv_w, m_conv_b, m_mla_q_norm, m_mla_w_uq, m_mla_kv_norm, m_mla_w_ukv, m_group_norm, m_w_out, m_norm_post, v_norm_pre, v_w_in, v_attn_sinks, v_conv_w, v_conv_b, v_mla_q_norm, v_mla_w_uq, v_mla_kv_norm, v_mla_w_ukv, v_group_norm, v_w_out, v_norm_post):
    w = dict(norm_pre=norm_pre, w_in=w_in, attn_sinks=attn_sinks, conv_w=conv_w, conv_b=conv_b,
             mla_q_norm=mla_q_norm, mla_w_uq=mla_w_uq, mla_kv_norm=mla_kv_norm, mla_w_ukv=mla_w_ukv,
             group_norm=group_norm, w_out=w_out, norm_post=norm_post)
    m = dict(norm_pre=m_norm_pre, w_in=m_w_in, attn_sinks=m_attn_sinks, conv_w=m_conv_w, conv_b=m_conv_b,
             mla_q_norm=m_mla_q_norm, mla_w_uq=m_mla_w_uq, mla_kv_norm=m_mla_kv_norm, mla_w_ukv=m_mla_w_ukv,
             group_norm=m_group_norm, w_out=m_w_out, norm_post=m_norm_post)
    v = dict(norm_pre=v_norm_pre, w_in=v_w_in, attn_sinks=v_attn_sinks, conv_w=v_conv_w, conv_b=v_conv_b,
             mla_q_norm=v_mla_q_norm, mla_w_uq=v_mla_w_uq, mla_kv_norm=v_mla_kv_norm, mla_w_ukv=v_mla_w_ukv,
             group_norm=v_group_norm, w_out=v_w_out, norm_post=v_norm_post)
    T = x.shape[1]
    xs = x[0]
    rope = _rope_tables(positions[0].reshape(T, 1))
    tgt = loss_target[0]
    core = lax.axis_index("c")

    def shard_parts(l):
        halves = lambda a: a.reshape((2, a.shape[0] // 2) + a.shape[1:])
        return [halves(jnp.swapaxes(w["w_in"][l], 0, 1).astype(BF16))] + [
            halves(w[n][l].astype(BF16)) for n in _BIG[1:4]] + [jnp.stack([w["conv_w"][l]] * 2)]

    def layer_params(l, got):
        whole = lambda a: a.reshape((4, 2 * a.shape[2]) + a.shape[3:])
        by_cols = lambda a: jnp.transpose(a, (1, 0, 2)).reshape(a.shape[1], 4 * a.shape[2])
        return dict(
            norm_pre=norm_pre[l:l + 1], w_in=_w_in_internal(whole(got[0]).reshape(D_IN, D_MODEL)),
            attn_sinks=attn_sinks[l], conv_w=by_cols(got[4][:, 0]), conv_b=conv_b[l:l + 1],
            mla_q_norm=mla_q_norm[l:l + 1], mla_w_uq=_uq_internal(by_cols(whole(got[2]))),
            mla_kv_norm=mla_kv_norm[l:l + 1], mla_w_ukv=_ukv_internal(by_cols(whole(got[3]))),
            group_norm=group_norm[l:l + 1], w_out=whole(got[1]).reshape(D_MODEL, D_MODEL),
            norm_post=norm_post[l:l + 1])

    layers, saved = [], []
    h, got = xs, _gather_weights(shard_parts(0))
    for l in range(DEPTH):
        last = l == DEPTH - 1
        layers.append(layer_params(l, got))
        h, s, got = _layer_fwd(h, rope, layers[l], tgt if last else None, () if last else shard_parts(l + 1))
        saved.append(s)
    dy, loss_part = h

    turned = ("w_in", "mla_w_uq")
    turn = lambda n, a: jnp.swapaxes(a, -1, -2) if n in turned else a

    def chunks(n, a):
        if n in ("w_out", "w_in"):
            return a.reshape(4, a.shape[0] // 4, a.shape[1])
        if n in turned:
            return a.T.reshape(4, a.shape[1] // 4, a.shape[0])
        return jnp.transpose(a.reshape(a.shape[0], 4, a.shape[1] // 4), (1, 0, 2))

    half_axis = {n: -2 if turn(n, w[n]).shape[-2] % 2 == 0 else -1 for n in _BIG}

    def reduce_begin(l):
        keep, give = [], []
        for n in _BIG:
            h0, h1 = jnp.split(chunks(n, grads[l]["w_in_t" if n == "w_in" else n]), 2, axis=half_axis[n])
            keep.append(jnp.where(core == 0, h0, h1))
            give.append(jnp.where(core == 0, h1, h0))
        got = _swap_cores(give, "swap_chunk_halves_%d" % l)
        return [_add(a, b, "add_cores_%d_%s" % (l, n)) for n, a, b in zip(_BIG, keep, got)]

    def reduce_end(l, received):
        mine = [_sum_leading(b, "sum_chips_%d_%s" % (l, n)) for n, b in zip(_BIG, received)]
        other = _swap_cores(mine, "swap_shard_halves_%d" % l)
        return [jnp.where(core == 0, jnp.concatenate([a, b], half_axis[n]), jnp.concatenate([b, a], half_axis[n]))
                for n, a, b in zip(_BIG, mine, other)]

    grads, shard_grads, pending = [None] * DEPTH, [None] * DEPTH, ()
    for l in reversed(range(DEPTH)):
        dy, grads[l], received = _layer_bwd(dy, rope, layers[l], saved[l], pending)
        if pending:
            shard_grads[l + 1] = reduce_end(l + 1, received)
        pending = reduce_begin(l)
    small = _pack_small({n: jnp.stack([grads[l][n] for l in range(DEPTH)]) for n in _SMALL}, loss_part[0, 0])
    *received, got_small = _exchange_chips(pending, small)
    shard_grads[0] = reduce_end(0, received)

    outs, loss = _adamw_small(w, m, v, got_small)
    for i, n in enumerate(_BIG):
        g = jnp.stack([shard_grads[l][i] for l in range(DEPTH)])
        update = _adamw_layers(turn(n, w[n]), turn(n, m[n]), turn(n, v[n]), g, "adamw_" + n)
        for d, a in zip(outs, [g, *update]):
            d[n] = turn(n, a)
    return (loss[0, 0], dy[None], *[outs[0][n] for n in _WEIGHTS], *[outs[1][n] for n in _WEIGHTS],
            *[outs[2][n] for n in _WEIGHTS], *[outs[3][n] for n in _WEIGHTS])
```

```python
import math

import jax
import jax.numpy as jnp
from jax import lax
from jax.experimental import pallas as pl
from jax.experimental.pallas import tpu as pltpu

F32 = jnp.float32
BF16 = jnp.bfloat16
MESH = pl.DeviceIdType.MESH

D_MODEL = 1024
DEPTH = 2
EPS = 1e-6
BLOCK = 128
HEAD = 64
LANES = 128
GROUP = 256
LOG2E = 1.4426950408889634
LN2 = 0.6931471805599453
MLA_QSCALE = 96 ** -0.5 * LOG2E
ROPE_HALF = 16
ROPE_THETA = 10000.0
SWA_SUB = 2
ATT_BLK = 256
MLA_BQ = 512
NEG = -1e30
SB_DEAD = -104.0

ADAM_LR, ADAM_B1, ADAM_B2, ADAM_EPS, ADAM_WD, ADAM_STEP = 0.001, 0.9, 0.999, 1e-08, 0.01, 10

_REAL = [("a_q", 256), ("a_k", 128), ("a_v", 128), ("b_b", 256), ("b_c", 256), ("b_x", 256),
         ("c_q", 256), ("c_kv", 128), ("c_kr", 32), ("d_q", 256), ("d_k", 256), ("d_v", 256),
         ("gate", 1024)]
_REAL_OFF = {}
_o = 0
for _n, _w in _REAL:
    _REAL_OFF[_n] = (_o, _w)
    _o += _w
D_IN = _o
_INT_ORDER = ["a_q", "a_k", "a_v", "d_q", "d_k", "d_v", "gate", "b_b", "b_c", "b_x", "c_q", "c_kv", "c_kr"]
_INT_W = dict(_REAL)
_INT_W["c_kr"] = 128
_INT_OFF = {}
_o = 0
for _n in _INT_ORDER:
    _INT_OFF[_n] = _o
    _o += _INT_W[_n]
N_INT = _o
N_HB = _INT_OFF["gate"]
N_HF = N_INT - N_HB

VMEM_LIMIT = 56 * 1024 * 1024


def _cparams(sem):
    return pltpu.CompilerParams(dimension_semantics=sem, vmem_limit_bytes=VMEM_LIMIT)


def _dot(a, b):
    return jnp.dot(a, b, preferred_element_type=F32)


def _dot_nt(a, b):
    return lax.dot_general(a, b, (((1,), (1,)), ((), ())), preferred_element_type=F32)


def _dot_tn(a, b):
    return lax.dot_general(a, b, (((0,), (0,)), ((), ())), preferred_element_type=F32)


def _split(x):
    hi = x.astype(BF16)
    lo = (x - hi.astype(F32)).astype(BF16)
    return hi, lo


def _rms(x):
    return lax.rsqrt(jnp.mean(x * x, axis=-1, keepdims=True) + EPS)


def _rms_bwd(dy, xhat, r, g):
    dxhat = dy * g
    return r * (dxhat - xhat * jnp.mean(dxhat * xhat, axis=-1, keepdims=True)), dy * xhat


def _colsum(x):
    return jnp.sum(x, axis=0, keepdims=True)


def _inproj_fwd(x, g, wt, fetch=()):
    T = x.shape[0]
    tm = 512
    nt, n = T // tm, len(fetch)

    def body(x_ref, g_ref, w_ref, *rest):
        xn_ref, hb_ref, hf_ref = rest[n:n + 3]
        if n:
            start, finish = _gather_plan(rest[:n], rest[n + 3:2 * n + 3], rest[2 * n + 3:])
            pl.when(pl.program_id(0) == 0)(start)
        xv = x_ref[...]
        xn = (xv * _rms(xv) * g_ref[...]).astype(BF16)
        xn_ref[...] = xn
        h = _dot_nt(xn, w_ref[...])
        hb_ref[...] = h[:, :N_HB].astype(BF16)
        hf_ref[...] = h[:, N_HB:]
        if n:
            pl.when(pl.program_id(0) == nt - 1)(finish)

    return pl.pallas_call(
        body, name="inproj_fwd_fetch" if n else "inproj_fwd", grid=(nt,),
        in_specs=[pl.BlockSpec((tm, D_MODEL), lambda i: (i, 0)),
                  pl.BlockSpec((1, D_MODEL), lambda i: (0, 0)),
                  pl.BlockSpec((N_INT, D_MODEL), lambda i: (0, 0))] + [HBM] * n,
        out_specs=[pl.BlockSpec((tm, D_MODEL), lambda i: (i, 0)),
                   pl.BlockSpec((tm, N_HB), lambda i: (i, 0)),
                   pl.BlockSpec((tm, N_HF), lambda i: (i, 0))] + [HBM] * n,
        out_shape=[jax.ShapeDtypeStruct((T, D_MODEL), BF16),
                   jax.ShapeDtypeStruct((T, N_HB), BF16),
                   jax.ShapeDtypeStruct((T, N_HF), F32)]
                  + [jax.ShapeDtypeStruct((4,) + s.shape, s.dtype) for s in fetch],
        scratch_shapes=_gather_sems(n) if n else [],
        compiler_params=_cparams(("arbitrary",) if n else ("parallel",)),
    )(x, g, wt, *fetch)


def _inproj_bwd_dx(x, g, wt, dx_next, pieces):
    T = x.shape[0]
    tm = 512
    widths = [p.shape[1] for p in pieces]
    assert sum(widths) == N_INT

    def body(x_ref, g_ref, w_ref, dxn_ref, *rest):
        p_refs = rest[:len(pieces)]
        dx_ref, dh_ref, dg_ref = rest[len(pieces):]
        dh = jnp.concatenate([p[...].astype(BF16) for p in p_refs], axis=1)
        dh_ref[...] = dh
        dxn = _dot(dh, w_ref[...])
        xv = x_ref[...]
        r = _rms(xv)
        dx, dgrow = _rms_bwd(dxn, xv * r, r, g_ref[...])
        dx_ref[...] = dx + dxn_ref[...]

        @pl.when(pl.program_id(0) == 0)
        def _():
            dg_ref[...] = jnp.zeros_like(dg_ref)

        dg_ref[...] += _colsum(dgrow)

    return pl.pallas_call(
        body, name="inproj_bwd_dx", grid=(T // tm,),
        in_specs=[pl.BlockSpec((tm, D_MODEL), lambda i: (i, 0)),
                  pl.BlockSpec((1, D_MODEL), lambda i: (0, 0)),
                  pl.BlockSpec((N_INT, D_MODEL), lambda i: (0, 0)),
                  pl.BlockSpec((tm, D_MODEL), lambda i: (i, 0))]
                 + [pl.BlockSpec((tm, wd), lambda i: (i, 0)) for wd in widths],
        out_specs=[pl.BlockSpec((tm, D_MODEL), lambda i: (i, 0)),
                   pl.BlockSpec((tm, N_INT), lambda i: (i, 0)),
                   pl.BlockSpec((1, D_MODEL), lambda i: (0, 0))],
        out_shape=[jax.ShapeDtypeStruct((T, D_MODEL), F32),
                   jax.ShapeDtypeStruct((T, N_INT), BF16),
                   jax.ShapeDtypeStruct((1, D_MODEL), F32)],
        compiler_params=_cparams(("arbitrary",)),
    )(x, g, wt, dx_next, *pieces)


def _grad_over_tokens(a, b, name):
    T, M = a.shape
    N = b.shape[1]
    tm, tn = min(1024, T), 896

    def body(a_ref, b_ref, o_ref):
        @pl.when(pl.program_id(1) == 0)
        def _():
            o_ref[...] = jnp.zeros_like(o_ref)

        o_ref[...] += _dot_tn(b_ref[...], a_ref[...])

    return pl.pallas_call(
        body, name=name, grid=(N // tn, T // tm),
        in_specs=[pl.BlockSpec((tm, M), lambda j, t: (t, 0)),
                  pl.BlockSpec((tm, tn), lambda j, t: (t, j))],
        out_specs=pl.BlockSpec((tn, M), lambda j, t: (j, 0)),
        out_shape=jax.ShapeDtypeStruct((N, M), F32),
        compiler_params=_cparams(("parallel", "arbitrary")),
    )(a, b)


def _roll_f32(x, shift):
    return pltpu.roll(x.astype(F32), shift, 1)


def _swa_operands(h, q, k_prev, k_cur, v_prev, v_cur):
    p, e = h // 2, h % 2
    lane = lax.broadcasted_iota(jnp.int32, (1, LANES), 1) // HEAD
    q = q[:, p * LANES:(p + 1) * LANES]
    if e != p:
        q = _roll_f32(q, HEAD).astype(BF16)
        v_prev = _roll_f32(v_prev, HEAD).astype(BF16)
        v_cur = _roll_f32(v_cur, HEAD).astype(BF16)
    qs = jnp.where(lane == p, q, 0) * 0.125
    return dict(p=p, e=e, lane=lane, qs=qs, k_prev=k_prev, k_cur=k_cur,
                v_prev=jnp.where(lane == e, v_prev, 0), v_cur=jnp.where(lane == e, v_cur, 0),
                s_prev=_dot_nt(qs, k_prev), s_cur=_dot_nt(qs, k_cur))


def _swa_probs(ops, sink, no_prev):
    row = lax.broadcasted_iota(jnp.int32, (BLOCK, BLOCK), 0)
    col = lax.broadcasted_iota(jnp.int32, (BLOCK, BLOCK), 1)
    ok_prev = col > row if no_prev is None else jnp.logical_and(col > row, jnp.logical_not(no_prev))
    s_prev = jnp.where(ok_prev, ops["s_prev"], NEG)
    s_cur = jnp.where(col <= row, ops["s_cur"], NEG)
    m = jnp.maximum(jnp.maximum(jnp.max(s_prev, axis=1, keepdims=True),
                                jnp.max(s_cur, axis=1, keepdims=True)), sink)
    p_prev = jnp.exp(s_prev - m)
    p_cur = jnp.exp(s_cur - m)
    p_sink = jnp.exp(sink - m)
    inv = 1.0 / (jnp.sum(p_prev, axis=1, keepdims=True) + jnp.sum(p_cur, axis=1, keepdims=True) + p_sink)
    return p_prev * inv, p_cur * inv, p_sink * inv


def _swa_specs(T):
    n = T // (BLOCK * SWA_SUB)
    qo, ko, vo = (_INT_OFF[name] // LANES for name in ("a_q", "a_k", "a_v"))
    halo = lambda i: jnp.maximum(i * SWA_SUB - 1, 0)
    return [pl.BlockSpec((BLOCK * SWA_SUB, 256), lambda i: (i, qo // 2)),
            pl.BlockSpec((BLOCK, LANES), lambda i: (halo(i), ko)),
            pl.BlockSpec((BLOCK * SWA_SUB, LANES), lambda i: (i, ko)),
            pl.BlockSpec((BLOCK, LANES), lambda i: (halo(i), vo)),
            pl.BlockSpec((BLOCK * SWA_SUB, LANES), lambda i: (i, vo)),
            pl.BlockSpec(memory_space=pltpu.SMEM)], n


def _swa_units(q_ref, kh_ref, kc_ref, vh_ref, vc_ref, s_ref):
    blk = lambda a: slice(a * BLOCK, (a + 1) * BLOCK)
    units = [(a, h) for a in range(SWA_SUB) for h in range(4)]
    ops = {}
    for a, h in units:
        k_prev, v_prev = (kh_ref[...], vh_ref[...]) if a == 0 else (kc_ref[blk(a - 1), :], vc_ref[blk(a - 1), :])
        ops[a, h] = _swa_operands(h, q_ref[blk(a), :], k_prev, kc_ref[blk(a), :], v_prev, vc_ref[blk(a), :])
    probs = {(a, h): _swa_probs(ops[a, h], s_ref[h], pl.program_id(0) == 0 if a == 0 else None) for a, h in units}
    return units, ops, probs, blk


def _swa_fwd(hb, sinks):
    T = hb.shape[0]
    specs, n = _swa_specs(T)

    def body(q_ref, kh_ref, kc_ref, vh_ref, vc_ref, s_ref, o_ref):
        units, ops, probs, blk = _swa_units(q_ref, kh_ref, kc_ref, vh_ref, vc_ref, s_ref)
        outs = {u: _dot(probs[u][0].astype(BF16), ops[u]["v_prev"]) + _dot(probs[u][1].astype(BF16), ops[u]["v_cur"])
                for u in units}
        for a in range(SWA_SUB):
            for p in range(2):
                o_ref[blk(a), p * LANES:(p + 1) * LANES] = outs[a, 2 * p] + outs[a, 2 * p + 1]

    return pl.pallas_call(
        body, name="swa_fwd", grid=(n,), in_specs=specs,
        out_specs=pl.BlockSpec((BLOCK * SWA_SUB, 256), lambda i: (i, 0)),
        out_shape=jax.ShapeDtypeStruct((T, 256), F32),
        compiler_params=_cparams(("parallel",)),
    )(hb, hb, hb, hb, hb, sinks)


def _swa_bwd(hb, sinks, dy):
    T = hb.shape[0]
    specs, n = _swa_specs(T)

    def body(q_ref, kh_ref, kc_ref, vh_ref, vc_ref, s_ref, dy_ref, dq_ref, dk_ref, dv_ref, ds_ref):
        i = pl.program_id(0)

        @pl.when(i == 0)
        def _():
            ds_ref[...] = jnp.zeros_like(ds_ref)

        lane_id = lax.broadcasted_iota(jnp.int32, (8, LANES), 1)
        units, ops, probs, blk = _swa_units(q_ref, kh_ref, kc_ref, vh_ref, vc_ref, s_ref)
        dos = {(a, h): jnp.where(ops[a, h]["lane"] == ops[a, h]["e"],
                                 dy_ref[blk(a), ops[a, h]["p"] * LANES:(ops[a, h]["p"] + 1) * LANES], 0.0)
               for a, h in units}
        dobs = {u: dos[u].astype(BF16) for u in units}
        pbs = {u: (probs[u][0].astype(BF16), probs[u][1].astype(BF16)) for u in units}
        outs = {u: _dot(pbs[u][0], ops[u]["v_prev"]) + _dot(pbs[u][1], ops[u]["v_cur"]) for u in units}
        dps = {u: (_dot_nt(dobs[u], ops[u]["v_prev"]), _dot_nt(dobs[u], ops[u]["v_cur"])) for u in units}
        dss, dsinks = {}, jnp.zeros((8, LANES), F32)
        for u in units:
            delta = jnp.sum(dos[u] * outs[u], axis=1, keepdims=True)
            dss[u] = ((probs[u][0] * (dps[u][0] - delta)).astype(BF16),
                      (probs[u][1] * (dps[u][1] - delta)).astype(BF16))
            dsink = -jnp.sum(probs[u][2] * delta, axis=0, keepdims=True)
            dsinks += jnp.where(lane_id == u[1], dsink, 0.0)
        ds_ref[...] += dsinks
        dqs = {u: (_dot(dss[u][0], ops[u]["k_prev"]) + _dot(dss[u][1], ops[u]["k_cur"])) * 0.125 for u in units}
        zero = jnp.zeros((BLOCK, LANES), F32)
        dk_as_prev, dk_as_cur = [zero] * SWA_SUB, [zero] * SWA_SUB
        dv_as_prev, dv_as_cur = [zero] * SWA_SUB, [zero] * SWA_SUB
        for a, h in units:
            p, e = ops[a, h]["p"], ops[a, h]["e"]
            dob_v = dobs[a, h] if e == p else pltpu.roll(dos[a, h], HEAD, 1).astype(BF16)
            dk_as_prev[a] = dk_as_prev[a] + _dot_tn(dss[a, h][0], ops[a, h]["qs"])
            dk_as_cur[a] = dk_as_cur[a] + _dot_tn(dss[a, h][1], ops[a, h]["qs"])
            dv_as_prev[a] = dv_as_prev[a] + _dot_tn(pbs[a, h][0], dob_v)
            dv_as_cur[a] = dv_as_cur[a] + _dot_tn(pbs[a, h][1], dob_v)
        base = i * SWA_SUB
        for a in range(SWA_SUB):
            rows = pl.ds(pl.multiple_of((base + a) * BLOCK, BLOCK), BLOCK)
            more = a + 1 < SWA_SUB
            dk_ref[rows, :] = dk_as_cur[a] + (dk_as_prev[a + 1] if more else 0.0)
            dv_ref[rows, :] = dv_as_cur[a] + (dv_as_prev[a + 1] if more else 0.0)
        halo = pl.ds(pl.multiple_of(jnp.maximum(base - 1, 0) * BLOCK, BLOCK), BLOCK)
        dk_ref[halo, :] += dk_as_prev[0]
        dv_ref[halo, :] += dv_as_prev[0]
        for a in range(SWA_SUB):
            for p in range(2):
                dq_pair = jnp.zeros((BLOCK, LANES), F32)
                for e in range(2):
                    dq = jnp.where(ops[a, 2 * p + e]["lane"] == p, dqs[a, 2 * p + e], 0.0)
                    dq_pair += dq if e == p else pltpu.roll(dq, HEAD, 1)
                dq_ref[blk(a), p * LANES:(p + 1) * LANES] = dq_pair.astype(BF16)

    return pl.pallas_call(
        body, name="swa_bwd", grid=(n,),
        in_specs=specs + [pl.BlockSpec((BLOCK * SWA_SUB, 256), lambda i: (i, 0))],
        out_specs=[pl.BlockSpec((BLOCK * SWA_SUB, 256), lambda i: (i, 0)),
                   pl.BlockSpec((T, LANES), lambda i: (0, 0)),
                   pl.BlockSpec((T, LANES), lambda i: (0, 0)),
                   pl.BlockSpec((8, LANES), lambda i: (0, 0))],
        out_shape=[jax.ShapeDtypeStruct((T, 256), BF16),
                   jax.ShapeDtypeStruct((T, LANES), F32),
                   jax.ShapeDtypeStruct((T, LANES), F32),
                   jax.ShapeDtypeStruct((8, LANES), F32)],
        compiler_params=_cparams(("arbitrary",)),
    )(hb, hb, hb, hb, hb, sinks, dy)


def _rope_tables(pos):
    T = pos.shape[0]
    tm = 512

    def body(pos_ref, o_ref):
        lane = lax.broadcasted_iota(jnp.int32, (1, LANES), 1)
        active = jnp.logical_and(lane >= HEAD, lane < HEAD + 2 * ROPE_HALF)
        idx = ((lane - HEAD) % ROPE_HALF).astype(F32)
        freq = jnp.exp(idx * (-math.log(ROPE_THETA) / ROPE_HALF))
        ang = pos_ref[...].astype(F32) * freq
        cos, sin = jnp.cos(ang), jnp.sin(ang)
        o_ref[:, 0:LANES] = jnp.where(active, cos, 1.0)
        o_ref[:, LANES:2 * LANES] = jnp.where(jnp.logical_and(active, lane >= HEAD + ROPE_HALF), sin, 0.0)
        o_ref[:, 2 * LANES:] = jnp.where(jnp.logical_and(active, lane < HEAD + ROPE_HALF), -sin, 0.0)

    return pl.pallas_call(
        body, name="rope_tables", grid=(T // tm,),
        in_specs=[pl.BlockSpec((tm, 1), lambda i: (i, 0))],
        out_specs=pl.BlockSpec((tm, 3 * LANES), lambda i: (i, 0)),
        out_shape=jax.ShapeDtypeStruct((T, 3 * LANES), F32),
        compiler_params=_cparams(("parallel",)),
    )(pos)


def _rope_factors(tab_ref):
    return tab_ref[:, 0:LANES], tab_ref[:, LANES:2 * LANES], tab_ref[:, 2 * LANES:]


def _rope(x, tabs):
    c, s_up, s_dn = tabs
    return x * c + pltpu.roll(x, ROPE_HALF, 1) * s_up + pltpu.roll(x, LANES - ROPE_HALF, 1) * s_dn


def _rope_t(dy, tabs):
    c, s_up, s_dn = tabs
    return dy * c + pltpu.roll(dy * s_up, LANES - ROPE_HALF, 1) + pltpu.roll(dy * s_dn, ROPE_HALF, 1)


def _mla_lat_specs(tm):
    cq, ckv, ckr = ((_INT_OFF[n] - N_HB) for n in ("c_q", "c_kv", "c_kr"))
    return [pl.BlockSpec((tm, 256), lambda i: (i, cq // 256)),
            pl.BlockSpec((tm, LANES), lambda i: (i, ckv // LANES)),
            pl.BlockSpec((tm, LANES), lambda i: (i, ckr // LANES)),
            pl.BlockSpec((tm, 3 * LANES), lambda i: (i, 0)),
            pl.BlockSpec((1, 256), lambda i: (0, 0)),
            pl.BlockSpec((1, LANES), lambda i: (0, 0)),
            pl.BlockSpec((256, 512), lambda i: (0, 0)),
            pl.BlockSpec((LANES, 768), lambda i: (0, 0))]


def _mla_prep_fwd(hf, rope, g_q, g_kv, w_uq, w_ukv):
    T = hf.shape[0]
    tm = 512
    sub = tm // ATT_BLK

    def body(cq_ref, ckv_ref, ckr_ref, tab_ref, gq_ref, gkv_ref, wq_ref, wkv_ref, qm_ref, km_ref, vm_ref, vt_ref):
        tabs = _rope_factors(tab_ref)
        cq = cq_ref[...]
        q = _dot((cq * _rms(cq) * gq_ref[...]).astype(BF16), wq_ref[...])
        ckv = ckv_ref[...]
        kv = _dot((ckv * _rms(ckv) * gkv_ref[...]).astype(BF16), wkv_ref[...])
        kr = _rope(pltpu.roll(ckr_ref[...], HEAD, 1), tabs)
        for h in range(4):
            sl = slice(h * LANES, (h + 1) * LANES)
            qm_ref[:, sl] = (_rope(q[:, sl], tabs) * MLA_QSCALE).astype(BF16)
            km_ref[:, sl] = (kv[:, sl] + kr).astype(BF16)
        vm_ref[...] = kv[:, 512:].astype(BF16)
        for p in range(2):
            for s in range(sub):
                tile = kv[s * ATT_BLK:(s + 1) * ATT_BLK, 512 + p * LANES:512 + (p + 1) * LANES]
                vt_ref[p, s] = jnp.transpose(tile).astype(BF16)

    return pl.pallas_call(
        body, name="mla_prep_fwd", grid=(T // tm,), in_specs=_mla_lat_specs(tm),
        out_specs=[pl.BlockSpec((tm, 512), lambda i: (i, 0)),
                   pl.BlockSpec((tm, 512), lambda i: (i, 0)),
                   pl.BlockSpec((tm, 256), lambda i: (i, 0)),
                   pl.BlockSpec((2, sub, LANES, ATT_BLK), lambda i: (0, i, 0, 0))],
        out_shape=[jax.ShapeDtypeStruct((T, 512), BF16),
                   jax.ShapeDtypeStruct((T, 512), BF16),
                   jax.ShapeDtypeStruct((T, 256), BF16),
                   jax.ShapeDtypeStruct((2, T // ATT_BLK, LANES, ATT_BLK), BF16)],
        compiler_params=_cparams(("parallel",)),
    )(hf, hf, hf, rope, g_q, g_kv, w_uq, w_ukv)


def _mla_prep_bwd(hf, rope, g_q, g_kv, w_uq, w_ukv, dqm, dkt, dvt):
    T = hf.shape[0]
    tm = 512
    sub = tm // ATT_BLK

    def body(cq_ref, ckv_ref, ckr_ref, tab_ref, gq_ref, gkv_ref, wq_ref, wkv_ref, dq_ref, dk_ref, dv_ref,
             dc_ref, dwq_ref, dwkv_ref, dgq_ref, dgkv_ref):
        @pl.when(pl.program_id(0) == 0)
        def _():
            dwq_ref[...] = jnp.zeros_like(dwq_ref)
            dwkv_ref[...] = jnp.zeros_like(dwkv_ref)
            dgq_ref[...] = jnp.zeros_like(dgq_ref)
            dgkv_ref[...] = jnp.zeros_like(dgkv_ref)

        tabs = _rope_factors(tab_ref)
        lane =lax.broadcasted_iota(jnp.int32, (1, LANES), 1)
        dq = jnp.concatenate([_rope_t(dq_ref[:, h * LANES:(h + 1) * LANES] * MLA_QSCALE, tabs)
                              for h in range(4)], axis=1).astype(BF16)
        cq = cq_ref[...]
        rq = _rms(cq)
        cqn = (cq * rq * gq_ref[...]).astype(BF16)
        dwq_ref[...] += _dot_tn(cqn, dq)
        dcq, dgrow = _rms_bwd(_dot_nt(dq, wq_ref[...]), cq * rq, rq, gq_ref[...])
        dgq_ref[...] += _colsum(dgrow)
        dc_ref[:, 0:256] = dcq.astype(BF16)

        dk = jnp.concatenate([jnp.concatenate([jnp.transpose(dk_ref[p, s]) for p in range(2)], axis=1)
                              for s in range(sub)], axis=0) * LN2
        dv = jnp.concatenate([jnp.concatenate([jnp.transpose(dv_ref[p, s]) for p in range(2)], axis=1)
                              for s in range(sub)], axis=0)
        dkr = dk[:, 0:LANES] + dk[:, LANES:2 * LANES] + dk[:, 2 * LANES:3 * LANES] + dk[:, 3 * LANES:]
        dkr = pltpu.roll(_rope_t(dkr, tabs), HEAD, 1)
        dc_ref[:, 384:512] = jnp.where(lane < 2 * ROPE_HALF, dkr, 0.0).astype(BF16)
        dkv = jnp.concatenate([dk.astype(BF16), dv.astype(BF16)], axis=1)
        ckv = ckv_ref[...]
        rkv = _rms(ckv)
        ckvn = (ckv * rkv * gkv_ref[...]).astype(BF16)
        dwkv_ref[...] += _dot_tn(ckvn, dkv)
        dckv, dgrow = _rms_bwd(_dot_nt(dkv, wkv_ref[...]), ckv * rkv, rkv, gkv_ref[...])
        dgkv_ref[...] += _colsum(dgrow)
        dc_ref[:, 256:384] = dckv.astype(BF16)

    return pl.pallas_call(
        body, name="mla_prep_bwd", grid=(T // tm,),
        in_specs=_mla_lat_specs(tm) + [pl.BlockSpec((tm, 512), lambda i: (i, 0)),
                                       pl.BlockSpec((2, sub, 256, ATT_BLK), lambda i: (0, i, 0, 0)),
                                       pl.BlockSpec((2, sub, LANES, ATT_BLK), lambda i: (0, i, 0, 0))],
        out_specs=[pl.BlockSpec((tm, 512), lambda i: (i, 0)),
                   pl.BlockSpec((256, 512), lambda i: (0, 0)),
                   pl.BlockSpec((LANES, 768), lambda i: (0, 0)),
                   pl.BlockSpec((1, 256), lambda i: (0, 0)),
                   pl.BlockSpec((1, LANES), lambda i: (0, 0))],
        out_shape=[jax.ShapeDtypeStruct((T, 512), BF16),
                   jax.ShapeDtypeStruct((256, 512), F32),
                   jax.ShapeDtypeStruct((LANES, 768), F32),
                   jax.ShapeDtypeStruct((1, 256), F32),
                   jax.ShapeDtypeStruct((1, LANES), F32)],
        compiler_params=_cparams(("arbitrary",)),
    )(hf, hf, hf, rope, g_q, g_kv, w_uq, w_ukv, dqm, dkt, dvt)


def _causal_masks(bq, bk):
    row = lax.broadcasted_iota(jnp.int32, (bq, bk), 0)
    col = lax.broadcasted_iota(jnp.int32, (bq, bk), 1)
    return row, col


def _mla_fwd(qm, km, vt):
    T = qm.shape[0]
    bq, bk = min(MLA_BQ, T), ATT_BLK
    nq, nsub, nk = T // bq, bq // bk, T // bk

    def body(q_ref, k_ref, vt_ref, o_ref, lse_ref, acc_ref, m_ref, l_ref):
        qi = pl.program_id(0)
        key = lax.broadcasted_iota(jnp.int32, (bk, bq), 0)
        qry = lax.broadcasted_iota(jnp.int32, (bk, bq), 1)
        ones = jnp.ones((8, bk), BF16)
        acc_ref[...] = jnp.zeros_like(acc_ref)
        m_ref[...] = jnp.full_like(m_ref, NEG)
        l_ref[...] = jnp.zeros_like(l_ref)

        def step(kb0, masked):
            kbs = [kb0 + d for d in range(nsub)]
            qs = [slice(d * bk if masked else 0, bq) for d in range(nsub)]

            def wide(a, d, fill):
                if not qs[d].start:
                    return a
                return jnp.concatenate([jnp.full((a.shape[0], qs[d].start), fill, a.dtype), a], axis=1)

            sts = [[_dot_nt(k_ref[pl.ds(pl.multiple_of(kb * bk, bk), bk), e * LANES:(e + 1) * LANES],
                            q_ref[qs[d], e * LANES:(e + 1) * LANES]) for d, kb in enumerate(kbs)] for e in range(4)]
            pts, alphas = [], []
            for e in range(4):
                st = ([jnp.where(key[:, qs[d]] + d * bk <= qry[:, qs[d]], sts[e][d], NEG) for d in range(nsub)]
                      if masked else sts[e])
                m_prev = m_ref[e, 0:1, :]
                m_new = m_prev
                for d in range(nsub):
                    m_new = jnp.maximum(m_new, wide(jnp.max(st[d], axis=0, keepdims=True), d, NEG))
                alpha = jnp.exp2(m_prev - m_new)
                pt = [jnp.exp2(st[d] - m_new[:, qs[d]]).astype(BF16) for d in range(nsub)]
                l_new = alpha * l_ref[e]
                for d in range(nsub):
                    l_new = l_new + wide(_dot(ones, pt[d]), d, 0.0)
                l_ref[e] = l_new
                m_ref[e] = jnp.broadcast_to(m_new, (8, bq))
                pts.append(pt)
                alphas.append(alpha)
            for e in range(4):
                acc = alphas[e] * acc_ref[e]
                for d in range(nsub):
                    v_t = vt_ref[e // 2, kbs[d], (e % 2) * HEAD:(e % 2 + 1) * HEAD, :]
                    acc = acc + wide(_dot(v_t, pts[e][d]), d, 0.0)
                acc_ref[e] = acc

        step(qi * nsub, True)

        def loop(t, c):
            step(t * nsub, False)
            return c

        lax.fori_loop(0, qi, loop, 0)
        outs, lses = [], []
        for e in range(4):
            l = l_ref[e, 0:1, :]
            outs.append(acc_ref[e] / l)
            lses.append(jnp.broadcast_to(m_ref[e, 0:1, :] * LN2 + jnp.log(l), (HEAD, bq)))
        o_ref[...] = jnp.transpose(jnp.concatenate(outs, axis=0))
        lse_ref[...] = jnp.transpose(jnp.concatenate(lses, axis=0))

    return pl.pallas_call(
        body, name="mla_fwd", grid=(nq,),
        in_specs=[pl.BlockSpec((bq, 512), lambda i: (i, 0)),
                  pl.BlockSpec((T, 512), lambda i: (0, 0)),
                  pl.BlockSpec((2, nk, LANES, bk), lambda i: (0, 0, 0, 0))],
        out_specs=[pl.BlockSpec((bq, 256), lambda i: (i, 0)),
                   pl.BlockSpec((bq, 256), lambda i: (i, 0))],
        out_shape=[jax.ShapeDtypeStruct((T, 256), F32), jax.ShapeDtypeStruct((T, 256), F32)],
        scratch_shapes=[pltpu.VMEM((4, HEAD, bq), F32), pltpu.VMEM((4, 8, bq), F32), pltpu.VMEM((4, 8, bq), F32)],
        compiler_params=_cparams(("arbitrary",)),
    )(qm, km, vt)


def _mla_bwd(qm, km, vm, y, lse, dy):
    T = qm.shape[0]
    bq, bk = min(MLA_BQ, T), ATT_BLK
    nq, nsub, nk = T // bq, bq // bk, T // bk

    def body(q_ref, k_ref, v_ref, y_ref, lse_ref, dy_ref, dq_ref, dkt_ref, dvt_ref, dob_ref, st_ref, qt_ref, dot_ref):
        qi = pl.program_id(1)

        @pl.when(qi == 0)
        def _():
            dkt_ref[...] = jnp.zeros_like(dkt_ref)
            dvt_ref[...] = jnp.zeros_like(dvt_ref)

        lane = lax.broadcasted_iota(jnp.int32, (1, LANES), 1) // HEAD
        row, col = _causal_masks(bq, bk)
        dq_ref[...] = jnp.zeros_like(dq_ref)
        lse = lse_ref[...]
        lse_other = pltpu.roll(lse, HEAD, 1)
        qt_ref[...] = jnp.transpose(q_ref[...].astype(F32)).astype(BF16)
        dot_ref[...] = jnp.transpose(dy_ref[...]).astype(BF16)
        for e in range(2):
            do = jnp.where(lane == e, dy_ref[...], 0.0)
            dob_ref[e] = do.astype(BF16)
            st_ref[2 * e] = jnp.where(lane == e, lse, lse_other) * LOG2E
            st_ref[2 * e + 1] = jnp.broadcast_to(jnp.sum(do * y_ref[...], axis=1, keepdims=True), (bq, LANES))

        hss = [slice(e * LANES, (e + 1) * LANES) for e in range(2)]
        tile = lambda a: jnp.concatenate([a] * (bk // LANES), axis=1)

        def step(kb0, masked):
            kbs = [kb0 + d for d in range(nsub)]
            rows = [pl.ds(pl.multiple_of(kb * bk, bk), bk) for kb in kbs]
            pairs = [(d, e) for d in range(nsub) for e in range(2)]
            qs = [slice(d * bk if masked else 0, bq) for d in range(nsub)]
            ss = {(d, e): _dot_nt(q_ref[qs[d], hss[e]], k_ref[rows[d], hss[e]]) for d, e in pairs}
            dps = {(d, e): _dot_nt(dob_ref[e, qs[d], :], jnp.where(lane == e, v_ref[rows[d], :], 0))
                   for d, e in pairs}
            ps, dss = {}, {}
            for d, e in pairs:
                s = jnp.where(col[qs[d]] + d * bk <= row[qs[d]], ss[d, e], NEG) if masked else ss[d, e]
                p = jnp.exp2(s - tile(st_ref[2 * e, qs[d], :]))
                dss[d, e] = (p * (dps[d, e] - tile(st_ref[2 * e + 1, qs[d], :]))).astype(BF16)
                ps[d, e] = p.astype(BF16)
            for d, e in pairs:
                dvt_ref[0, kbs[d], e * HEAD:(e + 1) * HEAD, :] += _dot(
                    dot_ref[e * HEAD:(e + 1) * HEAD, qs[d]], ps[d, e])
            for d, e in pairs:
                dkt_ref[0, kbs[d], hss[e], :] += _dot(qt_ref[hss[e], qs[d]], dss[d, e])
            for e in range(2):
                if masked:
                    for d in range(nsub):
                        dq_ref[qs[d], hss[e]] += _dot(dss[d, e], k_ref[rows[d], hss[e]])
                else:
                    dq = dq_ref[:, hss[e]]
                    for d in range(nsub):
                        dq = dq + _dot(dss[d, e], k_ref[rows[d], hss[e]])
                    dq_ref[:, hss[e]] = dq

        step(qi * nsub, True)

        def loop(t, c):
            step(t * nsub, False)
            return c

        lax.fori_loop(0, qi, loop, 0)
        dq_ref[...] *= LN2

    return pl.pallas_call(
        body, name="mla_bwd", grid=(2, nq),
        in_specs=[pl.BlockSpec((bq, 256), lambda j, i: (i, j)),
                  pl.BlockSpec((T, 256), lambda j, i: (0, j)),
                  pl.BlockSpec((T, LANES), lambda j, i: (0, j)),
                  pl.BlockSpec((bq, LANES), lambda j, i: (i, j)),
                  pl.BlockSpec((bq, LANES), lambda j, i: (i, j)),
                  pl.BlockSpec((bq, LANES), lambda j, i: (i, j))],
        out_specs=[pl.BlockSpec((bq, 256), lambda j, i: (i, j)),
                   pl.BlockSpec((1, nk, 256, bk), lambda j, i: (j, 0, 0, 0)),
                   pl.BlockSpec((1, nk, LANES, bk), lambda j, i: (j, 0, 0, 0))],
        out_shape=[jax.ShapeDtypeStruct((T, 512), F32),
                   jax.ShapeDtypeStruct((2, nk, 256, bk), F32),
                   jax.ShapeDtypeStruct((2, nk, LANES, bk), F32)],
        scratch_shapes=[pltpu.VMEM((2, bq, LANES), BF16), pltpu.VMEM((4, bq, LANES), F32),
                        pltpu.VMEM((256, bq), BF16), pltpu.VMEM((LANES, bq), BF16)],
        compiler_params=_cparams(("parallel", "arbitrary")),
    )(qm, km, vm, y, lse, dy)


def _suffix_ones(n):
    r = lax.broadcasted_iota(jnp.int32, (n, n), 0)
    c = lax.broadcasted_iota(jnp.int32, (n, n), 1)
    return (r >= c).astype(BF16)


def _prefix_ones(n):
    r = lax.broadcasted_iota(jnp.int32, (n, n), 0)
    c = lax.broadcasted_iota(jnp.int32, (n, n), 1)
    return (r <= c).astype(BF16)


def _sb_specs(T, bq):
    qo, ko, vo = (_INT_OFF[n] // 256 for n in ("d_q", "d_k", "d_v"))
    return [pl.BlockSpec((bq, 256), lambda i: (i, qo)),
            pl.BlockSpec((T, 256), lambda i: (0, ko)),
            pl.BlockSpec((T, 256), lambda i: (0, vo))]


def _sb_fwd(hb):
    T = hb.shape[0]
    bq = bk = ATT_BLK
    nq = T // bq

    def body(q_ref, k_ref, v_ref, o_ref, tot_ref, cnt_ref, qm_ref, car_ref):
        qi = pl.program_id(0)
        lane = lax.broadcasted_iota(jnp.int32, (1, LANES), 1) // HEAD
        row, col = _causal_masks(bq, bk)
        strict = col < row
        u = _suffix_ones(bk)
        o_ref[...] = jnp.zeros_like(o_ref)
        car_ref[...] = jnp.zeros_like(car_ref)
        pair = lambda h: slice((h // 2) * LANES, (h // 2 + 1) * LANES)
        for h in range(4):
            qm_ref[h] = jnp.where(lane == h % 2, q_ref[:, pair(h)], 0) * 0.125

        def step(blocks):
            tile = lambda a: jnp.concatenate([a] * (bk // LANES), axis=1)
            rows = [pl.ds(pl.multiple_of(kb * bk, bk), bk) for kb, _ in blocks]
            pairs = [(b, h) for b in range(len(blocks)) for h in range(4)]
            zs = {(b, h): _dot_nt(qm_ref[h], k_ref[rows[b], pair(h)]) for b, h in pairs}
            splits = {}
            for b, h in pairs:
                z = zs[b, h]
                lk = jnp.minimum(-z, 0.0) - jnp.log(1.0 + jnp.exp(-jnp.abs(z)))
                if blocks[b][1] is not None:
                    lk = jnp.where(blocks[b][1], lk, 0.0)
                splits[b, h] = _split(lk)
            sufs = {bh: _dot(hi, u) + _dot(lo, u) for bh, (hi, lo) in splits.items()}
            car = [car_ref[h] for h in range(4)]
            aas = {}
            for b, h in pairs:
                a = jnp.exp(zs[b, h] + sufs[b, h] + tile(car[h]))
                if blocks[b][1] is not None:
                    a = jnp.where(blocks[b][1], a, 0.0)
                aas[b, h] = a.astype(BF16)
                car[h] = car[h] + jnp.broadcast_to(sufs[b, h][:, 0:1], (bq, LANES))
            acc = [o_ref[:, pair(0)], o_ref[:, pair(2)]]
            for b, h in pairs:
                acc[h // 2] = acc[h // 2] + _dot(aas[b, h], jnp.where(lane == h % 2, v_ref[rows[b], pair(h)], 0))
            o_ref[:, pair(0)], o_ref[:, pair(2)] = acc
            for h in range(4):
                car_ref[h] = car[h]

        step([(qi, strict), (jnp.maximum(qi - 1, 0), qi > 0)])

        def live():
            worst = jnp.maximum(jnp.maximum(car_ref[0], car_ref[1]), jnp.maximum(car_ref[2], car_ref[3]))
            return jnp.max(worst) >= SB_DEAD

        def cond(c):
            return jnp.logical_and(c[0] < qi, c[1])

        def loop(c):
            step([(qi - 1 - c[0], None)])
            return c[0] + 1, live()

        done, _ = lax.while_loop(cond, loop, (jnp.minimum(qi, 1), live()))
        tot_ref[:, pair(0)] = jnp.where(lane == 0, car_ref[0], car_ref[1])
        tot_ref[:, pair(2)] = jnp.where(lane == 0, car_ref[2], car_ref[3])
        cnt_ref[0, qi] = done.astype(F32)

    return pl.pallas_call(
        body, name="sb_fwd", grid=(nq,), in_specs=_sb_specs(T, bq),
        out_specs=[pl.BlockSpec((bq, 256), lambda i: (i, 0)), pl.BlockSpec((bq, 256), lambda i: (i, 0)),
                   pl.BlockSpec(memory_space=pltpu.SMEM)],
        out_shape=[jax.ShapeDtypeStruct((T, 256), F32), jax.ShapeDtypeStruct((T, 256), F32),
                   jax.ShapeDtypeStruct((1, nq), F32)],
        scratch_shapes=[pltpu.VMEM((4, bq, LANES), BF16), pltpu.VMEM((4, bq, LANES), F32)],
        compiler_params=_cparams(("arbitrary",)),
    )(hb, hb, hb)


def _sb_bwd(hb, tot, cnt, dy):
    T = hb.shape[0]
    bq = bk = ATT_BLK
    nq = T // bq

    def body(q_ref, k_ref, v_ref, tot_ref, dy_ref, cnt_ref, dq_ref, dk_ref, dv_ref, qm_ref, dob_ref, dqa_ref, rem_ref,
             cg_ref):
        qi = pl.program_id(0)

        @pl.when(qi == 0)
        def _():
            dk_ref[...] = jnp.zeros_like(dk_ref)
            dv_ref[...] = jnp.zeros_like(dv_ref)

        lane = lax.broadcasted_iota(jnp.int32, (1, LANES), 1) // HEAD
        row, col = _causal_masks(bq, bk)
        strict = col < row
        u = _prefix_ones(bk)
        pair = lambda h: slice((h // 2) * LANES, (h // 2 + 1) * LANES)
        dqa_ref[...] = jnp.zeros_like(dqa_ref)
        cg_ref[...] = jnp.zeros_like(cg_ref)
        for h in range(4):
            tot = tot_ref[:, pair(h)]
            qm_ref[h] = jnp.where(lane == h % 2, q_ref[:, pair(h)], 0) * 0.125
            dob_ref[h] = jnp.where(lane == h % 2, dy_ref[:, pair(h)], 0.0).astype(BF16)
            rem_ref[h] = jnp.where(lane == h % 2, tot, pltpu.roll(tot, HEAD, 1))

        def step(blocks):
            tile = lambda a: jnp.concatenate([a] * (bk // LANES), axis=1)
            nb = len(blocks)
            rows = [pl.ds(pl.multiple_of(kb * bk, bk), bk) for kb, _ in blocks]
            pairs = [(b, h) for b in range(nb) for h in range(4)]
            mask = lambda b, x: x if blocks[b][1] is None else jnp.where(blocks[b][1], x, 0.0)
            zs = {(b, h): _dot_nt(qm_ref[h], k_ref[rows[b], pair(h)]) for b, h in pairs}
            das = {(b, h): _dot_nt(dob_ref[h], jnp.where(lane == h % 2, v_ref[rows[b], pair(h)], 0)) for b, h in pairs}
            zls, splits = {}, {}
            for b, h in pairs:
                z = zs[b, h]
                lk = mask(b, jnp.minimum(-z, 0.0) - jnp.log(1.0 + jnp.exp(-jnp.abs(z))))
                zls[b, h] = z + lk
                splits[b, h] = _split(lk)
            pres = {bh: _dot(hi, u) + _dot(lo, u) for bh, (hi, lo) in splits.items()}
            rem = [rem_ref[h] for h in range(4)]
            aas, gs, gsplits = {}, {}, {}
            for b, h in pairs:
                a = mask(b, jnp.exp(zls[b, h] + (tile(rem[h]) - pres[b, h])))
                gs[b, h] = a * das[b, h]
                aas[b, h] = a.astype(BF16)
                gsplits[b, h] = _split(gs[b, h])
                rem[h] = rem[h] - jnp.broadcast_to(pres[b, h][:, bk - 1:bk], (bq, LANES))
            for b in range(nb):
                for p in (0, 2):
                    dv_ref[rows[b], pair(p)] += _dot_tn(aas[b, p], dob_ref[p]) + _dot_tn(aas[b, p + 1], dob_ref[p + 1])
            gpres = {bh: _dot(hi, u) + _dot(lo, u) for bh, (hi, lo) in gsplits.items()}
            cg = [cg_ref[h] for h in range(4)]
            dzs = {}
            for b, h in pairs:
                dz = mask(b, gs[b, h] - jnp.exp(zls[b, h]) * (tile(cg[h]) + gpres[b, h]))
                dzs[b, h] = dz.astype(BF16)
                cg[h] = cg[h] + jnp.broadcast_to(gpres[b, h][:, bk - 1:bk], (bq, LANES))
            for b in range(nb):
                for p in (0, 2):
                    dk_ref[rows[b], pair(p)] += _dot_tn(dzs[b, p], qm_ref[p]) + _dot_tn(dzs[b, p + 1], qm_ref[p + 1])
            for h in range(4):
                dq = dqa_ref[h]
                for b in range(nb):
                    dq = dq + _dot(dzs[b, h], k_ref[rows[b], pair(h)])
                dqa_ref[h] = dq
                rem_ref[h] = rem[h]
                cg_ref[h] = cg[h]

        def loop(kb, c):
            step([(kb, None)])
            return c

        start = qi - jnp.clip(cnt_ref[0, qi].astype(jnp.int32), 0, qi)
        lax.fori_loop(start, qi - 1, loop, 0)
        step([(jnp.maximum(qi - 1, 0), qi > 0), (qi, strict)])
        for p in (0, 2):
            dq_ref[:, pair(p)] = (jnp.where(lane == 0, dqa_ref[p], dqa_ref[p + 1]) * 0.125).astype(BF16)

    return pl.pallas_call(
        body, name="sb_bwd", grid=(nq,),
        in_specs=_sb_specs(T, bq) + [pl.BlockSpec((bq, 256), lambda i: (i, 0)),
                                     pl.BlockSpec((bq, 256), lambda i: (i, 0)),
                                     pl.BlockSpec(memory_space=pltpu.SMEM)],
        out_specs=[pl.BlockSpec((bq, 256), lambda i: (i, 0)),
                   pl.BlockSpec((T, 256), lambda i: (0, 0)),
                   pl.BlockSpec((T, 256), lambda i: (0, 0))],
        out_shape=[jax.ShapeDtypeStruct((T, 256), BF16)] + [jax.ShapeDtypeStruct((T, 256), F32)] * 2,
        scratch_shapes=[pltpu.VMEM((4, bq, LANES), BF16), pltpu.VMEM((4, bq, LANES), BF16),
                        pltpu.VMEM((4, bq, LANES), F32), pltpu.VMEM((4, bq, LANES), F32),
                        pltpu.VMEM((4, bq, LANES), F32)],
        compiler_params=_cparams(("arbitrary",)),
    )(hb, hb, hb, tot, dy, cnt)


EP_TM = 512


def _ep_in_specs(tm, rev):
    idx = (lambda i: rev - i) if rev is not None else (lambda i: i)
    bo = (_INT_OFF["b_b"] - N_HB) // 256
    halo = lambda i: jnp.maximum(idx(i) * (tm // 8) - 1, 0)
    return [pl.BlockSpec((tm, 256), lambda i: (idx(i), 0)),
            pl.BlockSpec((tm, 256), lambda i: (idx(i), 0)),
            pl.BlockSpec((tm, 256), lambda i: (idx(i), 0)),
            pl.BlockSpec((tm, D_MODEL), lambda i: (idx(i), 0)),
            pl.BlockSpec((tm, 256), lambda i: (idx(i), bo)),
            pl.BlockSpec((tm, 256), lambda i: (idx(i), bo + 1)),
            pl.BlockSpec((tm, 256), lambda i: (idx(i), bo + 2)),
            pl.BlockSpec((8, 256), lambda i: (halo(i), bo + 1)),
            pl.BlockSpec((8, 256), lambda i: (halo(i), bo + 2)),
            pl.BlockSpec((3, 256), lambda i: (0, 0)),
            pl.BlockSpec((1, 256), lambda i: (0, 0)),
            pl.BlockSpec((1, D_MODEL), lambda i: (0, 0)),
            pl.BlockSpec((D_MODEL, D_MODEL), lambda i: (0, 0)),
            pl.BlockSpec((1, D_MODEL), lambda i: (0, 0))]


def _ep_mix(first, ya_ref, yc_ref, yd_ref, gate_ref, bb_ref, bc_ref, bx_ref, hc_ref, hx_ref, cw_ref, cb_ref, gg_ref):
    tm = ya_ref.shape[0]
    u = bc_ref[...] * bx_ref[...]
    halo = jnp.where(first, 0.0, hc_ref[...] * hx_ref[...])
    row = lax.broadcasted_iota(jnp.int32, (tm, 1), 0)
    u1 = jnp.where(row == 0, halo[7:8, :], pltpu.roll(u, 1, 0))
    u2 = jnp.where(row == 0, halo[6:7, :], jnp.where(row == 1, halo[7:8, :], pltpu.roll(u, 2, 0)))
    cw = cw_ref[...]
    conv = cw[0:1, :] * u2 + cw[1:2, :] * u1 + cw[2:3, :] * u + cb_ref[...]
    bb = bb_ref[...]
    ys = [ya_ref[...], bb * conv, yc_ref[...], yd_ref[...]]
    rs = [_rms(y) for y in ys]
    gg = gg_ref[...]
    yhat = jnp.concatenate([y * r for y, r in zip(ys, rs)], axis=1)
    gate = gate_ref[...]
    sig = 1.0 / (1.0 + jnp.exp(-gate))
    return u, u1, u2, conv, bb, rs, yhat, yhat * gg, gate, sig


def _epilogue_fwd(x, ya, yc, yd, hf, conv_w, conv_b, g_grp, w_out, g_post, tgt=None):
    T = x.shape[0]
    tm = EP_TM
    row_spec = pl.BlockSpec((tm, D_MODEL), lambda i: (i, 0))

    def layer_out(refs):
        (x_ref, ya_ref, yc_ref, yd_ref, gate_ref, bb_ref, bc_ref, bx_ref, hc_ref, hx_ref, cw_ref, cb_ref,
         gg_ref, wo_ref, gp_ref) = refs
        (_, _, _, _, _, _, _, yn, gate, sig) = _ep_mix(
            pl.program_id(0) == 0, ya_ref, yc_ref, yd_ref, gate_ref, bb_ref, bc_ref, bx_ref, hc_ref, hx_ref,
            cw_ref, cb_ref, gg_ref)
        z = _dot((yn * (gate * sig)).astype(BF16), wo_ref[...])
        return x_ref[...] + z * _rms(z) * gp_ref[...]

    args = (x, ya, yc, yd, hf, hf, hf, hf, hf, hf, conv_w, conv_b, g_grp, w_out, g_post)
    in_specs = [row_spec] + _ep_in_specs(tm, None)
    if tgt is None:
        def body(*refs):
            refs[-1][...] = layer_out(refs[:-1])

        return pl.pallas_call(
            body, name="epilogue_fwd", grid=(T // tm,), in_specs=in_specs, out_specs=row_spec,
            out_shape=jax.ShapeDtypeStruct((T, D_MODEL), F32), compiler_params=_cparams(("parallel",)),
        )(*args)

    def body_loss(*refs):
        t_ref, dy_ref, l_ref = refs[-3:]

        @pl.when(pl.program_id(0) == 0)
        def _():
            l_ref[...] = jnp.zeros_like(l_ref)

        d = layer_out(refs[:-3]) - t_ref[...]
        dy_ref[...] = d * (1.0 / D_MODEL)
        part = jnp.sum(jnp.sum(d * d, axis=1, keepdims=True), axis=0, keepdims=True)
        l_ref[...] += part * (0.5 / D_MODEL)

    return pl.pallas_call(
        body_loss, name="epilogue_fwd_loss", grid=(T // tm,), in_specs=in_specs + [row_spec],
        out_specs=[row_spec, pl.BlockSpec((8, LANES), lambda i: (0, 0))],
        out_shape=[jax.ShapeDtypeStruct((T, D_MODEL), F32), jax.ShapeDtypeStruct((8, LANES), F32)],
        compiler_params=_cparams(("arbitrary",)),
    )(*args, tgt)


def _epilogue_bwd(dxn, ya, yc, yd, hf, conv_w, conv_b, g_grp, w_out, g_post):
    T = dxn.shape[0]
    tm = EP_TM
    nt = T // tm
    ridx = lambda i: (nt - 1 - i, 0)

    def body(dx_ref, ya_ref, yc_ref, yd_ref, gate_ref, bb_ref, bc_ref, bx_ref, hc_ref, hx_ref, cw_ref, cb_ref,
             gg_ref, wo_ref, gp_ref,
             dya_ref, dyc_ref, dyd_ref, dhf_ref, dwo_ref, dgp_ref, dgg_ref, dcw_ref, dcb_ref, carry_ref):
        i = pl.program_id(0)

        @pl.when(i == 0)
        def _():
            for r in (dwo_ref, dgp_ref, dgg_ref, dcw_ref, dcb_ref, carry_ref):
                r[...] = jnp.zeros_like(r)

        (u, u1, u2, conv, bb, rs, yhat, yn, gate, sig) = _ep_mix(
            i == nt - 1, ya_ref, yc_ref, yd_ref, gate_ref, bb_ref, bc_ref, bx_ref, hc_ref, hx_ref,
            cw_ref, cb_ref, gg_ref)
        silu = gate * sig
        ymix = (yn * silu).astype(BF16)
        z = _dot(ymix, wo_ref[...])
        rz = _rms(z)
        dz, dgrow = _rms_bwd(dx_ref[...], z * rz, rz, gp_ref[...])
        dgp_ref[...] += _colsum(dgrow)
        dzb = dz.astype(BF16)
        dwo_ref[...] += _dot_tn(ymix, dzb)
        dymix = _dot_nt(dzb, wo_ref[...])
        dhf_ref[:, 0:D_MODEL] = (dymix * yn * (sig * (1.0 + gate * (1.0 - sig)))).astype(BF16)
        dyn = dymix * silu
        dgg_ref[...] += _colsum(dyn * yhat)
        gg = gg_ref[...]
        dys = []
        for gi in range(4):
            sl = slice(gi * GROUP, (gi + 1) * GROUP)
            dyh = dyn[:, sl] * gg[:, sl]
            yh = yhat[:, sl]
            dys.append(rs[gi] * (dyh - yh * jnp.mean(dyh * yh, axis=-1, keepdims=True)))
        dya_ref[...] = dys[0]
        dyc_ref[...] = dys[2]
        dyd_ref[...] = dys[3]
        dyb = dys[1]
        dhf_ref[:, D_MODEL:D_MODEL + 256] = (dyb * conv).astype(BF16)
        dconv = dyb * bb
        dcb_ref[...] += _colsum(dconv)
        dcw_ref[0:1, :] += _colsum(dconv * u2)
        dcw_ref[1:2, :] += _colsum(dconv * u1)
        dcw_ref[2:3, :] += _colsum(dconv * u)
        carry = carry_ref[...]
        row = lax.broadcasted_iota(jnp.int32, (tm, 1), 0)
        d1 = jnp.where(row == tm - 1, carry[0:1, :], pltpu.roll(dconv, tm - 1, 0))
        d2 = jnp.where(row == tm - 2, carry[0:1, :],
                       jnp.where(row == tm - 1, carry[1:2, :], pltpu.roll(dconv, tm - 2, 0)))
        cw = cw_ref[...]
        du = cw[2:3, :] * dconv + cw[1:2, :] * d1 + cw[0:1, :] * d2
        dhf_ref[:, D_MODEL + 256:D_MODEL + 512] = (du * bx_ref[...]).astype(BF16)
        dhf_ref[:, D_MODEL + 512:D_MODEL + 768] = (du * bc_ref[...]).astype(BF16)
        carry_ref[...] = dconv[0:8, :]

    in_specs = [pl.BlockSpec((tm, D_MODEL), ridx)] + _ep_in_specs(tm, nt - 1)
    return pl.pallas_call(
        body, name="epilogue_bwd", grid=(nt,), in_specs=in_specs,
        out_specs=[pl.BlockSpec((tm, 256), ridx), pl.BlockSpec((tm, 256), ridx), pl.BlockSpec((tm, 256), ridx),
                   pl.BlockSpec((tm, D_MODEL + 768), ridx),
                   pl.BlockSpec((D_MODEL, D_MODEL), lambda i: (0, 0)),
                   pl.BlockSpec((1, D_MODEL), lambda i: (0, 0)),
                   pl.BlockSpec((1, D_MODEL), lambda i: (0, 0)),
                   pl.BlockSpec((8, 256), lambda i: (0, 0)),
                   pl.BlockSpec((1, 256), lambda i: (0, 0))],
        out_shape=[jax.ShapeDtypeStruct((T, 256), F32)] * 3
                  + [jax.ShapeDtypeStruct((T, D_MODEL + 768), BF16),
                     jax.ShapeDtypeStruct((D_MODEL, D_MODEL), F32),
                     jax.ShapeDtypeStruct((1, D_MODEL), F32),
                     jax.ShapeDtypeStruct((1, D_MODEL), F32),
                     jax.ShapeDtypeStruct((8, 256), F32),
                     jax.ShapeDtypeStruct((1, 256), F32)],
        scratch_shapes=[pltpu.VMEM((8, 256), F32)],
        compiler_params=_cparams(("arbitrary",)),
    )(dxn, ya, yc, yd, hf, hf, hf, hf, hf, hf, conv_w, conv_b, g_grp, w_out, g_post)


def _place():
    return lax.axis_index("x"), lax.axis_index("y"), lax.axis_index("c")


def _other_chips(x, y):
    return [(1 - x, y), (x, 1 - y), (1 - x, 1 - y)]


HBM = pl.BlockSpec(memory_space=pl.ANY)


def _gather_plan(ins, outs, sems):
    n = len(ins)
    ici_send, ici_recv, d2d_send, d2d_recv, local_sems = sems
    x, y, c = _place()
    me = 2 * x + y
    chips = _other_chips(x, y)

    def ici(a, j, chip_from):
        px, py = chips[j]
        return pltpu.make_async_remote_copy(
            src_ref=ins[a].at[c], dst_ref=outs[a].at[chip_from, c], send_sem=ici_send.at[3 * a + j],
            recv_sem=ici_recv.at[3 * a + j], device_id=(px, py, c), device_id_type=MESH)

    def d2d(a, j, part):
        px, py = chips[j]
        blk = outs[a].at[2 * px + py, part]
        return pltpu.make_async_remote_copy(
            src_ref=blk, dst_ref=blk, send_sem=d2d_send.at[3 * a + j], recv_sem=d2d_recv.at[3 * a + j],
            device_id=(x, y, 1 - c), device_id_type=MESH)

    def local(a):
        return pltpu.make_async_copy(ins[a], outs[a].at[me], local_sems.at[a])

    hops = [(j, a) for j in range(3) for a in range(n)]

    def start():
        for a in range(n):
            local(a).start()
        for j, a in hops:
            ici(a, j, me).start()

    def finish():
        for j, a in hops:
            ici(a, j, 2 * chips[j][0] + chips[j][1]).wait_recv()
            d2d(a, j, c).start()
        for j, a in hops:
            d2d(a, j, 1 - c).wait_recv()
        for j, a in hops:
            ici(a, j, me).wait_send()
            d2d(a, j, c).wait_send()
        for a in range(n):
            local(a).wait()

    return start, finish


def _gather_sems(n):
    return [pltpu.SemaphoreType.DMA((3 * n,))] * 4 + [pltpu.SemaphoreType.DMA((n,))]


def _gather_weights(shards):
    n = len(shards)

    def body(*refs):
        start, finish = _gather_plan(refs[:n], refs[n:2 * n], refs[2 * n:])
        start()
        finish()

    return pl.pallas_call(
        body, name="gather_weights",
        in_specs=[HBM] * n, out_specs=[HBM] * n,
        out_shape=[jax.ShapeDtypeStruct((4,) + s.shape, s.dtype) for s in shards],
        scratch_shapes=_gather_sems(n),
    )(*shards)


def _exchange_chips(parts, small):
    n = len(parts)

    def body(*refs):
        ins, sm_ref = refs[:n], refs[n]
        outs, osm_ref = refs[n + 1:2 * n + 1], refs[2 * n + 1]
        send_sems, recv_sems, ssend_sems, srecv_sems, local_sems = refs[2 * n + 2:]
        x, y, c = _place()
        me = 2 * x + y
        dev = 4 * x + 2 * y + c
        local = [pltpu.make_async_copy(ins[a].at[me], outs[a].at[me], local_sems.at[a]) for a in range(n)]
        local.append(pltpu.make_async_copy(sm_ref, osm_ref.at[dev], local_sems.at[n]))
        for cp in local:
            cp.start()
        sends = []
        for j, (px, py) in enumerate(_other_chips(x, y)):
            for a in range(n):
                cp = pltpu.make_async_remote_copy(
                    src_ref=ins[a].at[2 * px + py], dst_ref=outs[a].at[me], send_sem=send_sems.at[3 * a + j],
                    recv_sem=recv_sems.at[3 * a + j], device_id=(px, py, c), device_id_type=MESH)
                cp.start()
                sends.append(cp)
        flips = [(fx, fy, fc) for fx in (0, 1) for fy in (0, 1) for fc in (0, 1)][1:]
        for j, (fx, fy, fc) in enumerate(flips):
            cp = pltpu.make_async_remote_copy(
                src_ref=sm_ref, dst_ref=osm_ref.at[dev], send_sem=ssend_sems.at[j], recv_sem=srecv_sems.at[j],
                device_id=(x ^ fx, y ^ fy, c ^ fc), device_id_type=MESH)
            cp.start()
            sends.append(cp)
        for j, (px, py) in enumerate(_other_chips(x, y)):
            for a in range(n):
                pltpu.make_async_remote_copy(
                    src_ref=ins[a].at[me], dst_ref=outs[a].at[2 * px + py], send_sem=send_sems.at[3 * a + j],
                    recv_sem=recv_sems.at[3 * a + j], device_id=(px, py, c), device_id_type=MESH).wait_recv()
        for j, (fx, fy, fc) in enumerate(flips):
            src = 4 * (x ^ fx) + 2 * (y ^ fy) + (c ^ fc)
            pltpu.make_async_remote_copy(
                src_ref=sm_ref, dst_ref=osm_ref.at[src], send_sem=ssend_sems.at[j], recv_sem=srecv_sems.at[j],
                device_id=(x ^ fx, y ^ fy, c ^ fc), device_id_type=MESH).wait_recv()
        for cp in sends:
            cp.wait_send()
        for cp in local:
            cp.wait()

    return pl.pallas_call(
        body, name="exchange_chips",
        in_specs=[HBM] * (n + 1), out_specs=[HBM] * (n + 1),
        out_shape=[jax.ShapeDtypeStruct(p.shape, p.dtype) for p in parts]
                  + [jax.ShapeDtypeStruct((8,) + small.shape, small.dtype)],
        scratch_shapes=[pltpu.SemaphoreType.DMA((3 * n,)), pltpu.SemaphoreType.DMA((3 * n,)),
                        pltpu.SemaphoreType.DMA((7,)), pltpu.SemaphoreType.DMA((7,)),
                        pltpu.SemaphoreType.DMA((n + 1,))],
    )(*parts, small)


def _swap_cores(parts, name):
    n = len(parts)

    def body(*refs):
        ins, outs, send_sems, recv_sems = refs[:n], refs[n:2 * n], refs[2 * n], refs[2 * n + 1]
        x, y, c = _place()
        copies = [pltpu.make_async_remote_copy(
            src_ref=ins[a], dst_ref=outs[a], send_sem=send_sems.at[a], recv_sem=recv_sems.at[a],
            device_id=(x, y, 1 - c), device_id_type=MESH) for a in range(n)]
        for cp in copies:
            cp.start()
        for cp in copies:
            cp.wait()

    return pl.pallas_call(
        body, name=name, in_specs=[HBM] * n, out_specs=[HBM] * n,
        out_shape=[jax.ShapeDtypeStruct(p.shape, p.dtype) for p in parts],
        scratch_shapes=[pltpu.SemaphoreType.DMA((n,)), pltpu.SemaphoreType.DMA((n,))],
    )(*parts)


def _tile(rows, cols):
    for cand in (256, 128, 64):
        if rows % cand == 0:
            return cand, cols
    if rows > 64 and cols % 256 == 0:
        return rows, 256
    return rows, cols


def _add(a, b, name):
    L, R, C = a.shape
    tr, tc = _tile(R, C)

    def body(a_ref, b_ref, o_ref):
        o_ref[...] = (a_ref[...] + b_ref[...]).astype(BF16)

    spec = pl.BlockSpec((1, tr, tc), lambda l, i, j: (l, i, j))
    return pl.pallas_call(
        body, name=name, grid=(L, R // tr, C // tc), in_specs=[spec, spec], out_specs=spec,
        out_shape=jax.ShapeDtypeStruct((L, R, C), BF16),
        compiler_params=_cparams(("parallel", "parallel", "parallel")),
    )(a, b)


def _sum_leading(buf, name):
    n, R, C = buf.shape
    tr, tc = _tile(R, C)

    def body(b_ref, o_ref):
        acc = b_ref[0].astype(F32)
        for k in range(1, n):
            acc = acc + b_ref[k].astype(F32)
        o_ref[...] = acc

    return pl.pallas_call(
        body, name=name, grid=(R // tr, C // tc),
        in_specs=[pl.BlockSpec((n, tr, tc), lambda i, j: (0, i, j))],
        out_specs=pl.BlockSpec((tr, tc), lambda i, j: (i, j)),
        out_shape=jax.ShapeDtypeStruct((R, C), F32),
        compiler_params=_cparams(("parallel", "parallel")),
    )(buf)


def _adam_update(w, g, m, v):
    c1 = 1.0 / (1.0 - ADAM_B1 ** ADAM_STEP)
    c2 = 1.0 / (1.0 - ADAM_B2 ** ADAM_STEP)
    mn = ADAM_B1 * m + (1.0 - ADAM_B1) * g
    vn = ADAM_B2 * v + (1.0 - ADAM_B2) * (g * g)
    return -ADAM_LR * ((mn * c1) / (jnp.sqrt(vn * c2) + ADAM_EPS) + ADAM_WD * w), mn, vn


def _adamw_layers(w, m, v, g_mine, g_other, name):
    _, R, C = w.shape
    tr, tc = _tile(R, C)

    def body(w_ref, m_ref, v_ref, gm_ref, go_ref, g_ref, d_ref, mo_ref, vo_ref):
        g = jnp.where(pl.program_id(0) == lax.axis_index("c"), gm_ref[...], go_ref[...])
        g_ref[0] = g
        d_ref[0], mo_ref[0], vo_ref[0] = _adam_update(w_ref[0], g, m_ref[0], v_ref[0])

    spec3 = pl.BlockSpec((1, tr, tc), lambda l, i, j: (l, i, j))
    spec2 = pl.BlockSpec((tr, tc), lambda l, i, j: (i, j))
    return pl.pallas_call(
        body, name=name, grid=(2, R // tr, C // tc),
        in_specs=[spec3] * 3 + [spec2] * 2, out_specs=[spec3] * 4,
        out_shape=[jax.ShapeDtypeStruct(w.shape, F32)] * 4,
        compiler_params=_cparams(("parallel", "parallel", "parallel")),
    )(w, m, v, g_mine, g_other)


PACK_C = 1024
_BIG = ("w_in", "w_out", "mla_w_uq", "mla_w_ukv", "conv_w")
_SMALL = ("norm_pre", "group_norm", "norm_post", "conv_b", "mla_q_norm", "mla_kv_norm", "attn_sinks")
_SMALL_W = {"norm_pre": 1024, "group_norm": 1024, "norm_post": 1024, "conv_b": 256, "mla_q_norm": 256,
            "mla_kv_norm": 128, "attn_sinks": 4}


_LOSS_AT = divmod(DEPTH * sum(_SMALL_W.values()), PACK_C)


def _pack_small(d, loss):
    flat = jnp.concatenate([d[n].reshape(-1) for n in _SMALL] + [loss.reshape(1)])
    return jnp.pad(flat, (0, 8 * PACK_C - flat.shape[0])).reshape(8, PACK_C)


def _adamw_small(w, m, v, got):
    ns = len(_SMALL)

    def body(*refs):
        got_ref = refs[3 * ns]
        outs = refs[3 * ns + 1:]
        gsum = got_ref[0]
        for d in range(1, 8):
            gsum = gsum + got_ref[d]
        outs[4 * ns][...] = gsum[_LOSS_AT[0]:_LOSS_AT[0] + 1, _LOSS_AT[1]:_LOSS_AT[1] + 1]
        off = 0
        for i, name in enumerate(_SMALL):
            wd = _SMALL_W[name]
            rows = []
            for l in range(DEPTH):
                r, c0 = divmod(off + l * wd, PACK_C)
                rows.append(gsum[r:r + 1, c0:c0 + wd])
            off += DEPTH * wd
            g = jnp.concatenate(rows, axis=0)
            delta, mn, vn = _adam_update(refs[i][...], g, refs[ns + i][...], refs[2 * ns + i][...])
            outs[i][...] = g
            outs[ns + i][...] = delta
            outs[2 * ns + i][...] = mn
            outs[3 * ns + i][...] = vn

    shapes = [jax.ShapeDtypeStruct(w[n].shape, F32) for n in _SMALL]
    res = pl.pallas_call(body, name="adamw_small", out_shape=shapes * 4 + [jax.ShapeDtypeStruct((1, 1), F32)])(
        *[w[n] for n in _SMALL], *[m[n] for n in _SMALL], *[v[n] for n in _SMALL], got)
    return [dict(zip(_SMALL, res[k * ns:(k + 1) * ns])) for k in range(4)], res[4 * ns]


def _w_in_internal(wt):
    rows = []
    for n in _INT_ORDER:
        o, wd = _REAL_OFF[n]
        rows.append(wt[o:o + wd])
        if _INT_W[n] != wd:
            rows.append(jnp.zeros((_INT_W[n] - wd, wt.shape[1]), wt.dtype))
    return jnp.concatenate(rows, axis=0)


def _w_in_real(dwt):
    return jnp.concatenate([dwt[_INT_OFF[n]:_INT_OFF[n] + wd] for n, wd in _REAL], axis=0)


def _uq_internal(w):
    return jnp.pad(w.reshape(256, 4, 96), ((0, 0), (0, 0), (0, 32))).reshape(256, 512)


def _uq_real(dw):
    return dw.reshape(256, 4, 128)[:, :, :96].reshape(256, 384)


def _ukv_internal(w):
    w4 = w.reshape(128, 4, 128)
    k = jnp.pad(w4[:, :, :64], ((0, 0), (0, 0), (0, 64))).reshape(128, 512)
    return jnp.concatenate([k, w4[:, :, 64:].reshape(128, 256)], axis=1)


def _ukv_real(dw):
    k = dw[:, :512].reshape(128, 4, 128)[:, :, :64]
    v = dw[:, 512:].reshape(128, 4, 64)
    return jnp.concatenate([k, v], axis=2).reshape(128, 512)


def _layer_fwd(x, rope, p, tgt=None, fetch=()):
    xn, hb, hf, *fetched = _inproj_fwd(x, p["norm_pre"], p["w_in"], fetch)
    ya = _swa_fwd(hb, p["attn_sinks"])
    qm, km, vm, vt = _mla_prep_fwd(hf, rope, p["mla_q_norm"], p["mla_kv_norm"], p["mla_w_uq"], p["mla_w_ukv"])
    yc, lse = _mla_fwd(qm, km, vt)
    yd, tot, cnt = _sb_fwd(hb)
    x_next = _epilogue_fwd(x, ya, yc, yd, hf, p["conv_w"], p["conv_b"], p["group_norm"], p["w_out"], p["norm_post"],
                           tgt)
    saved = dict(x=x, xn=xn, hb=hb, hf=hf, ya=ya, yc=yc, yd=yd, tot=tot, cnt=cnt, qm=qm, km=km, vm=vm, lse=lse)
    return x_next, saved, fetched


def _layer_bwd(dx_next, rope, p, s):
    (dya, dyc, dyd, dhf, dw_out, dg_post, dg_grp, dconv_w, dconv_b) = _epilogue_bwd(
        dx_next, s["ya"], s["yc"], s["yd"], s["hf"], p["conv_w"], p["conv_b"], p["group_norm"], p["w_out"],
        p["norm_post"])
    dq_d, dk_d, dv_d = _sb_bwd(s["hb"], s["tot"], s["cnt"], dyd)
    dqm, dkt, dvt = _mla_bwd(s["qm"], s["km"], s["vm"], s["yc"], s["lse"], dyc)
    dc, dw_uq, dw_ukv, dg_q, dg_kv = _mla_prep_bwd(
        s["hf"], rope, p["mla_q_norm"], p["mla_kv_norm"], p["mla_w_uq"], p["mla_w_ukv"], dqm, dkt, dvt)
    dq_a, dk_a, dv_a, dsinks = _swa_bwd(s["hb"], p["attn_sinks"], dya)
    dx, dh, dg_pre = _inproj_bwd_dx(s["x"], p["norm_pre"], p["w_in"], dx_next,
                                    [dq_a, dk_a, dv_a, dq_d, dk_d, dv_d, dhf, dc])
    dwt_in = _grad_over_tokens(s["xn"], dh, "inproj_bwd_dw")
    grads = dict(norm_pre=dg_pre[0], w_in_t=_w_in_real(dwt_in), attn_sinks=dsinks[0, :4], conv_w=dconv_w[:3],
                 conv_b=dconv_b[0], mla_q_norm=dg_q[0], mla_w_uq=_uq_real(dw_uq), mla_kv_norm=dg_kv[0],
                 mla_w_ukv=_ukv_real(dw_ukv), group_norm=dg_grp[0], w_out=dw_out, norm_post=dg_post[0])
    return dx, grads


_WEIGHTS = ["norm_pre", "w_in", "attn_sinks", "conv_w", "conv_b", "mla_q_norm", "mla_w_uq", "mla_kv_norm",
            "mla_w_ukv", "group_norm", "w_out", "norm_post"]


def kernel(x, positions, norm_pre, w_in, attn_sinks, conv_w, conv_b, mla_q_norm, mla_w_uq, mla_kv_norm, mla_w_ukv, group_norm, w_out, norm_post, loss_target, m_norm_pre, m_w_in, m_attn_sinks, m_conv_w, m_conv_b, m_mla_q_norm, m_mla_w_uq, m_mla_kv_norm, m_mla_w_ukv, m_group_norm, m_w_out, m_norm_post, v_norm_pre, v_w_in, v_attn_sinks, v_conv_w, v_conv_b, v_mla_q_norm, v_mla_w_uq, v_mla_kv_norm, v_mla_w_ukv, v_group_norm, v_w_out, v_norm_post):
    w = dict(norm_pre=norm_pre, w_in=w_in, attn_sinks=attn_sinks, conv_w=conv_w, conv_b=conv_b,
             mla_q_norm=mla_q_norm, mla_w_uq=mla_w_uq, mla_kv_norm=mla_kv_norm, mla_w_ukv=mla_w_ukv,
             group_norm=group_norm, w_out=w_out, norm_post=norm_post)
    m = dict(norm_pre=m_norm_pre, w_in=m_w_in, attn_sinks=m_attn_sinks, conv_w=m_conv_w, conv_b=m_conv_b,
             mla_q_norm=m_mla_q_norm, mla_w_uq=m_mla_w_uq, mla_kv_norm=m_mla_kv_norm, mla_w_ukv=m_mla_w_ukv,
             group_norm=m_group_norm, w_out=m_w_out, norm_post=m_norm_post)
    v = dict(norm_pre=v_norm_pre, w_in=v_w_in, attn_sinks=v_attn_sinks, conv_w=v_conv_w, conv_b=v_conv_b,
             mla_q_norm=v_mla_q_norm, mla_w_uq=v_mla_w_uq, mla_kv_norm=v_mla_kv_norm, mla_w_ukv=v_mla_w_ukv,
             group_norm=v_group_norm, w_out=v_w_out, norm_post=v_norm_post)
    T = x.shape[1]
    xs = x[0]
    rope = _rope_tables(positions[0].reshape(T, 1))
    tgt = loss_target[0]
    core = lax.axis_index("c")

    def shard_parts(l):
        halves = lambda a: a.reshape((2, a.shape[0] // 2) + a.shape[1:])
        return [halves(jnp.swapaxes(w["w_in"][l], 0, 1).astype(BF16))] + [
            halves(w[n][l].astype(BF16)) for n in _BIG[1:4]] + [jnp.stack([w["conv_w"][l]] * 2)]

    def layer_params(l, got):
        whole = lambda a: a.reshape((4, 2 * a.shape[2]) + a.shape[3:])
        by_cols = lambda a: jnp.transpose(a, (1, 0, 2)).reshape(a.shape[1], 4 * a.shape[2])
        return dict(
            norm_pre=norm_pre[l:l + 1], w_in=_w_in_internal(whole(got[0]).reshape(D_IN, D_MODEL)),
            attn_sinks=attn_sinks[l], conv_w=by_cols(got[4][:, 0]), conv_b=conv_b[l:l + 1],
            mla_q_norm=mla_q_norm[l:l + 1], mla_w_uq=_uq_internal(by_cols(whole(got[2]))),
            mla_kv_norm=mla_kv_norm[l:l + 1], mla_w_ukv=_ukv_internal(by_cols(whole(got[3]))),
            group_norm=group_norm[l:l + 1], w_out=whole(got[1]).reshape(D_MODEL, D_MODEL),
            norm_post=norm_post[l:l + 1])

    layers, saved = [], []
    h, got = xs, _gather_weights(shard_parts(0))
    for l in range(DEPTH):
        last = l == DEPTH - 1
        layers.append(layer_params(l, got))
        h, s, got = _layer_fwd(h, rope, layers[l], tgt if last else None, () if last else shard_parts(l + 1))
        saved.append(s)
    dy, loss_part = h

    grads = [None] * DEPTH
    for l in reversed(range(DEPTH)):
        dy, grads[l] = _layer_bwd(dy, rope, layers[l], saved[l])

    turned = ("w_in", "mla_w_uq")
    turn = lambda n, a: jnp.swapaxes(a, -1, -2) if n in turned else a

    def chunks(n, a):
        if n in ("w_out", "w_in"):
            return a.reshape(4, a.shape[0] // 4, a.shape[1])
        if n in turned:
            return a.T.reshape(4, a.shape[1] // 4, a.shape[0])
        return jnp.transpose(a.reshape(a.shape[0], 4, a.shape[1] // 4), (1, 0, 2))

    grad = lambda l, n: grads[l]["w_in_t" if n == "w_in" else n]
    mine = [chunks(n, jnp.where(core == 0, grad(0, n), grad(1, n))) for n in _BIG]
    theirs = [chunks(n, jnp.where(core == 0, grad(1, n), grad(0, n))) for n in _BIG]
    from_sibling = _swap_cores(theirs, "swap_layer_chunks")
    summed = [_add(a, b, "add_cores_" + n) for n, a, b in zip(_BIG, mine, from_sibling)]
    small = _pack_small({n: jnp.stack([grads[l][n] for l in range(DEPTH)]) for n in _SMALL}, loss_part[0, 0])
    *got, got_small = _exchange_chips(summed, small)
    done = [_sum_leading(b, "sum_chips_" + n) for n, b in zip(_BIG, got)]
    done_other = _swap_cores(done, "swap_layer_shards")

    outs, loss = _adamw_small(w, m, v, got_small)
    for n, gm, go in zip(_BIG, done, done_other):
        for d, a in zip(outs, _adamw_layers(turn(n, w[n]), turn(n, m[n]), turn(n, v[n]), gm, go, "adamw_" + n)):
            d[n] = turn(n, a)
    return (loss[0, 0], dy[None], *[outs[0][n] for n in _WEIGHTS], *[outs[1][n] for n in _WEIGHTS],
            *[outs[2][n] for n in _WEIGHTS], *[outs[3][n] for n in _WEIGHTS])
```

```python
import math

import jax
import jax.numpy as jnp
from jax import lax
from jax.experimental import pallas as pl
from jax.experimental.pallas import tpu as pltpu

F32 = jnp.float32
BF16 = jnp.bfloat16
MESH = pl.DeviceIdType.MESH

D_MODEL = 1024
DEPTH = 2
EPS = 1e-6
BLOCK = 128
HEAD = 64
LANES = 128
GROUP = 256
LOG2E = 1.4426950408889634
LN2 = 0.6931471805599453
MLA_QSCALE = 96 ** -0.5 * LOG2E
ROPE_HALF = 16
ROPE_THETA = 10000.0
SWA_SUB = 2
ATT_BLK = 256
MLA_BQ = 512
NEG = -1e30
SB_DEAD = -104.0

ADAM_LR, ADAM_B1, ADAM_B2, ADAM_EPS, ADAM_WD, ADAM_STEP = 0.001, 0.9, 0.999, 1e-08, 0.01, 10

_REAL = [("a_q", 256), ("a_k", 128), ("a_v", 128), ("b_b", 256), ("b_c", 256), ("b_x", 256),
         ("c_q", 256), ("c_kv", 128), ("c_kr", 32), ("d_q", 256), ("d_k", 256), ("d_v", 256),
         ("gate", 1024)]
_REAL_OFF = {}
_o = 0
for _n, _w in _REAL:
    _REAL_OFF[_n] = (_o, _w)
    _o += _w
D_IN = _o
_INT_ORDER = ["a_q", "a_k", "a_v", "d_q", "d_k", "d_v", "gate", "b_b", "b_c", "b_x", "c_q", "c_kv", "c_kr"]
_INT_W = dict(_REAL)
_INT_W["c_kr"] = 128
_INT_OFF = {}
_o = 0
for _n in _INT_ORDER:
    _INT_OFF[_n] = _o
    _o += _INT_W[_n]
N_INT = _o
N_HB = _INT_OFF["gate"]
N_HF = N_INT - N_HB

VMEM_LIMIT = 56 * 1024 * 1024


def _cparams(sem):
    return pltpu.CompilerParams(dimension_semantics=sem, vmem_limit_bytes=VMEM_LIMIT)


def _dot(a, b):
    return jnp.dot(a, b, preferred_element_type=F32)


def _dot_nt(a, b):
    return lax.dot_general(a, b, (((1,), (1,)), ((), ())), preferred_element_type=F32)


def _dot_tn(a, b):
    return lax.dot_general(a, b, (((0,), (0,)), ((), ())), preferred_element_type=F32)


def _split(x):
    hi = x.astype(BF16)
    lo = (x - hi.astype(F32)).astype(BF16)
    return hi, lo


def _rms(x):
    return lax.rsqrt(jnp.mean(x * x, axis=-1, keepdims=True) + EPS)


def _rms_bwd(dy, xhat, r, g):
    dxhat = dy * g
    return r * (dxhat - xhat * jnp.mean(dxhat * xhat, axis=-1, keepdims=True)), dy * xhat


def _colsum(x):
    return jnp.sum(x, axis=0, keepdims=True)


def _inproj_fwd(x, g, wt, fetch=()):
    T = x.shape[0]
    tm = 512
    nt, n = T // tm, len(fetch)

    def body(x_ref, g_ref, w_ref, *rest):
        xn_ref, hb_ref, hf_ref = rest[n:n + 3]
        if n:
            start, finish = _gather_plan(rest[:n], rest[n + 3:2 * n + 3], rest[2 * n + 3:])
            pl.when(pl.program_id(0) == 0)(start)
        xv = x_ref[...]
        xn = (xv * _rms(xv) * g_ref[...]).astype(BF16)
        xn_ref[...] = xn
        h = _dot_nt(xn, w_ref[...])
        hb_ref[...] = h[:, :N_HB].astype(BF16)
        hf_ref[...] = h[:, N_HB:]
        if n:
            pl.when(pl.program_id(0) == nt - 1)(finish)

    return pl.pallas_call(
        body, name="inproj_fwd_fetch" if n else "inproj_fwd", grid=(nt,),
        in_specs=[pl.BlockSpec((tm, D_MODEL), lambda i: (i, 0)),
                  pl.BlockSpec((1, D_MODEL), lambda i: (0, 0)),
                  pl.BlockSpec((N_INT, D_MODEL), lambda i: (0, 0))] + [HBM] * n,
        out_specs=[pl.BlockSpec((tm, D_MODEL), lambda i: (i, 0)),
                   pl.BlockSpec((tm, N_HB), lambda i: (i, 0)),
                   pl.BlockSpec((tm, N_HF), lambda i: (i, 0))] + [HBM] * n,
        out_shape=[jax.ShapeDtypeStruct((T, D_MODEL), BF16),
                   jax.ShapeDtypeStruct((T, N_HB), BF16),
                   jax.ShapeDtypeStruct((T, N_HF), F32)]
                  + [jax.ShapeDtypeStruct((4,) + s.shape, s.dtype) for s in fetch],
        scratch_shapes=_gather_sems(n) if n else [],
        compiler_params=_cparams(("arbitrary",) if n else ("parallel",)),
    )(x, g, wt, *fetch)


def _inproj_bwd_dx(x, g, wt, dx_next, pieces):
    T = x.shape[0]
    tm = 512
    widths = [p.shape[1] for p in pieces]
    assert sum(widths) == N_INT

    def body(x_ref, g_ref, w_ref, dxn_ref, *rest):
        p_refs = rest[:len(pieces)]
        dx_ref, dh_ref, dg_ref = rest[len(pieces):]
        dh = jnp.concatenate([p[...].astype(BF16) for p in p_refs], axis=1)
        dh_ref[...] = dh
        dxn = _dot(dh, w_ref[...])
        xv = x_ref[...]
        r = _rms(xv)
        dx, dgrow = _rms_bwd(dxn, xv * r, r, g_ref[...])
        dx_ref[...] = dx + dxn_ref[...]

        @pl.when(pl.program_id(0) == 0)
        def _():
            dg_ref[...] = jnp.zeros_like(dg_ref)

        dg_ref[...] += _colsum(dgrow)

    return pl.pallas_call(
        body, name="inproj_bwd_dx", grid=(T // tm,),
        in_specs=[pl.BlockSpec((tm, D_MODEL), lambda i: (i, 0)),
                  pl.BlockSpec((1, D_MODEL), lambda i: (0, 0)),
                  pl.BlockSpec((N_INT, D_MODEL), lambda i: (0, 0)),
                  pl.BlockSpec((tm, D_MODEL), lambda i: (i, 0))]
                 + [pl.BlockSpec((tm, wd), lambda i: (i, 0)) for wd in widths],
        out_specs=[pl.BlockSpec((tm, D_MODEL), lambda i: (i, 0)),
                   pl.BlockSpec((tm, N_INT), lambda i: (i, 0)),
                   pl.BlockSpec((1, D_MODEL), lambda i: (0, 0))],
        out_shape=[jax.ShapeDtypeStruct((T, D_MODEL), F32),
                   jax.ShapeDtypeStruct((T, N_INT), BF16),
                   jax.ShapeDtypeStruct((1, D_MODEL), F32)],
        compiler_params=_cparams(("arbitrary",)),
    )(x, g, wt, dx_next, *pieces)


def _inproj_bwd_dw(xn, dh):
    T = xn.shape[0]
    tm = min(512, T)

    def body(a_ref, b_ref, o_ref):
        @pl.when(pl.program_id(0) == 0)
        def _():
            o_ref[...] = jnp.zeros_like(o_ref)

        for n, wd in _REAL:
            o, oi = _REAL_OFF[n][0], _INT_OFF[n]
            o_ref[o:o + wd, :] += _dot_tn(b_ref[:, oi:oi + _INT_W[n]], a_ref[...])[:wd]

    return pl.pallas_call(
        body, name="inproj_bwd_dw", grid=(T // tm,),
        in_specs=[pl.BlockSpec((tm, D_MODEL), lambda t: (t, 0)),
                  pl.BlockSpec((tm, N_INT), lambda t: (t, 0))],
        out_specs=pl.BlockSpec((D_IN, D_MODEL), lambda t: (0, 0)),
        out_shape=jax.ShapeDtypeStruct((D_IN, D_MODEL), F32),
        compiler_params=_cparams(("arbitrary",)),
    )(xn, dh)


def _roll_f32(x, shift):
    return pltpu.roll(x.astype(F32), shift, 1)


def _swa_operands(h, q, k_prev, k_cur, v_prev, v_cur):
    p, e = h // 2, h % 2
    lane = lax.broadcasted_iota(jnp.int32, (1, LANES), 1) // HEAD
    q = q[:, p * LANES:(p + 1) * LANES]
    if e != p:
        q = _roll_f32(q, HEAD).astype(BF16)
        v_prev = _roll_f32(v_prev, HEAD).astype(BF16)
        v_cur = _roll_f32(v_cur, HEAD).astype(BF16)
    qs = jnp.where(lane == p, q, 0) * 0.125
    return dict(p=p, e=e, lane=lane, qs=qs, k_prev=k_prev, k_cur=k_cur,
                v_prev=jnp.where(lane == e, v_prev, 0), v_cur=jnp.where(lane == e, v_cur, 0),
                s_prev=_dot_nt(qs, k_prev), s_cur=_dot_nt(qs, k_cur))


def _swa_probs(ops, sink, no_prev):
    row = lax.broadcasted_iota(jnp.int32, (BLOCK, BLOCK), 0)
    col = lax.broadcasted_iota(jnp.int32, (BLOCK, BLOCK), 1)
    ok_prev = col > row if no_prev is None else jnp.logical_and(col > row, jnp.logical_not(no_prev))
    s_prev = jnp.where(ok_prev, ops["s_prev"], NEG)
    s_cur = jnp.where(col <= row, ops["s_cur"], NEG)
    m = jnp.maximum(jnp.maximum(jnp.max(s_prev, axis=1, keepdims=True),
                                jnp.max(s_cur, axis=1, keepdims=True)), sink)
    p_prev = jnp.exp(s_prev - m)
    p_cur = jnp.exp(s_cur - m)
    p_sink = jnp.exp(sink - m)
    inv = 1.0 / (jnp.sum(p_prev, axis=1, keepdims=True) + jnp.sum(p_cur, axis=1, keepdims=True) + p_sink)
    return p_prev * inv, p_cur * inv, p_sink * inv


def _swa_specs(T):
    n = T // (BLOCK * SWA_SUB)
    qo, ko, vo = (_INT_OFF[name] // LANES for name in ("a_q", "a_k", "a_v"))
    halo = lambda i: jnp.maximum(i * SWA_SUB - 1, 0)
    return [pl.BlockSpec((BLOCK * SWA_SUB, 256), lambda i: (i, qo // 2)),
            pl.BlockSpec((BLOCK, LANES), lambda i: (halo(i), ko)),
            pl.BlockSpec((BLOCK * SWA_SUB, LANES), lambda i: (i, ko)),
            pl.BlockSpec((BLOCK, LANES), lambda i: (halo(i), vo)),
            pl.BlockSpec((BLOCK * SWA_SUB, LANES), lambda i: (i, vo)),
            pl.BlockSpec(memory_space=pltpu.SMEM)], n


def _swa_units(q_ref, kh_ref, kc_ref, vh_ref, vc_ref, s_ref):
    blk = lambda a: slice(a * BLOCK, (a + 1) * BLOCK)
    units = [(a, h) for a in range(SWA_SUB) for h in range(4)]
    ops = {}
    for a, h in units:
        k_prev, v_prev = (kh_ref[...], vh_ref[...]) if a == 0 else (kc_ref[blk(a - 1), :], vc_ref[blk(a - 1), :])
        ops[a, h] = _swa_operands(h, q_ref[blk(a), :], k_prev, kc_ref[blk(a), :], v_prev, vc_ref[blk(a), :])
    probs = {(a, h): _swa_probs(ops[a, h], s_ref[h], pl.program_id(0) == 0 if a == 0 else None) for a, h in units}
    return units, ops, probs, blk


def _swa_fwd(hb, sinks):
    T = hb.shape[0]
    specs, n = _swa_specs(T)

    def body(q_ref, kh_ref, kc_ref, vh_ref, vc_ref, s_ref, o_ref):
        units, ops, probs, blk = _swa_units(q_ref, kh_ref, kc_ref, vh_ref, vc_ref, s_ref)
        outs = {u: _dot(probs[u][0].astype(BF16), ops[u]["v_prev"]) + _dot(probs[u][1].astype(BF16), ops[u]["v_cur"])
                for u in units}
        for a in range(SWA_SUB):
            for p in range(2):
                o_ref[blk(a), p * LANES:(p + 1) * LANES] = outs[a, 2 * p] + outs[a, 2 * p + 1]

    return pl.pallas_call(
        body, name="swa_fwd", grid=(n,), in_specs=specs,
        out_specs=pl.BlockSpec((BLOCK * SWA_SUB, 256), lambda i: (i, 0)),
        out_shape=jax.ShapeDtypeStruct((T, 256), F32),
        compiler_params=_cparams(("parallel",)),
    )(hb, hb, hb, hb, hb, sinks)


def _swa_bwd(hb, sinks, dy):
    T = hb.shape[0]
    specs, n = _swa_specs(T)

    def body(q_ref, kh_ref, kc_ref, vh_ref, vc_ref, s_ref, dy_ref, dq_ref, dk_ref, dv_ref, ds_ref):
        i = pl.program_id(0)

        @pl.when(i == 0)
        def _():
            ds_ref[...] = jnp.zeros_like(ds_ref)

        lane_id = lax.broadcasted_iota(jnp.int32, (8, LANES), 1)
        units, ops, probs, blk = _swa_units(q_ref, kh_ref, kc_ref, vh_ref, vc_ref, s_ref)
        dos = {(a, h): jnp.where(ops[a, h]["lane"] == ops[a, h]["e"],
                                 dy_ref[blk(a), ops[a, h]["p"] * LANES:(ops[a, h]["p"] + 1) * LANES], 0.0)
               for a, h in units}
        dobs = {u: dos[u].astype(BF16) for u in units}
        pbs = {u: (probs[u][0].astype(BF16), probs[u][1].astype(BF16)) for u in units}
        outs = {u: _dot(pbs[u][0], ops[u]["v_prev"]) + _dot(pbs[u][1], ops[u]["v_cur"]) for u in units}
        dps = {u: (_dot_nt(dobs[u], ops[u]["v_prev"]), _dot_nt(dobs[u], ops[u]["v_cur"])) for u in units}
        dss, dsinks = {}, jnp.zeros((8, LANES), F32)
        for u in units:
            delta = jnp.sum(dos[u] * outs[u], axis=1, keepdims=True)
            dss[u] = ((probs[u][0] * (dps[u][0] - delta)).astype(BF16),
                      (probs[u][1] * (dps[u][1] - delta)).astype(BF16))
            dsink = -jnp.sum(probs[u][2] * delta, axis=0, keepdims=True)
            dsinks += jnp.where(lane_id == u[1], dsink, 0.0)
        ds_ref[...] += dsinks
        dqs = {u: (_dot(dss[u][0], ops[u]["k_prev"]) + _dot(dss[u][1], ops[u]["k_cur"])) * 0.125 for u in units}
        zero = jnp.zeros((BLOCK, LANES), F32)
        dk_as_prev, dk_as_cur = [zero] * SWA_SUB, [zero] * SWA_SUB
        dv_as_prev, dv_as_cur = [zero] * SWA_SUB, [zero] * SWA_SUB
        for a, h in units:
            p, e = ops[a, h]["p"], ops[a, h]["e"]
            dob_v = dobs[a, h] if e == p else pltpu.roll(dos[a, h], HEAD, 1).astype(BF16)
            dk_as_prev[a] = dk_as_prev[a] + _dot_tn(dss[a, h][0], ops[a, h]["qs"])
            dk_as_cur[a] = dk_as_cur[a] + _dot_tn(dss[a, h][1], ops[a, h]["qs"])
            dv_as_prev[a] = dv_as_prev[a] + _dot_tn(pbs[a, h][0], dob_v)
            dv_as_cur[a] = dv_as_cur[a] + _dot_tn(pbs[a, h][1], dob_v)
        base = i * SWA_SUB
        for a in range(SWA_SUB):
            rows = pl.ds(pl.multiple_of((base + a) * BLOCK, BLOCK), BLOCK)
            more = a + 1 < SWA_SUB
            dk_ref[rows, :] = dk_as_cur[a] + (dk_as_prev[a + 1] if more else 0.0)
            dv_ref[rows, :] = dv_as_cur[a] + (dv_as_prev[a + 1] if more else 0.0)
        halo = pl.ds(pl.multiple_of(jnp.maximum(base - 1, 0) * BLOCK, BLOCK), BLOCK)
        dk_ref[halo, :] += dk_as_prev[0]
        dv_ref[halo, :] += dv_as_prev[0]
        for a in range(SWA_SUB):
            for p in range(2):
                dq_pair = jnp.zeros((BLOCK, LANES), F32)
                for e in range(2):
                    dq = jnp.where(ops[a, 2 * p + e]["lane"] == p, dqs[a, 2 * p + e], 0.0)
                    dq_pair += dq if e == p else pltpu.roll(dq, HEAD, 1)
                dq_ref[blk(a), p * LANES:(p + 1) * LANES] = dq_pair.astype(BF16)

    return pl.pallas_call(
        body, name="swa_bwd", grid=(n,),
        in_specs=specs + [pl.BlockSpec((BLOCK * SWA_SUB, 256), lambda i: (i, 0))],
        out_specs=[pl.BlockSpec((BLOCK * SWA_SUB, 256), lambda i: (i, 0)),
                   pl.BlockSpec((T, LANES), lambda i: (0, 0)),
                   pl.BlockSpec((T, LANES), lambda i: (0, 0)),
                   pl.BlockSpec((8, LANES), lambda i: (0, 0))],
        out_shape=[jax.ShapeDtypeStruct((T, 256), BF16),
                   jax.ShapeDtypeStruct((T, LANES), F32),
                   jax.ShapeDtypeStruct((T, LANES), F32),
                   jax.ShapeDtypeStruct((8, LANES), F32)],
        compiler_params=_cparams(("arbitrary",)),
    )(hb, hb, hb, hb, hb, sinks, dy)


def _rope_tables(pos):
    T = pos.shape[0]
    tm = 512

    def body(pos_ref, o_ref):
        lane = lax.broadcasted_iota(jnp.int32, (1, LANES), 1)
        active = jnp.logical_and(lane >= HEAD, lane < HEAD + 2 * ROPE_HALF)
        idx = ((lane - HEAD) % ROPE_HALF).astype(F32)
        freq = jnp.exp(idx * (-math.log(ROPE_THETA) / ROPE_HALF))
        ang = pos_ref[...].astype(F32) * freq
        cos, sin = jnp.cos(ang), jnp.sin(ang)
        o_ref[:, 0:LANES] = jnp.where(active, cos, 1.0)
        o_ref[:, LANES:2 * LANES] = jnp.where(jnp.logical_and(active, lane >= HEAD + ROPE_HALF), sin, 0.0)
        o_ref[:, 2 * LANES:] = jnp.where(jnp.logical_and(active, lane < HEAD + ROPE_HALF), -sin, 0.0)

    return pl.pallas_call(
        body, name="rope_tables", grid=(T // tm,),
        in_specs=[pl.BlockSpec((tm, 1), lambda i: (i, 0))],
        out_specs=pl.BlockSpec((tm, 3 * LANES), lambda i: (i, 0)),
        out_shape=jax.ShapeDtypeStruct((T, 3 * LANES), F32),
        compiler_params=_cparams(("parallel",)),
    )(pos)


def _rope_factors(tab_ref):
    return tab_ref[:, 0:LANES], tab_ref[:, LANES:2 * LANES], tab_ref[:, 2 * LANES:]


def _rope(x, tabs):
    c, s_up, s_dn = tabs
    return x * c + pltpu.roll(x, ROPE_HALF, 1) * s_up + pltpu.roll(x, LANES - ROPE_HALF, 1) * s_dn


def _rope_t(dy, tabs):
    c, s_up, s_dn = tabs
    return dy * c + pltpu.roll(dy * s_up, LANES - ROPE_HALF, 1) + pltpu.roll(dy * s_dn, ROPE_HALF, 1)


def _mla_lat_specs(tm):
    cq, ckv, ckr = ((_INT_OFF[n] - N_HB) for n in ("c_q", "c_kv", "c_kr"))
    return [pl.BlockSpec((tm, 256), lambda i: (i, cq // 256)),
            pl.BlockSpec((tm, LANES), lambda i: (i, ckv // LANES)),
            pl.BlockSpec((tm, LANES), lambda i: (i, ckr // LANES)),
            pl.BlockSpec((tm, 3 * LANES), lambda i: (i, 0)),
            pl.BlockSpec((1, 256), lambda i: (0, 0)),
            pl.BlockSpec((1, LANES), lambda i: (0, 0)),
            pl.BlockSpec((256, 512), lambda i: (0, 0)),
            pl.BlockSpec((LANES, 768), lambda i: (0, 0))]


def _mla_prep_fwd(hf, rope, g_q, g_kv, w_uq, w_ukv):
    T = hf.shape[0]
    tm = 512
    sub = tm // ATT_BLK

    def body(cq_ref, ckv_ref, ckr_ref, tab_ref, gq_ref, gkv_ref, wq_ref, wkv_ref, qm_ref, km_ref, vm_ref, vt_ref):
        tabs = _rope_factors(tab_ref)
        cq = cq_ref[...]
        q = _dot((cq * _rms(cq) * gq_ref[...]).astype(BF16), wq_ref[...])
        ckv = ckv_ref[...]
        kv = _dot((ckv * _rms(ckv) * gkv_ref[...]).astype(BF16), wkv_ref[...])
        kr = _rope(pltpu.roll(ckr_ref[...], HEAD, 1), tabs)
        for h in range(4):
            sl = slice(h * LANES, (h + 1) * LANES)
            qm_ref[:, sl] = (_rope(q[:, sl], tabs) * MLA_QSCALE).astype(BF16)
            km_ref[:, sl] = (kv[:, sl] + kr).astype(BF16)
        vm_ref[...] = kv[:, 512:].astype(BF16)
        for p in range(2):
            for s in range(sub):
                tile = kv[s * ATT_BLK:(s + 1) * ATT_BLK, 512 + p * LANES:512 + (p + 1) * LANES]
                vt_ref[p, s] = jnp.transpose(tile).astype(BF16)

    return pl.pallas_call(
        body, name="mla_prep_fwd", grid=(T // tm,), in_specs=_mla_lat_specs(tm),
        out_specs=[pl.BlockSpec((tm, 512), lambda i: (i, 0)),
                   pl.BlockSpec((tm, 512), lambda i: (i, 0)),
                   pl.BlockSpec((tm, 256), lambda i: (i, 0)),
                   pl.BlockSpec((2, sub, LANES, ATT_BLK), lambda i: (0, i, 0, 0))],
        out_shape=[jax.ShapeDtypeStruct((T, 512), BF16),
                   jax.ShapeDtypeStruct((T, 512), BF16),
                   jax.ShapeDtypeStruct((T, 256), BF16),
                   jax.ShapeDtypeStruct((2, T // ATT_BLK, LANES, ATT_BLK), BF16)],
        compiler_params=_cparams(("parallel",)),
    )(hf, hf, hf, rope, g_q, g_kv, w_uq, w_ukv)


def _mla_prep_bwd(hf, rope, g_q, g_kv, w_uq, w_ukv, dqm, dkt, dvt):
    T = hf.shape[0]
    tm = 512
    sub = tm // ATT_BLK

    def body(cq_ref, ckv_ref, ckr_ref, tab_ref, gq_ref, gkv_ref, wq_ref, wkv_ref, dq_ref, dk_ref, dv_ref,
             dc_ref, dwq_ref, dwkv_ref, dgq_ref, dgkv_ref):
        @pl.when(pl.program_id(0) == 0)
        def _():
            dwq_ref[...] = jnp.zeros_like(dwq_ref)
            dwkv_ref[...] = jnp.zeros_like(dwkv_ref)
            dgq_ref[...] = jnp.zeros_like(dgq_ref)
            dgkv_ref[...] = jnp.zeros_like(dgkv_ref)

        tabs = _rope_factors(tab_ref)
        lane =lax.broadcasted_iota(jnp.int32, (1, LANES), 1)
        dq = jnp.concatenate([_rope_t(dq_ref[:, h * LANES:(h + 1) * LANES] * MLA_QSCALE, tabs)
                              for h in range(4)], axis=1).astype(BF16)
        cq = cq_ref[...]
        rq = _rms(cq)
        cqn = (cq * rq * gq_ref[...]).astype(BF16)
        dwq_ref[...] += _dot_tn(cqn, dq)
        dcq, dgrow = _rms_bwd(_dot_nt(dq, wq_ref[...]), cq * rq, rq, gq_ref[...])
        dgq_ref[...] += _colsum(dgrow)
        dc_ref[:, 0:256] = dcq.astype(BF16)

        dk = jnp.concatenate([jnp.concatenate([jnp.transpose(dk_ref[p, s]) for p in range(2)], axis=1)
                              for s in range(sub)], axis=0) * LN2
        dv = jnp.concatenate([jnp.concatenate([jnp.transpose(dv_ref[p, s]) for p in range(2)], axis=1)
                              for s in range(sub)], axis=0)
        dkr = dk[:, 0:LANES] + dk[:, LANES:2 * LANES] + dk[:, 2 * LANES:3 * LANES] + dk[:, 3 * LANES:]
        dkr = pltpu.roll(_rope_t(dkr, tabs), HEAD, 1)
        dc_ref[:, 384:512] = jnp.where(lane < 2 * ROPE_HALF, dkr, 0.0).astype(BF16)
        dkv = jnp.concatenate([dk.astype(BF16), dv.astype(BF16)], axis=1)
        ckv = ckv_ref[...]
        rkv = _rms(ckv)
        ckvn = (ckv * rkv * gkv_ref[...]).astype(BF16)
        dwkv_ref[...] += _dot_tn(ckvn, dkv)
        dckv, dgrow = _rms_bwd(_dot_nt(dkv, wkv_ref[...]), ckv * rkv, rkv, gkv_ref[...])
        dgkv_ref[...] += _colsum(dgrow)
        dc_ref[:, 256:384] = dckv.astype(BF16)

    return pl.pallas_call(
        body, name="mla_prep_bwd", grid=(T // tm,),
        in_specs=_mla_lat_specs(tm) + [pl.BlockSpec((tm, 512), lambda i: (i, 0)),
                                       pl.BlockSpec((2, sub, 256, ATT_BLK), lambda i: (0, i, 0, 0)),
                                       pl.BlockSpec((2, sub, LANES, ATT_BLK), lambda i: (0, i, 0, 0))],
        out_specs=[pl.BlockSpec((tm, 512), lambda i: (i, 0)),
                   pl.BlockSpec((256, 512), lambda i: (0, 0)),
                   pl.BlockSpec((LANES, 768), lambda i: (0, 0)),
                   pl.BlockSpec((1, 256), lambda i: (0, 0)),
                   pl.BlockSpec((1, LANES), lambda i: (0, 0))],
        out_shape=[jax.ShapeDtypeStruct((T, 512), BF16),
                   jax.ShapeDtypeStruct((256, 512), F32),
                   jax.ShapeDtypeStruct((LANES, 768), F32),
                   jax.ShapeDtypeStruct((1, 256), F32),
                   jax.ShapeDtypeStruct((1, LANES), F32)],
        compiler_params=_cparams(("arbitrary",)),
    )(hf, hf, hf, rope, g_q, g_kv, w_uq, w_ukv, dqm, dkt, dvt)


def _causal_masks(bq, bk):
    row = lax.broadcasted_iota(jnp.int32, (bq, bk), 0)
    col = lax.broadcasted_iota(jnp.int32, (bq, bk), 1)
    return row, col


def _mla_fwd(qm, km, vt):
    T = qm.shape[0]
    bq, bk = min(MLA_BQ, T), ATT_BLK
    nq, nsub, nk = T // bq, bq // bk, T // bk

    def body(q_ref, k_ref, vt_ref, o_ref, lse_ref, acc_ref, m_ref, l_ref):
        qi = pl.program_id(0)
        key = lax.broadcasted_iota(jnp.int32, (bk, bq), 0)
        qry = lax.broadcasted_iota(jnp.int32, (bk, bq), 1)
        ones = jnp.ones((8, bk), BF16)
        acc_ref[...] = jnp.zeros_like(acc_ref)
        m_ref[...] = jnp.full_like(m_ref, NEG)
        l_ref[...] = jnp.zeros_like(l_ref)

        def step(kb0, masked):
            kbs = [kb0 + d for d in range(nsub)]
            qs = [slice(d * bk if masked else 0, bq) for d in range(nsub)]

            def wide(a, d, fill):
                if not qs[d].start:
                    return a
                return jnp.concatenate([jnp.full((a.shape[0], qs[d].start), fill, a.dtype), a], axis=1)

            sts = [[_dot_nt(k_ref[pl.ds(pl.multiple_of(kb * bk, bk), bk), e * LANES:(e + 1) * LANES],
                            q_ref[qs[d], e * LANES:(e + 1) * LANES]) for d, kb in enumerate(kbs)] for e in range(4)]
            pts, alphas = [], []
            for e in range(4):
                st = ([jnp.where(key[:, qs[d]] + d * bk <= qry[:, qs[d]], sts[e][d], NEG) for d in range(nsub)]
                      if masked else sts[e])
                m_prev = m_ref[e, 0:1, :]
                m_new = m_prev
                for d in range(nsub):
                    m_new = jnp.maximum(m_new, wide(jnp.max(st[d], axis=0, keepdims=True), d, NEG))
                alpha = jnp.exp2(m_prev - m_new)
                pt = [jnp.exp2(st[d] - m_new[:, qs[d]]).astype(BF16) for d in range(nsub)]
                l_new = alpha * l_ref[e]
                for d in range(nsub):
                    l_new = l_new + wide(_dot(ones, pt[d]), d, 0.0)
                l_ref[e] = l_new
                m_ref[e] = jnp.broadcast_to(m_new, (8, bq))
                pts.append(pt)
                alphas.append(alpha)
            for e in range(4):
                acc = alphas[e] * acc_ref[e]
                for d in range(nsub):
                    v_t = vt_ref[e // 2, kbs[d], (e % 2) * HEAD:(e % 2 + 1) * HEAD, :]
                    acc = acc + wide(_dot(v_t, pts[e][d]), d, 0.0)
                acc_ref[e] = acc

        step(qi * nsub, True)

        def loop(t, c):
            step(t * nsub, False)
            return c

        lax.fori_loop(0, qi, loop, 0)
        outs, lses = [], []
        for e in range(4):
            l = l_ref[e, 0:1, :]
            outs.append(acc_ref[e] / l)
            lses.append(jnp.broadcast_to(m_ref[e, 0:1, :] * LN2 + jnp.log(l), (HEAD, bq)))
        o_ref[...] = jnp.transpose(jnp.concatenate(outs, axis=0))
        lse_ref[...] = jnp.transpose(jnp.concatenate(lses, axis=0))

    return pl.pallas_call(
        body, name="mla_fwd", grid=(nq,),
        in_specs=[pl.BlockSpec((bq, 512), lambda i: (i, 0)),
                  pl.BlockSpec((T, 512), lambda i: (0, 0)),
                  pl.BlockSpec((2, nk, LANES, bk), lambda i: (0, 0, 0, 0))],
        out_specs=[pl.BlockSpec((bq, 256), lambda i: (i, 0)),
                   pl.BlockSpec((bq, 256), lambda i: (i, 0))],
        out_shape=[jax.ShapeDtypeStruct((T, 256), F32), jax.ShapeDtypeStruct((T, 256), F32)],
        scratch_shapes=[pltpu.VMEM((4, HEAD, bq), F32), pltpu.VMEM((4, 8, bq), F32), pltpu.VMEM((4, 8, bq), F32)],
        compiler_params=_cparams(("arbitrary",)),
    )(qm, km, vt)


def _mla_bwd(qm, km, vm, y, lse, dy):
    T = qm.shape[0]
    bq, bk = min(MLA_BQ, T), ATT_BLK
    nq, nsub, nk = T // bq, bq // bk, T // bk

    def body(q_ref, k_ref, v_ref, y_ref, lse_ref, dy_ref, dq_ref, dkt_ref, dvt_ref, dob_ref, st_ref, qt_ref, dot_ref):
        qi = pl.program_id(1)

        @pl.when(qi == 0)
        def _():
            dkt_ref[...] = jnp.zeros_like(dkt_ref)
            dvt_ref[...] = jnp.zeros_like(dvt_ref)

        lane = lax.broadcasted_iota(jnp.int32, (1, LANES), 1) // HEAD
        row, col = _causal_masks(bq, bk)
        dq_ref[...] = jnp.zeros_like(dq_ref)
        lse = lse_ref[...]
        lse_other = pltpu.roll(lse, HEAD, 1)
        qt_ref[...] = jnp.transpose(q_ref[...].astype(F32)).astype(BF16)
        dot_ref[...] = jnp.transpose(dy_ref[...]).astype(BF16)
        for e in range(2):
            do = jnp.where(lane == e, dy_ref[...], 0.0)
            dob_ref[e] = do.astype(BF16)
            st_ref[2 * e] = jnp.where(lane == e, lse, lse_other) * LOG2E
            st_ref[2 * e + 1] = jnp.broadcast_to(jnp.sum(do * y_ref[...], axis=1, keepdims=True), (bq, LANES))

        hss = [slice(e * LANES, (e + 1) * LANES) for e in range(2)]
        tile = lambda a: jnp.concatenate([a] * (bk // LANES), axis=1)

        def step(kb0, masked):
            kbs = [kb0 + d for d in range(nsub)]
            rows = [pl.ds(pl.multiple_of(kb * bk, bk), bk) for kb in kbs]
            pairs = [(d, e) for d in range(nsub) for e in range(2)]
            qs = [slice(d * bk if masked else 0, bq) for d in range(nsub)]
            ss = {(d, e): _dot_nt(q_ref[qs[d], hss[e]], k_ref[rows[d], hss[e]]) for d, e in pairs}
            dps = {(d, e): _dot_nt(dob_ref[e, qs[d], :], jnp.where(lane == e, v_ref[rows[d], :], 0))
                   for d, e in pairs}
            ps, dss = {}, {}
            for d, e in pairs:
                s = jnp.where(col[qs[d]] + d * bk <= row[qs[d]], ss[d, e], NEG) if masked else ss[d, e]
                p = jnp.exp2(s - tile(st_ref[2 * e, qs[d], :]))
                dss[d, e] = (p * (dps[d, e] - tile(st_ref[2 * e + 1, qs[d], :]))).astype(BF16)
                ps[d, e] = p.astype(BF16)
            for d, e in pairs:
                dvt_ref[0, kbs[d], e * HEAD:(e + 1) * HEAD, :] += _dot(
                    dot_ref[e * HEAD:(e + 1) * HEAD, qs[d]], ps[d, e])
            for d, e in pairs:
                dkt_ref[0, kbs[d], hss[e], :] += _dot(qt_ref[hss[e], qs[d]], dss[d, e])
            for e in range(2):
                if masked:
                    for d in range(nsub):
                        dq_ref[qs[d], hss[e]] += _dot(dss[d, e], k_ref[rows[d], hss[e]])
                else:
                    dq = dq_ref[:, hss[e]]
                    for d in range(nsub):
                        dq = dq + _dot(dss[d, e], k_ref[rows[d], hss[e]])
                    dq_ref[:, hss[e]] = dq

        step(qi * nsub, True)

        def loop(t, c):
            step(t * nsub, False)
            return c

        lax.fori_loop(0, qi, loop, 0)
        dq_ref[...] *= LN2

    return pl.pallas_call(
        body, name="mla_bwd", grid=(2, nq),
        in_specs=[pl.BlockSpec((bq, 256), lambda j, i: (i, j)),
                  pl.BlockSpec((T, 256), lambda j, i: (0, j)),
                  pl.BlockSpec((T, LANES), lambda j, i: (0, j)),
                  pl.BlockSpec((bq, LANES), lambda j, i: (i, j)),
                  pl.BlockSpec((bq, LANES), lambda j, i: (i, j)),
                  pl.BlockSpec((bq, LANES), lambda j, i: (i, j))],
        out_specs=[pl.BlockSpec((bq, 256), lambda j, i: (i, j)),
                   pl.BlockSpec((1, nk, 256, bk), lambda j, i: (j, 0, 0, 0)),
                   pl.BlockSpec((1, nk, LANES, bk), lambda j, i: (j, 0, 0, 0))],
        out_shape=[jax.ShapeDtypeStruct((T, 512), F32),
                   jax.ShapeDtypeStruct((2, nk, 256, bk), F32),
                   jax.ShapeDtypeStruct((2, nk, LANES, bk), F32)],
        scratch_shapes=[pltpu.VMEM((2, bq, LANES), BF16), pltpu.VMEM((4, bq, LANES), F32),
                        pltpu.VMEM((256, bq), BF16), pltpu.VMEM((LANES, bq), BF16)],
        compiler_params=_cparams(("parallel", "arbitrary")),
    )(qm, km, vm, y, lse, dy)


def _suffix_ones(n):
    r = lax.broadcasted_iota(jnp.int32, (n, n), 0)
    c = lax.broadcasted_iota(jnp.int32, (n, n), 1)
    return (r >= c).astype(BF16)


def _prefix_ones(n):
    r = lax.broadcasted_iota(jnp.int32, (n, n), 0)
    c = lax.broadcasted_iota(jnp.int32, (n, n), 1)
    return (r <= c).astype(BF16)


def _sb_specs(T, bq):
    qo, ko, vo = (_INT_OFF[n] // 256 for n in ("d_q", "d_k", "d_v"))
    return [pl.BlockSpec((bq, 256), lambda i: (i, qo)),
            pl.BlockSpec((T, 256), lambda i: (0, ko)),
            pl.BlockSpec((T, 256), lambda i: (0, vo))]


def _sb_fwd(hb):
    T = hb.shape[0]
    bq = bk = ATT_BLK
    nq = T // bq

    def body(q_ref, k_ref, v_ref, o_ref, tot_ref, cnt_ref, qm_ref, car_ref):
        qi = pl.program_id(0)
        lane = lax.broadcasted_iota(jnp.int32, (1, LANES), 1) // HEAD
        row, col = _causal_masks(bq, bk)
        strict = col < row
        u = _suffix_ones(bk)
        o_ref[...] = jnp.zeros_like(o_ref)
        car_ref[...] = jnp.zeros_like(car_ref)
        pair = lambda h: slice((h // 2) * LANES, (h // 2 + 1) * LANES)
        for h in range(4):
            qm_ref[h] = jnp.where(lane == h % 2, q_ref[:, pair(h)], 0) * 0.125

        def step(blocks):
            tile = lambda a: jnp.concatenate([a] * (bk // LANES), axis=1)
            rows = [pl.ds(pl.multiple_of(kb * bk, bk), bk) for kb, _ in blocks]
            pairs = [(b, h) for b in range(len(blocks)) for h in range(4)]
            zs = {(b, h): _dot_nt(qm_ref[h], k_ref[rows[b], pair(h)]) for b, h in pairs}
            splits = {}
            for b, h in pairs:
                z = zs[b, h]
                lk = jnp.minimum(-z, 0.0) - jnp.log(1.0 + jnp.exp(-jnp.abs(z)))
                if blocks[b][1] is not None:
                    lk = jnp.where(blocks[b][1], lk, 0.0)
                splits[b, h] = _split(lk)
            sufs = {bh: _dot(hi, u) + _dot(lo, u) for bh, (hi, lo) in splits.items()}
            car = [car_ref[h] for h in range(4)]
            aas = {}
            for b, h in pairs:
                a = jnp.exp(zs[b, h] + sufs[b, h] + tile(car[h]))
                if blocks[b][1] is not None:
                    a = jnp.where(blocks[b][1], a, 0.0)
                aas[b, h] = a.astype(BF16)
                car[h] = car[h] + jnp.broadcast_to(sufs[b, h][:, 0:1], (bq, LANES))
            acc = [o_ref[:, pair(0)], o_ref[:, pair(2)]]
            for b, h in pairs:
                acc[h // 2] = acc[h // 2] + _dot(aas[b, h], jnp.where(lane == h % 2, v_ref[rows[b], pair(h)], 0))
            o_ref[:, pair(0)], o_ref[:, pair(2)] = acc
            for h in range(4):
                car_ref[h] = car[h]

        step([(qi, strict), (jnp.maximum(qi - 1, 0), qi > 0)])

        def live():
            worst = jnp.maximum(jnp.maximum(car_ref[0], car_ref[1]), jnp.maximum(car_ref[2], car_ref[3]))
            return jnp.max(worst) >= SB_DEAD

        def cond(c):
            return jnp.logical_and(c[0] < qi, c[1])

        def loop(c):
            step([(qi - 1 - c[0], None)])
            return c[0] + 1, live()

        done, _ = lax.while_loop(cond, loop, (jnp.minimum(qi, 1), live()))
        tot_ref[:, pair(0)] = jnp.where(lane == 0, car_ref[0], car_ref[1])
        tot_ref[:, pair(2)] = jnp.where(lane == 0, car_ref[2], car_ref[3])
        cnt_ref[0, qi] = done.astype(F32)

    return pl.pallas_call(
        body, name="sb_fwd", grid=(nq,), in_specs=_sb_specs(T, bq),
        out_specs=[pl.BlockSpec((bq, 256), lambda i: (i, 0)), pl.BlockSpec((bq, 256), lambda i: (i, 0)),
                   pl.BlockSpec(memory_space=pltpu.SMEM)],
        out_shape=[jax.ShapeDtypeStruct((T, 256), F32), jax.ShapeDtypeStruct((T, 256), F32),
                   jax.ShapeDtypeStruct((1, nq), F32)],
        scratch_shapes=[pltpu.VMEM((4, bq, LANES), BF16), pltpu.VMEM((4, bq, LANES), F32)],
        compiler_params=_cparams(("arbitrary",)),
    )(hb, hb, hb)


def _sb_bwd(hb, tot, cnt, dy):
    T = hb.shape[0]
    bq = bk = ATT_BLK
    nq = T // bq

    def body(q_ref, k_ref, v_ref, tot_ref, dy_ref, cnt_ref, dq_ref, dk_ref, dv_ref, qm_ref, dob_ref, dqa_ref, rem_ref,
             cg_ref):
        qi = pl.program_id(0)

        @pl.when(qi == 0)
        def _():
            dk_ref[...] = jnp.zeros_like(dk_ref)
            dv_ref[...] = jnp.zeros_like(dv_ref)

        lane = lax.broadcasted_iota(jnp.int32, (1, LANES), 1) // HEAD
        row, col = _causal_masks(bq, bk)
        strict = col < row
        u = _prefix_ones(bk)
        pair = lambda h: slice((h // 2) * LANES, (h // 2 + 1) * LANES)
        dqa_ref[...] = jnp.zeros_like(dqa_ref)
        cg_ref[...] = jnp.zeros_like(cg_ref)
        for h in range(4):
            tot = tot_ref[:, pair(h)]
            qm_ref[h] = jnp.where(lane == h % 2, q_ref[:, pair(h)], 0) * 0.125
            dob_ref[h] = jnp.where(lane == h % 2, dy_ref[:, pair(h)], 0.0).astype(BF16)
            rem_ref[h] = jnp.where(lane == h % 2, tot, pltpu.roll(tot, HEAD, 1))

        def step(blocks):
            tile = lambda a: jnp.concatenate([a] * (bk // LANES), axis=1)
            nb = len(blocks)
            rows = [pl.ds(pl.multiple_of(kb * bk, bk), bk) for kb, _ in blocks]
            pairs = [(b, h) for b in range(nb) for h in range(4)]
            mask = lambda b, x: x if blocks[b][1] is None else jnp.where(blocks[b][1], x, 0.0)
            zs = {(b, h): _dot_nt(qm_ref[h], k_ref[rows[b], pair(h)]) for b, h in pairs}
            das = {(b, h): _dot_nt(dob_ref[h], jnp.where(lane == h % 2, v_ref[rows[b], pair(h)], 0)) for b, h in pairs}
            zls, splits = {}, {}
            for b, h in pairs:
                z = zs[b, h]
                lk = mask(b, jnp.minimum(-z, 0.0) - jnp.log(1.0 + jnp.exp(-jnp.abs(z))))
                zls[b, h] = z + lk
                splits[b, h] = _split(lk)
            pres = {bh: _dot(hi, u) + _dot(lo, u) for bh, (hi, lo) in splits.items()}
            rem = [rem_ref[h] for h in range(4)]
            aas, gs, gsplits = {}, {}, {}
            for b, h in pairs:
                a = mask(b, jnp.exp(zls[b, h] + (tile(rem[h]) - pres[b, h])))
                gs[b, h] = a * das[b, h]
                aas[b, h] = a.astype(BF16)
                gsplits[b, h] = _split(gs[b, h])
                rem[h] = rem[h] - jnp.broadcast_to(pres[b, h][:, bk - 1:bk], (bq, LANES))
            for b in range(nb):
                for p in (0, 2):
                    dv_ref[rows[b], pair(p)] += _dot_tn(aas[b, p], dob_ref[p]) + _dot_tn(aas[b, p + 1], dob_ref[p + 1])
            gpres = {bh: _dot(hi, u) + _dot(lo, u) for bh, (hi, lo) in gsplits.items()}
            cg = [cg_ref[h] for h in range(4)]
            dzs = {}
            for b, h in pairs:
                dz = mask(b, gs[b, h] - jnp.exp(zls[b, h]) * (tile(cg[h]) + gpres[b, h]))
                dzs[b, h] = dz.astype(BF16)
                cg[h] = cg[h] + jnp.broadcast_to(gpres[b, h][:, bk - 1:bk], (bq, LANES))
            for b in range(nb):
                for p in (0, 2):
                    dk_ref[rows[b], pair(p)] += _dot_tn(dzs[b, p], qm_ref[p]) + _dot_tn(dzs[b, p + 1], qm_ref[p + 1])
            for h in range(4):
                dq = dqa_ref[h]
                for b in range(nb):
                    dq = dq + _dot(dzs[b, h], k_ref[rows[b], pair(h)])
                dqa_ref[h] = dq
                rem_ref[h] = rem[h]
                cg_ref[h] = cg[h]

        def loop(kb, c):
            step([(kb, None)])
            return c

        start = qi - jnp.clip(cnt_ref[0, qi].astype(jnp.int32), 0, qi)
        lax.fori_loop(start, qi - 1, loop, 0)
        step([(jnp.maximum(qi - 1, 0), qi > 0), (qi, strict)])
        for p in (0, 2):
            dq_ref[:, pair(p)] = (jnp.where(lane == 0, dqa_ref[p], dqa_ref[p + 1]) * 0.125).astype(BF16)

    return pl.pallas_call(
        body, name="sb_bwd", grid=(nq,),
        in_specs=_sb_specs(T, bq) + [pl.BlockSpec((bq, 256), lambda i: (i, 0)),
                                     pl.BlockSpec((bq, 256), lambda i: (i, 0)),
                                     pl.BlockSpec(memory_space=pltpu.SMEM)],
        out_specs=[pl.BlockSpec((bq, 256), lambda i: (i, 0)),
                   pl.BlockSpec((T, 256), lambda i: (0, 0)),
                   pl.BlockSpec((T, 256), lambda i: (0, 0))],
        out_shape=[jax.ShapeDtypeStruct((T, 256), BF16)] + [jax.ShapeDtypeStruct((T, 256), F32)] * 2,
        scratch_shapes=[pltpu.VMEM((4, bq, LANES), BF16), pltpu.VMEM((4, bq, LANES), BF16),
                        pltpu.VMEM((4, bq, LANES), F32), pltpu.VMEM((4, bq, LANES), F32),
                        pltpu.VMEM((4, bq, LANES), F32)],
        compiler_params=_cparams(("arbitrary",)),
    )(hb, hb, hb, tot, dy, cnt)


EP_TM = 512


def _ep_in_specs(tm, rev):
    idx = (lambda i: rev - i) if rev is not None else (lambda i: i)
    bo = (_INT_OFF["b_b"] - N_HB) // 256
    halo = lambda i: jnp.maximum(idx(i) * (tm // 8) - 1, 0)
    return [pl.BlockSpec((tm, 256), lambda i: (idx(i), 0)),
            pl.BlockSpec((tm, 256), lambda i: (idx(i), 0)),
            pl.BlockSpec((tm, 256), lambda i: (idx(i), 0)),
            pl.BlockSpec((tm, D_MODEL), lambda i: (idx(i), 0)),
            pl.BlockSpec((tm, 256), lambda i: (idx(i), bo)),
            pl.BlockSpec((tm, 256), lambda i: (idx(i), bo + 1)),
            pl.BlockSpec((tm, 256), lambda i: (idx(i), bo + 2)),
            pl.BlockSpec((8, 256), lambda i: (halo(i), bo + 1)),
            pl.BlockSpec((8, 256), lambda i: (halo(i), bo + 2)),
            pl.BlockSpec((3, 256), lambda i: (0, 0)),
            pl.BlockSpec((1, 256), lambda i: (0, 0)),
            pl.BlockSpec((1, D_MODEL), lambda i: (0, 0)),
            pl.BlockSpec((D_MODEL, D_MODEL), lambda i: (0, 0)),
            pl.BlockSpec((1, D_MODEL), lambda i: (0, 0))]


def _ep_mix(first, ya_ref, yc_ref, yd_ref, gate_ref, bb_ref, bc_ref, bx_ref, hc_ref, hx_ref, cw_ref, cb_ref, gg_ref):
    tm = ya_ref.shape[0]
    u = bc_ref[...] * bx_ref[...]
    halo = jnp.where(first, 0.0, hc_ref[...] * hx_ref[...])
    row = lax.broadcasted_iota(jnp.int32, (tm, 1), 0)
    u1 = jnp.where(row == 0, halo[7:8, :], pltpu.roll(u, 1, 0))
    u2 = jnp.where(row == 0, halo[6:7, :], jnp.where(row == 1, halo[7:8, :], pltpu.roll(u, 2, 0)))
    cw = cw_ref[...]
    conv = cw[0:1, :] * u2 + cw[1:2, :] * u1 + cw[2:3, :] * u + cb_ref[...]
    bb = bb_ref[...]
    ys = [ya_ref[...], bb * conv, yc_ref[...], yd_ref[...]]
    rs = [_rms(y) for y in ys]
    gg = gg_ref[...]
    yhat = jnp.concatenate([y * r for y, r in zip(ys, rs)], axis=1)
    gate = gate_ref[...]
    sig = 1.0 / (1.0 + jnp.exp(-gate))
    return u, u1, u2, conv, bb, rs, yhat, yhat * gg, gate, sig


def _epilogue_fwd(x, ya, yc, yd, hf, conv_w, conv_b, g_grp, w_out, g_post, tgt=None):
    T = x.shape[0]
    tm = EP_TM
    row_spec = pl.BlockSpec((tm, D_MODEL), lambda i: (i, 0))

    def layer_out(refs):
        (x_ref, ya_ref, yc_ref, yd_ref, gate_ref, bb_ref, bc_ref, bx_ref, hc_ref, hx_ref, cw_ref, cb_ref,
         gg_ref, wo_ref, gp_ref) = refs
        (_, _, _, _, _, _, _, yn, gate, sig) = _ep_mix(
            pl.program_id(0) == 0, ya_ref, yc_ref, yd_ref, gate_ref, bb_ref, bc_ref, bx_ref, hc_ref, hx_ref,
            cw_ref, cb_ref, gg_ref)
        z = _dot((yn * (gate * sig)).astype(BF16), wo_ref[...])
        return x_ref[...] + z * _rms(z) * gp_ref[...]

    args = (x, ya, yc, yd, hf, hf, hf, hf, hf, hf, conv_w, conv_b, g_grp, w_out, g_post)
    in_specs = [row_spec] + _ep_in_specs(tm, None)
    if tgt is None:
        def body(*refs):
            refs[-1][...] = layer_out(refs[:-1])

        return pl.pallas_call(
            body, name="epilogue_fwd", grid=(T // tm,), in_specs=in_specs, out_specs=row_spec,
            out_shape=jax.ShapeDtypeStruct((T, D_MODEL), F32), compiler_params=_cparams(("parallel",)),
        )(*args)

    def body_loss(*refs):
        t_ref, dy_ref, l_ref = refs[-3:]

        @pl.when(pl.program_id(0) == 0)
        def _():
            l_ref[...] = jnp.zeros_like(l_ref)

        d = layer_out(refs[:-3]) - t_ref[...]
        dy_ref[...] = d * (1.0 / D_MODEL)
        part = jnp.sum(jnp.sum(d * d, axis=1, keepdims=True), axis=0, keepdims=True)
        l_ref[...] += part * (0.5 / D_MODEL)

    return pl.pallas_call(
        body_loss, name="epilogue_fwd_loss", grid=(T // tm,), in_specs=in_specs + [row_spec],
        out_specs=[row_spec, pl.BlockSpec((8, LANES), lambda i: (0, 0))],
        out_shape=[jax.ShapeDtypeStruct((T, D_MODEL), F32), jax.ShapeDtypeStruct((8, LANES), F32)],
        compiler_params=_cparams(("arbitrary",)),
    )(*args, tgt)


def _epilogue_bwd(dxn, ya, yc, yd, hf, conv_w, conv_b, g_grp, w_out, g_post):
    T = dxn.shape[0]
    tm = EP_TM
    nt = T // tm
    ridx = lambda i: (nt - 1 - i, 0)

    def body(dx_ref, ya_ref, yc_ref, yd_ref, gate_ref, bb_ref, bc_ref, bx_ref, hc_ref, hx_ref, cw_ref, cb_ref,
             gg_ref, wo_ref, gp_ref,
             dya_ref, dyc_ref, dyd_ref, dhf_ref, dwo_ref, dgp_ref, dgg_ref, dcw_ref, dcb_ref, carry_ref):
        i = pl.program_id(0)

        @pl.when(i == 0)
        def _():
            for r in (dwo_ref, dgp_ref, dgg_ref, dcw_ref, dcb_ref, carry_ref):
                r[...] = jnp.zeros_like(r)

        (u, u1, u2, conv, bb, rs, yhat, yn, gate, sig) = _ep_mix(
            i == nt - 1, ya_ref, yc_ref, yd_ref, gate_ref, bb_ref, bc_ref, bx_ref, hc_ref, hx_ref,
            cw_ref, cb_ref, gg_ref)
        silu = gate * sig
        ymix = (yn * silu).astype(BF16)
        z = _dot(ymix, wo_ref[...])
        rz = _rms(z)
        dz, dgrow = _rms_bwd(dx_ref[...], z * rz, rz, gp_ref[...])
        dgp_ref[...] += _colsum(dgrow)
        dzb = dz.astype(BF16)
        dwo_ref[...] += _dot_tn(ymix, dzb)
        dymix = _dot_nt(dzb, wo_ref[...])
        dhf_ref[:, 0:D_MODEL] = (dymix * yn * (sig * (1.0 + gate * (1.0 - sig)))).astype(BF16)
        dyn = dymix * silu
        dgg_ref[...] += _colsum(dyn * yhat)
        gg = gg_ref[...]
        dys = []
        for gi in range(4):
            sl = slice(gi * GROUP, (gi + 1) * GROUP)
            dyh = dyn[:, sl] * gg[:, sl]
            yh = yhat[:, sl]
            dys.append(rs[gi] * (dyh - yh * jnp.mean(dyh * yh, axis=-1, keepdims=True)))
        dya_ref[...] = dys[0]
        dyc_ref[...] = dys[2]
        dyd_ref[...] = dys[3]
        dyb = dys[1]
        dhf_ref[:, D_MODEL:D_MODEL + 256] = (dyb * conv).astype(BF16)
        dconv = dyb * bb
        dcb_ref[...] += _colsum(dconv)
        dcw_ref[0:1, :] += _colsum(dconv * u2)
        dcw_ref[1:2, :] += _colsum(dconv * u1)
        dcw_ref[2:3, :] += _colsum(dconv * u)
        carry = carry_ref[...]
        row = lax.broadcasted_iota(jnp.int32, (tm, 1), 0)
        d1 = jnp.where(row == tm - 1, carry[0:1, :], pltpu.roll(dconv, tm - 1, 0))
        d2 = jnp.where(row == tm - 2, carry[0:1, :],
                       jnp.where(row == tm - 1, carry[1:2, :], pltpu.roll(dconv, tm - 2, 0)))
        cw = cw_ref[...]
        du = cw[2:3, :] * dconv + cw[1:2, :] * d1 + cw[0:1, :] * d2
        dhf_ref[:, D_MODEL + 256:D_MODEL + 512] = (du * bx_ref[...]).astype(BF16)
        dhf_ref[:, D_MODEL + 512:D_MODEL + 768] = (du * bc_ref[...]).astype(BF16)
        carry_ref[...] = dconv[0:8, :]

    in_specs = [pl.BlockSpec((tm, D_MODEL), ridx)] + _ep_in_specs(tm, nt - 1)
    return pl.pallas_call(
        body, name="epilogue_bwd", grid=(nt,), in_specs=in_specs,
        out_specs=[pl.BlockSpec((tm, 256), ridx), pl.BlockSpec((tm, 256), ridx), pl.BlockSpec((tm, 256), ridx),
                   pl.BlockSpec((tm, D_MODEL + 768), ridx),
                   pl.BlockSpec((D_MODEL, D_MODEL), lambda i: (0, 0)),
                   pl.BlockSpec((1, D_MODEL), lambda i: (0, 0)),
                   pl.BlockSpec((1, D_MODEL), lambda i: (0, 0)),
                   pl.BlockSpec((8, 256), lambda i: (0, 0)),
                   pl.BlockSpec((1, 256), lambda i: (0, 0))],
        out_shape=[jax.ShapeDtypeStruct((T, 256), F32)] * 3
                  + [jax.ShapeDtypeStruct((T, D_MODEL + 768), BF16),
                     jax.ShapeDtypeStruct((D_MODEL, D_MODEL), F32),
                     jax.ShapeDtypeStruct((1, D_MODEL), F32),
                     jax.ShapeDtypeStruct((1, D_MODEL), F32),
                     jax.ShapeDtypeStruct((8, 256), F32),
                     jax.ShapeDtypeStruct((1, 256), F32)],
        scratch_shapes=[pltpu.VMEM((8, 256), F32)],
        compiler_params=_cparams(("arbitrary",)),
    )(dxn, ya, yc, yd, hf, hf, hf, hf, hf, hf, conv_w, conv_b, g_grp, w_out, g_post)


def _place():
    return lax.axis_index("x"), lax.axis_index("y"), lax.axis_index("c")


def _other_chips(x, y):
    return [(1 - x, y), (x, 1 - y), (1 - x, 1 - y)]


HBM = pl.BlockSpec(memory_space=pl.ANY)


def _gather_plan(ins, outs, sems):
    n = len(ins)
    ici_send, ici_recv, d2d_send, d2d_recv, local_sems = sems
    x, y, c = _place()
    me = 2 * x + y
    chips = _other_chips(x, y)

    def ici(a, j, chip_from):
        px, py = chips[j]
        return pltpu.make_async_remote_copy(
            src_ref=ins[a].at[c], dst_ref=outs[a].at[chip_from, c], send_sem=ici_send.at[3 * a + j],
            recv_sem=ici_recv.at[3 * a + j], device_id=(px, py, c), device_id_type=MESH)

    def d2d(a, j, part):
        px, py = chips[j]
        blk = outs[a].at[2 * px + py, part]
        return pltpu.make_async_remote_copy(
            src_ref=blk, dst_ref=blk, send_sem=d2d_send.at[3 * a + j], recv_sem=d2d_recv.at[3 * a + j],
            device_id=(x, y, 1 - c), device_id_type=MESH)

    def local(a):
        return pltpu.make_async_copy(ins[a], outs[a].at[me], local_sems.at[a])

    hops = [(j, a) for j in range(3) for a in range(n)]

    def start():
        for a in range(n):
            local(a).start()
        for j, a in hops:
            ici(a, j, me).start()

    def finish():
        for j, a in hops:
            ici(a, j, 2 * chips[j][0] + chips[j][1]).wait_recv()
            d2d(a, j, c).start()
        for j, a in hops:
            d2d(a, j, 1 - c).wait_recv()
        for j, a in hops:
            ici(a, j, me).wait_send()
            d2d(a, j, c).wait_send()
        for a in range(n):
            local(a).wait()

    return start, finish


def _gather_sems(n):
    return [pltpu.SemaphoreType.DMA((3 * n,))] * 4 + [pltpu.SemaphoreType.DMA((n,))]


def _gather_weights(shards):
    n = len(shards)

    def body(*refs):
        start, finish = _gather_plan(refs[:n], refs[n:2 * n], refs[2 * n:])
        start()
        finish()

    return pl.pallas_call(
        body, name="gather_weights",
        in_specs=[HBM] * n, out_specs=[HBM] * n,
        out_shape=[jax.ShapeDtypeStruct((4,) + s.shape, s.dtype) for s in shards],
        scratch_shapes=_gather_sems(n),
    )(*shards)


def _exchange_chips(parts, small):
    n = len(parts)

    def body(*refs):
        ins, sm_ref = refs[:n], refs[n]
        outs, osm_ref = refs[n + 1:2 * n + 1], refs[2 * n + 1]
        send_sems, recv_sems, ssend_sems, srecv_sems, local_sems = refs[2 * n + 2:]
        x, y, c = _place()
        me = 2 * x + y
        dev = 4 * x + 2 * y + c
        local = [pltpu.make_async_copy(ins[a].at[me], outs[a].at[me], local_sems.at[a]) for a in range(n)]
        local.append(pltpu.make_async_copy(sm_ref, osm_ref.at[dev], local_sems.at[n]))
        for cp in local:
            cp.start()
        sends = []
        for j, (px, py) in enumerate(_other_chips(x, y)):
            for a in range(n):
                cp = pltpu.make_async_remote_copy(
                    src_ref=ins[a].at[2 * px + py], dst_ref=outs[a].at[me], send_sem=send_sems.at[3 * a + j],
                    recv_sem=recv_sems.at[3 * a + j], device_id=(px, py, c), device_id_type=MESH)
                cp.start()
                sends.append(cp)
        flips = [(fx, fy, fc) for fx in (0, 1) for fy in (0, 1) for fc in (0, 1)][1:]
        for j, (fx, fy, fc) in enumerate(flips):
            cp = pltpu.make_async_remote_copy(
                src_ref=sm_ref, dst_ref=osm_ref.at[dev], send_sem=ssend_sems.at[j], recv_sem=srecv_sems.at[j],
                device_id=(x ^ fx, y ^ fy, c ^ fc), device_id_type=MESH)
            cp.start()
            sends.append(cp)
        for j, (px, py) in enumerate(_other_chips(x, y)):
            for a in range(n):
                pltpu.make_async_remote_copy(
                    src_ref=ins[a].at[me], dst_ref=outs[a].at[2 * px + py], send_sem=send_sems.at[3 * a + j],
                    recv_sem=recv_sems.at[3 * a + j], device_id=(px, py, c), device_id_type=MESH).wait_recv()
        for j, (fx, fy, fc) in enumerate(flips):
            src = 4 * (x ^ fx) + 2 * (y ^ fy) + (c ^ fc)
            pltpu.make_async_remote_copy(
                src_ref=sm_ref, dst_ref=osm_ref.at[src], send_sem=ssend_sems.at[j], recv_sem=srecv_sems.at[j],
                device_id=(x ^ fx, y ^ fy, c ^ fc), device_id_type=MESH).wait_recv()
        for cp in sends:
            cp.wait_send()
        for cp in local:
            cp.wait()

    return pl.pallas_call(
        body, name="exchange_chips",
        in_specs=[HBM] * (n + 1), out_specs=[HBM] * (n + 1),
        out_shape=[jax.ShapeDtypeStruct(p.shape, p.dtype) for p in parts]
                  + [jax.ShapeDtypeStruct((8,) + small.shape, small.dtype)],
        scratch_shapes=[pltpu.SemaphoreType.DMA((3 * n,)), pltpu.SemaphoreType.DMA((3 * n,)),
                        pltpu.SemaphoreType.DMA((7,)), pltpu.SemaphoreType.DMA((7,)),
                        pltpu.SemaphoreType.DMA((n + 1,))],
    )(*parts, small)


def _swap_cores(parts, name):
    n = len(parts)

    def body(*refs):
        ins, outs, send_sems, recv_sems = refs[:n], refs[n:2 * n], refs[2 * n], refs[2 * n + 1]
        x, y, c = _place()
        copies = [pltpu.make_async_remote_copy(
            src_ref=ins[a], dst_ref=outs[a], send_sem=send_sems.at[a], recv_sem=recv_sems.at[a],
            device_id=(x, y, 1 - c), device_id_type=MESH) for a in range(n)]
        for cp in copies:
            cp.start()
        for cp in copies:
            cp.wait()

    return pl.pallas_call(
        body, name=name, in_specs=[HBM] * n, out_specs=[HBM] * n,
        out_shape=[jax.ShapeDtypeStruct(p.shape, p.dtype) for p in parts],
        scratch_shapes=[pltpu.SemaphoreType.DMA((n,)), pltpu.SemaphoreType.DMA((n,))],
    )(*parts)


def _tile(rows, cols):
    for cand in (256, 128, 64):
        if rows % cand == 0:
            return cand, cols
    if rows > 64 and cols % 256 == 0:
        return rows, 256
    return rows, cols


def _add(a, b, name):
    L, R, C = a.shape
    tr, tc = _tile(R, C)

    def body(a_ref, b_ref, o_ref):
        o_ref[...] = (a_ref[...] + b_ref[...]).astype(BF16)

    spec = pl.BlockSpec((1, tr, tc), lambda l, i, j: (l, i, j))
    return pl.pallas_call(
        body, name=name, grid=(L, R // tr, C // tc), in_specs=[spec, spec], out_specs=spec,
        out_shape=jax.ShapeDtypeStruct((L, R, C), BF16),
        compiler_params=_cparams(("parallel", "parallel", "parallel")),
    )(a, b)


def _sum_leading(buf, name):
    n, R, C = buf.shape
    tr, tc = _tile(R, C)

    def body(b_ref, o_ref):
        acc = b_ref[0].astype(F32)
        for k in range(1, n):
            acc = acc + b_ref[k].astype(F32)
        o_ref[...] = acc

    return pl.pallas_call(
        body, name=name, grid=(R // tr, C // tc),
        in_specs=[pl.BlockSpec((n, tr, tc), lambda i, j: (0, i, j))],
        out_specs=pl.BlockSpec((tr, tc), lambda i, j: (i, j)),
        out_shape=jax.ShapeDtypeStruct((R, C), F32),
        compiler_params=_cparams(("parallel", "parallel")),
    )(buf)


def _adam_update(w, g, m, v):
    c1 = 1.0 / (1.0 - ADAM_B1 ** ADAM_STEP)
    c2 = 1.0 / (1.0 - ADAM_B2 ** ADAM_STEP)
    mn = ADAM_B1 * m + (1.0 - ADAM_B1) * g
    vn = ADAM_B2 * v + (1.0 - ADAM_B2) * (g * g)
    return -ADAM_LR * ((mn * c1) / (jnp.sqrt(vn * c2) + ADAM_EPS) + ADAM_WD * w), mn, vn


def _adamw_layers(w, m, v, g_mine, g_other, name):
    _, R, C = w.shape
    tr, tc = _tile(R, C)

    def body(w_ref, m_ref, v_ref, gm_ref, go_ref, g_ref, d_ref, mo_ref, vo_ref):
        g = jnp.where(pl.program_id(0) == lax.axis_index("c"), gm_ref[...], go_ref[...])
        g_ref[0] = g
        d_ref[0], mo_ref[0], vo_ref[0] = _adam_update(w_ref[0], g, m_ref[0], v_ref[0])

    spec3 = pl.BlockSpec((1, tr, tc), lambda l, i, j: (l, i, j))
    spec2 = pl.BlockSpec((tr, tc), lambda l, i, j: (i, j))
    return pl.pallas_call(
        body, name=name, grid=(2, R // tr, C // tc),
        in_specs=[spec3] * 3 + [spec2] * 2, out_specs=[spec3] * 4,
        out_shape=[jax.ShapeDtypeStruct(w.shape, F32)] * 4,
        compiler_params=_cparams(("parallel", "parallel", "parallel")),
    )(w, m, v, g_mine, g_other)


PACK_C = 1024
_BIG = ("w_in", "w_out", "mla_w_uq", "mla_w_ukv", "conv_w")
_SMALL = ("norm_pre", "group_norm", "norm_post", "conv_b", "mla_q_norm", "mla_kv_norm", "attn_sinks")
_SMALL_W = {"norm_pre": 1024, "group_norm": 1024, "norm_post": 1024, "conv_b": 256, "mla_q_norm": 256,
            "mla_kv_norm": 128, "attn_sinks": 4}


_LOSS_AT = divmod(DEPTH * sum(_SMALL_W.values()), PACK_C)


def _pack_small(d, loss):
    flat = jnp.concatenate([d[n].reshape(-1) for n in _SMALL] + [loss.reshape(1)])
    return jnp.pad(flat, (0, 8 * PACK_C - flat.shape[0])).reshape(8, PACK_C)


def _adamw_small(w, m, v, got):
    ns = len(_SMALL)

    def body(*refs):
        got_ref = refs[3 * ns]
        outs = refs[3 * ns + 1:]
        gsum = got_ref[0]
        for d in range(1, 8):
            gsum = gsum + got_ref[d]
        outs[4 * ns][...] = gsum[_LOSS_AT[0]:_LOSS_AT[0] + 1, _LOSS_AT[1]:_LOSS_AT[1] + 1]
        off = 0
        for i, name in enumerate(_SMALL):
            wd = _SMALL_W[name]
            rows = []
            for l in range(DEPTH):
                r, c0 = divmod(off + l * wd, PACK_C)
                rows.append(gsum[r:r + 1, c0:c0 + wd])
            off += DEPTH * wd
            g = jnp.concatenate(rows, axis=0)
            delta, mn, vn = _adam_update(refs[i][...], g, refs[ns + i][...], refs[2 * ns + i][...])
            outs[i][...] = g
            outs[ns + i][...] = delta
            outs[2 * ns + i][...] = mn
            outs[3 * ns + i][...] = vn

    shapes = [jax.ShapeDtypeStruct(w[n].shape, F32) for n in _SMALL]
    res = pl.pallas_call(body, name="adamw_small", out_shape=shapes * 4 + [jax.ShapeDtypeStruct((1, 1), F32)])(
        *[w[n] for n in _SMALL], *[m[n] for n in _SMALL], *[v[n] for n in _SMALL], got)
    return [dict(zip(_SMALL, res[k * ns:(k + 1) * ns])) for k in range(4)], res[4 * ns]


def _w_in_internal(wt):
    rows = []
    for n in _INT_ORDER:
        o, wd = _REAL_OFF[n]
        rows.append(wt[o:o + wd])
        if _INT_W[n] != wd:
            rows.append(jnp.zeros((_INT_W[n] - wd, wt.shape[1]), wt.dtype))
    return jnp.concatenate(rows, axis=0)


def _uq_internal(w):
    return jnp.pad(w.reshape(256, 4, 96), ((0, 0), (0, 0), (0, 32))).reshape(256, 512)


def _uq_real(dw):
    return dw.reshape(256, 4, 128)[:, :, :96].reshape(256, 384)


def _ukv_internal(w):
    w4 = w.reshape(128, 4, 128)
    k = jnp.pad(w4[:, :, :64], ((0, 0), (0, 0), (0, 64))).reshape(128, 512)
    return jnp.concatenate([k, w4[:, :, 64:].reshape(128, 256)], axis=1)


def _ukv_real(dw):
    k = dw[:, :512].reshape(128, 4, 128)[:, :, :64]
    v = dw[:, 512:].reshape(128, 4, 64)
    return jnp.concatenate([k, v], axis=2).reshape(128, 512)


def _layer_fwd(x, rope, p, tgt=None, fetch=()):
    xn, hb, hf, *fetched = _inproj_fwd(x, p["norm_pre"], p["w_in"], fetch)
    ya = _swa_fwd(hb, p["attn_sinks"])
    qm, km, vm, vt = _mla_prep_fwd(hf, rope, p["mla_q_norm"], p["mla_kv_norm"], p["mla_w_uq"], p["mla_w_ukv"])
    yc, lse = _mla_fwd(qm, km, vt)
    yd, tot, cnt = _sb_fwd(hb)
    x_next = _epilogue_fwd(x, ya, yc, yd, hf, p["conv_w"], p["conv_b"], p["group_norm"], p["w_out"], p["norm_post"],
                           tgt)
    saved = dict(x=x, xn=xn, hb=hb, hf=hf, ya=ya, yc=yc, yd=yd, tot=tot, cnt=cnt, qm=qm, km=km, vm=vm, lse=lse)
    return x_next, saved, fetched


def _layer_bwd(dx_next, rope, p, s):
    (dya, dyc, dyd, dhf, dw_out, dg_post, dg_grp, dconv_w, dconv_b) = _epilogue_bwd(
        dx_next, s["ya"], s["yc"], s["yd"], s["hf"], p["conv_w"], p["conv_b"], p["group_norm"], p["w_out"],
        p["norm_post"])
    dq_d, dk_d, dv_d = _sb_bwd(s["hb"], s["tot"], s["cnt"], dyd)
    dqm, dkt, dvt = _mla_bwd(s["qm"], s["km"], s["vm"], s["yc"], s["lse"], dyc)
    dc, dw_uq, dw_ukv, dg_q, dg_kv = _mla_prep_bwd(
        s["hf"], rope, p["mla_q_norm"], p["mla_kv_norm"], p["mla_w_uq"], p["mla_w_ukv"], dqm, dkt, dvt)
    dq_a, dk_a, dv_a, dsinks = _swa_bwd(s["hb"], p["attn_sinks"], dya)
    dx, dh, dg_pre = _inproj_bwd_dx(s["x"], p["norm_pre"], p["w_in"], dx_next,
                                    [dq_a, dk_a, dv_a, dq_d, dk_d, dv_d, dhf, dc])
    grads = dict(norm_pre=dg_pre[0], w_in_t=_inproj_bwd_dw(s["xn"], dh), attn_sinks=dsinks[0, :4], conv_w=dconv_w[:3],
                 conv_b=dconv_b[0], mla_q_norm=dg_q[0], mla_w_uq=_uq_real(dw_uq), mla_kv_norm=dg_kv[0],
                 mla_w_ukv=_ukv_real(dw_ukv), group_norm=dg_grp[0], w_out=dw_out, norm_post=dg_post[0])
    return dx, grads


_WEIGHTS = ["norm_pre", "w_in", "attn_sinks", "conv_w", "conv_b", "mla_q_norm", "mla_w_uq", "mla_kv_norm",
            "mla_w_ukv", "group_norm", "w_out", "norm_post"]


def kernel(x, positions, norm_pre, w_in, attn_sinks, conv_w, conv_b, mla_q_norm, mla_w_uq, mla_kv_norm, mla_w_ukv, group_norm, w_out, norm_post, loss_target, m_norm_pre, m_w_in, m_attn_sinks, m_conv_w, m_conv_b, m_mla_q_norm, m_mla_w_uq, m_mla_kv_norm, m_mla_w_ukv, m_group_norm, m_w_out, m_norm_post, v_norm_pre, v_w_in, v_attn_sinks, v_conv_w, v_conv_b, v_mla_q_norm, v_mla_w_uq, v_mla_kv_norm, v_mla_w_ukv, v_group_norm, v_w_out, v_norm_post):
    w = dict(norm_pre=norm_pre, w_in=w_in, attn_sinks=attn_sinks, conv_w=conv_w, conv_b=conv_b,
             mla_q_norm=mla_q_norm, mla_w_uq=mla_w_uq, mla_kv_norm=mla_kv_norm, mla_w_ukv=mla_w_ukv,
             group_norm=group_norm, w_out=w_out, norm_post=norm_post)
    m = dict(norm_pre=m_norm_pre, w_in=m_w_in, attn_sinks=m_attn_sinks, conv_w=m_conv_w, conv_b=m_conv_b,
             mla_q_norm=m_mla_q_norm, mla_w_uq=m_mla_w_uq, mla_kv_norm=m_mla_kv_norm, mla_w_ukv=m_mla_w_ukv,
             group_norm=m_group_norm, w_out=m_w_out, norm_post=m_norm_post)
    v = dict(norm_pre=v_norm_pre, w_in=v_w_in, attn_sinks=v_attn_sinks, conv_w=v_conv_w, conv_b=v_conv_b,
             mla_q_norm=v_mla_q_norm, mla_w_uq=v_mla_w_uq, mla_kv_norm=v_mla_kv_norm, mla_w_ukv=v_mla_w_ukv,
             group_norm=v_group_norm, w_out=v_w_out, norm_post=v_norm_post)
    T = x.shape[1]
    xs = x[0]
    rope = _rope_tables(positions[0].reshape(T, 1))
    tgt = loss_target[0]
    core = lax.axis_index("c")

    def shard_parts(l):
        halves = lambda a: a.reshape((2, a.shape[0] // 2) + a.shape[1:])
        return [halves(jnp.swapaxes(w["w_in"][l], 0, 1).astype(BF16))] + [
            halves(w[n][l].astype(BF16)) for n in _BIG[1:4]] + [jnp.stack([w["conv_w"][l]] * 2)]

    def layer_params(l, got):
        whole = lambda a: a.reshape((4, 2 * a.shape[2]) + a.shape[3:])
        by_cols = lambda a: jnp.transpose(a, (1, 0, 2)).reshape(a.shape[1], 4 * a.shape[2])
        return dict(
            norm_pre=norm_pre[l:l + 1], w_in=_w_in_internal(whole(got[0]).reshape(D_IN, D_MODEL)),
            attn_sinks=attn_sinks[l], conv_w=by_cols(got[4][:, 0]), conv_b=conv_b[l:l + 1],
            mla_q_norm=mla_q_norm[l:l + 1], mla_w_uq=_uq_internal(by_cols(whole(got[2]))),
            mla_kv_norm=mla_kv_norm[l:l + 1], mla_w_ukv=_ukv_internal(by_cols(whole(got[3]))),
            group_norm=group_norm[l:l + 1], w_out=whole(got[1]).reshape(D_MODEL, D_MODEL),
            norm_post=norm_post[l:l + 1])

    layers, saved = [], []
    h, got = xs, _gather_weights(shard_parts(0))
    for l in range(DEPTH):
        last = l == DEPTH - 1
        layers.append(layer_params(l, got))
        h, s, got = _layer_fwd(h, rope, layers[l], tgt if last else None, () if last else shard_parts(l + 1))
        saved.append(s)
    dy, loss_part = h

    grads = [None] * DEPTH
    for l in reversed(range(DEPTH)):
        dy, grads[l] = _layer_bwd(dy, rope, layers[l], saved[l])

    turned = ("w_in", "mla_w_uq")
    turn = lambda n, a: jnp.swapaxes(a, -1, -2) if n in turned else a

    def chunks(n, a):
        if n in ("w_out", "w_in"):
            return a.reshape(4, a.shape[0] // 4, a.shape[1])
        if n in turned:
            return a.T.reshape(4, a.shape[1] // 4, a.shape[0])
        return jnp.transpose(a.reshape(a.shape[0], 4, a.shape[1] // 4), (1, 0, 2))

    grad = lambda l, n: grads[l]["w_in_t" if n == "w_in" else n]
    mine = [chunks(n, jnp.where(core == 0, grad(0, n), grad(1, n))) for n in _BIG]
    theirs = [chunks(n, jnp.where(core == 0, grad(1, n), grad(0, n))) for n in _BIG]
    from_sibling = _swap_cores(theirs, "swap_layer_chunks")
    summed = [_add(a, b, "add_cores_" + n) for n, a, b in zip(_BIG, mine, from_sibling)]
    small = _pack_small({n: jnp.stack([grads[l][n] for l in range(DEPTH)]) for n in _SMALL}, loss_part[0, 0])
    *got, got_small = _exchange_chips(summed, small)
    done = [_sum_leading(b, "sum_chips_" + n) for n, b in zip(_BIG, got)]
    done_other = _swap_cores(done, "swap_layer_shards")

    outs, loss = _adamw_small(w, m, v, got_small)
    for n, gm, go in zip(_BIG, done, done_other):
        for d, a in zip(outs, _adamw_layers(turn(n, w[n]), turn(n, m[n]), turn(n, v[n]), gm, go, "adamw_" + n)):
            d[n] = turn(n, a)
    return (loss[0, 0], dy[None], *[outs[0][n] for n in _WEIGHTS], *[outs[1][n] for n in _WEIGHTS],
            *[outs[2][n] for n in _WEIGHTS], *[outs[3][n] for n in _WEIGHTS])
```

```python
import math

import jax
import jax.numpy as jnp
from jax import lax
from jax.experimental import pallas as pl
from jax.experimental.pallas import tpu as pltpu

F32 = jnp.float32
BF16 = jnp.bfloat16
MESH = pl.DeviceIdType.MESH

D_MODEL = 1024
DEPTH = 2
EPS = 1e-6
BLOCK = 128
HEAD = 64
LANES = 128
GROUP = 256
LOG2E = 1.4426950408889634
LN2 = 0.6931471805599453
MLA_QSCALE = 96 ** -0.5 * LOG2E
ROPE_HALF = 16
ROPE_THETA = 10000.0
SWA_SUB = 2
ATT_BLK = 256
MLA_BQ = 512
NEG = -1e30
SB_DEAD = -104.0

ADAM_LR, ADAM_B1, ADAM_B2, ADAM_EPS, ADAM_WD, ADAM_STEP = 0.001, 0.9, 0.999, 1e-08, 0.01, 10

_REAL = [("a_q", 256), ("a_k", 128), ("a_v", 128), ("b_b", 256), ("b_c", 256), ("b_x", 256),
         ("c_q", 256), ("c_kv", 128), ("c_kr", 32), ("d_q", 256), ("d_k", 256), ("d_v", 256),
         ("gate", 1024)]
_REAL_OFF = {}
_o = 0
for _n, _w in _REAL:
    _REAL_OFF[_n] = (_o, _w)
    _o += _w
D_IN = _o
_INT_ORDER = ["a_q", "a_k", "a_v", "d_q", "d_k", "d_v", "gate", "b_b", "b_c", "b_x", "c_q", "c_kv", "c_kr"]
_INT_W = dict(_REAL)
_INT_W["c_kr"] = 128
_INT_OFF = {}
_o = 0
for _n in _INT_ORDER:
    _INT_OFF[_n] = _o
    _o += _INT_W[_n]
N_INT = _o
N_HB = _INT_OFF["gate"]
N_HF = N_INT - N_HB

VMEM_LIMIT = 56 * 1024 * 1024


def _cparams(sem):
    return pltpu.CompilerParams(dimension_semantics=sem, vmem_limit_bytes=VMEM_LIMIT)


def _dot(a, b):
    return jnp.dot(a, b, preferred_element_type=F32)


def _dot_nt(a, b):
    return lax.dot_general(a, b, (((1,), (1,)), ((), ())), preferred_element_type=F32)


def _dot_tn(a, b):
    return lax.dot_general(a, b, (((0,), (0,)), ((), ())), preferred_element_type=F32)


def _split(x):
    hi = x.astype(BF16)
    lo = (x - hi.astype(F32)).astype(BF16)
    return hi, lo


def _rms(x):
    return lax.rsqrt(jnp.mean(x * x, axis=-1, keepdims=True) + EPS)


def _rms_bwd(dy, xhat, r, g):
    dxhat = dy * g
    return r * (dxhat - xhat * jnp.mean(dxhat * xhat, axis=-1, keepdims=True)), dy * xhat


def _colsum(x):
    return jnp.sum(x, axis=0, keepdims=True)


def _inproj_fwd(x, g, wt, fetch=()):
    T = x.shape[0]
    tm = 512
    nt, n = T // tm, len(fetch)

    def body(x_ref, g_ref, w_ref, *rest):
        xn_ref, hb_ref, hf_ref = rest[n:n + 3]
        if n:
            start, finish = _gather_plan(rest[:n], rest[n + 3:2 * n + 3], rest[2 * n + 3:])
            pl.when(pl.program_id(0) == 0)(start)
        xv = x_ref[...]
        xn = (xv * _rms(xv) * g_ref[...]).astype(BF16)
        xn_ref[...] = xn
        h = _dot_nt(xn, w_ref[...])
        hb_ref[...] = h[:, :N_HB].astype(BF16)
        hf_ref[...] = h[:, N_HB:]
        if n:
            pl.when(pl.program_id(0) == nt - 1)(finish)

    return pl.pallas_call(
        body, name="inproj_fwd_fetch" if n else "inproj_fwd", grid=(nt,),
        in_specs=[pl.BlockSpec((tm, D_MODEL), lambda i: (i, 0)),
                  pl.BlockSpec((1, D_MODEL), lambda i: (0, 0)),
                  pl.BlockSpec((N_INT, D_MODEL), lambda i: (0, 0))] + [HBM] * n,
        out_specs=[pl.BlockSpec((tm, D_MODEL), lambda i: (i, 0)),
                   pl.BlockSpec((tm, N_HB), lambda i: (i, 0)),
                   pl.BlockSpec((tm, N_HF), lambda i: (i, 0))] + [HBM] * n,
        out_shape=[jax.ShapeDtypeStruct((T, D_MODEL), BF16),
                   jax.ShapeDtypeStruct((T, N_HB), BF16),
                   jax.ShapeDtypeStruct((T, N_HF), F32)]
                  + [jax.ShapeDtypeStruct((4,) + s.shape, s.dtype) for s in fetch],
        scratch_shapes=_gather_sems(n) if n else [],
        compiler_params=_cparams(("arbitrary",) if n else ("parallel",)),
    )(x, g, wt, *fetch)


def _inproj_bwd_dx(x, g, wt, dx_next, pieces):
    T = x.shape[0]
    tm = 512
    widths = [p.shape[1] for p in pieces]
    assert sum(widths) == N_INT

    def body(x_ref, g_ref, w_ref, dxn_ref, *rest):
        p_refs = rest[:len(pieces)]
        dx_ref, dh_ref, dg_ref = rest[len(pieces):]
        dh = jnp.concatenate([p[...].astype(BF16) for p in p_refs], axis=1)
        dh_ref[...] = dh
        dxn = _dot(dh, w_ref[...])
        xv = x_ref[...]
        r = _rms(xv)
        dx, dgrow = _rms_bwd(dxn, xv * r, r, g_ref[...])
        dx_ref[...] = dx + dxn_ref[...]

        @pl.when(pl.program_id(0) == 0)
        def _():
            dg_ref[...] = jnp.zeros_like(dg_ref)

        dg_ref[...] += _colsum(dgrow)

    return pl.pallas_call(
        body, name="inproj_bwd_dx", grid=(T // tm,),
        in_specs=[pl.BlockSpec((tm, D_MODEL), lambda i: (i, 0)),
                  pl.BlockSpec((1, D_MODEL), lambda i: (0, 0)),
                  pl.BlockSpec((N_INT, D_MODEL), lambda i: (0, 0)),
                  pl.BlockSpec((tm, D_MODEL), lambda i: (i, 0))]
                 + [pl.BlockSpec((tm, wd), lambda i: (i, 0)) for wd in widths],
        out_specs=[pl.BlockSpec((tm, D_MODEL), lambda i: (i, 0)),
                   pl.BlockSpec((tm, N_INT), lambda i: (i, 0)),
                   pl.BlockSpec((1, D_MODEL), lambda i: (0, 0))],
        out_shape=[jax.ShapeDtypeStruct((T, D_MODEL), F32),
                   jax.ShapeDtypeStruct((T, N_INT), BF16),
                   jax.ShapeDtypeStruct((1, D_MODEL), F32)],
        compiler_params=_cparams(("arbitrary",)),
    )(x, g, wt, dx_next, *pieces)


def _inproj_bwd_dw(xn, dh):
    T = xn.shape[0]
    tm = min(512, T)

    def body(a_ref, b_ref, o_ref):
        @pl.when(pl.program_id(0) == 0)
        def _():
            o_ref[...] = jnp.zeros_like(o_ref)

        for n, wd in _REAL:
            o, oi = _REAL_OFF[n][0], _INT_OFF[n]
            o_ref[o:o + wd, :] += _dot_tn(b_ref[:, oi:oi + _INT_W[n]], a_ref[...])[:wd]

    return pl.pallas_call(
        body, name="inproj_bwd_dw", grid=(T // tm,),
        in_specs=[pl.BlockSpec((tm, D_MODEL), lambda t: (t, 0)),
                  pl.BlockSpec((tm, N_INT), lambda t: (t, 0))],
        out_specs=pl.BlockSpec((D_IN, D_MODEL), lambda t: (0, 0)),
        out_shape=jax.ShapeDtypeStruct((D_IN, D_MODEL), F32),
        compiler_params=_cparams(("arbitrary",)),
    )(xn, dh)


def _roll_f32(x, shift):
    return pltpu.roll(x.astype(F32), shift, 1)


def _swa_operands(h, q, k_prev, k_cur, v_prev, v_cur):
    p, e = h // 2, h % 2
    lane = lax.broadcasted_iota(jnp.int32, (1, LANES), 1) // HEAD
    q = q[:, p * LANES:(p + 1) * LANES]
    if e != p:
        q = _roll_f32(q, HEAD).astype(BF16)
        v_prev = _roll_f32(v_prev, HEAD).astype(BF16)
        v_cur = _roll_f32(v_cur, HEAD).astype(BF16)
    qs = jnp.where(lane == p, q, 0) * 0.125
    return dict(p=p, e=e, lane=lane, qs=qs, k_prev=k_prev, k_cur=k_cur,
                v_prev=jnp.where(lane == e, v_prev, 0), v_cur=jnp.where(lane == e, v_cur, 0),
                s_prev=_dot_nt(qs, k_prev), s_cur=_dot_nt(qs, k_cur))


def _swa_probs(ops, sink, no_prev):
    row = lax.broadcasted_iota(jnp.int32, (BLOCK, BLOCK), 0)
    col = lax.broadcasted_iota(jnp.int32, (BLOCK, BLOCK), 1)
    ok_prev = col > row if no_prev is None else jnp.logical_and(col > row, jnp.logical_not(no_prev))
    s_prev = jnp.where(ok_prev, ops["s_prev"], NEG)
    s_cur = jnp.where(col <= row, ops["s_cur"], NEG)
    m = jnp.maximum(jnp.maximum(jnp.max(s_prev, axis=1, keepdims=True),
                                jnp.max(s_cur, axis=1, keepdims=True)), sink)
    p_prev = jnp.exp(s_prev - m)
    p_cur = jnp.exp(s_cur - m)
    p_sink = jnp.exp(sink - m)
    inv = 1.0 / (jnp.sum(p_prev, axis=1, keepdims=True) + jnp.sum(p_cur, axis=1, keepdims=True) + p_sink)
    return p_prev * inv, p_cur * inv, p_sink * inv


def _swa_specs(T):
    n = T // (BLOCK * SWA_SUB)
    qo, ko, vo = (_INT_OFF[name] // LANES for name in ("a_q", "a_k", "a_v"))
    halo = lambda i: jnp.maximum(i * SWA_SUB - 1, 0)
    return [pl.BlockSpec((BLOCK * SWA_SUB, 256), lambda i: (i, qo // 2)),
            pl.BlockSpec((BLOCK, LANES), lambda i: (halo(i), ko)),
            pl.BlockSpec((BLOCK * SWA_SUB, LANES), lambda i: (i, ko)),
            pl.BlockSpec((BLOCK, LANES), lambda i: (halo(i), vo)),
            pl.BlockSpec((BLOCK * SWA_SUB, LANES), lambda i: (i, vo)),
            pl.BlockSpec(memory_space=pltpu.SMEM)], n


def _swa_units(q_ref, kh_ref, kc_ref, vh_ref, vc_ref, s_ref):
    blk = lambda a: slice(a * BLOCK, (a + 1) * BLOCK)
    units = [(a, h) for a in range(SWA_SUB) for h in range(4)]
    ops = {}
    for a, h in units:
        k_prev, v_prev = (kh_ref[...], vh_ref[...]) if a == 0 else (kc_ref[blk(a - 1), :], vc_ref[blk(a - 1), :])
        ops[a, h] = _swa_operands(h, q_ref[blk(a), :], k_prev, kc_ref[blk(a), :], v_prev, vc_ref[blk(a), :])
    probs = {(a, h): _swa_probs(ops[a, h], s_ref[h], pl.program_id(0) == 0 if a == 0 else None) for a, h in units}
    return units, ops, probs, blk


def _swa_fwd(hb, sinks):
    T = hb.shape[0]
    specs, n = _swa_specs(T)

    def body(q_ref, kh_ref, kc_ref, vh_ref, vc_ref, s_ref, o_ref):
        units, ops, probs, blk = _swa_units(q_ref, kh_ref, kc_ref, vh_ref, vc_ref, s_ref)
        outs = {u: _dot(probs[u][0].astype(BF16), ops[u]["v_prev"]) + _dot(probs[u][1].astype(BF16), ops[u]["v_cur"])
                for u in units}
        for a in range(SWA_SUB):
            for p in range(2):
                o_ref[blk(a), p * LANES:(p + 1) * LANES] = outs[a, 2 * p] + outs[a, 2 * p + 1]

    return pl.pallas_call(
        body, name="swa_fwd", grid=(n,), in_specs=specs,
        out_specs=pl.BlockSpec((BLOCK * SWA_SUB, 256), lambda i: (i, 0)),
        out_shape=jax.ShapeDtypeStruct((T, 256), F32),
        compiler_params=_cparams(("parallel",)),
    )(hb, hb, hb, hb, hb, sinks)


def _swa_bwd(hb, sinks, dy):
    T = hb.shape[0]
    specs, n = _swa_specs(T)

    def body(q_ref, kh_ref, kc_ref, vh_ref, vc_ref, s_ref, dy_ref, dq_ref, dk_ref, dv_ref, ds_ref):
        i = pl.program_id(0)

        @pl.when(i == 0)
        def _():
            ds_ref[...] = jnp.zeros_like(ds_ref)

        lane_id = lax.broadcasted_iota(jnp.int32, (8, LANES), 1)
        units, ops, probs, blk = _swa_units(q_ref, kh_ref, kc_ref, vh_ref, vc_ref, s_ref)
        dos = {(a, h): jnp.where(ops[a, h]["lane"] == ops[a, h]["e"],
                                 dy_ref[blk(a), ops[a, h]["p"] * LANES:(ops[a, h]["p"] + 1) * LANES], 0.0)
               for a, h in units}
        dobs = {u: dos[u].astype(BF16) for u in units}
        pbs = {u: (probs[u][0].astype(BF16), probs[u][1].astype(BF16)) for u in units}
        outs = {u: _dot(pbs[u][0], ops[u]["v_prev"]) + _dot(pbs[u][1], ops[u]["v_cur"]) for u in units}
        dps = {u: (_dot_nt(dobs[u], ops[u]["v_prev"]), _dot_nt(dobs[u], ops[u]["v_cur"])) for u in units}
        dss, dsinks = {}, jnp.zeros((8, LANES), F32)
        for u in units:
            delta = jnp.sum(dos[u] * outs[u], axis=1, keepdims=True)
            dss[u] = ((probs[u][0] * (dps[u][0] - delta)).astype(BF16),
                      (probs[u][1] * (dps[u][1] - delta)).astype(BF16))
            dsink = -jnp.sum(probs[u][2] * delta, axis=0, keepdims=True)
            dsinks += jnp.where(lane_id == u[1], dsink, 0.0)
        ds_ref[...] += dsinks
        dqs = {u: (_dot(dss[u][0], ops[u]["k_prev"]) + _dot(dss[u][1], ops[u]["k_cur"])) * 0.125 for u in units}
        zero = jnp.zeros((BLOCK, LANES), F32)
        dk_as_prev, dk_as_cur = [zero] * SWA_SUB, [zero] * SWA_SUB
        dv_as_prev, dv_as_cur = [zero] * SWA_SUB, [zero] * SWA_SUB
        for a, h in units:
            p, e = ops[a, h]["p"], ops[a, h]["e"]
            dob_v = dobs[a, h] if e == p else pltpu.roll(dos[a, h], HEAD, 1).astype(BF16)
            dk_as_prev[a] = dk_as_prev[a] + _dot_tn(dss[a, h][0], ops[a, h]["qs"])
            dk_as_cur[a] = dk_as_cur[a] + _dot_tn(dss[a, h][1], ops[a, h]["qs"])
            dv_as_prev[a] = dv_as_prev[a] + _dot_tn(pbs[a, h][0], dob_v)
            dv_as_cur[a] = dv_as_cur[a] + _dot_tn(pbs[a, h][1], dob_v)
        base = i * SWA_SUB
        for a in range(SWA_SUB):
            rows = pl.ds(pl.multiple_of((base + a) * BLOCK, BLOCK), BLOCK)
            more = a + 1 < SWA_SUB
            dk_ref[rows, :] = dk_as_cur[a] + (dk_as_prev[a + 1] if more else 0.0)
            dv_ref[rows, :] = dv_as_cur[a] + (dv_as_prev[a + 1] if more else 0.0)
        halo = pl.ds(pl.multiple_of(jnp.maximum(base - 1, 0) * BLOCK, BLOCK), BLOCK)
        dk_ref[halo, :] += dk_as_prev[0]
        dv_ref[halo, :] += dv_as_prev[0]
        for a in range(SWA_SUB):
            for p in range(2):
                dq_pair = jnp.zeros((BLOCK, LANES), F32)
                for e in range(2):
                    dq = jnp.where(ops[a, 2 * p + e]["lane"] == p, dqs[a, 2 * p + e], 0.0)
                    dq_pair += dq if e == p else pltpu.roll(dq, HEAD, 1)
                dq_ref[blk(a), p * LANES:(p + 1) * LANES] = dq_pair.astype(BF16)

    return pl.pallas_call(
        body, name="swa_bwd", grid=(n,),
        in_specs=specs + [pl.BlockSpec((BLOCK * SWA_SUB, 256), lambda i: (i, 0))],
        out_specs=[pl.BlockSpec((BLOCK * SWA_SUB, 256), lambda i: (i, 0)),
                   pl.BlockSpec((T, LANES), lambda i: (0, 0)),
                   pl.BlockSpec((T, LANES), lambda i: (0, 0)),
                   pl.BlockSpec((8, LANES), lambda i: (0, 0))],
        out_shape=[jax.ShapeDtypeStruct((T, 256), BF16),
                   jax.ShapeDtypeStruct((T, LANES), F32),
                   jax.ShapeDtypeStruct((T, LANES), F32),
                   jax.ShapeDtypeStruct((8, LANES), F32)],
        compiler_params=_cparams(("arbitrary",)),
    )(hb, hb, hb, hb, hb, sinks, dy)


def _rope_tables(pos):
    T = pos.shape[0]
    tm = 512

    def body(pos_ref, o_ref):
        lane = lax.broadcasted_iota(jnp.int32, (1, LANES), 1)
        active = jnp.logical_and(lane >= HEAD, lane < HEAD + 2 * ROPE_HALF)
        idx = ((lane - HEAD) % ROPE_HALF).astype(F32)
        freq = jnp.exp(idx * (-math.log(ROPE_THETA) / ROPE_HALF))
        ang = pos_ref[...].astype(F32) * freq
        cos, sin = jnp.cos(ang), jnp.sin(ang)
        o_ref[:, 0:LANES] = jnp.where(active, cos, 1.0)
        o_ref[:, LANES:2 * LANES] = jnp.where(jnp.logical_and(active, lane >= HEAD + ROPE_HALF), sin, 0.0)
        o_ref[:, 2 * LANES:] = jnp.where(jnp.logical_and(active, lane < HEAD + ROPE_HALF), -sin, 0.0)

    return pl.pallas_call(
        body, name="rope_tables", grid=(T // tm,),
        in_specs=[pl.BlockSpec((tm, 1), lambda i: (i, 0))],
        out_specs=pl.BlockSpec((tm, 3 * LANES), lambda i: (i, 0)),
        out_shape=jax.ShapeDtypeStruct((T, 3 * LANES), F32),
        compiler_params=_cparams(("parallel",)),
    )(pos)


def _rope_factors(tab_ref):
    return tab_ref[:, 0:LANES], tab_ref[:, LANES:2 * LANES], tab_ref[:, 2 * LANES:]


def _rope(x, tabs):
    c, s_up, s_dn = tabs
    return x * c + pltpu.roll(x, ROPE_HALF, 1) * s_up + pltpu.roll(x, LANES - ROPE_HALF, 1) * s_dn


def _rope_t(dy, tabs):
    c, s_up, s_dn = tabs
    return dy * c + pltpu.roll(dy * s_up, LANES - ROPE_HALF, 1) + pltpu.roll(dy * s_dn, ROPE_HALF, 1)


def _mla_lat_specs(tm):
    cq, ckv, ckr = ((_INT_OFF[n] - N_HB) for n in ("c_q", "c_kv", "c_kr"))
    return [pl.BlockSpec((tm, 256), lambda i: (i, cq // 256)),
            pl.BlockSpec((tm, LANES), lambda i: (i, ckv // LANES)),
            pl.BlockSpec((tm, LANES), lambda i: (i, ckr // LANES)),
            pl.BlockSpec((tm, 3 * LANES), lambda i: (i, 0)),
            pl.BlockSpec((1, 256), lambda i: (0, 0)),
            pl.BlockSpec((1, LANES), lambda i: (0, 0)),
            pl.BlockSpec((256, 512), lambda i: (0, 0)),
            pl.BlockSpec((LANES, 768), lambda i: (0, 0))]


def _mla_prep_fwd(hf, rope, g_q, g_kv, w_uq, w_ukv):
    T = hf.shape[0]
    tm = 512
    sub = tm // ATT_BLK

    def body(cq_ref, ckv_ref, ckr_ref, tab_ref, gq_ref, gkv_ref, wq_ref, wkv_ref, qm_ref, km_ref, vm_ref, vt_ref):
        tabs = _rope_factors(tab_ref)
        cq = cq_ref[...]
        q = _dot((cq * _rms(cq) * gq_ref[...]).astype(BF16), wq_ref[...])
        ckv = ckv_ref[...]
        kv = _dot((ckv * _rms(ckv) * gkv_ref[...]).astype(BF16), wkv_ref[...])
        kr = _rope(pltpu.roll(ckr_ref[...], HEAD, 1), tabs)
        for h in range(4):
            sl = slice(h * LANES, (h + 1) * LANES)
            qm_ref[:, sl] = (_rope(q[:, sl], tabs) * MLA_QSCALE).astype(BF16)
            km_ref[:, sl] = (kv[:, sl] + kr).astype(BF16)
        vm_ref[...] = kv[:, 512:].astype(BF16)
        for p in range(2):
            for s in range(sub):
                tile = kv[s * ATT_BLK:(s + 1) * ATT_BLK, 512 + p * LANES:512 + (p + 1) * LANES]
                vt_ref[p, s] = jnp.transpose(tile).astype(BF16)

    return pl.pallas_call(
        body, name="mla_prep_fwd", grid=(T // tm,), in_specs=_mla_lat_specs(tm),
        out_specs=[pl.BlockSpec((tm, 512), lambda i: (i, 0)),
                   pl.BlockSpec((tm, 512), lambda i: (i, 0)),
                   pl.BlockSpec((tm, 256), lambda i: (i, 0)),
                   pl.BlockSpec((2, sub, LANES, ATT_BLK), lambda i: (0, i, 0, 0))],
        out_shape=[jax.ShapeDtypeStruct((T, 512), BF16),
                   jax.ShapeDtypeStruct((T, 512), BF16),
                   jax.ShapeDtypeStruct((T, 256), BF16),
                   jax.ShapeDtypeStruct((2, T // ATT_BLK, LANES, ATT_BLK), BF16)],
        compiler_params=_cparams(("parallel",)),
    )(hf, hf, hf, rope, g_q, g_kv, w_uq, w_ukv)


def _mla_prep_bwd(hf, rope, g_q, g_kv, w_uq, w_ukv, dqm, dkt, dvt):
    T = hf.shape[0]
    tm = 512
    sub = tm // ATT_BLK

    def body(cq_ref, ckv_ref, ckr_ref, tab_ref, gq_ref, gkv_ref, wq_ref, wkv_ref, dq_ref, dk_ref, dv_ref,
             dc_ref, dwq_ref, dwkv_ref, dgq_ref, dgkv_ref):
        @pl.when(pl.program_id(0) == 0)
        def _():
            dwq_ref[...] = jnp.zeros_like(dwq_ref)
            dwkv_ref[...] = jnp.zeros_like(dwkv_ref)
            dgq_ref[...] = jnp.zeros_like(dgq_ref)
            dgkv_ref[...] = jnp.zeros_like(dgkv_ref)

        tabs = _rope_factors(tab_ref)
        lane =lax.broadcasted_iota(jnp.int32, (1, LANES), 1)
        dq = jnp.concatenate([_rope_t(dq_ref[:, h * LANES:(h + 1) * LANES] * MLA_QSCALE, tabs)
                              for h in range(4)], axis=1).astype(BF16)
        cq = cq_ref[...]
        rq = _rms(cq)
        cqn = (cq * rq * gq_ref[...]).astype(BF16)
        dwq_ref[...] += _dot_tn(cqn, dq)
        dcq, dgrow = _rms_bwd(_dot_nt(dq, wq_ref[...]), cq * rq, rq, gq_ref[...])
        dgq_ref[...] += _colsum(dgrow)
        dc_ref[:, 0:256] = dcq.astype(BF16)

        dk = jnp.concatenate([jnp.concatenate([jnp.transpose(dk_ref[p, s]) for p in range(2)], axis=1)
                              for s in range(sub)], axis=0) * LN2
        dv = jnp.concatenate([jnp.concatenate([jnp.transpose(dv_ref[p, s]) for p in range(2)], axis=1)
                              for s in range(sub)], axis=0)
        dkr = dk[:, 0:LANES] + dk[:, LANES:2 * LANES] + dk[:, 2 * LANES:3 * LANES] + dk[:, 3 * LANES:]
        dkr = pltpu.roll(_rope_t(dkr, tabs), HEAD, 1)
        dc_ref[:, 384:512] = jnp.where(lane < 2 * ROPE_HALF, dkr, 0.0).astype(BF16)
        dkv = jnp.concatenate([dk.astype(BF16), dv.astype(BF16)], axis=1)
        ckv = ckv_ref[...]
        rkv = _rms(ckv)
        ckvn = (ckv * rkv * gkv_ref[...]).astype(BF16)
        dwkv_ref[...] += _dot_tn(ckvn, dkv)
        dckv, dgrow = _rms_bwd(_dot_nt(dkv, wkv_ref[...]), ckv * rkv, rkv, gkv_ref[...])
        dgkv_ref[...] += _colsum(dgrow)
        dc_ref[:, 256:384] = dckv.astype(BF16)

    return pl.pallas_call(
        body, name="mla_prep_bwd", grid=(T // tm,),
        in_specs=_mla_lat_specs(tm) + [pl.BlockSpec((tm, 512), lambda i: (i, 0)),
                                       pl.BlockSpec((2, sub, 256, ATT_BLK), lambda i: (0, i, 0, 0)),
                                       pl.BlockSpec((2, sub, LANES, ATT_BLK), lambda i: (0, i, 0, 0))],
        out_specs=[pl.BlockSpec((tm, 512), lambda i: (i, 0)),
                   pl.BlockSpec((256, 512), lambda i: (0, 0)),
                   pl.BlockSpec((LANES, 768), lambda i: (0, 0)),
                   pl.BlockSpec((1, 256), lambda i: (0, 0)),
                   pl.BlockSpec((1, LANES), lambda i: (0, 0))],
        out_shape=[jax.ShapeDtypeStruct((T, 512), BF16),
                   jax.ShapeDtypeStruct((256, 512), F32),
                   jax.ShapeDtypeStruct((LANES, 768), F32),
                   jax.ShapeDtypeStruct((1, 256), F32),
                   jax.ShapeDtypeStruct((1, LANES), F32)],
        compiler_params=_cparams(("arbitrary",)),
    )(hf, hf, hf, rope, g_q, g_kv, w_uq, w_ukv, dqm, dkt, dvt)


def _causal_masks(bq, bk):
    row = lax.broadcasted_iota(jnp.int32, (bq, bk), 0)
    col = lax.broadcasted_iota(jnp.int32, (bq, bk), 1)
    return row, col


def _mla_fwd(qm, km, vt):
    T = qm.shape[0]
    bq, bk = min(MLA_BQ, T), ATT_BLK
    nq, nsub, nk = T // bq, bq // bk, T // bk

    def body(q_ref, k_ref, vt_ref, o_ref, lse_ref, acc_ref, m_ref, l_ref):
        qi = pl.program_id(0)
        key = lax.broadcasted_iota(jnp.int32, (bk, bq), 0)
        qry = lax.broadcasted_iota(jnp.int32, (bk, bq), 1)
        ones = jnp.ones((8, bk), BF16)
        acc_ref[...] = jnp.zeros_like(acc_ref)
        m_ref[...] = jnp.full_like(m_ref, NEG)
        l_ref[...] = jnp.zeros_like(l_ref)

        def step(kb0, masked):
            kbs = [kb0 + d for d in range(nsub)]
            qs = [slice(d * bk if masked else 0, bq) for d in range(nsub)]

            def wide(a, d, fill):
                if not qs[d].start:
                    return a
                return jnp.concatenate([jnp.full((a.shape[0], qs[d].start), fill, a.dtype), a], axis=1)

            sts = [[_dot_nt(k_ref[pl.ds(pl.multiple_of(kb * bk, bk), bk), e * LANES:(e + 1) * LANES],
                            q_ref[qs[d], e * LANES:(e + 1) * LANES]) for d, kb in enumerate(kbs)] for e in range(4)]
            pts, alphas = [], []
            for e in range(4):
                st = ([jnp.where(key[:, qs[d]] + d * bk <= qry[:, qs[d]], sts[e][d], NEG) for d in range(nsub)]
                      if masked else sts[e])
                m_prev = m_ref[e, 0:1, :]
                m_new = m_prev
                for d in range(nsub):
                    m_new = jnp.maximum(m_new, wide(jnp.max(st[d], axis=0, keepdims=True), d, NEG))
                alpha = jnp.exp2(m_prev - m_new)
                pt = [jnp.exp2(st[d] - m_new[:, qs[d]]).astype(BF16) for d in range(nsub)]
                l_new = alpha * l_ref[e]
                for d in range(nsub):
                    l_new = l_new + wide(_dot(ones, pt[d]), d, 0.0)
                l_ref[e] = l_new
                m_ref[e] = jnp.broadcast_to(m_new, (8, bq))
                pts.append(pt)
                alphas.append(alpha)
            for e in range(4):
                acc = alphas[e] * acc_ref[e]
                for d in range(nsub):
                    v_t = vt_ref[e // 2, kbs[d], (e % 2) * HEAD:(e % 2 + 1) * HEAD, :]
                    acc = acc + wide(_dot(v_t, pts[e][d]), d, 0.0)
                acc_ref[e] = acc

        step(qi * nsub, True)

        def loop(t, c):
            step(t * nsub, False)
            return c

        lax.fori_loop(0, qi, loop, 0)
        outs, lses = [], []
        for e in range(4):
            l = l_ref[e, 0:1, :]
            outs.append(acc_ref[e] / l)
            lses.append(jnp.broadcast_to(m_ref[e, 0:1, :] * LN2 + jnp.log(l), (HEAD, bq)))
        o_ref[...] = jnp.transpose(jnp.concatenate(outs, axis=0))
        lse_ref[...] = jnp.transpose(jnp.concatenate(lses, axis=0))

    return pl.pallas_call(
        body, name="mla_fwd", grid=(nq,),
        in_specs=[pl.BlockSpec((bq, 512), lambda i: (i, 0)),
                  pl.BlockSpec((T, 512), lambda i: (0, 0)),
                  pl.BlockSpec((2, nk, LANES, bk), lambda i: (0, 0, 0, 0))],
        out_specs=[pl.BlockSpec((bq, 256), lambda i: (i, 0)),
                   pl.BlockSpec((bq, 256), lambda i: (i, 0))],
        out_shape=[jax.ShapeDtypeStruct((T, 256), F32), jax.ShapeDtypeStruct((T, 256), F32)],
        scratch_shapes=[pltpu.VMEM((4, HEAD, bq), F32), pltpu.VMEM((4, 8, bq), F32), pltpu.VMEM((4, 8, bq), F32)],
        compiler_params=_cparams(("arbitrary",)),
    )(qm, km, vt)


def _mla_bwd(qm, km, vm, y, lse, dy):
    T = qm.shape[0]
    bq, bk = min(MLA_BQ, T), ATT_BLK
    nq, nsub, nk = T // bq, bq // bk, T // bk

    def body(q_ref, k_ref, v_ref, y_ref, lse_ref, dy_ref, dq_ref, dkt_ref, dvt_ref, dob_ref, st_ref, qt_ref, dot_ref):
        qi = pl.program_id(1)

        @pl.when(qi == 0)
        def _():
            dkt_ref[...] = jnp.zeros_like(dkt_ref)
            dvt_ref[...] = jnp.zeros_like(dvt_ref)

        lane = lax.broadcasted_iota(jnp.int32, (1, LANES), 1) // HEAD
        row, col = _causal_masks(bq, bk)
        dq_ref[...] = jnp.zeros_like(dq_ref)
        lse = lse_ref[...]
        lse_other = pltpu.roll(lse, HEAD, 1)
        qt_ref[...] = jnp.transpose(q_ref[...].astype(F32)).astype(BF16)
        dot_ref[...] = jnp.transpose(dy_ref[...]).astype(BF16)
        for e in range(2):
            do = jnp.where(lane == e, dy_ref[...], 0.0)
            dob_ref[e] = do.astype(BF16)
            st_ref[2 * e] = jnp.where(lane == e, lse, lse_other) * LOG2E
            st_ref[2 * e + 1] = jnp.broadcast_to(jnp.sum(do * y_ref[...], axis=1, keepdims=True), (bq, LANES))

        hss = [slice(e * LANES, (e + 1) * LANES) for e in range(2)]
        tile = lambda a: jnp.concatenate([a] * (bk // LANES), axis=1)

        def step(kb0, masked):
            kbs = [kb0 + d for d in range(nsub)]
            rows = [pl.ds(pl.multiple_of(kb * bk, bk), bk) for kb in kbs]
            pairs = [(d, e) for d in range(nsub) for e in range(2)]
            qs = [slice(d * bk if masked else 0, bq) for d in range(nsub)]
            ss = {(d, e): _dot_nt(q_ref[qs[d], hss[e]], k_ref[rows[d], hss[e]]) for d, e in pairs}
            dps = {(d, e): _dot_nt(dob_ref[e, qs[d], :], jnp.where(lane == e, v_ref[rows[d], :], 0))
                   for d, e in pairs}
            ps, dss = {}, {}
            for d, e in pairs:
                s = jnp.where(col[qs[d]] + d * bk <= row[qs[d]], ss[d, e], NEG) if masked else ss[d, e]
                p = jnp.exp2(s - tile(st_ref[2 * e, qs[d], :]))
                dss[d, e] = (p * (dps[d, e] - tile(st_ref[2 * e + 1, qs[d], :]))).astype(BF16)
                ps[d, e] = p.astype(BF16)
            for d, e in pairs:
                dvt_ref[0, kbs[d], e * HEAD:(e + 1) * HEAD, :] += _dot(
                    dot_ref[e * HEAD:(e + 1) * HEAD, qs[d]], ps[d, e])
            for d, e in pairs:
                dkt_ref[0, kbs[d], hss[e], :] += _dot(qt_ref[hss[e], qs[d]], dss[d, e])
            for e in range(2):
                if masked:
                    for d in range(nsub):
                        dq_ref[qs[d], hss[e]] += _dot(dss[d, e], k_ref[rows[d], hss[e]])
                else:
                    dq = dq_ref[:, hss[e]]
                    for d in range(nsub):
                        dq = dq + _dot(dss[d, e], k_ref[rows[d], hss[e]])
                    dq_ref[:, hss[e]] = dq

        step(qi * nsub, True)

        def loop(t, c):
            step(t * nsub, False)
            return c

        lax.fori_loop(0, qi, loop, 0)
        dq_ref[...] *= LN2

    return pl.pallas_call(
        body, name="mla_bwd", grid=(2, nq),
        in_specs=[pl.BlockSpec((bq, 256), lambda j, i: (i, j)),
                  pl.BlockSpec((T, 256), lambda j, i: (0, j)),
                  pl.BlockSpec((T, LANES), lambda j, i: (0, j)),
                  pl.BlockSpec((bq, LANES), lambda j, i: (i, j)),
                  pl.BlockSpec((bq, LANES), lambda j, i: (i, j)),
                  pl.BlockSpec((bq, LANES), lambda j, i: (i, j))],
        out_specs=[pl.BlockSpec((bq, 256), lambda j, i: (i, j)),
                   pl.BlockSpec((1, nk, 256, bk), lambda j, i: (j, 0, 0, 0)),
                   pl.BlockSpec((1, nk, LANES, bk), lambda j, i: (j, 0, 0, 0))],
        out_shape=[jax.ShapeDtypeStruct((T, 512), F32),
                   jax.ShapeDtypeStruct((2, nk, 256, bk), F32),
                   jax.ShapeDtypeStruct((2, nk, LANES, bk), F32)],
        scratch_shapes=[pltpu.VMEM((2, bq, LANES), BF16), pltpu.VMEM((4, bq, LANES), F32),
                        pltpu.VMEM((256, bq), BF16), pltpu.VMEM((LANES, bq), BF16)],
        compiler_params=_cparams(("parallel", "arbitrary")),
    )(qm, km, vm, y, lse, dy)


def _suffix_ones(n):
    r = lax.broadcasted_iota(jnp.int32, (n, n), 0)
    c = lax.broadcasted_iota(jnp.int32, (n, n), 1)
    return (r >= c).astype(BF16)


def _prefix_ones(n):
    r = lax.broadcasted_iota(jnp.int32, (n, n), 0)
    c = lax.broadcasted_iota(jnp.int32, (n, n), 1)
    return (r <= c).astype(BF16)


def _sb_specs(T, bq):
    qo, ko, vo = (_INT_OFF[n] // 256 for n in ("d_q", "d_k", "d_v"))
    return [pl.BlockSpec((bq, 256), lambda i: (i, qo)),
            pl.BlockSpec((T, 256), lambda i: (0, ko)),
            pl.BlockSpec((T, 256), lambda i: (0, vo))]


def _sb_fwd(hb):
    T = hb.shape[0]
    bq = bk = ATT_BLK
    nq = T // bq

    def body(q_ref, k_ref, v_ref, o_ref, tot_ref, cnt_ref, qm_ref, car_ref):
        qi = pl.program_id(0)
        lane = lax.broadcasted_iota(jnp.int32, (1, LANES), 1) // HEAD
        row, col = _causal_masks(bq, bk)
        strict = col < row
        u = _suffix_ones(bk)
        o_ref[...] = jnp.zeros_like(o_ref)
        car_ref[...] = jnp.zeros_like(car_ref)
        pair = lambda h: slice((h // 2) * LANES, (h // 2 + 1) * LANES)
        for h in range(4):
            qm_ref[h] = jnp.where(lane == h % 2, q_ref[:, pair(h)], 0) * 0.125

        def step(blocks):
            tile = lambda a: jnp.concatenate([a] * (bk // LANES), axis=1)
            rows = [pl.ds(pl.multiple_of(kb * bk, bk), bk) for kb, _ in blocks]
            pairs = [(b, h) for b in range(len(blocks)) for h in range(4)]
            zs = {(b, h): _dot_nt(qm_ref[h], k_ref[rows[b], pair(h)]) for b, h in pairs}
            splits = {}
            for b, h in pairs:
                z = zs[b, h]
                lk = jnp.minimum(-z, 0.0) - jnp.log(1.0 + jnp.exp(-jnp.abs(z)))
                if blocks[b][1] is not None:
                    lk = jnp.where(blocks[b][1], lk, 0.0)
                splits[b, h] = _split(lk)
            sufs = {bh: _dot(hi, u) + _dot(lo, u) for bh, (hi, lo) in splits.items()}
            car = [car_ref[h] for h in range(4)]
            aas = {}
            for b, h in pairs:
                a = jnp.exp(zs[b, h] + sufs[b, h] + tile(car[h]))
                if blocks[b][1] is not None:
                    a = jnp.where(blocks[b][1], a, 0.0)
                aas[b, h] = a.astype(BF16)
                car[h] = car[h] + jnp.broadcast_to(sufs[b, h][:, 0:1], (bq, LANES))
            acc = [o_ref[:, pair(0)], o_ref[:, pair(2)]]
            for b, h in pairs:
                acc[h // 2] = acc[h // 2] + _dot(aas[b, h], jnp.where(lane == h % 2, v_ref[rows[b], pair(h)], 0))
            o_ref[:, pair(0)], o_ref[:, pair(2)] = acc
            for h in range(4):
                car_ref[h] = car[h]

        step([(qi, strict), (jnp.maximum(qi - 1, 0), qi > 0)])

        def live():
            worst = jnp.maximum(jnp.maximum(car_ref[0], car_ref[1]), jnp.maximum(car_ref[2], car_ref[3]))
            return jnp.max(worst) >= SB_DEAD

        def cond(c):
            return jnp.logical_and(c[0] < qi, c[1])

        def loop(c):
            step([(qi - 1 - c[0], None)])
            return c[0] + 1, live()

        done, _ = lax.while_loop(cond, loop, (jnp.minimum(qi, 1), live()))
        tot_ref[:, pair(0)] = jnp.where(lane == 0, car_ref[0], car_ref[1])
        tot_ref[:, pair(2)] = jnp.where(lane == 0, car_ref[2], car_ref[3])
        cnt_ref[0, qi] = done.astype(F32)

    return pl.pallas_call(
        body, name="sb_fwd", grid=(nq,), in_specs=_sb_specs(T, bq),
        out_specs=[pl.BlockSpec((bq, 256), lambda i: (i, 0)), pl.BlockSpec((bq, 256), lambda i: (i, 0)),
                   pl.BlockSpec(memory_space=pltpu.SMEM)],
        out_shape=[jax.ShapeDtypeStruct((T, 256), F32), jax.ShapeDtypeStruct((T, 256), F32),
                   jax.ShapeDtypeStruct((1, nq), F32)],
        scratch_shapes=[pltpu.VMEM((4, bq, LANES), BF16), pltpu.VMEM((4, bq, LANES), F32)],
        compiler_params=_cparams(("arbitrary",)),
    )(hb, hb, hb)


def _sb_bwd(hb, tot, cnt, dy):
    T = hb.shape[0]
    bq = bk = ATT_BLK
    nq = T // bq

    def body(q_ref, k_ref, v_ref, tot_ref, dy_ref, cnt_ref, dq_ref, dk_ref, dv_ref, qm_ref, dob_ref, dqa_ref, rem_ref,
             cg_ref):
        qi = pl.program_id(0)

        @pl.when(qi == 0)
        def _():
            dk_ref[...] = jnp.zeros_like(dk_ref)
            dv_ref[...] = jnp.zeros_like(dv_ref)

        lane = lax.broadcasted_iota(jnp.int32, (1, LANES), 1) // HEAD
        row, col = _causal_masks(bq, bk)
        strict = col < row
        u = _prefix_ones(bk)
        pair = lambda h: slice((h // 2) * LANES, (h // 2 + 1) * LANES)
        dqa_ref[...] = jnp.zeros_like(dqa_ref)
        cg_ref[...] = jnp.zeros_like(cg_ref)
        for h in range(4):
            tot = tot_ref[:, pair(h)]
            qm_ref[h] = jnp.where(lane == h % 2, q_ref[:, pair(h)], 0) * 0.125
            dob_ref[h] = jnp.where(lane == h % 2, dy_ref[:, pair(h)], 0.0).astype(BF16)
            rem_ref[h] = jnp.where(lane == h % 2, tot, pltpu.roll(tot, HEAD, 1))

        def step(blocks):
            tile = lambda a: jnp.concatenate([a] * (bk // LANES), axis=1)
            nb = len(blocks)
            rows = [pl.ds(pl.multiple_of(kb * bk, bk), bk) for kb, _ in blocks]
            pairs = [(b, h) for b in range(nb) for h in range(4)]
            mask = lambda b, x: x if blocks[b][1] is None else jnp.where(blocks[b][1], x, 0.0)
            zs = {(b, h): _dot_nt(qm_ref[h], k_ref[rows[b], pair(h)]) for b, h in pairs}
            das = {(b, h): _dot_nt(dob_ref[h], jnp.where(lane == h % 2, v_ref[rows[b], pair(h)], 0)) for b, h in pairs}
            zls, splits = {}, {}
            for b, h in pairs:
                z = zs[b, h]
                lk = mask(b, jnp.minimum(-z, 0.0) - jnp.log(1.0 + jnp.exp(-jnp.abs(z))))
                zls[b, h] = z + lk
                splits[b, h] = _split(lk)
            pres = {bh: _dot(hi, u) + _dot(lo, u) for bh, (hi, lo) in splits.items()}
            rem = [rem_ref[h] for h in range(4)]
            aas, gs, gsplits = {}, {}, {}
            for b, h in pairs:
                a = mask(b, jnp.exp(zls[b, h] + (tile(rem[h]) - pres[b, h])))
                gs[b, h] = a * das[b, h]
                aas[b, h] = a.astype(BF16)
                gsplits[b, h] = _split(gs[b, h])
                rem[h] = rem[h] - jnp.broadcast_to(pres[b, h][:, bk - 1:bk], (bq, LANES))
            for b in range(nb):
                for p in (0, 2):
                    dv_ref[rows[b], pair(p)] += _dot_tn(aas[b, p], dob_ref[p]) + _dot_tn(aas[b, p + 1], dob_ref[p + 1])
            gpres = {bh: _dot(hi, u) + _dot(lo, u) for bh, (hi, lo) in gsplits.items()}
            cg = [cg_ref[h] for h in range(4)]
            dzs = {}
            for b, h in pairs:
                dz = mask(b, gs[b, h] - jnp.exp(zls[b, h]) * (tile(cg[h]) + gpres[b, h]))
                dzs[b, h] = dz.astype(BF16)
                cg[h] = cg[h] + jnp.broadcast_to(gpres[b, h][:, bk - 1:bk], (bq, LANES))
            for b in range(nb):
                for p in (0, 2):
                    dk_ref[rows[b], pair(p)] += _dot_tn(dzs[b, p], qm_ref[p]) + _dot_tn(dzs[b, p + 1], qm_ref[p + 1])
            for h in range(4):
                dq = dqa_ref[h]
                for b in range(nb):
                    dq = dq + _dot(dzs[b, h], k_ref[rows[b], pair(h)])
                dqa_ref[h] = dq
                rem_ref[h] = rem[h]
                cg_ref[h] = cg[h]

        def loop(kb, c):
            step([(kb, None)])
            return c

        start = qi - jnp.clip(cnt_ref[0, qi].astype(jnp.int32), 0, qi)
        lax.fori_loop(start, qi - 1, loop, 0)
        step([(jnp.maximum(qi - 1, 0), qi > 0), (qi, strict)])
        for p in (0, 2):
            dq_ref[:, pair(p)] = (jnp.where(lane == 0, dqa_ref[p], dqa_ref[p + 1]) * 0.125).astype(BF16)

    return pl.pallas_call(
        body, name="sb_bwd", grid=(nq,),
        in_specs=_sb_specs(T, bq) + [pl.BlockSpec((bq, 256), lambda i: (i, 0)),
                                     pl.BlockSpec((bq, 256), lambda i: (i, 0)),
                                     pl.BlockSpec(memory_space=pltpu.SMEM)],
        out_specs=[pl.BlockSpec((bq, 256), lambda i: (i, 0)),
                   pl.BlockSpec((T, 256), lambda i: (0, 0)),
                   pl.BlockSpec((T, 256), lambda i: (0, 0))],
        out_shape=[jax.ShapeDtypeStruct((T, 256), BF16)] + [jax.ShapeDtypeStruct((T, 256), F32)] * 2,
        scratch_shapes=[pltpu.VMEM((4, bq, LANES), BF16), pltpu.VMEM((4, bq, LANES), BF16),
                        pltpu.VMEM((4, bq, LANES), F32), pltpu.VMEM((4, bq, LANES), F32),
                        pltpu.VMEM((4, bq, LANES), F32)],
        compiler_params=_cparams(("arbitrary",)),
    )(hb, hb, hb, tot, dy, cnt)


EP_TM = 512


def _ep_in_specs(tm, rev):
    idx = (lambda i: rev - i) if rev is not None else (lambda i: i)
    bo = (_INT_OFF["b_b"] - N_HB) // 256
    halo = lambda i: jnp.maximum(idx(i) * (tm // 8) - 1, 0)
    return [pl.BlockSpec((tm, 256), lambda i: (idx(i), 0)),
            pl.BlockSpec((tm, 256), lambda i: (idx(i), 0)),
            pl.BlockSpec((tm, 256), lambda i: (idx(i), 0)),
            pl.BlockSpec((tm, D_MODEL), lambda i: (idx(i), 0)),
            pl.BlockSpec((tm, 256), lambda i: (idx(i), bo)),
            pl.BlockSpec((tm, 256), lambda i: (idx(i), bo + 1)),
            pl.BlockSpec((tm, 256), lambda i: (idx(i), bo + 2)),
            pl.BlockSpec((8, 256), lambda i: (halo(i), bo + 1)),
            pl.BlockSpec((8, 256), lambda i: (halo(i), bo + 2)),
            pl.BlockSpec((3, 256), lambda i: (0, 0)),
            pl.BlockSpec((1, 256), lambda i: (0, 0)),
            pl.BlockSpec((1, D_MODEL), lambda i: (0, 0)),
            pl.BlockSpec((D_MODEL, D_MODEL), lambda i: (0, 0)),
            pl.BlockSpec((1, D_MODEL), lambda i: (0, 0))]


def _ep_mix(first, ya_ref, yc_ref, yd_ref, gate_ref, bb_ref, bc_ref, bx_ref, hc_ref, hx_ref, cw_ref, cb_ref, gg_ref):
    tm = ya_ref.shape[0]
    u = bc_ref[...] * bx_ref[...]
    halo = jnp.where(first, 0.0, hc_ref[...] * hx_ref[...])
    row = lax.broadcasted_iota(jnp.int32, (tm, 1), 0)
    u1 = jnp.where(row == 0, halo[7:8, :], pltpu.roll(u, 1, 0))
    u2 = jnp.where(row == 0, halo[6:7, :], jnp.where(row == 1, halo[7:8, :], pltpu.roll(u, 2, 0)))
    cw = cw_ref[...]
    conv = cw[0:1, :] * u2 + cw[1:2, :] * u1 + cw[2:3, :] * u + cb_ref[...]
    bb = bb_ref[...]
    ys = [ya_ref[...], bb * conv, yc_ref[...], yd_ref[...]]
    rs = [_rms(y) for y in ys]
    gg = gg_ref[...]
    yhat = jnp.concatenate([y * r for y, r in zip(ys, rs)], axis=1)
    gate = gate_ref[...]
    sig = 1.0 / (1.0 + jnp.exp(-gate))
    return u, u1, u2, conv, bb, rs, yhat, yhat * gg, gate, sig


def _epilogue_fwd(x, ya, yc, yd, hf, conv_w, conv_b, g_grp, w_out, g_post, tgt=None):
    T = x.shape[0]
    tm = EP_TM
    row_spec = pl.BlockSpec((tm, D_MODEL), lambda i: (i, 0))

    def layer_out(refs):
        (x_ref, ya_ref, yc_ref, yd_ref, gate_ref, bb_ref, bc_ref, bx_ref, hc_ref, hx_ref, cw_ref, cb_ref,
         gg_ref, wo_ref, gp_ref) = refs
        (_, _, _, _, _, _, _, yn, gate, sig) = _ep_mix(
            pl.program_id(0) == 0, ya_ref, yc_ref, yd_ref, gate_ref, bb_ref, bc_ref, bx_ref, hc_ref, hx_ref,
            cw_ref, cb_ref, gg_ref)
        z = _dot((yn * (gate * sig)).astype(BF16), wo_ref[...])
        return x_ref[...] + z * _rms(z) * gp_ref[...]

    args = (x, ya, yc, yd, hf, hf, hf, hf, hf, hf, conv_w, conv_b, g_grp, w_out, g_post)
    in_specs = [row_spec] + _ep_in_specs(tm, None)
    if tgt is None:
        def body(*refs):
            refs[-1][...] = layer_out(refs[:-1])

        return pl.pallas_call(
            body, name="epilogue_fwd", grid=(T // tm,), in_specs=in_specs, out_specs=row_spec,
            out_shape=jax.ShapeDtypeStruct((T, D_MODEL), F32), compiler_params=_cparams(("parallel",)),
        )(*args)

    def body_loss(*refs):
        t_ref, dy_ref, l_ref = refs[-3:]

        @pl.when(pl.program_id(0) == 0)
        def _():
            l_ref[...] = jnp.zeros_like(l_ref)

        d = layer_out(refs[:-3]) - t_ref[...]
        dy_ref[...] = d * (1.0 / D_MODEL)
        part = jnp.sum(jnp.sum(d * d, axis=1, keepdims=True), axis=0, keepdims=True)
        l_ref[...] += part * (0.5 / D_MODEL)

    return pl.pallas_call(
        body_loss, name="epilogue_fwd_loss", grid=(T // tm,), in_specs=in_specs + [row_spec],
        out_specs=[row_spec, pl.BlockSpec((8, LANES), lambda i: (0, 0))],
        out_shape=[jax.ShapeDtypeStruct((T, D_MODEL), F32), jax.ShapeDtypeStruct((8, LANES), F32)],
        compiler_params=_cparams(("arbitrary",)),
    )(*args, tgt)


def _epilogue_bwd(dxn, ya, yc, yd, hf, conv_w, conv_b, g_grp, w_out, g_post):
    T = dxn.shape[0]
    tm = EP_TM
    nt = T // tm
    ridx = lambda i: (nt - 1 - i, 0)

    def body(dx_ref, ya_ref, yc_ref, yd_ref, gate_ref, bb_ref, bc_ref, bx_ref, hc_ref, hx_ref, cw_ref, cb_ref,
             gg_ref, wo_ref, gp_ref,
             dya_ref, dyc_ref, dyd_ref, dhf_ref, dwo_ref, dgp_ref, dgg_ref, dcw_ref, dcb_ref, carry_ref):
        i = pl.program_id(0)

        @pl.when(i == 0)
        def _():
            for r in (dwo_ref, dgp_ref, dgg_ref, dcw_ref, dcb_ref, carry_ref):
                r[...] = jnp.zeros_like(r)

        (u, u1, u2, conv, bb, rs, yhat, yn, gate, sig) = _ep_mix(
            i == nt - 1, ya_ref, yc_ref, yd_ref, gate_ref, bb_ref, bc_ref, bx_ref, hc_ref, hx_ref,
            cw_ref, cb_ref, gg_ref)
        silu = gate * sig
        ymix = (yn * silu).astype(BF16)
        z = _dot(ymix, wo_ref[...])
        rz = _rms(z)
        dz, dgrow = _rms_bwd(dx_ref[...], z * rz, rz, gp_ref[...])
        dgp_ref[...] += _colsum(dgrow)
        dzb = dz.astype(BF16)
        dwo_ref[...] += _dot_tn(ymix, dzb)
        dymix = _dot_nt(dzb, wo_ref[...])
        dhf_ref[:, 0:D_MODEL] = (dymix * yn * (sig * (1.0 + gate * (1.0 - sig)))).astype(BF16)
        dyn = dymix * silu
        dgg_ref[...] += _colsum(dyn * yhat)
        gg = gg_ref[...]
        dys = []
        for gi in range(4):
            sl = slice(gi * GROUP, (gi + 1) * GROUP)
            dyh = dyn[:, sl] * gg[:, sl]
            yh = yhat[:, sl]
            dys.append(rs[gi] * (dyh - yh * jnp.mean(dyh * yh, axis=-1, keepdims=True)))
        dya_ref[...] = dys[0]
        dyc_ref[...] = dys[2]
        dyd_ref[...] = dys[3]
        dyb = dys[1]
        dhf_ref[:, D_MODEL:D_MODEL + 256] = (dyb * conv).astype(BF16)
        dconv = dyb * bb
        dcb_ref[...] += _colsum(dconv)
        dcw_ref[0:1, :] += _colsum(dconv * u2)
        dcw_ref[1:2, :] += _colsum(dconv * u1)
        dcw_ref[2:3, :] += _colsum(dconv * u)
        carry = carry_ref[...]
        row = lax.broadcasted_iota(jnp.int32, (tm, 1), 0)
        d1 = jnp.where(row == tm - 1, carry[0:1, :], pltpu.roll(dconv, tm - 1, 0))
        d2 = jnp.where(row == tm - 2, carry[0:1, :],
                       jnp.where(row == tm - 1, carry[1:2, :], pltpu.roll(dconv, tm - 2, 0)))
        cw = cw_ref[...]
        du = cw[2:3, :] * dconv + cw[1:2, :] * d1 + cw[0:1, :] * d2
        dhf_ref[:, D_MODEL + 256:D_MODEL + 512] = (du * bx_ref[...]).astype(BF16)
        dhf_ref[:, D_MODEL + 512:D_MODEL + 768] = (du * bc_ref[...]).astype(BF16)
        carry_ref[...] = dconv[0:8, :]

    in_specs = [pl.BlockSpec((tm, D_MODEL), ridx)] + _ep_in_specs(tm, nt - 1)
    return pl.pallas_call(
        body, name="epilogue_bwd", grid=(nt,), in_specs=in_specs,
        out_specs=[pl.BlockSpec((tm, 256), ridx), pl.BlockSpec((tm, 256), ridx), pl.BlockSpec((tm, 256), ridx),
                   pl.BlockSpec((tm, D_MODEL + 768), ridx),
                   pl.BlockSpec((D_MODEL, D_MODEL), lambda i: (0, 0)),
                   pl.BlockSpec((1, D_MODEL), lambda i: (0, 0)),
                   pl.BlockSpec((1, D_MODEL), lambda i: (0, 0)),
                   pl.BlockSpec((8, 256), lambda i: (0, 0)),
                   pl.BlockSpec((1, 256), lambda i: (0, 0))],
        out_shape=[jax.ShapeDtypeStruct((T, 256), F32)] * 3
                  + [jax.ShapeDtypeStruct((T, D_MODEL + 768), BF16),
                     jax.ShapeDtypeStruct((D_MODEL, D_MODEL), F32),
                     jax.ShapeDtypeStruct((1, D_MODEL), F32),
                     jax.ShapeDtypeStruct((1, D_MODEL), F32),
                     jax.ShapeDtypeStruct((8, 256), F32),
                     jax.ShapeDtypeStruct((1, 256), F32)],
        scratch_shapes=[pltpu.VMEM((8, 256), F32)],
        compiler_params=_cparams(("arbitrary",)),
    )(dxn, ya, yc, yd, hf, hf, hf, hf, hf, hf, conv_w, conv_b, g_grp, w_out, g_post)


def _place():
    return lax.axis_index("x"), lax.axis_index("y"), lax.axis_index("c")


def _other_chips(x, y):
    return [(1 - x, y), (x, 1 - y), (1 - x, 1 - y)]


HBM = pl.BlockSpec(memory_space=pl.ANY)


def _gather_plan(ins, outs, sems):
    n = len(ins)
    ici_send, ici_recv, d2d_send, d2d_recv, local_sems = sems
    x, y, c = _place()
    me = 2 * x + y
    chips = _other_chips(x, y)

    def ici(a, j, chip_from):
        px, py = chips[j]
        return pltpu.make_async_remote_copy(
            src_ref=ins[a].at[c], dst_ref=outs[a].at[chip_from, c], send_sem=ici_send.at[3 * a + j],
            recv_sem=ici_recv.at[3 * a + j], device_id=(px, py, c), device_id_type=MESH)

    def d2d(a, j, part):
        px, py = chips[j]
        blk = outs[a].at[2 * px + py, part]
        return pltpu.make_async_remote_copy(
            src_ref=blk, dst_ref=blk, send_sem=d2d_send.at[3 * a + j], recv_sem=d2d_recv.at[3 * a + j],
            device_id=(x, y, 1 - c), device_id_type=MESH)

    def local(a):
        return pltpu.make_async_copy(ins[a], outs[a].at[me], local_sems.at[a])

    hops = [(j, a) for j in range(3) for a in range(n)]

    def start():
        for a in range(n):
            local(a).start()
        for j, a in hops:
            ici(a, j, me).start()

    def finish():
        for j, a in hops:
            ici(a, j, 2 * chips[j][0] + chips[j][1]).wait_recv()
            d2d(a, j, c).start()
        for j, a in hops:
            d2d(a, j, 1 - c).wait_recv()
        for j, a in hops:
            ici(a, j, me).wait_send()
            d2d(a, j, c).wait_send()
        for a in range(n):
            local(a).wait()

    return start, finish


def _gather_sems(n):
    return [pltpu.SemaphoreType.DMA((3 * n,))] * 4 + [pltpu.SemaphoreType.DMA((n,))]


def _gather_weights(shards):
    n = len(shards)

    def body(*refs):
        start, finish = _gather_plan(refs[:n], refs[n:2 * n], refs[2 * n:])
        start()
        finish()

    return pl.pallas_call(
        body, name="gather_weights",
        in_specs=[HBM] * n, out_specs=[HBM] * n,
        out_shape=[jax.ShapeDtypeStruct((4,) + s.shape, s.dtype) for s in shards],
        scratch_shapes=_gather_sems(n),
    )(*shards)


def _exchange_chips(parts, small):
    n = len(parts)

    def body(*refs):
        ins, sm_ref = refs[:n], refs[n]
        outs, osm_ref = refs[n + 1:2 * n + 1], refs[2 * n + 1]
        send_sems, recv_sems, ssend_sems, srecv_sems, local_sems = refs[2 * n + 2:]
        x, y, c = _place()
        me = 2 * x + y
        dev = 4 * x + 2 * y + c
        local = [pltpu.make_async_copy(ins[a].at[me], outs[a].at[me], local_sems.at[a]) for a in range(n)]
        local.append(pltpu.make_async_copy(sm_ref, osm_ref.at[dev], local_sems.at[n]))
        for cp in local:
            cp.start()
        sends = []
        for j, (px, py) in enumerate(_other_chips(x, y)):
            for a in range(n):
                cp = pltpu.make_async_remote_copy(
                    src_ref=ins[a].at[2 * px + py], dst_ref=outs[a].at[me], send_sem=send_sems.at[3 * a + j],
                    recv_sem=recv_sems.at[3 * a + j], device_id=(px, py, c), device_id_type=MESH)
                cp.start()
                sends.append(cp)
        flips = [(fx, fy, fc) for fx in (0, 1) for fy in (0, 1) for fc in (0, 1)][1:]
        for j, (fx, fy, fc) in enumerate(flips):
            cp = pltpu.make_async_remote_copy(
                src_ref=sm_ref, dst_ref=osm_ref.at[dev], send_sem=ssend_sems.at[j], recv_sem=srecv_sems.at[j],
                device_id=(x ^ fx, y ^ fy, c ^ fc), device_id_type=MESH)
            cp.start()
            sends.append(cp)
        for j, (px, py) in enumerate(_other_chips(x, y)):
            for a in range(n):
                pltpu.make_async_remote_copy(
                    src_ref=ins[a].at[me], dst_ref=outs[a].at[2 * px + py], send_sem=send_sems.at[3 * a + j],
                    recv_sem=recv_sems.at[3 * a + j], device_id=(px, py, c), device_id_type=MESH).wait_recv()
        for j, (fx, fy, fc) in enumerate(flips):
            src = 4 * (x ^ fx) + 2 * (y ^ fy) + (c ^ fc)
            pltpu.make_async_remote_copy(
                src_ref=sm_ref, dst_ref=osm_ref.at[src], send_sem=ssend_sems.at[j], recv_sem=srecv_sems.at[j],
                device_id=(x ^ fx, y ^ fy, c ^ fc), device_id_type=MESH).wait_recv()
        for cp in sends:
            cp.wait_send()
        for cp in local:
            cp.wait()

    return pl.pallas_call(
        body, name="exchange_chips",
        in_specs=[HBM] * (n + 1), out_specs=[HBM] * (n + 1),
        out_shape=[jax.ShapeDtypeStruct(p.shape, p.dtype) for p in parts]
                  + [jax.ShapeDtypeStruct((8,) + small.shape, small.dtype)],
        scratch_shapes=[pltpu.SemaphoreType.DMA((3 * n,)), pltpu.SemaphoreType.DMA((3 * n,)),
                        pltpu.SemaphoreType.DMA((7,)), pltpu.SemaphoreType.DMA((7,)),
                        pltpu.SemaphoreType.DMA((n + 1,))],
    )(*parts, small)


def _swap_cores(parts, name):
    n = len(parts)

    def body(*refs):
        ins, outs, send_sems, recv_sems = refs[:n], refs[n:2 * n], refs[2 * n], refs[2 * n + 1]
        x, y, c = _place()
        copies = [pltpu.make_async_remote_copy(
            src_ref=ins[a], dst_ref=outs[a], send_sem=send_sems.at[a], recv_sem=recv_sems.at[a],
            device_id=(x, y, 1 - c), device_id_type=MESH) for a in range(n)]
        for cp in copies:
            cp.start()
        for cp in copies:
            cp.wait()

    return pl.pallas_call(
        body, name=name, in_specs=[HBM] * n, out_specs=[HBM] * n,
        out_shape=[jax.ShapeDtypeStruct(p.shape, p.dtype) for p in parts],
        scratch_shapes=[pltpu.SemaphoreType.DMA((n,)), pltpu.SemaphoreType.DMA((n,))],
    )(*parts)


def _tile(rows, cols):
    for cand in (256, 128, 64):
        if rows % cand == 0:
            return cand, cols
    if rows > 64 and cols % 256 == 0:
        return rows, 256
    return rows, cols


def _add(a, b, name):
    L, R, C = a.shape
    tr, tc = _tile(R, C)

    def body(a_ref, b_ref, o_ref):
        o_ref[...] = (a_ref[...] + b_ref[...]).astype(BF16)

    spec = pl.BlockSpec((1, tr, tc), lambda l, i, j: (l, i, j))
    return pl.pallas_call(
        body, name=name, grid=(L, R // tr, C // tc), in_specs=[spec, spec], out_specs=spec,
        out_shape=jax.ShapeDtypeStruct((L, R, C), BF16),
        compiler_params=_cparams(("parallel", "parallel", "parallel")),
    )(a, b)


def _sum_leading(buf, name):
    n, R, C = buf.shape
    tr, tc = _tile(R, C)

    def body(b_ref, o_ref):
        acc = b_ref[0].astype(F32)
        for k in range(1, n):
            acc = acc + b_ref[k].astype(F32)
        o_ref[...] = acc

    return pl.pallas_call(
        body, name=name, grid=(R // tr, C // tc),
        in_specs=[pl.BlockSpec((n, tr, tc), lambda i, j: (0, i, j))],
        out_specs=pl.BlockSpec((tr, tc), lambda i, j: (i, j)),
        out_shape=jax.ShapeDtypeStruct((R, C), F32),
        compiler_params=_cparams(("parallel", "parallel")),
    )(buf)


def _adam_update(w, g, m, v):
    c1 = 1.0 / (1.0 - ADAM_B1 ** ADAM_STEP)
    c2 = 1.0 / (1.0 - ADAM_B2 ** ADAM_STEP)
    mn = ADAM_B1 * m + (1.0 - ADAM_B1) * g
    vn = ADAM_B2 * v + (1.0 - ADAM_B2) * (g * g)
    return -ADAM_LR * ((mn * c1) / (jnp.sqrt(vn * c2) + ADAM_EPS) + ADAM_WD * w), mn, vn


def _adamw_layers(w, m, v, g_mine, g_other, name):
    _, R, C = w.shape
    tr, tc = _tile(R, C)

    def body(w_ref, m_ref, v_ref, gm_ref, go_ref, g_ref, d_ref, mo_ref, vo_ref):
        g = jnp.where(pl.program_id(0) == lax.axis_index("c"), gm_ref[...], go_ref[...])
        g_ref[0] = g
        d_ref[0], mo_ref[0], vo_ref[0] = _adam_update(w_ref[0], g, m_ref[0], v_ref[0])

    spec3 = pl.BlockSpec((1, tr, tc), lambda l, i, j: (l, i, j))
    spec2 = pl.BlockSpec((tr, tc), lambda l, i, j: (i, j))
    return pl.pallas_call(
        body, name=name, grid=(2, R // tr, C // tc),
        in_specs=[spec3] * 3 + [spec2] * 2, out_specs=[spec3] * 4,
        out_shape=[jax.ShapeDtypeStruct(w.shape, F32)] * 4,
        compiler_params=_cparams(("parallel", "parallel", "parallel")),
    )(w, m, v, g_mine, g_other)


PACK_C = 1024
_BIG = ("w_in", "w_out", "mla_w_uq", "mla_w_ukv", "conv_w")
_SMALL = ("norm_pre", "group_norm", "norm_post", "conv_b", "mla_q_norm", "mla_kv_norm", "attn_sinks")
_SMALL_W = {"norm_pre": 1024, "group_norm": 1024, "norm_post": 1024, "conv_b": 256, "mla_q_norm": 256,
            "mla_kv_norm": 128, "attn_sinks": 4}


_LOSS_AT = divmod(DEPTH * sum(_SMALL_W.values()), PACK_C)


def _pack_small(d, loss):
    flat = jnp.concatenate([d[n].reshape(-1) for n in _SMALL] + [loss.reshape(1)])
    return jnp.pad(flat, (0, 8 * PACK_C - flat.shape[0])).reshape(8, PACK_C)


def _adamw_small(w, m, v, got):
    ns = len(_SMALL)

    def body(*refs):
        got_ref = refs[3 * ns]
        outs = refs[3 * ns + 1:]
        gsum = got_ref[0]
        for d in range(1, 8):
            gsum = gsum + got_ref[d]
        outs[4 * ns][...] = gsum[_LOSS_AT[0]:_LOSS_AT[0] + 1, _LOSS_AT[1]:_LOSS_AT[1] + 1]
        off = 0
        for i, name in enumerate(_SMALL):
            wd = _SMALL_W[name]
            rows = []
            for l in range(DEPTH):
                r, c0 = divmod(off + l * wd, PACK_C)
                rows.append(gsum[r:r + 1, c0:c0 + wd])
            off += DEPTH * wd
            g = jnp.concatenate(rows, axis=0)
            delta, mn, vn = _adam_update(refs[i][...], g, refs[ns + i][...], refs[2 * ns + i][...])
            outs[i][...] = g
            outs[ns + i][...] = delta
            outs[2 * ns + i][...] = mn
            outs[3 * ns + i][...] = vn

    shapes = [jax.ShapeDtypeStruct(w[n].shape, F32) for n in _SMALL]
    res = pl.pallas_call(body, name="adamw_small", out_shape=shapes * 4 + [jax.ShapeDtypeStruct((1, 1), F32)])(
        *[w[n] for n in _SMALL], *[m[n] for n in _SMALL], *[v[n] for n in _SMALL], got)
    return [dict(zip(_SMALL, res[k * ns:(k + 1) * ns])) for k in range(4)], res[4 * ns]


def _w_in_internal(slabs):
    S, R, D = slabs.shape
    assert S * R == D_IN

    def body(w_ref, o_ref):
        for n, wd in _REAL:
            o, oi = _REAL_OFF[n][0], _INT_OFF[n]
            r = o
            while r < o + wd:
                end = min(o + wd, (r // R + 1) * R)
                o_ref[oi + r - o:oi + end - o, :] = w_ref[r // R, r % R:r % R + end - r, :]
                r = end
            if _INT_W[n] != wd:
                o_ref[oi + wd:oi + _INT_W[n], :] = jnp.zeros((_INT_W[n] - wd, D), slabs.dtype)

    return pl.pallas_call(
        body, name="w_in_internal", out_shape=jax.ShapeDtypeStruct((N_INT, D), slabs.dtype),
        compiler_params=pltpu.CompilerParams(vmem_limit_bytes=VMEM_LIMIT),
    )(slabs)


def _uq_internal(w):
    return jnp.pad(w.reshape(256, 4, 96), ((0, 0), (0, 0), (0, 32))).reshape(256, 512)


def _uq_real(dw):
    return dw.reshape(256, 4, 128)[:, :, :96].reshape(256, 384)


def _ukv_internal(w):
    w4 = w.reshape(128, 4, 128)
    k = jnp.pad(w4[:, :, :64], ((0, 0), (0, 0), (0, 64))).reshape(128, 512)
    return jnp.concatenate([k, w4[:, :, 64:].reshape(128, 256)], axis=1)


def _ukv_real(dw):
    k = dw[:, :512].reshape(128, 4, 128)[:, :, :64]
    v = dw[:, 512:].reshape(128, 4, 64)
    return jnp.concatenate([k, v], axis=2).reshape(128, 512)


def _layer_fwd(x, rope, p, tgt=None, fetch=()):
    xn, hb, hf, *fetched = _inproj_fwd(x, p["norm_pre"], p["w_in"], fetch)
    ya = _swa_fwd(hb, p["attn_sinks"])
    qm, km, vm, vt = _mla_prep_fwd(hf, rope, p["mla_q_norm"], p["mla_kv_norm"], p["mla_w_uq"], p["mla_w_ukv"])
    yc, lse = _mla_fwd(qm, km, vt)
    yd, tot, cnt = _sb_fwd(hb)
    x_next = _epilogue_fwd(x, ya, yc, yd, hf, p["conv_w"], p["conv_b"], p["group_norm"], p["w_out"], p["norm_post"],
                           tgt)
    saved = dict(x=x, xn=xn, hb=hb, hf=hf, ya=ya, yc=yc, yd=yd, tot=tot, cnt=cnt, qm=qm, km=km, vm=vm, lse=lse)
    return x_next, saved, fetched


def _layer_bwd(dx_next, rope, p, s):
    (dya, dyc, dyd, dhf, dw_out, dg_post, dg_grp, dconv_w, dconv_b) = _epilogue_bwd(
        dx_next, s["ya"], s["yc"], s["yd"], s["hf"], p["conv_w"], p["conv_b"], p["group_norm"], p["w_out"],
        p["norm_post"])
    dq_d, dk_d, dv_d = _sb_bwd(s["hb"], s["tot"], s["cnt"], dyd)
    dqm, dkt, dvt = _mla_bwd(s["qm"], s["km"], s["vm"], s["yc"], s["lse"], dyc)
    dc, dw_uq, dw_ukv, dg_q, dg_kv = _mla_prep_bwd(
        s["hf"], rope, p["mla_q_norm"], p["mla_kv_norm"], p["mla_w_uq"], p["mla_w_ukv"], dqm, dkt, dvt)
    dq_a, dk_a, dv_a, dsinks = _swa_bwd(s["hb"], p["attn_sinks"], dya)
    dx, dh, dg_pre = _inproj_bwd_dx(s["x"], p["norm_pre"], p["w_in"], dx_next,
                                    [dq_a, dk_a, dv_a, dq_d, dk_d, dv_d, dhf, dc])
    grads = dict(norm_pre=dg_pre[0], w_in_t=_inproj_bwd_dw(s["xn"], dh), attn_sinks=dsinks[0, :4], conv_w=dconv_w[:3],
                 conv_b=dconv_b[0], mla_q_norm=dg_q[0], mla_w_uq=_uq_real(dw_uq), mla_kv_norm=dg_kv[0],
                 mla_w_ukv=_ukv_real(dw_ukv), group_norm=dg_grp[0], w_out=dw_out, norm_post=dg_post[0])
    return dx, grads


_WEIGHTS = ["norm_pre", "w_in", "attn_sinks", "conv_w", "conv_b", "mla_q_norm", "mla_w_uq", "mla_kv_norm",
            "mla_w_ukv", "group_norm", "w_out", "norm_post"]


def kernel(x, positions, norm_pre, w_in, attn_sinks, conv_w, conv_b, mla_q_norm, mla_w_uq, mla_kv_norm, mla_w_ukv, group_norm, w_out, norm_post, loss_target, m_norm_pre, m_w_in, m_attn_sinks, m_conv_w, m_conv_b, m_mla_q_norm, m_mla_w_uq, m_mla_kv_norm, m_mla_w_ukv, m_group_norm, m_w_out, m_norm_post, v_norm_pre, v_w_in, v_attn_sinks, v_conv_w, v_conv_b, v_mla_q_norm, v_mla_w_uq, v_mla_kv_norm, v_mla_w_ukv, v_group_norm, v_w_out, v_norm_post):
    w = dict(norm_pre=norm_pre, w_in=w_in, attn_sinks=attn_sinks, conv_w=conv_w, conv_b=conv_b,
             mla_q_norm=mla_q_norm, mla_w_uq=mla_w_uq, mla_kv_norm=mla_kv_norm, mla_w_ukv=mla_w_ukv,
             group_norm=group_norm, w_out=w_out, norm_post=norm_post)
    m = dict(norm_pre=m_norm_pre, w_in=m_w_in, attn_sinks=m_attn_sinks, conv_w=m_conv_w, conv_b=m_conv_b,
             mla_q_norm=m_mla_q_norm, mla_w_uq=m_mla_w_uq, mla_kv_norm=m_mla_kv_norm, mla_w_ukv=m_mla_w_ukv,
             group_norm=m_group_norm, w_out=m_w_out, norm_post=m_norm_post)
    v = dict(norm_pre=v_norm_pre, w_in=v_w_in, attn_sinks=v_attn_sinks, conv_w=v_conv_w, conv_b=v_conv_b,
             mla_q_norm=v_mla_q_norm, mla_w_uq=v_mla_w_uq, mla_kv_norm=v_mla_kv_norm, mla_w_ukv=v_mla_w_ukv,
             group_norm=v_group_norm, w_out=v_w_out, norm_post=v_norm_post)
    T = x.shape[1]
    xs = x[0]
    rope = _rope_tables(positions[0].reshape(T, 1))
    tgt = loss_target[0]
    core = lax.axis_index("c")

    def shard_parts(l):
        halves = lambda a: a.reshape((2, a.shape[0] // 2) + a.shape[1:])
        return [halves(jnp.swapaxes(w["w_in"][l], 0, 1).astype(BF16))] + [
            halves(w[n][l].astype(BF16)) for n in _BIG[1:4]] + [jnp.stack([w["conv_w"][l]] * 2)]

    def layer_params(l, got):
        whole = lambda a: a.reshape((4, 2 * a.shape[2]) + a.shape[3:])
        by_cols = lambda a: jnp.transpose(a, (1, 0, 2)).reshape(a.shape[1], 4 * a.shape[2])
        return dict(
            norm_pre=norm_pre[l:l + 1], w_in=_w_in_internal(got[0].reshape((8,) + got[0].shape[2:])),
            attn_sinks=attn_sinks[l], conv_w=by_cols(got[4][:, 0]), conv_b=conv_b[l:l + 1],
            mla_q_norm=mla_q_norm[l:l + 1], mla_w_uq=_uq_internal(by_cols(whole(got[2]))),
            mla_kv_norm=mla_kv_norm[l:l + 1], mla_w_ukv=_ukv_internal(by_cols(whole(got[3]))),
            group_norm=group_norm[l:l + 1], w_out=whole(got[1]).reshape(D_MODEL, D_MODEL),
            norm_post=norm_post[l:l + 1])

    layers, saved = [], []
    h, got = xs, _gather_weights(shard_parts(0))
    for l in range(DEPTH):
        last = l == DEPTH - 1
        layers.append(layer_params(l, got))
        h, s, got = _layer_fwd(h, rope, layers[l], tgt if last else None, () if last else shard_parts(l + 1))
        saved.append(s)
    dy, loss_part = h

    grads = [None] * DEPTH
    for l in reversed(range(DEPTH)):
        dy, grads[l] = _layer_bwd(dy, rope, layers[l], saved[l])

    turned = ("w_in", "mla_w_uq")
    turn = lambda n, a: jnp.swapaxes(a, -1, -2) if n in turned else a

    def chunks(n, a):
        if n in ("w_out", "w_in"):
            return a.reshape(4, a.shape[0] // 4, a.shape[1])
        if n in turned:
            return a.T.reshape(4, a.shape[1] // 4, a.shape[0])
        return jnp.transpose(a.reshape(a.shape[0], 4, a.shape[1] // 4), (1, 0, 2))

    grad = lambda l, n: grads[l]["w_in_t" if n == "w_in" else n]
    mine = [chunks(n, jnp.where(core == 0, grad(0, n), grad(1, n))) for n in _BIG]
    theirs = [chunks(n, jnp.where(core == 0, grad(1, n), grad(0, n))) for n in _BIG]
    from_sibling = _swap_cores(theirs, "swap_layer_chunks")
    summed = [_add(a, b, "add_cores_" + n) for n, a, b in zip(_BIG, mine, from_sibling)]
    small = _pack_small({n: jnp.stack([grads[l][n] for l in range(DEPTH)]) for n in _SMALL}, loss_part[0, 0])
    *got, got_small = _exchange_chips(summed, small)
    done = [_sum_leading(b, "sum_chips_" + n) for n, b in zip(_BIG, got)]
    done_other = _swap_cores(done, "swap_layer_shards")

    outs, loss = _adamw_small(w, m, v, got_small)
    for n, gm, go in zip(_BIG, done, done_other):
        for d, a in zip(outs, _adamw_layers(turn(n, w[n]), turn(n, m[n]), turn(n, v[n]), gm, go, "adamw_" + n)):
            d[n] = turn(n, a)
    return (loss[0, 0], dy[None], *[outs[0][n] for n in _WEIGHTS], *[outs[1][n] for n in _WEIGHTS],
            *[outs[2][n] for n in _WEIGHTS], *[outs[3][n] for n in _WEIGHTS])
```

```python
import math

import jax
import jax.numpy as jnp
from jax import lax
from jax.experimental import pallas as pl
from jax.experimental.pallas import tpu as pltpu

F32 = jnp.float32
BF16 = jnp.bfloat16
MESH = pl.DeviceIdType.MESH

D_MODEL = 1024
DEPTH = 2
EPS = 1e-6
BLOCK = 128
HEAD = 64
LANES = 128
GROUP = 256
LOG2E = 1.4426950408889634
LN2 = 0.6931471805599453
MLA_QSCALE = 96 ** -0.5 * LOG2E
ROPE_HALF = 16
ROPE_THETA = 10000.0
SWA_SUB = 2
ATT_BLK = 256
MLA_BQ = 512
NEG = -1e30
SB_DEAD = -104.0

ADAM_LR, ADAM_B1, ADAM_B2, ADAM_EPS, ADAM_WD, ADAM_STEP = 0.001, 0.9, 0.999, 1e-08, 0.01, 10

_REAL = [("a_q", 256), ("a_k", 128), ("a_v", 128), ("b_b", 256), ("b_c", 256), ("b_x", 256),
         ("c_q", 256), ("c_kv", 128), ("c_kr", 32), ("d_q", 256), ("d_k", 256), ("d_v", 256),
         ("gate", 1024)]
_REAL_OFF = {}
_o = 0
for _n, _w in _REAL:
    _REAL_OFF[_n] = (_o, _w)
    _o += _w
D_IN = _o
_INT_ORDER = ["a_q", "a_k", "a_v", "d_q", "d_k", "d_v", "gate", "b_b", "b_c", "b_x", "c_q", "c_kv", "c_kr"]
_INT_W = dict(_REAL)
_INT_W["c_kr"] = 128
_INT_OFF = {}
_o = 0
for _n in _INT_ORDER:
    _INT_OFF[_n] = _o
    _o += _INT_W[_n]
N_INT = _o
N_HB = _INT_OFF["gate"]
N_HF = N_INT - N_HB

VMEM_LIMIT = 56 * 1024 * 1024


def _cparams(sem):
    return pltpu.CompilerParams(dimension_semantics=sem, vmem_limit_bytes=VMEM_LIMIT)


def _dot(a, b):
    return jnp.dot(a, b, preferred_element_type=F32)


def _dot_nt(a, b):
    return lax.dot_general(a, b, (((1,), (1,)), ((), ())), preferred_element_type=F32)


def _dot_tn(a, b):
    return lax.dot_general(a, b, (((0,), (0,)), ((), ())), preferred_element_type=F32)


def _split(x):
    hi = x.astype(BF16)
    lo = (x - hi.astype(F32)).astype(BF16)
    return hi, lo


def _rms(x):
    return lax.rsqrt(jnp.mean(x * x, axis=-1, keepdims=True) + EPS)


def _rms_bwd(dy, xhat, r, g):
    dxhat = dy * g
    return r * (dxhat - xhat * jnp.mean(dxhat * xhat, axis=-1, keepdims=True)), dy * xhat


def _colsum(x):
    return jnp.sum(x, axis=0, keepdims=True)


def _inproj_fwd(x, g, wt, fetch=()):
    T = x.shape[0]
    tm = 512
    nt, n = T // tm, len(fetch)

    def body(x_ref, g_ref, w_ref, *rest):
        xn_ref, hb_ref, hf_ref = rest[n:n + 3]
        if n:
            start, finish = _gather_plan(rest[:n], rest[n + 3:2 * n + 3], rest[2 * n + 3:])
            pl.when(pl.program_id(0) == 0)(start)
        xv = x_ref[...]
        xn = (xv * _rms(xv) * g_ref[...]).astype(BF16)
        xn_ref[...] = xn
        h = _dot_nt(xn, w_ref[...])
        hb_ref[...] = h[:, :N_HB].astype(BF16)
        hf_ref[...] = h[:, N_HB:]
        if n:
            pl.when(pl.program_id(0) == nt - 1)(finish)

    return pl.pallas_call(
        body, name="inproj_fwd_fetch" if n else "inproj_fwd", grid=(nt,),
        in_specs=[pl.BlockSpec((tm, D_MODEL), lambda i: (i, 0)),
                  pl.BlockSpec((1, D_MODEL), lambda i: (0, 0)),
                  pl.BlockSpec((N_INT, D_MODEL), lambda i: (0, 0))] + [HBM] * n,
        out_specs=[pl.BlockSpec((tm, D_MODEL), lambda i: (i, 0)),
                   pl.BlockSpec((tm, N_HB), lambda i: (i, 0)),
                   pl.BlockSpec((tm, N_HF), lambda i: (i, 0))] + [HBM] * n,
        out_shape=[jax.ShapeDtypeStruct((T, D_MODEL), BF16),
                   jax.ShapeDtypeStruct((T, N_HB), BF16),
                   jax.ShapeDtypeStruct((T, N_HF), F32)]
                  + [jax.ShapeDtypeStruct((4,) + s.shape, s.dtype) for s in fetch],
        scratch_shapes=_gather_sems(n) if n else [],
        compiler_params=_cparams(("arbitrary",) if n else ("parallel",)),
    )(x, g, wt, *fetch)


def _inproj_bwd_dx(x, g, wt, dx_next, pieces):
    T = x.shape[0]
    tm = 512
    widths = [p.shape[1] for p in pieces]
    assert sum(widths) == N_INT

    def body(x_ref, g_ref, w_ref, dxn_ref, *rest):
        p_refs = rest[:len(pieces)]
        dx_ref, dh_ref, dg_ref = rest[len(pieces):]
        dh = jnp.concatenate([p[...].astype(BF16) for p in p_refs], axis=1)
        dh_ref[...] = dh
        dxn = _dot(dh, w_ref[...])
        xv = x_ref[...]
        r = _rms(xv)
        dx, dgrow = _rms_bwd(dxn, xv * r, r, g_ref[...])
        dx_ref[...] = dx + dxn_ref[...]

        @pl.when(pl.program_id(0) == 0)
        def _():
            dg_ref[...] = jnp.zeros_like(dg_ref)

        dg_ref[...] += _colsum(dgrow)

    return pl.pallas_call(
        body, name="inproj_bwd_dx", grid=(T // tm,),
        in_specs=[pl.BlockSpec((tm, D_MODEL), lambda i: (i, 0)),
                  pl.BlockSpec((1, D_MODEL), lambda i: (0, 0)),
                  pl.BlockSpec((N_INT, D_MODEL), lambda i: (0, 0)),
                  pl.BlockSpec((tm, D_MODEL), lambda i: (i, 0))]
                 + [pl.BlockSpec((tm, wd), lambda i: (i, 0)) for wd in widths],
        out_specs=[pl.BlockSpec((tm, D_MODEL), lambda i: (i, 0)),
                   pl.BlockSpec((tm, N_INT), lambda i: (i, 0)),
                   pl.BlockSpec((1, D_MODEL), lambda i: (0, 0))],
        out_shape=[jax.ShapeDtypeStruct((T, D_MODEL), F32),
                   jax.ShapeDtypeStruct((T, N_INT), BF16),
                   jax.ShapeDtypeStruct((1, D_MODEL), F32)],
        compiler_params=_cparams(("arbitrary",)),
    )(x, g, wt, dx_next, *pieces)


def _inproj_bwd_dw(xn, dh):
    T = xn.shape[0]
    tm = min(512, T)

    def body(a_ref, b_ref, o_ref):
        @pl.when(pl.program_id(0) == 0)
        def _():
            o_ref[...] = jnp.zeros_like(o_ref)

        for n, wd in _REAL:
            o, oi = _REAL_OFF[n][0], _INT_OFF[n]
            o_ref[o:o + wd, :] += _dot_tn(b_ref[:, oi:oi + _INT_W[n]], a_ref[...])[:wd]

    return pl.pallas_call(
        body, name="inproj_bwd_dw", grid=(T // tm,),
        in_specs=[pl.BlockSpec((tm, D_MODEL), lambda t: (t, 0)),
                  pl.BlockSpec((tm, N_INT), lambda t: (t, 0))],
        out_specs=pl.BlockSpec((D_IN, D_MODEL), lambda t: (0, 0)),
        out_shape=jax.ShapeDtypeStruct((D_IN, D_MODEL), F32),
        compiler_params=_cparams(("arbitrary",)),
    )(xn, dh)


def _roll_f32(x, shift):
    return pltpu.roll(x.astype(F32), shift, 1)


def _swa_operands(h, q, k_prev, k_cur, v_prev, v_cur):
    p, e = h // 2, h % 2
    lane = lax.broadcasted_iota(jnp.int32, (1, LANES), 1) // HEAD
    q = q[:, p * LANES:(p + 1) * LANES]
    if e != p:
        q = _roll_f32(q, HEAD).astype(BF16)
        v_prev = _roll_f32(v_prev, HEAD).astype(BF16)
        v_cur = _roll_f32(v_cur, HEAD).astype(BF16)
    qs = jnp.where(lane == p, q, 0) * 0.125
    return dict(p=p, e=e, lane=lane, qs=qs, k_prev=k_prev, k_cur=k_cur,
                v_prev=jnp.where(lane == e, v_prev, 0), v_cur=jnp.where(lane == e, v_cur, 0),
                s_prev=_dot_nt(qs, k_prev), s_cur=_dot_nt(qs, k_cur))


def _swa_probs(ops, sink, no_prev):
    row = lax.broadcasted_iota(jnp.int32, (BLOCK, BLOCK), 0)
    col = lax.broadcasted_iota(jnp.int32, (BLOCK, BLOCK), 1)
    ok_prev = col > row if no_prev is None else jnp.logical_and(col > row, jnp.logical_not(no_prev))
    s_prev = jnp.where(ok_prev, ops["s_prev"], NEG)
    s_cur = jnp.where(col <= row, ops["s_cur"], NEG)
    m = jnp.maximum(jnp.maximum(jnp.max(s_prev, axis=1, keepdims=True),
                                jnp.max(s_cur, axis=1, keepdims=True)), sink)
    p_prev = jnp.exp(s_prev - m)
    p_cur = jnp.exp(s_cur - m)
    p_sink = jnp.exp(sink - m)
    inv = 1.0 / (jnp.sum(p_prev, axis=1, keepdims=True) + jnp.sum(p_cur, axis=1, keepdims=True) + p_sink)
    return p_prev * inv, p_cur * inv, p_sink * inv


def _swa_specs(T):
    n = T // (BLOCK * SWA_SUB)
    qo, ko, vo = (_INT_OFF[name] // LANES for name in ("a_q", "a_k", "a_v"))
    halo = lambda i: jnp.maximum(i * SWA_SUB - 1, 0)
    return [pl.BlockSpec((BLOCK * SWA_SUB, 256), lambda i: (i, qo // 2)),
            pl.BlockSpec((BLOCK, LANES), lambda i: (halo(i), ko)),
            pl.BlockSpec((BLOCK * SWA_SUB, LANES), lambda i: (i, ko)),
            pl.BlockSpec((BLOCK, LANES), lambda i: (halo(i), vo)),
            pl.BlockSpec((BLOCK * SWA_SUB, LANES), lambda i: (i, vo)),
            pl.BlockSpec(memory_space=pltpu.SMEM)], n


def _swa_units(q_ref, kh_ref, kc_ref, vh_ref, vc_ref, s_ref):
    blk = lambda a: slice(a * BLOCK, (a + 1) * BLOCK)
    units = [(a, h) for a in range(SWA_SUB) for h in range(4)]
    ops = {}
    for a, h in units:
        k_prev, v_prev = (kh_ref[...], vh_ref[...]) if a == 0 else (kc_ref[blk(a - 1), :], vc_ref[blk(a - 1), :])
        ops[a, h] = _swa_operands(h, q_ref[blk(a), :], k_prev, kc_ref[blk(a), :], v_prev, vc_ref[blk(a), :])
    probs = {(a, h): _swa_probs(ops[a, h], s_ref[h], pl.program_id(0) == 0 if a == 0 else None) for a, h in units}
    return units, ops, probs, blk


def _swa_fwd(hb, sinks):
    T = hb.shape[0]
    specs, n = _swa_specs(T)

    def body(q_ref, kh_ref, kc_ref, vh_ref, vc_ref, s_ref, o_ref):
        units, ops, probs, blk = _swa_units(q_ref, kh_ref, kc_ref, vh_ref, vc_ref, s_ref)
        outs = {u: _dot(probs[u][0].astype(BF16), ops[u]["v_prev"]) + _dot(probs[u][1].astype(BF16), ops[u]["v_cur"])
                for u in units}
        for a in range(SWA_SUB):
            for p in range(2):
                o_ref[blk(a), p * LANES:(p + 1) * LANES] = outs[a, 2 * p] + outs[a, 2 * p + 1]

    return pl.pallas_call(
        body, name="swa_fwd", grid=(n,), in_specs=specs,
        out_specs=pl.BlockSpec((BLOCK * SWA_SUB, 256), lambda i: (i, 0)),
        out_shape=jax.ShapeDtypeStruct((T, 256), F32),
        compiler_params=_cparams(("parallel",)),
    )(hb, hb, hb, hb, hb, sinks)


def _swa_bwd(hb, sinks, dy):
    T = hb.shape[0]
    specs, n = _swa_specs(T)

    def body(q_ref, kh_ref, kc_ref, vh_ref, vc_ref, s_ref, dy_ref, dq_ref, dk_ref, dv_ref, ds_ref):
        i = pl.program_id(0)

        @pl.when(i == 0)
        def _():
            ds_ref[...] = jnp.zeros_like(ds_ref)

        lane_id = lax.broadcasted_iota(jnp.int32, (8, LANES), 1)
        units, ops, probs, blk = _swa_units(q_ref, kh_ref, kc_ref, vh_ref, vc_ref, s_ref)
        dos = {(a, h): jnp.where(ops[a, h]["lane"] == ops[a, h]["e"],
                                 dy_ref[blk(a), ops[a, h]["p"] * LANES:(ops[a, h]["p"] + 1) * LANES], 0.0)
               for a, h in units}
        dobs = {u: dos[u].astype(BF16) for u in units}
        pbs = {u: (probs[u][0].astype(BF16), probs[u][1].astype(BF16)) for u in units}
        outs = {u: _dot(pbs[u][0], ops[u]["v_prev"]) + _dot(pbs[u][1], ops[u]["v_cur"]) for u in units}
        dps = {u: (_dot_nt(dobs[u], ops[u]["v_prev"]), _dot_nt(dobs[u], ops[u]["v_cur"])) for u in units}
        dss, dsinks = {}, jnp.zeros((8, LANES), F32)
        for u in units:
            delta = jnp.sum(dos[u] * outs[u], axis=1, keepdims=True)
            dss[u] = ((probs[u][0] * (dps[u][0] - delta)).astype(BF16),
                      (probs[u][1] * (dps[u][1] - delta)).astype(BF16))
            dsink = -jnp.sum(probs[u][2] * delta, axis=0, keepdims=True)
            dsinks += jnp.where(lane_id == u[1], dsink, 0.0)
        ds_ref[...] += dsinks
        dqs = {u: (_dot(dss[u][0], ops[u]["k_prev"]) + _dot(dss[u][1], ops[u]["k_cur"])) * 0.125 for u in units}
        zero = jnp.zeros((BLOCK, LANES), F32)
        dk_as_prev, dk_as_cur = [zero] * SWA_SUB, [zero] * SWA_SUB
        dv_as_prev, dv_as_cur = [zero] * SWA_SUB, [zero] * SWA_SUB
        for a, h in units:
            p, e = ops[a, h]["p"], ops[a, h]["e"]
            dob_v = dobs[a, h] if e == p else pltpu.roll(dos[a, h], HEAD, 1).astype(BF16)
            dk_as_prev[a] = dk_as_prev[a] + _dot_tn(dss[a, h][0], ops[a, h]["qs"])
            dk_as_cur[a] = dk_as_cur[a] + _dot_tn(dss[a, h][1], ops[a, h]["qs"])
            dv_as_prev[a] = dv_as_prev[a] + _dot_tn(pbs[a, h][0], dob_v)
            dv_as_cur[a] = dv_as_cur[a] + _dot_tn(pbs[a, h][1], dob_v)
        base = i * SWA_SUB
        for a in range(SWA_SUB):
            rows = pl.ds(pl.multiple_of((base + a) * BLOCK, BLOCK), BLOCK)
            more = a + 1 < SWA_SUB
            dk_ref[rows, :] = dk_as_cur[a] + (dk_as_prev[a + 1] if more else 0.0)
            dv_ref[rows, :] = dv_as_cur[a] + (dv_as_prev[a + 1] if more else 0.0)
        halo = pl.ds(pl.multiple_of(jnp.maximum(base - 1, 0) * BLOCK, BLOCK), BLOCK)
        dk_ref[halo, :] += dk_as_prev[0]
        dv_ref[halo, :] += dv_as_prev[0]
        for a in range(SWA_SUB):
            for p in range(2):
                dq_pair = jnp.zeros((BLOCK, LANES), F32)
                for e in range(2):
                    dq = jnp.where(ops[a, 2 * p + e]["lane"] == p, dqs[a, 2 * p + e], 0.0)
                    dq_pair += dq if e == p else pltpu.roll(dq, HEAD, 1)
                dq_ref[blk(a), p * LANES:(p + 1) * LANES] = dq_pair.astype(BF16)

    return pl.pallas_call(
        body, name="swa_bwd", grid=(n,),
        in_specs=specs + [pl.BlockSpec((BLOCK * SWA_SUB, 256), lambda i: (i, 0))],
        out_specs=[pl.BlockSpec((BLOCK * SWA_SUB, 256), lambda i: (i, 0)),
                   pl.BlockSpec((T, LANES), lambda i: (0, 0)),
                   pl.BlockSpec((T, LANES), lambda i: (0, 0)),
                   pl.BlockSpec((8, LANES), lambda i: (0, 0))],
        out_shape=[jax.ShapeDtypeStruct((T, 256), BF16),
                   jax.ShapeDtypeStruct((T, LANES), F32),
                   jax.ShapeDtypeStruct((T, LANES), F32),
                   jax.ShapeDtypeStruct((8, LANES), F32)],
        compiler_params=_cparams(("arbitrary",)),
    )(hb, hb, hb, hb, hb, sinks, dy)


def _rope_tables(pos):
    T = pos.shape[0]
    tm = 512

    def body(pos_ref, o_ref):
        lane = lax.broadcasted_iota(jnp.int32, (1, LANES), 1)
        active = jnp.logical_and(lane >= HEAD, lane < HEAD + 2 * ROPE_HALF)
        idx = ((lane - HEAD) % ROPE_HALF).astype(F32)
        freq = jnp.exp(idx * (-math.log(ROPE_THETA) / ROPE_HALF))
        ang = pos_ref[...].astype(F32) * freq
        cos, sin = jnp.cos(ang), jnp.sin(ang)
        o_ref[:, 0:LANES] = jnp.where(active, cos, 1.0)
        o_ref[:, LANES:2 * LANES] = jnp.where(jnp.logical_and(active, lane >= HEAD + ROPE_HALF), sin, 0.0)
        o_ref[:, 2 * LANES:] = jnp.where(jnp.logical_and(active, lane < HEAD + ROPE_HALF), -sin, 0.0)

    return pl.pallas_call(
        body, name="rope_tables", grid=(T // tm,),
        in_specs=[pl.BlockSpec((tm, 1), lambda i: (i, 0))],
        out_specs=pl.BlockSpec((tm, 3 * LANES), lambda i: (i, 0)),
        out_shape=jax.ShapeDtypeStruct((T, 3 * LANES), F32),
        compiler_params=_cparams(("parallel",)),
    )(pos)


def _rope_factors(tab_ref):
    return tab_ref[:, 0:LANES], tab_ref[:, LANES:2 * LANES], tab_ref[:, 2 * LANES:]


def _rope(x, tabs):
    c, s_up, s_dn = tabs
    return x * c + pltpu.roll(x, ROPE_HALF, 1) * s_up + pltpu.roll(x, LANES - ROPE_HALF, 1) * s_dn


def _rope_t(dy, tabs):
    c, s_up, s_dn = tabs
    return dy * c + pltpu.roll(dy * s_up, LANES - ROPE_HALF, 1) + pltpu.roll(dy * s_dn, ROPE_HALF, 1)


def _mla_lat_specs(tm):
    cq, ckv, ckr = ((_INT_OFF[n] - N_HB) for n in ("c_q", "c_kv", "c_kr"))
    return [pl.BlockSpec((tm, 256), lambda i: (i, cq // 256)),
            pl.BlockSpec((tm, LANES), lambda i: (i, ckv // LANES)),
            pl.BlockSpec((tm, LANES), lambda i: (i, ckr // LANES)),
            pl.BlockSpec((tm, 3 * LANES), lambda i: (i, 0)),
            pl.BlockSpec((1, 256), lambda i: (0, 0)),
            pl.BlockSpec((1, LANES), lambda i: (0, 0)),
            pl.BlockSpec((256, 512), lambda i: (0, 0)),
            pl.BlockSpec((LANES, 768), lambda i: (0, 0))]


def _mla_prep_fwd(hf, rope, g_q, g_kv, w_uq, w_ukv):
    T = hf.shape[0]
    tm = 512
    sub = tm // ATT_BLK

    def body(cq_ref, ckv_ref, ckr_ref, tab_ref, gq_ref, gkv_ref, wq_ref, wkv_ref, qm_ref, km_ref, vm_ref, vt_ref):
        tabs = _rope_factors(tab_ref)
        cq = cq_ref[...]
        q = _dot((cq * _rms(cq) * gq_ref[...]).astype(BF16), wq_ref[...])
        ckv = ckv_ref[...]
        kv = _dot((ckv * _rms(ckv) * gkv_ref[...]).astype(BF16), wkv_ref[...])
        kr = _rope(pltpu.roll(ckr_ref[...], HEAD, 1), tabs)
        for h in range(4):
            sl = slice(h * LANES, (h + 1) * LANES)
            qm_ref[:, sl] = (_rope(q[:, sl], tabs) * MLA_QSCALE).astype(BF16)
            km_ref[:, sl] = (kv[:, sl] + kr).astype(BF16)
        vm_ref[...] = kv[:, 512:].astype(BF16)
        for p in range(2):
            for s in range(sub):
                tile = kv[s * ATT_BLK:(s + 1) * ATT_BLK, 512 + p * LANES:512 + (p + 1) * LANES]
                vt_ref[p, s] = jnp.transpose(tile).astype(BF16)

    return pl.pallas_call(
        body, name="mla_prep_fwd", grid=(T // tm,), in_specs=_mla_lat_specs(tm),
        out_specs=[pl.BlockSpec((tm, 512), lambda i: (i, 0)),
                   pl.BlockSpec((tm, 512), lambda i: (i, 0)),
                   pl.BlockSpec((tm, 256), lambda i: (i, 0)),
                   pl.BlockSpec((2, sub, LANES, ATT_BLK), lambda i: (0, i, 0, 0))],
        out_shape=[jax.ShapeDtypeStruct((T, 512), BF16),
                   jax.ShapeDtypeStruct((T, 512), BF16),
                   jax.ShapeDtypeStruct((T, 256), BF16),
                   jax.ShapeDtypeStruct((2, T // ATT_BLK, LANES, ATT_BLK), BF16)],
        compiler_params=_cparams(("parallel",)),
    )(hf, hf, hf, rope, g_q, g_kv, w_uq, w_ukv)


def _mla_prep_bwd(hf, rope, g_q, g_kv, w_uq, w_ukv, dqm, dkt, dvt):
    T = hf.shape[0]
    tm = 512
    sub = tm // ATT_BLK

    def body(cq_ref, ckv_ref, ckr_ref, tab_ref, gq_ref, gkv_ref, wq_ref, wkv_ref, dq_ref, dk_ref, dv_ref,
             dc_ref, dwq_ref, dwkv_ref, dgq_ref, dgkv_ref):
        @pl.when(pl.program_id(0) == 0)
        def _():
            dwq_ref[...] = jnp.zeros_like(dwq_ref)
            dwkv_ref[...] = jnp.zeros_like(dwkv_ref)
            dgq_ref[...] = jnp.zeros_like(dgq_ref)
            dgkv_ref[...] = jnp.zeros_like(dgkv_ref)

        tabs = _rope_factors(tab_ref)
        lane =lax.broadcasted_iota(jnp.int32, (1, LANES), 1)
        dq = jnp.concatenate([_rope_t(dq_ref[:, h * LANES:(h + 1) * LANES] * MLA_QSCALE, tabs)
                              for h in range(4)], axis=1).astype(BF16)
        cq = cq_ref[...]
        rq = _rms(cq)
        cqn = (cq * rq * gq_ref[...]).astype(BF16)
        dwq_ref[...] += _dot_tn(cqn, dq)
        dcq, dgrow = _rms_bwd(_dot_nt(dq, wq_ref[...]), cq * rq, rq, gq_ref[...])
        dgq_ref[...] += _colsum(dgrow)
        dc_ref[:, 0:256] = dcq.astype(BF16)

        dk = jnp.concatenate([jnp.concatenate([jnp.transpose(dk_ref[p, s]) for p in range(2)], axis=1)
                              for s in range(sub)], axis=0) * LN2
        dv = jnp.concatenate([jnp.concatenate([jnp.transpose(dv_ref[p, s]) for p in range(2)], axis=1)
                              for s in range(sub)], axis=0)
        dkr = dk[:, 0:LANES] + dk[:, LANES:2 * LANES] + dk[:, 2 * LANES:3 * LANES] + dk[:, 3 * LANES:]
        dkr = pltpu.roll(_rope_t(dkr, tabs), HEAD, 1)
        dc_ref[:, 384:512] = jnp.where(lane < 2 * ROPE_HALF, dkr, 0.0).astype(BF16)
        dkv = jnp.concatenate([dk.astype(BF16), dv.astype(BF16)], axis=1)
        ckv = ckv_ref[...]
        rkv = _rms(ckv)
        ckvn = (ckv * rkv * gkv_ref[...]).astype(BF16)
        dwkv_ref[...] += _dot_tn(ckvn, dkv)
        dckv, dgrow = _rms_bwd(_dot_nt(dkv, wkv_ref[...]), ckv * rkv, rkv, gkv_ref[...])
        dgkv_ref[...] += _colsum(dgrow)
        dc_ref[:, 256:384] = dckv.astype(BF16)

    return pl.pallas_call(
        body, name="mla_prep_bwd", grid=(T // tm,),
        in_specs=_mla_lat_specs(tm) + [pl.BlockSpec((tm, 512), lambda i: (i, 0)),
                                       pl.BlockSpec((2, sub, 256, ATT_BLK), lambda i: (0, i, 0, 0)),
                                       pl.BlockSpec((2, sub, LANES, ATT_BLK), lambda i: (0, i, 0, 0))],
        out_specs=[pl.BlockSpec((tm, 512), lambda i: (i, 0)),
                   pl.BlockSpec((256, 512), lambda i: (0, 0)),
                   pl.BlockSpec((LANES, 768), lambda i: (0, 0)),
                   pl.BlockSpec((1, 256), lambda i: (0, 0)),
                   pl.BlockSpec((1, LANES), lambda i: (0, 0))],
        out_shape=[jax.ShapeDtypeStruct((T, 512), BF16),
                   jax.ShapeDtypeStruct((256, 512), F32),
                   jax.ShapeDtypeStruct((LANES, 768), F32),
                   jax.ShapeDtypeStruct((1, 256), F32),
                   jax.ShapeDtypeStruct((1, LANES), F32)],
        compiler_params=_cparams(("arbitrary",)),
    )(hf, hf, hf, rope, g_q, g_kv, w_uq, w_ukv, dqm, dkt, dvt)


def _causal_masks(bq, bk):
    row = lax.broadcasted_iota(jnp.int32, (bq, bk), 0)
    col = lax.broadcasted_iota(jnp.int32, (bq, bk), 1)
    return row, col


def _mla_fwd(qm, km, vt):
    T = qm.shape[0]
    bq, bk = min(MLA_BQ, T), ATT_BLK
    nq, nsub, nk = T // bq, bq // bk, T // bk

    def body(q_ref, k_ref, vt_ref, o_ref, lse_ref, acc_ref, m_ref, l_ref):
        qi = pl.program_id(0)
        key = lax.broadcasted_iota(jnp.int32, (bk, bq), 0)
        qry = lax.broadcasted_iota(jnp.int32, (bk, bq), 1)
        ones = jnp.ones((8, bk), BF16)
        acc_ref[...] = jnp.zeros_like(acc_ref)
        m_ref[...] = jnp.full_like(m_ref, NEG)
        l_ref[...] = jnp.zeros_like(l_ref)

        def step(kb0, masked):
            kbs = [kb0 + d for d in range(nsub)]
            qs = [slice(d * bk if masked else 0, bq) for d in range(nsub)]

            def wide(a, d, fill):
                if not qs[d].start:
                    return a
                return jnp.concatenate([jnp.full((a.shape[0], qs[d].start), fill, a.dtype), a], axis=1)

            sts = [[_dot_nt(k_ref[pl.ds(pl.multiple_of(kb * bk, bk), bk), e * LANES:(e + 1) * LANES],
                            q_ref[qs[d], e * LANES:(e + 1) * LANES]) for d, kb in enumerate(kbs)] for e in range(4)]
            pts, alphas = [], []
            for e in range(4):
                st = ([jnp.where(key[:, qs[d]] + d * bk <= qry[:, qs[d]], sts[e][d], NEG) for d in range(nsub)]
                      if masked else sts[e])
                m_prev = m_ref[e, 0:1, :]
                m_new = m_prev
                for d in range(nsub):
                    m_new = jnp.maximum(m_new, wide(jnp.max(st[d], axis=0, keepdims=True), d, NEG))
                alpha = jnp.exp2(m_prev - m_new)
                pt = [jnp.exp2(st[d] - m_new[:, qs[d]]).astype(BF16) for d in range(nsub)]
                l_new = alpha * l_ref[e]
                for d in range(nsub):
                    l_new = l_new + wide(_dot(ones, pt[d]), d, 0.0)
                l_ref[e] = l_new
                m_ref[e] = jnp.broadcast_to(m_new, (8, bq))
                pts.append(pt)
                alphas.append(alpha)
            for e in range(4):
                acc = alphas[e] * acc_ref[e]
                for d in range(nsub):
                    v_t = vt_ref[e // 2, kbs[d], (e % 2) * HEAD:(e % 2 + 1) * HEAD, :]
                    acc = acc + wide(_dot(v_t, pts[e][d]), d, 0.0)
                acc_ref[e] = acc

        step(qi * nsub, True)

        def loop(t, c):
            step(t * nsub, False)
            return c

        lax.fori_loop(0, qi, loop, 0)
        outs, lses = [], []
        for e in range(4):
            l = l_ref[e, 0:1, :]
            outs.append(acc_ref[e] / l)
            lses.append(jnp.broadcast_to(m_ref[e, 0:1, :] * LN2 + jnp.log(l), (HEAD, bq)))
        o_ref[...] = jnp.transpose(jnp.concatenate(outs, axis=0))
        lse_ref[...] = jnp.transpose(jnp.concatenate(lses, axis=0))

    return pl.pallas_call(
        body, name="mla_fwd", grid=(nq,),
        in_specs=[pl.BlockSpec((bq, 512), lambda i: (i, 0)),
                  pl.BlockSpec((T, 512), lambda i: (0, 0)),
                  pl.BlockSpec((2, nk, LANES, bk), lambda i: (0, 0, 0, 0))],
        out_specs=[pl.BlockSpec((bq, 256), lambda i: (i, 0)),
                   pl.BlockSpec((bq, 256), lambda i: (i, 0))],
        out_shape=[jax.ShapeDtypeStruct((T, 256), F32), jax.ShapeDtypeStruct((T, 256), F32)],
        scratch_shapes=[pltpu.VMEM((4, HEAD, bq), F32), pltpu.VMEM((4, 8, bq), F32), pltpu.VMEM((4, 8, bq), F32)],
        compiler_params=_cparams(("arbitrary",)),
    )(qm, km, vt)


def _mla_bwd(qm, km, vm, y, lse, dy):
    T = qm.shape[0]
    bq, bk = min(MLA_BQ, T), ATT_BLK
    nq, nsub, nk = T // bq, bq // bk, T // bk

    def body(q_ref, k_ref, v_ref, y_ref, lse_ref, dy_ref, dq_ref, dkt_ref, dvt_ref, dob_ref, st_ref, qt_ref, dot_ref):
        qi = pl.program_id(1)

        @pl.when(qi == 0)
        def _():
            dkt_ref[...] = jnp.zeros_like(dkt_ref)
            dvt_ref[...] = jnp.zeros_like(dvt_ref)

        lane = lax.broadcasted_iota(jnp.int32, (1, LANES), 1) // HEAD
        row, col = _causal_masks(bq, bk)
        dq_ref[...] = jnp.zeros_like(dq_ref)
        lse = lse_ref[...]
        lse_other = pltpu.roll(lse, HEAD, 1)
        qt_ref[...] = jnp.transpose(q_ref[...].astype(F32)).astype(BF16)
        dot_ref[...] = jnp.transpose(dy_ref[...]).astype(BF16)
        for e in range(2):
            do = jnp.where(lane == e, dy_ref[...], 0.0)
            dob_ref[e] = do.astype(BF16)
            st_ref[2 * e] = jnp.where(lane == e, lse, lse_other) * LOG2E
            st_ref[2 * e + 1] = jnp.broadcast_to(jnp.sum(do * y_ref[...], axis=1, keepdims=True), (bq, LANES))

        hss = [slice(e * LANES, (e + 1) * LANES) for e in range(2)]
        tile = lambda a: jnp.concatenate([a] * (bk // LANES), axis=1)

        def step(kb0, masked):
            kbs = [kb0 + d for d in range(nsub)]
            rows = [pl.ds(pl.multiple_of(kb * bk, bk), bk) for kb in kbs]
            pairs = [(d, e) for d in range(nsub) for e in range(2)]
            qs = [slice(d * bk if masked else 0, bq) for d in range(nsub)]
            ss = {(d, e): _dot_nt(q_ref[qs[d], hss[e]], k_ref[rows[d], hss[e]]) for d, e in pairs}
            dps = {(d, e): _dot_nt(dob_ref[e, qs[d], :], jnp.where(lane == e, v_ref[rows[d], :], 0))
                   for d, e in pairs}
            ps, dss = {}, {}
            for d, e in pairs:
                s = jnp.where(col[qs[d]] + d * bk <= row[qs[d]], ss[d, e], NEG) if masked else ss[d, e]
                p = jnp.exp2(s - tile(st_ref[2 * e, qs[d], :]))
                dss[d, e] = (p * (dps[d, e] - tile(st_ref[2 * e + 1, qs[d], :]))).astype(BF16)
                ps[d, e] = p.astype(BF16)
            for d, e in pairs:
                dvt_ref[0, kbs[d], e * HEAD:(e + 1) * HEAD, :] += _dot(
                    dot_ref[e * HEAD:(e + 1) * HEAD, qs[d]], ps[d, e])
            for d, e in pairs:
                dkt_ref[0, kbs[d], hss[e], :] += _dot(qt_ref[hss[e], qs[d]], dss[d, e])
            for e in range(2):
                if masked:
                    for d in range(nsub):
                        dq_ref[qs[d], hss[e]] += _dot(dss[d, e], k_ref[rows[d], hss[e]])
                else:
                    dq = dq_ref[:, hss[e]]
                    for d in range(nsub):
                        dq = dq + _dot(dss[d, e], k_ref[rows[d], hss[e]])
                    dq_ref[:, hss[e]] = dq

        step(qi * nsub, True)

        def loop(t, c):
            step(t * nsub, False)
            return c

        lax.fori_loop(0, qi, loop, 0)
        dq_ref[...] *= LN2

    return pl.pallas_call(
        body, name="mla_bwd", grid=(2, nq),
        in_specs=[pl.BlockSpec((bq, 256), lambda j, i: (i, j)),
                  pl.BlockSpec((T, 256), lambda j, i: (0, j)),
                  pl.BlockSpec((T, LANES), lambda j, i: (0, j)),
                  pl.BlockSpec((bq, LANES), lambda j, i: (i, j)),
                  pl.BlockSpec((bq, LANES), lambda j, i: (i, j)),
                  pl.BlockSpec((bq, LANES), lambda j, i: (i, j))],
        out_specs=[pl.BlockSpec((bq, 256), lambda j, i: (i, j)),
                   pl.BlockSpec((1, nk, 256, bk), lambda j, i: (j, 0, 0, 0)),
                   pl.BlockSpec((1, nk, LANES, bk), lambda j, i: (j, 0, 0, 0))],
        out_shape=[jax.ShapeDtypeStruct((T, 512), F32),
                   jax.ShapeDtypeStruct((2, nk, 256, bk), F32),
                   jax.ShapeDtypeStruct((2, nk, LANES, bk), F32)],
        scratch_shapes=[pltpu.VMEM((2, bq, LANES), BF16), pltpu.VMEM((4, bq, LANES), F32),
                        pltpu.VMEM((256, bq), BF16), pltpu.VMEM((LANES, bq), BF16)],
        compiler_params=_cparams(("parallel", "arbitrary")),
    )(qm, km, vm, y, lse, dy)


def _suffix_ones(n):
    r = lax.broadcasted_iota(jnp.int32, (n, n), 0)
    c = lax.broadcasted_iota(jnp.int32, (n, n), 1)
    return (r >= c).astype(BF16)


def _prefix_ones(n):
    r = lax.broadcasted_iota(jnp.int32, (n, n), 0)
    c = lax.broadcasted_iota(jnp.int32, (n, n), 1)
    return (r <= c).astype(BF16)


def _sb_specs(T, bq):
    qo, ko, vo = (_INT_OFF[n] // 256 for n in ("d_q", "d_k", "d_v"))
    return [pl.BlockSpec((bq, 256), lambda i: (i, qo)),
            pl.BlockSpec((T, 256), lambda i: (0, ko)),
            pl.BlockSpec((T, 256), lambda i: (0, vo))]


def _sb_fwd(hb):
    T = hb.shape[0]
    bq = bk = ATT_BLK
    nq = T // bq

    def body(q_ref, k_ref, v_ref, o_ref, tot_ref, cnt_ref, qm_ref, car_ref):
        qi = pl.program_id(0)
        lane = lax.broadcasted_iota(jnp.int32, (1, LANES), 1) // HEAD
        row, col = _causal_masks(bq, bk)
        strict = col < row
        u = _suffix_ones(bk)
        o_ref[...] = jnp.zeros_like(o_ref)
        car_ref[...] = jnp.zeros_like(car_ref)
        pair = lambda h: slice((h // 2) * LANES, (h // 2 + 1) * LANES)
        for h in range(4):
            qm_ref[h] = jnp.where(lane == h % 2, q_ref[:, pair(h)], 0) * 0.125

        def step(blocks):
            tile = lambda a: jnp.concatenate([a] * (bk // LANES), axis=1)
            rows = [pl.ds(pl.multiple_of(kb * bk, bk), bk) for kb, _ in blocks]
            pairs = [(b, h) for b in range(len(blocks)) for h in range(4)]
            zs = {(b, h): _dot_nt(qm_ref[h], k_ref[rows[b], pair(h)]) for b, h in pairs}
            splits = {}
            for b, h in pairs:
                z = zs[b, h]
                lk = jnp.minimum(-z, 0.0) - jnp.log(1.0 + jnp.exp(-jnp.abs(z)))
                if blocks[b][1] is not None:
                    lk = jnp.where(blocks[b][1], lk, 0.0)
                splits[b, h] = _split(lk)
            sufs = {bh: _dot(hi, u) + _dot(lo, u) for bh, (hi, lo) in splits.items()}
            car = [car_ref[h] for h in range(4)]
            aas = {}
            for b, h in pairs:
                a = jnp.exp(zs[b, h] + sufs[b, h] + tile(car[h]))
                if blocks[b][1] is not None:
                    a = jnp.where(blocks[b][1], a, 0.0)
                aas[b, h] = a.astype(BF16)
                car[h] = car[h] + jnp.broadcast_to(sufs[b, h][:, 0:1], (bq, LANES))
            acc = [o_ref[:, pair(0)], o_ref[:, pair(2)]]
            for b, h in pairs:
                acc[h // 2] = acc[h // 2] + _dot(aas[b, h], jnp.where(lane == h % 2, v_ref[rows[b], pair(h)], 0))
            o_ref[:, pair(0)], o_ref[:, pair(2)] = acc
            for h in range(4):
                car_ref[h] = car[h]

        step([(qi, strict), (jnp.maximum(qi - 1, 0), qi > 0)])

        def live():
            worst = jnp.maximum(jnp.maximum(car_ref[0], car_ref[1]), jnp.maximum(car_ref[2], car_ref[3]))
            return jnp.max(worst) >= SB_DEAD

        def cond(c):
            return jnp.logical_and(c[0] < qi, c[1])

        def loop(c):
            step([(qi - 1 - c[0], None)])
            return c[0] + 1, live()

        done, _ = lax.while_loop(cond, loop, (jnp.minimum(qi, 1), live()))
        tot_ref[:, pair(0)] = jnp.where(lane == 0, car_ref[0], car_ref[1])
        tot_ref[:, pair(2)] = jnp.where(lane == 0, car_ref[2], car_ref[3])
        cnt_ref[0, qi] = done.astype(F32)

    return pl.pallas_call(
        body, name="sb_fwd", grid=(nq,), in_specs=_sb_specs(T, bq),
        out_specs=[pl.BlockSpec((bq, 256), lambda i: (i, 0)), pl.BlockSpec((bq, 256), lambda i: (i, 0)),
                   pl.BlockSpec(memory_space=pltpu.SMEM)],
        out_shape=[jax.ShapeDtypeStruct((T, 256), F32), jax.ShapeDtypeStruct((T, 256), F32),
                   jax.ShapeDtypeStruct((1, nq), F32)],
        scratch_shapes=[pltpu.VMEM((4, bq, LANES), BF16), pltpu.VMEM((4, bq, LANES), F32)],
        compiler_params=_cparams(("arbitrary",)),
    )(hb, hb, hb)


def _sb_bwd(hb, tot, cnt, dy):
    T = hb.shape[0]
    bq = bk = ATT_BLK
    nq = T // bq

    def body(q_ref, k_ref, v_ref, tot_ref, dy_ref, cnt_ref, dq_ref, dk_ref, dv_ref, qm_ref, dob_ref, dqa_ref, rem_ref,
             cg_ref):
        qi = pl.program_id(0)

        @pl.when(qi == 0)
        def _():
            dk_ref[...] = jnp.zeros_like(dk_ref)
            dv_ref[...] = jnp.zeros_like(dv_ref)

        lane = lax.broadcasted_iota(jnp.int32, (1, LANES), 1) // HEAD
        row, col = _causal_masks(bq, bk)
        strict = col < row
        u = _prefix_ones(bk)
        pair = lambda h: slice((h // 2) * LANES, (h // 2 + 1) * LANES)
        dqa_ref[...] = jnp.zeros_like(dqa_ref)
        cg_ref[...] = jnp.zeros_like(cg_ref)
        for h in range(4):
            tot = tot_ref[:, pair(h)]
            qm_ref[h] = jnp.where(lane == h % 2, q_ref[:, pair(h)], 0) * 0.125
            dob_ref[h] = jnp.where(lane == h % 2, dy_ref[:, pair(h)], 0.0).astype(BF16)
            rem_ref[h] = jnp.where(lane == h % 2, tot, pltpu.roll(tot, HEAD, 1))

        def step(blocks):
            tile = lambda a: jnp.concatenate([a] * (bk // LANES), axis=1)
            nb = len(blocks)
            rows = [pl.ds(pl.multiple_of(kb * bk, bk), bk) for kb, _ in blocks]
            pairs = [(b, h) for b in range(nb) for h in range(4)]
            mask = lambda b, x: x if blocks[b][1] is None else jnp.where(blocks[b][1], x, 0.0)
            zs = {(b, h): _dot_nt(qm_ref[h], k_ref[rows[b], pair(h)]) for b, h in pairs}
            das = {(b, h): _dot_nt(dob_ref[h], jnp.where(lane == h % 2, v_ref[rows[b], pair(h)], 0)) for b, h in pairs}
            zls, splits = {}, {}
            for b, h in pairs:
                z = zs[b, h]
                lk = mask(b, jnp.minimum(-z, 0.0) - jnp.log(1.0 + jnp.exp(-jnp.abs(z))))
                zls[b, h] = z + lk
                splits[b, h] = _split(lk)
            pres = {bh: _dot(hi, u) + _dot(lo, u) for bh, (hi, lo) in splits.items()}
            rem = [rem_ref[h] for h in range(4)]
            aas, gs, gsplits = {}, {}, {}
            for b, h in pairs:
                a = mask(b, jnp.exp(zls[b, h] + (tile(rem[h]) - pres[b, h])))
                gs[b, h] = a * das[b, h]
                aas[b, h] = a.astype(BF16)
                gsplits[b, h] = _split(gs[b, h])
                rem[h] = rem[h] - jnp.broadcast_to(pres[b, h][:, bk - 1:bk], (bq, LANES))
            for b in range(nb):
                for p in (0, 2):
                    dv_ref[rows[b], pair(p)] += _dot_tn(aas[b, p], dob_ref[p]) + _dot_tn(aas[b, p + 1], dob_ref[p + 1])
            gpres = {bh: _dot(hi, u) + _dot(lo, u) for bh, (hi, lo) in gsplits.items()}
            cg = [cg_ref[h] for h in range(4)]
            dzs = {}
            for b, h in pairs:
                dz = mask(b, gs[b, h] - jnp.exp(zls[b, h]) * (tile(cg[h]) + gpres[b, h]))
                dzs[b, h] = dz.astype(BF16)
                cg[h] = cg[h] + jnp.broadcast_to(gpres[b, h][:, bk - 1:bk], (bq, LANES))
            for b in range(nb):
                for p in (0, 2):
                    dk_ref[rows[b], pair(p)] += _dot_tn(dzs[b, p], qm_ref[p]) + _dot_tn(dzs[b, p + 1], qm_ref[p + 1])
            for h in range(4):
                dq = dqa_ref[h]
                for b in range(nb):
                    dq = dq + _dot(dzs[b, h], k_ref[rows[b], pair(h)])
                dqa_ref[h] = dq
                rem_ref[h] = rem[h]
                cg_ref[h] = cg[h]

        def loop(kb, c):
            step([(kb, None)])
            return c

        start = qi - jnp.clip(cnt_ref[0, qi].astype(jnp.int32), 0, qi)
        lax.fori_loop(start, qi - 1, loop, 0)
        step([(jnp.maximum(qi - 1, 0), qi > 0), (qi, strict)])
        for p in (0, 2):
            dq_ref[:, pair(p)] = (jnp.where(lane == 0, dqa_ref[p], dqa_ref[p + 1]) * 0.125).astype(BF16)

    return pl.pallas_call(
        body, name="sb_bwd", grid=(nq,),
        in_specs=_sb_specs(T, bq) + [pl.BlockSpec((bq, 256), lambda i: (i, 0)),
                                     pl.BlockSpec((bq, 256), lambda i: (i, 0)),
                                     pl.BlockSpec(memory_space=pltpu.SMEM)],
        out_specs=[pl.BlockSpec((bq, 256), lambda i: (i, 0)),
                   pl.BlockSpec((T, 256), lambda i: (0, 0)),
                   pl.BlockSpec((T, 256), lambda i: (0, 0))],
        out_shape=[jax.ShapeDtypeStruct((T, 256), BF16)] + [jax.ShapeDtypeStruct((T, 256), F32)] * 2,
        scratch_shapes=[pltpu.VMEM((4, bq, LANES), BF16), pltpu.VMEM((4, bq, LANES), BF16),
                        pltpu.VMEM((4, bq, LANES), F32), pltpu.VMEM((4, bq, LANES), F32),
                        pltpu.VMEM((4, bq, LANES), F32)],
        compiler_params=_cparams(("arbitrary",)),
    )(hb, hb, hb, tot, dy, cnt)


EP_TM = 512


def _ep_in_specs(tm, rev):
    idx = (lambda i: rev - i) if rev is not None else (lambda i: i)
    bo = (_INT_OFF["b_b"] - N_HB) // 256
    halo = lambda i: jnp.maximum(idx(i) * (tm // 8) - 1, 0)
    return [pl.BlockSpec((tm, 256), lambda i: (idx(i), 0)),
            pl.BlockSpec((tm, 256), lambda i: (idx(i), 0)),
            pl.BlockSpec((tm, 256), lambda i: (idx(i), 0)),
            pl.BlockSpec((tm, D_MODEL), lambda i: (idx(i), 0)),
            pl.BlockSpec((tm, 256), lambda i: (idx(i), bo)),
            pl.BlockSpec((tm, 256), lambda i: (idx(i), bo + 1)),
            pl.BlockSpec((tm, 256), lambda i: (idx(i), bo + 2)),
            pl.BlockSpec((8, 256), lambda i: (halo(i), bo + 1)),
            pl.BlockSpec((8, 256), lambda i: (halo(i), bo + 2)),
            pl.BlockSpec((3, 256), lambda i: (0, 0)),
            pl.BlockSpec((1, 256), lambda i: (0, 0)),
            pl.BlockSpec((1, D_MODEL), lambda i: (0, 0)),
            pl.BlockSpec((D_MODEL, D_MODEL), lambda i: (0, 0)),
            pl.BlockSpec((1, D_MODEL), lambda i: (0, 0))]


def _ep_mix(first, ya_ref, yc_ref, yd_ref, gate_ref, bb_ref, bc_ref, bx_ref, hc_ref, hx_ref, cw_ref, cb_ref, gg_ref):
    tm = ya_ref.shape[0]
    u = bc_ref[...] * bx_ref[...]
    halo = jnp.where(first, 0.0, hc_ref[...] * hx_ref[...])
    row = lax.broadcasted_iota(jnp.int32, (tm, 1), 0)
    u1 = jnp.where(row == 0, halo[7:8, :], pltpu.roll(u, 1, 0))
    u2 = jnp.where(row == 0, halo[6:7, :], jnp.where(row == 1, halo[7:8, :], pltpu.roll(u, 2, 0)))
    cw = cw_ref[...]
    conv = cw[0:1, :] * u2 + cw[1:2, :] * u1 + cw[2:3, :] * u + cb_ref[...]
    bb = bb_ref[...]
    ys = [ya_ref[...], bb * conv, yc_ref[...], yd_ref[...]]
    rs = [_rms(y) for y in ys]
    gg = gg_ref[...]
    yhat = jnp.concatenate([y * r for y, r in zip(ys, rs)], axis=1)
    gate = gate_ref[...]
    sig = 1.0 / (1.0 + jnp.exp(-gate))
    return u, u1, u2, conv, bb, rs, yhat, yhat * gg, gate, sig


def _epilogue_fwd(x, ya, yc, yd, hf, conv_w, conv_b, g_grp, w_out, g_post, tgt=None):
    T = x.shape[0]
    tm = EP_TM
    row_spec = pl.BlockSpec((tm, D_MODEL), lambda i: (i, 0))

    def layer_out(refs):
        (x_ref, ya_ref, yc_ref, yd_ref, gate_ref, bb_ref, bc_ref, bx_ref, hc_ref, hx_ref, cw_ref, cb_ref,
         gg_ref, wo_ref, gp_ref) = refs
        (_, _, _, _, _, _, _, yn, gate, sig) = _ep_mix(
            pl.program_id(0) == 0, ya_ref, yc_ref, yd_ref, gate_ref, bb_ref, bc_ref, bx_ref, hc_ref, hx_ref,
            cw_ref, cb_ref, gg_ref)
        z = _dot((yn * (gate * sig)).astype(BF16), wo_ref[...])
        return x_ref[...] + z * _rms(z) * gp_ref[...]

    args = (x, ya, yc, yd, hf, hf, hf, hf, hf, hf, conv_w, conv_b, g_grp, w_out, g_post)
    in_specs = [row_spec] + _ep_in_specs(tm, None)
    if tgt is None:
        def body(*refs):
            refs[-1][...] = layer_out(refs[:-1])

        return pl.pallas_call(
            body, name="epilogue_fwd", grid=(T // tm,), in_specs=in_specs, out_specs=row_spec,
            out_shape=jax.ShapeDtypeStruct((T, D_MODEL), F32), compiler_params=_cparams(("parallel",)),
        )(*args)

    def body_loss(*refs):
        t_ref, dy_ref, l_ref = refs[-3:]

        @pl.when(pl.program_id(0) == 0)
        def _():
            l_ref[...] = jnp.zeros_like(l_ref)

        d = layer_out(refs[:-3]) - t_ref[...]
        dy_ref[...] = d * (1.0 / D_MODEL)
        part = jnp.sum(jnp.sum(d * d, axis=1, keepdims=True), axis=0, keepdims=True)
        l_ref[...] += part * (0.5 / D_MODEL)

    return pl.pallas_call(
        body_loss, name="epilogue_fwd_loss", grid=(T // tm,), in_specs=in_specs + [row_spec],
        out_specs=[row_spec, pl.BlockSpec((8, LANES), lambda i: (0, 0))],
        out_shape=[jax.ShapeDtypeStruct((T, D_MODEL), F32), jax.ShapeDtypeStruct((8, LANES), F32)],
        compiler_params=_cparams(("arbitrary",)),
    )(*args, tgt)


def _epilogue_bwd(dxn, ya, yc, yd, hf, conv_w, conv_b, g_grp, w_out, g_post):
    T = dxn.shape[0]
    tm = EP_TM
    nt = T // tm
    ridx = lambda i: (nt - 1 - i, 0)

    def body(dx_ref, ya_ref, yc_ref, yd_ref, gate_ref, bb_ref, bc_ref, bx_ref, hc_ref, hx_ref, cw_ref, cb_ref,
             gg_ref, wo_ref, gp_ref,
             dya_ref, dyc_ref, dyd_ref, dhf_ref, dwo_ref, dgp_ref, dgg_ref, dcw_ref, dcb_ref, carry_ref):
        i = pl.program_id(0)

        @pl.when(i == 0)
        def _():
            for r in (dwo_ref, dgp_ref, dgg_ref, dcw_ref, dcb_ref, carry_ref):
                r[...] = jnp.zeros_like(r)

        (u, u1, u2, conv, bb, rs, yhat, yn, gate, sig) = _ep_mix(
            i == nt - 1, ya_ref, yc_ref, yd_ref, gate_ref, bb_ref, bc_ref, bx_ref, hc_ref, hx_ref,
            cw_ref, cb_ref, gg_ref)
        silu = gate * sig
        ymix = (yn * silu).astype(BF16)
        z = _dot(ymix, wo_ref[...])
        rz = _rms(z)
        dz, dgrow = _rms_bwd(dx_ref[...], z * rz, rz, gp_ref[...])
        dgp_ref[...] += _colsum(dgrow)
        dzb = dz.astype(BF16)
        dwo_ref[...] += _dot_tn(ymix, dzb)
        dymix = _dot_nt(dzb, wo_ref[...])
        dhf_ref[:, 0:D_MODEL] = (dymix * yn * (sig * (1.0 + gate * (1.0 - sig)))).astype(BF16)
        dyn = dymix * silu
        dgg_ref[...] += _colsum(dyn * yhat)
        gg = gg_ref[...]
        dys = []
        for gi in range(4):
            sl = slice(gi * GROUP, (gi + 1) * GROUP)
            dyh = dyn[:, sl] * gg[:, sl]
            yh = yhat[:, sl]
            dys.append(rs[gi] * (dyh - yh * jnp.mean(dyh * yh, axis=-1, keepdims=True)))
        dya_ref[...] = dys[0]
        dyc_ref[...] = dys[2]
        dyd_ref[...] = dys[3]
        dyb = dys[1]
        dhf_ref[:, D_MODEL:D_MODEL + 256] = (dyb * conv).astype(BF16)
        dconv = dyb * bb
        dcb_ref[...] += _colsum(dconv)
        dcw_ref[0:1, :] += _colsum(dconv * u2)
        dcw_ref[1:2, :] += _colsum(dconv * u1)
        dcw_ref[2:3, :] += _colsum(dconv * u)
        carry = carry_ref[...]
        row = lax.broadcasted_iota(jnp.int32, (tm, 1), 0)
        d1 = jnp.where(row == tm - 1, carry[0:1, :], pltpu.roll(dconv, tm - 1, 0))
        d2 = jnp.where(row == tm - 2, carry[0:1, :],
                       jnp.where(row == tm - 1, carry[1:2, :], pltpu.roll(dconv, tm - 2, 0)))
        cw = cw_ref[...]
        du = cw[2:3, :] * dconv + cw[1:2, :] * d1 + cw[0:1, :] * d2
        dhf_ref[:, D_MODEL + 256:D_MODEL + 512] = (du * bx_ref[...]).astype(BF16)
        dhf_ref[:, D_MODEL + 512:D_MODEL + 768] = (du * bc_ref[...]).astype(BF16)
        carry_ref[...] = dconv[0:8, :]

    in_specs = [pl.BlockSpec((tm, D_MODEL), ridx)] + _ep_in_specs(tm, nt - 1)
    return pl.pallas_call(
        body, name="epilogue_bwd", grid=(nt,), in_specs=in_specs,
        out_specs=[pl.BlockSpec((tm, 256), ridx), pl.BlockSpec((tm, 256), ridx), pl.BlockSpec((tm, 256), ridx),
                   pl.BlockSpec((tm, D_MODEL + 768), ridx),
                   pl.BlockSpec((D_MODEL, D_MODEL), lambda i: (0, 0)),
                   pl.BlockSpec((1, D_MODEL), lambda i: (0, 0)),
                   pl.BlockSpec((1, D_MODEL), lambda i: (0, 0)),
                   pl.BlockSpec((8, 256), lambda i: (0, 0)),
                   pl.BlockSpec((1, 256), lambda i: (0, 0))],
        out_shape=[jax.ShapeDtypeStruct((T, 256), F32)] * 3
                  + [jax.ShapeDtypeStruct((T, D_MODEL + 768), BF16),
                     jax.ShapeDtypeStruct((D_MODEL, D_MODEL), F32),
                     jax.ShapeDtypeStruct((1, D_MODEL), F32),
                     jax.ShapeDtypeStruct((1, D_MODEL), F32),
                     jax.ShapeDtypeStruct((8, 256), F32),
                     jax.ShapeDtypeStruct((1, 256), F32)],
        scratch_shapes=[pltpu.VMEM((8, 256), F32)],
        compiler_params=_cparams(("arbitrary",)),
    )(dxn, ya, yc, yd, hf, hf, hf, hf, hf, hf, conv_w, conv_b, g_grp, w_out, g_post)


def _place():
    return lax.axis_index("x"), lax.axis_index("y"), lax.axis_index("c")


def _other_chips(x, y):
    return [(1 - x, y), (x, 1 - y), (1 - x, 1 - y)]


HBM = pl.BlockSpec(memory_space=pl.ANY)


def _gather_plan(ins, outs, sems):
    n = len(ins)
    ici_send, ici_recv, d2d_send, d2d_recv, local_sems = sems
    x, y, c = _place()
    me = 2 * x + y
    chips = _other_chips(x, y)

    def ici(a, j, chip_from):
        px, py = chips[j]
        return pltpu.make_async_remote_copy(
            src_ref=ins[a].at[c], dst_ref=outs[a].at[chip_from, c], send_sem=ici_send.at[3 * a + j],
            recv_sem=ici_recv.at[3 * a + j], device_id=(px, py, c), device_id_type=MESH)

    def d2d(a, j, part):
        px, py = chips[j]
        blk = outs[a].at[2 * px + py, part]
        return pltpu.make_async_remote_copy(
            src_ref=blk, dst_ref=blk, send_sem=d2d_send.at[3 * a + j], recv_sem=d2d_recv.at[3 * a + j],
            device_id=(x, y, 1 - c), device_id_type=MESH)

    def local(a):
        return pltpu.make_async_copy(ins[a], outs[a].at[me], local_sems.at[a])

    hops = [(j, a) for j in range(3) for a in range(n)]

    def start():
        for a in range(n):
            local(a).start()
        for j, a in hops:
            ici(a, j, me).start()

    def finish():
        for j, a in hops:
            ici(a, j, 2 * chips[j][0] + chips[j][1]).wait_recv()
            d2d(a, j, c).start()
        for j, a in hops:
            d2d(a, j, 1 - c).wait_recv()
        for j, a in hops:
            ici(a, j, me).wait_send()
            d2d(a, j, c).wait_send()
        for a in range(n):
            local(a).wait()

    return start, finish


def _gather_sems(n):
    return [pltpu.SemaphoreType.DMA((3 * n,))] * 4 + [pltpu.SemaphoreType.DMA((n,))]


def _gather_weights(shards):
    n = len(shards)

    def body(*refs):
        start, finish = _gather_plan(refs[:n], refs[n:2 * n], refs[2 * n:])
        start()
        finish()

    return pl.pallas_call(
        body, name="gather_weights",
        in_specs=[HBM] * n, out_specs=[HBM] * n,
        out_shape=[jax.ShapeDtypeStruct((4,) + s.shape, s.dtype) for s in shards],
        scratch_shapes=_gather_sems(n),
    )(*shards)


def _exchange_chips(parts, small):
    n = len(parts)

    def body(*refs):
        ins, sm_ref = refs[:n], refs[n]
        outs, osm_ref = refs[n + 1:2 * n + 1], refs[2 * n + 1]
        send_sems, recv_sems, ssend_sems, srecv_sems, local_sems = refs[2 * n + 2:]
        x, y, c = _place()
        me = 2 * x + y
        dev = 4 * x + 2 * y + c
        local = [pltpu.make_async_copy(ins[a].at[me], outs[a].at[me], local_sems.at[a]) for a in range(n)]
        local.append(pltpu.make_async_copy(sm_ref, osm_ref.at[dev], local_sems.at[n]))
        for cp in local:
            cp.start()
        sends = []
        for j, (px, py) in enumerate(_other_chips(x, y)):
            for a in range(n):
                cp = pltpu.make_async_remote_copy(
                    src_ref=ins[a].at[2 * px + py], dst_ref=outs[a].at[me], send_sem=send_sems.at[3 * a + j],
                    recv_sem=recv_sems.at[3 * a + j], device_id=(px, py, c), device_id_type=MESH)
                cp.start()
                sends.append(cp)
        flips = [(fx, fy, fc) for fx in (0, 1) for fy in (0, 1) for fc in (0, 1)][1:]
        for j, (fx, fy, fc) in enumerate(flips):
            cp = pltpu.make_async_remote_copy(
                src_ref=sm_ref, dst_ref=osm_ref.at[dev], send_sem=ssend_sems.at[j], recv_sem=srecv_sems.at[j],
                device_id=(x ^ fx, y ^ fy, c ^ fc), device_id_type=MESH)
            cp.start()
            sends.append(cp)
        for j, (px, py) in enumerate(_other_chips(x, y)):
            for a in range(n):
                pltpu.make_async_remote_copy(
                    src_ref=ins[a].at[me], dst_ref=outs[a].at[2 * px + py], send_sem=send_sems.at[3 * a + j],
                    recv_sem=recv_sems.at[3 * a + j], device_id=(px, py, c), device_id_type=MESH).wait_recv()
        for j, (fx, fy, fc) in enumerate(flips):
            src = 4 * (x ^ fx) + 2 * (y ^ fy) + (c ^ fc)
            pltpu.make_async_remote_copy(
                src_ref=sm_ref, dst_ref=osm_ref.at[src], send_sem=ssend_sems.at[j], recv_sem=srecv_sems.at[j],
                device_id=(x ^ fx, y ^ fy, c ^ fc), device_id_type=MESH).wait_recv()
        for cp in sends:
            cp.wait_send()
        for cp in local:
            cp.wait()

    return pl.pallas_call(
        body, name="exchange_chips",
        in_specs=[HBM] * (n + 1), out_specs=[HBM] * (n + 1),
        out_shape=[jax.ShapeDtypeStruct(p.shape, p.dtype) for p in parts]
                  + [jax.ShapeDtypeStruct((8,) + small.shape, small.dtype)],
        scratch_shapes=[pltpu.SemaphoreType.DMA((3 * n,)), pltpu.SemaphoreType.DMA((3 * n,)),
                        pltpu.SemaphoreType.DMA((7,)), pltpu.SemaphoreType.DMA((7,)),
                        pltpu.SemaphoreType.DMA((n + 1,))],
    )(*parts, small)


def _swap_cores(parts, name):
    n = len(parts)

    def body(*refs):
        ins, outs, send_sems, recv_sems = refs[:n], refs[n:2 * n], refs[2 * n], refs[2 * n + 1]
        x, y, c = _place()
        copies = [pltpu.make_async_remote_copy(
            src_ref=ins[a], dst_ref=outs[a], send_sem=send_sems.at[a], recv_sem=recv_sems.at[a],
            device_id=(x, y, 1 - c), device_id_type=MESH) for a in range(n)]
        for cp in copies:
            cp.start()
        for cp in copies:
            cp.wait()

    return pl.pallas_call(
        body, name=name, in_specs=[HBM] * n, out_specs=[HBM] * n,
        out_shape=[jax.ShapeDtypeStruct(p.shape, p.dtype) for p in parts],
        scratch_shapes=[pltpu.SemaphoreType.DMA((n,)), pltpu.SemaphoreType.DMA((n,))],
    )(*parts)


def _tile(rows, cols):
    for cand in (256, 128, 64):
        if rows % cand == 0:
            return cand, cols
    if rows > 64 and cols % 256 == 0:
        return rows, 256
    return rows, cols


def _swap_layer_chunks(per_layer):
    n = len(per_layer[0])

    def body(*refs):
        ins, outs, send_sems, recv_sems = (refs[:n], refs[n:2 * n]), refs[2 * n:3 * n], refs[3 * n], refs[3 * n + 1]
        x, y, c = _place()
        for l in range(2):
            @pl.when(c == 1 - l)
            def _():
                copies = [pltpu.make_async_remote_copy(
                    src_ref=ins[l][a], dst_ref=outs[a], send_sem=send_sems.at[a], recv_sem=recv_sems.at[a],
                    device_id=(x, y, 1 - c), device_id_type=MESH) for a in range(n)]
                for cp in copies:
                    cp.start()
                for cp in copies:
                    cp.wait()

    return pl.pallas_call(
        body, name="swap_layer_chunks", in_specs=[HBM] * (2 * n), out_specs=[HBM] * n,
        out_shape=[jax.ShapeDtypeStruct(p.shape, p.dtype) for p in per_layer[0]],
        scratch_shapes=[pltpu.SemaphoreType.DMA((n,)), pltpu.SemaphoreType.DMA((n,))],
    )(*per_layer[0], *per_layer[1])


def _add(a0, a1, b, name):
    L, R, C = b.shape
    tr, tc = _tile(R, C)

    def body(a0_ref, a1_ref, b_ref, o_ref):
        mine = jnp.where(lax.axis_index("c") == 0, a0_ref[...], a1_ref[...])
        o_ref[...] = (mine + b_ref[...]).astype(BF16)

    spec = pl.BlockSpec((1, tr, tc), lambda l, i, j: (l, i, j))
    return pl.pallas_call(
        body, name=name, grid=(L, R // tr, C // tc), in_specs=[spec] * 3, out_specs=spec,
        out_shape=jax.ShapeDtypeStruct((L, R, C), BF16),
        compiler_params=_cparams(("parallel", "parallel", "parallel")),
    )(a0, a1, b)


def _sum_leading(buf, name):
    n, R, C = buf.shape
    tr, tc = _tile(R, C)

    def body(b_ref, o_ref):
        acc = b_ref[0].astype(F32)
        for k in range(1, n):
            acc = acc + b_ref[k].astype(F32)
        o_ref[...] = acc

    return pl.pallas_call(
        body, name=name, grid=(R // tr, C // tc),
        in_specs=[pl.BlockSpec((n, tr, tc), lambda i, j: (0, i, j))],
        out_specs=pl.BlockSpec((tr, tc), lambda i, j: (i, j)),
        out_shape=jax.ShapeDtypeStruct((R, C), F32),
        compiler_params=_cparams(("parallel", "parallel")),
    )(buf)


def _adam_update(w, g, m, v):
    c1 = 1.0 / (1.0 - ADAM_B1 ** ADAM_STEP)
    c2 = 1.0 / (1.0 - ADAM_B2 ** ADAM_STEP)
    mn = ADAM_B1 * m + (1.0 - ADAM_B1) * g
    vn = ADAM_B2 * v + (1.0 - ADAM_B2) * (g * g)
    return -ADAM_LR * ((mn * c1) / (jnp.sqrt(vn * c2) + ADAM_EPS) + ADAM_WD * w), mn, vn


def _adamw_layers(w, m, v, g_mine, g_other, name):
    _, R, C = w.shape
    tr, tc = _tile(R, C)

    def body(w_ref, m_ref, v_ref, gm_ref, go_ref, g_ref, d_ref, mo_ref, vo_ref):
        g = jnp.where(pl.program_id(0) == lax.axis_index("c"), gm_ref[...], go_ref[...])
        g_ref[0] = g
        d_ref[0], mo_ref[0], vo_ref[0] = _adam_update(w_ref[0], g, m_ref[0], v_ref[0])

    spec3 = pl.BlockSpec((1, tr, tc), lambda l, i, j: (l, i, j))
    spec2 = pl.BlockSpec((tr, tc), lambda l, i, j: (i, j))
    return pl.pallas_call(
        body, name=name, grid=(2, R // tr, C // tc),
        in_specs=[spec3] * 3 + [spec2] * 2, out_specs=[spec3] * 4,
        out_shape=[jax.ShapeDtypeStruct(w.shape, F32)] * 4,
        compiler_params=_cparams(("parallel", "parallel", "parallel")),
    )(w, m, v, g_mine, g_other)


PACK_C = 1024
_BIG = ("w_in", "w_out", "mla_w_uq", "mla_w_ukv", "conv_w")
_SMALL = ("norm_pre", "group_norm", "norm_post", "conv_b", "mla_q_norm", "mla_kv_norm", "attn_sinks")
_SMALL_W = {"norm_pre": 1024, "group_norm": 1024, "norm_post": 1024, "conv_b": 256, "mla_q_norm": 256,
            "mla_kv_norm": 128, "attn_sinks": 4}


_LOSS_AT = divmod(DEPTH * sum(_SMALL_W.values()), PACK_C)


def _pack_small(d, loss):
    flat = jnp.concatenate([d[n].reshape(-1) for n in _SMALL] + [loss.reshape(1)])
    return jnp.pad(flat, (0, 8 * PACK_C - flat.shape[0])).reshape(8, PACK_C)


def _adamw_small(w, m, v, got):
    ns = len(_SMALL)

    def body(*refs):
        got_ref = refs[3 * ns]
        outs = refs[3 * ns + 1:]
        gsum = got_ref[0]
        for d in range(1, 8):
            gsum = gsum + got_ref[d]
        outs[4 * ns][...] = gsum[_LOSS_AT[0]:_LOSS_AT[0] + 1, _LOSS_AT[1]:_LOSS_AT[1] + 1]
        off = 0
        for i, name in enumerate(_SMALL):
            wd = _SMALL_W[name]
            rows = []
            for l in range(DEPTH):
                r, c0 = divmod(off + l * wd, PACK_C)
                rows.append(gsum[r:r + 1, c0:c0 + wd])
            off += DEPTH * wd
            g = jnp.concatenate(rows, axis=0)
            delta, mn, vn = _adam_update(refs[i][...], g, refs[ns + i][...], refs[2 * ns + i][...])
            outs[i][...] = g
            outs[ns + i][...] = delta
            outs[2 * ns + i][...] = mn
            outs[3 * ns + i][...] = vn

    shapes = [jax.ShapeDtypeStruct(w[n].shape, F32) for n in _SMALL]
    res = pl.pallas_call(body, name="adamw_small", out_shape=shapes * 4 + [jax.ShapeDtypeStruct((1, 1), F32)])(
        *[w[n] for n in _SMALL], *[m[n] for n in _SMALL], *[v[n] for n in _SMALL], got)
    return [dict(zip(_SMALL, res[k * ns:(k + 1) * ns])) for k in range(4)], res[4 * ns]


def _w_in_internal(slabs):
    S, R, D = slabs.shape
    assert S * R == D_IN

    def body(w_ref, o_ref):
        for n, wd in _REAL:
            o, oi = _REAL_OFF[n][0], _INT_OFF[n]
            r = o
            while r < o + wd:
                end = min(o + wd, (r // R + 1) * R)
                o_ref[oi + r - o:oi + end - o, :] = w_ref[r // R, r % R:r % R + end - r, :]
                r = end
            if _INT_W[n] != wd:
                o_ref[oi + wd:oi + _INT_W[n], :] = jnp.zeros((_INT_W[n] - wd, D), slabs.dtype)

    return pl.pallas_call(
        body, name="w_in_internal", out_shape=jax.ShapeDtypeStruct((N_INT, D), slabs.dtype),
        compiler_params=pltpu.CompilerParams(vmem_limit_bytes=VMEM_LIMIT),
    )(slabs)


def _uq_internal(w):
    return jnp.pad(w.reshape(256, 4, 96), ((0, 0), (0, 0), (0, 32))).reshape(256, 512)


def _uq_real(dw):
    return dw.reshape(256, 4, 128)[:, :, :96].reshape(256, 384)


def _ukv_internal(w):
    w4 = w.reshape(128, 4, 128)
    k = jnp.pad(w4[:, :, :64], ((0, 0), (0, 0), (0, 64))).reshape(128, 512)
    return jnp.concatenate([k, w4[:, :, 64:].reshape(128, 256)], axis=1)


def _ukv_real(dw):
    k = dw[:, :512].reshape(128, 4, 128)[:, :, :64]
    v = dw[:, 512:].reshape(128, 4, 64)
    return jnp.concatenate([k, v], axis=2).reshape(128, 512)


def _layer_fwd(x, rope, p, tgt=None, fetch=()):
    xn, hb, hf, *fetched = _inproj_fwd(x, p["norm_pre"], p["w_in"], fetch)
    ya = _swa_fwd(hb, p["attn_sinks"])
    qm, km, vm, vt = _mla_prep_fwd(hf, rope, p["mla_q_norm"], p["mla_kv_norm"], p["mla_w_uq"], p["mla_w_ukv"])
    yc, lse = _mla_fwd(qm, km, vt)
    yd, tot, cnt = _sb_fwd(hb)
    x_next = _epilogue_fwd(x, ya, yc, yd, hf, p["conv_w"], p["conv_b"], p["group_norm"], p["w_out"], p["norm_post"],
                           tgt)
    saved = dict(x=x, xn=xn, hb=hb, hf=hf, ya=ya, yc=yc, yd=yd, tot=tot, cnt=cnt, qm=qm, km=km, vm=vm, lse=lse)
    return x_next, saved, fetched


def _layer_bwd(dx_next, rope, p, s):
    (dya, dyc, dyd, dhf, dw_out, dg_post, dg_grp, dconv_w, dconv_b) = _epilogue_bwd(
        dx_next, s["ya"], s["yc"], s["yd"], s["hf"], p["conv_w"], p["conv_b"], p["group_norm"], p["w_out"],
        p["norm_post"])
    dq_d, dk_d, dv_d = _sb_bwd(s["hb"], s["tot"], s["cnt"], dyd)
    dqm, dkt, dvt = _mla_bwd(s["qm"], s["km"], s["vm"], s["yc"], s["lse"], dyc)
    dc, dw_uq, dw_ukv, dg_q, dg_kv = _mla_prep_bwd(
        s["hf"], rope, p["mla_q_norm"], p["mla_kv_norm"], p["mla_w_uq"], p["mla_w_ukv"], dqm, dkt, dvt)
    dq_a, dk_a, dv_a, dsinks = _swa_bwd(s["hb"], p["attn_sinks"], dya)
    dx, dh, dg_pre = _inproj_bwd_dx(s["x"], p["norm_pre"], p["w_in"], dx_next,
                                    [dq_a, dk_a, dv_a, dq_d, dk_d, dv_d, dhf, dc])
    grads = dict(norm_pre=dg_pre[0], w_in_t=_inproj_bwd_dw(s["xn"], dh), attn_sinks=dsinks[0, :4], conv_w=dconv_w[:3],
                 conv_b=dconv_b[0], mla_q_norm=dg_q[0], mla_w_uq=_uq_real(dw_uq), mla_kv_norm=dg_kv[0],
                 mla_w_ukv=_ukv_real(dw_ukv), group_norm=dg_grp[0], w_out=dw_out, norm_post=dg_post[0])
    return dx, grads


_WEIGHTS = ["norm_pre", "w_in", "attn_sinks", "conv_w", "conv_b", "mla_q_norm", "mla_w_uq", "mla_kv_norm",
            "mla_w_ukv", "group_norm", "w_out", "norm_post"]


def kernel(x, positions, norm_pre, w_in, attn_sinks, conv_w, conv_b, mla_q_norm, mla_w_uq, mla_kv_norm, mla_w_ukv, group_norm, w_out, norm_post, loss_target, m_norm_pre, m_w_in, m_attn_sinks, m_conv_w, m_conv_b, m_mla_q_norm, m_mla_w_uq, m_mla_kv_norm, m_mla_w_ukv, m_group_norm, m_w_out, m_norm_post, v_norm_pre, v_w_in, v_attn_sinks, v_conv_w, v_conv_b, v_mla_q_norm, v_mla_w_uq, v_mla_kv_norm, v_mla_w_ukv, v_group_norm, v_w_out, v_norm_post):
    w = dict(norm_pre=norm_pre, w_in=w_in, attn_sinks=attn_sinks, conv_w=conv_w, conv_b=conv_b,
             mla_q_norm=mla_q_norm, mla_w_uq=mla_w_uq, mla_kv_norm=mla_kv_norm, mla_w_ukv=mla_w_ukv,
             group_norm=group_norm, w_out=w_out, norm_post=norm_post)
    m = dict(norm_pre=m_norm_pre, w_in=m_w_in, attn_sinks=m_attn_sinks, conv_w=m_conv_w, conv_b=m_conv_b,
             mla_q_norm=m_mla_q_norm, mla_w_uq=m_mla_w_uq, mla_kv_norm=m_mla_kv_norm, mla_w_ukv=m_mla_w_ukv,
             group_norm=m_group_norm, w_out=m_w_out, norm_post=m_norm_post)
    v = dict(norm_pre=v_norm_pre, w_in=v_w_in, attn_sinks=v_attn_sinks, conv_w=v_conv_w, conv_b=v_conv_b,
             mla_q_norm=v_mla_q_norm, mla_w_uq=v_mla_w_uq, mla_kv_norm=v_mla_kv_norm, mla_w_ukv=v_mla_w_ukv,
             group_norm=v_group_norm, w_out=v_w_out, norm_post=v_norm_post)
    T = x.shape[1]
    xs = x[0]
    rope = _rope_tables(positions[0].reshape(T, 1))
    tgt = loss_target[0]

    def shard_parts(l):
        halves = lambda a: a.reshape((2, a.shape[0] // 2) + a.shape[1:])
        return [halves(jnp.swapaxes(w["w_in"][l], 0, 1).astype(BF16))] + [
            halves(w[n][l].astype(BF16)) for n in _BIG[1:4]] + [jnp.stack([w["conv_w"][l]] * 2)]

    def layer_params(l, got):
        whole = lambda a: a.reshape((4, 2 * a.shape[2]) + a.shape[3:])
        by_cols = lambda a: jnp.transpose(a, (1, 0, 2)).reshape(a.shape[1], 4 * a.shape[2])
        return dict(
            norm_pre=norm_pre[l:l + 1], w_in=_w_in_internal(got[0].reshape((8,) + got[0].shape[2:])),
            attn_sinks=attn_sinks[l], conv_w=by_cols(got[4][:, 0]), conv_b=conv_b[l:l + 1],
            mla_q_norm=mla_q_norm[l:l + 1], mla_w_uq=_uq_internal(by_cols(whole(got[2]))),
            mla_kv_norm=mla_kv_norm[l:l + 1], mla_w_ukv=_ukv_internal(by_cols(whole(got[3]))),
            group_norm=group_norm[l:l + 1], w_out=whole(got[1]).reshape(D_MODEL, D_MODEL),
            norm_post=norm_post[l:l + 1])

    layers, saved = [], []
    h, got = xs, _gather_weights(shard_parts(0))
    for l in range(DEPTH):
        last = l == DEPTH - 1
        layers.append(layer_params(l, got))
        h, s, got = _layer_fwd(h, rope, layers[l], tgt if last else None, () if last else shard_parts(l + 1))
        saved.append(s)
    dy, loss_part = h

    grads = [None] * DEPTH
    for l in reversed(range(DEPTH)):
        dy, grads[l] = _layer_bwd(dy, rope, layers[l], saved[l])

    turned = ("w_in", "mla_w_uq")
    turn = lambda n, a: jnp.swapaxes(a, -1, -2) if n in turned else a

    def chunks(n, a):
        if n in ("w_out", "w_in"):
            return a.reshape(4, a.shape[0] // 4, a.shape[1])
        if n in turned:
            return a.T.reshape(4, a.shape[1] // 4, a.shape[0])
        return jnp.transpose(a.reshape(a.shape[0], 4, a.shape[1] // 4), (1, 0, 2))

    grad = lambda l, n: grads[l]["w_in_t" if n == "w_in" else n]
    per_layer = [[chunks(n, grad(l, n)) for n in _BIG] for l in range(DEPTH)]
    from_sibling = _swap_layer_chunks(per_layer)
    summed = [_add(a0, a1, b, "add_cores_" + n) for n, a0, a1, b in zip(_BIG, *per_layer, from_sibling)]
    small = _pack_small({n: jnp.stack([grads[l][n] for l in range(DEPTH)]) for n in _SMALL}, loss_part[0, 0])
    *got, got_small = _exchange_chips(summed, small)
    done = [_sum_leading(b, "sum_chips_" + n) for n, b in zip(_BIG, got)]
    done_other = _swap_cores(done, "swap_layer_shards")

    outs, loss = _adamw_small(w, m, v, got_small)
    for n, gm, go in zip(_BIG, done, done_other):
        for d, a in zip(outs, _adamw_layers(turn(n, w[n]), turn(n, m[n]), turn(n, v[n]), gm, go, "adamw_" + n)):
            d[n] = turn(n, a)
    return (loss[0, 0], dy[None], *[outs[0][n] for n in _WEIGHTS], *[outs[1][n] for n in _WEIGHTS],
            *[outs[2][n] for n in _WEIGHTS], *[outs[3][n] for n in _WEIGHTS])
```

```python
import math

import jax
import jax.numpy as jnp
from jax import lax
from jax.experimental import pallas as pl
from jax.experimental.pallas import tpu as pltpu

F32 = jnp.float32
BF16 = jnp.bfloat16
MESH = pl.DeviceIdType.MESH

D_MODEL = 1024
DEPTH = 2
EPS = 1e-6
BLOCK = 128
HEAD = 64
LANES = 128
GROUP = 256
LOG2E = 1.4426950408889634
LN2 = 0.6931471805599453
MLA_QSCALE = 96 ** -0.5 * LOG2E
ROPE_HALF = 16
ROPE_THETA = 10000.0
SWA_SUB = 2
ATT_BLK = 256
MLA_BQ = 512
NEG = -1e30
SB_DEAD = -104.0

ADAM_LR, ADAM_B1, ADAM_B2, ADAM_EPS, ADAM_WD, ADAM_STEP = 0.001, 0.9, 0.999, 1e-08, 0.01, 10

_REAL = [("a_q", 256), ("a_k", 128), ("a_v", 128), ("b_b", 256), ("b_c", 256), ("b_x", 256),
         ("c_q", 256), ("c_kv", 128), ("c_kr", 32), ("d_q", 256), ("d_k", 256), ("d_v", 256),
         ("gate", 1024)]
_REAL_OFF = {}
_o = 0
for _n, _w in _REAL:
    _REAL_OFF[_n] = (_o, _w)
    _o += _w
D_IN = _o
_INT_ORDER = ["a_q", "a_k", "a_v", "d_q", "d_k", "d_v", "gate", "b_b", "b_c", "b_x", "c_q", "c_kv", "c_kr"]
_INT_W = dict(_REAL)
_INT_W["c_kr"] = 128
_INT_OFF = {}
_o = 0
for _n in _INT_ORDER:
    _INT_OFF[_n] = _o
    _o += _INT_W[_n]
N_INT = _o
N_HB = _INT_OFF["gate"]
N_HF = N_INT - N_HB

VMEM_LIMIT = 56 * 1024 * 1024


def _cparams(sem):
    return pltpu.CompilerParams(dimension_semantics=sem, vmem_limit_bytes=VMEM_LIMIT)


def _dot(a, b):
    return jnp.dot(a, b, preferred_element_type=F32)


def _dot_nt(a, b):
    return lax.dot_general(a, b, (((1,), (1,)), ((), ())), preferred_element_type=F32)


def _dot_tn(a, b):
    return lax.dot_general(a, b, (((0,), (0,)), ((), ())), preferred_element_type=F32)


def _split(x):
    hi = x.astype(BF16)
    lo = (x - hi.astype(F32)).astype(BF16)
    return hi, lo


def _rms(x):
    return lax.rsqrt(jnp.mean(x * x, axis=-1, keepdims=True) + EPS)


def _rms_bwd(dy, xhat, r, g):
    dxhat = dy * g
    return r * (dxhat - xhat * jnp.mean(dxhat * xhat, axis=-1, keepdims=True)), dy * xhat


def _colsum(x):
    return jnp.sum(x, axis=0, keepdims=True)


def _inproj_fwd(x, g, wt, fetch=()):
    T = x.shape[0]
    tm = 512
    nt, n = T // tm, len(fetch)

    def body(x_ref, g_ref, w_ref, *rest):
        xn_ref, hb_ref, hf_ref = rest[n:n + 3]
        if n:
            start, finish = _gather_plan(rest[:n], rest[n + 3:2 * n + 3], rest[2 * n + 3:])
            pl.when(pl.program_id(0) == 0)(start)
        xv = x_ref[...]
        xn = (xv * _rms(xv) * g_ref[...]).astype(BF16)
        xn_ref[...] = xn
        h = _dot_nt(xn, w_ref[...])
        hb_ref[...] = h[:, :N_HB].astype(BF16)
        hf_ref[...] = h[:, N_HB:]
        if n:
            pl.when(pl.program_id(0) == nt - 1)(finish)

    return pl.pallas_call(
        body, name="inproj_fwd_fetch" if n else "inproj_fwd", grid=(nt,),
        in_specs=[pl.BlockSpec((tm, D_MODEL), lambda i: (i, 0)),
                  pl.BlockSpec((1, D_MODEL), lambda i: (0, 0)),
                  pl.BlockSpec((N_INT, D_MODEL), lambda i: (0, 0))] + [HBM] * n,
        out_specs=[pl.BlockSpec((tm, D_MODEL), lambda i: (i, 0)),
                   pl.BlockSpec((tm, N_HB), lambda i: (i, 0)),
                   pl.BlockSpec((tm, N_HF), lambda i: (i, 0))] + [HBM] * n,
        out_shape=[jax.ShapeDtypeStruct((T, D_MODEL), BF16),
                   jax.ShapeDtypeStruct((T, N_HB), BF16),
                   jax.ShapeDtypeStruct((T, N_HF), F32)]
                  + [jax.ShapeDtypeStruct((4,) + s.shape, s.dtype) for s in fetch],
        scratch_shapes=_gather_sems(n) if n else [],
        compiler_params=_cparams(("arbitrary",) if n else ("parallel",)),
    )(x, g, wt, *fetch)


def _inproj_bwd_dx(x, g, wt, dx_next, pieces):
    T = x.shape[0]
    tm = 512
    widths = [p.shape[1] for p in pieces]
    assert sum(widths) == N_INT

    def body(x_ref, g_ref, w_ref, dxn_ref, *rest):
        p_refs = rest[:len(pieces)]
        dx_ref, dh_ref, dg_ref = rest[len(pieces):]
        dh = jnp.concatenate([p[...].astype(BF16) for p in p_refs], axis=1)
        dh_ref[...] = dh
        dxn = _dot(dh, w_ref[...])
        xv = x_ref[...]
        r = _rms(xv)
        dx, dgrow = _rms_bwd(dxn, xv * r, r, g_ref[...])
        dx_ref[...] = dx + dxn_ref[...]

        @pl.when(pl.program_id(0) == 0)
        def _():
            dg_ref[...] = jnp.zeros_like(dg_ref)

        dg_ref[...] += _colsum(dgrow)

    return pl.pallas_call(
        body, name="inproj_bwd_dx", grid=(T // tm,),
        in_specs=[pl.BlockSpec((tm, D_MODEL), lambda i: (i, 0)),
                  pl.BlockSpec((1, D_MODEL), lambda i: (0, 0)),
                  pl.BlockSpec((N_INT, D_MODEL), lambda i: (0, 0)),
                  pl.BlockSpec((tm, D_MODEL), lambda i: (i, 0))]
                 + [pl.BlockSpec((tm, wd), lambda i: (i, 0)) for wd in widths],
        out_specs=[pl.BlockSpec((tm, D_MODEL), lambda i: (i, 0)),
                   pl.BlockSpec((tm, N_INT), lambda i: (i, 0)),
                   pl.BlockSpec((1, D_MODEL), lambda i: (0, 0))],
        out_shape=[jax.ShapeDtypeStruct((T, D_MODEL), F32),
                   jax.ShapeDtypeStruct((T, N_INT), BF16),
                   jax.ShapeDtypeStruct((1, D_MODEL), F32)],
        compiler_params=_cparams(("arbitrary",)),
    )(x, g, wt, dx_next, *pieces)


def _inproj_bwd_dw(xn, dh):
    T = xn.shape[0]
    tm = min(512, T)

    def body(a_ref, b_ref, o_ref):
        @pl.when(pl.program_id(0) == 0)
        def _():
            o_ref[...] = jnp.zeros_like(o_ref)

        for n, wd in _REAL:
            o, oi = _REAL_OFF[n][0], _INT_OFF[n]
            o_ref[o:o + wd, :] += _dot_tn(b_ref[:, oi:oi + _INT_W[n]], a_ref[...])[:wd]

    return pl.pallas_call(
        body, name="inproj_bwd_dw", grid=(T // tm,),
        in_specs=[pl.BlockSpec((tm, D_MODEL), lambda t: (t, 0)),
                  pl.BlockSpec((tm, N_INT), lambda t: (t, 0))],
        out_specs=pl.BlockSpec((D_IN, D_MODEL), lambda t: (0, 0)),
        out_shape=jax.ShapeDtypeStruct((D_IN, D_MODEL), F32),
        compiler_params=_cparams(("arbitrary",)),
    )(xn, dh)


def _roll_f32(x, shift):
    return pltpu.roll(x.astype(F32), shift, 1)


def _swa_operands(h, q, k_prev, k_cur, v_prev, v_cur):
    p, e = h // 2, h % 2
    lane = lax.broadcasted_iota(jnp.int32, (1, LANES), 1) // HEAD
    q = q[:, p * LANES:(p + 1) * LANES]
    if e != p:
        q = _roll_f32(q, HEAD).astype(BF16)
        v_prev = _roll_f32(v_prev, HEAD).astype(BF16)
        v_cur = _roll_f32(v_cur, HEAD).astype(BF16)
    qs = jnp.where(lane == p, q, 0) * 0.125
    return dict(p=p, e=e, lane=lane, qs=qs, k_prev=k_prev, k_cur=k_cur,
                v_prev=jnp.where(lane == e, v_prev, 0), v_cur=jnp.where(lane == e, v_cur, 0),
                s_prev=_dot_nt(qs, k_prev), s_cur=_dot_nt(qs, k_cur))


def _swa_probs(ops, sink, no_prev):
    row = lax.broadcasted_iota(jnp.int32, (BLOCK, BLOCK), 0)
    col = lax.broadcasted_iota(jnp.int32, (BLOCK, BLOCK), 1)
    ok_prev = col > row if no_prev is None else jnp.logical_and(col > row, jnp.logical_not(no_prev))
    s_prev = jnp.where(ok_prev, ops["s_prev"], NEG)
    s_cur = jnp.where(col <= row, ops["s_cur"], NEG)
    m = jnp.maximum(jnp.maximum(jnp.max(s_prev, axis=1, keepdims=True),
                                jnp.max(s_cur, axis=1, keepdims=True)), sink)
    p_prev = jnp.exp(s_prev - m)
    p_cur = jnp.exp(s_cur - m)
    p_sink = jnp.exp(sink - m)
    inv = 1.0 / (jnp.sum(p_prev, axis=1, keepdims=True) + jnp.sum(p_cur, axis=1, keepdims=True) + p_sink)
    return p_prev * inv, p_cur * inv, p_sink * inv


def _swa_specs(T):
    n = T // (BLOCK * SWA_SUB)
    qo, ko, vo = (_INT_OFF[name] // LANES for name in ("a_q", "a_k", "a_v"))
    halo = lambda i: jnp.maximum(i * SWA_SUB - 1, 0)
    return [pl.BlockSpec((BLOCK * SWA_SUB, 256), lambda i: (i, qo // 2)),
            pl.BlockSpec((BLOCK, LANES), lambda i: (halo(i), ko)),
            pl.BlockSpec((BLOCK * SWA_SUB, LANES), lambda i: (i, ko)),
            pl.BlockSpec((BLOCK, LANES), lambda i: (halo(i), vo)),
            pl.BlockSpec((BLOCK * SWA_SUB, LANES), lambda i: (i, vo)),
            pl.BlockSpec(memory_space=pltpu.SMEM)], n


def _swa_units(q_ref, kh_ref, kc_ref, vh_ref, vc_ref, s_ref):
    blk = lambda a: slice(a * BLOCK, (a + 1) * BLOCK)
    units = [(a, h) for a in range(SWA_SUB) for h in range(4)]
    ops = {}
    for a, h in units:
        k_prev, v_prev = (kh_ref[...], vh_ref[...]) if a == 0 else (kc_ref[blk(a - 1), :], vc_ref[blk(a - 1), :])
        ops[a, h] = _swa_operands(h, q_ref[blk(a), :], k_prev, kc_ref[blk(a), :], v_prev, vc_ref[blk(a), :])
    probs = {(a, h): _swa_probs(ops[a, h], s_ref[h], pl.program_id(0) == 0 if a == 0 else None) for a, h in units}
    return units, ops, probs, blk


def _swa_fwd(hb, sinks):
    T = hb.shape[0]
    specs, n = _swa_specs(T)

    def body(q_ref, kh_ref, kc_ref, vh_ref, vc_ref, s_ref, o_ref):
        units, ops, probs, blk = _swa_units(q_ref, kh_ref, kc_ref, vh_ref, vc_ref, s_ref)
        outs = {u: _dot(probs[u][0].astype(BF16), ops[u]["v_prev"]) + _dot(probs[u][1].astype(BF16), ops[u]["v_cur"])
                for u in units}
        for a in range(SWA_SUB):
            for p in range(2):
                o_ref[blk(a), p * LANES:(p + 1) * LANES] = outs[a, 2 * p] + outs[a, 2 * p + 1]

    return pl.pallas_call(
        body, name="swa_fwd", grid=(n,), in_specs=specs,
        out_specs=pl.BlockSpec((BLOCK * SWA_SUB, 256), lambda i: (i, 0)),
        out_shape=jax.ShapeDtypeStruct((T, 256), F32),
        compiler_params=_cparams(("parallel",)),
    )(hb, hb, hb, hb, hb, sinks)


def _swa_bwd(hb, sinks, dy):
    T = hb.shape[0]
    specs, n = _swa_specs(T)

    def body(q_ref, kh_ref, kc_ref, vh_ref, vc_ref, s_ref, dy_ref, dq_ref, dk_ref, dv_ref, ds_ref):
        i = pl.program_id(0)

        @pl.when(i == 0)
        def _():
            ds_ref[...] = jnp.zeros_like(ds_ref)

        lane_id = lax.broadcasted_iota(jnp.int32, (8, LANES), 1)
        units, ops, probs, blk = _swa_units(q_ref, kh_ref, kc_ref, vh_ref, vc_ref, s_ref)
        dos = {(a, h): jnp.where(ops[a, h]["lane"] == ops[a, h]["e"],
                                 dy_ref[blk(a), ops[a, h]["p"] * LANES:(ops[a, h]["p"] + 1) * LANES], 0.0)
               for a, h in units}
        dobs = {u: dos[u].astype(BF16) for u in units}
        pbs = {u: (probs[u][0].astype(BF16), probs[u][1].astype(BF16)) for u in units}
        outs = {u: _dot(pbs[u][0], ops[u]["v_prev"]) + _dot(pbs[u][1], ops[u]["v_cur"]) for u in units}
        dps = {u: (_dot_nt(dobs[u], ops[u]["v_prev"]), _dot_nt(dobs[u], ops[u]["v_cur"])) for u in units}
        dss, dsinks = {}, jnp.zeros((8, LANES), F32)
        for u in units:
            delta = jnp.sum(dos[u] * outs[u], axis=1, keepdims=True)
            dss[u] = ((probs[u][0] * (dps[u][0] - delta)).astype(BF16),
                      (probs[u][1] * (dps[u][1] - delta)).astype(BF16))
            dsink = -jnp.sum(probs[u][2] * delta, axis=0, keepdims=True)
            dsinks += jnp.where(lane_id == u[1], dsink, 0.0)
        ds_ref[...] += dsinks
        dqs = {u: (_dot(dss[u][0], ops[u]["k_prev"]) + _dot(dss[u][1], ops[u]["k_cur"])) * 0.125 for u in units}
        zero = jnp.zeros((BLOCK, LANES), F32)
        dk_as_prev, dk_as_cur = [zero] * SWA_SUB, [zero] * SWA_SUB
        dv_as_prev, dv_as_cur = [zero] * SWA_SUB, [zero] * SWA_SUB
        for a, h in units:
            p, e = ops[a, h]["p"], ops[a, h]["e"]
            dob_v = dobs[a, h] if e == p else pltpu.roll(dos[a, h], HEAD, 1).astype(BF16)
            dk_as_prev[a] = dk_as_prev[a] + _dot_tn(dss[a, h][0], ops[a, h]["qs"])
            dk_as_cur[a] = dk_as_cur[a] + _dot_tn(dss[a, h][1], ops[a, h]["qs"])
            dv_as_prev[a] = dv_as_prev[a] + _dot_tn(pbs[a, h][0], dob_v)
            dv_as_cur[a] = dv_as_cur[a] + _dot_tn(pbs[a, h][1], dob_v)
        base = i * SWA_SUB
        for a in range(SWA_SUB):
            rows = pl.ds(pl.multiple_of((base + a) * BLOCK, BLOCK), BLOCK)
            more = a + 1 < SWA_SUB
            dk_ref[rows, :] = dk_as_cur[a] + (dk_as_prev[a + 1] if more else 0.0)
            dv_ref[rows, :] = dv_as_cur[a] + (dv_as_prev[a + 1] if more else 0.0)
        halo = pl.ds(pl.multiple_of(jnp.maximum(base - 1, 0) * BLOCK, BLOCK), BLOCK)
        dk_ref[halo, :] += dk_as_prev[0]
        dv_ref[halo, :] += dv_as_prev[0]
        for a in range(SWA_SUB):
            for p in range(2):
                dq_pair = jnp.zeros((BLOCK, LANES), F32)
                for e in range(2):
                    dq = jnp.where(ops[a, 2 * p + e]["lane"] == p, dqs[a, 2 * p + e], 0.0)
                    dq_pair += dq if e == p else pltpu.roll(dq, HEAD, 1)
                dq_ref[blk(a), p * LANES:(p + 1) * LANES] = dq_pair.astype(BF16)

    return pl.pallas_call(
        body, name="swa_bwd", grid=(n,),
        in_specs=specs + [pl.BlockSpec((BLOCK * SWA_SUB, 256), lambda i: (i, 0))],
        out_specs=[pl.BlockSpec((BLOCK * SWA_SUB, 256), lambda i: (i, 0)),
                   pl.BlockSpec((T, LANES), lambda i: (0, 0)),
                   pl.BlockSpec((T, LANES), lambda i: (0, 0)),
                   pl.BlockSpec((8, LANES), lambda i: (0, 0))],
        out_shape=[jax.ShapeDtypeStruct((T, 256), BF16),
                   jax.ShapeDtypeStruct((T, LANES), F32),
                   jax.ShapeDtypeStruct((T, LANES), F32),
                   jax.ShapeDtypeStruct((8, LANES), F32)],
        compiler_params=_cparams(("arbitrary",)),
    )(hb, hb, hb, hb, hb, sinks, dy)


def _rope_tables(pos, fetch=()):
    T = pos.shape[0]
    tm = 512
    nt, n = T // tm, len(fetch)

    def body(pos_ref, *rest):
        o_ref = rest[n]
        if n:
            start, finish = _gather_plan(rest[:n], rest[n + 1:2 * n + 1], rest[2 * n + 1:])
            pl.when(pl.program_id(0) == 0)(start)
        lane = lax.broadcasted_iota(jnp.int32, (1, LANES), 1)
        active = jnp.logical_and(lane >= HEAD, lane < HEAD + 2 * ROPE_HALF)
        idx = ((lane - HEAD) % ROPE_HALF).astype(F32)
        freq = jnp.exp(idx * (-math.log(ROPE_THETA) / ROPE_HALF))
        ang = pos_ref[...].astype(F32) * freq
        cos, sin = jnp.cos(ang), jnp.sin(ang)
        o_ref[:, 0:LANES] = jnp.where(active, cos, 1.0)
        o_ref[:, LANES:2 * LANES] = jnp.where(jnp.logical_and(active, lane >= HEAD + ROPE_HALF), sin, 0.0)
        o_ref[:, 2 * LANES:] = jnp.where(jnp.logical_and(active, lane < HEAD + ROPE_HALF), -sin, 0.0)
        if n:
            pl.when(pl.program_id(0) == nt - 1)(finish)

    return pl.pallas_call(
        body, name="rope_tables_fetch" if n else "rope_tables", grid=(nt,),
        in_specs=[pl.BlockSpec((tm, 1), lambda i: (i, 0))] + [HBM] * n,
        out_specs=[pl.BlockSpec((tm, 3 * LANES), lambda i: (i, 0))] + [HBM] * n,
        out_shape=[jax.ShapeDtypeStruct((T, 3 * LANES), F32)]
                  + [jax.ShapeDtypeStruct((4,) + s.shape, s.dtype) for s in fetch],
        scratch_shapes=_gather_sems(n) if n else [],
        compiler_params=_cparams(("arbitrary",) if n else ("parallel",)),
    )(pos, *fetch)


def _rope_factors(tab_ref):
    return tab_ref[:, 0:LANES], tab_ref[:, LANES:2 * LANES], tab_ref[:, 2 * LANES:]


def _rope(x, tabs):
    c, s_up, s_dn = tabs
    return x * c + pltpu.roll(x, ROPE_HALF, 1) * s_up + pltpu.roll(x, LANES - ROPE_HALF, 1) * s_dn


def _rope_t(dy, tabs):
    c, s_up, s_dn = tabs
    return dy * c + pltpu.roll(dy * s_up, LANES - ROPE_HALF, 1) + pltpu.roll(dy * s_dn, ROPE_HALF, 1)


def _mla_lat_specs(tm):
    cq, ckv, ckr = ((_INT_OFF[n] - N_HB) for n in ("c_q", "c_kv", "c_kr"))
    return [pl.BlockSpec((tm, 256), lambda i: (i, cq // 256)),
            pl.BlockSpec((tm, LANES), lambda i: (i, ckv // LANES)),
            pl.BlockSpec((tm, LANES), lambda i: (i, ckr // LANES)),
            pl.BlockSpec((tm, 3 * LANES), lambda i: (i, 0)),
            pl.BlockSpec((1, 256), lambda i: (0, 0)),
            pl.BlockSpec((1, LANES), lambda i: (0, 0)),
            pl.BlockSpec((256, 512), lambda i: (0, 0)),
            pl.BlockSpec((LANES, 768), lambda i: (0, 0))]


def _mla_prep_fwd(hf, rope, g_q, g_kv, w_uq, w_ukv):
    T = hf.shape[0]
    tm = 512
    sub = tm // ATT_BLK

    def body(cq_ref, ckv_ref, ckr_ref, tab_ref, gq_ref, gkv_ref, wq_ref, wkv_ref, qm_ref, km_ref, vm_ref, vt_ref):
        tabs = _rope_factors(tab_ref)
        cq = cq_ref[...]
        q = _dot((cq * _rms(cq) * gq_ref[...]).astype(BF16), wq_ref[...])
        ckv = ckv_ref[...]
        kv = _dot((ckv * _rms(ckv) * gkv_ref[...]).astype(BF16), wkv_ref[...])
        kr = _rope(pltpu.roll(ckr_ref[...], HEAD, 1), tabs)
        for h in range(4):
            sl = slice(h * LANES, (h + 1) * LANES)
            qm_ref[:, sl] = (_rope(q[:, sl], tabs) * MLA_QSCALE).astype(BF16)
            km_ref[:, sl] = (kv[:, sl] + kr).astype(BF16)
        vm_ref[...] = kv[:, 512:].astype(BF16)
        for p in range(2):
            for s in range(sub):
                tile = kv[s * ATT_BLK:(s + 1) * ATT_BLK, 512 + p * LANES:512 + (p + 1) * LANES]
                vt_ref[p, s] = jnp.transpose(tile).astype(BF16)

    return pl.pallas_call(
        body, name="mla_prep_fwd", grid=(T // tm,), in_specs=_mla_lat_specs(tm),
        out_specs=[pl.BlockSpec((tm, 512), lambda i: (i, 0)),
                   pl.BlockSpec((tm, 512), lambda i: (i, 0)),
                   pl.BlockSpec((tm, 256), lambda i: (i, 0)),
                   pl.BlockSpec((2, sub, LANES, ATT_BLK), lambda i: (0, i, 0, 0))],
        out_shape=[jax.ShapeDtypeStruct((T, 512), BF16),
                   jax.ShapeDtypeStruct((T, 512), BF16),
                   jax.ShapeDtypeStruct((T, 256), BF16),
                   jax.ShapeDtypeStruct((2, T // ATT_BLK, LANES, ATT_BLK), BF16)],
        compiler_params=_cparams(("parallel",)),
    )(hf, hf, hf, rope, g_q, g_kv, w_uq, w_ukv)


def _mla_prep_bwd(hf, rope, g_q, g_kv, w_uq, w_ukv, dqm, dkt, dvt):
    T = hf.shape[0]
    tm = 512
    sub = tm // ATT_BLK

    def body(cq_ref, ckv_ref, ckr_ref, tab_ref, gq_ref, gkv_ref, wq_ref, wkv_ref, dq_ref, dk_ref, dv_ref,
             dc_ref, dwq_ref, dwkv_ref, dgq_ref, dgkv_ref):
        @pl.when(pl.program_id(0) == 0)
        def _():
            dwq_ref[...] = jnp.zeros_like(dwq_ref)
            dwkv_ref[...] = jnp.zeros_like(dwkv_ref)
            dgq_ref[...] = jnp.zeros_like(dgq_ref)
            dgkv_ref[...] = jnp.zeros_like(dgkv_ref)

        tabs = _rope_factors(tab_ref)
        lane =lax.broadcasted_iota(jnp.int32, (1, LANES), 1)
        dq = jnp.concatenate([_rope_t(dq_ref[:, h * LANES:(h + 1) * LANES] * MLA_QSCALE, tabs)
                              for h in range(4)], axis=1).astype(BF16)
        cq = cq_ref[...]
        rq = _rms(cq)
        cqn = (cq * rq * gq_ref[...]).astype(BF16)
        dwq_ref[...] += _dot_tn(cqn, dq)
        dcq, dgrow = _rms_bwd(_dot_nt(dq, wq_ref[...]), cq * rq, rq, gq_ref[...])
        dgq_ref[...] += _colsum(dgrow)
        dc_ref[:, 0:256] = dcq.astype(BF16)

        dk = jnp.concatenate([jnp.concatenate([jnp.transpose(dk_ref[p, s]) for p in range(2)], axis=1)
                              for s in range(sub)], axis=0) * LN2
        dv = jnp.concatenate([jnp.concatenate([jnp.transpose(dv_ref[p, s]) for p in range(2)], axis=1)
                              for s in range(sub)], axis=0)
        dkr = dk[:, 0:LANES] + dk[:, LANES:2 * LANES] + dk[:, 2 * LANES:3 * LANES] + dk[:, 3 * LANES:]
        dkr = pltpu.roll(_rope_t(dkr, tabs), HEAD, 1)
        dc_ref[:, 384:512] = jnp.where(lane < 2 * ROPE_HALF, dkr, 0.0).astype(BF16)
        dkv = jnp.concatenate([dk.astype(BF16), dv.astype(BF16)], axis=1)
        ckv = ckv_ref[...]
        rkv = _rms(ckv)
        ckvn = (ckv * rkv * gkv_ref[...]).astype(BF16)
        dwkv_ref[...] += _dot_tn(ckvn, dkv)
        dckv, dgrow = _rms_bwd(_dot_nt(dkv, wkv_ref[...]), ckv * rkv, rkv, gkv_ref[...])
        dgkv_ref[...] += _colsum(dgrow)
        dc_ref[:, 256:384] = dckv.astype(BF16)

    return pl.pallas_call(
        body, name="mla_prep_bwd", grid=(T // tm,),
        in_specs=_mla_lat_specs(tm) + [pl.BlockSpec((tm, 512), lambda i: (i, 0)),
                                       pl.BlockSpec((2, sub, 256, ATT_BLK), lambda i: (0, i, 0, 0)),
                                       pl.BlockSpec((2, sub, LANES, ATT_BLK), lambda i: (0, i, 0, 0))],
        out_specs=[pl.BlockSpec((tm, 512), lambda i: (i, 0)),
                   pl.BlockSpec((256, 512), lambda i: (0, 0)),
                   pl.BlockSpec((LANES, 768), lambda i: (0, 0)),
                   pl.BlockSpec((1, 256), lambda i: (0, 0)),
                   pl.BlockSpec((1, LANES), lambda i: (0, 0))],
        out_shape=[jax.ShapeDtypeStruct((T, 512), BF16),
                   jax.ShapeDtypeStruct((256, 512), F32),
                   jax.ShapeDtypeStruct((LANES, 768), F32),
                   jax.ShapeDtypeStruct((1, 256), F32),
                   jax.ShapeDtypeStruct((1, LANES), F32)],
        compiler_params=_cparams(("arbitrary",)),
    )(hf, hf, hf, rope, g_q, g_kv, w_uq, w_ukv, dqm, dkt, dvt)


def _causal_masks(bq, bk):
    row = lax.broadcasted_iota(jnp.int32, (bq, bk), 0)
    col = lax.broadcasted_iota(jnp.int32, (bq, bk), 1)
    return row, col


def _mla_fwd(qm, km, vt):
    T = qm.shape[0]
    bq, bk = min(MLA_BQ, T), ATT_BLK
    nq, nsub, nk = T // bq, bq // bk, T // bk

    def body(q_ref, k_ref, vt_ref, o_ref, lse_ref, acc_ref, m_ref, l_ref):
        qi = pl.program_id(0)
        key = lax.broadcasted_iota(jnp.int32, (bk, bq), 0)
        qry = lax.broadcasted_iota(jnp.int32, (bk, bq), 1)
        ones = jnp.ones((8, bk), BF16)
        acc_ref[...] = jnp.zeros_like(acc_ref)
        m_ref[...] = jnp.full_like(m_ref, NEG)
        l_ref[...] = jnp.zeros_like(l_ref)

        def step(kb0, masked):
            kbs = [kb0 + d for d in range(nsub)]
            qs = [slice(d * bk if masked else 0, bq) for d in range(nsub)]

            def wide(a, d, fill):
                if not qs[d].start:
                    return a
                return jnp.concatenate([jnp.full((a.shape[0], qs[d].start), fill, a.dtype), a], axis=1)

            sts = [[_dot_nt(k_ref[pl.ds(pl.multiple_of(kb * bk, bk), bk), e * LANES:(e + 1) * LANES],
                            q_ref[qs[d], e * LANES:(e + 1) * LANES]) for d, kb in enumerate(kbs)] for e in range(4)]
            pts, alphas = [], []
            for e in range(4):
                st = ([jnp.where(key[:, qs[d]] + d * bk <= qry[:, qs[d]], sts[e][d], NEG) for d in range(nsub)]
                      if masked else sts[e])
                m_prev = m_ref[e, 0:1, :]
                m_new = m_prev
                for d in range(nsub):
                    m_new = jnp.maximum(m_new, wide(jnp.max(st[d], axis=0, keepdims=True), d, NEG))
                alpha = jnp.exp2(m_prev - m_new)
                pt = [jnp.exp2(st[d] - m_new[:, qs[d]]).astype(BF16) for d in range(nsub)]
                l_new = alpha * l_ref[e]
                for d in range(nsub):
                    l_new = l_new + wide(_dot(ones, pt[d]), d, 0.0)
                l_ref[e] = l_new
                m_ref[e] = jnp.broadcast_to(m_new, (8, bq))
                pts.append(pt)
                alphas.append(alpha)
            for e in range(4):
                acc = alphas[e] * acc_ref[e]
                for d in range(nsub):
                    v_t = vt_ref[e // 2, kbs[d], (e % 2) * HEAD:(e % 2 + 1) * HEAD, :]
                    acc = acc + wide(_dot(v_t, pts[e][d]), d, 0.0)
                acc_ref[e] = acc

        step(qi * nsub, True)

        def loop(t, c):
            step(t * nsub, False)
            return c

        lax.fori_loop(0, qi, loop, 0)
        outs, lses = [], []
        for e in range(4):
            l = l_ref[e, 0:1, :]
            outs.append(acc_ref[e] / l)
            lses.append(jnp.broadcast_to(m_ref[e, 0:1, :] * LN2 + jnp.log(l), (HEAD, bq)))
        o_ref[...] = jnp.transpose(jnp.concatenate(outs, axis=0))
        lse_ref[...] = jnp.transpose(jnp.concatenate(lses, axis=0))

    return pl.pallas_call(
        body, name="mla_fwd", grid=(nq,),
        in_specs=[pl.BlockSpec((bq, 512), lambda i: (i, 0)),
                  pl.BlockSpec((T, 512), lambda i: (0, 0)),
                  pl.BlockSpec((2, nk, LANES, bk), lambda i: (0, 0, 0, 0))],
        out_specs=[pl.BlockSpec((bq, 256), lambda i: (i, 0)),
                   pl.BlockSpec((bq, 256), lambda i: (i, 0))],
        out_shape=[jax.ShapeDtypeStruct((T, 256), F32), jax.ShapeDtypeStruct((T, 256), F32)],
        scratch_shapes=[pltpu.VMEM((4, HEAD, bq), F32), pltpu.VMEM((4, 8, bq), F32), pltpu.VMEM((4, 8, bq), F32)],
        compiler_params=_cparams(("arbitrary",)),
    )(qm, km, vt)


def _mla_bwd(qm, km, vm, y, lse, dy):
    T = qm.shape[0]
    bq, bk = min(MLA_BQ, T), ATT_BLK
    nq, nsub, nk = T // bq, bq // bk, T // bk

    def body(q_ref, k_ref, v_ref, y_ref, lse_ref, dy_ref, dq_ref, dkt_ref, dvt_ref, dob_ref, st_ref, qt_ref, dot_ref):
        qi = pl.program_id(1)

        @pl.when(qi == 0)
        def _():
            dkt_ref[...] = jnp.zeros_like(dkt_ref)
            dvt_ref[...] = jnp.zeros_like(dvt_ref)

        lane = lax.broadcasted_iota(jnp.int32, (1, LANES), 1) // HEAD
        row, col = _causal_masks(bq, bk)
        dq_ref[...] = jnp.zeros_like(dq_ref)
        lse = lse_ref[...]
        lse_other = pltpu.roll(lse, HEAD, 1)
        qt_ref[...] = jnp.transpose(q_ref[...].astype(F32)).astype(BF16)
        dot_ref[...] = jnp.transpose(dy_ref[...]).astype(BF16)
        for e in range(2):
            do = jnp.where(lane == e, dy_ref[...], 0.0)
            dob_ref[e] = do.astype(BF16)
            st_ref[2 * e] = jnp.where(lane == e, lse, lse_other) * LOG2E
            st_ref[2 * e + 1] = jnp.broadcast_to(jnp.sum(do * y_ref[...], axis=1, keepdims=True), (bq, LANES))

        hss = [slice(e * LANES, (e + 1) * LANES) for e in range(2)]
        tile = lambda a: jnp.concatenate([a] * (bk // LANES), axis=1)

        def step(kb0, masked):
            kbs = [kb0 + d for d in range(nsub)]
            rows = [pl.ds(pl.multiple_of(kb * bk, bk), bk) for kb in kbs]
            pairs = [(d, e) for d in range(nsub) for e in range(2)]
            qs = [slice(d * bk if masked else 0, bq) for d in range(nsub)]
            ss = {(d, e): _dot_nt(q_ref[qs[d], hss[e]], k_ref[rows[d], hss[e]]) for d, e in pairs}
            dps = {(d, e): _dot_nt(dob_ref[e, qs[d], :], jnp.where(lane == e, v_ref[rows[d], :], 0))
                   for d, e in pairs}
            ps, dss = {}, {}
            for d, e in pairs:
                s = jnp.where(col[qs[d]] + d * bk <= row[qs[d]], ss[d, e], NEG) if masked else ss[d, e]
                p = jnp.exp2(s - tile(st_ref[2 * e, qs[d], :]))
                dss[d, e] = (p * (dps[d, e] - tile(st_ref[2 * e + 1, qs[d], :]))).astype(BF16)
                ps[d, e] = p.astype(BF16)
            for d, e in pairs:
                dvt_ref[0, kbs[d], e * HEAD:(e + 1) * HEAD, :] += _dot(
                    dot_ref[e * HEAD:(e + 1) * HEAD, qs[d]], ps[d, e])
            for d, e in pairs:
                dkt_ref[0, kbs[d], hss[e], :] += _dot(qt_ref[hss[e], qs[d]], dss[d, e])
            for e in range(2):
                if masked:
                    for d in range(nsub):
                        dq_ref[qs[d], hss[e]] += _dot(dss[d, e], k_ref[rows[d], hss[e]])
                else:
                    dq = dq_ref[:, hss[e]]
                    for d in range(nsub):
                        dq = dq + _dot(dss[d, e], k_ref[rows[d], hss[e]])
                    dq_ref[:, hss[e]] = dq

        step(qi * nsub, True)

        def loop(t, c):
            step(t * nsub, False)
            return c

        lax.fori_loop(0, qi, loop, 0)
        dq_ref[...] *= LN2

    return pl.pallas_call(
        body, name="mla_bwd", grid=(2, nq),
        in_specs=[pl.BlockSpec((bq, 256), lambda j, i: (i, j)),
                  pl.BlockSpec((T, 256), lambda j, i: (0, j)),
                  pl.BlockSpec((T, LANES), lambda j, i: (0, j)),
                  pl.BlockSpec((bq, LANES), lambda j, i: (i, j)),
                  pl.BlockSpec((bq, LANES), lambda j, i: (i, j)),
                  pl.BlockSpec((bq, LANES), lambda j, i: (i, j))],
        out_specs=[pl.BlockSpec((bq, 256), lambda j, i: (i, j)),
                   pl.BlockSpec((1, nk, 256, bk), lambda j, i: (j, 0, 0, 0)),
                   pl.BlockSpec((1, nk, LANES, bk), lambda j, i: (j, 0, 0, 0))],
        out_shape=[jax.ShapeDtypeStruct((T, 512), F32),
                   jax.ShapeDtypeStruct((2, nk, 256, bk), F32),
                   jax.ShapeDtypeStruct((2, nk, LANES, bk), F32)],
        scratch_shapes=[pltpu.VMEM((2, bq, LANES), BF16), pltpu.VMEM((4, bq, LANES), F32),
                        pltpu.VMEM((256, bq), BF16), pltpu.VMEM((LANES, bq), BF16)],
        compiler_params=_cparams(("parallel", "arbitrary")),
    )(qm, km, vm, y, lse, dy)


def _suffix_ones(n):
    r = lax.broadcasted_iota(jnp.int32, (n, n), 0)
    c = lax.broadcasted_iota(jnp.int32, (n, n), 1)
    return (r >= c).astype(BF16)


def _prefix_ones(n):
    r = lax.broadcasted_iota(jnp.int32, (n, n), 0)
    c = lax.broadcasted_iota(jnp.int32, (n, n), 1)
    return (r <= c).astype(BF16)


def _sb_specs(T, bq):
    qo, ko, vo = (_INT_OFF[n] // 256 for n in ("d_q", "d_k", "d_v"))
    return [pl.BlockSpec((bq, 256), lambda i: (i, qo)),
            pl.BlockSpec((T, 256), lambda i: (0, ko)),
            pl.BlockSpec((T, 256), lambda i: (0, vo))]


def _sb_fwd(hb):
    T = hb.shape[0]
    bq = bk = ATT_BLK
    nq = T // bq

    def body(q_ref, k_ref, v_ref, o_ref, tot_ref, cnt_ref, qm_ref, car_ref):
        qi = pl.program_id(0)
        lane = lax.broadcasted_iota(jnp.int32, (1, LANES), 1) // HEAD
        row, col = _causal_masks(bq, bk)
        strict = col < row
        u = _suffix_ones(bk)
        o_ref[...] = jnp.zeros_like(o_ref)
        car_ref[...] = jnp.zeros_like(car_ref)
        pair = lambda h: slice((h // 2) * LANES, (h // 2 + 1) * LANES)
        for h in range(4):
            qm_ref[h] = jnp.where(lane == h % 2, q_ref[:, pair(h)], 0) * 0.125

        def step(blocks):
            tile = lambda a: jnp.concatenate([a] * (bk // LANES), axis=1)
            rows = [pl.ds(pl.multiple_of(kb * bk, bk), bk) for kb, _ in blocks]
            pairs = [(b, h) for b in range(len(blocks)) for h in range(4)]
            zs = {(b, h): _dot_nt(qm_ref[h], k_ref[rows[b], pair(h)]) for b, h in pairs}
            splits = {}
            for b, h in pairs:
                z = zs[b, h]
                lk = jnp.minimum(-z, 0.0) - jnp.log(1.0 + jnp.exp(-jnp.abs(z)))
                if blocks[b][1] is not None:
                    lk = jnp.where(blocks[b][1], lk, 0.0)
                splits[b, h] = _split(lk)
            sufs = {bh: _dot(hi, u) + _dot(lo, u) for bh, (hi, lo) in splits.items()}
            car = [car_ref[h] for h in range(4)]
            aas = {}
            for b, h in pairs:
                a = jnp.exp(zs[b, h] + sufs[b, h] + tile(car[h]))
                if blocks[b][1] is not None:
                    a = jnp.where(blocks[b][1], a, 0.0)
                aas[b, h] = a.astype(BF16)
                car[h] = car[h] + jnp.broadcast_to(sufs[b, h][:, 0:1], (bq, LANES))
            acc = [o_ref[:, pair(0)], o_ref[:, pair(2)]]
            for b, h in pairs:
                acc[h // 2] = acc[h // 2] + _dot(aas[b, h], jnp.where(lane == h % 2, v_ref[rows[b], pair(h)], 0))
            o_ref[:, pair(0)], o_ref[:, pair(2)] = acc
            for h in range(4):
                car_ref[h] = car[h]

        step([(qi, strict), (jnp.maximum(qi - 1, 0), qi > 0)])

        def live():
            worst = jnp.maximum(jnp.maximum(car_ref[0], car_ref[1]), jnp.maximum(car_ref[2], car_ref[3]))
            return jnp.max(worst) >= SB_DEAD

        def cond(c):
            return jnp.logical_and(c[0] < qi, c[1])

        def loop(c):
            step([(qi - 1 - c[0], None)])
            return c[0] + 1, live()

        done, _ = lax.while_loop(cond, loop, (jnp.minimum(qi, 1), live()))
        tot_ref[:, pair(0)] = jnp.where(lane == 0, car_ref[0], car_ref[1])
        tot_ref[:, pair(2)] = jnp.where(lane == 0, car_ref[2], car_ref[3])
        cnt_ref[0, qi] = done.astype(F32)

    return pl.pallas_call(
        body, name="sb_fwd", grid=(nq,), in_specs=_sb_specs(T, bq),
        out_specs=[pl.BlockSpec((bq, 256), lambda i: (i, 0)), pl.BlockSpec((bq, 256), lambda i: (i, 0)),
                   pl.BlockSpec(memory_space=pltpu.SMEM)],
        out_shape=[jax.ShapeDtypeStruct((T, 256), F32), jax.ShapeDtypeStruct((T, 256), F32),
                   jax.ShapeDtypeStruct((1, nq), F32)],
        scratch_shapes=[pltpu.VMEM((4, bq, LANES), BF16), pltpu.VMEM((4, bq, LANES), F32)],
        compiler_params=_cparams(("arbitrary",)),
    )(hb, hb, hb)


def _sb_bwd(hb, tot, cnt, dy):
    T = hb.shape[0]
    bq = bk = ATT_BLK
    nq = T // bq

    def body(q_ref, k_ref, v_ref, tot_ref, dy_ref, cnt_ref, dq_ref, dk_ref, dv_ref, qm_ref, dob_ref, dqa_ref, rem_ref,
             cg_ref):
        qi = pl.program_id(0)

        @pl.when(qi == 0)
        def _():
            dk_ref[...] = jnp.zeros_like(dk_ref)
            dv_ref[...] = jnp.zeros_like(dv_ref)

        lane = lax.broadcasted_iota(jnp.int32, (1, LANES), 1) // HEAD
        row, col = _causal_masks(bq, bk)
        strict = col < row
        u = _prefix_ones(bk)
        pair = lambda h: slice((h // 2) * LANES, (h // 2 + 1) * LANES)
        dqa_ref[...] = jnp.zeros_like(dqa_ref)
        cg_ref[...] = jnp.zeros_like(cg_ref)
        for h in range(4):
            tot = tot_ref[:, pair(h)]
            qm_ref[h] = jnp.where(lane == h % 2, q_ref[:, pair(h)], 0) * 0.125
            dob_ref[h] = jnp.where(lane == h % 2, dy_ref[:, pair(h)], 0.0).astype(BF16)
            rem_ref[h] = jnp.where(lane == h % 2, tot, pltpu.roll(tot, HEAD, 1))

        def step(blocks):
            tile = lambda a: jnp.concatenate([a] * (bk // LANES), axis=1)
            nb = len(blocks)
            rows = [pl.ds(pl.multiple_of(kb * bk, bk), bk) for kb, _ in blocks]
            pairs = [(b, h) for b in range(nb) for h in range(4)]
            mask = lambda b, x: x if blocks[b][1] is None else jnp.where(blocks[b][1], x, 0.0)
            zs = {(b, h): _dot_nt(qm_ref[h], k_ref[rows[b], pair(h)]) for b, h in pairs}
            das = {(b, h): _dot_nt(dob_ref[h], jnp.where(lane == h % 2, v_ref[rows[b], pair(h)], 0)) for b, h in pairs}
            zls, splits = {}, {}
            for b, h in pairs:
                z = zs[b, h]
                lk = mask(b, jnp.minimum(-z, 0.0) - jnp.log(1.0 + jnp.exp(-jnp.abs(z))))
                zls[b, h] = z + lk
                splits[b, h] = _split(lk)
            pres = {bh: _dot(hi, u) + _dot(lo, u) for bh, (hi, lo) in splits.items()}
            rem = [rem_ref[h] for h in range(4)]
            aas, gs, gsplits = {}, {}, {}
            for b, h in pairs:
                a = mask(b, jnp.exp(zls[b, h] + (tile(rem[h]) - pres[b, h])))
                gs[b, h] = a * das[b, h]
                aas[b, h] = a.astype(BF16)
                gsplits[b, h] = _split(gs[b, h])
                rem[h] = rem[h] - jnp.broadcast_to(pres[b, h][:, bk - 1:bk], (bq, LANES))
            for b in range(nb):
                for p in (0, 2):
                    dv_ref[rows[b], pair(p)] += _dot_tn(aas[b, p], dob_ref[p]) + _dot_tn(aas[b, p + 1], dob_ref[p + 1])
            gpres = {bh: _dot(hi, u) + _dot(lo, u) for bh, (hi, lo) in gsplits.items()}
            cg = [cg_ref[h] for h in range(4)]
            dzs = {}
            for b, h in pairs:
                dz = mask(b, gs[b, h] - jnp.exp(zls[b, h]) * (tile(cg[h]) + gpres[b, h]))
                dzs[b, h] = dz.astype(BF16)
                cg[h] = cg[h] + jnp.broadcast_to(gpres[b, h][:, bk - 1:bk], (bq, LANES))
            for b in range(nb):
                for p in (0, 2):
                    dk_ref[rows[b], pair(p)] += _dot_tn(dzs[b, p], qm_ref[p]) + _dot_tn(dzs[b, p + 1], qm_ref[p + 1])
            for h in range(4):
                dq = dqa_ref[h]
                for b in range(nb):
                    dq = dq + _dot(dzs[b, h], k_ref[rows[b], pair(h)])
                dqa_ref[h] = dq
                rem_ref[h] = rem[h]
                cg_ref[h] = cg[h]

        def loop(kb, c):
            step([(kb, None)])
            return c

        start = qi - jnp.clip(cnt_ref[0, qi].astype(jnp.int32), 0, qi)
        lax.fori_loop(start, qi - 1, loop, 0)
        step([(jnp.maximum(qi - 1, 0), qi > 0), (qi, strict)])
        for p in (0, 2):
            dq_ref[:, pair(p)] = (jnp.where(lane == 0, dqa_ref[p], dqa_ref[p + 1]) * 0.125).astype(BF16)

    return pl.pallas_call(
        body, name="sb_bwd", grid=(nq,),
        in_specs=_sb_specs(T, bq) + [pl.BlockSpec((bq, 256), lambda i: (i, 0)),
                                     pl.BlockSpec((bq, 256), lambda i: (i, 0)),
                                     pl.BlockSpec(memory_space=pltpu.SMEM)],
        out_specs=[pl.BlockSpec((bq, 256), lambda i: (i, 0)),
                   pl.BlockSpec((T, 256), lambda i: (0, 0)),
                   pl.BlockSpec((T, 256), lambda i: (0, 0))],
        out_shape=[jax.ShapeDtypeStruct((T, 256), BF16)] + [jax.ShapeDtypeStruct((T, 256), F32)] * 2,
        scratch_shapes=[pltpu.VMEM((4, bq, LANES), BF16), pltpu.VMEM((4, bq, LANES), BF16),
                        pltpu.VMEM((4, bq, LANES), F32), pltpu.VMEM((4, bq, LANES), F32),
                        pltpu.VMEM((4, bq, LANES), F32)],
        compiler_params=_cparams(("arbitrary",)),
    )(hb, hb, hb, tot, dy, cnt)


EP_TM = 512


def _ep_in_specs(tm, rev):
    idx = (lambda i: rev - i) if rev is not None else (lambda i: i)
    bo = (_INT_OFF["b_b"] - N_HB) // 256
    halo = lambda i: jnp.maximum(idx(i) * (tm // 8) - 1, 0)
    return [pl.BlockSpec((tm, 256), lambda i: (idx(i), 0)),
            pl.BlockSpec((tm, 256), lambda i: (idx(i), 0)),
            pl.BlockSpec((tm, 256), lambda i: (idx(i), 0)),
            pl.BlockSpec((tm, D_MODEL), lambda i: (idx(i), 0)),
            pl.BlockSpec((tm, 256), lambda i: (idx(i), bo)),
            pl.BlockSpec((tm, 256), lambda i: (idx(i), bo + 1)),
            pl.BlockSpec((tm, 256), lambda i: (idx(i), bo + 2)),
            pl.BlockSpec((8, 256), lambda i: (halo(i), bo + 1)),
            pl.BlockSpec((8, 256), lambda i: (halo(i), bo + 2)),
            pl.BlockSpec((3, 256), lambda i: (0, 0)),
            pl.BlockSpec((1, 256), lambda i: (0, 0)),
            pl.BlockSpec((1, D_MODEL), lambda i: (0, 0)),
            pl.BlockSpec((D_MODEL, D_MODEL), lambda i: (0, 0)),
            pl.BlockSpec((1, D_MODEL), lambda i: (0, 0))]


def _ep_mix(first, ya_ref, yc_ref, yd_ref, gate_ref, bb_ref, bc_ref, bx_ref, hc_ref, hx_ref, cw_ref, cb_ref, gg_ref):
    tm = ya_ref.shape[0]
    u = bc_ref[...] * bx_ref[...]
    halo = jnp.where(first, 0.0, hc_ref[...] * hx_ref[...])
    row = lax.broadcasted_iota(jnp.int32, (tm, 1), 0)
    u1 = jnp.where(row == 0, halo[7:8, :], pltpu.roll(u, 1, 0))
    u2 = jnp.where(row == 0, halo[6:7, :], jnp.where(row == 1, halo[7:8, :], pltpu.roll(u, 2, 0)))
    cw = cw_ref[...]
    conv = cw[0:1, :] * u2 + cw[1:2, :] * u1 + cw[2:3, :] * u + cb_ref[...]
    bb = bb_ref[...]
    ys = [ya_ref[...], bb * conv, yc_ref[...], yd_ref[...]]
    rs = [_rms(y) for y in ys]
    gg = gg_ref[...]
    yhat = jnp.concatenate([y * r for y, r in zip(ys, rs)], axis=1)
    gate = gate_ref[...]
    sig = 1.0 / (1.0 + jnp.exp(-gate))
    return u, u1, u2, conv, bb, rs, yhat, yhat * gg, gate, sig


def _epilogue_fwd(x, ya, yc, yd, hf, conv_w, conv_b, g_grp, w_out, g_post, tgt=None):
    T = x.shape[0]
    tm = EP_TM
    row_spec = pl.BlockSpec((tm, D_MODEL), lambda i: (i, 0))

    def layer_out(refs):
        (x_ref, ya_ref, yc_ref, yd_ref, gate_ref, bb_ref, bc_ref, bx_ref, hc_ref, hx_ref, cw_ref, cb_ref,
         gg_ref, wo_ref, gp_ref) = refs
        (_, _, _, _, _, _, _, yn, gate, sig) = _ep_mix(
            pl.program_id(0) == 0, ya_ref, yc_ref, yd_ref, gate_ref, bb_ref, bc_ref, bx_ref, hc_ref, hx_ref,
            cw_ref, cb_ref, gg_ref)
        z = _dot((yn * (gate * sig)).astype(BF16), wo_ref[...])
        return x_ref[...] + z * _rms(z) * gp_ref[...]

    args = (x, ya, yc, yd, hf, hf, hf, hf, hf, hf, conv_w, conv_b, g_grp, w_out, g_post)
    in_specs = [row_spec] + _ep_in_specs(tm, None)
    if tgt is None:
        def body(*refs):
            refs[-1][...] = layer_out(refs[:-1])

        return pl.pallas_call(
            body, name="epilogue_fwd", grid=(T // tm,), in_specs=in_specs, out_specs=row_spec,
            out_shape=jax.ShapeDtypeStruct((T, D_MODEL), F32), compiler_params=_cparams(("parallel",)),
        )(*args)

    def body_loss(*refs):
        t_ref, dy_ref, l_ref = refs[-3:]

        @pl.when(pl.program_id(0) == 0)
        def _():
            l_ref[...] = jnp.zeros_like(l_ref)

        d = layer_out(refs[:-3]) - t_ref[...]
        dy_ref[...] = d * (1.0 / D_MODEL)
        part = jnp.sum(jnp.sum(d * d, axis=1, keepdims=True), axis=0, keepdims=True)
        l_ref[...] += part * (0.5 / D_MODEL)

    return pl.pallas_call(
        body_loss, name="epilogue_fwd_loss", grid=(T // tm,), in_specs=in_specs + [row_spec],
        out_specs=[row_spec, pl.BlockSpec((8, LANES), lambda i: (0, 0))],
        out_shape=[jax.ShapeDtypeStruct((T, D_MODEL), F32), jax.ShapeDtypeStruct((8, LANES), F32)],
        compiler_params=_cparams(("arbitrary",)),
    )(*args, tgt)


def _epilogue_bwd(dxn, ya, yc, yd, hf, conv_w, conv_b, g_grp, w_out, g_post):
    T = dxn.shape[0]
    tm = EP_TM
    nt = T // tm
    ridx = lambda i: (nt - 1 - i, 0)

    def body(dx_ref, ya_ref, yc_ref, yd_ref, gate_ref, bb_ref, bc_ref, bx_ref, hc_ref, hx_ref, cw_ref, cb_ref,
             gg_ref, wo_ref, gp_ref,
             dya_ref, dyc_ref, dyd_ref, dhf_ref, dwo_ref, dgp_ref, dgg_ref, dcw_ref, dcb_ref, carry_ref):
        i = pl.program_id(0)

        @pl.when(i == 0)
        def _():
            for r in (dwo_ref, dgp_ref, dgg_ref, dcw_ref, dcb_ref, carry_ref):
                r[...] = jnp.zeros_like(r)

        (u, u1, u2, conv, bb, rs, yhat, yn, gate, sig) = _ep_mix(
            i == nt - 1, ya_ref, yc_ref, yd_ref, gate_ref, bb_ref, bc_ref, bx_ref, hc_ref, hx_ref,
            cw_ref, cb_ref, gg_ref)
        silu = gate * sig
        ymix = (yn * silu).astype(BF16)
        z = _dot(ymix, wo_ref[...])
        rz = _rms(z)
        dz, dgrow = _rms_bwd(dx_ref[...], z * rz, rz, gp_ref[...])
        dgp_ref[...] += _colsum(dgrow)
        dzb = dz.astype(BF16)
        dwo_ref[...] += _dot_tn(ymix, dzb)
        dymix = _dot_nt(dzb, wo_ref[...])
        dhf_ref[:, 0:D_MODEL] = (dymix * yn * (sig * (1.0 + gate * (1.0 - sig)))).astype(BF16)
        dyn = dymix * silu
        dgg_ref[...] += _colsum(dyn * yhat)
        gg = gg_ref[...]
        dys = []
        for gi in range(4):
            sl = slice(gi * GROUP, (gi + 1) * GROUP)
            dyh = dyn[:, sl] * gg[:, sl]
            yh = yhat[:, sl]
            dys.append(rs[gi] * (dyh - yh * jnp.mean(dyh * yh, axis=-1, keepdims=True)))
        dya_ref[...] = dys[0]
        dyc_ref[...] = dys[2]
        dyd_ref[...] = dys[3]
        dyb = dys[1]
        dhf_ref[:, D_MODEL:D_MODEL + 256] = (dyb * conv).astype(BF16)
        dconv = dyb * bb
        dcb_ref[...] += _colsum(dconv)
        dcw_ref[0:1, :] += _colsum(dconv * u2)
        dcw_ref[1:2, :] += _colsum(dconv * u1)
        dcw_ref[2:3, :] += _colsum(dconv * u)
        carry = carry_ref[...]
        row = lax.broadcasted_iota(jnp.int32, (tm, 1), 0)
        d1 = jnp.where(row == tm - 1, carry[0:1, :], pltpu.roll(dconv, tm - 1, 0))
        d2 = jnp.where(row == tm - 2, carry[0:1, :],
                       jnp.where(row == tm - 1, carry[1:2, :], pltpu.roll(dconv, tm - 2, 0)))
        cw = cw_ref[...]
        du = cw[2:3, :] * dconv + cw[1:2, :] * d1 + cw[0:1, :] * d2
        dhf_ref[:, D_MODEL + 256:D_MODEL + 512] = (du * bx_ref[...]).astype(BF16)
        dhf_ref[:, D_MODEL + 512:D_MODEL + 768] = (du * bc_ref[...]).astype(BF16)
        carry_ref[...] = dconv[0:8, :]

    in_specs = [pl.BlockSpec((tm, D_MODEL), ridx)] + _ep_in_specs(tm, nt - 1)
    return pl.pallas_call(
        body, name="epilogue_bwd", grid=(nt,), in_specs=in_specs,
        out_specs=[pl.BlockSpec((tm, 256), ridx), pl.BlockSpec((tm, 256), ridx), pl.BlockSpec((tm, 256), ridx),
                   pl.BlockSpec((tm, D_MODEL + 768), ridx),
                   pl.BlockSpec((D_MODEL, D_MODEL), lambda i: (0, 0)),
                   pl.BlockSpec((1, D_MODEL), lambda i: (0, 0)),
                   pl.BlockSpec((1, D_MODEL), lambda i: (0, 0)),
                   pl.BlockSpec((8, 256), lambda i: (0, 0)),
                   pl.BlockSpec((1, 256), lambda i: (0, 0))],
        out_shape=[jax.ShapeDtypeStruct((T, 256), F32)] * 3
                  + [jax.ShapeDtypeStruct((T, D_MODEL + 768), BF16),
                     jax.ShapeDtypeStruct((D_MODEL, D_MODEL), F32),
                     jax.ShapeDtypeStruct((1, D_MODEL), F32),
                     jax.ShapeDtypeStruct((1, D_MODEL), F32),
                     jax.ShapeDtypeStruct((8, 256), F32),
                     jax.ShapeDtypeStruct((1, 256), F32)],
        scratch_shapes=[pltpu.VMEM((8, 256), F32)],
        compiler_params=_cparams(("arbitrary",)),
    )(dxn, ya, yc, yd, hf, hf, hf, hf, hf, hf, conv_w, conv_b, g_grp, w_out, g_post)


def _place():
    return lax.axis_index("x"), lax.axis_index("y"), lax.axis_index("c")


def _other_chips(x, y):
    return [(1 - x, y), (x, 1 - y), (1 - x, 1 - y)]


HBM = pl.BlockSpec(memory_space=pl.ANY)


def _gather_plan(ins, outs, sems):
    n = len(ins)
    ici_send, ici_recv, d2d_send, d2d_recv, local_sems = sems
    x, y, c = _place()
    me = 2 * x + y
    chips = _other_chips(x, y)

    def ici(a, j, chip_from):
        px, py = chips[j]
        return pltpu.make_async_remote_copy(
            src_ref=ins[a].at[c], dst_ref=outs[a].at[chip_from, c], send_sem=ici_send.at[3 * a + j],
            recv_sem=ici_recv.at[3 * a + j], device_id=(px, py, c), device_id_type=MESH)

    def d2d(a, j, part):
        px, py = chips[j]
        blk = outs[a].at[2 * px + py, part]
        return pltpu.make_async_remote_copy(
            src_ref=blk, dst_ref=blk, send_sem=d2d_send.at[3 * a + j], recv_sem=d2d_recv.at[3 * a + j],
            device_id=(x, y, 1 - c), device_id_type=MESH)

    def local(a):
        return pltpu.make_async_copy(ins[a], outs[a].at[me], local_sems.at[a])

    hops = [(j, a) for j in range(3) for a in range(n)]

    def start():
        for a in range(n):
            local(a).start()
        for j, a in hops:
            ici(a, j, me).start()

    def finish():
        for j, a in hops:
            ici(a, j, 2 * chips[j][0] + chips[j][1]).wait_recv()
            d2d(a, j, c).start()
        for j, a in hops:
            d2d(a, j, 1 - c).wait_recv()
        for j, a in hops:
            ici(a, j, me).wait_send()
            d2d(a, j, c).wait_send()
        for a in range(n):
            local(a).wait()

    return start, finish


def _gather_sems(n):
    return [pltpu.SemaphoreType.DMA((3 * n,))] * 4 + [pltpu.SemaphoreType.DMA((n,))]


def _exchange_chips(parts, small):
    n = len(parts)

    def body(*refs):
        ins, sm_ref = refs[:n], refs[n]
        outs, osm_ref = refs[n + 1:2 * n + 1], refs[2 * n + 1]
        send_sems, recv_sems, ssend_sems, srecv_sems, local_sems = refs[2 * n + 2:]
        x, y, c = _place()
        me = 2 * x + y
        dev = 4 * x + 2 * y + c
        local = [pltpu.make_async_copy(ins[a].at[me], outs[a].at[me], local_sems.at[a]) for a in range(n)]
        local.append(pltpu.make_async_copy(sm_ref, osm_ref.at[dev], local_sems.at[n]))
        for cp in local:
            cp.start()
        sends = []
        for j, (px, py) in enumerate(_other_chips(x, y)):
            for a in range(n):
                cp = pltpu.make_async_remote_copy(
                    src_ref=ins[a].at[2 * px + py], dst_ref=outs[a].at[me], send_sem=send_sems.at[3 * a + j],
                    recv_sem=recv_sems.at[3 * a + j], device_id=(px, py, c), device_id_type=MESH)
                cp.start()
                sends.append(cp)
        flips = [(fx, fy, fc) for fx in (0, 1) for fy in (0, 1) for fc in (0, 1)][1:]
        for j, (fx, fy, fc) in enumerate(flips):
            cp = pltpu.make_async_remote_copy(
                src_ref=sm_ref, dst_ref=osm_ref.at[dev], send_sem=ssend_sems.at[j], recv_sem=srecv_sems.at[j],
                device_id=(x ^ fx, y ^ fy, c ^ fc), device_id_type=MESH)
            cp.start()
            sends.append(cp)
        for j, (px, py) in enumerate(_other_chips(x, y)):
            for a in range(n):
                pltpu.make_async_remote_copy(
                    src_ref=ins[a].at[me], dst_ref=outs[a].at[2 * px + py], send_sem=send_sems.at[3 * a + j],
                    recv_sem=recv_sems.at[3 * a + j], device_id=(px, py, c), device_id_type=MESH).wait_recv()
        for j, (fx, fy, fc) in enumerate(flips):
            src = 4 * (x ^ fx) + 2 * (y ^ fy) + (c ^ fc)
            pltpu.make_async_remote_copy(
                src_ref=sm_ref, dst_ref=osm_ref.at[src], send_sem=ssend_sems.at[j], recv_sem=srecv_sems.at[j],
                device_id=(x ^ fx, y ^ fy, c ^ fc), device_id_type=MESH).wait_recv()
        for cp in sends:
            cp.wait_send()
        for cp in local:
            cp.wait()

    return pl.pallas_call(
        body, name="exchange_chips",
        in_specs=[HBM] * (n + 1), out_specs=[HBM] * (n + 1),
        out_shape=[jax.ShapeDtypeStruct(p.shape, p.dtype) for p in parts]
                  + [jax.ShapeDtypeStruct((8,) + small.shape, small.dtype)],
        scratch_shapes=[pltpu.SemaphoreType.DMA((3 * n,)), pltpu.SemaphoreType.DMA((3 * n,)),
                        pltpu.SemaphoreType.DMA((7,)), pltpu.SemaphoreType.DMA((7,)),
                        pltpu.SemaphoreType.DMA((n + 1,))],
    )(*parts, small)


def _swap_cores(parts, name):
    n = len(parts)

    def body(*refs):
        ins, outs, send_sems, recv_sems = refs[:n], refs[n:2 * n], refs[2 * n], refs[2 * n + 1]
        x, y, c = _place()
        copies = [pltpu.make_async_remote_copy(
            src_ref=ins[a], dst_ref=outs[a], send_sem=send_sems.at[a], recv_sem=recv_sems.at[a],
            device_id=(x, y, 1 - c), device_id_type=MESH) for a in range(n)]
        for cp in copies:
            cp.start()
        for cp in copies:
            cp.wait()

    return pl.pallas_call(
        body, name=name, in_specs=[HBM] * n, out_specs=[HBM] * n,
        out_shape=[jax.ShapeDtypeStruct(p.shape, p.dtype) for p in parts],
        scratch_shapes=[pltpu.SemaphoreType.DMA((n,)), pltpu.SemaphoreType.DMA((n,))],
    )(*parts)


def _tile(rows, cols):
    for cand in (256, 128, 64):
        if rows % cand == 0:
            return cand, cols
    if rows > 64 and cols % 256 == 0:
        return rows, 256
    return rows, cols


def _swap_layer_chunks(per_layer):
    n = len(per_layer[0])

    def body(*refs):
        ins, outs, send_sems, recv_sems = (refs[:n], refs[n:2 * n]), refs[2 * n:3 * n], refs[3 * n], refs[3 * n + 1]
        x, y, c = _place()
        for l in range(2):
            @pl.when(c == 1 - l)
            def _():
                copies = [pltpu.make_async_remote_copy(
                    src_ref=ins[l][a], dst_ref=outs[a], send_sem=send_sems.at[a], recv_sem=recv_sems.at[a],
                    device_id=(x, y, 1 - c), device_id_type=MESH) for a in range(n)]
                for cp in copies:
                    cp.start()
                for cp in copies:
                    cp.wait()

    return pl.pallas_call(
        body, name="swap_layer_chunks", in_specs=[HBM] * (2 * n), out_specs=[HBM] * n,
        out_shape=[jax.ShapeDtypeStruct(p.shape, p.dtype) for p in per_layer[0]],
        scratch_shapes=[pltpu.SemaphoreType.DMA((n,)), pltpu.SemaphoreType.DMA((n,))],
    )(*per_layer[0], *per_layer[1])


def _add(a0, a1, b, name):
    L, R, C = b.shape
    tr, tc = _tile(R, C)

    def body(a0_ref, a1_ref, b_ref, o_ref):
        mine = jnp.where(lax.axis_index("c") == 0, a0_ref[...], a1_ref[...])
        o_ref[...] = (mine + b_ref[...]).astype(BF16)

    spec = pl.BlockSpec((1, tr, tc), lambda l, i, j: (l, i, j))
    return pl.pallas_call(
        body, name=name, grid=(L, R // tr, C // tc), in_specs=[spec] * 3, out_specs=spec,
        out_shape=jax.ShapeDtypeStruct((L, R, C), BF16),
        compiler_params=_cparams(("parallel", "parallel", "parallel")),
    )(a0, a1, b)


def _sum_leading(buf, name):
    n, R, C = buf.shape
    tr, tc = _tile(R, C)

    def body(b_ref, o_ref):
        acc = b_ref[0].astype(F32)
        for k in range(1, n):
            acc = acc + b_ref[k].astype(F32)
        o_ref[...] = acc

    return pl.pallas_call(
        body, name=name, grid=(R // tr, C // tc),
        in_specs=[pl.BlockSpec((n, tr, tc), lambda i, j: (0, i, j))],
        out_specs=pl.BlockSpec((tr, tc), lambda i, j: (i, j)),
        out_shape=jax.ShapeDtypeStruct((R, C), F32),
        compiler_params=_cparams(("parallel", "parallel")),
    )(buf)


def _adam_update(w, g, m, v):
    c1 = 1.0 / (1.0 - ADAM_B1 ** ADAM_STEP)
    c2 = 1.0 / (1.0 - ADAM_B2 ** ADAM_STEP)
    mn = ADAM_B1 * m + (1.0 - ADAM_B1) * g
    vn = ADAM_B2 * v + (1.0 - ADAM_B2) * (g * g)
    return -ADAM_LR * ((mn * c1) / (jnp.sqrt(vn * c2) + ADAM_EPS) + ADAM_WD * w), mn, vn


def _adamw_layers(w, m, v, g_mine, g_other, name):
    _, R, C = w.shape
    tr, tc = _tile(R, C)

    def body(w_ref, m_ref, v_ref, gm_ref, go_ref, g_ref, d_ref, mo_ref, vo_ref):
        g = jnp.where(pl.program_id(0) == lax.axis_index("c"), gm_ref[...], go_ref[...])
        g_ref[0] = g
        d_ref[0], mo_ref[0], vo_ref[0] = _adam_update(w_ref[0], g, m_ref[0], v_ref[0])

    spec3 = pl.BlockSpec((1, tr, tc), lambda l, i, j: (l, i, j))
    spec2 = pl.BlockSpec((tr, tc), lambda l, i, j: (i, j))
    return pl.pallas_call(
        body, name=name, grid=(2, R // tr, C // tc),
        in_specs=[spec3] * 3 + [spec2] * 2, out_specs=[spec3] * 4,
        out_shape=[jax.ShapeDtypeStruct(w.shape, F32)] * 4,
        compiler_params=_cparams(("parallel", "parallel", "parallel")),
    )(w, m, v, g_mine, g_other)


PACK_C = 1024
_BIG = ("w_in", "w_out", "mla_w_uq", "mla_w_ukv", "conv_w")
_SMALL = ("norm_pre", "group_norm", "norm_post", "conv_b", "mla_q_norm", "mla_kv_norm", "attn_sinks")
_SMALL_W = {"norm_pre": 1024, "group_norm": 1024, "norm_post": 1024, "conv_b": 256, "mla_q_norm": 256,
            "mla_kv_norm": 128, "attn_sinks": 4}


_LOSS_AT = divmod(DEPTH * sum(_SMALL_W.values()), PACK_C)


def _pack_small(d, loss):
    flat = jnp.concatenate([d[n].reshape(-1) for n in _SMALL] + [loss.reshape(1)])
    return jnp.pad(flat, (0, 8 * PACK_C - flat.shape[0])).reshape(8, PACK_C)


def _adamw_small(w, m, v, got):
    ns = len(_SMALL)

    def body(*refs):
        got_ref = refs[3 * ns]
        outs = refs[3 * ns + 1:]
        gsum = got_ref[0]
        for d in range(1, 8):
            gsum = gsum + got_ref[d]
        outs[4 * ns][...] = gsum[_LOSS_AT[0]:_LOSS_AT[0] + 1, _LOSS_AT[1]:_LOSS_AT[1] + 1]
        off = 0
        for i, name in enumerate(_SMALL):
            wd = _SMALL_W[name]
            rows = []
            for l in range(DEPTH):
                r, c0 = divmod(off + l * wd, PACK_C)
                rows.append(gsum[r:r + 1, c0:c0 + wd])
            off += DEPTH * wd
            g = jnp.concatenate(rows, axis=0)
            delta, mn, vn = _adam_update(refs[i][...], g, refs[ns + i][...], refs[2 * ns + i][...])
            outs[i][...] = g
            outs[ns + i][...] = delta
            outs[2 * ns + i][...] = mn
            outs[3 * ns + i][...] = vn

    shapes = [jax.ShapeDtypeStruct(w[n].shape, F32) for n in _SMALL]
    res = pl.pallas_call(body, name="adamw_small", out_shape=shapes * 4 + [jax.ShapeDtypeStruct((1, 1), F32)])(
        *[w[n] for n in _SMALL], *[m[n] for n in _SMALL], *[v[n] for n in _SMALL], got)
    return [dict(zip(_SMALL, res[k * ns:(k + 1) * ns])) for k in range(4)], res[4 * ns]


def _w_in_internal(slabs):
    S, R, D = slabs.shape
    assert S * R == D_IN

    def body(w_ref, o_ref):
        for n, wd in _REAL:
            o, oi = _REAL_OFF[n][0], _INT_OFF[n]
            r = o
            while r < o + wd:
                end = min(o + wd, (r // R + 1) * R)
                o_ref[oi + r - o:oi + end - o, :] = w_ref[r // R, r % R:r % R + end - r, :]
                r = end
            if _INT_W[n] != wd:
                o_ref[oi + wd:oi + _INT_W[n], :] = jnp.zeros((_INT_W[n] - wd, D), slabs.dtype)

    return pl.pallas_call(
        body, name="w_in_internal", out_shape=jax.ShapeDtypeStruct((N_INT, D), slabs.dtype),
        compiler_params=pltpu.CompilerParams(vmem_limit_bytes=VMEM_LIMIT),
    )(slabs)


def _uq_internal(w):
    return jnp.pad(w.reshape(256, 4, 96), ((0, 0), (0, 0), (0, 32))).reshape(256, 512)


def _uq_real(dw):
    return dw.reshape(256, 4, 128)[:, :, :96].reshape(256, 384)


def _ukv_internal(w):
    w4 = w.reshape(128, 4, 128)
    k = jnp.pad(w4[:, :, :64], ((0, 0), (0, 0), (0, 64))).reshape(128, 512)
    return jnp.concatenate([k, w4[:, :, 64:].reshape(128, 256)], axis=1)


def _ukv_real(dw):
    k = dw[:, :512].reshape(128, 4, 128)[:, :, :64]
    v = dw[:, 512:].reshape(128, 4, 64)
    return jnp.concatenate([k, v], axis=2).reshape(128, 512)


def _layer_fwd(x, rope, p, tgt=None, fetch=()):
    xn, hb, hf, *fetched = _inproj_fwd(x, p["norm_pre"], p["w_in"], fetch)
    ya = _swa_fwd(hb, p["attn_sinks"])
    qm, km, vm, vt = _mla_prep_fwd(hf, rope, p["mla_q_norm"], p["mla_kv_norm"], p["mla_w_uq"], p["mla_w_ukv"])
    yc, lse = _mla_fwd(qm, km, vt)
    yd, tot, cnt = _sb_fwd(hb)
    x_next = _epilogue_fwd(x, ya, yc, yd, hf, p["conv_w"], p["conv_b"], p["group_norm"], p["w_out"], p["norm_post"],
                           tgt)
    saved = dict(x=x, xn=xn, hb=hb, hf=hf, ya=ya, yc=yc, yd=yd, tot=tot, cnt=cnt, qm=qm, km=km, vm=vm, lse=lse)
    return x_next, saved, fetched


def _layer_bwd(dx_next, rope, p, s):
    (dya, dyc, dyd, dhf, dw_out, dg_post, dg_grp, dconv_w, dconv_b) = _epilogue_bwd(
        dx_next, s["ya"], s["yc"], s["yd"], s["hf"], p["conv_w"], p["conv_b"], p["group_norm"], p["w_out"],
        p["norm_post"])
    dq_d, dk_d, dv_d = _sb_bwd(s["hb"], s["tot"], s["cnt"], dyd)
    dqm, dkt, dvt = _mla_bwd(s["qm"], s["km"], s["vm"], s["yc"], s["lse"], dyc)
    dc, dw_uq, dw_ukv, dg_q, dg_kv = _mla_prep_bwd(
        s["hf"], rope, p["mla_q_norm"], p["mla_kv_norm"], p["mla_w_uq"], p["mla_w_ukv"], dqm, dkt, dvt)
    dq_a, dk_a, dv_a, dsinks = _swa_bwd(s["hb"], p["attn_sinks"], dya)
    dx, dh, dg_pre = _inproj_bwd_dx(s["x"], p["norm_pre"], p["w_in"], dx_next,
                                    [dq_a, dk_a, dv_a, dq_d, dk_d, dv_d, dhf, dc])
    grads = dict(norm_pre=dg_pre[0], w_in_t=_inproj_bwd_dw(s["xn"], dh), attn_sinks=dsinks[0, :4], conv_w=dconv_w[:3],
                 conv_b=dconv_b[0], mla_q_norm=dg_q[0], mla_w_uq=_uq_real(dw_uq), mla_kv_norm=dg_kv[0],
                 mla_w_ukv=_ukv_real(dw_ukv), group_norm=dg_grp[0], w_out=dw_out, norm_post=dg_post[0])
    return dx, grads


_WEIGHTS = ["norm_pre", "w_in", "attn_sinks", "conv_w", "conv_b", "mla_q_norm", "mla_w_uq", "mla_kv_norm",
            "mla_w_ukv", "group_norm", "w_out", "norm_post"]


def kernel(x, positions, norm_pre, w_in, attn_sinks, conv_w, conv_b, mla_q_norm, mla_w_uq, mla_kv_norm, mla_w_ukv, group_norm, w_out, norm_post, loss_target, m_norm_pre, m_w_in, m_attn_sinks, m_conv_w, m_conv_b, m_mla_q_norm, m_mla_w_uq, m_mla_kv_norm, m_mla_w_ukv, m_group_norm, m_w_out, m_norm_post, v_norm_pre, v_w_in, v_attn_sinks, v_conv_w, v_conv_b, v_mla_q_norm, v_mla_w_uq, v_mla_kv_norm, v_mla_w_ukv, v_group_norm, v_w_out, v_norm_post):
    w = dict(norm_pre=norm_pre, w_in=w_in, attn_sinks=attn_sinks, conv_w=conv_w, conv_b=conv_b,
             mla_q_norm=mla_q_norm, mla_w_uq=mla_w_uq, mla_kv_norm=mla_kv_norm, mla_w_ukv=mla_w_ukv,
             group_norm=group_norm, w_out=w_out, norm_post=norm_post)
    m = dict(norm_pre=m_norm_pre, w_in=m_w_in, attn_sinks=m_attn_sinks, conv_w=m_conv_w, conv_b=m_conv_b,
             mla_q_norm=m_mla_q_norm, mla_w_uq=m_mla_w_uq, mla_kv_norm=m_mla_kv_norm, mla_w_ukv=m_mla_w_ukv,
             group_norm=m_group_norm, w_out=m_w_out, norm_post=m_norm_post)
    v = dict(norm_pre=v_norm_pre, w_in=v_w_in, attn_sinks=v_attn_sinks, conv_w=v_conv_w, conv_b=v_conv_b,
             mla_q_norm=v_mla_q_norm, mla_w_uq=v_mla_w_uq, mla_kv_norm=v_mla_kv_norm, mla_w_ukv=v_mla_w_ukv,
             group_norm=v_group_norm, w_out=v_w_out, norm_post=v_norm_post)
    T = x.shape[1]
    xs = x[0]
    tgt = loss_target[0]

    def shard_parts(l):
        halves = lambda a: a.reshape((2, a.shape[0] // 2) + a.shape[1:])
        return [halves(jnp.swapaxes(w["w_in"][l], 0, 1).astype(BF16))] + [
            halves(w[n][l].astype(BF16)) for n in _BIG[1:4]] + [jnp.stack([w["conv_w"][l]] * 2)]

    def layer_params(l, got):
        whole = lambda a: a.reshape((4, 2 * a.shape[2]) + a.shape[3:])
        by_cols = lambda a: jnp.transpose(a, (1, 0, 2)).reshape(a.shape[1], 4 * a.shape[2])
        return dict(
            norm_pre=norm_pre[l:l + 1], w_in=_w_in_internal(got[0].reshape((8,) + got[0].shape[2:])),
            attn_sinks=attn_sinks[l], conv_w=by_cols(got[4][:, 0]), conv_b=conv_b[l:l + 1],
            mla_q_norm=mla_q_norm[l:l + 1], mla_w_uq=_uq_internal(by_cols(whole(got[2]))),
            mla_kv_norm=mla_kv_norm[l:l + 1], mla_w_ukv=_ukv_internal(by_cols(whole(got[3]))),
            group_norm=group_norm[l:l + 1], w_out=whole(got[1]).reshape(D_MODEL, D_MODEL),
            norm_post=norm_post[l:l + 1])

    layers, saved = [], []
    rope, *got = _rope_tables(positions[0].reshape(T, 1), shard_parts(0))
    h = xs
    for l in range(DEPTH):
        last = l == DEPTH - 1
        layers.append(layer_params(l, got))
        h, s, got = _layer_fwd(h, rope, layers[l], tgt if last else None, () if last else shard_parts(l + 1))
        saved.append(s)
    dy, loss_part = h

    grads = [None] * DEPTH
    for l in reversed(range(DEPTH)):
        dy, grads[l] = _layer_bwd(dy, rope, layers[l], saved[l])

    turned = ("w_in", "mla_w_uq")
    turn = lambda n, a: jnp.swapaxes(a, -1, -2) if n in turned else a

    def chunks(n, a):
        if n in ("w_out", "w_in"):
            return a.reshape(4, a.shape[0] // 4, a.shape[1])
        if n in turned:
            return a.T.reshape(4, a.shape[1] // 4, a.shape[0])
        return jnp.transpose(a.reshape(a.shape[0], 4, a.shape[1] // 4), (1, 0, 2))

    grad = lambda l, n: grads[l]["w_in_t" if n == "w_in" else n]
    per_layer = [[chunks(n, grad(l, n)) for n in _BIG] for l in range(DEPTH)]
    from_sibling = _swap_layer_chunks(per_layer)
    summed = [_add(a0, a1, b, "add_cores_" + n) for n, a0, a1, b in zip(_BIG, *per_layer, from_sibling)]
    small = _pack_small({n: jnp.stack([grads[l][n] for l in range(DEPTH)]) for n in _SMALL}, loss_part[0, 0])
    *got, got_small = _exchange_chips(summed, small)
    done = [_sum_leading(b, "sum_chips_" + n) for n, b in zip(_BIG, got)]
    done_other = _swap_cores(done, "swap_layer_shards")

    outs, loss = _adamw_small(w, m, v, got_small)
    for n, gm, go in zip(_BIG, done, done_other):
        for d, a in zip(outs, _adamw_layers(turn(n, w[n]), turn(n, m[n]), turn(n, v[n]), gm, go, "adamw_" + n)):
            d[n] = turn(n, a)
    return (loss[0, 0], dy[None], *[outs[0][n] for n in _WEIGHTS], *[outs[1][n] for n in _WEIGHTS],
            *[outs[2][n] for n in _WEIGHTS], *[outs[3][n] for n in _WEIGHTS])
```

```python
import math

import jax
import jax.numpy as jnp
from jax import lax
from jax.experimental import pallas as pl
from jax.experimental.pallas import tpu as pltpu

F32 = jnp.float32
BF16 = jnp.bfloat16
MESH = pl.DeviceIdType.MESH

D_MODEL = 1024
DEPTH = 2
EPS = 1e-6
BLOCK = 128
HEAD = 64
LANES = 128
GROUP = 256
LOG2E = 1.4426950408889634
LN2 = 0.6931471805599453
MLA_QSCALE = 96 ** -0.5 * LOG2E
ROPE_HALF = 16
ROPE_THETA = 10000.0
SWA_SUB = 2
ATT_BLK = 256
MLA_BQ = 512
NEG = -1e30
SB_DEAD = -104.0

ADAM_LR, ADAM_B1, ADAM_B2, ADAM_EPS, ADAM_WD, ADAM_STEP = 0.001, 0.9, 0.999, 1e-08, 0.01, 10

_REAL = [("a_q", 256), ("a_k", 128), ("a_v", 128), ("b_b", 256), ("b_c", 256), ("b_x", 256),
         ("c_q", 256), ("c_kv", 128), ("c_kr", 32), ("d_q", 256), ("d_k", 256), ("d_v", 256),
         ("gate", 1024)]
_REAL_OFF = {}
_o = 0
for _n, _w in _REAL:
    _REAL_OFF[_n] = (_o, _w)
    _o += _w
D_IN = _o
_INT_ORDER = ["a_q", "a_k", "a_v", "d_q", "d_k", "d_v", "gate", "b_b", "b_c", "b_x", "c_q", "c_kv", "c_kr"]
_INT_W = dict(_REAL)
_INT_W["c_kr"] = 128
_INT_OFF = {}
_o = 0
for _n in _INT_ORDER:
    _INT_OFF[_n] = _o
    _o += _INT_W[_n]
N_INT = _o
N_HB = _INT_OFF["gate"]
N_HF = N_INT - N_HB

VMEM_LIMIT = 56 * 1024 * 1024


def _cparams(sem):
    return pltpu.CompilerParams(dimension_semantics=sem, vmem_limit_bytes=VMEM_LIMIT)


def _dot(a, b):
    return jnp.dot(a, b, preferred_element_type=F32)


def _dot_nt(a, b):
    return lax.dot_general(a, b, (((1,), (1,)), ((), ())), preferred_element_type=F32)


def _dot_tn(a, b):
    return lax.dot_general(a, b, (((0,), (0,)), ((), ())), preferred_element_type=F32)


def _split(x):
    hi = x.astype(BF16)
    lo = (x - hi.astype(F32)).astype(BF16)
    return hi, lo


def _rms(x):
    return lax.rsqrt(jnp.mean(x * x, axis=-1, keepdims=True) + EPS)


def _rms_bwd(dy, xhat, r, g):
    dxhat = dy * g
    return r * (dxhat - xhat * jnp.mean(dxhat * xhat, axis=-1, keepdims=True)), dy * xhat


def _colsum(x):
    return jnp.sum(x, axis=0, keepdims=True)


def _inproj_fwd(x, g, wt):
    T = x.shape[0]
    tm = 512

    def body(x_ref, g_ref, w_ref, xn_ref, hb_ref, hf_ref):
        xv = x_ref[...]
        xn = (xv * _rms(xv) * g_ref[...]).astype(BF16)
        xn_ref[...] = xn
        h = _dot_nt(xn, w_ref[...])
        hb_ref[...] = h[:, :N_HB].astype(BF16)
        hf_ref[...] = h[:, N_HB:]

    return pl.pallas_call(
        body, name="inproj_fwd", grid=(T // tm,),
        in_specs=[pl.BlockSpec((tm, D_MODEL), lambda i: (i, 0)),
                  pl.BlockSpec((1, D_MODEL), lambda i: (0, 0)),
                  pl.BlockSpec((N_INT, D_MODEL), lambda i: (0, 0))],
        out_specs=[pl.BlockSpec((tm, D_MODEL), lambda i: (i, 0)),
                   pl.BlockSpec((tm, N_HB), lambda i: (i, 0)),
                   pl.BlockSpec((tm, N_HF), lambda i: (i, 0))],
        out_shape=[jax.ShapeDtypeStruct((T, D_MODEL), BF16),
                   jax.ShapeDtypeStruct((T, N_HB), BF16),
                   jax.ShapeDtypeStruct((T, N_HF), F32)],
        compiler_params=_cparams(("parallel",)),
    )(x, g, wt)


def _inproj_bwd_dx(x, g, wt, dx_next, pieces):
    T = x.shape[0]
    tm = 512
    widths = [p.shape[1] for p in pieces]
    assert sum(widths) == N_INT

    def body(x_ref, g_ref, w_ref, dxn_ref, *rest):
        p_refs = rest[:len(pieces)]
        dx_ref, dh_ref, dg_ref = rest[len(pieces):]
        dh = jnp.concatenate([p[...].astype(BF16) for p in p_refs], axis=1)
        dh_ref[...] = dh
        dxn = _dot(dh, w_ref[...])
        xv = x_ref[...]
        r = _rms(xv)
        dx, dgrow = _rms_bwd(dxn, xv * r, r, g_ref[...])
        dx_ref[...] = dx + dxn_ref[...]

        @pl.when(pl.program_id(0) == 0)
        def _():
            dg_ref[...] = jnp.zeros_like(dg_ref)

        dg_ref[...] += _colsum(dgrow)

    return pl.pallas_call(
        body, name="inproj_bwd_dx", grid=(T // tm,),
        in_specs=[pl.BlockSpec((tm, D_MODEL), lambda i: (i, 0)),
                  pl.BlockSpec((1, D_MODEL), lambda i: (0, 0)),
                  pl.BlockSpec((N_INT, D_MODEL), lambda i: (0, 0)),
                  pl.BlockSpec((tm, D_MODEL), lambda i: (i, 0))]
                 + [pl.BlockSpec((tm, wd), lambda i: (i, 0)) for wd in widths],
        out_specs=[pl.BlockSpec((tm, D_MODEL), lambda i: (i, 0)),
                   pl.BlockSpec((tm, N_INT), lambda i: (i, 0)),
                   pl.BlockSpec((1, D_MODEL), lambda i: (0, 0))],
        out_shape=[jax.ShapeDtypeStruct((T, D_MODEL), F32),
                   jax.ShapeDtypeStruct((T, N_INT), BF16),
                   jax.ShapeDtypeStruct((1, D_MODEL), F32)],
        compiler_params=_cparams(("arbitrary",)),
    )(x, g, wt, dx_next, *pieces)


def _inproj_bwd_dw(xn, dh):
    T = xn.shape[0]
    tm = min(512, T)

    def body(a_ref, b_ref, o_ref):
        @pl.when(pl.program_id(0) == 0)
        def _():
            o_ref[...] = jnp.zeros_like(o_ref)

        for n, wd in _REAL:
            o, oi = _REAL_OFF[n][0], _INT_OFF[n]
            o_ref[o:o + wd, :] += _dot_tn(b_ref[:, oi:oi + _INT_W[n]], a_ref[...])[:wd]

    return pl.pallas_call(
        body, name="inproj_bwd_dw", grid=(T // tm,),
        in_specs=[pl.BlockSpec((tm, D_MODEL), lambda t: (t, 0)),
                  pl.BlockSpec((tm, N_INT), lambda t: (t, 0))],
        out_specs=pl.BlockSpec((D_IN, D_MODEL), lambda t: (0, 0)),
        out_shape=jax.ShapeDtypeStruct((D_IN, D_MODEL), F32),
        compiler_params=_cparams(("arbitrary",)),
    )(xn, dh)


def _roll_f32(x, shift):
    return pltpu.roll(x.astype(F32), shift, 1)


def _swa_operands(h, q, k_prev, k_cur, v_prev, v_cur):
    p, e = h // 2, h % 2
    lane = lax.broadcasted_iota(jnp.int32, (1, LANES), 1) // HEAD
    q = q[:, p * LANES:(p + 1) * LANES]
    if e != p:
        q = _roll_f32(q, HEAD).astype(BF16)
        v_prev = _roll_f32(v_prev, HEAD).astype(BF16)
        v_cur = _roll_f32(v_cur, HEAD).astype(BF16)
    qs = jnp.where(lane == p, q, 0) * 0.125
    return dict(p=p, e=e, lane=lane, qs=qs, k_prev=k_prev, k_cur=k_cur,
                v_prev=jnp.where(lane == e, v_prev, 0), v_cur=jnp.where(lane == e, v_cur, 0),
                s_prev=_dot_nt(qs, k_prev), s_cur=_dot_nt(qs, k_cur))


def _swa_probs(ops, sink, no_prev):
    row = lax.broadcasted_iota(jnp.int32, (BLOCK, BLOCK), 0)
    col = lax.broadcasted_iota(jnp.int32, (BLOCK, BLOCK), 1)
    ok_prev = col > row if no_prev is None else jnp.logical_and(col > row, jnp.logical_not(no_prev))
    s_prev = jnp.where(ok_prev, ops["s_prev"], NEG)
    s_cur = jnp.where(col <= row, ops["s_cur"], NEG)
    m = jnp.maximum(jnp.maximum(jnp.max(s_prev, axis=1, keepdims=True),
                                jnp.max(s_cur, axis=1, keepdims=True)), sink)
    p_prev = jnp.exp(s_prev - m)
    p_cur = jnp.exp(s_cur - m)
    p_sink = jnp.exp(sink - m)
    inv = 1.0 / (jnp.sum(p_prev, axis=1, keepdims=True) + jnp.sum(p_cur, axis=1, keepdims=True) + p_sink)
    return p_prev * inv, p_cur * inv, p_sink * inv


def _swa_specs(T):
    n = T // (BLOCK * SWA_SUB)
    qo, ko, vo = (_INT_OFF[name] // LANES for name in ("a_q", "a_k", "a_v"))
    halo = lambda i: jnp.maximum(i * SWA_SUB - 1, 0)
    return [pl.BlockSpec((BLOCK * SWA_SUB, 256), lambda i: (i, qo // 2)),
            pl.BlockSpec((BLOCK, LANES), lambda i: (halo(i), ko)),
            pl.BlockSpec((BLOCK * SWA_SUB, LANES), lambda i: (i, ko)),
            pl.BlockSpec((BLOCK, LANES), lambda i: (halo(i), vo)),
            pl.BlockSpec((BLOCK * SWA_SUB, LANES), lambda i: (i, vo)),
            pl.BlockSpec(memory_space=pltpu.SMEM)], n


def _swa_units(q_ref, kh_ref, kc_ref, vh_ref, vc_ref, s_ref):
    blk = lambda a: slice(a * BLOCK, (a + 1) * BLOCK)
    units = [(a, h) for a in range(SWA_SUB) for h in range(4)]
    ops = {}
    for a, h in units:
        k_prev, v_prev = (kh_ref[...], vh_ref[...]) if a == 0 else (kc_ref[blk(a - 1), :], vc_ref[blk(a - 1), :])
        ops[a, h] = _swa_operands(h, q_ref[blk(a), :], k_prev, kc_ref[blk(a), :], v_prev, vc_ref[blk(a), :])
    probs = {(a, h): _swa_probs(ops[a, h], s_ref[h], pl.program_id(0) == 0 if a == 0 else None) for a, h in units}
    return units, ops, probs, blk


def _swa_fwd(hb, sinks):
    T = hb.shape[0]
    specs, n = _swa_specs(T)

    def body(q_ref, kh_ref, kc_ref, vh_ref, vc_ref, s_ref, o_ref):
        units, ops, probs, blk = _swa_units(q_ref, kh_ref, kc_ref, vh_ref, vc_ref, s_ref)
        outs = {u: _dot(probs[u][0].astype(BF16), ops[u]["v_prev"]) + _dot(probs[u][1].astype(BF16), ops[u]["v_cur"])
                for u in units}
        for a in range(SWA_SUB):
            for p in range(2):
                o_ref[blk(a), p * LANES:(p + 1) * LANES] = outs[a, 2 * p] + outs[a, 2 * p + 1]

    return pl.pallas_call(
        body, name="swa_fwd", grid=(n,), in_specs=specs,
        out_specs=pl.BlockSpec((BLOCK * SWA_SUB, 256), lambda i: (i, 0)),
        out_shape=jax.ShapeDtypeStruct((T, 256), F32),
        compiler_params=_cparams(("parallel",)),
    )(hb, hb, hb, hb, hb, sinks)


def _swa_bwd(hb, sinks, dy):
    T = hb.shape[0]
    specs, n = _swa_specs(T)

    def body(q_ref, kh_ref, kc_ref, vh_ref, vc_ref, s_ref, dy_ref, dq_ref, dk_ref, dv_ref, ds_ref):
        i = pl.program_id(0)

        @pl.when(i == 0)
        def _():
            ds_ref[...] = jnp.zeros_like(ds_ref)

        lane_id = lax.broadcasted_iota(jnp.int32, (8, LANES), 1)
        units, ops, probs, blk = _swa_units(q_ref, kh_ref, kc_ref, vh_ref, vc_ref, s_ref)
        dos = {(a, h): jnp.where(ops[a, h]["lane"] == ops[a, h]["e"],
                                 dy_ref[blk(a), ops[a, h]["p"] * LANES:(ops[a, h]["p"] + 1) * LANES], 0.0)
               for a, h in units}
        dobs = {u: dos[u].astype(BF16) for u in units}
        pbs = {u: (probs[u][0].astype(BF16), probs[u][1].astype(BF16)) for u in units}
        outs = {u: _dot(pbs[u][0], ops[u]["v_prev"]) + _dot(pbs[u][1], ops[u]["v_cur"]) for u in units}
        dps = {u: (_dot_nt(dobs[u], ops[u]["v_prev"]), _dot_nt(dobs[u], ops[u]["v_cur"])) for u in units}
        dss, dsinks = {}, jnp.zeros((8, LANES), F32)
        for u in units:
            delta = jnp.sum(dos[u] * outs[u], axis=1, keepdims=True)
            dss[u] = ((probs[u][0] * (dps[u][0] - delta)).astype(BF16),
                      (probs[u][1] * (dps[u][1] - delta)).astype(BF16))
            dsink = -jnp.sum(probs[u][2] * delta, axis=0, keepdims=True)
            dsinks += jnp.where(lane_id == u[1], dsink, 0.0)
        ds_ref[...] += dsinks
        dqs = {u: (_dot(dss[u][0], ops[u]["k_prev"]) + _dot(dss[u][1], ops[u]["k_cur"])) * 0.125 for u in units}
        zero = jnp.zeros((BLOCK, LANES), F32)
        dk_as_prev, dk_as_cur = [zero] * SWA_SUB, [zero] * SWA_SUB
        dv_as_prev, dv_as_cur = [zero] * SWA_SUB, [zero] * SWA_SUB
        for a, h in units:
            p, e = ops[a, h]["p"], ops[a, h]["e"]
            dob_v = dobs[a, h] if e == p else pltpu.roll(dos[a, h], HEAD, 1).astype(BF16)
            dk_as_prev[a] = dk_as_prev[a] + _dot_tn(dss[a, h][0], ops[a, h]["qs"])
            dk_as_cur[a] = dk_as_cur[a] + _dot_tn(dss[a, h][1], ops[a, h]["qs"])
            dv_as_prev[a] = dv_as_prev[a] + _dot_tn(pbs[a, h][0], dob_v)
            dv_as_cur[a] = dv_as_cur[a] + _dot_tn(pbs[a, h][1], dob_v)
        base = i * SWA_SUB
        for a in range(SWA_SUB):
            rows = pl.ds(pl.multiple_of((base + a) * BLOCK, BLOCK), BLOCK)
            more = a + 1 < SWA_SUB
            dk_ref[rows, :] = dk_as_cur[a] + (dk_as_prev[a + 1] if more else 0.0)
            dv_ref[rows, :] = dv_as_cur[a] + (dv_as_prev[a + 1] if more else 0.0)
        halo = pl.ds(pl.multiple_of(jnp.maximum(base - 1, 0) * BLOCK, BLOCK), BLOCK)
        dk_ref[halo, :] += dk_as_prev[0]
        dv_ref[halo, :] += dv_as_prev[0]
        for a in range(SWA_SUB):
            for p in range(2):
                dq_pair = jnp.zeros((BLOCK, LANES), F32)
                for e in range(2):
                    dq = jnp.where(ops[a, 2 * p + e]["lane"] == p, dqs[a, 2 * p + e], 0.0)
                    dq_pair += dq if e == p else pltpu.roll(dq, HEAD, 1)
                dq_ref[blk(a), p * LANES:(p + 1) * LANES] = dq_pair.astype(BF16)

    return pl.pallas_call(
        body, name="swa_bwd", grid=(n,),
        in_specs=specs + [pl.BlockSpec((BLOCK * SWA_SUB, 256), lambda i: (i, 0))],
        out_specs=[pl.BlockSpec((BLOCK * SWA_SUB, 256), lambda i: (i, 0)),
                   pl.BlockSpec((T, LANES), lambda i: (0, 0)),
                   pl.BlockSpec((T, LANES), lambda i: (0, 0)),
                   pl.BlockSpec((8, LANES), lambda i: (0, 0))],
        out_shape=[jax.ShapeDtypeStruct((T, 256), BF16),
                   jax.ShapeDtypeStruct((T, LANES), F32),
                   jax.ShapeDtypeStruct((T, LANES), F32),
                   jax.ShapeDtypeStruct((8, LANES), F32)],
        compiler_params=_cparams(("arbitrary",)),
    )(hb, hb, hb, hb, hb, sinks, dy)


def _rope_tables(pos, fetch=()):
    T = pos.shape[0]
    tm = 512
    nt, n = T // tm, len(fetch)

    def body(pos_ref, *rest):
        o_ref = rest[n]
        if n:
            start, finish = _gather_plan(rest[:n], rest[n + 1:2 * n + 1], rest[2 * n + 1:])
            pl.when(pl.program_id(0) == 0)(start)
        lane = lax.broadcasted_iota(jnp.int32, (1, LANES), 1)
        active = jnp.logical_and(lane >= HEAD, lane < HEAD + 2 * ROPE_HALF)
        idx = ((lane - HEAD) % ROPE_HALF).astype(F32)
        freq = jnp.exp(idx * (-math.log(ROPE_THETA) / ROPE_HALF))
        ang = pos_ref[...].astype(F32) * freq
        cos, sin = jnp.cos(ang), jnp.sin(ang)
        o_ref[:, 0:LANES] = jnp.where(active, cos, 1.0)
        o_ref[:, LANES:2 * LANES] = jnp.where(jnp.logical_and(active, lane >= HEAD + ROPE_HALF), sin, 0.0)
        o_ref[:, 2 * LANES:] = jnp.where(jnp.logical_and(active, lane < HEAD + ROPE_HALF), -sin, 0.0)
        if n:
            pl.when(pl.program_id(0) == nt - 1)(finish)

    return pl.pallas_call(
        body, name="rope_tables_fetch" if n else "rope_tables", grid=(nt,),
        in_specs=[pl.BlockSpec((tm, 1), lambda i: (i, 0))] + [HBM] * n,
        out_specs=[pl.BlockSpec((tm, 3 * LANES), lambda i: (i, 0))] + [HBM] * n,
        out_shape=[jax.ShapeDtypeStruct((T, 3 * LANES), F32)]
                  + [jax.ShapeDtypeStruct((4,) + s.shape, s.dtype) for s in fetch],
        scratch_shapes=_gather_sems(n) if n else [],
        compiler_params=_cparams(("arbitrary",) if n else ("parallel",)),
    )(pos, *fetch)


def _rope_factors(tab_ref):
    return tab_ref[:, 0:LANES], tab_ref[:, LANES:2 * LANES], tab_ref[:, 2 * LANES:]


def _rope(x, tabs):
    c, s_up, s_dn = tabs
    return x * c + pltpu.roll(x, ROPE_HALF, 1) * s_up + pltpu.roll(x, LANES - ROPE_HALF, 1) * s_dn


def _rope_t(dy, tabs):
    c, s_up, s_dn = tabs
    return dy * c + pltpu.roll(dy * s_up, LANES - ROPE_HALF, 1) + pltpu.roll(dy * s_dn, ROPE_HALF, 1)


def _mla_lat_specs(tm):
    cq, ckv, ckr = ((_INT_OFF[n] - N_HB) for n in ("c_q", "c_kv", "c_kr"))
    return [pl.BlockSpec((tm, 256), lambda i: (i, cq // 256)),
            pl.BlockSpec((tm, LANES), lambda i: (i, ckv // LANES)),
            pl.BlockSpec((tm, LANES), lambda i: (i, ckr // LANES)),
            pl.BlockSpec((tm, 3 * LANES), lambda i: (i, 0)),
            pl.BlockSpec((1, 256), lambda i: (0, 0)),
            pl.BlockSpec((1, LANES), lambda i: (0, 0)),
            pl.BlockSpec((256, 512), lambda i: (0, 0)),
            pl.BlockSpec((LANES, 768), lambda i: (0, 0))]


def _mla_prep_fwd(hf, rope, g_q, g_kv, w_uq, w_ukv):
    T = hf.shape[0]
    tm = 512
    sub = tm // ATT_BLK

    def body(cq_ref, ckv_ref, ckr_ref, tab_ref, gq_ref, gkv_ref, wq_ref, wkv_ref, qm_ref, km_ref, vm_ref, vt_ref):
        tabs = _rope_factors(tab_ref)
        cq = cq_ref[...]
        q = _dot((cq * _rms(cq) * gq_ref[...]).astype(BF16), wq_ref[...])
        ckv = ckv_ref[...]
        kv = _dot((ckv * _rms(ckv) * gkv_ref[...]).astype(BF16), wkv_ref[...])
        kr = _rope(pltpu.roll(ckr_ref[...], HEAD, 1), tabs)
        for h in range(4):
            sl = slice(h * LANES, (h + 1) * LANES)
            qm_ref[:, sl] = (_rope(q[:, sl], tabs) * MLA_QSCALE).astype(BF16)
            km_ref[:, sl] = (kv[:, sl] + kr).astype(BF16)
        vm_ref[...] = kv[:, 512:].astype(BF16)
        for p in range(2):
            for s in range(sub):
                tile = kv[s * ATT_BLK:(s + 1) * ATT_BLK, 512 + p * LANES:512 + (p + 1) * LANES]
                vt_ref[p, s] = jnp.transpose(tile).astype(BF16)

    return pl.pallas_call(
        body, name="mla_prep_fwd", grid=(T // tm,), in_specs=_mla_lat_specs(tm),
        out_specs=[pl.BlockSpec((tm, 512), lambda i: (i, 0)),
                   pl.BlockSpec((tm, 512), lambda i: (i, 0)),
                   pl.BlockSpec((tm, 256), lambda i: (i, 0)),
                   pl.BlockSpec((2, sub, LANES, ATT_BLK), lambda i: (0, i, 0, 0))],
        out_shape=[jax.ShapeDtypeStruct((T, 512), BF16),
                   jax.ShapeDtypeStruct((T, 512), BF16),
                   jax.ShapeDtypeStruct((T, 256), BF16),
                   jax.ShapeDtypeStruct((2, T // ATT_BLK, LANES, ATT_BLK), BF16)],
        compiler_params=_cparams(("parallel",)),
    )(hf, hf, hf, rope, g_q, g_kv, w_uq, w_ukv)


def _mla_prep_bwd(hf, rope, g_q, g_kv, w_uq, w_ukv, dqm, dkt, dvt):
    T = hf.shape[0]
    tm = 512
    sub = tm // ATT_BLK

    def body(cq_ref, ckv_ref, ckr_ref, tab_ref, gq_ref, gkv_ref, wq_ref, wkv_ref, dq_ref, dk_ref, dv_ref,
             dc_ref, dwq_ref, dwkv_ref, dgq_ref, dgkv_ref):
        @pl.when(pl.program_id(0) == 0)
        def _():
            dwq_ref[...] = jnp.zeros_like(dwq_ref)
            dwkv_ref[...] = jnp.zeros_like(dwkv_ref)
            dgq_ref[...] = jnp.zeros_like(dgq_ref)
            dgkv_ref[...] = jnp.zeros_like(dgkv_ref)

        tabs = _rope_factors(tab_ref)
        lane =lax.broadcasted_iota(jnp.int32, (1, LANES), 1)
        dq = jnp.concatenate([_rope_t(dq_ref[:, h * LANES:(h + 1) * LANES] * MLA_QSCALE, tabs)
                              for h in range(4)], axis=1).astype(BF16)
        cq = cq_ref[...]
        rq = _rms(cq)
        cqn = (cq * rq * gq_ref[...]).astype(BF16)
        dwq_ref[...] += _dot_tn(cqn, dq)
        dcq, dgrow = _rms_bwd(_dot_nt(dq, wq_ref[...]), cq * rq, rq, gq_ref[...])
        dgq_ref[...] += _colsum(dgrow)
        dc_ref[:, 0:256] = dcq.astype(BF16)

        dk = jnp.concatenate([jnp.concatenate([jnp.transpose(dk_ref[p, s]) for p in range(2)], axis=1)
                              for s in range(sub)], axis=0) * LN2
        dv = jnp.concatenate([jnp.concatenate([jnp.transpose(dv_ref[p, s]) for p in range(2)], axis=1)
                              for s in range(sub)], axis=0)
        dkr = dk[:, 0:LANES] + dk[:, LANES:2 * LANES] + dk[:, 2 * LANES:3 * LANES] + dk[:, 3 * LANES:]
        dkr = pltpu.roll(_rope_t(dkr, tabs), HEAD, 1)
        dc_ref[:, 384:512] = jnp.where(lane < 2 * ROPE_HALF, dkr, 0.0).astype(BF16)
        dkv = jnp.concatenate([dk.astype(BF16), dv.astype(BF16)], axis=1)
        ckv = ckv_ref[...]
        rkv = _rms(ckv)
        ckvn = (ckv * rkv * gkv_ref[...]).astype(BF16)
        dwkv_ref[...] += _dot_tn(ckvn, dkv)
        dckv, dgrow = _rms_bwd(_dot_nt(dkv, wkv_ref[...]), ckv * rkv, rkv, gkv_ref[...])
        dgkv_ref[...] += _colsum(dgrow)
        dc_ref[:, 256:384] = dckv.astype(BF16)

    return pl.pallas_call(
        body, name="mla_prep_bwd", grid=(T // tm,),
        in_specs=_mla_lat_specs(tm) + [pl.BlockSpec((tm, 512), lambda i: (i, 0)),
                                       pl.BlockSpec((2, sub, 256, ATT_BLK), lambda i: (0, i, 0, 0)),
                                       pl.BlockSpec((2, sub, LANES, ATT_BLK), lambda i: (0, i, 0, 0))],
        out_specs=[pl.BlockSpec((tm, 512), lambda i: (i, 0)),
                   pl.BlockSpec((256, 512), lambda i: (0, 0)),
                   pl.BlockSpec((LANES, 768), lambda i: (0, 0)),
                   pl.BlockSpec((1, 256), lambda i: (0, 0)),
                   pl.BlockSpec((1, LANES), lambda i: (0, 0))],
        out_shape=[jax.ShapeDtypeStruct((T, 512), BF16),
                   jax.ShapeDtypeStruct((256, 512), F32),
                   jax.ShapeDtypeStruct((LANES, 768), F32),
                   jax.ShapeDtypeStruct((1, 256), F32),
                   jax.ShapeDtypeStruct((1, LANES), F32)],
        compiler_params=_cparams(("arbitrary",)),
    )(hf, hf, hf, rope, g_q, g_kv, w_uq, w_ukv, dqm, dkt, dvt)


def _causal_masks(bq, bk):
    row = lax.broadcasted_iota(jnp.int32, (bq, bk), 0)
    col = lax.broadcasted_iota(jnp.int32, (bq, bk), 1)
    return row, col


def _mla_fwd(qm, km, vt, fetch=()):
    T = qm.shape[0]
    bq, bk = min(MLA_BQ, T), ATT_BLK
    nq, nsub, nk = T // bq, bq // bk, T // bk
    n = len(fetch)

    def body(q_ref, k_ref, vt_ref, *rest):
        o_ref, lse_ref = rest[n:n + 2]
        acc_ref, m_ref, l_ref = rest[2 * n + 2:2 * n + 5]
        if n:
            start, finish = _gather_plan(rest[:n], rest[n + 2:2 * n + 2], rest[2 * n + 5:])
            pl.when(pl.program_id(0) == 0)(start)
        qi = pl.program_id(0)
        key = lax.broadcasted_iota(jnp.int32, (bk, bq), 0)
        qry = lax.broadcasted_iota(jnp.int32, (bk, bq), 1)
        ones = jnp.ones((8, bk), BF16)
        acc_ref[...] = jnp.zeros_like(acc_ref)
        m_ref[...] = jnp.full_like(m_ref, NEG)
        l_ref[...] = jnp.zeros_like(l_ref)

        def step(kb0, masked):
            kbs = [kb0 + d for d in range(nsub)]
            qs = [slice(d * bk if masked else 0, bq) for d in range(nsub)]

            def wide(a, d, fill):
                if not qs[d].start:
                    return a
                return jnp.concatenate([jnp.full((a.shape[0], qs[d].start), fill, a.dtype), a], axis=1)

            sts = [[_dot_nt(k_ref[pl.ds(pl.multiple_of(kb * bk, bk), bk), e * LANES:(e + 1) * LANES],
                            q_ref[qs[d], e * LANES:(e + 1) * LANES]) for d, kb in enumerate(kbs)] for e in range(4)]
            pts, alphas = [], []
            for e in range(4):
                st = ([jnp.where(key[:, qs[d]] + d * bk <= qry[:, qs[d]], sts[e][d], NEG) for d in range(nsub)]
                      if masked else sts[e])
                m_prev = m_ref[e, 0:1, :]
                m_new = m_prev
                for d in range(nsub):
                    m_new = jnp.maximum(m_new, wide(jnp.max(st[d], axis=0, keepdims=True), d, NEG))
                alpha = jnp.exp2(m_prev - m_new)
                pt = [jnp.exp2(st[d] - m_new[:, qs[d]]).astype(BF16) for d in range(nsub)]
                l_new = alpha * l_ref[e]
                for d in range(nsub):
                    l_new = l_new + wide(_dot(ones, pt[d]), d, 0.0)
                l_ref[e] = l_new
                m_ref[e] = jnp.broadcast_to(m_new, (8, bq))
                pts.append(pt)
                alphas.append(alpha)
            for e in range(4):
                acc = alphas[e] * acc_ref[e]
                for d in range(nsub):
                    v_t = vt_ref[e // 2, kbs[d], (e % 2) * HEAD:(e % 2 + 1) * HEAD, :]
                    acc = acc + wide(_dot(v_t, pts[e][d]), d, 0.0)
                acc_ref[e] = acc

        step(qi * nsub, True)

        def loop(t, c):
            step(t * nsub, False)
            return c

        lax.fori_loop(0, qi, loop, 0)
        outs, lses = [], []
        for e in range(4):
            l = l_ref[e, 0:1, :]
            outs.append(acc_ref[e] / l)
            lses.append(jnp.broadcast_to(m_ref[e, 0:1, :] * LN2 + jnp.log(l), (HEAD, bq)))
        o_ref[...] = jnp.transpose(jnp.concatenate(outs, axis=0))
        lse_ref[...] = jnp.transpose(jnp.concatenate(lses, axis=0))
        if n:
            pl.when(pl.program_id(0) == nq - 1)(finish)

    return pl.pallas_call(
        body, name="mla_fwd_fetch" if n else "mla_fwd", grid=(nq,),
        in_specs=[pl.BlockSpec((bq, 512), lambda i: (i, 0)),
                  pl.BlockSpec((T, 512), lambda i: (0, 0)),
                  pl.BlockSpec((2, nk, LANES, bk), lambda i: (0, 0, 0, 0))] + [HBM] * n,
        out_specs=[pl.BlockSpec((bq, 256), lambda i: (i, 0)),
                   pl.BlockSpec((bq, 256), lambda i: (i, 0))] + [HBM] * n,
        out_shape=[jax.ShapeDtypeStruct((T, 256), F32), jax.ShapeDtypeStruct((T, 256), F32)]
                  + [jax.ShapeDtypeStruct((4,) + s.shape, s.dtype) for s in fetch],
        scratch_shapes=[pltpu.VMEM((4, HEAD, bq), F32), pltpu.VMEM((4, 8, bq), F32), pltpu.VMEM((4, 8, bq), F32)]
                       + (_gather_sems(n) if n else []),
        compiler_params=_cparams(("arbitrary",)),
    )(qm, km, vt, *fetch)


def _mla_bwd(qm, km, vm, y, lse, dy):
    T = qm.shape[0]
    bq, bk = min(MLA_BQ, T), ATT_BLK
    nq, nsub, nk = T // bq, bq // bk, T // bk

    def body(q_ref, k_ref, v_ref, y_ref, lse_ref, dy_ref, dq_ref, dkt_ref, dvt_ref, dob_ref, st_ref, qt_ref, dot_ref):
        qi = pl.program_id(1)

        @pl.when(qi == 0)
        def _():
            dkt_ref[...] = jnp.zeros_like(dkt_ref)
            dvt_ref[...] = jnp.zeros_like(dvt_ref)

        lane = lax.broadcasted_iota(jnp.int32, (1, LANES), 1) // HEAD
        row, col = _causal_masks(bq, bk)
        dq_ref[...] = jnp.zeros_like(dq_ref)
        lse = lse_ref[...]
        lse_other = pltpu.roll(lse, HEAD, 1)
        qt_ref[...] = jnp.transpose(q_ref[...].astype(F32)).astype(BF16)
        dot_ref[...] = jnp.transpose(dy_ref[...]).astype(BF16)
        for e in range(2):
            do = jnp.where(lane == e, dy_ref[...], 0.0)
            dob_ref[e] = do.astype(BF16)
            st_ref[2 * e] = jnp.where(lane == e, lse, lse_other) * LOG2E
            st_ref[2 * e + 1] = jnp.broadcast_to(jnp.sum(do * y_ref[...], axis=1, keepdims=True), (bq, LANES))

        hss = [slice(e * LANES, (e + 1) * LANES) for e in range(2)]
        tile = lambda a: jnp.concatenate([a] * (bk // LANES), axis=1)

        def step(kb0, masked):
            kbs = [kb0 + d for d in range(nsub)]
            rows = [pl.ds(pl.multiple_of(kb * bk, bk), bk) for kb in kbs]
            pairs = [(d, e) for d in range(nsub) for e in range(2)]
            qs = [slice(d * bk if masked else 0, bq) for d in range(nsub)]
            ss = {(d, e): _dot_nt(q_ref[qs[d], hss[e]], k_ref[rows[d], hss[e]]) for d, e in pairs}
            dps = {(d, e): _dot_nt(dob_ref[e, qs[d], :], jnp.where(lane == e, v_ref[rows[d], :], 0))
                   for d, e in pairs}
            ps, dss = {}, {}
            for d, e in pairs:
                s = jnp.where(col[qs[d]] + d * bk <= row[qs[d]], ss[d, e], NEG) if masked else ss[d, e]
                p = jnp.exp2(s - tile(st_ref[2 * e, qs[d], :]))
                dss[d, e] = (p * (dps[d, e] - tile(st_ref[2 * e + 1, qs[d], :]))).astype(BF16)
                ps[d, e] = p.astype(BF16)
            for d, e in pairs:
                dvt_ref[0, kbs[d], e * HEAD:(e + 1) * HEAD, :] += _dot(
                    dot_ref[e * HEAD:(e + 1) * HEAD, qs[d]], ps[d, e])
            for d, e in pairs:
                dkt_ref[0, kbs[d], hss[e], :] += _dot(qt_ref[hss[e], qs[d]], dss[d, e])
            for e in range(2):
                if masked:
                    for d in range(nsub):
                        dq_ref[qs[d], hss[e]] += _dot(dss[d, e], k_ref[rows[d], hss[e]])
                else:
                    dq = dq_ref[:, hss[e]]
                    for d in range(nsub):
                        dq = dq + _dot(dss[d, e], k_ref[rows[d], hss[e]])
                    dq_ref[:, hss[e]] = dq

        step(qi * nsub, True)

        def loop(t, c):
            step(t * nsub, False)
            return c

        lax.fori_loop(0, qi, loop, 0)
        dq_ref[...] *= LN2

    return pl.pallas_call(
        body, name="mla_bwd", grid=(2, nq),
        in_specs=[pl.BlockSpec((bq, 256), lambda j, i: (i, j)),
                  pl.BlockSpec((T, 256), lambda j, i: (0, j)),
                  pl.BlockSpec((T, LANES), lambda j, i: (0, j)),
                  pl.BlockSpec((bq, LANES), lambda j, i: (i, j)),
                  pl.BlockSpec((bq, LANES), lambda j, i: (i, j)),
                  pl.BlockSpec((bq, LANES), lambda j, i: (i, j))],
        out_specs=[pl.BlockSpec((bq, 256), lambda j, i: (i, j)),
                   pl.BlockSpec((1, nk, 256, bk), lambda j, i: (j, 0, 0, 0)),
                   pl.BlockSpec((1, nk, LANES, bk), lambda j, i: (j, 0, 0, 0))],
        out_shape=[jax.ShapeDtypeStruct((T, 512), F32),
                   jax.ShapeDtypeStruct((2, nk, 256, bk), F32),
                   jax.ShapeDtypeStruct((2, nk, LANES, bk), F32)],
        scratch_shapes=[pltpu.VMEM((2, bq, LANES), BF16), pltpu.VMEM((4, bq, LANES), F32),
                        pltpu.VMEM((256, bq), BF16), pltpu.VMEM((LANES, bq), BF16)],
        compiler_params=_cparams(("parallel", "arbitrary")),
    )(qm, km, vm, y, lse, dy)


def _suffix_ones(n):
    r = lax.broadcasted_iota(jnp.int32, (n, n), 0)
    c = lax.broadcasted_iota(jnp.int32, (n, n), 1)
    return (r >= c).astype(BF16)


def _prefix_ones(n):
    r = lax.broadcasted_iota(jnp.int32, (n, n), 0)
    c = lax.broadcasted_iota(jnp.int32, (n, n), 1)
    return (r <= c).astype(BF16)


def _sb_specs(T, bq):
    qo, ko, vo = (_INT_OFF[n] // 256 for n in ("d_q", "d_k", "d_v"))
    return [pl.BlockSpec((bq, 256), lambda i: (i, qo)),
            pl.BlockSpec((T, 256), lambda i: (0, ko)),
            pl.BlockSpec((T, 256), lambda i: (0, vo))]


def _sb_fwd(hb):
    T = hb.shape[0]
    bq = bk = ATT_BLK
    nq = T // bq

    def body(q_ref, k_ref, v_ref, o_ref, tot_ref, cnt_ref, qm_ref, car_ref):
        qi = pl.program_id(0)
        lane = lax.broadcasted_iota(jnp.int32, (1, LANES), 1) // HEAD
        row, col = _causal_masks(bq, bk)
        strict = col < row
        u = _suffix_ones(bk)
        o_ref[...] = jnp.zeros_like(o_ref)
        car_ref[...] = jnp.zeros_like(car_ref)
        pair = lambda h: slice((h // 2) * LANES, (h // 2 + 1) * LANES)
        for h in range(4):
            qm_ref[h] = jnp.where(lane == h % 2, q_ref[:, pair(h)], 0) * 0.125

        def step(blocks):
            tile = lambda a: jnp.concatenate([a] * (bk // LANES), axis=1)
            rows = [pl.ds(pl.multiple_of(kb * bk, bk), bk) for kb, _ in blocks]
            pairs = [(b, h) for b in range(len(blocks)) for h in range(4)]
            zs = {(b, h): _dot_nt(qm_ref[h], k_ref[rows[b], pair(h)]) for b, h in pairs}
            splits = {}
            for b, h in pairs:
                z = zs[b, h]
                lk = jnp.minimum(-z, 0.0) - jnp.log(1.0 + jnp.exp(-jnp.abs(z)))
                if blocks[b][1] is not None:
                    lk = jnp.where(blocks[b][1], lk, 0.0)
                splits[b, h] = _split(lk)
            sufs = {bh: _dot(hi, u) + _dot(lo, u) for bh, (hi, lo) in splits.items()}
            car = [car_ref[h] for h in range(4)]
            aas = {}
            for b, h in pairs:
                a = jnp.exp(zs[b, h] + sufs[b, h] + tile(car[h]))
                if blocks[b][1] is not None:
                    a = jnp.where(blocks[b][1], a, 0.0)
                aas[b, h] = a.astype(BF16)
                car[h] = car[h] + jnp.broadcast_to(sufs[b, h][:, 0:1], (bq, LANES))
            acc = [o_ref[:, pair(0)], o_ref[:, pair(2)]]
            for b, h in pairs:
                acc[h // 2] = acc[h // 2] + _dot(aas[b, h], jnp.where(lane == h % 2, v_ref[rows[b], pair(h)], 0))
            o_ref[:, pair(0)], o_ref[:, pair(2)] = acc
            for h in range(4):
                car_ref[h] = car[h]

        step([(qi, strict), (jnp.maximum(qi - 1, 0), qi > 0)])

        def live():
            worst = jnp.maximum(jnp.maximum(car_ref[0], car_ref[1]), jnp.maximum(car_ref[2], car_ref[3]))
            return jnp.max(worst) >= SB_DEAD

        def cond(c):
            return jnp.logical_and(c[0] < qi, c[1])

        def loop(c):
            step([(qi - 1 - c[0], None)])
            return c[0] + 1, live()

        done, _ = lax.while_loop(cond, loop, (jnp.minimum(qi, 1), live()))
        tot_ref[:, pair(0)] = jnp.where(lane == 0, car_ref[0], car_ref[1])
        tot_ref[:, pair(2)] = jnp.where(lane == 0, car_ref[2], car_ref[3])
        cnt_ref[0, qi] = done.astype(F32)

    return pl.pallas_call(
        body, name="sb_fwd", grid=(nq,), in_specs=_sb_specs(T, bq),
        out_specs=[pl.BlockSpec((bq, 256), lambda i: (i, 0)), pl.BlockSpec((bq, 256), lambda i: (i, 0)),
                   pl.BlockSpec(memory_space=pltpu.SMEM)],
        out_shape=[jax.ShapeDtypeStruct((T, 256), F32), jax.ShapeDtypeStruct((T, 256), F32),
                   jax.ShapeDtypeStruct((1, nq), F32)],
        scratch_shapes=[pltpu.VMEM((4, bq, LANES), BF16), pltpu.VMEM((4, bq, LANES), F32)],
        compiler_params=_cparams(("arbitrary",)),
    )(hb, hb, hb)


def _sb_bwd(hb, tot, cnt, dy):
    T = hb.shape[0]
    bq = bk = ATT_BLK
    nq = T // bq

    def body(q_ref, k_ref, v_ref, tot_ref, dy_ref, cnt_ref, dq_ref, dk_ref, dv_ref, qm_ref, dob_ref, dqa_ref, rem_ref,
             cg_ref):
        qi = pl.program_id(0)

        @pl.when(qi == 0)
        def _():
            dk_ref[...] = jnp.zeros_like(dk_ref)
            dv_ref[...] = jnp.zeros_like(dv_ref)

        lane = lax.broadcasted_iota(jnp.int32, (1, LANES), 1) // HEAD
        row, col = _causal_masks(bq, bk)
        strict = col < row
        u = _prefix_ones(bk)
        pair = lambda h: slice((h // 2) * LANES, (h // 2 + 1) * LANES)
        dqa_ref[...] = jnp.zeros_like(dqa_ref)
        cg_ref[...] = jnp.zeros_like(cg_ref)
        for h in range(4):
            tot = tot_ref[:, pair(h)]
            qm_ref[h] = jnp.where(lane == h % 2, q_ref[:, pair(h)], 0) * 0.125
            dob_ref[h] = jnp.where(lane == h % 2, dy_ref[:, pair(h)], 0.0).astype(BF16)
            rem_ref[h] = jnp.where(lane == h % 2, tot, pltpu.roll(tot, HEAD, 1))

        def step(blocks):
            tile = lambda a: jnp.concatenate([a] * (bk // LANES), axis=1)
            nb = len(blocks)
            rows = [pl.ds(pl.multiple_of(kb * bk, bk), bk) for kb, _ in blocks]
            pairs = [(b, h) for b in range(nb) for h in range(4)]
            mask = lambda b, x: x if blocks[b][1] is None else jnp.where(blocks[b][1], x, 0.0)
            zs = {(b, h): _dot_nt(qm_ref[h], k_ref[rows[b], pair(h)]) for b, h in pairs}
            das = {(b, h): _dot_nt(dob_ref[h], jnp.where(lane == h % 2, v_ref[rows[b], pair(h)], 0)) for b, h in pairs}
            zls, splits = {}, {}
            for b, h in pairs:
                z = zs[b, h]
                lk = mask(b, jnp.minimum(-z, 0.0) - jnp.log(1.0 + jnp.exp(-jnp.abs(z))))
                zls[b, h] = z + lk
                splits[b, h] = _split(lk)
            pres = {bh: _dot(hi, u) + _dot(lo, u) for bh, (hi, lo) in splits.items()}
            rem = [rem_ref[h] for h in range(4)]
            aas, gs, gsplits = {}, {}, {}
            for b, h in pairs:
                a = mask(b, jnp.exp(zls[b, h] + (tile(rem[h]) - pres[b, h])))
                gs[b, h] = a * das[b, h]
                aas[b, h] = a.astype(BF16)
                gsplits[b, h] = _split(gs[b, h])
                rem[h] = rem[h] - jnp.broadcast_to(pres[b, h][:, bk - 1:bk], (bq, LANES))
            for b in range(nb):
                for p in (0, 2):
                    dv_ref[rows[b], pair(p)] += _dot_tn(aas[b, p], dob_ref[p]) + _dot_tn(aas[b, p + 1], dob_ref[p + 1])
            gpres = {bh: _dot(hi, u) + _dot(lo, u) for bh, (hi, lo) in gsplits.items()}
            cg = [cg_ref[h] for h in range(4)]
            dzs = {}
            for b, h in pairs:
                dz = mask(b, gs[b, h] - jnp.exp(zls[b, h]) * (tile(cg[h]) + gpres[b, h]))
                dzs[b, h] = dz.astype(BF16)
                cg[h] = cg[h] + jnp.broadcast_to(gpres[b, h][:, bk - 1:bk], (bq, LANES))
            for b in range(nb):
                for p in (0, 2):
                    dk_ref[rows[b], pair(p)] += _dot_tn(dzs[b, p], qm_ref[p]) + _dot_tn(dzs[b, p + 1], qm_ref[p + 1])
            for h in range(4):
                dq = dqa_ref[h]
                for b in range(nb):
                    dq = dq + _dot(dzs[b, h], k_ref[rows[b], pair(h)])
                dqa_ref[h] = dq
                rem_ref[h] = rem[h]
                cg_ref[h] = cg[h]

        def loop(kb, c):
            step([(kb, None)])
            return c

        start = qi - jnp.clip(cnt_ref[0, qi].astype(jnp.int32), 0, qi)
        lax.fori_loop(start, qi - 1, loop, 0)
        step([(jnp.maximum(qi - 1, 0), qi > 0), (qi, strict)])
        for p in (0, 2):
            dq_ref[:, pair(p)] = (jnp.where(lane == 0, dqa_ref[p], dqa_ref[p + 1]) * 0.125).astype(BF16)

    return pl.pallas_call(
        body, name="sb_bwd", grid=(nq,),
        in_specs=_sb_specs(T, bq) + [pl.BlockSpec((bq, 256), lambda i: (i, 0)),
                                     pl.BlockSpec((bq, 256), lambda i: (i, 0)),
                                     pl.BlockSpec(memory_space=pltpu.SMEM)],
        out_specs=[pl.BlockSpec((bq, 256), lambda i: (i, 0)),
                   pl.BlockSpec((T, 256), lambda i: (0, 0)),
                   pl.BlockSpec((T, 256), lambda i: (0, 0))],
        out_shape=[jax.ShapeDtypeStruct((T, 256), BF16)] + [jax.ShapeDtypeStruct((T, 256), F32)] * 2,
        scratch_shapes=[pltpu.VMEM((4, bq, LANES), BF16), pltpu.VMEM((4, bq, LANES), BF16),
                        pltpu.VMEM((4, bq, LANES), F32), pltpu.VMEM((4, bq, LANES), F32),
                        pltpu.VMEM((4, bq, LANES), F32)],
        compiler_params=_cparams(("arbitrary",)),
    )(hb, hb, hb, tot, dy, cnt)


EP_TM = 512


def _ep_in_specs(tm, rev):
    idx = (lambda i: rev - i) if rev is not None else (lambda i: i)
    bo = (_INT_OFF["b_b"] - N_HB) // 256
    halo = lambda i: jnp.maximum(idx(i) * (tm // 8) - 1, 0)
    return [pl.BlockSpec((tm, 256), lambda i: (idx(i), 0)),
            pl.BlockSpec((tm, 256), lambda i: (idx(i), 0)),
            pl.BlockSpec((tm, 256), lambda i: (idx(i), 0)),
            pl.BlockSpec((tm, D_MODEL), lambda i: (idx(i), 0)),
            pl.BlockSpec((tm, 256), lambda i: (idx(i), bo)),
            pl.BlockSpec((tm, 256), lambda i: (idx(i), bo + 1)),
            pl.BlockSpec((tm, 256), lambda i: (idx(i), bo + 2)),
            pl.BlockSpec((8, 256), lambda i: (halo(i), bo + 1)),
            pl.BlockSpec((8, 256), lambda i: (halo(i), bo + 2)),
            pl.BlockSpec((3, 256), lambda i: (0, 0)),
            pl.BlockSpec((1, 256), lambda i: (0, 0)),
            pl.BlockSpec((1, D_MODEL), lambda i: (0, 0)),
            pl.BlockSpec((D_MODEL, D_MODEL), lambda i: (0, 0)),
            pl.BlockSpec((1, D_MODEL), lambda i: (0, 0))]


def _ep_mix(first, ya_ref, yc_ref, yd_ref, gate_ref, bb_ref, bc_ref, bx_ref, hc_ref, hx_ref, cw_ref, cb_ref, gg_ref):
    tm = ya_ref.shape[0]
    u = bc_ref[...] * bx_ref[...]
    halo = jnp.where(first, 0.0, hc_ref[...] * hx_ref[...])
    row = lax.broadcasted_iota(jnp.int32, (tm, 1), 0)
    u1 = jnp.where(row == 0, halo[7:8, :], pltpu.roll(u, 1, 0))
    u2 = jnp.where(row == 0, halo[6:7, :], jnp.where(row == 1, halo[7:8, :], pltpu.roll(u, 2, 0)))
    cw = cw_ref[...]
    conv = cw[0:1, :] * u2 + cw[1:2, :] * u1 + cw[2:3, :] * u + cb_ref[...]
    bb = bb_ref[...]
    ys = [ya_ref[...], bb * conv, yc_ref[...], yd_ref[...]]
    rs = [_rms(y) for y in ys]
    gg = gg_ref[...]
    yhat = jnp.concatenate([y * r for y, r in zip(ys, rs)], axis=1)
    gate = gate_ref[...]
    sig = 1.0 / (1.0 + jnp.exp(-gate))
    return u, u1, u2, conv, bb, rs, yhat, yhat * gg, gate, sig


def _epilogue_fwd(x, ya, yc, yd, hf, conv_w, conv_b, g_grp, w_out, g_post, tgt=None):
    T = x.shape[0]
    tm = EP_TM
    row_spec = pl.BlockSpec((tm, D_MODEL), lambda i: (i, 0))

    def layer_out(refs):
        (x_ref, ya_ref, yc_ref, yd_ref, gate_ref, bb_ref, bc_ref, bx_ref, hc_ref, hx_ref, cw_ref, cb_ref,
         gg_ref, wo_ref, gp_ref) = refs
        (_, _, _, _, _, _, _, yn, gate, sig) = _ep_mix(
            pl.program_id(0) == 0, ya_ref, yc_ref, yd_ref, gate_ref, bb_ref, bc_ref, bx_ref, hc_ref, hx_ref,
            cw_ref, cb_ref, gg_ref)
        z = _dot((yn * (gate * sig)).astype(BF16), wo_ref[...])
        return x_ref[...] + z * _rms(z) * gp_ref[...]

    args = (x, ya, yc, yd, hf, hf, hf, hf, hf, hf, conv_w, conv_b, g_grp, w_out, g_post)
    in_specs = [row_spec] + _ep_in_specs(tm, None)
    if tgt is None:
        def body(*refs):
            refs[-1][...] = layer_out(refs[:-1])

        return pl.pallas_call(
            body, name="epilogue_fwd", grid=(T // tm,), in_specs=in_specs, out_specs=row_spec,
            out_shape=jax.ShapeDtypeStruct((T, D_MODEL), F32), compiler_params=_cparams(("parallel",)),
        )(*args)

    def body_loss(*refs):
        t_ref, dy_ref, l_ref = refs[-3:]

        @pl.when(pl.program_id(0) == 0)
        def _():
            l_ref[...] = jnp.zeros_like(l_ref)

        d = layer_out(refs[:-3]) - t_ref[...]
        dy_ref[...] = d * (1.0 / D_MODEL)
        part = jnp.sum(jnp.sum(d * d, axis=1, keepdims=True), axis=0, keepdims=True)
        l_ref[...] += part * (0.5 / D_MODEL)

    return pl.pallas_call(
        body_loss, name="epilogue_fwd_loss", grid=(T // tm,), in_specs=in_specs + [row_spec],
        out_specs=[row_spec, pl.BlockSpec((8, LANES), lambda i: (0, 0))],
        out_shape=[jax.ShapeDtypeStruct((T, D_MODEL), F32), jax.ShapeDtypeStruct((8, LANES), F32)],
        compiler_params=_cparams(("arbitrary",)),
    )(*args, tgt)


def _epilogue_bwd(dxn, ya, yc, yd, hf, conv_w, conv_b, g_grp, w_out, g_post):
    T = dxn.shape[0]
    tm = EP_TM
    nt = T // tm
    ridx = lambda i: (nt - 1 - i, 0)

    def body(dx_ref, ya_ref, yc_ref, yd_ref, gate_ref, bb_ref, bc_ref, bx_ref, hc_ref, hx_ref, cw_ref, cb_ref,
             gg_ref, wo_ref, gp_ref,
             dya_ref, dyc_ref, dyd_ref, dhf_ref, dwo_ref, dgp_ref, dgg_ref, dcw_ref, dcb_ref, carry_ref):
        i = pl.program_id(0)

        @pl.when(i == 0)
        def _():
            for r in (dwo_ref, dgp_ref, dgg_ref, dcw_ref, dcb_ref, carry_ref):
                r[...] = jnp.zeros_like(r)

        (u, u1, u2, conv, bb, rs, yhat, yn, gate, sig) = _ep_mix(
            i == nt - 1, ya_ref, yc_ref, yd_ref, gate_ref, bb_ref, bc_ref, bx_ref, hc_ref, hx_ref,
            cw_ref, cb_ref, gg_ref)
        silu = gate * sig
        ymix = (yn * silu).astype(BF16)
        z = _dot(ymix, wo_ref[...])
        rz = _rms(z)
        dz, dgrow = _rms_bwd(dx_ref[...], z * rz, rz, gp_ref[...])
        dgp_ref[...] += _colsum(dgrow)
        dzb = dz.astype(BF16)
        dwo_ref[...] += _dot_tn(ymix, dzb)
        dymix = _dot_nt(dzb, wo_ref[...])
        dhf_ref[:, 0:D_MODEL] = (dymix * yn * (sig * (1.0 + gate * (1.0 - sig)))).astype(BF16)
        dyn = dymix * silu
        dgg_ref[...] += _colsum(dyn * yhat)
        gg = gg_ref[...]
        dys = []
        for gi in range(4):
            sl = slice(gi * GROUP, (gi + 1) * GROUP)
            dyh = dyn[:, sl] * gg[:, sl]
            yh = yhat[:, sl]
            dys.append(rs[gi] * (dyh - yh * jnp.mean(dyh * yh, axis=-1, keepdims=True)))
        dya_ref[...] = dys[0]
        dyc_ref[...] = dys[2]
        dyd_ref[...] = dys[3]
        dyb = dys[1]
        dhf_ref[:, D_MODEL:D_MODEL + 256] = (dyb * conv).astype(BF16)
        dconv = dyb * bb
        dcb_ref[...] += _colsum(dconv)
        dcw_ref[0:1, :] += _colsum(dconv * u2)
        dcw_ref[1:2, :] += _colsum(dconv * u1)
        dcw_ref[2:3, :] += _colsum(dconv * u)
        carry = carry_ref[...]
        row = lax.broadcasted_iota(jnp.int32, (tm, 1), 0)
        d1 = jnp.where(row == tm - 1, carry[0:1, :], pltpu.roll(dconv, tm - 1, 0))
        d2 = jnp.where(row == tm - 2, carry[0:1, :],
                       jnp.where(row == tm - 1, carry[1:2, :], pltpu.roll(dconv, tm - 2, 0)))
        cw = cw_ref[...]
        du = cw[2:3, :] * dconv + cw[1:2, :] * d1 + cw[0:1, :] * d2
        dhf_ref[:, D_MODEL + 256:D_MODEL + 512] = (du * bx_ref[...]).astype(BF16)
        dhf_ref[:, D_MODEL + 512:D_MODEL + 768] = (du * bc_ref[...]).astype(BF16)
        carry_ref[...] = dconv[0:8, :]

    in_specs = [pl.BlockSpec((tm, D_MODEL), ridx)] + _ep_in_specs(tm, nt - 1)
    return pl.pallas_call(
        body, name="epilogue_bwd", grid=(nt,), in_specs=in_specs,
        out_specs=[pl.BlockSpec((tm, 256), ridx), pl.BlockSpec((tm, 256), ridx), pl.BlockSpec((tm, 256), ridx),
                   pl.BlockSpec((tm, D_MODEL + 768), ridx),
                   pl.BlockSpec((D_MODEL, D_MODEL), lambda i: (0, 0)),
                   pl.BlockSpec((1, D_MODEL), lambda i: (0, 0)),
                   pl.BlockSpec((1, D_MODEL), lambda i: (0, 0)),
                   pl.BlockSpec((8, 256), lambda i: (0, 0)),
                   pl.BlockSpec((1, 256), lambda i: (0, 0))],
        out_shape=[jax.ShapeDtypeStruct((T, 256), F32)] * 3
                  + [jax.ShapeDtypeStruct((T, D_MODEL + 768), BF16),
                     jax.ShapeDtypeStruct((D_MODEL, D_MODEL), F32),
                     jax.ShapeDtypeStruct((1, D_MODEL), F32),
                     jax.ShapeDtypeStruct((1, D_MODEL), F32),
                     jax.ShapeDtypeStruct((8, 256), F32),
                     jax.ShapeDtypeStruct((1, 256), F32)],
        scratch_shapes=[pltpu.VMEM((8, 256), F32)],
        compiler_params=_cparams(("arbitrary",)),
    )(dxn, ya, yc, yd, hf, hf, hf, hf, hf, hf, conv_w, conv_b, g_grp, w_out, g_post)


def _place():
    return lax.axis_index("x"), lax.axis_index("y"), lax.axis_index("c")


def _other_chips(x, y):
    return [(1 - x, y), (x, 1 - y), (1 - x, 1 - y)]


HBM = pl.BlockSpec(memory_space=pl.ANY)


def _gather_plan(ins, outs, sems):
    n = len(ins)
    ici_send, ici_recv, d2d_send, d2d_recv, local_sems = sems
    x, y, c = _place()
    me = 2 * x + y
    chips = _other_chips(x, y)

    def ici(a, j, chip_from):
        px, py = chips[j]
        return pltpu.make_async_remote_copy(
            src_ref=ins[a].at[c], dst_ref=outs[a].at[chip_from, c], send_sem=ici_send.at[3 * a + j],
            recv_sem=ici_recv.at[3 * a + j], device_id=(px, py, c), device_id_type=MESH)

    def d2d(a, j, part):
        px, py = chips[j]
        blk = outs[a].at[2 * px + py, part]
        return pltpu.make_async_remote_copy(
            src_ref=blk, dst_ref=blk, send_sem=d2d_send.at[3 * a + j], recv_sem=d2d_recv.at[3 * a + j],
            device_id=(x, y, 1 - c), device_id_type=MESH)

    def local(a):
        return pltpu.make_async_copy(ins[a], outs[a].at[me], local_sems.at[a])

    hops = [(j, a) for j in range(3) for a in range(n)]

    def start():
        for a in range(n):
            local(a).start()
        for j, a in hops:
            ici(a, j, me).start()

    def finish():
        for j, a in hops:
            ici(a, j, 2 * chips[j][0] + chips[j][1]).wait_recv()
            d2d(a, j, c).start()
        for j, a in hops:
            d2d(a, j, 1 - c).wait_recv()
        for j, a in hops:
            ici(a, j, me).wait_send()
            d2d(a, j, c).wait_send()
        for a in range(n):
            local(a).wait()

    return start, finish


def _gather_sems(n):
    return [pltpu.SemaphoreType.DMA((3 * n,))] * 4 + [pltpu.SemaphoreType.DMA((n,))]


def _exchange_chips(parts, small):
    n = len(parts)

    def body(*refs):
        ins, sm_ref = refs[:n], refs[n]
        outs, osm_ref = refs[n + 1:2 * n + 1], refs[2 * n + 1]
        send_sems, recv_sems, ssend_sems, srecv_sems, local_sems = refs[2 * n + 2:]
        x, y, c = _place()
        me = 2 * x + y
        dev = 4 * x + 2 * y + c
        local = [pltpu.make_async_copy(ins[a].at[me], outs[a].at[me], local_sems.at[a]) for a in range(n)]
        local.append(pltpu.make_async_copy(sm_ref, osm_ref.at[dev], local_sems.at[n]))
        for cp in local:
            cp.start()
        sends = []
        for j, (px, py) in enumerate(_other_chips(x, y)):
            for a in range(n):
                cp = pltpu.make_async_remote_copy(
                    src_ref=ins[a].at[2 * px + py], dst_ref=outs[a].at[me], send_sem=send_sems.at[3 * a + j],
                    recv_sem=recv_sems.at[3 * a + j], device_id=(px, py, c), device_id_type=MESH)
                cp.start()
                sends.append(cp)
        flips = [(fx, fy, fc) for fx in (0, 1) for fy in (0, 1) for fc in (0, 1)][1:]
        for j, (fx, fy, fc) in enumerate(flips):
            cp = pltpu.make_async_remote_copy(
                src_ref=sm_ref, dst_ref=osm_ref.at[dev], send_sem=ssend_sems.at[j], recv_sem=srecv_sems.at[j],
                device_id=(x ^ fx, y ^ fy, c ^ fc), device_id_type=MESH)
            cp.start()
            sends.append(cp)
        for j, (px, py) in enumerate(_other_chips(x, y)):
            for a in range(n):
                pltpu.make_async_remote_copy(
                    src_ref=ins[a].at[me], dst_ref=outs[a].at[2 * px + py], send_sem=send_sems.at[3 * a + j],
                    recv_sem=recv_sems.at[3 * a + j], device_id=(px, py, c), device_id_type=MESH).wait_recv()
        for j, (fx, fy, fc) in enumerate(flips):
            src = 4 * (x ^ fx) + 2 * (y ^ fy) + (c ^ fc)
            pltpu.make_async_remote_copy(
                src_ref=sm_ref, dst_ref=osm_ref.at[src], send_sem=ssend_sems.at[j], recv_sem=srecv_sems.at[j],
                device_id=(x ^ fx, y ^ fy, c ^ fc), device_id_type=MESH).wait_recv()
        for cp in sends:
            cp.wait_send()
        for cp in local:
            cp.wait()

    return pl.pallas_call(
        body, name="exchange_chips",
        in_specs=[HBM] * (n + 1), out_specs=[HBM] * (n + 1),
        out_shape=[jax.ShapeDtypeStruct(p.shape, p.dtype) for p in parts]
                  + [jax.ShapeDtypeStruct((8,) + small.shape, small.dtype)],
        scratch_shapes=[pltpu.SemaphoreType.DMA((3 * n,)), pltpu.SemaphoreType.DMA((3 * n,)),
                        pltpu.SemaphoreType.DMA((7,)), pltpu.SemaphoreType.DMA((7,)),
                        pltpu.SemaphoreType.DMA((n + 1,))],
    )(*parts, small)


def _swap_cores(parts, name):
    n = len(parts)

    def body(*refs):
        ins, outs, send_sems, recv_sems = refs[:n], refs[n:2 * n], refs[2 * n], refs[2 * n + 1]
        x, y, c = _place()
        copies = [pltpu.make_async_remote_copy(
            src_ref=ins[a], dst_ref=outs[a], send_sem=send_sems.at[a], recv_sem=recv_sems.at[a],
            device_id=(x, y, 1 - c), device_id_type=MESH) for a in range(n)]
        for cp in copies:
            cp.start()
        for cp in copies:
            cp.wait()

    return pl.pallas_call(
        body, name=name, in_specs=[HBM] * n, out_specs=[HBM] * n,
        out_shape=[jax.ShapeDtypeStruct(p.shape, p.dtype) for p in parts],
        scratch_shapes=[pltpu.SemaphoreType.DMA((n,)), pltpu.SemaphoreType.DMA((n,))],
    )(*parts)


def _tile(rows, cols):
    for cand in (256, 128, 64):
        if rows % cand == 0:
            return cand, cols
    if rows > 64 and cols % 256 == 0:
        return rows, 256
    return rows, cols


def _swap_layer_chunks(per_layer):
    n = len(per_layer[0])

    def body(*refs):
        ins, outs, send_sems, recv_sems = (refs[:n], refs[n:2 * n]), refs[2 * n:3 * n], refs[3 * n], refs[3 * n + 1]
        x, y, c = _place()
        for l in range(2):
            @pl.when(c == 1 - l)
            def _():
                copies = [pltpu.make_async_remote_copy(
                    src_ref=ins[l][a], dst_ref=outs[a], send_sem=send_sems.at[a], recv_sem=recv_sems.at[a],
                    device_id=(x, y, 1 - c), device_id_type=MESH) for a in range(n)]
                for cp in copies:
                    cp.start()
                for cp in copies:
                    cp.wait()

    return pl.pallas_call(
        body, name="swap_layer_chunks", in_specs=[HBM] * (2 * n), out_specs=[HBM] * n,
        out_shape=[jax.ShapeDtypeStruct(p.shape, p.dtype) for p in per_layer[0]],
        scratch_shapes=[pltpu.SemaphoreType.DMA((n,)), pltpu.SemaphoreType.DMA((n,))],
    )(*per_layer[0], *per_layer[1])


def _add(a0, a1, b, name):
    L, R, C = b.shape
    tr, tc = _tile(R, C)

    def body(a0_ref, a1_ref, b_ref, o_ref):
        mine = jnp.where(lax.axis_index("c") == 0, a0_ref[...], a1_ref[...])
        o_ref[...] = (mine + b_ref[...]).astype(BF16)

    spec = pl.BlockSpec((1, tr, tc), lambda l, i, j: (l, i, j))
    return pl.pallas_call(
        body, name=name, grid=(L, R // tr, C // tc), in_specs=[spec] * 3, out_specs=spec,
        out_shape=jax.ShapeDtypeStruct((L, R, C), BF16),
        compiler_params=_cparams(("parallel", "parallel", "parallel")),
    )(a0, a1, b)


def _sum_leading(buf, name):
    n, R, C = buf.shape
    tr, tc = _tile(R, C)

    def body(b_ref, o_ref):
        acc = b_ref[0].astype(F32)
        for k in range(1, n):
            acc = acc + b_ref[k].astype(F32)
        o_ref[...] = acc

    return pl.pallas_call(
        body, name=name, grid=(R // tr, C // tc),
        in_specs=[pl.BlockSpec((n, tr, tc), lambda i, j: (0, i, j))],
        out_specs=pl.BlockSpec((tr, tc), lambda i, j: (i, j)),
        out_shape=jax.ShapeDtypeStruct((R, C), F32),
        compiler_params=_cparams(("parallel", "parallel")),
    )(buf)


def _adam_update(w, g, m, v):
    c1 = 1.0 / (1.0 - ADAM_B1 ** ADAM_STEP)
    c2 = 1.0 / (1.0 - ADAM_B2 ** ADAM_STEP)
    mn = ADAM_B1 * m + (1.0 - ADAM_B1) * g
    vn = ADAM_B2 * v + (1.0 - ADAM_B2) * (g * g)
    return -ADAM_LR * ((mn * c1) / (jnp.sqrt(vn * c2) + ADAM_EPS) + ADAM_WD * w), mn, vn


def _adamw_layers(w, m, v, g_mine, g_other, name):
    _, R, C = w.shape
    tr, tc = _tile(R, C)

    def body(w_ref, m_ref, v_ref, gm_ref, go_ref, g_ref, d_ref, mo_ref, vo_ref):
        g = jnp.where(pl.program_id(0) == lax.axis_index("c"), gm_ref[...], go_ref[...])
        g_ref[0] = g
        d_ref[0], mo_ref[0], vo_ref[0] = _adam_update(w_ref[0], g, m_ref[0], v_ref[0])

    spec3 = pl.BlockSpec((1, tr, tc), lambda l, i, j: (l, i, j))
    spec2 = pl.BlockSpec((tr, tc), lambda l, i, j: (i, j))
    return pl.pallas_call(
        body, name=name, grid=(2, R // tr, C // tc),
        in_specs=[spec3] * 3 + [spec2] * 2, out_specs=[spec3] * 4,
        out_shape=[jax.ShapeDtypeStruct(w.shape, F32)] * 4,
        compiler_params=_cparams(("parallel", "parallel", "parallel")),
    )(w, m, v, g_mine, g_other)


PACK_C = 1024
_BIG = ("w_in", "w_out", "mla_w_uq", "mla_w_ukv", "conv_w")
_SMALL = ("norm_pre", "group_norm", "norm_post", "conv_b", "mla_q_norm", "mla_kv_norm", "attn_sinks")
_SMALL_W = {"norm_pre": 1024, "group_norm": 1024, "norm_post": 1024, "conv_b": 256, "mla_q_norm": 256,
            "mla_kv_norm": 128, "attn_sinks": 4}


_LOSS_AT = divmod(DEPTH * sum(_SMALL_W.values()), PACK_C)


def _pack_small(d, loss):
    flat = jnp.concatenate([d[n].reshape(-1) for n in _SMALL] + [loss.reshape(1)])
    return jnp.pad(flat, (0, 8 * PACK_C - flat.shape[0])).reshape(8, PACK_C)


def _adamw_small(w, m, v, got):
    ns = len(_SMALL)

    def body(*refs):
        got_ref = refs[3 * ns]
        outs = refs[3 * ns + 1:]
        gsum = got_ref[0]
        for d in range(1, 8):
            gsum = gsum + got_ref[d]
        outs[4 * ns][...] = gsum[_LOSS_AT[0]:_LOSS_AT[0] + 1, _LOSS_AT[1]:_LOSS_AT[1] + 1]
        off = 0
        for i, name in enumerate(_SMALL):
            wd = _SMALL_W[name]
            rows = []
            for l in range(DEPTH):
                r, c0 = divmod(off + l * wd, PACK_C)
                rows.append(gsum[r:r + 1, c0:c0 + wd])
            off += DEPTH * wd
            g = jnp.concatenate(rows, axis=0)
            delta, mn, vn = _adam_update(refs[i][...], g, refs[ns + i][...], refs[2 * ns + i][...])
            outs[i][...] = g
            outs[ns + i][...] = delta
            outs[2 * ns + i][...] = mn
            outs[3 * ns + i][...] = vn

    shapes = [jax.ShapeDtypeStruct(w[n].shape, F32) for n in _SMALL]
    res = pl.pallas_call(body, name="adamw_small", out_shape=shapes * 4 + [jax.ShapeDtypeStruct((1, 1), F32)])(
        *[w[n] for n in _SMALL], *[m[n] for n in _SMALL], *[v[n] for n in _SMALL], got)
    return [dict(zip(_SMALL, res[k * ns:(k + 1) * ns])) for k in range(4)], res[4 * ns]


def _w_in_internal(slabs):
    S, R, D = slabs.shape
    assert S * R == D_IN

    def body(w_ref, o_ref):
        for n, wd in _REAL:
            o, oi = _REAL_OFF[n][0], _INT_OFF[n]
            r = o
            while r < o + wd:
                end = min(o + wd, (r // R + 1) * R)
                o_ref[oi + r - o:oi + end - o, :] = w_ref[r // R, r % R:r % R + end - r, :]
                r = end
            if _INT_W[n] != wd:
                o_ref[oi + wd:oi + _INT_W[n], :] = jnp.zeros((_INT_W[n] - wd, D), slabs.dtype)

    return pl.pallas_call(
        body, name="w_in_internal", out_shape=jax.ShapeDtypeStruct((N_INT, D), slabs.dtype),
        compiler_params=pltpu.CompilerParams(vmem_limit_bytes=VMEM_LIMIT),
    )(slabs)


def _uq_internal(w):
    return jnp.pad(w.reshape(256, 4, 96), ((0, 0), (0, 0), (0, 32))).reshape(256, 512)


def _uq_real(dw):
    return dw.reshape(256, 4, 128)[:, :, :96].reshape(256, 384)


def _ukv_internal(w):
    w4 = w.reshape(128, 4, 128)
    k = jnp.pad(w4[:, :, :64], ((0, 0), (0, 0), (0, 64))).reshape(128, 512)
    return jnp.concatenate([k, w4[:, :, 64:].reshape(128, 256)], axis=1)


def _ukv_real(dw):
    k = dw[:, :512].reshape(128, 4, 128)[:, :, :64]
    v = dw[:, 512:].reshape(128, 4, 64)
    return jnp.concatenate([k, v], axis=2).reshape(128, 512)


def _layer_fwd(x, rope, p, tgt=None, fetch=()):
    xn, hb, hf = _inproj_fwd(x, p["norm_pre"], p["w_in"])
    ya = _swa_fwd(hb, p["attn_sinks"])
    qm, km, vm, vt = _mla_prep_fwd(hf, rope, p["mla_q_norm"], p["mla_kv_norm"], p["mla_w_uq"], p["mla_w_ukv"])
    yc, lse, *fetched = _mla_fwd(qm, km, vt, fetch)
    yd, tot, cnt = _sb_fwd(hb)
    x_next = _epilogue_fwd(x, ya, yc, yd, hf, p["conv_w"], p["conv_b"], p["group_norm"], p["w_out"], p["norm_post"],
                           tgt)
    saved = dict(x=x, xn=xn, hb=hb, hf=hf, ya=ya, yc=yc, yd=yd, tot=tot, cnt=cnt, qm=qm, km=km, vm=vm, lse=lse)
    return x_next, saved, fetched


def _layer_bwd(dx_next, rope, p, s):
    (dya, dyc, dyd, dhf, dw_out, dg_post, dg_grp, dconv_w, dconv_b) = _epilogue_bwd(
        dx_next, s["ya"], s["yc"], s["yd"], s["hf"], p["conv_w"], p["conv_b"], p["group_norm"], p["w_out"],
        p["norm_post"])
    dq_d, dk_d, dv_d = _sb_bwd(s["hb"], s["tot"], s["cnt"], dyd)
    dqm, dkt, dvt = _mla_bwd(s["qm"], s["km"], s["vm"], s["yc"], s["lse"], dyc)
    dc, dw_uq, dw_ukv, dg_q, dg_kv = _mla_prep_bwd(
        s["hf"], rope, p["mla_q_norm"], p["mla_kv_norm"], p["mla_w_uq"], p["mla_w_ukv"], dqm, dkt, dvt)
    dq_a, dk_a, dv_a, dsinks = _swa_bwd(s["hb"], p["attn_sinks"], dya)
    dx, dh, dg_pre = _inproj_bwd_dx(s["x"], p["norm_pre"], p["w_in"], dx_next,
                                    [dq_a, dk_a, dv_a, dq_d, dk_d, dv_d, dhf, dc])
    grads = dict(norm_pre=dg_pre[0], w_in_t=_inproj_bwd_dw(s["xn"], dh), attn_sinks=dsinks[0, :4], conv_w=dconv_w[:3],
                 conv_b=dconv_b[0], mla_q_norm=dg_q[0], mla_w_uq=_uq_real(dw_uq), mla_kv_norm=dg_kv[0],
                 mla_w_ukv=_ukv_real(dw_ukv), group_norm=dg_grp[0], w_out=dw_out, norm_post=dg_post[0])
    return dx, grads


_WEIGHTS = ["norm_pre", "w_in", "attn_sinks", "conv_w", "conv_b", "mla_q_norm", "mla_w_uq", "mla_kv_norm",
            "mla_w_ukv", "group_norm", "w_out", "norm_post"]


def kernel(x, positions, norm_pre, w_in, attn_sinks, conv_w, conv_b, mla_q_norm, mla_w_uq, mla_kv_norm, mla_w_ukv, group_norm, w_out, norm_post, loss_target, m_norm_pre, m_w_in, m_attn_sinks, m_conv_w, m_conv_b, m_mla_q_norm, m_mla_w_uq, m_mla_kv_norm, m_mla_w_ukv, m_group_norm, m_w_out, m_norm_post, v_norm_pre, v_w_in, v_attn_sinks, v_conv_w, v_conv_b, v_mla_q_norm, v_mla_w_uq, v_mla_kv_norm, v_mla_w_ukv, v_group_norm, v_w_out, v_norm_post):
    w = dict(norm_pre=norm_pre, w_in=w_in, attn_sinks=attn_sinks, conv_w=conv_w, conv_b=conv_b,
             mla_q_norm=mla_q_norm, mla_w_uq=mla_w_uq, mla_kv_norm=mla_kv_norm, mla_w_ukv=mla_w_ukv,
             group_norm=group_norm, w_out=w_out, norm_post=norm_post)
    m = dict(norm_pre=m_norm_pre, w_in=m_w_in, attn_sinks=m_attn_sinks, conv_w=m_conv_w, conv_b=m_conv_b,
             mla_q_norm=m_mla_q_norm, mla_w_uq=m_mla_w_uq, mla_kv_norm=m_mla_kv_norm, mla_w_ukv=m_mla_w_ukv,
             group_norm=m_group_norm, w_out=m_w_out, norm_post=m_norm_post)
    v = dict(norm_pre=v_norm_pre, w_in=v_w_in, attn_sinks=v_attn_sinks, conv_w=v_conv_w, conv_b=v_conv_b,
             mla_q_norm=v_mla_q_norm, mla_w_uq=v_mla_w_uq, mla_kv_norm=v_mla_kv_norm, mla_w_ukv=v_mla_w_ukv,
             group_norm=v_group_norm, w_out=v_w_out, norm_post=v_norm_post)
    T = x.shape[1]
    xs = x[0]
    tgt = loss_target[0]

    def shard_parts(l):
        halves = lambda a: a.reshape((2, a.shape[0] // 2) + a.shape[1:])
        return [halves(jnp.swapaxes(w["w_in"][l], 0, 1).astype(BF16))] + [
            halves(w[n][l].astype(BF16)) for n in _BIG[1:4]] + [jnp.stack([w["conv_w"][l]] * 2)]

    def layer_params(l, got):
        whole = lambda a: a.reshape((4, 2 * a.shape[2]) + a.shape[3:])
        by_cols = lambda a: jnp.transpose(a, (1, 0, 2)).reshape(a.shape[1], 4 * a.shape[2])
        return dict(
            norm_pre=norm_pre[l:l + 1], w_in=_w_in_internal(got[0].reshape((8,) + got[0].shape[2:])),
            attn_sinks=attn_sinks[l], conv_w=by_cols(got[4][:, 0]), conv_b=conv_b[l:l + 1],
            mla_q_norm=mla_q_norm[l:l + 1], mla_w_uq=_uq_internal(by_cols(whole(got[2]))),
            mla_kv_norm=mla_kv_norm[l:l + 1], mla_w_ukv=_ukv_internal(by_cols(whole(got[3]))),
            group_norm=group_norm[l:l + 1], w_out=whole(got[1]).reshape(D_MODEL, D_MODEL),
            norm_post=norm_post[l:l + 1])

    layers, saved = [], []
    rope, *got = _rope_tables(positions[0].reshape(T, 1), shard_parts(0))
    h = xs
    for l in range(DEPTH):
        last = l == DEPTH - 1
        layers.append(layer_params(l, got))
        h, s, got = _layer_fwd(h, rope, layers[l], tgt if last else None, () if last else shard_parts(l + 1))
        saved.append(s)
    dy, loss_part = h

    grads = [None] * DEPTH
    for l in reversed(range(DEPTH)):
        dy, grads[l] = _layer_bwd(dy, rope, layers[l], saved[l])

    turned = ("w_in", "mla_w_uq")
    turn = lambda n, a: jnp.swapaxes(a, -1, -2) if n in turned else a

    def chunks(n, a):
        if n in ("w_out", "w_in"):
            return a.reshape(4, a.shape[0] // 4, a.shape[1])
        if n in turned:
            return a.T.reshape(4, a.shape[1] // 4, a.shape[0])
        return jnp.transpose(a.reshape(a.shape[0], 4, a.shape[1] // 4), (1, 0, 2))

    grad = lambda l, n: grads[l]["w_in_t" if n == "w_in" else n]
    per_layer = [[chunks(n, grad(l, n)) for n in _BIG] for l in range(DEPTH)]
    from_sibling = _swap_layer_chunks(per_layer)
    summed = [_add(a0, a1, b, "add_cores_" + n) for n, a0, a1, b in zip(_BIG, *per_layer, from_sibling)]
    small = _pack_small({n: jnp.stack([grads[l][n] for l in range(DEPTH)]) for n in _SMALL}, loss_part[0, 0])
    *got, got_small = _exchange_chips(summed, small)
    done = [_sum_leading(b, "sum_chips_" + n) for n, b in zip(_BIG, got)]
    done_other = _swap_cores(done, "swap_layer_shards")

    outs, loss = _adamw_small(w, m, v, got_small)
    for n, gm, go in zip(_BIG, done, done_other):
        for d, a in zip(outs, _adamw_layers(turn(n, w[n]), turn(n, m[n]), turn(n, v[n]), gm, go, "adamw_" + n)):
            d[n] = turn(n, a)
    return (loss[0, 0], dy[None], *[outs[0][n] for n in _WEIGHTS], *[outs[1][n] for n in _WEIGHTS],
            *[outs[2][n] for n in _WEIGHTS], *[outs[3][n] for n in _WEIGHTS])
```

```python
import math

import jax
import jax.numpy as jnp
from jax import lax
from jax.experimental import pallas as pl
from jax.experimental.pallas import tpu as pltpu

F32 = jnp.float32
BF16 = jnp.bfloat16
MESH = pl.DeviceIdType.MESH

D_MODEL = 1024
DEPTH = 2
EPS = 1e-6
BLOCK = 128
HEAD = 64
LANES = 128
GROUP = 256
LOG2E = 1.4426950408889634
LN2 = 0.6931471805599453
MLA_QSCALE = 96 ** -0.5 * LOG2E
ROPE_HALF = 16
ROPE_THETA = 10000.0
SWA_SUB = 2
ATT_BLK = 256
MLA_BQ = 512
NEG = -1e30
SB_DEAD = -104.0

ADAM_LR, ADAM_B1, ADAM_B2, ADAM_EPS, ADAM_WD, ADAM_STEP = 0.001, 0.9, 0.999, 1e-08, 0.01, 10

_REAL = [("a_q", 256), ("a_k", 128), ("a_v", 128), ("b_b", 256), ("b_c", 256), ("b_x", 256),
         ("c_q", 256), ("c_kv", 128), ("c_kr", 32), ("d_q", 256), ("d_k", 256), ("d_v", 256),
         ("gate", 1024)]
_REAL_OFF = {}
_o = 0
for _n, _w in _REAL:
    _REAL_OFF[_n] = (_o, _w)
    _o += _w
D_IN = _o
_INT_ORDER = ["a_q", "a_k", "a_v", "d_q", "d_k", "d_v", "gate", "b_b", "b_c", "b_x", "c_q", "c_kv", "c_kr"]
_INT_W = dict(_REAL)
_INT_W["c_kr"] = 128
_INT_OFF = {}
_o = 0
for _n in _INT_ORDER:
    _INT_OFF[_n] = _o
    _o += _INT_W[_n]
N_INT = _o
N_HB = _INT_OFF["gate"]
N_HF = N_INT - N_HB

VMEM_LIMIT = 56 * 1024 * 1024


def _cparams(sem):
    return pltpu.CompilerParams(dimension_semantics=sem, vmem_limit_bytes=VMEM_LIMIT)


def _dot(a, b):
    return jnp.dot(a, b, preferred_element_type=F32)


def _dot_nt(a, b):
    return lax.dot_general(a, b, (((1,), (1,)), ((), ())), preferred_element_type=F32)


def _dot_tn(a, b):
    return lax.dot_general(a, b, (((0,), (0,)), ((), ())), preferred_element_type=F32)


def _split(x):
    hi = x.astype(BF16)
    lo = (x - hi.astype(F32)).astype(BF16)
    return hi, lo


def _rms(x):
    return lax.rsqrt(jnp.mean(x * x, axis=-1, keepdims=True) + EPS)


def _rms_bwd(dy, xhat, r, g):
    dxhat = dy * g
    return r * (dxhat - xhat * jnp.mean(dxhat * xhat, axis=-1, keepdims=True)), dy * xhat


def _colsum(x):
    return jnp.sum(x, axis=0, keepdims=True)


def _inproj_fwd(x, g, wt):
    T = x.shape[0]
    tm = 512

    def body(x_ref, g_ref, w_ref, xn_ref, hb_ref, hf_ref):
        xv = x_ref[...]
        xn = (xv * _rms(xv) * g_ref[...]).astype(BF16)
        xn_ref[...] = xn
        h = _dot_nt(xn, w_ref[...])
        hb_ref[...] = h[:, :N_HB].astype(BF16)
        hf_ref[...] = h[:, N_HB:]

    return pl.pallas_call(
        body, name="inproj_fwd", grid=(T // tm,),
        in_specs=[pl.BlockSpec((tm, D_MODEL), lambda i: (i, 0)),
                  pl.BlockSpec((1, D_MODEL), lambda i: (0, 0)),
                  pl.BlockSpec((N_INT, D_MODEL), lambda i: (0, 0))],
        out_specs=[pl.BlockSpec((tm, D_MODEL), lambda i: (i, 0)),
                   pl.BlockSpec((tm, N_HB), lambda i: (i, 0)),
                   pl.BlockSpec((tm, N_HF), lambda i: (i, 0))],
        out_shape=[jax.ShapeDtypeStruct((T, D_MODEL), BF16),
                   jax.ShapeDtypeStruct((T, N_HB), BF16),
                   jax.ShapeDtypeStruct((T, N_HF), F32)],
        compiler_params=_cparams(("parallel",)),
    )(x, g, wt)


def _inproj_bwd_dx(x, g, wt, dx_next, pieces):
    T = x.shape[0]
    tm = 512
    widths = [p.shape[1] for p in pieces]
    assert sum(widths) == N_INT

    def body(x_ref, g_ref, w_ref, dxn_ref, *rest):
        p_refs = rest[:len(pieces)]
        dx_ref, dh_ref, dg_ref = rest[len(pieces):]
        dh = jnp.concatenate([p[...].astype(BF16) for p in p_refs], axis=1)
        dh_ref[...] = dh
        dxn = _dot(dh, w_ref[...])
        xv = x_ref[...]
        r = _rms(xv)
        dx, dgrow = _rms_bwd(dxn, xv * r, r, g_ref[...])
        dx_ref[...] = dx + dxn_ref[...]

        @pl.when(pl.program_id(0) == 0)
        def _():
            dg_ref[...] = jnp.zeros_like(dg_ref)

        dg_ref[...] += _colsum(dgrow)

    return pl.pallas_call(
        body, name="inproj_bwd_dx", grid=(T // tm,),
        in_specs=[pl.BlockSpec((tm, D_MODEL), lambda i: (i, 0)),
                  pl.BlockSpec((1, D_MODEL), lambda i: (0, 0)),
                  pl.BlockSpec((N_INT, D_MODEL), lambda i: (0, 0)),
                  pl.BlockSpec((tm, D_MODEL), lambda i: (i, 0))]
                 + [pl.BlockSpec((tm, wd), lambda i: (i, 0)) for wd in widths],
        out_specs=[pl.BlockSpec((tm, D_MODEL), lambda i: (i, 0)),
                   pl.BlockSpec((tm, N_INT), lambda i: (i, 0)),
                   pl.BlockSpec((1, D_MODEL), lambda i: (0, 0))],
        out_shape=[jax.ShapeDtypeStruct((T, D_MODEL), F32),
                   jax.ShapeDtypeStruct((T, N_INT), BF16),
                   jax.ShapeDtypeStruct((1, D_MODEL), F32)],
        compiler_params=_cparams(("arbitrary",)),
    )(x, g, wt, dx_next, *pieces)


def _inproj_bwd_dw(xn, dh):
    T = xn.shape[0]
    tm = min(512, T)

    def body(a_ref, b_ref, o_ref):
        @pl.when(pl.program_id(0) == 0)
        def _():
            o_ref[...] = jnp.zeros_like(o_ref)

        for n, wd in _REAL:
            o, oi = _REAL_OFF[n][0], _INT_OFF[n]
            o_ref[o:o + wd, :] += _dot_tn(b_ref[:, oi:oi + _INT_W[n]], a_ref[...])[:wd]

    return pl.pallas_call(
        body, name="inproj_bwd_dw", grid=(T // tm,),
        in_specs=[pl.BlockSpec((tm, D_MODEL), lambda t: (t, 0)),
                  pl.BlockSpec((tm, N_INT), lambda t: (t, 0))],
        out_specs=pl.BlockSpec((D_IN, D_MODEL), lambda t: (0, 0)),
        out_shape=jax.ShapeDtypeStruct((D_IN, D_MODEL), F32),
        compiler_params=_cparams(("arbitrary",)),
    )(xn, dh)


def _roll_f32(x, shift):
    return pltpu.roll(x.astype(F32), shift, 1)


def _swa_operands(h, q, k_prev, k_cur, v_prev, v_cur):
    p, e = h // 2, h % 2
    lane = lax.broadcasted_iota(jnp.int32, (1, LANES), 1) // HEAD
    q = q[:, p * LANES:(p + 1) * LANES]
    if e != p:
        q = _roll_f32(q, HEAD).astype(BF16)
        v_prev = _roll_f32(v_prev, HEAD).astype(BF16)
        v_cur = _roll_f32(v_cur, HEAD).astype(BF16)
    qs = jnp.where(lane == p, q, 0) * 0.125
    return dict(p=p, e=e, lane=lane, qs=qs, k_prev=k_prev, k_cur=k_cur,
                v_prev=jnp.where(lane == e, v_prev, 0), v_cur=jnp.where(lane == e, v_cur, 0),
                s_prev=_dot_nt(qs, k_prev), s_cur=_dot_nt(qs, k_cur))


def _swa_probs(ops, sink, no_prev):
    row = lax.broadcasted_iota(jnp.int32, (BLOCK, BLOCK), 0)
    col = lax.broadcasted_iota(jnp.int32, (BLOCK, BLOCK), 1)
    ok_prev = col > row if no_prev is None else jnp.logical_and(col > row, jnp.logical_not(no_prev))
    s_prev = jnp.where(ok_prev, ops["s_prev"], NEG)
    s_cur = jnp.where(col <= row, ops["s_cur"], NEG)
    m = jnp.maximum(jnp.maximum(jnp.max(s_prev, axis=1, keepdims=True),
                                jnp.max(s_cur, axis=1, keepdims=True)), sink)
    p_prev = jnp.exp(s_prev - m)
    p_cur = jnp.exp(s_cur - m)
    p_sink = jnp.exp(sink - m)
    inv = 1.0 / (jnp.sum(p_prev, axis=1, keepdims=True) + jnp.sum(p_cur, axis=1, keepdims=True) + p_sink)
    return p_prev * inv, p_cur * inv, p_sink * inv


def _swa_specs(T):
    n = T // (BLOCK * SWA_SUB)
    qo, ko, vo = (_INT_OFF[name] // LANES for name in ("a_q", "a_k", "a_v"))
    halo = lambda i: jnp.maximum(i * SWA_SUB - 1, 0)
    return [pl.BlockSpec((BLOCK * SWA_SUB, 256), lambda i: (i, qo // 2)),
            pl.BlockSpec((BLOCK, LANES), lambda i: (halo(i), ko)),
            pl.BlockSpec((BLOCK * SWA_SUB, LANES), lambda i: (i, ko)),
            pl.BlockSpec((BLOCK, LANES), lambda i: (halo(i), vo)),
            pl.BlockSpec((BLOCK * SWA_SUB, LANES), lambda i: (i, vo)),
            pl.BlockSpec(memory_space=pltpu.SMEM)], n


def _swa_units(q_ref, kh_ref, kc_ref, vh_ref, vc_ref, s_ref):
    blk = lambda a: slice(a * BLOCK, (a + 1) * BLOCK)
    units = [(a, h) for a in range(SWA_SUB) for h in range(4)]
    ops = {}
    for a, h in units:
        k_prev, v_prev = (kh_ref[...], vh_ref[...]) if a == 0 else (kc_ref[blk(a - 1), :], vc_ref[blk(a - 1), :])
        ops[a, h] = _swa_operands(h, q_ref[blk(a), :], k_prev, kc_ref[blk(a), :], v_prev, vc_ref[blk(a), :])
    probs = {(a, h): _swa_probs(ops[a, h], s_ref[h], pl.program_id(0) == 0 if a == 0 else None) for a, h in units}
    return units, ops, probs, blk


def _swa_fwd(hb, sinks):
    T = hb.shape[0]
    specs, n = _swa_specs(T)

    def body(q_ref, kh_ref, kc_ref, vh_ref, vc_ref, s_ref, o_ref):
        units, ops, probs, blk = _swa_units(q_ref, kh_ref, kc_ref, vh_ref, vc_ref, s_ref)
        outs = {u: _dot(probs[u][0].astype(BF16), ops[u]["v_prev"]) + _dot(probs[u][1].astype(BF16), ops[u]["v_cur"])
                for u in units}
        for a in range(SWA_SUB):
            for p in range(2):
                o_ref[blk(a), p * LANES:(p + 1) * LANES] = outs[a, 2 * p] + outs[a, 2 * p + 1]

    return pl.pallas_call(
        body, name="swa_fwd", grid=(n,), in_specs=specs,
        out_specs=pl.BlockSpec((BLOCK * SWA_SUB, 256), lambda i: (i, 0)),
        out_shape=jax.ShapeDtypeStruct((T, 256), F32),
        compiler_params=_cparams(("parallel",)),
    )(hb, hb, hb, hb, hb, sinks)


def _swa_bwd(hb, sinks, dy):
    T = hb.shape[0]
    specs, n = _swa_specs(T)

    def body(q_ref, kh_ref, kc_ref, vh_ref, vc_ref, s_ref, dy_ref, dq_ref, dk_ref, dv_ref, ds_ref):
        i = pl.program_id(0)

        @pl.when(i == 0)
        def _():
            ds_ref[...] = jnp.zeros_like(ds_ref)

        lane_id = lax.broadcasted_iota(jnp.int32, (8, LANES), 1)
        units, ops, probs, blk = _swa_units(q_ref, kh_ref, kc_ref, vh_ref, vc_ref, s_ref)
        dos = {(a, h): jnp.where(ops[a, h]["lane"] == ops[a, h]["e"],
                                 dy_ref[blk(a), ops[a, h]["p"] * LANES:(ops[a, h]["p"] + 1) * LANES], 0.0)
               for a, h in units}
        dobs = {u: dos[u].astype(BF16) for u in units}
        pbs = {u: (probs[u][0].astype(BF16), probs[u][1].astype(BF16)) for u in units}
        outs = {u: _dot(pbs[u][0], ops[u]["v_prev"]) + _dot(pbs[u][1], ops[u]["v_cur"]) for u in units}
        dps = {u: (_dot_nt(dobs[u], ops[u]["v_prev"]), _dot_nt(dobs[u], ops[u]["v_cur"])) for u in units}
        dss, dsinks = {}, jnp.zeros((8, LANES), F32)
        for u in units:
            delta = jnp.sum(dos[u] * outs[u], axis=1, keepdims=True)
            dss[u] = ((probs[u][0] * (dps[u][0] - delta)).astype(BF16),
                      (probs[u][1] * (dps[u][1] - delta)).astype(BF16))
            dsink = -jnp.sum(probs[u][2] * delta, axis=0, keepdims=True)
            dsinks += jnp.where(lane_id == u[1], dsink, 0.0)
        ds_ref[...] += dsinks
        dqs = {u: (_dot(dss[u][0], ops[u]["k_prev"]) + _dot(dss[u][1], ops[u]["k_cur"])) * 0.125 for u in units}
        zero = jnp.zeros((BLOCK, LANES), F32)
        dk_as_prev, dk_as_cur = [zero] * SWA_SUB, [zero] * SWA_SUB
        dv_as_prev, dv_as_cur = [zero] * SWA_SUB, [zero] * SWA_SUB
        for a, h in units:
            p, e = ops[a, h]["p"], ops[a, h]["e"]
            dob_v = dobs[a, h] if e == p else pltpu.roll(dos[a, h], HEAD, 1).astype(BF16)
            dk_as_prev[a] = dk_as_prev[a] + _dot_tn(dss[a, h][0], ops[a, h]["qs"])
            dk_as_cur[a] = dk_as_cur[a] + _dot_tn(dss[a, h][1], ops[a, h]["qs"])
            dv_as_prev[a] = dv_as_prev[a] + _dot_tn(pbs[a, h][0], dob_v)
            dv_as_cur[a] = dv_as_cur[a] + _dot_tn(pbs[a, h][1], dob_v)
        base = i * SWA_SUB
        for a in range(SWA_SUB):
            rows = pl.ds(pl.multiple_of((base + a) * BLOCK, BLOCK), BLOCK)
            more = a + 1 < SWA_SUB
            dk_ref[rows, :] = dk_as_cur[a] + (dk_as_prev[a + 1] if more else 0.0)
            dv_ref[rows, :] = dv_as_cur[a] + (dv_as_prev[a + 1] if more else 0.0)
        halo = pl.ds(pl.multiple_of(jnp.maximum(base - 1, 0) * BLOCK, BLOCK), BLOCK)
        dk_ref[halo, :] += dk_as_prev[0]
        dv_ref[halo, :] += dv_as_prev[0]
        for a in range(SWA_SUB):
            for p in range(2):
                dq_pair = jnp.zeros((BLOCK, LANES), F32)
                for e in range(2):
                    dq = jnp.where(ops[a, 2 * p + e]["lane"] == p, dqs[a, 2 * p + e], 0.0)
                    dq_pair += dq if e == p else pltpu.roll(dq, HEAD, 1)
                dq_ref[blk(a), p * LANES:(p + 1) * LANES] = dq_pair.astype(BF16)

    return pl.pallas_call(
        body, name="swa_bwd", grid=(n,),
        in_specs=specs + [pl.BlockSpec((BLOCK * SWA_SUB, 256), lambda i: (i, 0))],
        out_specs=[pl.BlockSpec((BLOCK * SWA_SUB, 256), lambda i: (i, 0)),
                   pl.BlockSpec((T, LANES), lambda i: (0, 0)),
                   pl.BlockSpec((T, LANES), lambda i: (0, 0)),
                   pl.BlockSpec((8, LANES), lambda i: (0, 0))],
        out_shape=[jax.ShapeDtypeStruct((T, 256), BF16),
                   jax.ShapeDtypeStruct((T, LANES), F32),
                   jax.ShapeDtypeStruct((T, LANES), F32),
                   jax.ShapeDtypeStruct((8, LANES), F32)],
        compiler_params=_cparams(("arbitrary",)),
    )(hb, hb, hb, hb, hb, sinks, dy)


def _rope_tables(pos, fetch=()):
    T = pos.shape[0]
    tm = 512
    nt, n = T // tm, len(fetch)

    def body(pos_ref, *rest):
        o_ref = rest[n]
        if n:
            start, relay, finish = _gather_plan(rest[:n], rest[n + 1:2 * n + 1], rest[2 * n + 1:])
            pl.when(pl.program_id(0) == 0)(start)
        lane = lax.broadcasted_iota(jnp.int32, (1, LANES), 1)
        active = jnp.logical_and(lane >= HEAD, lane < HEAD + 2 * ROPE_HALF)
        idx = ((lane - HEAD) % ROPE_HALF).astype(F32)
        freq = jnp.exp(idx * (-math.log(ROPE_THETA) / ROPE_HALF))
        ang = pos_ref[...].astype(F32) * freq
        cos, sin = jnp.cos(ang), jnp.sin(ang)
        o_ref[:, 0:LANES] = jnp.where(active, cos, 1.0)
        o_ref[:, LANES:2 * LANES] = jnp.where(jnp.logical_and(active, lane >= HEAD + ROPE_HALF), sin, 0.0)
        o_ref[:, 2 * LANES:] = jnp.where(jnp.logical_and(active, lane < HEAD + ROPE_HALF), -sin, 0.0)
        if n:
            @pl.when(pl.program_id(0) == nt - 1)
            def _():
                relay()
                finish()

    return pl.pallas_call(
        body, name="rope_tables_fetch" if n else "rope_tables", grid=(nt,),
        in_specs=[pl.BlockSpec((tm, 1), lambda i: (i, 0))] + [HBM] * n,
        out_specs=[pl.BlockSpec((tm, 3 * LANES), lambda i: (i, 0))] + [HBM] * n,
        out_shape=[jax.ShapeDtypeStruct((T, 3 * LANES), F32)]
                  + [jax.ShapeDtypeStruct((4,) + s.shape, s.dtype) for s in fetch],
        scratch_shapes=_gather_sems(n) if n else [],
        compiler_params=_cparams(("arbitrary",) if n else ("parallel",)),
    )(pos, *fetch)


def _rope_factors(tab_ref):
    return tab_ref[:, 0:LANES], tab_ref[:, LANES:2 * LANES], tab_ref[:, 2 * LANES:]


def _rope(x, tabs):
    c, s_up, s_dn = tabs
    return x * c + pltpu.roll(x, ROPE_HALF, 1) * s_up + pltpu.roll(x, LANES - ROPE_HALF, 1) * s_dn


def _rope_t(dy, tabs):
    c, s_up, s_dn = tabs
    return dy * c + pltpu.roll(dy * s_up, LANES - ROPE_HALF, 1) + pltpu.roll(dy * s_dn, ROPE_HALF, 1)


def _mla_lat_specs(tm):
    cq, ckv, ckr = ((_INT_OFF[n] - N_HB) for n in ("c_q", "c_kv", "c_kr"))
    return [pl.BlockSpec((tm, 256), lambda i: (i, cq // 256)),
            pl.BlockSpec((tm, LANES), lambda i: (i, ckv // LANES)),
            pl.BlockSpec((tm, LANES), lambda i: (i, ckr // LANES)),
            pl.BlockSpec((tm, 3 * LANES), lambda i: (i, 0)),
            pl.BlockSpec((1, 256), lambda i: (0, 0)),
            pl.BlockSpec((1, LANES), lambda i: (0, 0)),
            pl.BlockSpec((256, 512), lambda i: (0, 0)),
            pl.BlockSpec((LANES, 768), lambda i: (0, 0))]


def _mla_prep_fwd(hf, rope, g_q, g_kv, w_uq, w_ukv):
    T = hf.shape[0]
    tm = 512
    sub = tm // ATT_BLK

    def body(cq_ref, ckv_ref, ckr_ref, tab_ref, gq_ref, gkv_ref, wq_ref, wkv_ref, qm_ref, km_ref, vm_ref, vt_ref):
        tabs = _rope_factors(tab_ref)
        cq = cq_ref[...]
        q = _dot((cq * _rms(cq) * gq_ref[...]).astype(BF16), wq_ref[...])
        ckv = ckv_ref[...]
        kv = _dot((ckv * _rms(ckv) * gkv_ref[...]).astype(BF16), wkv_ref[...])
        kr = _rope(pltpu.roll(ckr_ref[...], HEAD, 1), tabs)
        for h in range(4):
            sl = slice(h * LANES, (h + 1) * LANES)
            qm_ref[:, sl] = (_rope(q[:, sl], tabs) * MLA_QSCALE).astype(BF16)
            km_ref[:, sl] = (kv[:, sl] + kr).astype(BF16)
        vm_ref[...] = kv[:, 512:].astype(BF16)
        for p in range(2):
            for s in range(sub):
                tile = kv[s * ATT_BLK:(s + 1) * ATT_BLK, 512 + p * LANES:512 + (p + 1) * LANES]
                vt_ref[p, s] = jnp.transpose(tile).astype(BF16)

    return pl.pallas_call(
        body, name="mla_prep_fwd", grid=(T // tm,), in_specs=_mla_lat_specs(tm),
        out_specs=[pl.BlockSpec((tm, 512), lambda i: (i, 0)),
                   pl.BlockSpec((tm, 512), lambda i: (i, 0)),
                   pl.BlockSpec((tm, 256), lambda i: (i, 0)),
                   pl.BlockSpec((2, sub, LANES, ATT_BLK), lambda i: (0, i, 0, 0))],
        out_shape=[jax.ShapeDtypeStruct((T, 512), BF16),
                   jax.ShapeDtypeStruct((T, 512), BF16),
                   jax.ShapeDtypeStruct((T, 256), BF16),
                   jax.ShapeDtypeStruct((2, T // ATT_BLK, LANES, ATT_BLK), BF16)],
        compiler_params=_cparams(("parallel",)),
    )(hf, hf, hf, rope, g_q, g_kv, w_uq, w_ukv)


def _mla_prep_bwd(hf, rope, g_q, g_kv, w_uq, w_ukv, dqm, dkt, dvt):
    T = hf.shape[0]
    tm = 512
    sub = tm // ATT_BLK

    def body(cq_ref, ckv_ref, ckr_ref, tab_ref, gq_ref, gkv_ref, wq_ref, wkv_ref, dq_ref, dk_ref, dv_ref,
             dc_ref, dwq_ref, dwkv_ref, dgq_ref, dgkv_ref):
        @pl.when(pl.program_id(0) == 0)
        def _():
            dwq_ref[...] = jnp.zeros_like(dwq_ref)
            dwkv_ref[...] = jnp.zeros_like(dwkv_ref)
            dgq_ref[...] = jnp.zeros_like(dgq_ref)
            dgkv_ref[...] = jnp.zeros_like(dgkv_ref)

        tabs = _rope_factors(tab_ref)
        lane =lax.broadcasted_iota(jnp.int32, (1, LANES), 1)
        dq = jnp.concatenate([_rope_t(dq_ref[:, h * LANES:(h + 1) * LANES] * MLA_QSCALE, tabs)
                              for h in range(4)], axis=1).astype(BF16)
        cq = cq_ref[...]
        rq = _rms(cq)
        cqn = (cq * rq * gq_ref[...]).astype(BF16)
        dwq_ref[...] += _dot_tn(cqn, dq)
        dcq, dgrow = _rms_bwd(_dot_nt(dq, wq_ref[...]), cq * rq, rq, gq_ref[...])
        dgq_ref[...] += _colsum(dgrow)
        dc_ref[:, 0:256] = dcq.astype(BF16)

        dk = jnp.concatenate([jnp.concatenate([jnp.transpose(dk_ref[p, s]) for p in range(2)], axis=1)
                              for s in range(sub)], axis=0) * LN2
        dv = jnp.concatenate([jnp.concatenate([jnp.transpose(dv_ref[p, s]) for p in range(2)], axis=1)
                              for s in range(sub)], axis=0)
        dkr = dk[:, 0:LANES] + dk[:, LANES:2 * LANES] + dk[:, 2 * LANES:3 * LANES] + dk[:, 3 * LANES:]
        dkr = pltpu.roll(_rope_t(dkr, tabs), HEAD, 1)
        dc_ref[:, 384:512] = jnp.where(lane < 2 * ROPE_HALF, dkr, 0.0).astype(BF16)
        dkv = jnp.concatenate([dk.astype(BF16), dv.astype(BF16)], axis=1)
        ckv = ckv_ref[...]
        rkv = _rms(ckv)
        ckvn = (ckv * rkv * gkv_ref[...]).astype(BF16)
        dwkv_ref[...] += _dot_tn(ckvn, dkv)
        dckv, dgrow = _rms_bwd(_dot_nt(dkv, wkv_ref[...]), ckv * rkv, rkv, gkv_ref[...])
        dgkv_ref[...] += _colsum(dgrow)
        dc_ref[:, 256:384] = dckv.astype(BF16)

    return pl.pallas_call(
        body, name="mla_prep_bwd", grid=(T // tm,),
        in_specs=_mla_lat_specs(tm) + [pl.BlockSpec((tm, 512), lambda i: (i, 0)),
                                       pl.BlockSpec((2, sub, 256, ATT_BLK), lambda i: (0, i, 0, 0)),
                                       pl.BlockSpec((2, sub, LANES, ATT_BLK), lambda i: (0, i, 0, 0))],
        out_specs=[pl.BlockSpec((tm, 512), lambda i: (i, 0)),
                   pl.BlockSpec((256, 512), lambda i: (0, 0)),
                   pl.BlockSpec((LANES, 768), lambda i: (0, 0)),
                   pl.BlockSpec((1, 256), lambda i: (0, 0)),
                   pl.BlockSpec((1, LANES), lambda i: (0, 0))],
        out_shape=[jax.ShapeDtypeStruct((T, 512), BF16),
                   jax.ShapeDtypeStruct((256, 512), F32),
                   jax.ShapeDtypeStruct((LANES, 768), F32),
                   jax.ShapeDtypeStruct((1, 256), F32),
                   jax.ShapeDtypeStruct((1, LANES), F32)],
        compiler_params=_cparams(("arbitrary",)),
    )(hf, hf, hf, rope, g_q, g_kv, w_uq, w_ukv, dqm, dkt, dvt)


def _causal_masks(bq, bk):
    row = lax.broadcasted_iota(jnp.int32, (bq, bk), 0)
    col = lax.broadcasted_iota(jnp.int32, (bq, bk), 1)
    return row, col


def _mla_fwd(qm, km, vt, fetch=()):
    T = qm.shape[0]
    bq, bk = min(MLA_BQ, T), ATT_BLK
    nq, nsub, nk = T // bq, bq // bk, T // bk
    n = len(fetch)

    def body(q_ref, k_ref, vt_ref, *rest):
        o_ref, lse_ref = rest[n:n + 2]
        acc_ref, m_ref, l_ref = rest[2 * n + 2:2 * n + 5]
        if n:
            start, relay, finish = _gather_plan(rest[:n], rest[n + 2:2 * n + 2], rest[2 * n + 5:])
            pl.when(pl.program_id(0) == 0)(start)
            pl.when(pl.program_id(0) == (3 * nq) // 4)(relay)
        qi = pl.program_id(0)
        key = lax.broadcasted_iota(jnp.int32, (bk, bq), 0)
        qry = lax.broadcasted_iota(jnp.int32, (bk, bq), 1)
        ones = jnp.ones((8, bk), BF16)
        acc_ref[...] = jnp.zeros_like(acc_ref)
        m_ref[...] = jnp.full_like(m_ref, NEG)
        l_ref[...] = jnp.zeros_like(l_ref)

        def step(kb0, masked):
            kbs = [kb0 + d for d in range(nsub)]
            qs = [slice(d * bk if masked else 0, bq) for d in range(nsub)]

            def wide(a, d, fill):
                if not qs[d].start:
                    return a
                return jnp.concatenate([jnp.full((a.shape[0], qs[d].start), fill, a.dtype), a], axis=1)

            sts = [[_dot_nt(k_ref[pl.ds(pl.multiple_of(kb * bk, bk), bk), e * LANES:(e + 1) * LANES],
                            q_ref[qs[d], e * LANES:(e + 1) * LANES]) for d, kb in enumerate(kbs)] for e in range(4)]
            pts, alphas = [], []
            for e in range(4):
                st = ([jnp.where(key[:, qs[d]] + d * bk <= qry[:, qs[d]], sts[e][d], NEG) for d in range(nsub)]
                      if masked else sts[e])
                m_prev = m_ref[e, 0:1, :]
                m_new = m_prev
                for d in range(nsub):
                    m_new = jnp.maximum(m_new, wide(jnp.max(st[d], axis=0, keepdims=True), d, NEG))
                alpha = jnp.exp2(m_prev - m_new)
                pt = [jnp.exp2(st[d] - m_new[:, qs[d]]).astype(BF16) for d in range(nsub)]
                l_new = alpha * l_ref[e]
                for d in range(nsub):
                    l_new = l_new + wide(_dot(ones, pt[d]), d, 0.0)
                l_ref[e] = l_new
                m_ref[e] = jnp.broadcast_to(m_new, (8, bq))
                pts.append(pt)
                alphas.append(alpha)
            for e in range(4):
                acc = alphas[e] * acc_ref[e]
                for d in range(nsub):
                    v_t = vt_ref[e // 2, kbs[d], (e % 2) * HEAD:(e % 2 + 1) * HEAD, :]
                    acc = acc + wide(_dot(v_t, pts[e][d]), d, 0.0)
                acc_ref[e] = acc

        step(qi * nsub, True)

        def loop(t, c):
            step(t * nsub, False)
            return c

        lax.fori_loop(0, qi, loop, 0)
        outs, lses = [], []
        for e in range(4):
            l = l_ref[e, 0:1, :]
            outs.append(acc_ref[e] / l)
            lses.append(jnp.broadcast_to(m_ref[e, 0:1, :] * LN2 + jnp.log(l), (HEAD, bq)))
        o_ref[...] = jnp.transpose(jnp.concatenate(outs, axis=0))
        lse_ref[...] = jnp.transpose(jnp.concatenate(lses, axis=0))
        if n:
            pl.when(pl.program_id(0) == nq - 1)(finish)

    return pl.pallas_call(
        body, name="mla_fwd_fetch" if n else "mla_fwd", grid=(nq,),
        in_specs=[pl.BlockSpec((bq, 512), lambda i: (i, 0)),
                  pl.BlockSpec((T, 512), lambda i: (0, 0)),
                  pl.BlockSpec((2, nk, LANES, bk), lambda i: (0, 0, 0, 0))] + [HBM] * n,
        out_specs=[pl.BlockSpec((bq, 256), lambda i: (i, 0)),
                   pl.BlockSpec((bq, 256), lambda i: (i, 0))] + [HBM] * n,
        out_shape=[jax.ShapeDtypeStruct((T, 256), F32), jax.ShapeDtypeStruct((T, 256), F32)]
                  + [jax.ShapeDtypeStruct((4,) + s.shape, s.dtype) for s in fetch],
        scratch_shapes=[pltpu.VMEM((4, HEAD, bq), F32), pltpu.VMEM((4, 8, bq), F32), pltpu.VMEM((4, 8, bq), F32)]
                       + (_gather_sems(n) if n else []),
        compiler_params=_cparams(("arbitrary",)),
    )(qm, km, vt, *fetch)


def _mla_bwd(qm, km, vm, y, lse, dy):
    T = qm.shape[0]
    bq, bk = min(MLA_BQ, T), ATT_BLK
    nq, nsub, nk = T // bq, bq // bk, T // bk

    def body(q_ref, k_ref, v_ref, y_ref, lse_ref, dy_ref, dq_ref, dkt_ref, dvt_ref, dob_ref, st_ref, qt_ref, dot_ref):
        qi = pl.program_id(1)

        @pl.when(qi == 0)
        def _():
            dkt_ref[...] = jnp.zeros_like(dkt_ref)
            dvt_ref[...] = jnp.zeros_like(dvt_ref)

        lane = lax.broadcasted_iota(jnp.int32, (1, LANES), 1) // HEAD
        row, col = _causal_masks(bq, bk)
        dq_ref[...] = jnp.zeros_like(dq_ref)
        lse = lse_ref[...]
        lse_other = pltpu.roll(lse, HEAD, 1)
        qt_ref[...] = jnp.transpose(q_ref[...].astype(F32)).astype(BF16)
        dot_ref[...] = jnp.transpose(dy_ref[...]).astype(BF16)
        for e in range(2):
            do = jnp.where(lane == e, dy_ref[...], 0.0)
            dob_ref[e] = do.astype(BF16)
            st_ref[2 * e] = jnp.where(lane == e, lse, lse_other) * LOG2E
            st_ref[2 * e + 1] = jnp.broadcast_to(jnp.sum(do * y_ref[...], axis=1, keepdims=True), (bq, LANES))

        hss = [slice(e * LANES, (e + 1) * LANES) for e in range(2)]
        tile = lambda a: jnp.concatenate([a] * (bk // LANES), axis=1)

        def step(kb0, masked):
            kbs = [kb0 + d for d in range(nsub)]
            rows = [pl.ds(pl.multiple_of(kb * bk, bk), bk) for kb in kbs]
            pairs = [(d, e) for d in range(nsub) for e in range(2)]
            qs = [slice(d * bk if masked else 0, bq) for d in range(nsub)]
            ss = {(d, e): _dot_nt(q_ref[qs[d], hss[e]], k_ref[rows[d], hss[e]]) for d, e in pairs}
            dps = {(d, e): _dot_nt(dob_ref[e, qs[d], :], jnp.where(lane == e, v_ref[rows[d], :], 0))
                   for d, e in pairs}
            ps, dss = {}, {}
            for d, e in pairs:
                s = jnp.where(col[qs[d]] + d * bk <= row[qs[d]], ss[d, e], NEG) if masked else ss[d, e]
                p = jnp.exp2(s - tile(st_ref[2 * e, qs[d], :]))
                dss[d, e] = (p * (dps[d, e] - tile(st_ref[2 * e + 1, qs[d], :]))).astype(BF16)
                ps[d, e] = p.astype(BF16)
            for d, e in pairs:
                dvt_ref[0, kbs[d], e * HEAD:(e + 1) * HEAD, :] += _dot(
                    dot_ref[e * HEAD:(e + 1) * HEAD, qs[d]], ps[d, e])
            for d, e in pairs:
                dkt_ref[0, kbs[d], hss[e], :] += _dot(qt_ref[hss[e], qs[d]], dss[d, e])
            for e in range(2):
                if masked:
                    for d in range(nsub):
                        dq_ref[qs[d], hss[e]] += _dot(dss[d, e], k_ref[rows[d], hss[e]])
                else:
                    dq = dq_ref[:, hss[e]]
                    for d in range(nsub):
                        dq = dq + _dot(dss[d, e], k_ref[rows[d], hss[e]])
                    dq_ref[:, hss[e]] = dq

        step(qi * nsub, True)

        def loop(t, c):
            step(t * nsub, False)
            return c

        lax.fori_loop(0, qi, loop, 0)
        dq_ref[...] *= LN2

    return pl.pallas_call(
        body, name="mla_bwd", grid=(2, nq),
        in_specs=[pl.BlockSpec((bq, 256), lambda j, i: (i, j)),
                  pl.BlockSpec((T, 256), lambda j, i: (0, j)),
                  pl.BlockSpec((T, LANES), lambda j, i: (0, j)),
                  pl.BlockSpec((bq, LANES), lambda j, i: (i, j)),
                  pl.BlockSpec((bq, LANES), lambda j, i: (i, j)),
                  pl.BlockSpec((bq, LANES), lambda j, i: (i, j))],
        out_specs=[pl.BlockSpec((bq, 256), lambda j, i: (i, j)),
                   pl.BlockSpec((1, nk, 256, bk), lambda j, i: (j, 0, 0, 0)),
                   pl.BlockSpec((1, nk, LANES, bk), lambda j, i: (j, 0, 0, 0))],
        out_shape=[jax.ShapeDtypeStruct((T, 512), F32),
                   jax.ShapeDtypeStruct((2, nk, 256, bk), F32),
                   jax.ShapeDtypeStruct((2, nk, LANES, bk), F32)],
        scratch_shapes=[pltpu.VMEM((2, bq, LANES), BF16), pltpu.VMEM((4, bq, LANES), F32),
                        pltpu.VMEM((256, bq), BF16), pltpu.VMEM((LANES, bq), BF16)],
        compiler_params=_cparams(("parallel", "arbitrary")),
    )(qm, km, vm, y, lse, dy)


def _suffix_ones(n):
    r = lax.broadcasted_iota(jnp.int32, (n, n), 0)
    c = lax.broadcasted_iota(jnp.int32, (n, n), 1)
    return (r >= c).astype(BF16)


def _prefix_ones(n):
    r = lax.broadcasted_iota(jnp.int32, (n, n), 0)
    c = lax.broadcasted_iota(jnp.int32, (n, n), 1)
    return (r <= c).astype(BF16)


def _sb_specs(T, bq):
    qo, ko, vo = (_INT_OFF[n] // 256 for n in ("d_q", "d_k", "d_v"))
    return [pl.BlockSpec((bq, 256), lambda i: (i, qo)),
            pl.BlockSpec((T, 256), lambda i: (0, ko)),
            pl.BlockSpec((T, 256), lambda i: (0, vo))]


def _sb_fwd(hb):
    T = hb.shape[0]
    bq = bk = ATT_BLK
    nq = T // bq

    def body(q_ref, k_ref, v_ref, o_ref, tot_ref, cnt_ref, qm_ref, car_ref):
        qi = pl.program_id(0)
        lane = lax.broadcasted_iota(jnp.int32, (1, LANES), 1) // HEAD
        row, col = _causal_masks(bq, bk)
        strict = col < row
        u = _suffix_ones(bk)
        o_ref[...] = jnp.zeros_like(o_ref)
        car_ref[...] = jnp.zeros_like(car_ref)
        pair = lambda h: slice((h // 2) * LANES, (h // 2 + 1) * LANES)
        for h in range(4):
            qm_ref[h] = jnp.where(lane == h % 2, q_ref[:, pair(h)], 0) * 0.125

        def step(blocks):
            tile = lambda a: jnp.concatenate([a] * (bk // LANES), axis=1)
            rows = [pl.ds(pl.multiple_of(kb * bk, bk), bk) for kb, _ in blocks]
            pairs = [(b, h) for b in range(len(blocks)) for h in range(4)]
            zs = {(b, h): _dot_nt(qm_ref[h], k_ref[rows[b], pair(h)]) for b, h in pairs}
            splits = {}
            for b, h in pairs:
                z = zs[b, h]
                lk = jnp.minimum(-z, 0.0) - jnp.log(1.0 + jnp.exp(-jnp.abs(z)))
                if blocks[b][1] is not None:
                    lk = jnp.where(blocks[b][1], lk, 0.0)
                splits[b, h] = _split(lk)
            sufs = {bh: _dot(hi, u) + _dot(lo, u) for bh, (hi, lo) in splits.items()}
            car = [car_ref[h] for h in range(4)]
            aas = {}
            for b, h in pairs:
                a = jnp.exp(zs[b, h] + sufs[b, h] + tile(car[h]))
                if blocks[b][1] is not None:
                    a = jnp.where(blocks[b][1], a, 0.0)
                aas[b, h] = a.astype(BF16)
                car[h] = car[h] + jnp.broadcast_to(sufs[b, h][:, 0:1], (bq, LANES))
            acc = [o_ref[:, pair(0)], o_ref[:, pair(2)]]
            for b, h in pairs:
                acc[h // 2] = acc[h // 2] + _dot(aas[b, h], jnp.where(lane == h % 2, v_ref[rows[b], pair(h)], 0))
            o_ref[:, pair(0)], o_ref[:, pair(2)] = acc
            for h in range(4):
                car_ref[h] = car[h]

        step([(qi, strict), (jnp.maximum(qi - 1, 0), qi > 0)])

        def live():
            worst = jnp.maximum(jnp.maximum(car_ref[0], car_ref[1]), jnp.maximum(car_ref[2], car_ref[3]))
            return jnp.max(worst) >= SB_DEAD

        def cond(c):
            return jnp.logical_and(c[0] < qi, c[1])

        def loop(c):
            step([(qi - 1 - c[0], None)])
            return c[0] + 1, live()

        done, _ = lax.while_loop(cond, loop, (jnp.minimum(qi, 1), live()))
        tot_ref[:, pair(0)] = jnp.where(lane == 0, car_ref[0], car_ref[1])
        tot_ref[:, pair(2)] = jnp.where(lane == 0, car_ref[2], car_ref[3])
        cnt_ref[0, qi] = done.astype(F32)

    return pl.pallas_call(
        body, name="sb_fwd", grid=(nq,), in_specs=_sb_specs(T, bq),
        out_specs=[pl.BlockSpec((bq, 256), lambda i: (i, 0)), pl.BlockSpec((bq, 256), lambda i: (i, 0)),
                   pl.BlockSpec(memory_space=pltpu.SMEM)],
        out_shape=[jax.ShapeDtypeStruct((T, 256), F32), jax.ShapeDtypeStruct((T, 256), F32),
                   jax.ShapeDtypeStruct((1, nq), F32)],
        scratch_shapes=[pltpu.VMEM((4, bq, LANES), BF16), pltpu.VMEM((4, bq, LANES), F32)],
        compiler_params=_cparams(("arbitrary",)),
    )(hb, hb, hb)


def _sb_bwd(hb, tot, cnt, dy):
    T = hb.shape[0]
    bq = bk = ATT_BLK
    nq = T // bq

    def body(q_ref, k_ref, v_ref, tot_ref, dy_ref, cnt_ref, dq_ref, dk_ref, dv_ref, qm_ref, dob_ref, dqa_ref, rem_ref,
             cg_ref):
        qi = pl.program_id(0)

        @pl.when(qi == 0)
        def _():
            dk_ref[...] = jnp.zeros_like(dk_ref)
            dv_ref[...] = jnp.zeros_like(dv_ref)

        lane = lax.broadcasted_iota(jnp.int32, (1, LANES), 1) // HEAD
        row, col = _causal_masks(bq, bk)
        strict = col < row
        u = _prefix_ones(bk)
        pair = lambda h: slice((h // 2) * LANES, (h // 2 + 1) * LANES)
        dqa_ref[...] = jnp.zeros_like(dqa_ref)
        cg_ref[...] = jnp.zeros_like(cg_ref)
        for h in range(4):
            tot = tot_ref[:, pair(h)]
            qm_ref[h] = jnp.where(lane == h % 2, q_ref[:, pair(h)], 0) * 0.125
            dob_ref[h] = jnp.where(lane == h % 2, dy_ref[:, pair(h)], 0.0).astype(BF16)
            rem_ref[h] = jnp.where(lane == h % 2, tot, pltpu.roll(tot, HEAD, 1))

        def step(blocks):
            tile = lambda a: jnp.concatenate([a] * (bk // LANES), axis=1)
            nb = len(blocks)
            rows = [pl.ds(pl.multiple_of(kb * bk, bk), bk) for kb, _ in blocks]
            pairs = [(b, h) for b in range(nb) for h in range(4)]
            mask = lambda b, x: x if blocks[b][1] is None else jnp.where(blocks[b][1], x, 0.0)
            zs = {(b, h): _dot_nt(qm_ref[h], k_ref[rows[b], pair(h)]) for b, h in pairs}
            das = {(b, h): _dot_nt(dob_ref[h], jnp.where(lane == h % 2, v_ref[rows[b], pair(h)], 0)) for b, h in pairs}
            zls, splits = {}, {}
            for b, h in pairs:
                z = zs[b, h]
                lk = mask(b, jnp.minimum(-z, 0.0) - jnp.log(1.0 + jnp.exp(-jnp.abs(z))))
                zls[b, h] = z + lk
                splits[b, h] = _split(lk)
            pres = {bh: _dot(hi, u) + _dot(lo, u) for bh, (hi, lo) in splits.items()}
            rem = [rem_ref[h] for h in range(4)]
            aas, gs, gsplits = {}, {}, {}
            for b, h in pairs:
                a = mask(b, jnp.exp(zls[b, h] + (tile(rem[h]) - pres[b, h])))
                gs[b, h] = a * das[b, h]
                aas[b, h] = a.astype(BF16)
                gsplits[b, h] = _split(gs[b, h])
                rem[h] = rem[h] - jnp.broadcast_to(pres[b, h][:, bk - 1:bk], (bq, LANES))
            for b in range(nb):
                for p in (0, 2):
                    dv_ref[rows[b], pair(p)] += _dot_tn(aas[b, p], dob_ref[p]) + _dot_tn(aas[b, p + 1], dob_ref[p + 1])
            gpres = {bh: _dot(hi, u) + _dot(lo, u) for bh, (hi, lo) in gsplits.items()}
            cg = [cg_ref[h] for h in range(4)]
            dzs = {}
            for b, h in pairs:
                dz = mask(b, gs[b, h] - jnp.exp(zls[b, h]) * (tile(cg[h]) + gpres[b, h]))
                dzs[b, h] = dz.astype(BF16)
                cg[h] = cg[h] + jnp.broadcast_to(gpres[b, h][:, bk - 1:bk], (bq, LANES))
            for b in range(nb):
                for p in (0, 2):
                    dk_ref[rows[b], pair(p)] += _dot_tn(dzs[b, p], qm_ref[p]) + _dot_tn(dzs[b, p + 1], qm_ref[p + 1])
            for h in range(4):
                dq = dqa_ref[h]
                for b in range(nb):
                    dq = dq + _dot(dzs[b, h], k_ref[rows[b], pair(h)])
                dqa_ref[h] = dq
                rem_ref[h] = rem[h]
                cg_ref[h] = cg[h]

        def loop(kb, c):
            step([(kb, None)])
            return c

        start = qi - jnp.clip(cnt_ref[0, qi].astype(jnp.int32), 0, qi)
        lax.fori_loop(start, qi - 1, loop, 0)
        step([(jnp.maximum(qi - 1, 0), qi > 0), (qi, strict)])
        for p in (0, 2):
            dq_ref[:, pair(p)] = (jnp.where(lane == 0, dqa_ref[p], dqa_ref[p + 1]) * 0.125).astype(BF16)

    return pl.pallas_call(
        body, name="sb_bwd", grid=(nq,),
        in_specs=_sb_specs(T, bq) + [pl.BlockSpec((bq, 256), lambda i: (i, 0)),
                                     pl.BlockSpec((bq, 256), lambda i: (i, 0)),
                                     pl.BlockSpec(memory_space=pltpu.SMEM)],
        out_specs=[pl.BlockSpec((bq, 256), lambda i: (i, 0)),
                   pl.BlockSpec((T, 256), lambda i: (0, 0)),
                   pl.BlockSpec((T, 256), lambda i: (0, 0))],
        out_shape=[jax.ShapeDtypeStruct((T, 256), BF16)] + [jax.ShapeDtypeStruct((T, 256), F32)] * 2,
        scratch_shapes=[pltpu.VMEM((4, bq, LANES), BF16), pltpu.VMEM((4, bq, LANES), BF16),
                        pltpu.VMEM((4, bq, LANES), F32), pltpu.VMEM((4, bq, LANES), F32),
                        pltpu.VMEM((4, bq, LANES), F32)],
        compiler_params=_cparams(("arbitrary",)),
    )(hb, hb, hb, tot, dy, cnt)


EP_TM = 512


def _ep_in_specs(tm, rev):
    idx = (lambda i: rev - i) if rev is not None else (lambda i: i)
    bo = (_INT_OFF["b_b"] - N_HB) // 256
    halo = lambda i: jnp.maximum(idx(i) * (tm // 8) - 1, 0)
    return [pl.BlockSpec((tm, 256), lambda i: (idx(i), 0)),
            pl.BlockSpec((tm, 256), lambda i: (idx(i), 0)),
            pl.BlockSpec((tm, 256), lambda i: (idx(i), 0)),
            pl.BlockSpec((tm, D_MODEL), lambda i: (idx(i), 0)),
            pl.BlockSpec((tm, 256), lambda i: (idx(i), bo)),
            pl.BlockSpec((tm, 256), lambda i: (idx(i), bo + 1)),
            pl.BlockSpec((tm, 256), lambda i: (idx(i), bo + 2)),
            pl.BlockSpec((8, 256), lambda i: (halo(i), bo + 1)),
            pl.BlockSpec((8, 256), lambda i: (halo(i), bo + 2)),
            pl.BlockSpec((3, 256), lambda i: (0, 0)),
            pl.BlockSpec((1, 256), lambda i: (0, 0)),
            pl.BlockSpec((1, D_MODEL), lambda i: (0, 0)),
            pl.BlockSpec((D_MODEL, D_MODEL), lambda i: (0, 0)),
            pl.BlockSpec((1, D_MODEL), lambda i: (0, 0))]


def _ep_mix(first, ya_ref, yc_ref, yd_ref, gate_ref, bb_ref, bc_ref, bx_ref, hc_ref, hx_ref, cw_ref, cb_ref, gg_ref):
    tm = ya_ref.shape[0]
    u = bc_ref[...] * bx_ref[...]
    halo = jnp.where(first, 0.0, hc_ref[...] * hx_ref[...])
    row = lax.broadcasted_iota(jnp.int32, (tm, 1), 0)
    u1 = jnp.where(row == 0, halo[7:8, :], pltpu.roll(u, 1, 0))
    u2 = jnp.where(row == 0, halo[6:7, :], jnp.where(row == 1, halo[7:8, :], pltpu.roll(u, 2, 0)))
    cw = cw_ref[...]
    conv = cw[0:1, :] * u2 + cw[1:2, :] * u1 + cw[2:3, :] * u + cb_ref[...]
    bb = bb_ref[...]
    ys = [ya_ref[...], bb * conv, yc_ref[...], yd_ref[...]]
    rs = [_rms(y) for y in ys]
    gg = gg_ref[...]
    yhat = jnp.concatenate([y * r for y, r in zip(ys, rs)], axis=1)
    gate = gate_ref[...]
    sig = 1.0 / (1.0 + jnp.exp(-gate))
    return u, u1, u2, conv, bb, rs, yhat, yhat * gg, gate, sig


def _epilogue_fwd(x, ya, yc, yd, hf, conv_w, conv_b, g_grp, w_out, g_post, tgt=None):
    T = x.shape[0]
    tm = EP_TM
    row_spec = pl.BlockSpec((tm, D_MODEL), lambda i: (i, 0))

    def layer_out(refs):
        (x_ref, ya_ref, yc_ref, yd_ref, gate_ref, bb_ref, bc_ref, bx_ref, hc_ref, hx_ref, cw_ref, cb_ref,
         gg_ref, wo_ref, gp_ref) = refs
        (_, _, _, _, _, _, _, yn, gate, sig) = _ep_mix(
            pl.program_id(0) == 0, ya_ref, yc_ref, yd_ref, gate_ref, bb_ref, bc_ref, bx_ref, hc_ref, hx_ref,
            cw_ref, cb_ref, gg_ref)
        z = _dot((yn * (gate * sig)).astype(BF16), wo_ref[...])
        return x_ref[...] + z * _rms(z) * gp_ref[...]

    args = (x, ya, yc, yd, hf, hf, hf, hf, hf, hf, conv_w, conv_b, g_grp, w_out, g_post)
    in_specs = [row_spec] + _ep_in_specs(tm, None)
    if tgt is None:
        def body(*refs):
            refs[-1][...] = layer_out(refs[:-1])

        return pl.pallas_call(
            body, name="epilogue_fwd", grid=(T // tm,), in_specs=in_specs, out_specs=row_spec,
            out_shape=jax.ShapeDtypeStruct((T, D_MODEL), F32), compiler_params=_cparams(("parallel",)),
        )(*args)

    def body_loss(*refs):
        t_ref, dy_ref, l_ref = refs[-3:]

        @pl.when(pl.program_id(0) == 0)
        def _():
            l_ref[...] = jnp.zeros_like(l_ref)

        d = layer_out(refs[:-3]) - t_ref[...]
        dy_ref[...] = d * (1.0 / D_MODEL)
        part = jnp.sum(jnp.sum(d * d, axis=1, keepdims=True), axis=0, keepdims=True)
        l_ref[...] += part * (0.5 / D_MODEL)

    return pl.pallas_call(
        body_loss, name="epilogue_fwd_loss", grid=(T // tm,), in_specs=in_specs + [row_spec],
        out_specs=[row_spec, pl.BlockSpec((8, LANES), lambda i: (0, 0))],
        out_shape=[jax.ShapeDtypeStruct((T, D_MODEL), F32), jax.ShapeDtypeStruct((8, LANES), F32)],
        compiler_params=_cparams(("arbitrary",)),
    )(*args, tgt)


def _epilogue_bwd(dxn, ya, yc, yd, hf, conv_w, conv_b, g_grp, w_out, g_post):
    T = dxn.shape[0]
    tm = EP_TM
    nt = T // tm
    ridx = lambda i: (nt - 1 - i, 0)

    def body(dx_ref, ya_ref, yc_ref, yd_ref, gate_ref, bb_ref, bc_ref, bx_ref, hc_ref, hx_ref, cw_ref, cb_ref,
             gg_ref, wo_ref, gp_ref,
             dya_ref, dyc_ref, dyd_ref, dhf_ref, dwo_ref, dgp_ref, dgg_ref, dcw_ref, dcb_ref, carry_ref):
        i = pl.program_id(0)

        @pl.when(i == 0)
        def _():
            for r in (dwo_ref, dgp_ref, dgg_ref, dcw_ref, dcb_ref, carry_ref):
                r[...] = jnp.zeros_like(r)

        (u, u1, u2, conv, bb, rs, yhat, yn, gate, sig) = _ep_mix(
            i == nt - 1, ya_ref, yc_ref, yd_ref, gate_ref, bb_ref, bc_ref, bx_ref, hc_ref, hx_ref,
            cw_ref, cb_ref, gg_ref)
        silu = gate * sig
        ymix = (yn * silu).astype(BF16)
        z = _dot(ymix, wo_ref[...])
        rz = _rms(z)
        dz, dgrow = _rms_bwd(dx_ref[...], z * rz, rz, gp_ref[...])
        dgp_ref[...] += _colsum(dgrow)
        dzb = dz.astype(BF16)
        dwo_ref[...] += _dot_tn(ymix, dzb)
        dymix = _dot_nt(dzb, wo_ref[...])
        dhf_ref[:, 0:D_MODEL] = (dymix * yn * (sig * (1.0 + gate * (1.0 - sig)))).astype(BF16)
        dyn = dymix * silu
        dgg_ref[...] += _colsum(dyn * yhat)
        gg = gg_ref[...]
        dys = []
        for gi in range(4):
            sl = slice(gi * GROUP, (gi + 1) * GROUP)
            dyh = dyn[:, sl] * gg[:, sl]
            yh = yhat[:, sl]
            dys.append(rs[gi] * (dyh - yh * jnp.mean(dyh * yh, axis=-1, keepdims=True)))
        dya_ref[...] = dys[0]
        dyc_ref[...] = dys[2]
        dyd_ref[...] = dys[3]
        dyb = dys[1]
        dhf_ref[:, D_MODEL:D_MODEL + 256] = (dyb * conv).astype(BF16)
        dconv = dyb * bb
        dcb_ref[...] += _colsum(dconv)
        dcw_ref[0:1, :] += _colsum(dconv * u2)
        dcw_ref[1:2, :] += _colsum(dconv * u1)
        dcw_ref[2:3, :] += _colsum(dconv * u)
        carry = carry_ref[...]
        row = lax.broadcasted_iota(jnp.int32, (tm, 1), 0)
        d1 = jnp.where(row == tm - 1, carry[0:1, :], pltpu.roll(dconv, tm - 1, 0))
        d2 = jnp.where(row == tm - 2, carry[0:1, :],
                       jnp.where(row == tm - 1, carry[1:2, :], pltpu.roll(dconv, tm - 2, 0)))
        cw = cw_ref[...]
        du = cw[2:3, :] * dconv + cw[1:2, :] * d1 + cw[0:1, :] * d2
        dhf_ref[:, D_MODEL + 256:D_MODEL + 512] = (du * bx_ref[...]).astype(BF16)
        dhf_ref[:, D_MODEL + 512:D_MODEL + 768] = (du * bc_ref[...]).astype(BF16)
        carry_ref[...] = dconv[0:8, :]

    in_specs = [pl.BlockSpec((tm, D_MODEL), ridx)] + _ep_in_specs(tm, nt - 1)
    return pl.pallas_call(
        body, name="epilogue_bwd", grid=(nt,), in_specs=in_specs,
        out_specs=[pl.BlockSpec((tm, 256), ridx), pl.BlockSpec((tm, 256), ridx), pl.BlockSpec((tm, 256), ridx),
                   pl.BlockSpec((tm, D_MODEL + 768), ridx),
                   pl.BlockSpec((D_MODEL, D_MODEL), lambda i: (0, 0)),
                   pl.BlockSpec((1, D_MODEL), lambda i: (0, 0)),
                   pl.BlockSpec((1, D_MODEL), lambda i: (0, 0)),
                   pl.BlockSpec((8, 256), lambda i: (0, 0)),
                   pl.BlockSpec((1, 256), lambda i: (0, 0))],
        out_shape=[jax.ShapeDtypeStruct((T, 256), F32)] * 3
                  + [jax.ShapeDtypeStruct((T, D_MODEL + 768), BF16),
                     jax.ShapeDtypeStruct((D_MODEL, D_MODEL), F32),
                     jax.ShapeDtypeStruct((1, D_MODEL), F32),
                     jax.ShapeDtypeStruct((1, D_MODEL), F32),
                     jax.ShapeDtypeStruct((8, 256), F32),
                     jax.ShapeDtypeStruct((1, 256), F32)],
        scratch_shapes=[pltpu.VMEM((8, 256), F32)],
        compiler_params=_cparams(("arbitrary",)),
    )(dxn, ya, yc, yd, hf, hf, hf, hf, hf, hf, conv_w, conv_b, g_grp, w_out, g_post)


def _place():
    return lax.axis_index("x"), lax.axis_index("y"), lax.axis_index("c")


def _other_chips(x, y):
    return [(1 - x, y), (x, 1 - y), (1 - x, 1 - y)]


HBM = pl.BlockSpec(memory_space=pl.ANY)


def _gather_plan(ins, outs, sems):
    n = len(ins)
    ici_send, ici_recv, d2d_send, d2d_recv, local_sems = sems
    x, y, c = _place()
    me = 2 * x + y
    chips = _other_chips(x, y)

    def ici(a, j, chip_from):
        px, py = chips[j]
        return pltpu.make_async_remote_copy(
            src_ref=ins[a].at[c], dst_ref=outs[a].at[chip_from, c], send_sem=ici_send.at[3 * a + j],
            recv_sem=ici_recv.at[3 * a + j], device_id=(px, py, c), device_id_type=MESH)

    def d2d(a, j, part):
        px, py = chips[j]
        blk = outs[a].at[2 * px + py, part]
        return pltpu.make_async_remote_copy(
            src_ref=blk, dst_ref=blk, send_sem=d2d_send.at[3 * a + j], recv_sem=d2d_recv.at[3 * a + j],
            device_id=(x, y, 1 - c), device_id_type=MESH)

    def local(a):
        return pltpu.make_async_copy(ins[a], outs[a].at[me], local_sems.at[a])

    hops = [(j, a) for j in range(3) for a in range(n)]

    def start():
        for a in range(n):
            local(a).start()
        for j, a in hops:
            ici(a, j, me).start()

    def relay():
        for j, a in hops:
            ici(a, j, 2 * chips[j][0] + chips[j][1]).wait_recv()
            d2d(a, j, c).start()

    def finish():
        for j, a in hops:
            d2d(a, j, 1 - c).wait_recv()
        for j, a in hops:
            ici(a, j, me).wait_send()
            d2d(a, j, c).wait_send()
        for a in range(n):
            local(a).wait()

    return start, relay, finish


def _gather_sems(n):
    return [pltpu.SemaphoreType.DMA((3 * n,))] * 4 + [pltpu.SemaphoreType.DMA((n,))]


def _exchange_chips(parts, small):
    n = len(parts)

    def body(*refs):
        ins, sm_ref = refs[:n], refs[n]
        outs, osm_ref = refs[n + 1:2 * n + 1], refs[2 * n + 1]
        send_sems, recv_sems, ssend_sems, srecv_sems, local_sems = refs[2 * n + 2:]
        x, y, c = _place()
        me = 2 * x + y
        dev = 4 * x + 2 * y + c
        local = [pltpu.make_async_copy(ins[a].at[me], outs[a].at[me], local_sems.at[a]) for a in range(n)]
        local.append(pltpu.make_async_copy(sm_ref, osm_ref.at[dev], local_sems.at[n]))
        for cp in local:
            cp.start()
        sends = []
        for j, (px, py) in enumerate(_other_chips(x, y)):
            for a in range(n):
                cp = pltpu.make_async_remote_copy(
                    src_ref=ins[a].at[2 * px + py], dst_ref=outs[a].at[me], send_sem=send_sems.at[3 * a + j],
                    recv_sem=recv_sems.at[3 * a + j], device_id=(px, py, c), device_id_type=MESH)
                cp.start()
                sends.append(cp)
        flips = [(fx, fy, fc) for fx in (0, 1) for fy in (0, 1) for fc in (0, 1)][1:]
        for j, (fx, fy, fc) in enumerate(flips):
            cp = pltpu.make_async_remote_copy(
                src_ref=sm_ref, dst_ref=osm_ref.at[dev], send_sem=ssend_sems.at[j], recv_sem=srecv_sems.at[j],
                device_id=(x ^ fx, y ^ fy, c ^ fc), device_id_type=MESH)
            cp.start()
            sends.append(cp)
        for j, (px, py) in enumerate(_other_chips(x, y)):
            for a in range(n):
                pltpu.make_async_remote_copy(
                    src_ref=ins[a].at[me], dst_ref=outs[a].at[2 * px + py], send_sem=send_sems.at[3 * a + j],
                    recv_sem=recv_sems.at[3 * a + j], device_id=(px, py, c), device_id_type=MESH).wait_recv()
        for j, (fx, fy, fc) in enumerate(flips):
            src = 4 * (x ^ fx) + 2 * (y ^ fy) + (c ^ fc)
            pltpu.make_async_remote_copy(
                src_ref=sm_ref, dst_ref=osm_ref.at[src], send_sem=ssend_sems.at[j], recv_sem=srecv_sems.at[j],
                device_id=(x ^ fx, y ^ fy, c ^ fc), device_id_type=MESH).wait_recv()
        for cp in sends:
            cp.wait_send()
        for cp in local:
            cp.wait()

    return pl.pallas_call(
        body, name="exchange_chips",
        in_specs=[HBM] * (n + 1), out_specs=[HBM] * (n + 1),
        out_shape=[jax.ShapeDtypeStruct(p.shape, p.dtype) for p in parts]
                  + [jax.ShapeDtypeStruct((8,) + small.shape, small.dtype)],
        scratch_shapes=[pltpu.SemaphoreType.DMA((3 * n,)), pltpu.SemaphoreType.DMA((3 * n,)),
                        pltpu.SemaphoreType.DMA((7,)), pltpu.SemaphoreType.DMA((7,)),
                        pltpu.SemaphoreType.DMA((n + 1,))],
    )(*parts, small)


def _swap_cores(parts, name):
    n = len(parts)

    def body(*refs):
        ins, outs, send_sems, recv_sems = refs[:n], refs[n:2 * n], refs[2 * n], refs[2 * n + 1]
        x, y, c = _place()
        copies = [pltpu.make_async_remote_copy(
            src_ref=ins[a], dst_ref=outs[a], send_sem=send_sems.at[a], recv_sem=recv_sems.at[a],
            device_id=(x, y, 1 - c), device_id_type=MESH) for a in range(n)]
        for cp in copies:
            cp.start()
        for cp in copies:
            cp.wait()

    return pl.pallas_call(
        body, name=name, in_specs=[HBM] * n, out_specs=[HBM] * n,
        out_shape=[jax.ShapeDtypeStruct(p.shape, p.dtype) for p in parts],
        scratch_shapes=[pltpu.SemaphoreType.DMA((n,)), pltpu.SemaphoreType.DMA((n,))],
    )(*parts)


def _tile(rows, cols):
    for cand in (256, 128, 64):
        if rows % cand == 0:
            return cand, cols
    if rows > 64 and cols % 256 == 0:
        return rows, 256
    return rows, cols


def _swap_layer_chunks(per_layer):
    n = len(per_layer[0])

    def body(*refs):
        ins, outs, send_sems, recv_sems = (refs[:n], refs[n:2 * n]), refs[2 * n:3 * n], refs[3 * n], refs[3 * n + 1]
        x, y, c = _place()
        for l in range(2):
            @pl.when(c == 1 - l)
            def _():
                copies = [pltpu.make_async_remote_copy(
                    src_ref=ins[l][a], dst_ref=outs[a], send_sem=send_sems.at[a], recv_sem=recv_sems.at[a],
                    device_id=(x, y, 1 - c), device_id_type=MESH) for a in range(n)]
                for cp in copies:
                    cp.start()
                for cp in copies:
                    cp.wait()

    return pl.pallas_call(
        body, name="swap_layer_chunks", in_specs=[HBM] * (2 * n), out_specs=[HBM] * n,
        out_shape=[jax.ShapeDtypeStruct(p.shape, p.dtype) for p in per_layer[0]],
        scratch_shapes=[pltpu.SemaphoreType.DMA((n,)), pltpu.SemaphoreType.DMA((n,))],
    )(*per_layer[0], *per_layer[1])


def _add(a0, a1, b, name):
    L, R, C = b.shape
    tr, tc = _tile(R, C)

    def body(a0_ref, a1_ref, b_ref, o_ref):
        mine = jnp.where(lax.axis_index("c") == 0, a0_ref[...], a1_ref[...])
        o_ref[...] = (mine + b_ref[...]).astype(BF16)

    spec = pl.BlockSpec((1, tr, tc), lambda l, i, j: (l, i, j))
    return pl.pallas_call(
        body, name=name, grid=(L, R // tr, C // tc), in_specs=[spec] * 3, out_specs=spec,
        out_shape=jax.ShapeDtypeStruct((L, R, C), BF16),
        compiler_params=_cparams(("parallel", "parallel", "parallel")),
    )(a0, a1, b)


def _sum_leading(buf, name):
    n, R, C = buf.shape
    tr, tc = _tile(R, C)

    def body(b_ref, o_ref):
        acc = b_ref[0].astype(F32)
        for k in range(1, n):
            acc = acc + b_ref[k].astype(F32)
        o_ref[...] = acc

    return pl.pallas_call(
        body, name=name, grid=(R // tr, C // tc),
        in_specs=[pl.BlockSpec((n, tr, tc), lambda i, j: (0, i, j))],
        out_specs=pl.BlockSpec((tr, tc), lambda i, j: (i, j)),
        out_shape=jax.ShapeDtypeStruct((R, C), F32),
        compiler_params=_cparams(("parallel", "parallel")),
    )(buf)


def _adam_update(w, g, m, v):
    c1 = 1.0 / (1.0 - ADAM_B1 ** ADAM_STEP)
    c2 = 1.0 / (1.0 - ADAM_B2 ** ADAM_STEP)
    mn = ADAM_B1 * m + (1.0 - ADAM_B1) * g
    vn = ADAM_B2 * v + (1.0 - ADAM_B2) * (g * g)
    return -ADAM_LR * ((mn * c1) / (jnp.sqrt(vn * c2) + ADAM_EPS) + ADAM_WD * w), mn, vn


def _adamw_layers(w, m, v, g_mine, g_other, name):
    _, R, C = w.shape
    tr, tc = _tile(R, C)

    def body(w_ref, m_ref, v_ref, gm_ref, go_ref, g_ref, d_ref, mo_ref, vo_ref):
        g = jnp.where(pl.program_id(0) == lax.axis_index("c"), gm_ref[...], go_ref[...])
        g_ref[0] = g
        d_ref[0], mo_ref[0], vo_ref[0] = _adam_update(w_ref[0], g, m_ref[0], v_ref[0])

    spec3 = pl.BlockSpec((1, tr, tc), lambda l, i, j: (l, i, j))
    spec2 = pl.BlockSpec((tr, tc), lambda l, i, j: (i, j))
    return pl.pallas_call(
        body, name=name, grid=(2, R // tr, C // tc),
        in_specs=[spec3] * 3 + [spec2] * 2, out_specs=[spec3] * 4,
        out_shape=[jax.ShapeDtypeStruct(w.shape, F32)] * 4,
        compiler_params=_cparams(("parallel", "parallel", "parallel")),
    )(w, m, v, g_mine, g_other)


PACK_C = 1024
_BIG = ("w_in", "w_out", "mla_w_uq", "mla_w_ukv", "conv_w")
_SMALL = ("norm_pre", "group_norm", "norm_post", "conv_b", "mla_q_norm", "mla_kv_norm", "attn_sinks")
_SMALL_W = {"norm_pre": 1024, "group_norm": 1024, "norm_post": 1024, "conv_b": 256, "mla_q_norm": 256,
            "mla_kv_norm": 128, "attn_sinks": 4}


_LOSS_AT = divmod(DEPTH * sum(_SMALL_W.values()), PACK_C)


def _pack_small(d, loss):
    flat = jnp.concatenate([d[n].reshape(-1) for n in _SMALL] + [loss.reshape(1)])
    return jnp.pad(flat, (0, 8 * PACK_C - flat.shape[0])).reshape(8, PACK_C)


def _adamw_small(w, m, v, got):
    ns = len(_SMALL)

    def body(*refs):
        got_ref = refs[3 * ns]
        outs = refs[3 * ns + 1:]
        gsum = got_ref[0]
        for d in range(1, 8):
            gsum = gsum + got_ref[d]
        outs[4 * ns][...] = gsum[_LOSS_AT[0]:_LOSS_AT[0] + 1, _LOSS_AT[1]:_LOSS_AT[1] + 1]
        off = 0
        for i, name in enumerate(_SMALL):
            wd = _SMALL_W[name]
            rows = []
            for l in range(DEPTH):
                r, c0 = divmod(off + l * wd, PACK_C)
                rows.append(gsum[r:r + 1, c0:c0 + wd])
            off += DEPTH * wd
            g = jnp.concatenate(rows, axis=0)
            delta, mn, vn = _adam_update(refs[i][...], g, refs[ns + i][...], refs[2 * ns + i][...])
            outs[i][...] = g
            outs[ns + i][...] = delta
            outs[2 * ns + i][...] = mn
            outs[3 * ns + i][...] = vn

    shapes = [jax.ShapeDtypeStruct(w[n].shape, F32) for n in _SMALL]
    res = pl.pallas_call(body, name="adamw_small", out_shape=shapes * 4 + [jax.ShapeDtypeStruct((1, 1), F32)])(
        *[w[n] for n in _SMALL], *[m[n] for n in _SMALL], *[v[n] for n in _SMALL], got)
    return [dict(zip(_SMALL, res[k * ns:(k + 1) * ns])) for k in range(4)], res[4 * ns]


def _w_in_internal(slabs):
    S, R, D = slabs.shape
    assert S * R == D_IN

    def body(w_ref, o_ref):
        for n, wd in _REAL:
            o, oi = _REAL_OFF[n][0], _INT_OFF[n]
            r = o
            while r < o + wd:
                end = min(o + wd, (r // R + 1) * R)
                o_ref[oi + r - o:oi + end - o, :] = w_ref[r // R, r % R:r % R + end - r, :]
                r = end
            if _INT_W[n] != wd:
                o_ref[oi + wd:oi + _INT_W[n], :] = jnp.zeros((_INT_W[n] - wd, D), slabs.dtype)

    return pl.pallas_call(
        body, name="w_in_internal", out_shape=jax.ShapeDtypeStruct((N_INT, D), slabs.dtype),
        compiler_params=pltpu.CompilerParams(vmem_limit_bytes=VMEM_LIMIT),
    )(slabs)


def _uq_internal(w):
    return jnp.pad(w.reshape(256, 4, 96), ((0, 0), (0, 0), (0, 32))).reshape(256, 512)


def _uq_real(dw):
    return dw.reshape(256, 4, 128)[:, :, :96].reshape(256, 384)


def _ukv_internal(w):
    w4 = w.reshape(128, 4, 128)
    k = jnp.pad(w4[:, :, :64], ((0, 0), (0, 0), (0, 64))).reshape(128, 512)
    return jnp.concatenate([k, w4[:, :, 64:].reshape(128, 256)], axis=1)


def _ukv_real(dw):
    k = dw[:, :512].reshape(128, 4, 128)[:, :, :64]
    v = dw[:, 512:].reshape(128, 4, 64)
    return jnp.concatenate([k, v], axis=2).reshape(128, 512)


def _layer_fwd(x, rope, p, tgt=None, fetch=()):
    xn, hb, hf = _inproj_fwd(x, p["norm_pre"], p["w_in"])
    ya = _swa_fwd(hb, p["attn_sinks"])
    qm, km, vm, vt = _mla_prep_fwd(hf, rope, p["mla_q_norm"], p["mla_kv_norm"], p["mla_w_uq"], p["mla_w_ukv"])
    yc, lse, *fetched = _mla_fwd(qm, km, vt, fetch)
    yd, tot, cnt = _sb_fwd(hb)
    x_next = _epilogue_fwd(x, ya, yc, yd, hf, p["conv_w"], p["conv_b"], p["group_norm"], p["w_out"], p["norm_post"],
                           tgt)
    saved = dict(x=x, xn=xn, hb=hb, hf=hf, ya=ya, yc=yc, yd=yd, tot=tot, cnt=cnt, qm=qm, km=km, vm=vm, lse=lse)
    return x_next, saved, fetched


def _layer_bwd(dx_next, rope, p, s):
    (dya, dyc, dyd, dhf, dw_out, dg_post, dg_grp, dconv_w, dconv_b) = _epilogue_bwd(
        dx_next, s["ya"], s["yc"], s["yd"], s["hf"], p["conv_w"], p["conv_b"], p["group_norm"], p["w_out"],
        p["norm_post"])
    dq_d, dk_d, dv_d = _sb_bwd(s["hb"], s["tot"], s["cnt"], dyd)
    dqm, dkt, dvt = _mla_bwd(s["qm"], s["km"], s["vm"], s["yc"], s["lse"], dyc)
    dc, dw_uq, dw_ukv, dg_q, dg_kv = _mla_prep_bwd(
        s["hf"], rope, p["mla_q_norm"], p["mla_kv_norm"], p["mla_w_uq"], p["mla_w_ukv"], dqm, dkt, dvt)
    dq_a, dk_a, dv_a, dsinks = _swa_bwd(s["hb"], p["attn_sinks"], dya)
    dx, dh, dg_pre = _inproj_bwd_dx(s["x"], p["norm_pre"], p["w_in"], dx_next,
                                    [dq_a, dk_a, dv_a, dq_d, dk_d, dv_d, dhf, dc])
    grads = dict(norm_pre=dg_pre[0], w_in_t=_inproj_bwd_dw(s["xn"], dh), attn_sinks=dsinks[0, :4], conv_w=dconv_w[:3],
                 conv_b=dconv_b[0], mla_q_norm=dg_q[0], mla_w_uq=_uq_real(dw_uq), mla_kv_norm=dg_kv[0],
                 mla_w_ukv=_ukv_real(dw_ukv), group_norm=dg_grp[0], w_out=dw_out, norm_post=dg_post[0])
    return dx, grads


_WEIGHTS = ["norm_pre", "w_in", "attn_sinks", "conv_w", "conv_b", "mla_q_norm", "mla_w_uq", "mla_kv_norm",
            "mla_w_ukv", "group_norm", "w_out", "norm_post"]


def kernel(x, positions, norm_pre, w_in, attn_sinks, conv_w, conv_b, mla_q_norm, mla_w_uq, mla_kv_norm, mla_w_ukv, group_norm, w_out, norm_post, loss_target, m_norm_pre, m_w_in, m_attn_sinks, m_conv_w, m_conv_b, m_mla_q_norm, m_mla_w_uq, m_mla_kv_norm, m_mla_w_ukv, m_group_norm, m_w_out, m_norm_post, v_norm_pre, v_w_in, v_attn_sinks, v_conv_w, v_conv_b, v_mla_q_norm, v_mla_w_uq, v_mla_kv_norm, v_mla_w_ukv, v_group_norm, v_w_out, v_norm_post):
    w = dict(norm_pre=norm_pre, w_in=w_in, attn_sinks=attn_sinks, conv_w=conv_w, conv_b=conv_b,
             mla_q_norm=mla_q_norm, mla_w_uq=mla_w_uq, mla_kv_norm=mla_kv_norm, mla_w_ukv=mla_w_ukv,
             group_norm=group_norm, w_out=w_out, norm_post=norm_post)
    m = dict(norm_pre=m_norm_pre, w_in=m_w_in, attn_sinks=m_attn_sinks, conv_w=m_conv_w, conv_b=m_conv_b,
             mla_q_norm=m_mla_q_norm, mla_w_uq=m_mla_w_uq, mla_kv_norm=m_mla_kv_norm, mla_w_ukv=m_mla_w_ukv,
             group_norm=m_group_norm, w_out=m_w_out, norm_post=m_norm_post)
    v = dict(norm_pre=v_norm_pre, w_in=v_w_in, attn_sinks=v_attn_sinks, conv_w=v_conv_w, conv_b=v_conv_b,
             mla_q_norm=v_mla_q_norm, mla_w_uq=v_mla_w_uq, mla_kv_norm=v_mla_kv_norm, mla_w_ukv=v_mla_w_ukv,
             group_norm=v_group_norm, w_out=v_w_out, norm_post=v_norm_post)
    T = x.shape[1]
    xs = x[0]
    tgt = loss_target[0]

    def shard_parts(l):
        halves = lambda a: a.reshape((2, a.shape[0] // 2) + a.shape[1:])
        return [halves(jnp.swapaxes(w["w_in"][l], 0, 1).astype(BF16))] + [
            halves(w[n][l].astype(BF16)) for n in _BIG[1:4]] + [jnp.stack([w["conv_w"][l]] * 2)]

    def layer_params(l, got):
        whole = lambda a: a.reshape((4, 2 * a.shape[2]) + a.shape[3:])
        by_cols = lambda a: jnp.transpose(a, (1, 0, 2)).reshape(a.shape[1], 4 * a.shape[2])
        return dict(
            norm_pre=norm_pre[l:l + 1], w_in=_w_in_internal(got[0].reshape((8,) + got[0].shape[2:])),
            attn_sinks=attn_sinks[l], conv_w=by_cols(got[4][:, 0]), conv_b=conv_b[l:l + 1],
            mla_q_norm=mla_q_norm[l:l + 1], mla_w_uq=_uq_internal(by_cols(whole(got[2]))),
            mla_kv_norm=mla_kv_norm[l:l + 1], mla_w_ukv=_ukv_internal(by_cols(whole(got[3]))),
            group_norm=group_norm[l:l + 1], w_out=whole(got[1]).reshape(D_MODEL, D_MODEL),
            norm_post=norm_post[l:l + 1])

    layers, saved = [], []
    rope, *got = _rope_tables(positions[0].reshape(T, 1), shard_parts(0))
    h = xs
    for l in range(DEPTH):
        last = l == DEPTH - 1
        layers.append(layer_params(l, got))
        h, s, got = _layer_fwd(h, rope, layers[l], tgt if last else None, () if last else shard_parts(l + 1))
        saved.append(s)
    dy, loss_part = h

    grads = [None] * DEPTH
    for l in reversed(range(DEPTH)):
        dy, grads[l] = _layer_bwd(dy, rope, layers[l], saved[l])

    turned = ("w_in", "mla_w_uq")
    turn = lambda n, a: jnp.swapaxes(a, -1, -2) if n in turned else a

    def chunks(n, a):
        if n in ("w_out", "w_in"):
            return a.reshape(4, a.shape[0] // 4, a.shape[1])
        if n in turned:
            return a.T.reshape(4, a.shape[1] // 4, a.shape[0])
        return jnp.transpose(a.reshape(a.shape[0], 4, a.shape[1] // 4), (1, 0, 2))

    grad = lambda l, n: grads[l]["w_in_t" if n == "w_in" else n]
    per_layer = [[chunks(n, grad(l, n)) for n in _BIG] for l in range(DEPTH)]
    from_sibling = _swap_layer_chunks(per_layer)
    summed = [_add(a0, a1, b, "add_cores_" + n) for n, a0, a1, b in zip(_BIG, *per_layer, from_sibling)]
    small = _pack_small({n: jnp.stack([grads[l][n] for l in range(DEPTH)]) for n in _SMALL}, loss_part[0, 0])
    *got, got_small = _exchange_chips(summed, small)
    done = [_sum_leading(b, "sum_chips_" + n) for n, b in zip(_BIG, got)]
    done_other = _swap_cores(done, "swap_layer_shards")

    outs, loss = _adamw_small(w, m, v, got_small)
    for n, gm, go in zip(_BIG, done, done_other):
        for d, a in zip(outs, _adamw_layers(turn(n, w[n]), turn(n, m[n]), turn(n, v[n]), gm, go, "adamw_" + n)):
            d[n] = turn(n, a)
    return (loss[0, 0], dy[None], *[outs[0][n] for n in _WEIGHTS], *[outs[1][n] for n in _WEIGHTS],
            *[outs[2][n] for n in _WEIGHTS], *[outs[3][n] for n in _WEIGHTS])
```

```python
import math

import jax
import jax.numpy as jnp
from jax import lax
from jax.experimental import pallas as pl
from jax.experimental.pallas import tpu as pltpu

F32 = jnp.float32
BF16 = jnp.bfloat16
MESH = pl.DeviceIdType.MESH

D_MODEL = 1024
DEPTH = 2
EPS = 1e-6
BLOCK = 128
HEAD = 64
LANES = 128
GROUP = 256
LOG2E = 1.4426950408889634
LN2 = 0.6931471805599453
MLA_QSCALE = 96 ** -0.5 * LOG2E
ROPE_HALF = 16
ROPE_THETA = 10000.0
SWA_SUB = 2
ATT_BLK = 256
MLA_BQ = 512
NEG = -1e30
SB_DEAD = -104.0

ADAM_LR, ADAM_B1, ADAM_B2, ADAM_EPS, ADAM_WD, ADAM_STEP = 0.001, 0.9, 0.999, 1e-08, 0.01, 10

_REAL = [("a_q", 256), ("a_k", 128), ("a_v", 128), ("b_b", 256), ("b_c", 256), ("b_x", 256),
         ("c_q", 256), ("c_kv", 128), ("c_kr", 32), ("d_q", 256), ("d_k", 256), ("d_v", 256),
         ("gate", 1024)]
_REAL_OFF = {}
_o = 0
for _n, _w in _REAL:
    _REAL_OFF[_n] = (_o, _w)
    _o += _w
D_IN = _o
_INT_ORDER = ["a_q", "a_k", "a_v", "d_q", "d_k", "d_v", "gate", "b_b", "b_c", "b_x", "c_q", "c_kv", "c_kr"]
_INT_W = dict(_REAL)
_INT_W["c_kr"] = 128
_INT_OFF = {}
_o = 0
for _n in _INT_ORDER:
    _INT_OFF[_n] = _o
    _o += _INT_W[_n]
N_INT = _o
N_HB = _INT_OFF["gate"]
N_HF = N_INT - N_HB

VMEM_LIMIT = 56 * 1024 * 1024


def _cparams(sem):
    return pltpu.CompilerParams(dimension_semantics=sem, vmem_limit_bytes=VMEM_LIMIT)


def _dot(a, b):
    return jnp.dot(a, b, preferred_element_type=F32)


def _dot_nt(a, b):
    return lax.dot_general(a, b, (((1,), (1,)), ((), ())), preferred_element_type=F32)


def _dot_tn(a, b):
    return lax.dot_general(a, b, (((0,), (0,)), ((), ())), preferred_element_type=F32)


def _split(x):
    hi = x.astype(BF16)
    lo = (x - hi.astype(F32)).astype(BF16)
    return hi, lo


def _rms(x):
    return lax.rsqrt(jnp.mean(x * x, axis=-1, keepdims=True) + EPS)


def _rms_bwd(dy, xhat, r, g):
    dxhat = dy * g
    return r * (dxhat - xhat * jnp.mean(dxhat * xhat, axis=-1, keepdims=True)), dy * xhat


def _colsum(x):
    return jnp.sum(x, axis=0, keepdims=True)


def _inproj_fwd(x, g, wt):
    T = x.shape[0]
    tm = 512

    def body(x_ref, g_ref, w_ref, xn_ref, hb_ref, hf_ref):
        xv = x_ref[...]
        xn = (xv * _rms(xv) * g_ref[...]).astype(BF16)
        xn_ref[...] = xn
        h = _dot_nt(xn, w_ref[...])
        hb_ref[...] = h[:, :N_HB].astype(BF16)
        hf_ref[...] = h[:, N_HB:]

    return pl.pallas_call(
        body, name="inproj_fwd", grid=(T // tm,),
        in_specs=[pl.BlockSpec((tm, D_MODEL), lambda i: (i, 0)),
                  pl.BlockSpec((1, D_MODEL), lambda i: (0, 0)),
                  pl.BlockSpec((N_INT, D_MODEL), lambda i: (0, 0))],
        out_specs=[pl.BlockSpec((tm, D_MODEL), lambda i: (i, 0)),
                   pl.BlockSpec((tm, N_HB), lambda i: (i, 0)),
                   pl.BlockSpec((tm, N_HF), lambda i: (i, 0))],
        out_shape=[jax.ShapeDtypeStruct((T, D_MODEL), BF16),
                   jax.ShapeDtypeStruct((T, N_HB), BF16),
                   jax.ShapeDtypeStruct((T, N_HF), F32)],
        compiler_params=_cparams(("parallel",)),
    )(x, g, wt)


def _inproj_bwd(x, g, wt, xn, dx_next, pieces):
    T = x.shape[0]
    tm = 256
    nt, npc = T // tm, len(pieces)
    widths = [p.shape[1] for p in pieces]
    assert sum(widths) == N_INT

    def body(x_ref, g_ref, w_ref, xn_ref, dxn_ref, *rest):
        p_refs = rest[:npc]
        dx_ref, dg_ref, dw_ref, acc_ref, sem = rest[npc:]

        @pl.when(pl.program_id(0) == 0)
        def _():
            dg_ref[...] = jnp.zeros_like(dg_ref)
            acc_ref[...] = jnp.zeros_like(acc_ref)

        dh = jnp.concatenate([p[...].astype(BF16) for p in p_refs], axis=1)
        dxn = _dot(dh, w_ref[...])
        xv = x_ref[...]
        r = _rms(xv)
        dx, dgrow = _rms_bwd(dxn, xv * r, r, g_ref[...])
        dx_ref[...] = dx + dxn_ref[...]
        dg_ref[...] += _colsum(dgrow)
        for n, wd in _REAL:
            o, oi = _REAL_OFF[n][0], _INT_OFF[n]
            acc_ref[o:o + wd, :] += _dot_tn(dh[:, oi:oi + _INT_W[n]], xn_ref[...])[:wd]

        @pl.when(pl.program_id(0) == nt - 1)
        def _():
            out = pltpu.make_async_copy(acc_ref, dw_ref, sem.at[0])
            out.start()
            out.wait()

    return pl.pallas_call(
        body, name="inproj_bwd", grid=(nt,),
        in_specs=[pl.BlockSpec((tm, D_MODEL), lambda i: (i, 0)),
                  pl.BlockSpec((1, D_MODEL), lambda i: (0, 0)),
                  pl.BlockSpec((N_INT, D_MODEL), lambda i: (0, 0)),
                  pl.BlockSpec((tm, D_MODEL), lambda i: (i, 0)),
                  pl.BlockSpec((tm, D_MODEL), lambda i: (i, 0))]
                 + [pl.BlockSpec((tm, wd), lambda i: (i, 0)) for wd in widths],
        out_specs=[pl.BlockSpec((tm, D_MODEL), lambda i: (i, 0)),
                   pl.BlockSpec((1, D_MODEL), lambda i: (0, 0)),
                   pl.BlockSpec(memory_space=pl.ANY)],
        out_shape=[jax.ShapeDtypeStruct((T, D_MODEL), F32),
                   jax.ShapeDtypeStruct((1, D_MODEL), F32),
                   jax.ShapeDtypeStruct((D_IN, D_MODEL), F32)],
        scratch_shapes=[pltpu.VMEM((D_IN, D_MODEL), F32), pltpu.SemaphoreType.DMA((1,))],
        compiler_params=_cparams(("arbitrary",)),
    )(x, g, wt, xn, dx_next, *pieces)


def _roll_f32(x, shift):
    return pltpu.roll(x.astype(F32), shift, 1)


def _swa_operands(h, q, k_prev, k_cur, v_prev, v_cur):
    p, e = h // 2, h % 2
    lane = lax.broadcasted_iota(jnp.int32, (1, LANES), 1) // HEAD
    q = q[:, p * LANES:(p + 1) * LANES]
    if e != p:
        q = _roll_f32(q, HEAD).astype(BF16)
        v_prev = _roll_f32(v_prev, HEAD).astype(BF16)
        v_cur = _roll_f32(v_cur, HEAD).astype(BF16)
    qs = jnp.where(lane == p, q, 0) * 0.125
    return dict(p=p, e=e, lane=lane, qs=qs, k_prev=k_prev, k_cur=k_cur,
                v_prev=jnp.where(lane == e, v_prev, 0), v_cur=jnp.where(lane == e, v_cur, 0),
                s_prev=_dot_nt(qs, k_prev), s_cur=_dot_nt(qs, k_cur))


def _swa_probs(ops, sink, no_prev):
    row = lax.broadcasted_iota(jnp.int32, (BLOCK, BLOCK), 0)
    col = lax.broadcasted_iota(jnp.int32, (BLOCK, BLOCK), 1)
    ok_prev = col > row if no_prev is None else jnp.logical_and(col > row, jnp.logical_not(no_prev))
    s_prev = jnp.where(ok_prev, ops["s_prev"], NEG)
    s_cur = jnp.where(col <= row, ops["s_cur"], NEG)
    m = jnp.maximum(jnp.maximum(jnp.max(s_prev, axis=1, keepdims=True),
                                jnp.max(s_cur, axis=1, keepdims=True)), sink)
    p_prev = jnp.exp(s_prev - m)
    p_cur = jnp.exp(s_cur - m)
    p_sink = jnp.exp(sink - m)
    inv = 1.0 / (jnp.sum(p_prev, axis=1, keepdims=True) + jnp.sum(p_cur, axis=1, keepdims=True) + p_sink)
    return p_prev * inv, p_cur * inv, p_sink * inv


def _swa_specs(T):
    n = T // (BLOCK * SWA_SUB)
    qo, ko, vo = (_INT_OFF[name] // LANES for name in ("a_q", "a_k", "a_v"))
    halo = lambda i: jnp.maximum(i * SWA_SUB - 1, 0)
    return [pl.BlockSpec((BLOCK * SWA_SUB, 256), lambda i: (i, qo // 2)),
            pl.BlockSpec((BLOCK, LANES), lambda i: (halo(i), ko)),
            pl.BlockSpec((BLOCK * SWA_SUB, LANES), lambda i: (i, ko)),
            pl.BlockSpec((BLOCK, LANES), lambda i: (halo(i), vo)),
            pl.BlockSpec((BLOCK * SWA_SUB, LANES), lambda i: (i, vo)),
            pl.BlockSpec(memory_space=pltpu.SMEM)], n


def _swa_units(q_ref, kh_ref, kc_ref, vh_ref, vc_ref, s_ref):
    blk = lambda a: slice(a * BLOCK, (a + 1) * BLOCK)
    units = [(a, h) for a in range(SWA_SUB) for h in range(4)]
    ops = {}
    for a, h in units:
        k_prev, v_prev = (kh_ref[...], vh_ref[...]) if a == 0 else (kc_ref[blk(a - 1), :], vc_ref[blk(a - 1), :])
        ops[a, h] = _swa_operands(h, q_ref[blk(a), :], k_prev, kc_ref[blk(a), :], v_prev, vc_ref[blk(a), :])
    probs = {(a, h): _swa_probs(ops[a, h], s_ref[h], pl.program_id(0) == 0 if a == 0 else None) for a, h in units}
    return units, ops, probs, blk


def _swa_fwd(hb, sinks):
    T = hb.shape[0]
    specs, n = _swa_specs(T)

    def body(q_ref, kh_ref, kc_ref, vh_ref, vc_ref, s_ref, o_ref):
        units, ops, probs, blk = _swa_units(q_ref, kh_ref, kc_ref, vh_ref, vc_ref, s_ref)
        outs = {u: _dot(probs[u][0].astype(BF16), ops[u]["v_prev"]) + _dot(probs[u][1].astype(BF16), ops[u]["v_cur"])
                for u in units}
        for a in range(SWA_SUB):
            for p in range(2):
                o_ref[blk(a), p * LANES:(p + 1) * LANES] = outs[a, 2 * p] + outs[a, 2 * p + 1]

    return pl.pallas_call(
        body, name="swa_fwd", grid=(n,), in_specs=specs,
        out_specs=pl.BlockSpec((BLOCK * SWA_SUB, 256), lambda i: (i, 0)),
        out_shape=jax.ShapeDtypeStruct((T, 256), F32),
        compiler_params=_cparams(("parallel",)),
    )(hb, hb, hb, hb, hb, sinks)


def _swa_bwd(hb, sinks, dy):
    T = hb.shape[0]
    specs, n = _swa_specs(T)

    def body(q_ref, kh_ref, kc_ref, vh_ref, vc_ref, s_ref, dy_ref, dq_ref, dk_ref, dv_ref, ds_ref):
        i = pl.program_id(0)

        @pl.when(i == 0)
        def _():
            ds_ref[...] = jnp.zeros_like(ds_ref)

        lane_id = lax.broadcasted_iota(jnp.int32, (8, LANES), 1)
        units, ops, probs, blk = _swa_units(q_ref, kh_ref, kc_ref, vh_ref, vc_ref, s_ref)
        dos = {(a, h): jnp.where(ops[a, h]["lane"] == ops[a, h]["e"],
                                 dy_ref[blk(a), ops[a, h]["p"] * LANES:(ops[a, h]["p"] + 1) * LANES], 0.0)
               for a, h in units}
        dobs = {u: dos[u].astype(BF16) for u in units}
        pbs = {u: (probs[u][0].astype(BF16), probs[u][1].astype(BF16)) for u in units}
        outs = {u: _dot(pbs[u][0], ops[u]["v_prev"]) + _dot(pbs[u][1], ops[u]["v_cur"]) for u in units}
        dps = {u: (_dot_nt(dobs[u], ops[u]["v_prev"]), _dot_nt(dobs[u], ops[u]["v_cur"])) for u in units}
        dss, dsinks = {}, jnp.zeros((8, LANES), F32)
        for u in units:
            delta = jnp.sum(dos[u] * outs[u], axis=1, keepdims=True)
            dss[u] = ((probs[u][0] * (dps[u][0] - delta)).astype(BF16),
                      (probs[u][1] * (dps[u][1] - delta)).astype(BF16))
            dsink = -jnp.sum(probs[u][2] * delta, axis=0, keepdims=True)
            dsinks += jnp.where(lane_id == u[1], dsink, 0.0)
        ds_ref[...] += dsinks
        dqs = {u: (_dot(dss[u][0], ops[u]["k_prev"]) + _dot(dss[u][1], ops[u]["k_cur"])) * 0.125 for u in units}
        zero = jnp.zeros((BLOCK, LANES), F32)
        dk_as_prev, dk_as_cur = [zero] * SWA_SUB, [zero] * SWA_SUB
        dv_as_prev, dv_as_cur = [zero] * SWA_SUB, [zero] * SWA_SUB
        for a, h in units:
            p, e = ops[a, h]["p"], ops[a, h]["e"]
            dob_v = dobs[a, h] if e == p else pltpu.roll(dos[a, h], HEAD, 1).astype(BF16)
            dk_as_prev[a] = dk_as_prev[a] + _dot_tn(dss[a, h][0], ops[a, h]["qs"])
            dk_as_cur[a] = dk_as_cur[a] + _dot_tn(dss[a, h][1], ops[a, h]["qs"])
            dv_as_prev[a] = dv_as_prev[a] + _dot_tn(pbs[a, h][0], dob_v)
            dv_as_cur[a] = dv_as_cur[a] + _dot_tn(pbs[a, h][1], dob_v)
        base = i * SWA_SUB
        for a in range(SWA_SUB):
            rows = pl.ds(pl.multiple_of((base + a) * BLOCK, BLOCK), BLOCK)
            more = a + 1 < SWA_SUB
            dk_ref[rows, :] = dk_as_cur[a] + (dk_as_prev[a + 1] if more else 0.0)
            dv_ref[rows, :] = dv_as_cur[a] + (dv_as_prev[a + 1] if more else 0.0)
        halo = pl.ds(pl.multiple_of(jnp.maximum(base - 1, 0) * BLOCK, BLOCK), BLOCK)
        dk_ref[halo, :] += dk_as_prev[0]
        dv_ref[halo, :] += dv_as_prev[0]
        for a in range(SWA_SUB):
            for p in range(2):
                dq_pair = jnp.zeros((BLOCK, LANES), F32)
                for e in range(2):
                    dq = jnp.where(ops[a, 2 * p + e]["lane"] == p, dqs[a, 2 * p + e], 0.0)
                    dq_pair += dq if e == p else pltpu.roll(dq, HEAD, 1)
                dq_ref[blk(a), p * LANES:(p + 1) * LANES] = dq_pair.astype(BF16)

    return pl.pallas_call(
        body, name="swa_bwd", grid=(n,),
        in_specs=specs + [pl.BlockSpec((BLOCK * SWA_SUB, 256), lambda i: (i, 0))],
        out_specs=[pl.BlockSpec((BLOCK * SWA_SUB, 256), lambda i: (i, 0)),
                   pl.BlockSpec((T, LANES), lambda i: (0, 0)),
                   pl.BlockSpec((T, LANES), lambda i: (0, 0)),
                   pl.BlockSpec((8, LANES), lambda i: (0, 0))],
        out_shape=[jax.ShapeDtypeStruct((T, 256), BF16),
                   jax.ShapeDtypeStruct((T, LANES), F32),
                   jax.ShapeDtypeStruct((T, LANES), F32),
                   jax.ShapeDtypeStruct((8, LANES), F32)],
        compiler_params=_cparams(("arbitrary",)),
    )(hb, hb, hb, hb, hb, sinks, dy)


def _rope_tables(pos, fetch=()):
    T = pos.shape[0]
    tm = 512
    nt, n = T // tm, len(fetch)

    def body(pos_ref, *rest):
        o_ref = rest[n]
        if n:
            start, finish = _gather_plan(rest[:n], rest[n + 1:2 * n + 1], rest[2 * n + 1:])
            pl.when(pl.program_id(0) == 0)(start)
        lane = lax.broadcasted_iota(jnp.int32, (1, LANES), 1)
        active = jnp.logical_and(lane >= HEAD, lane < HEAD + 2 * ROPE_HALF)
        idx = ((lane - HEAD) % ROPE_HALF).astype(F32)
        freq = jnp.exp(idx * (-math.log(ROPE_THETA) / ROPE_HALF))
        ang = pos_ref[...].astype(F32) * freq
        cos, sin = jnp.cos(ang), jnp.sin(ang)
        o_ref[:, 0:LANES] = jnp.where(active, cos, 1.0)
        o_ref[:, LANES:2 * LANES] = jnp.where(jnp.logical_and(active, lane >= HEAD + ROPE_HALF), sin, 0.0)
        o_ref[:, 2 * LANES:] = jnp.where(jnp.logical_and(active, lane < HEAD + ROPE_HALF), -sin, 0.0)
        if n:
            pl.when(pl.program_id(0) == nt - 1)(finish)

    return pl.pallas_call(
        body, name="rope_tables_fetch" if n else "rope_tables", grid=(nt,),
        in_specs=[pl.BlockSpec((tm, 1), lambda i: (i, 0))] + [HBM] * n,
        out_specs=[pl.BlockSpec((tm, 3 * LANES), lambda i: (i, 0))] + [HBM] * n,
        out_shape=[jax.ShapeDtypeStruct((T, 3 * LANES), F32)]
                  + [jax.ShapeDtypeStruct((4,) + s.shape, s.dtype) for s in fetch],
        scratch_shapes=_gather_sems(n) if n else [],
        compiler_params=_cparams(("arbitrary",) if n else ("parallel",)),
    )(pos, *fetch)


def _rope_factors(tab_ref):
    return tab_ref[:, 0:LANES], tab_ref[:, LANES:2 * LANES], tab_ref[:, 2 * LANES:]


def _rope(x, tabs):
    c, s_up, s_dn = tabs
    return x * c + pltpu.roll(x, ROPE_HALF, 1) * s_up + pltpu.roll(x, LANES - ROPE_HALF, 1) * s_dn


def _rope_t(dy, tabs):
    c, s_up, s_dn = tabs
    return dy * c + pltpu.roll(dy * s_up, LANES - ROPE_HALF, 1) + pltpu.roll(dy * s_dn, ROPE_HALF, 1)


def _mla_lat_specs(tm):
    cq, ckv, ckr = ((_INT_OFF[n] - N_HB) for n in ("c_q", "c_kv", "c_kr"))
    return [pl.BlockSpec((tm, 256), lambda i: (i, cq // 256)),
            pl.BlockSpec((tm, LANES), lambda i: (i, ckv // LANES)),
            pl.BlockSpec((tm, LANES), lambda i: (i, ckr // LANES)),
            pl.BlockSpec((tm, 3 * LANES), lambda i: (i, 0)),
            pl.BlockSpec((1, 256), lambda i: (0, 0)),
            pl.BlockSpec((1, LANES), lambda i: (0, 0)),
            pl.BlockSpec((256, 512), lambda i: (0, 0)),
            pl.BlockSpec((LANES, 768), lambda i: (0, 0))]


def _mla_prep_fwd(hf, rope, g_q, g_kv, w_uq, w_ukv):
    T = hf.shape[0]
    tm = 512
    sub = tm // ATT_BLK

    def body(cq_ref, ckv_ref, ckr_ref, tab_ref, gq_ref, gkv_ref, wq_ref, wkv_ref, qm_ref, km_ref, vm_ref, vt_ref):
        tabs = _rope_factors(tab_ref)
        cq = cq_ref[...]
        q = _dot((cq * _rms(cq) * gq_ref[...]).astype(BF16), wq_ref[...])
        ckv = ckv_ref[...]
        kv = _dot((ckv * _rms(ckv) * gkv_ref[...]).astype(BF16), wkv_ref[...])
        kr = _rope(pltpu.roll(ckr_ref[...], HEAD, 1), tabs)
        for h in range(4):
            sl = slice(h * LANES, (h + 1) * LANES)
            qm_ref[:, sl] = (_rope(q[:, sl], tabs) * MLA_QSCALE).astype(BF16)
            km_ref[:, sl] = (kv[:, sl] + kr).astype(BF16)
        vm_ref[...] = kv[:, 512:].astype(BF16)
        for p in range(2):
            for s in range(sub):
                tile = kv[s * ATT_BLK:(s + 1) * ATT_BLK, 512 + p * LANES:512 + (p + 1) * LANES]
                vt_ref[p, s] = jnp.transpose(tile).astype(BF16)

    return pl.pallas_call(
        body, name="mla_prep_fwd", grid=(T // tm,), in_specs=_mla_lat_specs(tm),
        out_specs=[pl.BlockSpec((tm, 512), lambda i: (i, 0)),
                   pl.BlockSpec((tm, 512), lambda i: (i, 0)),
                   pl.BlockSpec((tm, 256), lambda i: (i, 0)),
                   pl.BlockSpec((2, sub, LANES, ATT_BLK), lambda i: (0, i, 0, 0))],
        out_shape=[jax.ShapeDtypeStruct((T, 512), BF16),
                   jax.ShapeDtypeStruct((T, 512), BF16),
                   jax.ShapeDtypeStruct((T, 256), BF16),
                   jax.ShapeDtypeStruct((2, T // ATT_BLK, LANES, ATT_BLK), BF16)],
        compiler_params=_cparams(("parallel",)),
    )(hf, hf, hf, rope, g_q, g_kv, w_uq, w_ukv)


def _mla_prep_bwd(hf, rope, g_q, g_kv, w_uq, w_ukv, dqm, dkt, dvt):
    T = hf.shape[0]
    tm = 512
    sub = tm // ATT_BLK

    def body(cq_ref, ckv_ref, ckr_ref, tab_ref, gq_ref, gkv_ref, wq_ref, wkv_ref, dq_ref, dk_ref, dv_ref,
             dc_ref, dwq_ref, dwkv_ref, dgq_ref, dgkv_ref):
        @pl.when(pl.program_id(0) == 0)
        def _():
            dwq_ref[...] = jnp.zeros_like(dwq_ref)
            dwkv_ref[...] = jnp.zeros_like(dwkv_ref)
            dgq_ref[...] = jnp.zeros_like(dgq_ref)
            dgkv_ref[...] = jnp.zeros_like(dgkv_ref)

        tabs = _rope_factors(tab_ref)
        lane =lax.broadcasted_iota(jnp.int32, (1, LANES), 1)
        dq = jnp.concatenate([_rope_t(dq_ref[:, h * LANES:(h + 1) * LANES] * MLA_QSCALE, tabs)
                              for h in range(4)], axis=1).astype(BF16)
        cq = cq_ref[...]
        rq = _rms(cq)
        cqn = (cq * rq * gq_ref[...]).astype(BF16)
        dwq_ref[...] += _dot_tn(cqn, dq)
        dcq, dgrow = _rms_bwd(_dot_nt(dq, wq_ref[...]), cq * rq, rq, gq_ref[...])
        dgq_ref[...] += _colsum(dgrow)
        dc_ref[:, 0:256] = dcq.astype(BF16)

        dk = jnp.concatenate([jnp.concatenate([jnp.transpose(dk_ref[p, s]) for p in range(2)], axis=1)
                              for s in range(sub)], axis=0) * LN2
        dv = jnp.concatenate([jnp.concatenate([jnp.transpose(dv_ref[p, s]) for p in range(2)], axis=1)
                              for s in range(sub)], axis=0)
        dkr = dk[:, 0:LANES] + dk[:, LANES:2 * LANES] + dk[:, 2 * LANES:3 * LANES] + dk[:, 3 * LANES:]
        dkr = pltpu.roll(_rope_t(dkr, tabs), HEAD, 1)
        dc_ref[:, 384:512] = jnp.where(lane < 2 * ROPE_HALF, dkr, 0.0).astype(BF16)
        dkv = jnp.concatenate([dk.astype(BF16), dv.astype(BF16)], axis=1)
        ckv = ckv_ref[...]
        rkv = _rms(ckv)
        ckvn = (ckv * rkv * gkv_ref[...]).astype(BF16)
        dwkv_ref[...] += _dot_tn(ckvn, dkv)
        dckv, dgrow = _rms_bwd(_dot_nt(dkv, wkv_ref[...]), ckv * rkv, rkv, gkv_ref[...])
        dgkv_ref[...] += _colsum(dgrow)
        dc_ref[:, 256:384] = dckv.astype(BF16)

    return pl.pallas_call(
        body, name="mla_prep_bwd", grid=(T // tm,),
        in_specs=_mla_lat_specs(tm) + [pl.BlockSpec((tm, 512), lambda i: (i, 0)),
                                       pl.BlockSpec((2, sub, 256, ATT_BLK), lambda i: (0, i, 0, 0)),
                                       pl.BlockSpec((2, sub, LANES, ATT_BLK), lambda i: (0, i, 0, 0))],
        out_specs=[pl.BlockSpec((tm, 512), lambda i: (i, 0)),
                   pl.BlockSpec((256, 512), lambda i: (0, 0)),
                   pl.BlockSpec((LANES, 768), lambda i: (0, 0)),
                   pl.BlockSpec((1, 256), lambda i: (0, 0)),
                   pl.BlockSpec((1, LANES), lambda i: (0, 0))],
        out_shape=[jax.ShapeDtypeStruct((T, 512), BF16),
                   jax.ShapeDtypeStruct((256, 512), F32),
                   jax.ShapeDtypeStruct((LANES, 768), F32),
                   jax.ShapeDtypeStruct((1, 256), F32),
                   jax.ShapeDtypeStruct((1, LANES), F32)],
        compiler_params=_cparams(("arbitrary",)),
    )(hf, hf, hf, rope, g_q, g_kv, w_uq, w_ukv, dqm, dkt, dvt)


def _causal_masks(bq, bk):
    row = lax.broadcasted_iota(jnp.int32, (bq, bk), 0)
    col = lax.broadcasted_iota(jnp.int32, (bq, bk), 1)
    return row, col


def _mla_fwd(qm, km, vt, fetch=()):
    T = qm.shape[0]
    bq, bk = min(MLA_BQ, T), ATT_BLK
    nq, nsub, nk = T // bq, bq // bk, T // bk
    n = len(fetch)

    def body(q_ref, k_ref, vt_ref, *rest):
        o_ref, lse_ref = rest[n:n + 2]
        acc_ref, m_ref, l_ref = rest[2 * n + 2:2 * n + 5]
        if n:
            start, finish = _gather_plan(rest[:n], rest[n + 2:2 * n + 2], rest[2 * n + 5:])
            pl.when(pl.program_id(0) == 0)(start)
        qi = pl.program_id(0)
        key = lax.broadcasted_iota(jnp.int32, (bk, bq), 0)
        qry = lax.broadcasted_iota(jnp.int32, (bk, bq), 1)
        ones = jnp.ones((8, bk), BF16)
        acc_ref[...] = jnp.zeros_like(acc_ref)
        m_ref[...] = jnp.full_like(m_ref, NEG)
        l_ref[...] = jnp.zeros_like(l_ref)

        def step(kb0, masked):
            kbs = [kb0 + d for d in range(nsub)]
            qs = [slice(d * bk if masked else 0, bq) for d in range(nsub)]

            def wide(a, d, fill):
                if not qs[d].start:
                    return a
                return jnp.concatenate([jnp.full((a.shape[0], qs[d].start), fill, a.dtype), a], axis=1)

            sts = [[_dot_nt(k_ref[pl.ds(pl.multiple_of(kb * bk, bk), bk), e * LANES:(e + 1) * LANES],
                            q_ref[qs[d], e * LANES:(e + 1) * LANES]) for d, kb in enumerate(kbs)] for e in range(4)]
            pts, alphas = [], []
            for e in range(4):
                st = ([jnp.where(key[:, qs[d]] + d * bk <= qry[:, qs[d]], sts[e][d], NEG) for d in range(nsub)]
                      if masked else sts[e])
                m_prev = m_ref[e, 0:1, :]
                m_new = m_prev
                for d in range(nsub):
                    m_new = jnp.maximum(m_new, wide(jnp.max(st[d], axis=0, keepdims=True), d, NEG))
                alpha = jnp.exp2(m_prev - m_new)
                pt = [jnp.exp2(st[d] - m_new[:, qs[d]]).astype(BF16) for d in range(nsub)]
                l_new = alpha * l_ref[e]
                for d in range(nsub):
                    l_new = l_new + wide(_dot(ones, pt[d]), d, 0.0)
                l_ref[e] = l_new
                m_ref[e] = jnp.broadcast_to(m_new, (8, bq))
                pts.append(pt)
                alphas.append(alpha)
            for e in range(4):
                acc = alphas[e] * acc_ref[e]
                for d in range(nsub):
                    v_t = vt_ref[e // 2, kbs[d], (e % 2) * HEAD:(e % 2 + 1) * HEAD, :]
                    acc = acc + wide(_dot(v_t, pts[e][d]), d, 0.0)
                acc_ref[e] = acc

        step(qi * nsub, True)

        def loop(t, c):
            step(t * nsub, False)
            return c

        lax.fori_loop(0, qi, loop, 0)
        outs, lses = [], []
        for e in range(4):
            l = l_ref[e, 0:1, :]
            outs.append(acc_ref[e] / l)
            lses.append(jnp.broadcast_to(m_ref[e, 0:1, :] * LN2 + jnp.log(l), (HEAD, bq)))
        o_ref[...] = jnp.transpose(jnp.concatenate(outs, axis=0))
        lse_ref[...] = jnp.transpose(jnp.concatenate(lses, axis=0))
        if n:
            pl.when(pl.program_id(0) == nq - 1)(finish)

    return pl.pallas_call(
        body, name="mla_fwd_fetch" if n else "mla_fwd", grid=(nq,),
        in_specs=[pl.BlockSpec((bq, 512), lambda i: (i, 0)),
                  pl.BlockSpec((T, 512), lambda i: (0, 0)),
                  pl.BlockSpec((2, nk, LANES, bk), lambda i: (0, 0, 0, 0))] + [HBM] * n,
        out_specs=[pl.BlockSpec((bq, 256), lambda i: (i, 0)),
                   pl.BlockSpec((bq, 256), lambda i: (i, 0))] + [HBM] * n,
        out_shape=[jax.ShapeDtypeStruct((T, 256), F32), jax.ShapeDtypeStruct((T, 256), F32)]
                  + [jax.ShapeDtypeStruct((4,) + s.shape, s.dtype) for s in fetch],
        scratch_shapes=[pltpu.VMEM((4, HEAD, bq), F32), pltpu.VMEM((4, 8, bq), F32), pltpu.VMEM((4, 8, bq), F32)]
                       + (_gather_sems(n) if n else []),
        compiler_params=_cparams(("arbitrary",)),
    )(qm, km, vt, *fetch)


def _mla_bwd(qm, km, vm, y, lse, dy):
    T = qm.shape[0]
    bq, bk = min(MLA_BQ, T), ATT_BLK
    nq, nsub, nk = T // bq, bq // bk, T // bk

    def body(q_ref, k_ref, v_ref, y_ref, lse_ref, dy_ref, dq_ref, dkt_ref, dvt_ref, dob_ref, st_ref, qt_ref, dot_ref):
        qi = pl.program_id(1)

        @pl.when(qi == 0)
        def _():
            dkt_ref[...] = jnp.zeros_like(dkt_ref)
            dvt_ref[...] = jnp.zeros_like(dvt_ref)

        lane = lax.broadcasted_iota(jnp.int32, (1, LANES), 1) // HEAD
        row, col = _causal_masks(bq, bk)
        dq_ref[...] = jnp.zeros_like(dq_ref)
        lse = lse_ref[...]
        lse_other = pltpu.roll(lse, HEAD, 1)
        qt_ref[...] = jnp.transpose(q_ref[...].astype(F32)).astype(BF16)
        dot_ref[...] = jnp.transpose(dy_ref[...]).astype(BF16)
        for e in range(2):
            do = jnp.where(lane == e, dy_ref[...], 0.0)
            dob_ref[e] = do.astype(BF16)
            st_ref[2 * e] = jnp.where(lane == e, lse, lse_other) * LOG2E
            st_ref[2 * e + 1] = jnp.broadcast_to(jnp.sum(do * y_ref[...], axis=1, keepdims=True), (bq, LANES))

        hss = [slice(e * LANES, (e + 1) * LANES) for e in range(2)]
        tile = lambda a: jnp.concatenate([a] * (bk // LANES), axis=1)

        def step(kb0, masked):
            kbs = [kb0 + d for d in range(nsub)]
            rows = [pl.ds(pl.multiple_of(kb * bk, bk), bk) for kb in kbs]
            pairs = [(d, e) for d in range(nsub) for e in range(2)]
            qs = [slice(d * bk if masked else 0, bq) for d in range(nsub)]
            ss = {(d, e): _dot_nt(q_ref[qs[d], hss[e]], k_ref[rows[d], hss[e]]) for d, e in pairs}
            dps = {(d, e): _dot_nt(dob_ref[e, qs[d], :], jnp.where(lane == e, v_ref[rows[d], :], 0))
                   for d, e in pairs}
            ps, dss = {}, {}
            for d, e in pairs:
                s = jnp.where(col[qs[d]] + d * bk <= row[qs[d]], ss[d, e], NEG) if masked else ss[d, e]
                p = jnp.exp2(s - tile(st_ref[2 * e, qs[d], :]))
                dss[d, e] = (p * (dps[d, e] - tile(st_ref[2 * e + 1, qs[d], :]))).astype(BF16)
                ps[d, e] = p.astype(BF16)
            for d, e in pairs:
                dvt_ref[0, kbs[d], e * HEAD:(e + 1) * HEAD, :] += _dot(
                    dot_ref[e * HEAD:(e + 1) * HEAD, qs[d]], ps[d, e])
            for d, e in pairs:
                dkt_ref[0, kbs[d], hss[e], :] += _dot(qt_ref[hss[e], qs[d]], dss[d, e])
            for e in range(2):
                if masked:
                    for d in range(nsub):
                        dq_ref[qs[d], hss[e]] += _dot(dss[d, e], k_ref[rows[d], hss[e]])
                else:
                    dq = dq_ref[:, hss[e]]
                    for d in range(nsub):
                        dq = dq + _dot(dss[d, e], k_ref[rows[d], hss[e]])
                    dq_ref[:, hss[e]] = dq

        step(qi * nsub, True)

        def loop(t, c):
            step(t * nsub, False)
            return c

        lax.fori_loop(0, qi, loop, 0)
        dq_ref[...] *= LN2

    return pl.pallas_call(
        body, name="mla_bwd", grid=(2, nq),
        in_specs=[pl.BlockSpec((bq, 256), lambda j, i: (i, j)),
                  pl.BlockSpec((T, 256), lambda j, i: (0, j)),
                  pl.BlockSpec((T, LANES), lambda j, i: (0, j)),
                  pl.BlockSpec((bq, LANES), lambda j, i: (i, j)),
                  pl.BlockSpec((bq, LANES), lambda j, i: (i, j)),
                  pl.BlockSpec((bq, LANES), lambda j, i: (i, j))],
        out_specs=[pl.BlockSpec((bq, 256), lambda j, i: (i, j)),
                   pl.BlockSpec((1, nk, 256, bk), lambda j, i: (j, 0, 0, 0)),
                   pl.BlockSpec((1, nk, LANES, bk), lambda j, i: (j, 0, 0, 0))],
        out_shape=[jax.ShapeDtypeStruct((T, 512), F32),
                   jax.ShapeDtypeStruct((2, nk, 256, bk), F32),
                   jax.ShapeDtypeStruct((2, nk, LANES, bk), F32)],
        scratch_shapes=[pltpu.VMEM((2, bq, LANES), BF16), pltpu.VMEM((4, bq, LANES), F32),
                        pltpu.VMEM((256, bq), BF16), pltpu.VMEM((LANES, bq), BF16)],
        compiler_params=_cparams(("parallel", "arbitrary")),
    )(qm, km, vm, y, lse, dy)


def _suffix_ones(n):
    r = lax.broadcasted_iota(jnp.int32, (n, n), 0)
    c = lax.broadcasted_iota(jnp.int32, (n, n), 1)
    return (r >= c).astype(BF16)


def _prefix_ones(n):
    r = lax.broadcasted_iota(jnp.int32, (n, n), 0)
    c = lax.broadcasted_iota(jnp.int32, (n, n), 1)
    return (r <= c).astype(BF16)


def _sb_specs(T, bq):
    qo, ko, vo = (_INT_OFF[n] // 256 for n in ("d_q", "d_k", "d_v"))
    return [pl.BlockSpec((bq, 256), lambda i: (i, qo)),
            pl.BlockSpec((T, 256), lambda i: (0, ko)),
            pl.BlockSpec((T, 256), lambda i: (0, vo))]


def _sb_fwd(hb):
    T = hb.shape[0]
    bq = bk = ATT_BLK
    nq = T // bq

    def body(q_ref, k_ref, v_ref, o_ref, tot_ref, cnt_ref, qm_ref, car_ref):
        qi = pl.program_id(0)
        lane = lax.broadcasted_iota(jnp.int32, (1, LANES), 1) // HEAD
        row, col = _causal_masks(bq, bk)
        strict = col < row
        u = _suffix_ones(bk)
        o_ref[...] = jnp.zeros_like(o_ref)
        car_ref[...] = jnp.zeros_like(car_ref)
        pair = lambda h: slice((h // 2) * LANES, (h // 2 + 1) * LANES)
        for h in range(4):
            qm_ref[h] = jnp.where(lane == h % 2, q_ref[:, pair(h)], 0) * 0.125

        def step(blocks):
            tile = lambda a: jnp.concatenate([a] * (bk // LANES), axis=1)
            rows = [pl.ds(pl.multiple_of(kb * bk, bk), bk) for kb, _ in blocks]
            pairs = [(b, h) for b in range(len(blocks)) for h in range(4)]
            zs = {(b, h): _dot_nt(qm_ref[h], k_ref[rows[b], pair(h)]) for b, h in pairs}
            splits = {}
            for b, h in pairs:
                z = zs[b, h]
                lk = jnp.minimum(-z, 0.0) - jnp.log(1.0 + jnp.exp(-jnp.abs(z)))
                if blocks[b][1] is not None:
                    lk = jnp.where(blocks[b][1], lk, 0.0)
                splits[b, h] = _split(lk)
            sufs = {bh: _dot(hi, u) + _dot(lo, u) for bh, (hi, lo) in splits.items()}
            car = [car_ref[h] for h in range(4)]
            aas = {}
            for b, h in pairs:
                a = jnp.exp(zs[b, h] + sufs[b, h] + tile(car[h]))
                if blocks[b][1] is not None:
                    a = jnp.where(blocks[b][1], a, 0.0)
                aas[b, h] = a.astype(BF16)
                car[h] = car[h] + jnp.broadcast_to(sufs[b, h][:, 0:1], (bq, LANES))
            acc = [o_ref[:, pair(0)], o_ref[:, pair(2)]]
            for b, h in pairs:
                acc[h // 2] = acc[h // 2] + _dot(aas[b, h], jnp.where(lane == h % 2, v_ref[rows[b], pair(h)], 0))
            o_ref[:, pair(0)], o_ref[:, pair(2)] = acc
            for h in range(4):
                car_ref[h] = car[h]

        step([(qi, strict), (jnp.maximum(qi - 1, 0), qi > 0)])

        def live():
            worst = jnp.maximum(jnp.maximum(car_ref[0], car_ref[1]), jnp.maximum(car_ref[2], car_ref[3]))
            return jnp.max(worst) >= SB_DEAD

        def cond(c):
            return jnp.logical_and(c[0] < qi, c[1])

        def loop(c):
            step([(qi - 1 - c[0], None)])
            return c[0] + 1, live()

        done, _ = lax.while_loop(cond, loop, (jnp.minimum(qi, 1), live()))
        tot_ref[:, pair(0)] = jnp.where(lane == 0, car_ref[0], car_ref[1])
        tot_ref[:, pair(2)] = jnp.where(lane == 0, car_ref[2], car_ref[3])
        cnt_ref[0, qi] = done.astype(F32)

    return pl.pallas_call(
        body, name="sb_fwd", grid=(nq,), in_specs=_sb_specs(T, bq),
        out_specs=[pl.BlockSpec((bq, 256), lambda i: (i, 0)), pl.BlockSpec((bq, 256), lambda i: (i, 0)),
                   pl.BlockSpec(memory_space=pltpu.SMEM)],
        out_shape=[jax.ShapeDtypeStruct((T, 256), F32), jax.ShapeDtypeStruct((T, 256), F32),
                   jax.ShapeDtypeStruct((1, nq), F32)],
        scratch_shapes=[pltpu.VMEM((4, bq, LANES), BF16), pltpu.VMEM((4, bq, LANES), F32)],
        compiler_params=_cparams(("arbitrary",)),
    )(hb, hb, hb)


def _sb_bwd(hb, tot, cnt, dy):
    T = hb.shape[0]
    bq = bk = ATT_BLK
    nq = T // bq

    def body(q_ref, k_ref, v_ref, tot_ref, dy_ref, cnt_ref, dq_ref, dk_ref, dv_ref, qm_ref, dob_ref, dqa_ref, rem_ref,
             cg_ref):
        qi = pl.program_id(0)

        @pl.when(qi == 0)
        def _():
            dk_ref[...] = jnp.zeros_like(dk_ref)
            dv_ref[...] = jnp.zeros_like(dv_ref)

        lane = lax.broadcasted_iota(jnp.int32, (1, LANES), 1) // HEAD
        row, col = _causal_masks(bq, bk)
        strict = col < row
        u = _prefix_ones(bk)
        pair = lambda h: slice((h // 2) * LANES, (h // 2 + 1) * LANES)
        dqa_ref[...] = jnp.zeros_like(dqa_ref)
        cg_ref[...] = jnp.zeros_like(cg_ref)
        for h in range(4):
            tot = tot_ref[:, pair(h)]
            qm_ref[h] = jnp.where(lane == h % 2, q_ref[:, pair(h)], 0) * 0.125
            dob_ref[h] = jnp.where(lane == h % 2, dy_ref[:, pair(h)], 0.0).astype(BF16)
            rem_ref[h] = jnp.where(lane == h % 2, tot, pltpu.roll(tot, HEAD, 1))

        def step(blocks):
            tile = lambda a: jnp.concatenate([a] * (bk // LANES), axis=1)
            nb = len(blocks)
            rows = [pl.ds(pl.multiple_of(kb * bk, bk), bk) for kb, _ in blocks]
            pairs = [(b, h) for b in range(nb) for h in range(4)]
            mask = lambda b, x: x if blocks[b][1] is None else jnp.where(blocks[b][1], x, 0.0)
            zs = {(b, h): _dot_nt(qm_ref[h], k_ref[rows[b], pair(h)]) for b, h in pairs}
            das = {(b, h): _dot_nt(dob_ref[h], jnp.where(lane == h % 2, v_ref[rows[b], pair(h)], 0)) for b, h in pairs}
            zls, splits = {}, {}
            for b, h in pairs:
                z = zs[b, h]
                lk = mask(b, jnp.minimum(-z, 0.0) - jnp.log(1.0 + jnp.exp(-jnp.abs(z))))
                zls[b, h] = z + lk
                splits[b, h] = _split(lk)
            pres = {bh: _dot(hi, u) + _dot(lo, u) for bh, (hi, lo) in splits.items()}
            rem = [rem_ref[h] for h in range(4)]
            aas, gs, gsplits = {}, {}, {}
            for b, h in pairs:
                a = mask(b, jnp.exp(zls[b, h] + (tile(rem[h]) - pres[b, h])))
                gs[b, h] = a * das[b, h]
                aas[b, h] = a.astype(BF16)
                gsplits[b, h] = _split(gs[b, h])
                rem[h] = rem[h] - jnp.broadcast_to(pres[b, h][:, bk - 1:bk], (bq, LANES))
            for b in range(nb):
                for p in (0, 2):
                    dv_ref[rows[b], pair(p)] += _dot_tn(aas[b, p], dob_ref[p]) + _dot_tn(aas[b, p + 1], dob_ref[p + 1])
            gpres = {bh: _dot(hi, u) + _dot(lo, u) for bh, (hi, lo) in gsplits.items()}
            cg = [cg_ref[h] for h in range(4)]
            dzs = {}
            for b, h in pairs:
                dz = mask(b, gs[b, h] - jnp.exp(zls[b, h]) * (tile(cg[h]) + gpres[b, h]))
                dzs[b, h] = dz.astype(BF16)
                cg[h] = cg[h] + jnp.broadcast_to(gpres[b, h][:, bk - 1:bk], (bq, LANES))
            for b in range(nb):
                for p in (0, 2):
                    dk_ref[rows[b], pair(p)] += _dot_tn(dzs[b, p], qm_ref[p]) + _dot_tn(dzs[b, p + 1], qm_ref[p + 1])
            for h in range(4):
                dq = dqa_ref[h]
                for b in range(nb):
                    dq = dq + _dot(dzs[b, h], k_ref[rows[b], pair(h)])
                dqa_ref[h] = dq
                rem_ref[h] = rem[h]
                cg_ref[h] = cg[h]

        def loop(kb, c):
            step([(kb, None)])
            return c

        start = qi - jnp.clip(cnt_ref[0, qi].astype(jnp.int32), 0, qi)
        lax.fori_loop(start, qi - 1, loop, 0)
        step([(jnp.maximum(qi - 1, 0), qi > 0), (qi, strict)])
        for p in (0, 2):
            dq_ref[:, pair(p)] = (jnp.where(lane == 0, dqa_ref[p], dqa_ref[p + 1]) * 0.125).astype(BF16)

    return pl.pallas_call(
        body, name="sb_bwd", grid=(nq,),
        in_specs=_sb_specs(T, bq) + [pl.BlockSpec((bq, 256), lambda i: (i, 0)),
                                     pl.BlockSpec((bq, 256), lambda i: (i, 0)),
                                     pl.BlockSpec(memory_space=pltpu.SMEM)],
        out_specs=[pl.BlockSpec((bq, 256), lambda i: (i, 0)),
                   pl.BlockSpec((T, 256), lambda i: (0, 0)),
                   pl.BlockSpec((T, 256), lambda i: (0, 0))],
        out_shape=[jax.ShapeDtypeStruct((T, 256), BF16)] + [jax.ShapeDtypeStruct((T, 256), F32)] * 2,
        scratch_shapes=[pltpu.VMEM((4, bq, LANES), BF16), pltpu.VMEM((4, bq, LANES), BF16),
                        pltpu.VMEM((4, bq, LANES), F32), pltpu.VMEM((4, bq, LANES), F32),
                        pltpu.VMEM((4, bq, LANES), F32)],
        compiler_params=_cparams(("arbitrary",)),
    )(hb, hb, hb, tot, dy, cnt)


EP_TM = 512


def _ep_in_specs(tm, rev):
    idx = (lambda i: rev - i) if rev is not None else (lambda i: i)
    bo = (_INT_OFF["b_b"] - N_HB) // 256
    halo = lambda i: jnp.maximum(idx(i) * (tm // 8) - 1, 0)
    return [pl.BlockSpec((tm, 256), lambda i: (idx(i), 0)),
            pl.BlockSpec((tm, 256), lambda i: (idx(i), 0)),
            pl.BlockSpec((tm, 256), lambda i: (idx(i), 0)),
            pl.BlockSpec((tm, D_MODEL), lambda i: (idx(i), 0)),
            pl.BlockSpec((tm, 256), lambda i: (idx(i), bo)),
            pl.BlockSpec((tm, 256), lambda i: (idx(i), bo + 1)),
            pl.BlockSpec((tm, 256), lambda i: (idx(i), bo + 2)),
            pl.BlockSpec((8, 256), lambda i: (halo(i), bo + 1)),
            pl.BlockSpec((8, 256), lambda i: (halo(i), bo + 2)),
            pl.BlockSpec((3, 256), lambda i: (0, 0)),
            pl.BlockSpec((1, 256), lambda i: (0, 0)),
            pl.BlockSpec((1, D_MODEL), lambda i: (0, 0)),
            pl.BlockSpec((D_MODEL, D_MODEL), lambda i: (0, 0)),
            pl.BlockSpec((1, D_MODEL), lambda i: (0, 0))]


def _ep_mix(first, ya_ref, yc_ref, yd_ref, gate_ref, bb_ref, bc_ref, bx_ref, hc_ref, hx_ref, cw_ref, cb_ref, gg_ref):
    tm = ya_ref.shape[0]
    u = bc_ref[...] * bx_ref[...]
    halo = jnp.where(first, 0.0, hc_ref[...] * hx_ref[...])
    row = lax.broadcasted_iota(jnp.int32, (tm, 1), 0)
    u1 = jnp.where(row == 0, halo[7:8, :], pltpu.roll(u, 1, 0))
    u2 = jnp.where(row == 0, halo[6:7, :], jnp.where(row == 1, halo[7:8, :], pltpu.roll(u, 2, 0)))
    cw = cw_ref[...]
    conv = cw[0:1, :] * u2 + cw[1:2, :] * u1 + cw[2:3, :] * u + cb_ref[...]
    bb = bb_ref[...]
    ys = [ya_ref[...], bb * conv, yc_ref[...], yd_ref[...]]
    rs = [_rms(y) for y in ys]
    gg = gg_ref[...]
    yhat = jnp.concatenate([y * r for y, r in zip(ys, rs)], axis=1)
    gate = gate_ref[...]
    sig = 1.0 / (1.0 + jnp.exp(-gate))
    return u, u1, u2, conv, bb, rs, yhat, yhat * gg, gate, sig


def _epilogue_fwd(x, ya, yc, yd, hf, conv_w, conv_b, g_grp, w_out, g_post, tgt=None):
    T = x.shape[0]
    tm = EP_TM
    row_spec = pl.BlockSpec((tm, D_MODEL), lambda i: (i, 0))

    def layer_out(refs):
        (x_ref, ya_ref, yc_ref, yd_ref, gate_ref, bb_ref, bc_ref, bx_ref, hc_ref, hx_ref, cw_ref, cb_ref,
         gg_ref, wo_ref, gp_ref) = refs
        (_, _, _, _, _, _, _, yn, gate, sig) = _ep_mix(
            pl.program_id(0) == 0, ya_ref, yc_ref, yd_ref, gate_ref, bb_ref, bc_ref, bx_ref, hc_ref, hx_ref,
            cw_ref, cb_ref, gg_ref)
        z = _dot((yn * (gate * sig)).astype(BF16), wo_ref[...])
        return x_ref[...] + z * _rms(z) * gp_ref[...]

    args = (x, ya, yc, yd, hf, hf, hf, hf, hf, hf, conv_w, conv_b, g_grp, w_out, g_post)
    in_specs = [row_spec] + _ep_in_specs(tm, None)
    if tgt is None:
        def body(*refs):
            refs[-1][...] = layer_out(refs[:-1])

        return pl.pallas_call(
            body, name="epilogue_fwd", grid=(T // tm,), in_specs=in_specs, out_specs=row_spec,
            out_shape=jax.ShapeDtypeStruct((T, D_MODEL), F32), compiler_params=_cparams(("parallel",)),
        )(*args)

    def body_loss(*refs):
        t_ref, dy_ref, l_ref = refs[-3:]

        @pl.when(pl.program_id(0) == 0)
        def _():
            l_ref[...] = jnp.zeros_like(l_ref)

        d = layer_out(refs[:-3]) - t_ref[...]
        dy_ref[...] = d * (1.0 / D_MODEL)
        part = jnp.sum(jnp.sum(d * d, axis=1, keepdims=True), axis=0, keepdims=True)
        l_ref[...] += part * (0.5 / D_MODEL)

    return pl.pallas_call(
        body_loss, name="epilogue_fwd_loss", grid=(T // tm,), in_specs=in_specs + [row_spec],
        out_specs=[row_spec, pl.BlockSpec((8, LANES), lambda i: (0, 0))],
        out_shape=[jax.ShapeDtypeStruct((T, D_MODEL), F32), jax.ShapeDtypeStruct((8, LANES), F32)],
        compiler_params=_cparams(("arbitrary",)),
    )(*args, tgt)


def _epilogue_bwd(dxn, ya, yc, yd, hf, conv_w, conv_b, g_grp, w_out, g_post):
    T = dxn.shape[0]
    tm = EP_TM
    nt = T // tm
    ridx = lambda i: (nt - 1 - i, 0)

    def body(dx_ref, ya_ref, yc_ref, yd_ref, gate_ref, bb_ref, bc_ref, bx_ref, hc_ref, hx_ref, cw_ref, cb_ref,
             gg_ref, wo_ref, gp_ref,
             dya_ref, dyc_ref, dyd_ref, dhf_ref, dwo_ref, dgp_ref, dgg_ref, dcw_ref, dcb_ref, carry_ref):
        i = pl.program_id(0)

        @pl.when(i == 0)
        def _():
            for r in (dwo_ref, dgp_ref, dgg_ref, dcw_ref, dcb_ref, carry_ref):
                r[...] = jnp.zeros_like(r)

        (u, u1, u2, conv, bb, rs, yhat, yn, gate, sig) = _ep_mix(
            i == nt - 1, ya_ref, yc_ref, yd_ref, gate_ref, bb_ref, bc_ref, bx_ref, hc_ref, hx_ref,
            cw_ref, cb_ref, gg_ref)
        silu = gate * sig
        ymix = (yn * silu).astype(BF16)
        z = _dot(ymix, wo_ref[...])
        rz = _rms(z)
        dz, dgrow = _rms_bwd(dx_ref[...], z * rz, rz, gp_ref[...])
        dgp_ref[...] += _colsum(dgrow)
        dzb = dz.astype(BF16)
        dwo_ref[...] += _dot_tn(ymix, dzb)
        dymix = _dot_nt(dzb, wo_ref[...])
        dhf_ref[:, 0:D_MODEL] = (dymix * yn * (sig * (1.0 + gate * (1.0 - sig)))).astype(BF16)
        dyn = dymix * silu
        dgg_ref[...] += _colsum(dyn * yhat)
        gg = gg_ref[...]
        dys = []
        for gi in range(4):
            sl = slice(gi * GROUP, (gi + 1) * GROUP)
            dyh = dyn[:, sl] * gg[:, sl]
            yh = yhat[:, sl]
            dys.append(rs[gi] * (dyh - yh * jnp.mean(dyh * yh, axis=-1, keepdims=True)))
        dya_ref[...] = dys[0]
        dyc_ref[...] = dys[2]
        dyd_ref[...] = dys[3]
        dyb = dys[1]
        dhf_ref[:, D_MODEL:D_MODEL + 256] = (dyb * conv).astype(BF16)
        dconv = dyb * bb
        dcb_ref[...] += _colsum(dconv)
        dcw_ref[0:1, :] += _colsum(dconv * u2)
        dcw_ref[1:2, :] += _colsum(dconv * u1)
        dcw_ref[2:3, :] += _colsum(dconv * u)
        carry = carry_ref[...]
        row = lax.broadcasted_iota(jnp.int32, (tm, 1), 0)
        d1 = jnp.where(row == tm - 1, carry[0:1, :], pltpu.roll(dconv, tm - 1, 0))
        d2 = jnp.where(row == tm - 2, carry[0:1, :],
                       jnp.where(row == tm - 1, carry[1:2, :], pltpu.roll(dconv, tm - 2, 0)))
        cw = cw_ref[...]
        du = cw[2:3, :] * dconv + cw[1:2, :] * d1 + cw[0:1, :] * d2
        dhf_ref[:, D_MODEL + 256:D_MODEL + 512] = (du * bx_ref[...]).astype(BF16)
        dhf_ref[:, D_MODEL + 512:D_MODEL + 768] = (du * bc_ref[...]).astype(BF16)
        carry_ref[...] = dconv[0:8, :]

    in_specs = [pl.BlockSpec((tm, D_MODEL), ridx)] + _ep_in_specs(tm, nt - 1)
    return pl.pallas_call(
        body, name="epilogue_bwd", grid=(nt,), in_specs=in_specs,
        out_specs=[pl.BlockSpec((tm, 256), ridx), pl.BlockSpec((tm, 256), ridx), pl.BlockSpec((tm, 256), ridx),
                   pl.BlockSpec((tm, D_MODEL + 768), ridx),
                   pl.BlockSpec((D_MODEL, D_MODEL), lambda i: (0, 0)),
                   pl.BlockSpec((1, D_MODEL), lambda i: (0, 0)),
                   pl.BlockSpec((1, D_MODEL), lambda i: (0, 0)),
                   pl.BlockSpec((8, 256), lambda i: (0, 0)),
                   pl.BlockSpec((1, 256), lambda i: (0, 0))],
        out_shape=[jax.ShapeDtypeStruct((T, 256), F32)] * 3
                  + [jax.ShapeDtypeStruct((T, D_MODEL + 768), BF16),
                     jax.ShapeDtypeStruct((D_MODEL, D_MODEL), F32),
                     jax.ShapeDtypeStruct((1, D_MODEL), F32),
                     jax.ShapeDtypeStruct((1, D_MODEL), F32),
                     jax.ShapeDtypeStruct((8, 256), F32),
                     jax.ShapeDtypeStruct((1, 256), F32)],
        scratch_shapes=[pltpu.VMEM((8, 256), F32)],
        compiler_params=_cparams(("arbitrary",)),
    )(dxn, ya, yc, yd, hf, hf, hf, hf, hf, hf, conv_w, conv_b, g_grp, w_out, g_post)


def _place():
    return lax.axis_index("x"), lax.axis_index("y"), lax.axis_index("c")


def _other_chips(x, y):
    return [(1 - x, y), (x, 1 - y), (1 - x, 1 - y)]


HBM = pl.BlockSpec(memory_space=pl.ANY)


def _gather_plan(ins, outs, sems):
    n = len(ins)
    ici_send, ici_recv, d2d_send, d2d_recv, local_sems = sems
    x, y, c = _place()
    me = 2 * x + y
    chips = _other_chips(x, y)

    def ici(a, j, chip_from):
        px, py = chips[j]
        return pltpu.make_async_remote_copy(
            src_ref=ins[a].at[c], dst_ref=outs[a].at[chip_from, c], send_sem=ici_send.at[3 * a + j],
            recv_sem=ici_recv.at[3 * a + j], device_id=(px, py, c), device_id_type=MESH)

    def d2d(a, j, part):
        px, py = chips[j]
        blk = outs[a].at[2 * px + py, part]
        return pltpu.make_async_remote_copy(
            src_ref=blk, dst_ref=blk, send_sem=d2d_send.at[3 * a + j], recv_sem=d2d_recv.at[3 * a + j],
            device_id=(x, y, 1 - c), device_id_type=MESH)

    def local(a):
        return pltpu.make_async_copy(ins[a], outs[a].at[me], local_sems.at[a])

    hops = [(j, a) for j in range(3) for a in range(n)]

    def start():
        for a in range(n):
            local(a).start()
        for j, a in hops:
            ici(a, j, me).start()

    def finish():
        for j, a in hops:
            ici(a, j, 2 * chips[j][0] + chips[j][1]).wait_recv()
            d2d(a, j, c).start()
        for j, a in hops:
            d2d(a, j, 1 - c).wait_recv()
        for j, a in hops:
            ici(a, j, me).wait_send()
            d2d(a, j, c).wait_send()
        for a in range(n):
            local(a).wait()

    return start, finish


def _gather_sems(n):
    return [pltpu.SemaphoreType.DMA((3 * n,))] * 4 + [pltpu.SemaphoreType.DMA((n,))]


def _exchange_chips(parts, small):
    n = len(parts)

    def body(*refs):
        ins, sm_ref = refs[:n], refs[n]
        outs, osm_ref = refs[n + 1:2 * n + 1], refs[2 * n + 1]
        send_sems, recv_sems, ssend_sems, srecv_sems, local_sems = refs[2 * n + 2:]
        x, y, c = _place()
        me = 2 * x + y
        dev = 4 * x + 2 * y + c
        local = [pltpu.make_async_copy(ins[a].at[me], outs[a].at[me], local_sems.at[a]) for a in range(n)]
        local.append(pltpu.make_async_copy(sm_ref, osm_ref.at[dev], local_sems.at[n]))
        for cp in local:
            cp.start()
        sends = []
        for j, (px, py) in enumerate(_other_chips(x, y)):
            for a in range(n):
                cp = pltpu.make_async_remote_copy(
                    src_ref=ins[a].at[2 * px + py], dst_ref=outs[a].at[me], send_sem=send_sems.at[3 * a + j],
                    recv_sem=recv_sems.at[3 * a + j], device_id=(px, py, c), device_id_type=MESH)
                cp.start()
                sends.append(cp)
        flips = [(fx, fy, fc) for fx in (0, 1) for fy in (0, 1) for fc in (0, 1)][1:]
        for j, (fx, fy, fc) in enumerate(flips):
            cp = pltpu.make_async_remote_copy(
                src_ref=sm_ref, dst_ref=osm_ref.at[dev], send_sem=ssend_sems.at[j], recv_sem=srecv_sems.at[j],
                device_id=(x ^ fx, y ^ fy, c ^ fc), device_id_type=MESH)
            cp.start()
            sends.append(cp)
        for j, (px, py) in enumerate(_other_chips(x, y)):
            for a in range(n):
                pltpu.make_async_remote_copy(
                    src_ref=ins[a].at[me], dst_ref=outs[a].at[2 * px + py], send_sem=send_sems.at[3 * a + j],
                    recv_sem=recv_sems.at[3 * a + j], device_id=(px, py, c), device_id_type=MESH).wait_recv()
        for j, (fx, fy, fc) in enumerate(flips):
            src = 4 * (x ^ fx) + 2 * (y ^ fy) + (c ^ fc)
            pltpu.make_async_remote_copy(
                src_ref=sm_ref, dst_ref=osm_ref.at[src], send_sem=ssend_sems.at[j], recv_sem=srecv_sems.at[j],
                device_id=(x ^ fx, y ^ fy, c ^ fc), device_id_type=MESH).wait_recv()
        for cp in sends:
            cp.wait_send()
        for cp in local:
            cp.wait()

    return pl.pallas_call(
        body, name="exchange_chips",
        in_specs=[HBM] * (n + 1), out_specs=[HBM] * (n + 1),
        out_shape=[jax.ShapeDtypeStruct(p.shape, p.dtype) for p in parts]
                  + [jax.ShapeDtypeStruct((8,) + small.shape, small.dtype)],
        scratch_shapes=[pltpu.SemaphoreType.DMA((3 * n,)), pltpu.SemaphoreType.DMA((3 * n,)),
                        pltpu.SemaphoreType.DMA((7,)), pltpu.SemaphoreType.DMA((7,)),
                        pltpu.SemaphoreType.DMA((n + 1,))],
    )(*parts, small)


def _swap_cores(parts, name):
    n = len(parts)

    def body(*refs):
        ins, outs, send_sems, recv_sems = refs[:n], refs[n:2 * n], refs[2 * n], refs[2 * n + 1]
        x, y, c = _place()
        copies = [pltpu.make_async_remote_copy(
            src_ref=ins[a], dst_ref=outs[a], send_sem=send_sems.at[a], recv_sem=recv_sems.at[a],
            device_id=(x, y, 1 - c), device_id_type=MESH) for a in range(n)]
        for cp in copies:
            cp.start()
        for cp in copies:
            cp.wait()

    return pl.pallas_call(
        body, name=name, in_specs=[HBM] * n, out_specs=[HBM] * n,
        out_shape=[jax.ShapeDtypeStruct(p.shape, p.dtype) for p in parts],
        scratch_shapes=[pltpu.SemaphoreType.DMA((n,)), pltpu.SemaphoreType.DMA((n,))],
    )(*parts)


def _tile(rows, cols):
    for cand in (256, 128, 64):
        if rows % cand == 0:
            return cand, cols
    if rows > 64 and cols % 256 == 0:
        return rows, 256
    return rows, cols


def _swap_layer_chunks(per_layer):
    n = len(per_layer[0])

    def body(*refs):
        ins, outs, send_sems, recv_sems = (refs[:n], refs[n:2 * n]), refs[2 * n:3 * n], refs[3 * n], refs[3 * n + 1]
        x, y, c = _place()
        for l in range(2):
            @pl.when(c == 1 - l)
            def _():
                copies = [pltpu.make_async_remote_copy(
                    src_ref=ins[l][a], dst_ref=outs[a], send_sem=send_sems.at[a], recv_sem=recv_sems.at[a],
                    device_id=(x, y, 1 - c), device_id_type=MESH) for a in range(n)]
                for cp in copies:
                    cp.start()
                for cp in copies:
                    cp.wait()

    return pl.pallas_call(
        body, name="swap_layer_chunks", in_specs=[HBM] * (2 * n), out_specs=[HBM] * n,
        out_shape=[jax.ShapeDtypeStruct(p.shape, p.dtype) for p in per_layer[0]],
        scratch_shapes=[pltpu.SemaphoreType.DMA((n,)), pltpu.SemaphoreType.DMA((n,))],
    )(*per_layer[0], *per_layer[1])


def _add(a0, a1, b, name):
    L, R, C = b.shape
    tr, tc = _tile(R, C)

    def body(a0_ref, a1_ref, b_ref, o_ref):
        mine = jnp.where(lax.axis_index("c") == 0, a0_ref[...], a1_ref[...])
        o_ref[...] = (mine + b_ref[...]).astype(BF16)

    spec = pl.BlockSpec((1, tr, tc), lambda l, i, j: (l, i, j))
    return pl.pallas_call(
        body, name=name, grid=(L, R // tr, C // tc), in_specs=[spec] * 3, out_specs=spec,
        out_shape=jax.ShapeDtypeStruct((L, R, C), BF16),
        compiler_params=_cparams(("parallel", "parallel", "parallel")),
    )(a0, a1, b)


def _sum_leading(buf, name):
    n, R, C = buf.shape
    tr, tc = _tile(R, C)

    def body(b_ref, o_ref):
        acc = b_ref[0].astype(F32)
        for k in range(1, n):
            acc = acc + b_ref[k].astype(F32)
        o_ref[...] = acc

    return pl.pallas_call(
        body, name=name, grid=(R // tr, C // tc),
        in_specs=[pl.BlockSpec((n, tr, tc), lambda i, j: (0, i, j))],
        out_specs=pl.BlockSpec((tr, tc), lambda i, j: (i, j)),
        out_shape=jax.ShapeDtypeStruct((R, C), F32),
        compiler_params=_cparams(("parallel", "parallel")),
    )(buf)


def _adam_update(w, g, m, v):
    c1 = 1.0 / (1.0 - ADAM_B1 ** ADAM_STEP)
    c2 = 1.0 / (1.0 - ADAM_B2 ** ADAM_STEP)
    mn = ADAM_B1 * m + (1.0 - ADAM_B1) * g
    vn = ADAM_B2 * v + (1.0 - ADAM_B2) * (g * g)
    return -ADAM_LR * ((mn * c1) / (jnp.sqrt(vn * c2) + ADAM_EPS) + ADAM_WD * w), mn, vn


def _adamw_layers(w, m, v, g_mine, g_other, name):
    _, R, C = w.shape
    tr, tc = _tile(R, C)

    def body(w_ref, m_ref, v_ref, gm_ref, go_ref, g_ref, d_ref, mo_ref, vo_ref):
        g = jnp.where(pl.program_id(0) == lax.axis_index("c"), gm_ref[...], go_ref[...])
        g_ref[0] = g
        d_ref[0], mo_ref[0], vo_ref[0] = _adam_update(w_ref[0], g, m_ref[0], v_ref[0])

    spec3 = pl.BlockSpec((1, tr, tc), lambda l, i, j: (l, i, j))
    spec2 = pl.BlockSpec((tr, tc), lambda l, i, j: (i, j))
    return pl.pallas_call(
        body, name=name, grid=(2, R // tr, C // tc),
        in_specs=[spec3] * 3 + [spec2] * 2, out_specs=[spec3] * 4,
        out_shape=[jax.ShapeDtypeStruct(w.shape, F32)] * 4,
        compiler_params=_cparams(("parallel", "parallel", "parallel")),
    )(w, m, v, g_mine, g_other)


PACK_C = 1024
_BIG = ("w_in", "w_out", "mla_w_uq", "mla_w_ukv", "conv_w")
_SMALL = ("norm_pre", "group_norm", "norm_post", "conv_b", "mla_q_norm", "mla_kv_norm", "attn_sinks")
_SMALL_W = {"norm_pre": 1024, "group_norm": 1024, "norm_post": 1024, "conv_b": 256, "mla_q_norm": 256,
            "mla_kv_norm": 128, "attn_sinks": 4}


_LOSS_AT = divmod(DEPTH * sum(_SMALL_W.values()), PACK_C)


def _pack_small(d, loss):
    flat = jnp.concatenate([d[n].reshape(-1) for n in _SMALL] + [loss.reshape(1)])
    return jnp.pad(flat, (0, 8 * PACK_C - flat.shape[0])).reshape(8, PACK_C)


def _adamw_small(w, m, v, got):
    ns = len(_SMALL)

    def body(*refs):
        got_ref = refs[3 * ns]
        outs = refs[3 * ns + 1:]
        gsum = got_ref[0]
        for d in range(1, 8):
            gsum = gsum + got_ref[d]
        outs[4 * ns][...] = gsum[_LOSS_AT[0]:_LOSS_AT[0] + 1, _LOSS_AT[1]:_LOSS_AT[1] + 1]
        off = 0
        for i, name in enumerate(_SMALL):
            wd = _SMALL_W[name]
            rows = []
            for l in range(DEPTH):
                r, c0 = divmod(off + l * wd, PACK_C)
                rows.append(gsum[r:r + 1, c0:c0 + wd])
            off += DEPTH * wd
            g = jnp.concatenate(rows, axis=0)
            delta, mn, vn = _adam_update(refs[i][...], g, refs[ns + i][...], refs[2 * ns + i][...])
            outs[i][...] = g
            outs[ns + i][...] = delta
            outs[2 * ns + i][...] = mn
            outs[3 * ns + i][...] = vn

    shapes = [jax.ShapeDtypeStruct(w[n].shape, F32) for n in _SMALL]
    res = pl.pallas_call(body, name="adamw_small", out_shape=shapes * 4 + [jax.ShapeDtypeStruct((1, 1), F32)])(
        *[w[n] for n in _SMALL], *[m[n] for n in _SMALL], *[v[n] for n in _SMALL], got)
    return [dict(zip(_SMALL, res[k * ns:(k + 1) * ns])) for k in range(4)], res[4 * ns]


def _w_in_internal(slabs):
    S, R, D = slabs.shape
    assert S * R == D_IN

    def body(w_ref, o_ref):
        for n, wd in _REAL:
            o, oi = _REAL_OFF[n][0], _INT_OFF[n]
            r = o
            while r < o + wd:
                end = min(o + wd, (r // R + 1) * R)
                o_ref[oi + r - o:oi + end - o, :] = w_ref[r // R, r % R:r % R + end - r, :]
                r = end
            if _INT_W[n] != wd:
                o_ref[oi + wd:oi + _INT_W[n], :] = jnp.zeros((_INT_W[n] - wd, D), slabs.dtype)

    return pl.pallas_call(
        body, name="w_in_internal", out_shape=jax.ShapeDtypeStruct((N_INT, D), slabs.dtype),
        compiler_params=pltpu.CompilerParams(vmem_limit_bytes=VMEM_LIMIT),
    )(slabs)


def _uq_internal(w):
    return jnp.pad(w.reshape(256, 4, 96), ((0, 0), (0, 0), (0, 32))).reshape(256, 512)


def _uq_real(dw):
    return dw.reshape(256, 4, 128)[:, :, :96].reshape(256, 384)


def _ukv_internal(w):
    w4 = w.reshape(128, 4, 128)
    k = jnp.pad(w4[:, :, :64], ((0, 0), (0, 0), (0, 64))).reshape(128, 512)
    return jnp.concatenate([k, w4[:, :, 64:].reshape(128, 256)], axis=1)


def _ukv_real(dw):
    k = dw[:, :512].reshape(128, 4, 128)[:, :, :64]
    v = dw[:, 512:].reshape(128, 4, 64)
    return jnp.concatenate([k, v], axis=2).reshape(128, 512)


def _layer_fwd(x, rope, p, tgt=None, fetch=()):
    xn, hb, hf = _inproj_fwd(x, p["norm_pre"], p["w_in"])
    ya = _swa_fwd(hb, p["attn_sinks"])
    qm, km, vm, vt = _mla_prep_fwd(hf, rope, p["mla_q_norm"], p["mla_kv_norm"], p["mla_w_uq"], p["mla_w_ukv"])
    yc, lse, *fetched = _mla_fwd(qm, km, vt, fetch)
    yd, tot, cnt = _sb_fwd(hb)
    x_next = _epilogue_fwd(x, ya, yc, yd, hf, p["conv_w"], p["conv_b"], p["group_norm"], p["w_out"], p["norm_post"],
                           tgt)
    saved = dict(x=x, xn=xn, hb=hb, hf=hf, ya=ya, yc=yc, yd=yd, tot=tot, cnt=cnt, qm=qm, km=km, vm=vm, lse=lse)
    return x_next, saved, fetched


def _layer_bwd(dx_next, rope, p, s):
    (dya, dyc, dyd, dhf, dw_out, dg_post, dg_grp, dconv_w, dconv_b) = _epilogue_bwd(
        dx_next, s["ya"], s["yc"], s["yd"], s["hf"], p["conv_w"], p["conv_b"], p["group_norm"], p["w_out"],
        p["norm_post"])
    dq_d, dk_d, dv_d = _sb_bwd(s["hb"], s["tot"], s["cnt"], dyd)
    dqm, dkt, dvt = _mla_bwd(s["qm"], s["km"], s["vm"], s["yc"], s["lse"], dyc)
    dc, dw_uq, dw_ukv, dg_q, dg_kv = _mla_prep_bwd(
        s["hf"], rope, p["mla_q_norm"], p["mla_kv_norm"], p["mla_w_uq"], p["mla_w_ukv"], dqm, dkt, dvt)
    dq_a, dk_a, dv_a, dsinks = _swa_bwd(s["hb"], p["attn_sinks"], dya)
    dx, dg_pre, dwt_in = _inproj_bwd(s["x"], p["norm_pre"], p["w_in"], s["xn"], dx_next,
                                     [dq_a, dk_a, dv_a, dq_d, dk_d, dv_d, dhf, dc])
    grads = dict(norm_pre=dg_pre[0], w_in_t=dwt_in, attn_sinks=dsinks[0, :4], conv_w=dconv_w[:3],
                 conv_b=dconv_b[0], mla_q_norm=dg_q[0], mla_w_uq=_uq_real(dw_uq), mla_kv_norm=dg_kv[0],
                 mla_w_ukv=_ukv_real(dw_ukv), group_norm=dg_grp[0], w_out=dw_out, norm_post=dg_post[0])
    return dx, grads


_WEIGHTS = ["norm_pre", "w_in", "attn_sinks", "conv_w", "conv_b", "mla_q_norm", "mla_w_uq", "mla_kv_norm",
            "mla_w_ukv", "group_norm", "w_out", "norm_post"]


def kernel(x, positions, norm_pre, w_in, attn_sinks, conv_w, conv_b, mla_q_norm, mla_w_uq, mla_kv_norm, mla_w_ukv, group_norm, w_out, norm_post, loss_target, m_norm_pre, m_w_in, m_attn_sinks, m_conv_w, m_conv_b, m_mla_q_norm, m_mla_w_uq, m_mla_kv_norm, m_mla_w_ukv, m_group_norm, m_w_out, m_norm_post, v_norm_pre, v_w_in, v_attn_sinks, v_conv_w, v_conv_b, v_mla_q_norm, v_mla_w_uq, v_mla_kv_norm, v_mla_w_ukv, v_group_norm, v_w_out, v_norm_post):
    w = dict(norm_pre=norm_pre, w_in=w_in, attn_sinks=attn_sinks, conv_w=conv_w, conv_b=conv_b,
             mla_q_norm=mla_q_norm, mla_w_uq=mla_w_uq, mla_kv_norm=mla_kv_norm, mla_w_ukv=mla_w_ukv,
             group_norm=group_norm, w_out=w_out, norm_post=norm_post)
    m = dict(norm_pre=m_norm_pre, w_in=m_w_in, attn_sinks=m_attn_sinks, conv_w=m_conv_w, conv_b=m_conv_b,
             mla_q_norm=m_mla_q_norm, mla_w_uq=m_mla_w_uq, mla_kv_norm=m_mla_kv_norm, mla_w_ukv=m_mla_w_ukv,
             group_norm=m_group_norm, w_out=m_w_out, norm_post=m_norm_post)
    v = dict(norm_pre=v_norm_pre, w_in=v_w_in, attn_sinks=v_attn_sinks, conv_w=v_conv_w, conv_b=v_conv_b,
             mla_q_norm=v_mla_q_norm, mla_w_uq=v_mla_w_uq, mla_kv_norm=v_mla_kv_norm, mla_w_ukv=v_mla_w_ukv,
             group_norm=v_group_norm, w_out=v_w_out, norm_post=v_norm_post)
    T = x.shape[1]
    xs = x[0]
    tgt = loss_target[0]

    def shard_parts(l):
        halves = lambda a: a.reshape((2, a.shape[0] // 2) + a.shape[1:])
        return [halves(jnp.swapaxes(w["w_in"][l], 0, 1).astype(BF16))] + [
            halves(w[n][l].astype(BF16)) for n in _BIG[1:4]] + [jnp.stack([w["conv_w"][l]] * 2)]

    def layer_params(l, got):
        whole = lambda a: a.reshape((4, 2 * a.shape[2]) + a.shape[3:])
        by_cols = lambda a: jnp.transpose(a, (1, 0, 2)).reshape(a.shape[1], 4 * a.shape[2])
        return dict(
            norm_pre=norm_pre[l:l + 1], w_in=_w_in_internal(got[0].reshape((8,) + got[0].shape[2:])),
            attn_sinks=attn_sinks[l], conv_w=by_cols(got[4][:, 0]), conv_b=conv_b[l:l + 1],
            mla_q_norm=mla_q_norm[l:l + 1], mla_w_uq=_uq_internal(by_cols(whole(got[2]))),
            mla_kv_norm=mla_kv_norm[l:l + 1], mla_w_ukv=_ukv_internal(by_cols(whole(got[3]))),
            group_norm=group_norm[l:l + 1], w_out=whole(got[1]).reshape(D_MODEL, D_MODEL),
            norm_post=norm_post[l:l + 1])

    layers, saved = [], []
    rope, *got = _rope_tables(positions[0].reshape(T, 1), shard_parts(0))
    h = xs
    for l in range(DEPTH):
        last = l == DEPTH - 1
        layers.append(layer_params(l, got))
        h, s, got = _layer_fwd(h, rope, layers[l], tgt if last else None, () if last else shard_parts(l + 1))
        saved.append(s)
    dy, loss_part = h

    grads = [None] * DEPTH
    for l in reversed(range(DEPTH)):
        dy, grads[l] = _layer_bwd(dy, rope, layers[l], saved[l])

    turned = ("w_in", "mla_w_uq")
    turn = lambda n, a: jnp.swapaxes(a, -1, -2) if n in turned else a

    def chunks(n, a):
        if n in ("w_out", "w_in"):
            return a.reshape(4, a.shape[0] // 4, a.shape[1])
        if n in turned:
            return a.T.reshape(4, a.shape[1] // 4, a.shape[0])
        return jnp.transpose(a.reshape(a.shape[0], 4, a.shape[1] // 4), (1, 0, 2))

    grad = lambda l, n: grads[l]["w_in_t" if n == "w_in" else n]
    per_layer = [[chunks(n, grad(l, n)) for n in _BIG] for l in range(DEPTH)]
    from_sibling = _swap_layer_chunks(per_layer)
    summed = [_add(a0, a1, b, "add_cores_" + n) for n, a0, a1, b in zip(_BIG, *per_layer, from_sibling)]
    small = _pack_small({n: jnp.stack([grads[l][n] for l in range(DEPTH)]) for n in _SMALL}, loss_part[0, 0])
    *got, got_small = _exchange_chips(summed, small)
    done = [_sum_leading(b, "sum_chips_" + n) for n, b in zip(_BIG, got)]
    done_other = _swap_cores(done, "swap_layer_shards")

    outs, loss = _adamw_small(w, m, v, got_small)
    for n, gm, go in zip(_BIG, done, done_other):
        for d, a in zip(outs, _adamw_layers(turn(n, w[n]), turn(n, m[n]), turn(n, v[n]), gm, go, "adamw_" + n)):
            d[n] = turn(n, a)
    return (loss[0, 0], dy[None], *[outs[0][n] for n in _WEIGHTS], *[outs[1][n] for n in _WEIGHTS],
            *[outs[2][n] for n in _WEIGHTS], *[outs[3][n] for n in _WEIGHTS])
```

```python
import math

import jax
import jax.numpy as jnp
from jax import lax
from jax.experimental import pallas as pl
from jax.experimental.pallas import tpu as pltpu

F32 = jnp.float32
BF16 = jnp.bfloat16
MESH = pl.DeviceIdType.MESH

D_MODEL = 1024
DEPTH = 2
EPS = 1e-6
BLOCK = 128
HEAD = 64
LANES = 128
GROUP = 256
LOG2E = 1.4426950408889634
LN2 = 0.6931471805599453
MLA_QSCALE = 96 ** -0.5 * LOG2E
ROPE_HALF = 16
ROPE_THETA = 10000.0
SWA_SUB = 2
ATT_BLK = 256
MLA_BQ = 512
NEG = -1e30
SB_DEAD = -104.0

ADAM_LR, ADAM_B1, ADAM_B2, ADAM_EPS, ADAM_WD, ADAM_STEP = 0.001, 0.9, 0.999, 1e-08, 0.01, 10

_REAL = [("a_q", 256), ("a_k", 128), ("a_v", 128), ("b_b", 256), ("b_c", 256), ("b_x", 256),
         ("c_q", 256), ("c_kv", 128), ("c_kr", 32), ("d_q", 256), ("d_k", 256), ("d_v", 256),
         ("gate", 1024)]
_REAL_OFF = {}
_o = 0
for _n, _w in _REAL:
    _REAL_OFF[_n] = (_o, _w)
    _o += _w
D_IN = _o
_INT_ORDER = ["a_q", "a_k", "a_v", "d_q", "d_k", "d_v", "gate", "b_b", "b_c", "b_x", "c_q", "c_kv", "c_kr"]
_INT_W = dict(_REAL)
_INT_W["c_kr"] = 128
_INT_OFF = {}
_o = 0
for _n in _INT_ORDER:
    _INT_OFF[_n] = _o
    _o += _INT_W[_n]
N_INT = _o
N_HB = _INT_OFF["gate"]
N_HF = N_INT - N_HB

VMEM_LIMIT = 56 * 1024 * 1024


def _cparams(sem):
    return pltpu.CompilerParams(dimension_semantics=sem, vmem_limit_bytes=VMEM_LIMIT)


def _dot(a, b):
    return jnp.dot(a, b, preferred_element_type=F32)


def _dot_nt(a, b):
    return lax.dot_general(a, b, (((1,), (1,)), ((), ())), preferred_element_type=F32)


def _dot_tn(a, b):
    return lax.dot_general(a, b, (((0,), (0,)), ((), ())), preferred_element_type=F32)


def _split(x):
    hi = x.astype(BF16)
    lo = (x - hi.astype(F32)).astype(BF16)
    return hi, lo


def _rms(x):
    return lax.rsqrt(jnp.mean(x * x, axis=-1, keepdims=True) + EPS)


def _rms_bwd(dy, xhat, r, g):
    dxhat = dy * g
    return r * (dxhat - xhat * jnp.mean(dxhat * xhat, axis=-1, keepdims=True)), dy * xhat


def _colsum(x):
    return jnp.sum(x, axis=0, keepdims=True)


def _inproj_fwd(x, g, wt):
    T = x.shape[0]
    tm = 512

    def body(x_ref, g_ref, w_ref, xn_ref, hb_ref, hf_ref):
        xv = x_ref[...]
        xn = (xv * _rms(xv) * g_ref[...]).astype(BF16)
        xn_ref[...] = xn
        h = _dot_nt(xn, w_ref[...])
        hb_ref[...] = h[:, :N_HB].astype(BF16)
        hf_ref[...] = h[:, N_HB:]

    return pl.pallas_call(
        body, name="inproj_fwd", grid=(T // tm,),
        in_specs=[pl.BlockSpec((tm, D_MODEL), lambda i: (i, 0)),
                  pl.BlockSpec((1, D_MODEL), lambda i: (0, 0)),
                  pl.BlockSpec((N_INT, D_MODEL), lambda i: (0, 0))],
        out_specs=[pl.BlockSpec((tm, D_MODEL), lambda i: (i, 0)),
                   pl.BlockSpec((tm, N_HB), lambda i: (i, 0)),
                   pl.BlockSpec((tm, N_HF), lambda i: (i, 0))],
        out_shape=[jax.ShapeDtypeStruct((T, D_MODEL), BF16),
                   jax.ShapeDtypeStruct((T, N_HB), BF16),
                   jax.ShapeDtypeStruct((T, N_HF), F32)],
        compiler_params=_cparams(("parallel",)),
    )(x, g, wt)


def _inproj_bwd_dx(x, g, wt, dx_next, pieces):
    T = x.shape[0]
    tm = 512
    widths = [p.shape[1] for p in pieces]
    assert sum(widths) == N_INT

    def body(x_ref, g_ref, w_ref, dxn_ref, *rest):
        p_refs = rest[:len(pieces)]
        dx_ref, dh_ref, dg_ref = rest[len(pieces):]
        dh = jnp.concatenate([p[...].astype(BF16) for p in p_refs], axis=1)
        dh_ref[...] = dh
        dxn = _dot(dh, w_ref[...])
        xv = x_ref[...]
        r = _rms(xv)
        dx, dgrow = _rms_bwd(dxn, xv * r, r, g_ref[...])
        dx_ref[...] = dx + dxn_ref[...]

        @pl.when(pl.program_id(0) == 0)
        def _():
            dg_ref[...] = jnp.zeros_like(dg_ref)

        dg_ref[...] += _colsum(dgrow)

    return pl.pallas_call(
        body, name="inproj_bwd_dx", grid=(T // tm,),
        in_specs=[pl.BlockSpec((tm, D_MODEL), lambda i: (i, 0)),
                  pl.BlockSpec((1, D_MODEL), lambda i: (0, 0)),
                  pl.BlockSpec((N_INT, D_MODEL), lambda i: (0, 0)),
                  pl.BlockSpec((tm, D_MODEL), lambda i: (i, 0))]
                 + [pl.BlockSpec((tm, wd), lambda i: (i, 0)) for wd in widths],
        out_specs=[pl.BlockSpec((tm, D_MODEL), lambda i: (i, 0)),
                   pl.BlockSpec((tm, N_INT), lambda i: (i, 0)),
                   pl.BlockSpec((1, D_MODEL), lambda i: (0, 0))],
        out_shape=[jax.ShapeDtypeStruct((T, D_MODEL), F32),
                   jax.ShapeDtypeStruct((T, N_INT), BF16),
                   jax.ShapeDtypeStruct((1, D_MODEL), F32)],
        compiler_params=_cparams(("arbitrary",)),
    )(x, g, wt, dx_next, *pieces)


def _inproj_bwd_dw(xn, dh):
    T = xn.shape[0]
    tm = min(512, T)

    def body(a_ref, b_ref, o_ref):
        @pl.when(pl.program_id(0) == 0)
        def _():
            o_ref[...] = jnp.zeros_like(o_ref)

        for n, wd in _REAL:
            o, oi = _REAL_OFF[n][0], _INT_OFF[n]
            o_ref[o:o + wd, :] += _dot_tn(b_ref[:, oi:oi + _INT_W[n]], a_ref[...])[:wd]

    return pl.pallas_call(
        body, name="inproj_bwd_dw", grid=(T // tm,),
        in_specs=[pl.BlockSpec((tm, D_MODEL), lambda t: (t, 0)),
                  pl.BlockSpec((tm, N_INT), lambda t: (t, 0))],
        out_specs=pl.BlockSpec((D_IN, D_MODEL), lambda t: (0, 0)),
        out_shape=jax.ShapeDtypeStruct((D_IN, D_MODEL), F32),
        compiler_params=_cparams(("arbitrary",)),
    )(xn, dh)


def _roll_f32(x, shift):
    return pltpu.roll(x.astype(F32), shift, 1)


def _swa_operands(h, q, k_prev, k_cur, v_prev, v_cur):
    p, e = h // 2, h % 2
    lane = lax.broadcasted_iota(jnp.int32, (1, LANES), 1) // HEAD
    q = q[:, p * LANES:(p + 1) * LANES]
    if e != p:
        q = _roll_f32(q, HEAD).astype(BF16)
        v_prev = _roll_f32(v_prev, HEAD).astype(BF16)
        v_cur = _roll_f32(v_cur, HEAD).astype(BF16)
    qs = jnp.where(lane == p, q, 0) * 0.125
    return dict(p=p, e=e, lane=lane, qs=qs, k_prev=k_prev, k_cur=k_cur,
                v_prev=jnp.where(lane == e, v_prev, 0), v_cur=jnp.where(lane == e, v_cur, 0),
                s_prev=_dot_nt(qs, k_prev), s_cur=_dot_nt(qs, k_cur))


def _swa_probs(ops, sink, no_prev):
    row = lax.broadcasted_iota(jnp.int32, (BLOCK, BLOCK), 0)
    col = lax.broadcasted_iota(jnp.int32, (BLOCK, BLOCK), 1)
    ok_prev = col > row if no_prev is None else jnp.logical_and(col > row, jnp.logical_not(no_prev))
    s_prev = jnp.where(ok_prev, ops["s_prev"], NEG)
    s_cur = jnp.where(col <= row, ops["s_cur"], NEG)
    m = jnp.maximum(jnp.maximum(jnp.max(s_prev, axis=1, keepdims=True),
                                jnp.max(s_cur, axis=1, keepdims=True)), sink)
    p_prev = jnp.exp(s_prev - m)
    p_cur = jnp.exp(s_cur - m)
    p_sink = jnp.exp(sink - m)
    inv = 1.0 / (jnp.sum(p_prev, axis=1, keepdims=True) + jnp.sum(p_cur, axis=1, keepdims=True) + p_sink)
    return p_prev * inv, p_cur * inv, p_sink * inv


def _swa_specs(T):
    n = T // (BLOCK * SWA_SUB)
    qo, ko, vo = (_INT_OFF[name] // LANES for name in ("a_q", "a_k", "a_v"))
    halo = lambda i: jnp.maximum(i * SWA_SUB - 1, 0)
    return [pl.BlockSpec((BLOCK * SWA_SUB, 256), lambda i: (i, qo // 2)),
            pl.BlockSpec((BLOCK, LANES), lambda i: (halo(i), ko)),
            pl.BlockSpec((BLOCK * SWA_SUB, LANES), lambda i: (i, ko)),
            pl.BlockSpec((BLOCK, LANES), lambda i: (halo(i), vo)),
            pl.BlockSpec((BLOCK * SWA_SUB, LANES), lambda i: (i, vo)),
            pl.BlockSpec(memory_space=pltpu.SMEM)], n


def _swa_units(q_ref, kh_ref, kc_ref, vh_ref, vc_ref, s_ref):
    blk = lambda a: slice(a * BLOCK, (a + 1) * BLOCK)
    units = [(a, h) for a in range(SWA_SUB) for h in range(4)]
    ops = {}
    for a, h in units:
        k_prev, v_prev = (kh_ref[...], vh_ref[...]) if a == 0 else (kc_ref[blk(a - 1), :], vc_ref[blk(a - 1), :])
        ops[a, h] = _swa_operands(h, q_ref[blk(a), :], k_prev, kc_ref[blk(a), :], v_prev, vc_ref[blk(a), :])
    probs = {(a, h): _swa_probs(ops[a, h], s_ref[h], pl.program_id(0) == 0 if a == 0 else None) for a, h in units}
    return units, ops, probs, blk


def _swa_fwd(hb, sinks):
    T = hb.shape[0]
    specs, n = _swa_specs(T)

    def body(q_ref, kh_ref, kc_ref, vh_ref, vc_ref, s_ref, o_ref):
        units, ops, probs, blk = _swa_units(q_ref, kh_ref, kc_ref, vh_ref, vc_ref, s_ref)
        outs = {u: _dot(probs[u][0].astype(BF16), ops[u]["v_prev"]) + _dot(probs[u][1].astype(BF16), ops[u]["v_cur"])
                for u in units}
        for a in range(SWA_SUB):
            for p in range(2):
                o_ref[blk(a), p * LANES:(p + 1) * LANES] = outs[a, 2 * p] + outs[a, 2 * p + 1]

    return pl.pallas_call(
        body, name="swa_fwd", grid=(n,), in_specs=specs,
        out_specs=pl.BlockSpec((BLOCK * SWA_SUB, 256), lambda i: (i, 0)),
        out_shape=jax.ShapeDtypeStruct((T, 256), F32),
        compiler_params=_cparams(("parallel",)),
    )(hb, hb, hb, hb, hb, sinks)


def _swa_bwd(hb, sinks, dy):
    T = hb.shape[0]
    specs, n = _swa_specs(T)

    def body(q_ref, kh_ref, kc_ref, vh_ref, vc_ref, s_ref, dy_ref, dq_ref, dk_ref, dv_ref, ds_ref):
        i = pl.program_id(0)

        @pl.when(i == 0)
        def _():
            ds_ref[...] = jnp.zeros_like(ds_ref)

        lane_id = lax.broadcasted_iota(jnp.int32, (8, LANES), 1)
        units, ops, probs, blk = _swa_units(q_ref, kh_ref, kc_ref, vh_ref, vc_ref, s_ref)
        dos = {(a, h): jnp.where(ops[a, h]["lane"] == ops[a, h]["e"],
                                 dy_ref[blk(a), ops[a, h]["p"] * LANES:(ops[a, h]["p"] + 1) * LANES], 0.0)
               for a, h in units}
        dobs = {u: dos[u].astype(BF16) for u in units}
        pbs = {u: (probs[u][0].astype(BF16), probs[u][1].astype(BF16)) for u in units}
        outs = {u: _dot(pbs[u][0], ops[u]["v_prev"]) + _dot(pbs[u][1], ops[u]["v_cur"]) for u in units}
        dps = {u: (_dot_nt(dobs[u], ops[u]["v_prev"]), _dot_nt(dobs[u], ops[u]["v_cur"])) for u in units}
        dss, dsinks = {}, jnp.zeros((8, LANES), F32)
        for u in units:
            delta = jnp.sum(dos[u] * outs[u], axis=1, keepdims=True)
            dss[u] = ((probs[u][0] * (dps[u][0] - delta)).astype(BF16),
                      (probs[u][1] * (dps[u][1] - delta)).astype(BF16))
            dsink = -jnp.sum(probs[u][2] * delta, axis=0, keepdims=True)
            dsinks += jnp.where(lane_id == u[1], dsink, 0.0)
        ds_ref[...] += dsinks
        dqs = {u: (_dot(dss[u][0], ops[u]["k_prev"]) + _dot(dss[u][1], ops[u]["k_cur"])) * 0.125 for u in units}
        zero = jnp.zeros((BLOCK, LANES), F32)
        dk_as_prev, dk_as_cur = [zero] * SWA_SUB, [zero] * SWA_SUB
        dv_as_prev, dv_as_cur = [zero] * SWA_SUB, [zero] * SWA_SUB
        for a, h in units:
            p, e = ops[a, h]["p"], ops[a, h]["e"]
            dob_v = dobs[a, h] if e == p else pltpu.roll(dos[a, h], HEAD, 1).astype(BF16)
            dk_as_prev[a] = dk_as_prev[a] + _dot_tn(dss[a, h][0], ops[a, h]["qs"])
            dk_as_cur[a] = dk_as_cur[a] + _dot_tn(dss[a, h][1], ops[a, h]["qs"])
            dv_as_prev[a] = dv_as_prev[a] + _dot_tn(pbs[a, h][0], dob_v)
            dv_as_cur[a] = dv_as_cur[a] + _dot_tn(pbs[a, h][1], dob_v)
        base = i * SWA_SUB
        for a in range(SWA_SUB):
            rows = pl.ds(pl.multiple_of((base + a) * BLOCK, BLOCK), BLOCK)
            more = a + 1 < SWA_SUB
            dk_ref[rows, :] = dk_as_cur[a] + (dk_as_prev[a + 1] if more else 0.0)
            dv_ref[rows, :] = dv_as_cur[a] + (dv_as_prev[a + 1] if more else 0.0)
        halo = pl.ds(pl.multiple_of(jnp.maximum(base - 1, 0) * BLOCK, BLOCK), BLOCK)
        dk_ref[halo, :] += dk_as_prev[0]
        dv_ref[halo, :] += dv_as_prev[0]
        for a in range(SWA_SUB):
            for p in range(2):
                dq_pair = jnp.zeros((BLOCK, LANES), F32)
                for e in range(2):
                    dq = jnp.where(ops[a, 2 * p + e]["lane"] == p, dqs[a, 2 * p + e], 0.0)
                    dq_pair += dq if e == p else pltpu.roll(dq, HEAD, 1)
                dq_ref[blk(a), p * LANES:(p + 1) * LANES] = dq_pair.astype(BF16)

    return pl.pallas_call(
        body, name="swa_bwd", grid=(n,),
        in_specs=specs + [pl.BlockSpec((BLOCK * SWA_SUB, 256), lambda i: (i, 0))],
        out_specs=[pl.BlockSpec((BLOCK * SWA_SUB, 256), lambda i: (i, 0)),
                   pl.BlockSpec((T, LANES), lambda i: (0, 0)),
                   pl.BlockSpec((T, LANES), lambda i: (0, 0)),
                   pl.BlockSpec((8, LANES), lambda i: (0, 0))],
        out_shape=[jax.ShapeDtypeStruct((T, 256), BF16),
                   jax.ShapeDtypeStruct((T, LANES), F32),
                   jax.ShapeDtypeStruct((T, LANES), F32),
                   jax.ShapeDtypeStruct((8, LANES), F32)],
        compiler_params=_cparams(("arbitrary",)),
    )(hb, hb, hb, hb, hb, sinks, dy)


def _rope_tables(pos, fetch=()):
    T = pos.shape[0]
    tm = 512
    nt, n = T // tm, len(fetch)

    def body(pos_ref, *rest):
        o_ref = rest[n]
        if n:
            start, finish = _gather_plan(rest[:n], rest[n + 1:2 * n + 1], rest[2 * n + 1:])
            pl.when(pl.program_id(0) == 0)(start)
        lane = lax.broadcasted_iota(jnp.int32, (1, LANES), 1)
        active = jnp.logical_and(lane >= HEAD, lane < HEAD + 2 * ROPE_HALF)
        idx = ((lane - HEAD) % ROPE_HALF).astype(F32)
        freq = jnp.exp(idx * (-math.log(ROPE_THETA) / ROPE_HALF))
        ang = pos_ref[...].astype(F32) * freq
        cos, sin = jnp.cos(ang), jnp.sin(ang)
        o_ref[:, 0:LANES] = jnp.where(active, cos, 1.0)
        o_ref[:, LANES:2 * LANES] = jnp.where(jnp.logical_and(active, lane >= HEAD + ROPE_HALF), sin, 0.0)
        o_ref[:, 2 * LANES:] = jnp.where(jnp.logical_and(active, lane < HEAD + ROPE_HALF), -sin, 0.0)
        if n:
            pl.when(pl.program_id(0) == nt - 1)(finish)

    return pl.pallas_call(
        body, name="rope_tables_fetch" if n else "rope_tables", grid=(nt,),
        in_specs=[pl.BlockSpec((tm, 1), lambda i: (i, 0))] + [HBM] * n,
        out_specs=[pl.BlockSpec((tm, 3 * LANES), lambda i: (i, 0))] + [HBM] * n,
        out_shape=[jax.ShapeDtypeStruct((T, 3 * LANES), F32)]
                  + [jax.ShapeDtypeStruct((4,) + s.shape, s.dtype) for s in fetch],
        scratch_shapes=_gather_sems(n) if n else [],
        compiler_params=_cparams(("arbitrary",) if n else ("parallel",)),
    )(pos, *fetch)


def _rope_factors(tab_ref):
    return tab_ref[:, 0:LANES], tab_ref[:, LANES:2 * LANES], tab_ref[:, 2 * LANES:]


def _rope(x, tabs):
    c, s_up, s_dn = tabs
    return x * c + pltpu.roll(x, ROPE_HALF, 1) * s_up + pltpu.roll(x, LANES - ROPE_HALF, 1) * s_dn


def _rope_t(dy, tabs):
    c, s_up, s_dn = tabs
    return dy * c + pltpu.roll(dy * s_up, LANES - ROPE_HALF, 1) + pltpu.roll(dy * s_dn, ROPE_HALF, 1)


def _mla_lat_specs(tm):
    cq, ckv, ckr = ((_INT_OFF[n] - N_HB) for n in ("c_q", "c_kv", "c_kr"))
    return [pl.BlockSpec((tm, 256), lambda i: (i, cq // 256)),
            pl.BlockSpec((tm, LANES), lambda i: (i, ckv // LANES)),
            pl.BlockSpec((tm, LANES), lambda i: (i, ckr // LANES)),
            pl.BlockSpec((tm, 3 * LANES), lambda i: (i, 0)),
            pl.BlockSpec((1, 256), lambda i: (0, 0)),
            pl.BlockSpec((1, LANES), lambda i: (0, 0)),
            pl.BlockSpec((256, 512), lambda i: (0, 0)),
            pl.BlockSpec((LANES, 768), lambda i: (0, 0))]


def _mla_prep_fwd(hf, rope, g_q, g_kv, w_uq, w_ukv):
    T = hf.shape[0]
    tm = 512
    sub = tm // ATT_BLK

    def body(cq_ref, ckv_ref, ckr_ref, tab_ref, gq_ref, gkv_ref, wq_ref, wkv_ref, qm_ref, km_ref, vm_ref, vt_ref):
        tabs = _rope_factors(tab_ref)
        cq = cq_ref[...]
        q = _dot((cq * _rms(cq) * gq_ref[...]).astype(BF16), wq_ref[...])
        ckv = ckv_ref[...]
        kv = _dot((ckv * _rms(ckv) * gkv_ref[...]).astype(BF16), wkv_ref[...])
        kr = _rope(pltpu.roll(ckr_ref[...], HEAD, 1), tabs)
        for h in range(4):
            sl = slice(h * LANES, (h + 1) * LANES)
            qm_ref[:, sl] = (_rope(q[:, sl], tabs) * MLA_QSCALE).astype(BF16)
            km_ref[:, sl] = (kv[:, sl] + kr).astype(BF16)
        vm_ref[...] = kv[:, 512:].astype(BF16)
        for p in range(2):
            for s in range(sub):
                tile = kv[s * ATT_BLK:(s + 1) * ATT_BLK, 512 + p * LANES:512 + (p + 1) * LANES]
                vt_ref[p, s] = jnp.transpose(tile).astype(BF16)

    return pl.pallas_call(
        body, name="mla_prep_fwd", grid=(T // tm,), in_specs=_mla_lat_specs(tm),
        out_specs=[pl.BlockSpec((tm, 512), lambda i: (i, 0)),
                   pl.BlockSpec((tm, 512), lambda i: (i, 0)),
                   pl.BlockSpec((tm, 256), lambda i: (i, 0)),
                   pl.BlockSpec((2, sub, LANES, ATT_BLK), lambda i: (0, i, 0, 0))],
        out_shape=[jax.ShapeDtypeStruct((T, 512), BF16),
                   jax.ShapeDtypeStruct((T, 512), BF16),
                   jax.ShapeDtypeStruct((T, 256), BF16),
                   jax.ShapeDtypeStruct((2, T // ATT_BLK, LANES, ATT_BLK), BF16)],
        compiler_params=_cparams(("parallel",)),
    )(hf, hf, hf, rope, g_q, g_kv, w_uq, w_ukv)


def _mla_prep_bwd(hf, rope, g_q, g_kv, w_uq, w_ukv, dqm, dkt, dvt):
    T = hf.shape[0]
    tm = 512
    sub = tm // ATT_BLK

    def body(cq_ref, ckv_ref, ckr_ref, tab_ref, gq_ref, gkv_ref, wq_ref, wkv_ref, dq_ref, dk_ref, dv_ref,
             dc_ref, dwq_ref, dwkv_ref, dgq_ref, dgkv_ref):
        @pl.when(pl.program_id(0) == 0)
        def _():
            dwq_ref[...] = jnp.zeros_like(dwq_ref)
            dwkv_ref[...] = jnp.zeros_like(dwkv_ref)
            dgq_ref[...] = jnp.zeros_like(dgq_ref)
            dgkv_ref[...] = jnp.zeros_like(dgkv_ref)

        tabs = _rope_factors(tab_ref)
        lane =lax.broadcasted_iota(jnp.int32, (1, LANES), 1)
        dq = jnp.concatenate([_rope_t(dq_ref[:, h * LANES:(h + 1) * LANES] * MLA_QSCALE, tabs)
                              for h in range(4)], axis=1).astype(BF16)
        cq = cq_ref[...]
        rq = _rms(cq)
        cqn = (cq * rq * gq_ref[...]).astype(BF16)
        dwq_ref[...] += _dot_tn(cqn, dq)
        dcq, dgrow = _rms_bwd(_dot_nt(dq, wq_ref[...]), cq * rq, rq, gq_ref[...])
        dgq_ref[...] += _colsum(dgrow)
        dc_ref[:, 0:256] = dcq.astype(BF16)

        dk = jnp.concatenate([jnp.concatenate([jnp.transpose(dk_ref[p, s]) for p in range(2)], axis=1)
                              for s in range(sub)], axis=0) * LN2
        dv = jnp.concatenate([jnp.concatenate([jnp.transpose(dv_ref[p, s]) for p in range(2)], axis=1)
                              for s in range(sub)], axis=0)
        dkr = dk[:, 0:LANES] + dk[:, LANES:2 * LANES] + dk[:, 2 * LANES:3 * LANES] + dk[:, 3 * LANES:]
        dkr = pltpu.roll(_rope_t(dkr, tabs), HEAD, 1)
        dc_ref[:, 384:512] = jnp.where(lane < 2 * ROPE_HALF, dkr, 0.0).astype(BF16)
        dkv = jnp.concatenate([dk.astype(BF16), dv.astype(BF16)], axis=1)
        ckv = ckv_ref[...]
        rkv = _rms(ckv)
        ckvn = (ckv * rkv * gkv_ref[...]).astype(BF16)
        dwkv_ref[...] += _dot_tn(ckvn, dkv)
        dckv, dgrow = _rms_bwd(_dot_nt(dkv, wkv_ref[...]), ckv * rkv, rkv, gkv_ref[...])
        dgkv_ref[...] += _colsum(dgrow)
        dc_ref[:, 256:384] = dckv.astype(BF16)

    return pl.pallas_call(
        body, name="mla_prep_bwd", grid=(T // tm,),
        in_specs=_mla_lat_specs(tm) + [pl.BlockSpec((tm, 512), lambda i: (i, 0)),
                                       pl.BlockSpec((2, sub, 256, ATT_BLK), lambda i: (0, i, 0, 0)),
                                       pl.BlockSpec((2, sub, LANES, ATT_BLK), lambda i: (0, i, 0, 0))],
        out_specs=[pl.BlockSpec((tm, 512), lambda i: (i, 0)),
                   pl.BlockSpec((256, 512), lambda i: (0, 0)),
                   pl.BlockSpec((LANES, 768), lambda i: (0, 0)),
                   pl.BlockSpec((1, 256), lambda i: (0, 0)),
                   pl.BlockSpec((1, LANES), lambda i: (0, 0))],
        out_shape=[jax.ShapeDtypeStruct((T, 512), BF16),
                   jax.ShapeDtypeStruct((256, 512), F32),
                   jax.ShapeDtypeStruct((LANES, 768), F32),
                   jax.ShapeDtypeStruct((1, 256), F32),
                   jax.ShapeDtypeStruct((1, LANES), F32)],
        compiler_params=_cparams(("arbitrary",)),
    )(hf, hf, hf, rope, g_q, g_kv, w_uq, w_ukv, dqm, dkt, dvt)


def _causal_masks(bq, bk):
    row = lax.broadcasted_iota(jnp.int32, (bq, bk), 0)
    col = lax.broadcasted_iota(jnp.int32, (bq, bk), 1)
    return row, col


def _mla_fwd(qm, km, vt, fetch=()):
    T = qm.shape[0]
    bq, bk = min(MLA_BQ, T), ATT_BLK
    nq, nsub, nk = T // bq, bq // bk, T // bk
    n = len(fetch)

    def body(q_ref, k_ref, vt_ref, *rest):
        o_ref, lse_ref = rest[n:n + 2]
        acc_ref, m_ref, l_ref = rest[2 * n + 2:2 * n + 5]
        if n:
            start, finish = _gather_plan(rest[:n], rest[n + 2:2 * n + 2], rest[2 * n + 5:])
            pl.when(pl.program_id(0) == 0)(start)
        qi = pl.program_id(0)
        key = lax.broadcasted_iota(jnp.int32, (bk, bq), 0)
        qry = lax.broadcasted_iota(jnp.int32, (bk, bq), 1)
        ones = jnp.ones((8, bk), BF16)
        acc_ref[...] = jnp.zeros_like(acc_ref)
        m_ref[...] = jnp.full_like(m_ref, NEG)
        l_ref[...] = jnp.zeros_like(l_ref)

        def step(kb0, masked):
            kbs = [kb0 + d for d in range(nsub)]
            qs = [slice(d * bk if masked else 0, bq) for d in range(nsub)]

            def wide(a, d, fill):
                if not qs[d].start:
                    return a
                return jnp.concatenate([jnp.full((a.shape[0], qs[d].start), fill, a.dtype), a], axis=1)

            sts = [[_dot_nt(k_ref[pl.ds(pl.multiple_of(kb * bk, bk), bk), e * LANES:(e + 1) * LANES],
                            q_ref[qs[d], e * LANES:(e + 1) * LANES]) for d, kb in enumerate(kbs)] for e in range(4)]
            pts, alphas = [], []
            for e in range(4):
                st = ([jnp.where(key[:, qs[d]] + d * bk <= qry[:, qs[d]], sts[e][d], NEG) for d in range(nsub)]
                      if masked else sts[e])
                m_prev = m_ref[e, 0:1, :]
                m_new = m_prev
                for d in range(nsub):
                    m_new = jnp.maximum(m_new, wide(jnp.max(st[d], axis=0, keepdims=True), d, NEG))
                alpha = jnp.exp2(m_prev - m_new)
                pt = [jnp.exp2(st[d] - m_new[:, qs[d]]).astype(BF16) for d in range(nsub)]
                l_new = alpha * l_ref[e]
                for d in range(nsub):
                    l_new = l_new + wide(_dot(ones, pt[d]), d, 0.0)
                l_ref[e] = l_new
                m_ref[e] = jnp.broadcast_to(m_new, (8, bq))
                pts.append(pt)
                alphas.append(alpha)
            for e in range(4):
                acc = alphas[e] * acc_ref[e]
                for d in range(nsub):
                    v_t = vt_ref[e // 2, kbs[d], (e % 2) * HEAD:(e % 2 + 1) * HEAD, :]
                    acc = acc + wide(_dot(v_t, pts[e][d]), d, 0.0)
                acc_ref[e] = acc

        step(qi * nsub, True)

        def loop(t, c):
            step(t * nsub, False)
            return c

        lax.fori_loop(0, qi, loop, 0)
        outs, lses = [], []
        for e in range(4):
            l = l_ref[e, 0:1, :]
            outs.append(acc_ref[e] / l)
            lses.append(jnp.broadcast_to(m_ref[e, 0:1, :] * LN2 + jnp.log(l), (HEAD, bq)))
        o_ref[...] = jnp.transpose(jnp.concatenate(outs, axis=0))
        lse_ref[...] = jnp.transpose(jnp.concatenate(lses, axis=0))
        if n:
            pl.when(pl.program_id(0) == nq - 1)(finish)

    return pl.pallas_call(
        body, name="mla_fwd_fetch" if n else "mla_fwd", grid=(nq,),
        in_specs=[pl.BlockSpec((bq, 512), lambda i: (i, 0)),
                  pl.BlockSpec((T, 512), lambda i: (0, 0)),
                  pl.BlockSpec((2, nk, LANES, bk), lambda i: (0, 0, 0, 0))] + [HBM] * n,
        out_specs=[pl.BlockSpec((bq, 256), lambda i: (i, 0)),
                   pl.BlockSpec((bq, 256), lambda i: (i, 0))] + [HBM] * n,
        out_shape=[jax.ShapeDtypeStruct((T, 256), F32), jax.ShapeDtypeStruct((T, 256), F32)]
                  + [jax.ShapeDtypeStruct((4,) + s.shape, s.dtype) for s in fetch],
        scratch_shapes=[pltpu.VMEM((4, HEAD, bq), F32), pltpu.VMEM((4, 8, bq), F32), pltpu.VMEM((4, 8, bq), F32)]
                       + (_gather_sems(n) if n else []),
        compiler_params=_cparams(("arbitrary",)),
    )(qm, km, vt, *fetch)


def _mla_bwd(qm, km, vm, y, lse, dy):
    T = qm.shape[0]
    bq, bk = min(MLA_BQ, T), ATT_BLK
    nq, nsub, nk = T // bq, bq // bk, T // bk

    def body(q_ref, k_ref, v_ref, y_ref, lse_ref, dy_ref, dq_ref, dkt_ref, dvt_ref, dob_ref, st_ref, qt_ref, dot_ref):
        qi = pl.program_id(1)

        @pl.when(qi == 0)
        def _():
            dkt_ref[...] = jnp.zeros_like(dkt_ref)
            dvt_ref[...] = jnp.zeros_like(dvt_ref)

        lane = lax.broadcasted_iota(jnp.int32, (1, LANES), 1) // HEAD
        row, col = _causal_masks(bq, bk)
        dq_ref[...] = jnp.zeros_like(dq_ref)
        lse = lse_ref[...]
        lse_other = pltpu.roll(lse, HEAD, 1)
        qt_ref[...] = jnp.transpose(q_ref[...].astype(F32)).astype(BF16)
        dot_ref[...] = jnp.transpose(dy_ref[...]).astype(BF16)
        for e in range(2):
            do = jnp.where(lane == e, dy_ref[...], 0.0)
            dob_ref[e] = do.astype(BF16)
            st_ref[2 * e] = jnp.where(lane == e, lse, lse_other) * LOG2E
            st_ref[2 * e + 1] = jnp.broadcast_to(jnp.sum(do * y_ref[...], axis=1, keepdims=True), (bq, LANES))

        hss = [slice(e * LANES, (e + 1) * LANES) for e in range(2)]
        tile = lambda a: jnp.concatenate([a] * (bk // LANES), axis=1)

        def step(kb0, masked):
            kbs = [kb0 + d for d in range(nsub)]
            rows = [pl.ds(pl.multiple_of(kb * bk, bk), bk) for kb in kbs]
            pairs = [(d, e) for d in range(nsub) for e in range(2)]
            qs = [slice(d * bk if masked else 0, bq) for d in range(nsub)]
            ss, dps = {}, {}
            for d, e in pairs:
                ss[d, e] = _dot_nt(q_ref[qs[d], hss[e]], k_ref[rows[d], hss[e]])
                dps[d, e] = _dot_nt(dob_ref[e, qs[d], :], jnp.where(lane == e, v_ref[rows[d], :], 0))
            ps, dss = {}, {}
            for d, e in pairs:
                s = jnp.where(col[qs[d]] + d * bk <= row[qs[d]], ss[d, e], NEG) if masked else ss[d, e]
                p = jnp.exp2(s - tile(st_ref[2 * e, qs[d], :]))
                dss[d, e] = (p * (dps[d, e] - tile(st_ref[2 * e + 1, qs[d], :]))).astype(BF16)
                ps[d, e] = p.astype(BF16)
            for d, e in pairs:
                dvt_ref[0, kbs[d], e * HEAD:(e + 1) * HEAD, :] += _dot(
                    dot_ref[e * HEAD:(e + 1) * HEAD, qs[d]], ps[d, e])
            for d, e in pairs:
                dkt_ref[0, kbs[d], hss[e], :] += _dot(qt_ref[hss[e], qs[d]], dss[d, e])
            for e in range(2):
                if masked:
                    for d in range(nsub):
                        dq_ref[qs[d], hss[e]] += _dot(dss[d, e], k_ref[rows[d], hss[e]])
                else:
                    dq = dq_ref[:, hss[e]]
                    for d in range(nsub):
                        dq = dq + _dot(dss[d, e], k_ref[rows[d], hss[e]])
                    dq_ref[:, hss[e]] = dq

        step(qi * nsub, True)

        def loop(t, c):
            step(t * nsub, False)
            return c

        lax.fori_loop(0, qi, loop, 0)
        dq_ref[...] *= LN2

    return pl.pallas_call(
        body, name="mla_bwd", grid=(2, nq),
        in_specs=[pl.BlockSpec((bq, 256), lambda j, i: (i, j)),
                  pl.BlockSpec((T, 256), lambda j, i: (0, j)),
                  pl.BlockSpec((T, LANES), lambda j, i: (0, j)),
                  pl.BlockSpec((bq, LANES), lambda j, i: (i, j)),
                  pl.BlockSpec((bq, LANES), lambda j, i: (i, j)),
                  pl.BlockSpec((bq, LANES), lambda j, i: (i, j))],
        out_specs=[pl.BlockSpec((bq, 256), lambda j, i: (i, j)),
                   pl.BlockSpec((1, nk, 256, bk), lambda j, i: (j, 0, 0, 0)),
                   pl.BlockSpec((1, nk, LANES, bk), lambda j, i: (j, 0, 0, 0))],
        out_shape=[jax.ShapeDtypeStruct((T, 512), F32),
                   jax.ShapeDtypeStruct((2, nk, 256, bk), F32),
                   jax.ShapeDtypeStruct((2, nk, LANES, bk), F32)],
        scratch_shapes=[pltpu.VMEM((2, bq, LANES), BF16), pltpu.VMEM((4, bq, LANES), F32),
                        pltpu.VMEM((256, bq), BF16), pltpu.VMEM((LANES, bq), BF16)],
        compiler_params=_cparams(("parallel", "arbitrary")),
    )(qm, km, vm, y, lse, dy)


def _suffix_ones(n):
    r = lax.broadcasted_iota(jnp.int32, (n, n), 0)
    c = lax.broadcasted_iota(jnp.int32, (n, n), 1)
    return (r >= c).astype(BF16)


def _prefix_ones(n):
    r = lax.broadcasted_iota(jnp.int32, (n, n), 0)
    c = lax.broadcasted_iota(jnp.int32, (n, n), 1)
    return (r <= c).astype(BF16)


def _sb_specs(T, bq):
    qo, ko, vo = (_INT_OFF[n] // 256 for n in ("d_q", "d_k", "d_v"))
    return [pl.BlockSpec((bq, 256), lambda i: (i, qo)),
            pl.BlockSpec((T, 256), lambda i: (0, ko)),
            pl.BlockSpec((T, 256), lambda i: (0, vo))]


def _sb_fwd(hb):
    T = hb.shape[0]
    bq = bk = ATT_BLK
    nq = T // bq

    def body(q_ref, k_ref, v_ref, o_ref, tot_ref, cnt_ref, qm_ref, car_ref):
        qi = pl.program_id(0)
        lane = lax.broadcasted_iota(jnp.int32, (1, LANES), 1) // HEAD
        row, col = _causal_masks(bq, bk)
        strict = col < row
        u = _suffix_ones(bk)
        o_ref[...] = jnp.zeros_like(o_ref)
        car_ref[...] = jnp.zeros_like(car_ref)
        pair = lambda h: slice((h // 2) * LANES, (h // 2 + 1) * LANES)
        for h in range(4):
            qm_ref[h] = jnp.where(lane == h % 2, q_ref[:, pair(h)], 0) * 0.125

        def step(blocks):
            tile = lambda a: jnp.concatenate([a] * (bk // LANES), axis=1)
            rows = [pl.ds(pl.multiple_of(kb * bk, bk), bk) for kb, _ in blocks]
            pairs = [(b, h) for b in range(len(blocks)) for h in range(4)]
            zs = {(b, h): _dot_nt(qm_ref[h], k_ref[rows[b], pair(h)]) for b, h in pairs}
            splits = {}
            for b, h in pairs:
                z = zs[b, h]
                lk = jnp.minimum(-z, 0.0) - jnp.log(1.0 + jnp.exp(-jnp.abs(z)))
                if blocks[b][1] is not None:
                    lk = jnp.where(blocks[b][1], lk, 0.0)
                splits[b, h] = _split(lk)
            sufs = {bh: _dot(hi, u) + _dot(lo, u) for bh, (hi, lo) in splits.items()}
            car = [car_ref[h] for h in range(4)]
            aas = {}
            for b, h in pairs:
                a = jnp.exp(zs[b, h] + sufs[b, h] + tile(car[h]))
                if blocks[b][1] is not None:
                    a = jnp.where(blocks[b][1], a, 0.0)
                aas[b, h] = a.astype(BF16)
                car[h] = car[h] + jnp.broadcast_to(sufs[b, h][:, 0:1], (bq, LANES))
            acc = [o_ref[:, pair(0)], o_ref[:, pair(2)]]
            for b, h in pairs:
                acc[h // 2] = acc[h // 2] + _dot(aas[b, h], jnp.where(lane == h % 2, v_ref[rows[b], pair(h)], 0))
            o_ref[:, pair(0)], o_ref[:, pair(2)] = acc
            for h in range(4):
                car_ref[h] = car[h]

        step([(qi, strict), (jnp.maximum(qi - 1, 0), qi > 0)])

        def live():
            worst = jnp.maximum(jnp.maximum(car_ref[0], car_ref[1]), jnp.maximum(car_ref[2], car_ref[3]))
            return jnp.max(worst) >= SB_DEAD

        def cond(c):
            return jnp.logical_and(c[0] < qi, c[1])

        def loop(c):
            step([(qi - 1 - c[0], None)])
            return c[0] + 1, live()

        done, _ = lax.while_loop(cond, loop, (jnp.minimum(qi, 1), live()))
        tot_ref[:, pair(0)] = jnp.where(lane == 0, car_ref[0], car_ref[1])
        tot_ref[:, pair(2)] = jnp.where(lane == 0, car_ref[2], car_ref[3])
        cnt_ref[0, qi] = done.astype(F32)

    return pl.pallas_call(
        body, name="sb_fwd", grid=(nq,), in_specs=_sb_specs(T, bq),
        out_specs=[pl.BlockSpec((bq, 256), lambda i: (i, 0)), pl.BlockSpec((bq, 256), lambda i: (i, 0)),
                   pl.BlockSpec(memory_space=pltpu.SMEM)],
        out_shape=[jax.ShapeDtypeStruct((T, 256), F32), jax.ShapeDtypeStruct((T, 256), F32),
                   jax.ShapeDtypeStruct((1, nq), F32)],
        scratch_shapes=[pltpu.VMEM((4, bq, LANES), BF16), pltpu.VMEM((4, bq, LANES), F32)],
        compiler_params=_cparams(("arbitrary",)),
    )(hb, hb, hb)


def _sb_bwd(hb, tot, cnt, dy):
    T = hb.shape[0]
    bq = bk = ATT_BLK
    nq = T // bq

    def body(q_ref, k_ref, v_ref, tot_ref, dy_ref, cnt_ref, dq_ref, dk_ref, dv_ref, qm_ref, dob_ref, dqa_ref, rem_ref,
             cg_ref):
        qi = pl.program_id(0)

        @pl.when(qi == 0)
        def _():
            dk_ref[...] = jnp.zeros_like(dk_ref)
            dv_ref[...] = jnp.zeros_like(dv_ref)

        lane = lax.broadcasted_iota(jnp.int32, (1, LANES), 1) // HEAD
        row, col = _causal_masks(bq, bk)
        strict = col < row
        u = _prefix_ones(bk)
        pair = lambda h: slice((h // 2) * LANES, (h // 2 + 1) * LANES)
        dqa_ref[...] = jnp.zeros_like(dqa_ref)
        cg_ref[...] = jnp.zeros_like(cg_ref)
        for h in range(4):
            tot = tot_ref[:, pair(h)]
            qm_ref[h] = jnp.where(lane == h % 2, q_ref[:, pair(h)], 0) * 0.125
            dob_ref[h] = jnp.where(lane == h % 2, dy_ref[:, pair(h)], 0.0).astype(BF16)
            rem_ref[h] = jnp.where(lane == h % 2, tot, pltpu.roll(tot, HEAD, 1))

        def step(blocks):
            tile = lambda a: jnp.concatenate([a] * (bk // LANES), axis=1)
            nb = len(blocks)
            rows = [pl.ds(pl.multiple_of(kb * bk, bk), bk) for kb, _ in blocks]
            pairs = [(b, h) for b in range(nb) for h in range(4)]
            mask = lambda b, x: x if blocks[b][1] is None else jnp.where(blocks[b][1], x, 0.0)
            zs, das = {}, {}
            for b, h in pairs:
                zs[b, h] = _dot_nt(qm_ref[h], k_ref[rows[b], pair(h)])
                das[b, h] = _dot_nt(dob_ref[h], jnp.where(lane == h % 2, v_ref[rows[b], pair(h)], 0))
            zls, splits = {}, {}
            for b, h in pairs:
                z = zs[b, h]
                lk = mask(b, jnp.minimum(-z, 0.0) - jnp.log(1.0 + jnp.exp(-jnp.abs(z))))
                zls[b, h] = z + lk
                splits[b, h] = _split(lk)
            pres = {bh: _dot(hi, u) + _dot(lo, u) for bh, (hi, lo) in splits.items()}
            rem = [rem_ref[h] for h in range(4)]
            aas, gs, gsplits = {}, {}, {}
            for b, h in pairs:
                a = mask(b, jnp.exp(zls[b, h] + (tile(rem[h]) - pres[b, h])))
                gs[b, h] = a * das[b, h]
                aas[b, h] = a.astype(BF16)
                gsplits[b, h] = _split(gs[b, h])
                rem[h] = rem[h] - jnp.broadcast_to(pres[b, h][:, bk - 1:bk], (bq, LANES))
            for b in range(nb):
                for p in (0, 2):
                    dv_ref[rows[b], pair(p)] += _dot_tn(aas[b, p], dob_ref[p]) + _dot_tn(aas[b, p + 1], dob_ref[p + 1])
            gpres = {bh: _dot(hi, u) + _dot(lo, u) for bh, (hi, lo) in gsplits.items()}
            cg = [cg_ref[h] for h in range(4)]
            dzs = {}
            for b, h in pairs:
                dz = mask(b, gs[b, h] - jnp.exp(zls[b, h]) * (tile(cg[h]) + gpres[b, h]))
                dzs[b, h] = dz.astype(BF16)
                cg[h] = cg[h] + jnp.broadcast_to(gpres[b, h][:, bk - 1:bk], (bq, LANES))
            for b in range(nb):
                for p in (0, 2):
                    dk_ref[rows[b], pair(p)] += _dot_tn(dzs[b, p], qm_ref[p]) + _dot_tn(dzs[b, p + 1], qm_ref[p + 1])
            for h in range(4):
                dq = dqa_ref[h]
                for b in range(nb):
                    dq = dq + _dot(dzs[b, h], k_ref[rows[b], pair(h)])
                dqa_ref[h] = dq
                rem_ref[h] = rem[h]
                cg_ref[h] = cg[h]

        def loop(kb, c):
            step([(kb, None)])
            return c

        start = qi - jnp.clip(cnt_ref[0, qi].astype(jnp.int32), 0, qi)
        lax.fori_loop(start, qi - 1, loop, 0)
        step([(jnp.maximum(qi - 1, 0), qi > 0), (qi, strict)])
        for p in (0, 2):
            dq_ref[:, pair(p)] = (jnp.where(lane == 0, dqa_ref[p], dqa_ref[p + 1]) * 0.125).astype(BF16)

    return pl.pallas_call(
        body, name="sb_bwd", grid=(nq,),
        in_specs=_sb_specs(T, bq) + [pl.BlockSpec((bq, 256), lambda i: (i, 0)),
                                     pl.BlockSpec((bq, 256), lambda i: (i, 0)),
                                     pl.BlockSpec(memory_space=pltpu.SMEM)],
        out_specs=[pl.BlockSpec((bq, 256), lambda i: (i, 0)),
                   pl.BlockSpec((T, 256), lambda i: (0, 0)),
                   pl.BlockSpec((T, 256), lambda i: (0, 0))],
        out_shape=[jax.ShapeDtypeStruct((T, 256), BF16)] + [jax.ShapeDtypeStruct((T, 256), F32)] * 2,
        scratch_shapes=[pltpu.VMEM((4, bq, LANES), BF16), pltpu.VMEM((4, bq, LANES), BF16),
                        pltpu.VMEM((4, bq, LANES), F32), pltpu.VMEM((4, bq, LANES), F32),
                        pltpu.VMEM((4, bq, LANES), F32)],
        compiler_params=_cparams(("arbitrary",)),
    )(hb, hb, hb, tot, dy, cnt)


EP_TM = 512


def _ep_in_specs(tm, rev):
    idx = (lambda i: rev - i) if rev is not None else (lambda i: i)
    bo = (_INT_OFF["b_b"] - N_HB) // 256
    halo = lambda i: jnp.maximum(idx(i) * (tm // 8) - 1, 0)
    return [pl.BlockSpec((tm, 256), lambda i: (idx(i), 0)),
            pl.BlockSpec((tm, 256), lambda i: (idx(i), 0)),
            pl.BlockSpec((tm, 256), lambda i: (idx(i), 0)),
            pl.BlockSpec((tm, D_MODEL), lambda i: (idx(i), 0)),
            pl.BlockSpec((tm, 256), lambda i: (idx(i), bo)),
            pl.BlockSpec((tm, 256), lambda i: (idx(i), bo + 1)),
            pl.BlockSpec((tm, 256), lambda i: (idx(i), bo + 2)),
            pl.BlockSpec((8, 256), lambda i: (halo(i), bo + 1)),
            pl.BlockSpec((8, 256), lambda i: (halo(i), bo + 2)),
            pl.BlockSpec((3, 256), lambda i: (0, 0)),
            pl.BlockSpec((1, 256), lambda i: (0, 0)),
            pl.BlockSpec((1, D_MODEL), lambda i: (0, 0)),
            pl.BlockSpec((D_MODEL, D_MODEL), lambda i: (0, 0)),
            pl.BlockSpec((1, D_MODEL), lambda i: (0, 0))]


def _ep_mix(first, ya_ref, yc_ref, yd_ref, gate_ref, bb_ref, bc_ref, bx_ref, hc_ref, hx_ref, cw_ref, cb_ref, gg_ref):
    tm = ya_ref.shape[0]
    u = bc_ref[...] * bx_ref[...]
    halo = jnp.where(first, 0.0, hc_ref[...] * hx_ref[...])
    row = lax.broadcasted_iota(jnp.int32, (tm, 1), 0)
    u1 = jnp.where(row == 0, halo[7:8, :], pltpu.roll(u, 1, 0))
    u2 = jnp.where(row == 0, halo[6:7, :], jnp.where(row == 1, halo[7:8, :], pltpu.roll(u, 2, 0)))
    cw = cw_ref[...]
    conv = cw[0:1, :] * u2 + cw[1:2, :] * u1 + cw[2:3, :] * u + cb_ref[...]
    bb = bb_ref[...]
    ys = [ya_ref[...], bb * conv, yc_ref[...], yd_ref[...]]
    rs = [_rms(y) for y in ys]
    gg = gg_ref[...]
    yhat = jnp.concatenate([y * r for y, r in zip(ys, rs)], axis=1)
    gate = gate_ref[...]
    sig = 1.0 / (1.0 + jnp.exp(-gate))
    return u, u1, u2, conv, bb, rs, yhat, yhat * gg, gate, sig


def _epilogue_fwd(x, ya, yc, yd, hf, conv_w, conv_b, g_grp, w_out, g_post, tgt=None):
    T = x.shape[0]
    tm = EP_TM
    row_spec = pl.BlockSpec((tm, D_MODEL), lambda i: (i, 0))

    def layer_out(refs):
        (x_ref, ya_ref, yc_ref, yd_ref, gate_ref, bb_ref, bc_ref, bx_ref, hc_ref, hx_ref, cw_ref, cb_ref,
         gg_ref, wo_ref, gp_ref) = refs
        (_, _, _, _, _, _, _, yn, gate, sig) = _ep_mix(
            pl.program_id(0) == 0, ya_ref, yc_ref, yd_ref, gate_ref, bb_ref, bc_ref, bx_ref, hc_ref, hx_ref,
            cw_ref, cb_ref, gg_ref)
        z = _dot((yn * (gate * sig)).astype(BF16), wo_ref[...])
        return x_ref[...] + z * _rms(z) * gp_ref[...]

    args = (x, ya, yc, yd, hf, hf, hf, hf, hf, hf, conv_w, conv_b, g_grp, w_out, g_post)
    in_specs = [row_spec] + _ep_in_specs(tm, None)
    if tgt is None:
        def body(*refs):
            refs[-1][...] = layer_out(refs[:-1])

        return pl.pallas_call(
            body, name="epilogue_fwd", grid=(T // tm,), in_specs=in_specs, out_specs=row_spec,
            out_shape=jax.ShapeDtypeStruct((T, D_MODEL), F32), compiler_params=_cparams(("parallel",)),
        )(*args)

    def body_loss(*refs):
        t_ref, dy_ref, l_ref = refs[-3:]

        @pl.when(pl.program_id(0) == 0)
        def _():
            l_ref[...] = jnp.zeros_like(l_ref)

        d = layer_out(refs[:-3]) - t_ref[...]
        dy_ref[...] = d * (1.0 / D_MODEL)
        part = jnp.sum(jnp.sum(d * d, axis=1, keepdims=True), axis=0, keepdims=True)
        l_ref[...] += part * (0.5 / D_MODEL)

    return pl.pallas_call(
        body_loss, name="epilogue_fwd_loss", grid=(T // tm,), in_specs=in_specs + [row_spec],
        out_specs=[row_spec, pl.BlockSpec((8, LANES), lambda i: (0, 0))],
        out_shape=[jax.ShapeDtypeStruct((T, D_MODEL), F32), jax.ShapeDtypeStruct((8, LANES), F32)],
        compiler_params=_cparams(("arbitrary",)),
    )(*args, tgt)


def _epilogue_bwd(dxn, ya, yc, yd, hf, conv_w, conv_b, g_grp, w_out, g_post):
    T = dxn.shape[0]
    tm = EP_TM
    nt = T // tm
    ridx = lambda i: (nt - 1 - i, 0)

    def body(dx_ref, ya_ref, yc_ref, yd_ref, gate_ref, bb_ref, bc_ref, bx_ref, hc_ref, hx_ref, cw_ref, cb_ref,
             gg_ref, wo_ref, gp_ref,
             dya_ref, dyc_ref, dyd_ref, dhf_ref, dwo_ref, dgp_ref, dgg_ref, dcw_ref, dcb_ref, carry_ref):
        i = pl.program_id(0)

        @pl.when(i == 0)
        def _():
            for r in (dwo_ref, dgp_ref, dgg_ref, dcw_ref, dcb_ref, carry_ref):
                r[...] = jnp.zeros_like(r)

        (u, u1, u2, conv, bb, rs, yhat, yn, gate, sig) = _ep_mix(
            i == nt - 1, ya_ref, yc_ref, yd_ref, gate_ref, bb_ref, bc_ref, bx_ref, hc_ref, hx_ref,
            cw_ref, cb_ref, gg_ref)
        silu = gate * sig
        ymix = (yn * silu).astype(BF16)
        z = _dot(ymix, wo_ref[...])
        rz = _rms(z)
        dz, dgrow = _rms_bwd(dx_ref[...], z * rz, rz, gp_ref[...])
        dgp_ref[...] += _colsum(dgrow)
        dzb = dz.astype(BF16)
        dwo_ref[...] += _dot_tn(ymix, dzb)
        dymix = _dot_nt(dzb, wo_ref[...])
        dhf_ref[:, 0:D_MODEL] = (dymix * yn * (sig * (1.0 + gate * (1.0 - sig)))).astype(BF16)
        dyn = dymix * silu
        dgg_ref[...] += _colsum(dyn * yhat)
        gg = gg_ref[...]
        dys = []
        for gi in range(4):
            sl = slice(gi * GROUP, (gi + 1) * GROUP)
            dyh = dyn[:, sl] * gg[:, sl]
            yh = yhat[:, sl]
            dys.append(rs[gi] * (dyh - yh * jnp.mean(dyh * yh, axis=-1, keepdims=True)))
        dya_ref[...] = dys[0]
        dyc_ref[...] = dys[2]
        dyd_ref[...] = dys[3]
        dyb = dys[1]
        dhf_ref[:, D_MODEL:D_MODEL + 256] = (dyb * conv).astype(BF16)
        dconv = dyb * bb
        dcb_ref[...] += _colsum(dconv)
        dcw_ref[0:1, :] += _colsum(dconv * u2)
        dcw_ref[1:2, :] += _colsum(dconv * u1)
        dcw_ref[2:3, :] += _colsum(dconv * u)
        carry = carry_ref[...]
        row = lax.broadcasted_iota(jnp.int32, (tm, 1), 0)
        d1 = jnp.where(row == tm - 1, carry[0:1, :], pltpu.roll(dconv, tm - 1, 0))
        d2 = jnp.where(row == tm - 2, carry[0:1, :],
                       jnp.where(row == tm - 1, carry[1:2, :], pltpu.roll(dconv, tm - 2, 0)))
        cw = cw_ref[...]
        du = cw[2:3, :] * dconv + cw[1:2, :] * d1 + cw[0:1, :] * d2
        dhf_ref[:, D_MODEL + 256:D_MODEL + 512] = (du * bx_ref[...]).astype(BF16)
        dhf_ref[:, D_MODEL + 512:D_MODEL + 768] = (du * bc_ref[...]).astype(BF16)
        carry_ref[...] = dconv[0:8, :]

    in_specs = [pl.BlockSpec((tm, D_MODEL), ridx)] + _ep_in_specs(tm, nt - 1)
    return pl.pallas_call(
        body, name="epilogue_bwd", grid=(nt,), in_specs=in_specs,
        out_specs=[pl.BlockSpec((tm, 256), ridx), pl.BlockSpec((tm, 256), ridx), pl.BlockSpec((tm, 256), ridx),
                   pl.BlockSpec((tm, D_MODEL + 768), ridx),
                   pl.BlockSpec((D_MODEL, D_MODEL), lambda i: (0, 0)),
                   pl.BlockSpec((1, D_MODEL), lambda i: (0, 0)),
                   pl.BlockSpec((1, D_MODEL), lambda i: (0, 0)),
                   pl.BlockSpec((8, 256), lambda i: (0, 0)),
                   pl.BlockSpec((1, 256), lambda i: (0, 0))],
        out_shape=[jax.ShapeDtypeStruct((T, 256), F32)] * 3
                  + [jax.ShapeDtypeStruct((T, D_MODEL + 768), BF16),
                     jax.ShapeDtypeStruct((D_MODEL, D_MODEL), F32),
                     jax.ShapeDtypeStruct((1, D_MODEL), F32),
                     jax.ShapeDtypeStruct((1, D_MODEL), F32),
                     jax.ShapeDtypeStruct((8, 256), F32),
                     jax.ShapeDtypeStruct((1, 256), F32)],
        scratch_shapes=[pltpu.VMEM((8, 256), F32)],
        compiler_params=_cparams(("arbitrary",)),
    )(dxn, ya, yc, yd, hf, hf, hf, hf, hf, hf, conv_w, conv_b, g_grp, w_out, g_post)


def _place():
    return lax.axis_index("x"), lax.axis_index("y"), lax.axis_index("c")


def _other_chips(x, y):
    return [(1 - x, y), (x, 1 - y), (1 - x, 1 - y)]


HBM = pl.BlockSpec(memory_space=pl.ANY)


def _gather_plan(ins, outs, sems):
    n = len(ins)
    ici_send, ici_recv, d2d_send, d2d_recv, local_sems = sems
    x, y, c = _place()
    me = 2 * x + y
    chips = _other_chips(x, y)

    def ici(a, j, chip_from):
        px, py = chips[j]
        return pltpu.make_async_remote_copy(
            src_ref=ins[a].at[c], dst_ref=outs[a].at[chip_from, c], send_sem=ici_send.at[3 * a + j],
            recv_sem=ici_recv.at[3 * a + j], device_id=(px, py, c), device_id_type=MESH)

    def d2d(a, j, part):
        px, py = chips[j]
        blk = outs[a].at[2 * px + py, part]
        return pltpu.make_async_remote_copy(
            src_ref=blk, dst_ref=blk, send_sem=d2d_send.at[3 * a + j], recv_sem=d2d_recv.at[3 * a + j],
            device_id=(x, y, 1 - c), device_id_type=MESH)

    def local(a):
        return pltpu.make_async_copy(ins[a], outs[a].at[me], local_sems.at[a])

    hops = [(j, a) for j in range(3) for a in range(n)]

    def start():
        for a in range(n):
            local(a).start()
        for j, a in hops:
            ici(a, j, me).start()

    def finish():
        for j, a in hops:
            ici(a, j, 2 * chips[j][0] + chips[j][1]).wait_recv()
            d2d(a, j, c).start()
        for j, a in hops:
            d2d(a, j, 1 - c).wait_recv()
        for j, a in hops:
            ici(a, j, me).wait_send()
            d2d(a, j, c).wait_send()
        for a in range(n):
            local(a).wait()

    return start, finish


def _gather_sems(n):
    return [pltpu.SemaphoreType.DMA((3 * n,))] * 4 + [pltpu.SemaphoreType.DMA((n,))]


def _exchange_chips(parts, small):
    n = len(parts)

    def body(*refs):
        ins, sm_ref = refs[:n], refs[n]
        outs, osm_ref = refs[n + 1:2 * n + 1], refs[2 * n + 1]
        send_sems, recv_sems, ssend_sems, srecv_sems, local_sems = refs[2 * n + 2:]
        x, y, c = _place()
        me = 2 * x + y
        dev = 4 * x + 2 * y + c
        local = [pltpu.make_async_copy(ins[a].at[me], outs[a].at[me], local_sems.at[a]) for a in range(n)]
        local.append(pltpu.make_async_copy(sm_ref, osm_ref.at[dev], local_sems.at[n]))
        for cp in local:
            cp.start()
        sends = []
        for j, (px, py) in enumerate(_other_chips(x, y)):
            for a in range(n):
                cp = pltpu.make_async_remote_copy(
                    src_ref=ins[a].at[2 * px + py], dst_ref=outs[a].at[me], send_sem=send_sems.at[3 * a + j],
                    recv_sem=recv_sems.at[3 * a + j], device_id=(px, py, c), device_id_type=MESH)
                cp.start()
                sends.append(cp)
        flips = [(fx, fy, fc) for fx in (0, 1) for fy in (0, 1) for fc in (0, 1)][1:]
        for j, (fx, fy, fc) in enumerate(flips):
            cp = pltpu.make_async_remote_copy(
                src_ref=sm_ref, dst_ref=osm_ref.at[dev], send_sem=ssend_sems.at[j], recv_sem=srecv_sems.at[j],
                device_id=(x ^ fx, y ^ fy, c ^ fc), device_id_type=MESH)
            cp.start()
            sends.append(cp)
        for j, (px, py) in enumerate(_other_chips(x, y)):
            for a in range(n):
                pltpu.make_async_remote_copy(
                    src_ref=ins[a].at[me], dst_ref=outs[a].at[2 * px + py], send_sem=send_sems.at[3 * a + j],
                    recv_sem=recv_sems.at[3 * a + j], device_id=(px, py, c), device_id_type=MESH).wait_recv()
        for j, (fx, fy, fc) in enumerate(flips):
            src = 4 * (x ^ fx) + 2 * (y ^ fy) + (c ^ fc)
            pltpu.make_async_remote_copy(
                src_ref=sm_ref, dst_ref=osm_ref.at[src], send_sem=ssend_sems.at[j], recv_sem=srecv_sems.at[j],
                device_id=(x ^ fx, y ^ fy, c ^ fc), device_id_type=MESH).wait_recv()
        for cp in sends:
            cp.wait_send()
        for cp in local:
            cp.wait()

    return pl.pallas_call(
        body, name="exchange_chips",
        in_specs=[HBM] * (n + 1), out_specs=[HBM] * (n + 1),
        out_shape=[jax.ShapeDtypeStruct(p.shape, p.dtype) for p in parts]
                  + [jax.ShapeDtypeStruct((8,) + small.shape, small.dtype)],
        scratch_shapes=[pltpu.SemaphoreType.DMA((3 * n,)), pltpu.SemaphoreType.DMA((3 * n,)),
                        pltpu.SemaphoreType.DMA((7,)), pltpu.SemaphoreType.DMA((7,)),
                        pltpu.SemaphoreType.DMA((n + 1,))],
    )(*parts, small)


def _swap_cores(parts, name):
    n = len(parts)

    def body(*refs):
        ins, outs, send_sems, recv_sems = refs[:n], refs[n:2 * n], refs[2 * n], refs[2 * n + 1]
        x, y, c = _place()
        copies = [pltpu.make_async_remote_copy(
            src_ref=ins[a], dst_ref=outs[a], send_sem=send_sems.at[a], recv_sem=recv_sems.at[a],
            device_id=(x, y, 1 - c), device_id_type=MESH) for a in range(n)]
        for cp in copies:
            cp.start()
        for cp in copies:
            cp.wait()

    return pl.pallas_call(
        body, name=name, in_specs=[HBM] * n, out_specs=[HBM] * n,
        out_shape=[jax.ShapeDtypeStruct(p.shape, p.dtype) for p in parts],
        scratch_shapes=[pltpu.SemaphoreType.DMA((n,)), pltpu.SemaphoreType.DMA((n,))],
    )(*parts)


def _tile(rows, cols):
    for cand in (256, 128, 64):
        if rows % cand == 0:
            return cand, cols
    if rows > 64 and cols % 256 == 0:
        return rows, 256
    return rows, cols


def _swap_layer_chunks(per_layer):
    n = len(per_layer[0])

    def body(*refs):
        ins, outs, send_sems, recv_sems = (refs[:n], refs[n:2 * n]), refs[2 * n:3 * n], refs[3 * n], refs[3 * n + 1]
        x, y, c = _place()
        for l in range(2):
            @pl.when(c == 1 - l)
            def _():
                copies = [pltpu.make_async_remote_copy(
                    src_ref=ins[l][a], dst_ref=outs[a], send_sem=send_sems.at[a], recv_sem=recv_sems.at[a],
                    device_id=(x, y, 1 - c), device_id_type=MESH) for a in range(n)]
                for cp in copies:
                    cp.start()
                for cp in copies:
                    cp.wait()

    return pl.pallas_call(
        body, name="swap_layer_chunks", in_specs=[HBM] * (2 * n), out_specs=[HBM] * n,
        out_shape=[jax.ShapeDtypeStruct(p.shape, p.dtype) for p in per_layer[0]],
        scratch_shapes=[pltpu.SemaphoreType.DMA((n,)), pltpu.SemaphoreType.DMA((n,))],
    )(*per_layer[0], *per_layer[1])


def _add(a0, a1, b, name):
    L, R, C = b.shape
    tr, tc = _tile(R, C)

    def body(a0_ref, a1_ref, b_ref, o_ref):
        mine = jnp.where(lax.axis_index("c") == 0, a0_ref[...], a1_ref[...])
        o_ref[...] = (mine + b_ref[...]).astype(BF16)

    spec = pl.BlockSpec((1, tr, tc), lambda l, i, j: (l, i, j))
    return pl.pallas_call(
        body, name=name, grid=(L, R // tr, C // tc), in_specs=[spec] * 3, out_specs=spec,
        out_shape=jax.ShapeDtypeStruct((L, R, C), BF16),
        compiler_params=_cparams(("parallel", "parallel", "parallel")),
    )(a0, a1, b)


def _sum_leading(buf, name):
    n, R, C = buf.shape
    tr, tc = _tile(R, C)

    def body(b_ref, o_ref):
        acc = b_ref[0].astype(F32)
        for k in range(1, n):
            acc = acc + b_ref[k].astype(F32)
        o_ref[...] = acc

    return pl.pallas_call(
        body, name=name, grid=(R // tr, C // tc),
        in_specs=[pl.BlockSpec((n, tr, tc), lambda i, j: (0, i, j))],
        out_specs=pl.BlockSpec((tr, tc), lambda i, j: (i, j)),
        out_shape=jax.ShapeDtypeStruct((R, C), F32),
        compiler_params=_cparams(("parallel", "parallel")),
    )(buf)


def _adam_update(w, g, m, v):
    c1 = 1.0 / (1.0 - ADAM_B1 ** ADAM_STEP)
    c2 = 1.0 / (1.0 - ADAM_B2 ** ADAM_STEP)
    mn = ADAM_B1 * m + (1.0 - ADAM_B1) * g
    vn = ADAM_B2 * v + (1.0 - ADAM_B2) * (g * g)
    return -ADAM_LR * ((mn * c1) / (jnp.sqrt(vn * c2) + ADAM_EPS) + ADAM_WD * w), mn, vn


def _adamw_layers(w, m, v, g_mine, g_other, name):
    _, R, C = w.shape
    tr, tc = _tile(R, C)

    def body(w_ref, m_ref, v_ref, gm_ref, go_ref, g_ref, d_ref, mo_ref, vo_ref):
        g = jnp.where(pl.program_id(0) == lax.axis_index("c"), gm_ref[...], go_ref[...])
        g_ref[0] = g
        d_ref[0], mo_ref[0], vo_ref[0] = _adam_update(w_ref[0], g, m_ref[0], v_ref[0])

    spec3 = pl.BlockSpec((1, tr, tc), lambda l, i, j: (l, i, j))
    spec2 = pl.BlockSpec((tr, tc), lambda l, i, j: (i, j))
    return pl.pallas_call(
        body, name=name, grid=(2, R // tr, C // tc),
        in_specs=[spec3] * 3 + [spec2] * 2, out_specs=[spec3] * 4,
        out_shape=[jax.ShapeDtypeStruct(w.shape, F32)] * 4,
        compiler_params=_cparams(("parallel", "parallel", "parallel")),
    )(w, m, v, g_mine, g_other)


PACK_C = 1024
_BIG = ("w_in", "w_out", "mla_w_uq", "mla_w_ukv", "conv_w")
_SMALL = ("norm_pre", "group_norm", "norm_post", "conv_b", "mla_q_norm", "mla_kv_norm", "attn_sinks")
_SMALL_W = {"norm_pre": 1024, "group_norm": 1024, "norm_post": 1024, "conv_b": 256, "mla_q_norm": 256,
            "mla_kv_norm": 128, "attn_sinks": 4}


_LOSS_AT = divmod(DEPTH * sum(_SMALL_W.values()), PACK_C)


def _pack_small(d, loss):
    flat = jnp.concatenate([d[n].reshape(-1) for n in _SMALL] + [loss.reshape(1)])
    return jnp.pad(flat, (0, 8 * PACK_C - flat.shape[0])).reshape(8, PACK_C)


def _adamw_small(w, m, v, got):
    ns = len(_SMALL)

    def body(*refs):
        got_ref = refs[3 * ns]
        outs = refs[3 * ns + 1:]
        gsum = got_ref[0]
        for d in range(1, 8):
            gsum = gsum + got_ref[d]
        outs[4 * ns][...] = gsum[_LOSS_AT[0]:_LOSS_AT[0] + 1, _LOSS_AT[1]:_LOSS_AT[1] + 1]
        off = 0
        for i, name in enumerate(_SMALL):
            wd = _SMALL_W[name]
            rows = []
            for l in range(DEPTH):
                r, c0 = divmod(off + l * wd, PACK_C)
                rows.append(gsum[r:r + 1, c0:c0 + wd])
            off += DEPTH * wd
            g = jnp.concatenate(rows, axis=0)
            delta, mn, vn = _adam_update(refs[i][...], g, refs[ns + i][...], refs[2 * ns + i][...])
            outs[i][...] = g
            outs[ns + i][...] = delta
            outs[2 * ns + i][...] = mn
            outs[3 * ns + i][...] = vn

    shapes = [jax.ShapeDtypeStruct(w[n].shape, F32) for n in _SMALL]
    res = pl.pallas_call(body, name="adamw_small", out_shape=shapes * 4 + [jax.ShapeDtypeStruct((1, 1), F32)])(
        *[w[n] for n in _SMALL], *[m[n] for n in _SMALL], *[v[n] for n in _SMALL], got)
    return [dict(zip(_SMALL, res[k * ns:(k + 1) * ns])) for k in range(4)], res[4 * ns]


def _w_in_internal(slabs):
    S, R, D = slabs.shape
    assert S * R == D_IN

    def body(w_ref, o_ref):
        for n, wd in _REAL:
            o, oi = _REAL_OFF[n][0], _INT_OFF[n]
            r = o
            while r < o + wd:
                end = min(o + wd, (r // R + 1) * R)
                o_ref[oi + r - o:oi + end - o, :] = w_ref[r // R, r % R:r % R + end - r, :]
                r = end
            if _INT_W[n] != wd:
                o_ref[oi + wd:oi + _INT_W[n], :] = jnp.zeros((_INT_W[n] - wd, D), slabs.dtype)

    return pl.pallas_call(
        body, name="w_in_internal", out_shape=jax.ShapeDtypeStruct((N_INT, D), slabs.dtype),
        compiler_params=pltpu.CompilerParams(vmem_limit_bytes=VMEM_LIMIT),
    )(slabs)


def _uq_internal(w):
    return jnp.pad(w.reshape(256, 4, 96), ((0, 0), (0, 0), (0, 32))).reshape(256, 512)


def _uq_real(dw):
    return dw.reshape(256, 4, 128)[:, :, :96].reshape(256, 384)


def _ukv_internal(w):
    w4 = w.reshape(128, 4, 128)
    k = jnp.pad(w4[:, :, :64], ((0, 0), (0, 0), (0, 64))).reshape(128, 512)
    return jnp.concatenate([k, w4[:, :, 64:].reshape(128, 256)], axis=1)


def _ukv_real(dw):
    k = dw[:, :512].reshape(128, 4, 128)[:, :, :64]
    v = dw[:, 512:].reshape(128, 4, 64)
    return jnp.concatenate([k, v], axis=2).reshape(128, 512)


def _layer_fwd(x, rope, p, tgt=None, fetch=()):
    xn, hb, hf = _inproj_fwd(x, p["norm_pre"], p["w_in"])
    ya = _swa_fwd(hb, p["attn_sinks"])
    qm, km, vm, vt = _mla_prep_fwd(hf, rope, p["mla_q_norm"], p["mla_kv_norm"], p["mla_w_uq"], p["mla_w_ukv"])
    yc, lse, *fetched = _mla_fwd(qm, km, vt, fetch)
    yd, tot, cnt = _sb_fwd(hb)
    x_next = _epilogue_fwd(x, ya, yc, yd, hf, p["conv_w"], p["conv_b"], p["group_norm"], p["w_out"], p["norm_post"],
                           tgt)
    saved = dict(x=x, xn=xn, hb=hb, hf=hf, ya=ya, yc=yc, yd=yd, tot=tot, cnt=cnt, qm=qm, km=km, vm=vm, lse=lse)
    return x_next, saved, fetched


def _layer_bwd(dx_next, rope, p, s):
    (dya, dyc, dyd, dhf, dw_out, dg_post, dg_grp, dconv_w, dconv_b) = _epilogue_bwd(
        dx_next, s["ya"], s["yc"], s["yd"], s["hf"], p["conv_w"], p["conv_b"], p["group_norm"], p["w_out"],
        p["norm_post"])
    dq_d, dk_d, dv_d = _sb_bwd(s["hb"], s["tot"], s["cnt"], dyd)
    dqm, dkt, dvt = _mla_bwd(s["qm"], s["km"], s["vm"], s["yc"], s["lse"], dyc)
    dc, dw_uq, dw_ukv, dg_q, dg_kv = _mla_prep_bwd(
        s["hf"], rope, p["mla_q_norm"], p["mla_kv_norm"], p["mla_w_uq"], p["mla_w_ukv"], dqm, dkt, dvt)
    dq_a, dk_a, dv_a, dsinks = _swa_bwd(s["hb"], p["attn_sinks"], dya)
    dx, dh, dg_pre = _inproj_bwd_dx(s["x"], p["norm_pre"], p["w_in"], dx_next,
                                    [dq_a, dk_a, dv_a, dq_d, dk_d, dv_d, dhf, dc])
    grads = dict(norm_pre=dg_pre[0], w_in_t=_inproj_bwd_dw(s["xn"], dh), attn_sinks=dsinks[0, :4], conv_w=dconv_w[:3],
                 conv_b=dconv_b[0], mla_q_norm=dg_q[0], mla_w_uq=_uq_real(dw_uq), mla_kv_norm=dg_kv[0],
                 mla_w_ukv=_ukv_real(dw_ukv), group_norm=dg_grp[0], w_out=dw_out, norm_post=dg_post[0])
    return dx, grads


_WEIGHTS = ["norm_pre", "w_in", "attn_sinks", "conv_w", "conv_b", "mla_q_norm", "mla_w_uq", "mla_kv_norm",
            "mla_w_ukv", "group_norm", "w_out", "norm_post"]


def kernel(x, positions, norm_pre, w_in, attn_sinks, conv_w, conv_b, mla_q_norm, mla_w_uq, mla_kv_norm, mla_w_ukv, group_norm, w_out, norm_post, loss_target, m_norm_pre, m_w_in, m_attn_sinks, m_conv_w, m_conv_b, m_mla_q_norm, m_mla_w_uq, m_mla_kv_norm, m_mla_w_ukv, m_group_norm, m_w_out, m_norm_post, v_norm_pre, v_w_in, v_attn_sinks, v_conv_w, v_conv_b, v_mla_q_norm, v_mla_w_uq, v_mla_kv_norm, v_mla_w_ukv, v_group_norm, v_w_out, v_norm_post):
    w = dict(norm_pre=norm_pre, w_in=w_in, attn_sinks=attn_sinks, conv_w=conv_w, conv_b=conv_b,
             mla_q_norm=mla_q_norm, mla_w_uq=mla_w_uq, mla_kv_norm=mla_kv_norm, mla_w_ukv=mla_w_ukv,
             group_norm=group_norm, w_out=w_out, norm_post=norm_post)
    m = dict(norm_pre=m_norm_pre, w_in=m_w_in, attn_sinks=m_attn_sinks, conv_w=m_conv_w, conv_b=m_conv_b,
             mla_q_norm=m_mla_q_norm, mla_w_uq=m_mla_w_uq, mla_kv_norm=m_mla_kv_norm, mla_w_ukv=m_mla_w_ukv,
             group_norm=m_group_norm, w_out=m_w_out, norm_post=m_norm_post)
    v = dict(norm_pre=v_norm_pre, w_in=v_w_in, attn_sinks=v_attn_sinks, conv_w=v_conv_w, conv_b=v_conv_b,
             mla_q_norm=v_mla_q_norm, mla_w_uq=v_mla_w_uq, mla_kv_norm=v_mla_kv_norm, mla_w_ukv=v_mla_w_ukv,
             group_norm=v_group_norm, w_out=v_w_out, norm_post=v_norm_post)
    T = x.shape[1]
    xs = x[0]
    tgt = loss_target[0]

    def shard_parts(l):
        halves = lambda a: a.reshape((2, a.shape[0] // 2) + a.shape[1:])
        return [halves(jnp.swapaxes(w["w_in"][l], 0, 1).astype(BF16))] + [
            halves(w[n][l].astype(BF16)) for n in _BIG[1:4]] + [jnp.stack([w["conv_w"][l]] * 2)]

    def layer_params(l, got):
        whole = lambda a: a.reshape((4, 2 * a.shape[2]) + a.shape[3:])
        by_cols = lambda a: jnp.transpose(a, (1, 0, 2)).reshape(a.shape[1], 4 * a.shape[2])
        return dict(
            norm_pre=norm_pre[l:l + 1], w_in=_w_in_internal(got[0].reshape((8,) + got[0].shape[2:])),
            attn_sinks=attn_sinks[l], conv_w=by_cols(got[4][:, 0]), conv_b=conv_b[l:l + 1],
            mla_q_norm=mla_q_norm[l:l + 1], mla_w_uq=_uq_internal(by_cols(whole(got[2]))),
            mla_kv_norm=mla_kv_norm[l:l + 1], mla_w_ukv=_ukv_internal(by_cols(whole(got[3]))),
            group_norm=group_norm[l:l + 1], w_out=whole(got[1]).reshape(D_MODEL, D_MODEL),
            norm_post=norm_post[l:l + 1])

    layers, saved = [], []
    rope, *got = _rope_tables(positions[0].reshape(T, 1), shard_parts(0))
    h = xs
    for l in range(DEPTH):
        last = l == DEPTH - 1
        layers.append(layer_params(l, got))
        h, s, got = _layer_fwd(h, rope, layers[l], tgt if last else None, () if last else shard_parts(l + 1))
        saved.append(s)
    dy, loss_part = h

    grads = [None] * DEPTH
    for l in reversed(range(DEPTH)):
        dy, grads[l] = _layer_bwd(dy, rope, layers[l], saved[l])

    turned = ("w_in", "mla_w_uq")
    turn = lambda n, a: jnp.swapaxes(a, -1, -2) if n in turned else a

    def chunks(n, a):
        if n in ("w_out", "w_in"):
            return a.reshape(4, a.shape[0] // 4, a.shape[1])
        if n in turned:
            return a.T.reshape(4, a.shape[1] // 4, a.shape[0])
        return jnp.transpose(a.reshape(a.shape[0], 4, a.shape[1] // 4), (1, 0, 2))

    grad = lambda l, n: grads[l]["w_in_t" if n == "w_in" else n]
    per_layer = [[chunks(n, grad(l, n)) for n in _BIG] for l in range(DEPTH)]
    from_sibling = _swap_layer_chunks(per_layer)
    summed = [_add(a0, a1, b, "add_cores_" + n) for n, a0, a1, b in zip(_BIG, *per_layer, from_sibling)]
    small = _pack_small({n: jnp.stack([grads[l][n] for l in range(DEPTH)]) for n in _SMALL}, loss_part[0, 0])
    *got, got_small = _exchange_chips(summed, small)
    done = [_sum_leading(b, "sum_chips_" + n) for n, b in zip(_BIG, got)]
    done_other = _swap_cores(done, "swap_layer_shards")

    outs, loss = _adamw_small(w, m, v, got_small)
    for n, gm, go in zip(_BIG, done, done_other):
        for d, a in zip(outs, _adamw_layers(turn(n, w[n]), turn(n, m[n]), turn(n, v[n]), gm, go, "adamw_" + n)):
            d[n] = turn(n, a)
    return (loss[0, 0], dy[None], *[outs[0][n] for n in _WEIGHTS], *[outs[1][n] for n in _WEIGHTS],
            *[outs[2][n] for n in _WEIGHTS], *[outs[3][n] for n in _WEIGHTS])
```
